```python
import math
import jax, jax.numpy as jnp
from jax import lax
import numpy as np

D_MODEL = 1024
BATCH = 8
SEQ = 4096
DEPTH = 2

HEAD_DIM = 64
N_HEADS_TOTAL = D_MODEL // HEAD_DIM
N_SB_HEADS = N_HEADS_TOTAL // 4
DIL_GROUPS = ((128, 1), (512, 4), (2048, 16))
N_DIL_GROUPS = len(DIL_GROUPS)
HEADS_PER_GROUP = (N_HEADS_TOTAL - N_SB_HEADS) // N_DIL_GROUPS
N_DIL_HEADS = HEADS_PER_GROUP * N_DIL_GROUPS
D_DIL = N_DIL_HEADS * HEAD_DIM
D_DIL_OUT = HEADS_PER_GROUP * HEAD_DIM
D_SB = N_SB_HEADS * HEAD_DIM
D_IN = 3 * D_DIL + 3 * D_SB + 2 * D_MODEL
D_FF = 128 * ((8 * D_MODEL // 3 + 127) // 128)
ROPE_THETA = 500000.0
ROPE_DIM = HEAD_DIM // 4
Q_BLOCK = 128
RMS_EPS = 1e-6

kernel_name = "hybrid_dilated_stickbreaking_macaron"


def rms_norm(x, gain):
    xf = x.astype(jnp.float32)
    xf = xf * lax.rsqrt(jnp.mean(xf * xf, axis=-1, keepdims=True) + RMS_EPS)
    return (xf * gain.astype(jnp.float32)).astype(x.dtype)


def swiglu(x, w_gate, w_up, w_down):
    return (jax.nn.silu(x @ w_gate) * (x @ w_up)) @ w_down


def rope_tables(seq_len):
    pos = jnp.arange(seq_len, dtype=jnp.float32)
    inv_freq = ROPE_THETA ** (-jnp.arange(0, ROPE_DIM, 2, dtype=jnp.float32) / ROPE_DIM)
    ang = pos[:, None] * inv_freq[None, :]
    return jnp.cos(ang), jnp.sin(ang)


def apply_partial_rope(x, cos, sin):
    half = ROPE_DIM // 2
    x1 = x[..., :half]
    x2 = x[..., half:ROPE_DIM]
    c = cos.astype(x.dtype)
    s = sin.astype(x.dtype)
    return jnp.concatenate([x1 * c - x2 * s, x2 * c + x1 * s, x[..., ROPE_DIM:]], axis=-1)


def dilated_window_attention(q, k, v, window, dilation):
    B, H, T, dh = q.shape
    span = window // dilation
    unit = span * dilation
    t_pad = -(-T // unit) * unit
    n_sub = t_pad // dilation
    n_blk = n_sub // span

    def to_blocks(a):
        a = jnp.pad(a, ((0, 0), (0, 0), (0, t_pad - T), (0, 0)))
        a = a.reshape(B, H, n_sub, dilation, dh).transpose(0, 1, 3, 2, 4)
        return a.reshape(B, H, dilation, n_blk, span, dh)

    qb, kb, vb = to_blocks(q), to_blocks(k), to_blocks(v)

    def with_prev(a):
        prev = jnp.pad(a, ((0, 0), (0, 0), (0, 0), (1, 0), (0, 0), (0, 0)))[:, :, :, :-1]
        return jnp.concatenate([prev, a], axis=4)

    kw, vw = with_prev(kb), with_prev(vb)
    s = jnp.einsum('bhrnqd,bhrnkd->bhrnqk', qb, kw).astype(jnp.float32) * (dh ** -0.5)
    qi = jnp.arange(span)[:, None]
    kj = jnp.arange(2 * span)[None, :]
    dist = qi + span - kj
    band = (dist >= 0) & (dist <= span)
    blk = jnp.arange(n_blk)[:, None, None]
    valid = band[None] & ((blk > 0) | (kj >= span)[None])
    s = jnp.where(valid, s, -jnp.inf)
    m = jnp.max(s, axis=-1, keepdims=True)
    p = jnp.exp(s - m)
    denom = jnp.sum(p, axis=-1, keepdims=True)
    o = jnp.einsum('bhrnqk,bhrnkd->bhrnqd', p, vw.astype(jnp.float32)) / denom
    lse = (m + jnp.log(denom))[..., 0]
    o = o.reshape(B, H, dilation, n_sub, dh).transpose(0, 1, 3, 2, 4).reshape(B, H, t_pad, dh)[:, :, :T]
    lse = lse.reshape(B, H, dilation, n_sub).transpose(0, 1, 3, 2).reshape(B, H, t_pad)[:, :, :T]
    return o, lse


def stick_breaking_attention(q, k, v):
    B, H, T, dh = q.shape
    n_blk = T // Q_BLOCK
    qb = q.reshape(B, H, n_blk, Q_BLOCK, dh).transpose(2, 0, 1, 3, 4)
    kpos = jnp.arange(T)
    vf = v.astype(jnp.float32)

    def block(args):
        q_blk, b = args
        z = jnp.einsum('bhqd,bhkd->bhqk', q_blk, k).astype(jnp.float32) * (dh ** -0.5)
        qpos = b * Q_BLOCK + jnp.arange(Q_BLOCK)
        past = kpos[None, :] < qpos[:, None]
        log_beta = jax.nn.log_sigmoid(z)
        log_keep = jnp.where(past, jax.nn.log_sigmoid(-z), 0.0)
        after = lax.cumsum(log_keep, axis=3, reverse=True) - log_keep
        w = jnp.where(past, jnp.exp(log_beta + after), 0.0)
        return jnp.einsum('bhqk,bhkd->bhqd', w, vf)

    o = lax.map(block, (qb, jnp.arange(n_blk)))
    return o.transpose(1, 2, 0, 3, 4).reshape(B, H, T, dh).astype(q.dtype)


def hybrid_mixer(h, w_in, w_proj_dil, w_proj_sb, w_out, cos, sin):
    B, T, _ = h.shape
    proj = h @ w_in
    o1 = 3 * D_DIL
    o2 = o1 + 3 * D_SB
    o3 = o2 + D_MODEL
    qkv_d = proj[..., :o1].reshape(B, T, 3, N_DIL_HEADS, HEAD_DIM).transpose(2, 0, 3, 1, 4)
    q_d = apply_partial_rope(qkv_d[0], cos, sin)
    k_d = apply_partial_rope(qkv_d[1], cos, sin)
    v_d = qkv_d[2]
    outs, lses = [], []
    for g, (window, dilation) in enumerate(DIL_GROUPS):
        hs = slice(g * HEADS_PER_GROUP, (g + 1) * HEADS_PER_GROUP)
        o, lse = dilated_window_attention(q_d[:, hs], k_d[:, hs], v_d[:, hs], window, dilation)
        outs.append(o)
        lses.append(lse)
    w_grp = jax.nn.softmax(jnp.stack(lses, axis=0), axis=0)
    o_dil = jnp.sum(w_grp[..., None] * jnp.stack(outs, axis=0), axis=0).astype(h.dtype)
    o_dil = o_dil.transpose(0, 2, 1, 3).reshape(B, T, D_DIL_OUT)
    qkv_s = proj[..., o1:o2].reshape(B, T, 3, N_SB_HEADS, HEAD_DIM).transpose(2, 0, 3, 1, 4)
    o_sb = stick_breaking_attention(qkv_s[0], qkv_s[1], qkv_s[2])
    o_sb = o_sb.transpose(0, 2, 1, 3).reshape(B, T, D_SB)
    gate_dil = jax.nn.sigmoid(proj[..., o2:o3])
    gate_sb = jax.nn.sigmoid(proj[..., o3:])
    y = gate_dil * (o_dil @ w_proj_dil) + gate_sb * (o_sb @ w_proj_sb)
    return y @ w_out


def _fwd_setup_inputs(seed: int = 0) -> dict:
    key = jax.random.key(seed)
    ks = jax.random.split(key, 16)
    f32 = jnp.float32

    def dense(k, shape, fan_in):
        return jax.random.normal(k, shape, f32) * (fan_in ** -0.5)

    def gain(k, shape):
        return 1.0 + 0.05 * jax.random.normal(k, shape, f32)

    return {
        "x": jax.random.normal(ks[0], (BATCH, SEQ, D_MODEL), f32),
        "norm_ffn1": gain(ks[1], (DEPTH, D_MODEL)),
        "ffn1_w_gate": dense(ks[2], (DEPTH, D_MODEL, D_FF), D_MODEL),
        "ffn1_w_up": dense(ks[3], (DEPTH, D_MODEL, D_FF), D_MODEL),
        "ffn1_w_down": dense(ks[4], (DEPTH, D_FF, D_MODEL), D_FF),
        "norm_mix": gain(ks[5], (DEPTH, D_MODEL)),
        "w_in": dense(ks[6], (DEPTH, D_MODEL, D_IN), D_MODEL),
        "w_proj_dil": dense(ks[7], (DEPTH, D_DIL_OUT, D_MODEL), D_DIL_OUT),
        "w_proj_sb": dense(ks[8], (DEPTH, D_SB, D_MODEL), D_SB),
        "w_out": dense(ks[9], (DEPTH, D_MODEL, D_MODEL), D_MODEL),
        "norm_ffn2": gain(ks[10], (DEPTH, D_MODEL)),
        "ffn2_w_gate": dense(ks[11], (DEPTH, D_MODEL, D_FF), D_MODEL),
        "ffn2_w_up": dense(ks[12], (DEPTH, D_MODEL, D_FF), D_MODEL),
        "ffn2_w_down": dense(ks[13], (DEPTH, D_FF, D_MODEL), D_FF),
        "norm_final": gain(ks[14], (D_MODEL,)),
    }


def _fwd_reference(x, norm_ffn1, ffn1_w_gate, ffn1_w_up, ffn1_w_down, norm_mix, w_in,
              w_proj_dil, w_proj_sb, w_out, norm_ffn2, ffn2_w_gate, ffn2_w_up,
              ffn2_w_down, norm_final):
    T = x.shape[1]
    cos, sin = rope_tables(T)
    for l in range(DEPTH):
        x = x + 0.5 * swiglu(rms_norm(x, norm_ffn1[l]), ffn1_w_gate[l], ffn1_w_up[l], ffn1_w_down[l])
        x = x + hybrid_mixer(rms_norm(x, norm_mix[l]), w_in[l], w_proj_dil[l], w_proj_sb[l],
                             w_out[l], cos, sin)
        x = x + 0.5 * swiglu(rms_norm(x, norm_ffn2[l]), ffn2_w_gate[l], ffn2_w_up[l], ffn2_w_down[l])
    return rms_norm(x, norm_final)


import jax as _jax
import jax.numpy as _jnp

TWIN_FORMAT = 'train_step'
FWD_PARAMS = ['x', 'norm_ffn1', 'ffn1_w_gate', 'ffn1_w_up', 'ffn1_w_down', 'norm_mix', 'w_in', 'w_proj_dil', 'w_proj_sb', 'w_out', 'norm_ffn2', 'ffn2_w_gate', 'ffn2_w_up', 'ffn2_w_down', 'norm_final']
TWIN_WEIGHTS = ['norm_ffn1', 'ffn1_w_gate', 'ffn1_w_up', 'ffn1_w_down', 'norm_mix', 'w_in', 'w_proj_dil', 'w_proj_sb', 'w_out', 'norm_ffn2', 'ffn2_w_gate', 'ffn2_w_up', 'ffn2_w_down', 'norm_final']
TWIN_DIFF_INPUT = 'x'
TWIN_INPUTS = ['x', 'norm_ffn1', 'ffn1_w_gate', 'ffn1_w_up', 'ffn1_w_down', 'norm_mix', 'w_in', 'w_proj_dil', 'w_proj_sb', 'w_out', 'norm_ffn2', 'ffn2_w_gate', 'ffn2_w_up', 'ffn2_w_down', 'norm_final', 'loss_target', 'm_norm_ffn1', 'm_ffn1_w_gate', 'm_ffn1_w_up', 'm_ffn1_w_down', 'm_norm_mix', 'm_w_in', 'm_w_proj_dil', 'm_w_proj_sb', 'm_w_out', 'm_norm_ffn2', 'm_ffn2_w_gate', 'm_ffn2_w_up', 'm_ffn2_w_down', 'm_norm_final', 'v_norm_ffn1', 'v_ffn1_w_gate', 'v_ffn1_w_up', 'v_ffn1_w_down', 'v_norm_mix', 'v_w_in', 'v_w_proj_dil', 'v_w_proj_sb', 'v_w_out', 'v_norm_ffn2', 'v_ffn2_w_gate', 'v_ffn2_w_up', 'v_ffn2_w_down', 'v_norm_final']
TWIN_OUTPUTS = ['loss', 'grad_x', 'grad_norm_ffn1', 'grad_ffn1_w_gate', 'grad_ffn1_w_up', 'grad_ffn1_w_down', 'grad_norm_mix', 'grad_w_in', 'grad_w_proj_dil', 'grad_w_proj_sb', 'grad_w_out', 'grad_norm_ffn2', 'grad_ffn2_w_gate', 'grad_ffn2_w_up', 'grad_ffn2_w_down', 'grad_norm_final', 'delta_norm_ffn1', 'delta_ffn1_w_gate', 'delta_ffn1_w_up', 'delta_ffn1_w_down', 'delta_norm_mix', 'delta_w_in', 'delta_w_proj_dil', 'delta_w_proj_sb', 'delta_w_out', 'delta_norm_ffn2', 'delta_ffn2_w_gate', 'delta_ffn2_w_up', 'delta_ffn2_w_down', 'delta_norm_final', 'new_m_norm_ffn1', 'new_m_ffn1_w_gate', 'new_m_ffn1_w_up', 'new_m_ffn1_w_down', 'new_m_norm_mix', 'new_m_w_in', 'new_m_w_proj_dil', 'new_m_w_proj_sb', 'new_m_w_out', 'new_m_norm_ffn2', 'new_m_ffn2_w_gate', 'new_m_ffn2_w_up', 'new_m_ffn2_w_down', 'new_m_norm_final', 'new_v_norm_ffn1', 'new_v_ffn1_w_gate', 'new_v_ffn1_w_up', 'new_v_ffn1_w_down', 'new_v_norm_mix', 'new_v_w_in', 'new_v_w_proj_dil', 'new_v_w_proj_sb', 'new_v_w_out', 'new_v_norm_ffn2', 'new_v_ffn2_w_gate', 'new_v_ffn2_w_up', 'new_v_ffn2_w_down', 'new_v_norm_final']
TWIN_LEAF_KINDS = {'loss': 'loss', 'grad_x': 'grad_x', 'grad_norm_ffn1': 'grad_w', 'grad_ffn1_w_gate': 'grad_w', 'grad_ffn1_w_up': 'grad_w', 'grad_ffn1_w_down': 'grad_w', 'grad_norm_mix': 'grad_w', 'grad_w_in': 'grad_w', 'grad_w_proj_dil': 'grad_w', 'grad_w_proj_sb': 'grad_w', 'grad_w_out': 'grad_w', 'grad_norm_ffn2': 'grad_w', 'grad_ffn2_w_gate': 'grad_w', 'grad_ffn2_w_up': 'grad_w', 'grad_ffn2_w_down': 'grad_w', 'grad_norm_final': 'grad_w', 'delta_norm_ffn1': 'delta_w', 'delta_ffn1_w_gate': 'delta_w', 'delta_ffn1_w_up': 'delta_w', 'delta_ffn1_w_down': 'delta_w', 'delta_norm_mix': 'delta_w', 'delta_w_in': 'delta_w', 'delta_w_proj_dil': 'delta_w', 'delta_w_proj_sb': 'delta_w', 'delta_w_out': 'delta_w', 'delta_norm_ffn2': 'delta_w', 'delta_ffn2_w_gate': 'delta_w', 'delta_ffn2_w_up': 'delta_w', 'delta_ffn2_w_down': 'delta_w', 'delta_norm_final': 'delta_w', 'new_m_norm_ffn1': 'new_m', 'new_m_ffn1_w_gate': 'new_m', 'new_m_ffn1_w_up': 'new_m', 'new_m_ffn1_w_down': 'new_m', 'new_m_norm_mix': 'new_m', 'new_m_w_in': 'new_m', 'new_m_w_proj_dil': 'new_m', 'new_m_w_proj_sb': 'new_m', 'new_m_w_out': 'new_m', 'new_m_norm_ffn2': 'new_m', 'new_m_ffn2_w_gate': 'new_m', 'new_m_ffn2_w_up': 'new_m', 'new_m_ffn2_w_down': 'new_m', 'new_m_norm_final': 'new_m', 'new_v_norm_ffn1': 'new_v', 'new_v_ffn1_w_gate': 'new_v', 'new_v_ffn1_w_up': 'new_v', 'new_v_ffn1_w_down': 'new_v', 'new_v_norm_mix': 'new_v', 'new_v_w_in': 'new_v', 'new_v_w_proj_dil': 'new_v', 'new_v_w_proj_sb': 'new_v', 'new_v_w_out': 'new_v', 'new_v_norm_ffn2': 'new_v', 'new_v_ffn2_w_gate': 'new_v', 'new_v_ffn2_w_up': 'new_v', 'new_v_ffn2_w_down': 'new_v', 'new_v_norm_final': 'new_v'}


def _forward(args):
    return _fwd_reference(*[args[k] for k in FWD_PARAMS])


def _output_shape():
    def fwd():
        inp = _fwd_setup_inputs(0)
        return _fwd_reference(*[inp[k] for k in FWD_PARAMS])
    out = _jax.eval_shape(fwd)
    return out.shape, out.dtype

N_MICROBATCH = 1
ADAM_LR = 0.001
ADAM_B1 = 0.9
ADAM_B2 = 0.999
ADAM_EPS = 1e-08
ADAM_WD = 0.01
ADAM_STEP = 10
PER_EXAMPLE_BATCH_AXIS = {'x': 0, 'loss_target': 0}
SHARED_INPUTS = []
_WEIGHT_DTYPES = {'norm_ffn1': _jnp.float32, 'ffn1_w_gate': _jnp.float32, 'ffn1_w_up': _jnp.float32, 'ffn1_w_down': _jnp.float32, 'norm_mix': _jnp.float32, 'w_in': _jnp.float32, 'w_proj_dil': _jnp.float32, 'w_proj_sb': _jnp.float32, 'w_out': _jnp.float32, 'norm_ffn2': _jnp.float32, 'ffn2_w_gate': _jnp.float32, 'ffn2_w_up': _jnp.float32, 'ffn2_w_down': _jnp.float32, 'norm_final': _jnp.float32}
MOMENT_SCALE = {'norm_ffn1': 8.180850e-02, 'ffn1_w_gate': 3.369193e-02, 'ffn1_w_up': 3.265168e-02, 'ffn1_w_down': 5.407230e-02, 'norm_mix': 8.151876e-02, 'w_in': 3.507670e-02, 'w_proj_dil': 1.742200e-02, 'w_proj_sb': 5.959321e-02, 'w_out': 6.095253e-02, 'norm_ffn2': 6.880589e-02, 'ffn2_w_gate': 2.908169e-02, 'ffn2_w_up': 2.835792e-02, 'ffn2_w_down': 4.701067e-02, 'norm_final': 3.205208e+01}


def _to_microbatches(a, axis):
    t = _jnp.moveaxis(a, axis, 0)
    t = t.reshape((N_MICROBATCH, t.shape[0] // N_MICROBATCH) + t.shape[1:])
    return _jnp.moveaxis(t, 1, axis + 1)


def setup_inputs(seed: int = 0) -> dict:
    inp = _fwd_setup_inputs(seed)
    key = _jax.random.fold_in(_jax.random.key(seed), 7919)
    shape, _ = _output_shape()
    out = dict(inp)
    out["loss_target"] = _jax.random.normal(_jax.random.fold_in(key, 0), shape, _jnp.float32)
    for i, name in enumerate(TWIN_WEIGHTS):
        w = inp[name].astype(_jnp.float32)
        if MOMENT_SCALE is None:
            s = _jnp.sqrt(_jnp.mean(_jnp.square(w)) + 1e-30)
        else:
            s = MOMENT_SCALE[name]
        km, kv = _jax.random.split(_jax.random.fold_in(key, i + 1))
        out[name] = w
        out["m_" + name] = s * _jax.random.normal(km, w.shape, _jnp.float32)
        out["v_" + name] = (s * s) * _jax.random.uniform(kv, w.shape, _jnp.float32, 0.5, 1.5)
    if N_MICROBATCH > 1:
        for name, axis in PER_EXAMPLE_BATCH_AXIS.items():
            out[name] = _to_microbatches(out[name], axis)
    return {'x': out['x'], 'norm_ffn1': out['norm_ffn1'], 'ffn1_w_gate': out['ffn1_w_gate'], 'ffn1_w_up': out['ffn1_w_up'], 'ffn1_w_down': out['ffn1_w_down'], 'norm_mix': out['norm_mix'], 'w_in': out['w_in'], 'w_proj_dil': out['w_proj_dil'], 'w_proj_sb': out['w_proj_sb'], 'w_out': out['w_out'], 'norm_ffn2': out['norm_ffn2'], 'ffn2_w_gate': out['ffn2_w_gate'], 'ffn2_w_up': out['ffn2_w_up'], 'ffn2_w_down': out['ffn2_w_down'], 'norm_final': out['norm_final'], 'loss_target': out['loss_target'], 'm_norm_ffn1': out['m_norm_ffn1'], 'm_ffn1_w_gate': out['m_ffn1_w_gate'], 'm_ffn1_w_up': out['m_ffn1_w_up'], 'm_ffn1_w_down': out['m_ffn1_w_down'], 'm_norm_mix': out['m_norm_mix'], 'm_w_in': out['m_w_in'], 'm_w_proj_dil': out['m_w_proj_dil'], 'm_w_proj_sb': out['m_w_proj_sb'], 'm_w_out': out['m_w_out'], 'm_norm_ffn2': out['m_norm_ffn2'], 'm_ffn2_w_gate': out['m_ffn2_w_gate'], 'm_ffn2_w_up': out['m_ffn2_w_up'], 'm_ffn2_w_down': out['m_ffn2_w_down'], 'm_norm_final': out['m_norm_final'], 'v_norm_ffn1': out['v_norm_ffn1'], 'v_ffn1_w_gate': out['v_ffn1_w_gate'], 'v_ffn1_w_up': out['v_ffn1_w_up'], 'v_ffn1_w_down': out['v_ffn1_w_down'], 'v_norm_mix': out['v_norm_mix'], 'v_w_in': out['v_w_in'], 'v_w_proj_dil': out['v_w_proj_dil'], 'v_w_proj_sb': out['v_w_proj_sb'], 'v_w_out': out['v_w_out'], 'v_norm_ffn2': out['v_norm_ffn2'], 'v_ffn2_w_gate': out['v_ffn2_w_gate'], 'v_ffn2_w_up': out['v_ffn2_w_up'], 'v_ffn2_w_down': out['v_ffn2_w_down'], 'v_norm_final': out['v_norm_final']}


def _loss(weights, diff, rest, loss_target):
    with _jax.named_scope("forward"):
        args = {**rest, TWIN_DIFF_INPUT: diff, **{k: w.astype(_WEIGHT_DTYPES[k]) for k, w in weights.items()}}
        y = _forward(args)
    with _jax.named_scope("loss_head"):
        err = _jnp.square(y.astype(_jnp.float32) - loss_target)
        return 0.5 * _jnp.sum(_jnp.mean(err, axis=-1)) if err.ndim else 0.5 * err


def _adamw(w, g, m, v):
    m = ADAM_B1 * m + (1.0 - ADAM_B1) * g
    v = ADAM_B2 * v + (1.0 - ADAM_B2) * _jnp.square(g)
    m_hat = m / (1.0 - ADAM_B1 ** ADAM_STEP)
    v_hat = v / (1.0 - ADAM_B2 ** ADAM_STEP)
    delta = -ADAM_LR * (m_hat / (_jnp.sqrt(v_hat) + ADAM_EPS) + ADAM_WD * w)
    return delta, m, v


def reference(x, norm_ffn1, ffn1_w_gate, ffn1_w_up, ffn1_w_down, norm_mix, w_in, w_proj_dil, w_proj_sb, w_out, norm_ffn2, ffn2_w_gate, ffn2_w_up, ffn2_w_down, norm_final, loss_target, m_norm_ffn1, m_ffn1_w_gate, m_ffn1_w_up, m_ffn1_w_down, m_norm_mix, m_w_in, m_w_proj_dil, m_w_proj_sb, m_w_out, m_norm_ffn2, m_ffn2_w_gate, m_ffn2_w_up, m_ffn2_w_down, m_norm_final, v_norm_ffn1, v_ffn1_w_gate, v_ffn1_w_up, v_ffn1_w_down, v_norm_mix, v_w_in, v_w_proj_dil, v_w_proj_sb, v_w_out, v_norm_ffn2, v_ffn2_w_gate, v_ffn2_w_up, v_ffn2_w_down, v_norm_final):
    given = dict(x=x, norm_ffn1=norm_ffn1, ffn1_w_gate=ffn1_w_gate, ffn1_w_up=ffn1_w_up, ffn1_w_down=ffn1_w_down, norm_mix=norm_mix, w_in=w_in, w_proj_dil=w_proj_dil, w_proj_sb=w_proj_sb, w_out=w_out, norm_ffn2=norm_ffn2, ffn2_w_gate=ffn2_w_gate, ffn2_w_up=ffn2_w_up, ffn2_w_down=ffn2_w_down, norm_final=norm_final, loss_target=loss_target, m_norm_ffn1=m_norm_ffn1, m_ffn1_w_gate=m_ffn1_w_gate, m_ffn1_w_up=m_ffn1_w_up, m_ffn1_w_down=m_ffn1_w_down, m_norm_mix=m_norm_mix, m_w_in=m_w_in, m_w_proj_dil=m_w_proj_dil, m_w_proj_sb=m_w_proj_sb, m_w_out=m_w_out, m_norm_ffn2=m_norm_ffn2, m_ffn2_w_gate=m_ffn2_w_gate, m_ffn2_w_up=m_ffn2_w_up, m_ffn2_w_down=m_ffn2_w_down, m_norm_final=m_norm_final, v_norm_ffn1=v_norm_ffn1, v_ffn1_w_gate=v_ffn1_w_gate, v_ffn1_w_up=v_ffn1_w_up, v_ffn1_w_down=v_ffn1_w_down, v_norm_mix=v_norm_mix, v_w_in=v_w_in, v_w_proj_dil=v_w_proj_dil, v_w_proj_sb=v_w_proj_sb, v_w_out=v_w_out, v_norm_ffn2=v_norm_ffn2, v_ffn2_w_gate=v_ffn2_w_gate, v_ffn2_w_up=v_ffn2_w_up, v_ffn2_w_down=v_ffn2_w_down, v_norm_final=v_norm_final)
    weights = {n: given[n] for n in TWIN_WEIGHTS}
    shared = {n: given[n] for n in SHARED_INPUTS}
    per_example = {n: given[n] for n in ['x']}
    grad_fn = _jax.value_and_grad(_loss, argnums=(0, 1))

    def one_microbatch(ex, loss_target):
        ex = dict(ex)
        diff = ex.pop(TWIN_DIFF_INPUT)
        return grad_fn(weights, diff, {**shared, **ex}, loss_target)

    if N_MICROBATCH == 1:
        loss, (grad_w, grad_x) = one_microbatch(per_example, given["loss_target"])
    else:
        def body(carry, xs):
            loss_sum, grad_sum = carry
            l_k, (gw_k, gx_k) = one_microbatch(xs[0], xs[1])
            with _jax.named_scope("update"):
                return (loss_sum + l_k, _jax.tree.map(_jnp.add, grad_sum, gw_k)), gx_k

        init = (_jnp.zeros((), _jnp.float32), _jax.tree.map(_jnp.zeros_like, weights))
        (loss, grad_w), grad_x = _jax.lax.scan(body, init, (per_example, given["loss_target"]))
    with _jax.named_scope("update"):
        delta_w, new_m, new_v = {}, {}, {}
        for n in TWIN_WEIGHTS:
            delta_w[n], new_m[n], new_v[n] = _adamw(weights[n], grad_w[n], given["m_" + n], given["v_" + n])
    return (loss, grad_x, *[grad_w[n] for n in TWIN_WEIGHTS], *[delta_w[n] for n in TWIN_WEIGHTS],
            *[new_m[n] for n in TWIN_WEIGHTS], *[new_v[n] for n in TWIN_WEIGHTS])
```

```python
import functools

import jax
import jax.numpy as jnp
from jax import lax
from jax.experimental import pallas as pl
from jax.experimental.pallas import tpu as pltpu

F32 = jnp.float32
BF16 = jnp.bfloat16

D_MODEL = 1024
DEPTH = 2
N_CHIPS = 4
HEAD_DIM = 64
ROPE_DIM = 16
ROPE_THETA = 500000.0
DIL_GROUPS = ((128, 1), (512, 4), (2048, 16))
SPAN = 128
Q_BLOCK = 128
RMS_EPS = 1e-6
D_ATT = 256
COL_QS = 2304
COL_GD = 3072
COL_GS = 4096
ADAM_LR, ADAM_B1, ADAM_B2, ADAM_EPS, ADAM_WD, ADAM_STEP = 0.001, 0.9, 0.999, 1e-08, 0.01, 10

VMEM_LIMIT = 52 * 1024 * 1024
TM = 512
NEG = -1e30

NN = (((1,), (0,)), ((), ()))
NT = (((1,), (1,)), ((), ()))
TN = (((0,), (0,)), ((), ()))
MESH = pl.DeviceIdType.MESH

WEIGHT_NAMES = ("ffn1_w_gate", "ffn1_w_up", "ffn1_w_down", "w_in", "w_proj_dil",
                "w_proj_sb", "w_out", "ffn2_w_gate", "ffn2_w_up", "ffn2_w_down")
NORM_NAMES = ("norm_ffn1", "norm_mix", "norm_ffn2")


def _params(**kw):
    return pltpu.CompilerParams(vmem_limit_bytes=VMEM_LIMIT, **kw)


def _sigmoid(x):
    return 1.0 / (1.0 + jnp.exp(-x))


def _mm_body(pairs, n_in, n_out, n_acc, dims, nk, epilogue, *refs):
    ins = refs[:n_in]
    outs = refs[n_in:n_in + n_out]
    accs = refs[n_in + n_out:]
    i = pl.program_id(0)
    k = pl.program_id(2)
    parts = [None] * n_acc
    for ia, ib, ic in pairs:
        d = lax.dot_general(ins[ia][...].astype(BF16), ins[ib][...].astype(BF16), dims,
                            preferred_element_type=F32)
        parts[ic] = d if parts[ic] is None else parts[ic] + d
    if nk == 1:
        epilogue(parts, ins, outs, i)
        return

    @pl.when(k == 0)
    def _():
        for c in range(n_acc):
            accs[c][...] = parts[c]

    @pl.when(k > 0)
    def _():
        for c in range(n_acc):
            accs[c][...] += parts[c]

    @pl.when(k == nk - 1)
    def _():
        epilogue([a[...] for a in accs], ins, outs, i)


def _mm(name, ins, in_specs, pairs, n_acc, acc_shape, dims, grid, epilogue, out_shapes, out_specs):
    nk = grid[2]
    scratch = [pltpu.VMEM(acc_shape, F32) for _ in range(n_acc)] if nk > 1 else []
    body = functools.partial(_mm_body, tuple(pairs), len(ins), len(out_shapes), n_acc, dims, nk, epilogue)
    return pl.pallas_call(
        body, name=name, grid=grid, in_specs=in_specs, out_specs=out_specs, out_shape=out_shapes,
        scratch_shapes=scratch,
        compiler_params=_params(dimension_semantics=("arbitrary", "arbitrary", "arbitrary")),
    )(*ins)


def _wspec(r, c, l, by):
    if by == 1:
        return pl.BlockSpec((None, None, r, c), lambda i, j, k: (j, l, 0, 0))
    return pl.BlockSpec((None, None, r, c), lambda i, j, k: (k, l, 0, 0))


def _rms_bwd_epilogue(x_idx, g_idx, dxo_idx):
    def ep(vals, ins, outs, i):
        dh = vals[0]
        x = ins[x_idx][...]
        g = ins[g_idx][...]
        rstd = lax.rsqrt(jnp.mean(x * x, axis=-1, keepdims=True) + RMS_EPS)
        xhat = x * rstd
        dxhat = dh * g
        dx = rstd * (dxhat - xhat * jnp.mean(dxhat * xhat, axis=-1, keepdims=True))
        outs[0][...] = ins[dxo_idx][...] + dx
        dg = jnp.broadcast_to(jnp.sum(dh * xhat, axis=0, keepdims=True), outs[1].shape)

        @pl.when(i == 0)
        def _():
            outs[1][...] = dg

        @pl.when(i > 0)
        def _():
            outs[1][...] += dg
    return ep


def _rms_fwd(x, gain):
    T = x.shape[0]

    def body(x_ref, g_ref, h_ref):
        xv = x_ref[...]
        h = xv * lax.rsqrt(jnp.mean(xv * xv, axis=-1, keepdims=True) + RMS_EPS)
        h_ref[...] = (h * g_ref[...]).astype(BF16)

    return pl.pallas_call(
        body, name="rms_fwd", grid=(T // TM,),
        in_specs=[pl.BlockSpec((TM, D_MODEL), lambda i: (i, 0)), pl.BlockSpec((1, D_MODEL), lambda i: (0, 0))],
        out_specs=pl.BlockSpec((TM, D_MODEL), lambda i: (i, 0)),
        out_shape=jax.ShapeDtypeStruct((T, D_MODEL), BF16), compiler_params=_params(),
    )(x, gain)


def _rope_tables(T):
    pos = jnp.arange(T, dtype=F32)
    inv_freq = ROPE_THETA ** (-jnp.arange(0, ROPE_DIM, 2, dtype=F32) / ROPE_DIM)
    ang = pos[:, None] * inv_freq[None, :]
    cos, sin = jnp.cos(ang), jnp.sin(ang)
    half = ROPE_DIM // 2
    one = jnp.ones((T, HEAD_DIM - ROPE_DIM), F32)
    zero = jnp.zeros((T, HEAD_DIM - ROPE_DIM), F32)
    zh = jnp.zeros((T, half), F32)
    c = jnp.concatenate([cos, cos, one], axis=1)
    s1 = jnp.concatenate([-sin, zh, zero], axis=1)
    s2 = jnp.concatenate([zh, sin, zero], axis=1)
    return tuple(jnp.concatenate([t, t], axis=1) for t in (c, s1, s2))


def _rope(proj, tabs, inverse, out_dtype):
    T = proj.shape[0]
    half = ROPE_DIM // 2

    def body(x_ref, c_ref, s1_ref, s2_ref, o_ref):
        xv = x_ref[...]
        c, s1, s2 = c_ref[...], s1_ref[...], s2_ref[...]
        if inverse:
            y = xv * c + pltpu.roll(xv * s1, half, 1) + pltpu.roll(xv * s2, 128 - half, 1)
        else:
            y = xv * c + pltpu.roll(xv, 128 - half, 1) * s1 + pltpu.roll(xv, half, 1) * s2
        o_ref[...] = y.astype(out_dtype)

    tab = pl.BlockSpec((TM, 128), lambda i, j: (i, 0))
    return pl.pallas_call(
        body, name="rope_inv" if inverse else "rope", grid=(T // TM, 6 * D_ATT // 128),
        in_specs=[pl.BlockSpec((TM, 128), lambda i, j: (i, j)), tab, tab, tab],
        out_specs=pl.BlockSpec((TM, 128), lambda i, j: (i, j)),
        out_shape=jax.ShapeDtypeStruct((T, 6 * D_ATT), out_dtype), compiler_params=_params(),
    )(proj, *tabs)


def _dil_merge(os_, lses):
    T = os_[0].shape[0]

    def body(o0, o1, o2, l0, l1, l2, o_ref, lse_ref):
        a, b, c = l0[...], l1[...], l2[...]
        m = jnp.maximum(jnp.maximum(a, b), c)
        ea, eb, ec = jnp.exp(a - m), jnp.exp(b - m), jnp.exp(c - m)
        den = ea + eb + ec
        o_ref[...] = (ea * o0[...] + eb * o1[...] + ec * o2[...]) / den
        lse_ref[...] = m + jnp.log(den)

    blk = pl.BlockSpec((TM, D_ATT), lambda i: (i, 0))
    sh = jax.ShapeDtypeStruct((T, D_ATT), F32)
    return pl.pallas_call(
        body, name="dil_merge", grid=(T // TM,), in_specs=[blk] * 6, out_specs=[blk, blk],
        out_shape=[sh, sh], compiler_params=_params(),
    )(*os_, *lses)


def _final_loss(x, gain, target):
    T = x.shape[0]

    def body(x_ref, g_ref, t_ref, dx_ref, dg_ref, loss_ref):
        xv = x_ref[...]
        g = g_ref[...]
        rstd = lax.rsqrt(jnp.mean(xv * xv, axis=-1, keepdims=True) + RMS_EPS)
        xhat = xv * rstd
        err = xhat * g - t_ref[...]
        loss = 0.5 * jnp.sum(jnp.mean(err * err, axis=-1, keepdims=True), axis=0, keepdims=True)
        dy = err * (1.0 / D_MODEL)
        dxhat = dy * g
        dx_ref[...] = rstd * (dxhat - xhat * jnp.mean(dxhat * xhat, axis=-1, keepdims=True))
        dg = jnp.broadcast_to(jnp.sum(dy * xhat, axis=0, keepdims=True), dg_ref.shape)
        ls = jnp.broadcast_to(loss, loss_ref.shape)

        @pl.when(pl.program_id(0) == 0)
        def _():
            dg_ref[...] = dg
            loss_ref[...] = ls

        @pl.when(pl.program_id(0) > 0)
        def _():
            dg_ref[...] += dg
            loss_ref[...] += ls

    blk = pl.BlockSpec((TM, D_MODEL), lambda i: (i, 0))
    row = pl.BlockSpec((1, D_MODEL), lambda i: (0, 0))
    acc = pl.BlockSpec((8, D_MODEL), lambda i: (0, 0))
    return pl.pallas_call(
        body, name="final_loss", grid=(T // TM,), in_specs=[blk, row, blk], out_specs=[blk, acc, acc],
        out_shape=[jax.ShapeDtypeStruct((T, D_MODEL), F32), jax.ShapeDtypeStruct((8, D_MODEL), F32),
                   jax.ShapeDtypeStruct((8, D_MODEL), F32)],
        compiler_params=_params(dimension_semantics=("arbitrary",)),
    )(x, gain, target)


def _head_masks():
    lane = lax.broadcasted_iota(jnp.int32, (SPAN, 128), 1)
    return lane < HEAD_DIM


def _dil_rows(idx, d):
    u = idx // d
    r = idx - u * d
    own = pl.ds(u * (SPAN * d) + r, SPAN, stride=d) if d > 1 else pl.ds(pl.multiple_of(u * SPAN, SPAN), SPAN)
    up = jnp.maximum(u - 1, 0)
    prev = pl.ds(up * (SPAN * d) + r, SPAN, stride=d) if d > 1 else pl.ds(pl.multiple_of(up * SPAN, SPAN), SPAN)
    return u, own, prev


def _dil_scores(qm, k_own, k_prev, u):
    qi = lax.broadcasted_iota(jnp.int32, (SPAN, SPAN), 0)
    kj = lax.broadcasted_iota(jnp.int32, (SPAN, SPAN), 1)
    s_own = lax.dot_general(qm, k_own, NT, preferred_element_type=F32) * (HEAD_DIM ** -0.5)
    s_prev = lax.dot_general(qm, k_prev, NT, preferred_element_type=F32) * (HEAD_DIM ** -0.5)
    ok_own = kj <= qi
    ok_prev = kj >= qi + jnp.where(u > 0, 0, SPAN)
    return s_own, s_prev, ok_own, ok_prev


def _dil_fwd(qk, proj, g, d):
    T = proj.shape[0]
    n_iter = T // SPAN

    def body(q_ref, k_ref, v_ref, o_ref, lse_ref):
        first = _head_masks()

        def step(idx, carry):
            u, own, prev = _dil_rows(idx, d)
            q = q_ref[own, :]
            k_own = k_ref[own, :].astype(BF16)
            k_prev = k_ref[prev, :].astype(BF16)
            v_own = v_ref[own, :].astype(BF16)
            v_prev = v_ref[prev, :].astype(BF16)
            o_h, lse_h = [], []
            for h in range(2):
                qm = jnp.where(first if h == 0 else ~first, q, 0.0).astype(BF16)
                s_own, s_prev, ok_own, ok_prev = _dil_scores(qm, k_own, k_prev, u)
                s_own = jnp.where(ok_own, s_own, NEG)
                s_prev = jnp.where(ok_prev, s_prev, NEG)
                m = jnp.maximum(jnp.max(s_own, axis=1, keepdims=True), jnp.max(s_prev, axis=1, keepdims=True))
                p_own = jnp.exp(s_own - m)
                p_prev = jnp.exp(s_prev - m)
                den = jnp.sum(p_own, axis=1, keepdims=True) + jnp.sum(p_prev, axis=1, keepdims=True)
                pv = (lax.dot_general(p_own.astype(BF16), v_own, NN, preferred_element_type=F32)
                      + lax.dot_general(p_prev.astype(BF16), v_prev, NN, preferred_element_type=F32))
                o_h.append(pv / den)
                lse_h.append(jnp.broadcast_to(m + jnp.log(den), (SPAN, 128)))
            o_ref[own, :] = jnp.where(first, o_h[0], o_h[1])
            lse_ref[own, :] = jnp.where(first, lse_h[0], lse_h[1])
            return carry

        lax.fori_loop(0, n_iter, step, 0)

    def col(b):
        return pl.BlockSpec((T, 128), lambda p: (0, b + p))

    sh = jax.ShapeDtypeStruct((T, D_ATT), F32)
    out = pl.BlockSpec((T, 128), lambda p: (0, p))
    return pl.pallas_call(
        body, name=f"dil_fwd_d{d}", grid=(2,),
        in_specs=[col(2 * g), col(6 + 2 * g), col(12 + 2 * g)], out_specs=[out, out], out_shape=[sh, sh],
        compiler_params=_params(dimension_semantics=("arbitrary",)),
    )(qk, qk, proj)


def _dil_bwd(qk, proj, do, o_dil, lse, g, d):
    T = proj.shape[0]
    n_iter = T // SPAN

    def body(q_ref, k_ref, v_ref, do_ref, o_ref, lse_ref, dq_ref, dk_ref, dv_ref):
        first = _head_masks()

        def step(idx, carry):
            u, own, prev = _dil_rows(idx, d)
            q = q_ref[own, :]
            k_own = k_ref[own, :].astype(BF16)
            k_prev = k_ref[prev, :].astype(BF16)
            v_own = v_ref[own, :].astype(BF16)
            v_prev = v_ref[prev, :].astype(BF16)
            do_v = do_ref[own, :]
            oo = o_ref[own, :]
            ls = lse_ref[own, :]
            dq_h = []
            dk_own = dk_prev = dv_own = dv_prev = None
            for h in range(2):
                hm = first if h == 0 else ~first
                qm = jnp.where(hm, q, 0.0).astype(BF16)
                dom = jnp.where(hm, do_v, 0.0)
                dob = dom.astype(BF16)
                delta = jnp.sum(dom * oo, axis=1, keepdims=True)
                lrow = jnp.max(jnp.where(hm, ls, NEG), axis=1, keepdims=True)
                s_own, s_prev, ok_own, ok_prev = _dil_scores(qm, k_own, k_prev, u)
                p_own = jnp.where(ok_own, jnp.exp(s_own - lrow), 0.0)
                p_prev = jnp.where(ok_prev, jnp.exp(s_prev - lrow), 0.0)
                dp_own = lax.dot_general(dob, v_own, NT, preferred_element_type=F32)
                dp_prev = lax.dot_general(dob, v_prev, NT, preferred_element_type=F32)
                ds_own = (p_own * (dp_own - delta) * (HEAD_DIM ** -0.5)).astype(BF16)
                ds_prev = (p_prev * (dp_prev - delta) * (HEAD_DIM ** -0.5)).astype(BF16)
                dq_h.append(lax.dot_general(ds_own, k_own, NN, preferred_element_type=F32)
                            + lax.dot_general(ds_prev, k_prev, NN, preferred_element_type=F32))
                a = lax.dot_general(ds_own, qm, TN, preferred_element_type=F32)
                b = lax.dot_general(ds_prev, qm, TN, preferred_element_type=F32)
                c = lax.dot_general(p_own.astype(BF16), dob, TN, preferred_element_type=F32)
                e = lax.dot_general(p_prev.astype(BF16), dob, TN, preferred_element_type=F32)
                dk_own = a if dk_own is None else dk_own + a
                dk_prev = b if dk_prev is None else dk_prev + b
                dv_own = c if dv_own is None else dv_own + c
                dv_prev = e if dv_prev is None else dv_prev + e
            dq_ref[own, :] = jnp.where(first, dq_h[0], dq_h[1])
            dk_ref[own, :] = dk_own
            dv_ref[own, :] = dv_own
            dk_ref[prev, :] = dk_ref[prev, :] + dk_prev
            dv_ref[prev, :] = dv_ref[prev, :] + dv_prev
            return carry

        lax.fori_loop(0, n_iter, step, 0)

    def col(b):
        return pl.BlockSpec((T, 128), lambda p: (0, b + p))

    sh = jax.ShapeDtypeStruct((T, D_ATT), F32)
    return pl.pallas_call(
        body, name=f"dil_bwd_d{d}", grid=(2,),
        in_specs=[col(2 * g), col(6 + 2 * g), col(12 + 2 * g), col(0), col(0), col(0)],
        out_specs=[col(0), col(0), col(0)], out_shape=[sh, sh, sh],
        compiler_params=_params(dimension_semantics=("arbitrary",)),
    )(qk, qk, proj, do, o_dil, lse)


def _sb_tri(strict):
    a = lax.broadcasted_iota(jnp.int32, (Q_BLOCK, Q_BLOCK), 0)
    b = lax.broadcasted_iota(jnp.int32, (Q_BLOCK, Q_BLOCK), 1)
    return jnp.where((a > b) if strict else (a >= b), 1.0, 0.0).astype(BF16)


def _split_dot(x, m):
    hi = x.astype(BF16)
    lo = (x - hi.astype(F32)).astype(BF16)
    return (lax.dot_general(hi, m, NN, preferred_element_type=F32)
            + lax.dot_general(lo, m, NN, preferred_element_type=F32))


def _sb_block(qm, kb, i, j, c, tri):
    row = lax.broadcasted_iota(jnp.int32, (Q_BLOCK, Q_BLOCK), 0)
    col = lax.broadcasted_iota(jnp.int32, (Q_BLOCK, Q_BLOCK), 1)
    z = lax.dot_general(qm, kb, NT, preferred_element_type=F32) * (HEAD_DIM ** -0.5)
    lsz = jnp.minimum(z, 0.0) - jnp.log1p(jnp.exp(-jnp.abs(z)))
    past = col + j * Q_BLOCK < row + i * Q_BLOCK
    lk = jnp.where(past, lsz - z, 0.0)
    after = c + _split_dot(lk, tri)
    w = jnp.where(past, jnp.exp(lsz + after), 0.0)
    return z, lsz, past, lk, w


def _sb_fwd(proj):
    T = proj.shape[0]

    def body(q_ref, k_ref, v_ref, o_ref):
        i = pl.program_id(1)
        first = _head_masks()
        tri = _sb_tri(True)
        q = q_ref[...]
        outs = []
        for h in range(2):
            qm = jnp.where(first if h == 0 else ~first, q, 0.0).astype(BF16)

            def step(jj, carry, qm=qm):
                acc, c = carry
                j = i - jj
                rows = pl.ds(pl.multiple_of(j * Q_BLOCK, Q_BLOCK), Q_BLOCK)
                kb = k_ref[rows, :].astype(BF16)
                vb = v_ref[rows, :].astype(BF16)
                _, _, _, lk, w = _sb_block(qm, kb, i, j, c, tri)
                acc = acc + lax.dot_general(w.astype(BF16), vb, NN, preferred_element_type=F32)
                return acc, c + jnp.sum(lk, axis=1, keepdims=True)

            acc, _ = lax.fori_loop(0, i + 1, step,
                                   (jnp.zeros((Q_BLOCK, 128), F32), jnp.zeros((Q_BLOCK, 1), F32)))
            outs.append(acc)
        o_ref[...] = jnp.where(first, outs[0], outs[1])

    cb = COL_QS // 128
    return pl.pallas_call(
        body, name="sb_fwd", grid=(2, T // Q_BLOCK),
        in_specs=[pl.BlockSpec((Q_BLOCK, 128), lambda p, i: (i, cb + p)),
                  pl.BlockSpec((T, 128), lambda p, i: (0, cb + 2 + p)),
                  pl.BlockSpec((T, 128), lambda p, i: (0, cb + 4 + p))],
        out_specs=pl.BlockSpec((Q_BLOCK, 128), lambda p, i: (i, p)),
        out_shape=jax.ShapeDtypeStruct((T, D_ATT), F32),
        compiler_params=_params(dimension_semantics=("arbitrary", "arbitrary")),
    )(proj, proj, proj)


def _sb_bwd(proj, do, o):
    T = proj.shape[0]

    def body(q_ref, k_ref, v_ref, do_ref, o_ref, dq_ref, dk_ref, dv_ref):
        i = pl.program_id(1)
        first = _head_masks()
        tri = _sb_tri(True)
        tri_incl = _sb_tri(False)

        @pl.when(i == 0)
        def _():
            dk_ref[...] = jnp.zeros_like(dk_ref)
            dv_ref[...] = jnp.zeros_like(dv_ref)

        q = q_ref[...]
        do_v = do_ref[...]
        oo = o_ref[...]
        dqs = []
        for h in range(2):
            hm = first if h == 0 else ~first
            qm = jnp.where(hm, q, 0.0).astype(BF16)
            dob = jnp.where(hm, do_v, 0.0).astype(BF16)
            delta = jnp.sum(dob.astype(F32) * oo, axis=1, keepdims=True)

            def step(jj, carry, qm=qm, dob=dob, delta=delta):
                dq, c, ce = carry
                j = i - jj
                rows = pl.ds(pl.multiple_of(j * Q_BLOCK, Q_BLOCK), Q_BLOCK)
                kb = k_ref[rows, :].astype(BF16)
                vb = v_ref[rows, :].astype(BF16)
                z, lsz, past, lk, w = _sb_block(qm, kb, i, j, c, tri)
                gv = lax.dot_general(dob, vb, NT, preferred_element_type=F32)
                wb = w.astype(BF16)
                e = wb.astype(F32) * gv
                big_e = delta - (ce + _split_dot(e, tri_incl))
                beta = jnp.exp(lsz)
                dz = jnp.where(past, e * jnp.exp(lsz - z) - big_e * beta, 0.0) * (HEAD_DIM ** -0.5)
                dzb = dz.astype(BF16)
                dq = dq + lax.dot_general(dzb, kb, NN, preferred_element_type=F32)
                dk_ref[rows, :] = dk_ref[rows, :] + lax.dot_general(dzb, qm, TN, preferred_element_type=F32)
                dv_ref[rows, :] = dv_ref[rows, :] + lax.dot_general(wb, dob, TN, preferred_element_type=F32)
                return dq, c + jnp.sum(lk, axis=1, keepdims=True), ce + jnp.sum(e, axis=1, keepdims=True)

            zero_col = jnp.zeros((Q_BLOCK, 1), F32)
            dq, _, _ = lax.fori_loop(0, i + 1, step, (jnp.zeros((Q_BLOCK, 128), F32), zero_col, zero_col))
            dqs.append(dq)
        dq_ref[...] = jnp.where(first, dqs[0], dqs[1])

    cb = COL_QS // 128
    blk = pl.BlockSpec((Q_BLOCK, 128), lambda p, i: (i, p))
    full = pl.BlockSpec((T, 128), lambda p, i: (0, p))
    sh = jax.ShapeDtypeStruct((T, D_ATT), F32)
    return pl.pallas_call(
        body, name="sb_bwd", grid=(2, T // Q_BLOCK),
        in_specs=[pl.BlockSpec((Q_BLOCK, 128), lambda p, i: (i, cb + p)),
                  pl.BlockSpec((T, 128), lambda p, i: (0, cb + 2 + p)),
                  pl.BlockSpec((T, 128), lambda p, i: (0, cb + 4 + p)), blk, blk],
        out_specs=[blk, full, full], out_shape=[sh, sh, sh],
        compiler_params=_params(dimension_semantics=("arbitrary", "arbitrary")),
    )(proj, proj, proj, do, o)


def _tok(c, by=None):
    if by is None:
        return pl.BlockSpec((TM, c), lambda i, j, k: (i, 0))
    if by == 1:
        return pl.BlockSpec((TM, c), lambda i, j, k: (i, j))
    return pl.BlockSpec((TM, c), lambda i, j, k: (i, k))


def _chunked(c, by):
    if by == 1:
        return pl.BlockSpec((None, TM, c), lambda i, j, k: (j, i, 0))
    return pl.BlockSpec((None, TM, c), lambda i, j, k: (k, i, 0))


def _gain_spec():
    return pl.BlockSpec((1, D_MODEL), lambda i, j, k: (0, 0))


def _ffn_fwd(x, gain, wg, wu, wd, l):
    T = x.shape[0]
    ffs = wg.shape[3]
    h = _rms_fwd(x, gain)

    def swiglu(vals, ins, outs, i):
        gt, up = vals
        outs[0][...] = gt.astype(BF16)
        outs[1][...] = up.astype(BF16)
        outs[2][...] = (gt * _sigmoid(gt) * up).astype(BF16)

    csh = jax.ShapeDtypeStruct((N_CHIPS, T, ffs), BF16)
    gate, up, act = _mm(
        "ffn_up", [h, wg, wu], [_tok(D_MODEL), _wspec(D_MODEL, ffs, l, 1), _wspec(D_MODEL, ffs, l, 1)],
        [(0, 1, 0), (0, 2, 1)], 2, None, NN, (T // TM, N_CHIPS, 1), swiglu,
        [csh, csh, csh], [_chunked(ffs, 1)] * 3)

    def resid(vals, ins, outs, i):
        outs[0][...] = ins[2][...] + 0.5 * vals[0]

    (y,) = _mm(
        "ffn_down", [act, wd, x], [_chunked(ffs, 2), _wspec(ffs, D_MODEL, l, 2), _tok(D_MODEL)],
        [(0, 1, 0)], 1, (TM, D_MODEL), NN, (T // TM, 1, N_CHIPS), resid,
        [jax.ShapeDtypeStruct((T, D_MODEL), F32)], [_tok(D_MODEL)])
    return y, (x, h, gate, up, act)


def _ffn_bwd(dxo, gain, wg, wu, wd, l, saved):
    x, h, gate, up, act = saved
    T = x.shape[0]
    ffs = wg.shape[3]
    tk = TM

    def dswiglu(vals, ins, outs, i):
        da = 0.5 * vals[0]
        gt = ins[2][...].astype(F32)
        u = ins[3][...].astype(F32)
        s = _sigmoid(gt)
        outs[0][...] = (da * u * (s * (1.0 + gt * (1.0 - s)))).astype(BF16)
        outs[1][...] = (da * (gt * s)).astype(BF16)

    csh = jax.ShapeDtypeStruct((N_CHIPS, T, ffs), BF16)
    dgate, dup = _mm(
        "ffn_dact", [dxo, wd, gate, up],
        [_tok(D_MODEL), _wspec(ffs, D_MODEL, l, 1), _chunked(ffs, 1), _chunked(ffs, 1)],
        [(0, 1, 0)], 1, None, NT, (T // TM, N_CHIPS, 1), dswiglu, [csh, csh], [_chunked(ffs, 1)] * 2)

    def half(vals, ins, outs, i):
        outs[0][...] = (0.5 * vals[0]).astype(BF16)

    (dwd,) = _mm(
        "ffn_dwd", [act, dxo],
        [pl.BlockSpec((None, tk, ffs), lambda i, j, k: (j, k, 0)), pl.BlockSpec((tk, D_MODEL), lambda i, j, k: (k, 0))],
        [(0, 1, 0)], 1, (ffs, D_MODEL), TN, (1, N_CHIPS, T // tk), half,
        [jax.ShapeDtypeStruct((N_CHIPS, ffs, D_MODEL), BF16)],
        [pl.BlockSpec((None, ffs, D_MODEL), lambda i, j, k: (j, 0, 0))])

    dx, dgain = _mm(
        "ffn_dx", [dgate, dup, wg, wu, x, gain, dxo],
        [_chunked(ffs, 2), _chunked(ffs, 2), _wspec(D_MODEL, ffs, l, 2), _wspec(D_MODEL, ffs, l, 2),
         _tok(D_MODEL), _gain_spec(), _tok(D_MODEL)],
        [(0, 2, 0), (1, 3, 0)], 1, (TM, D_MODEL), NT, (T // TM, 1, N_CHIPS), _rms_bwd_epilogue(4, 5, 6),
        [jax.ShapeDtypeStruct((T, D_MODEL), F32), jax.ShapeDtypeStruct((8, D_MODEL), F32)],
        [_tok(D_MODEL), pl.BlockSpec((8, D_MODEL), lambda i, j, k: (0, 0))])

    def two(vals, ins, outs, i):
        outs[0][...] = vals[0].astype(BF16)
        outs[1][...] = vals[1].astype(BF16)

    wsh = jax.ShapeDtypeStruct((N_CHIPS, D_MODEL, ffs), BF16)
    wout = pl.BlockSpec((None, D_MODEL, ffs), lambda i, j, k: (j, 0, 0))
    cin = pl.BlockSpec((None, tk, ffs), lambda i, j, k: (j, k, 0))
    dwg, dwu = _mm(
        "ffn_dwgu", [h, dgate, dup], [pl.BlockSpec((tk, D_MODEL), lambda i, j, k: (k, 0)), cin, cin],
        [(0, 1, 0), (0, 2, 1)], 2, (D_MODEL, ffs), TN, (1, N_CHIPS, T // tk), two, [wsh, wsh], [wout, wout])
    return dx, dgain, dwg, dwu, dwd


def _mixer_fwd(x, gain, W, l, tabs):
    T = x.shape[0]
    win, wpd, wps, wo = W["w_in"], W["w_proj_dil"], W["w_proj_sb"], W["w_out"]
    cin = win.shape[3]
    cp = wpd.shape[3]
    h = _rms_fwd(x, gain)

    def plain(vals, ins, outs, i):
        outs[0][...] = vals[0]

    (proj,) = _mm(
        "mix_in", [h, win], [_tok(D_MODEL), _wspec(D_MODEL, cin, l, 1)], [(0, 1, 0)], 1, None, NN,
        (T // TM, N_CHIPS, 1), plain, [jax.ShapeDtypeStruct((T, N_CHIPS * cin), F32)], [_tok(cin, 1)])

    qk = _rope(proj, tabs, False, F32)
    os_, lses = [], []
    for g, (window, dil) in enumerate(DIL_GROUPS):
        o_g, lse_g = _dil_fwd(qk, proj, g, dil)
        os_.append(o_g)
        lses.append(lse_g)
    o_dil, lse = _dil_merge(os_, lses)
    o_sb = _sb_fwd(proj)

    def gated(vals, ins, outs, i):
        pd, ps = vals
        outs[0][...] = (_sigmoid(ins[4][...]) * pd + _sigmoid(ins[5][...]) * ps).astype(BF16)
        outs[1][...] = pd.astype(BF16)
        outs[2][...] = ps.astype(BF16)

    gd0, gs0 = COL_GD // cp, COL_GS // cp
    ush = jax.ShapeDtypeStruct((T, D_MODEL), BF16)
    u, pd, ps = _mm(
        "mix_gate", [o_dil, o_sb, wpd, wps, proj, proj],
        [_tok(D_ATT), _tok(D_ATT), _wspec(D_ATT, cp, l, 1), _wspec(D_ATT, cp, l, 1),
         pl.BlockSpec((TM, cp), lambda i, j, k: (i, gd0 + j)), pl.BlockSpec((TM, cp), lambda i, j, k: (i, gs0 + j))],
        [(0, 2, 0), (1, 3, 1)], 2, None, NN, (T // TM, N_CHIPS, 1), gated, [ush] * 3, [_tok(cp, 1)] * 3)

    def resid(vals, ins, outs, i):
        outs[0][...] = ins[2][...] + vals[0]

    (y,) = _mm(
        "mix_out", [u, wo, x], [_tok(cp, 2), _wspec(cp, D_MODEL, l, 2), _tok(D_MODEL)],
        [(0, 1, 0)], 1, (TM, D_MODEL), NN, (T // TM, 1, N_CHIPS), resid,
        [jax.ShapeDtypeStruct((T, D_MODEL), F32)], [_tok(D_MODEL)])
    return y, (x, h, proj, qk, o_dil, lse, o_sb, u, pd, ps)


def _mixer_bwd(dxo, gain, W, l, tabs, saved):
    x, h, proj, qk, o_dil, lse, o_sb, u, pd, ps = saved
    T = x.shape[0]
    win, wpd, wps, wo = W["w_in"], W["w_proj_dil"], W["w_proj_sb"], W["w_out"]
    cin = win.shape[3]
    cp = wpd.shape[3]
    tk = TM
    gd0, gs0 = COL_GD // cp, COL_GS // cp

    def dgated(vals, ins, outs, i):
        du = vals[0]
        sd = _sigmoid(ins[4][...])
        ss = _sigmoid(ins[5][...])
        outs[0][...] = (du * sd).astype(BF16)
        outs[1][...] = (du * ss).astype(BF16)
        outs[2][...] = (du * ins[2][...].astype(F32) * sd * (1.0 - sd)).astype(BF16)
        outs[3][...] = (du * ins[3][...].astype(F32) * ss * (1.0 - ss)).astype(BF16)

    ush = jax.ShapeDtypeStruct((T, D_MODEL), BF16)
    dpd, dps, dgd, dgs = _mm(
        "mix_du", [dxo, wo, pd, ps, proj, proj],
        [_tok(D_MODEL), _wspec(cp, D_MODEL, l, 1), _tok(cp, 1), _tok(cp, 1),
         pl.BlockSpec((TM, cp), lambda i, j, k: (i, gd0 + j)), pl.BlockSpec((TM, cp), lambda i, j, k: (i, gs0 + j))],
        [(0, 1, 0)], 1, None, NT, (T // TM, N_CHIPS, 1), dgated, [ush] * 4, [_tok(cp, 1)] * 4)

    def one(vals, ins, outs, i):
        outs[0][...] = vals[0].astype(BF16)

    def two(vals, ins, outs, i):
        outs[0][...] = vals[0].astype(BF16)
        outs[1][...] = vals[1].astype(BF16)

    (dwo,) = _mm(
        "mix_dwo", [u, dxo],
        [pl.BlockSpec((tk, cp), lambda i, j, k: (k, j)), pl.BlockSpec((tk, D_MODEL), lambda i, j, k: (k, 0))],
        [(0, 1, 0)], 1, (cp, D_MODEL), TN, (1, N_CHIPS, T // tk), one,
        [jax.ShapeDtypeStruct((N_CHIPS, cp, D_MODEL), BF16)],
        [pl.BlockSpec((None, cp, D_MODEL), lambda i, j, k: (j, 0, 0))])

    def plain2(vals, ins, outs, i):
        outs[0][...] = vals[0]
        outs[1][...] = vals[1]

    ash = jax.ShapeDtypeStruct((T, D_ATT), F32)
    do_dil, do_sb = _mm(
        "mix_do", [dpd, dps, wpd, wps], [_tok(cp, 2), _tok(cp, 2), _wspec(D_ATT, cp, l, 2), _wspec(D_ATT, cp, l, 2)],
        [(0, 2, 0), (1, 3, 1)], 2, (TM, D_ATT), NT, (T // TM, 1, N_CHIPS), plain2, [ash, ash], [_tok(D_ATT)] * 2)

    psh = jax.ShapeDtypeStruct((N_CHIPS, D_ATT, cp), BF16)
    pspec = pl.BlockSpec((None, D_ATT, cp), lambda i, j, k: (j, 0, 0))
    arow = pl.BlockSpec((tk, D_ATT), lambda i, j, k: (k, 0))
    dcol = pl.BlockSpec((tk, cp), lambda i, j, k: (k, j))
    dwpd, dwps = _mm(
        "mix_dwp", [o_dil, o_sb, dpd, dps], [arow, arow, dcol, dcol], [(0, 2, 0), (1, 3, 1)], 2, (D_ATT, cp), TN,
        (1, N_CHIPS, T // tk), two, [psh, psh], [pspec, pspec])

    dqs, dks, dvs = [], [], []
    for g, (window, dil) in enumerate(DIL_GROUPS):
        dq, dk, dv = _dil_bwd(qk, proj, do_dil, o_dil, lse, g, dil)
        dqs.append(dq)
        dks.append(dk)
        dvs.append(dv)
    dqk = _rope(jnp.concatenate(dqs + dks, axis=1), tabs, True, BF16)
    dq_s, dk_s, dv_s = _sb_bwd(proj, do_sb, o_sb)
    dproj = jnp.concatenate(
        [dqk] + [a.astype(BF16) for a in dvs + [dq_s, dk_s, dv_s]] + [dgd, dgs], axis=1)

    dx, dgain = _mm(
        "mix_dx", [dproj, win, x, gain, dxo],
        [_tok(cin, 2), _wspec(D_MODEL, cin, l, 2), _tok(D_MODEL), _gain_spec(), _tok(D_MODEL)],
        [(0, 1, 0)], 1, (TM, D_MODEL), NT, (T // TM, 1, N_CHIPS), _rms_bwd_epilogue(2, 3, 4),
        [jax.ShapeDtypeStruct((T, D_MODEL), F32), jax.ShapeDtypeStruct((8, D_MODEL), F32)],
        [_tok(D_MODEL), pl.BlockSpec((8, D_MODEL), lambda i, j, k: (0, 0))])

    (dwin,) = _mm(
        "mix_dwin", [h, dproj],
        [pl.BlockSpec((tk, D_MODEL), lambda i, j, k: (k, 0)), pl.BlockSpec((tk, cin), lambda i, j, k: (k, j))],
        [(0, 1, 0)], 1, (D_MODEL, cin), TN, (1, N_CHIPS, T // tk), one,
        [jax.ShapeDtypeStruct((N_CHIPS, D_MODEL, cin), BF16)],
        [pl.BlockSpec((None, D_MODEL, cin), lambda i, j, k: (j, 0, 0))])
    return dx, dgain, dwin, dwpd, dwps, dwo


def _local_step(x, target, norms, norm_final, W):
    T = x.shape[0]
    tabs = _rope_tables(T)
    saved = []
    for l in range(DEPTH):
        x, s1 = _ffn_fwd(x, norms["norm_ffn1"][l:l + 1], W["ffn1_w_gate"], W["ffn1_w_up"], W["ffn1_w_down"], l)
        x, s2 = _mixer_fwd(x, norms["norm_mix"][l:l + 1], W, l, tabs)
        x, s3 = _ffn_fwd(x, norms["norm_ffn2"][l:l + 1], W["ffn2_w_gate"], W["ffn2_w_up"], W["ffn2_w_down"], l)
        saved.append((s1, s2, s3))
    dx, dg_final, loss = _final_loss(x, norm_final.reshape(1, D_MODEL), target)
    grads = [None] * DEPTH
    gains = [None] * DEPTH
    for l in reversed(range(DEPTH)):
        s1, s2, s3 = saved[l]
        dx, dg2, dwg2, dwu2, dwd2 = _ffn_bwd(dx, norms["norm_ffn2"][l:l + 1], W["ffn2_w_gate"], W["ffn2_w_up"],
                                             W["ffn2_w_down"], l, s3)
        dx, dgm, dwin, dwpd, dwps, dwo = _mixer_bwd(dx, norms["norm_mix"][l:l + 1], W, l, tabs, s2)
        dx, dg1, dwg1, dwu1, dwd1 = _ffn_bwd(dx, norms["norm_ffn1"][l:l + 1], W["ffn1_w_gate"], W["ffn1_w_up"],
                                             W["ffn1_w_down"], l, s1)
        grads[l] = dict(ffn1_w_gate=dwg1, ffn1_w_up=dwu1, ffn1_w_down=dwd1, w_in=dwin, w_proj_dil=dwpd,
                        w_proj_sb=dwps, w_out=dwo, ffn2_w_gate=dwg2, ffn2_w_up=dwu2, ffn2_w_down=dwd2)
        gains[l] = dict(norm_ffn1=dg1, norm_mix=dgm, norm_ffn2=dg2)
    return loss, dx, grads, gains, dg_final


def _place():
    x, y, c = lax.axis_index("x"), lax.axis_index("y"), lax.axis_index("c")
    chips = [(1 - x, y), (x, 1 - y), (1 - x, 1 - y)]
    return x, y, c, chips


def _gather_weights(ws):
    n = len(ws)

    def body(*refs):
        w_refs, out_refs = refs[:n], refs[n:2 * n]
        send_sems, recv_sems, fsend_sems, frecv_sems, local_sems = refs[2 * n:]
        x, y, c, chips = _place()
        me = 2 * x + y
        sibling = (x, y, 1 - c)
        local = [pltpu.make_async_copy(w_refs[a], out_refs[a].at[me], local_sems.at[a]) for a in range(n)]
        for cp in local:
            cp.start()

        def ici(a, j, chip_id, to):
            return pltpu.make_async_remote_copy(
                src_ref=w_refs[a].at[c], dst_ref=out_refs[a].at[chip_id, c],
                send_sem=send_sems.at[a, j], recv_sem=recv_sems.at[a, j], device_id=to, device_id_type=MESH)

        def d2d(a, j, chip_id, layer):
            return pltpu.make_async_remote_copy(
                src_ref=out_refs[a].at[chip_id, layer], dst_ref=out_refs[a].at[chip_id, layer],
                send_sem=fsend_sems.at[a, j], recv_sem=frecv_sems.at[a, j], device_id=sibling, device_id_type=MESH)

        sends = [ici(a, j, me, (*chip, c)) for a in range(n) for j, chip in enumerate(chips)]
        for cp in sends:
            cp.start()
        passed = []
        for a in range(n):
            for j, chip in enumerate(chips):
                cid = 2 * chip[0] + chip[1]
                ici(a, j, cid, (*chip, c)).wait_recv()
                fw = d2d(a, j, cid, c)
                fw.start()
                passed.append(fw)
        for a in range(n):
            for j, chip in enumerate(chips):
                d2d(a, j, 2 * chip[0] + chip[1], 1 - c).wait_recv()
        for cp in sends + passed:
            cp.wait_send()
        for cp in local:
            cp.wait()

    any_spec = pl.BlockSpec(memory_space=pl.ANY)
    return pl.pallas_call(
        body, name="gather_weights", in_specs=[any_spec] * n, out_specs=[any_spec] * n,
        out_shape=[jax.ShapeDtypeStruct((N_CHIPS,) + w.shape, w.dtype) for w in ws],
        scratch_shapes=[pltpu.SemaphoreType.DMA((n, 3))] * 4 + [pltpu.SemaphoreType.DMA((n,))],
    )(*ws)


def _half(c, r):
    return pl.ds(pl.multiple_of(c * (r // 2), 8), r // 2)


def _exchange_halves(gs):
    n = len(gs)

    def body(*refs):
        g_refs, out_refs = refs[:n], refs[n:2 * n]
        send_sems, recv_sems = refs[2 * n:]
        x, y, c, _ = _place()
        cps = []
        for a in range(n):
            r = g_refs[a].shape[2]
            cps.append(pltpu.make_async_remote_copy(
                src_ref=g_refs[a].at[:, :, _half(1 - c, r), :], dst_ref=out_refs[a],
                send_sem=send_sems.at[a], recv_sem=recv_sems.at[a], device_id=(x, y, 1 - c), device_id_type=MESH))
        for cp in cps:
            cp.start()
        for cp in cps:
            cp.wait()

    any_spec = pl.BlockSpec(memory_space=pl.ANY)
    return pl.pallas_call(
        body, name="grad_to_sibling", in_specs=[any_spec] * n, out_specs=[any_spec] * n,
        out_shape=[jax.ShapeDtypeStruct((g.shape[0], g.shape[1], g.shape[2] // 2, g.shape[3]), g.dtype) for g in gs],
        scratch_shapes=[pltpu.SemaphoreType.DMA((n,))] * 2,
    )(*gs)


def _add_half(g, got, c_arr):
    n, _, r, cw = g.shape

    def body(c_ref, a_ref, b_ref, o_ref):
        o_ref[...] = (a_ref[...].astype(F32) + b_ref[...].astype(F32)).astype(BF16)

    return pl.pallas_call(
        body, name="grad_add_half",
        grid_spec=pltpu.PrefetchScalarGridSpec(
            num_scalar_prefetch=1, grid=(n, N_CHIPS),
            in_specs=[pl.BlockSpec((None, None, r // 2, cw), lambda a, k, cr: (a, k, cr[0], 0)),
                      pl.BlockSpec((None, None, r // 2, cw), lambda a, k, cr: (a, k, 0, 0))],
            out_specs=pl.BlockSpec((None, None, r // 2, cw), lambda a, k, cr: (a, k, 0, 0))),
        out_shape=jax.ShapeDtypeStruct((n, N_CHIPS, r // 2, cw), BF16), compiler_params=_params(),
    )(c_arr, g, got)


def _scatter_to_chips(ss):
    n = len(ss)

    def body(*refs):
        s_refs, out_refs = refs[:n], refs[n:2 * n]
        send_sems, recv_sems, local_sems = refs[2 * n:]
        x, y, c, chips = _place()
        me = 2 * x + y
        local = [pltpu.make_async_copy(s_refs[a].at[:, me], out_refs[a].at[:, me], local_sems.at[a])
                 for a in range(n)]
        for cp in local:
            cp.start()
        cps = []
        for a in range(n):
            for j, chip in enumerate(chips):
                cid = 2 * chip[0] + chip[1]
                cps.append(pltpu.make_async_remote_copy(
                    src_ref=s_refs[a].at[:, cid], dst_ref=out_refs[a].at[:, me],
                    send_sem=send_sems.at[a, j], recv_sem=recv_sems.at[a, j], device_id=(*chip, c),
                    device_id_type=MESH))
        for cp in cps:
            cp.start()
        for cp in cps:
            cp.wait()
        for cp in local:
            cp.wait()

    any_spec = pl.BlockSpec(memory_space=pl.ANY)
    return pl.pallas_call(
        body, name="grad_to_chips", in_specs=[any_spec] * n, out_specs=[any_spec] * n,
        out_shape=[jax.ShapeDtypeStruct(s.shape, s.dtype) for s in ss],
        scratch_shapes=[pltpu.SemaphoreType.DMA((n, 3))] * 2 + [pltpu.SemaphoreType.DMA((n,))],
    )(*ss)


def _sum_chips(got):
    n, _, rh, cw = got.shape

    def body(g_ref, o_ref):
        acc = g_ref[0].astype(F32)
        for k in range(1, N_CHIPS):
            acc = acc + g_ref[k].astype(F32)
        o_ref[...] = acc

    return pl.pallas_call(
        body, name="grad_sum_chips", grid=(n,),
        in_specs=[pl.BlockSpec((None, N_CHIPS, rh, cw), lambda a: (a, 0, 0, 0))],
        out_specs=pl.BlockSpec((None, rh, cw), lambda a: (a, 0, 0)),
        out_shape=jax.ShapeDtypeStruct((n, rh, cw), F32), compiler_params=_params(),
    )(got)


def _join_halves(fs):
    n = len(fs)

    def body(*refs):
        f_refs, out_refs = refs[:n], refs[n:2 * n]
        send_sems, recv_sems, local_sems = refs[2 * n:]
        x, y, c, _ = _place()
        cps, local = [], []
        for a in range(n):
            r = out_refs[a].shape[1]
            dst = out_refs[a].at[:, _half(c, r), :]
            local.append(pltpu.make_async_copy(f_refs[a], dst, local_sems.at[a]))
            cps.append(pltpu.make_async_remote_copy(
                src_ref=f_refs[a], dst_ref=dst, send_sem=send_sems.at[a], recv_sem=recv_sems.at[a],
                device_id=(x, y, 1 - c), device_id_type=MESH))
        for cp in local + cps:
            cp.start()
        for cp in cps:
            cp.wait_send()
        for a in range(n):
            r = out_refs[a].shape[1]
            pltpu.make_async_remote_copy(
                src_ref=f_refs[a], dst_ref=out_refs[a].at[:, _half(1 - c, r), :], send_sem=send_sems.at[a],
                recv_sem=recv_sems.at[a], device_id=(x, y, 1 - c), device_id_type=MESH).wait_recv()
        for cp in local:
            cp.wait()

    any_spec = pl.BlockSpec(memory_space=pl.ANY)
    return pl.pallas_call(
        body, name="grad_join_halves", in_specs=[any_spec] * n, out_specs=[any_spec] * n,
        out_shape=[jax.ShapeDtypeStruct((f.shape[0], 2 * f.shape[1], f.shape[2]), f.dtype) for f in fs],
        scratch_shapes=[pltpu.SemaphoreType.DMA((n,))] * 3,
    )(*fs)


def _allreduce_rows(stats):
    def body(s_ref, o_ref, buf, send_sems, recv_sems):
        x, y, c, _ = _place()
        me = 4 * x + 2 * y + c
        buf[me] = s_ref[...]
        cps = []
        for k in range(1, 8):
            px = jnp.where(k & 4, 1 - x, x)
            py = jnp.where(k & 2, 1 - y, y)
            pc = jnp.where(k & 1, 1 - c, c)
            cps.append(pltpu.make_async_remote_copy(
                src_ref=s_ref, dst_ref=buf.at[me], send_sem=send_sems.at[k - 1], recv_sem=recv_sems.at[k - 1],
                device_id=(px, py, pc), device_id_type=MESH))
        for cp in cps:
            cp.start()
        for cp in cps:
            cp.wait()
        acc = buf[0]
        for d in range(1, 8):
            acc = acc + buf[d]
        o_ref[...] = acc

    vm = pl.BlockSpec(memory_space=pltpu.VMEM)
    return pl.pallas_call(
        body, name="allreduce_rows", in_specs=[vm], out_specs=vm,
        out_shape=jax.ShapeDtypeStruct(stats.shape, F32),
        scratch_shapes=[pltpu.VMEM((8,) + stats.shape, F32), pltpu.SemaphoreType.DMA((7,)),
                        pltpu.SemaphoreType.DMA((7,))],
    )(stats)


def _adamw_math(w, g, m, v):
    m = ADAM_B1 * m + (1.0 - ADAM_B1) * g
    v = ADAM_B2 * v + (1.0 - ADAM_B2) * (g * g)
    m_hat = m / (1.0 - ADAM_B1 ** ADAM_STEP)
    v_hat = v / (1.0 - ADAM_B2 ** ADAM_STEP)
    delta = -ADAM_LR * (m_hat / (jnp.sqrt(v_hat) + ADAM_EPS) + ADAM_WD * w)
    return delta, m, v


def _adamw(w, m, v, gfull, first):
    L, r, cw = w.shape
    tr = r // 4 if r > 256 else r

    def body(w_ref, m_ref, v_ref, g_ref, go_ref, d_ref, mo_ref, vo_ref):
        g = g_ref[...]
        delta, mn, vn = _adamw_math(w_ref[...], g, m_ref[...], v_ref[...])
        go_ref[...] = g
        d_ref[...] = delta
        mo_ref[...] = mn
        vo_ref[...] = vn

    blk = pl.BlockSpec((None, tr, cw), lambda l, i: (l, i, 0))
    sh = jax.ShapeDtypeStruct(w.shape, F32)
    return pl.pallas_call(
        body, name="adamw", grid=(L, r // tr),
        in_specs=[blk, blk, blk, pl.BlockSpec((None, tr, cw), lambda l, i: (first + l, i, 0))],
        out_specs=[blk] * 4, out_shape=[sh] * 4, compiler_params=_params(),
    )(w, m, v, gfull)


def _adamw_rows(w, m, v, g):
    def body(w_ref, m_ref, v_ref, g_ref, d_ref, mo_ref, vo_ref):
        delta, mn, vn = _adamw_math(w_ref[...], g_ref[...], m_ref[...], v_ref[...])
        d_ref[...] = delta
        mo_ref[...] = mn
        vo_ref[...] = vn

    vm = pl.BlockSpec(memory_space=pltpu.VMEM)
    sh = jax.ShapeDtypeStruct(w.shape, F32)
    return pl.pallas_call(body, name="adamw_rows", in_specs=[vm] * 4, out_specs=[vm] * 3, out_shape=[sh] * 3)(w, m, v, g)


GROUPS = (("ffn1_w_gate", "ffn1_w_up", "ffn2_w_gate", "ffn2_w_up"), ("ffn1_w_down", "ffn2_w_down"),
          ("w_in",), ("w_proj_dil", "w_proj_sb"), ("w_out",))


def _pick_row(blocks):
    row = lax.broadcasted_iota(jnp.int32, (8, D_MODEL), 0)
    out = jnp.zeros((8, D_MODEL), F32)
    for i, b in enumerate(blocks):
        out = out + jnp.where(row == i, b, 0.0)
    return out


def kernel(x, norm_ffn1, ffn1_w_gate, ffn1_w_up, ffn1_w_down, norm_mix, w_in, w_proj_dil, w_proj_sb, w_out, norm_ffn2, ffn2_w_gate, ffn2_w_up, ffn2_w_down, norm_final, loss_target, m_norm_ffn1, m_ffn1_w_gate, m_ffn1_w_up, m_ffn1_w_down, m_norm_mix, m_w_in, m_w_proj_dil, m_w_proj_sb, m_w_out, m_norm_ffn2, m_ffn2_w_gate, m_ffn2_w_up, m_ffn2_w_down, m_norm_final, v_norm_ffn1, v_ffn1_w_gate, v_ffn1_w_up, v_ffn1_w_down, v_norm_mix, v_w_in, v_w_proj_dil, v_w_proj_sb, v_w_out, v_norm_ffn2, v_ffn2_w_gate, v_ffn2_w_up, v_ffn2_w_down, v_norm_final):
    given = dict(locals())
    weights = {n: given[n] for n in WEIGHT_NAMES}
    norms = {n: given[n] for n in NORM_NAMES}

    gathered = _gather_weights([weights[n].astype(BF16) for n in WEIGHT_NAMES])
    W = dict(zip(WEIGHT_NAMES, gathered))
    loss_blk, grad_x, grads, gains, dg_final = _local_step(x[0], loss_target[0], norms, norm_final, W)

    c_arr = lax.axis_index("c").astype(jnp.int32).reshape(1)
    stacked = [jnp.stack([grads[l][n] for n in grp for l in range(DEPTH)]) for grp in GROUPS]
    from_sibling = _exchange_halves(stacked)
    chip_sums = [_add_half(g, got, c_arr) for g, got in zip(stacked, from_sibling)]
    from_chips = _scatter_to_chips(chip_sums)
    reduced = _join_halves([_sum_chips(got) for got in from_chips])

    out = {"grad_x": grad_x[None]}
    for grp, gfull in zip(GROUPS, reduced):
        for i, n in enumerate(grp):
            g, d, mn, vn = _adamw(weights[n], given["m_" + n], given["v_" + n], gfull, i * DEPTH)
            out["grad_" + n], out["delta_" + n], out["new_m_" + n], out["new_v_" + n] = g, d, mn, vn

    rows = [gains[l][n] for n in NORM_NAMES for l in range(DEPTH)] + [dg_final, loss_blk]
    total = _allreduce_rows(_pick_row(rows))
    out["loss"] = total[7, 0]
    wn = jnp.concatenate([given[n] for n in NORM_NAMES] + [norm_final[None], jnp.zeros((1, D_MODEL), F32)])
    mn_ = jnp.concatenate([given["m_" + n] for n in NORM_NAMES] + [m_norm_final[None], jnp.zeros((1, D_MODEL), F32)])
    vn_ = jnp.concatenate([given["v_" + n] for n in NORM_NAMES] + [v_norm_final[None], jnp.ones((1, D_MODEL), F32)])
    d_n, m_n, v_n = _adamw_rows(wn, mn_, vn_, total)
    for i, n in enumerate(NORM_NAMES):
        sl = slice(i * DEPTH, (i + 1) * DEPTH)
        out["grad_" + n], out["delta_" + n], out["new_m_" + n], out["new_v_" + n] = total[sl], d_n[sl], m_n[sl], v_n[sl]
    out["grad_norm_final"], out["delta_norm_final"] = total[6], d_n[6]
    out["new_m_norm_final"], out["new_v_norm_final"] = m_n[6], v_n[6]

    order = WEIGHT_NAMES
    del order
    names = ["norm_ffn1", "ffn1_w_gate", "ffn1_w_up", "ffn1_w_down", "norm_mix", "w_in", "w_proj_dil", "w_proj_sb",
             "w_out", "norm_ffn2", "ffn2_w_gate", "ffn2_w_up", "ffn2_w_down", "norm_final"]
    return (out["loss"], out["grad_x"], *[out["grad_" + n] for n in names], *[out["delta_" + n] for n in names],
            *[out["new_m_" + n] for n in names], *[out["new_v_" + n] for n in names])
```

```python
import functools

import jax
import jax.numpy as jnp
from jax import lax
from jax.experimental import pallas as pl
from jax.experimental.pallas import tpu as pltpu

F32 = jnp.float32
BF16 = jnp.bfloat16

D_MODEL = 1024
DEPTH = 2
N_CHIPS = 4
HEAD_DIM = 64
ROPE_DIM = 16
ROPE_THETA = 500000.0
DIL_GROUPS = ((128, 1), (512, 4), (2048, 16))
SPAN = 128
Q_BLOCK = 128
RMS_EPS = 1e-6
D_ATT = 256
COL_QS = 2304
COL_GD = 3072
COL_GS = 4096
ADAM_LR, ADAM_B1, ADAM_B2, ADAM_EPS, ADAM_WD, ADAM_STEP = 0.001, 0.9, 0.999, 1e-08, 0.01, 10

VMEM_LIMIT = 52 * 1024 * 1024
TM = 512
NEG = -1e30

NN = (((1,), (0,)), ((), ()))
NT = (((1,), (1,)), ((), ()))
TN = (((0,), (0,)), ((), ()))
MESH = pl.DeviceIdType.MESH

WEIGHT_NAMES = ("ffn1_w_gate", "ffn1_w_up", "ffn1_w_down", "w_in", "w_proj_dil",
                "w_proj_sb", "w_out", "ffn2_w_gate", "ffn2_w_up", "ffn2_w_down")
NORM_NAMES = ("norm_ffn1", "norm_mix", "norm_ffn2")


def _params(**kw):
    return pltpu.CompilerParams(vmem_limit_bytes=VMEM_LIMIT, **kw)


def _sigmoid(x):
    return 1.0 / (1.0 + jnp.exp(-x))


def _mm_body(pairs, n_in, n_out, n_acc, dims, nk, epilogue, *refs):
    ins = refs[:n_in]
    outs = refs[n_in:n_in + n_out]
    accs = refs[n_in + n_out:]
    i = pl.program_id(0)
    k = pl.program_id(2)
    parts = [None] * n_acc
    for ia, ib, ic in pairs:
        d = lax.dot_general(ins[ia][...].astype(BF16), ins[ib][...].astype(BF16), dims,
                            preferred_element_type=F32)
        parts[ic] = d if parts[ic] is None else parts[ic] + d
    if nk == 1:
        epilogue(parts, ins, outs, i)
        return

    @pl.when(k == 0)
    def _():
        for c in range(n_acc):
            accs[c][...] = parts[c]

    @pl.when(k > 0)
    def _():
        for c in range(n_acc):
            accs[c][...] += parts[c]

    @pl.when(k == nk - 1)
    def _():
        epilogue([a[...] for a in accs], ins, outs, i)


def _mm(name, ins, in_specs, pairs, n_acc, acc_shape, dims, grid, epilogue, out_shapes, out_specs):
    nk = grid[2]
    scratch = [pltpu.VMEM(acc_shape, F32) for _ in range(n_acc)] if nk > 1 else []
    body = functools.partial(_mm_body, tuple(pairs), len(ins), len(out_shapes), n_acc, dims, nk, epilogue)
    return pl.pallas_call(
        body, name=name, grid=grid, in_specs=in_specs, out_specs=out_specs, out_shape=out_shapes,
        scratch_shapes=scratch,
        compiler_params=_params(dimension_semantics=("arbitrary", "arbitrary", "arbitrary")),
    )(*ins)


def _wspec(r, c, l, by):
    if by == 1:
        return pl.BlockSpec((None, None, r, c), lambda i, j, k: (j, l, 0, 0))
    return pl.BlockSpec((None, None, r, c), lambda i, j, k: (k, l, 0, 0))


def _rms_bwd_epilogue(x_idx, g_idx, dxo_idx):
    def ep(vals, ins, outs, i):
        dh = vals[0]
        x = ins[x_idx][...]
        g = ins[g_idx][...]
        rstd = lax.rsqrt(jnp.mean(x * x, axis=-1, keepdims=True) + RMS_EPS)
        xhat = x * rstd
        dxhat = dh * g
        dx = rstd * (dxhat - xhat * jnp.mean(dxhat * xhat, axis=-1, keepdims=True))
        outs[0][...] = ins[dxo_idx][...] + dx
        dg = jnp.broadcast_to(jnp.sum(dh * xhat, axis=0, keepdims=True), outs[1].shape)

        @pl.when(i == 0)
        def _():
            outs[1][...] = dg

        @pl.when(i > 0)
        def _():
            outs[1][...] += dg
    return ep


def _rms_fwd(x, gain):
    T = x.shape[0]

    def body(x_ref, g_ref, h_ref):
        xv = x_ref[...]
        h = xv * lax.rsqrt(jnp.mean(xv * xv, axis=-1, keepdims=True) + RMS_EPS)
        h_ref[...] = (h * g_ref[...]).astype(BF16)

    return pl.pallas_call(
        body, name="rms_fwd", grid=(T // TM,),
        in_specs=[pl.BlockSpec((TM, D_MODEL), lambda i: (i, 0)), pl.BlockSpec((1, D_MODEL), lambda i: (0, 0))],
        out_specs=pl.BlockSpec((TM, D_MODEL), lambda i: (i, 0)),
        out_shape=jax.ShapeDtypeStruct((T, D_MODEL), BF16), compiler_params=_params(),
    )(x, gain)


def _rope_tables(T):
    pos = jnp.arange(T, dtype=F32)
    inv_freq = ROPE_THETA ** (-jnp.arange(0, ROPE_DIM, 2, dtype=F32) / ROPE_DIM)
    ang = pos[:, None] * inv_freq[None, :]
    cos, sin = jnp.cos(ang), jnp.sin(ang)
    half = ROPE_DIM // 2
    one = jnp.ones((T, HEAD_DIM - ROPE_DIM), F32)
    zero = jnp.zeros((T, HEAD_DIM - ROPE_DIM), F32)
    zh = jnp.zeros((T, half), F32)
    c = jnp.concatenate([cos, cos, one], axis=1)
    s1 = jnp.concatenate([-sin, zh, zero], axis=1)
    s2 = jnp.concatenate([zh, sin, zero], axis=1)
    return tuple(jnp.concatenate([t, t], axis=1) for t in (c, s1, s2))


def _rope(proj, tabs, inverse, out_dtype):
    T = proj.shape[0]
    half = ROPE_DIM // 2

    def body(x_ref, c_ref, s1_ref, s2_ref, o_ref):
        xv = x_ref[...]
        c, s1, s2 = c_ref[...], s1_ref[...], s2_ref[...]
        if inverse:
            y = xv * c + pltpu.roll(xv * s1, half, 1) + pltpu.roll(xv * s2, 128 - half, 1)
        else:
            y = xv * c + pltpu.roll(xv, 128 - half, 1) * s1 + pltpu.roll(xv, half, 1) * s2
        o_ref[...] = y.astype(out_dtype)

    tab = pl.BlockSpec((TM, 128), lambda i, j: (i, 0))
    return pl.pallas_call(
        body, name="rope_inv" if inverse else "rope", grid=(T // TM, 6 * D_ATT // 128),
        in_specs=[pl.BlockSpec((TM, 128), lambda i, j: (i, j)), tab, tab, tab],
        out_specs=pl.BlockSpec((TM, 128), lambda i, j: (i, j)),
        out_shape=jax.ShapeDtypeStruct((T, 6 * D_ATT), out_dtype), compiler_params=_params(),
    )(proj, *tabs)


def _dil_merge(os_, lses):
    T = os_[0].shape[0]

    def body(o0, o1, o2, l0, l1, l2, o_ref, lse_ref):
        a, b, c = l0[...], l1[...], l2[...]
        m = jnp.maximum(jnp.maximum(a, b), c)
        ea, eb, ec = jnp.exp(a - m), jnp.exp(b - m), jnp.exp(c - m)
        den = ea + eb + ec
        o_ref[...] = (ea * o0[...] + eb * o1[...] + ec * o2[...]) / den
        lse_ref[...] = m + jnp.log(den)

    blk = pl.BlockSpec((TM, D_ATT), lambda i: (i, 0))
    sh = jax.ShapeDtypeStruct((T, D_ATT), F32)
    return pl.pallas_call(
        body, name="dil_merge", grid=(T // TM,), in_specs=[blk] * 6, out_specs=[blk, blk],
        out_shape=[sh, sh], compiler_params=_params(),
    )(*os_, *lses)


def _final_loss(x, gain, target):
    T = x.shape[0]

    def body(x_ref, g_ref, t_ref, dx_ref, dg_ref, loss_ref):
        xv = x_ref[...]
        g = g_ref[...]
        rstd = lax.rsqrt(jnp.mean(xv * xv, axis=-1, keepdims=True) + RMS_EPS)
        xhat = xv * rstd
        err = xhat * g - t_ref[...]
        loss = 0.5 * jnp.sum(jnp.mean(err * err, axis=-1, keepdims=True), axis=0, keepdims=True)
        dy = err * (1.0 / D_MODEL)
        dxhat = dy * g
        dx_ref[...] = rstd * (dxhat - xhat * jnp.mean(dxhat * xhat, axis=-1, keepdims=True))
        dg = jnp.broadcast_to(jnp.sum(dy * xhat, axis=0, keepdims=True), dg_ref.shape)
        ls = jnp.broadcast_to(loss, loss_ref.shape)

        @pl.when(pl.program_id(0) == 0)
        def _():
            dg_ref[...] = dg
            loss_ref[...] = ls

        @pl.when(pl.program_id(0) > 0)
        def _():
            dg_ref[...] += dg
            loss_ref[...] += ls

    blk = pl.BlockSpec((TM, D_MODEL), lambda i: (i, 0))
    row = pl.BlockSpec((1, D_MODEL), lambda i: (0, 0))
    acc = pl.BlockSpec((8, D_MODEL), lambda i: (0, 0))
    return pl.pallas_call(
        body, name="final_loss", grid=(T // TM,), in_specs=[blk, row, blk], out_specs=[blk, acc, acc],
        out_shape=[jax.ShapeDtypeStruct((T, D_MODEL), F32), jax.ShapeDtypeStruct((8, D_MODEL), F32),
                   jax.ShapeDtypeStruct((8, D_MODEL), F32)],
        compiler_params=_params(dimension_semantics=("arbitrary",)),
    )(x, gain, target)


def _head_masks():
    lane = lax.broadcasted_iota(jnp.int32, (SPAN, 128), 1)
    return lane < HEAD_DIM


def _dil_rows(idx, d):
    u = idx // d
    r = idx - u * d
    own = pl.ds(u * (SPAN * d) + r, SPAN, stride=d) if d > 1 else pl.ds(pl.multiple_of(u * SPAN, SPAN), SPAN)
    up = jnp.maximum(u - 1, 0)
    prev = pl.ds(up * (SPAN * d) + r, SPAN, stride=d) if d > 1 else pl.ds(pl.multiple_of(up * SPAN, SPAN), SPAN)
    return u, own, prev


def _dil_scores(qm, k_own, k_prev, u):
    qi = lax.broadcasted_iota(jnp.int32, (SPAN, SPAN), 0)
    kj = lax.broadcasted_iota(jnp.int32, (SPAN, SPAN), 1)
    s_own = lax.dot_general(qm, k_own, NT, preferred_element_type=F32) * (HEAD_DIM ** -0.5)
    s_prev = lax.dot_general(qm, k_prev, NT, preferred_element_type=F32) * (HEAD_DIM ** -0.5)
    ok_own = kj <= qi
    ok_prev = kj >= qi + jnp.where(u > 0, 0, SPAN)
    return s_own, s_prev, ok_own, ok_prev


def _dil_fwd(qk, proj, g, d):
    T = proj.shape[0]
    n_iter = T // SPAN

    def body(q_ref, k_ref, v_ref, o_ref, lse_ref):
        first = _head_masks()

        def step(idx, carry):
            u, own, prev = _dil_rows(idx, d)
            q = q_ref[own, :]
            k_own = k_ref[own, :].astype(BF16)
            k_prev = k_ref[prev, :].astype(BF16)
            v_own = v_ref[own, :].astype(BF16)
            v_prev = v_ref[prev, :].astype(BF16)
            o_h, lse_h = [], []
            for h in range(2):
                qm = jnp.where(first if h == 0 else ~first, q, 0.0).astype(BF16)
                s_own, s_prev, ok_own, ok_prev = _dil_scores(qm, k_own, k_prev, u)
                s_own = jnp.where(ok_own, s_own, NEG)
                s_prev = jnp.where(ok_prev, s_prev, NEG)
                m = jnp.maximum(jnp.max(s_own, axis=1, keepdims=True), jnp.max(s_prev, axis=1, keepdims=True))
                p_own = jnp.exp(s_own - m)
                p_prev = jnp.exp(s_prev - m)
                den = jnp.sum(p_own, axis=1, keepdims=True) + jnp.sum(p_prev, axis=1, keepdims=True)
                pv = (lax.dot_general(p_own.astype(BF16), v_own, NN, preferred_element_type=F32)
                      + lax.dot_general(p_prev.astype(BF16), v_prev, NN, preferred_element_type=F32))
                o_h.append(pv / den)
                lse_h.append(jnp.broadcast_to(m + jnp.log(den), (SPAN, 128)))
            o_ref[own, :] = jnp.where(first, o_h[0], o_h[1])
            lse_ref[own, :] = jnp.where(first, lse_h[0], lse_h[1])
            return carry

        lax.fori_loop(0, n_iter, step, 0)

    def col(b):
        return pl.BlockSpec((T, 128), lambda p: (0, b + p))

    sh = jax.ShapeDtypeStruct((T, D_ATT), F32)
    out = pl.BlockSpec((T, 128), lambda p: (0, p))
    return pl.pallas_call(
        body, name=f"dil_fwd_d{d}", grid=(2,),
        in_specs=[col(2 * g), col(6 + 2 * g), col(12 + 2 * g)], out_specs=[out, out], out_shape=[sh, sh],
        compiler_params=_params(dimension_semantics=("arbitrary",)),
    )(qk, qk, proj)


def _dil_bwd(qk, proj, do, o_dil, lse, g, d):
    T = proj.shape[0]
    n_iter = T // SPAN

    def body(q_ref, k_ref, v_ref, do_ref, o_ref, lse_ref, dq_ref, dk_ref, dv_ref):
        first = _head_masks()

        def step(idx, carry):
            u, own, prev = _dil_rows(idx, d)
            q = q_ref[own, :]
            k_own = k_ref[own, :].astype(BF16)
            k_prev = k_ref[prev, :].astype(BF16)
            v_own = v_ref[own, :].astype(BF16)
            v_prev = v_ref[prev, :].astype(BF16)
            do_v = do_ref[own, :]
            oo = o_ref[own, :]
            ls = lse_ref[own, :]
            dq_h = []
            dk_own = dk_prev = dv_own = dv_prev = None
            for h in range(2):
                hm = first if h == 0 else ~first
                qm = jnp.where(hm, q, 0.0).astype(BF16)
                dom = jnp.where(hm, do_v, 0.0)
                dob = dom.astype(BF16)
                delta = jnp.sum(dom * oo, axis=1, keepdims=True)
                lrow = jnp.max(jnp.where(hm, ls, NEG), axis=1, keepdims=True)
                s_own, s_prev, ok_own, ok_prev = _dil_scores(qm, k_own, k_prev, u)
                p_own = jnp.where(ok_own, jnp.exp(s_own - lrow), 0.0)
                p_prev = jnp.where(ok_prev, jnp.exp(s_prev - lrow), 0.0)
                dp_own = lax.dot_general(dob, v_own, NT, preferred_element_type=F32)
                dp_prev = lax.dot_general(dob, v_prev, NT, preferred_element_type=F32)
                ds_own = (p_own * (dp_own - delta) * (HEAD_DIM ** -0.5)).astype(BF16)
                ds_prev = (p_prev * (dp_prev - delta) * (HEAD_DIM ** -0.5)).astype(BF16)
                dq_h.append(lax.dot_general(ds_own, k_own, NN, preferred_element_type=F32)
                            + lax.dot_general(ds_prev, k_prev, NN, preferred_element_type=F32))
                a = lax.dot_general(ds_own, qm, TN, preferred_element_type=F32)
                b = lax.dot_general(ds_prev, qm, TN, preferred_element_type=F32)
                c = lax.dot_general(p_own.astype(BF16), dob, TN, preferred_element_type=F32)
                e = lax.dot_general(p_prev.astype(BF16), dob, TN, preferred_element_type=F32)
                dk_own = a if dk_own is None else dk_own + a
                dk_prev = b if dk_prev is None else dk_prev + b
                dv_own = c if dv_own is None else dv_own + c
                dv_prev = e if dv_prev is None else dv_prev + e
            dq_ref[own, :] = jnp.where(first, dq_h[0], dq_h[1])
            dk_ref[own, :] = dk_own
            dv_ref[own, :] = dv_own
            dk_ref[prev, :] = dk_ref[prev, :] + dk_prev
            dv_ref[prev, :] = dv_ref[prev, :] + dv_prev
            return carry

        lax.fori_loop(0, n_iter, step, 0)

    def col(b):
        return pl.BlockSpec((T, 128), lambda p: (0, b + p))

    sh = jax.ShapeDtypeStruct((T, D_ATT), F32)
    return pl.pallas_call(
        body, name=f"dil_bwd_d{d}", grid=(2,),
        in_specs=[col(2 * g), col(6 + 2 * g), col(12 + 2 * g), col(0), col(0), col(0)],
        out_specs=[col(0), col(0), col(0)], out_shape=[sh, sh, sh],
        compiler_params=_params(dimension_semantics=("arbitrary",)),
    )(qk, qk, proj, do, o_dil, lse)


SB_KT = 512


def _sb_tri(strict):
    a = lax.broadcasted_iota(jnp.int32, (Q_BLOCK, Q_BLOCK), 0)
    b = lax.broadcasted_iota(jnp.int32, (Q_BLOCK, Q_BLOCK), 1)
    return jnp.where((a > b) if strict else (a >= b), 1.0, 0.0).astype(BF16)


def _suffix(x, c, tri):
    nb = x.shape[1] // Q_BLOCK
    blocks = [x[:, Q_BLOCK * b:Q_BLOCK * (b + 1)] for b in range(nb)]
    hi = [b.astype(BF16) for b in blocks]
    lo = [(b - h.astype(F32)).astype(BF16) for b, h in zip(blocks, hi)]
    y = lax.dot_general(jnp.concatenate(hi + lo, axis=0), tri, NN, preferred_element_type=F32)
    outs = [None] * nb
    run = c
    for b in reversed(range(nb)):
        outs[b] = run + y[Q_BLOCK * b:Q_BLOCK * (b + 1)] + y[Q_BLOCK * (nb + b):Q_BLOCK * (nb + b + 1)]
        run = run + jnp.sum(blocks[b], axis=1, keepdims=True)
    return jnp.concatenate(outs, axis=1), run


def _sb_tile(qm, kb, past, c, tri):
    z = lax.dot_general(qm, kb, NT, preferred_element_type=F32) * (HEAD_DIM ** -0.5)
    lsz = jnp.minimum(z, 0.0) - jnp.log1p(jnp.exp(-jnp.abs(z)))
    lk = jnp.where(past, lsz - z, 0.0)
    after, c_new = _suffix(lk, c, tri)
    w = jnp.where(past, jnp.exp(lsz + after), 0.0)
    return z, lsz, w, c_new


def _sb_past(i, t):
    row = lax.broadcasted_iota(jnp.int32, (Q_BLOCK, SB_KT), 0)
    col = lax.broadcasted_iota(jnp.int32, (Q_BLOCK, SB_KT), 1)
    return col + t * SB_KT < row + i * Q_BLOCK


def _sb_fwd(proj):
    T = proj.shape[0]

    def body(q_ref, k_ref, v_ref, o_ref):
        i = pl.program_id(1)
        first = _head_masks()
        tri = _sb_tri(True)
        q = q_ref[...]
        qms = [jnp.where(first, q, 0.0).astype(BF16), jnp.where(first, 0.0, q).astype(BF16)]
        n_tiles = (i * Q_BLOCK) // SB_KT + 1

        def step(tt, carry):
            t = n_tiles - 1 - tt
            rows = pl.ds(pl.multiple_of(t * SB_KT, SB_KT), SB_KT)
            kb = k_ref[rows, :].astype(BF16)
            vb = v_ref[rows, :].astype(BF16)
            past = _sb_past(i, t)
            out = []
            for h in range(2):
                acc, c = carry[2 * h], carry[2 * h + 1]
                _, _, w, c = _sb_tile(qms[h], kb, past, c, tri)
                out += [acc + lax.dot_general(w.astype(BF16), vb, NN, preferred_element_type=F32), c]
            return tuple(out)

        zero, zcol = jnp.zeros((Q_BLOCK, 128), F32), jnp.zeros((Q_BLOCK, 1), F32)
        res = lax.fori_loop(0, n_tiles, step, (zero, zcol, zero, zcol))
        o_ref[...] = jnp.where(first, res[0], res[2])

    cb = COL_QS // 128
    return pl.pallas_call(
        body, name="sb_fwd", grid=(2, T // Q_BLOCK),
        in_specs=[pl.BlockSpec((Q_BLOCK, 128), lambda p, i: (i, cb + p)),
                  pl.BlockSpec((T, 128), lambda p, i: (0, cb + 2 + p)),
                  pl.BlockSpec((T, 128), lambda p, i: (0, cb + 4 + p))],
        out_specs=pl.BlockSpec((Q_BLOCK, 128), lambda p, i: (i, p)),
        out_shape=jax.ShapeDtypeStruct((T, D_ATT), F32),
        compiler_params=_params(dimension_semantics=("arbitrary", "arbitrary")),
    )(proj, proj, proj)


def _sb_bwd(proj, do, o):
    T = proj.shape[0]

    def body(q_ref, k_ref, v_ref, do_ref, o_ref, dq_ref, dk_ref, dv_ref):
        i = pl.program_id(1)
        first = _head_masks()
        tri = _sb_tri(True)
        tri_incl = _sb_tri(False)

        @pl.when(i == 0)
        def _():
            dk_ref[...] = jnp.zeros_like(dk_ref)
            dv_ref[...] = jnp.zeros_like(dv_ref)

        q = q_ref[...]
        do_v = do_ref[...]
        oo = o_ref[...]
        qms = [jnp.where(first, q, 0.0).astype(BF16), jnp.where(first, 0.0, q).astype(BF16)]
        dobs = [jnp.where(first, do_v, 0.0).astype(BF16), jnp.where(first, 0.0, do_v).astype(BF16)]
        deltas = [jnp.sum(d.astype(F32) * oo, axis=1, keepdims=True) for d in dobs]
        n_tiles = (i * Q_BLOCK) // SB_KT + 1

        def step(tt, carry):
            t = n_tiles - 1 - tt
            rows = pl.ds(pl.multiple_of(t * SB_KT, SB_KT), SB_KT)
            kb = k_ref[rows, :].astype(BF16)
            vb = v_ref[rows, :].astype(BF16)
            past = _sb_past(i, t)
            out = []
            dk_t = dv_t = None
            for h in range(2):
                dq, c, ce = carry[3 * h:3 * h + 3]
                z, lsz, w, c = _sb_tile(qms[h], kb, past, c, tri)
                gv = lax.dot_general(dobs[h], vb, NT, preferred_element_type=F32)
                wb = w.astype(BF16)
                e = wb.astype(F32) * gv
                suf, ce = _suffix(e, ce, tri_incl)
                big_e = deltas[h] - suf
                dz = jnp.where(past, e * jnp.exp(lsz - z) - big_e * jnp.exp(lsz), 0.0) * (HEAD_DIM ** -0.5)
                dzb = dz.astype(BF16)
                dq = dq + lax.dot_general(dzb, kb, NN, preferred_element_type=F32)
                a = lax.dot_general(dzb, qms[h], TN, preferred_element_type=F32)
                b = lax.dot_general(wb, dobs[h], TN, preferred_element_type=F32)
                dk_t = a if dk_t is None else dk_t + a
                dv_t = b if dv_t is None else dv_t + b
                out += [dq, c, ce]
            dk_ref[rows, :] = dk_ref[rows, :] + dk_t
            dv_ref[rows, :] = dv_ref[rows, :] + dv_t
            return tuple(out)

        zero, zcol = jnp.zeros((Q_BLOCK, 128), F32), jnp.zeros((Q_BLOCK, 1), F32)
        res = lax.fori_loop(0, n_tiles, step, (zero, zcol, zcol, zero, zcol, zcol))
        dq_ref[...] = jnp.where(first, res[0], res[3])

    cb = COL_QS // 128
    blk = pl.BlockSpec((Q_BLOCK, 128), lambda p, i: (i, p))
    full = pl.BlockSpec((T, 128), lambda p, i: (0, p))
    sh = jax.ShapeDtypeStruct((T, D_ATT), F32)
    return pl.pallas_call(
        body, name="sb_bwd", grid=(2, T // Q_BLOCK),
        in_specs=[pl.BlockSpec((Q_BLOCK, 128), lambda p, i: (i, cb + p)),
                  pl.BlockSpec((T, 128), lambda p, i: (0, cb + 2 + p)),
                  pl.BlockSpec((T, 128), lambda p, i: (0, cb + 4 + p)), blk, blk],
        out_specs=[blk, full, full], out_shape=[sh, sh, sh],
        compiler_params=_params(dimension_semantics=("arbitrary", "arbitrary")),
    )(proj, proj, proj, do, o)


def _tok(c, by=None):
    if by is None:
        return pl.BlockSpec((TM, c), lambda i, j, k: (i, 0))
    if by == 1:
        return pl.BlockSpec((TM, c), lambda i, j, k: (i, j))
    return pl.BlockSpec((TM, c), lambda i, j, k: (i, k))


def _chunked(c, by):
    if by == 1:
        return pl.BlockSpec((None, TM, c), lambda i, j, k: (j, i, 0))
    return pl.BlockSpec((None, TM, c), lambda i, j, k: (k, i, 0))


def _gain_spec():
    return pl.BlockSpec((1, D_MODEL), lambda i, j, k: (0, 0))


def _ffn_fwd(x, gain, wg, wu, wd, l):
    T = x.shape[0]
    ffs = wg.shape[3]
    h = _rms_fwd(x, gain)

    def swiglu(vals, ins, outs, i):
        gt, up = vals
        outs[0][...] = gt.astype(BF16)
        outs[1][...] = up.astype(BF16)
        outs[2][...] = (gt * _sigmoid(gt) * up).astype(BF16)

    csh = jax.ShapeDtypeStruct((N_CHIPS, T, ffs), BF16)
    gate, up, act = _mm(
        "ffn_up", [h, wg, wu], [_tok(D_MODEL), _wspec(D_MODEL, ffs, l, 1), _wspec(D_MODEL, ffs, l, 1)],
        [(0, 1, 0), (0, 2, 1)], 2, None, NN, (T // TM, N_CHIPS, 1), swiglu,
        [csh, csh, csh], [_chunked(ffs, 1)] * 3)

    def resid(vals, ins, outs, i):
        outs[0][...] = ins[2][...] + 0.5 * vals[0]

    (y,) = _mm(
        "ffn_down", [act, wd, x], [_chunked(ffs, 2), _wspec(ffs, D_MODEL, l, 2), _tok(D_MODEL)],
        [(0, 1, 0)], 1, (TM, D_MODEL), NN, (T // TM, 1, N_CHIPS), resid,
        [jax.ShapeDtypeStruct((T, D_MODEL), F32)], [_tok(D_MODEL)])
    return y, (x, h, gate, up, act)


def _ffn_bwd(dxo, gain, wg, wu, wd, l, saved):
    x, h, gate, up, act = saved
    T = x.shape[0]
    ffs = wg.shape[3]
    tk = TM

    def dswiglu(vals, ins, outs, i):
        da = 0.5 * vals[0]
        gt = ins[2][...].astype(F32)
        u = ins[3][...].astype(F32)
        s = _sigmoid(gt)
        outs[0][...] = (da * u * (s * (1.0 + gt * (1.0 - s)))).astype(BF16)
        outs[1][...] = (da * (gt * s)).astype(BF16)

    csh = jax.ShapeDtypeStruct((N_CHIPS, T, ffs), BF16)
    dgate, dup = _mm(
        "ffn_dact", [dxo, wd, gate, up],
        [_tok(D_MODEL), _wspec(ffs, D_MODEL, l, 1), _chunked(ffs, 1), _chunked(ffs, 1)],
        [(0, 1, 0)], 1, None, NT, (T // TM, N_CHIPS, 1), dswiglu, [csh, csh], [_chunked(ffs, 1)] * 2)

    def half(vals, ins, outs, i):
        outs[0][...] = (0.5 * vals[0]).astype(BF16)

    (dwd,) = _mm(
        "ffn_dwd", [act, dxo],
        [pl.BlockSpec((None, tk, ffs), lambda i, j, k: (j, k, 0)), pl.BlockSpec((tk, D_MODEL), lambda i, j, k: (k, 0))],
        [(0, 1, 0)], 1, (ffs, D_MODEL), TN, (1, N_CHIPS, T // tk), half,
        [jax.ShapeDtypeStruct((N_CHIPS, ffs, D_MODEL), BF16)],
        [pl.BlockSpec((None, ffs, D_MODEL), lambda i, j, k: (j, 0, 0))])

    dx, dgain = _mm(
        "ffn_dx", [dgate, dup, wg, wu, x, gain, dxo],
        [_chunked(ffs, 2), _chunked(ffs, 2), _wspec(D_MODEL, ffs, l, 2), _wspec(D_MODEL, ffs, l, 2),
         _tok(D_MODEL), _gain_spec(), _tok(D_MODEL)],
        [(0, 2, 0), (1, 3, 0)], 1, (TM, D_MODEL), NT, (T // TM, 1, N_CHIPS), _rms_bwd_epilogue(4, 5, 6),
        [jax.ShapeDtypeStruct((T, D_MODEL), F32), jax.ShapeDtypeStruct((8, D_MODEL), F32)],
        [_tok(D_MODEL), pl.BlockSpec((8, D_MODEL), lambda i, j, k: (0, 0))])

    def two(vals, ins, outs, i):
        outs[0][...] = vals[0].astype(BF16)
        outs[1][...] = vals[1].astype(BF16)

    wsh = jax.ShapeDtypeStruct((N_CHIPS, D_MODEL, ffs), BF16)
    wout = pl.BlockSpec((None, D_MODEL, ffs), lambda i, j, k: (j, 0, 0))
    cin = pl.BlockSpec((None, tk, ffs), lambda i, j, k: (j, k, 0))
    dwg, dwu = _mm(
        "ffn_dwgu", [h, dgate, dup], [pl.BlockSpec((tk, D_MODEL), lambda i, j, k: (k, 0)), cin, cin],
        [(0, 1, 0), (0, 2, 1)], 2, (D_MODEL, ffs), TN, (1, N_CHIPS, T // tk), two, [wsh, wsh], [wout, wout])
    return dx, dgain, dwg, dwu, dwd


def _mixer_fwd(x, gain, W, l, tabs):
    T = x.shape[0]
    win, wpd, wps, wo = W["w_in"], W["w_proj_dil"], W["w_proj_sb"], W["w_out"]
    cin = win.shape[3]
    cp = wpd.shape[3]
    h = _rms_fwd(x, gain)

    def plain(vals, ins, outs, i):
        outs[0][...] = vals[0]

    (proj,) = _mm(
        "mix_in", [h, win], [_tok(D_MODEL), _wspec(D_MODEL, cin, l, 1)], [(0, 1, 0)], 1, None, NN,
        (T // TM, N_CHIPS, 1), plain, [jax.ShapeDtypeStruct((T, N_CHIPS * cin), F32)], [_tok(cin, 1)])

    qk = _rope(proj, tabs, False, F32)
    os_, lses = [], []
    for g, (window, dil) in enumerate(DIL_GROUPS):
        o_g, lse_g = _dil_fwd(qk, proj, g, dil)
        os_.append(o_g)
        lses.append(lse_g)
    o_dil, lse = _dil_merge(os_, lses)
    o_sb = _sb_fwd(proj)

    def gated(vals, ins, outs, i):
        pd, ps = vals
        outs[0][...] = (_sigmoid(ins[4][...]) * pd + _sigmoid(ins[5][...]) * ps).astype(BF16)
        outs[1][...] = pd.astype(BF16)
        outs[2][...] = ps.astype(BF16)

    gd0, gs0 = COL_GD // cp, COL_GS // cp
    ush = jax.ShapeDtypeStruct((T, D_MODEL), BF16)
    u, pd, ps = _mm(
        "mix_gate", [o_dil, o_sb, wpd, wps, proj, proj],
        [_tok(D_ATT), _tok(D_ATT), _wspec(D_ATT, cp, l, 1), _wspec(D_ATT, cp, l, 1),
         pl.BlockSpec((TM, cp), lambda i, j, k: (i, gd0 + j)), pl.BlockSpec((TM, cp), lambda i, j, k: (i, gs0 + j))],
        [(0, 2, 0), (1, 3, 1)], 2, None, NN, (T // TM, N_CHIPS, 1), gated, [ush] * 3, [_tok(cp, 1)] * 3)

    def resid(vals, ins, outs, i):
        outs[0][...] = ins[2][...] + vals[0]

    (y,) = _mm(
        "mix_out", [u, wo, x], [_tok(cp, 2), _wspec(cp, D_MODEL, l, 2), _tok(D_MODEL)],
        [(0, 1, 0)], 1, (TM, D_MODEL), NN, (T // TM, 1, N_CHIPS), resid,
        [jax.ShapeDtypeStruct((T, D_MODEL), F32)], [_tok(D_MODEL)])
    return y, (x, h, proj, qk, o_dil, lse, o_sb, u, pd, ps)


def _mixer_bwd(dxo, gain, W, l, tabs, saved):
    x, h, proj, qk, o_dil, lse, o_sb, u, pd, ps = saved
    T = x.shape[0]
    win, wpd, wps, wo = W["w_in"], W["w_proj_dil"], W["w_proj_sb"], W["w_out"]
    cin = win.shape[3]
    cp = wpd.shape[3]
    tk = TM
    gd0, gs0 = COL_GD // cp, COL_GS // cp

    def dgated(vals, ins, outs, i):
        du = vals[0]
        sd = _sigmoid(ins[4][...])
        ss = _sigmoid(ins[5][...])
        outs[0][...] = (du * sd).astype(BF16)
        outs[1][...] = (du * ss).astype(BF16)
        outs[2][...] = (du * ins[2][...].astype(F32) * sd * (1.0 - sd)).astype(BF16)
        outs[3][...] = (du * ins[3][...].astype(F32) * ss * (1.0 - ss)).astype(BF16)

    ush = jax.ShapeDtypeStruct((T, D_MODEL), BF16)
    dpd, dps, dgd, dgs = _mm(
        "mix_du", [dxo, wo, pd, ps, proj, proj],
        [_tok(D_MODEL), _wspec(cp, D_MODEL, l, 1), _tok(cp, 1), _tok(cp, 1),
         pl.BlockSpec((TM, cp), lambda i, j, k: (i, gd0 + j)), pl.BlockSpec((TM, cp), lambda i, j, k: (i, gs0 + j))],
        [(0, 1, 0)], 1, None, NT, (T // TM, N_CHIPS, 1), dgated, [ush] * 4, [_tok(cp, 1)] * 4)

    def one(vals, ins, outs, i):
        outs[0][...] = vals[0].astype(BF16)

    def two(vals, ins, outs, i):
        outs[0][...] = vals[0].astype(BF16)
        outs[1][...] = vals[1].astype(BF16)

    (dwo,) = _mm(
        "mix_dwo", [u, dxo],
        [pl.BlockSpec((tk, cp), lambda i, j, k: (k, j)), pl.BlockSpec((tk, D_MODEL), lambda i, j, k: (k, 0))],
        [(0, 1, 0)], 1, (cp, D_MODEL), TN, (1, N_CHIPS, T // tk), one,
        [jax.ShapeDtypeStruct((N_CHIPS, cp, D_MODEL), BF16)],
        [pl.BlockSpec((None, cp, D_MODEL), lambda i, j, k: (j, 0, 0))])

    def plain2(vals, ins, outs, i):
        outs[0][...] = vals[0]
        outs[1][...] = vals[1]

    ash = jax.ShapeDtypeStruct((T, D_ATT), F32)
    do_dil, do_sb = _mm(
        "mix_do", [dpd, dps, wpd, wps], [_tok(cp, 2), _tok(cp, 2), _wspec(D_ATT, cp, l, 2), _wspec(D_ATT, cp, l, 2)],
        [(0, 2, 0), (1, 3, 1)], 2, (TM, D_ATT), NT, (T // TM, 1, N_CHIPS), plain2, [ash, ash], [_tok(D_ATT)] * 2)

    psh = jax.ShapeDtypeStruct((N_CHIPS, D_ATT, cp), BF16)
    pspec = pl.BlockSpec((None, D_ATT, cp), lambda i, j, k: (j, 0, 0))
    arow = pl.BlockSpec((tk, D_ATT), lambda i, j, k: (k, 0))
    dcol = pl.BlockSpec((tk, cp), lambda i, j, k: (k, j))
    dwpd, dwps = _mm(
        "mix_dwp", [o_dil, o_sb, dpd, dps], [arow, arow, dcol, dcol], [(0, 2, 0), (1, 3, 1)], 2, (D_ATT, cp), TN,
        (1, N_CHIPS, T // tk), two, [psh, psh], [pspec, pspec])

    dqs, dks, dvs = [], [], []
    for g, (window, dil) in enumerate(DIL_GROUPS):
        dq, dk, dv = _dil_bwd(qk, proj, do_dil, o_dil, lse, g, dil)
        dqs.append(dq)
        dks.append(dk)
        dvs.append(dv)
    dqk = _rope(jnp.concatenate(dqs + dks, axis=1), tabs, True, BF16)
    dq_s, dk_s, dv_s = _sb_bwd(proj, do_sb, o_sb)
    dproj = jnp.concatenate(
        [dqk] + [a.astype(BF16) for a in dvs + [dq_s, dk_s, dv_s]] + [dgd, dgs], axis=1)

    dx, dgain = _mm(
        "mix_dx", [dproj, win, x, gain, dxo],
        [_tok(cin, 2), _wspec(D_MODEL, cin, l, 2), _tok(D_MODEL), _gain_spec(), _tok(D_MODEL)],
        [(0, 1, 0)], 1, (TM, D_MODEL), NT, (T // TM, 1, N_CHIPS), _rms_bwd_epilogue(2, 3, 4),
        [jax.ShapeDtypeStruct((T, D_MODEL), F32), jax.ShapeDtypeStruct((8, D_MODEL), F32)],
        [_tok(D_MODEL), pl.BlockSpec((8, D_MODEL), lambda i, j, k: (0, 0))])

    (dwin,) = _mm(
        "mix_dwin", [h, dproj],
        [pl.BlockSpec((tk, D_MODEL), lambda i, j, k: (k, 0)), pl.BlockSpec((tk, cin), lambda i, j, k: (k, j))],
        [(0, 1, 0)], 1, (D_MODEL, cin), TN, (1, N_CHIPS, T // tk), one,
        [jax.ShapeDtypeStruct((N_CHIPS, D_MODEL, cin), BF16)],
        [pl.BlockSpec((None, D_MODEL, cin), lambda i, j, k: (j, 0, 0))])
    return dx, dgain, dwin, dwpd, dwps, dwo


def _local_step(x, target, norms, norm_final, W):
    T = x.shape[0]
    tabs = _rope_tables(T)
    saved = []
    for l in range(DEPTH):
        x, s1 = _ffn_fwd(x, norms["norm_ffn1"][l:l + 1], W["ffn1_w_gate"], W["ffn1_w_up"], W["ffn1_w_down"], l)
        x, s2 = _mixer_fwd(x, norms["norm_mix"][l:l + 1], W, l, tabs)
        x, s3 = _ffn_fwd(x, norms["norm_ffn2"][l:l + 1], W["ffn2_w_gate"], W["ffn2_w_up"], W["ffn2_w_down"], l)
        saved.append((s1, s2, s3))
    dx, dg_final, loss = _final_loss(x, norm_final.reshape(1, D_MODEL), target)
    grads = [None] * DEPTH
    gains = [None] * DEPTH
    for l in reversed(range(DEPTH)):
        s1, s2, s3 = saved[l]
        dx, dg2, dwg2, dwu2, dwd2 = _ffn_bwd(dx, norms["norm_ffn2"][l:l + 1], W["ffn2_w_gate"], W["ffn2_w_up"],
                                             W["ffn2_w_down"], l, s3)
        dx, dgm, dwin, dwpd, dwps, dwo = _mixer_bwd(dx, norms["norm_mix"][l:l + 1], W, l, tabs, s2)
        dx, dg1, dwg1, dwu1, dwd1 = _ffn_bwd(dx, norms["norm_ffn1"][l:l + 1], W["ffn1_w_gate"], W["ffn1_w_up"],
                                             W["ffn1_w_down"], l, s1)
        grads[l] = dict(ffn1_w_gate=dwg1, ffn1_w_up=dwu1, ffn1_w_down=dwd1, w_in=dwin, w_proj_dil=dwpd,
                        w_proj_sb=dwps, w_out=dwo, ffn2_w_gate=dwg2, ffn2_w_up=dwu2, ffn2_w_down=dwd2)
        gains[l] = dict(norm_ffn1=dg1, norm_mix=dgm, norm_ffn2=dg2)
    return loss, dx, grads, gains, dg_final


def _place():
    x, y, c = lax.axis_index("x"), lax.axis_index("y"), lax.axis_index("c")
    chips = [(1 - x, y), (x, 1 - y), (1 - x, 1 - y)]
    return x, y, c, chips


def _gather_weights(ws):
    n = len(ws)

    def body(*refs):
        w_refs, out_refs = refs[:n], refs[n:2 * n]
        send_sems, recv_sems, fsend_sems, frecv_sems, local_sems = refs[2 * n:]
        x, y, c, chips = _place()
        me = 2 * x + y
        sibling = (x, y, 1 - c)
        local = [pltpu.make_async_copy(w_refs[a], out_refs[a].at[me], local_sems.at[a]) for a in range(n)]
        for cp in local:
            cp.start()

        def ici(a, j, chip_id, to):
            return pltpu.make_async_remote_copy(
                src_ref=w_refs[a].at[c], dst_ref=out_refs[a].at[chip_id, c],
                send_sem=send_sems.at[a, j], recv_sem=recv_sems.at[a, j], device_id=to, device_id_type=MESH)

        def d2d(a, j, chip_id, layer):
            return pltpu.make_async_remote_copy(
                src_ref=out_refs[a].at[chip_id, layer], dst_ref=out_refs[a].at[chip_id, layer],
                send_sem=fsend_sems.at[a, j], recv_sem=frecv_sems.at[a, j], device_id=sibling, device_id_type=MESH)

        sends = [ici(a, j, me, (*chip, c)) for a in range(n) for j, chip in enumerate(chips)]
        for cp in sends:
            cp.start()
        passed = []
        for a in range(n):
            for j, chip in enumerate(chips):
                cid = 2 * chip[0] + chip[1]
                ici(a, j, cid, (*chip, c)).wait_recv()
                fw = d2d(a, j, cid, c)
                fw.start()
                passed.append(fw)
        for a in range(n):
            for j, chip in enumerate(chips):
                d2d(a, j, 2 * chip[0] + chip[1], 1 - c).wait_recv()
        for cp in sends + passed:
            cp.wait_send()
        for cp in local:
            cp.wait()

    any_spec = pl.BlockSpec(memory_space=pl.ANY)
    return pl.pallas_call(
        body, name="gather_weights", in_specs=[any_spec] * n, out_specs=[any_spec] * n,
        out_shape=[jax.ShapeDtypeStruct((N_CHIPS,) + w.shape, w.dtype) for w in ws],
        scratch_shapes=[pltpu.SemaphoreType.DMA((n, 3))] * 4 + [pltpu.SemaphoreType.DMA((n,))],
    )(*ws)


def _half(c, r):
    return pl.ds(pl.multiple_of(c * (r // 2), 8), r // 2)


def _exchange_halves(gs):
    n = len(gs)

    def body(*refs):
        g_refs, out_refs = refs[:n], refs[n:2 * n]
        send_sems, recv_sems = refs[2 * n:]
        x, y, c, _ = _place()
        cps = []
        for a in range(n):
            r = g_refs[a].shape[2]
            cps.append(pltpu.make_async_remote_copy(
                src_ref=g_refs[a].at[:, :, _half(1 - c, r), :], dst_ref=out_refs[a],
                send_sem=send_sems.at[a], recv_sem=recv_sems.at[a], device_id=(x, y, 1 - c), device_id_type=MESH))
        for cp in cps:
            cp.start()
        for cp in cps:
            cp.wait()

    any_spec = pl.BlockSpec(memory_space=pl.ANY)
    return pl.pallas_call(
        body, name="grad_to_sibling", in_specs=[any_spec] * n, out_specs=[any_spec] * n,
        out_shape=[jax.ShapeDtypeStruct((g.shape[0], g.shape[1], g.shape[2] // 2, g.shape[3]), g.dtype) for g in gs],
        scratch_shapes=[pltpu.SemaphoreType.DMA((n,))] * 2,
    )(*gs)


def _add_half(g, got, c_arr):
    n, _, r, cw = g.shape

    def body(c_ref, a_ref, b_ref, o_ref):
        o_ref[...] = (a_ref[...].astype(F32) + b_ref[...].astype(F32)).astype(BF16)

    return pl.pallas_call(
        body, name="grad_add_half",
        grid_spec=pltpu.PrefetchScalarGridSpec(
            num_scalar_prefetch=1, grid=(n, N_CHIPS),
            in_specs=[pl.BlockSpec((None, None, r // 2, cw), lambda a, k, cr: (a, k, cr[0], 0)),
                      pl.BlockSpec((None, None, r // 2, cw), lambda a, k, cr: (a, k, 0, 0))],
            out_specs=pl.BlockSpec((None, None, r // 2, cw), lambda a, k, cr: (a, k, 0, 0))),
        out_shape=jax.ShapeDtypeStruct((n, N_CHIPS, r // 2, cw), BF16), compiler_params=_params(),
    )(c_arr, g, got)


def _scatter_to_chips(ss):
    n = len(ss)

    def body(*refs):
        s_refs, out_refs = refs[:n], refs[n:2 * n]
        send_sems, recv_sems, local_sems = refs[2 * n:]
        x, y, c, chips = _place()
        me = 2 * x + y
        local = [pltpu.make_async_copy(s_refs[a].at[:, me], out_refs[a].at[:, me], local_sems.at[a])
                 for a in range(n)]
        for cp in local:
            cp.start()
        cps = []
        for a in range(n):
            for j, chip in enumerate(chips):
                cid = 2 * chip[0] + chip[1]
                cps.append(pltpu.make_async_remote_copy(
                    src_ref=s_refs[a].at[:, cid], dst_ref=out_refs[a].at[:, me],
                    send_sem=send_sems.at[a, j], recv_sem=recv_sems.at[a, j], device_id=(*chip, c),
                    device_id_type=MESH))
        for cp in cps:
            cp.start()
        for cp in cps:
            cp.wait()
        for cp in local:
            cp.wait()

    any_spec = pl.BlockSpec(memory_space=pl.ANY)
    return pl.pallas_call(
        body, name="grad_to_chips", in_specs=[any_spec] * n, out_specs=[any_spec] * n,
        out_shape=[jax.ShapeDtypeStruct(s.shape, s.dtype) for s in ss],
        scratch_shapes=[pltpu.SemaphoreType.DMA((n, 3))] * 2 + [pltpu.SemaphoreType.DMA((n,))],
    )(*ss)


def _sum_chips(got):
    n, _, rh, cw = got.shape

    def body(g_ref, o_ref):
        acc = g_ref[0].astype(F32)
        for k in range(1, N_CHIPS):
            acc = acc + g_ref[k].astype(F32)
        o_ref[...] = acc

    return pl.pallas_call(
        body, name="grad_sum_chips", grid=(n,),
        in_specs=[pl.BlockSpec((None, N_CHIPS, rh, cw), lambda a: (a, 0, 0, 0))],
        out_specs=pl.BlockSpec((None, rh, cw), lambda a: (a, 0, 0)),
        out_shape=jax.ShapeDtypeStruct((n, rh, cw), F32), compiler_params=_params(),
    )(got)


def _join_halves(fs):
    n = len(fs)

    def body(*refs):
        f_refs, out_refs = refs[:n], refs[n:2 * n]
        send_sems, recv_sems, local_sems = refs[2 * n:]
        x, y, c, _ = _place()
        cps, local = [], []
        for a in range(n):
            r = out_refs[a].shape[1]
            dst = out_refs[a].at[:, _half(c, r), :]
            local.append(pltpu.make_async_copy(f_refs[a], dst, local_sems.at[a]))
            cps.append(pltpu.make_async_remote_copy(
                src_ref=f_refs[a], dst_ref=dst, send_sem=send_sems.at[a], recv_sem=recv_sems.at[a],
                device_id=(x, y, 1 - c), device_id_type=MESH))
        for cp in local + cps:
            cp.start()
        for cp in cps:
            cp.wait_send()
        for a in range(n):
            r = out_refs[a].shape[1]
            pltpu.make_async_remote_copy(
                src_ref=f_refs[a], dst_ref=out_refs[a].at[:, _half(1 - c, r), :], send_sem=send_sems.at[a],
                recv_sem=recv_sems.at[a], device_id=(x, y, 1 - c), device_id_type=MESH).wait_recv()
        for cp in local:
            cp.wait()

    any_spec = pl.BlockSpec(memory_space=pl.ANY)
    return pl.pallas_call(
        body, name="grad_join_halves", in_specs=[any_spec] * n, out_specs=[any_spec] * n,
        out_shape=[jax.ShapeDtypeStruct((f.shape[0], 2 * f.shape[1], f.shape[2]), f.dtype) for f in fs],
        scratch_shapes=[pltpu.SemaphoreType.DMA((n,))] * 3,
    )(*fs)


def _allreduce_rows(stats):
    def body(s_ref, o_ref, buf, send_sems, recv_sems):
        x, y, c, _ = _place()
        me = 4 * x + 2 * y + c
        buf[me] = s_ref[...]
        cps = []
        for k in range(1, 8):
            px = jnp.where(k & 4, 1 - x, x)
            py = jnp.where(k & 2, 1 - y, y)
            pc = jnp.where(k & 1, 1 - c, c)
            cps.append(pltpu.make_async_remote_copy(
                src_ref=s_ref, dst_ref=buf.at[me], send_sem=send_sems.at[k - 1], recv_sem=recv_sems.at[k - 1],
                device_id=(px, py, pc), device_id_type=MESH))
        for cp in cps:
            cp.start()
        for cp in cps:
            cp.wait()
        acc = buf[0]
        for d in range(1, 8):
            acc = acc + buf[d]
        o_ref[...] = acc

    vm = pl.BlockSpec(memory_space=pltpu.VMEM)
    return pl.pallas_call(
        body, name="allreduce_rows", in_specs=[vm], out_specs=vm,
        out_shape=jax.ShapeDtypeStruct(stats.shape, F32),
        scratch_shapes=[pltpu.VMEM((8,) + stats.shape, F32), pltpu.SemaphoreType.DMA((7,)),
                        pltpu.SemaphoreType.DMA((7,))],
    )(stats)


def _adamw_math(w, g, m, v):
    m = ADAM_B1 * m + (1.0 - ADAM_B1) * g
    v = ADAM_B2 * v + (1.0 - ADAM_B2) * (g * g)
    m_hat = m / (1.0 - ADAM_B1 ** ADAM_STEP)
    v_hat = v / (1.0 - ADAM_B2 ** ADAM_STEP)
    delta = -ADAM_LR * (m_hat / (jnp.sqrt(v_hat) + ADAM_EPS) + ADAM_WD * w)
    return delta, m, v


def _adamw(w, m, v, gfull, first):
    L, r, cw = w.shape
    tr = r // 4 if r > 256 else r

    def body(w_ref, m_ref, v_ref, g_ref, go_ref, d_ref, mo_ref, vo_ref):
        g = g_ref[...]
        delta, mn, vn = _adamw_math(w_ref[...], g, m_ref[...], v_ref[...])
        go_ref[...] = g
        d_ref[...] = delta
        mo_ref[...] = mn
        vo_ref[...] = vn

    blk = pl.BlockSpec((None, tr, cw), lambda l, i: (l, i, 0))
    sh = jax.ShapeDtypeStruct(w.shape, F32)
    return pl.pallas_call(
        body, name="adamw", grid=(L, r // tr),
        in_specs=[blk, blk, blk, pl.BlockSpec((None, tr, cw), lambda l, i: (first + l, i, 0))],
        out_specs=[blk] * 4, out_shape=[sh] * 4, compiler_params=_params(),
    )(w, m, v, gfull)


def _adamw_rows(w, m, v, g):
    def body(w_ref, m_ref, v_ref, g_ref, d_ref, mo_ref, vo_ref):
        delta, mn, vn = _adamw_math(w_ref[...], g_ref[...], m_ref[...], v_ref[...])
        d_ref[...] = delta
        mo_ref[...] = mn
        vo_ref[...] = vn

    vm = pl.BlockSpec(memory_space=pltpu.VMEM)
    sh = jax.ShapeDtypeStruct(w.shape, F32)
    return pl.pallas_call(body, name="adamw_rows", in_specs=[vm] * 4, out_specs=[vm] * 3, out_shape=[sh] * 3)(w, m, v, g)


GROUPS = (("ffn1_w_gate", "ffn1_w_up", "ffn2_w_gate", "ffn2_w_up"), ("ffn1_w_down", "ffn2_w_down"),
          ("w_in",), ("w_proj_dil", "w_proj_sb"), ("w_out",))


def _pick_row(blocks):
    row = lax.broadcasted_iota(jnp.int32, (8, D_MODEL), 0)
    out = jnp.zeros((8, D_MODEL), F32)
    for i, b in enumerate(blocks):
        out = out + jnp.where(row == i, b, 0.0)
    return out


def kernel(x, norm_ffn1, ffn1_w_gate, ffn1_w_up, ffn1_w_down, norm_mix, w_in, w_proj_dil, w_proj_sb, w_out, norm_ffn2, ffn2_w_gate, ffn2_w_up, ffn2_w_down, norm_final, loss_target, m_norm_ffn1, m_ffn1_w_gate, m_ffn1_w_up, m_ffn1_w_down, m_norm_mix, m_w_in, m_w_proj_dil, m_w_proj_sb, m_w_out, m_norm_ffn2, m_ffn2_w_gate, m_ffn2_w_up, m_ffn2_w_down, m_norm_final, v_norm_ffn1, v_ffn1_w_gate, v_ffn1_w_up, v_ffn1_w_down, v_norm_mix, v_w_in, v_w_proj_dil, v_w_proj_sb, v_w_out, v_norm_ffn2, v_ffn2_w_gate, v_ffn2_w_up, v_ffn2_w_down, v_norm_final):
    given = dict(locals())
    weights = {n: given[n] for n in WEIGHT_NAMES}
    norms = {n: given[n] for n in NORM_NAMES}

    gathered = _gather_weights([weights[n].astype(BF16) for n in WEIGHT_NAMES])
    W = dict(zip(WEIGHT_NAMES, gathered))
    loss_blk, grad_x, grads, gains, dg_final = _local_step(x[0], loss_target[0], norms, norm_final, W)

    c_arr = lax.axis_index("c").astype(jnp.int32).reshape(1)
    stacked = [jnp.stack([grads[l][n] for n in grp for l in range(DEPTH)]) for grp in GROUPS]
    from_sibling = _exchange_halves(stacked)
    chip_sums = [_add_half(g, got, c_arr) for g, got in zip(stacked, from_sibling)]
    from_chips = _scatter_to_chips(chip_sums)
    reduced = _join_halves([_sum_chips(got) for got in from_chips])

    out = {"grad_x": grad_x[None]}
    for grp, gfull in zip(GROUPS, reduced):
        for i, n in enumerate(grp):
            g, d, mn, vn = _adamw(weights[n], given["m_" + n], given["v_" + n], gfull, i * DEPTH)
            out["grad_" + n], out["delta_" + n], out["new_m_" + n], out["new_v_" + n] = g, d, mn, vn

    rows = [gains[l][n] for n in NORM_NAMES for l in range(DEPTH)] + [dg_final, loss_blk]
    total = _allreduce_rows(_pick_row(rows))
    out["loss"] = total[7, 0]
    wn = jnp.concatenate([given[n] for n in NORM_NAMES] + [norm_final[None], jnp.zeros((1, D_MODEL), F32)])
    mn_ = jnp.concatenate([given["m_" + n] for n in NORM_NAMES] + [m_norm_final[None], jnp.zeros((1, D_MODEL), F32)])
    vn_ = jnp.concatenate([given["v_" + n] for n in NORM_NAMES] + [v_norm_final[None], jnp.ones((1, D_MODEL), F32)])
    d_n, m_n, v_n = _adamw_rows(wn, mn_, vn_, total)
    for i, n in enumerate(NORM_NAMES):
        sl = slice(i * DEPTH, (i + 1) * DEPTH)
        out["grad_" + n], out["delta_" + n], out["new_m_" + n], out["new_v_" + n] = total[sl], d_n[sl], m_n[sl], v_n[sl]
    out["grad_norm_final"], out["delta_norm_final"] = total[6], d_n[6]
    out["new_m_norm_final"], out["new_v_norm_final"] = m_n[6], v_n[6]

    order = WEIGHT_NAMES
    del order
    names = ["norm_ffn1", "ffn1_w_gate", "ffn1_w_up", "ffn1_w_down", "norm_mix", "w_in", "w_proj_dil", "w_proj_sb",
             "w_out", "norm_ffn2", "ffn2_w_gate", "ffn2_w_up", "ffn2_w_down", "norm_final"]
    return (out["loss"], out["grad_x"], *[out["grad_" + n] for n in names], *[out["delta_" + n] for n in names],
            *[out["new_m_" + n] for n in names], *[out["new_v_" + n] for n in names])
```

```python
import functools

import jax
import jax.numpy as jnp
from jax import lax
from jax.experimental import pallas as pl
from jax.experimental.pallas import tpu as pltpu

F32 = jnp.float32
BF16 = jnp.bfloat16

D_MODEL = 1024
DEPTH = 2
N_CHIPS = 4
HEAD_DIM = 64
ROPE_DIM = 16
ROPE_THETA = 500000.0
DIL_GROUPS = ((128, 1), (512, 4), (2048, 16))
SPAN = 128
Q_BLOCK = 128
RMS_EPS = 1e-6
D_ATT = 256
COL_QS = 2304
COL_GD = 3072
COL_GS = 4096
ADAM_LR, ADAM_B1, ADAM_B2, ADAM_EPS, ADAM_WD, ADAM_STEP = 0.001, 0.9, 0.999, 1e-08, 0.01, 10

VMEM_LIMIT = 52 * 1024 * 1024
TM = 512
NEG = -1e30

NN = (((1,), (0,)), ((), ()))
NT = (((1,), (1,)), ((), ()))
TN = (((0,), (0,)), ((), ()))
MESH = pl.DeviceIdType.MESH

WEIGHT_NAMES = ("ffn1_w_gate", "ffn1_w_up", "ffn1_w_down", "w_in", "w_proj_dil",
                "w_proj_sb", "w_out", "ffn2_w_gate", "ffn2_w_up", "ffn2_w_down")
NORM_NAMES = ("norm_ffn1", "norm_mix", "norm_ffn2")


def _params(**kw):
    return pltpu.CompilerParams(vmem_limit_bytes=VMEM_LIMIT, **kw)


def _sigmoid(x):
    return 1.0 / (1.0 + jnp.exp(-x))


def _mm_body(pairs, n_in, n_out, n_acc, dims, nk, epilogue, *refs):
    ins = refs[:n_in]
    outs = refs[n_in:n_in + n_out]
    accs = refs[n_in + n_out:]
    i = pl.program_id(0)
    k = pl.program_id(2)
    parts = [None] * n_acc
    for ia, ib, ic in pairs:
        d = lax.dot_general(ins[ia][...].astype(BF16), ins[ib][...].astype(BF16), dims,
                            preferred_element_type=F32)
        parts[ic] = d if parts[ic] is None else parts[ic] + d
    if nk == 1:
        epilogue(parts, ins, outs, i)
        return

    @pl.when(k == 0)
    def _():
        for c in range(n_acc):
            accs[c][...] = parts[c]

    @pl.when(k > 0)
    def _():
        for c in range(n_acc):
            accs[c][...] += parts[c]

    @pl.when(k == nk - 1)
    def _():
        epilogue([a[...] for a in accs], ins, outs, i)


def _mm(name, ins, in_specs, pairs, n_acc, acc_shape, dims, grid, epilogue, out_shapes, out_specs):
    nk = grid[2]
    scratch = [pltpu.VMEM(acc_shape, F32) for _ in range(n_acc)] if nk > 1 else []
    body = functools.partial(_mm_body, tuple(pairs), len(ins), len(out_shapes), n_acc, dims, nk, epilogue)
    return pl.pallas_call(
        body, name=name, grid=grid, in_specs=in_specs, out_specs=out_specs, out_shape=out_shapes,
        scratch_shapes=scratch,
        compiler_params=_params(dimension_semantics=("arbitrary", "arbitrary", "arbitrary")),
    )(*ins)


def _wspec(r, c, l, by):
    if by == 1:
        return pl.BlockSpec((None, None, r, c), lambda i, j, k: (j, l, 0, 0))
    return pl.BlockSpec((None, None, r, c), lambda i, j, k: (k, l, 0, 0))


def _rms_bwd_epilogue(x_idx, g_idx, dxo_idx):
    def ep(vals, ins, outs, i):
        dh = vals[0]
        x = ins[x_idx][...]
        g = ins[g_idx][...]
        rstd = lax.rsqrt(jnp.mean(x * x, axis=-1, keepdims=True) + RMS_EPS)
        xhat = x * rstd
        dxhat = dh * g
        dx = rstd * (dxhat - xhat * jnp.mean(dxhat * xhat, axis=-1, keepdims=True))
        outs[0][...] = ins[dxo_idx][...] + dx
        dg = jnp.broadcast_to(jnp.sum(dh * xhat, axis=0, keepdims=True), outs[1].shape)

        @pl.when(i == 0)
        def _():
            outs[1][...] = dg

        @pl.when(i > 0)
        def _():
            outs[1][...] += dg
    return ep


def _rms_fwd(x, gain):
    T = x.shape[0]

    def body(x_ref, g_ref, h_ref):
        xv = x_ref[...]
        h = xv * lax.rsqrt(jnp.mean(xv * xv, axis=-1, keepdims=True) + RMS_EPS)
        h_ref[...] = (h * g_ref[...]).astype(BF16)

    return pl.pallas_call(
        body, name="rms_fwd", grid=(T // TM,),
        in_specs=[pl.BlockSpec((TM, D_MODEL), lambda i: (i, 0)), pl.BlockSpec((1, D_MODEL), lambda i: (0, 0))],
        out_specs=pl.BlockSpec((TM, D_MODEL), lambda i: (i, 0)),
        out_shape=jax.ShapeDtypeStruct((T, D_MODEL), BF16), compiler_params=_params(),
    )(x, gain)


def _rope_tables(T):
    pos = jnp.arange(T, dtype=F32)
    inv_freq = ROPE_THETA ** (-jnp.arange(0, ROPE_DIM, 2, dtype=F32) / ROPE_DIM)
    ang = pos[:, None] * inv_freq[None, :]
    cos, sin = jnp.cos(ang), jnp.sin(ang)
    half = ROPE_DIM // 2
    one = jnp.ones((T, HEAD_DIM - ROPE_DIM), F32)
    zero = jnp.zeros((T, HEAD_DIM - ROPE_DIM), F32)
    zh = jnp.zeros((T, half), F32)
    c = jnp.concatenate([cos, cos, one], axis=1)
    s1 = jnp.concatenate([-sin, zh, zero], axis=1)
    s2 = jnp.concatenate([zh, sin, zero], axis=1)
    return tuple(jnp.concatenate([t, t], axis=1) for t in (c, s1, s2))


def _rope(proj, tabs, inverse, out_dtype):
    T = proj.shape[0]
    half = ROPE_DIM // 2

    def body(x_ref, c_ref, s1_ref, s2_ref, o_ref):
        xv = x_ref[...]
        c, s1, s2 = c_ref[...], s1_ref[...], s2_ref[...]
        if inverse:
            y = xv * c + pltpu.roll(xv * s1, half, 1) + pltpu.roll(xv * s2, 128 - half, 1)
        else:
            y = xv * c + pltpu.roll(xv, 128 - half, 1) * s1 + pltpu.roll(xv, half, 1) * s2
        o_ref[...] = y.astype(out_dtype)

    tab = pl.BlockSpec((TM, 128), lambda i, j: (i, 0))
    return pl.pallas_call(
        body, name="rope_inv" if inverse else "rope", grid=(T // TM, 6 * D_ATT // 128),
        in_specs=[pl.BlockSpec((TM, 128), lambda i, j: (i, j)), tab, tab, tab],
        out_specs=pl.BlockSpec((TM, 128), lambda i, j: (i, j)),
        out_shape=jax.ShapeDtypeStruct((T, 6 * D_ATT), out_dtype), compiler_params=_params(),
    )(proj, *tabs)


def _dil_merge(os_, lses):
    T = os_[0].shape[0]

    def body(o0, o1, o2, l0, l1, l2, o_ref, lse_ref):
        a, b, c = l0[...], l1[...], l2[...]
        m = jnp.maximum(jnp.maximum(a, b), c)
        ea, eb, ec = jnp.exp(a - m), jnp.exp(b - m), jnp.exp(c - m)
        den = ea + eb + ec
        o_ref[...] = (ea * o0[...] + eb * o1[...] + ec * o2[...]) / den
        lse_ref[...] = m + jnp.log(den)

    blk = pl.BlockSpec((TM, D_ATT), lambda i: (i, 0))
    sh = jax.ShapeDtypeStruct((T, D_ATT), F32)
    return pl.pallas_call(
        body, name="dil_merge", grid=(T // TM,), in_specs=[blk] * 6, out_specs=[blk, blk],
        out_shape=[sh, sh], compiler_params=_params(),
    )(*os_, *lses)


def _final_loss(x, gain, target):
    T = x.shape[0]

    def body(x_ref, g_ref, t_ref, dx_ref, dg_ref, loss_ref):
        xv = x_ref[...]
        g = g_ref[...]
        rstd = lax.rsqrt(jnp.mean(xv * xv, axis=-1, keepdims=True) + RMS_EPS)
        xhat = xv * rstd
        err = xhat * g - t_ref[...]
        loss = 0.5 * jnp.sum(jnp.mean(err * err, axis=-1, keepdims=True), axis=0, keepdims=True)
        dy = err * (1.0 / D_MODEL)
        dxhat = dy * g
        dx_ref[...] = rstd * (dxhat - xhat * jnp.mean(dxhat * xhat, axis=-1, keepdims=True))
        dg = jnp.broadcast_to(jnp.sum(dy * xhat, axis=0, keepdims=True), dg_ref.shape)
        ls = jnp.broadcast_to(loss, loss_ref.shape)

        @pl.when(pl.program_id(0) == 0)
        def _():
            dg_ref[...] = dg
            loss_ref[...] = ls

        @pl.when(pl.program_id(0) > 0)
        def _():
            dg_ref[...] += dg
            loss_ref[...] += ls

    blk = pl.BlockSpec((TM, D_MODEL), lambda i: (i, 0))
    row = pl.BlockSpec((1, D_MODEL), lambda i: (0, 0))
    acc = pl.BlockSpec((8, D_MODEL), lambda i: (0, 0))
    return pl.pallas_call(
        body, name="final_loss", grid=(T // TM,), in_specs=[blk, row, blk], out_specs=[blk, acc, acc],
        out_shape=[jax.ShapeDtypeStruct((T, D_MODEL), F32), jax.ShapeDtypeStruct((8, D_MODEL), F32),
                   jax.ShapeDtypeStruct((8, D_MODEL), F32)],
        compiler_params=_params(dimension_semantics=("arbitrary",)),
    )(x, gain, target)


def _head_masks():
    lane = lax.broadcasted_iota(jnp.int32, (SPAN, 128), 1)
    return lane < HEAD_DIM


def _dil_rows(idx, d):
    u = idx // d
    r = idx - u * d
    own = pl.ds(u * (SPAN * d) + r, SPAN, stride=d) if d > 1 else pl.ds(pl.multiple_of(u * SPAN, SPAN), SPAN)
    up = jnp.maximum(u - 1, 0)
    prev = pl.ds(up * (SPAN * d) + r, SPAN, stride=d) if d > 1 else pl.ds(pl.multiple_of(up * SPAN, SPAN), SPAN)
    return u, own, prev


def _dil_scores(qm, k_own, k_prev, u):
    qi = lax.broadcasted_iota(jnp.int32, (SPAN, SPAN), 0)
    kj = lax.broadcasted_iota(jnp.int32, (SPAN, SPAN), 1)
    s_own = lax.dot_general(qm, k_own, NT, preferred_element_type=F32) * (HEAD_DIM ** -0.5)
    s_prev = lax.dot_general(qm, k_prev, NT, preferred_element_type=F32) * (HEAD_DIM ** -0.5)
    ok_own = kj <= qi
    ok_prev = kj >= qi + jnp.where(u > 0, 0, SPAN)
    return s_own, s_prev, ok_own, ok_prev


def _dil_fwd(qk, proj, g, d):
    T = proj.shape[0]
    n_iter = T // SPAN

    def body(q_ref, k_ref, v_ref, o_ref, lse_ref):
        first = _head_masks()

        def step(idx, carry):
            u, own, prev = _dil_rows(idx, d)
            q = q_ref[own, :]
            k_own = k_ref[own, :].astype(BF16)
            k_prev = k_ref[prev, :].astype(BF16)
            v_own = v_ref[own, :].astype(BF16)
            v_prev = v_ref[prev, :].astype(BF16)
            o_h, lse_h = [], []
            for h in range(2):
                qm = jnp.where(first if h == 0 else ~first, q, 0.0).astype(BF16)
                s_own, s_prev, ok_own, ok_prev = _dil_scores(qm, k_own, k_prev, u)
                s_own = jnp.where(ok_own, s_own, NEG)
                s_prev = jnp.where(ok_prev, s_prev, NEG)
                m = jnp.maximum(jnp.max(s_own, axis=1, keepdims=True), jnp.max(s_prev, axis=1, keepdims=True))
                p_own = jnp.exp(s_own - m)
                p_prev = jnp.exp(s_prev - m)
                den = jnp.sum(p_own, axis=1, keepdims=True) + jnp.sum(p_prev, axis=1, keepdims=True)
                pv = (lax.dot_general(p_own.astype(BF16), v_own, NN, preferred_element_type=F32)
                      + lax.dot_general(p_prev.astype(BF16), v_prev, NN, preferred_element_type=F32))
                o_h.append(pv / den)
                lse_h.append(jnp.broadcast_to(m + jnp.log(den), (SPAN, 128)))
            o_ref[own, :] = jnp.where(first, o_h[0], o_h[1])
            lse_ref[own, :] = jnp.where(first, lse_h[0], lse_h[1])
            return carry

        lax.fori_loop(0, n_iter, step, 0)

    def col(b):
        return pl.BlockSpec((T, 128), lambda p: (0, b + p))

    sh = jax.ShapeDtypeStruct((T, D_ATT), F32)
    out = pl.BlockSpec((T, 128), lambda p: (0, p))
    return pl.pallas_call(
        body, name=f"dil_fwd_d{d}", grid=(2,),
        in_specs=[col(2 * g), col(6 + 2 * g), col(12 + 2 * g)], out_specs=[out, out], out_shape=[sh, sh],
        compiler_params=_params(dimension_semantics=("arbitrary",)),
    )(qk, qk, proj)


def _dil_bwd(qk, proj, do, o_dil, lse, g, d):
    T = proj.shape[0]
    n_iter = T // SPAN

    def body(q_ref, k_ref, v_ref, do_ref, o_ref, lse_ref, dq_ref, dk_ref, dv_ref):
        first = _head_masks()

        def step(idx, carry):
            u, own, prev = _dil_rows(idx, d)
            q = q_ref[own, :]
            k_own = k_ref[own, :].astype(BF16)
            k_prev = k_ref[prev, :].astype(BF16)
            v_own = v_ref[own, :].astype(BF16)
            v_prev = v_ref[prev, :].astype(BF16)
            do_v = do_ref[own, :]
            oo = o_ref[own, :]
            ls = lse_ref[own, :]
            dq_h = []
            dk_own = dk_prev = dv_own = dv_prev = None
            for h in range(2):
                hm = first if h == 0 else ~first
                qm = jnp.where(hm, q, 0.0).astype(BF16)
                dom = jnp.where(hm, do_v, 0.0)
                dob = dom.astype(BF16)
                delta = jnp.sum(dom * oo, axis=1, keepdims=True)
                lrow = jnp.max(jnp.where(hm, ls, NEG), axis=1, keepdims=True)
                s_own, s_prev, ok_own, ok_prev = _dil_scores(qm, k_own, k_prev, u)
                p_own = jnp.where(ok_own, jnp.exp(s_own - lrow), 0.0)
                p_prev = jnp.where(ok_prev, jnp.exp(s_prev - lrow), 0.0)
                dp_own = lax.dot_general(dob, v_own, NT, preferred_element_type=F32)
                dp_prev = lax.dot_general(dob, v_prev, NT, preferred_element_type=F32)
                ds_own = (p_own * (dp_own - delta) * (HEAD_DIM ** -0.5)).astype(BF16)
                ds_prev = (p_prev * (dp_prev - delta) * (HEAD_DIM ** -0.5)).astype(BF16)
                dq_h.append(lax.dot_general(ds_own, k_own, NN, preferred_element_type=F32)
                            + lax.dot_general(ds_prev, k_prev, NN, preferred_element_type=F32))
                a = lax.dot_general(ds_own, qm, TN, preferred_element_type=F32)
                b = lax.dot_general(ds_prev, qm, TN, preferred_element_type=F32)
                c = lax.dot_general(p_own.astype(BF16), dob, TN, preferred_element_type=F32)
                e = lax.dot_general(p_prev.astype(BF16), dob, TN, preferred_element_type=F32)
                dk_own = a if dk_own is None else dk_own + a
                dk_prev = b if dk_prev is None else dk_prev + b
                dv_own = c if dv_own is None else dv_own + c
                dv_prev = e if dv_prev is None else dv_prev + e
            dq_ref[own, :] = jnp.where(first, dq_h[0], dq_h[1])
            dk_ref[own, :] = dk_own
            dv_ref[own, :] = dv_own
            dk_ref[prev, :] = dk_ref[prev, :] + dk_prev
            dv_ref[prev, :] = dv_ref[prev, :] + dv_prev
            return carry

        lax.fori_loop(0, n_iter, step, 0)

    def col(b):
        return pl.BlockSpec((T, 128), lambda p: (0, b + p))

    sh = jax.ShapeDtypeStruct((T, D_ATT), F32)
    return pl.pallas_call(
        body, name=f"dil_bwd_d{d}", grid=(2,),
        in_specs=[col(2 * g), col(6 + 2 * g), col(12 + 2 * g), col(0), col(0), col(0)],
        out_specs=[col(0), col(0), col(0)], out_shape=[sh, sh, sh],
        compiler_params=_params(dimension_semantics=("arbitrary",)),
    )(qk, qk, proj, do, o_dil, lse)


SB_KT = 512


def _sb_tri(strict):
    a = lax.broadcasted_iota(jnp.int32, (Q_BLOCK, Q_BLOCK), 0)
    b = lax.broadcasted_iota(jnp.int32, (Q_BLOCK, Q_BLOCK), 1)
    return jnp.where((a > b) if strict else (a >= b), 1.0, 0.0).astype(BF16)


def _suffix(x, c, tri):
    nb = x.shape[1] // Q_BLOCK
    blocks = [x[:, Q_BLOCK * b:Q_BLOCK * (b + 1)] for b in range(nb)]
    hi = [b.astype(BF16) for b in blocks]
    lo = [(b - h.astype(F32)).astype(BF16) for b, h in zip(blocks, hi)]
    y = lax.dot_general(jnp.concatenate(hi + lo, axis=0), tri, NN, preferred_element_type=F32)
    outs = [None] * nb
    run = c
    for b in reversed(range(nb)):
        outs[b] = run + y[Q_BLOCK * b:Q_BLOCK * (b + 1)] + y[Q_BLOCK * (nb + b):Q_BLOCK * (nb + b + 1)]
        run = run + jnp.sum(blocks[b], axis=1, keepdims=True)
    return jnp.concatenate(outs, axis=1), run


def _sb_tile(qm, kb, past, c, tri):
    z = lax.dot_general(qm, kb, NT, preferred_element_type=F32) * (HEAD_DIM ** -0.5)
    lsz = jnp.minimum(z, 0.0) - jnp.log1p(jnp.exp(-jnp.abs(z)))
    lk = jnp.where(past, lsz - z, 0.0)
    after, c_new = _suffix(lk, c, tri)
    w = jnp.where(past, jnp.exp(lsz + after), 0.0)
    return z, lsz, w, c_new


def _sb_past(i, t):
    row = lax.broadcasted_iota(jnp.int32, (Q_BLOCK, SB_KT), 0)
    col = lax.broadcasted_iota(jnp.int32, (Q_BLOCK, SB_KT), 1)
    return col + t * SB_KT < row + i * Q_BLOCK


def _sb_fwd(proj):
    T = proj.shape[0]

    def body(q_ref, k_ref, v_ref, o_ref):
        i = pl.program_id(1)
        first = _head_masks()
        tri = _sb_tri(True)
        q = q_ref[...]
        qms = [jnp.where(first, q, 0.0).astype(BF16), jnp.where(first, 0.0, q).astype(BF16)]
        n_tiles = (i * Q_BLOCK) // SB_KT + 1

        def step(tt, carry):
            t = n_tiles - 1 - tt
            rows = pl.ds(pl.multiple_of(t * SB_KT, SB_KT), SB_KT)
            kb = k_ref[rows, :].astype(BF16)
            vb = v_ref[rows, :].astype(BF16)
            past = _sb_past(i, t)
            out = []
            for h in range(2):
                acc, c = carry[2 * h], carry[2 * h + 1]
                _, _, w, c = _sb_tile(qms[h], kb, past, c, tri)
                out += [acc + lax.dot_general(w.astype(BF16), vb, NN, preferred_element_type=F32), c]
            return tuple(out)

        zero, zcol = jnp.zeros((Q_BLOCK, 128), F32), jnp.zeros((Q_BLOCK, 1), F32)
        res = lax.fori_loop(0, n_tiles, step, (zero, zcol, zero, zcol))
        o_ref[...] = jnp.where(first, res[0], res[2])

    cb = COL_QS // 128
    return pl.pallas_call(
        body, name="sb_fwd", grid=(2, T // Q_BLOCK),
        in_specs=[pl.BlockSpec((Q_BLOCK, 128), lambda p, i: (i, cb + p)),
                  pl.BlockSpec((T, 128), lambda p, i: (0, cb + 2 + p)),
                  pl.BlockSpec((T, 128), lambda p, i: (0, cb + 4 + p))],
        out_specs=pl.BlockSpec((Q_BLOCK, 128), lambda p, i: (i, p)),
        out_shape=jax.ShapeDtypeStruct((T, D_ATT), F32),
        compiler_params=_params(dimension_semantics=("arbitrary", "arbitrary")),
    )(proj, proj, proj)


def _sb_bwd(proj, do, o):
    T = proj.shape[0]

    def body(q_ref, k_ref, v_ref, do_ref, o_ref, dq_ref, dk_ref, dv_ref):
        i = pl.program_id(1)
        first = _head_masks()
        tri = _sb_tri(True)
        tri_incl = _sb_tri(False)

        @pl.when(i == 0)
        def _():
            dk_ref[...] = jnp.zeros_like(dk_ref)
            dv_ref[...] = jnp.zeros_like(dv_ref)

        q = q_ref[...]
        do_v = do_ref[...]
        oo = o_ref[...]
        qms = [jnp.where(first, q, 0.0).astype(BF16), jnp.where(first, 0.0, q).astype(BF16)]
        dobs = [jnp.where(first, do_v, 0.0).astype(BF16), jnp.where(first, 0.0, do_v).astype(BF16)]
        deltas = [jnp.sum(d.astype(F32) * oo, axis=1, keepdims=True) for d in dobs]
        n_tiles = (i * Q_BLOCK) // SB_KT + 1

        def step(tt, carry):
            t = n_tiles - 1 - tt
            rows = pl.ds(pl.multiple_of(t * SB_KT, SB_KT), SB_KT)
            kb = k_ref[rows, :].astype(BF16)
            vb = v_ref[rows, :].astype(BF16)
            past = _sb_past(i, t)
            out = []
            dk_t = dv_t = None
            for h in range(2):
                dq, c, ce = carry[3 * h:3 * h + 3]
                z, lsz, w, c = _sb_tile(qms[h], kb, past, c, tri)
                gv = lax.dot_general(dobs[h], vb, NT, preferred_element_type=F32)
                wb = w.astype(BF16)
                e = wb.astype(F32) * gv
                suf, ce = _suffix(e, ce, tri_incl)
                big_e = deltas[h] - suf
                dz = jnp.where(past, e * jnp.exp(lsz - z) - big_e * jnp.exp(lsz), 0.0) * (HEAD_DIM ** -0.5)
                dzb = dz.astype(BF16)
                dq = dq + lax.dot_general(dzb, kb, NN, preferred_element_type=F32)
                a = lax.dot_general(dzb, qms[h], TN, preferred_element_type=F32)
                b = lax.dot_general(wb, dobs[h], TN, preferred_element_type=F32)
                dk_t = a if dk_t is None else dk_t + a
                dv_t = b if dv_t is None else dv_t + b
                out += [dq, c, ce]
            dk_ref[rows, :] = dk_ref[rows, :] + dk_t
            dv_ref[rows, :] = dv_ref[rows, :] + dv_t
            return tuple(out)

        zero, zcol = jnp.zeros((Q_BLOCK, 128), F32), jnp.zeros((Q_BLOCK, 1), F32)
        res = lax.fori_loop(0, n_tiles, step, (zero, zcol, zcol, zero, zcol, zcol))
        dq_ref[...] = jnp.where(first, res[0], res[3])

    cb = COL_QS // 128
    blk = pl.BlockSpec((Q_BLOCK, 128), lambda p, i: (i, p))
    full = pl.BlockSpec((T, 128), lambda p, i: (0, p))
    sh = jax.ShapeDtypeStruct((T, D_ATT), F32)
    return pl.pallas_call(
        body, name="sb_bwd", grid=(2, T // Q_BLOCK),
        in_specs=[pl.BlockSpec((Q_BLOCK, 128), lambda p, i: (i, cb + p)),
                  pl.BlockSpec((T, 128), lambda p, i: (0, cb + 2 + p)),
                  pl.BlockSpec((T, 128), lambda p, i: (0, cb + 4 + p)), blk, blk],
        out_specs=[blk, full, full], out_shape=[sh, sh, sh],
        compiler_params=_params(dimension_semantics=("arbitrary", "arbitrary")),
    )(proj, proj, proj, do, o)


def _tok(c, by=None):
    if by is None:
        return pl.BlockSpec((TM, c), lambda i, j, k: (i, 0))
    if by == 1:
        return pl.BlockSpec((TM, c), lambda i, j, k: (i, j))
    return pl.BlockSpec((TM, c), lambda i, j, k: (i, k))


def _chunked(c, by):
    if by == 1:
        return pl.BlockSpec((None, TM, c), lambda i, j, k: (j, i, 0))
    return pl.BlockSpec((None, TM, c), lambda i, j, k: (k, i, 0))


def _gain_spec():
    return pl.BlockSpec((1, D_MODEL), lambda i, j, k: (0, 0))


def _ffn_fwd(x, gain, wg, wu, wd, l):
    T = x.shape[0]
    ffs = wg.shape[3]
    h = _rms_fwd(x, gain)

    def swiglu(vals, ins, outs, i):
        gt, up = vals
        outs[0][...] = gt.astype(BF16)
        outs[1][...] = up.astype(BF16)
        outs[2][...] = (gt * _sigmoid(gt) * up).astype(BF16)

    csh = jax.ShapeDtypeStruct((N_CHIPS, T, ffs), BF16)
    gate, up, act = _mm(
        "ffn_up", [h, wg, wu], [_tok(D_MODEL), _wspec(D_MODEL, ffs, l, 1), _wspec(D_MODEL, ffs, l, 1)],
        [(0, 1, 0), (0, 2, 1)], 2, None, NN, (T // TM, N_CHIPS, 1), swiglu,
        [csh, csh, csh], [_chunked(ffs, 1)] * 3)

    def resid(vals, ins, outs, i):
        outs[0][...] = ins[2][...] + 0.5 * vals[0]

    (y,) = _mm(
        "ffn_down", [act, wd, x], [_chunked(ffs, 2), _wspec(ffs, D_MODEL, l, 2), _tok(D_MODEL)],
        [(0, 1, 0)], 1, (TM, D_MODEL), NN, (T // TM, 1, N_CHIPS), resid,
        [jax.ShapeDtypeStruct((T, D_MODEL), F32)], [_tok(D_MODEL)])
    return y, (x, h, gate, up, act)


def _ffn_bwd(dxo, gain, wg, wu, wd, l, saved):
    x, h, gate, up, act = saved
    T = x.shape[0]
    ffs = wg.shape[3]
    tk = TM

    def dswiglu(vals, ins, outs, i):
        da = 0.5 * vals[0]
        gt = ins[2][...].astype(F32)
        u = ins[3][...].astype(F32)
        s = _sigmoid(gt)
        outs[0][...] = (da * u * (s * (1.0 + gt * (1.0 - s)))).astype(BF16)
        outs[1][...] = (da * (gt * s)).astype(BF16)

    csh = jax.ShapeDtypeStruct((N_CHIPS, T, ffs), BF16)
    dgate, dup = _mm(
        "ffn_dact", [dxo, wd, gate, up],
        [_tok(D_MODEL), _wspec(ffs, D_MODEL, l, 1), _chunked(ffs, 1), _chunked(ffs, 1)],
        [(0, 1, 0)], 1, None, NT, (T // TM, N_CHIPS, 1), dswiglu, [csh, csh], [_chunked(ffs, 1)] * 2)

    def half(vals, ins, outs, i):
        outs[0][...] = (0.5 * vals[0]).astype(BF16)

    (dwd,) = _mm(
        "ffn_dwd", [act, dxo],
        [pl.BlockSpec((None, tk, ffs), lambda i, j, k: (j, k, 0)), pl.BlockSpec((tk, D_MODEL), lambda i, j, k: (k, 0))],
        [(0, 1, 0)], 1, (ffs, D_MODEL), TN, (1, N_CHIPS, T // tk), half,
        [jax.ShapeDtypeStruct((N_CHIPS, ffs, D_MODEL), BF16)],
        [pl.BlockSpec((None, ffs, D_MODEL), lambda i, j, k: (j, 0, 0))])

    dx, dgain = _mm(
        "ffn_dx", [dgate, dup, wg, wu, x, gain, dxo],
        [_chunked(ffs, 2), _chunked(ffs, 2), _wspec(D_MODEL, ffs, l, 2), _wspec(D_MODEL, ffs, l, 2),
         _tok(D_MODEL), _gain_spec(), _tok(D_MODEL)],
        [(0, 2, 0), (1, 3, 0)], 1, (TM, D_MODEL), NT, (T // TM, 1, N_CHIPS), _rms_bwd_epilogue(4, 5, 6),
        [jax.ShapeDtypeStruct((T, D_MODEL), F32), jax.ShapeDtypeStruct((8, D_MODEL), F32)],
        [_tok(D_MODEL), pl.BlockSpec((8, D_MODEL), lambda i, j, k: (0, 0))])

    def two(vals, ins, outs, i):
        outs[0][...] = vals[0].astype(BF16)
        outs[1][...] = vals[1].astype(BF16)

    wsh = jax.ShapeDtypeStruct((N_CHIPS, D_MODEL, ffs), BF16)
    wout = pl.BlockSpec((None, D_MODEL, ffs), lambda i, j, k: (j, 0, 0))
    cin = pl.BlockSpec((None, tk, ffs), lambda i, j, k: (j, k, 0))
    dwg, dwu = _mm(
        "ffn_dwgu", [h, dgate, dup], [pl.BlockSpec((tk, D_MODEL), lambda i, j, k: (k, 0)), cin, cin],
        [(0, 1, 0), (0, 2, 1)], 2, (D_MODEL, ffs), TN, (1, N_CHIPS, T // tk), two, [wsh, wsh], [wout, wout])
    return dx, dgain, dwg, dwu, dwd


def _mixer_fwd(x, gain, W, l, tabs):
    T = x.shape[0]
    win, wpd, wps, wo = W["w_in"], W["w_proj_dil"], W["w_proj_sb"], W["w_out"]
    cin = win.shape[3]
    cp = wpd.shape[3]
    h = _rms_fwd(x, gain)

    def plain(vals, ins, outs, i):
        outs[0][...] = vals[0]

    (proj,) = _mm(
        "mix_in", [h, win], [_tok(D_MODEL), _wspec(D_MODEL, cin, l, 1)], [(0, 1, 0)], 1, None, NN,
        (T // TM, N_CHIPS, 1), plain, [jax.ShapeDtypeStruct((T, N_CHIPS * cin), F32)], [_tok(cin, 1)])

    qk = _rope(proj, tabs, False, F32)
    os_, lses = [], []
    for g, (window, dil) in enumerate(DIL_GROUPS):
        o_g, lse_g = _dil_fwd(qk, proj, g, dil)
        os_.append(o_g)
        lses.append(lse_g)
    o_dil, lse = _dil_merge(os_, lses)
    o_sb = _sb_fwd(proj)

    def gated(vals, ins, outs, i):
        pd, ps = vals
        outs[0][...] = (_sigmoid(ins[4][...]) * pd + _sigmoid(ins[5][...]) * ps).astype(BF16)
        outs[1][...] = pd.astype(BF16)
        outs[2][...] = ps.astype(BF16)

    gd0, gs0 = COL_GD // cp, COL_GS // cp
    ush = jax.ShapeDtypeStruct((T, D_MODEL), BF16)
    u, pd, ps = _mm(
        "mix_gate", [o_dil, o_sb, wpd, wps, proj, proj],
        [_tok(D_ATT), _tok(D_ATT), _wspec(D_ATT, cp, l, 1), _wspec(D_ATT, cp, l, 1),
         pl.BlockSpec((TM, cp), lambda i, j, k: (i, gd0 + j)), pl.BlockSpec((TM, cp), lambda i, j, k: (i, gs0 + j))],
        [(0, 2, 0), (1, 3, 1)], 2, None, NN, (T // TM, N_CHIPS, 1), gated, [ush] * 3, [_tok(cp, 1)] * 3)

    def resid(vals, ins, outs, i):
        outs[0][...] = ins[2][...] + vals[0]

    (y,) = _mm(
        "mix_out", [u, wo, x], [_tok(cp, 2), _wspec(cp, D_MODEL, l, 2), _tok(D_MODEL)],
        [(0, 1, 0)], 1, (TM, D_MODEL), NN, (T // TM, 1, N_CHIPS), resid,
        [jax.ShapeDtypeStruct((T, D_MODEL), F32)], [_tok(D_MODEL)])
    return y, (x, h, proj, qk, o_dil, lse, o_sb, u, pd, ps)


def _mixer_bwd(dxo, gain, W, l, tabs, saved):
    x, h, proj, qk, o_dil, lse, o_sb, u, pd, ps = saved
    T = x.shape[0]
    win, wpd, wps, wo = W["w_in"], W["w_proj_dil"], W["w_proj_sb"], W["w_out"]
    cin = win.shape[3]
    cp = wpd.shape[3]
    tk = TM
    gd0, gs0 = COL_GD // cp, COL_GS // cp

    def dgated(vals, ins, outs, i):
        du = vals[0]
        sd = _sigmoid(ins[4][...])
        ss = _sigmoid(ins[5][...])
        outs[0][...] = (du * sd).astype(BF16)
        outs[1][...] = (du * ss).astype(BF16)
        outs[2][...] = (du * ins[2][...].astype(F32) * sd * (1.0 - sd)).astype(BF16)
        outs[3][...] = (du * ins[3][...].astype(F32) * ss * (1.0 - ss)).astype(BF16)

    ush = jax.ShapeDtypeStruct((T, D_MODEL), BF16)
    dpd, dps, dgd, dgs = _mm(
        "mix_du", [dxo, wo, pd, ps, proj, proj],
        [_tok(D_MODEL), _wspec(cp, D_MODEL, l, 1), _tok(cp, 1), _tok(cp, 1),
         pl.BlockSpec((TM, cp), lambda i, j, k: (i, gd0 + j)), pl.BlockSpec((TM, cp), lambda i, j, k: (i, gs0 + j))],
        [(0, 1, 0)], 1, None, NT, (T // TM, N_CHIPS, 1), dgated, [ush] * 4, [_tok(cp, 1)] * 4)

    def one(vals, ins, outs, i):
        outs[0][...] = vals[0].astype(BF16)

    def two(vals, ins, outs, i):
        outs[0][...] = vals[0].astype(BF16)
        outs[1][...] = vals[1].astype(BF16)

    (dwo,) = _mm(
        "mix_dwo", [u, dxo],
        [pl.BlockSpec((tk, cp), lambda i, j, k: (k, j)), pl.BlockSpec((tk, D_MODEL), lambda i, j, k: (k, 0))],
        [(0, 1, 0)], 1, (cp, D_MODEL), TN, (1, N_CHIPS, T // tk), one,
        [jax.ShapeDtypeStruct((N_CHIPS, cp, D_MODEL), BF16)],
        [pl.BlockSpec((None, cp, D_MODEL), lambda i, j, k: (j, 0, 0))])

    def plain2(vals, ins, outs, i):
        outs[0][...] = vals[0]
        outs[1][...] = vals[1]

    ash = jax.ShapeDtypeStruct((T, D_ATT), F32)
    do_dil, do_sb = _mm(
        "mix_do", [dpd, dps, wpd, wps], [_tok(cp, 2), _tok(cp, 2), _wspec(D_ATT, cp, l, 2), _wspec(D_ATT, cp, l, 2)],
        [(0, 2, 0), (1, 3, 1)], 2, (TM, D_ATT), NT, (T // TM, 1, N_CHIPS), plain2, [ash, ash], [_tok(D_ATT)] * 2)

    psh = jax.ShapeDtypeStruct((N_CHIPS, D_ATT, cp), BF16)
    pspec = pl.BlockSpec((None, D_ATT, cp), lambda i, j, k: (j, 0, 0))
    arow = pl.BlockSpec((tk, D_ATT), lambda i, j, k: (k, 0))
    dcol = pl.BlockSpec((tk, cp), lambda i, j, k: (k, j))
    dwpd, dwps = _mm(
        "mix_dwp", [o_dil, o_sb, dpd, dps], [arow, arow, dcol, dcol], [(0, 2, 0), (1, 3, 1)], 2, (D_ATT, cp), TN,
        (1, N_CHIPS, T // tk), two, [psh, psh], [pspec, pspec])

    dqs, dks, dvs = [], [], []
    for g, (window, dil) in enumerate(DIL_GROUPS):
        dq, dk, dv = _dil_bwd(qk, proj, do_dil, o_dil, lse, g, dil)
        dqs.append(dq)
        dks.append(dk)
        dvs.append(dv)
    dqk = _rope(jnp.concatenate(dqs + dks, axis=1), tabs, True, BF16)
    dq_s, dk_s, dv_s = _sb_bwd(proj, do_sb, o_sb)
    dproj = jnp.concatenate(
        [dqk] + [a.astype(BF16) for a in dvs + [dq_s, dk_s, dv_s]] + [dgd, dgs], axis=1)

    dx, dgain = _mm(
        "mix_dx", [dproj, win, x, gain, dxo],
        [_tok(cin, 2), _wspec(D_MODEL, cin, l, 2), _tok(D_MODEL), _gain_spec(), _tok(D_MODEL)],
        [(0, 1, 0)], 1, (TM, D_MODEL), NT, (T // TM, 1, N_CHIPS), _rms_bwd_epilogue(2, 3, 4),
        [jax.ShapeDtypeStruct((T, D_MODEL), F32), jax.ShapeDtypeStruct((8, D_MODEL), F32)],
        [_tok(D_MODEL), pl.BlockSpec((8, D_MODEL), lambda i, j, k: (0, 0))])

    (dwin,) = _mm(
        "mix_dwin", [h, dproj],
        [pl.BlockSpec((tk, D_MODEL), lambda i, j, k: (k, 0)), pl.BlockSpec((tk, cin), lambda i, j, k: (k, j))],
        [(0, 1, 0)], 1, (D_MODEL, cin), TN, (1, N_CHIPS, T // tk), one,
        [jax.ShapeDtypeStruct((N_CHIPS, D_MODEL, cin), BF16)],
        [pl.BlockSpec((None, D_MODEL, cin), lambda i, j, k: (j, 0, 0))])
    return dx, dgain, dwin, dwpd, dwps, dwo


def _local_step(x, target, norms, norm_final, W):
    T = x.shape[0]
    tabs = _rope_tables(T)
    saved = []
    for l in range(DEPTH):
        x, s1 = _ffn_fwd(x, norms["norm_ffn1"][l:l + 1], W["ffn1_w_gate"], W["ffn1_w_up"], W["ffn1_w_down"], l)
        x, s2 = _mixer_fwd(x, norms["norm_mix"][l:l + 1], W, l, tabs)
        x, s3 = _ffn_fwd(x, norms["norm_ffn2"][l:l + 1], W["ffn2_w_gate"], W["ffn2_w_up"], W["ffn2_w_down"], l)
        saved.append((s1, s2, s3))
    dx, dg_final, loss = _final_loss(x, norm_final.reshape(1, D_MODEL), target)
    grads = [None] * DEPTH
    gains = [None] * DEPTH
    for l in reversed(range(DEPTH)):
        s1, s2, s3 = saved[l]
        dx, dg2, dwg2, dwu2, dwd2 = _ffn_bwd(dx, norms["norm_ffn2"][l:l + 1], W["ffn2_w_gate"], W["ffn2_w_up"],
                                             W["ffn2_w_down"], l, s3)
        dx, dgm, dwin, dwpd, dwps, dwo = _mixer_bwd(dx, norms["norm_mix"][l:l + 1], W, l, tabs, s2)
        dx, dg1, dwg1, dwu1, dwd1 = _ffn_bwd(dx, norms["norm_ffn1"][l:l + 1], W["ffn1_w_gate"], W["ffn1_w_up"],
                                             W["ffn1_w_down"], l, s1)
        grads[l] = dict(ffn1_w_gate=dwg1, ffn1_w_up=dwu1, ffn1_w_down=dwd1, w_in=dwin, w_proj_dil=dwpd,
                        w_proj_sb=dwps, w_out=dwo, ffn2_w_gate=dwg2, ffn2_w_up=dwu2, ffn2_w_down=dwd2)
        gains[l] = dict(norm_ffn1=dg1, norm_mix=dgm, norm_ffn2=dg2)
    return loss, dx, grads, gains, dg_final


def _place():
    x, y, c = lax.axis_index("x"), lax.axis_index("y"), lax.axis_index("c")
    chips = [(1 - x, y), (x, 1 - y), (1 - x, 1 - y)]
    return x, y, c, chips


def _cast_into_slot(w, me_arr):
    L, r, cw = w.shape
    tr = r // 4 if r > 256 else r

    def body(me_ref, w_ref, o_ref):
        o_ref[...] = w_ref[...].astype(BF16)

    return pl.pallas_call(
        body, name="cast_weights",
        grid_spec=pltpu.PrefetchScalarGridSpec(
            num_scalar_prefetch=1, grid=(L, r // tr),
            in_specs=[pl.BlockSpec((None, tr, cw), lambda l, i, me: (l, i, 0))],
            out_specs=pl.BlockSpec((None, None, tr, cw), lambda l, i, me: (me[0], l, i, 0))),
        out_shape=jax.ShapeDtypeStruct((N_CHIPS, L, r, cw), BF16), compiler_params=_params(),
    )(me_arr, w)


def _gather_weights(bufs):
    n = len(bufs)

    def body(*refs):
        out_refs = refs[n:2 * n]
        send_sems, recv_sems, fsend_sems, frecv_sems = refs[2 * n:]
        x, y, c, chips = _place()
        me = 2 * x + y
        sibling = (x, y, 1 - c)

        def ici(a, j, chip_id, to):
            return pltpu.make_async_remote_copy(
                src_ref=out_refs[a].at[me, c], dst_ref=out_refs[a].at[chip_id, c],
                send_sem=send_sems.at[a, j], recv_sem=recv_sems.at[a, j], device_id=to, device_id_type=MESH)

        def d2d(a, j, chip_id, layer):
            return pltpu.make_async_remote_copy(
                src_ref=out_refs[a].at[chip_id, layer], dst_ref=out_refs[a].at[chip_id, layer],
                send_sem=fsend_sems.at[a, j], recv_sem=frecv_sems.at[a, j], device_id=sibling, device_id_type=MESH)

        sends = [ici(a, j, me, (*chip, c)) for a in range(n) for j, chip in enumerate(chips)]
        for cp in sends:
            cp.start()
        passed = []
        for a in range(n):
            for j, chip in enumerate(chips):
                cid = 2 * chip[0] + chip[1]
                ici(a, j, cid, (*chip, c)).wait_recv()
                fw = d2d(a, j, cid, c)
                fw.start()
                passed.append(fw)
        for a in range(n):
            for j, chip in enumerate(chips):
                d2d(a, j, 2 * chip[0] + chip[1], 1 - c).wait_recv()
        for cp in sends + passed:
            cp.wait_send()

    any_spec = pl.BlockSpec(memory_space=pl.ANY)
    return pl.pallas_call(
        body, name="gather_weights", in_specs=[any_spec] * n, out_specs=[any_spec] * n,
        out_shape=[jax.ShapeDtypeStruct(b.shape, b.dtype) for b in bufs],
        input_output_aliases={a: a for a in range(n)},
        scratch_shapes=[pltpu.SemaphoreType.DMA((n, 3))] * 4,
    )(*bufs)


def _half(c, r):
    return pl.ds(pl.multiple_of(c * (r // 2), 8), r // 2)


def _exchange_halves(gs):
    n = len(gs)

    def body(*refs):
        g_refs, out_refs = refs[:n], refs[n:2 * n]
        send_sems, recv_sems = refs[2 * n:]
        x, y, c, _ = _place()
        cps = []
        for a in range(n):
            r = g_refs[a].shape[2]
            cps.append(pltpu.make_async_remote_copy(
                src_ref=g_refs[a].at[:, :, _half(1 - c, r), :], dst_ref=out_refs[a],
                send_sem=send_sems.at[a], recv_sem=recv_sems.at[a], device_id=(x, y, 1 - c), device_id_type=MESH))
        for cp in cps:
            cp.start()
        for cp in cps:
            cp.wait()

    any_spec = pl.BlockSpec(memory_space=pl.ANY)
    return pl.pallas_call(
        body, name="grad_to_sibling", in_specs=[any_spec] * n, out_specs=[any_spec] * n,
        out_shape=[jax.ShapeDtypeStruct((g.shape[0], g.shape[1], g.shape[2] // 2, g.shape[3]), g.dtype) for g in gs],
        scratch_shapes=[pltpu.SemaphoreType.DMA((n,))] * 2,
    )(*gs)


def _add_half(g, got, c_arr):
    n, _, r, cw = g.shape

    def body(c_ref, a_ref, b_ref, o_ref):
        o_ref[...] = (a_ref[...].astype(F32) + b_ref[...].astype(F32)).astype(BF16)

    return pl.pallas_call(
        body, name="grad_add_half",
        grid_spec=pltpu.PrefetchScalarGridSpec(
            num_scalar_prefetch=1, grid=(n, N_CHIPS),
            in_specs=[pl.BlockSpec((None, None, r // 2, cw), lambda a, k, cr: (a, k, cr[0], 0)),
                      pl.BlockSpec((None, None, r // 2, cw), lambda a, k, cr: (a, k, 0, 0))],
            out_specs=pl.BlockSpec((None, None, r // 2, cw), lambda a, k, cr: (a, k, 0, 0))),
        out_shape=jax.ShapeDtypeStruct((n, N_CHIPS, r // 2, cw), BF16), compiler_params=_params(),
    )(c_arr, g, got)


def _scatter_to_chips(ss):
    n = len(ss)

    def body(*refs):
        s_refs, out_refs = refs[:n], refs[n:2 * n]
        send_sems, recv_sems, local_sems = refs[2 * n:]
        x, y, c, chips = _place()
        me = 2 * x + y
        local = [pltpu.make_async_copy(s_refs[a].at[:, me], out_refs[a].at[:, me], local_sems.at[a])
                 for a in range(n)]
        for cp in local:
            cp.start()
        cps = []
        for a in range(n):
            for j, chip in enumerate(chips):
                cid = 2 * chip[0] + chip[1]
                cps.append(pltpu.make_async_remote_copy(
                    src_ref=s_refs[a].at[:, cid], dst_ref=out_refs[a].at[:, me],
                    send_sem=send_sems.at[a, j], recv_sem=recv_sems.at[a, j], device_id=(*chip, c),
                    device_id_type=MESH))
        for cp in cps:
            cp.start()
        for cp in cps:
            cp.wait()
        for cp in local:
            cp.wait()

    any_spec = pl.BlockSpec(memory_space=pl.ANY)
    return pl.pallas_call(
        body, name="grad_to_chips", in_specs=[any_spec] * n, out_specs=[any_spec] * n,
        out_shape=[jax.ShapeDtypeStruct(s.shape, s.dtype) for s in ss],
        scratch_shapes=[pltpu.SemaphoreType.DMA((n, 3))] * 2 + [pltpu.SemaphoreType.DMA((n,))],
    )(*ss)


def _sum_chips(got):
    n, _, rh, cw = got.shape

    def body(g_ref, o_ref):
        acc = g_ref[0].astype(F32)
        for k in range(1, N_CHIPS):
            acc = acc + g_ref[k].astype(F32)
        o_ref[...] = acc

    return pl.pallas_call(
        body, name="grad_sum_chips", grid=(n,),
        in_specs=[pl.BlockSpec((None, N_CHIPS, rh, cw), lambda a: (a, 0, 0, 0))],
        out_specs=pl.BlockSpec((None, rh, cw), lambda a: (a, 0, 0)),
        out_shape=jax.ShapeDtypeStruct((n, rh, cw), F32), compiler_params=_params(),
    )(got)


def _swap_halves(fs):
    n = len(fs)

    def body(*refs):
        f_refs, out_refs = refs[:n], refs[n:2 * n]
        send_sems, recv_sems = refs[2 * n:]
        x, y, c, _ = _place()
        cps = [pltpu.make_async_remote_copy(
            src_ref=f_refs[a], dst_ref=out_refs[a], send_sem=send_sems.at[a], recv_sem=recv_sems.at[a],
            device_id=(x, y, 1 - c), device_id_type=MESH) for a in range(n)]
        for cp in cps:
            cp.start()
        for cp in cps:
            cp.wait()

    any_spec = pl.BlockSpec(memory_space=pl.ANY)
    return pl.pallas_call(
        body, name="grad_swap_halves", in_specs=[any_spec] * n, out_specs=[any_spec] * n,
        out_shape=[jax.ShapeDtypeStruct(f.shape, f.dtype) for f in fs],
        scratch_shapes=[pltpu.SemaphoreType.DMA((n,))] * 2,
    )(*fs)


def _allreduce_rows(stats):
    def body(s_ref, o_ref, buf, send_sems, recv_sems):
        x, y, c, _ = _place()
        me = 4 * x + 2 * y + c
        buf[me] = s_ref[...]
        cps = []
        for k in range(1, 8):
            px = jnp.where(k & 4, 1 - x, x)
            py = jnp.where(k & 2, 1 - y, y)
            pc = jnp.where(k & 1, 1 - c, c)
            cps.append(pltpu.make_async_remote_copy(
                src_ref=s_ref, dst_ref=buf.at[me], send_sem=send_sems.at[k - 1], recv_sem=recv_sems.at[k - 1],
                device_id=(px, py, pc), device_id_type=MESH))
        for cp in cps:
            cp.start()
        for cp in cps:
            cp.wait()
        acc = buf[0]
        for d in range(1, 8):
            acc = acc + buf[d]
        o_ref[...] = acc

    vm = pl.BlockSpec(memory_space=pltpu.VMEM)
    return pl.pallas_call(
        body, name="allreduce_rows", in_specs=[vm], out_specs=vm,
        out_shape=jax.ShapeDtypeStruct(stats.shape, F32),
        scratch_shapes=[pltpu.VMEM((8,) + stats.shape, F32), pltpu.SemaphoreType.DMA((7,)),
                        pltpu.SemaphoreType.DMA((7,))],
    )(stats)


def _adamw_math(w, g, m, v):
    m = ADAM_B1 * m + (1.0 - ADAM_B1) * g
    v = ADAM_B2 * v + (1.0 - ADAM_B2) * (g * g)
    m_hat = m / (1.0 - ADAM_B1 ** ADAM_STEP)
    v_hat = v / (1.0 - ADAM_B2 ** ADAM_STEP)
    delta = -ADAM_LR * (m_hat / (jnp.sqrt(v_hat) + ADAM_EPS) + ADAM_WD * w)
    return delta, m, v


def _adamw(w, m, v, mine, theirs, first, c_arr):
    L, r, cw = w.shape
    tr = r // 4 if r > 256 else r // 2
    nblk = (r // 2) // tr

    def body(c_ref, w_ref, m_ref, v_ref, a_ref, b_ref, go_ref, d_ref, mo_ref, vo_ref):
        g = jnp.where(pl.program_id(1) == c_ref[0], a_ref[...], b_ref[...])
        delta, mn, vn = _adamw_math(w_ref[...], g, m_ref[...], v_ref[...])
        go_ref[...] = g
        d_ref[...] = delta
        mo_ref[...] = mn
        vo_ref[...] = vn

    blk = pl.BlockSpec((None, tr, cw), lambda l, hh, i, cr: (l, hh * nblk + i, 0))
    half = pl.BlockSpec((None, tr, cw), lambda l, hh, i, cr: (first + l, i, 0))
    sh = jax.ShapeDtypeStruct(w.shape, F32)
    return pl.pallas_call(
        body, name="adamw",
        grid_spec=pltpu.PrefetchScalarGridSpec(
            num_scalar_prefetch=1, grid=(L, 2, nblk), in_specs=[blk, blk, blk, half, half], out_specs=[blk] * 4),
        out_shape=[sh] * 4, compiler_params=_params(),
    )(c_arr, w, m, v, mine, theirs)


def _adamw_rows(w, m, v, g):
    def body(w_ref, m_ref, v_ref, g_ref, d_ref, mo_ref, vo_ref):
        delta, mn, vn = _adamw_math(w_ref[...], g_ref[...], m_ref[...], v_ref[...])
        d_ref[...] = delta
        mo_ref[...] = mn
        vo_ref[...] = vn

    vm = pl.BlockSpec(memory_space=pltpu.VMEM)
    sh = jax.ShapeDtypeStruct(w.shape, F32)
    return pl.pallas_call(body, name="adamw_rows", in_specs=[vm] * 4, out_specs=[vm] * 3, out_shape=[sh] * 3)(w, m, v, g)


GROUPS = (("ffn1_w_gate", "ffn1_w_up", "ffn2_w_gate", "ffn2_w_up"), ("ffn1_w_down", "ffn2_w_down"),
          ("w_in",), ("w_proj_dil", "w_proj_sb"), ("w_out",))


def _pick_row(blocks):
    row = lax.broadcasted_iota(jnp.int32, (8, D_MODEL), 0)
    out = jnp.zeros((8, D_MODEL), F32)
    for i, b in enumerate(blocks):
        out = out + jnp.where(row == i, b, 0.0)
    return out


def kernel(x, norm_ffn1, ffn1_w_gate, ffn1_w_up, ffn1_w_down, norm_mix, w_in, w_proj_dil, w_proj_sb, w_out, norm_ffn2, ffn2_w_gate, ffn2_w_up, ffn2_w_down, norm_final, loss_target, m_norm_ffn1, m_ffn1_w_gate, m_ffn1_w_up, m_ffn1_w_down, m_norm_mix, m_w_in, m_w_proj_dil, m_w_proj_sb, m_w_out, m_norm_ffn2, m_ffn2_w_gate, m_ffn2_w_up, m_ffn2_w_down, m_norm_final, v_norm_ffn1, v_ffn1_w_gate, v_ffn1_w_up, v_ffn1_w_down, v_norm_mix, v_w_in, v_w_proj_dil, v_w_proj_sb, v_w_out, v_norm_ffn2, v_ffn2_w_gate, v_ffn2_w_up, v_ffn2_w_down, v_norm_final):
    given = dict(locals())
    weights = {n: given[n] for n in WEIGHT_NAMES}
    norms = {n: given[n] for n in NORM_NAMES}

    c_arr = lax.axis_index("c").astype(jnp.int32).reshape(1)
    me_arr = (2 * lax.axis_index("x") + lax.axis_index("y")).astype(jnp.int32).reshape(1)
    gathered = _gather_weights([_cast_into_slot(weights[n], me_arr) for n in WEIGHT_NAMES])
    W = dict(zip(WEIGHT_NAMES, gathered))
    loss_blk, grad_x, grads, gains, dg_final = _local_step(x[0], loss_target[0], norms, norm_final, W)

    stacked = [jnp.stack([grads[l][n] for n in grp for l in range(DEPTH)]) for grp in GROUPS]
    from_sibling = _exchange_halves(stacked)
    chip_sums = [_add_half(g, got, c_arr) for g, got in zip(stacked, from_sibling)]
    from_chips = _scatter_to_chips(chip_sums)
    mine = [_sum_chips(got) for got in from_chips]
    theirs = _swap_halves(mine)

    out = {"grad_x": grad_x[None]}
    for grp, ga, gb in zip(GROUPS, mine, theirs):
        for i, n in enumerate(grp):
            g, d, mn, vn = _adamw(weights[n], given["m_" + n], given["v_" + n], ga, gb, i * DEPTH, c_arr)
            out["grad_" + n], out["delta_" + n], out["new_m_" + n], out["new_v_" + n] = g, d, mn, vn

    rows = [gains[l][n] for n in NORM_NAMES for l in range(DEPTH)] + [dg_final, loss_blk]
    total = _allreduce_rows(_pick_row(rows))
    out["loss"] = total[7, 0]
    wn = jnp.concatenate([given[n] for n in NORM_NAMES] + [norm_final[None], jnp.zeros((1, D_MODEL), F32)])
    mn_ = jnp.concatenate([given["m_" + n] for n in NORM_NAMES] + [m_norm_final[None], jnp.zeros((1, D_MODEL), F32)])
    vn_ = jnp.concatenate([given["v_" + n] for n in NORM_NAMES] + [v_norm_final[None], jnp.ones((1, D_MODEL), F32)])
    d_n, m_n, v_n = _adamw_rows(wn, mn_, vn_, total)
    for i, n in enumerate(NORM_NAMES):
        sl = slice(i * DEPTH, (i + 1) * DEPTH)
        out["grad_" + n], out["delta_" + n], out["new_m_" + n], out["new_v_" + n] = total[sl], d_n[sl], m_n[sl], v_n[sl]
    out["grad_norm_final"], out["delta_norm_final"] = total[6], d_n[6]
    out["new_m_norm_final"], out["new_v_norm_final"] = m_n[6], v_n[6]

    order = WEIGHT_NAMES
    del order
    names = ["norm_ffn1", "ffn1_w_gate", "ffn1_w_up", "ffn1_w_down", "norm_mix", "w_in", "w_proj_dil", "w_proj_sb",
             "w_out", "norm_ffn2", "ffn2_w_gate", "ffn2_w_up", "ffn2_w_down", "norm_final"]
    return (out["loss"], out["grad_x"], *[out["grad_" + n] for n in names], *[out["delta_" + n] for n in names],
            *[out["new_m_" + n] for n in names], *[out["new_v_" + n] for n in names])
```

```python
import functools

import jax
import jax.numpy as jnp
from jax import lax
from jax.experimental import pallas as pl
from jax.experimental.pallas import tpu as pltpu

F32 = jnp.float32
BF16 = jnp.bfloat16

D_MODEL = 1024
DEPTH = 2
N_CHIPS = 4
HEAD_DIM = 64
ROPE_DIM = 16
ROPE_THETA = 500000.0
DIL_GROUPS = ((128, 1), (512, 4), (2048, 16))
SPAN = 128
Q_BLOCK = 128
RMS_EPS = 1e-6
D_ATT = 256
COL_QS = 2304
COL_GD = 3072
COL_GS = 4096
ADAM_LR, ADAM_B1, ADAM_B2, ADAM_EPS, ADAM_WD, ADAM_STEP = 0.001, 0.9, 0.999, 1e-08, 0.01, 10

VMEM_LIMIT = 52 * 1024 * 1024
TM = 512
NEG = -1e30

NN = (((1,), (0,)), ((), ()))
NT = (((1,), (1,)), ((), ()))
TN = (((0,), (0,)), ((), ()))
MESH = pl.DeviceIdType.MESH

WEIGHT_NAMES = ("ffn1_w_gate", "ffn1_w_up", "ffn1_w_down", "w_in", "w_proj_dil",
                "w_proj_sb", "w_out", "ffn2_w_gate", "ffn2_w_up", "ffn2_w_down")
NORM_NAMES = ("norm_ffn1", "norm_mix", "norm_ffn2")


def _params(**kw):
    return pltpu.CompilerParams(vmem_limit_bytes=VMEM_LIMIT, **kw)


def _sigmoid(x):
    return 0.5 * jnp.tanh(0.5 * x) + 0.5


def _mm_body(pairs, n_in, n_out, n_acc, dims, nk, i_axis, epilogue, *refs):
    ins = refs[:n_in]
    outs = refs[n_in:n_in + n_out]
    accs = refs[n_in + n_out:]
    i = pl.program_id(i_axis)
    k = pl.program_id(2)

    def dot(ia, ib):
        return lax.dot_general(ins[ia][...].astype(BF16), ins[ib][...].astype(BF16), dims,
                               preferred_element_type=F32)

    if nk == 1:
        parts = [None] * n_acc
        for ia, ib, ic in pairs:
            parts[ic] = dot(ia, ib) if parts[ic] is None else parts[ic] + dot(ia, ib)
        epilogue(parts, ins, outs, i)
        return

    @pl.when(k == 0)
    def _():
        for c in range(n_acc):
            accs[c][...] = jnp.zeros_like(accs[c])

    for ia, ib, ic in pairs:
        accs[ic][...] += dot(ia, ib)

    @pl.when(k == nk - 1)
    def _():
        epilogue([a[...] for a in accs], ins, outs, i)


def _j_outer(spec):
    f = spec.index_map
    return pl.BlockSpec(spec.block_shape, lambda j, i, k: f(i, j, k))


def _mm(name, ins, in_specs, pairs, n_acc, acc_shape, dims, grid, epilogue, out_shapes, out_specs, j_outer=False):
    nk = grid[2]
    if j_outer:
        grid = (grid[1], grid[0], grid[2])
        in_specs = [_j_outer(s) for s in in_specs]
        out_specs = [_j_outer(s) for s in out_specs]
    scratch = [pltpu.VMEM(acc_shape, F32) for _ in range(n_acc)] if nk > 1 else []
    body = functools.partial(_mm_body, tuple(pairs), len(ins), len(out_shapes), n_acc, dims, nk,
                             1 if j_outer else 0, epilogue)
    return pl.pallas_call(
        body, name=name, grid=grid, in_specs=in_specs, out_specs=out_specs, out_shape=out_shapes,
        scratch_shapes=scratch,
        compiler_params=_params(dimension_semantics=("arbitrary", "arbitrary", "arbitrary")),
    )(*ins)


def _wspec(r, c, l, by):
    if by == 1:
        return pl.BlockSpec((None, None, r, c), lambda i, j, k: (j, l, 0, 0))
    return pl.BlockSpec((None, None, r, c), lambda i, j, k: (k, l, 0, 0))


def _rms_bwd_epilogue(x_idx, g_idx, dxo_idx):
    def ep(vals, ins, outs, i):
        dh = vals[0]
        x = ins[x_idx][...]
        g = ins[g_idx][...]
        rstd = lax.rsqrt(jnp.mean(x * x, axis=-1, keepdims=True) + RMS_EPS)
        xhat = x * rstd
        dxhat = dh * g
        dx = rstd * (dxhat - xhat * jnp.mean(dxhat * xhat, axis=-1, keepdims=True))
        outs[0][...] = ins[dxo_idx][...] + dx
        dg = jnp.broadcast_to(jnp.sum(dh * xhat, axis=0, keepdims=True), outs[1].shape)

        @pl.when(i == 0)
        def _():
            outs[1][...] = dg

        @pl.when(i > 0)
        def _():
            outs[1][...] += dg
    return ep


def _rms_fwd(x, gain):
    T = x.shape[0]

    def body(x_ref, g_ref, h_ref):
        xv = x_ref[...]
        h = xv * lax.rsqrt(jnp.mean(xv * xv, axis=-1, keepdims=True) + RMS_EPS)
        h_ref[...] = (h * g_ref[...]).astype(BF16)

    return pl.pallas_call(
        body, name="rms_fwd", grid=(T // TM,),
        in_specs=[pl.BlockSpec((TM, D_MODEL), lambda i: (i, 0)), pl.BlockSpec((1, D_MODEL), lambda i: (0, 0))],
        out_specs=pl.BlockSpec((TM, D_MODEL), lambda i: (i, 0)),
        out_shape=jax.ShapeDtypeStruct((T, D_MODEL), BF16), compiler_params=_params(),
    )(x, gain)


def _rope_tables(T):
    pos = jnp.arange(T, dtype=F32)
    inv_freq = ROPE_THETA ** (-jnp.arange(0, ROPE_DIM, 2, dtype=F32) / ROPE_DIM)
    ang = pos[:, None] * inv_freq[None, :]
    cos, sin = jnp.cos(ang), jnp.sin(ang)
    half = ROPE_DIM // 2
    one = jnp.ones((T, HEAD_DIM - ROPE_DIM), F32)
    zero = jnp.zeros((T, HEAD_DIM - ROPE_DIM), F32)
    zh = jnp.zeros((T, half), F32)
    c = jnp.concatenate([cos, cos, one], axis=1)
    s1 = jnp.concatenate([-sin, zh, zero], axis=1)
    s2 = jnp.concatenate([zh, sin, zero], axis=1)
    return tuple(jnp.concatenate([t, t], axis=1) for t in (c, s1, s2))


def _rope_fwd(xv, c, s1, s2):
    w = xv.shape[1]
    half = ROPE_DIM // 2
    return xv * c + pltpu.roll(xv, w - half, 1) * s1 + pltpu.roll(xv, half, 1) * s2


def _rope_bwd(dy, c, s1, s2):
    w = dy.shape[1]
    half = ROPE_DIM // 2
    return dy * c + pltpu.roll(dy * s1, half, 1) + pltpu.roll(dy * s2, w - half, 1)


def _assemble_dproj(dqk, rest, gates, tabs):
    T = gates[0].shape[0]
    n_qk, n_rest = len(dqk), len(rest)
    width = (n_qk + n_rest) * D_ATT + 2 * D_MODEL

    def body(*refs):
        ins, (c_ref, s1_ref, s2_ref), o_ref = refs[:n_qk + n_rest + 2], refs[-4:-1], refs[-1]
        c = jnp.concatenate([c_ref[...]] * 2, axis=1)
        s1 = jnp.concatenate([s1_ref[...]] * 2, axis=1)
        s2 = jnp.concatenate([s2_ref[...]] * 2, axis=1)
        for b in range(n_qk + n_rest):
            v = ins[b][...]
            if b < n_qk:
                v = _rope_bwd(v, c, s1, s2)
            o_ref[:, b * D_ATT:(b + 1) * D_ATT] = v.astype(BF16)
        off = (n_qk + n_rest) * D_ATT
        o_ref[:, off:off + D_MODEL] = ins[-2][...]
        o_ref[:, off + D_MODEL:] = ins[-1][...]

    att = pl.BlockSpec((TM, D_ATT), lambda i: (i, 0))
    wide = pl.BlockSpec((TM, D_MODEL), lambda i: (i, 0))
    tab = pl.BlockSpec((TM, 128), lambda i: (i, 0))
    return pl.pallas_call(
        body, name="assemble_dproj", grid=(T // TM,),
        in_specs=[att] * (n_qk + n_rest) + [wide, wide, tab, tab, tab],
        out_specs=pl.BlockSpec((TM, width), lambda i: (i, 0)),
        out_shape=jax.ShapeDtypeStruct((T, width), BF16), compiler_params=_params(),
    )(*dqk, *rest, *gates, *tabs)


def _dil_merge(os_, lses):
    T = os_[0].shape[0]

    def body(o0, o1, o2, l0, l1, l2, o_ref, lse_ref):
        a, b, c = l0[...], l1[...], l2[...]
        m = jnp.maximum(jnp.maximum(a, b), c)
        ea, eb, ec = jnp.exp(a - m), jnp.exp(b - m), jnp.exp(c - m)
        den = ea + eb + ec
        o_ref[...] = (ea * o0[...] + eb * o1[...] + ec * o2[...]) / den
        lse_ref[...] = m + jnp.log(den)

    blk = pl.BlockSpec((TM, D_ATT), lambda i: (i, 0))
    sh = jax.ShapeDtypeStruct((T, D_ATT), F32)
    return pl.pallas_call(
        body, name="dil_merge", grid=(T // TM,), in_specs=[blk] * 6, out_specs=[blk, blk],
        out_shape=[sh, sh], compiler_params=_params(),
    )(*os_, *lses)


def _final_loss(x, gain, target):
    T = x.shape[0]

    def body(x_ref, g_ref, t_ref, dx_ref, dg_ref, loss_ref):
        xv = x_ref[...]
        g = g_ref[...]
        rstd = lax.rsqrt(jnp.mean(xv * xv, axis=-1, keepdims=True) + RMS_EPS)
        xhat = xv * rstd
        err = xhat * g - t_ref[...]
        loss = 0.5 * jnp.sum(jnp.mean(err * err, axis=-1, keepdims=True), axis=0, keepdims=True)
        dy = err * (1.0 / D_MODEL)
        dxhat = dy * g
        dx_ref[...] = rstd * (dxhat - xhat * jnp.mean(dxhat * xhat, axis=-1, keepdims=True))
        dg = jnp.broadcast_to(jnp.sum(dy * xhat, axis=0, keepdims=True), dg_ref.shape)
        ls = jnp.broadcast_to(loss, loss_ref.shape)

        @pl.when(pl.program_id(0) == 0)
        def _():
            dg_ref[...] = dg
            loss_ref[...] = ls

        @pl.when(pl.program_id(0) > 0)
        def _():
            dg_ref[...] += dg
            loss_ref[...] += ls

    blk = pl.BlockSpec((TM, D_MODEL), lambda i: (i, 0))
    row = pl.BlockSpec((1, D_MODEL), lambda i: (0, 0))
    acc = pl.BlockSpec((8, D_MODEL), lambda i: (0, 0))
    return pl.pallas_call(
        body, name="final_loss", grid=(T // TM,), in_specs=[blk, row, blk], out_specs=[blk, acc, acc],
        out_shape=[jax.ShapeDtypeStruct((T, D_MODEL), F32), jax.ShapeDtypeStruct((8, D_MODEL), F32),
                   jax.ShapeDtypeStruct((8, D_MODEL), F32)],
        compiler_params=_params(dimension_semantics=("arbitrary",)),
    )(x, gain, target)


def _head_masks():
    lane = lax.broadcasted_iota(jnp.int32, (SPAN, 128), 1)
    return lane < HEAD_DIM


def _dil_rows(idx, d):
    u = idx // d
    r = idx - u * d
    own = pl.ds(u * (SPAN * d) + r, SPAN, stride=d) if d > 1 else pl.ds(pl.multiple_of(u * SPAN, SPAN), SPAN)
    up = jnp.maximum(u - 1, 0)
    prev = pl.ds(up * (SPAN * d) + r, SPAN, stride=d) if d > 1 else pl.ds(pl.multiple_of(up * SPAN, SPAN), SPAN)
    return u, own, prev


def _dil_scores(qm, k_own, k_prev, u):
    qi = lax.broadcasted_iota(jnp.int32, (SPAN, SPAN), 0)
    kj = lax.broadcasted_iota(jnp.int32, (SPAN, SPAN), 1)
    s_own = lax.dot_general(qm, k_own, NT, preferred_element_type=F32) * (HEAD_DIM ** -0.5)
    s_prev = lax.dot_general(qm, k_prev, NT, preferred_element_type=F32) * (HEAD_DIM ** -0.5)
    ok_own = kj <= qi
    ok_prev = kj >= qi + jnp.where(u > 0, 0, SPAN)
    return s_own, s_prev, ok_own, ok_prev


def _dil_fwd(proj, g, d):
    T = proj.shape[0]
    n_iter = T // SPAN

    def body(q_ref, k_ref, v_ref, o_ref, lse_ref):
        first = _head_masks()

        def step(idx, carry):
            u, own, prev = _dil_rows(idx, d)
            q = q_ref[own, :]
            k_own = k_ref[own, :].astype(BF16)
            k_prev = k_ref[prev, :].astype(BF16)
            v_own = v_ref[own, :].astype(BF16)
            v_prev = v_ref[prev, :].astype(BF16)
            o_h, lse_h = [], []
            for h in range(2):
                qm = jnp.where(first if h == 0 else ~first, q, 0.0).astype(BF16)
                s_own, s_prev, ok_own, ok_prev = _dil_scores(qm, k_own, k_prev, u)
                s_own = jnp.where(ok_own, s_own, NEG)
                s_prev = jnp.where(ok_prev, s_prev, NEG)
                m = jnp.maximum(jnp.max(s_own, axis=1, keepdims=True), jnp.max(s_prev, axis=1, keepdims=True))
                p_own = jnp.exp(s_own - m)
                p_prev = jnp.exp(s_prev - m)
                den = jnp.sum(p_own, axis=1, keepdims=True) + jnp.sum(p_prev, axis=1, keepdims=True)
                pv = (lax.dot_general(p_own.astype(BF16), v_own, NN, preferred_element_type=F32)
                      + lax.dot_general(p_prev.astype(BF16), v_prev, NN, preferred_element_type=F32))
                o_h.append(pv / den)
                lse_h.append(jnp.broadcast_to(m + jnp.log(den), (SPAN, 128)))
            o_ref[own, :] = jnp.where(first, o_h[0], o_h[1])
            lse_ref[own, :] = jnp.where(first, lse_h[0], lse_h[1])
            return carry

        lax.fori_loop(0, n_iter, step, 0)

    def col(b):
        return pl.BlockSpec((T, 128), lambda p: (0, b + p))

    sh = jax.ShapeDtypeStruct((T, D_ATT), F32)
    out = pl.BlockSpec((T, 128), lambda p: (0, p))
    return pl.pallas_call(
        body, name=f"dil_fwd_d{d}", grid=(2,),
        in_specs=[col(2 * g), col(6 + 2 * g), col(12 + 2 * g)], out_specs=[out, out], out_shape=[sh, sh],
        compiler_params=_params(dimension_semantics=("arbitrary",)),
    )(proj, proj, proj)


def _dil_bwd(proj, do, o_dil, lse, g, d):
    T = proj.shape[0]
    n_iter = T // SPAN

    def body(q_ref, k_ref, v_ref, do_ref, o_ref, lse_ref, dq_ref, dk_ref, dv_ref):
        first = _head_masks()

        def step(idx, carry):
            u, own, prev = _dil_rows(idx, d)
            q = q_ref[own, :]
            k_own = k_ref[own, :].astype(BF16)
            k_prev = k_ref[prev, :].astype(BF16)
            v_own = v_ref[own, :].astype(BF16)
            v_prev = v_ref[prev, :].astype(BF16)
            do_v = do_ref[own, :]
            oo = o_ref[own, :]
            ls = lse_ref[own, :]
            dq_h = []
            dk_own = dk_prev = dv_own = dv_prev = None
            for h in range(2):
                hm = first if h == 0 else ~first
                qm = jnp.where(hm, q, 0.0).astype(BF16)
                dom = jnp.where(hm, do_v, 0.0)
                dob = dom.astype(BF16)
                delta = jnp.sum(dom * oo, axis=1, keepdims=True)
                lrow = jnp.max(jnp.where(hm, ls, NEG), axis=1, keepdims=True)
                s_own, s_prev, ok_own, ok_prev = _dil_scores(qm, k_own, k_prev, u)
                p_own = jnp.where(ok_own, jnp.exp(s_own - lrow), 0.0)
                p_prev = jnp.where(ok_prev, jnp.exp(s_prev - lrow), 0.0)
                dp_own = lax.dot_general(dob, v_own, NT, preferred_element_type=F32)
                dp_prev = lax.dot_general(dob, v_prev, NT, preferred_element_type=F32)
                ds_own = (p_own * (dp_own - delta) * (HEAD_DIM ** -0.5)).astype(BF16)
                ds_prev = (p_prev * (dp_prev - delta) * (HEAD_DIM ** -0.5)).astype(BF16)
                dq_h.append(lax.dot_general(ds_own, k_own, NN, preferred_element_type=F32)
                            + lax.dot_general(ds_prev, k_prev, NN, preferred_element_type=F32))
                a = lax.dot_general(ds_own, qm, TN, preferred_element_type=F32)
                b = lax.dot_general(ds_prev, qm, TN, preferred_element_type=F32)
                c = lax.dot_general(p_own.astype(BF16), dob, TN, preferred_element_type=F32)
                e = lax.dot_general(p_prev.astype(BF16), dob, TN, preferred_element_type=F32)
                dk_own = a if dk_own is None else dk_own + a
                dk_prev = b if dk_prev is None else dk_prev + b
                dv_own = c if dv_own is None else dv_own + c
                dv_prev = e if dv_prev is None else dv_prev + e
            dq_ref[own, :] = jnp.where(first, dq_h[0], dq_h[1])
            dk_ref[own, :] = dk_own
            dv_ref[own, :] = dv_own
            dk_ref[prev, :] = dk_ref[prev, :] + dk_prev
            dv_ref[prev, :] = dv_ref[prev, :] + dv_prev
            return carry

        lax.fori_loop(0, n_iter, step, 0)

    def col(b):
        return pl.BlockSpec((T, 128), lambda p: (0, b + p))

    sh = jax.ShapeDtypeStruct((T, D_ATT), F32)
    return pl.pallas_call(
        body, name=f"dil_bwd_d{d}", grid=(2,),
        in_specs=[col(2 * g), col(6 + 2 * g), col(12 + 2 * g), col(0), col(0), col(0)],
        out_specs=[col(0), col(0), col(0)], out_shape=[sh, sh, sh],
        compiler_params=_params(dimension_semantics=("arbitrary",)),
    )(proj, proj, proj, do, o_dil, lse)


SB_KT = 512


def _sb_tri(strict):
    a = lax.broadcasted_iota(jnp.int32, (Q_BLOCK, Q_BLOCK), 0)
    b = lax.broadcasted_iota(jnp.int32, (Q_BLOCK, Q_BLOCK), 1)
    return jnp.where((a > b) if strict else (a >= b), 1.0, 0.0).astype(BF16)


def _suffix(x, c, tri):
    nb = x.shape[1] // Q_BLOCK
    blocks = [x[:, Q_BLOCK * b:Q_BLOCK * (b + 1)] for b in range(nb)]
    hi = [b.astype(BF16) for b in blocks]
    lo = [(b - h.astype(F32)).astype(BF16) for b, h in zip(blocks, hi)]
    y = lax.dot_general(jnp.concatenate(hi + lo, axis=0), tri, NN, preferred_element_type=F32)
    outs = [None] * nb
    run = c
    for b in reversed(range(nb)):
        outs[b] = run + y[Q_BLOCK * b:Q_BLOCK * (b + 1)] + y[Q_BLOCK * (nb + b):Q_BLOCK * (nb + b + 1)]
        run = run + jnp.sum(blocks[b], axis=1, keepdims=True)
    return jnp.concatenate(outs, axis=1), run


def _sb_tile(qm, kb, past, c, tri):
    z = lax.dot_general(qm, kb, NT, preferred_element_type=F32) * (HEAD_DIM ** -0.5)
    lsz = jnp.minimum(z, 0.0) - jnp.log1p(jnp.exp(-jnp.abs(z)))
    lk = jnp.where(past, lsz - z, 0.0)
    after, c_new = _suffix(lk, c, tri)
    w = jnp.where(past, jnp.exp(lsz + after), 0.0)
    return z, lsz, w, c_new


def _sb_past(i, t):
    row = lax.broadcasted_iota(jnp.int32, (Q_BLOCK, SB_KT), 0)
    col = lax.broadcasted_iota(jnp.int32, (Q_BLOCK, SB_KT), 1)
    return col + t * SB_KT < row + i * Q_BLOCK


def _sb_fwd(proj):
    T = proj.shape[0]

    def body(q_ref, k_ref, v_ref, o_ref):
        i = pl.program_id(1)
        first = _head_masks()
        tri = _sb_tri(True)
        q = q_ref[...]
        qms = [jnp.where(first, q, 0.0).astype(BF16), jnp.where(first, 0.0, q).astype(BF16)]
        n_tiles = (i * Q_BLOCK) // SB_KT + 1

        def step(tt, carry):
            t = n_tiles - 1 - tt
            rows = pl.ds(pl.multiple_of(t * SB_KT, SB_KT), SB_KT)
            kb = k_ref[rows, :].astype(BF16)
            vb = v_ref[rows, :].astype(BF16)
            past = _sb_past(i, t)
            out = []
            for h in range(2):
                acc, c = carry[2 * h], carry[2 * h + 1]
                _, _, w, c = _sb_tile(qms[h], kb, past, c, tri)
                out += [acc + lax.dot_general(w.astype(BF16), vb, NN, preferred_element_type=F32), c]
            return tuple(out)

        zero, zcol = jnp.zeros((Q_BLOCK, 128), F32), jnp.zeros((Q_BLOCK, 1), F32)
        res = lax.fori_loop(0, n_tiles, step, (zero, zcol, zero, zcol))
        o_ref[...] = jnp.where(first, res[0], res[2])

    cb = COL_QS // 128
    return pl.pallas_call(
        body, name="sb_fwd", grid=(2, T // Q_BLOCK),
        in_specs=[pl.BlockSpec((Q_BLOCK, 128), lambda p, i: (i, cb + p)),
                  pl.BlockSpec((T, 128), lambda p, i: (0, cb + 2 + p)),
                  pl.BlockSpec((T, 128), lambda p, i: (0, cb + 4 + p))],
        out_specs=pl.BlockSpec((Q_BLOCK, 128), lambda p, i: (i, p)),
        out_shape=jax.ShapeDtypeStruct((T, D_ATT), F32),
        compiler_params=_params(dimension_semantics=("arbitrary", "arbitrary")),
    )(proj, proj, proj)


def _sb_bwd(proj, do, o):
    T = proj.shape[0]

    def body(q_ref, k_ref, v_ref, do_ref, o_ref, dq_ref, dk_ref, dv_ref):
        i = pl.program_id(1)
        first = _head_masks()
        tri = _sb_tri(True)
        tri_incl = _sb_tri(False)

        @pl.when(i == 0)
        def _():
            dk_ref[...] = jnp.zeros_like(dk_ref)
            dv_ref[...] = jnp.zeros_like(dv_ref)

        q = q_ref[...]
        do_v = do_ref[...]
        oo = o_ref[...]
        qms = [jnp.where(first, q, 0.0).astype(BF16), jnp.where(first, 0.0, q).astype(BF16)]
        dobs = [jnp.where(first, do_v, 0.0).astype(BF16), jnp.where(first, 0.0, do_v).astype(BF16)]
        deltas = [jnp.sum(d.astype(F32) * oo, axis=1, keepdims=True) for d in dobs]
        n_tiles = (i * Q_BLOCK) // SB_KT + 1

        def step(tt, carry):
            t = n_tiles - 1 - tt
            rows = pl.ds(pl.multiple_of(t * SB_KT, SB_KT), SB_KT)
            kb = k_ref[rows, :].astype(BF16)
            vb = v_ref[rows, :].astype(BF16)
            past = _sb_past(i, t)
            out = []
            dk_t = dv_t = None
            for h in range(2):
                dq, c, ce = carry[3 * h:3 * h + 3]
                z, lsz, w, c = _sb_tile(qms[h], kb, past, c, tri)
                gv = lax.dot_general(dobs[h], vb, NT, preferred_element_type=F32)
                wb = w.astype(BF16)
                e = wb.astype(F32) * gv
                suf, ce = _suffix(e, ce, tri_incl)
                big_e = deltas[h] - suf
                dz = jnp.where(past, e * jnp.exp(lsz - z) - big_e * jnp.exp(lsz), 0.0) * (HEAD_DIM ** -0.5)
                dzb = dz.astype(BF16)
                dq = dq + lax.dot_general(dzb, kb, NN, preferred_element_type=F32)
                a = lax.dot_general(dzb, qms[h], TN, preferred_element_type=F32)
                b = lax.dot_general(wb, dobs[h], TN, preferred_element_type=F32)
                dk_t = a if dk_t is None else dk_t + a
                dv_t = b if dv_t is None else dv_t + b
                out += [dq, c, ce]
            dk_ref[rows, :] = dk_ref[rows, :] + dk_t
            dv_ref[rows, :] = dv_ref[rows, :] + dv_t
            return tuple(out)

        zero, zcol = jnp.zeros((Q_BLOCK, 128), F32), jnp.zeros((Q_BLOCK, 1), F32)
        res = lax.fori_loop(0, n_tiles, step, (zero, zcol, zcol, zero, zcol, zcol))
        dq_ref[...] = jnp.where(first, res[0], res[3])

    cb = COL_QS // 128
    blk = pl.BlockSpec((Q_BLOCK, 128), lambda p, i: (i, p))
    full = pl.BlockSpec((T, 128), lambda p, i: (0, p))
    sh = jax.ShapeDtypeStruct((T, D_ATT), F32)
    return pl.pallas_call(
        body, name="sb_bwd", grid=(2, T // Q_BLOCK),
        in_specs=[pl.BlockSpec((Q_BLOCK, 128), lambda p, i: (i, cb + p)),
                  pl.BlockSpec((T, 128), lambda p, i: (0, cb + 2 + p)),
                  pl.BlockSpec((T, 128), lambda p, i: (0, cb + 4 + p)), blk, blk],
        out_specs=[blk, full, full], out_shape=[sh, sh, sh],
        compiler_params=_params(dimension_semantics=("arbitrary", "arbitrary")),
    )(proj, proj, proj, do, o)


def _tok(c, by=None):
    if by is None:
        return pl.BlockSpec((TM, c), lambda i, j, k: (i, 0))
    if by == 1:
        return pl.BlockSpec((TM, c), lambda i, j, k: (i, j))
    return pl.BlockSpec((TM, c), lambda i, j, k: (i, k))


def _chunked(c, by):
    if by == 1:
        return pl.BlockSpec((None, TM, c), lambda i, j, k: (j, i, 0))
    return pl.BlockSpec((None, TM, c), lambda i, j, k: (k, i, 0))


def _gain_spec():
    return pl.BlockSpec((1, D_MODEL), lambda i, j, k: (0, 0))


def _ffn_fwd(x, gain, wg, wu, wd, l):
    T = x.shape[0]
    ffs = wg.shape[3]
    h = _rms_fwd(x, gain)

    def swiglu(vals, ins, outs, i):
        gt, up = vals
        outs[0][...] = gt.astype(BF16)
        outs[1][...] = up.astype(BF16)
        outs[2][...] = (gt * _sigmoid(gt) * up).astype(BF16)

    csh = jax.ShapeDtypeStruct((N_CHIPS, T, ffs), BF16)
    gate, up, act = _mm(
        "ffn_up", [h, wg, wu], [_tok(D_MODEL), _wspec(D_MODEL, ffs, l, 1), _wspec(D_MODEL, ffs, l, 1)],
        [(0, 1, 0), (0, 2, 1)], 2, None, NN, (T // TM, N_CHIPS, 1), swiglu,
        [csh, csh, csh], [_chunked(ffs, 1)] * 3, j_outer=True)

    def resid(vals, ins, outs, i):
        outs[0][...] = ins[2][...] + 0.5 * vals[0]

    (y,) = _mm(
        "ffn_down", [act, wd, x], [_chunked(ffs, 2), _wspec(ffs, D_MODEL, l, 2), _tok(D_MODEL)],
        [(0, 1, 0)], 1, (TM, D_MODEL), NN, (T // TM, 1, N_CHIPS), resid,
        [jax.ShapeDtypeStruct((T, D_MODEL), F32)], [_tok(D_MODEL)])
    return y, (x, h, gate, up, act)


def _ffn_bwd(dxo, gain, wg, wu, wd, l, saved):
    x, h, gate, up, act = saved
    T = x.shape[0]
    ffs = wg.shape[3]
    tk = TM

    def dswiglu(vals, ins, outs, i):
        da = 0.5 * vals[0]
        gt = ins[2][...].astype(F32)
        u = ins[3][...].astype(F32)
        s = _sigmoid(gt)
        outs[0][...] = (da * u * (s * (1.0 + gt * (1.0 - s)))).astype(BF16)
        outs[1][...] = (da * (gt * s)).astype(BF16)

    csh = jax.ShapeDtypeStruct((N_CHIPS, T, ffs), BF16)
    dgate, dup = _mm(
        "ffn_dact", [dxo, wd, gate, up],
        [_tok(D_MODEL), _wspec(ffs, D_MODEL, l, 1), _chunked(ffs, 1), _chunked(ffs, 1)],
        [(0, 1, 0)], 1, None, NT, (T // TM, N_CHIPS, 1), dswiglu, [csh, csh], [_chunked(ffs, 1)] * 2)

    def half(vals, ins, outs, i):
        outs[0][...] = (0.5 * vals[0]).astype(BF16)

    (dwd,) = _mm(
        "ffn_dwd", [act, dxo],
        [pl.BlockSpec((None, tk, ffs), lambda i, j, k: (j, k, 0)), pl.BlockSpec((tk, D_MODEL), lambda i, j, k: (k, 0))],
        [(0, 1, 0)], 1, (ffs, D_MODEL), TN, (1, N_CHIPS, T // tk), half,
        [jax.ShapeDtypeStruct((N_CHIPS, ffs, D_MODEL), BF16)],
        [pl.BlockSpec((None, ffs, D_MODEL), lambda i, j, k: (j, 0, 0))])

    dx, dgain = _mm(
        "ffn_dx", [dgate, dup, wg, wu, x, gain, dxo],
        [_chunked(ffs, 2), _chunked(ffs, 2), _wspec(D_MODEL, ffs, l, 2), _wspec(D_MODEL, ffs, l, 2),
         _tok(D_MODEL), _gain_spec(), _tok(D_MODEL)],
        [(0, 2, 0), (1, 3, 0)], 1, (TM, D_MODEL), NT, (T // TM, 1, N_CHIPS), _rms_bwd_epilogue(4, 5, 6),
        [jax.ShapeDtypeStruct((T, D_MODEL), F32), jax.ShapeDtypeStruct((8, D_MODEL), F32)],
        [_tok(D_MODEL), pl.BlockSpec((8, D_MODEL), lambda i, j, k: (0, 0))])

    def two(vals, ins, outs, i):
        outs[0][...] = vals[0].astype(BF16)
        outs[1][...] = vals[1].astype(BF16)

    wsh = jax.ShapeDtypeStruct((N_CHIPS, D_MODEL, ffs), BF16)
    wout = pl.BlockSpec((None, D_MODEL, ffs), lambda i, j, k: (j, 0, 0))
    cin = pl.BlockSpec((None, tk, ffs), lambda i, j, k: (j, k, 0))
    dwg, dwu = _mm(
        "ffn_dwgu", [h, dgate, dup], [pl.BlockSpec((tk, D_MODEL), lambda i, j, k: (k, 0)), cin, cin],
        [(0, 1, 0), (0, 2, 1)], 2, (D_MODEL, ffs), TN, (1, N_CHIPS, T // tk), two, [wsh, wsh], [wout, wout])
    return dx, dgain, dwg, dwu, dwd


def _mixer_fwd(x, gain, W, l, tabs):
    T = x.shape[0]
    win, wpd, wps, wo = W["w_in"], W["w_proj_dil"], W["w_proj_sb"], W["w_out"]
    cin = win.shape[3]
    cp = wpd.shape[3]
    h = _rms_fwd(x, gain)

    n_rope = 6 * D_ATT

    def roped(vals, ins, outs, i):
        v = vals[0]
        col0 = pl.program_id(0) * cin

        @pl.when(col0 < n_rope)
        def _():
            on = lax.broadcasted_iota(jnp.int32, v.shape, 1) + col0 < n_rope
            c = jnp.where(on, jnp.concatenate([ins[2][...]] * (cin // 128), axis=1), 1.0)
            s1 = jnp.where(on, jnp.concatenate([ins[3][...]] * (cin // 128), axis=1), 0.0)
            s2 = jnp.where(on, jnp.concatenate([ins[4][...]] * (cin // 128), axis=1), 0.0)
            outs[0][...] = _rope_fwd(v, c, s1, s2)

        @pl.when(col0 >= n_rope)
        def _():
            outs[0][...] = v

    (proj,) = _mm(
        "mix_in", [h, win, *tabs], [_tok(D_MODEL), _wspec(D_MODEL, cin, l, 1)] + [_tok(128)] * 3, [(0, 1, 0)], 1,
        None, NN, (T // TM, N_CHIPS, 1), roped, [jax.ShapeDtypeStruct((T, N_CHIPS * cin), F32)], [_tok(cin, 1)],
        j_outer=True)

    os_, lses = [], []
    for g, (window, dil) in enumerate(DIL_GROUPS):
        o_g, lse_g = _dil_fwd(proj, g, dil)
        os_.append(o_g)
        lses.append(lse_g)
    o_dil, lse = _dil_merge(os_, lses)
    o_sb = _sb_fwd(proj)

    def gated(vals, ins, outs, i):
        pd, ps = vals
        outs[0][...] = (_sigmoid(ins[4][...]) * pd + _sigmoid(ins[5][...]) * ps).astype(BF16)
        outs[1][...] = pd.astype(BF16)
        outs[2][...] = ps.astype(BF16)

    gd0, gs0 = COL_GD // cp, COL_GS // cp
    ush = jax.ShapeDtypeStruct((T, D_MODEL), BF16)
    u, pd, ps = _mm(
        "mix_gate", [o_dil, o_sb, wpd, wps, proj, proj],
        [_tok(D_ATT), _tok(D_ATT), _wspec(D_ATT, cp, l, 1), _wspec(D_ATT, cp, l, 1),
         pl.BlockSpec((TM, cp), lambda i, j, k: (i, gd0 + j)), pl.BlockSpec((TM, cp), lambda i, j, k: (i, gs0 + j))],
        [(0, 2, 0), (1, 3, 1)], 2, None, NN, (T // TM, N_CHIPS, 1), gated, [ush] * 3, [_tok(cp, 1)] * 3)

    def resid(vals, ins, outs, i):
        outs[0][...] = ins[2][...] + vals[0]

    (y,) = _mm(
        "mix_out", [u, wo, x], [_tok(cp, 2), _wspec(cp, D_MODEL, l, 2), _tok(D_MODEL)],
        [(0, 1, 0)], 1, (TM, D_MODEL), NN, (T // TM, 1, N_CHIPS), resid,
        [jax.ShapeDtypeStruct((T, D_MODEL), F32)], [_tok(D_MODEL)])
    return y, (x, h, proj, o_dil, lse, o_sb, u, pd, ps)


def _mixer_bwd(dxo, gain, W, l, tabs, saved):
    x, h, proj, o_dil, lse, o_sb, u, pd, ps = saved
    T = x.shape[0]
    win, wpd, wps, wo = W["w_in"], W["w_proj_dil"], W["w_proj_sb"], W["w_out"]
    cin = win.shape[3]
    cp = wpd.shape[3]
    tk = TM
    gd0, gs0 = COL_GD // cp, COL_GS // cp

    def dgated(vals, ins, outs, i):
        du = vals[0]
        sd = _sigmoid(ins[4][...])
        ss = _sigmoid(ins[5][...])
        outs[0][...] = (du * sd).astype(BF16)
        outs[1][...] = (du * ss).astype(BF16)
        outs[2][...] = (du * ins[2][...].astype(F32) * sd * (1.0 - sd)).astype(BF16)
        outs[3][...] = (du * ins[3][...].astype(F32) * ss * (1.0 - ss)).astype(BF16)

    ush = jax.ShapeDtypeStruct((T, D_MODEL), BF16)
    dpd, dps, dgd, dgs = _mm(
        "mix_du", [dxo, wo, pd, ps, proj, proj],
        [_tok(D_MODEL), _wspec(cp, D_MODEL, l, 1), _tok(cp, 1), _tok(cp, 1),
         pl.BlockSpec((TM, cp), lambda i, j, k: (i, gd0 + j)), pl.BlockSpec((TM, cp), lambda i, j, k: (i, gs0 + j))],
        [(0, 1, 0)], 1, None, NT, (T // TM, N_CHIPS, 1), dgated, [ush] * 4, [_tok(cp, 1)] * 4)

    def one(vals, ins, outs, i):
        outs[0][...] = vals[0].astype(BF16)

    def two(vals, ins, outs, i):
        outs[0][...] = vals[0].astype(BF16)
        outs[1][...] = vals[1].astype(BF16)

    (dwo,) = _mm(
        "mix_dwo", [u, dxo],
        [pl.BlockSpec((tk, cp), lambda i, j, k: (k, j)), pl.BlockSpec((tk, D_MODEL), lambda i, j, k: (k, 0))],
        [(0, 1, 0)], 1, (cp, D_MODEL), TN, (1, N_CHIPS, T // tk), one,
        [jax.ShapeDtypeStruct((N_CHIPS, cp, D_MODEL), BF16)],
        [pl.BlockSpec((None, cp, D_MODEL), lambda i, j, k: (j, 0, 0))])

    def plain2(vals, ins, outs, i):
        outs[0][...] = vals[0]
        outs[1][...] = vals[1]

    ash = jax.ShapeDtypeStruct((T, D_ATT), F32)
    do_dil, do_sb = _mm(
        "mix_do", [dpd, dps, wpd, wps], [_tok(cp, 2), _tok(cp, 2), _wspec(D_ATT, cp, l, 2), _wspec(D_ATT, cp, l, 2)],
        [(0, 2, 0), (1, 3, 1)], 2, (TM, D_ATT), NT, (T // TM, 1, N_CHIPS), plain2, [ash, ash], [_tok(D_ATT)] * 2)

    psh = jax.ShapeDtypeStruct((N_CHIPS, D_ATT, cp), BF16)
    pspec = pl.BlockSpec((None, D_ATT, cp), lambda i, j, k: (j, 0, 0))
    arow = pl.BlockSpec((tk, D_ATT), lambda i, j, k: (k, 0))
    dcol = pl.BlockSpec((tk, cp), lambda i, j, k: (k, j))
    dwpd, dwps = _mm(
        "mix_dwp", [o_dil, o_sb, dpd, dps], [arow, arow, dcol, dcol], [(0, 2, 0), (1, 3, 1)], 2, (D_ATT, cp), TN,
        (1, N_CHIPS, T // tk), two, [psh, psh], [pspec, pspec])

    dqs, dks, dvs = [], [], []
    for g, (window, dil) in enumerate(DIL_GROUPS):
        dq, dk, dv = _dil_bwd(proj, do_dil, o_dil, lse, g, dil)
        dqs.append(dq)
        dks.append(dk)
        dvs.append(dv)
    dq_s, dk_s, dv_s = _sb_bwd(proj, do_sb, o_sb)
    dproj = _assemble_dproj(dqs + dks, dvs + [dq_s, dk_s, dv_s], [dgd, dgs], tabs)

    dx, dgain = _mm(
        "mix_dx", [dproj, win, x, gain, dxo],
        [_tok(cin, 2), _wspec(D_MODEL, cin, l, 2), _tok(D_MODEL), _gain_spec(), _tok(D_MODEL)],
        [(0, 1, 0)], 1, (TM, D_MODEL), NT, (T // TM, 1, N_CHIPS), _rms_bwd_epilogue(2, 3, 4),
        [jax.ShapeDtypeStruct((T, D_MODEL), F32), jax.ShapeDtypeStruct((8, D_MODEL), F32)],
        [_tok(D_MODEL), pl.BlockSpec((8, D_MODEL), lambda i, j, k: (0, 0))])

    (dwin,) = _mm(
        "mix_dwin", [h, dproj],
        [pl.BlockSpec((tk, D_MODEL), lambda i, j, k: (k, 0)), pl.BlockSpec((tk, cin), lambda i, j, k: (k, j))],
        [(0, 1, 0)], 1, (D_MODEL, cin), TN, (1, N_CHIPS, T // tk), one,
        [jax.ShapeDtypeStruct((N_CHIPS, D_MODEL, cin), BF16)],
        [pl.BlockSpec((None, D_MODEL, cin), lambda i, j, k: (j, 0, 0))])
    return dx, dgain, dwin, dwpd, dwps, dwo


def _local_step(x, target, norms, norm_final, W):
    T = x.shape[0]
    tabs = _rope_tables(T)
    saved = []
    for l in range(DEPTH):
        x, s1 = _ffn_fwd(x, norms["norm_ffn1"][l:l + 1], W["ffn1_w_gate"], W["ffn1_w_up"], W["ffn1_w_down"], l)
        x, s2 = _mixer_fwd(x, norms["norm_mix"][l:l + 1], W, l, tabs)
        x, s3 = _ffn_fwd(x, norms["norm_ffn2"][l:l + 1], W["ffn2_w_gate"], W["ffn2_w_up"], W["ffn2_w_down"], l)
        saved.append((s1, s2, s3))
    dx, dg_final, loss = _final_loss(x, norm_final.reshape(1, D_MODEL), target)
    grads = [None] * DEPTH
    gains = [None] * DEPTH
    for l in reversed(range(DEPTH)):
        s1, s2, s3 = saved[l]
        dx, dg2, dwg2, dwu2, dwd2 = _ffn_bwd(dx, norms["norm_ffn2"][l:l + 1], W["ffn2_w_gate"], W["ffn2_w_up"],
                                             W["ffn2_w_down"], l, s3)
        dx, dgm, dwin, dwpd, dwps, dwo = _mixer_bwd(dx, norms["norm_mix"][l:l + 1], W, l, tabs, s2)
        dx, dg1, dwg1, dwu1, dwd1 = _ffn_bwd(dx, norms["norm_ffn1"][l:l + 1], W["ffn1_w_gate"], W["ffn1_w_up"],
                                             W["ffn1_w_down"], l, s1)
        grads[l] = dict(ffn1_w_gate=dwg1, ffn1_w_up=dwu1, ffn1_w_down=dwd1, w_in=dwin, w_proj_dil=dwpd,
                        w_proj_sb=dwps, w_out=dwo, ffn2_w_gate=dwg2, ffn2_w_up=dwu2, ffn2_w_down=dwd2)
        gains[l] = dict(norm_ffn1=dg1, norm_mix=dgm, norm_ffn2=dg2)
    return loss, dx, grads, gains, dg_final


def _place():
    x, y, c = lax.axis_index("x"), lax.axis_index("y"), lax.axis_index("c")
    chips = [(1 - x, y), (x, 1 - y), (1 - x, 1 - y)]
    return x, y, c, chips


def _cast_into_slot(w, me_arr):
    L, r, cw = w.shape
    tr = r // 4 if r > 256 else r

    def body(me_ref, w_ref, o_ref):
        o_ref[...] = w_ref[...].astype(BF16)

    return pl.pallas_call(
        body, name="cast_weights",
        grid_spec=pltpu.PrefetchScalarGridSpec(
            num_scalar_prefetch=1, grid=(L, r // tr),
            in_specs=[pl.BlockSpec((None, tr, cw), lambda l, i, me: (l, i, 0))],
            out_specs=pl.BlockSpec((None, None, tr, cw), lambda l, i, me: (me[0], l, i, 0))),
        out_shape=jax.ShapeDtypeStruct((N_CHIPS, L, r, cw), BF16), compiler_params=_params(),
    )(me_arr, w)


def _gather_weights(bufs):
    n = len(bufs)

    def body(*refs):
        out_refs = refs[n:2 * n]
        send_sems, recv_sems, fsend_sems, frecv_sems = refs[2 * n:]
        x, y, c, chips = _place()
        me = 2 * x + y
        sibling = (x, y, 1 - c)

        def ici(a, j, chip_id, to):
            return pltpu.make_async_remote_copy(
                src_ref=out_refs[a].at[me, c], dst_ref=out_refs[a].at[chip_id, c],
                send_sem=send_sems.at[a, j], recv_sem=recv_sems.at[a, j], device_id=to, device_id_type=MESH)

        def d2d(a, j, chip_id, layer):
            return pltpu.make_async_remote_copy(
                src_ref=out_refs[a].at[chip_id, layer], dst_ref=out_refs[a].at[chip_id, layer],
                send_sem=fsend_sems.at[a, j], recv_sem=frecv_sems.at[a, j], device_id=sibling, device_id_type=MESH)

        sends = [ici(a, j, me, (*chip, c)) for a in range(n) for j, chip in enumerate(chips)]
        for cp in sends:
            cp.start()
        passed = []
        for a in range(n):
            for j, chip in enumerate(chips):
                cid = 2 * chip[0] + chip[1]
                ici(a, j, cid, (*chip, c)).wait_recv()
                fw = d2d(a, j, cid, c)
                fw.start()
                passed.append(fw)
        for a in range(n):
            for j, chip in enumerate(chips):
                d2d(a, j, 2 * chip[0] + chip[1], 1 - c).wait_recv()
        for cp in sends + passed:
            cp.wait_send()

    any_spec = pl.BlockSpec(memory_space=pl.ANY)
    return pl.pallas_call(
        body, name="gather_weights", in_specs=[any_spec] * n, out_specs=[any_spec] * n,
        out_shape=[jax.ShapeDtypeStruct(b.shape, b.dtype) for b in bufs],
        input_output_aliases={a: a for a in range(n)},
        scratch_shapes=[pltpu.SemaphoreType.DMA((n, 3))] * 4,
    )(*bufs)


def _half(c, r):
    return pl.ds(pl.multiple_of(c * (r // 2), 8), r // 2)


def _exchange_halves(gs):
    n = len(gs)

    def body(*refs):
        g_refs, out_refs = refs[:n], refs[n:2 * n]
        send_sems, recv_sems = refs[2 * n:]
        x, y, c, _ = _place()
        cps = []
        for a in range(n):
            r = g_refs[a].shape[2]
            cps.append(pltpu.make_async_remote_copy(
                src_ref=g_refs[a].at[:, :, _half(1 - c, r), :], dst_ref=out_refs[a],
                send_sem=send_sems.at[a], recv_sem=recv_sems.at[a], device_id=(x, y, 1 - c), device_id_type=MESH))
        for cp in cps:
            cp.start()
        for cp in cps:
            cp.wait()

    any_spec = pl.BlockSpec(memory_space=pl.ANY)
    return pl.pallas_call(
        body, name="grad_to_sibling", in_specs=[any_spec] * n, out_specs=[any_spec] * n,
        out_shape=[jax.ShapeDtypeStruct((g.shape[0], g.shape[1], g.shape[2] // 2, g.shape[3]), g.dtype) for g in gs],
        scratch_shapes=[pltpu.SemaphoreType.DMA((n,))] * 2,
    )(*gs)


def _add_half(g, got, c_arr):
    n, _, r, cw = g.shape

    def body(c_ref, a_ref, b_ref, o_ref):
        o_ref[...] = (a_ref[...].astype(F32) + b_ref[...].astype(F32)).astype(BF16)

    return pl.pallas_call(
        body, name="grad_add_half",
        grid_spec=pltpu.PrefetchScalarGridSpec(
            num_scalar_prefetch=1, grid=(n, N_CHIPS),
            in_specs=[pl.BlockSpec((None, None, r // 2, cw), lambda a, k, cr: (a, k, cr[0], 0)),
                      pl.BlockSpec((None, None, r // 2, cw), lambda a, k, cr: (a, k, 0, 0))],
            out_specs=pl.BlockSpec((None, None, r // 2, cw), lambda a, k, cr: (a, k, 0, 0))),
        out_shape=jax.ShapeDtypeStruct((n, N_CHIPS, r // 2, cw), BF16), compiler_params=_params(),
    )(c_arr, g, got)


def _scatter_to_chips(ss):
    n = len(ss)

    def body(*refs):
        s_refs, out_refs = refs[:n], refs[n:2 * n]
        send_sems, recv_sems, local_sems = refs[2 * n:]
        x, y, c, chips = _place()
        me = 2 * x + y
        local = [pltpu.make_async_copy(s_refs[a].at[:, me], out_refs[a].at[:, me], local_sems.at[a])
                 for a in range(n)]
        for cp in local:
            cp.start()
        cps = []
        for a in range(n):
            for j, chip in enumerate(chips):
                cid = 2 * chip[0] + chip[1]
                cps.append(pltpu.make_async_remote_copy(
                    src_ref=s_refs[a].at[:, cid], dst_ref=out_refs[a].at[:, me],
                    send_sem=send_sems.at[a, j], recv_sem=recv_sems.at[a, j], device_id=(*chip, c),
                    device_id_type=MESH))
        for cp in cps:
            cp.start()
        for cp in cps:
            cp.wait()
        for cp in local:
            cp.wait()

    any_spec = pl.BlockSpec(memory_space=pl.ANY)
    return pl.pallas_call(
        body, name="grad_to_chips", in_specs=[any_spec] * n, out_specs=[any_spec] * n,
        out_shape=[jax.ShapeDtypeStruct(s.shape, s.dtype) for s in ss],
        scratch_shapes=[pltpu.SemaphoreType.DMA((n, 3))] * 2 + [pltpu.SemaphoreType.DMA((n,))],
    )(*ss)


def _sum_chips(got):
    n, _, rh, cw = got.shape

    def body(g_ref, o_ref):
        acc = g_ref[0].astype(F32)
        for k in range(1, N_CHIPS):
            acc = acc + g_ref[k].astype(F32)
        o_ref[...] = acc

    return pl.pallas_call(
        body, name="grad_sum_chips", grid=(n,),
        in_specs=[pl.BlockSpec((None, N_CHIPS, rh, cw), lambda a: (a, 0, 0, 0))],
        out_specs=pl.BlockSpec((None, rh, cw), lambda a: (a, 0, 0)),
        out_shape=jax.ShapeDtypeStruct((n, rh, cw), F32), compiler_params=_params(),
    )(got)


def _swap_halves(fs):
    n = len(fs)

    def body(*refs):
        f_refs, out_refs = refs[:n], refs[n:2 * n]
        send_sems, recv_sems = refs[2 * n:]
        x, y, c, _ = _place()
        cps = [pltpu.make_async_remote_copy(
            src_ref=f_refs[a], dst_ref=out_refs[a], send_sem=send_sems.at[a], recv_sem=recv_sems.at[a],
            device_id=(x, y, 1 - c), device_id_type=MESH) for a in range(n)]
        for cp in cps:
            cp.start()
        for cp in cps:
            cp.wait()

    any_spec = pl.BlockSpec(memory_space=pl.ANY)
    return pl.pallas_call(
        body, name="grad_swap_halves", in_specs=[any_spec] * n, out_specs=[any_spec] * n,
        out_shape=[jax.ShapeDtypeStruct(f.shape, f.dtype) for f in fs],
        scratch_shapes=[pltpu.SemaphoreType.DMA((n,))] * 2,
    )(*fs)


def _allreduce_rows(stats):
    def body(s_ref, o_ref, buf, send_sems, recv_sems):
        x, y, c, _ = _place()
        me = 4 * x + 2 * y + c
        buf[me] = s_ref[...]
        cps = []
        for k in range(1, 8):
            px = jnp.where(k & 4, 1 - x, x)
            py = jnp.where(k & 2, 1 - y, y)
            pc = jnp.where(k & 1, 1 - c, c)
            cps.append(pltpu.make_async_remote_copy(
                src_ref=s_ref, dst_ref=buf.at[me], send_sem=send_sems.at[k - 1], recv_sem=recv_sems.at[k - 1],
                device_id=(px, py, pc), device_id_type=MESH))
        for cp in cps:
            cp.start()
        for cp in cps:
            cp.wait()
        acc = buf[0]
        for d in range(1, 8):
            acc = acc + buf[d]
        o_ref[...] = acc

    vm = pl.BlockSpec(memory_space=pltpu.VMEM)
    return pl.pallas_call(
        body, name="allreduce_rows", in_specs=[vm], out_specs=vm,
        out_shape=jax.ShapeDtypeStruct(stats.shape, F32),
        scratch_shapes=[pltpu.VMEM((8,) + stats.shape, F32), pltpu.SemaphoreType.DMA((7,)),
                        pltpu.SemaphoreType.DMA((7,))],
    )(stats)


def _adamw_math(w, g, m, v):
    m = ADAM_B1 * m + (1.0 - ADAM_B1) * g
    v = ADAM_B2 * v + (1.0 - ADAM_B2) * (g * g)
    m_hat = m / (1.0 - ADAM_B1 ** ADAM_STEP)
    v_hat = v / (1.0 - ADAM_B2 ** ADAM_STEP)
    delta = -ADAM_LR * (m_hat / (jnp.sqrt(v_hat) + ADAM_EPS) + ADAM_WD * w)
    return delta, m, v


def _adamw(w, m, v, mine, theirs, first, c_arr):
    L, r, cw = w.shape
    tr = r // 4 if r > 256 else r // 2
    nblk = (r // 2) // tr

    def body(c_ref, w_ref, m_ref, v_ref, a_ref, b_ref, go_ref, d_ref, mo_ref, vo_ref):
        g = jnp.where(pl.program_id(1) == c_ref[0], a_ref[...], b_ref[...])
        delta, mn, vn = _adamw_math(w_ref[...], g, m_ref[...], v_ref[...])
        go_ref[...] = g
        d_ref[...] = delta
        mo_ref[...] = mn
        vo_ref[...] = vn

    blk = pl.BlockSpec((None, tr, cw), lambda l, hh, i, cr: (l, hh * nblk + i, 0))
    half = pl.BlockSpec((None, tr, cw), lambda l, hh, i, cr: (first + l, i, 0))
    sh = jax.ShapeDtypeStruct(w.shape, F32)
    return pl.pallas_call(
        body, name="adamw",
        grid_spec=pltpu.PrefetchScalarGridSpec(
            num_scalar_prefetch=1, grid=(L, 2, nblk), in_specs=[blk, blk, blk, half, half], out_specs=[blk] * 4),
        out_shape=[sh] * 4, compiler_params=_params(),
    )(c_arr, w, m, v, mine, theirs)


def _adamw_rows(w, m, v, g):
    def body(w_ref, m_ref, v_ref, g_ref, d_ref, mo_ref, vo_ref):
        delta, mn, vn = _adamw_math(w_ref[...], g_ref[...], m_ref[...], v_ref[...])
        d_ref[...] = delta
        mo_ref[...] = mn
        vo_ref[...] = vn

    vm = pl.BlockSpec(memory_space=pltpu.VMEM)
    sh = jax.ShapeDtypeStruct(w.shape, F32)
    return pl.pallas_call(body, name="adamw_rows", in_specs=[vm] * 4, out_specs=[vm] * 3, out_shape=[sh] * 3)(w, m, v, g)


GROUPS = (("ffn1_w_gate", "ffn1_w_up", "ffn2_w_gate", "ffn2_w_up"), ("ffn1_w_down", "ffn2_w_down"),
          ("w_in",), ("w_proj_dil", "w_proj_sb"), ("w_out",))


def _pick_row(blocks):
    row = lax.broadcasted_iota(jnp.int32, (8, D_MODEL), 0)
    out = jnp.zeros((8, D_MODEL), F32)
    for i, b in enumerate(blocks):
        out = out + jnp.where(row == i, b, 0.0)
    return out


def kernel(x, norm_ffn1, ffn1_w_gate, ffn1_w_up, ffn1_w_down, norm_mix, w_in, w_proj_dil, w_proj_sb, w_out, norm_ffn2, ffn2_w_gate, ffn2_w_up, ffn2_w_down, norm_final, loss_target, m_norm_ffn1, m_ffn1_w_gate, m_ffn1_w_up, m_ffn1_w_down, m_norm_mix, m_w_in, m_w_proj_dil, m_w_proj_sb, m_w_out, m_norm_ffn2, m_ffn2_w_gate, m_ffn2_w_up, m_ffn2_w_down, m_norm_final, v_norm_ffn1, v_ffn1_w_gate, v_ffn1_w_up, v_ffn1_w_down, v_norm_mix, v_w_in, v_w_proj_dil, v_w_proj_sb, v_w_out, v_norm_ffn2, v_ffn2_w_gate, v_ffn2_w_up, v_ffn2_w_down, v_norm_final):
    given = dict(locals())
    weights = {n: given[n] for n in WEIGHT_NAMES}
    norms = {n: given[n] for n in NORM_NAMES}

    c_arr = lax.axis_index("c").astype(jnp.int32).reshape(1)
    me_arr = (2 * lax.axis_index("x") + lax.axis_index("y")).astype(jnp.int32).reshape(1)
    gathered = _gather_weights([_cast_into_slot(weights[n], me_arr) for n in WEIGHT_NAMES])
    W = dict(zip(WEIGHT_NAMES, gathered))
    loss_blk, grad_x, grads, gains, dg_final = _local_step(x[0], loss_target[0], norms, norm_final, W)

    stacked = [jnp.stack([grads[l][n] for n in grp for l in range(DEPTH)]) for grp in GROUPS]
    from_sibling = _exchange_halves(stacked)
    chip_sums = [_add_half(g, got, c_arr) for g, got in zip(stacked, from_sibling)]
    from_chips = _scatter_to_chips(chip_sums)
    mine = [_sum_chips(got) for got in from_chips]
    theirs = _swap_halves(mine)

    out = {"grad_x": grad_x[None]}
    for grp, ga, gb in zip(GROUPS, mine, theirs):
        for i, n in enumerate(grp):
            g, d, mn, vn = _adamw(weights[n], given["m_" + n], given["v_" + n], ga, gb, i * DEPTH, c_arr)
            out["grad_" + n], out["delta_" + n], out["new_m_" + n], out["new_v_" + n] = g, d, mn, vn

    rows = [gains[l][n] for n in NORM_NAMES for l in range(DEPTH)] + [dg_final, loss_blk]
    total = _allreduce_rows(_pick_row(rows))
    out["loss"] = total[7, 0]
    wn = jnp.concatenate([given[n] for n in NORM_NAMES] + [norm_final[None], jnp.zeros((1, D_MODEL), F32)])
    mn_ = jnp.concatenate([given["m_" + n] for n in NORM_NAMES] + [m_norm_final[None], jnp.zeros((1, D_MODEL), F32)])
    vn_ = jnp.concatenate([given["v_" + n] for n in NORM_NAMES] + [v_norm_final[None], jnp.ones((1, D_MODEL), F32)])
    d_n, m_n, v_n = _adamw_rows(wn, mn_, vn_, total)
    for i, n in enumerate(NORM_NAMES):
        sl = slice(i * DEPTH, (i + 1) * DEPTH)
        out["grad_" + n], out["delta_" + n], out["new_m_" + n], out["new_v_" + n] = total[sl], d_n[sl], m_n[sl], v_n[sl]
    out["grad_norm_final"], out["delta_norm_final"] = total[6], d_n[6]
    out["new_m_norm_final"], out["new_v_norm_final"] = m_n[6], v_n[6]

    order = WEIGHT_NAMES
    del order
    names = ["norm_ffn1", "ffn1_w_gate", "ffn1_w_up", "ffn1_w_down", "norm_mix", "w_in", "w_proj_dil", "w_proj_sb",
             "w_out", "norm_ffn2", "ffn2_w_gate", "ffn2_w_up", "ffn2_w_down", "norm_final"]
    return (out["loss"], out["grad_x"], *[out["grad_" + n] for n in names], *[out["delta_" + n] for n in names],
            *[out["new_m_" + n] for n in names], *[out["new_v_" + n] for n in names])
```

```python
import functools

import jax
import jax.numpy as jnp
from jax import lax
from jax.experimental import pallas as pl
from jax.experimental.pallas import tpu as pltpu

F32 = jnp.float32
BF16 = jnp.bfloat16

D_MODEL = 1024
DEPTH = 2
N_CHIPS = 4
HEAD_DIM = 64
ROPE_DIM = 16
ROPE_THETA = 500000.0
DIL_GROUPS = ((128, 1), (512, 4), (2048, 16))
SPAN = 128
Q_BLOCK = 128
RMS_EPS = 1e-6
D_ATT = 256
COL_QS = 2304
COL_GD = 3072
COL_GS = 4096
ADAM_LR, ADAM_B1, ADAM_B2, ADAM_EPS, ADAM_WD, ADAM_STEP = 0.001, 0.9, 0.999, 1e-08, 0.01, 10

VMEM_LIMIT = 52 * 1024 * 1024
TM = 512
NEG = -1e30

NN = (((1,), (0,)), ((), ()))
NT = (((1,), (1,)), ((), ()))
TN = (((0,), (0,)), ((), ()))
MESH = pl.DeviceIdType.MESH

WEIGHT_NAMES = ("ffn1_w_gate", "ffn1_w_up", "ffn1_w_down", "w_in", "w_proj_dil",
                "w_proj_sb", "w_out", "ffn2_w_gate", "ffn2_w_up", "ffn2_w_down")
NORM_NAMES = ("norm_ffn1", "norm_mix", "norm_ffn2")


def _params(**kw):
    return pltpu.CompilerParams(vmem_limit_bytes=VMEM_LIMIT, **kw)


def _sigmoid(x):
    return 0.5 * jnp.tanh(0.5 * x) + 0.5


def _mm_body(pairs, n_in, n_out, n_acc, dims, nk, i_axis, epilogue, *refs):
    ins = refs[:n_in]
    outs = refs[n_in:n_in + n_out]
    accs = refs[n_in + n_out:]
    i = pl.program_id(i_axis)
    k = pl.program_id(2)

    def operand(a):
        return (a(ins) if callable(a) else ins[a][...]).astype(BF16)

    def dot(ia, ib):
        return lax.dot_general(operand(ia), operand(ib), dims, preferred_element_type=F32)

    if nk == 1:
        parts = [None] * n_acc
        for ia, ib, ic in pairs:
            parts[ic] = dot(ia, ib) if parts[ic] is None else parts[ic] + dot(ia, ib)
        epilogue(parts, ins, outs, i)
        return

    @pl.when(k == 0)
    def _():
        for c in range(n_acc):
            accs[c][...] = jnp.zeros_like(accs[c])

    for ia, ib, ic in pairs:
        accs[ic][...] += dot(ia, ib)

    @pl.when(k == nk - 1)
    def _():
        epilogue([a[...] for a in accs], ins, outs, i)


def _j_outer(spec):
    f = spec.index_map
    return pl.BlockSpec(spec.block_shape, lambda j, i, k: f(i, j, k))


def _mm(name, ins, in_specs, pairs, n_acc, acc_shape, dims, grid, epilogue, out_shapes, out_specs, j_outer=False):
    nk = grid[2]
    if j_outer:
        grid = (grid[1], grid[0], grid[2])
        in_specs = [_j_outer(s) for s in in_specs]
        out_specs = [_j_outer(s) for s in out_specs]
    scratch = [pltpu.VMEM(acc_shape, F32) for _ in range(n_acc)] if nk > 1 else []
    body = functools.partial(_mm_body, tuple(pairs), len(ins), len(out_shapes), n_acc, dims, nk,
                             1 if j_outer else 0, epilogue)
    return pl.pallas_call(
        body, name=name, grid=grid, in_specs=in_specs, out_specs=out_specs, out_shape=out_shapes,
        scratch_shapes=scratch,
        compiler_params=_params(dimension_semantics=("arbitrary", "arbitrary", "arbitrary")),
    )(*ins)


def _wspec(r, c, l, by):
    if by == 1:
        return pl.BlockSpec((None, None, r, c), lambda i, j, k: (j, l, 0, 0))
    return pl.BlockSpec((None, None, r, c), lambda i, j, k: (k, l, 0, 0))


def _rms_bwd_epilogue(x_idx, g_idx, dxo_idx):
    def ep(vals, ins, outs, i):
        dh = vals[0]
        x = ins[x_idx][...]
        g = ins[g_idx][...]
        rstd = lax.rsqrt(jnp.mean(x * x, axis=-1, keepdims=True) + RMS_EPS)
        xhat = x * rstd
        dxhat = dh * g
        dx = rstd * (dxhat - xhat * jnp.mean(dxhat * xhat, axis=-1, keepdims=True))
        outs[0][...] = ins[dxo_idx][...] + dx
        dg = jnp.broadcast_to(jnp.sum(dh * xhat, axis=0, keepdims=True), outs[1].shape)

        @pl.when(i == 0)
        def _():
            outs[1][...] = dg

        @pl.when(i > 0)
        def _():
            outs[1][...] += dg
    return ep


def _rms_fwd(x, gain):
    T = x.shape[0]

    def body(x_ref, g_ref, h_ref):
        xv = x_ref[...]
        h = xv * lax.rsqrt(jnp.mean(xv * xv, axis=-1, keepdims=True) + RMS_EPS)
        h_ref[...] = (h * g_ref[...]).astype(BF16)

    return pl.pallas_call(
        body, name="rms_fwd", grid=(T // TM,),
        in_specs=[pl.BlockSpec((TM, D_MODEL), lambda i: (i, 0)), pl.BlockSpec((1, D_MODEL), lambda i: (0, 0))],
        out_specs=pl.BlockSpec((TM, D_MODEL), lambda i: (i, 0)),
        out_shape=jax.ShapeDtypeStruct((T, D_MODEL), BF16), compiler_params=_params(),
    )(x, gain)


def _rope_tables(T):
    pos = jnp.arange(T, dtype=F32)
    inv_freq = ROPE_THETA ** (-jnp.arange(0, ROPE_DIM, 2, dtype=F32) / ROPE_DIM)
    ang = pos[:, None] * inv_freq[None, :]
    cos, sin = jnp.cos(ang), jnp.sin(ang)
    half = ROPE_DIM // 2
    one = jnp.ones((T, HEAD_DIM - ROPE_DIM), F32)
    zero = jnp.zeros((T, HEAD_DIM - ROPE_DIM), F32)
    zh = jnp.zeros((T, half), F32)
    c = jnp.concatenate([cos, cos, one], axis=1)
    s1 = jnp.concatenate([-sin, zh, zero], axis=1)
    s2 = jnp.concatenate([zh, sin, zero], axis=1)
    return tuple(jnp.concatenate([t, t], axis=1) for t in (c, s1, s2))


def _rope_fwd(xv, c, s1, s2):
    w = xv.shape[1]
    half = ROPE_DIM // 2
    return xv * c + pltpu.roll(xv, w - half, 1) * s1 + pltpu.roll(xv, half, 1) * s2


def _rope_bwd(dy, c, s1, s2):
    w = dy.shape[1]
    half = ROPE_DIM // 2
    return dy * c + pltpu.roll(dy * s1, half, 1) + pltpu.roll(dy * s2, w - half, 1)


def _assemble_dproj(dqk, rest, gates, tabs):
    T = gates[0].shape[0]
    n_qk, n_rest = len(dqk), len(rest)
    width = (n_qk + n_rest) * D_ATT + 2 * D_MODEL

    def body(*refs):
        ins, (c_ref, s1_ref, s2_ref), o_ref = refs[:n_qk + n_rest + 2], refs[-4:-1], refs[-1]
        c = jnp.concatenate([c_ref[...]] * 2, axis=1)
        s1 = jnp.concatenate([s1_ref[...]] * 2, axis=1)
        s2 = jnp.concatenate([s2_ref[...]] * 2, axis=1)
        for b in range(n_qk + n_rest):
            v = ins[b][...]
            if b < n_qk:
                v = _rope_bwd(v, c, s1, s2)
            o_ref[:, b * D_ATT:(b + 1) * D_ATT] = v.astype(BF16)
        off = (n_qk + n_rest) * D_ATT
        o_ref[:, off:off + D_MODEL] = ins[-2][...]
        o_ref[:, off + D_MODEL:] = ins[-1][...]

    att = pl.BlockSpec((TM, D_ATT), lambda i: (i, 0))
    wide = pl.BlockSpec((TM, D_MODEL), lambda i: (i, 0))
    tab = pl.BlockSpec((TM, 128), lambda i: (i, 0))
    return pl.pallas_call(
        body, name="assemble_dproj", grid=(T // TM,),
        in_specs=[att] * (n_qk + n_rest) + [wide, wide, tab, tab, tab],
        out_specs=pl.BlockSpec((TM, width), lambda i: (i, 0)),
        out_shape=jax.ShapeDtypeStruct((T, width), BF16), compiler_params=_params(),
    )(*dqk, *rest, *gates, *tabs)


def _dil_merge(os_, lses):
    T = os_[0].shape[0]

    def body(o0, o1, o2, l0, l1, l2, o_ref, lse_ref):
        a, b, c = l0[...], l1[...], l2[...]
        m = jnp.maximum(jnp.maximum(a, b), c)
        ea, eb, ec = jnp.exp(a - m), jnp.exp(b - m), jnp.exp(c - m)
        den = ea + eb + ec
        o_ref[...] = (ea * o0[...] + eb * o1[...] + ec * o2[...]) / den
        lse_ref[...] = m + jnp.log(den)

    blk = pl.BlockSpec((TM, D_ATT), lambda i: (i, 0))
    sh = jax.ShapeDtypeStruct((T, D_ATT), F32)
    return pl.pallas_call(
        body, name="dil_merge", grid=(T // TM,), in_specs=[blk] * 6, out_specs=[blk, blk],
        out_shape=[sh, sh], compiler_params=_params(),
    )(*os_, *lses)


def _final_loss(x, gain, target):
    T = x.shape[0]

    def body(x_ref, g_ref, t_ref, dx_ref, dg_ref, loss_ref):
        xv = x_ref[...]
        g = g_ref[...]
        rstd = lax.rsqrt(jnp.mean(xv * xv, axis=-1, keepdims=True) + RMS_EPS)
        xhat = xv * rstd
        err = xhat * g - t_ref[...]
        loss = 0.5 * jnp.sum(jnp.mean(err * err, axis=-1, keepdims=True), axis=0, keepdims=True)
        dy = err * (1.0 / D_MODEL)
        dxhat = dy * g
        dx_ref[...] = rstd * (dxhat - xhat * jnp.mean(dxhat * xhat, axis=-1, keepdims=True))
        dg = jnp.broadcast_to(jnp.sum(dy * xhat, axis=0, keepdims=True), dg_ref.shape)
        ls = jnp.broadcast_to(loss, loss_ref.shape)

        @pl.when(pl.program_id(0) == 0)
        def _():
            dg_ref[...] = dg
            loss_ref[...] = ls

        @pl.when(pl.program_id(0) > 0)
        def _():
            dg_ref[...] += dg
            loss_ref[...] += ls

    blk = pl.BlockSpec((TM, D_MODEL), lambda i: (i, 0))
    row = pl.BlockSpec((1, D_MODEL), lambda i: (0, 0))
    acc = pl.BlockSpec((8, D_MODEL), lambda i: (0, 0))
    return pl.pallas_call(
        body, name="final_loss", grid=(T // TM,), in_specs=[blk, row, blk], out_specs=[blk, acc, acc],
        out_shape=[jax.ShapeDtypeStruct((T, D_MODEL), F32), jax.ShapeDtypeStruct((8, D_MODEL), F32),
                   jax.ShapeDtypeStruct((8, D_MODEL), F32)],
        compiler_params=_params(dimension_semantics=("arbitrary",)),
    )(x, gain, target)


def _head_masks():
    lane = lax.broadcasted_iota(jnp.int32, (SPAN, 128), 1)
    return lane < HEAD_DIM


def _dil_rows(idx, d):
    u = idx // d
    r = idx - u * d
    own = pl.ds(u * (SPAN * d) + r, SPAN, stride=d) if d > 1 else pl.ds(pl.multiple_of(u * SPAN, SPAN), SPAN)
    up = jnp.maximum(u - 1, 0)
    prev = pl.ds(up * (SPAN * d) + r, SPAN, stride=d) if d > 1 else pl.ds(pl.multiple_of(up * SPAN, SPAN), SPAN)
    return u, own, prev


def _dil_scores(qm, k_own, k_prev, u):
    qi = lax.broadcasted_iota(jnp.int32, (SPAN, SPAN), 0)
    kj = lax.broadcasted_iota(jnp.int32, (SPAN, SPAN), 1)
    s_own = lax.dot_general(qm, k_own, NT, preferred_element_type=F32) * (HEAD_DIM ** -0.5)
    s_prev = lax.dot_general(qm, k_prev, NT, preferred_element_type=F32) * (HEAD_DIM ** -0.5)
    ok_own = kj <= qi
    ok_prev = kj >= qi + jnp.where(u > 0, 0, SPAN)
    return s_own, s_prev, ok_own, ok_prev


def _dil_fwd(proj, g, d):
    T = proj.shape[0]
    n_iter = T // SPAN

    def body(q_ref, k_ref, v_ref, o_ref, lse_ref):
        first = _head_masks()

        def step(idx, carry):
            u, own, prev = _dil_rows(idx, d)
            q = q_ref[own, :]
            k_own = k_ref[own, :].astype(BF16)
            k_prev = k_ref[prev, :].astype(BF16)
            v_own = v_ref[own, :].astype(BF16)
            v_prev = v_ref[prev, :].astype(BF16)
            o_h, lse_h = [], []
            for h in range(2):
                qm = jnp.where(first if h == 0 else ~first, q, 0.0).astype(BF16)
                s_own, s_prev, ok_own, ok_prev = _dil_scores(qm, k_own, k_prev, u)
                s_own = jnp.where(ok_own, s_own, NEG)
                s_prev = jnp.where(ok_prev, s_prev, NEG)
                m = jnp.maximum(jnp.max(s_own, axis=1, keepdims=True), jnp.max(s_prev, axis=1, keepdims=True))
                p_own = jnp.exp(s_own - m)
                p_prev = jnp.exp(s_prev - m)
                den = jnp.sum(p_own, axis=1, keepdims=True) + jnp.sum(p_prev, axis=1, keepdims=True)
                pv = (lax.dot_general(p_own.astype(BF16), v_own, NN, preferred_element_type=F32)
                      + lax.dot_general(p_prev.astype(BF16), v_prev, NN, preferred_element_type=F32))
                o_h.append(pv / den)
                lse_h.append(jnp.broadcast_to(m + jnp.log(den), (SPAN, 128)))
            o_ref[own, :] = jnp.where(first, o_h[0], o_h[1])
            lse_ref[own, :] = jnp.where(first, lse_h[0], lse_h[1])
            return carry

        lax.fori_loop(0, n_iter, step, 0)

    def col(b):
        return pl.BlockSpec((T, 128), lambda p: (0, b + p))

    sh = jax.ShapeDtypeStruct((T, D_ATT), F32)
    out = pl.BlockSpec((T, 128), lambda p: (0, p))
    return pl.pallas_call(
        body, name=f"dil_fwd_d{d}", grid=(2,),
        in_specs=[col(2 * g), col(6 + 2 * g), col(12 + 2 * g)], out_specs=[out, out], out_shape=[sh, sh],
        compiler_params=_params(dimension_semantics=("arbitrary",)),
    )(proj, proj, proj)


def _dil_bwd(proj, do, o_dil, lse, g, d):
    T = proj.shape[0]
    n_iter = T // SPAN

    def body(q_ref, k_ref, v_ref, do_ref, o_ref, lse_ref, dq_ref, dk_ref, dv_ref):
        first = _head_masks()

        def step(idx, carry):
            u, own, prev = _dil_rows(idx, d)
            q = q_ref[own, :]
            k_own = k_ref[own, :].astype(BF16)
            k_prev = k_ref[prev, :].astype(BF16)
            v_own = v_ref[own, :].astype(BF16)
            v_prev = v_ref[prev, :].astype(BF16)
            do_v = do_ref[own, :]
            oo = o_ref[own, :]
            ls = lse_ref[own, :]
            dq_h = []
            dk_own = dk_prev = dv_own = dv_prev = None
            for h in range(2):
                hm = first if h == 0 else ~first
                qm = jnp.where(hm, q, 0.0).astype(BF16)
                dom = jnp.where(hm, do_v, 0.0)
                dob = dom.astype(BF16)
                delta = jnp.sum(dom * oo, axis=1, keepdims=True)
                lrow = jnp.max(jnp.where(hm, ls, NEG), axis=1, keepdims=True)
                s_own, s_prev, ok_own, ok_prev = _dil_scores(qm, k_own, k_prev, u)
                p_own = jnp.where(ok_own, jnp.exp(s_own - lrow), 0.0)
                p_prev = jnp.where(ok_prev, jnp.exp(s_prev - lrow), 0.0)
                dp_own = lax.dot_general(dob, v_own, NT, preferred_element_type=F32)
                dp_prev = lax.dot_general(dob, v_prev, NT, preferred_element_type=F32)
                ds_own = (p_own * (dp_own - delta) * (HEAD_DIM ** -0.5)).astype(BF16)
                ds_prev = (p_prev * (dp_prev - delta) * (HEAD_DIM ** -0.5)).astype(BF16)
                dq_h.append(lax.dot_general(ds_own, k_own, NN, preferred_element_type=F32)
                            + lax.dot_general(ds_prev, k_prev, NN, preferred_element_type=F32))
                a = lax.dot_general(ds_own, qm, TN, preferred_element_type=F32)
                b = lax.dot_general(ds_prev, qm, TN, preferred_element_type=F32)
                c = lax.dot_general(p_own.astype(BF16), dob, TN, preferred_element_type=F32)
                e = lax.dot_general(p_prev.astype(BF16), dob, TN, preferred_element_type=F32)
                dk_own = a if dk_own is None else dk_own + a
                dk_prev = b if dk_prev is None else dk_prev + b
                dv_own = c if dv_own is None else dv_own + c
                dv_prev = e if dv_prev is None else dv_prev + e
            dq_ref[own, :] = jnp.where(first, dq_h[0], dq_h[1])
            dk_ref[own, :] = dk_own
            dv_ref[own, :] = dv_own
            dk_ref[prev, :] = dk_ref[prev, :] + dk_prev
            dv_ref[prev, :] = dv_ref[prev, :] + dv_prev
            return carry

        lax.fori_loop(0, n_iter, step, 0)

    def col(b):
        return pl.BlockSpec((T, 128), lambda p: (0, b + p))

    sh = jax.ShapeDtypeStruct((T, D_ATT), F32)
    return pl.pallas_call(
        body, name=f"dil_bwd_d{d}", grid=(2,),
        in_specs=[col(2 * g), col(6 + 2 * g), col(12 + 2 * g), col(0), col(0), col(0)],
        out_specs=[col(0), col(0), col(0)], out_shape=[sh, sh, sh],
        compiler_params=_params(dimension_semantics=("arbitrary",)),
    )(proj, proj, proj, do, o_dil, lse)


SB_KT = 512


def _sb_tri(strict):
    a = lax.broadcasted_iota(jnp.int32, (Q_BLOCK, Q_BLOCK), 0)
    b = lax.broadcasted_iota(jnp.int32, (Q_BLOCK, Q_BLOCK), 1)
    return jnp.where((a > b) if strict else (a >= b), 1.0, 0.0).astype(BF16)


def _suffix(x, c, tri):
    nb = x.shape[1] // Q_BLOCK
    blocks = [x[:, Q_BLOCK * b:Q_BLOCK * (b + 1)] for b in range(nb)]
    hi = [b.astype(BF16) for b in blocks]
    lo = [(b - h.astype(F32)).astype(BF16) for b, h in zip(blocks, hi)]
    y = lax.dot_general(jnp.concatenate(hi + lo, axis=0), tri, NN, preferred_element_type=F32)
    outs = [None] * nb
    run = c
    for b in reversed(range(nb)):
        outs[b] = run + y[Q_BLOCK * b:Q_BLOCK * (b + 1)] + y[Q_BLOCK * (nb + b):Q_BLOCK * (nb + b + 1)]
        run = run + jnp.sum(blocks[b], axis=1, keepdims=True)
    return jnp.concatenate(outs, axis=1), run


def _sb_tile(qm, kb, past, c, tri):
    z = lax.dot_general(qm, kb, NT, preferred_element_type=F32) * (HEAD_DIM ** -0.5)
    lsz = jnp.minimum(z, 0.0) - jnp.log1p(jnp.exp(-jnp.abs(z)))
    lk = jnp.where(past, lsz - z, 0.0)
    after, c_new = _suffix(lk, c, tri)
    w = jnp.where(past, jnp.exp(lsz + after), 0.0)
    return z, lsz, w, c_new


def _sb_past(i, t):
    row = lax.broadcasted_iota(jnp.int32, (Q_BLOCK, SB_KT), 0)
    col = lax.broadcasted_iota(jnp.int32, (Q_BLOCK, SB_KT), 1)
    return col + t * SB_KT < row + i * Q_BLOCK


def _sb_fwd(proj):
    T = proj.shape[0]

    def body(q_ref, k_ref, v_ref, o_ref):
        i = pl.program_id(1)
        first = _head_masks()
        tri = _sb_tri(True)
        q = q_ref[...]
        qms = [jnp.where(first, q, 0.0).astype(BF16), jnp.where(first, 0.0, q).astype(BF16)]
        n_tiles = (i * Q_BLOCK) // SB_KT + 1

        def step(tt, carry):
            t = n_tiles - 1 - tt
            rows = pl.ds(pl.multiple_of(t * SB_KT, SB_KT), SB_KT)
            kb = k_ref[rows, :].astype(BF16)
            vb = v_ref[rows, :].astype(BF16)
            past = _sb_past(i, t)
            out = []
            for h in range(2):
                acc, c = carry[2 * h], carry[2 * h + 1]
                _, _, w, c = _sb_tile(qms[h], kb, past, c, tri)
                out += [acc + lax.dot_general(w.astype(BF16), vb, NN, preferred_element_type=F32), c]
            return tuple(out)

        zero, zcol = jnp.zeros((Q_BLOCK, 128), F32), jnp.zeros((Q_BLOCK, 1), F32)
        res = lax.fori_loop(0, n_tiles, step, (zero, zcol, zero, zcol))
        o_ref[...] = jnp.where(first, res[0], res[2])

    cb = COL_QS // 128
    return pl.pallas_call(
        body, name="sb_fwd", grid=(2, T // Q_BLOCK),
        in_specs=[pl.BlockSpec((Q_BLOCK, 128), lambda p, i: (i, cb + p)),
                  pl.BlockSpec((T, 128), lambda p, i: (0, cb + 2 + p)),
                  pl.BlockSpec((T, 128), lambda p, i: (0, cb + 4 + p))],
        out_specs=pl.BlockSpec((Q_BLOCK, 128), lambda p, i: (i, p)),
        out_shape=jax.ShapeDtypeStruct((T, D_ATT), F32),
        compiler_params=_params(dimension_semantics=("arbitrary", "arbitrary")),
    )(proj, proj, proj)


def _sb_bwd(proj, do, o):
    T = proj.shape[0]

    def body(q_ref, k_ref, v_ref, do_ref, o_ref, dq_ref, dk_ref, dv_ref):
        i = pl.program_id(1)
        first = _head_masks()
        tri = _sb_tri(True)
        tri_incl = _sb_tri(False)

        @pl.when(i == 0)
        def _():
            dk_ref[...] = jnp.zeros_like(dk_ref)
            dv_ref[...] = jnp.zeros_like(dv_ref)

        q = q_ref[...]
        do_v = do_ref[...]
        oo = o_ref[...]
        qms = [jnp.where(first, q, 0.0).astype(BF16), jnp.where(first, 0.0, q).astype(BF16)]
        dobs = [jnp.where(first, do_v, 0.0).astype(BF16), jnp.where(first, 0.0, do_v).astype(BF16)]
        deltas = [jnp.sum(d.astype(F32) * oo, axis=1, keepdims=True) for d in dobs]
        n_tiles = (i * Q_BLOCK) // SB_KT + 1

        def step(tt, carry):
            t = n_tiles - 1 - tt
            rows = pl.ds(pl.multiple_of(t * SB_KT, SB_KT), SB_KT)
            kb = k_ref[rows, :].astype(BF16)
            vb = v_ref[rows, :].astype(BF16)
            past = _sb_past(i, t)
            out = []
            dk_t = dv_t = None
            for h in range(2):
                dq, c, ce = carry[3 * h:3 * h + 3]
                z, lsz, w, c = _sb_tile(qms[h], kb, past, c, tri)
                gv = lax.dot_general(dobs[h], vb, NT, preferred_element_type=F32)
                wb = w.astype(BF16)
                e = wb.astype(F32) * gv
                suf, ce = _suffix(e, ce, tri_incl)
                big_e = deltas[h] - suf
                dz = jnp.where(past, e * jnp.exp(lsz - z) - big_e * jnp.exp(lsz), 0.0) * (HEAD_DIM ** -0.5)
                dzb = dz.astype(BF16)
                dq = dq + lax.dot_general(dzb, kb, NN, preferred_element_type=F32)
                a = lax.dot_general(dzb, qms[h], TN, preferred_element_type=F32)
                b = lax.dot_general(wb, dobs[h], TN, preferred_element_type=F32)
                dk_t = a if dk_t is None else dk_t + a
                dv_t = b if dv_t is None else dv_t + b
                out += [dq, c, ce]
            dk_ref[rows, :] = dk_ref[rows, :] + dk_t
            dv_ref[rows, :] = dv_ref[rows, :] + dv_t
            return tuple(out)

        zero, zcol = jnp.zeros((Q_BLOCK, 128), F32), jnp.zeros((Q_BLOCK, 1), F32)
        res = lax.fori_loop(0, n_tiles, step, (zero, zcol, zcol, zero, zcol, zcol))
        dq_ref[...] = jnp.where(first, res[0], res[3])

    cb = COL_QS // 128
    blk = pl.BlockSpec((Q_BLOCK, 128), lambda p, i: (i, p))
    full = pl.BlockSpec((T, 128), lambda p, i: (0, p))
    sh = jax.ShapeDtypeStruct((T, D_ATT), F32)
    return pl.pallas_call(
        body, name="sb_bwd", grid=(2, T // Q_BLOCK),
        in_specs=[pl.BlockSpec((Q_BLOCK, 128), lambda p, i: (i, cb + p)),
                  pl.BlockSpec((T, 128), lambda p, i: (0, cb + 2 + p)),
                  pl.BlockSpec((T, 128), lambda p, i: (0, cb + 4 + p)), blk, blk],
        out_specs=[blk, full, full], out_shape=[sh, sh, sh],
        compiler_params=_params(dimension_semantics=("arbitrary", "arbitrary")),
    )(proj, proj, proj, do, o)


def _tok(c, by=None):
    if by is None:
        return pl.BlockSpec((TM, c), lambda i, j, k: (i, 0))
    if by == 1:
        return pl.BlockSpec((TM, c), lambda i, j, k: (i, j))
    return pl.BlockSpec((TM, c), lambda i, j, k: (i, k))


def _chunked(c, by):
    if by == 1:
        return pl.BlockSpec((None, TM, c), lambda i, j, k: (j, i, 0))
    return pl.BlockSpec((None, TM, c), lambda i, j, k: (k, i, 0))


def _gain_spec():
    return pl.BlockSpec((1, D_MODEL), lambda i, j, k: (0, 0))


def _all_chunks(rows, c):
    return pl.BlockSpec((N_CHIPS, rows, c), lambda i, j, k: (0, i, 0))


def _wfull(r, c, l):
    return pl.BlockSpec((N_CHIPS, None, r, c), lambda i, j, k: (0, l, 0, 0))


def _pick(idx, c):
    return lambda ins: ins[idx][c]


def _cols(idx, c, w):
    return lambda ins: ins[idx][:, c * w:(c + 1) * w]


def _ffn_fwd(x, gain, wg, wu, wd, l):
    T = x.shape[0]
    ffs = wg.shape[3]
    h = _rms_fwd(x, gain)

    def swiglu(vals, ins, outs, i):
        gt, up = vals
        outs[0][...] = gt.astype(BF16)
        outs[1][...] = up.astype(BF16)
        outs[2][...] = (gt * _sigmoid(gt) * up).astype(BF16)

    csh = jax.ShapeDtypeStruct((N_CHIPS, T, ffs), BF16)
    gate, up, act = _mm(
        "ffn_up", [h, wg, wu], [_tok(D_MODEL), _wspec(D_MODEL, ffs, l, 1), _wspec(D_MODEL, ffs, l, 1)],
        [(0, 1, 0), (0, 2, 1)], 2, None, NN, (T // TM, N_CHIPS, 1), swiglu,
        [csh, csh, csh], [_chunked(ffs, 1)] * 3, j_outer=True)

    def resid(vals, ins, outs, i):
        outs[0][...] = ins[2][...] + 0.5 * vals[0]

    (y,) = _mm(
        "ffn_down", [act, wd, x], [_all_chunks(TM, ffs), _wfull(ffs, D_MODEL, l), _tok(D_MODEL)],
        [(_pick(0, c), _pick(1, c), 0) for c in range(N_CHIPS)], 1, None, NN, (T // TM, 1, 1), resid,
        [jax.ShapeDtypeStruct((T, D_MODEL), F32)], [_tok(D_MODEL)])
    return y, (x, h, gate, up, act)


def _ffn_bwd(dxo, gain, wg, wu, wd, l, saved):
    x, h, gate, up, act = saved
    T = x.shape[0]
    ffs = wg.shape[3]
    tk = TM
    tm = TM // 2

    def dswiglu(vals, ins, outs, i):
        for c in range(N_CHIPS):
            da = 0.5 * vals[c]
            gt = ins[2][c].astype(F32)
            u = ins[3][c].astype(F32)
            s = _sigmoid(gt)
            outs[0][c] = (da * u * (s * (1.0 + gt * (1.0 - s)))).astype(BF16)
            outs[1][c] = (da * (gt * s)).astype(BF16)

    csh = jax.ShapeDtypeStruct((N_CHIPS, T, ffs), BF16)
    row = pl.BlockSpec((tm, D_MODEL), lambda i, j, k: (i, 0))
    dgate, dup = _mm(
        "ffn_dact", [dxo, wd, gate, up],
        [row, _wfull(ffs, D_MODEL, l), _all_chunks(tm, ffs), _all_chunks(tm, ffs)],
        [(0, _pick(1, c), c) for c in range(N_CHIPS)], N_CHIPS, None, NT, (T // tm, 1, 1), dswiglu,
        [csh, csh], [_all_chunks(tm, ffs)] * 2)

    def halves(vals, ins, outs, i):
        for c in range(N_CHIPS):
            outs[0][c] = (0.5 * vals[c]).astype(BF16)

    def casts(vals, ins, outs, i):
        for c in range(N_CHIPS):
            outs[0][c] = vals[c].astype(BF16)

    tok_k = pl.BlockSpec((tk, D_MODEL), lambda i, j, k: (k, 0))
    chunks_k = pl.BlockSpec((N_CHIPS, tk, ffs), lambda i, j, k: (0, k, 0))
    (dwd,) = _mm(
        "ffn_dwd", [act, dxo], [chunks_k, tok_k], [(_pick(0, c), 1, c) for c in range(N_CHIPS)], N_CHIPS,
        (ffs, D_MODEL), TN, (1, 1, T // tk), halves, [jax.ShapeDtypeStruct((N_CHIPS, ffs, D_MODEL), BF16)],
        [pl.BlockSpec((N_CHIPS, ffs, D_MODEL), lambda i, j, k: (0, 0, 0))])

    dx, dgain = _mm(
        "ffn_dx", [dgate, dup, wg, wu, x, gain, dxo],
        [_all_chunks(tm, ffs), _all_chunks(tm, ffs), _wfull(D_MODEL, ffs, l), _wfull(D_MODEL, ffs, l),
         row, _gain_spec(), row],
        [(_pick(a, c), _pick(a + 2, c), 0) for c in range(N_CHIPS) for a in range(2)], 1, None, NT,
        (T // tm, 1, 1), _rms_bwd_epilogue(4, 5, 6),
        [jax.ShapeDtypeStruct((T, D_MODEL), F32), jax.ShapeDtypeStruct((8, D_MODEL), F32)],
        [row, pl.BlockSpec((8, D_MODEL), lambda i, j, k: (0, 0))])

    wsh = jax.ShapeDtypeStruct((N_CHIPS, D_MODEL, ffs), BF16)
    wout = pl.BlockSpec((N_CHIPS, D_MODEL, ffs), lambda i, j, k: (0, 0, 0))
    dws = []
    for dact in (dgate, dup):
        dws += _mm("ffn_dwgu", [h, dact], [tok_k, chunks_k], [(0, _pick(1, c), c) for c in range(N_CHIPS)],
                   N_CHIPS, (D_MODEL, ffs), TN, (1, 1, T // tk), casts, [wsh], [wout])
    return dx, dgain, dws[0], dws[1], dwd


def _mixer_fwd(x, gain, W, l, tabs):
    T = x.shape[0]
    win, wpd, wps, wo = W["w_in"], W["w_proj_dil"], W["w_proj_sb"], W["w_out"]
    cin = win.shape[3]
    cp = wpd.shape[3]
    h = _rms_fwd(x, gain)

    n_rope = 6 * D_ATT

    def roped(vals, ins, outs, i):
        v = vals[0]
        col0 = pl.program_id(0) * cin

        @pl.when(col0 < n_rope)
        def _():
            on = lax.broadcasted_iota(jnp.int32, v.shape, 1) + col0 < n_rope
            c = jnp.where(on, jnp.concatenate([ins[2][...]] * (cin // 128), axis=1), 1.0)
            s1 = jnp.where(on, jnp.concatenate([ins[3][...]] * (cin // 128), axis=1), 0.0)
            s2 = jnp.where(on, jnp.concatenate([ins[4][...]] * (cin // 128), axis=1), 0.0)
            outs[0][...] = _rope_fwd(v, c, s1, s2)

        @pl.when(col0 >= n_rope)
        def _():
            outs[0][...] = v

    (proj,) = _mm(
        "mix_in", [h, win, *tabs], [_tok(D_MODEL), _wspec(D_MODEL, cin, l, 1)] + [_tok(128)] * 3, [(0, 1, 0)], 1,
        None, NN, (T // TM, N_CHIPS, 1), roped, [jax.ShapeDtypeStruct((T, N_CHIPS * cin), F32)], [_tok(cin, 1)],
        j_outer=True)

    os_, lses = [], []
    for g, (window, dil) in enumerate(DIL_GROUPS):
        o_g, lse_g = _dil_fwd(proj, g, dil)
        os_.append(o_g)
        lses.append(lse_g)
    o_dil, lse = _dil_merge(os_, lses)
    o_sb = _sb_fwd(proj)

    def gated(vals, ins, outs, i):
        pd, ps = vals
        outs[0][...] = (_sigmoid(ins[4][...]) * pd + _sigmoid(ins[5][...]) * ps).astype(BF16)
        outs[1][...] = pd.astype(BF16)
        outs[2][...] = ps.astype(BF16)

    gd0, gs0 = COL_GD // cp, COL_GS // cp
    ush = jax.ShapeDtypeStruct((T, D_MODEL), BF16)
    u, pd, ps = _mm(
        "mix_gate", [o_dil, o_sb, wpd, wps, proj, proj],
        [_tok(D_ATT), _tok(D_ATT), _wspec(D_ATT, cp, l, 1), _wspec(D_ATT, cp, l, 1),
         pl.BlockSpec((TM, cp), lambda i, j, k: (i, gd0 + j)), pl.BlockSpec((TM, cp), lambda i, j, k: (i, gs0 + j))],
        [(0, 2, 0), (1, 3, 1)], 2, None, NN, (T // TM, N_CHIPS, 1), gated, [ush] * 3, [_tok(cp, 1)] * 3)

    def resid(vals, ins, outs, i):
        outs[0][...] = ins[2][...] + vals[0]

    (y,) = _mm(
        "mix_out", [u, wo, x], [_tok(D_MODEL), _wfull(cp, D_MODEL, l), _tok(D_MODEL)],
        [(_cols(0, c, cp), _pick(1, c), 0) for c in range(N_CHIPS)], 1, None, NN, (T // TM, 1, 1), resid,
        [jax.ShapeDtypeStruct((T, D_MODEL), F32)], [_tok(D_MODEL)])
    return y, (x, h, proj, o_dil, lse, o_sb, u, pd, ps)


def _mixer_bwd(dxo, gain, W, l, tabs, saved):
    x, h, proj, o_dil, lse, o_sb, u, pd, ps = saved
    T = x.shape[0]
    win, wpd, wps, wo = W["w_in"], W["w_proj_dil"], W["w_proj_sb"], W["w_out"]
    cin = win.shape[3]
    cp = wpd.shape[3]
    tk = TM
    tm = TM // 2
    row = pl.BlockSpec((tm, D_MODEL), lambda i, j, k: (i, 0))
    gd0, gs0 = COL_GD // cp, COL_GS // cp

    def dgated(vals, ins, outs, i):
        du = vals[0]
        sd = _sigmoid(ins[4][...])
        ss = _sigmoid(ins[5][...])
        outs[0][...] = (du * sd).astype(BF16)
        outs[1][...] = (du * ss).astype(BF16)
        outs[2][...] = (du * ins[2][...].astype(F32) * sd * (1.0 - sd)).astype(BF16)
        outs[3][...] = (du * ins[3][...].astype(F32) * ss * (1.0 - ss)).astype(BF16)

    ush = jax.ShapeDtypeStruct((T, D_MODEL), BF16)
    dpd, dps, dgd, dgs = _mm(
        "mix_du", [dxo, wo, pd, ps, proj, proj],
        [_tok(D_MODEL), _wspec(cp, D_MODEL, l, 1), _tok(cp, 1), _tok(cp, 1),
         pl.BlockSpec((TM, cp), lambda i, j, k: (i, gd0 + j)), pl.BlockSpec((TM, cp), lambda i, j, k: (i, gs0 + j))],
        [(0, 1, 0)], 1, None, NT, (T // TM, N_CHIPS, 1), dgated, [ush] * 4, [_tok(cp, 1)] * 4)

    def one(vals, ins, outs, i):
        outs[0][...] = vals[0].astype(BF16)

    def two(vals, ins, outs, i):
        outs[0][...] = vals[0].astype(BF16)
        outs[1][...] = vals[1].astype(BF16)

    (dwo,) = _mm(
        "mix_dwo", [u, dxo],
        [pl.BlockSpec((tk, cp), lambda i, j, k: (k, j)), pl.BlockSpec((tk, D_MODEL), lambda i, j, k: (k, 0))],
        [(0, 1, 0)], 1, (cp, D_MODEL), TN, (1, N_CHIPS, T // tk), one,
        [jax.ShapeDtypeStruct((N_CHIPS, cp, D_MODEL), BF16)],
        [pl.BlockSpec((None, cp, D_MODEL), lambda i, j, k: (j, 0, 0))])

    def plain2(vals, ins, outs, i):
        outs[0][...] = vals[0]
        outs[1][...] = vals[1]

    ash = jax.ShapeDtypeStruct((T, D_ATT), F32)
    do_dil, do_sb = _mm(
        "mix_do", [dpd, dps, wpd, wps], [_tok(cp, 2), _tok(cp, 2), _wspec(D_ATT, cp, l, 2), _wspec(D_ATT, cp, l, 2)],
        [(0, 2, 0), (1, 3, 1)], 2, (TM, D_ATT), NT, (T // TM, 1, N_CHIPS), plain2, [ash, ash], [_tok(D_ATT)] * 2)

    psh = jax.ShapeDtypeStruct((N_CHIPS, D_ATT, cp), BF16)
    pspec = pl.BlockSpec((None, D_ATT, cp), lambda i, j, k: (j, 0, 0))
    arow = pl.BlockSpec((tk, D_ATT), lambda i, j, k: (k, 0))
    dcol = pl.BlockSpec((tk, cp), lambda i, j, k: (k, j))
    dwpd, dwps = _mm(
        "mix_dwp", [o_dil, o_sb, dpd, dps], [arow, arow, dcol, dcol], [(0, 2, 0), (1, 3, 1)], 2, (D_ATT, cp), TN,
        (1, N_CHIPS, T // tk), two, [psh, psh], [pspec, pspec])

    dqs, dks, dvs = [], [], []
    for g, (window, dil) in enumerate(DIL_GROUPS):
        dq, dk, dv = _dil_bwd(proj, do_dil, o_dil, lse, g, dil)
        dqs.append(dq)
        dks.append(dk)
        dvs.append(dv)
    dq_s, dk_s, dv_s = _sb_bwd(proj, do_sb, o_sb)
    dproj = _assemble_dproj(dqs + dks, dvs + [dq_s, dk_s, dv_s], [dgd, dgs], tabs)

    dx, dgain = _mm(
        "mix_dx", [dproj, win, x, gain, dxo],
        [pl.BlockSpec((tm, N_CHIPS * cin), lambda i, j, k: (i, 0)), _wfull(D_MODEL, cin, l), row, _gain_spec(), row],
        [(_cols(0, c, cin), _pick(1, c), 0) for c in range(N_CHIPS)], 1, None, NT, (T // tm, 1, 1),
        _rms_bwd_epilogue(2, 3, 4),
        [jax.ShapeDtypeStruct((T, D_MODEL), F32), jax.ShapeDtypeStruct((8, D_MODEL), F32)],
        [row, pl.BlockSpec((8, D_MODEL), lambda i, j, k: (0, 0))])

    (dwin,) = _mm(
        "mix_dwin", [h, dproj],
        [pl.BlockSpec((tk, D_MODEL), lambda i, j, k: (k, 0)), pl.BlockSpec((tk, cin), lambda i, j, k: (k, j))],
        [(0, 1, 0)], 1, (D_MODEL, cin), TN, (1, N_CHIPS, T // tk), one,
        [jax.ShapeDtypeStruct((N_CHIPS, D_MODEL, cin), BF16)],
        [pl.BlockSpec((None, D_MODEL, cin), lambda i, j, k: (j, 0, 0))])
    return dx, dgain, dwin, dwpd, dwps, dwo


def _local_step(x, target, norms, norm_final, W):
    T = x.shape[0]
    tabs = _rope_tables(T)
    saved = []
    for l in range(DEPTH):
        x, s1 = _ffn_fwd(x, norms["norm_ffn1"][l:l + 1], W["ffn1_w_gate"], W["ffn1_w_up"], W["ffn1_w_down"], l)
        x, s2 = _mixer_fwd(x, norms["norm_mix"][l:l + 1], W, l, tabs)
        x, s3 = _ffn_fwd(x, norms["norm_ffn2"][l:l + 1], W["ffn2_w_gate"], W["ffn2_w_up"], W["ffn2_w_down"], l)
        saved.append((s1, s2, s3))
    dx, dg_final, loss = _final_loss(x, norm_final.reshape(1, D_MODEL), target)
    grads = [None] * DEPTH
    gains = [None] * DEPTH
    for l in reversed(range(DEPTH)):
        s1, s2, s3 = saved[l]
        dx, dg2, dwg2, dwu2, dwd2 = _ffn_bwd(dx, norms["norm_ffn2"][l:l + 1], W["ffn2_w_gate"], W["ffn2_w_up"],
                                             W["ffn2_w_down"], l, s3)
        dx, dgm, dwin, dwpd, dwps, dwo = _mixer_bwd(dx, norms["norm_mix"][l:l + 1], W, l, tabs, s2)
        dx, dg1, dwg1, dwu1, dwd1 = _ffn_bwd(dx, norms["norm_ffn1"][l:l + 1], W["ffn1_w_gate"], W["ffn1_w_up"],
                                             W["ffn1_w_down"], l, s1)
        grads[l] = dict(ffn1_w_gate=dwg1, ffn1_w_up=dwu1, ffn1_w_down=dwd1, w_in=dwin, w_proj_dil=dwpd,
                        w_proj_sb=dwps, w_out=dwo, ffn2_w_gate=dwg2, ffn2_w_up=dwu2, ffn2_w_down=dwd2)
        gains[l] = dict(norm_ffn1=dg1, norm_mix=dgm, norm_ffn2=dg2)
    return loss, dx, grads, gains, dg_final


def _place():
    x, y, c = lax.axis_index("x"), lax.axis_index("y"), lax.axis_index("c")
    chips = [(1 - x, y), (x, 1 - y), (1 - x, 1 - y)]
    return x, y, c, chips


def _cast_into_slot(w, me_arr):
    L, r, cw = w.shape
    tr = r // 4 if r > 256 else r

    def body(me_ref, w_ref, o_ref):
        o_ref[...] = w_ref[...].astype(BF16)

    return pl.pallas_call(
        body, name="cast_weights",
        grid_spec=pltpu.PrefetchScalarGridSpec(
            num_scalar_prefetch=1, grid=(L, r // tr),
            in_specs=[pl.BlockSpec((None, tr, cw), lambda l, i, me: (l, i, 0))],
            out_specs=pl.BlockSpec((None, None, tr, cw), lambda l, i, me: (me[0], l, i, 0))),
        out_shape=jax.ShapeDtypeStruct((N_CHIPS, L, r, cw), BF16), compiler_params=_params(),
    )(me_arr, w)


def _gather_weights(bufs):
    n = len(bufs)

    def body(*refs):
        out_refs = refs[n:2 * n]
        send_sems, recv_sems, fsend_sems, frecv_sems = refs[2 * n:]
        x, y, c, chips = _place()
        me = 2 * x + y
        sibling = (x, y, 1 - c)

        def ici(a, j, chip_id, to):
            return pltpu.make_async_remote_copy(
                src_ref=out_refs[a].at[me, c], dst_ref=out_refs[a].at[chip_id, c],
                send_sem=send_sems.at[a, j], recv_sem=recv_sems.at[a, j], device_id=to, device_id_type=MESH)

        def d2d(a, j, chip_id, layer):
            return pltpu.make_async_remote_copy(
                src_ref=out_refs[a].at[chip_id, layer], dst_ref=out_refs[a].at[chip_id, layer],
                send_sem=fsend_sems.at[a, j], recv_sem=frecv_sems.at[a, j], device_id=sibling, device_id_type=MESH)

        sends = [ici(a, j, me, (*chip, c)) for a in range(n) for j, chip in enumerate(chips)]
        for cp in sends:
            cp.start()
        passed = []
        for a in range(n):
            for j, chip in enumerate(chips):
                cid = 2 * chip[0] + chip[1]
                ici(a, j, cid, (*chip, c)).wait_recv()
                fw = d2d(a, j, cid, c)
                fw.start()
                passed.append(fw)
        for a in range(n):
            for j, chip in enumerate(chips):
                d2d(a, j, 2 * chip[0] + chip[1], 1 - c).wait_recv()
        for cp in sends + passed:
            cp.wait_send()

    any_spec = pl.BlockSpec(memory_space=pl.ANY)
    return pl.pallas_call(
        body, name="gather_weights", in_specs=[any_spec] * n, out_specs=[any_spec] * n,
        out_shape=[jax.ShapeDtypeStruct(b.shape, b.dtype) for b in bufs],
        input_output_aliases={a: a for a in range(n)},
        scratch_shapes=[pltpu.SemaphoreType.DMA((n, 3))] * 4,
    )(*bufs)


def _half(c, r):
    return pl.ds(pl.multiple_of(c * (r // 2), 8), r // 2)


def _exchange_halves(gs):
    n = len(gs)

    def body(*refs):
        g_refs, out_refs = refs[:n], refs[n:2 * n]
        send_sems, recv_sems = refs[2 * n:]
        x, y, c, _ = _place()
        cps = []
        for a in range(n):
            r = g_refs[a].shape[2]
            cps.append(pltpu.make_async_remote_copy(
                src_ref=g_refs[a].at[:, :, _half(1 - c, r), :], dst_ref=out_refs[a],
                send_sem=send_sems.at[a], recv_sem=recv_sems.at[a], device_id=(x, y, 1 - c), device_id_type=MESH))
        for cp in cps:
            cp.start()
        for cp in cps:
            cp.wait()

    any_spec = pl.BlockSpec(memory_space=pl.ANY)
    return pl.pallas_call(
        body, name="grad_to_sibling", in_specs=[any_spec] * n, out_specs=[any_spec] * n,
        out_shape=[jax.ShapeDtypeStruct((g.shape[0], g.shape[1], g.shape[2] // 2, g.shape[3]), g.dtype) for g in gs],
        scratch_shapes=[pltpu.SemaphoreType.DMA((n,))] * 2,
    )(*gs)


def _add_half(g, got, c_arr):
    n, _, r, cw = g.shape

    def body(c_ref, a_ref, b_ref, o_ref):
        o_ref[...] = (a_ref[...].astype(F32) + b_ref[...].astype(F32)).astype(BF16)

    return pl.pallas_call(
        body, name="grad_add_half",
        grid_spec=pltpu.PrefetchScalarGridSpec(
            num_scalar_prefetch=1, grid=(n, N_CHIPS),
            in_specs=[pl.BlockSpec((None, None, r // 2, cw), lambda a, k, cr: (a, k, cr[0], 0)),
                      pl.BlockSpec((None, None, r // 2, cw), lambda a, k, cr: (a, k, 0, 0))],
            out_specs=pl.BlockSpec((None, None, r // 2, cw), lambda a, k, cr: (a, k, 0, 0))),
        out_shape=jax.ShapeDtypeStruct((n, N_CHIPS, r // 2, cw), BF16), compiler_params=_params(),
    )(c_arr, g, got)


def _scatter_to_chips(ss):
    n = len(ss)

    def body(*refs):
        s_refs, out_refs = refs[:n], refs[n:2 * n]
        send_sems, recv_sems, local_sems = refs[2 * n:]
        x, y, c, chips = _place()
        me = 2 * x + y
        local = [pltpu.make_async_copy(s_refs[a].at[:, me], out_refs[a].at[:, me], local_sems.at[a])
                 for a in range(n)]
        for cp in local:
            cp.start()
        cps = []
        for a in range(n):
            for j, chip in enumerate(chips):
                cid = 2 * chip[0] + chip[1]
                cps.append(pltpu.make_async_remote_copy(
                    src_ref=s_refs[a].at[:, cid], dst_ref=out_refs[a].at[:, me],
                    send_sem=send_sems.at[a, j], recv_sem=recv_sems.at[a, j], device_id=(*chip, c),
                    device_id_type=MESH))
        for cp in cps:
            cp.start()
        for cp in cps:
            cp.wait()
        for cp in local:
            cp.wait()

    any_spec = pl.BlockSpec(memory_space=pl.ANY)
    return pl.pallas_call(
        body, name="grad_to_chips", in_specs=[any_spec] * n, out_specs=[any_spec] * n,
        out_shape=[jax.ShapeDtypeStruct(s.shape, s.dtype) for s in ss],
        scratch_shapes=[pltpu.SemaphoreType.DMA((n, 3))] * 2 + [pltpu.SemaphoreType.DMA((n,))],
    )(*ss)


def _sum_chips(got):
    n, _, rh, cw = got.shape

    def body(g_ref, o_ref):
        acc = g_ref[0].astype(F32)
        for k in range(1, N_CHIPS):
            acc = acc + g_ref[k].astype(F32)
        o_ref[...] = acc

    return pl.pallas_call(
        body, name="grad_sum_chips", grid=(n,),
        in_specs=[pl.BlockSpec((None, N_CHIPS, rh, cw), lambda a: (a, 0, 0, 0))],
        out_specs=pl.BlockSpec((None, rh, cw), lambda a: (a, 0, 0)),
        out_shape=jax.ShapeDtypeStruct((n, rh, cw), F32), compiler_params=_params(),
    )(got)


def _swap_halves(fs):
    n = len(fs)

    def body(*refs):
        f_refs, out_refs = refs[:n], refs[n:2 * n]
        send_sems, recv_sems = refs[2 * n:]
        x, y, c, _ = _place()
        cps = [pltpu.make_async_remote_copy(
            src_ref=f_refs[a], dst_ref=out_refs[a], send_sem=send_sems.at[a], recv_sem=recv_sems.at[a],
            device_id=(x, y, 1 - c), device_id_type=MESH) for a in range(n)]
        for cp in cps:
            cp.start()
        for cp in cps:
            cp.wait()

    any_spec = pl.BlockSpec(memory_space=pl.ANY)
    return pl.pallas_call(
        body, name="grad_swap_halves", in_specs=[any_spec] * n, out_specs=[any_spec] * n,
        out_shape=[jax.ShapeDtypeStruct(f.shape, f.dtype) for f in fs],
        scratch_shapes=[pltpu.SemaphoreType.DMA((n,))] * 2,
    )(*fs)


def _allreduce_rows(stats):
    def body(s_ref, o_ref, buf, send_sems, recv_sems):
        x, y, c, _ = _place()
        me = 4 * x + 2 * y + c
        buf[me] = s_ref[...]
        cps = []
        for k in range(1, 8):
            px = jnp.where(k & 4, 1 - x, x)
            py = jnp.where(k & 2, 1 - y, y)
            pc = jnp.where(k & 1, 1 - c, c)
            cps.append(pltpu.make_async_remote_copy(
                src_ref=s_ref, dst_ref=buf.at[me], send_sem=send_sems.at[k - 1], recv_sem=recv_sems.at[k - 1],
                device_id=(px, py, pc), device_id_type=MESH))
        for cp in cps:
            cp.start()
        for cp in cps:
            cp.wait()
        acc = buf[0]
        for d in range(1, 8):
            acc = acc + buf[d]
        o_ref[...] = acc

    vm = pl.BlockSpec(memory_space=pltpu.VMEM)
    return pl.pallas_call(
        body, name="allreduce_rows", in_specs=[vm], out_specs=vm,
        out_shape=jax.ShapeDtypeStruct(stats.shape, F32),
        scratch_shapes=[pltpu.VMEM((8,) + stats.shape, F32), pltpu.SemaphoreType.DMA((7,)),
                        pltpu.SemaphoreType.DMA((7,))],
    )(stats)


def _adamw_math(w, g, m, v):
    m = ADAM_B1 * m + (1.0 - ADAM_B1) * g
    v = ADAM_B2 * v + (1.0 - ADAM_B2) * (g * g)
    m_hat = m / (1.0 - ADAM_B1 ** ADAM_STEP)
    v_hat = v / (1.0 - ADAM_B2 ** ADAM_STEP)
    delta = -ADAM_LR * (m_hat / (jnp.sqrt(v_hat) + ADAM_EPS) + ADAM_WD * w)
    return delta, m, v


def _adamw(w, m, v, mine, theirs, first, c_arr):
    L, r, cw = w.shape
    tr = r // 4 if r > 256 else r // 2
    nblk = (r // 2) // tr

    def body(c_ref, w_ref, m_ref, v_ref, a_ref, b_ref, go_ref, d_ref, mo_ref, vo_ref):
        g = jnp.where(pl.program_id(1) == c_ref[0], a_ref[...], b_ref[...])
        delta, mn, vn = _adamw_math(w_ref[...], g, m_ref[...], v_ref[...])
        go_ref[...] = g
        d_ref[...] = delta
        mo_ref[...] = mn
        vo_ref[...] = vn

    blk = pl.BlockSpec((None, tr, cw), lambda l, hh, i, cr: (l, hh * nblk + i, 0))
    half = pl.BlockSpec((None, tr, cw), lambda l, hh, i, cr: (first + l, i, 0))
    sh = jax.ShapeDtypeStruct(w.shape, F32)
    return pl.pallas_call(
        body, name="adamw",
        grid_spec=pltpu.PrefetchScalarGridSpec(
            num_scalar_prefetch=1, grid=(L, 2, nblk), in_specs=[blk, blk, blk, half, half], out_specs=[blk] * 4),
        out_shape=[sh] * 4, compiler_params=_params(),
    )(c_arr, w, m, v, mine, theirs)


def _adamw_rows(w, m, v, g):
    def body(w_ref, m_ref, v_ref, g_ref, d_ref, mo_ref, vo_ref):
        delta, mn, vn = _adamw_math(w_ref[...], g_ref[...], m_ref[...], v_ref[...])
        d_ref[...] = delta
        mo_ref[...] = mn
        vo_ref[...] = vn

    vm = pl.BlockSpec(memory_space=pltpu.VMEM)
    sh = jax.ShapeDtypeStruct(w.shape, F32)
    return pl.pallas_call(body, name="adamw_rows", in_specs=[vm] * 4, out_specs=[vm] * 3, out_shape=[sh] * 3)(w, m, v, g)


GROUPS = (("ffn1_w_gate", "ffn1_w_up", "ffn2_w_gate", "ffn2_w_up"), ("ffn1_w_down", "ffn2_w_down"),
          ("w_in",), ("w_proj_dil", "w_proj_sb"), ("w_out",))


def _pick_row(blocks):
    row = lax.broadcasted_iota(jnp.int32, (8, D_MODEL), 0)
    out = jnp.zeros((8, D_MODEL), F32)
    for i, b in enumerate(blocks):
        out = out + jnp.where(row == i, b, 0.0)
    return out


def kernel(x, norm_ffn1, ffn1_w_gate, ffn1_w_up, ffn1_w_down, norm_mix, w_in, w_proj_dil, w_proj_sb, w_out, norm_ffn2, ffn2_w_gate, ffn2_w_up, ffn2_w_down, norm_final, loss_target, m_norm_ffn1, m_ffn1_w_gate, m_ffn1_w_up, m_ffn1_w_down, m_norm_mix, m_w_in, m_w_proj_dil, m_w_proj_sb, m_w_out, m_norm_ffn2, m_ffn2_w_gate, m_ffn2_w_up, m_ffn2_w_down, m_norm_final, v_norm_ffn1, v_ffn1_w_gate, v_ffn1_w_up, v_ffn1_w_down, v_norm_mix, v_w_in, v_w_proj_dil, v_w_proj_sb, v_w_out, v_norm_ffn2, v_ffn2_w_gate, v_ffn2_w_up, v_ffn2_w_down, v_norm_final):
    given = dict(locals())
    weights = {n: given[n] for n in WEIGHT_NAMES}
    norms = {n: given[n] for n in NORM_NAMES}

    c_arr = lax.axis_index("c").astype(jnp.int32).reshape(1)
    me_arr = (2 * lax.axis_index("x") + lax.axis_index("y")).astype(jnp.int32).reshape(1)
    gathered = _gather_weights([_cast_into_slot(weights[n], me_arr) for n in WEIGHT_NAMES])
    W = dict(zip(WEIGHT_NAMES, gathered))
    loss_blk, grad_x, grads, gains, dg_final = _local_step(x[0], loss_target[0], norms, norm_final, W)

    stacked = [jnp.stack([grads[l][n] for n in grp for l in range(DEPTH)]) for grp in GROUPS]
    from_sibling = _exchange_halves(stacked)
    chip_sums = [_add_half(g, got, c_arr) for g, got in zip(stacked, from_sibling)]
    from_chips = _scatter_to_chips(chip_sums)
    mine = [_sum_chips(got) for got in from_chips]
    theirs = _swap_halves(mine)

    out = {"grad_x": grad_x[None]}
    for grp, ga, gb in zip(GROUPS, mine, theirs):
        for i, n in enumerate(grp):
            g, d, mn, vn = _adamw(weights[n], given["m_" + n], given["v_" + n], ga, gb, i * DEPTH, c_arr)
            out["grad_" + n], out["delta_" + n], out["new_m_" + n], out["new_v_" + n] = g, d, mn, vn

    rows = [gains[l][n] for n in NORM_NAMES for l in range(DEPTH)] + [dg_final, loss_blk]
    total = _allreduce_rows(_pick_row(rows))
    out["loss"] = total[7, 0]
    wn = jnp.concatenate([given[n] for n in NORM_NAMES] + [norm_final[None], jnp.zeros((1, D_MODEL), F32)])
    mn_ = jnp.concatenate([given["m_" + n] for n in NORM_NAMES] + [m_norm_final[None], jnp.zeros((1, D_MODEL), F32)])
    vn_ = jnp.concatenate([given["v_" + n] for n in NORM_NAMES] + [v_norm_final[None], jnp.ones((1, D_MODEL), F32)])
    d_n, m_n, v_n = _adamw_rows(wn, mn_, vn_, total)
    for i, n in enumerate(NORM_NAMES):
        sl = slice(i * DEPTH, (i + 1) * DEPTH)
        out["grad_" + n], out["delta_" + n], out["new_m_" + n], out["new_v_" + n] = total[sl], d_n[sl], m_n[sl], v_n[sl]
    out["grad_norm_final"], out["delta_norm_final"] = total[6], d_n[6]
    out["new_m_norm_final"], out["new_v_norm_final"] = m_n[6], v_n[6]

    order = WEIGHT_NAMES
    del order
    names = ["norm_ffn1", "ffn1_w_gate", "ffn1_w_up", "ffn1_w_down", "norm_mix", "w_in", "w_proj_dil", "w_proj_sb",
             "w_out", "norm_ffn2", "ffn2_w_gate", "ffn2_w_up", "ffn2_w_down", "norm_final"]
    return (out["loss"], out["grad_x"], *[out["grad_" + n] for n in names], *[out["delta_" + n] for n in names],
            *[out["new_m_" + n] for n in names], *[out["new_v_" + n] for n in names])
```

```python
import functools

import jax
import jax.numpy as jnp
from jax import lax
from jax.experimental import pallas as pl
from jax.experimental.pallas import tpu as pltpu

F32 = jnp.float32
BF16 = jnp.bfloat16

D_MODEL = 1024
DEPTH = 2
N_CHIPS = 4
HEAD_DIM = 64
ROPE_DIM = 16
ROPE_THETA = 500000.0
DIL_GROUPS = ((128, 1), (512, 4), (2048, 16))
SPAN = 128
Q_BLOCK = 128
RMS_EPS = 1e-6
D_ATT = 256
COL_QS = 2304
COL_GD = 3072
COL_GS = 4096
ADAM_LR, ADAM_B1, ADAM_B2, ADAM_EPS, ADAM_WD, ADAM_STEP = 0.001, 0.9, 0.999, 1e-08, 0.01, 10

VMEM_LIMIT = 52 * 1024 * 1024
TM = 512
NEG = -1e30

NN = (((1,), (0,)), ((), ()))
NT = (((1,), (1,)), ((), ()))
TN = (((0,), (0,)), ((), ()))
MESH = pl.DeviceIdType.MESH

WEIGHT_NAMES = ("ffn1_w_gate", "ffn1_w_up", "ffn1_w_down", "w_in", "w_proj_dil",
                "w_proj_sb", "w_out", "ffn2_w_gate", "ffn2_w_up", "ffn2_w_down")
NORM_NAMES = ("norm_ffn1", "norm_mix", "norm_ffn2")


def _params(**kw):
    return pltpu.CompilerParams(vmem_limit_bytes=VMEM_LIMIT, **kw)


def _sigmoid(x):
    return 0.5 * jnp.tanh(0.5 * x) + 0.5


def _mm_body(pairs, n_in, n_out, n_acc, dims, nk, i_axis, epilogue, *refs):
    ins = refs[:n_in]
    outs = refs[n_in:n_in + n_out]
    accs = refs[n_in + n_out:]
    i = pl.program_id(i_axis)
    k = pl.program_id(2)

    def operand(a):
        return (a(ins) if callable(a) else ins[a][...]).astype(BF16)

    def dot(ia, ib):
        return lax.dot_general(operand(ia), operand(ib), dims, preferred_element_type=F32)

    if nk == 1:
        parts = [None] * n_acc
        for ia, ib, ic in pairs:
            parts[ic] = dot(ia, ib) if parts[ic] is None else parts[ic] + dot(ia, ib)
        epilogue(parts, ins, outs, i)
        return

    @pl.when(k == 0)
    def _():
        for c in range(n_acc):
            accs[c][...] = jnp.zeros_like(accs[c])

    for ia, ib, ic in pairs:
        accs[ic][...] += dot(ia, ib)

    @pl.when(k == nk - 1)
    def _():
        epilogue([a[...] for a in accs], ins, outs, i)


def _j_outer(spec):
    f = spec.index_map
    return pl.BlockSpec(spec.block_shape, lambda j, i, k: f(i, j, k))


def _mm(name, ins, in_specs, pairs, n_acc, acc_shape, dims, grid, epilogue, out_shapes, out_specs, j_outer=False):
    nk = grid[2]
    if j_outer:
        grid = (grid[1], grid[0], grid[2])
        in_specs = [_j_outer(s) for s in in_specs]
        out_specs = [_j_outer(s) for s in out_specs]
    scratch = [pltpu.VMEM(acc_shape, F32) for _ in range(n_acc)] if nk > 1 else []
    body = functools.partial(_mm_body, tuple(pairs), len(ins), len(out_shapes), n_acc, dims, nk,
                             1 if j_outer else 0, epilogue)
    return pl.pallas_call(
        body, name=name, grid=grid, in_specs=in_specs, out_specs=out_specs, out_shape=out_shapes,
        scratch_shapes=scratch,
        compiler_params=_params(dimension_semantics=("arbitrary", "arbitrary", "arbitrary")),
    )(*ins)


def _wspec(r, c, l, by):
    if by == 1:
        return pl.BlockSpec((None, None, r, c), lambda i, j, k: (j, l, 0, 0))
    return pl.BlockSpec((None, None, r, c), lambda i, j, k: (k, l, 0, 0))


def _rms_bwd_epilogue(x_idx, g_idx, dxo_idx):
    def ep(vals, ins, outs, i):
        dh = vals[0]
        x = ins[x_idx][...]
        g = ins[g_idx][...]
        rstd = lax.rsqrt(jnp.mean(x * x, axis=-1, keepdims=True) + RMS_EPS)
        xhat = x * rstd
        dxhat = dh * g
        dx = rstd * (dxhat - xhat * jnp.mean(dxhat * xhat, axis=-1, keepdims=True))
        outs[0][...] = ins[dxo_idx][...] + dx
        dg = jnp.broadcast_to(jnp.sum(dh * xhat, axis=0, keepdims=True), outs[1].shape)

        @pl.when(i == 0)
        def _():
            outs[1][...] = dg

        @pl.when(i > 0)
        def _():
            outs[1][...] += dg
    return ep


def _rms_fwd(x, gain):
    T = x.shape[0]

    def body(x_ref, g_ref, h_ref):
        xv = x_ref[...]
        h = xv * lax.rsqrt(jnp.mean(xv * xv, axis=-1, keepdims=True) + RMS_EPS)
        h_ref[...] = (h * g_ref[...]).astype(BF16)

    return pl.pallas_call(
        body, name="rms_fwd", grid=(T // TM,),
        in_specs=[pl.BlockSpec((TM, D_MODEL), lambda i: (i, 0)), pl.BlockSpec((1, D_MODEL), lambda i: (0, 0))],
        out_specs=pl.BlockSpec((TM, D_MODEL), lambda i: (i, 0)),
        out_shape=jax.ShapeDtypeStruct((T, D_MODEL), BF16), compiler_params=_params(),
    )(x, gain)


def _rope_tables(T):
    pos = jnp.arange(T, dtype=F32)
    inv_freq = ROPE_THETA ** (-jnp.arange(0, ROPE_DIM, 2, dtype=F32) / ROPE_DIM)
    ang = pos[:, None] * inv_freq[None, :]
    cos, sin = jnp.cos(ang), jnp.sin(ang)
    half = ROPE_DIM // 2
    one = jnp.ones((T, HEAD_DIM - ROPE_DIM), F32)
    zero = jnp.zeros((T, HEAD_DIM - ROPE_DIM), F32)
    zh = jnp.zeros((T, half), F32)
    c = jnp.concatenate([cos, cos, one], axis=1)
    s1 = jnp.concatenate([-sin, zh, zero], axis=1)
    s2 = jnp.concatenate([zh, sin, zero], axis=1)
    return tuple(jnp.concatenate([t, t], axis=1) for t in (c, s1, s2))


def _rope_fwd(xv, c, s1, s2):
    w = xv.shape[1]
    half = ROPE_DIM // 2
    return xv * c + pltpu.roll(xv, w - half, 1) * s1 + pltpu.roll(xv, half, 1) * s2


def _rope_bwd(dy, c, s1, s2):
    w = dy.shape[1]
    half = ROPE_DIM // 2
    return dy * c + pltpu.roll(dy * s1, half, 1) + pltpu.roll(dy * s2, w - half, 1)


def _assemble_dproj(dqk, rest, gates, tabs):
    T = gates[0].shape[0]
    n_qk, n_rest = len(dqk), len(rest)
    width = (n_qk + n_rest) * D_ATT + 2 * D_MODEL

    def body(*refs):
        ins, (c_ref, s1_ref, s2_ref), o_ref = refs[:n_qk + n_rest + 2], refs[-4:-1], refs[-1]
        c = jnp.concatenate([c_ref[...]] * 2, axis=1)
        s1 = jnp.concatenate([s1_ref[...]] * 2, axis=1)
        s2 = jnp.concatenate([s2_ref[...]] * 2, axis=1)
        for b in range(n_qk + n_rest):
            v = ins[b][...]
            if b < n_qk:
                v = _rope_bwd(v, c, s1, s2)
            o_ref[:, b * D_ATT:(b + 1) * D_ATT] = v.astype(BF16)
        off = (n_qk + n_rest) * D_ATT
        o_ref[:, off:off + D_MODEL] = ins[-2][...]
        o_ref[:, off + D_MODEL:] = ins[-1][...]

    att = pl.BlockSpec((TM, D_ATT), lambda i: (i, 0))
    wide = pl.BlockSpec((TM, D_MODEL), lambda i: (i, 0))
    tab = pl.BlockSpec((TM, 128), lambda i: (i, 0))
    return pl.pallas_call(
        body, name="assemble_dproj", grid=(T // TM,),
        in_specs=[att] * (n_qk + n_rest) + [wide, wide, tab, tab, tab],
        out_specs=pl.BlockSpec((TM, width), lambda i: (i, 0)),
        out_shape=jax.ShapeDtypeStruct((T, width), BF16), compiler_params=_params(),
    )(*dqk, *rest, *gates, *tabs)


def _dil_merge(os_, lses):
    T = os_[0].shape[0]

    def body(o0, o1, o2, l0, l1, l2, o_ref, lse_ref):
        a, b, c = l0[...], l1[...], l2[...]
        m = jnp.maximum(jnp.maximum(a, b), c)
        ea, eb, ec = jnp.exp(a - m), jnp.exp(b - m), jnp.exp(c - m)
        den = ea + eb + ec
        o_ref[...] = (ea * o0[...] + eb * o1[...] + ec * o2[...]) / den
        lse_ref[...] = m + jnp.log(den)

    blk = pl.BlockSpec((TM, D_ATT), lambda i: (i, 0))
    sh = jax.ShapeDtypeStruct((T, D_ATT), F32)
    return pl.pallas_call(
        body, name="dil_merge", grid=(T // TM,), in_specs=[blk] * 6, out_specs=[blk, blk],
        out_shape=[sh, sh], compiler_params=_params(),
    )(*os_, *lses)


def _final_loss(x, gain, target):
    T = x.shape[0]

    def body(x_ref, g_ref, t_ref, dx_ref, dg_ref, loss_ref):
        xv = x_ref[...]
        g = g_ref[...]
        rstd = lax.rsqrt(jnp.mean(xv * xv, axis=-1, keepdims=True) + RMS_EPS)
        xhat = xv * rstd
        err = xhat * g - t_ref[...]
        loss = 0.5 * jnp.sum(jnp.mean(err * err, axis=-1, keepdims=True), axis=0, keepdims=True)
        dy = err * (1.0 / D_MODEL)
        dxhat = dy * g
        dx_ref[...] = rstd * (dxhat - xhat * jnp.mean(dxhat * xhat, axis=-1, keepdims=True))
        dg = jnp.broadcast_to(jnp.sum(dy * xhat, axis=0, keepdims=True), dg_ref.shape)
        ls = jnp.broadcast_to(loss, loss_ref.shape)

        @pl.when(pl.program_id(0) == 0)
        def _():
            dg_ref[...] = dg
            loss_ref[...] = ls

        @pl.when(pl.program_id(0) > 0)
        def _():
            dg_ref[...] += dg
            loss_ref[...] += ls

    blk = pl.BlockSpec((TM, D_MODEL), lambda i: (i, 0))
    row = pl.BlockSpec((1, D_MODEL), lambda i: (0, 0))
    acc = pl.BlockSpec((8, D_MODEL), lambda i: (0, 0))
    return pl.pallas_call(
        body, name="final_loss", grid=(T // TM,), in_specs=[blk, row, blk], out_specs=[blk, acc, acc],
        out_shape=[jax.ShapeDtypeStruct((T, D_MODEL), F32), jax.ShapeDtypeStruct((8, D_MODEL), F32),
                   jax.ShapeDtypeStruct((8, D_MODEL), F32)],
        compiler_params=_params(dimension_semantics=("arbitrary",)),
    )(x, gain, target)


def _pair_masks():
    lane = lax.broadcasted_iota(jnp.int32, (SPAN, 128), 1)
    return [lane < HEAD_DIM, lane >= HEAD_DIM]


def _stack_heads(x, masks):
    return jnp.concatenate([jnp.where(m, x, 0.0) for m in masks], axis=0)


def _unstack_heads(y, masks):
    rows = y.shape[0] // len(masks)
    out = jnp.where(masks[0], y[:rows], 0.0)
    for h in range(1, len(masks)):
        out = out + jnp.where(masks[h], y[rows * h:rows * (h + 1)], 0.0)
    return out


def _dil_rows(idx, d):
    u = idx // d
    r = idx - u * d
    own = pl.ds(u * (SPAN * d) + r, SPAN, stride=d) if d > 1 else pl.ds(pl.multiple_of(u * SPAN, SPAN), SPAN)
    up = jnp.maximum(u - 1, 0)
    prev = pl.ds(up * (SPAN * d) + r, SPAN, stride=d) if d > 1 else pl.ds(pl.multiple_of(up * SPAN, SPAN), SPAN)
    return u, own, prev


def _dil_valid(u):
    qi = lax.broadcasted_iota(jnp.int32, (2 * SPAN, 2 * SPAN), 0) & (SPAN - 1)
    kj = lax.broadcasted_iota(jnp.int32, (2 * SPAN, 2 * SPAN), 1)
    in_prev = (kj < SPAN) & (kj >= qi + jnp.where(u > 0, 0, SPAN))
    return in_prev | ((kj >= SPAN) & (kj - SPAN <= qi))


def _dil_keys(ref, own, prev):
    return jnp.concatenate([ref[prev, :], ref[own, :]], axis=0).astype(BF16)


def _dil_fwd(proj, g, d):
    T = proj.shape[0]
    n_iter = T // SPAN

    def body(q_ref, k_ref, v_ref, o_ref, lse_ref):
        masks = _pair_masks()

        def step(idx, carry):
            u, own, prev = _dil_rows(idx, d)
            qs = _stack_heads(q_ref[own, :] * (HEAD_DIM ** -0.5), masks).astype(BF16)
            kk = _dil_keys(k_ref, own, prev)
            vv = _dil_keys(v_ref, own, prev)
            s = jnp.where(_dil_valid(u), lax.dot_general(qs, kk, NT, preferred_element_type=F32), NEG)
            m = jnp.max(s, axis=1, keepdims=True)
            p = jnp.exp(s - m)
            den = jnp.sum(p, axis=1, keepdims=True)
            pv = lax.dot_general(p.astype(BF16), vv, NN, preferred_element_type=F32) / den
            o_ref[own, :] = _unstack_heads(pv, masks)
            lse_ref[own, :] = _unstack_heads(jnp.broadcast_to(m + jnp.log(den), pv.shape), masks)
            return carry

        lax.fori_loop(0, n_iter, step, 0, unroll=2)

    def col(b):
        return pl.BlockSpec((T, 128), lambda p: (0, b + p))

    sh = jax.ShapeDtypeStruct((T, D_ATT), F32)
    out = pl.BlockSpec((T, 128), lambda p: (0, p))
    return pl.pallas_call(
        body, name=f"dil_fwd_d{d}", grid=(2,),
        in_specs=[col(2 * g), col(6 + 2 * g), col(12 + 2 * g)], out_specs=[out, out], out_shape=[sh, sh],
        compiler_params=_params(dimension_semantics=("arbitrary",)),
    )(proj, proj, proj)


def _dil_bwd(proj, do, o_dil, lse, g, d):
    T = proj.shape[0]
    n_iter = T // SPAN

    def body(q_ref, k_ref, v_ref, do_ref, o_ref, lse_ref, dq_ref, dk_ref, dv_ref):
        masks = _pair_masks()
        head_lanes = jnp.concatenate(masks, axis=0)

        def step(idx, carry):
            u, own, prev = _dil_rows(idx, d)
            qs = _stack_heads(q_ref[own, :] * (HEAD_DIM ** -0.5), masks).astype(BF16)
            kk = _dil_keys(k_ref, own, prev)
            vv = _dil_keys(v_ref, own, prev)
            dom = _stack_heads(do_ref[own, :], masks)
            dos = dom.astype(BF16)
            delta = jnp.sum(dom * jnp.concatenate([o_ref[own, :]] * 2, axis=0), axis=1, keepdims=True)
            lrow = jnp.max(jnp.where(head_lanes, jnp.concatenate([lse_ref[own, :]] * 2, axis=0), NEG),
                           axis=1, keepdims=True)
            s = lax.dot_general(qs, kk, NT, preferred_element_type=F32)
            p = jnp.where(_dil_valid(u), jnp.exp(s - lrow), 0.0)
            dp = lax.dot_general(dos, vv, NT, preferred_element_type=F32)
            ds = (p * (dp - delta)).astype(BF16)
            dq = lax.dot_general(ds, kk, NN, preferred_element_type=F32)
            dkk = lax.dot_general(ds, qs, TN, preferred_element_type=F32)
            dvv = lax.dot_general(p.astype(BF16), dos, TN, preferred_element_type=F32)
            dq_ref[own, :] = _unstack_heads(dq, masks) * (HEAD_DIM ** -0.5)
            dk_ref[own, :] = dkk[SPAN:]
            dv_ref[own, :] = dvv[SPAN:]
            dk_ref[prev, :] = dk_ref[prev, :] + dkk[:SPAN]
            dv_ref[prev, :] = dv_ref[prev, :] + dvv[:SPAN]
            return carry

        lax.fori_loop(0, n_iter, step, 0, unroll=2)

    def col(b):
        return pl.BlockSpec((T, 128), lambda p: (0, b + p))

    sh = jax.ShapeDtypeStruct((T, D_ATT), F32)
    return pl.pallas_call(
        body, name=f"dil_bwd_d{d}", grid=(2,),
        in_specs=[col(2 * g), col(6 + 2 * g), col(12 + 2 * g), col(0), col(0), col(0)],
        out_specs=[col(0), col(0), col(0)], out_shape=[sh, sh, sh],
        compiler_params=_params(dimension_semantics=("arbitrary",)),
    )(proj, proj, proj, do, o_dil, lse)


SB_KT = 512


def _sb_tri(strict):
    a = lax.broadcasted_iota(jnp.int32, (Q_BLOCK, Q_BLOCK), 0)
    b = lax.broadcasted_iota(jnp.int32, (Q_BLOCK, Q_BLOCK), 1)
    return jnp.where((a > b) if strict else (a >= b), 1.0, 0.0).astype(BF16)


def _suffix(x, c, tri):
    r = x.shape[0]
    nb = x.shape[1] // Q_BLOCK
    blocks = [x[:, Q_BLOCK * b:Q_BLOCK * (b + 1)] for b in range(nb)]
    hi = [b.astype(BF16) for b in blocks]
    lo = [(b - h.astype(F32)).astype(BF16) for b, h in zip(blocks, hi)]
    y = lax.dot_general(jnp.concatenate(hi + lo, axis=0), tri, NN, preferred_element_type=F32)
    outs = [None] * nb
    run = c
    for b in reversed(range(nb)):
        outs[b] = run + y[r * b:r * (b + 1)] + y[r * (nb + b):r * (nb + b + 1)]
        run = run + jnp.sum(blocks[b], axis=1, keepdims=True)
    return jnp.concatenate(outs, axis=1), run


def _sb_tile(qs, kb, past, c, tri):
    z = lax.dot_general(qs, kb, NT, preferred_element_type=F32)
    lsz = jnp.minimum(z, 0.0) - jnp.log(1.0 + jnp.exp(-jnp.abs(z)))
    lk = lsz - z
    if past is not None:
        lk = jnp.where(past, lk, 0.0)
    after, c_new = _suffix(lk, c, tri)
    w = jnp.exp(lsz + after)
    if past is not None:
        w = jnp.where(past, w, 0.0)
    return z, lsz, w, c_new


SB_HEADS = D_ATT // HEAD_DIM
SB_ROWS = SB_HEADS * Q_BLOCK


def _sb_past(i, t):
    row = lax.broadcasted_iota(jnp.int32, (SB_ROWS, SB_KT), 0) & (Q_BLOCK - 1)
    col = lax.broadcasted_iota(jnp.int32, (SB_ROWS, SB_KT), 1)
    return col + t * SB_KT < row + i * Q_BLOCK


def _sb_head_masks():
    lane = lax.broadcasted_iota(jnp.int32, (Q_BLOCK, D_ATT), 1)
    return [(lane >= HEAD_DIM * h) & (lane < HEAD_DIM * (h + 1)) for h in range(SB_HEADS)]


def _sb_rows(t):
    return pl.ds(pl.multiple_of(t * SB_KT, SB_KT), SB_KT)


def _sb_fwd(proj):
    T = proj.shape[0]

    def body(q_ref, k_ref, v_ref, o_ref):
        i = pl.program_id(0)
        masks = _sb_head_masks()
        tri = _sb_tri(True)
        qs = _stack_heads(q_ref[...] * (HEAD_DIM ** -0.5), masks).astype(BF16)
        n_tiles = (i * Q_BLOCK) // SB_KT + 1

        def tile(t, carry, masked):
            kb = k_ref[_sb_rows(t), :].astype(BF16)
            vb = v_ref[_sb_rows(t), :].astype(BF16)
            acc, c = carry
            _, _, w, c = _sb_tile(qs, kb, _sb_past(i, t) if masked else None, c, tri)
            pv = lax.dot_general(w.astype(BF16), vb, NN, preferred_element_type=F32)
            return acc + _unstack_heads(pv, masks), c

        carry = tile(n_tiles - 1, (jnp.zeros((Q_BLOCK, D_ATT), F32), jnp.zeros((SB_ROWS, 1), F32)), True)
        carry = lax.fori_loop(0, n_tiles - 1, lambda tt, cr: tile(n_tiles - 2 - tt, cr, False), carry)
        o_ref[...] = carry[0]

    cb = COL_QS // D_ATT
    return pl.pallas_call(
        body, name="sb_fwd", grid=(T // Q_BLOCK,),
        in_specs=[pl.BlockSpec((Q_BLOCK, D_ATT), lambda i: (i, cb)),
                  pl.BlockSpec((T, D_ATT), lambda i: (0, cb + 1)),
                  pl.BlockSpec((T, D_ATT), lambda i: (0, cb + 2))],
        out_specs=pl.BlockSpec((Q_BLOCK, D_ATT), lambda i: (i, 0)),
        out_shape=jax.ShapeDtypeStruct((T, D_ATT), F32),
        compiler_params=_params(dimension_semantics=("arbitrary",)),
    )(proj, proj, proj)


def _sb_bwd(proj, do, o):
    T = proj.shape[0]

    def body(q_ref, k_ref, v_ref, do_ref, o_ref, dq_ref, dk_ref, dv_ref):
        i = pl.program_id(0)
        masks = _sb_head_masks()
        tri = _sb_tri(True)
        tri_incl = _sb_tri(False)

        @pl.when(i == 0)
        def _():
            dk_ref[...] = jnp.zeros_like(dk_ref)
            dv_ref[...] = jnp.zeros_like(dv_ref)

        qs = _stack_heads(q_ref[...] * (HEAD_DIM ** -0.5), masks).astype(BF16)
        dos = _stack_heads(do_ref[...], masks).astype(BF16)
        delta = jnp.sum(dos.astype(F32) * jnp.concatenate([o_ref[...]] * SB_HEADS, axis=0), axis=1, keepdims=True)
        n_tiles = (i * Q_BLOCK) // SB_KT + 1

        def tile(t, carry, masked):
            rows = _sb_rows(t)
            kb = k_ref[rows, :].astype(BF16)
            vb = v_ref[rows, :].astype(BF16)
            past = _sb_past(i, t) if masked else None
            dq, c, ce = carry
            z, lsz, w, c = _sb_tile(qs, kb, past, c, tri)
            gv = lax.dot_general(dos, vb, NT, preferred_element_type=F32)
            wb = w.astype(BF16)
            e = wb.astype(F32) * gv
            suf, ce = _suffix(e, ce, tri_incl)
            dz = e * jnp.exp(lsz - z) - (delta - suf) * jnp.exp(lsz)
            if masked:
                dz = jnp.where(past, dz, 0.0)
            dzb = dz.astype(BF16)
            dq = dq + _unstack_heads(lax.dot_general(dzb, kb, NN, preferred_element_type=F32), masks)
            dk_ref[rows, :] = dk_ref[rows, :] + lax.dot_general(dzb, qs, TN, preferred_element_type=F32)
            dv_ref[rows, :] = dv_ref[rows, :] + lax.dot_general(wb, dos, TN, preferred_element_type=F32)
            return dq, c, ce

        zcol = jnp.zeros((SB_ROWS, 1), F32)
        carry = tile(n_tiles - 1, (jnp.zeros((Q_BLOCK, D_ATT), F32), zcol, zcol), True)
        carry = lax.fori_loop(0, n_tiles - 1, lambda tt, cr: tile(n_tiles - 2 - tt, cr, False), carry)
        dq_ref[...] = carry[0] * (HEAD_DIM ** -0.5)

    cb = COL_QS // D_ATT
    blk = pl.BlockSpec((Q_BLOCK, D_ATT), lambda i: (i, 0))
    full = pl.BlockSpec((T, D_ATT), lambda i: (0, 0))
    sh = jax.ShapeDtypeStruct((T, D_ATT), F32)
    return pl.pallas_call(
        body, name="sb_bwd", grid=(T // Q_BLOCK,),
        in_specs=[pl.BlockSpec((Q_BLOCK, D_ATT), lambda i: (i, cb)),
                  pl.BlockSpec((T, D_ATT), lambda i: (0, cb + 1)),
                  pl.BlockSpec((T, D_ATT), lambda i: (0, cb + 2)), blk, blk],
        out_specs=[blk, full, full], out_shape=[sh, sh, sh],
        compiler_params=_params(dimension_semantics=("arbitrary",)),
    )(proj, proj, proj, do, o)


def _tok(c, by=None):
    if by is None:
        return pl.BlockSpec((TM, c), lambda i, j, k: (i, 0))
    if by == 1:
        return pl.BlockSpec((TM, c), lambda i, j, k: (i, j))
    return pl.BlockSpec((TM, c), lambda i, j, k: (i, k))


def _chunked(c, by):
    if by == 1:
        return pl.BlockSpec((None, TM, c), lambda i, j, k: (j, i, 0))
    return pl.BlockSpec((None, TM, c), lambda i, j, k: (k, i, 0))


def _gain_spec():
    return pl.BlockSpec((1, D_MODEL), lambda i, j, k: (0, 0))


def _all_chunks(rows, c):
    return pl.BlockSpec((N_CHIPS, rows, c), lambda i, j, k: (0, i, 0))


def _wfull(r, c, l):
    return pl.BlockSpec((N_CHIPS, None, r, c), lambda i, j, k: (0, l, 0, 0))


def _pick(idx, c):
    return lambda ins: ins[idx][c]


def _cols(idx, c, w):
    return lambda ins: ins[idx][:, c * w:(c + 1) * w]


def _ffn_fwd(x, gain, wg, wu, wd, l):
    T = x.shape[0]
    ffs = wg.shape[3]
    h = _rms_fwd(x, gain)

    def swiglu(vals, ins, outs, i):
        gt, up = vals
        outs[0][...] = gt.astype(BF16)
        outs[1][...] = up.astype(BF16)
        outs[2][...] = (gt * _sigmoid(gt) * up).astype(BF16)

    csh = jax.ShapeDtypeStruct((N_CHIPS, T, ffs), BF16)
    gate, up, act = _mm(
        "ffn_up", [h, wg, wu], [_tok(D_MODEL), _wspec(D_MODEL, ffs, l, 1), _wspec(D_MODEL, ffs, l, 1)],
        [(0, 1, 0), (0, 2, 1)], 2, None, NN, (T // TM, N_CHIPS, 1), swiglu,
        [csh, csh, csh], [_chunked(ffs, 1)] * 3, j_outer=True)

    def resid(vals, ins, outs, i):
        outs[0][...] = ins[2][...] + 0.5 * vals[0]

    (y,) = _mm(
        "ffn_down", [act, wd, x], [_all_chunks(TM, ffs), _wfull(ffs, D_MODEL, l), _tok(D_MODEL)],
        [(_pick(0, c), _pick(1, c), 0) for c in range(N_CHIPS)], 1, None, NN, (T // TM, 1, 1), resid,
        [jax.ShapeDtypeStruct((T, D_MODEL), F32)], [_tok(D_MODEL)])
    return y, (x, h, gate, up, act)


def _ffn_bwd(dxo, gain, wg, wu, wd, l, saved):
    x, h, gate, up, act = saved
    T = x.shape[0]
    ffs = wg.shape[3]
    tk = TM
    tm = TM // 2

    def dswiglu(vals, ins, outs, i):
        for c in range(N_CHIPS):
            da = 0.5 * vals[c]
            gt = ins[2][c].astype(F32)
            u = ins[3][c].astype(F32)
            s = _sigmoid(gt)
            outs[0][c] = (da * u * (s * (1.0 + gt * (1.0 - s)))).astype(BF16)
            outs[1][c] = (da * (gt * s)).astype(BF16)

    csh = jax.ShapeDtypeStruct((N_CHIPS, T, ffs), BF16)
    row = pl.BlockSpec((tm, D_MODEL), lambda i, j, k: (i, 0))
    dgate, dup = _mm(
        "ffn_dact", [dxo, wd, gate, up],
        [row, _wfull(ffs, D_MODEL, l), _all_chunks(tm, ffs), _all_chunks(tm, ffs)],
        [(0, _pick(1, c), c) for c in range(N_CHIPS)], N_CHIPS, None, NT, (T // tm, 1, 1), dswiglu,
        [csh, csh], [_all_chunks(tm, ffs)] * 2)

    def halves(vals, ins, outs, i):
        for c in range(N_CHIPS):
            outs[0][c] = (0.5 * vals[c]).astype(BF16)

    def casts(vals, ins, outs, i):
        for c in range(N_CHIPS):
            outs[0][c] = vals[c].astype(BF16)

    tok_k = pl.BlockSpec((tk, D_MODEL), lambda i, j, k: (k, 0))
    chunks_k = pl.BlockSpec((N_CHIPS, tk, ffs), lambda i, j, k: (0, k, 0))
    (dwd,) = _mm(
        "ffn_dwd", [act, dxo], [chunks_k, tok_k], [(_pick(0, c), 1, c) for c in range(N_CHIPS)], N_CHIPS,
        (ffs, D_MODEL), TN, (1, 1, T // tk), halves, [jax.ShapeDtypeStruct((N_CHIPS, ffs, D_MODEL), BF16)],
        [pl.BlockSpec((N_CHIPS, ffs, D_MODEL), lambda i, j, k: (0, 0, 0))])

    dx, dgain = _mm(
        "ffn_dx", [dgate, dup, wg, wu, x, gain, dxo],
        [_all_chunks(tm, ffs), _all_chunks(tm, ffs), _wfull(D_MODEL, ffs, l), _wfull(D_MODEL, ffs, l),
         row, _gain_spec(), row],
        [(_pick(a, c), _pick(a + 2, c), 0) for c in range(N_CHIPS) for a in range(2)], 1, None, NT,
        (T // tm, 1, 1), _rms_bwd_epilogue(4, 5, 6),
        [jax.ShapeDtypeStruct((T, D_MODEL), F32), jax.ShapeDtypeStruct((8, D_MODEL), F32)],
        [row, pl.BlockSpec((8, D_MODEL), lambda i, j, k: (0, 0))])

    wsh = jax.ShapeDtypeStruct((N_CHIPS, D_MODEL, ffs), BF16)
    wout = pl.BlockSpec((N_CHIPS, D_MODEL, ffs), lambda i, j, k: (0, 0, 0))
    dws = []
    for dact in (dgate, dup):
        dws += _mm("ffn_dwgu", [h, dact], [tok_k, chunks_k], [(0, _pick(1, c), c) for c in range(N_CHIPS)],
                   N_CHIPS, (D_MODEL, ffs), TN, (1, 1, T // tk), casts, [wsh], [wout])
    return dx, dgain, dws[0], dws[1], dwd


def _mixer_fwd(x, gain, W, l, tabs):
    T = x.shape[0]
    win, wpd, wps, wo = W["w_in"], W["w_proj_dil"], W["w_proj_sb"], W["w_out"]
    cin = win.shape[3]
    cp = wpd.shape[3]
    h = _rms_fwd(x, gain)

    n_rope = 6 * D_ATT

    def roped(vals, ins, outs, i):
        v = vals[0]
        col0 = pl.program_id(0) * cin

        @pl.when(col0 < n_rope)
        def _():
            on = lax.broadcasted_iota(jnp.int32, v.shape, 1) + col0 < n_rope
            c = jnp.where(on, jnp.concatenate([ins[2][...]] * (cin // 128), axis=1), 1.0)
            s1 = jnp.where(on, jnp.concatenate([ins[3][...]] * (cin // 128), axis=1), 0.0)
            s2 = jnp.where(on, jnp.concatenate([ins[4][...]] * (cin // 128), axis=1), 0.0)
            outs[0][...] = _rope_fwd(v, c, s1, s2)

        @pl.when(col0 >= n_rope)
        def _():
            outs[0][...] = v

    (proj,) = _mm(
        "mix_in", [h, win, *tabs], [_tok(D_MODEL), _wspec(D_MODEL, cin, l, 1)] + [_tok(128)] * 3, [(0, 1, 0)], 1,
        None, NN, (T // TM, N_CHIPS, 1), roped, [jax.ShapeDtypeStruct((T, N_CHIPS * cin), F32)], [_tok(cin, 1)],
        j_outer=True)

    os_, lses = [], []
    for g, (window, dil) in enumerate(DIL_GROUPS):
        o_g, lse_g = _dil_fwd(proj, g, dil)
        os_.append(o_g)
        lses.append(lse_g)
    o_dil, lse = _dil_merge(os_, lses)
    o_sb = _sb_fwd(proj)

    def gated(vals, ins, outs, i):
        pd, ps = vals
        outs[0][...] = (_sigmoid(ins[4][...]) * pd + _sigmoid(ins[5][...]) * ps).astype(BF16)
        outs[1][...] = pd.astype(BF16)
        outs[2][...] = ps.astype(BF16)

    gd0, gs0 = COL_GD // cp, COL_GS // cp
    ush = jax.ShapeDtypeStruct((T, D_MODEL), BF16)
    u, pd, ps = _mm(
        "mix_gate", [o_dil, o_sb, wpd, wps, proj, proj],
        [_tok(D_ATT), _tok(D_ATT), _wspec(D_ATT, cp, l, 1), _wspec(D_ATT, cp, l, 1),
         pl.BlockSpec((TM, cp), lambda i, j, k: (i, gd0 + j)), pl.BlockSpec((TM, cp), lambda i, j, k: (i, gs0 + j))],
        [(0, 2, 0), (1, 3, 1)], 2, None, NN, (T // TM, N_CHIPS, 1), gated, [ush] * 3, [_tok(cp, 1)] * 3)

    def resid(vals, ins, outs, i):
        outs[0][...] = ins[2][...] + vals[0]

    (y,) = _mm(
        "mix_out", [u, wo, x], [_tok(D_MODEL), _wfull(cp, D_MODEL, l), _tok(D_MODEL)],
        [(_cols(0, c, cp), _pick(1, c), 0) for c in range(N_CHIPS)], 1, None, NN, (T // TM, 1, 1), resid,
        [jax.ShapeDtypeStruct((T, D_MODEL), F32)], [_tok(D_MODEL)])
    return y, (x, h, proj, o_dil, lse, o_sb, u, pd, ps)


def _mixer_bwd(dxo, gain, W, l, tabs, saved):
    x, h, proj, o_dil, lse, o_sb, u, pd, ps = saved
    T = x.shape[0]
    win, wpd, wps, wo = W["w_in"], W["w_proj_dil"], W["w_proj_sb"], W["w_out"]
    cin = win.shape[3]
    cp = wpd.shape[3]
    tk = TM
    tm = TM // 2
    row = pl.BlockSpec((tm, D_MODEL), lambda i, j, k: (i, 0))
    gd0, gs0 = COL_GD // cp, COL_GS // cp

    def dgated(vals, ins, outs, i):
        du = vals[0]
        sd = _sigmoid(ins[4][...])
        ss = _sigmoid(ins[5][...])
        outs[0][...] = (du * sd).astype(BF16)
        outs[1][...] = (du * ss).astype(BF16)
        outs[2][...] = (du * ins[2][...].astype(F32) * sd * (1.0 - sd)).astype(BF16)
        outs[3][...] = (du * ins[3][...].astype(F32) * ss * (1.0 - ss)).astype(BF16)

    ush = jax.ShapeDtypeStruct((T, D_MODEL), BF16)
    dpd, dps, dgd, dgs = _mm(
        "mix_du", [dxo, wo, pd, ps, proj, proj],
        [_tok(D_MODEL), _wspec(cp, D_MODEL, l, 1), _tok(cp, 1), _tok(cp, 1),
         pl.BlockSpec((TM, cp), lambda i, j, k: (i, gd0 + j)), pl.BlockSpec((TM, cp), lambda i, j, k: (i, gs0 + j))],
        [(0, 1, 0)], 1, None, NT, (T // TM, N_CHIPS, 1), dgated, [ush] * 4, [_tok(cp, 1)] * 4)

    def one(vals, ins, outs, i):
        outs[0][...] = vals[0].astype(BF16)

    def two(vals, ins, outs, i):
        outs[0][...] = vals[0].astype(BF16)
        outs[1][...] = vals[1].astype(BF16)

    (dwo,) = _mm(
        "mix_dwo", [u, dxo],
        [pl.BlockSpec((tk, cp), lambda i, j, k: (k, j)), pl.BlockSpec((tk, D_MODEL), lambda i, j, k: (k, 0))],
        [(0, 1, 0)], 1, (cp, D_MODEL), TN, (1, N_CHIPS, T // tk), one,
        [jax.ShapeDtypeStruct((N_CHIPS, cp, D_MODEL), BF16)],
        [pl.BlockSpec((None, cp, D_MODEL), lambda i, j, k: (j, 0, 0))])

    def plain2(vals, ins, outs, i):
        outs[0][...] = vals[0]
        outs[1][...] = vals[1]

    ash = jax.ShapeDtypeStruct((T, D_ATT), F32)
    do_dil, do_sb = _mm(
        "mix_do", [dpd, dps, wpd, wps], [_tok(cp, 2), _tok(cp, 2), _wspec(D_ATT, cp, l, 2), _wspec(D_ATT, cp, l, 2)],
        [(0, 2, 0), (1, 3, 1)], 2, (TM, D_ATT), NT, (T // TM, 1, N_CHIPS), plain2, [ash, ash], [_tok(D_ATT)] * 2)

    psh = jax.ShapeDtypeStruct((N_CHIPS, D_ATT, cp), BF16)
    pspec = pl.BlockSpec((None, D_ATT, cp), lambda i, j, k: (j, 0, 0))
    arow = pl.BlockSpec((tk, D_ATT), lambda i, j, k: (k, 0))
    dcol = pl.BlockSpec((tk, cp), lambda i, j, k: (k, j))
    dwpd, dwps = _mm(
        "mix_dwp", [o_dil, o_sb, dpd, dps], [arow, arow, dcol, dcol], [(0, 2, 0), (1, 3, 1)], 2, (D_ATT, cp), TN,
        (1, N_CHIPS, T // tk), two, [psh, psh], [pspec, pspec])

    dqs, dks, dvs = [], [], []
    for g, (window, dil) in enumerate(DIL_GROUPS):
        dq, dk, dv = _dil_bwd(proj, do_dil, o_dil, lse, g, dil)
        dqs.append(dq)
        dks.append(dk)
        dvs.append(dv)
    dq_s, dk_s, dv_s = _sb_bwd(proj, do_sb, o_sb)
    dproj = _assemble_dproj(dqs + dks, dvs + [dq_s, dk_s, dv_s], [dgd, dgs], tabs)

    dx, dgain = _mm(
        "mix_dx", [dproj, win, x, gain, dxo],
        [pl.BlockSpec((tm, N_CHIPS * cin), lambda i, j, k: (i, 0)), _wfull(D_MODEL, cin, l), row, _gain_spec(), row],
        [(_cols(0, c, cin), _pick(1, c), 0) for c in range(N_CHIPS)], 1, None, NT, (T // tm, 1, 1),
        _rms_bwd_epilogue(2, 3, 4),
        [jax.ShapeDtypeStruct((T, D_MODEL), F32), jax.ShapeDtypeStruct((8, D_MODEL), F32)],
        [row, pl.BlockSpec((8, D_MODEL), lambda i, j, k: (0, 0))])

    (dwin,) = _mm(
        "mix_dwin", [h, dproj],
        [pl.BlockSpec((tk, D_MODEL), lambda i, j, k: (k, 0)), pl.BlockSpec((tk, cin), lambda i, j, k: (k, j))],
        [(0, 1, 0)], 1, (D_MODEL, cin), TN, (1, N_CHIPS, T // tk), one,
        [jax.ShapeDtypeStruct((N_CHIPS, D_MODEL, cin), BF16)],
        [pl.BlockSpec((None, D_MODEL, cin), lambda i, j, k: (j, 0, 0))])
    return dx, dgain, dwin, dwpd, dwps, dwo


def _local_step(x, target, norms, norm_final, W):
    T = x.shape[0]
    tabs = _rope_tables(T)
    saved = []
    for l in range(DEPTH):
        x, s1 = _ffn_fwd(x, norms["norm_ffn1"][l:l + 1], W["ffn1_w_gate"], W["ffn1_w_up"], W["ffn1_w_down"], l)
        x, s2 = _mixer_fwd(x, norms["norm_mix"][l:l + 1], W, l, tabs)
        x, s3 = _ffn_fwd(x, norms["norm_ffn2"][l:l + 1], W["ffn2_w_gate"], W["ffn2_w_up"], W["ffn2_w_down"], l)
        saved.append((s1, s2, s3))
    dx, dg_final, loss = _final_loss(x, norm_final.reshape(1, D_MODEL), target)
    grads = [None] * DEPTH
    gains = [None] * DEPTH
    for l in reversed(range(DEPTH)):
        s1, s2, s3 = saved[l]
        dx, dg2, dwg2, dwu2, dwd2 = _ffn_bwd(dx, norms["norm_ffn2"][l:l + 1], W["ffn2_w_gate"], W["ffn2_w_up"],
                                             W["ffn2_w_down"], l, s3)
        dx, dgm, dwin, dwpd, dwps, dwo = _mixer_bwd(dx, norms["norm_mix"][l:l + 1], W, l, tabs, s2)
        dx, dg1, dwg1, dwu1, dwd1 = _ffn_bwd(dx, norms["norm_ffn1"][l:l + 1], W["ffn1_w_gate"], W["ffn1_w_up"],
                                             W["ffn1_w_down"], l, s1)
        grads[l] = dict(ffn1_w_gate=dwg1, ffn1_w_up=dwu1, ffn1_w_down=dwd1, w_in=dwin, w_proj_dil=dwpd,
                        w_proj_sb=dwps, w_out=dwo, ffn2_w_gate=dwg2, ffn2_w_up=dwu2, ffn2_w_down=dwd2)
        gains[l] = dict(norm_ffn1=dg1, norm_mix=dgm, norm_ffn2=dg2)
    return loss, dx, grads, gains, dg_final


def _place():
    x, y, c = lax.axis_index("x"), lax.axis_index("y"), lax.axis_index("c")
    chips = [(1 - x, y), (x, 1 - y), (1 - x, 1 - y)]
    return x, y, c, chips


def _cast_into_slot(w, me_arr):
    L, r, cw = w.shape
    tr = r // 4 if r > 256 else r

    def body(me_ref, w_ref, o_ref):
        o_ref[...] = w_ref[...].astype(BF16)

    return pl.pallas_call(
        body, name="cast_weights",
        grid_spec=pltpu.PrefetchScalarGridSpec(
            num_scalar_prefetch=1, grid=(L, r // tr),
            in_specs=[pl.BlockSpec((None, tr, cw), lambda l, i, me: (l, i, 0))],
            out_specs=pl.BlockSpec((None, None, tr, cw), lambda l, i, me: (me[0], l, i, 0))),
        out_shape=jax.ShapeDtypeStruct((N_CHIPS, L, r, cw), BF16), compiler_params=_params(),
    )(me_arr, w)


def _gather_weights(bufs):
    n = len(bufs)

    def body(*refs):
        out_refs = refs[n:2 * n]
        send_sems, recv_sems, fsend_sems, frecv_sems = refs[2 * n:]
        x, y, c, chips = _place()
        me = 2 * x + y
        sibling = (x, y, 1 - c)

        def ici(a, j, chip_id, to):
            return pltpu.make_async_remote_copy(
                src_ref=out_refs[a].at[me, c], dst_ref=out_refs[a].at[chip_id, c],
                send_sem=send_sems.at[a, j], recv_sem=recv_sems.at[a, j], device_id=to, device_id_type=MESH)

        def d2d(a, j, chip_id, layer):
            return pltpu.make_async_remote_copy(
                src_ref=out_refs[a].at[chip_id, layer], dst_ref=out_refs[a].at[chip_id, layer],
                send_sem=fsend_sems.at[a, j], recv_sem=frecv_sems.at[a, j], device_id=sibling, device_id_type=MESH)

        sends = [ici(a, j, me, (*chip, c)) for a in range(n) for j, chip in enumerate(chips)]
        for cp in sends:
            cp.start()
        passed = []
        for a in range(n):
            for j, chip in enumerate(chips):
                cid = 2 * chip[0] + chip[1]
                ici(a, j, cid, (*chip, c)).wait_recv()
                fw = d2d(a, j, cid, c)
                fw.start()
                passed.append(fw)
        for a in range(n):
            for j, chip in enumerate(chips):
                d2d(a, j, 2 * chip[0] + chip[1], 1 - c).wait_recv()
        for cp in sends + passed:
            cp.wait_send()

    any_spec = pl.BlockSpec(memory_space=pl.ANY)
    return pl.pallas_call(
        body, name="gather_weights", in_specs=[any_spec] * n, out_specs=[any_spec] * n,
        out_shape=[jax.ShapeDtypeStruct(b.shape, b.dtype) for b in bufs],
        input_output_aliases={a: a for a in range(n)},
        scratch_shapes=[pltpu.SemaphoreType.DMA((n, 3))] * 4,
    )(*bufs)


def _half(c, r):
    return pl.ds(pl.multiple_of(c * (r // 2), 8), r // 2)


def _exchange_halves(gs):
    n = len(gs)

    def body(*refs):
        g_refs, out_refs = refs[:n], refs[n:2 * n]
        send_sems, recv_sems = refs[2 * n:]
        x, y, c, _ = _place()
        cps = []
        for a in range(n):
            r = g_refs[a].shape[2]
            cps.append(pltpu.make_async_remote_copy(
                src_ref=g_refs[a].at[:, :, _half(1 - c, r), :], dst_ref=out_refs[a],
                send_sem=send_sems.at[a], recv_sem=recv_sems.at[a], device_id=(x, y, 1 - c), device_id_type=MESH))
        for cp in cps:
            cp.start()
        for cp in cps:
            cp.wait()

    any_spec = pl.BlockSpec(memory_space=pl.ANY)
    return pl.pallas_call(
        body, name="grad_to_sibling", in_specs=[any_spec] * n, out_specs=[any_spec] * n,
        out_shape=[jax.ShapeDtypeStruct((g.shape[0], g.shape[1], g.shape[2] // 2, g.shape[3]), g.dtype) for g in gs],
        scratch_shapes=[pltpu.SemaphoreType.DMA((n,))] * 2,
    )(*gs)


def _add_half(g, got, c_arr):
    n, _, r, cw = g.shape

    def body(c_ref, a_ref, b_ref, o_ref):
        o_ref[...] = (a_ref[...].astype(F32) + b_ref[...].astype(F32)).astype(BF16)

    return pl.pallas_call(
        body, name="grad_add_half",
        grid_spec=pltpu.PrefetchScalarGridSpec(
            num_scalar_prefetch=1, grid=(n, N_CHIPS),
            in_specs=[pl.BlockSpec((None, None, r // 2, cw), lambda a, k, cr: (a, k, cr[0], 0)),
                      pl.BlockSpec((None, None, r // 2, cw), lambda a, k, cr: (a, k, 0, 0))],
            out_specs=pl.BlockSpec((None, None, r // 2, cw), lambda a, k, cr: (a, k, 0, 0))),
        out_shape=jax.ShapeDtypeStruct((n, N_CHIPS, r // 2, cw), BF16), compiler_params=_params(),
    )(c_arr, g, got)


def _scatter_to_chips(ss):
    n = len(ss)

    def body(*refs):
        s_refs, out_refs = refs[:n], refs[n:2 * n]
        send_sems, recv_sems, local_sems = refs[2 * n:]
        x, y, c, chips = _place()
        me = 2 * x + y
        local = [pltpu.make_async_copy(s_refs[a].at[:, me], out_refs[a].at[:, me], local_sems.at[a])
                 for a in range(n)]
        for cp in local:
            cp.start()
        cps = []
        for a in range(n):
            for j, chip in enumerate(chips):
                cid = 2 * chip[0] + chip[1]
                cps.append(pltpu.make_async_remote_copy(
                    src_ref=s_refs[a].at[:, cid], dst_ref=out_refs[a].at[:, me],
                    send_sem=send_sems.at[a, j], recv_sem=recv_sems.at[a, j], device_id=(*chip, c),
                    device_id_type=MESH))
        for cp in cps:
            cp.start()
        for cp in cps:
            cp.wait()
        for cp in local:
            cp.wait()

    any_spec = pl.BlockSpec(memory_space=pl.ANY)
    return pl.pallas_call(
        body, name="grad_to_chips", in_specs=[any_spec] * n, out_specs=[any_spec] * n,
        out_shape=[jax.ShapeDtypeStruct(s.shape, s.dtype) for s in ss],
        scratch_shapes=[pltpu.SemaphoreType.DMA((n, 3))] * 2 + [pltpu.SemaphoreType.DMA((n,))],
    )(*ss)


def _sum_chips(got):
    n, _, rh, cw = got.shape

    def body(g_ref, o_ref):
        acc = g_ref[0].astype(F32)
        for k in range(1, N_CHIPS):
            acc = acc + g_ref[k].astype(F32)
        o_ref[...] = acc

    return pl.pallas_call(
        body, name="grad_sum_chips", grid=(n,),
        in_specs=[pl.BlockSpec((None, N_CHIPS, rh, cw), lambda a: (a, 0, 0, 0))],
        out_specs=pl.BlockSpec((None, rh, cw), lambda a: (a, 0, 0)),
        out_shape=jax.ShapeDtypeStruct((n, rh, cw), F32), compiler_params=_params(),
    )(got)


def _swap_halves(fs):
    n = len(fs)

    def body(*refs):
        f_refs, out_refs = refs[:n], refs[n:2 * n]
        send_sems, recv_sems = refs[2 * n:]
        x, y, c, _ = _place()
        cps = [pltpu.make_async_remote_copy(
            src_ref=f_refs[a], dst_ref=out_refs[a], send_sem=send_sems.at[a], recv_sem=recv_sems.at[a],
            device_id=(x, y, 1 - c), device_id_type=MESH) for a in range(n)]
        for cp in cps:
            cp.start()
        for cp in cps:
            cp.wait()

    any_spec = pl.BlockSpec(memory_space=pl.ANY)
    return pl.pallas_call(
        body, name="grad_swap_halves", in_specs=[any_spec] * n, out_specs=[any_spec] * n,
        out_shape=[jax.ShapeDtypeStruct(f.shape, f.dtype) for f in fs],
        scratch_shapes=[pltpu.SemaphoreType.DMA((n,))] * 2,
    )(*fs)


def _allreduce_rows(stats):
    def body(s_ref, o_ref, buf, send_sems, recv_sems):
        x, y, c, _ = _place()
        me = 4 * x + 2 * y + c
        buf[me] = s_ref[...]
        cps = []
        for k in range(1, 8):
            px = jnp.where(k & 4, 1 - x, x)
            py = jnp.where(k & 2, 1 - y, y)
            pc = jnp.where(k & 1, 1 - c, c)
            cps.append(pltpu.make_async_remote_copy(
                src_ref=s_ref, dst_ref=buf.at[me], send_sem=send_sems.at[k - 1], recv_sem=recv_sems.at[k - 1],
                device_id=(px, py, pc), device_id_type=MESH))
        for cp in cps:
            cp.start()
        for cp in cps:
            cp.wait()
        acc = buf[0]
        for d in range(1, 8):
            acc = acc + buf[d]
        o_ref[...] = acc

    vm = pl.BlockSpec(memory_space=pltpu.VMEM)
    return pl.pallas_call(
        body, name="allreduce_rows", in_specs=[vm], out_specs=vm,
        out_shape=jax.ShapeDtypeStruct(stats.shape, F32),
        scratch_shapes=[pltpu.VMEM((8,) + stats.shape, F32), pltpu.SemaphoreType.DMA((7,)),
                        pltpu.SemaphoreType.DMA((7,))],
    )(stats)


def _adamw_math(w, g, m, v):
    m = ADAM_B1 * m + (1.0 - ADAM_B1) * g
    v = ADAM_B2 * v + (1.0 - ADAM_B2) * (g * g)
    m_hat = m / (1.0 - ADAM_B1 ** ADAM_STEP)
    v_hat = v / (1.0 - ADAM_B2 ** ADAM_STEP)
    delta = -ADAM_LR * (m_hat / (jnp.sqrt(v_hat) + ADAM_EPS) + ADAM_WD * w)
    return delta, m, v


def _adamw(w, m, v, mine, theirs, first, c_arr):
    L, r, cw = w.shape
    tr = r // 4 if r > 256 else r // 2
    nblk = (r // 2) // tr

    def body(c_ref, w_ref, m_ref, v_ref, a_ref, b_ref, go_ref, d_ref, mo_ref, vo_ref):
        g = jnp.where(pl.program_id(1) == c_ref[0], a_ref[...], b_ref[...])
        delta, mn, vn = _adamw_math(w_ref[...], g, m_ref[...], v_ref[...])
        go_ref[...] = g
        d_ref[...] = delta
        mo_ref[...] = mn
        vo_ref[...] = vn

    blk = pl.BlockSpec((None, tr, cw), lambda l, hh, i, cr: (l, hh * nblk + i, 0))
    half = pl.BlockSpec((None, tr, cw), lambda l, hh, i, cr: (first + l, i, 0))
    sh = jax.ShapeDtypeStruct(w.shape, F32)
    return pl.pallas_call(
        body, name="adamw",
        grid_spec=pltpu.PrefetchScalarGridSpec(
            num_scalar_prefetch=1, grid=(L, 2, nblk), in_specs=[blk, blk, blk, half, half], out_specs=[blk] * 4),
        out_shape=[sh] * 4, compiler_params=_params(),
    )(c_arr, w, m, v, mine, theirs)


def _adamw_rows(w, m, v, g):
    def body(w_ref, m_ref, v_ref, g_ref, d_ref, mo_ref, vo_ref):
        delta, mn, vn = _adamw_math(w_ref[...], g_ref[...], m_ref[...], v_ref[...])
        d_ref[...] = delta
        mo_ref[...] = mn
        vo_ref[...] = vn

    vm = pl.BlockSpec(memory_space=pltpu.VMEM)
    sh = jax.ShapeDtypeStruct(w.shape, F32)
    return pl.pallas_call(body, name="adamw_rows", in_specs=[vm] * 4, out_specs=[vm] * 3, out_shape=[sh] * 3)(w, m, v, g)


GROUPS = (("ffn1_w_gate", "ffn1_w_up", "ffn2_w_gate", "ffn2_w_up"), ("ffn1_w_down", "ffn2_w_down"),
          ("w_in",), ("w_proj_dil", "w_proj_sb"), ("w_out",))


def _pick_row(blocks):
    row = lax.broadcasted_iota(jnp.int32, (8, D_MODEL), 0)
    out = jnp.zeros((8, D_MODEL), F32)
    for i, b in enumerate(blocks):
        out = out + jnp.where(row == i, b, 0.0)
    return out


def kernel(x, norm_ffn1, ffn1_w_gate, ffn1_w_up, ffn1_w_down, norm_mix, w_in, w_proj_dil, w_proj_sb, w_out, norm_ffn2, ffn2_w_gate, ffn2_w_up, ffn2_w_down, norm_final, loss_target, m_norm_ffn1, m_ffn1_w_gate, m_ffn1_w_up, m_ffn1_w_down, m_norm_mix, m_w_in, m_w_proj_dil, m_w_proj_sb, m_w_out, m_norm_ffn2, m_ffn2_w_gate, m_ffn2_w_up, m_ffn2_w_down, m_norm_final, v_norm_ffn1, v_ffn1_w_gate, v_ffn1_w_up, v_ffn1_w_down, v_norm_mix, v_w_in, v_w_proj_dil, v_w_proj_sb, v_w_out, v_norm_ffn2, v_ffn2_w_gate, v_ffn2_w_up, v_ffn2_w_down, v_norm_final):
    given = dict(locals())
    weights = {n: given[n] for n in WEIGHT_NAMES}
    norms = {n: given[n] for n in NORM_NAMES}

    c_arr = lax.axis_index("c").astype(jnp.int32).reshape(1)
    me_arr = (2 * lax.axis_index("x") + lax.axis_index("y")).astype(jnp.int32).reshape(1)
    gathered = _gather_weights([_cast_into_slot(weights[n], me_arr) for n in WEIGHT_NAMES])
    W = dict(zip(WEIGHT_NAMES, gathered))
    loss_blk, grad_x, grads, gains, dg_final = _local_step(x[0], loss_target[0], norms, norm_final, W)

    stacked = [jnp.stack([grads[l][n] for n in grp for l in range(DEPTH)]) for grp in GROUPS]
    from_sibling = _exchange_halves(stacked)
    chip_sums = [_add_half(g, got, c_arr) for g, got in zip(stacked, from_sibling)]
    from_chips = _scatter_to_chips(chip_sums)
    mine = [_sum_chips(got) for got in from_chips]
    theirs = _swap_halves(mine)

    out = {"grad_x": grad_x[None]}
    for grp, ga, gb in zip(GROUPS, mine, theirs):
        for i, n in enumerate(grp):
            g, d, mn, vn = _adamw(weights[n], given["m_" + n], given["v_" + n], ga, gb, i * DEPTH, c_arr)
            out["grad_" + n], out["delta_" + n], out["new_m_" + n], out["new_v_" + n] = g, d, mn, vn

    rows = [gains[l][n] for n in NORM_NAMES for l in range(DEPTH)] + [dg_final, loss_blk]
    total = _allreduce_rows(_pick_row(rows))
    out["loss"] = total[7, 0]
    wn = jnp.concatenate([given[n] for n in NORM_NAMES] + [norm_final[None], jnp.zeros((1, D_MODEL), F32)])
    mn_ = jnp.concatenate([given["m_" + n] for n in NORM_NAMES] + [m_norm_final[None], jnp.zeros((1, D_MODEL), F32)])
    vn_ = jnp.concatenate([given["v_" + n] for n in NORM_NAMES] + [v_norm_final[None], jnp.ones((1, D_MODEL), F32)])
    d_n, m_n, v_n = _adamw_rows(wn, mn_, vn_, total)
    for i, n in enumerate(NORM_NAMES):
        sl = slice(i * DEPTH, (i + 1) * DEPTH)
        out["grad_" + n], out["delta_" + n], out["new_m_" + n], out["new_v_" + n] = total[sl], d_n[sl], m_n[sl], v_n[sl]
    out["grad_norm_final"], out["delta_norm_final"] = total[6], d_n[6]
    out["new_m_norm_final"], out["new_v_norm_final"] = m_n[6], v_n[6]

    order = WEIGHT_NAMES
    del order
    names = ["norm_ffn1", "ffn1_w_gate", "ffn1_w_up", "ffn1_w_down", "norm_mix", "w_in", "w_proj_dil", "w_proj_sb",
             "w_out", "norm_ffn2", "ffn2_w_gate", "ffn2_w_up", "ffn2_w_down", "norm_final"]
    return (out["loss"], out["grad_x"], *[out["grad_" + n] for n in names], *[out["delta_" + n] for n in names],
            *[out["new_m_" + n] for n in names], *[out["new_v_" + n] for n in names])
```

```python
import functools

import jax
import jax.numpy as jnp
from jax import lax
from jax.experimental import pallas as pl
from jax.experimental.pallas import tpu as pltpu

F32 = jnp.float32
BF16 = jnp.bfloat16

D_MODEL = 1024
DEPTH = 2
N_CHIPS = 4
HEAD_DIM = 64
ROPE_DIM = 16
ROPE_THETA = 500000.0
DIL_GROUPS = ((128, 1), (512, 4), (2048, 16))
SPAN = 128
Q_BLOCK = 128
RMS_EPS = 1e-6
D_ATT = 256
COL_QS = 2304
COL_GD = 3072
COL_GS = 4096
ADAM_LR, ADAM_B1, ADAM_B2, ADAM_EPS, ADAM_WD, ADAM_STEP = 0.001, 0.9, 0.999, 1e-08, 0.01, 10

VMEM_LIMIT = 52 * 1024 * 1024
TM = 512
NEG = -1e30

NN = (((1,), (0,)), ((), ()))
NT = (((1,), (1,)), ((), ()))
TN = (((0,), (0,)), ((), ()))
MESH = pl.DeviceIdType.MESH

WEIGHT_NAMES = ("ffn1_w_gate", "ffn1_w_up", "ffn1_w_down", "w_in", "w_proj_dil",
                "w_proj_sb", "w_out", "ffn2_w_gate", "ffn2_w_up", "ffn2_w_down")
NORM_NAMES = ("norm_ffn1", "norm_mix", "norm_ffn2")


def _params(**kw):
    return pltpu.CompilerParams(vmem_limit_bytes=VMEM_LIMIT, **kw)


def _sigmoid(x):
    return 0.5 * jnp.tanh(0.5 * x) + 0.5


def _mm_body(pairs, n_in, n_out, n_acc, dims, nk, i_axis, epilogue, *refs):
    ins = refs[:n_in]
    outs = refs[n_in:n_in + n_out]
    accs = refs[n_in + n_out:]
    i = pl.program_id(i_axis)
    k = pl.program_id(2)

    def operand(a):
        return (a(ins) if callable(a) else ins[a][...]).astype(BF16)

    def dot(ia, ib):
        return lax.dot_general(operand(ia), operand(ib), dims, preferred_element_type=F32)

    if nk == 1:
        parts = [None] * n_acc
        for ia, ib, ic in pairs:
            parts[ic] = dot(ia, ib) if parts[ic] is None else parts[ic] + dot(ia, ib)
        epilogue(parts, ins, outs, i)
        return

    @pl.when(k == 0)
    def _():
        for c in range(n_acc):
            accs[c][...] = jnp.zeros_like(accs[c])

    for ia, ib, ic in pairs:
        accs[ic][...] += dot(ia, ib)

    @pl.when(k == nk - 1)
    def _():
        epilogue([a[...] for a in accs], ins, outs, i)


def _j_outer(spec):
    f = spec.index_map
    return pl.BlockSpec(spec.block_shape, lambda j, i, k: f(i, j, k))


def _mm(name, ins, in_specs, pairs, n_acc, acc_shape, dims, grid, epilogue, out_shapes, out_specs, j_outer=False):
    nk = grid[2]
    if j_outer:
        grid = (grid[1], grid[0], grid[2])
        in_specs = [_j_outer(s) for s in in_specs]
        out_specs = [_j_outer(s) for s in out_specs]
    scratch = [pltpu.VMEM(acc_shape, F32) for _ in range(n_acc)] if nk > 1 else []
    body = functools.partial(_mm_body, tuple(pairs), len(ins), len(out_shapes), n_acc, dims, nk,
                             1 if j_outer else 0, epilogue)
    return pl.pallas_call(
        body, name=name, grid=grid, in_specs=in_specs, out_specs=out_specs, out_shape=out_shapes,
        scratch_shapes=scratch,
        compiler_params=_params(dimension_semantics=("arbitrary", "arbitrary", "arbitrary")),
    )(*ins)


def _wspec(r, c, l, by):
    if by == 1:
        return pl.BlockSpec((None, None, r, c), lambda i, j, k: (j, l, 0, 0))
    return pl.BlockSpec((None, None, r, c), lambda i, j, k: (k, l, 0, 0))


def _rms_bwd_epilogue(x_idx, g_idx, dxo_idx):
    def ep(vals, ins, outs, i):
        dh = vals[0]
        x = ins[x_idx][...]
        g = ins[g_idx][...]
        rstd = lax.rsqrt(jnp.mean(x * x, axis=-1, keepdims=True) + RMS_EPS)
        xhat = x * rstd
        dxhat = dh * g
        dx = rstd * (dxhat - xhat * jnp.mean(dxhat * xhat, axis=-1, keepdims=True))
        outs[0][...] = ins[dxo_idx][...] + dx
        dg = jnp.broadcast_to(jnp.sum(dh * xhat, axis=0, keepdims=True), outs[1].shape)

        @pl.when(i == 0)
        def _():
            outs[1][...] = dg

        @pl.when(i > 0)
        def _():
            outs[1][...] += dg
    return ep


def _rms_fwd(x, gain):
    T = x.shape[0]

    def body(x_ref, g_ref, h_ref):
        xv = x_ref[...]
        h = xv * lax.rsqrt(jnp.mean(xv * xv, axis=-1, keepdims=True) + RMS_EPS)
        h_ref[...] = (h * g_ref[...]).astype(BF16)

    return pl.pallas_call(
        body, name="rms_fwd", grid=(T // TM,),
        in_specs=[pl.BlockSpec((TM, D_MODEL), lambda i: (i, 0)), pl.BlockSpec((1, D_MODEL), lambda i: (0, 0))],
        out_specs=pl.BlockSpec((TM, D_MODEL), lambda i: (i, 0)),
        out_shape=jax.ShapeDtypeStruct((T, D_MODEL), BF16), compiler_params=_params(),
    )(x, gain)


def _rope_tables(T):
    pos = jnp.arange(T, dtype=F32)
    inv_freq = ROPE_THETA ** (-jnp.arange(0, ROPE_DIM, 2, dtype=F32) / ROPE_DIM)
    ang = pos[:, None] * inv_freq[None, :]
    cos, sin = jnp.cos(ang), jnp.sin(ang)
    half = ROPE_DIM // 2
    one = jnp.ones((T, HEAD_DIM - ROPE_DIM), F32)
    zero = jnp.zeros((T, HEAD_DIM - ROPE_DIM), F32)
    zh = jnp.zeros((T, half), F32)
    c = jnp.concatenate([cos, cos, one], axis=1)
    s1 = jnp.concatenate([-sin, zh, zero], axis=1)
    s2 = jnp.concatenate([zh, sin, zero], axis=1)
    return tuple(jnp.concatenate([t, t], axis=1) for t in (c, s1, s2))


def _rope_fwd(xv, c, s1, s2):
    w = xv.shape[1]
    half = ROPE_DIM // 2
    return xv * c + pltpu.roll(xv, w - half, 1) * s1 + pltpu.roll(xv, half, 1) * s2


def _rope_bwd(dy, c, s1, s2):
    w = dy.shape[1]
    half = ROPE_DIM // 2
    return dy * c + pltpu.roll(dy * s1, half, 1) + pltpu.roll(dy * s2, w - half, 1)


def _assemble_dproj(dqk, rest, gates, tabs):
    T = gates[0].shape[0]
    n_qk, n_rest = len(dqk), len(rest)
    width = (n_qk + n_rest) * D_ATT + 2 * D_MODEL

    def body(*refs):
        ins, (c_ref, s1_ref, s2_ref), o_ref = refs[:n_qk + n_rest + 2], refs[-4:-1], refs[-1]
        c = jnp.concatenate([c_ref[...]] * 2, axis=1)
        s1 = jnp.concatenate([s1_ref[...]] * 2, axis=1)
        s2 = jnp.concatenate([s2_ref[...]] * 2, axis=1)
        for b in range(n_qk + n_rest):
            v = ins[b][...]
            if b < n_qk:
                v = _rope_bwd(v, c, s1, s2)
            o_ref[:, b * D_ATT:(b + 1) * D_ATT] = v.astype(BF16)
        off = (n_qk + n_rest) * D_ATT
        o_ref[:, off:off + D_MODEL] = ins[-2][...]
        o_ref[:, off + D_MODEL:] = ins[-1][...]

    att = pl.BlockSpec((TM, D_ATT), lambda i: (i, 0))
    wide = pl.BlockSpec((TM, D_MODEL), lambda i: (i, 0))
    tab = pl.BlockSpec((TM, 128), lambda i: (i, 0))
    return pl.pallas_call(
        body, name="assemble_dproj", grid=(T // TM,),
        in_specs=[att] * (n_qk + n_rest) + [wide, wide, tab, tab, tab],
        out_specs=pl.BlockSpec((TM, width), lambda i: (i, 0)),
        out_shape=jax.ShapeDtypeStruct((T, width), BF16), compiler_params=_params(),
    )(*dqk, *rest, *gates, *tabs)


def _dil_merge(os_, lses):
    T = os_[0].shape[0]

    def body(o0, o1, o2, l0, l1, l2, o_ref, lse_ref):
        a, b, c = l0[...], l1[...], l2[...]
        m = jnp.maximum(jnp.maximum(a, b), c)
        ea, eb, ec = jnp.exp(a - m), jnp.exp(b - m), jnp.exp(c - m)
        den = ea + eb + ec
        o_ref[...] = (ea * o0[...] + eb * o1[...] + ec * o2[...]) / den
        lse_ref[...] = m + jnp.log(den)

    blk = pl.BlockSpec((TM, D_ATT), lambda i: (i, 0))
    sh = jax.ShapeDtypeStruct((T, D_ATT), F32)
    return pl.pallas_call(
        body, name="dil_merge", grid=(T // TM,), in_specs=[blk] * 6, out_specs=[blk, blk],
        out_shape=[sh, sh], compiler_params=_params(),
    )(*os_, *lses)


def _final_loss(x, gain, target):
    T = x.shape[0]

    def body(x_ref, g_ref, t_ref, dx_ref, dg_ref, loss_ref):
        xv = x_ref[...]
        g = g_ref[...]
        rstd = lax.rsqrt(jnp.mean(xv * xv, axis=-1, keepdims=True) + RMS_EPS)
        xhat = xv * rstd
        err = xhat * g - t_ref[...]
        loss = 0.5 * jnp.sum(jnp.mean(err * err, axis=-1, keepdims=True), axis=0, keepdims=True)
        dy = err * (1.0 / D_MODEL)
        dxhat = dy * g
        dx_ref[...] = rstd * (dxhat - xhat * jnp.mean(dxhat * xhat, axis=-1, keepdims=True))
        dg = jnp.broadcast_to(jnp.sum(dy * xhat, axis=0, keepdims=True), dg_ref.shape)
        ls = jnp.broadcast_to(loss, loss_ref.shape)

        @pl.when(pl.program_id(0) == 0)
        def _():
            dg_ref[...] = dg
            loss_ref[...] = ls

        @pl.when(pl.program_id(0) > 0)
        def _():
            dg_ref[...] += dg
            loss_ref[...] += ls

    blk = pl.BlockSpec((TM, D_MODEL), lambda i: (i, 0))
    row = pl.BlockSpec((1, D_MODEL), lambda i: (0, 0))
    acc = pl.BlockSpec((8, D_MODEL), lambda i: (0, 0))
    return pl.pallas_call(
        body, name="final_loss", grid=(T // TM,), in_specs=[blk, row, blk], out_specs=[blk, acc, acc],
        out_shape=[jax.ShapeDtypeStruct((T, D_MODEL), F32), jax.ShapeDtypeStruct((8, D_MODEL), F32),
                   jax.ShapeDtypeStruct((8, D_MODEL), F32)],
        compiler_params=_params(dimension_semantics=("arbitrary",)),
    )(x, gain, target)


def _pair_masks():
    lane = lax.broadcasted_iota(jnp.int32, (SPAN, 128), 1)
    return [lane < HEAD_DIM, lane >= HEAD_DIM]


def _stack_heads(x, masks):
    return jnp.concatenate([jnp.where(m, x, 0.0) for m in masks], axis=0)


def _unstack_heads(y, masks):
    rows = y.shape[0] // len(masks)
    out = jnp.where(masks[0], y[:rows], 0.0)
    for h in range(1, len(masks)):
        out = out + jnp.where(masks[h], y[rows * h:rows * (h + 1)], 0.0)
    return out


def _dil_rows(idx, d):
    u = idx // d
    r = idx - u * d
    own = pl.ds(u * (SPAN * d) + r, SPAN, stride=d) if d > 1 else pl.ds(pl.multiple_of(u * SPAN, SPAN), SPAN)
    up = jnp.maximum(u - 1, 0)
    prev = pl.ds(up * (SPAN * d) + r, SPAN, stride=d) if d > 1 else pl.ds(pl.multiple_of(up * SPAN, SPAN), SPAN)
    return u, own, prev


def _dil_valid(u):
    qi = lax.broadcasted_iota(jnp.int32, (2 * SPAN, 2 * SPAN), 0) & (SPAN - 1)
    kj = lax.broadcasted_iota(jnp.int32, (2 * SPAN, 2 * SPAN), 1)
    in_prev = (kj < SPAN) & (kj >= qi + jnp.where(u > 0, 0, SPAN))
    return in_prev | ((kj >= SPAN) & (kj - SPAN <= qi))


def _dil_keys(ref, own, prev):
    return jnp.concatenate([ref[prev, :], ref[own, :]], axis=0).astype(BF16)


def _dil_fwd(proj, g, d):
    T = proj.shape[0]
    n_iter = T // SPAN

    def body(q_ref, k_ref, v_ref, o_ref, lse_ref):
        masks = _pair_masks()

        def step(idx, carry):
            u, own, prev = _dil_rows(idx, d)
            qs = _stack_heads(q_ref[own, :] * (HEAD_DIM ** -0.5), masks).astype(BF16)
            kk = _dil_keys(k_ref, own, prev)
            vv = _dil_keys(v_ref, own, prev)
            s = jnp.where(_dil_valid(u), lax.dot_general(qs, kk, NT, preferred_element_type=F32), NEG)
            m = jnp.max(s, axis=1, keepdims=True)
            p = jnp.exp(s - m)
            den = jnp.sum(p, axis=1, keepdims=True)
            pv = lax.dot_general(p.astype(BF16), vv, NN, preferred_element_type=F32) / den
            o_ref[own, :] = _unstack_heads(pv, masks)
            lse_ref[own, :] = _unstack_heads(jnp.broadcast_to(m + jnp.log(den), pv.shape), masks)
            return carry

        lax.fori_loop(0, n_iter, step, 0, unroll=2)

    def col(b):
        return pl.BlockSpec((T, 128), lambda p: (0, b + p))

    sh = jax.ShapeDtypeStruct((T, D_ATT), F32)
    out = pl.BlockSpec((T, 128), lambda p: (0, p))
    return pl.pallas_call(
        body, name=f"dil_fwd_d{d}", grid=(2,),
        in_specs=[col(2 * g), col(6 + 2 * g), col(12 + 2 * g)], out_specs=[out, out], out_shape=[sh, sh],
        compiler_params=_params(dimension_semantics=("arbitrary",)),
    )(proj, proj, proj)


def _dil_bwd(proj, do, o_dil, lse, g, d):
    T = proj.shape[0]
    n_iter = T // SPAN

    def body(q_ref, k_ref, v_ref, do_ref, o_ref, lse_ref, dq_ref, dk_ref, dv_ref):
        masks = _pair_masks()
        head_lanes = jnp.concatenate(masks, axis=0)

        def step(idx, carry):
            u, own, prev = _dil_rows(idx, d)
            qs = _stack_heads(q_ref[own, :] * (HEAD_DIM ** -0.5), masks).astype(BF16)
            kk = _dil_keys(k_ref, own, prev)
            vv = _dil_keys(v_ref, own, prev)
            dom = _stack_heads(do_ref[own, :], masks)
            dos = dom.astype(BF16)
            delta = jnp.sum(dom * jnp.concatenate([o_ref[own, :]] * 2, axis=0), axis=1, keepdims=True)
            lrow = jnp.max(jnp.where(head_lanes, jnp.concatenate([lse_ref[own, :]] * 2, axis=0), NEG),
                           axis=1, keepdims=True)
            s = lax.dot_general(qs, kk, NT, preferred_element_type=F32)
            p = jnp.where(_dil_valid(u), jnp.exp(s - lrow), 0.0)
            dp = lax.dot_general(dos, vv, NT, preferred_element_type=F32)
            ds = (p * (dp - delta)).astype(BF16)
            dq = lax.dot_general(ds, kk, NN, preferred_element_type=F32)
            dkk = lax.dot_general(ds, qs, TN, preferred_element_type=F32)
            dvv = lax.dot_general(p.astype(BF16), dos, TN, preferred_element_type=F32)
            dq_ref[own, :] = _unstack_heads(dq, masks) * (HEAD_DIM ** -0.5)
            dk_ref[own, :] = dkk[SPAN:]
            dv_ref[own, :] = dvv[SPAN:]
            dk_ref[prev, :] = dk_ref[prev, :] + dkk[:SPAN]
            dv_ref[prev, :] = dv_ref[prev, :] + dvv[:SPAN]
            return carry

        lax.fori_loop(0, n_iter, step, 0, unroll=2)

    def col(b):
        return pl.BlockSpec((T, 128), lambda p: (0, b + p))

    sh = jax.ShapeDtypeStruct((T, D_ATT), F32)
    return pl.pallas_call(
        body, name=f"dil_bwd_d{d}", grid=(2,),
        in_specs=[col(2 * g), col(6 + 2 * g), col(12 + 2 * g), col(0), col(0), col(0)],
        out_specs=[col(0), col(0), col(0)], out_shape=[sh, sh, sh],
        compiler_params=_params(dimension_semantics=("arbitrary",)),
    )(proj, proj, proj, do, o_dil, lse)


SB_KT = 512


def _sb_tri(strict):
    a = lax.broadcasted_iota(jnp.int32, (Q_BLOCK, Q_BLOCK), 0)
    b = lax.broadcasted_iota(jnp.int32, (Q_BLOCK, Q_BLOCK), 1)
    return jnp.where((a > b) if strict else (a >= b), 1.0, 0.0).astype(BF16)


def _suffix(x, c, tri):
    r = x.shape[0]
    nb = x.shape[1] // Q_BLOCK
    blocks = [x[:, Q_BLOCK * b:Q_BLOCK * (b + 1)] for b in range(nb)]
    hi = [b.astype(BF16) for b in blocks]
    lo = [(b - h.astype(F32)).astype(BF16) for b, h in zip(blocks, hi)]
    y = lax.dot_general(jnp.concatenate(hi + lo, axis=0), tri, NN, preferred_element_type=F32)
    outs = [None] * nb
    run = c
    for b in reversed(range(nb)):
        outs[b] = run + y[r * b:r * (b + 1)] + y[r * (nb + b):r * (nb + b + 1)]
        run = run + jnp.sum(blocks[b], axis=1, keepdims=True)
    return jnp.concatenate(outs, axis=1), run


def _sb_tile(qs, kb, past, c, tri):
    z = lax.dot_general(qs, kb, NT, preferred_element_type=F32)
    lsz = jnp.minimum(z, 0.0) - jnp.log(1.0 + jnp.exp(-jnp.abs(z)))
    lk = lsz - z
    if past is not None:
        lk = jnp.where(past, lk, 0.0)
    after, c_new = _suffix(lk, c, tri)
    w = jnp.exp(lsz + after)
    if past is not None:
        w = jnp.where(past, w, 0.0)
    return z, lsz, w, c_new


SB_HEADS = D_ATT // HEAD_DIM
SB_ROWS = SB_HEADS * Q_BLOCK


def _sb_past(i, t):
    row = lax.broadcasted_iota(jnp.int32, (SB_ROWS, SB_KT), 0) & (Q_BLOCK - 1)
    col = lax.broadcasted_iota(jnp.int32, (SB_ROWS, SB_KT), 1)
    return col + t * SB_KT < row + i * Q_BLOCK


def _sb_head_masks():
    lane = lax.broadcasted_iota(jnp.int32, (Q_BLOCK, D_ATT), 1)
    return [(lane >= HEAD_DIM * h) & (lane < HEAD_DIM * (h + 1)) for h in range(SB_HEADS)]


def _sb_rows(t):
    return pl.ds(pl.multiple_of(t * SB_KT, SB_KT), SB_KT)


def _sb_fwd(proj):
    T = proj.shape[0]

    def body(q_ref, k_ref, v_ref, o_ref):
        i = pl.program_id(0)
        masks = _sb_head_masks()
        tri = _sb_tri(True)
        qs = _stack_heads(q_ref[...] * (HEAD_DIM ** -0.5), masks).astype(BF16)
        n_tiles = (i * Q_BLOCK) // SB_KT + 1

        def tile(t, carry, masked):
            kb = k_ref[_sb_rows(t), :].astype(BF16)
            vb = v_ref[_sb_rows(t), :].astype(BF16)
            acc, c = carry
            _, _, w, c = _sb_tile(qs, kb, _sb_past(i, t) if masked else None, c, tri)
            pv = lax.dot_general(w.astype(BF16), vb, NN, preferred_element_type=F32)
            return acc + _unstack_heads(pv, masks), c

        carry = tile(n_tiles - 1, (jnp.zeros((Q_BLOCK, D_ATT), F32), jnp.zeros((SB_ROWS, 1), F32)), True)
        carry = lax.fori_loop(0, n_tiles - 1, lambda tt, cr: tile(n_tiles - 2 - tt, cr, False), carry)
        o_ref[...] = carry[0]

    cb = COL_QS // D_ATT
    return pl.pallas_call(
        body, name="sb_fwd", grid=(T // Q_BLOCK,),
        in_specs=[pl.BlockSpec((Q_BLOCK, D_ATT), lambda i: (i, cb)),
                  pl.BlockSpec((T, D_ATT), lambda i: (0, cb + 1)),
                  pl.BlockSpec((T, D_ATT), lambda i: (0, cb + 2))],
        out_specs=pl.BlockSpec((Q_BLOCK, D_ATT), lambda i: (i, 0)),
        out_shape=jax.ShapeDtypeStruct((T, D_ATT), F32),
        compiler_params=_params(dimension_semantics=("arbitrary",)),
    )(proj, proj, proj)


def _sb_bwd(proj, do, o):
    T = proj.shape[0]

    def body(q_ref, k_ref, v_ref, do_ref, o_ref, dq_ref, dk_ref, dv_ref):
        i = pl.program_id(0)
        masks = _sb_head_masks()
        tri = _sb_tri(True)
        tri_incl = _sb_tri(False)

        @pl.when(i == 0)
        def _():
            dk_ref[...] = jnp.zeros_like(dk_ref)
            dv_ref[...] = jnp.zeros_like(dv_ref)

        qs = _stack_heads(q_ref[...] * (HEAD_DIM ** -0.5), masks).astype(BF16)
        dos = _stack_heads(do_ref[...], masks).astype(BF16)
        delta = jnp.sum(dos.astype(F32) * jnp.concatenate([o_ref[...]] * SB_HEADS, axis=0), axis=1, keepdims=True)
        n_tiles = (i * Q_BLOCK) // SB_KT + 1

        def tile(t, carry, masked):
            rows = _sb_rows(t)
            kb = k_ref[rows, :].astype(BF16)
            vb = v_ref[rows, :].astype(BF16)
            past = _sb_past(i, t) if masked else None
            dq, c, ce = carry
            z, lsz, w, c = _sb_tile(qs, kb, past, c, tri)
            gv = lax.dot_general(dos, vb, NT, preferred_element_type=F32)
            wb = w.astype(BF16)
            e = wb.astype(F32) * gv
            suf, ce = _suffix(e, ce, tri_incl)
            dz = e * jnp.exp(lsz - z) - (delta - suf) * jnp.exp(lsz)
            if masked:
                dz = jnp.where(past, dz, 0.0)
            dzb = dz.astype(BF16)
            dq = dq + _unstack_heads(lax.dot_general(dzb, kb, NN, preferred_element_type=F32), masks)
            dk_ref[rows, :] = dk_ref[rows, :] + lax.dot_general(dzb, qs, TN, preferred_element_type=F32)
            dv_ref[rows, :] = dv_ref[rows, :] + lax.dot_general(wb, dos, TN, preferred_element_type=F32)
            return dq, c, ce

        zcol = jnp.zeros((SB_ROWS, 1), F32)
        carry = tile(n_tiles - 1, (jnp.zeros((Q_BLOCK, D_ATT), F32), zcol, zcol), True)
        carry = lax.fori_loop(0, n_tiles - 1, lambda tt, cr: tile(n_tiles - 2 - tt, cr, False), carry)
        dq_ref[...] = carry[0] * (HEAD_DIM ** -0.5)

    cb = COL_QS // D_ATT
    blk = pl.BlockSpec((Q_BLOCK, D_ATT), lambda i: (i, 0))
    full = pl.BlockSpec((T, D_ATT), lambda i: (0, 0))
    sh = jax.ShapeDtypeStruct((T, D_ATT), F32)
    return pl.pallas_call(
        body, name="sb_bwd", grid=(T // Q_BLOCK,),
        in_specs=[pl.BlockSpec((Q_BLOCK, D_ATT), lambda i: (i, cb)),
                  pl.BlockSpec((T, D_ATT), lambda i: (0, cb + 1)),
                  pl.BlockSpec((T, D_ATT), lambda i: (0, cb + 2)), blk, blk],
        out_specs=[blk, full, full], out_shape=[sh, sh, sh],
        compiler_params=_params(dimension_semantics=("arbitrary",)),
    )(proj, proj, proj, do, o)


def _tok(c, by=None):
    if by is None:
        return pl.BlockSpec((TM, c), lambda i, j, k: (i, 0))
    if by == 1:
        return pl.BlockSpec((TM, c), lambda i, j, k: (i, j))
    return pl.BlockSpec((TM, c), lambda i, j, k: (i, k))


def _chunked(c, by):
    if by == 1:
        return pl.BlockSpec((None, TM, c), lambda i, j, k: (j, i, 0))
    return pl.BlockSpec((None, TM, c), lambda i, j, k: (k, i, 0))


def _gain_spec():
    return pl.BlockSpec((1, D_MODEL), lambda i, j, k: (0, 0))


def _all_chunks(rows, c):
    return pl.BlockSpec((N_CHIPS, rows, c), lambda i, j, k: (0, i, 0))


def _wfull(r, c, l):
    return pl.BlockSpec((N_CHIPS, None, r, c), lambda i, j, k: (0, l, 0, 0))


def _pick(idx, c):
    return lambda ins: ins[idx][c]


def _cols(idx, c, w):
    return lambda ins: ins[idx][:, c * w:(c + 1) * w]


def _ffn_fwd(x, gain, wg, wu, wd, l):
    T = x.shape[0]
    ffs = wg.shape[3]
    h = _rms_fwd(x, gain)

    def swiglu(vals, ins, outs, i):
        gt, up = vals
        outs[0][...] = gt.astype(BF16)
        outs[1][...] = up.astype(BF16)
        outs[2][...] = (gt * _sigmoid(gt) * up).astype(BF16)

    csh = jax.ShapeDtypeStruct((N_CHIPS, T, ffs), BF16)
    gate, up, act = _mm(
        "ffn_up", [h, wg, wu], [_tok(D_MODEL), _wspec(D_MODEL, ffs, l, 1), _wspec(D_MODEL, ffs, l, 1)],
        [(0, 1, 0), (0, 2, 1)], 2, None, NN, (T // TM, N_CHIPS, 1), swiglu,
        [csh, csh, csh], [_chunked(ffs, 1)] * 3, j_outer=True)

    def resid(vals, ins, outs, i):
        outs[0][...] = ins[2][...] + 0.5 * vals[0]

    (y,) = _mm(
        "ffn_down", [act, wd, x], [_all_chunks(TM, ffs), _wfull(ffs, D_MODEL, l), _tok(D_MODEL)],
        [(_pick(0, c), _pick(1, c), 0) for c in range(N_CHIPS)], 1, None, NN, (T // TM, 1, 1), resid,
        [jax.ShapeDtypeStruct((T, D_MODEL), F32)], [_tok(D_MODEL)])
    return y, (x, h, gate, up, act)


def _ffn_bwd(dxo, gain, wg, wu, wd, l, saved):
    x, h, gate, up, act = saved
    T = x.shape[0]
    ffs = wg.shape[3]
    tk = TM
    tm = TM // 2

    def dswiglu(vals, ins, outs, i):
        for c in range(N_CHIPS):
            da = 0.5 * vals[c]
            gt = ins[2][c].astype(F32)
            u = ins[3][c].astype(F32)
            s = _sigmoid(gt)
            outs[0][c] = (da * u * (s * (1.0 + gt * (1.0 - s)))).astype(BF16)
            outs[1][c] = (da * (gt * s)).astype(BF16)

    csh = jax.ShapeDtypeStruct((N_CHIPS, T, ffs), BF16)
    row = pl.BlockSpec((tm, D_MODEL), lambda i, j, k: (i, 0))
    dgate, dup = _mm(
        "ffn_dact", [dxo, wd, gate, up],
        [row, _wfull(ffs, D_MODEL, l), _all_chunks(tm, ffs), _all_chunks(tm, ffs)],
        [(0, _pick(1, c), c) for c in range(N_CHIPS)], N_CHIPS, None, NT, (T // tm, 1, 1), dswiglu,
        [csh, csh], [_all_chunks(tm, ffs)] * 2)

    def halves(vals, ins, outs, i):
        for c in range(N_CHIPS):
            outs[0][c] = (0.5 * vals[c]).astype(BF16)

    def casts(vals, ins, outs, i):
        for c in range(N_CHIPS):
            outs[0][c] = vals[c].astype(BF16)

    tok_k = pl.BlockSpec((tk, D_MODEL), lambda i, j, k: (k, 0))
    chunks_k = pl.BlockSpec((N_CHIPS, tk, ffs), lambda i, j, k: (0, k, 0))
    (dwd,) = _mm(
        "ffn_dwd", [act, dxo], [chunks_k, tok_k], [(_pick(0, c), 1, c) for c in range(N_CHIPS)], N_CHIPS,
        (ffs, D_MODEL), TN, (1, 1, T // tk), halves, [jax.ShapeDtypeStruct((N_CHIPS, ffs, D_MODEL), BF16)],
        [pl.BlockSpec((N_CHIPS, ffs, D_MODEL), lambda i, j, k: (0, 0, 0))])

    dx, dgain = _mm(
        "ffn_dx", [dgate, dup, wg, wu, x, gain, dxo],
        [_all_chunks(tm, ffs), _all_chunks(tm, ffs), _wfull(D_MODEL, ffs, l), _wfull(D_MODEL, ffs, l),
         row, _gain_spec(), row],
        [(_pick(a, c), _pick(a + 2, c), 0) for c in range(N_CHIPS) for a in range(2)], 1, None, NT,
        (T // tm, 1, 1), _rms_bwd_epilogue(4, 5, 6),
        [jax.ShapeDtypeStruct((T, D_MODEL), F32), jax.ShapeDtypeStruct((8, D_MODEL), F32)],
        [row, pl.BlockSpec((8, D_MODEL), lambda i, j, k: (0, 0))])

    wsh = jax.ShapeDtypeStruct((N_CHIPS, D_MODEL, ffs), BF16)
    wout = pl.BlockSpec((N_CHIPS, D_MODEL, ffs), lambda i, j, k: (0, 0, 0))
    dws = []
    for dact in (dgate, dup):
        dws += _mm("ffn_dwgu", [h, dact], [tok_k, chunks_k], [(0, _pick(1, c), c) for c in range(N_CHIPS)],
                   N_CHIPS, (D_MODEL, ffs), TN, (1, 1, T // tk), casts, [wsh], [wout])
    return dx, dgain, dws[0], dws[1], dwd


def _mixer_fwd(x, gain, W, l, tabs):
    T = x.shape[0]
    win, wpd, wps, wo = W["w_in"], W["w_proj_dil"], W["w_proj_sb"], W["w_out"]
    cin = win.shape[3]
    cp = wpd.shape[3]
    h = _rms_fwd(x, gain)

    n_rope = 6 * D_ATT

    def roped(vals, ins, outs, i):
        v = vals[0]
        col0 = pl.program_id(0) * cin

        @pl.when(col0 < n_rope)
        def _():
            on = lax.broadcasted_iota(jnp.int32, v.shape, 1) + col0 < n_rope
            c = jnp.where(on, jnp.concatenate([ins[2][...]] * (cin // 128), axis=1), 1.0)
            s1 = jnp.where(on, jnp.concatenate([ins[3][...]] * (cin // 128), axis=1), 0.0)
            s2 = jnp.where(on, jnp.concatenate([ins[4][...]] * (cin // 128), axis=1), 0.0)
            outs[0][...] = _rope_fwd(v, c, s1, s2)

        @pl.when(col0 >= n_rope)
        def _():
            outs[0][...] = v

    (proj,) = _mm(
        "mix_in", [h, win, *tabs], [_tok(D_MODEL), _wspec(D_MODEL, cin, l, 1)] + [_tok(128)] * 3, [(0, 1, 0)], 1,
        None, NN, (T // TM, N_CHIPS, 1), roped, [jax.ShapeDtypeStruct((T, N_CHIPS * cin), F32)], [_tok(cin, 1)],
        j_outer=True)

    os_, lses = [], []
    for g, (window, dil) in enumerate(DIL_GROUPS):
        o_g, lse_g = _dil_fwd(proj, g, dil)
        os_.append(o_g)
        lses.append(lse_g)
    o_dil, lse = _dil_merge(os_, lses)
    o_sb = _sb_fwd(proj)

    def gated(vals, ins, outs, i):
        pd, ps = vals
        outs[0][...] = (_sigmoid(ins[4][...]) * pd + _sigmoid(ins[5][...]) * ps).astype(BF16)
        outs[1][...] = pd.astype(BF16)
        outs[2][...] = ps.astype(BF16)

    gd0, gs0 = COL_GD // cp, COL_GS // cp
    ush = jax.ShapeDtypeStruct((T, D_MODEL), BF16)
    u, pd, ps = _mm(
        "mix_gate", [o_dil, o_sb, wpd, wps, proj, proj],
        [_tok(D_ATT), _tok(D_ATT), _wspec(D_ATT, cp, l, 1), _wspec(D_ATT, cp, l, 1),
         pl.BlockSpec((TM, cp), lambda i, j, k: (i, gd0 + j)), pl.BlockSpec((TM, cp), lambda i, j, k: (i, gs0 + j))],
        [(0, 2, 0), (1, 3, 1)], 2, None, NN, (T // TM, N_CHIPS, 1), gated, [ush] * 3, [_tok(cp, 1)] * 3)

    def resid(vals, ins, outs, i):
        outs[0][...] = ins[2][...] + vals[0]

    (y,) = _mm(
        "mix_out", [u, wo, x], [_tok(D_MODEL), _wfull(cp, D_MODEL, l), _tok(D_MODEL)],
        [(_cols(0, c, cp), _pick(1, c), 0) for c in range(N_CHIPS)], 1, None, NN, (T // TM, 1, 1), resid,
        [jax.ShapeDtypeStruct((T, D_MODEL), F32)], [_tok(D_MODEL)])
    return y, (x, h, proj, o_dil, lse, o_sb, u, pd, ps)


def _mixer_bwd(dxo, gain, W, l, tabs, saved):
    x, h, proj, o_dil, lse, o_sb, u, pd, ps = saved
    T = x.shape[0]
    win, wpd, wps, wo = W["w_in"], W["w_proj_dil"], W["w_proj_sb"], W["w_out"]
    cin = win.shape[3]
    cp = wpd.shape[3]
    tk = TM
    tm = TM // 2
    row = pl.BlockSpec((tm, D_MODEL), lambda i, j, k: (i, 0))
    gd0, gs0 = COL_GD // cp, COL_GS // cp

    def dgated(vals, ins, outs, i):
        du = vals[0]
        sd = _sigmoid(ins[4][...])
        ss = _sigmoid(ins[5][...])
        outs[0][...] = (du * sd).astype(BF16)
        outs[1][...] = (du * ss).astype(BF16)
        outs[2][...] = (du * ins[2][...].astype(F32) * sd * (1.0 - sd)).astype(BF16)
        outs[3][...] = (du * ins[3][...].astype(F32) * ss * (1.0 - ss)).astype(BF16)

    ush = jax.ShapeDtypeStruct((T, D_MODEL), BF16)
    dpd, dps, dgd, dgs = _mm(
        "mix_du", [dxo, wo, pd, ps, proj, proj],
        [_tok(D_MODEL), _wspec(cp, D_MODEL, l, 1), _tok(cp, 1), _tok(cp, 1),
         pl.BlockSpec((TM, cp), lambda i, j, k: (i, gd0 + j)), pl.BlockSpec((TM, cp), lambda i, j, k: (i, gs0 + j))],
        [(0, 1, 0)], 1, None, NT, (T // TM, N_CHIPS, 1), dgated, [ush] * 4, [_tok(cp, 1)] * 4)

    def one(vals, ins, outs, i):
        outs[0][...] = vals[0].astype(BF16)

    def two(vals, ins, outs, i):
        outs[0][...] = vals[0].astype(BF16)
        outs[1][...] = vals[1].astype(BF16)

    (dwo,) = _mm(
        "mix_dwo", [u, dxo],
        [pl.BlockSpec((tk, cp), lambda i, j, k: (k, j)), pl.BlockSpec((tk, D_MODEL), lambda i, j, k: (k, 0))],
        [(0, 1, 0)], 1, (cp, D_MODEL), TN, (1, N_CHIPS, T // tk), one,
        [jax.ShapeDtypeStruct((N_CHIPS, cp, D_MODEL), BF16)],
        [pl.BlockSpec((None, cp, D_MODEL), lambda i, j, k: (j, 0, 0))])

    def plain2(vals, ins, outs, i):
        outs[0][...] = vals[0]
        outs[1][...] = vals[1]

    ash = jax.ShapeDtypeStruct((T, D_ATT), F32)
    do_dil, do_sb = _mm(
        "mix_do", [dpd, dps, wpd, wps], [_tok(cp, 2), _tok(cp, 2), _wspec(D_ATT, cp, l, 2), _wspec(D_ATT, cp, l, 2)],
        [(0, 2, 0), (1, 3, 1)], 2, (TM, D_ATT), NT, (T // TM, 1, N_CHIPS), plain2, [ash, ash], [_tok(D_ATT)] * 2)

    psh = jax.ShapeDtypeStruct((N_CHIPS, D_ATT, cp), BF16)
    pspec = pl.BlockSpec((None, D_ATT, cp), lambda i, j, k: (j, 0, 0))
    arow = pl.BlockSpec((tk, D_ATT), lambda i, j, k: (k, 0))
    dcol = pl.BlockSpec((tk, cp), lambda i, j, k: (k, j))
    dwpd, dwps = _mm(
        "mix_dwp", [o_dil, o_sb, dpd, dps], [arow, arow, dcol, dcol], [(0, 2, 0), (1, 3, 1)], 2, (D_ATT, cp), TN,
        (1, N_CHIPS, T // tk), two, [psh, psh], [pspec, pspec])

    dqs, dks, dvs = [], [], []
    for g, (window, dil) in enumerate(DIL_GROUPS):
        dq, dk, dv = _dil_bwd(proj, do_dil, o_dil, lse, g, dil)
        dqs.append(dq)
        dks.append(dk)
        dvs.append(dv)
    dq_s, dk_s, dv_s = _sb_bwd(proj, do_sb, o_sb)
    dproj = _assemble_dproj(dqs + dks, dvs + [dq_s, dk_s, dv_s], [dgd, dgs], tabs)

    dx, dgain = _mm(
        "mix_dx", [dproj, win, x, gain, dxo],
        [pl.BlockSpec((tm, N_CHIPS * cin), lambda i, j, k: (i, 0)), _wfull(D_MODEL, cin, l), row, _gain_spec(), row],
        [(_cols(0, c, cin), _pick(1, c), 0) for c in range(N_CHIPS)], 1, None, NT, (T // tm, 1, 1),
        _rms_bwd_epilogue(2, 3, 4),
        [jax.ShapeDtypeStruct((T, D_MODEL), F32), jax.ShapeDtypeStruct((8, D_MODEL), F32)],
        [row, pl.BlockSpec((8, D_MODEL), lambda i, j, k: (0, 0))])

    (dwin,) = _mm(
        "mix_dwin", [h, dproj],
        [pl.BlockSpec((tk, D_MODEL), lambda i, j, k: (k, 0)), pl.BlockSpec((tk, cin), lambda i, j, k: (k, j))],
        [(0, 1, 0)], 1, (D_MODEL, cin), TN, (1, N_CHIPS, T // tk), one,
        [jax.ShapeDtypeStruct((N_CHIPS, D_MODEL, cin), BF16)],
        [pl.BlockSpec((None, D_MODEL, cin), lambda i, j, k: (j, 0, 0))])
    return dx, dgain, dwin, dwpd, dwps, dwo


def _local_step(x, target, norms, norm_final, weights_of):
    T = x.shape[0]
    tabs = _rope_tables(T)
    saved, held = [], []
    for l in range(DEPTH):
        w1 = weights_of(l, 0, x)
        x, s1 = _ffn_fwd(x, norms["norm_ffn1"][l:l + 1], w1["ffn1_w_gate"], w1["ffn1_w_up"], w1["ffn1_w_down"], 0)
        w2 = weights_of(l, 1, x)
        x, s2 = _mixer_fwd(x, norms["norm_mix"][l:l + 1], w2, 0, tabs)
        w3 = weights_of(l, 2, x)
        x, s3 = _ffn_fwd(x, norms["norm_ffn2"][l:l + 1], w3["ffn2_w_gate"], w3["ffn2_w_up"], w3["ffn2_w_down"], 0)
        saved.append((s1, s2, s3))
        held.append((w1, w2, w3))
    dx, dg_final, loss = _final_loss(x, norm_final.reshape(1, D_MODEL), target)
    grads = [None] * DEPTH
    gains = [None] * DEPTH
    for l in reversed(range(DEPTH)):
        s1, s2, s3 = saved[l]
        w1, w2, w3 = held[l]
        dx, dg2, dwg2, dwu2, dwd2 = _ffn_bwd(dx, norms["norm_ffn2"][l:l + 1], w3["ffn2_w_gate"], w3["ffn2_w_up"],
                                             w3["ffn2_w_down"], 0, s3)
        dx, dgm, dwin, dwpd, dwps, dwo = _mixer_bwd(dx, norms["norm_mix"][l:l + 1], w2, 0, tabs, s2)
        dx, dg1, dwg1, dwu1, dwd1 = _ffn_bwd(dx, norms["norm_ffn1"][l:l + 1], w1["ffn1_w_gate"], w1["ffn1_w_up"],
                                             w1["ffn1_w_down"], 0, s1)
        grads[l] = dict(ffn1_w_gate=dwg1, ffn1_w_up=dwu1, ffn1_w_down=dwd1, w_in=dwin, w_proj_dil=dwpd,
                        w_proj_sb=dwps, w_out=dwo, ffn2_w_gate=dwg2, ffn2_w_up=dwu2, ffn2_w_down=dwd2)
        gains[l] = dict(norm_ffn1=dg1, norm_mix=dgm, norm_ffn2=dg2)
    return loss, dx, grads, gains, dg_final


def _place():
    x, y, c = lax.axis_index("x"), lax.axis_index("y"), lax.axis_index("c")
    chips = [(1 - x, y), (x, 1 - y), (1 - x, 1 - y)]
    return x, y, c, chips


def _half(c, r):
    return pl.ds(pl.multiple_of(c * (r // 2), 8), r // 2)


def _cast_into_slot(w, l, me_arr):
    _, r, cw = w.shape
    tr = r // 4 if r > 256 else r

    def body(me_ref, w_ref, o_ref):
        o_ref[...] = w_ref[...].astype(BF16)

    return pl.pallas_call(
        body, name="cast_weights",
        grid_spec=pltpu.PrefetchScalarGridSpec(
            num_scalar_prefetch=1, grid=(r // tr,),
            in_specs=[pl.BlockSpec((None, tr, cw), lambda i, me: (l, i, 0))],
            out_specs=pl.BlockSpec((None, None, tr, cw), lambda i, me: (me[0], 0, i, 0))),
        out_shape=jax.ShapeDtypeStruct((N_CHIPS, 1, r, cw), BF16), compiler_params=_params(),
    )(me_arr, w)


HBM_SPEC = pl.BlockSpec(memory_space=pltpu.HBM)
SEM_SPEC = pl.BlockSpec(memory_space=pltpu.SEMAPHORE)
SPLIT_COPY = pltpu.CompilerParams(has_side_effects=pltpu.SideEffectType.DATAFLOW_SIDE_EFFECTING)


def _gather_piece(ref, chip_id, c):
    return ref.at[chip_id, 0, _half(c, ref.shape[2]), :]


def _gather_start(bufs):
    n = len(bufs)

    def body(*refs):
        out_refs = refs[n:2 * n]
        send_sems, recv_sems = refs[2 * n:]
        x, y, c, chips = _place()
        me = 2 * x + y
        for a in range(n):
            piece = _gather_piece(out_refs[a], me, c)
            for j, chip in enumerate(chips):
                pltpu.make_async_remote_copy(
                    src_ref=piece, dst_ref=piece, send_sem=send_sems.at[3 * a + j], recv_sem=recv_sems.at[3 * a + j],
                    device_id=(*chip, c), device_id_type=MESH).start()

    outs = pl.pallas_call(
        body, name="gather_start", in_specs=[HBM_SPEC] * n, out_specs=[HBM_SPEC] * n + [SEM_SPEC, SEM_SPEC],
        out_shape=[pltpu.HBM(b.shape, b.dtype) for b in bufs] + [pltpu.SemaphoreType.DMA((3 * n,))] * 2,
        input_output_aliases={a: a for a in range(n)}, compiler_params=SPLIT_COPY,
    )(*[pltpu.with_memory_space_constraint(b, pltpu.HBM) for b in bufs])
    return outs[:n], outs[n], outs[n + 1]


def _gather_wait(k, bufs, places, send_sems, recv_sems, after):
    m = len(bufs)

    def body(*refs):
        in_refs = refs[:m]
        ssem, rsem = refs[m], refs[m + 1]
        x, y, c, chips = _place()
        me = 2 * x + y
        for t, a in enumerate(places):
            for j, chip in enumerate(chips):
                cp = pltpu.make_async_remote_copy(
                    src_ref=_gather_piece(in_refs[t], me, c),
                    dst_ref=_gather_piece(in_refs[t], 2 * chip[0] + chip[1], c),
                    send_sem=ssem.at[3 * a + j], recv_sem=rsem.at[3 * a + j], device_id=(*chip, c),
                    device_id_type=MESH)
                cp.wait_send()
                cp.wait_recv()

    return pl.pallas_call(
        body, name=f"gather_wait_{k}",
        in_specs=[HBM_SPEC] * m + [SEM_SPEC, SEM_SPEC, pl.BlockSpec(memory_space=pl.ANY)], out_specs=[HBM_SPEC] * m,
        out_shape=[pltpu.HBM(b.shape, b.dtype) for b in bufs], input_output_aliases={t: t for t in range(m)},
        compiler_params=SPLIT_COPY,
    )(*bufs, send_sems, recv_sems, after)


def _gather_relay(bufs):
    n = len(bufs)

    def body(*refs):
        out_refs = refs[n:2 * n]
        send_sems, recv_sems = refs[2 * n:]
        x, y, c, chips = _place()
        cps = []
        for a in range(n):
            for j, chip in enumerate(chips):
                piece = _gather_piece(out_refs[a], 2 * chip[0] + chip[1], c)
                cps.append(pltpu.make_async_remote_copy(
                    src_ref=piece, dst_ref=piece, send_sem=send_sems.at[a, j], recv_sem=recv_sems.at[a, j],
                    device_id=(x, y, 1 - c), device_id_type=MESH))
        for cp in cps:
            cp.start()
        for a in range(n):
            for j, chip in enumerate(chips):
                theirs = _gather_piece(out_refs[a], 2 * chip[0] + chip[1], 1 - c)
                pltpu.make_async_remote_copy(
                    src_ref=theirs, dst_ref=theirs, send_sem=send_sems.at[a, j], recv_sem=recv_sems.at[a, j],
                    device_id=(x, y, 1 - c), device_id_type=MESH).wait_recv()
        for cp in cps:
            cp.wait_send()

    any_spec = pl.BlockSpec(memory_space=pl.ANY)
    return pl.pallas_call(
        body, name="gather_relay", in_specs=[any_spec] * n, out_specs=[any_spec] * n,
        out_shape=[jax.ShapeDtypeStruct(b.shape, b.dtype) for b in bufs],
        input_output_aliases={a: a for a in range(n)},
        scratch_shapes=[pltpu.SemaphoreType.DMA((n, 3))] * 2,
    )(*bufs)


def _exchange_halves(gs):
    n = len(gs)

    def body(*refs):
        g_refs, out_refs = refs[:n], refs[n:2 * n]
        send_sems, recv_sems = refs[2 * n:]
        x, y, c, _ = _place()
        cps = []
        for a in range(n):
            r = g_refs[a].shape[2]
            cps.append(pltpu.make_async_remote_copy(
                src_ref=g_refs[a].at[:, :, _half(1 - c, r), :], dst_ref=out_refs[a],
                send_sem=send_sems.at[a], recv_sem=recv_sems.at[a], device_id=(x, y, 1 - c), device_id_type=MESH))
        for cp in cps:
            cp.start()
        for cp in cps:
            cp.wait()

    any_spec = pl.BlockSpec(memory_space=pl.ANY)
    return pl.pallas_call(
        body, name="grad_to_sibling", in_specs=[any_spec] * n, out_specs=[any_spec] * n,
        out_shape=[jax.ShapeDtypeStruct((g.shape[0], g.shape[1], g.shape[2] // 2, g.shape[3]), g.dtype) for g in gs],
        scratch_shapes=[pltpu.SemaphoreType.DMA((n,))] * 2,
    )(*gs)


def _add_half(g, got, c_arr):
    n, _, r, cw = g.shape

    def body(c_ref, a_ref, b_ref, o_ref):
        o_ref[...] = (a_ref[...].astype(F32) + b_ref[...].astype(F32)).astype(BF16)

    return pl.pallas_call(
        body, name="grad_add_half",
        grid_spec=pltpu.PrefetchScalarGridSpec(
            num_scalar_prefetch=1, grid=(n, N_CHIPS),
            in_specs=[pl.BlockSpec((None, None, r // 2, cw), lambda a, k, cr: (a, k, cr[0], 0)),
                      pl.BlockSpec((None, None, r // 2, cw), lambda a, k, cr: (a, k, 0, 0))],
            out_specs=pl.BlockSpec((None, None, r // 2, cw), lambda a, k, cr: (a, k, 0, 0))),
        out_shape=jax.ShapeDtypeStruct((n, N_CHIPS, r // 2, cw), BF16), compiler_params=_params(),
    )(c_arr, g, got)


def _scatter_to_chips(ss):
    n = len(ss)

    def body(*refs):
        s_refs, out_refs = refs[:n], refs[n:2 * n]
        send_sems, recv_sems, local_sems = refs[2 * n:]
        x, y, c, chips = _place()
        me = 2 * x + y
        local = [pltpu.make_async_copy(s_refs[a].at[:, me], out_refs[a].at[:, me], local_sems.at[a])
                 for a in range(n)]
        for cp in local:
            cp.start()
        cps = []
        for a in range(n):
            for j, chip in enumerate(chips):
                cid = 2 * chip[0] + chip[1]
                cps.append(pltpu.make_async_remote_copy(
                    src_ref=s_refs[a].at[:, cid], dst_ref=out_refs[a].at[:, me],
                    send_sem=send_sems.at[a, j], recv_sem=recv_sems.at[a, j], device_id=(*chip, c),
                    device_id_type=MESH))
        for cp in cps:
            cp.start()
        for cp in cps:
            cp.wait()
        for cp in local:
            cp.wait()

    any_spec = pl.BlockSpec(memory_space=pl.ANY)
    return pl.pallas_call(
        body, name="grad_to_chips", in_specs=[any_spec] * n, out_specs=[any_spec] * n,
        out_shape=[jax.ShapeDtypeStruct(s.shape, s.dtype) for s in ss],
        scratch_shapes=[pltpu.SemaphoreType.DMA((n, 3))] * 2 + [pltpu.SemaphoreType.DMA((n,))],
    )(*ss)


def _sum_chips(got):
    n, _, rh, cw = got.shape

    def body(g_ref, o_ref):
        acc = g_ref[0].astype(F32)
        for k in range(1, N_CHIPS):
            acc = acc + g_ref[k].astype(F32)
        o_ref[...] = acc

    return pl.pallas_call(
        body, name="grad_sum_chips", grid=(n,),
        in_specs=[pl.BlockSpec((None, N_CHIPS, rh, cw), lambda a: (a, 0, 0, 0))],
        out_specs=pl.BlockSpec((None, rh, cw), lambda a: (a, 0, 0)),
        out_shape=jax.ShapeDtypeStruct((n, rh, cw), F32), compiler_params=_params(),
    )(got)


def _swap_halves(fs):
    n = len(fs)

    def body(*refs):
        f_refs, out_refs = refs[:n], refs[n:2 * n]
        send_sems, recv_sems = refs[2 * n:]
        x, y, c, _ = _place()
        cps = [pltpu.make_async_remote_copy(
            src_ref=f_refs[a], dst_ref=out_refs[a], send_sem=send_sems.at[a], recv_sem=recv_sems.at[a],
            device_id=(x, y, 1 - c), device_id_type=MESH) for a in range(n)]
        for cp in cps:
            cp.start()
        for cp in cps:
            cp.wait()

    any_spec = pl.BlockSpec(memory_space=pl.ANY)
    return pl.pallas_call(
        body, name="grad_swap_halves", in_specs=[any_spec] * n, out_specs=[any_spec] * n,
        out_shape=[jax.ShapeDtypeStruct(f.shape, f.dtype) for f in fs],
        scratch_shapes=[pltpu.SemaphoreType.DMA((n,))] * 2,
    )(*fs)


def _allreduce_rows(stats):
    def body(s_ref, o_ref, buf, send_sems, recv_sems):
        x, y, c, _ = _place()
        me = 4 * x + 2 * y + c
        buf[me] = s_ref[...]
        cps = []
        for k in range(1, 8):
            px = jnp.where(k & 4, 1 - x, x)
            py = jnp.where(k & 2, 1 - y, y)
            pc = jnp.where(k & 1, 1 - c, c)
            cps.append(pltpu.make_async_remote_copy(
                src_ref=s_ref, dst_ref=buf.at[me], send_sem=send_sems.at[k - 1], recv_sem=recv_sems.at[k - 1],
                device_id=(px, py, pc), device_id_type=MESH))
        for cp in cps:
            cp.start()
        for cp in cps:
            cp.wait()
        acc = buf[0]
        for d in range(1, 8):
            acc = acc + buf[d]
        o_ref[...] = acc

    vm = pl.BlockSpec(memory_space=pltpu.VMEM)
    return pl.pallas_call(
        body, name="allreduce_rows", in_specs=[vm], out_specs=vm,
        out_shape=jax.ShapeDtypeStruct(stats.shape, F32),
        scratch_shapes=[pltpu.VMEM((8,) + stats.shape, F32), pltpu.SemaphoreType.DMA((7,)),
                        pltpu.SemaphoreType.DMA((7,))],
    )(stats)


def _adamw_math(w, g, m, v):
    m = ADAM_B1 * m + (1.0 - ADAM_B1) * g
    v = ADAM_B2 * v + (1.0 - ADAM_B2) * (g * g)
    m_hat = m / (1.0 - ADAM_B1 ** ADAM_STEP)
    v_hat = v / (1.0 - ADAM_B2 ** ADAM_STEP)
    delta = -ADAM_LR * (m_hat / (jnp.sqrt(v_hat) + ADAM_EPS) + ADAM_WD * w)
    return delta, m, v


def _adamw(w, m, v, mine, theirs, first, c_arr):
    L, r, cw = w.shape
    tr = r // 4 if r > 256 else r // 2
    nblk = (r // 2) // tr

    def body(c_ref, w_ref, m_ref, v_ref, a_ref, b_ref, go_ref, d_ref, mo_ref, vo_ref):
        g = jnp.where(pl.program_id(1) == c_ref[0], a_ref[...], b_ref[...])
        delta, mn, vn = _adamw_math(w_ref[...], g, m_ref[...], v_ref[...])
        go_ref[...] = g
        d_ref[...] = delta
        mo_ref[...] = mn
        vo_ref[...] = vn

    blk = pl.BlockSpec((None, tr, cw), lambda l, hh, i, cr: (l, hh * nblk + i, 0))
    half = pl.BlockSpec((None, tr, cw), lambda l, hh, i, cr: (first + l, i, 0))
    sh = jax.ShapeDtypeStruct(w.shape, F32)
    return pl.pallas_call(
        body, name="adamw",
        grid_spec=pltpu.PrefetchScalarGridSpec(
            num_scalar_prefetch=1, grid=(L, 2, nblk), in_specs=[blk, blk, blk, half, half], out_specs=[blk] * 4),
        out_shape=[sh] * 4, compiler_params=_params(),
    )(c_arr, w, m, v, mine, theirs)


def _adamw_rows(w, m, v, g):
    def body(w_ref, m_ref, v_ref, g_ref, d_ref, mo_ref, vo_ref):
        delta, mn, vn = _adamw_math(w_ref[...], g_ref[...], m_ref[...], v_ref[...])
        d_ref[...] = delta
        mo_ref[...] = mn
        vo_ref[...] = vn

    vm = pl.BlockSpec(memory_space=pltpu.VMEM)
    sh = jax.ShapeDtypeStruct(w.shape, F32)
    return pl.pallas_call(body, name="adamw_rows", in_specs=[vm] * 4, out_specs=[vm] * 3, out_shape=[sh] * 3)(w, m, v, g)


SUBLAYERS = (("ffn1_w_gate", "ffn1_w_up", "ffn1_w_down"), ("w_in", "w_proj_dil", "w_proj_sb", "w_out"),
             ("ffn2_w_gate", "ffn2_w_up", "ffn2_w_down"))
GROUPS = (("ffn1_w_gate", "ffn1_w_up", "ffn2_w_gate", "ffn2_w_up"), ("ffn1_w_down", "ffn2_w_down"),
          ("w_in",), ("w_proj_dil", "w_proj_sb"), ("w_out",))


def _pick_row(blocks):
    row = lax.broadcasted_iota(jnp.int32, (8, D_MODEL), 0)
    out = jnp.zeros((8, D_MODEL), F32)
    for i, b in enumerate(blocks):
        out = out + jnp.where(row == i, b, 0.0)
    return out


def kernel(x, norm_ffn1, ffn1_w_gate, ffn1_w_up, ffn1_w_down, norm_mix, w_in, w_proj_dil, w_proj_sb, w_out, norm_ffn2, ffn2_w_gate, ffn2_w_up, ffn2_w_down, norm_final, loss_target, m_norm_ffn1, m_ffn1_w_gate, m_ffn1_w_up, m_ffn1_w_down, m_norm_mix, m_w_in, m_w_proj_dil, m_w_proj_sb, m_w_out, m_norm_ffn2, m_ffn2_w_gate, m_ffn2_w_up, m_ffn2_w_down, m_norm_final, v_norm_ffn1, v_ffn1_w_gate, v_ffn1_w_up, v_ffn1_w_down, v_norm_mix, v_w_in, v_w_proj_dil, v_w_proj_sb, v_w_out, v_norm_ffn2, v_ffn2_w_gate, v_ffn2_w_up, v_ffn2_w_down, v_norm_final):
    given = dict(locals())
    weights = {n: given[n] for n in WEIGHT_NAMES}
    norms = {n: given[n] for n in NORM_NAMES}

    c_arr = lax.axis_index("c").astype(jnp.int32).reshape(1)
    me_arr = (2 * lax.axis_index("x") + lax.axis_index("y")).astype(jnp.int32).reshape(1)
    order = [(l, s, n) for l in range(DEPTH) for s in range(len(SUBLAYERS)) for n in SUBLAYERS[s]]
    bufs, send_sems, recv_sems = _gather_start([_cast_into_slot(weights[n], l, me_arr) for l, s, n in order])

    def weights_of(l, s, after):
        places = [i for i, (ll, ss, _) in enumerate(order) if (ll, ss) == (l, s)]
        got = _gather_wait(len(SUBLAYERS) * l + s, [bufs[i] for i in places], places, send_sems, recv_sems, after)
        return {order[i][2]: g for i, g in zip(places, _gather_relay(got))}

    loss_blk, grad_x, grads, gains, dg_final = _local_step(x[0], loss_target[0], norms, norm_final, weights_of)

    stacked = [jnp.stack([grads[l][n] for n in grp for l in range(DEPTH)]) for grp in GROUPS]
    from_sibling = _exchange_halves(stacked)
    chip_sums = [_add_half(g, got, c_arr) for g, got in zip(stacked, from_sibling)]
    from_chips = _scatter_to_chips(chip_sums)
    mine = [_sum_chips(got) for got in from_chips]
    theirs = _swap_halves(mine)

    out = {"grad_x": grad_x[None]}
    for grp, ga, gb in zip(GROUPS, mine, theirs):
        for i, n in enumerate(grp):
            g, d, mn, vn = _adamw(weights[n], given["m_" + n], given["v_" + n], ga, gb, i * DEPTH, c_arr)
            out["grad_" + n], out["delta_" + n], out["new_m_" + n], out["new_v_" + n] = g, d, mn, vn

    rows = [gains[l][n] for n in NORM_NAMES for l in range(DEPTH)] + [dg_final, loss_blk]
    total = _allreduce_rows(_pick_row(rows))
    out["loss"] = total[7, 0]
    wn = jnp.concatenate([given[n] for n in NORM_NAMES] + [norm_final[None], jnp.zeros((1, D_MODEL), F32)])
    mn_ = jnp.concatenate([given["m_" + n] for n in NORM_NAMES] + [m_norm_final[None], jnp.zeros((1, D_MODEL), F32)])
    vn_ = jnp.concatenate([given["v_" + n] for n in NORM_NAMES] + [v_norm_final[None], jnp.ones((1, D_MODEL), F32)])
    d_n, m_n, v_n = _adamw_rows(wn, mn_, vn_, total)
    for i, n in enumerate(NORM_NAMES):
        sl = slice(i * DEPTH, (i + 1) * DEPTH)
        out["grad_" + n], out["delta_" + n], out["new_m_" + n], out["new_v_" + n] = total[sl], d_n[sl], m_n[sl], v_n[sl]
    out["grad_norm_final"], out["delta_norm_final"] = total[6], d_n[6]
    out["new_m_norm_final"], out["new_v_norm_final"] = m_n[6], v_n[6]

    order = WEIGHT_NAMES
    del order
    names = ["norm_ffn1", "ffn1_w_gate", "ffn1_w_up", "ffn1_w_down", "norm_mix", "w_in", "w_proj_dil", "w_proj_sb",
             "w_out", "norm_ffn2", "ffn2_w_gate", "ffn2_w_up", "ffn2_w_down", "norm_final"]
    return (out["loss"], out["grad_x"], *[out["grad_" + n] for n in names], *[out["delta_" + n] for n in names],
            *[out["new_m_" + n] for n in names], *[out["new_v_" + n] for n in names])
```

```python
import functools

import jax
import jax.numpy as jnp
from jax import lax
from jax.experimental import pallas as pl
from jax.experimental.pallas import tpu as pltpu

F32 = jnp.float32
BF16 = jnp.bfloat16

D_MODEL = 1024
DEPTH = 2
N_CHIPS = 4
HEAD_DIM = 64
ROPE_DIM = 16
ROPE_THETA = 500000.0
DIL_GROUPS = ((128, 1), (512, 4), (2048, 16))
SPAN = 128
Q_BLOCK = 128
RMS_EPS = 1e-6
D_ATT = 256
COL_QS = 2304
COL_GD = 3072
COL_GS = 4096
ADAM_LR, ADAM_B1, ADAM_B2, ADAM_EPS, ADAM_WD, ADAM_STEP = 0.001, 0.9, 0.999, 1e-08, 0.01, 10

VMEM_LIMIT = 52 * 1024 * 1024
TM = 512
NEG = -1e30

NN = (((1,), (0,)), ((), ()))
NT = (((1,), (1,)), ((), ()))
TN = (((0,), (0,)), ((), ()))
MESH = pl.DeviceIdType.MESH

WEIGHT_NAMES = ("ffn1_w_gate", "ffn1_w_up", "ffn1_w_down", "w_in", "w_proj_dil",
                "w_proj_sb", "w_out", "ffn2_w_gate", "ffn2_w_up", "ffn2_w_down")
NORM_NAMES = ("norm_ffn1", "norm_mix", "norm_ffn2")


def _params(**kw):
    return pltpu.CompilerParams(vmem_limit_bytes=VMEM_LIMIT, **kw)


def _sigmoid(x):
    return 0.5 * jnp.tanh(0.5 * x) + 0.5


def _mm_body(pairs, n_in, n_out, n_acc, dims, nk, i_axis, epilogue, *refs):
    ins = refs[:n_in]
    outs = refs[n_in:n_in + n_out]
    accs = refs[n_in + n_out:]
    i = pl.program_id(i_axis)
    k = pl.program_id(2)

    def operand(a):
        return (a(ins) if callable(a) else ins[a][...]).astype(BF16)

    def dot(ia, ib):
        return lax.dot_general(operand(ia), operand(ib), dims, preferred_element_type=F32)

    if nk == 1:
        parts = [None] * n_acc
        for ia, ib, ic in pairs:
            parts[ic] = dot(ia, ib) if parts[ic] is None else parts[ic] + dot(ia, ib)
        epilogue(parts, ins, outs, i)
        return

    @pl.when(k == 0)
    def _():
        for c in range(n_acc):
            accs[c][...] = jnp.zeros_like(accs[c])

    for ia, ib, ic in pairs:
        accs[ic][...] += dot(ia, ib)

    @pl.when(k == nk - 1)
    def _():
        epilogue([a[...] for a in accs], ins, outs, i)


def _j_outer(spec):
    f = spec.index_map
    return pl.BlockSpec(spec.block_shape, lambda j, i, k: f(i, j, k))


def _mm(name, ins, in_specs, pairs, n_acc, acc_shape, dims, grid, epilogue, out_shapes, out_specs, j_outer=False):
    nk = grid[2]
    if j_outer:
        grid = (grid[1], grid[0], grid[2])
        in_specs = [_j_outer(s) for s in in_specs]
        out_specs = [_j_outer(s) for s in out_specs]
    scratch = [pltpu.VMEM(acc_shape, F32) for _ in range(n_acc)] if nk > 1 else []
    body = functools.partial(_mm_body, tuple(pairs), len(ins), len(out_shapes), n_acc, dims, nk,
                             1 if j_outer else 0, epilogue)
    return pl.pallas_call(
        body, name=name, grid=grid, in_specs=in_specs, out_specs=out_specs, out_shape=out_shapes,
        scratch_shapes=scratch,
        compiler_params=_params(dimension_semantics=("arbitrary", "arbitrary", "arbitrary")),
    )(*ins)


def _wspec(r, c, l, by):
    if by == 1:
        return pl.BlockSpec((None, None, r, c), lambda i, j, k: (j, l, 0, 0))
    return pl.BlockSpec((None, None, r, c), lambda i, j, k: (k, l, 0, 0))


def _rms_bwd_epilogue(x_idx, g_idx, dxo_idx):
    def ep(vals, ins, outs, i):
        dh = vals[0]
        x = ins[x_idx][...]
        g = ins[g_idx][...]
        rstd = lax.rsqrt(jnp.mean(x * x, axis=-1, keepdims=True) + RMS_EPS)
        xhat = x * rstd
        dxhat = dh * g
        dx = rstd * (dxhat - xhat * jnp.mean(dxhat * xhat, axis=-1, keepdims=True))
        outs[0][...] = ins[dxo_idx][...] + dx
        dg = jnp.broadcast_to(jnp.sum(dh * xhat, axis=0, keepdims=True), outs[1].shape)

        @pl.when(i == 0)
        def _():
            outs[1][...] = dg

        @pl.when(i > 0)
        def _():
            outs[1][...] += dg
    return ep


def _rms_fwd(x, gain):
    T = x.shape[0]

    def body(x_ref, g_ref, h_ref):
        xv = x_ref[...]
        h = xv * lax.rsqrt(jnp.mean(xv * xv, axis=-1, keepdims=True) + RMS_EPS)
        h_ref[...] = (h * g_ref[...]).astype(BF16)

    return pl.pallas_call(
        body, name="rms_fwd", grid=(T // TM,),
        in_specs=[pl.BlockSpec((TM, D_MODEL), lambda i: (i, 0)), pl.BlockSpec((1, D_MODEL), lambda i: (0, 0))],
        out_specs=pl.BlockSpec((TM, D_MODEL), lambda i: (i, 0)),
        out_shape=jax.ShapeDtypeStruct((T, D_MODEL), BF16), compiler_params=_params(),
    )(x, gain)


def _rope_tables(T):
    pos = jnp.arange(T, dtype=F32)
    inv_freq = ROPE_THETA ** (-jnp.arange(0, ROPE_DIM, 2, dtype=F32) / ROPE_DIM)
    ang = pos[:, None] * inv_freq[None, :]
    cos, sin = jnp.cos(ang), jnp.sin(ang)
    half = ROPE_DIM // 2
    one = jnp.ones((T, HEAD_DIM - ROPE_DIM), F32)
    zero = jnp.zeros((T, HEAD_DIM - ROPE_DIM), F32)
    zh = jnp.zeros((T, half), F32)
    c = jnp.concatenate([cos, cos, one], axis=1)
    s1 = jnp.concatenate([-sin, zh, zero], axis=1)
    s2 = jnp.concatenate([zh, sin, zero], axis=1)
    return tuple(jnp.concatenate([t, t], axis=1) for t in (c, s1, s2))


def _rope_fwd(xv, c, s1, s2):
    w = xv.shape[1]
    half = ROPE_DIM // 2
    return xv * c + pltpu.roll(xv, w - half, 1) * s1 + pltpu.roll(xv, half, 1) * s2


def _rope_bwd(dy, c, s1, s2):
    w = dy.shape[1]
    half = ROPE_DIM // 2
    return dy * c + pltpu.roll(dy * s1, half, 1) + pltpu.roll(dy * s2, w - half, 1)


def _assemble_dproj(dqk, rest, gates, tabs):
    T = gates[0].shape[0]
    n_qk, n_rest = len(dqk), len(rest)
    width = (n_qk + n_rest) * D_ATT + 2 * D_MODEL

    def body(*refs):
        ins, (c_ref, s1_ref, s2_ref), o_ref = refs[:n_qk + n_rest + 2], refs[-4:-1], refs[-1]
        c = jnp.concatenate([c_ref[...]] * 2, axis=1)
        s1 = jnp.concatenate([s1_ref[...]] * 2, axis=1)
        s2 = jnp.concatenate([s2_ref[...]] * 2, axis=1)
        for b in range(n_qk + n_rest):
            v = ins[b][...]
            if b < n_qk:
                v = _rope_bwd(v, c, s1, s2)
            o_ref[:, b * D_ATT:(b + 1) * D_ATT] = v.astype(BF16)
        off = (n_qk + n_rest) * D_ATT
        o_ref[:, off:off + D_MODEL] = ins[-2][...]
        o_ref[:, off + D_MODEL:] = ins[-1][...]

    att = pl.BlockSpec((TM, D_ATT), lambda i: (i, 0))
    wide = pl.BlockSpec((TM, D_MODEL), lambda i: (i, 0))
    tab = pl.BlockSpec((TM, 128), lambda i: (i, 0))
    return pl.pallas_call(
        body, name="assemble_dproj", grid=(T // TM,),
        in_specs=[att] * (n_qk + n_rest) + [wide, wide, tab, tab, tab],
        out_specs=pl.BlockSpec((TM, width), lambda i: (i, 0)),
        out_shape=jax.ShapeDtypeStruct((T, width), BF16), compiler_params=_params(),
    )(*dqk, *rest, *gates, *tabs)


def _dil_merge(os_, lses):
    T = os_[0].shape[0]

    def body(o0, o1, o2, l0, l1, l2, o_ref, lse_ref):
        a, b, c = l0[...], l1[...], l2[...]
        m = jnp.maximum(jnp.maximum(a, b), c)
        ea, eb, ec = jnp.exp(a - m), jnp.exp(b - m), jnp.exp(c - m)
        den = ea + eb + ec
        o_ref[...] = (ea * o0[...] + eb * o1[...] + ec * o2[...]) / den
        lse_ref[...] = m + jnp.log(den)

    blk = pl.BlockSpec((TM, D_ATT), lambda i: (i, 0))
    sh = jax.ShapeDtypeStruct((T, D_ATT), F32)
    return pl.pallas_call(
        body, name="dil_merge", grid=(T // TM,), in_specs=[blk] * 6, out_specs=[blk, blk],
        out_shape=[sh, sh], compiler_params=_params(),
    )(*os_, *lses)


def _final_loss(x, gain, target):
    T = x.shape[0]

    def body(x_ref, g_ref, t_ref, dx_ref, dg_ref, loss_ref):
        xv = x_ref[...]
        g = g_ref[...]
        rstd = lax.rsqrt(jnp.mean(xv * xv, axis=-1, keepdims=True) + RMS_EPS)
        xhat = xv * rstd
        err = xhat * g - t_ref[...]
        loss = 0.5 * jnp.sum(jnp.mean(err * err, axis=-1, keepdims=True), axis=0, keepdims=True)
        dy = err * (1.0 / D_MODEL)
        dxhat = dy * g
        dx_ref[...] = rstd * (dxhat - xhat * jnp.mean(dxhat * xhat, axis=-1, keepdims=True))
        dg = jnp.broadcast_to(jnp.sum(dy * xhat, axis=0, keepdims=True), dg_ref.shape)
        ls = jnp.broadcast_to(loss, loss_ref.shape)

        @pl.when(pl.program_id(0) == 0)
        def _():
            dg_ref[...] = dg
            loss_ref[...] = ls

        @pl.when(pl.program_id(0) > 0)
        def _():
            dg_ref[...] += dg
            loss_ref[...] += ls

    blk = pl.BlockSpec((TM, D_MODEL), lambda i: (i, 0))
    row = pl.BlockSpec((1, D_MODEL), lambda i: (0, 0))
    acc = pl.BlockSpec((8, D_MODEL), lambda i: (0, 0))
    return pl.pallas_call(
        body, name="final_loss", grid=(T // TM,), in_specs=[blk, row, blk], out_specs=[blk, acc, acc],
        out_shape=[jax.ShapeDtypeStruct((T, D_MODEL), F32), jax.ShapeDtypeStruct((8, D_MODEL), F32),
                   jax.ShapeDtypeStruct((8, D_MODEL), F32)],
        compiler_params=_params(dimension_semantics=("arbitrary",)),
    )(x, gain, target)


def _pair_masks():
    lane = lax.broadcasted_iota(jnp.int32, (SPAN, 128), 1)
    return [lane < HEAD_DIM, lane >= HEAD_DIM]


def _stack_heads(x, masks):
    return jnp.concatenate([jnp.where(m, x, 0.0) for m in masks], axis=0)


def _unstack_heads(y, masks):
    rows = y.shape[0] // len(masks)
    out = jnp.where(masks[0], y[:rows], 0.0)
    for h in range(1, len(masks)):
        out = out + jnp.where(masks[h], y[rows * h:rows * (h + 1)], 0.0)
    return out


def _dil_rows(idx, d):
    u = idx // d
    r = idx - u * d
    own = pl.ds(u * (SPAN * d) + r, SPAN, stride=d) if d > 1 else pl.ds(pl.multiple_of(u * SPAN, SPAN), SPAN)
    up = jnp.maximum(u - 1, 0)
    prev = pl.ds(up * (SPAN * d) + r, SPAN, stride=d) if d > 1 else pl.ds(pl.multiple_of(up * SPAN, SPAN), SPAN)
    return u, own, prev


def _dil_valid(u):
    qi = lax.broadcasted_iota(jnp.int32, (2 * SPAN, 2 * SPAN), 0) & (SPAN - 1)
    kj = lax.broadcasted_iota(jnp.int32, (2 * SPAN, 2 * SPAN), 1)
    in_prev = (kj < SPAN) & (kj >= qi + jnp.where(u > 0, 0, SPAN))
    return in_prev | ((kj >= SPAN) & (kj - SPAN <= qi))


def _dil_keys(ref, own, prev):
    return jnp.concatenate([ref[prev, :], ref[own, :]], axis=0).astype(BF16)


def _dil_fwd(proj, g, d):
    T = proj.shape[0]
    n_iter = T // SPAN

    def body(q_ref, k_ref, v_ref, o_ref, lse_ref):
        masks = _pair_masks()

        def step(idx, carry):
            u, own, prev = _dil_rows(idx, d)
            qs = _stack_heads(q_ref[own, :] * (HEAD_DIM ** -0.5), masks).astype(BF16)
            kk = _dil_keys(k_ref, own, prev)
            vv = _dil_keys(v_ref, own, prev)
            s = jnp.where(_dil_valid(u), lax.dot_general(qs, kk, NT, preferred_element_type=F32), NEG)
            m = jnp.max(s, axis=1, keepdims=True)
            p = jnp.exp(s - m)
            den = jnp.sum(p, axis=1, keepdims=True)
            pv = lax.dot_general(p.astype(BF16), vv, NN, preferred_element_type=F32) / den
            o_ref[own, :] = _unstack_heads(pv, masks)
            lse_ref[own, :] = _unstack_heads(jnp.broadcast_to(m + jnp.log(den), pv.shape), masks)
            return carry

        lax.fori_loop(0, n_iter, step, 0, unroll=2)

    def col(b):
        return pl.BlockSpec((T, 128), lambda p: (0, b + p))

    sh = jax.ShapeDtypeStruct((T, D_ATT), F32)
    out = pl.BlockSpec((T, 128), lambda p: (0, p))
    return pl.pallas_call(
        body, name=f"dil_fwd_d{d}", grid=(2,),
        in_specs=[col(2 * g), col(6 + 2 * g), col(12 + 2 * g)], out_specs=[out, out], out_shape=[sh, sh],
        compiler_params=_params(dimension_semantics=("arbitrary",)),
    )(proj, proj, proj)


def _dil_bwd(proj, do, o_dil, lse, g, d):
    T = proj.shape[0]
    n_iter = T // SPAN

    def body(q_ref, k_ref, v_ref, do_ref, o_ref, lse_ref, dq_ref, dk_ref, dv_ref):
        masks = _pair_masks()
        head_lanes = jnp.concatenate(masks, axis=0)

        def step(idx, carry):
            u, own, prev = _dil_rows(idx, d)
            qs = _stack_heads(q_ref[own, :] * (HEAD_DIM ** -0.5), masks).astype(BF16)
            kk = _dil_keys(k_ref, own, prev)
            vv = _dil_keys(v_ref, own, prev)
            dom = _stack_heads(do_ref[own, :], masks)
            dos = dom.astype(BF16)
            delta = jnp.sum(dom * jnp.concatenate([o_ref[own, :]] * 2, axis=0), axis=1, keepdims=True)
            lrow = jnp.max(jnp.where(head_lanes, jnp.concatenate([lse_ref[own, :]] * 2, axis=0), NEG),
                           axis=1, keepdims=True)
            s = lax.dot_general(qs, kk, NT, preferred_element_type=F32)
            p = jnp.where(_dil_valid(u), jnp.exp(s - lrow), 0.0)
            dp = lax.dot_general(dos, vv, NT, preferred_element_type=F32)
            ds = (p * (dp - delta)).astype(BF16)
            dq = lax.dot_general(ds, kk, NN, preferred_element_type=F32)
            dkk = lax.dot_general(ds, qs, TN, preferred_element_type=F32)
            dvv = lax.dot_general(p.astype(BF16), dos, TN, preferred_element_type=F32)
            dq_ref[own, :] = _unstack_heads(dq, masks) * (HEAD_DIM ** -0.5)
            dk_ref[own, :] = dkk[SPAN:]
            dv_ref[own, :] = dvv[SPAN:]
            dk_ref[prev, :] = dk_ref[prev, :] + dkk[:SPAN]
            dv_ref[prev, :] = dv_ref[prev, :] + dvv[:SPAN]
            return carry

        lax.fori_loop(0, n_iter, step, 0, unroll=2)

    def col(b):
        return pl.BlockSpec((T, 128), lambda p: (0, b + p))

    sh = jax.ShapeDtypeStruct((T, D_ATT), F32)
    return pl.pallas_call(
        body, name=f"dil_bwd_d{d}", grid=(2,),
        in_specs=[col(2 * g), col(6 + 2 * g), col(12 + 2 * g), col(0), col(0), col(0)],
        out_specs=[col(0), col(0), col(0)], out_shape=[sh, sh, sh],
        compiler_params=_params(dimension_semantics=("arbitrary",)),
    )(proj, proj, proj, do, o_dil, lse)


SB_KT = 512


def _sb_tri(strict):
    a = lax.broadcasted_iota(jnp.int32, (Q_BLOCK, Q_BLOCK), 0)
    b = lax.broadcasted_iota(jnp.int32, (Q_BLOCK, Q_BLOCK), 1)
    return jnp.where((a > b) if strict else (a >= b), 1.0, 0.0).astype(BF16)


def _suffix(x, c, tri):
    r = x.shape[0]
    nb = x.shape[1] // Q_BLOCK
    blocks = [x[:, Q_BLOCK * b:Q_BLOCK * (b + 1)] for b in range(nb)]
    hi = [b.astype(BF16) for b in blocks]
    lo = [(b - h.astype(F32)).astype(BF16) for b, h in zip(blocks, hi)]
    y = lax.dot_general(jnp.concatenate(hi + lo, axis=0), tri, NN, preferred_element_type=F32)
    outs = [None] * nb
    run = c
    for b in reversed(range(nb)):
        outs[b] = run + y[r * b:r * (b + 1)] + y[r * (nb + b):r * (nb + b + 1)]
        run = run + jnp.sum(blocks[b], axis=1, keepdims=True)
    return jnp.concatenate(outs, axis=1), run


def _sb_tile(qs, kb, past, c, tri):
    z = lax.dot_general(qs, kb, NT, preferred_element_type=F32)
    lsz = jnp.minimum(z, 0.0) - jnp.log(1.0 + jnp.exp(-jnp.abs(z)))
    lk = lsz - z
    if past is not None:
        lk = jnp.where(past, lk, 0.0)
    after, c_new = _suffix(lk, c, tri)
    w = jnp.exp(lsz + after)
    if past is not None:
        w = jnp.where(past, w, 0.0)
    return z, lsz, w, c_new


SB_HEADS = D_ATT // HEAD_DIM
SB_ROWS = SB_HEADS * Q_BLOCK


def _sb_past(i, t):
    row = lax.broadcasted_iota(jnp.int32, (SB_ROWS, SB_KT), 0) & (Q_BLOCK - 1)
    col = lax.broadcasted_iota(jnp.int32, (SB_ROWS, SB_KT), 1)
    return col + t * SB_KT < row + i * Q_BLOCK


def _sb_head_masks():
    lane = lax.broadcasted_iota(jnp.int32, (Q_BLOCK, D_ATT), 1)
    return [(lane >= HEAD_DIM * h) & (lane < HEAD_DIM * (h + 1)) for h in range(SB_HEADS)]


def _sb_rows(t):
    return pl.ds(pl.multiple_of(t * SB_KT, SB_KT), SB_KT)


def _sb_fwd(proj):
    T = proj.shape[0]

    def body(q_ref, k_ref, v_ref, o_ref):
        i = pl.program_id(0)
        masks = _sb_head_masks()
        tri = _sb_tri(True)
        qs = _stack_heads(q_ref[...] * (HEAD_DIM ** -0.5), masks).astype(BF16)
        n_tiles = (i * Q_BLOCK) // SB_KT + 1

        def tile(t, carry, masked):
            kb = k_ref[_sb_rows(t), :].astype(BF16)
            vb = v_ref[_sb_rows(t), :].astype(BF16)
            acc, c = carry
            _, _, w, c = _sb_tile(qs, kb, _sb_past(i, t) if masked else None, c, tri)
            pv = lax.dot_general(w.astype(BF16), vb, NN, preferred_element_type=F32)
            return acc + _unstack_heads(pv, masks), c

        carry = tile(n_tiles - 1, (jnp.zeros((Q_BLOCK, D_ATT), F32), jnp.zeros((SB_ROWS, 1), F32)), True)
        carry = lax.fori_loop(0, n_tiles - 1, lambda tt, cr: tile(n_tiles - 2 - tt, cr, False), carry)
        o_ref[...] = carry[0]

    cb = COL_QS // D_ATT
    return pl.pallas_call(
        body, name="sb_fwd", grid=(T // Q_BLOCK,),
        in_specs=[pl.BlockSpec((Q_BLOCK, D_ATT), lambda i: (i, cb)),
                  pl.BlockSpec((T, D_ATT), lambda i: (0, cb + 1)),
                  pl.BlockSpec((T, D_ATT), lambda i: (0, cb + 2))],
        out_specs=pl.BlockSpec((Q_BLOCK, D_ATT), lambda i: (i, 0)),
        out_shape=jax.ShapeDtypeStruct((T, D_ATT), F32),
        compiler_params=_params(dimension_semantics=("arbitrary",)),
    )(proj, proj, proj)


def _sb_bwd(proj, do, o):
    T = proj.shape[0]

    def body(q_ref, k_ref, v_ref, do_ref, o_ref, dq_ref, dk_ref, dv_ref):
        i = pl.program_id(0)
        masks = _sb_head_masks()
        tri = _sb_tri(True)
        tri_incl = _sb_tri(False)

        @pl.when(i == 0)
        def _():
            dk_ref[...] = jnp.zeros_like(dk_ref)
            dv_ref[...] = jnp.zeros_like(dv_ref)

        qs = _stack_heads(q_ref[...] * (HEAD_DIM ** -0.5), masks).astype(BF16)
        dos = _stack_heads(do_ref[...], masks).astype(BF16)
        delta = jnp.sum(dos.astype(F32) * jnp.concatenate([o_ref[...]] * SB_HEADS, axis=0), axis=1, keepdims=True)
        n_tiles = (i * Q_BLOCK) // SB_KT + 1

        def tile(t, carry, masked):
            rows = _sb_rows(t)
            kb = k_ref[rows, :].astype(BF16)
            vb = v_ref[rows, :].astype(BF16)
            past = _sb_past(i, t) if masked else None
            dq, c, ce = carry
            z, lsz, w, c = _sb_tile(qs, kb, past, c, tri)
            gv = lax.dot_general(dos, vb, NT, preferred_element_type=F32)
            wb = w.astype(BF16)
            e = wb.astype(F32) * gv
            suf, ce = _suffix(e, ce, tri_incl)
            dz = e * jnp.exp(lsz - z) - (delta - suf) * jnp.exp(lsz)
            if masked:
                dz = jnp.where(past, dz, 0.0)
            dzb = dz.astype(BF16)
            dq = dq + _unstack_heads(lax.dot_general(dzb, kb, NN, preferred_element_type=F32), masks)
            dk_ref[rows, :] = dk_ref[rows, :] + lax.dot_general(dzb, qs, TN, preferred_element_type=F32)
            dv_ref[rows, :] = dv_ref[rows, :] + lax.dot_general(wb, dos, TN, preferred_element_type=F32)
            return dq, c, ce

        zcol = jnp.zeros((SB_ROWS, 1), F32)
        carry = tile(n_tiles - 1, (jnp.zeros((Q_BLOCK, D_ATT), F32), zcol, zcol), True)
        carry = lax.fori_loop(0, n_tiles - 1, lambda tt, cr: tile(n_tiles - 2 - tt, cr, False), carry)
        dq_ref[...] = carry[0] * (HEAD_DIM ** -0.5)

    cb = COL_QS // D_ATT
    blk = pl.BlockSpec((Q_BLOCK, D_ATT), lambda i: (i, 0))
    full = pl.BlockSpec((T, D_ATT), lambda i: (0, 0))
    sh = jax.ShapeDtypeStruct((T, D_ATT), F32)
    return pl.pallas_call(
        body, name="sb_bwd", grid=(T // Q_BLOCK,),
        in_specs=[pl.BlockSpec((Q_BLOCK, D_ATT), lambda i: (i, cb)),
                  pl.BlockSpec((T, D_ATT), lambda i: (0, cb + 1)),
                  pl.BlockSpec((T, D_ATT), lambda i: (0, cb + 2)), blk, blk],
        out_specs=[blk, full, full], out_shape=[sh, sh, sh],
        compiler_params=_params(dimension_semantics=("arbitrary",)),
    )(proj, proj, proj, do, o)


def _tok(c, by=None):
    if by is None:
        return pl.BlockSpec((TM, c), lambda i, j, k: (i, 0))
    if by == 1:
        return pl.BlockSpec((TM, c), lambda i, j, k: (i, j))
    return pl.BlockSpec((TM, c), lambda i, j, k: (i, k))


def _chunked(c, by):
    if by == 1:
        return pl.BlockSpec((None, TM, c), lambda i, j, k: (j, i, 0))
    return pl.BlockSpec((None, TM, c), lambda i, j, k: (k, i, 0))


def _gain_spec():
    return pl.BlockSpec((1, D_MODEL), lambda i, j, k: (0, 0))


def _all_chunks(rows, c):
    return pl.BlockSpec((N_CHIPS, rows, c), lambda i, j, k: (0, i, 0))


def _wfull(r, c, l):
    return pl.BlockSpec((N_CHIPS, None, r, c), lambda i, j, k: (0, l, 0, 0))


def _pick(idx, c):
    return lambda ins: ins[idx][c]


def _cols(idx, c, w):
    return lambda ins: ins[idx][:, c * w:(c + 1) * w]


def _ffn_fwd(x, gain, wg, wu, wd, l):
    T = x.shape[0]
    ffs = wg.shape[3]
    h = _rms_fwd(x, gain)

    def swiglu(vals, ins, outs, i):
        gt, up = vals
        outs[0][...] = gt.astype(BF16)
        outs[1][...] = up.astype(BF16)
        outs[2][...] = (gt * _sigmoid(gt) * up).astype(BF16)

    csh = jax.ShapeDtypeStruct((N_CHIPS, T, ffs), BF16)
    gate, up, act = _mm(
        "ffn_up", [h, wg, wu], [_tok(D_MODEL), _wspec(D_MODEL, ffs, l, 1), _wspec(D_MODEL, ffs, l, 1)],
        [(0, 1, 0), (0, 2, 1)], 2, None, NN, (T // TM, N_CHIPS, 1), swiglu,
        [csh, csh, csh], [_chunked(ffs, 1)] * 3, j_outer=True)

    def resid(vals, ins, outs, i):
        outs[0][...] = ins[2][...] + 0.5 * vals[0]

    (y,) = _mm(
        "ffn_down", [act, wd, x], [_all_chunks(TM, ffs), _wfull(ffs, D_MODEL, l), _tok(D_MODEL)],
        [(_pick(0, c), _pick(1, c), 0) for c in range(N_CHIPS)], 1, None, NN, (T // TM, 1, 1), resid,
        [jax.ShapeDtypeStruct((T, D_MODEL), F32)], [_tok(D_MODEL)])
    return y, (x, h, gate, up, act)


def _ffn_bwd(dxo, gain, wg, wu, wd, l, saved):
    x, h, gate, up, act = saved
    T = x.shape[0]
    ffs = wg.shape[3]
    tk = TM
    tm = TM // 2

    def dswiglu(vals, ins, outs, i):
        for c in range(N_CHIPS):
            da = 0.5 * vals[c]
            gt = ins[2][c].astype(F32)
            u = ins[3][c].astype(F32)
            s = _sigmoid(gt)
            outs[0][c] = (da * u * (s * (1.0 + gt * (1.0 - s)))).astype(BF16)
            outs[1][c] = (da * (gt * s)).astype(BF16)

    csh = jax.ShapeDtypeStruct((N_CHIPS, T, ffs), BF16)
    row = pl.BlockSpec((tm, D_MODEL), lambda i, j, k: (i, 0))
    dgate, dup = _mm(
        "ffn_dact", [dxo, wd, gate, up],
        [row, _wfull(ffs, D_MODEL, l), _all_chunks(tm, ffs), _all_chunks(tm, ffs)],
        [(0, _pick(1, c), c) for c in range(N_CHIPS)], N_CHIPS, None, NT, (T // tm, 1, 1), dswiglu,
        [csh, csh], [_all_chunks(tm, ffs)] * 2)

    def halves(vals, ins, outs, i):
        for c in range(N_CHIPS):
            outs[0][c] = (0.5 * vals[c]).astype(BF16)

    def casts(vals, ins, outs, i):
        for c in range(N_CHIPS):
            outs[0][c] = vals[c].astype(BF16)

    tok_k = pl.BlockSpec((tk, D_MODEL), lambda i, j, k: (k, 0))
    chunks_k = pl.BlockSpec((N_CHIPS, tk, ffs), lambda i, j, k: (0, k, 0))
    (dwd,) = _mm(
        "ffn_dwd", [act, dxo], [chunks_k, tok_k], [(_pick(0, c), 1, c) for c in range(N_CHIPS)], N_CHIPS,
        (ffs, D_MODEL), TN, (1, 1, T // tk), halves, [jax.ShapeDtypeStruct((N_CHIPS, ffs, D_MODEL), BF16)],
        [pl.BlockSpec((N_CHIPS, ffs, D_MODEL), lambda i, j, k: (0, 0, 0))])

    dx, dgain = _mm(
        "ffn_dx", [dgate, dup, wg, wu, x, gain, dxo],
        [_all_chunks(tm, ffs), _all_chunks(tm, ffs), _wfull(D_MODEL, ffs, l), _wfull(D_MODEL, ffs, l),
         row, _gain_spec(), row],
        [(_pick(a, c), _pick(a + 2, c), 0) for c in range(N_CHIPS) for a in range(2)], 1, None, NT,
        (T // tm, 1, 1), _rms_bwd_epilogue(4, 5, 6),
        [jax.ShapeDtypeStruct((T, D_MODEL), F32), jax.ShapeDtypeStruct((8, D_MODEL), F32)],
        [row, pl.BlockSpec((8, D_MODEL), lambda i, j, k: (0, 0))])

    wsh = jax.ShapeDtypeStruct((N_CHIPS, D_MODEL, ffs), BF16)
    wout = pl.BlockSpec((N_CHIPS, D_MODEL, ffs), lambda i, j, k: (0, 0, 0))
    dws = []
    for dact in (dgate, dup):
        dws += _mm("ffn_dwgu", [h, dact], [tok_k, chunks_k], [(0, _pick(1, c), c) for c in range(N_CHIPS)],
                   N_CHIPS, (D_MODEL, ffs), TN, (1, 1, T // tk), casts, [wsh], [wout])
    return dx, dgain, dws[0], dws[1], dwd


def _mixer_fwd(x, gain, W, l, tabs):
    T = x.shape[0]
    win, wpd, wps, wo = W["w_in"], W["w_proj_dil"], W["w_proj_sb"], W["w_out"]
    cin = win.shape[3]
    cp = wpd.shape[3]
    h = _rms_fwd(x, gain)

    n_rope = 6 * D_ATT

    def roped(vals, ins, outs, i):
        v = vals[0]
        col0 = pl.program_id(0) * cin

        @pl.when(col0 < n_rope)
        def _():
            on = lax.broadcasted_iota(jnp.int32, v.shape, 1) + col0 < n_rope
            c = jnp.where(on, jnp.concatenate([ins[2][...]] * (cin // 128), axis=1), 1.0)
            s1 = jnp.where(on, jnp.concatenate([ins[3][...]] * (cin // 128), axis=1), 0.0)
            s2 = jnp.where(on, jnp.concatenate([ins[4][...]] * (cin // 128), axis=1), 0.0)
            outs[0][...] = _rope_fwd(v, c, s1, s2)

        @pl.when(col0 >= n_rope)
        def _():
            outs[0][...] = v

    (proj,) = _mm(
        "mix_in", [h, win, *tabs], [_tok(D_MODEL), _wspec(D_MODEL, cin, l, 1)] + [_tok(128)] * 3, [(0, 1, 0)], 1,
        None, NN, (T // TM, N_CHIPS, 1), roped, [jax.ShapeDtypeStruct((T, N_CHIPS * cin), F32)], [_tok(cin, 1)],
        j_outer=True)

    os_, lses = [], []
    for g, (window, dil) in enumerate(DIL_GROUPS):
        o_g, lse_g = _dil_fwd(proj, g, dil)
        os_.append(o_g)
        lses.append(lse_g)
    o_dil, lse = _dil_merge(os_, lses)
    o_sb = _sb_fwd(proj)

    def gated(vals, ins, outs, i):
        pd, ps = vals
        outs[0][...] = (_sigmoid(ins[4][...]) * pd + _sigmoid(ins[5][...]) * ps).astype(BF16)
        outs[1][...] = pd.astype(BF16)
        outs[2][...] = ps.astype(BF16)

    gd0, gs0 = COL_GD // cp, COL_GS // cp
    ush = jax.ShapeDtypeStruct((T, D_MODEL), BF16)
    u, pd, ps = _mm(
        "mix_gate", [o_dil, o_sb, wpd, wps, proj, proj],
        [_tok(D_ATT), _tok(D_ATT), _wspec(D_ATT, cp, l, 1), _wspec(D_ATT, cp, l, 1),
         pl.BlockSpec((TM, cp), lambda i, j, k: (i, gd0 + j)), pl.BlockSpec((TM, cp), lambda i, j, k: (i, gs0 + j))],
        [(0, 2, 0), (1, 3, 1)], 2, None, NN, (T // TM, N_CHIPS, 1), gated, [ush] * 3, [_tok(cp, 1)] * 3)

    def resid(vals, ins, outs, i):
        outs[0][...] = ins[2][...] + vals[0]

    (y,) = _mm(
        "mix_out", [u, wo, x], [_tok(D_MODEL), _wfull(cp, D_MODEL, l), _tok(D_MODEL)],
        [(_cols(0, c, cp), _pick(1, c), 0) for c in range(N_CHIPS)], 1, None, NN, (T // TM, 1, 1), resid,
        [jax.ShapeDtypeStruct((T, D_MODEL), F32)], [_tok(D_MODEL)])
    return y, (x, h, proj, o_dil, lse, o_sb, u, pd, ps)


def _mixer_bwd(dxo, gain, W, l, tabs, saved):
    x, h, proj, o_dil, lse, o_sb, u, pd, ps = saved
    T = x.shape[0]
    win, wpd, wps, wo = W["w_in"], W["w_proj_dil"], W["w_proj_sb"], W["w_out"]
    cin = win.shape[3]
    cp = wpd.shape[3]
    tk = TM
    tm = TM // 2
    row = pl.BlockSpec((tm, D_MODEL), lambda i, j, k: (i, 0))
    gd0, gs0 = COL_GD // cp, COL_GS // cp

    def dgated(vals, ins, outs, i):
        du = vals[0]
        sd = _sigmoid(ins[4][...])
        ss = _sigmoid(ins[5][...])
        outs[0][...] = (du * sd).astype(BF16)
        outs[1][...] = (du * ss).astype(BF16)
        outs[2][...] = (du * ins[2][...].astype(F32) * sd * (1.0 - sd)).astype(BF16)
        outs[3][...] = (du * ins[3][...].astype(F32) * ss * (1.0 - ss)).astype(BF16)

    ush = jax.ShapeDtypeStruct((T, D_MODEL), BF16)
    dpd, dps, dgd, dgs = _mm(
        "mix_du", [dxo, wo, pd, ps, proj, proj],
        [_tok(D_MODEL), _wspec(cp, D_MODEL, l, 1), _tok(cp, 1), _tok(cp, 1),
         pl.BlockSpec((TM, cp), lambda i, j, k: (i, gd0 + j)), pl.BlockSpec((TM, cp), lambda i, j, k: (i, gs0 + j))],
        [(0, 1, 0)], 1, None, NT, (T // TM, N_CHIPS, 1), dgated, [ush] * 4, [_tok(cp, 1)] * 4)

    def one(vals, ins, outs, i):
        outs[0][...] = vals[0].astype(BF16)

    def two(vals, ins, outs, i):
        outs[0][...] = vals[0].astype(BF16)
        outs[1][...] = vals[1].astype(BF16)

    (dwo,) = _mm(
        "mix_dwo", [u, dxo],
        [pl.BlockSpec((tk, cp), lambda i, j, k: (k, j)), pl.BlockSpec((tk, D_MODEL), lambda i, j, k: (k, 0))],
        [(0, 1, 0)], 1, (cp, D_MODEL), TN, (1, N_CHIPS, T // tk), one,
        [jax.ShapeDtypeStruct((N_CHIPS, cp, D_MODEL), BF16)],
        [pl.BlockSpec((None, cp, D_MODEL), lambda i, j, k: (j, 0, 0))])

    def plain2(vals, ins, outs, i):
        outs[0][...] = vals[0]
        outs[1][...] = vals[1]

    ash = jax.ShapeDtypeStruct((T, D_ATT), F32)
    do_dil, do_sb = _mm(
        "mix_do", [dpd, dps, wpd, wps], [_tok(cp, 2), _tok(cp, 2), _wspec(D_ATT, cp, l, 2), _wspec(D_ATT, cp, l, 2)],
        [(0, 2, 0), (1, 3, 1)], 2, (TM, D_ATT), NT, (T // TM, 1, N_CHIPS), plain2, [ash, ash], [_tok(D_ATT)] * 2)

    psh = jax.ShapeDtypeStruct((N_CHIPS, D_ATT, cp), BF16)
    pspec = pl.BlockSpec((None, D_ATT, cp), lambda i, j, k: (j, 0, 0))
    arow = pl.BlockSpec((tk, D_ATT), lambda i, j, k: (k, 0))
    dcol = pl.BlockSpec((tk, cp), lambda i, j, k: (k, j))
    dwpd, dwps = _mm(
        "mix_dwp", [o_dil, o_sb, dpd, dps], [arow, arow, dcol, dcol], [(0, 2, 0), (1, 3, 1)], 2, (D_ATT, cp), TN,
        (1, N_CHIPS, T // tk), two, [psh, psh], [pspec, pspec])

    dqs, dks, dvs = [], [], []
    for g, (window, dil) in enumerate(DIL_GROUPS):
        dq, dk, dv = _dil_bwd(proj, do_dil, o_dil, lse, g, dil)
        dqs.append(dq)
        dks.append(dk)
        dvs.append(dv)
    dq_s, dk_s, dv_s = _sb_bwd(proj, do_sb, o_sb)
    dproj = _assemble_dproj(dqs + dks, dvs + [dq_s, dk_s, dv_s], [dgd, dgs], tabs)

    dx, dgain = _mm(
        "mix_dx", [dproj, win, x, gain, dxo],
        [pl.BlockSpec((tm, N_CHIPS * cin), lambda i, j, k: (i, 0)), _wfull(D_MODEL, cin, l), row, _gain_spec(), row],
        [(_cols(0, c, cin), _pick(1, c), 0) for c in range(N_CHIPS)], 1, None, NT, (T // tm, 1, 1),
        _rms_bwd_epilogue(2, 3, 4),
        [jax.ShapeDtypeStruct((T, D_MODEL), F32), jax.ShapeDtypeStruct((8, D_MODEL), F32)],
        [row, pl.BlockSpec((8, D_MODEL), lambda i, j, k: (0, 0))])

    (dwin,) = _mm(
        "mix_dwin", [h, dproj],
        [pl.BlockSpec((tk, D_MODEL), lambda i, j, k: (k, 0)), pl.BlockSpec((tk, cin), lambda i, j, k: (k, j))],
        [(0, 1, 0)], 1, (D_MODEL, cin), TN, (1, N_CHIPS, T // tk), one,
        [jax.ShapeDtypeStruct((N_CHIPS, D_MODEL, cin), BF16)],
        [pl.BlockSpec((None, D_MODEL, cin), lambda i, j, k: (j, 0, 0))])
    return dx, dgain, dwin, dwpd, dwps, dwo


def _local_step(x, target, norms, norm_final, weights_of, on_grads):
    T = x.shape[0]
    tabs = _rope_tables(T)
    saved, held = [], []
    for l in range(DEPTH):
        w1 = weights_of(l, 0, x)
        x, s1 = _ffn_fwd(x, norms["norm_ffn1"][l:l + 1], w1["ffn1_w_gate"], w1["ffn1_w_up"], w1["ffn1_w_down"], 0)
        w2 = weights_of(l, 1, x)
        x, s2 = _mixer_fwd(x, norms["norm_mix"][l:l + 1], w2, 0, tabs)
        w3 = weights_of(l, 2, x)
        x, s3 = _ffn_fwd(x, norms["norm_ffn2"][l:l + 1], w3["ffn2_w_gate"], w3["ffn2_w_up"], w3["ffn2_w_down"], 0)
        saved.append((s1, s2, s3))
        held.append((w1, w2, w3))
    dx, dg_final, loss = _final_loss(x, norm_final.reshape(1, D_MODEL), target)
    gains = [None] * DEPTH
    for l in reversed(range(DEPTH)):
        s1, s2, s3 = saved[l]
        w1, w2, w3 = held[l]
        dx, dg2, dwg2, dwu2, dwd2 = _ffn_bwd(dx, norms["norm_ffn2"][l:l + 1], w3["ffn2_w_gate"], w3["ffn2_w_up"],
                                             w3["ffn2_w_down"], 0, s3)
        on_grads(l, 2, dict(ffn2_w_gate=dwg2, ffn2_w_up=dwu2, ffn2_w_down=dwd2), dx)
        dx, dgm, dwin, dwpd, dwps, dwo = _mixer_bwd(dx, norms["norm_mix"][l:l + 1], w2, 0, tabs, s2)
        on_grads(l, 1, dict(w_in=dwin, w_proj_dil=dwpd, w_proj_sb=dwps, w_out=dwo), dx)
        dx, dg1, dwg1, dwu1, dwd1 = _ffn_bwd(dx, norms["norm_ffn1"][l:l + 1], w1["ffn1_w_gate"], w1["ffn1_w_up"],
                                             w1["ffn1_w_down"], 0, s1)
        on_grads(l, 0, dict(ffn1_w_gate=dwg1, ffn1_w_up=dwu1, ffn1_w_down=dwd1), dx)
        gains[l] = dict(norm_ffn1=dg1, norm_mix=dgm, norm_ffn2=dg2)
    return loss, dx, gains, dg_final


def _place():
    x, y, c = lax.axis_index("x"), lax.axis_index("y"), lax.axis_index("c")
    chips = [(1 - x, y), (x, 1 - y), (1 - x, 1 - y)]
    return x, y, c, chips


def _half(c, r):
    return pl.ds(pl.multiple_of(c * (r // 2), 8), r // 2)


def _cast_into_slot(w, l, me_arr):
    _, r, cw = w.shape
    tr = r // 4 if r > 256 else r

    def body(me_ref, w_ref, o_ref):
        o_ref[...] = w_ref[...].astype(BF16)

    return pl.pallas_call(
        body, name="cast_weights",
        grid_spec=pltpu.PrefetchScalarGridSpec(
            num_scalar_prefetch=1, grid=(r // tr,),
            in_specs=[pl.BlockSpec((None, tr, cw), lambda i, me: (l, i, 0))],
            out_specs=pl.BlockSpec((None, None, tr, cw), lambda i, me: (me[0], 0, i, 0))),
        out_shape=jax.ShapeDtypeStruct((N_CHIPS, 1, r, cw), BF16), compiler_params=_params(),
    )(me_arr, w)


HBM_SPEC = pl.BlockSpec(memory_space=pltpu.HBM)
SEM_SPEC = pl.BlockSpec(memory_space=pltpu.SEMAPHORE)
SPLIT_COPY = pltpu.CompilerParams(has_side_effects=pltpu.SideEffectType.DATAFLOW_SIDE_EFFECTING)


def _gather_piece(ref, chip_id, c):
    return ref.at[chip_id, 0, _half(c, ref.shape[2]), :]


def _gather_start(bufs):
    n = len(bufs)

    def body(*refs):
        out_refs = refs[n:2 * n]
        send_sems, recv_sems = refs[2 * n:]
        x, y, c, chips = _place()
        me = 2 * x + y
        for a in range(n):
            piece = _gather_piece(out_refs[a], me, c)
            for j, chip in enumerate(chips):
                pltpu.make_async_remote_copy(
                    src_ref=piece, dst_ref=piece, send_sem=send_sems.at[3 * a + j], recv_sem=recv_sems.at[3 * a + j],
                    device_id=(*chip, c), device_id_type=MESH).start()

    outs = pl.pallas_call(
        body, name="gather_start", in_specs=[HBM_SPEC] * n, out_specs=[HBM_SPEC] * n + [SEM_SPEC, SEM_SPEC],
        out_shape=[pltpu.HBM(b.shape, b.dtype) for b in bufs] + [pltpu.SemaphoreType.DMA((3 * n,))] * 2,
        input_output_aliases={a: a for a in range(n)}, compiler_params=SPLIT_COPY,
    )(*[pltpu.with_memory_space_constraint(b, pltpu.HBM) for b in bufs])
    return outs[:n], outs[n], outs[n + 1]


def _gather_wait(k, bufs, places, send_sems, recv_sems, after):
    m = len(bufs)

    def body(*refs):
        in_refs = refs[:m]
        ssem, rsem = refs[m], refs[m + 1]
        x, y, c, chips = _place()
        me = 2 * x + y
        for t, a in enumerate(places):
            for j, chip in enumerate(chips):
                cp = pltpu.make_async_remote_copy(
                    src_ref=_gather_piece(in_refs[t], me, c),
                    dst_ref=_gather_piece(in_refs[t], 2 * chip[0] + chip[1], c),
                    send_sem=ssem.at[3 * a + j], recv_sem=rsem.at[3 * a + j], device_id=(*chip, c),
                    device_id_type=MESH)
                cp.wait_send()
                cp.wait_recv()

    return pl.pallas_call(
        body, name=f"gather_wait_{k}",
        in_specs=[HBM_SPEC] * m + [SEM_SPEC, SEM_SPEC, pl.BlockSpec(memory_space=pl.ANY)], out_specs=[HBM_SPEC] * m,
        out_shape=[pltpu.HBM(b.shape, b.dtype) for b in bufs], input_output_aliases={t: t for t in range(m)},
        compiler_params=SPLIT_COPY,
    )(*bufs, send_sems, recv_sems, after)


def _gather_relay(bufs):
    n = len(bufs)

    def body(*refs):
        out_refs = refs[n:2 * n]
        send_sems, recv_sems = refs[2 * n:]
        x, y, c, chips = _place()
        cps = []
        for a in range(n):
            for j, chip in enumerate(chips):
                piece = _gather_piece(out_refs[a], 2 * chip[0] + chip[1], c)
                cps.append(pltpu.make_async_remote_copy(
                    src_ref=piece, dst_ref=piece, send_sem=send_sems.at[a, j], recv_sem=recv_sems.at[a, j],
                    device_id=(x, y, 1 - c), device_id_type=MESH))
        for cp in cps:
            cp.start()
        for a in range(n):
            for j, chip in enumerate(chips):
                theirs = _gather_piece(out_refs[a], 2 * chip[0] + chip[1], 1 - c)
                pltpu.make_async_remote_copy(
                    src_ref=theirs, dst_ref=theirs, send_sem=send_sems.at[a, j], recv_sem=recv_sems.at[a, j],
                    device_id=(x, y, 1 - c), device_id_type=MESH).wait_recv()
        for cp in cps:
            cp.wait_send()

    any_spec = pl.BlockSpec(memory_space=pl.ANY)
    return pl.pallas_call(
        body, name="gather_relay", in_specs=[any_spec] * n, out_specs=[any_spec] * n,
        out_shape=[jax.ShapeDtypeStruct(b.shape, b.dtype) for b in bufs],
        input_output_aliases={a: a for a in range(n)},
        scratch_shapes=[pltpu.SemaphoreType.DMA((n, 3))] * 2,
    )(*bufs)


def _exchange_halves(gs):
    n = len(gs)

    def body(*refs):
        g_refs, out_refs = refs[:n], refs[n:2 * n]
        send_sems, recv_sems = refs[2 * n:]
        x, y, c, _ = _place()
        cps = []
        for a in range(n):
            r = g_refs[a].shape[1]
            cps.append(pltpu.make_async_remote_copy(
                src_ref=g_refs[a].at[:, _half(1 - c, r), :], dst_ref=out_refs[a],
                send_sem=send_sems.at[a], recv_sem=recv_sems.at[a], device_id=(x, y, 1 - c), device_id_type=MESH))
        for cp in cps:
            cp.start()
        for cp in cps:
            cp.wait()

    any_spec = pl.BlockSpec(memory_space=pl.ANY)
    return pl.pallas_call(
        body, name="grad_to_sibling", in_specs=[any_spec] * n, out_specs=[any_spec] * n,
        out_shape=[jax.ShapeDtypeStruct((g.shape[0], g.shape[1] // 2, g.shape[2]), g.dtype) for g in gs],
        scratch_shapes=[pltpu.SemaphoreType.DMA((n,))] * 2,
    )(*gs)


def _add_half(g, got, c_arr):
    _, r, cw = g.shape

    def body(c_ref, a_ref, b_ref, o_ref):
        o_ref[...] = (a_ref[...].astype(F32) + b_ref[...].astype(F32)).astype(BF16)

    return pl.pallas_call(
        body, name="grad_add_half",
        grid_spec=pltpu.PrefetchScalarGridSpec(
            num_scalar_prefetch=1, grid=(N_CHIPS,),
            in_specs=[pl.BlockSpec((None, r // 2, cw), lambda k, cr: (k, cr[0], 0)),
                      pl.BlockSpec((None, r // 2, cw), lambda k, cr: (k, 0, 0))],
            out_specs=pl.BlockSpec((None, r // 2, cw), lambda k, cr: (k, 0, 0))),
        out_shape=jax.ShapeDtypeStruct((N_CHIPS, r // 2, cw), BF16), compiler_params=_params(),
    )(c_arr, g, got)


def _scatter_start(k, ss):
    n = len(ss)

    def body(*refs):
        s_refs, land_refs = refs[2 * n:3 * n], refs[3 * n:4 * n]
        send_sems, recv_sems = refs[4 * n:]
        x, y, c, chips = _place()
        me = 2 * x + y
        for a in range(n):
            for j, chip in enumerate(chips):
                pltpu.make_async_remote_copy(
                    src_ref=s_refs[a].at[2 * chip[0] + chip[1]], dst_ref=land_refs[a].at[me],
                    send_sem=send_sems.at[3 * a + j], recv_sem=recv_sems.at[3 * a + j], device_id=(*chip, c),
                    device_id_type=MESH).start()

    lands = [lax.empty(s.shape, s.dtype) for s in ss]
    hbm = [pltpu.HBM(s.shape, s.dtype) for s in ss]
    outs = pl.pallas_call(
        body, name=f"grad_scatter_start_{k}", in_specs=[HBM_SPEC] * (2 * n),
        out_specs=[HBM_SPEC] * (2 * n) + [SEM_SPEC, SEM_SPEC],
        out_shape=hbm + hbm + [pltpu.SemaphoreType.DMA((3 * n,))] * 2,
        input_output_aliases={a: a for a in range(2 * n)}, compiler_params=SPLIT_COPY,
    )(*[pltpu.with_memory_space_constraint(v, pltpu.HBM) for v in list(ss) + lands])
    return outs[:n], outs[n:2 * n], outs[2 * n], outs[2 * n + 1]


def _scatter_wait(k, ss, lands, send_sems, recv_sems, after):
    n = len(ss)

    def body(*refs):
        s_refs, land_refs = refs[:n], refs[n:2 * n]
        ssem, rsem = refs[2 * n], refs[2 * n + 1]
        x, y, c, chips = _place()
        me = 2 * x + y
        for a in range(n):
            for j, chip in enumerate(chips):
                cid = 2 * chip[0] + chip[1]
                cp = pltpu.make_async_remote_copy(
                    src_ref=s_refs[a].at[cid], dst_ref=land_refs[a].at[cid], send_sem=ssem.at[3 * a + j],
                    recv_sem=rsem.at[3 * a + j], device_id=(*chip, c), device_id_type=MESH)
                cp.wait_send()
                cp.wait_recv()

    hbm = [pltpu.HBM(s.shape, s.dtype) for s in ss]
    outs = pl.pallas_call(
        body, name=f"grad_scatter_wait_{k}",
        in_specs=[HBM_SPEC] * (2 * n) + [SEM_SPEC, SEM_SPEC, pl.BlockSpec(memory_space=pl.ANY)],
        out_specs=[HBM_SPEC] * (2 * n), out_shape=hbm + hbm,
        input_output_aliases={a: a for a in range(2 * n)}, compiler_params=SPLIT_COPY,
    )(*ss, *lands, send_sems, recv_sems, after)
    return outs[:n], outs[n:]


def _sum_chips(land, s, me_arr):
    _, rh, cw = land.shape

    def body(me_ref, land_ref, s_ref, o_ref):
        for own in range(N_CHIPS):
            @pl.when(me_ref[0] == own)
            def _(own=own):
                acc = None
                for k in range(N_CHIPS):
                    term = (s_ref[...] if k == own else land_ref[k]).astype(F32)
                    acc = term if acc is None else acc + term
                o_ref[...] = acc

    return pl.pallas_call(
        body, name="grad_sum_chips",
        grid_spec=pltpu.PrefetchScalarGridSpec(
            num_scalar_prefetch=1, grid=(1,),
            in_specs=[pl.BlockSpec((N_CHIPS, rh, cw), lambda i, me: (0, 0, 0)),
                      pl.BlockSpec((None, rh, cw), lambda i, me: (me[0], 0, 0))],
            out_specs=pl.BlockSpec((rh, cw), lambda i, me: (0, 0))),
        out_shape=jax.ShapeDtypeStruct((rh, cw), F32), compiler_params=_params(),
    )(me_arr, land, s)


def _swap_halves(fs):
    n = len(fs)

    def body(*refs):
        f_refs, out_refs = refs[:n], refs[n:2 * n]
        send_sems, recv_sems = refs[2 * n:]
        x, y, c, _ = _place()
        cps = [pltpu.make_async_remote_copy(
            src_ref=f_refs[a], dst_ref=out_refs[a], send_sem=send_sems.at[a], recv_sem=recv_sems.at[a],
            device_id=(x, y, 1 - c), device_id_type=MESH) for a in range(n)]
        for cp in cps:
            cp.start()
        for cp in cps:
            cp.wait()

    any_spec = pl.BlockSpec(memory_space=pl.ANY)
    return pl.pallas_call(
        body, name="grad_swap_halves", in_specs=[any_spec] * n, out_specs=[any_spec] * n,
        out_shape=[jax.ShapeDtypeStruct(f.shape, f.dtype) for f in fs],
        scratch_shapes=[pltpu.SemaphoreType.DMA((n,))] * 2,
    )(*fs)


def _allreduce_rows(stats):
    def body(s_ref, o_ref, buf, send_sems, recv_sems):
        x, y, c, _ = _place()
        me = 4 * x + 2 * y + c
        buf[me] = s_ref[...]
        cps = []
        for k in range(1, 8):
            px = jnp.where(k & 4, 1 - x, x)
            py = jnp.where(k & 2, 1 - y, y)
            pc = jnp.where(k & 1, 1 - c, c)
            cps.append(pltpu.make_async_remote_copy(
                src_ref=s_ref, dst_ref=buf.at[me], send_sem=send_sems.at[k - 1], recv_sem=recv_sems.at[k - 1],
                device_id=(px, py, pc), device_id_type=MESH))
        for cp in cps:
            cp.start()
        for cp in cps:
            cp.wait()
        acc = buf[0]
        for d in range(1, 8):
            acc = acc + buf[d]
        o_ref[...] = acc

    vm = pl.BlockSpec(memory_space=pltpu.VMEM)
    return pl.pallas_call(
        body, name="allreduce_rows", in_specs=[vm], out_specs=vm,
        out_shape=jax.ShapeDtypeStruct(stats.shape, F32),
        scratch_shapes=[pltpu.VMEM((8,) + stats.shape, F32), pltpu.SemaphoreType.DMA((7,)),
                        pltpu.SemaphoreType.DMA((7,))],
    )(stats)


def _adamw_math(w, g, m, v):
    m = ADAM_B1 * m + (1.0 - ADAM_B1) * g
    v = ADAM_B2 * v + (1.0 - ADAM_B2) * (g * g)
    m_hat = m / (1.0 - ADAM_B1 ** ADAM_STEP)
    v_hat = v / (1.0 - ADAM_B2 ** ADAM_STEP)
    delta = -ADAM_LR * (m_hat / (jnp.sqrt(v_hat) + ADAM_EPS) + ADAM_WD * w)
    return delta, m, v


def _adamw(w, m, v, mine, theirs, l, c_arr, earlier):
    L, r, cw = w.shape
    tr = r // 4 if r > 256 else r // 2
    nblk = (r // 2) // tr

    def body(c_ref, w_ref, m_ref, v_ref, a_ref, b_ref, *rest):
        go_ref, d_ref, mo_ref, vo_ref = rest[-4:]
        g = jnp.where(pl.program_id(0) == c_ref[0], a_ref[...], b_ref[...])
        delta, mn, vn = _adamw_math(w_ref[...], g, m_ref[...], v_ref[...])
        go_ref[...] = g
        d_ref[...] = delta
        mo_ref[...] = mn
        vo_ref[...] = vn

    blk = pl.BlockSpec((None, tr, cw), lambda hh, i, cr: (l, hh * nblk + i, 0))
    half = pl.BlockSpec((tr, cw), lambda hh, i, cr: (i, 0))
    sh = jax.ShapeDtypeStruct(w.shape, F32)
    held = [] if earlier is None else list(earlier)
    return pl.pallas_call(
        body, name="adamw",
        grid_spec=pltpu.PrefetchScalarGridSpec(
            num_scalar_prefetch=1, grid=(2, nblk),
            in_specs=[blk, blk, blk, half, half] + [pl.BlockSpec(memory_space=pl.ANY)] * len(held),
            out_specs=[blk] * 4),
        out_shape=[sh] * 4, input_output_aliases={6 + t: t for t in range(len(held))},
        compiler_params=_params(),
    )(c_arr, w, m, v, mine, theirs, *held)


def _adamw_rows(w, m, v, g):
    def body(w_ref, m_ref, v_ref, g_ref, d_ref, mo_ref, vo_ref):
        delta, mn, vn = _adamw_math(w_ref[...], g_ref[...], m_ref[...], v_ref[...])
        d_ref[...] = delta
        mo_ref[...] = mn
        vo_ref[...] = vn

    vm = pl.BlockSpec(memory_space=pltpu.VMEM)
    sh = jax.ShapeDtypeStruct(w.shape, F32)
    return pl.pallas_call(body, name="adamw_rows", in_specs=[vm] * 4, out_specs=[vm] * 3, out_shape=[sh] * 3)(w, m, v, g)


SUBLAYERS = (("ffn1_w_gate", "ffn1_w_up", "ffn1_w_down"), ("w_in", "w_proj_dil", "w_proj_sb", "w_out"),
             ("ffn2_w_gate", "ffn2_w_up", "ffn2_w_down"))
LAG = 2


def _pick_row(blocks):
    row = lax.broadcasted_iota(jnp.int32, (8, D_MODEL), 0)
    out = jnp.zeros((8, D_MODEL), F32)
    for i, b in enumerate(blocks):
        out = out + jnp.where(row == i, b, 0.0)
    return out


def kernel(x, norm_ffn1, ffn1_w_gate, ffn1_w_up, ffn1_w_down, norm_mix, w_in, w_proj_dil, w_proj_sb, w_out, norm_ffn2, ffn2_w_gate, ffn2_w_up, ffn2_w_down, norm_final, loss_target, m_norm_ffn1, m_ffn1_w_gate, m_ffn1_w_up, m_ffn1_w_down, m_norm_mix, m_w_in, m_w_proj_dil, m_w_proj_sb, m_w_out, m_norm_ffn2, m_ffn2_w_gate, m_ffn2_w_up, m_ffn2_w_down, m_norm_final, v_norm_ffn1, v_ffn1_w_gate, v_ffn1_w_up, v_ffn1_w_down, v_norm_mix, v_w_in, v_w_proj_dil, v_w_proj_sb, v_w_out, v_norm_ffn2, v_ffn2_w_gate, v_ffn2_w_up, v_ffn2_w_down, v_norm_final):
    given = dict(locals())
    weights = {n: given[n] for n in WEIGHT_NAMES}
    norms = {n: given[n] for n in NORM_NAMES}

    c_arr = lax.axis_index("c").astype(jnp.int32).reshape(1)
    me_arr = (2 * lax.axis_index("x") + lax.axis_index("y")).astype(jnp.int32).reshape(1)
    order = [(l, s, n) for l in range(DEPTH) for s in range(len(SUBLAYERS)) for n in SUBLAYERS[s]]
    bufs, send_sems, recv_sems = _gather_start([_cast_into_slot(weights[n], l, me_arr) for l, s, n in order])

    def weights_of(l, s, after):
        places = [i for i, (ll, ss, _) in enumerate(order) if (ll, ss) == (l, s)]
        got = _gather_wait(len(SUBLAYERS) * l + s, [bufs[i] for i in places], places, send_sems, recv_sems, after)
        return {order[i][2]: g for i, g in zip(places, _gather_relay(got))}

    out = {}
    in_flight = []

    def finish(l, s, names, sums, lands, ssem, rsem, after):
        sums, lands = _scatter_wait(len(SUBLAYERS) * l + s, sums, lands, ssem, rsem, after)
        mine = [_sum_chips(land, sm, me_arr) for land, sm in zip(lands, sums)]
        theirs = _swap_halves(mine)
        for n, ga, gb in zip(names, mine, theirs):
            out[n] = _adamw(weights[n], given["m_" + n], given["v_" + n], ga, gb, l, c_arr, out.get(n))

    def on_grads(l, s, grads, after):
        names = list(grads)
        gs = [grads[n] for n in names]
        sums = [_add_half(g, got, c_arr) for g, got in zip(gs, _exchange_halves(gs))]
        in_flight.append((l, s, names) + _scatter_start(len(SUBLAYERS) * l + s, sums))
        if len(in_flight) > LAG:
            finish(*in_flight.pop(0), after)

    loss_blk, grad_x, gains, dg_final = _local_step(x[0], loss_target[0], norms, norm_final, weights_of, on_grads)
    while in_flight:
        finish(*in_flight.pop(0), grad_x)
    out = {k + n: v for n, res in out.items() for k, v in zip(("grad_", "delta_", "new_m_", "new_v_"), res)}
    out["grad_x"] = grad_x[None]

    rows = [gains[l][n] for n in NORM_NAMES for l in range(DEPTH)] + [dg_final, loss_blk]
    total = _allreduce_rows(_pick_row(rows))
    out["loss"] = total[7, 0]
    wn = jnp.concatenate([given[n] for n in NORM_NAMES] + [norm_final[None], jnp.zeros((1, D_MODEL), F32)])
    mn_ = jnp.concatenate([given["m_" + n] for n in NORM_NAMES] + [m_norm_final[None], jnp.zeros((1, D_MODEL), F32)])
    vn_ = jnp.concatenate([given["v_" + n] for n in NORM_NAMES] + [v_norm_final[None], jnp.ones((1, D_MODEL), F32)])
    d_n, m_n, v_n = _adamw_rows(wn, mn_, vn_, total)
    for i, n in enumerate(NORM_NAMES):
        sl = slice(i * DEPTH, (i + 1) * DEPTH)
        out["grad_" + n], out["delta_" + n], out["new_m_" + n], out["new_v_" + n] = total[sl], d_n[sl], m_n[sl], v_n[sl]
    out["grad_norm_final"], out["delta_norm_final"] = total[6], d_n[6]
    out["new_m_norm_final"], out["new_v_norm_final"] = m_n[6], v_n[6]

    names = ["norm_ffn1", "ffn1_w_gate", "ffn1_w_up", "ffn1_w_down", "norm_mix", "w_in", "w_proj_dil", "w_proj_sb",
             "w_out", "norm_ffn2", "ffn2_w_gate", "ffn2_w_up", "ffn2_w_down", "norm_final"]
    return (out["loss"], out["grad_x"], *[out["grad_" + n] for n in names], *[out["delta_" + n] for n in names],
            *[out["new_m_" + n] for n in names], *[out["new_v_" + n] for n in names])
```

```python
import functools

import jax
import jax.numpy as jnp
from jax import lax
from jax.experimental import pallas as pl
from jax.experimental.pallas import tpu as pltpu

F32 = jnp.float32
BF16 = jnp.bfloat16

D_MODEL = 1024
DEPTH = 2
N_CHIPS = 4
HEAD_DIM = 64
ROPE_DIM = 16
ROPE_THETA = 500000.0
DIL_GROUPS = ((128, 1), (512, 4), (2048, 16))
SPAN = 128
Q_BLOCK = 128
RMS_EPS = 1e-6
D_ATT = 256
COL_QS = 2304
COL_GD = 3072
COL_GS = 4096
ADAM_LR, ADAM_B1, ADAM_B2, ADAM_EPS, ADAM_WD, ADAM_STEP = 0.001, 0.9, 0.999, 1e-08, 0.01, 10

VMEM_LIMIT = 52 * 1024 * 1024
TM = 512
NEG = -1e30

NN = (((1,), (0,)), ((), ()))
NT = (((1,), (1,)), ((), ()))
TN = (((0,), (0,)), ((), ()))
MESH = pl.DeviceIdType.MESH

WEIGHT_NAMES = ("ffn1_w_gate", "ffn1_w_up", "ffn1_w_down", "w_in", "w_proj_dil",
                "w_proj_sb", "w_out", "ffn2_w_gate", "ffn2_w_up", "ffn2_w_down")
NORM_NAMES = ("norm_ffn1", "norm_mix", "norm_ffn2")


def _params(**kw):
    return pltpu.CompilerParams(vmem_limit_bytes=VMEM_LIMIT, **kw)


def _sigmoid(x):
    return 0.5 * jnp.tanh(0.5 * x) + 0.5


def _mm_body(pairs, n_in, n_out, n_acc, dims, nk, i_axis, epilogue, *refs):
    ins = refs[:n_in]
    outs = refs[n_in:n_in + n_out]
    accs = refs[n_in + n_out:]
    i = pl.program_id(i_axis)
    k = pl.program_id(2)

    def operand(a):
        return (a(ins) if callable(a) else ins[a][...]).astype(BF16)

    def dot(ia, ib):
        return lax.dot_general(operand(ia), operand(ib), dims, preferred_element_type=F32)

    if nk == 1:
        parts = [None] * n_acc
        for ia, ib, ic in pairs:
            parts[ic] = dot(ia, ib) if parts[ic] is None else parts[ic] + dot(ia, ib)
        epilogue(parts, ins, outs, i)
        return

    @pl.when(k == 0)
    def _():
        for c in range(n_acc):
            accs[c][...] = jnp.zeros_like(accs[c])

    for ia, ib, ic in pairs:
        accs[ic][...] += dot(ia, ib)

    @pl.when(k == nk - 1)
    def _():
        epilogue([a[...] for a in accs], ins, outs, i)


def _j_outer(spec):
    f = spec.index_map
    return pl.BlockSpec(spec.block_shape, lambda j, i, k: f(i, j, k))


def _mm(name, ins, in_specs, pairs, n_acc, acc_shape, dims, grid, epilogue, out_shapes, out_specs, j_outer=False):
    nk = grid[2]
    if j_outer:
        grid = (grid[1], grid[0], grid[2])
        in_specs = [_j_outer(s) for s in in_specs]
        out_specs = [_j_outer(s) for s in out_specs]
    scratch = [pltpu.VMEM(acc_shape, F32) for _ in range(n_acc)] if nk > 1 else []
    body = functools.partial(_mm_body, tuple(pairs), len(ins), len(out_shapes), n_acc, dims, nk,
                             1 if j_outer else 0, epilogue)
    return pl.pallas_call(
        body, name=name, grid=grid, in_specs=in_specs, out_specs=out_specs, out_shape=out_shapes,
        scratch_shapes=scratch,
        compiler_params=_params(dimension_semantics=("arbitrary", "arbitrary", "arbitrary")),
    )(*ins)


def _wspec(r, c, l, by):
    if by == 1:
        return pl.BlockSpec((None, None, r, c), lambda i, j, k: (j, l, 0, 0))
    return pl.BlockSpec((None, None, r, c), lambda i, j, k: (k, l, 0, 0))


def _rms_bwd_epilogue(x_idx, g_idx, dxo_idx):
    def ep(vals, ins, outs, i):
        dh = vals[0]
        x = ins[x_idx][...]
        g = ins[g_idx][...]
        rstd = lax.rsqrt(jnp.mean(x * x, axis=-1, keepdims=True) + RMS_EPS)
        xhat = x * rstd
        dxhat = dh * g
        dx = rstd * (dxhat - xhat * jnp.mean(dxhat * xhat, axis=-1, keepdims=True))
        outs[0][...] = ins[dxo_idx][...] + dx
        dg = jnp.broadcast_to(jnp.sum(dh * xhat, axis=0, keepdims=True), outs[1].shape)

        @pl.when(i == 0)
        def _():
            outs[1][...] = dg

        @pl.when(i > 0)
        def _():
            outs[1][...] += dg
    return ep


def _rms_fwd(x, gain):
    T = x.shape[0]

    def body(x_ref, g_ref, h_ref):
        xv = x_ref[...]
        h = xv * lax.rsqrt(jnp.mean(xv * xv, axis=-1, keepdims=True) + RMS_EPS)
        h_ref[...] = (h * g_ref[...]).astype(BF16)

    return pl.pallas_call(
        body, name="rms_fwd", grid=(T // TM,),
        in_specs=[pl.BlockSpec((TM, D_MODEL), lambda i: (i, 0)), pl.BlockSpec((1, D_MODEL), lambda i: (0, 0))],
        out_specs=pl.BlockSpec((TM, D_MODEL), lambda i: (i, 0)),
        out_shape=jax.ShapeDtypeStruct((T, D_MODEL), BF16), compiler_params=_params(),
    )(x, gain)


def _rope_tables(T):
    pos = jnp.arange(T, dtype=F32)
    inv_freq = ROPE_THETA ** (-jnp.arange(0, ROPE_DIM, 2, dtype=F32) / ROPE_DIM)
    ang = pos[:, None] * inv_freq[None, :]
    cos, sin = jnp.cos(ang), jnp.sin(ang)
    half = ROPE_DIM // 2
    one = jnp.ones((T, HEAD_DIM - ROPE_DIM), F32)
    zero = jnp.zeros((T, HEAD_DIM - ROPE_DIM), F32)
    zh = jnp.zeros((T, half), F32)
    c = jnp.concatenate([cos, cos, one], axis=1)
    s1 = jnp.concatenate([-sin, zh, zero], axis=1)
    s2 = jnp.concatenate([zh, sin, zero], axis=1)
    return tuple(jnp.concatenate([t, t], axis=1) for t in (c, s1, s2))


def _rope_fwd(xv, c, s1, s2):
    w = xv.shape[1]
    half = ROPE_DIM // 2
    return xv * c + pltpu.roll(xv, w - half, 1) * s1 + pltpu.roll(xv, half, 1) * s2


def _rope_bwd(dy, c, s1, s2):
    w = dy.shape[1]
    half = ROPE_DIM // 2
    return dy * c + pltpu.roll(dy * s1, half, 1) + pltpu.roll(dy * s2, w - half, 1)


def _assemble_dproj(dqk, rest, gates, tabs):
    T = gates[0].shape[0]
    n_qk, n_rest = len(dqk), len(rest)
    width = (n_qk + n_rest) * D_ATT + 2 * D_MODEL

    def body(*refs):
        ins, (c_ref, s1_ref, s2_ref), o_ref = refs[:n_qk + n_rest + 2], refs[-4:-1], refs[-1]
        c = jnp.concatenate([c_ref[...]] * 2, axis=1)
        s1 = jnp.concatenate([s1_ref[...]] * 2, axis=1)
        s2 = jnp.concatenate([s2_ref[...]] * 2, axis=1)
        for b in range(n_qk + n_rest):
            v = ins[b][...]
            if b < n_qk:
                v = _rope_bwd(v, c, s1, s2)
            o_ref[:, b * D_ATT:(b + 1) * D_ATT] = v.astype(BF16)
        off = (n_qk + n_rest) * D_ATT
        o_ref[:, off:off + D_MODEL] = ins[-2][...]
        o_ref[:, off + D_MODEL:] = ins[-1][...]

    att = pl.BlockSpec((TM, D_ATT), lambda i: (i, 0))
    wide = pl.BlockSpec((TM, D_MODEL), lambda i: (i, 0))
    tab = pl.BlockSpec((TM, 128), lambda i: (i, 0))
    return pl.pallas_call(
        body, name="assemble_dproj", grid=(T // TM,),
        in_specs=[att] * (n_qk + n_rest) + [wide, wide, tab, tab, tab],
        out_specs=pl.BlockSpec((TM, width), lambda i: (i, 0)),
        out_shape=jax.ShapeDtypeStruct((T, width), BF16), compiler_params=_params(),
    )(*dqk, *rest, *gates, *tabs)


def _dil_merge(os_, lses):
    T = os_[0].shape[0]

    def body(o0, o1, o2, l0, l1, l2, o_ref, lse_ref):
        a, b, c = l0[...], l1[...], l2[...]
        m = jnp.maximum(jnp.maximum(a, b), c)
        ea, eb, ec = jnp.exp(a - m), jnp.exp(b - m), jnp.exp(c - m)
        den = ea + eb + ec
        o_ref[...] = (ea * o0[...] + eb * o1[...] + ec * o2[...]) / den
        lse_ref[...] = m + jnp.log(den)

    blk = pl.BlockSpec((TM, D_ATT), lambda i: (i, 0))
    sh = jax.ShapeDtypeStruct((T, D_ATT), F32)
    return pl.pallas_call(
        body, name="dil_merge", grid=(T // TM,), in_specs=[blk] * 6, out_specs=[blk, blk],
        out_shape=[sh, sh], compiler_params=_params(),
    )(*os_, *lses)


def _final_loss(x, gain, target):
    T = x.shape[0]

    def body(x_ref, g_ref, t_ref, dx_ref, dg_ref, loss_ref):
        xv = x_ref[...]
        g = g_ref[...]
        rstd = lax.rsqrt(jnp.mean(xv * xv, axis=-1, keepdims=True) + RMS_EPS)
        xhat = xv * rstd
        err = xhat * g - t_ref[...]
        loss = 0.5 * jnp.sum(jnp.mean(err * err, axis=-1, keepdims=True), axis=0, keepdims=True)
        dy = err * (1.0 / D_MODEL)
        dxhat = dy * g
        dx_ref[...] = rstd * (dxhat - xhat * jnp.mean(dxhat * xhat, axis=-1, keepdims=True))
        dg = jnp.broadcast_to(jnp.sum(dy * xhat, axis=0, keepdims=True), dg_ref.shape)
        ls = jnp.broadcast_to(loss, loss_ref.shape)

        @pl.when(pl.program_id(0) == 0)
        def _():
            dg_ref[...] = dg
            loss_ref[...] = ls

        @pl.when(pl.program_id(0) > 0)
        def _():
            dg_ref[...] += dg
            loss_ref[...] += ls

    blk = pl.BlockSpec((TM, D_MODEL), lambda i: (i, 0))
    row = pl.BlockSpec((1, D_MODEL), lambda i: (0, 0))
    acc = pl.BlockSpec((8, D_MODEL), lambda i: (0, 0))
    return pl.pallas_call(
        body, name="final_loss", grid=(T // TM,), in_specs=[blk, row, blk], out_specs=[blk, acc, acc],
        out_shape=[jax.ShapeDtypeStruct((T, D_MODEL), F32), jax.ShapeDtypeStruct((8, D_MODEL), F32),
                   jax.ShapeDtypeStruct((8, D_MODEL), F32)],
        compiler_params=_params(dimension_semantics=("arbitrary",)),
    )(x, gain, target)


def _pair_masks():
    lane = lax.broadcasted_iota(jnp.int32, (SPAN, 128), 1)
    return [lane < HEAD_DIM, lane >= HEAD_DIM]


def _stack_heads(x, masks):
    return jnp.concatenate([jnp.where(m, x, 0.0) for m in masks], axis=0)


def _unstack_heads(y, masks):
    rows = y.shape[0] // len(masks)
    out = jnp.where(masks[0], y[:rows], 0.0)
    for h in range(1, len(masks)):
        out = out + jnp.where(masks[h], y[rows * h:rows * (h + 1)], 0.0)
    return out


def _dil_rows(idx, d):
    u = idx // d
    r = idx - u * d
    own = pl.ds(u * (SPAN * d) + r, SPAN, stride=d) if d > 1 else pl.ds(pl.multiple_of(u * SPAN, SPAN), SPAN)
    up = jnp.maximum(u - 1, 0)
    prev = pl.ds(up * (SPAN * d) + r, SPAN, stride=d) if d > 1 else pl.ds(pl.multiple_of(up * SPAN, SPAN), SPAN)
    return u, own, prev


def _dil_valid(u):
    qi = lax.broadcasted_iota(jnp.int32, (2 * SPAN, 2 * SPAN), 0) & (SPAN - 1)
    kj = lax.broadcasted_iota(jnp.int32, (2 * SPAN, 2 * SPAN), 1)
    in_prev = (kj < SPAN) & (kj >= qi + jnp.where(u > 0, 0, SPAN))
    return in_prev | ((kj >= SPAN) & (kj - SPAN <= qi))


def _dil_keys(ref, own, prev):
    return jnp.concatenate([ref[prev, :], ref[own, :]], axis=0).astype(BF16)


def _dil_fwd(proj, g, d):
    T = proj.shape[0]
    n_iter = T // SPAN

    def body(q_ref, k_ref, v_ref, o_ref, lse_ref):
        masks = _pair_masks()

        def step(idx, carry):
            u, own, prev = _dil_rows(idx, d)
            qs = _stack_heads(q_ref[own, :] * (HEAD_DIM ** -0.5), masks).astype(BF16)
            kk = _dil_keys(k_ref, own, prev)
            vv = _dil_keys(v_ref, own, prev)
            s = jnp.where(_dil_valid(u), lax.dot_general(qs, kk, NT, preferred_element_type=F32), NEG)
            m = jnp.max(s, axis=1, keepdims=True)
            p = jnp.exp(s - m)
            den = jnp.sum(p, axis=1, keepdims=True)
            pv = lax.dot_general(p.astype(BF16), vv, NN, preferred_element_type=F32) / den
            o_ref[own, :] = _unstack_heads(pv, masks)
            lse_ref[own, :] = _unstack_heads(jnp.broadcast_to(m + jnp.log(den), pv.shape), masks)
            return carry

        lax.fori_loop(0, n_iter, step, 0, unroll=2)

    def col(b):
        return pl.BlockSpec((T, 128), lambda p: (0, b + p))

    sh = jax.ShapeDtypeStruct((T, D_ATT), F32)
    out = pl.BlockSpec((T, 128), lambda p: (0, p))
    return pl.pallas_call(
        body, name=f"dil_fwd_d{d}", grid=(2,),
        in_specs=[col(2 * g), col(6 + 2 * g), col(12 + 2 * g)], out_specs=[out, out], out_shape=[sh, sh],
        compiler_params=_params(dimension_semantics=("arbitrary",)),
    )(proj, proj, proj)


def _dil_bwd(proj, do, o_dil, lse, g, d):
    T = proj.shape[0]
    n_iter = T // SPAN

    def body(q_ref, k_ref, v_ref, do_ref, o_ref, lse_ref, dq_ref, dk_ref, dv_ref):
        masks = _pair_masks()
        head_lanes = jnp.concatenate(masks, axis=0)

        def step(idx, carry):
            u, own, prev = _dil_rows(idx, d)
            qs = _stack_heads(q_ref[own, :] * (HEAD_DIM ** -0.5), masks).astype(BF16)
            kk = _dil_keys(k_ref, own, prev)
            vv = _dil_keys(v_ref, own, prev)
            dom = _stack_heads(do_ref[own, :], masks)
            dos = dom.astype(BF16)
            delta = jnp.sum(dom * jnp.concatenate([o_ref[own, :]] * 2, axis=0), axis=1, keepdims=True)
            lrow = jnp.max(jnp.where(head_lanes, jnp.concatenate([lse_ref[own, :]] * 2, axis=0), NEG),
                           axis=1, keepdims=True)
            s = lax.dot_general(qs, kk, NT, preferred_element_type=F32)
            p = jnp.where(_dil_valid(u), jnp.exp(s - lrow), 0.0)
            dp = lax.dot_general(dos, vv, NT, preferred_element_type=F32)
            ds = (p * (dp - delta)).astype(BF16)
            dq = lax.dot_general(ds, kk, NN, preferred_element_type=F32)
            dkk = lax.dot_general(ds, qs, TN, preferred_element_type=F32)
            dvv = lax.dot_general(p.astype(BF16), dos, TN, preferred_element_type=F32)
            dq_ref[own, :] = _unstack_heads(dq, masks) * (HEAD_DIM ** -0.5)
            dk_ref[own, :] = dkk[SPAN:]
            dv_ref[own, :] = dvv[SPAN:]
            dk_ref[prev, :] = dk_ref[prev, :] + dkk[:SPAN]
            dv_ref[prev, :] = dv_ref[prev, :] + dvv[:SPAN]
            return carry

        lax.fori_loop(0, n_iter, step, 0, unroll=2)

    def col(b):
        return pl.BlockSpec((T, 128), lambda p: (0, b + p))

    sh = jax.ShapeDtypeStruct((T, D_ATT), F32)
    return pl.pallas_call(
        body, name=f"dil_bwd_d{d}", grid=(2,),
        in_specs=[col(2 * g), col(6 + 2 * g), col(12 + 2 * g), col(0), col(0), col(0)],
        out_specs=[col(0), col(0), col(0)], out_shape=[sh, sh, sh],
        compiler_params=_params(dimension_semantics=("arbitrary",)),
    )(proj, proj, proj, do, o_dil, lse)


SB_KT = 512


def _sb_tri(strict):
    a = lax.broadcasted_iota(jnp.int32, (Q_BLOCK, Q_BLOCK), 0)
    b = lax.broadcasted_iota(jnp.int32, (Q_BLOCK, Q_BLOCK), 1)
    return jnp.where((a > b) if strict else (a >= b), 1.0, 0.0).astype(BF16)


def _suffix(x, c, tri):
    r = x.shape[0]
    nb = x.shape[1] // Q_BLOCK
    blocks = [x[:, Q_BLOCK * b:Q_BLOCK * (b + 1)] for b in range(nb)]
    hi = [b.astype(BF16) for b in blocks]
    lo = [(b - h.astype(F32)).astype(BF16) for b, h in zip(blocks, hi)]
    y = lax.dot_general(jnp.concatenate(hi + lo, axis=0), tri, NN, preferred_element_type=F32)
    outs = [None] * nb
    run = c
    for b in reversed(range(nb)):
        outs[b] = run + y[r * b:r * (b + 1)] + y[r * (nb + b):r * (nb + b + 1)]
        run = run + jnp.sum(blocks[b], axis=1, keepdims=True)
    return jnp.concatenate(outs, axis=1), run


def _sb_tile(qs, kb, past, c, tri):
    z = lax.dot_general(qs, kb, NT, preferred_element_type=F32)
    lsz = jnp.minimum(z, 0.0) - jnp.log(1.0 + jnp.exp(-jnp.abs(z)))
    lk = lsz - z
    if past is not None:
        lk = jnp.where(past, lk, 0.0)
    after, c_new = _suffix(lk, c, tri)
    w = jnp.exp(lsz + after)
    if past is not None:
        w = jnp.where(past, w, 0.0)
    return z, lsz, w, c_new


SB_HEADS = D_ATT // HEAD_DIM
SB_ROWS = SB_HEADS * Q_BLOCK


def _sb_past(i, t):
    row = lax.broadcasted_iota(jnp.int32, (SB_ROWS, SB_KT), 0) & (Q_BLOCK - 1)
    col = lax.broadcasted_iota(jnp.int32, (SB_ROWS, SB_KT), 1)
    return col + t * SB_KT < row + i * Q_BLOCK


def _sb_head_masks():
    lane = lax.broadcasted_iota(jnp.int32, (Q_BLOCK, D_ATT), 1)
    return [(lane >= HEAD_DIM * h) & (lane < HEAD_DIM * (h + 1)) for h in range(SB_HEADS)]


def _sb_rows(t):
    return pl.ds(pl.multiple_of(t * SB_KT, SB_KT), SB_KT)


def _sb_fwd(proj):
    T = proj.shape[0]

    def body(q_ref, k_ref, v_ref, o_ref):
        i = pl.program_id(0)
        masks = _sb_head_masks()
        tri = _sb_tri(True)
        qs = _stack_heads(q_ref[...] * (HEAD_DIM ** -0.5), masks).astype(BF16)
        n_tiles = (i * Q_BLOCK) // SB_KT + 1

        def tile(t, carry, masked):
            kb = k_ref[_sb_rows(t), :].astype(BF16)
            vb = v_ref[_sb_rows(t), :].astype(BF16)
            acc, c = carry
            _, _, w, c = _sb_tile(qs, kb, _sb_past(i, t) if masked else None, c, tri)
            pv = lax.dot_general(w.astype(BF16), vb, NN, preferred_element_type=F32)
            return acc + _unstack_heads(pv, masks), c

        carry = tile(n_tiles - 1, (jnp.zeros((Q_BLOCK, D_ATT), F32), jnp.zeros((SB_ROWS, 1), F32)), True)
        carry = lax.fori_loop(0, n_tiles - 1, lambda tt, cr: tile(n_tiles - 2 - tt, cr, False), carry)
        o_ref[...] = carry[0]

    cb = COL_QS // D_ATT
    return pl.pallas_call(
        body, name="sb_fwd", grid=(T // Q_BLOCK,),
        in_specs=[pl.BlockSpec((Q_BLOCK, D_ATT), lambda i: (i, cb)),
                  pl.BlockSpec((T, D_ATT), lambda i: (0, cb + 1)),
                  pl.BlockSpec((T, D_ATT), lambda i: (0, cb + 2))],
        out_specs=pl.BlockSpec((Q_BLOCK, D_ATT), lambda i: (i, 0)),
        out_shape=jax.ShapeDtypeStruct((T, D_ATT), F32),
        compiler_params=_params(dimension_semantics=("arbitrary",)),
    )(proj, proj, proj)


def _sb_bwd(proj, do, o):
    T = proj.shape[0]

    def body(q_ref, k_ref, v_ref, do_ref, o_ref, dq_ref, dk_ref, dv_ref):
        i = pl.program_id(0)
        masks = _sb_head_masks()
        tri = _sb_tri(True)
        tri_incl = _sb_tri(False)

        @pl.when(i == 0)
        def _():
            dk_ref[...] = jnp.zeros_like(dk_ref)
            dv_ref[...] = jnp.zeros_like(dv_ref)

        qs = _stack_heads(q_ref[...] * (HEAD_DIM ** -0.5), masks).astype(BF16)
        dos = _stack_heads(do_ref[...], masks).astype(BF16)
        delta = jnp.sum(dos.astype(F32) * jnp.concatenate([o_ref[...]] * SB_HEADS, axis=0), axis=1, keepdims=True)
        n_tiles = (i * Q_BLOCK) // SB_KT + 1

        def tile(t, carry, masked):
            rows = _sb_rows(t)
            kb = k_ref[rows, :].astype(BF16)
            vb = v_ref[rows, :].astype(BF16)
            past = _sb_past(i, t) if masked else None
            dq, c, ce = carry
            z, lsz, w, c = _sb_tile(qs, kb, past, c, tri)
            gv = lax.dot_general(dos, vb, NT, preferred_element_type=F32)
            wb = w.astype(BF16)
            e = wb.astype(F32) * gv
            suf, ce = _suffix(e, ce, tri_incl)
            dz = e * jnp.exp(lsz - z) - (delta - suf) * jnp.exp(lsz)
            if masked:
                dz = jnp.where(past, dz, 0.0)
            dzb = dz.astype(BF16)
            dq = dq + _unstack_heads(lax.dot_general(dzb, kb, NN, preferred_element_type=F32), masks)
            dk_ref[rows, :] = dk_ref[rows, :] + lax.dot_general(dzb, qs, TN, preferred_element_type=F32)
            dv_ref[rows, :] = dv_ref[rows, :] + lax.dot_general(wb, dos, TN, preferred_element_type=F32)
            return dq, c, ce

        zcol = jnp.zeros((SB_ROWS, 1), F32)
        carry = tile(n_tiles - 1, (jnp.zeros((Q_BLOCK, D_ATT), F32), zcol, zcol), True)
        carry = lax.fori_loop(0, n_tiles - 1, lambda tt, cr: tile(n_tiles - 2 - tt, cr, False), carry)
        dq_ref[...] = carry[0] * (HEAD_DIM ** -0.5)

    cb = COL_QS // D_ATT
    blk = pl.BlockSpec((Q_BLOCK, D_ATT), lambda i: (i, 0))
    full = pl.BlockSpec((T, D_ATT), lambda i: (0, 0))
    sh = jax.ShapeDtypeStruct((T, D_ATT), F32)
    return pl.pallas_call(
        body, name="sb_bwd", grid=(T // Q_BLOCK,),
        in_specs=[pl.BlockSpec((Q_BLOCK, D_ATT), lambda i: (i, cb)),
                  pl.BlockSpec((T, D_ATT), lambda i: (0, cb + 1)),
                  pl.BlockSpec((T, D_ATT), lambda i: (0, cb + 2)), blk, blk],
        out_specs=[blk, full, full], out_shape=[sh, sh, sh],
        compiler_params=_params(dimension_semantics=("arbitrary",)),
    )(proj, proj, proj, do, o)


def _tok(c, by=None):
    if by is None:
        return pl.BlockSpec((TM, c), lambda i, j, k: (i, 0))
    if by == 1:
        return pl.BlockSpec((TM, c), lambda i, j, k: (i, j))
    return pl.BlockSpec((TM, c), lambda i, j, k: (i, k))


def _chunked(c, by):
    if by == 1:
        return pl.BlockSpec((None, TM, c), lambda i, j, k: (j, i, 0))
    return pl.BlockSpec((None, TM, c), lambda i, j, k: (k, i, 0))


def _gain_spec():
    return pl.BlockSpec((1, D_MODEL), lambda i, j, k: (0, 0))


def _all_chunks(rows, c):
    return pl.BlockSpec((N_CHIPS, rows, c), lambda i, j, k: (0, i, 0))


def _wfull(r, c, l):
    return pl.BlockSpec((N_CHIPS, None, r, c), lambda i, j, k: (0, l, 0, 0))


def _pick(idx, c):
    return lambda ins: ins[idx][c]


def _cols(idx, c, w):
    return lambda ins: ins[idx][:, c * w:(c + 1) * w]


def _ffn_fwd(x, gain, wg, wu, wd, l):
    T = x.shape[0]
    ffs = wd.shape[2]
    h = _rms_fwd(x, gain)

    def swiglu(vals, ins, outs, i):
        gt, up = vals
        outs[0][...] = gt.astype(BF16)
        outs[1][...] = up.astype(BF16)
        outs[2][...] = (gt * _sigmoid(gt) * up).astype(BF16)

    csh = jax.ShapeDtypeStruct((N_CHIPS, T, ffs), BF16)
    gate, up, act = _mm(
        "ffn_up", [h, wg, wu], [_tok(D_MODEL), _wspec(ffs, D_MODEL, l, 1), _wspec(ffs, D_MODEL, l, 1)],
        [(0, 1, 0), (0, 2, 1)], 2, None, NT, (T // TM, N_CHIPS, 1), swiglu,
        [csh, csh, csh], [_chunked(ffs, 1)] * 3, j_outer=True)

    def resid(vals, ins, outs, i):
        outs[0][...] = ins[2][...] + 0.5 * vals[0]

    (y,) = _mm(
        "ffn_down", [act, wd, x], [_all_chunks(TM, ffs), _wfull(ffs, D_MODEL, l), _tok(D_MODEL)],
        [(_pick(0, c), _pick(1, c), 0) for c in range(N_CHIPS)], 1, None, NN, (T // TM, 1, 1), resid,
        [jax.ShapeDtypeStruct((T, D_MODEL), F32)], [_tok(D_MODEL)])
    return y, (x, h, gate, up, act)


def _ffn_bwd(dxo, gain, wg, wu, wd, l, saved):
    x, h, gate, up, act = saved
    T = x.shape[0]
    ffs = wd.shape[2]
    tk = TM
    tm = TM // 2

    def dswiglu(vals, ins, outs, i):
        for c in range(N_CHIPS):
            da = 0.5 * vals[c]
            gt = ins[2][c].astype(F32)
            u = ins[3][c].astype(F32)
            s = _sigmoid(gt)
            outs[0][c] = (da * u * (s * (1.0 + gt * (1.0 - s)))).astype(BF16)
            outs[1][c] = (da * (gt * s)).astype(BF16)

    csh = jax.ShapeDtypeStruct((N_CHIPS, T, ffs), BF16)
    row = pl.BlockSpec((tm, D_MODEL), lambda i, j, k: (i, 0))
    dgate, dup = _mm(
        "ffn_dact", [dxo, wd, gate, up],
        [row, _wfull(ffs, D_MODEL, l), _all_chunks(tm, ffs), _all_chunks(tm, ffs)],
        [(0, _pick(1, c), c) for c in range(N_CHIPS)], N_CHIPS, None, NT, (T // tm, 1, 1), dswiglu,
        [csh, csh], [_all_chunks(tm, ffs)] * 2)

    def halves(vals, ins, outs, i):
        for c in range(N_CHIPS):
            outs[0][c] = (0.5 * vals[c]).astype(BF16)

    def casts(vals, ins, outs, i):
        for c in range(N_CHIPS):
            outs[0][c] = vals[c].astype(BF16)

    tok_k = pl.BlockSpec((tk, D_MODEL), lambda i, j, k: (k, 0))
    chunks_k = pl.BlockSpec((N_CHIPS, tk, ffs), lambda i, j, k: (0, k, 0))
    (dwd,) = _mm(
        "ffn_dwd", [act, dxo], [chunks_k, tok_k], [(_pick(0, c), 1, c) for c in range(N_CHIPS)], N_CHIPS,
        (ffs, D_MODEL), TN, (1, 1, T // tk), halves, [jax.ShapeDtypeStruct((N_CHIPS, ffs, D_MODEL), BF16)],
        [pl.BlockSpec((N_CHIPS, ffs, D_MODEL), lambda i, j, k: (0, 0, 0))])

    dx, dgain = _mm(
        "ffn_dx", [dgate, dup, wg, wu, x, gain, dxo],
        [_all_chunks(tm, ffs), _all_chunks(tm, ffs), _wfull(ffs, D_MODEL, l), _wfull(ffs, D_MODEL, l),
         row, _gain_spec(), row],
        [(_pick(a, c), _pick(a + 2, c), 0) for c in range(N_CHIPS) for a in range(2)], 1, None, NN,
        (T // tm, 1, 1), _rms_bwd_epilogue(4, 5, 6),
        [jax.ShapeDtypeStruct((T, D_MODEL), F32), jax.ShapeDtypeStruct((8, D_MODEL), F32)],
        [row, pl.BlockSpec((8, D_MODEL), lambda i, j, k: (0, 0))])

    wsh = jax.ShapeDtypeStruct((N_CHIPS, ffs, D_MODEL), BF16)
    wout = pl.BlockSpec((N_CHIPS, ffs, D_MODEL), lambda i, j, k: (0, 0, 0))
    dws = []
    for dact in (dgate, dup):
        dws += _mm("ffn_dwgu", [h, dact], [tok_k, chunks_k], [(_pick(1, c), 0, c) for c in range(N_CHIPS)],
                   N_CHIPS, (ffs, D_MODEL), TN, (1, 1, T // tk), casts, [wsh], [wout])
    return dx, dgain, dws[0], dws[1], dwd


def _mixer_fwd(x, gain, W, l, tabs):
    T = x.shape[0]
    win, wpd, wps, wo = W["w_in"], W["w_proj_dil"], W["w_proj_sb"], W["w_out"]
    cin = win.shape[3]
    cp = wpd.shape[3]
    h = _rms_fwd(x, gain)

    n_rope = 6 * D_ATT

    def roped(vals, ins, outs, i):
        v = vals[0]
        col0 = pl.program_id(0) * cin

        @pl.when(col0 < n_rope)
        def _():
            on = lax.broadcasted_iota(jnp.int32, v.shape, 1) + col0 < n_rope
            c = jnp.where(on, jnp.concatenate([ins[2][...]] * (cin // 128), axis=1), 1.0)
            s1 = jnp.where(on, jnp.concatenate([ins[3][...]] * (cin // 128), axis=1), 0.0)
            s2 = jnp.where(on, jnp.concatenate([ins[4][...]] * (cin // 128), axis=1), 0.0)
            outs[0][...] = _rope_fwd(v, c, s1, s2)

        @pl.when(col0 >= n_rope)
        def _():
            outs[0][...] = v

    (proj,) = _mm(
        "mix_in", [h, win, *tabs], [_tok(D_MODEL), _wspec(D_MODEL, cin, l, 1)] + [_tok(128)] * 3, [(0, 1, 0)], 1,
        None, NN, (T // TM, N_CHIPS, 1), roped, [jax.ShapeDtypeStruct((T, N_CHIPS * cin), F32)], [_tok(cin, 1)],
        j_outer=True)

    os_, lses = [], []
    for g, (window, dil) in enumerate(DIL_GROUPS):
        o_g, lse_g = _dil_fwd(proj, g, dil)
        os_.append(o_g)
        lses.append(lse_g)
    o_dil, lse = _dil_merge(os_, lses)
    o_sb = _sb_fwd(proj)

    def gated(vals, ins, outs, i):
        pd, ps = vals
        outs[0][...] = (_sigmoid(ins[4][...]) * pd + _sigmoid(ins[5][...]) * ps).astype(BF16)
        outs[1][...] = pd.astype(BF16)
        outs[2][...] = ps.astype(BF16)

    gd0, gs0 = COL_GD // cp, COL_GS // cp
    ush = jax.ShapeDtypeStruct((T, D_MODEL), BF16)
    u, pd, ps = _mm(
        "mix_gate", [o_dil, o_sb, wpd, wps, proj, proj],
        [_tok(D_ATT), _tok(D_ATT), _wspec(D_ATT, cp, l, 1), _wspec(D_ATT, cp, l, 1),
         pl.BlockSpec((TM, cp), lambda i, j, k: (i, gd0 + j)), pl.BlockSpec((TM, cp), lambda i, j, k: (i, gs0 + j))],
        [(0, 2, 0), (1, 3, 1)], 2, None, NN, (T // TM, N_CHIPS, 1), gated, [ush] * 3, [_tok(cp, 1)] * 3)

    def resid(vals, ins, outs, i):
        outs[0][...] = ins[2][...] + vals[0]

    (y,) = _mm(
        "mix_out", [u, wo, x], [_tok(D_MODEL), _wfull(cp, D_MODEL, l), _tok(D_MODEL)],
        [(_cols(0, c, cp), _pick(1, c), 0) for c in range(N_CHIPS)], 1, None, NN, (T // TM, 1, 1), resid,
        [jax.ShapeDtypeStruct((T, D_MODEL), F32)], [_tok(D_MODEL)])
    return y, (x, h, proj, o_dil, lse, o_sb, u, pd, ps)


def _mixer_bwd(dxo, gain, W, l, tabs, saved):
    x, h, proj, o_dil, lse, o_sb, u, pd, ps = saved
    T = x.shape[0]
    win, wpd, wps, wo = W["w_in"], W["w_proj_dil"], W["w_proj_sb"], W["w_out"]
    cin = win.shape[3]
    cp = wpd.shape[3]
    tk = TM
    tm = TM // 2
    row = pl.BlockSpec((tm, D_MODEL), lambda i, j, k: (i, 0))
    gd0, gs0 = COL_GD // cp, COL_GS // cp

    def dgated(vals, ins, outs, i):
        du = vals[0]
        sd = _sigmoid(ins[4][...])
        ss = _sigmoid(ins[5][...])
        outs[0][...] = (du * sd).astype(BF16)
        outs[1][...] = (du * ss).astype(BF16)
        outs[2][...] = (du * ins[2][...].astype(F32) * sd * (1.0 - sd)).astype(BF16)
        outs[3][...] = (du * ins[3][...].astype(F32) * ss * (1.0 - ss)).astype(BF16)

    ush = jax.ShapeDtypeStruct((T, D_MODEL), BF16)
    dpd, dps, dgd, dgs = _mm(
        "mix_du", [dxo, wo, pd, ps, proj, proj],
        [_tok(D_MODEL), _wspec(cp, D_MODEL, l, 1), _tok(cp, 1), _tok(cp, 1),
         pl.BlockSpec((TM, cp), lambda i, j, k: (i, gd0 + j)), pl.BlockSpec((TM, cp), lambda i, j, k: (i, gs0 + j))],
        [(0, 1, 0)], 1, None, NT, (T // TM, N_CHIPS, 1), dgated, [ush] * 4, [_tok(cp, 1)] * 4)

    def one(vals, ins, outs, i):
        outs[0][...] = vals[0].astype(BF16)

    def two(vals, ins, outs, i):
        outs[0][...] = vals[0].astype(BF16)
        outs[1][...] = vals[1].astype(BF16)

    (dwo,) = _mm(
        "mix_dwo", [u, dxo],
        [pl.BlockSpec((tk, cp), lambda i, j, k: (k, j)), pl.BlockSpec((tk, D_MODEL), lambda i, j, k: (k, 0))],
        [(0, 1, 0)], 1, (cp, D_MODEL), TN, (1, N_CHIPS, T // tk), one,
        [jax.ShapeDtypeStruct((N_CHIPS, cp, D_MODEL), BF16)],
        [pl.BlockSpec((None, cp, D_MODEL), lambda i, j, k: (j, 0, 0))])

    def plain2(vals, ins, outs, i):
        outs[0][...] = vals[0]
        outs[1][...] = vals[1]

    ash = jax.ShapeDtypeStruct((T, D_ATT), F32)
    do_dil, do_sb = _mm(
        "mix_do", [dpd, dps, wpd, wps], [_tok(cp, 2), _tok(cp, 2), _wspec(D_ATT, cp, l, 2), _wspec(D_ATT, cp, l, 2)],
        [(0, 2, 0), (1, 3, 1)], 2, (TM, D_ATT), NT, (T // TM, 1, N_CHIPS), plain2, [ash, ash], [_tok(D_ATT)] * 2)

    psh = jax.ShapeDtypeStruct((N_CHIPS, D_ATT, cp), BF16)
    pspec = pl.BlockSpec((None, D_ATT, cp), lambda i, j, k: (j, 0, 0))
    arow = pl.BlockSpec((tk, D_ATT), lambda i, j, k: (k, 0))
    dcol = pl.BlockSpec((tk, cp), lambda i, j, k: (k, j))
    dwpd, dwps = _mm(
        "mix_dwp", [o_dil, o_sb, dpd, dps], [arow, arow, dcol, dcol], [(0, 2, 0), (1, 3, 1)], 2, (D_ATT, cp), TN,
        (1, N_CHIPS, T // tk), two, [psh, psh], [pspec, pspec])

    dqs, dks, dvs = [], [], []
    for g, (window, dil) in enumerate(DIL_GROUPS):
        dq, dk, dv = _dil_bwd(proj, do_dil, o_dil, lse, g, dil)
        dqs.append(dq)
        dks.append(dk)
        dvs.append(dv)
    dq_s, dk_s, dv_s = _sb_bwd(proj, do_sb, o_sb)
    dproj = _assemble_dproj(dqs + dks, dvs + [dq_s, dk_s, dv_s], [dgd, dgs], tabs)

    dx, dgain = _mm(
        "mix_dx", [dproj, win, x, gain, dxo],
        [pl.BlockSpec((tm, N_CHIPS * cin), lambda i, j, k: (i, 0)), _wfull(D_MODEL, cin, l), row, _gain_spec(), row],
        [(_cols(0, c, cin), _pick(1, c), 0) for c in range(N_CHIPS)], 1, None, NT, (T // tm, 1, 1),
        _rms_bwd_epilogue(2, 3, 4),
        [jax.ShapeDtypeStruct((T, D_MODEL), F32), jax.ShapeDtypeStruct((8, D_MODEL), F32)],
        [row, pl.BlockSpec((8, D_MODEL), lambda i, j, k: (0, 0))])

    (dwin,) = _mm(
        "mix_dwin", [h, dproj],
        [pl.BlockSpec((tk, D_MODEL), lambda i, j, k: (k, 0)), pl.BlockSpec((tk, cin), lambda i, j, k: (k, j))],
        [(0, 1, 0)], 1, (D_MODEL, cin), TN, (1, N_CHIPS, T // tk), one,
        [jax.ShapeDtypeStruct((N_CHIPS, D_MODEL, cin), BF16)],
        [pl.BlockSpec((None, D_MODEL, cin), lambda i, j, k: (j, 0, 0))])
    return dx, dgain, dwin, dwpd, dwps, dwo


def _local_step(x, target, norms, norm_final, weights_of, on_grads):
    T = x.shape[0]
    tabs = _rope_tables(T)
    saved, held = [], []
    for l in range(DEPTH):
        w1 = weights_of(l, 0, x)
        x, s1 = _ffn_fwd(x, norms["norm_ffn1"][l:l + 1], w1["ffn1_w_gate"], w1["ffn1_w_up"], w1["ffn1_w_down"], 0)
        w2 = weights_of(l, 1, x)
        x, s2 = _mixer_fwd(x, norms["norm_mix"][l:l + 1], w2, 0, tabs)
        w3 = weights_of(l, 2, x)
        x, s3 = _ffn_fwd(x, norms["norm_ffn2"][l:l + 1], w3["ffn2_w_gate"], w3["ffn2_w_up"], w3["ffn2_w_down"], 0)
        saved.append((s1, s2, s3))
        held.append((w1, w2, w3))
    dx, dg_final, loss = _final_loss(x, norm_final.reshape(1, D_MODEL), target)
    gains = [None] * DEPTH
    for l in reversed(range(DEPTH)):
        s1, s2, s3 = saved[l]
        w1, w2, w3 = held[l]
        dx, dg2, dwg2, dwu2, dwd2 = _ffn_bwd(dx, norms["norm_ffn2"][l:l + 1], w3["ffn2_w_gate"], w3["ffn2_w_up"],
                                             w3["ffn2_w_down"], 0, s3)
        dx = on_grads(l, 2, dict(ffn2_w_gate=dwg2, ffn2_w_up=dwu2, ffn2_w_down=dwd2), dx)
        dx, dgm, dwin, dwpd, dwps, dwo = _mixer_bwd(dx, norms["norm_mix"][l:l + 1], w2, 0, tabs, s2)
        dx = on_grads(l, 1, dict(w_in=dwin, w_proj_dil=dwpd, w_proj_sb=dwps, w_out=dwo), dx)
        dx, dg1, dwg1, dwu1, dwd1 = _ffn_bwd(dx, norms["norm_ffn1"][l:l + 1], w1["ffn1_w_gate"], w1["ffn1_w_up"],
                                             w1["ffn1_w_down"], 0, s1)
        dx = on_grads(l, 0, dict(ffn1_w_gate=dwg1, ffn1_w_up=dwu1, ffn1_w_down=dwd1), dx)
        gains[l] = dict(norm_ffn1=dg1, norm_mix=dgm, norm_ffn2=dg2)
    return loss, dx, gains, dg_final


def _place():
    x, y, c = lax.axis_index("x"), lax.axis_index("y"), lax.axis_index("c")
    chips = [(1 - x, y), (x, 1 - y), (1 - x, 1 - y)]
    return x, y, c, chips


def _half(c, r):
    return pl.ds(pl.multiple_of(c * (r // 2), 8), r // 2)


def _cast_into_slot(w, l, me_arr):
    _, r, cw = w.shape
    tr = r // 4 if r > 256 else r

    def body(me_ref, w_ref, o_ref):
        o_ref[...] = w_ref[...].astype(BF16)

    return pl.pallas_call(
        body, name="cast_weights",
        grid_spec=pltpu.PrefetchScalarGridSpec(
            num_scalar_prefetch=1, grid=(r // tr,),
            in_specs=[pl.BlockSpec((None, tr, cw), lambda i, me: (l, i, 0))],
            out_specs=pl.BlockSpec((None, None, tr, cw), lambda i, me: (me[0], 0, i, 0))),
        out_shape=jax.ShapeDtypeStruct((N_CHIPS, 1, r, cw), BF16), compiler_params=_params(),
    )(me_arr, w)


HBM_SPEC = pl.BlockSpec(memory_space=pltpu.HBM)
SEM_SPEC = pl.BlockSpec(memory_space=pltpu.SEMAPHORE)
SPLIT_COPY = pltpu.CompilerParams(has_side_effects=pltpu.SideEffectType.DATAFLOW_SIDE_EFFECTING)


def _gather_piece(ref, chip_id, c):
    return ref.at[chip_id, 0, _half(c, ref.shape[2]), :]


def _gather_start(bufs):
    n = len(bufs)

    def body(*refs):
        out_refs = refs[n:2 * n]
        send_sems, recv_sems = refs[2 * n:]
        x, y, c, chips = _place()
        me = 2 * x + y
        for a in range(n):
            piece = _gather_piece(out_refs[a], me, c)
            for j, chip in enumerate(chips):
                pltpu.make_async_remote_copy(
                    src_ref=piece, dst_ref=piece, send_sem=send_sems.at[3 * a + j], recv_sem=recv_sems.at[3 * a + j],
                    device_id=(*chip, c), device_id_type=MESH).start()

    outs = pl.pallas_call(
        body, name="gather_start", in_specs=[HBM_SPEC] * n, out_specs=[HBM_SPEC] * n + [SEM_SPEC, SEM_SPEC],
        out_shape=[pltpu.HBM(b.shape, b.dtype) for b in bufs] + [pltpu.SemaphoreType.DMA((3 * n,))] * 2,
        input_output_aliases={a: a for a in range(n)}, compiler_params=SPLIT_COPY,
    )(*[pltpu.with_memory_space_constraint(b, pltpu.HBM) for b in bufs])
    return outs[:n], outs[n], outs[n + 1]


def _gather_wait(k, bufs, places, send_sems, recv_sems, after):
    m = len(bufs)

    def body(*refs):
        in_refs = refs[:m]
        ssem, rsem = refs[m], refs[m + 1]
        x, y, c, chips = _place()
        me = 2 * x + y
        for t, a in enumerate(places):
            for j, chip in enumerate(chips):
                cp = pltpu.make_async_remote_copy(
                    src_ref=_gather_piece(in_refs[t], me, c),
                    dst_ref=_gather_piece(in_refs[t], 2 * chip[0] + chip[1], c),
                    send_sem=ssem.at[3 * a + j], recv_sem=rsem.at[3 * a + j], device_id=(*chip, c),
                    device_id_type=MESH)
                cp.wait_send()
                cp.wait_recv()

    return pl.pallas_call(
        body, name=f"gather_wait_{k}",
        in_specs=[HBM_SPEC] * m + [SEM_SPEC, SEM_SPEC, pl.BlockSpec(memory_space=pl.ANY)], out_specs=[HBM_SPEC] * m,
        out_shape=[pltpu.HBM(b.shape, b.dtype) for b in bufs], input_output_aliases={t: t for t in range(m)},
        compiler_params=SPLIT_COPY,
    )(*bufs, send_sems, recv_sems, after)


def _gather_relay(bufs):
    n = len(bufs)

    def body(*refs):
        out_refs = refs[n:2 * n]
        send_sems, recv_sems = refs[2 * n:]
        x, y, c, chips = _place()
        cps = []
        for a in range(n):
            for j, chip in enumerate(chips):
                piece = _gather_piece(out_refs[a], 2 * chip[0] + chip[1], c)
                cps.append(pltpu.make_async_remote_copy(
                    src_ref=piece, dst_ref=piece, send_sem=send_sems.at[a, j], recv_sem=recv_sems.at[a, j],
                    device_id=(x, y, 1 - c), device_id_type=MESH))
        for cp in cps:
            cp.start()
        for a in range(n):
            for j, chip in enumerate(chips):
                theirs = _gather_piece(out_refs[a], 2 * chip[0] + chip[1], 1 - c)
                pltpu.make_async_remote_copy(
                    src_ref=theirs, dst_ref=theirs, send_sem=send_sems.at[a, j], recv_sem=recv_sems.at[a, j],
                    device_id=(x, y, 1 - c), device_id_type=MESH).wait_recv()
        for cp in cps:
            cp.wait_send()

    any_spec = pl.BlockSpec(memory_space=pl.ANY)
    return pl.pallas_call(
        body, name="gather_relay", in_specs=[any_spec] * n, out_specs=[any_spec] * n,
        out_shape=[jax.ShapeDtypeStruct(b.shape, b.dtype) for b in bufs],
        input_output_aliases={a: a for a in range(n)},
        scratch_shapes=[pltpu.SemaphoreType.DMA((n, 3))] * 2,
    )(*bufs)


def _exchange_halves(gs):
    n = len(gs)

    def body(*refs):
        g_refs, out_refs = refs[:n], refs[n:2 * n]
        send_sems, recv_sems = refs[2 * n:]
        x, y, c, _ = _place()
        cps = []
        for a in range(n):
            r = g_refs[a].shape[1]
            cps.append(pltpu.make_async_remote_copy(
                src_ref=g_refs[a].at[:, _half(1 - c, r), :], dst_ref=out_refs[a],
                send_sem=send_sems.at[a], recv_sem=recv_sems.at[a], device_id=(x, y, 1 - c), device_id_type=MESH))
        for cp in cps:
            cp.start()
        for cp in cps:
            cp.wait()

    any_spec = pl.BlockSpec(memory_space=pl.ANY)
    return pl.pallas_call(
        body, name="grad_to_sibling", in_specs=[any_spec] * n, out_specs=[any_spec] * n,
        out_shape=[jax.ShapeDtypeStruct((g.shape[0], g.shape[1] // 2, g.shape[2]), g.dtype) for g in gs],
        scratch_shapes=[pltpu.SemaphoreType.DMA((n,))] * 2,
    )(*gs)


def _add_half(g, got, c_arr):
    _, r, cw = g.shape

    def body(c_ref, a_ref, b_ref, o_ref):
        o_ref[...] = (a_ref[...].astype(F32) + b_ref[...].astype(F32)).astype(BF16)

    return pl.pallas_call(
        body, name="grad_add_half",
        grid_spec=pltpu.PrefetchScalarGridSpec(
            num_scalar_prefetch=1, grid=(N_CHIPS,),
            in_specs=[pl.BlockSpec((None, r // 2, cw), lambda k, cr: (k, cr[0], 0)),
                      pl.BlockSpec((None, r // 2, cw), lambda k, cr: (k, 0, 0))],
            out_specs=pl.BlockSpec((None, r // 2, cw), lambda k, cr: (k, 0, 0))),
        out_shape=jax.ShapeDtypeStruct((N_CHIPS, r // 2, cw), BF16), compiler_params=_params(),
    )(c_arr, g, got)


def _scatter_start(k, ss, thru):
    n = len(ss)

    def body(*refs):
        s_refs, land_refs = refs[2 * n + 1:3 * n + 1], refs[3 * n + 1:4 * n + 1]
        send_sems, recv_sems = refs[4 * n + 2:]
        x, y, c, chips = _place()
        me = 2 * x + y
        for a in range(n):
            for j, chip in enumerate(chips):
                pltpu.make_async_remote_copy(
                    src_ref=s_refs[a].at[2 * chip[0] + chip[1]], dst_ref=land_refs[a].at[me],
                    send_sem=send_sems.at[3 * a + j], recv_sem=recv_sems.at[3 * a + j], device_id=(*chip, c),
                    device_id_type=MESH).start()

    lands = [lax.empty(s.shape, s.dtype) for s in ss]
    hbm = [pltpu.HBM(s.shape, s.dtype) for s in ss]
    outs = pl.pallas_call(
        body, name=f"grad_scatter_start_{k}", in_specs=[HBM_SPEC] * (2 * n + 1),
        out_specs=[HBM_SPEC] * (2 * n + 1) + [SEM_SPEC, SEM_SPEC],
        out_shape=hbm + hbm + [pltpu.HBM(thru.shape, thru.dtype)] + [pltpu.SemaphoreType.DMA((3 * n,))] * 2,
        input_output_aliases={a: a for a in range(2 * n + 1)}, compiler_params=SPLIT_COPY,
    )(*[pltpu.with_memory_space_constraint(v, pltpu.HBM) for v in list(ss) + lands + [thru]])
    return (outs[:n], outs[n:2 * n], outs[2 * n + 1], outs[2 * n + 2]), outs[2 * n]


def _scatter_wait(k, ss, lands, send_sems, recv_sems, after):
    n = len(ss)

    def body(*refs):
        s_refs, land_refs = refs[:n], refs[n:2 * n]
        ssem, rsem = refs[2 * n], refs[2 * n + 1]
        x, y, c, chips = _place()
        me = 2 * x + y
        for a in range(n):
            for j, chip in enumerate(chips):
                cid = 2 * chip[0] + chip[1]
                cp = pltpu.make_async_remote_copy(
                    src_ref=s_refs[a].at[cid], dst_ref=land_refs[a].at[cid], send_sem=ssem.at[3 * a + j],
                    recv_sem=rsem.at[3 * a + j], device_id=(*chip, c), device_id_type=MESH)
                cp.wait_send()
                cp.wait_recv()

    hbm = [pltpu.HBM(s.shape, s.dtype) for s in ss]
    outs = pl.pallas_call(
        body, name=f"grad_scatter_wait_{k}",
        in_specs=[HBM_SPEC] * (2 * n) + [SEM_SPEC, SEM_SPEC, pl.BlockSpec(memory_space=pl.ANY)],
        out_specs=[HBM_SPEC] * (2 * n), out_shape=hbm + hbm,
        input_output_aliases={a: a for a in range(2 * n)}, compiler_params=SPLIT_COPY,
    )(*ss, *lands, send_sems, recv_sems, after)
    return outs[:n], outs[n:]


def _sum_chips(land, s, me_arr):
    _, rh, cw = land.shape

    def body(me_ref, land_ref, s_ref, o_ref):
        for own in range(N_CHIPS):
            @pl.when(me_ref[0] == own)
            def _(own=own):
                acc = None
                for k in range(N_CHIPS):
                    term = (s_ref[...] if k == own else land_ref[k]).astype(F32)
                    acc = term if acc is None else acc + term
                o_ref[...] = acc

    return pl.pallas_call(
        body, name="grad_sum_chips",
        grid_spec=pltpu.PrefetchScalarGridSpec(
            num_scalar_prefetch=1, grid=(1,),
            in_specs=[pl.BlockSpec((N_CHIPS, rh, cw), lambda i, me: (0, 0, 0)),
                      pl.BlockSpec((None, rh, cw), lambda i, me: (me[0], 0, 0))],
            out_specs=pl.BlockSpec((rh, cw), lambda i, me: (0, 0))),
        out_shape=jax.ShapeDtypeStruct((rh, cw), F32), compiler_params=_params(),
    )(me_arr, land, s)


def _swap_halves(fs):
    n = len(fs)

    def body(*refs):
        f_refs, out_refs = refs[:n], refs[n:2 * n]
        send_sems, recv_sems = refs[2 * n:]
        x, y, c, _ = _place()
        cps = [pltpu.make_async_remote_copy(
            src_ref=f_refs[a], dst_ref=out_refs[a], send_sem=send_sems.at[a], recv_sem=recv_sems.at[a],
            device_id=(x, y, 1 - c), device_id_type=MESH) for a in range(n)]
        for cp in cps:
            cp.start()
        for cp in cps:
            cp.wait()

    any_spec = pl.BlockSpec(memory_space=pl.ANY)
    return pl.pallas_call(
        body, name="grad_swap_halves", in_specs=[any_spec] * n, out_specs=[any_spec] * n,
        out_shape=[jax.ShapeDtypeStruct(f.shape, f.dtype) for f in fs],
        scratch_shapes=[pltpu.SemaphoreType.DMA((n,))] * 2,
    )(*fs)


def _allreduce_rows(stats):
    def body(s_ref, o_ref, buf, send_sems, recv_sems):
        x, y, c, _ = _place()
        me = 4 * x + 2 * y + c
        buf[me] = s_ref[...]
        cps = []
        for k in range(1, 8):
            px = jnp.where(k & 4, 1 - x, x)
            py = jnp.where(k & 2, 1 - y, y)
            pc = jnp.where(k & 1, 1 - c, c)
            cps.append(pltpu.make_async_remote_copy(
                src_ref=s_ref, dst_ref=buf.at[me], send_sem=send_sems.at[k - 1], recv_sem=recv_sems.at[k - 1],
                device_id=(px, py, pc), device_id_type=MESH))
        for cp in cps:
            cp.start()
        for cp in cps:
            cp.wait()
        acc = buf[0]
        for d in range(1, 8):
            acc = acc + buf[d]
        o_ref[...] = acc

    vm = pl.BlockSpec(memory_space=pltpu.VMEM)
    return pl.pallas_call(
        body, name="allreduce_rows", in_specs=[vm], out_specs=vm,
        out_shape=jax.ShapeDtypeStruct(stats.shape, F32),
        scratch_shapes=[pltpu.VMEM((8,) + stats.shape, F32), pltpu.SemaphoreType.DMA((7,)),
                        pltpu.SemaphoreType.DMA((7,))],
    )(stats)


def _adamw_math(w, g, m, v):
    m = ADAM_B1 * m + (1.0 - ADAM_B1) * g
    v = ADAM_B2 * v + (1.0 - ADAM_B2) * (g * g)
    m_hat = m / (1.0 - ADAM_B1 ** ADAM_STEP)
    v_hat = v / (1.0 - ADAM_B2 ** ADAM_STEP)
    delta = -ADAM_LR * (m_hat / (jnp.sqrt(v_hat) + ADAM_EPS) + ADAM_WD * w)
    return delta, m, v


def _adamw(w, m, v, mine, theirs, l, c_arr, earlier):
    L, r, cw = w.shape
    tr = r // 4 if r > 256 else r // 2
    nblk = (r // 2) // tr

    def body(c_ref, w_ref, m_ref, v_ref, a_ref, b_ref, *rest):
        go_ref, d_ref, mo_ref, vo_ref = rest[-4:]
        g = jnp.where(pl.program_id(0) == c_ref[0], a_ref[...], b_ref[...])
        delta, mn, vn = _adamw_math(w_ref[...], g, m_ref[...], v_ref[...])
        go_ref[...] = g
        d_ref[...] = delta
        mo_ref[...] = mn
        vo_ref[...] = vn

    blk = pl.BlockSpec((None, tr, cw), lambda hh, i, cr: (l, hh * nblk + i, 0))
    half = pl.BlockSpec((tr, cw), lambda hh, i, cr: (i, 0))
    sh = jax.ShapeDtypeStruct(w.shape, F32)
    held = [] if earlier is None else list(earlier)
    return pl.pallas_call(
        body, name="adamw",
        grid_spec=pltpu.PrefetchScalarGridSpec(
            num_scalar_prefetch=1, grid=(2, nblk),
            in_specs=[blk, blk, blk, half, half] + [pl.BlockSpec(memory_space=pl.ANY)] * len(held),
            out_specs=[blk] * 4),
        out_shape=[sh] * 4, input_output_aliases={6 + t: t for t in range(len(held))},
        compiler_params=_params(),
    )(c_arr, w, m, v, mine, theirs, *held)


def _adamw_rows(w, m, v, g):
    def body(w_ref, m_ref, v_ref, g_ref, d_ref, mo_ref, vo_ref):
        delta, mn, vn = _adamw_math(w_ref[...], g_ref[...], m_ref[...], v_ref[...])
        d_ref[...] = delta
        mo_ref[...] = mn
        vo_ref[...] = vn

    vm = pl.BlockSpec(memory_space=pltpu.VMEM)
    sh = jax.ShapeDtypeStruct(w.shape, F32)
    return pl.pallas_call(body, name="adamw_rows", in_specs=[vm] * 4, out_specs=[vm] * 3, out_shape=[sh] * 3)(w, m, v, g)


SUBLAYERS = (("ffn1_w_gate", "ffn1_w_up", "ffn1_w_down"), ("w_in", "w_proj_dil", "w_proj_sb", "w_out"),
             ("ffn2_w_gate", "ffn2_w_up", "ffn2_w_down"))
TRANSPOSED = ("ffn1_w_gate", "ffn1_w_up", "ffn2_w_gate", "ffn2_w_up")
LAG = 2


def _pick_row(blocks):
    row = lax.broadcasted_iota(jnp.int32, (8, D_MODEL), 0)
    out = jnp.zeros((8, D_MODEL), F32)
    for i, b in enumerate(blocks):
        out = out + jnp.where(row == i, b, 0.0)
    return out


def kernel(x, norm_ffn1, ffn1_w_gate, ffn1_w_up, ffn1_w_down, norm_mix, w_in, w_proj_dil, w_proj_sb, w_out, norm_ffn2, ffn2_w_gate, ffn2_w_up, ffn2_w_down, norm_final, loss_target, m_norm_ffn1, m_ffn1_w_gate, m_ffn1_w_up, m_ffn1_w_down, m_norm_mix, m_w_in, m_w_proj_dil, m_w_proj_sb, m_w_out, m_norm_ffn2, m_ffn2_w_gate, m_ffn2_w_up, m_ffn2_w_down, m_norm_final, v_norm_ffn1, v_ffn1_w_gate, v_ffn1_w_up, v_ffn1_w_down, v_norm_mix, v_w_in, v_w_proj_dil, v_w_proj_sb, v_w_out, v_norm_ffn2, v_ffn2_w_gate, v_ffn2_w_up, v_ffn2_w_down, v_norm_final):
    given = dict(locals())
    for n in TRANSPOSED:
        for k in ("", "m_", "v_"):
            given[k + n] = jnp.swapaxes(given[k + n], 1, 2)
    weights = {n: given[n] for n in WEIGHT_NAMES}
    norms = {n: given[n] for n in NORM_NAMES}

    c_arr = lax.axis_index("c").astype(jnp.int32).reshape(1)
    me_arr = (2 * lax.axis_index("x") + lax.axis_index("y")).astype(jnp.int32).reshape(1)
    order = [(l, s, n) for l in range(DEPTH) for s in range(len(SUBLAYERS)) for n in SUBLAYERS[s]]
    bufs, send_sems, recv_sems = _gather_start([_cast_into_slot(weights[n], l, me_arr) for l, s, n in order])

    def weights_of(l, s, after):
        places = [i for i, (ll, ss, _) in enumerate(order) if (ll, ss) == (l, s)]
        got = _gather_wait(len(SUBLAYERS) * l + s, [bufs[i] for i in places], places, send_sems, recv_sems, after)
        return {order[i][2]: g for i, g in zip(places, _gather_relay(got))}

    out = {}
    in_flight = []

    def finish(l, s, names, sums, lands, ssem, rsem, after):
        sums, lands = _scatter_wait(len(SUBLAYERS) * l + s, sums, lands, ssem, rsem, after)
        mine = [_sum_chips(land, sm, me_arr) for land, sm in zip(lands, sums)]
        theirs = _swap_halves(mine)
        for n, ga, gb in zip(names, mine, theirs):
            out[n] = _adamw(weights[n], given["m_" + n], given["v_" + n], ga, gb, l, c_arr, out.get(n))

    def on_grads(l, s, grads, after):
        names = list(grads)
        gs = [grads[n] for n in names]
        sums = [_add_half(g, got, c_arr) for g, got in zip(gs, _exchange_halves(gs))]
        sent, after = _scatter_start(len(SUBLAYERS) * l + s, sums, after)
        in_flight.append((l, s, names) + sent)
        if len(in_flight) > LAG:
            finish(*in_flight.pop(0), after)
        return after

    loss_blk, grad_x, gains, dg_final = _local_step(x[0], loss_target[0], norms, norm_final, weights_of, on_grads)
    while in_flight:
        finish(*in_flight.pop(0), grad_x)
    out = {k + n: (jnp.swapaxes(v, 1, 2) if n in TRANSPOSED else v)
           for n, res in out.items() for k, v in zip(("grad_", "delta_", "new_m_", "new_v_"), res)}
    out["grad_x"] = grad_x[None]

    rows = [gains[l][n] for n in NORM_NAMES for l in range(DEPTH)] + [dg_final, loss_blk]
    total = _allreduce_rows(_pick_row(rows))
    out["loss"] = total[7, 0]
    wn = jnp.concatenate([given[n] for n in NORM_NAMES] + [norm_final[None], jnp.zeros((1, D_MODEL), F32)])
    mn_ = jnp.concatenate([given["m_" + n] for n in NORM_NAMES] + [m_norm_final[None], jnp.zeros((1, D_MODEL), F32)])
    vn_ = jnp.concatenate([given["v_" + n] for n in NORM_NAMES] + [v_norm_final[None], jnp.ones((1, D_MODEL), F32)])
    d_n, m_n, v_n = _adamw_rows(wn, mn_, vn_, total)
    for i, n in enumerate(NORM_NAMES):
        sl = slice(i * DEPTH, (i + 1) * DEPTH)
        out["grad_" + n], out["delta_" + n], out["new_m_" + n], out["new_v_" + n] = total[sl], d_n[sl], m_n[sl], v_n[sl]
    out["grad_norm_final"], out["delta_norm_final"] = total[6], d_n[6]
    out["new_m_norm_final"], out["new_v_norm_final"] = m_n[6], v_n[6]

    names = ["norm_ffn1", "ffn1_w_gate", "ffn1_w_up", "ffn1_w_down", "norm_mix", "w_in", "w_proj_dil", "w_proj_sb",
             "w_out", "norm_ffn2", "ffn2_w_gate", "ffn2_w_up", "ffn2_w_down", "norm_final"]
    return (out["loss"], out["grad_x"], *[out["grad_" + n] for n in names], *[out["delta_" + n] for n in names],
            *[out["new_m_" + n] for n in names], *[out["new_v_" + n] for n in names])
```

```python
import functools

import jax
import jax.numpy as jnp
from jax import lax
from jax.experimental import pallas as pl
from jax.experimental.pallas import tpu as pltpu

F32 = jnp.float32
BF16 = jnp.bfloat16

D_MODEL = 1024
DEPTH = 2
N_CHIPS = 4
HEAD_DIM = 64
ROPE_DIM = 16
ROPE_THETA = 500000.0
DIL_GROUPS = ((128, 1), (512, 4), (2048, 16))
SPAN = 128
Q_BLOCK = 128
RMS_EPS = 1e-6
D_ATT = 256
COL_QS = 2304
COL_GD = 3072
COL_GS = 4096
ADAM_LR, ADAM_B1, ADAM_B2, ADAM_EPS, ADAM_WD, ADAM_STEP = 0.001, 0.9, 0.999, 1e-08, 0.01, 10

VMEM_LIMIT = 52 * 1024 * 1024
TM = 512
NEG = -1e30

NN = (((1,), (0,)), ((), ()))
NT = (((1,), (1,)), ((), ()))
TN = (((0,), (0,)), ((), ()))
MESH = pl.DeviceIdType.MESH

WEIGHT_NAMES = ("ffn1_w_gate", "ffn1_w_up", "ffn1_w_down", "w_in", "w_proj_dil",
                "w_proj_sb", "w_out", "ffn2_w_gate", "ffn2_w_up", "ffn2_w_down")
NORM_NAMES = ("norm_ffn1", "norm_mix", "norm_ffn2")


def _params(**kw):
    return pltpu.CompilerParams(vmem_limit_bytes=VMEM_LIMIT, **kw)


def _sigmoid(x):
    return 0.5 * jnp.tanh(0.5 * x) + 0.5


def _mm_body(pairs, n_in, n_out, n_acc, dims, nk, i_axis, epilogue, *refs):
    ins = refs[:n_in]
    outs = refs[n_in:n_in + n_out]
    accs = refs[n_in + n_out:]
    i = pl.program_id(i_axis)
    k = pl.program_id(2)

    def operand(a):
        return (a(ins) if callable(a) else ins[a][...]).astype(BF16)

    def dot(ia, ib):
        return lax.dot_general(operand(ia), operand(ib), dims, preferred_element_type=F32)

    if nk == 1:
        parts = [None] * n_acc
        for ia, ib, ic in pairs:
            parts[ic] = dot(ia, ib) if parts[ic] is None else parts[ic] + dot(ia, ib)
        epilogue(parts, ins, outs, i)
        return

    @pl.when(k == 0)
    def _():
        for c in range(n_acc):
            accs[c][...] = jnp.zeros_like(accs[c])

    for ia, ib, ic in pairs:
        accs[ic][...] += dot(ia, ib)

    @pl.when(k == nk - 1)
    def _():
        epilogue([a[...] for a in accs], ins, outs, i)


def _j_outer(spec):
    f = spec.index_map
    return pl.BlockSpec(spec.block_shape, lambda j, i, k: f(i, j, k))


def _mm(name, ins, in_specs, pairs, n_acc, acc_shape, dims, grid, epilogue, out_shapes, out_specs, j_outer=False):
    nk = grid[2]
    if j_outer:
        grid = (grid[1], grid[0], grid[2])
        in_specs = [_j_outer(s) for s in in_specs]
        out_specs = [_j_outer(s) for s in out_specs]
    scratch = [pltpu.VMEM(acc_shape, F32) for _ in range(n_acc)] if nk > 1 else []
    body = functools.partial(_mm_body, tuple(pairs), len(ins), len(out_shapes), n_acc, dims, nk,
                             1 if j_outer else 0, epilogue)
    return pl.pallas_call(
        body, name=name, grid=grid, in_specs=in_specs, out_specs=out_specs, out_shape=out_shapes,
        scratch_shapes=scratch,
        compiler_params=_params(dimension_semantics=("arbitrary", "arbitrary", "arbitrary")),
    )(*ins)


def _wspec(r, c, l, by):
    if by == 1:
        return pl.BlockSpec((None, None, r, c), lambda i, j, k: (j, l, 0, 0))
    return pl.BlockSpec((None, None, r, c), lambda i, j, k: (k, l, 0, 0))


def _rms_bwd_epilogue(x_idx, g_idx, dxo_idx):
    def ep(vals, ins, outs, i):
        dh = vals[0]
        x = ins[x_idx][...]
        g = ins[g_idx][...]
        rstd = lax.rsqrt(jnp.mean(x * x, axis=-1, keepdims=True) + RMS_EPS)
        xhat = x * rstd
        dxhat = dh * g
        dx = rstd * (dxhat - xhat * jnp.mean(dxhat * xhat, axis=-1, keepdims=True))
        outs[0][...] = ins[dxo_idx][...] + dx
        dg = jnp.broadcast_to(jnp.sum(dh * xhat, axis=0, keepdims=True), outs[1].shape)

        @pl.when(i == 0)
        def _():
            outs[1][...] = dg

        @pl.when(i > 0)
        def _():
            outs[1][...] += dg
    return ep


def _rms_fwd(x, gain):
    T = x.shape[0]

    def body(x_ref, g_ref, h_ref):
        xv = x_ref[...]
        h = xv * lax.rsqrt(jnp.mean(xv * xv, axis=-1, keepdims=True) + RMS_EPS)
        h_ref[...] = (h * g_ref[...]).astype(BF16)

    return pl.pallas_call(
        body, name="rms_fwd", grid=(T // TM,),
        in_specs=[pl.BlockSpec((TM, D_MODEL), lambda i: (i, 0)), pl.BlockSpec((1, D_MODEL), lambda i: (0, 0))],
        out_specs=pl.BlockSpec((TM, D_MODEL), lambda i: (i, 0)),
        out_shape=jax.ShapeDtypeStruct((T, D_MODEL), BF16), compiler_params=_params(),
    )(x, gain)


def _rope_tables(T):
    pos = jnp.arange(T, dtype=F32)
    inv_freq = ROPE_THETA ** (-jnp.arange(0, ROPE_DIM, 2, dtype=F32) / ROPE_DIM)
    ang = pos[:, None] * inv_freq[None, :]
    cos, sin = jnp.cos(ang), jnp.sin(ang)
    half = ROPE_DIM // 2
    one = jnp.ones((T, HEAD_DIM - ROPE_DIM), F32)
    zero = jnp.zeros((T, HEAD_DIM - ROPE_DIM), F32)
    zh = jnp.zeros((T, half), F32)
    c = jnp.concatenate([cos, cos, one], axis=1)
    s1 = jnp.concatenate([-sin, zh, zero], axis=1)
    s2 = jnp.concatenate([zh, sin, zero], axis=1)
    return tuple(jnp.concatenate([t, t], axis=1) for t in (c, s1, s2))


def _rope_fwd(xv, c, s1, s2):
    w = xv.shape[1]
    half = ROPE_DIM // 2
    return xv * c + pltpu.roll(xv, w - half, 1) * s1 + pltpu.roll(xv, half, 1) * s2


def _rope_bwd(dy, c, s1, s2):
    w = dy.shape[1]
    half = ROPE_DIM // 2
    return dy * c + pltpu.roll(dy * s1, half, 1) + pltpu.roll(dy * s2, w - half, 1)


def _assemble_dproj(dqk, rest, gates, tabs):
    T = gates[0].shape[0]
    n_qk, n_rest = len(dqk), len(rest)
    width = (n_qk + n_rest) * D_ATT + 2 * D_MODEL

    def body(*refs):
        ins, (c_ref, s1_ref, s2_ref), o_ref = refs[:n_qk + n_rest + 2], refs[-4:-1], refs[-1]
        c = jnp.concatenate([c_ref[...]] * 2, axis=1)
        s1 = jnp.concatenate([s1_ref[...]] * 2, axis=1)
        s2 = jnp.concatenate([s2_ref[...]] * 2, axis=1)
        for b in range(n_qk + n_rest):
            v = ins[b][...]
            if b < n_qk:
                v = _rope_bwd(v, c, s1, s2)
            o_ref[:, b * D_ATT:(b + 1) * D_ATT] = v.astype(BF16)
        off = (n_qk + n_rest) * D_ATT
        o_ref[:, off:off + D_MODEL] = ins[-2][...]
        o_ref[:, off + D_MODEL:] = ins[-1][...]

    att = pl.BlockSpec((TM, D_ATT), lambda i: (i, 0))
    wide = pl.BlockSpec((TM, D_MODEL), lambda i: (i, 0))
    tab = pl.BlockSpec((TM, 128), lambda i: (i, 0))
    return pl.pallas_call(
        body, name="assemble_dproj", grid=(T // TM,),
        in_specs=[att] * (n_qk + n_rest) + [wide, wide, tab, tab, tab],
        out_specs=pl.BlockSpec((TM, width), lambda i: (i, 0)),
        out_shape=jax.ShapeDtypeStruct((T, width), BF16), compiler_params=_params(),
    )(*dqk, *rest, *gates, *tabs)


def _dil_merge(os_, lses):
    T = os_[0].shape[0]

    def body(o0, o1, o2, l0, l1, l2, o_ref, lse_ref):
        a, b, c = l0[...], l1[...], l2[...]
        m = jnp.maximum(jnp.maximum(a, b), c)
        ea, eb, ec = jnp.exp(a - m), jnp.exp(b - m), jnp.exp(c - m)
        den = ea + eb + ec
        o_ref[...] = (ea * o0[...] + eb * o1[...] + ec * o2[...]) / den
        lse_ref[...] = m + jnp.log(den)

    blk = pl.BlockSpec((TM, D_ATT), lambda i: (i, 0))
    sh = jax.ShapeDtypeStruct((T, D_ATT), F32)
    return pl.pallas_call(
        body, name="dil_merge", grid=(T // TM,), in_specs=[blk] * 6, out_specs=[blk, blk],
        out_shape=[sh, sh], compiler_params=_params(),
    )(*os_, *lses)


def _final_loss(x, gain, target):
    T = x.shape[0]

    def body(x_ref, g_ref, t_ref, dx_ref, dg_ref, loss_ref):
        xv = x_ref[...]
        g = g_ref[...]
        rstd = lax.rsqrt(jnp.mean(xv * xv, axis=-1, keepdims=True) + RMS_EPS)
        xhat = xv * rstd
        err = xhat * g - t_ref[...]
        loss = 0.5 * jnp.sum(jnp.mean(err * err, axis=-1, keepdims=True), axis=0, keepdims=True)
        dy = err * (1.0 / D_MODEL)
        dxhat = dy * g
        dx_ref[...] = rstd * (dxhat - xhat * jnp.mean(dxhat * xhat, axis=-1, keepdims=True))
        dg = jnp.broadcast_to(jnp.sum(dy * xhat, axis=0, keepdims=True), dg_ref.shape)
        ls = jnp.broadcast_to(loss, loss_ref.shape)

        @pl.when(pl.program_id(0) == 0)
        def _():
            dg_ref[...] = dg
            loss_ref[...] = ls

        @pl.when(pl.program_id(0) > 0)
        def _():
            dg_ref[...] += dg
            loss_ref[...] += ls

    blk = pl.BlockSpec((TM, D_MODEL), lambda i: (i, 0))
    row = pl.BlockSpec((1, D_MODEL), lambda i: (0, 0))
    acc = pl.BlockSpec((8, D_MODEL), lambda i: (0, 0))
    return pl.pallas_call(
        body, name="final_loss", grid=(T // TM,), in_specs=[blk, row, blk], out_specs=[blk, acc, acc],
        out_shape=[jax.ShapeDtypeStruct((T, D_MODEL), F32), jax.ShapeDtypeStruct((8, D_MODEL), F32),
                   jax.ShapeDtypeStruct((8, D_MODEL), F32)],
        compiler_params=_params(dimension_semantics=("arbitrary",)),
    )(x, gain, target)


def _pair_masks():
    lane = lax.broadcasted_iota(jnp.int32, (SPAN, 128), 1)
    return [lane < HEAD_DIM, lane >= HEAD_DIM]


def _stack_heads(x, masks):
    return jnp.concatenate([jnp.where(m, x, 0.0) for m in masks], axis=0)


def _unstack_heads(y, masks):
    rows = y.shape[0] // len(masks)
    out = jnp.where(masks[0], y[:rows], 0.0)
    for h in range(1, len(masks)):
        out = out + jnp.where(masks[h], y[rows * h:rows * (h + 1)], 0.0)
    return out


def _dil_rows(idx, d):
    u = idx // d
    r = idx - u * d
    own = pl.ds(u * (SPAN * d) + r, SPAN, stride=d) if d > 1 else pl.ds(pl.multiple_of(u * SPAN, SPAN), SPAN)
    up = jnp.maximum(u - 1, 0)
    prev = pl.ds(up * (SPAN * d) + r, SPAN, stride=d) if d > 1 else pl.ds(pl.multiple_of(up * SPAN, SPAN), SPAN)
    return u, own, prev


def _dil_valid(u):
    qi = lax.broadcasted_iota(jnp.int32, (2 * SPAN, 2 * SPAN), 0) & (SPAN - 1)
    kj = lax.broadcasted_iota(jnp.int32, (2 * SPAN, 2 * SPAN), 1)
    in_prev = (kj < SPAN) & (kj >= qi + jnp.where(u > 0, 0, SPAN))
    return in_prev | ((kj >= SPAN) & (kj - SPAN <= qi))


def _dil_keys(ref, own, prev):
    return jnp.concatenate([ref[prev, :], ref[own, :]], axis=0).astype(BF16)


def _dil_fwd(proj, g, d):
    T = proj.shape[0]
    n_iter = T // SPAN

    def body(q_ref, k_ref, v_ref, o_ref, lse_ref):
        masks = _pair_masks()

        def step(idx, carry):
            u, own, prev = _dil_rows(idx, d)
            qs = _stack_heads(q_ref[own, :] * (HEAD_DIM ** -0.5), masks).astype(BF16)
            kk = _dil_keys(k_ref, own, prev)
            vv = _dil_keys(v_ref, own, prev)
            s = jnp.where(_dil_valid(u), lax.dot_general(qs, kk, NT, preferred_element_type=F32), NEG)
            m = jnp.max(s, axis=1, keepdims=True)
            p = jnp.exp(s - m)
            den = jnp.sum(p, axis=1, keepdims=True)
            pv = lax.dot_general(p.astype(BF16), vv, NN, preferred_element_type=F32) / den
            o_ref[own, :] = _unstack_heads(pv, masks)
            lse_ref[own, :] = _unstack_heads(jnp.broadcast_to(m + jnp.log(den), pv.shape), masks)
            return carry

        lax.fori_loop(0, n_iter, step, 0, unroll=2)

    def col(b):
        return pl.BlockSpec((T, 128), lambda p: (0, b + p))

    sh = jax.ShapeDtypeStruct((T, D_ATT), F32)
    out = pl.BlockSpec((T, 128), lambda p: (0, p))
    return pl.pallas_call(
        body, name=f"dil_fwd_d{d}", grid=(2,),
        in_specs=[col(2 * g), col(6 + 2 * g), col(12 + 2 * g)], out_specs=[out, out], out_shape=[sh, sh],
        compiler_params=_params(dimension_semantics=("arbitrary",)),
    )(proj, proj, proj)


def _dil_bwd(proj, do, o_dil, lse, g, d):
    T = proj.shape[0]
    n_iter = T // SPAN

    def body(q_ref, k_ref, v_ref, do_ref, o_ref, lse_ref, dq_ref, dk_ref, dv_ref):
        masks = _pair_masks()
        head_lanes = jnp.concatenate(masks, axis=0)

        def step(idx, carry):
            u, own, prev = _dil_rows(idx, d)
            qs = _stack_heads(q_ref[own, :] * (HEAD_DIM ** -0.5), masks).astype(BF16)
            kk = _dil_keys(k_ref, own, prev)
            vv = _dil_keys(v_ref, own, prev)
            dom = _stack_heads(do_ref[own, :], masks)
            dos = dom.astype(BF16)
            delta = jnp.sum(dom * jnp.concatenate([o_ref[own, :]] * 2, axis=0), axis=1, keepdims=True)
            lrow = jnp.max(jnp.where(head_lanes, jnp.concatenate([lse_ref[own, :]] * 2, axis=0), NEG),
                           axis=1, keepdims=True)
            s = lax.dot_general(qs, kk, NT, preferred_element_type=F32)
            p = jnp.where(_dil_valid(u), jnp.exp(s - lrow), 0.0)
            dp = lax.dot_general(dos, vv, NT, preferred_element_type=F32)
            ds = (p * (dp - delta)).astype(BF16)
            dq = lax.dot_general(ds, kk, NN, preferred_element_type=F32)
            dkk = lax.dot_general(ds, qs, TN, preferred_element_type=F32)
            dvv = lax.dot_general(p.astype(BF16), dos, TN, preferred_element_type=F32)
            dq_ref[own, :] = _unstack_heads(dq, masks) * (HEAD_DIM ** -0.5)
            dk_ref[own, :] = dkk[SPAN:]
            dv_ref[own, :] = dvv[SPAN:]
            dk_ref[prev, :] = dk_ref[prev, :] + dkk[:SPAN]
            dv_ref[prev, :] = dv_ref[prev, :] + dvv[:SPAN]
            return carry

        lax.fori_loop(0, n_iter, step, 0, unroll=2)

    def col(b):
        return pl.BlockSpec((T, 128), lambda p: (0, b + p))

    sh = jax.ShapeDtypeStruct((T, D_ATT), F32)
    return pl.pallas_call(
        body, name=f"dil_bwd_d{d}", grid=(2,),
        in_specs=[col(2 * g), col(6 + 2 * g), col(12 + 2 * g), col(0), col(0), col(0)],
        out_specs=[col(0), col(0), col(0)], out_shape=[sh, sh, sh],
        compiler_params=_params(dimension_semantics=("arbitrary",)),
    )(proj, proj, proj, do, o_dil, lse)


SB_KT = 512


def _sb_tri(strict):
    a = lax.broadcasted_iota(jnp.int32, (Q_BLOCK, Q_BLOCK), 0)
    b = lax.broadcasted_iota(jnp.int32, (Q_BLOCK, Q_BLOCK), 1)
    return jnp.where((a > b) if strict else (a >= b), 1.0, 0.0).astype(BF16)


def _suffix(x, c, tri):
    r = x.shape[0]
    nb = x.shape[1] // Q_BLOCK
    blocks = [x[:, Q_BLOCK * b:Q_BLOCK * (b + 1)] for b in range(nb)]
    hi = [b.astype(BF16) for b in blocks]
    lo = [(b - h.astype(F32)).astype(BF16) for b, h in zip(blocks, hi)]
    y = lax.dot_general(jnp.concatenate(hi + lo, axis=0), tri, NN, preferred_element_type=F32)
    outs = [None] * nb
    run = c
    for b in reversed(range(nb)):
        outs[b] = run + y[r * b:r * (b + 1)] + y[r * (nb + b):r * (nb + b + 1)]
        run = run + jnp.sum(blocks[b], axis=1, keepdims=True)
    return jnp.concatenate(outs, axis=1), run


def _sb_tile(qs, kb, past, c, tri):
    z = lax.dot_general(qs, kb, NT, preferred_element_type=F32)
    lsz = jnp.minimum(z, 0.0) - jnp.log(1.0 + jnp.exp(-jnp.abs(z)))
    lk = lsz - z
    if past is not None:
        lk = jnp.where(past, lk, 0.0)
    after, c_new = _suffix(lk, c, tri)
    w = jnp.exp(lsz + after)
    if past is not None:
        w = jnp.where(past, w, 0.0)
    return z, lsz, w, c_new


SB_HEADS = D_ATT // HEAD_DIM
SB_ROWS = SB_HEADS * Q_BLOCK


def _sb_past(i, t):
    row = lax.broadcasted_iota(jnp.int32, (SB_ROWS, SB_KT), 0) & (Q_BLOCK - 1)
    col = lax.broadcasted_iota(jnp.int32, (SB_ROWS, SB_KT), 1)
    return col + t * SB_KT < row + i * Q_BLOCK


def _sb_head_masks():
    lane = lax.broadcasted_iota(jnp.int32, (Q_BLOCK, D_ATT), 1)
    return [(lane >= HEAD_DIM * h) & (lane < HEAD_DIM * (h + 1)) for h in range(SB_HEADS)]


def _sb_rows(t):
    return pl.ds(pl.multiple_of(t * SB_KT, SB_KT), SB_KT)


def _sb_fwd(proj):
    T = proj.shape[0]

    def body(q_ref, k_ref, v_ref, o_ref):
        i = pl.program_id(0)
        masks = _sb_head_masks()
        tri = _sb_tri(True)
        qs = _stack_heads(q_ref[...] * (HEAD_DIM ** -0.5), masks).astype(BF16)
        n_tiles = (i * Q_BLOCK) // SB_KT + 1

        def tile(t, carry, masked):
            kb = k_ref[_sb_rows(t), :].astype(BF16)
            vb = v_ref[_sb_rows(t), :].astype(BF16)
            acc, c = carry
            _, _, w, c = _sb_tile(qs, kb, _sb_past(i, t) if masked else None, c, tri)
            pv = lax.dot_general(w.astype(BF16), vb, NN, preferred_element_type=F32)
            return acc + _unstack_heads(pv, masks), c

        carry = tile(n_tiles - 1, (jnp.zeros((Q_BLOCK, D_ATT), F32), jnp.zeros((SB_ROWS, 1), F32)), True)
        carry = lax.fori_loop(0, n_tiles - 1, lambda tt, cr: tile(n_tiles - 2 - tt, cr, False), carry)
        o_ref[...] = carry[0]

    cb = COL_QS // D_ATT
    return pl.pallas_call(
        body, name="sb_fwd", grid=(T // Q_BLOCK,),
        in_specs=[pl.BlockSpec((Q_BLOCK, D_ATT), lambda i: (i, cb)),
                  pl.BlockSpec((T, D_ATT), lambda i: (0, cb + 1)),
                  pl.BlockSpec((T, D_ATT), lambda i: (0, cb + 2))],
        out_specs=pl.BlockSpec((Q_BLOCK, D_ATT), lambda i: (i, 0)),
        out_shape=jax.ShapeDtypeStruct((T, D_ATT), F32),
        compiler_params=_params(dimension_semantics=("arbitrary",)),
    )(proj, proj, proj)


def _sb_bwd(proj, do, o):
    T = proj.shape[0]

    def body(q_ref, k_ref, v_ref, do_ref, o_ref, dq_ref, dk_ref, dv_ref):
        i = pl.program_id(0)
        masks = _sb_head_masks()
        tri = _sb_tri(True)
        tri_incl = _sb_tri(False)

        @pl.when(i == 0)
        def _():
            dk_ref[...] = jnp.zeros_like(dk_ref)
            dv_ref[...] = jnp.zeros_like(dv_ref)

        qs = _stack_heads(q_ref[...] * (HEAD_DIM ** -0.5), masks).astype(BF16)
        dos = _stack_heads(do_ref[...], masks).astype(BF16)
        delta = jnp.sum(dos.astype(F32) * jnp.concatenate([o_ref[...]] * SB_HEADS, axis=0), axis=1, keepdims=True)
        n_tiles = (i * Q_BLOCK) // SB_KT + 1

        def tile(t, carry, masked):
            rows = _sb_rows(t)
            kb = k_ref[rows, :].astype(BF16)
            vb = v_ref[rows, :].astype(BF16)
            past = _sb_past(i, t) if masked else None
            dq, c, ce = carry
            z, lsz, w, c = _sb_tile(qs, kb, past, c, tri)
            gv = lax.dot_general(dos, vb, NT, preferred_element_type=F32)
            wb = w.astype(BF16)
            e = wb.astype(F32) * gv
            suf, ce = _suffix(e, ce, tri_incl)
            dz = e * jnp.exp(lsz - z) - (delta - suf) * jnp.exp(lsz)
            if masked:
                dz = jnp.where(past, dz, 0.0)
            dzb = dz.astype(BF16)
            dq = dq + _unstack_heads(lax.dot_general(dzb, kb, NN, preferred_element_type=F32), masks)
            dk_ref[rows, :] = dk_ref[rows, :] + lax.dot_general(dzb, qs, TN, preferred_element_type=F32)
            dv_ref[rows, :] = dv_ref[rows, :] + lax.dot_general(wb, dos, TN, preferred_element_type=F32)
            return dq, c, ce

        zcol = jnp.zeros((SB_ROWS, 1), F32)
        carry = tile(n_tiles - 1, (jnp.zeros((Q_BLOCK, D_ATT), F32), zcol, zcol), True)
        carry = lax.fori_loop(0, n_tiles - 1, lambda tt, cr: tile(n_tiles - 2 - tt, cr, False), carry)
        dq_ref[...] = carry[0] * (HEAD_DIM ** -0.5)

    cb = COL_QS // D_ATT
    blk = pl.BlockSpec((Q_BLOCK, D_ATT), lambda i: (i, 0))
    full = pl.BlockSpec((T, D_ATT), lambda i: (0, 0))
    sh = jax.ShapeDtypeStruct((T, D_ATT), F32)
    return pl.pallas_call(
        body, name="sb_bwd", grid=(T // Q_BLOCK,),
        in_specs=[pl.BlockSpec((Q_BLOCK, D_ATT), lambda i: (i, cb)),
                  pl.BlockSpec((T, D_ATT), lambda i: (0, cb + 1)),
                  pl.BlockSpec((T, D_ATT), lambda i: (0, cb + 2)), blk, blk],
        out_specs=[blk, full, full], out_shape=[sh, sh, sh],
        compiler_params=_params(dimension_semantics=("arbitrary",)),
    )(proj, proj, proj, do, o)


def _tok(c, by=None):
    if by is None:
        return pl.BlockSpec((TM, c), lambda i, j, k: (i, 0))
    if by == 1:
        return pl.BlockSpec((TM, c), lambda i, j, k: (i, j))
    return pl.BlockSpec((TM, c), lambda i, j, k: (i, k))


def _chunked(c, by):
    if by == 1:
        return pl.BlockSpec((None, TM, c), lambda i, j, k: (j, i, 0))
    return pl.BlockSpec((None, TM, c), lambda i, j, k: (k, i, 0))


def _gain_spec():
    return pl.BlockSpec((1, D_MODEL), lambda i, j, k: (0, 0))


def _all_chunks(rows, c):
    return pl.BlockSpec((N_CHIPS, rows, c), lambda i, j, k: (0, i, 0))


def _wfull(r, c, l):
    return pl.BlockSpec((N_CHIPS, None, r, c), lambda i, j, k: (0, l, 0, 0))


def _pick(idx, c):
    return lambda ins: ins[idx][c]


def _cols(idx, c, w):
    return lambda ins: ins[idx][:, c * w:(c + 1) * w]


def _rows(rows, width):
    return pl.BlockSpec((rows, width), lambda i, j, k: (i, 0))


def _whole(shape):
    return pl.BlockSpec(shape, lambda i, j, k: (0, 0))


def _ffn_fwd(x, gain, wg, wu, wd):
    T = x.shape[0]
    wg, wu, wd = (w.reshape(-1, D_MODEL) for w in (wg, wu, wd))
    ff = wd.shape[0]
    tm = TM // 2
    h = _rms_fwd(x, gain)

    def swiglu(vals, ins, outs, i):
        gt, up = vals
        outs[0][...] = gt.astype(BF16)
        outs[1][...] = up.astype(BF16)
        outs[2][...] = (gt * _sigmoid(gt) * up).astype(BF16)

    ash = jax.ShapeDtypeStruct((T, ff), BF16)
    gate, up, act = _mm(
        "ffn_up", [h, wg, wu], [_rows(tm, D_MODEL), _whole(wg.shape), _whole(wu.shape)],
        [(0, 1, 0), (0, 2, 1)], 2, None, NT, (T // tm, 1, 1), swiglu, [ash] * 3, [_rows(tm, ff)] * 3)

    def resid(vals, ins, outs, i):
        outs[0][...] = ins[2][...] + 0.5 * vals[0]

    (y,) = _mm(
        "ffn_down", [act, wd, x], [_rows(TM, ff), _whole(wd.shape), _tok(D_MODEL)], [(0, 1, 0)], 1, None, NN,
        (T // TM, 1, 1), resid, [jax.ShapeDtypeStruct((T, D_MODEL), F32)], [_tok(D_MODEL)])
    return y, (x, h, gate, up, act)


def _ffn_bwd(dxo, gain, wg, wu, wd, saved):
    x, h, gate, up, act = saved
    T = x.shape[0]
    n_chips, _, ffs, _ = wd.shape
    wg, wu, wd = (w.reshape(-1, D_MODEL) for w in (wg, wu, wd))
    ff = wd.shape[0]
    tk = TM
    tm = TM // 2

    def dswiglu(vals, ins, outs, i):
        da = 0.5 * vals[0]
        gt = ins[2][...].astype(F32)
        u = ins[3][...].astype(F32)
        s = _sigmoid(gt)
        outs[0][...] = (da * u * (s * (1.0 + gt * (1.0 - s)))).astype(BF16)
        outs[1][...] = (da * (gt * s)).astype(BF16)

    ash = jax.ShapeDtypeStruct((T, ff), BF16)
    dgate, dup = _mm(
        "ffn_dact", [dxo, wd, gate, up], [_rows(tm, D_MODEL), _whole(wd.shape), _rows(tm, ff), _rows(tm, ff)],
        [(0, 1, 0)], 1, None, NT, (T // tm, 1, 1), dswiglu, [ash, ash], [_rows(tm, ff)] * 2)

    def half(vals, ins, outs, i):
        outs[0][...] = (0.5 * vals[0]).astype(BF16)

    def cast(vals, ins, outs, i):
        outs[0][...] = vals[0].astype(BF16)

    tok_k = pl.BlockSpec((tk, D_MODEL), lambda i, j, k: (k, 0))
    hid_k = pl.BlockSpec((tk, ff), lambda i, j, k: (k, 0))
    wsh = jax.ShapeDtypeStruct((ff, D_MODEL), BF16)
    (dwd,) = _mm("ffn_dwd", [act, dxo], [hid_k, tok_k], [(0, 1, 0)], 1, (ff, D_MODEL), TN, (1, 1, T // tk), half,
                 [wsh], [_whole((ff, D_MODEL))])

    dx, dgain = _mm(
        "ffn_dx", [dgate, dup, wg, wu, x, gain, dxo],
        [_rows(tm, ff), _rows(tm, ff), _whole(wg.shape), _whole(wu.shape), _rows(tm, D_MODEL), _gain_spec(),
         _rows(tm, D_MODEL)],
        [(0, 2, 0), (1, 3, 0)], 1, None, NN, (T // tm, 1, 1), _rms_bwd_epilogue(4, 5, 6),
        [jax.ShapeDtypeStruct((T, D_MODEL), F32), jax.ShapeDtypeStruct((8, D_MODEL), F32)],
        [_rows(tm, D_MODEL), pl.BlockSpec((8, D_MODEL), lambda i, j, k: (0, 0))])

    dws = []
    for dact in (dgate, dup):
        dws += _mm("ffn_dwgu", [dact, h], [hid_k, tok_k], [(0, 1, 0)], 1, (ff, D_MODEL), TN, (1, 1, T // tk), cast,
                   [wsh], [_whole((ff, D_MODEL))])
    dwg, dwu, dwd = (w.reshape(n_chips, ffs, D_MODEL) for w in (dws[0], dws[1], dwd))
    return dx, dgain, dwg, dwu, dwd


def _mixer_fwd(x, gain, W, l, tabs):
    T = x.shape[0]
    win, wpd, wps, wo = W["w_in"], W["w_proj_dil"], W["w_proj_sb"], W["w_out"]
    cin = win.shape[3]
    cp = wpd.shape[3]
    h = _rms_fwd(x, gain)

    n_rope = 6 * D_ATT

    def roped(vals, ins, outs, i):
        v = vals[0]
        col0 = pl.program_id(0) * cin

        @pl.when(col0 < n_rope)
        def _():
            on = lax.broadcasted_iota(jnp.int32, v.shape, 1) + col0 < n_rope
            c = jnp.where(on, jnp.concatenate([ins[2][...]] * (cin // 128), axis=1), 1.0)
            s1 = jnp.where(on, jnp.concatenate([ins[3][...]] * (cin // 128), axis=1), 0.0)
            s2 = jnp.where(on, jnp.concatenate([ins[4][...]] * (cin // 128), axis=1), 0.0)
            outs[0][...] = _rope_fwd(v, c, s1, s2)

        @pl.when(col0 >= n_rope)
        def _():
            outs[0][...] = v

    (proj,) = _mm(
        "mix_in", [h, win, *tabs], [_tok(D_MODEL), _wspec(D_MODEL, cin, l, 1)] + [_tok(128)] * 3, [(0, 1, 0)], 1,
        None, NN, (T // TM, N_CHIPS, 1), roped, [jax.ShapeDtypeStruct((T, N_CHIPS * cin), F32)], [_tok(cin, 1)],
        j_outer=True)

    os_, lses = [], []
    for g, (window, dil) in enumerate(DIL_GROUPS):
        o_g, lse_g = _dil_fwd(proj, g, dil)
        os_.append(o_g)
        lses.append(lse_g)
    o_dil, lse = _dil_merge(os_, lses)
    o_sb = _sb_fwd(proj)

    def gated(vals, ins, outs, i):
        pd, ps = vals
        outs[0][...] = (_sigmoid(ins[4][...]) * pd + _sigmoid(ins[5][...]) * ps).astype(BF16)
        outs[1][...] = pd.astype(BF16)
        outs[2][...] = ps.astype(BF16)

    gd0, gs0 = COL_GD // cp, COL_GS // cp
    ush = jax.ShapeDtypeStruct((T, D_MODEL), BF16)
    u, pd, ps = _mm(
        "mix_gate", [o_dil, o_sb, wpd, wps, proj, proj],
        [_tok(D_ATT), _tok(D_ATT), _wspec(D_ATT, cp, l, 1), _wspec(D_ATT, cp, l, 1),
         pl.BlockSpec((TM, cp), lambda i, j, k: (i, gd0 + j)), pl.BlockSpec((TM, cp), lambda i, j, k: (i, gs0 + j))],
        [(0, 2, 0), (1, 3, 1)], 2, None, NN, (T // TM, N_CHIPS, 1), gated, [ush] * 3, [_tok(cp, 1)] * 3)

    def resid(vals, ins, outs, i):
        outs[0][...] = ins[2][...] + vals[0]

    (y,) = _mm(
        "mix_out", [u, wo, x], [_tok(D_MODEL), _wfull(cp, D_MODEL, l), _tok(D_MODEL)],
        [(_cols(0, c, cp), _pick(1, c), 0) for c in range(N_CHIPS)], 1, None, NN, (T // TM, 1, 1), resid,
        [jax.ShapeDtypeStruct((T, D_MODEL), F32)], [_tok(D_MODEL)])
    return y, (x, h, proj, o_dil, lse, o_sb, u, pd, ps)


def _mixer_bwd(dxo, gain, W, l, tabs, saved):
    x, h, proj, o_dil, lse, o_sb, u, pd, ps = saved
    T = x.shape[0]
    win, wpd, wps, wo = W["w_in"], W["w_proj_dil"], W["w_proj_sb"], W["w_out"]
    cin = win.shape[3]
    cp = wpd.shape[3]
    tk = TM
    tm = TM // 2
    row = pl.BlockSpec((tm, D_MODEL), lambda i, j, k: (i, 0))
    gd0, gs0 = COL_GD // cp, COL_GS // cp

    def dgated(vals, ins, outs, i):
        du = vals[0]
        sd = _sigmoid(ins[4][...])
        ss = _sigmoid(ins[5][...])
        outs[0][...] = (du * sd).astype(BF16)
        outs[1][...] = (du * ss).astype(BF16)
        outs[2][...] = (du * ins[2][...].astype(F32) * sd * (1.0 - sd)).astype(BF16)
        outs[3][...] = (du * ins[3][...].astype(F32) * ss * (1.0 - ss)).astype(BF16)

    ush = jax.ShapeDtypeStruct((T, D_MODEL), BF16)
    dpd, dps, dgd, dgs = _mm(
        "mix_du", [dxo, wo, pd, ps, proj, proj],
        [_tok(D_MODEL), _wspec(cp, D_MODEL, l, 1), _tok(cp, 1), _tok(cp, 1),
         pl.BlockSpec((TM, cp), lambda i, j, k: (i, gd0 + j)), pl.BlockSpec((TM, cp), lambda i, j, k: (i, gs0 + j))],
        [(0, 1, 0)], 1, None, NT, (T // TM, N_CHIPS, 1), dgated, [ush] * 4, [_tok(cp, 1)] * 4)

    def one(vals, ins, outs, i):
        outs[0][...] = vals[0].astype(BF16)

    def two(vals, ins, outs, i):
        outs[0][...] = vals[0].astype(BF16)
        outs[1][...] = vals[1].astype(BF16)

    (dwo,) = _mm(
        "mix_dwo", [u, dxo],
        [pl.BlockSpec((tk, cp), lambda i, j, k: (k, j)), pl.BlockSpec((tk, D_MODEL), lambda i, j, k: (k, 0))],
        [(0, 1, 0)], 1, (cp, D_MODEL), TN, (1, N_CHIPS, T // tk), one,
        [jax.ShapeDtypeStruct((N_CHIPS, cp, D_MODEL), BF16)],
        [pl.BlockSpec((None, cp, D_MODEL), lambda i, j, k: (j, 0, 0))])

    def plain2(vals, ins, outs, i):
        outs[0][...] = vals[0]
        outs[1][...] = vals[1]

    ash = jax.ShapeDtypeStruct((T, D_ATT), F32)
    do_dil, do_sb = _mm(
        "mix_do", [dpd, dps, wpd, wps], [_tok(cp, 2), _tok(cp, 2), _wspec(D_ATT, cp, l, 2), _wspec(D_ATT, cp, l, 2)],
        [(0, 2, 0), (1, 3, 1)], 2, (TM, D_ATT), NT, (T // TM, 1, N_CHIPS), plain2, [ash, ash], [_tok(D_ATT)] * 2)

    psh = jax.ShapeDtypeStruct((N_CHIPS, D_ATT, cp), BF16)
    pspec = pl.BlockSpec((None, D_ATT, cp), lambda i, j, k: (j, 0, 0))
    arow = pl.BlockSpec((tk, D_ATT), lambda i, j, k: (k, 0))
    dcol = pl.BlockSpec((tk, cp), lambda i, j, k: (k, j))
    dwpd, dwps = _mm(
        "mix_dwp", [o_dil, o_sb, dpd, dps], [arow, arow, dcol, dcol], [(0, 2, 0), (1, 3, 1)], 2, (D_ATT, cp), TN,
        (1, N_CHIPS, T // tk), two, [psh, psh], [pspec, pspec])

    dqs, dks, dvs = [], [], []
    for g, (window, dil) in enumerate(DIL_GROUPS):
        dq, dk, dv = _dil_bwd(proj, do_dil, o_dil, lse, g, dil)
        dqs.append(dq)
        dks.append(dk)
        dvs.append(dv)
    dq_s, dk_s, dv_s = _sb_bwd(proj, do_sb, o_sb)
    dproj = _assemble_dproj(dqs + dks, dvs + [dq_s, dk_s, dv_s], [dgd, dgs], tabs)

    dx, dgain = _mm(
        "mix_dx", [dproj, win, x, gain, dxo],
        [pl.BlockSpec((tm, N_CHIPS * cin), lambda i, j, k: (i, 0)), _wfull(D_MODEL, cin, l), row, _gain_spec(), row],
        [(_cols(0, c, cin), _pick(1, c), 0) for c in range(N_CHIPS)], 1, None, NT, (T // tm, 1, 1),
        _rms_bwd_epilogue(2, 3, 4),
        [jax.ShapeDtypeStruct((T, D_MODEL), F32), jax.ShapeDtypeStruct((8, D_MODEL), F32)],
        [row, pl.BlockSpec((8, D_MODEL), lambda i, j, k: (0, 0))])

    (dwin,) = _mm(
        "mix_dwin", [h, dproj],
        [pl.BlockSpec((tk, D_MODEL), lambda i, j, k: (k, 0)), pl.BlockSpec((tk, cin), lambda i, j, k: (k, j))],
        [(0, 1, 0)], 1, (D_MODEL, cin), TN, (1, N_CHIPS, T // tk), one,
        [jax.ShapeDtypeStruct((N_CHIPS, D_MODEL, cin), BF16)],
        [pl.BlockSpec((None, D_MODEL, cin), lambda i, j, k: (j, 0, 0))])
    return dx, dgain, dwin, dwpd, dwps, dwo


def _local_step(x, target, norms, norm_final, weights_of, on_grads):
    T = x.shape[0]
    tabs = _rope_tables(T)
    saved, held = [], []
    for l in range(DEPTH):
        w1 = weights_of(l, 0, x)
        x, s1 = _ffn_fwd(x, norms["norm_ffn1"][l:l + 1], w1["ffn1_w_gate"], w1["ffn1_w_up"], w1["ffn1_w_down"])
        w2 = weights_of(l, 1, x)
        x, s2 = _mixer_fwd(x, norms["norm_mix"][l:l + 1], w2, 0, tabs)
        w3 = weights_of(l, 2, x)
        x, s3 = _ffn_fwd(x, norms["norm_ffn2"][l:l + 1], w3["ffn2_w_gate"], w3["ffn2_w_up"], w3["ffn2_w_down"])
        saved.append((s1, s2, s3))
        held.append((w1, w2, w3))
    dx, dg_final, loss = _final_loss(x, norm_final.reshape(1, D_MODEL), target)
    gains = [None] * DEPTH
    for l in reversed(range(DEPTH)):
        s1, s2, s3 = saved[l]
        w1, w2, w3 = held[l]
        dx, dg2, dwg2, dwu2, dwd2 = _ffn_bwd(dx, norms["norm_ffn2"][l:l + 1], w3["ffn2_w_gate"], w3["ffn2_w_up"],
                                             w3["ffn2_w_down"], s3)
        dx = on_grads(l, 2, dict(ffn2_w_gate=dwg2, ffn2_w_up=dwu2, ffn2_w_down=dwd2), dx)
        dx, dgm, dwin, dwpd, dwps, dwo = _mixer_bwd(dx, norms["norm_mix"][l:l + 1], w2, 0, tabs, s2)
        dx = on_grads(l, 1, dict(w_in=dwin, w_proj_dil=dwpd, w_proj_sb=dwps, w_out=dwo), dx)
        dx, dg1, dwg1, dwu1, dwd1 = _ffn_bwd(dx, norms["norm_ffn1"][l:l + 1], w1["ffn1_w_gate"], w1["ffn1_w_up"],
                                             w1["ffn1_w_down"], s1)
        dx = on_grads(l, 0, dict(ffn1_w_gate=dwg1, ffn1_w_up=dwu1, ffn1_w_down=dwd1), dx)
        gains[l] = dict(norm_ffn1=dg1, norm_mix=dgm, norm_ffn2=dg2)
    return loss, dx, gains, dg_final


def _place():
    x, y, c = lax.axis_index("x"), lax.axis_index("y"), lax.axis_index("c")
    chips = [(1 - x, y), (x, 1 - y), (1 - x, 1 - y)]
    return x, y, c, chips


def _half(c, r):
    return pl.ds(pl.multiple_of(c * (r // 2), 8), r // 2)


def _cast_into_slot(w, l, me_arr, after):
    _, r, cw = w.shape
    tr = r // 4 if r > 256 else r
    late = [] if after is None else [after]

    def body(me_ref, w_ref, *rest):
        rest[-1][...] = w_ref[...].astype(BF16)

    return pl.pallas_call(
        body, name="cast_weights",
        grid_spec=pltpu.PrefetchScalarGridSpec(
            num_scalar_prefetch=1, grid=(r // tr,),
            in_specs=[pl.BlockSpec((None, tr, cw), lambda i, me: (l, i, 0))]
            + [pl.BlockSpec(memory_space=pl.ANY)] * len(late),
            out_specs=pl.BlockSpec((None, None, tr, cw), lambda i, me: (me[0], 0, i, 0))),
        out_shape=jax.ShapeDtypeStruct((N_CHIPS, 1, r, cw), BF16), compiler_params=_params(),
    )(me_arr, w, *late)


HBM_SPEC = pl.BlockSpec(memory_space=pltpu.HBM)
SEM_SPEC = pl.BlockSpec(memory_space=pltpu.SEMAPHORE)
SPLIT_COPY = pltpu.CompilerParams(has_side_effects=pltpu.SideEffectType.DATAFLOW_SIDE_EFFECTING)


def _gather_piece(ref, chip_id, c):
    return ref.at[chip_id, 0, _half(c, ref.shape[2]), :]


def _gather_start(tag, bufs):
    n = len(bufs)

    def body(*refs):
        out_refs = refs[n:2 * n]
        send_sems, recv_sems, token = refs[2 * n:]
        x, y, c, chips = _place()
        me = 2 * x + y
        for a in range(n):
            piece = _gather_piece(out_refs[a], me, c)
            for j, chip in enumerate(chips):
                pltpu.make_async_remote_copy(
                    src_ref=piece, dst_ref=piece, send_sem=send_sems.at[3 * a + j], recv_sem=recv_sems.at[3 * a + j],
                    device_id=(*chip, c), device_id_type=MESH).start()
        token[...] = jnp.zeros_like(token)

    outs = pl.pallas_call(
        body, name=f"gather_start_{tag}", in_specs=[HBM_SPEC] * n,
        out_specs=[HBM_SPEC] * n + [SEM_SPEC, SEM_SPEC, pl.BlockSpec(memory_space=pltpu.VMEM)],
        out_shape=[pltpu.HBM(b.shape, b.dtype) for b in bufs] + [pltpu.SemaphoreType.DMA((3 * n,))] * 2
        + [jax.ShapeDtypeStruct((8, 128), F32)],
        input_output_aliases={a: a for a in range(n)}, compiler_params=SPLIT_COPY,
    )(*[pltpu.with_memory_space_constraint(b, pltpu.HBM) for b in bufs])
    return outs[:n], outs[n], outs[n + 1], outs[n + 2]


def _gather_wait(k, bufs, places, send_sems, recv_sems, after):
    m = len(bufs)

    def body(*refs):
        in_refs = refs[:m]
        ssem, rsem = refs[m], refs[m + 1]
        x, y, c, chips = _place()
        me = 2 * x + y
        for t, a in enumerate(places):
            for j, chip in enumerate(chips):
                cp = pltpu.make_async_remote_copy(
                    src_ref=_gather_piece(in_refs[t], me, c),
                    dst_ref=_gather_piece(in_refs[t], 2 * chip[0] + chip[1], c),
                    send_sem=ssem.at[3 * a + j], recv_sem=rsem.at[3 * a + j], device_id=(*chip, c),
                    device_id_type=MESH)
                cp.wait_send()
                cp.wait_recv()

    return pl.pallas_call(
        body, name=f"gather_wait_{k}",
        in_specs=[HBM_SPEC] * m + [SEM_SPEC, SEM_SPEC, pl.BlockSpec(memory_space=pl.ANY)], out_specs=[HBM_SPEC] * m,
        out_shape=[pltpu.HBM(b.shape, b.dtype) for b in bufs], input_output_aliases={t: t for t in range(m)},
        compiler_params=SPLIT_COPY,
    )(*bufs, send_sems, recv_sems, after)


def _gather_relay(bufs):
    n = len(bufs)

    def body(*refs):
        out_refs = refs[n:2 * n]
        send_sems, recv_sems = refs[2 * n:]
        x, y, c, chips = _place()
        cps = []
        for a in range(n):
            for j, chip in enumerate(chips):
                piece = _gather_piece(out_refs[a], 2 * chip[0] + chip[1], c)
                cps.append(pltpu.make_async_remote_copy(
                    src_ref=piece, dst_ref=piece, send_sem=send_sems.at[a, j], recv_sem=recv_sems.at[a, j],
                    device_id=(x, y, 1 - c), device_id_type=MESH))
        for cp in cps:
            cp.start()
        for a in range(n):
            for j, chip in enumerate(chips):
                theirs = _gather_piece(out_refs[a], 2 * chip[0] + chip[1], 1 - c)
                pltpu.make_async_remote_copy(
                    src_ref=theirs, dst_ref=theirs, send_sem=send_sems.at[a, j], recv_sem=recv_sems.at[a, j],
                    device_id=(x, y, 1 - c), device_id_type=MESH).wait_recv()
        for cp in cps:
            cp.wait_send()

    any_spec = pl.BlockSpec(memory_space=pl.ANY)
    return pl.pallas_call(
        body, name="gather_relay", in_specs=[any_spec] * n, out_specs=[any_spec] * n,
        out_shape=[jax.ShapeDtypeStruct(b.shape, b.dtype) for b in bufs],
        input_output_aliases={a: a for a in range(n)},
        scratch_shapes=[pltpu.SemaphoreType.DMA((n, 3))] * 2,
    )(*bufs)


def _exchange_halves(gs):
    n = len(gs)

    def body(*refs):
        g_refs, out_refs = refs[:n], refs[n:2 * n]
        send_sems, recv_sems = refs[2 * n:]
        x, y, c, _ = _place()
        cps = []
        for a in range(n):
            r = g_refs[a].shape[1]
            cps.append(pltpu.make_async_remote_copy(
                src_ref=g_refs[a].at[:, _half(1 - c, r), :], dst_ref=out_refs[a],
                send_sem=send_sems.at[a], recv_sem=recv_sems.at[a], device_id=(x, y, 1 - c), device_id_type=MESH))
        for cp in cps:
            cp.start()
        for cp in cps:
            cp.wait()

    any_spec = pl.BlockSpec(memory_space=pl.ANY)
    return pl.pallas_call(
        body, name="grad_to_sibling", in_specs=[any_spec] * n, out_specs=[any_spec] * n,
        out_shape=[jax.ShapeDtypeStruct((g.shape[0], g.shape[1] // 2, g.shape[2]), g.dtype) for g in gs],
        scratch_shapes=[pltpu.SemaphoreType.DMA((n,))] * 2,
    )(*gs)


def _add_half(g, got, c_arr):
    _, r, cw = g.shape

    def body(c_ref, a_ref, b_ref, o_ref):
        o_ref[...] = (a_ref[...].astype(F32) + b_ref[...].astype(F32)).astype(BF16)

    return pl.pallas_call(
        body, name="grad_add_half",
        grid_spec=pltpu.PrefetchScalarGridSpec(
            num_scalar_prefetch=1, grid=(N_CHIPS,),
            in_specs=[pl.BlockSpec((None, r // 2, cw), lambda k, cr: (k, cr[0], 0)),
                      pl.BlockSpec((None, r // 2, cw), lambda k, cr: (k, 0, 0))],
            out_specs=pl.BlockSpec((None, r // 2, cw), lambda k, cr: (k, 0, 0))),
        out_shape=jax.ShapeDtypeStruct((N_CHIPS, r // 2, cw), BF16), compiler_params=_params(),
    )(c_arr, g, got)


def _scatter_start(k, ss, thru):
    n = len(ss)

    def body(*refs):
        s_refs, land_refs = refs[2 * n + 1:3 * n + 1], refs[3 * n + 1:4 * n + 1]
        send_sems, recv_sems = refs[4 * n + 2:]
        x, y, c, chips = _place()
        me = 2 * x + y
        for a in range(n):
            for j, chip in enumerate(chips):
                pltpu.make_async_remote_copy(
                    src_ref=s_refs[a].at[2 * chip[0] + chip[1]], dst_ref=land_refs[a].at[me],
                    send_sem=send_sems.at[3 * a + j], recv_sem=recv_sems.at[3 * a + j], device_id=(*chip, c),
                    device_id_type=MESH).start()

    lands = [lax.empty(s.shape, s.dtype) for s in ss]
    hbm = [pltpu.HBM(s.shape, s.dtype) for s in ss]
    outs = pl.pallas_call(
        body, name=f"grad_scatter_start_{k}", in_specs=[HBM_SPEC] * (2 * n + 1),
        out_specs=[HBM_SPEC] * (2 * n + 1) + [SEM_SPEC, SEM_SPEC],
        out_shape=hbm + hbm + [pltpu.HBM(thru.shape, thru.dtype)] + [pltpu.SemaphoreType.DMA((3 * n,))] * 2,
        input_output_aliases={a: a for a in range(2 * n + 1)}, compiler_params=SPLIT_COPY,
    )(*[pltpu.with_memory_space_constraint(v, pltpu.HBM) for v in list(ss) + lands + [thru]])
    return (outs[:n], outs[n:2 * n], outs[2 * n + 1], outs[2 * n + 2]), outs[2 * n]


def _scatter_wait(k, ss, lands, send_sems, recv_sems, after):
    n = len(ss)

    def body(*refs):
        s_refs, land_refs = refs[:n], refs[n:2 * n]
        ssem, rsem = refs[2 * n], refs[2 * n + 1]
        x, y, c, chips = _place()
        me = 2 * x + y
        for a in range(n):
            for j, chip in enumerate(chips):
                cid = 2 * chip[0] + chip[1]
                cp = pltpu.make_async_remote_copy(
                    src_ref=s_refs[a].at[cid], dst_ref=land_refs[a].at[cid], send_sem=ssem.at[3 * a + j],
                    recv_sem=rsem.at[3 * a + j], device_id=(*chip, c), device_id_type=MESH)
                cp.wait_send()
                cp.wait_recv()

    hbm = [pltpu.HBM(s.shape, s.dtype) for s in ss]
    outs = pl.pallas_call(
        body, name=f"grad_scatter_wait_{k}",
        in_specs=[HBM_SPEC] * (2 * n) + [SEM_SPEC, SEM_SPEC, pl.BlockSpec(memory_space=pl.ANY)],
        out_specs=[HBM_SPEC] * (2 * n), out_shape=hbm + hbm,
        input_output_aliases={a: a for a in range(2 * n)}, compiler_params=SPLIT_COPY,
    )(*ss, *lands, send_sems, recv_sems, after)
    return outs[:n], outs[n:]


def _sum_chips(land, s, me_arr):
    _, rh, cw = land.shape

    def body(me_ref, land_ref, s_ref, o_ref):
        for own in range(N_CHIPS):
            @pl.when(me_ref[0] == own)
            def _(own=own):
                acc = None
                for k in range(N_CHIPS):
                    term = (s_ref[...] if k == own else land_ref[k]).astype(F32)
                    acc = term if acc is None else acc + term
                o_ref[...] = acc

    return pl.pallas_call(
        body, name="grad_sum_chips",
        grid_spec=pltpu.PrefetchScalarGridSpec(
            num_scalar_prefetch=1, grid=(1,),
            in_specs=[pl.BlockSpec((N_CHIPS, rh, cw), lambda i, me: (0, 0, 0)),
                      pl.BlockSpec((None, rh, cw), lambda i, me: (me[0], 0, 0))],
            out_specs=pl.BlockSpec((rh, cw), lambda i, me: (0, 0))),
        out_shape=jax.ShapeDtypeStruct((rh, cw), F32), compiler_params=_params(),
    )(me_arr, land, s)


def _swap_halves(fs):
    n = len(fs)

    def body(*refs):
        f_refs, out_refs = refs[:n], refs[n:2 * n]
        send_sems, recv_sems = refs[2 * n:]
        x, y, c, _ = _place()
        cps = [pltpu.make_async_remote_copy(
            src_ref=f_refs[a], dst_ref=out_refs[a], send_sem=send_sems.at[a], recv_sem=recv_sems.at[a],
            device_id=(x, y, 1 - c), device_id_type=MESH) for a in range(n)]
        for cp in cps:
            cp.start()
        for cp in cps:
            cp.wait()

    any_spec = pl.BlockSpec(memory_space=pl.ANY)
    return pl.pallas_call(
        body, name="grad_swap_halves", in_specs=[any_spec] * n, out_specs=[any_spec] * n,
        out_shape=[jax.ShapeDtypeStruct(f.shape, f.dtype) for f in fs],
        scratch_shapes=[pltpu.SemaphoreType.DMA((n,))] * 2,
    )(*fs)


def _allreduce_rows(stats):
    def body(s_ref, o_ref, buf, send_sems, recv_sems):
        x, y, c, _ = _place()
        me = 4 * x + 2 * y + c
        buf[me] = s_ref[...]
        cps = []
        for k in range(1, 8):
            px = jnp.where(k & 4, 1 - x, x)
            py = jnp.where(k & 2, 1 - y, y)
            pc = jnp.where(k & 1, 1 - c, c)
            cps.append(pltpu.make_async_remote_copy(
                src_ref=s_ref, dst_ref=buf.at[me], send_sem=send_sems.at[k - 1], recv_sem=recv_sems.at[k - 1],
                device_id=(px, py, pc), device_id_type=MESH))
        for cp in cps:
            cp.start()
        for cp in cps:
            cp.wait()
        acc = buf[0]
        for d in range(1, 8):
            acc = acc + buf[d]
        o_ref[...] = acc

    vm = pl.BlockSpec(memory_space=pltpu.VMEM)
    return pl.pallas_call(
        body, name="allreduce_rows", in_specs=[vm], out_specs=vm,
        out_shape=jax.ShapeDtypeStruct(stats.shape, F32),
        scratch_shapes=[pltpu.VMEM((8,) + stats.shape, F32), pltpu.SemaphoreType.DMA((7,)),
                        pltpu.SemaphoreType.DMA((7,))],
    )(stats)


def _adamw_math(w, g, m, v):
    m = ADAM_B1 * m + (1.0 - ADAM_B1) * g
    v = ADAM_B2 * v + (1.0 - ADAM_B2) * (g * g)
    m_hat = m / (1.0 - ADAM_B1 ** ADAM_STEP)
    v_hat = v / (1.0 - ADAM_B2 ** ADAM_STEP)
    delta = -ADAM_LR * (m_hat / (jnp.sqrt(v_hat) + ADAM_EPS) + ADAM_WD * w)
    return delta, m, v


def _adamw(w, m, v, mine, theirs, l, c_arr, earlier):
    L, r, cw = w.shape
    tr = r // 4 if r > 256 else r // 2
    nblk = (r // 2) // tr

    def body(c_ref, w_ref, m_ref, v_ref, a_ref, b_ref, *rest):
        go_ref, d_ref, mo_ref, vo_ref = rest[-4:]
        g = jnp.where(pl.program_id(0) == c_ref[0], a_ref[...], b_ref[...])
        delta, mn, vn = _adamw_math(w_ref[...], g, m_ref[...], v_ref[...])
        go_ref[...] = g
        d_ref[...] = delta
        mo_ref[...] = mn
        vo_ref[...] = vn

    blk = pl.BlockSpec((None, tr, cw), lambda hh, i, cr: (l, hh * nblk + i, 0))
    half = pl.BlockSpec((tr, cw), lambda hh, i, cr: (i, 0))
    sh = jax.ShapeDtypeStruct(w.shape, F32)
    held = [] if earlier is None else list(earlier)
    return pl.pallas_call(
        body, name="adamw",
        grid_spec=pltpu.PrefetchScalarGridSpec(
            num_scalar_prefetch=1, grid=(2, nblk),
            in_specs=[blk, blk, blk, half, half] + [pl.BlockSpec(memory_space=pl.ANY)] * len(held),
            out_specs=[blk] * 4),
        out_shape=[sh] * 4, input_output_aliases={6 + t: t for t in range(len(held))},
        compiler_params=_params(),
    )(c_arr, w, m, v, mine, theirs, *held)


def _adamw_rows(w, m, v, g):
    def body(w_ref, m_ref, v_ref, g_ref, d_ref, mo_ref, vo_ref):
        delta, mn, vn = _adamw_math(w_ref[...], g_ref[...], m_ref[...], v_ref[...])
        d_ref[...] = delta
        mo_ref[...] = mn
        vo_ref[...] = vn

    vm = pl.BlockSpec(memory_space=pltpu.VMEM)
    sh = jax.ShapeDtypeStruct(w.shape, F32)
    return pl.pallas_call(body, name="adamw_rows", in_specs=[vm] * 4, out_specs=[vm] * 3, out_shape=[sh] * 3)(w, m, v, g)


SUBLAYERS = (("ffn1_w_gate", "ffn1_w_up", "ffn1_w_down"), ("w_in", "w_proj_dil", "w_proj_sb", "w_out"),
             ("ffn2_w_gate", "ffn2_w_up", "ffn2_w_down"))
TRANSPOSED = ("ffn1_w_gate", "ffn1_w_up", "ffn2_w_gate", "ffn2_w_up")
LAG = 2


def _pick_row(blocks):
    row = lax.broadcasted_iota(jnp.int32, (8, D_MODEL), 0)
    out = jnp.zeros((8, D_MODEL), F32)
    for i, b in enumerate(blocks):
        out = out + jnp.where(row == i, b, 0.0)
    return out


def kernel(x, norm_ffn1, ffn1_w_gate, ffn1_w_up, ffn1_w_down, norm_mix, w_in, w_proj_dil, w_proj_sb, w_out, norm_ffn2, ffn2_w_gate, ffn2_w_up, ffn2_w_down, norm_final, loss_target, m_norm_ffn1, m_ffn1_w_gate, m_ffn1_w_up, m_ffn1_w_down, m_norm_mix, m_w_in, m_w_proj_dil, m_w_proj_sb, m_w_out, m_norm_ffn2, m_ffn2_w_gate, m_ffn2_w_up, m_ffn2_w_down, m_norm_final, v_norm_ffn1, v_ffn1_w_gate, v_ffn1_w_up, v_ffn1_w_down, v_norm_mix, v_w_in, v_w_proj_dil, v_w_proj_sb, v_w_out, v_norm_ffn2, v_ffn2_w_gate, v_ffn2_w_up, v_ffn2_w_down, v_norm_final):
    given = dict(locals())
    for n in TRANSPOSED:
        for k in ("", "m_", "v_"):
            given[k + n] = jnp.swapaxes(given[k + n], 1, 2)
    weights = {n: given[n] for n in WEIGHT_NAMES}
    norms = {n: given[n] for n in NORM_NAMES}

    c_arr = lax.axis_index("c").astype(jnp.int32).reshape(1)
    me_arr = (2 * lax.axis_index("x") + lax.axis_index("y")).astype(jnp.int32).reshape(1)
    order = [(l, s, n) for l in range(DEPTH) for s in range(len(SUBLAYERS)) for n in SUBLAYERS[s]]
    n_first = len(SUBLAYERS[0])
    sent, token = {}, None
    for tag, idxs in (("a", range(n_first)), ("b", range(n_first, len(order)))):
        cast = [_cast_into_slot(weights[order[i][2]], order[i][0], me_arr, token) for i in idxs]
        bufs, send_sems, recv_sems, token = _gather_start(tag, cast)
        for p, i in enumerate(idxs):
            sent[i] = (bufs[p], p, send_sems, recv_sems)

    def weights_of(l, s, after):
        idxs = [i for i, (ll, ss, _) in enumerate(order) if (ll, ss) == (l, s)]
        got = _gather_wait(len(SUBLAYERS) * l + s, [sent[i][0] for i in idxs], [sent[i][1] for i in idxs],
                           sent[idxs[0]][2], sent[idxs[0]][3], after)
        return {order[i][2]: g for i, g in zip(idxs, _gather_relay(got))}

    out = {}
    in_flight = []

    def finish(l, s, names, sums, lands, ssem, rsem, after):
        sums, lands = _scatter_wait(len(SUBLAYERS) * l + s, sums, lands, ssem, rsem, after)
        mine = [_sum_chips(land, sm, me_arr) for land, sm in zip(lands, sums)]
        theirs = _swap_halves(mine)
        for n, ga, gb in zip(names, mine, theirs):
            out[n] = _adamw(weights[n], given["m_" + n], given["v_" + n], ga, gb, l, c_arr, out.get(n))

    def on_grads(l, s, grads, after):
        names = list(grads)
        gs = [grads[n] for n in names]
        sums = [_add_half(g, got, c_arr) for g, got in zip(gs, _exchange_halves(gs))]
        sent, after = _scatter_start(len(SUBLAYERS) * l + s, sums, after)
        in_flight.append((l, s, names) + sent)
        if len(in_flight) > LAG:
            finish(*in_flight.pop(0), after)
        return after

    loss_blk, grad_x, gains, dg_final = _local_step(x[0], loss_target[0], norms, norm_final, weights_of, on_grads)
    while in_flight:
        finish(*in_flight.pop(0), grad_x)
    out = {k + n: (jnp.swapaxes(v, 1, 2) if n in TRANSPOSED else v)
           for n, res in out.items() for k, v in zip(("grad_", "delta_", "new_m_", "new_v_"), res)}
    out["grad_x"] = grad_x[None]

    rows = [gains[l][n] for n in NORM_NAMES for l in range(DEPTH)] + [dg_final, loss_blk]
    total = _allreduce_rows(_pick_row(rows))
    out["loss"] = total[7, 0]
    wn = jnp.concatenate([given[n] for n in NORM_NAMES] + [norm_final[None], jnp.zeros((1, D_MODEL), F32)])
    mn_ = jnp.concatenate([given["m_" + n] for n in NORM_NAMES] + [m_norm_final[None], jnp.zeros((1, D_MODEL), F32)])
    vn_ = jnp.concatenate([given["v_" + n] for n in NORM_NAMES] + [v_norm_final[None], jnp.ones((1, D_MODEL), F32)])
    d_n, m_n, v_n = _adamw_rows(wn, mn_, vn_, total)
    for i, n in enumerate(NORM_NAMES):
        sl = slice(i * DEPTH, (i + 1) * DEPTH)
        out["grad_" + n], out["delta_" + n], out["new_m_" + n], out["new_v_" + n] = total[sl], d_n[sl], m_n[sl], v_n[sl]
    out["grad_norm_final"], out["delta_norm_final"] = total[6], d_n[6]
    out["new_m_norm_final"], out["new_v_norm_final"] = m_n[6], v_n[6]

    names = ["norm_ffn1", "ffn1_w_gate", "ffn1_w_up", "ffn1_w_down", "norm_mix", "w_in", "w_proj_dil", "w_proj_sb",
             "w_out", "norm_ffn2", "ffn2_w_gate", "ffn2_w_up", "ffn2_w_down", "norm_final"]
    return (out["loss"], out["grad_x"], *[out["grad_" + n] for n in names], *[out["delta_" + n] for n in names],
            *[out["new_m_" + n] for n in names], *[out["new_v_" + n] for n in names])
```

```python
import functools

import jax
import jax.numpy as jnp
from jax import lax
from jax.experimental import pallas as pl
from jax.experimental.pallas import tpu as pltpu

F32 = jnp.float32
BF16 = jnp.bfloat16

D_MODEL = 1024
DEPTH = 2
N_CHIPS = 4
HEAD_DIM = 64
ROPE_DIM = 16
ROPE_THETA = 500000.0
DIL_GROUPS = ((128, 1), (512, 4), (2048, 16))
SPAN = 128
Q_BLOCK = 128
RMS_EPS = 1e-6
D_ATT = 256
COL_QS = 2304
COL_GD = 3072
COL_GS = 4096
ADAM_LR, ADAM_B1, ADAM_B2, ADAM_EPS, ADAM_WD, ADAM_STEP = 0.001, 0.9, 0.999, 1e-08, 0.01, 10

VMEM_LIMIT = 52 * 1024 * 1024
TM = 512
NEG = -1e30

NN = (((1,), (0,)), ((), ()))
NT = (((1,), (1,)), ((), ()))
TN = (((0,), (0,)), ((), ()))
MESH = pl.DeviceIdType.MESH

WEIGHT_NAMES = ("ffn1_w_gate", "ffn1_w_up", "ffn1_w_down", "w_in", "w_proj_dil",
                "w_proj_sb", "w_out", "ffn2_w_gate", "ffn2_w_up", "ffn2_w_down")
NORM_NAMES = ("norm_ffn1", "norm_mix", "norm_ffn2")


def _params(**kw):
    return pltpu.CompilerParams(vmem_limit_bytes=VMEM_LIMIT, **kw)


def _sigmoid(x):
    return 0.5 * jnp.tanh(0.5 * x) + 0.5


def _mm_body(pairs, n_in, n_out, n_acc, dims, nk, i_axis, epilogue, *refs):
    ins = refs[:n_in]
    outs = refs[n_in:n_in + n_out]
    accs = refs[n_in + n_out:]
    i = pl.program_id(i_axis)
    k = pl.program_id(2)

    def operand(a):
        return (a(ins) if callable(a) else ins[a][...]).astype(BF16)

    def dot(ia, ib):
        return lax.dot_general(operand(ia), operand(ib), dims, preferred_element_type=F32)

    if nk == 1:
        parts = [None] * n_acc
        for ia, ib, ic in pairs:
            parts[ic] = dot(ia, ib) if parts[ic] is None else parts[ic] + dot(ia, ib)
        epilogue(parts, ins, outs, i)
        return

    @pl.when(k == 0)
    def _():
        for c in range(n_acc):
            accs[c][...] = jnp.zeros_like(accs[c])

    for ia, ib, ic in pairs:
        accs[ic][...] += dot(ia, ib)

    @pl.when(k == nk - 1)
    def _():
        epilogue([a[...] for a in accs], ins, outs, i)


def _j_outer(spec):
    f = spec.index_map
    return pl.BlockSpec(spec.block_shape, lambda j, i, k: f(i, j, k))


def _mm(name, ins, in_specs, pairs, n_acc, acc_shape, dims, grid, epilogue, out_shapes, out_specs, j_outer=False):
    nk = grid[2]
    if j_outer:
        grid = (grid[1], grid[0], grid[2])
        in_specs = [_j_outer(s) for s in in_specs]
        out_specs = [_j_outer(s) for s in out_specs]
    scratch = [pltpu.VMEM(acc_shape, F32) for _ in range(n_acc)] if nk > 1 else []
    body = functools.partial(_mm_body, tuple(pairs), len(ins), len(out_shapes), n_acc, dims, nk,
                             1 if j_outer else 0, epilogue)
    return pl.pallas_call(
        body, name=name, grid=grid, in_specs=in_specs, out_specs=out_specs, out_shape=out_shapes,
        scratch_shapes=scratch,
        compiler_params=_params(dimension_semantics=("arbitrary", "arbitrary", "arbitrary")),
    )(*ins)


def _wspec(r, c, l, by):
    if by == 1:
        return pl.BlockSpec((None, None, r, c), lambda i, j, k: (j, l, 0, 0))
    return pl.BlockSpec((None, None, r, c), lambda i, j, k: (k, l, 0, 0))


def _rms_bwd_epilogue(x_idx, g_idx, dxo_idx):
    def ep(vals, ins, outs, i):
        dh = vals[0]
        x = ins[x_idx][...]
        g = ins[g_idx][...]
        rstd = lax.rsqrt(jnp.mean(x * x, axis=-1, keepdims=True) + RMS_EPS)
        xhat = x * rstd
        dxhat = dh * g
        dx = rstd * (dxhat - xhat * jnp.mean(dxhat * xhat, axis=-1, keepdims=True))
        outs[0][...] = ins[dxo_idx][...] + dx
        dg = jnp.broadcast_to(jnp.sum(dh * xhat, axis=0, keepdims=True), outs[1].shape)

        @pl.when(i == 0)
        def _():
            outs[1][...] = dg

        @pl.when(i > 0)
        def _():
            outs[1][...] += dg
    return ep


def _rms_fwd(x, gain):
    T = x.shape[0]

    def body(x_ref, g_ref, h_ref):
        xv = x_ref[...]
        h = xv * lax.rsqrt(jnp.mean(xv * xv, axis=-1, keepdims=True) + RMS_EPS)
        h_ref[...] = (h * g_ref[...]).astype(BF16)

    return pl.pallas_call(
        body, name="rms_fwd", grid=(T // TM,),
        in_specs=[pl.BlockSpec((TM, D_MODEL), lambda i: (i, 0)), pl.BlockSpec((1, D_MODEL), lambda i: (0, 0))],
        out_specs=pl.BlockSpec((TM, D_MODEL), lambda i: (i, 0)),
        out_shape=jax.ShapeDtypeStruct((T, D_MODEL), BF16), compiler_params=_params(),
    )(x, gain)


def _rope_tables(T):
    pos = jnp.arange(T, dtype=F32)
    inv_freq = ROPE_THETA ** (-jnp.arange(0, ROPE_DIM, 2, dtype=F32) / ROPE_DIM)
    ang = pos[:, None] * inv_freq[None, :]
    cos, sin = jnp.cos(ang), jnp.sin(ang)
    half = ROPE_DIM // 2
    one = jnp.ones((T, HEAD_DIM - ROPE_DIM), F32)
    zero = jnp.zeros((T, HEAD_DIM - ROPE_DIM), F32)
    zh = jnp.zeros((T, half), F32)
    c = jnp.concatenate([cos, cos, one], axis=1)
    s1 = jnp.concatenate([-sin, zh, zero], axis=1)
    s2 = jnp.concatenate([zh, sin, zero], axis=1)
    return tuple(jnp.concatenate([t, t], axis=1) for t in (c, s1, s2))


def _rope_fwd(xv, c, s1, s2):
    w = xv.shape[1]
    half = ROPE_DIM // 2
    return xv * c + pltpu.roll(xv, w - half, 1) * s1 + pltpu.roll(xv, half, 1) * s2


def _rope_bwd(dy, c, s1, s2):
    w = dy.shape[1]
    half = ROPE_DIM // 2
    return dy * c + pltpu.roll(dy * s1, half, 1) + pltpu.roll(dy * s2, w - half, 1)


def _assemble_dproj(dqk, rest, gates, tabs):
    T = gates[0].shape[0]
    n_qk, n_rest = len(dqk), len(rest)
    width = (n_qk + n_rest) * D_ATT + 2 * D_MODEL

    def body(*refs):
        ins, (c_ref, s1_ref, s2_ref), o_ref = refs[:n_qk + n_rest + 2], refs[-4:-1], refs[-1]
        c = jnp.concatenate([c_ref[...]] * 2, axis=1)
        s1 = jnp.concatenate([s1_ref[...]] * 2, axis=1)
        s2 = jnp.concatenate([s2_ref[...]] * 2, axis=1)
        for b in range(n_qk + n_rest):
            v = ins[b][...]
            if b < n_qk:
                v = _rope_bwd(v, c, s1, s2)
            o_ref[:, b * D_ATT:(b + 1) * D_ATT] = v.astype(BF16)
        off = (n_qk + n_rest) * D_ATT
        o_ref[:, off:off + D_MODEL] = ins[-2][...]
        o_ref[:, off + D_MODEL:] = ins[-1][...]

    att = pl.BlockSpec((TM, D_ATT), lambda i: (i, 0))
    wide = pl.BlockSpec((TM, D_MODEL), lambda i: (i, 0))
    tab = pl.BlockSpec((TM, 128), lambda i: (i, 0))
    return pl.pallas_call(
        body, name="assemble_dproj", grid=(T // TM,),
        in_specs=[att] * (n_qk + n_rest) + [wide, wide, tab, tab, tab],
        out_specs=pl.BlockSpec((TM, width), lambda i: (i, 0)),
        out_shape=jax.ShapeDtypeStruct((T, width), BF16), compiler_params=_params(),
    )(*dqk, *rest, *gates, *tabs)


def _dil_merge(os_, lses):
    T = os_[0].shape[0]

    def body(o0, o1, o2, l0, l1, l2, o_ref, lse_ref):
        a, b, c = l0[...], l1[...], l2[...]
        m = jnp.maximum(jnp.maximum(a, b), c)
        ea, eb, ec = jnp.exp(a - m), jnp.exp(b - m), jnp.exp(c - m)
        den = ea + eb + ec
        o_ref[...] = (ea * o0[...] + eb * o1[...] + ec * o2[...]) / den
        lse_ref[...] = m + jnp.log(den)

    blk = pl.BlockSpec((TM, D_ATT), lambda i: (i, 0))
    sh = jax.ShapeDtypeStruct((T, D_ATT), F32)
    return pl.pallas_call(
        body, name="dil_merge", grid=(T // TM,), in_specs=[blk] * 6, out_specs=[blk, blk],
        out_shape=[sh, sh], compiler_params=_params(),
    )(*os_, *lses)


def _final_loss(x, gain, target):
    T = x.shape[0]

    def body(x_ref, g_ref, t_ref, dx_ref, dg_ref, loss_ref):
        xv = x_ref[...]
        g = g_ref[...]
        rstd = lax.rsqrt(jnp.mean(xv * xv, axis=-1, keepdims=True) + RMS_EPS)
        xhat = xv * rstd
        err = xhat * g - t_ref[...]
        loss = 0.5 * jnp.sum(jnp.mean(err * err, axis=-1, keepdims=True), axis=0, keepdims=True)
        dy = err * (1.0 / D_MODEL)
        dxhat = dy * g
        dx_ref[...] = rstd * (dxhat - xhat * jnp.mean(dxhat * xhat, axis=-1, keepdims=True))
        dg = jnp.broadcast_to(jnp.sum(dy * xhat, axis=0, keepdims=True), dg_ref.shape)
        ls = jnp.broadcast_to(loss, loss_ref.shape)

        @pl.when(pl.program_id(0) == 0)
        def _():
            dg_ref[...] = dg
            loss_ref[...] = ls

        @pl.when(pl.program_id(0) > 0)
        def _():
            dg_ref[...] += dg
            loss_ref[...] += ls

    blk = pl.BlockSpec((TM, D_MODEL), lambda i: (i, 0))
    row = pl.BlockSpec((1, D_MODEL), lambda i: (0, 0))
    acc = pl.BlockSpec((8, D_MODEL), lambda i: (0, 0))
    return pl.pallas_call(
        body, name="final_loss", grid=(T // TM,), in_specs=[blk, row, blk], out_specs=[blk, acc, acc],
        out_shape=[jax.ShapeDtypeStruct((T, D_MODEL), F32), jax.ShapeDtypeStruct((8, D_MODEL), F32),
                   jax.ShapeDtypeStruct((8, D_MODEL), F32)],
        compiler_params=_params(dimension_semantics=("arbitrary",)),
    )(x, gain, target)


def _pair_masks():
    lane = lax.broadcasted_iota(jnp.int32, (SPAN, 128), 1)
    return [lane < HEAD_DIM, lane >= HEAD_DIM]


def _stack_heads(x, masks):
    return jnp.concatenate([jnp.where(m, x, 0.0) for m in masks], axis=0)


def _unstack_heads(y, masks):
    rows = y.shape[0] // len(masks)
    out = jnp.where(masks[0], y[:rows], 0.0)
    for h in range(1, len(masks)):
        out = out + jnp.where(masks[h], y[rows * h:rows * (h + 1)], 0.0)
    return out


def _dil_rows(idx, d):
    u = idx // d
    r = idx - u * d
    own = pl.ds(u * (SPAN * d) + r, SPAN, stride=d) if d > 1 else pl.ds(pl.multiple_of(u * SPAN, SPAN), SPAN)
    up = jnp.maximum(u - 1, 0)
    prev = pl.ds(up * (SPAN * d) + r, SPAN, stride=d) if d > 1 else pl.ds(pl.multiple_of(up * SPAN, SPAN), SPAN)
    return u, own, prev


def _dil_valid(u):
    qi = lax.broadcasted_iota(jnp.int32, (2 * SPAN, 2 * SPAN), 0) & (SPAN - 1)
    kj = lax.broadcasted_iota(jnp.int32, (2 * SPAN, 2 * SPAN), 1)
    in_prev = (kj < SPAN) & (kj >= qi + jnp.where(u > 0, 0, SPAN))
    return in_prev | ((kj >= SPAN) & (kj - SPAN <= qi))


def _dil_keys(ref, own, prev):
    return jnp.concatenate([ref[prev, :], ref[own, :]], axis=0).astype(BF16)


def _dil_fwd(proj, g, d):
    T = proj.shape[0]
    n_iter = T // SPAN

    def body(q_ref, k_ref, v_ref, o_ref, lse_ref):
        masks = _pair_masks()

        def step(idx, carry):
            u, own, prev = _dil_rows(idx, d)
            qs = _stack_heads(q_ref[own, :] * (HEAD_DIM ** -0.5), masks).astype(BF16)
            kk = _dil_keys(k_ref, own, prev)
            vv = _dil_keys(v_ref, own, prev)
            s = jnp.where(_dil_valid(u), lax.dot_general(qs, kk, NT, preferred_element_type=F32), NEG)
            m = jnp.max(s, axis=1, keepdims=True)
            p = jnp.exp(s - m)
            den = jnp.sum(p, axis=1, keepdims=True)
            pv = lax.dot_general(p.astype(BF16), vv, NN, preferred_element_type=F32) / den
            o_ref[own, :] = _unstack_heads(pv, masks)
            lse_ref[own, :] = _unstack_heads(jnp.broadcast_to(m + jnp.log(den), pv.shape), masks)
            return carry

        lax.fori_loop(0, n_iter, step, 0, unroll=2)

    def col(b):
        return pl.BlockSpec((T, 128), lambda p: (0, b + p))

    sh = jax.ShapeDtypeStruct((T, D_ATT), F32)
    out = pl.BlockSpec((T, 128), lambda p: (0, p))
    return pl.pallas_call(
        body, name=f"dil_fwd_d{d}", grid=(2,),
        in_specs=[col(2 * g), col(6 + 2 * g), col(12 + 2 * g)], out_specs=[out, out], out_shape=[sh, sh],
        compiler_params=_params(dimension_semantics=("arbitrary",)),
    )(proj, proj, proj)


def _dil_bwd(proj, do, o_dil, lse, g, d):
    T = proj.shape[0]
    n_iter = T // SPAN

    def body(q_ref, k_ref, v_ref, do_ref, o_ref, lse_ref, dq_ref, dk_ref, dv_ref):
        masks = _pair_masks()
        head_lanes = jnp.concatenate(masks, axis=0)

        def step(idx, carry):
            u, own, prev = _dil_rows(idx, d)
            qs = _stack_heads(q_ref[own, :] * (HEAD_DIM ** -0.5), masks).astype(BF16)
            kk = _dil_keys(k_ref, own, prev)
            vv = _dil_keys(v_ref, own, prev)
            dom = _stack_heads(do_ref[own, :], masks)
            dos = dom.astype(BF16)
            delta = jnp.sum(dom * jnp.concatenate([o_ref[own, :]] * 2, axis=0), axis=1, keepdims=True)
            lrow = jnp.max(jnp.where(head_lanes, jnp.concatenate([lse_ref[own, :]] * 2, axis=0), NEG),
                           axis=1, keepdims=True)
            s = lax.dot_general(qs, kk, NT, preferred_element_type=F32)
            p = jnp.where(_dil_valid(u), jnp.exp(s - lrow), 0.0)
            dp = lax.dot_general(dos, vv, NT, preferred_element_type=F32)
            ds = (p * (dp - delta)).astype(BF16)
            dq = lax.dot_general(ds, kk, NN, preferred_element_type=F32)
            dkk = lax.dot_general(ds, qs, TN, preferred_element_type=F32)
            dvv = lax.dot_general(p.astype(BF16), dos, TN, preferred_element_type=F32)
            dq_ref[own, :] = _unstack_heads(dq, masks) * (HEAD_DIM ** -0.5)
            dk_ref[own, :] = dkk[SPAN:]
            dv_ref[own, :] = dvv[SPAN:]
            dk_ref[prev, :] = dk_ref[prev, :] + dkk[:SPAN]
            dv_ref[prev, :] = dv_ref[prev, :] + dvv[:SPAN]
            return carry

        lax.fori_loop(0, n_iter, step, 0, unroll=2)

    def col(b):
        return pl.BlockSpec((T, 128), lambda p: (0, b + p))

    sh = jax.ShapeDtypeStruct((T, D_ATT), F32)
    return pl.pallas_call(
        body, name=f"dil_bwd_d{d}", grid=(2,),
        in_specs=[col(2 * g), col(6 + 2 * g), col(12 + 2 * g), col(0), col(0), col(0)],
        out_specs=[col(0), col(0), col(0)], out_shape=[sh, sh, sh],
        compiler_params=_params(dimension_semantics=("arbitrary",)),
    )(proj, proj, proj, do, o_dil, lse)


SB_KT = 512


def _sb_tri(strict):
    a = lax.broadcasted_iota(jnp.int32, (Q_BLOCK, Q_BLOCK), 0)
    b = lax.broadcasted_iota(jnp.int32, (Q_BLOCK, Q_BLOCK), 1)
    return jnp.where((a > b) if strict else (a >= b), 1.0, 0.0).astype(BF16)


def _suffix(x, c, tri):
    r = x.shape[0]
    nb = x.shape[1] // Q_BLOCK
    blocks = [x[:, Q_BLOCK * b:Q_BLOCK * (b + 1)] for b in range(nb)]
    hi = [b.astype(BF16) for b in blocks]
    lo = [(b - h.astype(F32)).astype(BF16) for b, h in zip(blocks, hi)]
    y = lax.dot_general(jnp.concatenate(hi + lo, axis=0), tri, NN, preferred_element_type=F32)
    outs = [None] * nb
    run = c
    for b in reversed(range(nb)):
        outs[b] = run + y[r * b:r * (b + 1)] + y[r * (nb + b):r * (nb + b + 1)]
        run = run + jnp.sum(blocks[b], axis=1, keepdims=True)
    return jnp.concatenate(outs, axis=1), run


def _sb_tile(qs, kb, past, c, tri):
    z = lax.dot_general(qs, kb, NT, preferred_element_type=F32)
    lsz = jnp.minimum(z, 0.0) - jnp.log(1.0 + jnp.exp(-jnp.abs(z)))
    lk = lsz - z
    if past is not None:
        lk = jnp.where(past, lk, 0.0)
    after, c_new = _suffix(lk, c, tri)
    w = jnp.exp(lsz + after)
    if past is not None:
        w = jnp.where(past, w, 0.0)
    return z, lsz, w, c_new


SB_HEADS = D_ATT // HEAD_DIM
SB_ROWS = SB_HEADS * Q_BLOCK


def _sb_past(i, t):
    row = lax.broadcasted_iota(jnp.int32, (SB_ROWS, SB_KT), 0) & (Q_BLOCK - 1)
    col = lax.broadcasted_iota(jnp.int32, (SB_ROWS, SB_KT), 1)
    return col + t * SB_KT < row + i * Q_BLOCK


def _sb_head_masks():
    lane = lax.broadcasted_iota(jnp.int32, (Q_BLOCK, D_ATT), 1)
    return [(lane >= HEAD_DIM * h) & (lane < HEAD_DIM * (h + 1)) for h in range(SB_HEADS)]


def _sb_rows(t):
    return pl.ds(pl.multiple_of(t * SB_KT, SB_KT), SB_KT)


def _sb_fwd(proj):
    T = proj.shape[0]

    def body(q_ref, k_ref, v_ref, o_ref):
        i = pl.program_id(0)
        masks = _sb_head_masks()
        tri = _sb_tri(True)
        qs = _stack_heads(q_ref[...] * (HEAD_DIM ** -0.5), masks).astype(BF16)
        n_tiles = (i * Q_BLOCK) // SB_KT + 1

        def tile(t, carry, masked):
            kb = k_ref[_sb_rows(t), :].astype(BF16)
            vb = v_ref[_sb_rows(t), :].astype(BF16)
            acc, c = carry
            _, _, w, c = _sb_tile(qs, kb, _sb_past(i, t) if masked else None, c, tri)
            pv = lax.dot_general(w.astype(BF16), vb, NN, preferred_element_type=F32)
            return acc + _unstack_heads(pv, masks), c

        carry = tile(n_tiles - 1, (jnp.zeros((Q_BLOCK, D_ATT), F32), jnp.zeros((SB_ROWS, 1), F32)), True)
        carry = lax.fori_loop(0, n_tiles - 1, lambda tt, cr: tile(n_tiles - 2 - tt, cr, False), carry)
        o_ref[...] = carry[0]

    cb = COL_QS // D_ATT
    return pl.pallas_call(
        body, name="sb_fwd", grid=(T // Q_BLOCK,),
        in_specs=[pl.BlockSpec((Q_BLOCK, D_ATT), lambda i: (i, cb)),
                  pl.BlockSpec((T, D_ATT), lambda i: (0, cb + 1)),
                  pl.BlockSpec((T, D_ATT), lambda i: (0, cb + 2))],
        out_specs=pl.BlockSpec((Q_BLOCK, D_ATT), lambda i: (i, 0)),
        out_shape=jax.ShapeDtypeStruct((T, D_ATT), F32),
        compiler_params=_params(dimension_semantics=("arbitrary",)),
    )(proj, proj, proj)


def _sb_bwd(proj, do, o):
    T = proj.shape[0]

    def body(q_ref, k_ref, v_ref, do_ref, o_ref, dq_ref, dk_ref, dv_ref):
        i = pl.program_id(0)
        masks = _sb_head_masks()
        tri = _sb_tri(True)
        tri_incl = _sb_tri(False)

        @pl.when(i == 0)
        def _():
            dk_ref[...] = jnp.zeros_like(dk_ref)
            dv_ref[...] = jnp.zeros_like(dv_ref)

        qs = _stack_heads(q_ref[...] * (HEAD_DIM ** -0.5), masks).astype(BF16)
        dos = _stack_heads(do_ref[...], masks).astype(BF16)
        delta = jnp.sum(dos.astype(F32) * jnp.concatenate([o_ref[...]] * SB_HEADS, axis=0), axis=1, keepdims=True)
        n_tiles = (i * Q_BLOCK) // SB_KT + 1

        def tile(t, carry, masked):
            rows = _sb_rows(t)
            kb = k_ref[rows, :].astype(BF16)
            vb = v_ref[rows, :].astype(BF16)
            past = _sb_past(i, t) if masked else None
            dq, c, ce = carry
            z, lsz, w, c = _sb_tile(qs, kb, past, c, tri)
            gv = lax.dot_general(dos, vb, NT, preferred_element_type=F32)
            wb = w.astype(BF16)
            e = wb.astype(F32) * gv
            suf, ce = _suffix(e, ce, tri_incl)
            dz = e * jnp.exp(lsz - z) - (delta - suf) * jnp.exp(lsz)
            if masked:
                dz = jnp.where(past, dz, 0.0)
            dzb = dz.astype(BF16)
            dq = dq + _unstack_heads(lax.dot_general(dzb, kb, NN, preferred_element_type=F32), masks)
            dk_ref[rows, :] = dk_ref[rows, :] + lax.dot_general(dzb, qs, TN, preferred_element_type=F32)
            dv_ref[rows, :] = dv_ref[rows, :] + lax.dot_general(wb, dos, TN, preferred_element_type=F32)
            return dq, c, ce

        zcol = jnp.zeros((SB_ROWS, 1), F32)
        carry = tile(n_tiles - 1, (jnp.zeros((Q_BLOCK, D_ATT), F32), zcol, zcol), True)
        carry = lax.fori_loop(0, n_tiles - 1, lambda tt, cr: tile(n_tiles - 2 - tt, cr, False), carry)
        dq_ref[...] = carry[0] * (HEAD_DIM ** -0.5)

    cb = COL_QS // D_ATT
    blk = pl.BlockSpec((Q_BLOCK, D_ATT), lambda i: (i, 0))
    full = pl.BlockSpec((T, D_ATT), lambda i: (0, 0))
    sh = jax.ShapeDtypeStruct((T, D_ATT), F32)
    return pl.pallas_call(
        body, name="sb_bwd", grid=(T // Q_BLOCK,),
        in_specs=[pl.BlockSpec((Q_BLOCK, D_ATT), lambda i: (i, cb)),
                  pl.BlockSpec((T, D_ATT), lambda i: (0, cb + 1)),
                  pl.BlockSpec((T, D_ATT), lambda i: (0, cb + 2)), blk, blk],
        out_specs=[blk, full, full], out_shape=[sh, sh, sh],
        compiler_params=_params(dimension_semantics=("arbitrary",)),
    )(proj, proj, proj, do, o)


def _tok(c, by=None):
    if by is None:
        return pl.BlockSpec((TM, c), lambda i, j, k: (i, 0))
    if by == 1:
        return pl.BlockSpec((TM, c), lambda i, j, k: (i, j))
    return pl.BlockSpec((TM, c), lambda i, j, k: (i, k))


def _chunked(c, by):
    if by == 1:
        return pl.BlockSpec((None, TM, c), lambda i, j, k: (j, i, 0))
    return pl.BlockSpec((None, TM, c), lambda i, j, k: (k, i, 0))


def _gain_spec():
    return pl.BlockSpec((1, D_MODEL), lambda i, j, k: (0, 0))


def _all_chunks(rows, c):
    return pl.BlockSpec((N_CHIPS, rows, c), lambda i, j, k: (0, i, 0))


def _wfull(r, c, l):
    return pl.BlockSpec((N_CHIPS, None, r, c), lambda i, j, k: (0, l, 0, 0))


def _pick(idx, c):
    return lambda ins: ins[idx][c]


def _cols(idx, c, w):
    return lambda ins: ins[idx][:, c * w:(c + 1) * w]


def _rows(rows, width):
    return pl.BlockSpec((rows, width), lambda i, j, k: (i, 0))


def _whole(shape):
    return pl.BlockSpec(shape, lambda i, j, k: (0, 0))


def _ffn_fwd(x, gain, wg, wu, wd):
    T = x.shape[0]
    wg, wu, wd = (w.reshape(-1, D_MODEL) for w in (wg, wu, wd))
    ff = wd.shape[0]
    tm = TM // 2
    h = _rms_fwd(x, gain)

    def swiglu(vals, ins, outs, i):
        gt, up = vals
        outs[0][...] = gt.astype(BF16)
        outs[1][...] = up.astype(BF16)
        outs[2][...] = (gt * _sigmoid(gt) * up).astype(BF16)

    ash = jax.ShapeDtypeStruct((T, ff), BF16)
    gate, up, act = _mm(
        "ffn_up", [h, wg, wu], [_rows(tm, D_MODEL), _whole(wg.shape), _whole(wu.shape)],
        [(0, 1, 0), (0, 2, 1)], 2, None, NT, (T // tm, 1, 1), swiglu, [ash] * 3, [_rows(tm, ff)] * 3)

    def resid(vals, ins, outs, i):
        outs[0][...] = ins[2][...] + 0.5 * vals[0]

    (y,) = _mm(
        "ffn_down", [act, wd, x], [_rows(TM, ff), _whole(wd.shape), _tok(D_MODEL)], [(0, 1, 0)], 1, None, NN,
        (T // TM, 1, 1), resid, [jax.ShapeDtypeStruct((T, D_MODEL), F32)], [_tok(D_MODEL)])
    return y, (x, h, gate, up, act)


def _ffn_bwd(dxo, gain, wg, wu, wd, saved):
    x, h, gate, up, act = saved
    T = x.shape[0]
    n_chips, _, ffs, _ = wd.shape
    wg, wu, wd = (w.reshape(-1, D_MODEL) for w in (wg, wu, wd))
    ff = wd.shape[0]
    tk = TM
    tm = TM // 2

    def dswiglu(vals, ins, outs, i):
        da = 0.5 * vals[0]
        gt = ins[2][...].astype(F32)
        u = ins[3][...].astype(F32)
        s = _sigmoid(gt)
        outs[0][...] = (da * u * (s * (1.0 + gt * (1.0 - s)))).astype(BF16)
        outs[1][...] = (da * (gt * s)).astype(BF16)

    ash = jax.ShapeDtypeStruct((T, ff), BF16)
    dgate, dup = _mm(
        "ffn_dact", [dxo, wd, gate, up], [_rows(tm, D_MODEL), _whole(wd.shape), _rows(tm, ff), _rows(tm, ff)],
        [(0, 1, 0)], 1, None, NT, (T // tm, 1, 1), dswiglu, [ash, ash], [_rows(tm, ff)] * 2)

    def half(vals, ins, outs, i):
        outs[0][...] = (0.5 * vals[0]).astype(BF16)

    def cast(vals, ins, outs, i):
        outs[0][...] = vals[0].astype(BF16)

    tok_k = pl.BlockSpec((tk, D_MODEL), lambda i, j, k: (k, 0))
    hid_k = pl.BlockSpec((tk, ff), lambda i, j, k: (k, 0))
    wsh = jax.ShapeDtypeStruct((ff, D_MODEL), BF16)
    (dwd,) = _mm("ffn_dwd", [act, dxo], [hid_k, tok_k], [(0, 1, 0)], 1, (ff, D_MODEL), TN, (1, 1, T // tk), half,
                 [wsh], [_whole((ff, D_MODEL))])

    dx, dgain = _mm(
        "ffn_dx", [dgate, dup, wg, wu, x, gain, dxo],
        [_rows(tm, ff), _rows(tm, ff), _whole(wg.shape), _whole(wu.shape), _rows(tm, D_MODEL), _gain_spec(),
         _rows(tm, D_MODEL)],
        [(0, 2, 0), (1, 3, 0)], 1, None, NN, (T // tm, 1, 1), _rms_bwd_epilogue(4, 5, 6),
        [jax.ShapeDtypeStruct((T, D_MODEL), F32), jax.ShapeDtypeStruct((8, D_MODEL), F32)],
        [_rows(tm, D_MODEL), pl.BlockSpec((8, D_MODEL), lambda i, j, k: (0, 0))])

    dws = []
    for dact in (dgate, dup):
        dws += _mm("ffn_dwgu", [dact, h], [hid_k, tok_k], [(0, 1, 0)], 1, (ff, D_MODEL), TN, (1, 1, T // tk), cast,
                   [wsh], [_whole((ff, D_MODEL))])
    dwg, dwu, dwd = (w.reshape(n_chips, ffs, D_MODEL) for w in (dws[0], dws[1], dwd))
    return dx, dgain, dwg, dwu, dwd


def _mixer_fwd(x, gain, W, l, tabs):
    T = x.shape[0]
    win, wpd, wps, wo = W["w_in"], W["w_proj_dil"], W["w_proj_sb"], W["w_out"]
    cin = win.shape[3]
    cp = wpd.shape[3]
    h = _rms_fwd(x, gain)

    n_rope = 6 * D_ATT

    tm = TM // 2

    def roped(vals, ins, outs, i):
        for j, v in enumerate(vals):
            lo = j * cin
            k = min(max(n_rope - lo, 0), cin)
            if k:
                tab = [jnp.concatenate([ins[t][...]] * (k // 128), axis=1) for t in (2, 3, 4)]
                outs[0][:, lo:lo + k] = _rope_fwd(v[:, :k], *tab)
            if k < cin:
                outs[0][:, lo + k:lo + cin] = v[:, k:]

    (proj,) = _mm(
        "mix_in", [h, win, *tabs], [_rows(tm, D_MODEL), _wfull(D_MODEL, cin, l)] + [_rows(tm, 128)] * 3,
        [(0, _pick(1, c), c) for c in range(N_CHIPS)], N_CHIPS, None, NN, (T // tm, 1, 1), roped,
        [jax.ShapeDtypeStruct((T, N_CHIPS * cin), F32)], [_rows(tm, N_CHIPS * cin)])

    os_, lses = [], []
    for g, (window, dil) in enumerate(DIL_GROUPS):
        o_g, lse_g = _dil_fwd(proj, g, dil)
        os_.append(o_g)
        lses.append(lse_g)
    o_dil, lse = _dil_merge(os_, lses)
    o_sb = _sb_fwd(proj)

    def gated(vals, ins, outs, i):
        pd, ps = vals
        outs[0][...] = (_sigmoid(ins[4][...]) * pd + _sigmoid(ins[5][...]) * ps).astype(BF16)
        outs[1][...] = pd.astype(BF16)
        outs[2][...] = ps.astype(BF16)

    gd0, gs0 = COL_GD // cp, COL_GS // cp
    ush = jax.ShapeDtypeStruct((T, D_MODEL), BF16)
    u, pd, ps = _mm(
        "mix_gate", [o_dil, o_sb, wpd, wps, proj, proj],
        [_tok(D_ATT), _tok(D_ATT), _wspec(D_ATT, cp, l, 1), _wspec(D_ATT, cp, l, 1),
         pl.BlockSpec((TM, cp), lambda i, j, k: (i, gd0 + j)), pl.BlockSpec((TM, cp), lambda i, j, k: (i, gs0 + j))],
        [(0, 2, 0), (1, 3, 1)], 2, None, NN, (T // TM, N_CHIPS, 1), gated, [ush] * 3, [_tok(cp, 1)] * 3)

    def resid(vals, ins, outs, i):
        outs[0][...] = ins[2][...] + vals[0]

    (y,) = _mm(
        "mix_out", [u, wo, x], [_tok(D_MODEL), _wfull(cp, D_MODEL, l), _tok(D_MODEL)],
        [(_cols(0, c, cp), _pick(1, c), 0) for c in range(N_CHIPS)], 1, None, NN, (T // TM, 1, 1), resid,
        [jax.ShapeDtypeStruct((T, D_MODEL), F32)], [_tok(D_MODEL)])
    return y, (x, h, proj, o_dil, lse, o_sb, u, pd, ps)


def _mixer_bwd(dxo, gain, W, l, tabs, saved):
    x, h, proj, o_dil, lse, o_sb, u, pd, ps = saved
    T = x.shape[0]
    win, wpd, wps, wo = W["w_in"], W["w_proj_dil"], W["w_proj_sb"], W["w_out"]
    cin = win.shape[3]
    cp = wpd.shape[3]
    tk = TM
    tm = TM // 2
    row = pl.BlockSpec((tm, D_MODEL), lambda i, j, k: (i, 0))
    gd0, gs0 = COL_GD // cp, COL_GS // cp

    def dgated(vals, ins, outs, i):
        du = vals[0]
        sd = _sigmoid(ins[4][...])
        ss = _sigmoid(ins[5][...])
        outs[0][...] = (du * sd).astype(BF16)
        outs[1][...] = (du * ss).astype(BF16)
        outs[2][...] = (du * ins[2][...].astype(F32) * sd * (1.0 - sd)).astype(BF16)
        outs[3][...] = (du * ins[3][...].astype(F32) * ss * (1.0 - ss)).astype(BF16)

    ush = jax.ShapeDtypeStruct((T, D_MODEL), BF16)
    dpd, dps, dgd, dgs = _mm(
        "mix_du", [dxo, wo, pd, ps, proj, proj],
        [_tok(D_MODEL), _wspec(cp, D_MODEL, l, 1), _tok(cp, 1), _tok(cp, 1),
         pl.BlockSpec((TM, cp), lambda i, j, k: (i, gd0 + j)), pl.BlockSpec((TM, cp), lambda i, j, k: (i, gs0 + j))],
        [(0, 1, 0)], 1, None, NT, (T // TM, N_CHIPS, 1), dgated, [ush] * 4, [_tok(cp, 1)] * 4)

    def one(vals, ins, outs, i):
        outs[0][...] = vals[0].astype(BF16)

    def two(vals, ins, outs, i):
        outs[0][...] = vals[0].astype(BF16)
        outs[1][...] = vals[1].astype(BF16)

    (dwo,) = _mm(
        "mix_dwo", [u, dxo],
        [pl.BlockSpec((tk, cp), lambda i, j, k: (k, j)), pl.BlockSpec((tk, D_MODEL), lambda i, j, k: (k, 0))],
        [(0, 1, 0)], 1, (cp, D_MODEL), TN, (1, N_CHIPS, T // tk), one,
        [jax.ShapeDtypeStruct((N_CHIPS, cp, D_MODEL), BF16)],
        [pl.BlockSpec((None, cp, D_MODEL), lambda i, j, k: (j, 0, 0))])

    def plain2(vals, ins, outs, i):
        outs[0][...] = vals[0]
        outs[1][...] = vals[1]

    ash = jax.ShapeDtypeStruct((T, D_ATT), F32)
    do_dil, do_sb = _mm(
        "mix_do", [dpd, dps, wpd, wps], [_tok(cp, 2), _tok(cp, 2), _wspec(D_ATT, cp, l, 2), _wspec(D_ATT, cp, l, 2)],
        [(0, 2, 0), (1, 3, 1)], 2, (TM, D_ATT), NT, (T // TM, 1, N_CHIPS), plain2, [ash, ash], [_tok(D_ATT)] * 2)

    psh = jax.ShapeDtypeStruct((N_CHIPS, D_ATT, cp), BF16)
    pspec = pl.BlockSpec((None, D_ATT, cp), lambda i, j, k: (j, 0, 0))
    arow = pl.BlockSpec((tk, D_ATT), lambda i, j, k: (k, 0))
    dcol = pl.BlockSpec((tk, cp), lambda i, j, k: (k, j))
    dwpd, dwps = _mm(
        "mix_dwp", [o_dil, o_sb, dpd, dps], [arow, arow, dcol, dcol], [(0, 2, 0), (1, 3, 1)], 2, (D_ATT, cp), TN,
        (1, N_CHIPS, T // tk), two, [psh, psh], [pspec, pspec])

    dqs, dks, dvs = [], [], []
    for g, (window, dil) in enumerate(DIL_GROUPS):
        dq, dk, dv = _dil_bwd(proj, do_dil, o_dil, lse, g, dil)
        dqs.append(dq)
        dks.append(dk)
        dvs.append(dv)
    dq_s, dk_s, dv_s = _sb_bwd(proj, do_sb, o_sb)
    dproj = _assemble_dproj(dqs + dks, dvs + [dq_s, dk_s, dv_s], [dgd, dgs], tabs)

    dx, dgain = _mm(
        "mix_dx", [dproj, win, x, gain, dxo],
        [pl.BlockSpec((tm, N_CHIPS * cin), lambda i, j, k: (i, 0)), _wfull(D_MODEL, cin, l), row, _gain_spec(), row],
        [(_cols(0, c, cin), _pick(1, c), 0) for c in range(N_CHIPS)], 1, None, NT, (T // tm, 1, 1),
        _rms_bwd_epilogue(2, 3, 4),
        [jax.ShapeDtypeStruct((T, D_MODEL), F32), jax.ShapeDtypeStruct((8, D_MODEL), F32)],
        [row, pl.BlockSpec((8, D_MODEL), lambda i, j, k: (0, 0))])

    (dwin,) = _mm(
        "mix_dwin", [h, dproj],
        [pl.BlockSpec((tk, D_MODEL), lambda i, j, k: (k, 0)), pl.BlockSpec((tk, cin), lambda i, j, k: (k, j))],
        [(0, 1, 0)], 1, (D_MODEL, cin), TN, (1, N_CHIPS, T // tk), one,
        [jax.ShapeDtypeStruct((N_CHIPS, D_MODEL, cin), BF16)],
        [pl.BlockSpec((None, D_MODEL, cin), lambda i, j, k: (j, 0, 0))])
    return dx, dgain, dwin, dwpd, dwps, dwo


def _local_step(x, target, norms, norm_final, weights_of, on_grads):
    T = x.shape[0]
    tabs = _rope_tables(T)
    saved, held = [], []
    for l in range(DEPTH):
        w1 = weights_of(l, 0, x)
        x, s1 = _ffn_fwd(x, norms["norm_ffn1"][l:l + 1], w1["ffn1_w_gate"], w1["ffn1_w_up"], w1["ffn1_w_down"])
        w2 = weights_of(l, 1, x)
        x, s2 = _mixer_fwd(x, norms["norm_mix"][l:l + 1], w2, 0, tabs)
        w3 = weights_of(l, 2, x)
        x, s3 = _ffn_fwd(x, norms["norm_ffn2"][l:l + 1], w3["ffn2_w_gate"], w3["ffn2_w_up"], w3["ffn2_w_down"])
        saved.append((s1, s2, s3))
        held.append((w1, w2, w3))
    dx, dg_final, loss = _final_loss(x, norm_final.reshape(1, D_MODEL), target)
    gains = [None] * DEPTH
    for l in reversed(range(DEPTH)):
        s1, s2, s3 = saved[l]
        w1, w2, w3 = held[l]
        dx, dg2, dwg2, dwu2, dwd2 = _ffn_bwd(dx, norms["norm_ffn2"][l:l + 1], w3["ffn2_w_gate"], w3["ffn2_w_up"],
                                             w3["ffn2_w_down"], s3)
        dx = on_grads(l, 2, dict(ffn2_w_gate=dwg2, ffn2_w_up=dwu2, ffn2_w_down=dwd2), dx)
        dx, dgm, dwin, dwpd, dwps, dwo = _mixer_bwd(dx, norms["norm_mix"][l:l + 1], w2, 0, tabs, s2)
        dx = on_grads(l, 1, dict(w_in=dwin, w_proj_dil=dwpd, w_proj_sb=dwps, w_out=dwo), dx)
        dx, dg1, dwg1, dwu1, dwd1 = _ffn_bwd(dx, norms["norm_ffn1"][l:l + 1], w1["ffn1_w_gate"], w1["ffn1_w_up"],
                                             w1["ffn1_w_down"], s1)
        dx = on_grads(l, 0, dict(ffn1_w_gate=dwg1, ffn1_w_up=dwu1, ffn1_w_down=dwd1), dx)
        gains[l] = dict(norm_ffn1=dg1, norm_mix=dgm, norm_ffn2=dg2)
    return loss, dx, gains, dg_final


def _place():
    x, y, c = lax.axis_index("x"), lax.axis_index("y"), lax.axis_index("c")
    chips = [(1 - x, y), (x, 1 - y), (1 - x, 1 - y)]
    return x, y, c, chips


def _half(c, r):
    return pl.ds(pl.multiple_of(c * (r // 2), 8), r // 2)


def _cast_into_slot(ws, ls, me_arr, after):
    n = len(ws)
    late = [] if after is None else [after]

    def body(me_ref, *refs):
        for a in range(n):
            refs[len(refs) - n + a][...] = refs[a][...].astype(BF16)

    def src(w, l):
        return pl.BlockSpec((None, w.shape[1] // 4, w.shape[2]), lambda i, me: (l, i, 0))

    def dst(w):
        return pl.BlockSpec((None, None, w.shape[1] // 4, w.shape[2]), lambda i, me: (me[0], 0, i, 0))

    return pl.pallas_call(
        body, name="cast_weights",
        grid_spec=pltpu.PrefetchScalarGridSpec(
            num_scalar_prefetch=1, grid=(4,),
            in_specs=[src(w, l) for w, l in zip(ws, ls)] + [pl.BlockSpec(memory_space=pl.ANY)] * len(late),
            out_specs=[dst(w) for w in ws]),
        out_shape=[jax.ShapeDtypeStruct((N_CHIPS, 1) + w.shape[1:], BF16) for w in ws], compiler_params=_params(),
    )(me_arr, *ws, *late)


HBM_SPEC = pl.BlockSpec(memory_space=pltpu.HBM)
SEM_SPEC = pl.BlockSpec(memory_space=pltpu.SEMAPHORE)
SPLIT_COPY = pltpu.CompilerParams(has_side_effects=pltpu.SideEffectType.DATAFLOW_SIDE_EFFECTING)


def _gather_piece(ref, chip_id, c):
    return ref.at[chip_id, 0, _half(c, ref.shape[2]), :]


def _gather_start(tag, bufs):
    n = len(bufs)

    def body(*refs):
        out_refs = refs[n:2 * n]
        send_sems, recv_sems, token = refs[2 * n:]
        x, y, c, chips = _place()
        me = 2 * x + y
        for a in range(n):
            piece = _gather_piece(out_refs[a], me, c)
            for j, chip in enumerate(chips):
                pltpu.make_async_remote_copy(
                    src_ref=piece, dst_ref=piece, send_sem=send_sems.at[3 * a + j], recv_sem=recv_sems.at[3 * a + j],
                    device_id=(*chip, c), device_id_type=MESH).start()
        token[...] = jnp.zeros_like(token)

    outs = pl.pallas_call(
        body, name=f"gather_start_{tag}", in_specs=[HBM_SPEC] * n,
        out_specs=[HBM_SPEC] * n + [SEM_SPEC, SEM_SPEC, pl.BlockSpec(memory_space=pltpu.VMEM)],
        out_shape=[pltpu.HBM(b.shape, b.dtype) for b in bufs] + [pltpu.SemaphoreType.DMA((3 * n,))] * 2
        + [jax.ShapeDtypeStruct((8, 128), F32)],
        input_output_aliases={a: a for a in range(n)}, compiler_params=SPLIT_COPY,
    )(*[pltpu.with_memory_space_constraint(b, pltpu.HBM) for b in bufs])
    return outs[:n], outs[n], outs[n + 1], outs[n + 2]


def _gather_wait(k, bufs, places, send_sems, recv_sems, after):
    m = len(bufs)

    def body(*refs):
        in_refs = refs[:m]
        ssem, rsem = refs[m], refs[m + 1]
        x, y, c, chips = _place()
        me = 2 * x + y
        for t, a in enumerate(places):
            for j, chip in enumerate(chips):
                cp = pltpu.make_async_remote_copy(
                    src_ref=_gather_piece(in_refs[t], me, c),
                    dst_ref=_gather_piece(in_refs[t], 2 * chip[0] + chip[1], c),
                    send_sem=ssem.at[3 * a + j], recv_sem=rsem.at[3 * a + j], device_id=(*chip, c),
                    device_id_type=MESH)
                cp.wait_send()
                cp.wait_recv()

    return pl.pallas_call(
        body, name=f"gather_wait_{k}",
        in_specs=[HBM_SPEC] * m + [SEM_SPEC, SEM_SPEC, pl.BlockSpec(memory_space=pl.ANY)], out_specs=[HBM_SPEC] * m,
        out_shape=[pltpu.HBM(b.shape, b.dtype) for b in bufs], input_output_aliases={t: t for t in range(m)},
        compiler_params=SPLIT_COPY,
    )(*bufs, send_sems, recv_sems, after)


def _gather_relay(bufs):
    n = len(bufs)

    def body(*refs):
        out_refs = refs[n:2 * n]
        send_sems, recv_sems = refs[2 * n:]
        x, y, c, chips = _place()
        cps = []
        for a in range(n):
            for j, chip in enumerate(chips):
                piece = _gather_piece(out_refs[a], 2 * chip[0] + chip[1], c)
                cps.append(pltpu.make_async_remote_copy(
                    src_ref=piece, dst_ref=piece, send_sem=send_sems.at[a, j], recv_sem=recv_sems.at[a, j],
                    device_id=(x, y, 1 - c), device_id_type=MESH))
        for cp in cps:
            cp.start()
        for a in range(n):
            for j, chip in enumerate(chips):
                theirs = _gather_piece(out_refs[a], 2 * chip[0] + chip[1], 1 - c)
                pltpu.make_async_remote_copy(
                    src_ref=theirs, dst_ref=theirs, send_sem=send_sems.at[a, j], recv_sem=recv_sems.at[a, j],
                    device_id=(x, y, 1 - c), device_id_type=MESH).wait_recv()
        for cp in cps:
            cp.wait_send()

    any_spec = pl.BlockSpec(memory_space=pl.ANY)
    return pl.pallas_call(
        body, name="gather_relay", in_specs=[any_spec] * n, out_specs=[any_spec] * n,
        out_shape=[jax.ShapeDtypeStruct(b.shape, b.dtype) for b in bufs],
        input_output_aliases={a: a for a in range(n)},
        scratch_shapes=[pltpu.SemaphoreType.DMA((n, 3))] * 2,
    )(*bufs)


def _exchange_halves(gs):
    n = len(gs)

    def body(*refs):
        g_refs, out_refs = refs[:n], refs[n:2 * n]
        send_sems, recv_sems = refs[2 * n:]
        x, y, c, _ = _place()
        cps = []
        for a in range(n):
            r = g_refs[a].shape[1]
            cps.append(pltpu.make_async_remote_copy(
                src_ref=g_refs[a].at[:, _half(1 - c, r), :], dst_ref=out_refs[a],
                send_sem=send_sems.at[a], recv_sem=recv_sems.at[a], device_id=(x, y, 1 - c), device_id_type=MESH))
        for cp in cps:
            cp.start()
        for cp in cps:
            cp.wait()

    any_spec = pl.BlockSpec(memory_space=pl.ANY)
    return pl.pallas_call(
        body, name="grad_to_sibling", in_specs=[any_spec] * n, out_specs=[any_spec] * n,
        out_shape=[jax.ShapeDtypeStruct((g.shape[0], g.shape[1] // 2, g.shape[2]), g.dtype) for g in gs],
        scratch_shapes=[pltpu.SemaphoreType.DMA((n,))] * 2,
    )(*gs)


def _add_half(gs, gots, c_arr):
    n = len(gs)

    def body(c_ref, *refs):
        for a in range(n):
            refs[2 * n + a][...] = (refs[a][...].astype(F32) + refs[n + a][...].astype(F32)).astype(BF16)

    def own(g):
        return pl.BlockSpec((None, g.shape[1] // 2, g.shape[2]), lambda k, cr: (k, cr[0], 0))

    def half(g):
        return pl.BlockSpec((None, g.shape[1] // 2, g.shape[2]), lambda k, cr: (k, 0, 0))

    return pl.pallas_call(
        body, name="grad_add_half",
        grid_spec=pltpu.PrefetchScalarGridSpec(
            num_scalar_prefetch=1, grid=(N_CHIPS,),
            in_specs=[own(g) for g in gs] + [half(g) for g in gs], out_specs=[half(g) for g in gs]),
        out_shape=[jax.ShapeDtypeStruct(got.shape, BF16) for got in gots], compiler_params=_params(),
    )(c_arr, *gs, *gots)


def _scatter_start(k, ss, thru):
    n = len(ss)

    def body(*refs):
        s_refs, land_refs = refs[2 * n + 1:3 * n + 1], refs[3 * n + 1:4 * n + 1]
        send_sems, recv_sems = refs[4 * n + 2:]
        x, y, c, chips = _place()
        me = 2 * x + y
        for a in range(n):
            for j, chip in enumerate(chips):
                pltpu.make_async_remote_copy(
                    src_ref=s_refs[a].at[2 * chip[0] + chip[1]], dst_ref=land_refs[a].at[me],
                    send_sem=send_sems.at[3 * a + j], recv_sem=recv_sems.at[3 * a + j], device_id=(*chip, c),
                    device_id_type=MESH).start()

    lands = [lax.empty(s.shape, s.dtype) for s in ss]
    hbm = [pltpu.HBM(s.shape, s.dtype) for s in ss]
    outs = pl.pallas_call(
        body, name=f"grad_scatter_start_{k}", in_specs=[HBM_SPEC] * (2 * n + 1),
        out_specs=[HBM_SPEC] * (2 * n + 1) + [SEM_SPEC, SEM_SPEC],
        out_shape=hbm + hbm + [pltpu.HBM(thru.shape, thru.dtype)] + [pltpu.SemaphoreType.DMA((3 * n,))] * 2,
        input_output_aliases={a: a for a in range(2 * n + 1)}, compiler_params=SPLIT_COPY,
    )(*[pltpu.with_memory_space_constraint(v, pltpu.HBM) for v in list(ss) + lands + [thru]])
    return (outs[:n], outs[n:2 * n], outs[2 * n + 1], outs[2 * n + 2]), outs[2 * n]


def _scatter_wait(k, ss, lands, send_sems, recv_sems, after):
    n = len(ss)

    def body(*refs):
        s_refs, land_refs = refs[:n], refs[n:2 * n]
        ssem, rsem = refs[2 * n], refs[2 * n + 1]
        x, y, c, chips = _place()
        me = 2 * x + y
        for a in range(n):
            for j, chip in enumerate(chips):
                cid = 2 * chip[0] + chip[1]
                cp = pltpu.make_async_remote_copy(
                    src_ref=s_refs[a].at[cid], dst_ref=land_refs[a].at[cid], send_sem=ssem.at[3 * a + j],
                    recv_sem=rsem.at[3 * a + j], device_id=(*chip, c), device_id_type=MESH)
                cp.wait_send()
                cp.wait_recv()

    hbm = [pltpu.HBM(s.shape, s.dtype) for s in ss]
    outs = pl.pallas_call(
        body, name=f"grad_scatter_wait_{k}",
        in_specs=[HBM_SPEC] * (2 * n) + [SEM_SPEC, SEM_SPEC, pl.BlockSpec(memory_space=pl.ANY)],
        out_specs=[HBM_SPEC] * (2 * n), out_shape=hbm + hbm,
        input_output_aliases={a: a for a in range(2 * n)}, compiler_params=SPLIT_COPY,
    )(*ss, *lands, send_sems, recv_sems, after)
    return outs[:n], outs[n:]


def _sum_chips(lands, ss, me_arr):
    n = len(lands)

    def body(me_ref, *refs):
        for own in range(N_CHIPS):
            @pl.when(me_ref[0] == own)
            def _(own=own):
                for a in range(n):
                    acc = None
                    for k in range(N_CHIPS):
                        term = (refs[n + a][...] if k == own else refs[a][k]).astype(F32)
                        acc = term if acc is None else acc + term
                    refs[2 * n + a][...] = acc

    return pl.pallas_call(
        body, name="grad_sum_chips",
        grid_spec=pltpu.PrefetchScalarGridSpec(
            num_scalar_prefetch=1, grid=(1,),
            in_specs=[pl.BlockSpec(la.shape, lambda i, me: (0, 0, 0)) for la in lands]
            + [pl.BlockSpec((None,) + la.shape[1:], lambda i, me: (me[0], 0, 0)) for la in lands],
            out_specs=[pl.BlockSpec(la.shape[1:], lambda i, me: (0, 0)) for la in lands]),
        out_shape=[jax.ShapeDtypeStruct(la.shape[1:], F32) for la in lands], compiler_params=_params(),
    )(me_arr, *lands, *ss)


def _swap_halves(fs):
    n = len(fs)

    def body(*refs):
        f_refs, out_refs = refs[:n], refs[n:2 * n]
        send_sems, recv_sems = refs[2 * n:]
        x, y, c, _ = _place()
        cps = [pltpu.make_async_remote_copy(
            src_ref=f_refs[a], dst_ref=out_refs[a], send_sem=send_sems.at[a], recv_sem=recv_sems.at[a],
            device_id=(x, y, 1 - c), device_id_type=MESH) for a in range(n)]
        for cp in cps:
            cp.start()
        for cp in cps:
            cp.wait()

    any_spec = pl.BlockSpec(memory_space=pl.ANY)
    return pl.pallas_call(
        body, name="grad_swap_halves", in_specs=[any_spec] * n, out_specs=[any_spec] * n,
        out_shape=[jax.ShapeDtypeStruct(f.shape, f.dtype) for f in fs],
        scratch_shapes=[pltpu.SemaphoreType.DMA((n,))] * 2,
    )(*fs)


def _allreduce_rows(stats):
    def body(s_ref, o_ref, buf, send_sems, recv_sems):
        x, y, c, _ = _place()
        me = 4 * x + 2 * y + c
        buf[me] = s_ref[...]
        cps = []
        for k in range(1, 8):
            px = jnp.where(k & 4, 1 - x, x)
            py = jnp.where(k & 2, 1 - y, y)
            pc = jnp.where(k & 1, 1 - c, c)
            cps.append(pltpu.make_async_remote_copy(
                src_ref=s_ref, dst_ref=buf.at[me], send_sem=send_sems.at[k - 1], recv_sem=recv_sems.at[k - 1],
                device_id=(px, py, pc), device_id_type=MESH))
        for cp in cps:
            cp.start()
        for cp in cps:
            cp.wait()
        acc = buf[0]
        for d in range(1, 8):
            acc = acc + buf[d]
        o_ref[...] = acc

    vm = pl.BlockSpec(memory_space=pltpu.VMEM)
    return pl.pallas_call(
        body, name="allreduce_rows", in_specs=[vm], out_specs=vm,
        out_shape=jax.ShapeDtypeStruct(stats.shape, F32),
        scratch_shapes=[pltpu.VMEM((8,) + stats.shape, F32), pltpu.SemaphoreType.DMA((7,)),
                        pltpu.SemaphoreType.DMA((7,))],
    )(stats)


def _adamw_math(w, g, m, v):
    m = ADAM_B1 * m + (1.0 - ADAM_B1) * g
    v = ADAM_B2 * v + (1.0 - ADAM_B2) * (g * g)
    m_hat = m / (1.0 - ADAM_B1 ** ADAM_STEP)
    v_hat = v / (1.0 - ADAM_B2 ** ADAM_STEP)
    delta = -ADAM_LR * (m_hat / (jnp.sqrt(v_hat) + ADAM_EPS) + ADAM_WD * w)
    return delta, m, v


def _adamw(ws, ms, vs, mines, theirs, l, c_arr, earlier):
    n = len(ws)
    held = [t for e in earlier if e is not None for t in e]
    assert len(held) in (0, 4 * n)

    def body(c_ref, *refs):
        outs = refs[len(refs) - 4 * n:]
        for a in range(n):
            w_ref, m_ref, v_ref, a_ref, b_ref = refs[5 * a:5 * a + 5]
            g = jnp.where(pl.program_id(0) == c_ref[0], a_ref[...], b_ref[...])
            delta, mn, vn = _adamw_math(w_ref[...], g, m_ref[...], v_ref[...])
            outs[4 * a][...] = g
            outs[4 * a + 1][...] = delta
            outs[4 * a + 2][...] = mn
            outs[4 * a + 3][...] = vn

    def blk(w):
        tr = w.shape[1] // 4
        return pl.BlockSpec((None, tr, w.shape[2]), lambda hh, i, cr: (l, 2 * hh + i, 0))

    def half(w):
        return pl.BlockSpec((w.shape[1] // 4, w.shape[2]), lambda hh, i, cr: (i, 0))

    outs = pl.pallas_call(
        body, name="adamw",
        grid_spec=pltpu.PrefetchScalarGridSpec(
            num_scalar_prefetch=1, grid=(2, 2),
            in_specs=[sp for w in ws for sp in (blk(w), blk(w), blk(w), half(w), half(w))]
            + [pl.BlockSpec(memory_space=pl.ANY)] * len(held),
            out_specs=[blk(w) for w in ws for _ in range(4)]),
        out_shape=[jax.ShapeDtypeStruct(w.shape, F32) for w in ws for _ in range(4)],
        input_output_aliases={1 + 5 * n + t: t for t in range(len(held))}, compiler_params=_params(),
    )(c_arr, *[t for grp in zip(ws, ms, vs, mines, theirs) for t in grp], *held)
    return [outs[4 * a:4 * a + 4] for a in range(n)]


def _adamw_rows(w, m, v, g):
    def body(w_ref, m_ref, v_ref, g_ref, d_ref, mo_ref, vo_ref):
        delta, mn, vn = _adamw_math(w_ref[...], g_ref[...], m_ref[...], v_ref[...])
        d_ref[...] = delta
        mo_ref[...] = mn
        vo_ref[...] = vn

    vm = pl.BlockSpec(memory_space=pltpu.VMEM)
    sh = jax.ShapeDtypeStruct(w.shape, F32)
    return pl.pallas_call(body, name="adamw_rows", in_specs=[vm] * 4, out_specs=[vm] * 3, out_shape=[sh] * 3)(w, m, v, g)


SUBLAYERS = (("ffn1_w_gate", "ffn1_w_up", "ffn1_w_down"), ("w_in", "w_proj_dil", "w_proj_sb", "w_out"),
             ("ffn2_w_gate", "ffn2_w_up", "ffn2_w_down"))
TRANSPOSED = ("ffn1_w_gate", "ffn1_w_up", "ffn2_w_gate", "ffn2_w_up")
LAG = 2


def _pick_row(blocks):
    row = lax.broadcasted_iota(jnp.int32, (8, D_MODEL), 0)
    out = jnp.zeros((8, D_MODEL), F32)
    for i, b in enumerate(blocks):
        out = out + jnp.where(row == i, b, 0.0)
    return out


def kernel(x, norm_ffn1, ffn1_w_gate, ffn1_w_up, ffn1_w_down, norm_mix, w_in, w_proj_dil, w_proj_sb, w_out, norm_ffn2, ffn2_w_gate, ffn2_w_up, ffn2_w_down, norm_final, loss_target, m_norm_ffn1, m_ffn1_w_gate, m_ffn1_w_up, m_ffn1_w_down, m_norm_mix, m_w_in, m_w_proj_dil, m_w_proj_sb, m_w_out, m_norm_ffn2, m_ffn2_w_gate, m_ffn2_w_up, m_ffn2_w_down, m_norm_final, v_norm_ffn1, v_ffn1_w_gate, v_ffn1_w_up, v_ffn1_w_down, v_norm_mix, v_w_in, v_w_proj_dil, v_w_proj_sb, v_w_out, v_norm_ffn2, v_ffn2_w_gate, v_ffn2_w_up, v_ffn2_w_down, v_norm_final):
    given = dict(locals())
    for n in TRANSPOSED:
        for k in ("", "m_", "v_"):
            given[k + n] = jnp.swapaxes(given[k + n], 1, 2)
    weights = {n: given[n] for n in WEIGHT_NAMES}
    norms = {n: given[n] for n in NORM_NAMES}

    c_arr = lax.axis_index("c").astype(jnp.int32).reshape(1)
    me_arr = (2 * lax.axis_index("x") + lax.axis_index("y")).astype(jnp.int32).reshape(1)
    order = [(l, s, n) for l in range(DEPTH) for s in range(len(SUBLAYERS)) for n in SUBLAYERS[s]]
    n_first = len(SUBLAYERS[0])
    sent, token = {}, None
    for tag, idxs in (("a", range(n_first)), ("b", range(n_first, len(order)))):
        cast = _cast_into_slot([weights[order[i][2]] for i in idxs], [order[i][0] for i in idxs], me_arr, token)
        bufs, send_sems, recv_sems, token = _gather_start(tag, cast)
        for p, i in enumerate(idxs):
            sent[i] = (bufs[p], p, send_sems, recv_sems)

    def weights_of(l, s, after):
        idxs = [i for i, (ll, ss, _) in enumerate(order) if (ll, ss) == (l, s)]
        got = _gather_wait(len(SUBLAYERS) * l + s, [sent[i][0] for i in idxs], [sent[i][1] for i in idxs],
                           sent[idxs[0]][2], sent[idxs[0]][3], after)
        return {order[i][2]: g for i, g in zip(idxs, _gather_relay(got))}

    out = {}
    in_flight = []

    def finish(l, s, names, sums, lands, ssem, rsem, after):
        sums, lands = _scatter_wait(len(SUBLAYERS) * l + s, sums, lands, ssem, rsem, after)
        mine = _sum_chips(lands, sums, me_arr)
        theirs = _swap_halves(mine)
        res = _adamw([weights[n] for n in names], [given["m_" + n] for n in names], [given["v_" + n] for n in names],
                     mine, theirs, l, c_arr, [out.get(n) for n in names])
        out.update(zip(names, res))

    def on_grads(l, s, grads, after):
        names = list(grads)
        gs = [grads[n] for n in names]
        sums = _add_half(gs, _exchange_halves(gs), c_arr)
        sent, after = _scatter_start(len(SUBLAYERS) * l + s, sums, after)
        in_flight.append((l, s, names) + sent)
        if len(in_flight) > LAG:
            finish(*in_flight.pop(0), after)
        return after

    loss_blk, grad_x, gains, dg_final = _local_step(x[0], loss_target[0], norms, norm_final, weights_of, on_grads)
    while in_flight:
        finish(*in_flight.pop(0), grad_x)
    out = {k + n: (jnp.swapaxes(v, 1, 2) if n in TRANSPOSED else v)
           for n, res in out.items() for k, v in zip(("grad_", "delta_", "new_m_", "new_v_"), res)}
    out["grad_x"] = grad_x[None]

    rows = [gains[l][n] for n in NORM_NAMES for l in range(DEPTH)] + [dg_final, loss_blk]
    total = _allreduce_rows(_pick_row(rows))
    out["loss"] = total[7, 0]
    wn = jnp.concatenate([given[n] for n in NORM_NAMES] + [norm_final[None], jnp.zeros((1, D_MODEL), F32)])
    mn_ = jnp.concatenate([given["m_" + n] for n in NORM_NAMES] + [m_norm_final[None], jnp.zeros((1, D_MODEL), F32)])
    vn_ = jnp.concatenate([given["v_" + n] for n in NORM_NAMES] + [v_norm_final[None], jnp.ones((1, D_MODEL), F32)])
    d_n, m_n, v_n = _adamw_rows(wn, mn_, vn_, total)
    for i, n in enumerate(NORM_NAMES):
        sl = slice(i * DEPTH, (i + 1) * DEPTH)
        out["grad_" + n], out["delta_" + n], out["new_m_" + n], out["new_v_" + n] = total[sl], d_n[sl], m_n[sl], v_n[sl]
    out["grad_norm_final"], out["delta_norm_final"] = total[6], d_n[6]
    out["new_m_norm_final"], out["new_v_norm_final"] = m_n[6], v_n[6]

    names = ["norm_ffn1", "ffn1_w_gate", "ffn1_w_up", "ffn1_w_down", "norm_mix", "w_in", "w_proj_dil", "w_proj_sb",
             "w_out", "norm_ffn2", "ffn2_w_gate", "ffn2_w_up", "ffn2_w_down", "norm_final"]
    return (out["loss"], out["grad_x"], *[out["grad_" + n] for n in names], *[out["delta_" + n] for n in names],
            *[out["new_m_" + n] for n in names], *[out["new_v_" + n] for n in names])
```

```python
import functools

import jax
import jax.numpy as jnp
from jax import lax
from jax.experimental import pallas as pl
from jax.experimental.pallas import tpu as pltpu

F32 = jnp.float32
BF16 = jnp.bfloat16

D_MODEL = 1024
DEPTH = 2
N_CHIPS = 4
HEAD_DIM = 64
ROPE_DIM = 16
ROPE_THETA = 500000.0
DIL_GROUPS = ((128, 1), (512, 4), (2048, 16))
SPAN = 128
Q_BLOCK = 128
RMS_EPS = 1e-6
D_ATT = 256
COL_QS = 2304
COL_GD = 3072
COL_GS = 4096
ADAM_LR, ADAM_B1, ADAM_B2, ADAM_EPS, ADAM_WD, ADAM_STEP = 0.001, 0.9, 0.999, 1e-08, 0.01, 10

VMEM_LIMIT = 52 * 1024 * 1024
TM = 512
NEG = -1e30

NN = (((1,), (0,)), ((), ()))
NT = (((1,), (1,)), ((), ()))
TN = (((0,), (0,)), ((), ()))
MESH = pl.DeviceIdType.MESH

WEIGHT_NAMES = ("ffn1_w_gate", "ffn1_w_up", "ffn1_w_down", "w_in", "w_proj_dil",
                "w_proj_sb", "w_out", "ffn2_w_gate", "ffn2_w_up", "ffn2_w_down")
NORM_NAMES = ("norm_ffn1", "norm_mix", "norm_ffn2")


def _params(**kw):
    return pltpu.CompilerParams(vmem_limit_bytes=VMEM_LIMIT, **kw)


def _sigmoid(x):
    return 0.5 * jnp.tanh(0.5 * x) + 0.5


def _mm_body(pairs, n_in, n_out, n_acc, dims, nk, i_axis, epilogue, *refs):
    ins = refs[:n_in]
    outs = refs[n_in:n_in + n_out]
    accs = refs[n_in + n_out:]
    i = pl.program_id(i_axis)
    k = pl.program_id(2)

    def operand(a):
        return (a(ins) if callable(a) else ins[a][...]).astype(BF16)

    def dot(ia, ib):
        return lax.dot_general(operand(ia), operand(ib), dims, preferred_element_type=F32)

    if nk == 1:
        parts = [None] * n_acc
        for ia, ib, ic in pairs:
            parts[ic] = dot(ia, ib) if parts[ic] is None else parts[ic] + dot(ia, ib)
        epilogue(parts, ins, outs, i)
        return

    @pl.when(k == 0)
    def _():
        for c in range(n_acc):
            accs[c][...] = jnp.zeros_like(accs[c])

    for ia, ib, ic in pairs:
        accs[ic][...] += dot(ia, ib)

    @pl.when(k == nk - 1)
    def _():
        epilogue([a[...] for a in accs], ins, outs, i)


def _j_outer(spec):
    f = spec.index_map
    return pl.BlockSpec(spec.block_shape, lambda j, i, k: f(i, j, k))


def _mm(name, ins, in_specs, pairs, n_acc, acc_shape, dims, grid, epilogue, out_shapes, out_specs, j_outer=False):
    nk = grid[2]
    if j_outer:
        grid = (grid[1], grid[0], grid[2])
        in_specs = [_j_outer(s) for s in in_specs]
        out_specs = [_j_outer(s) for s in out_specs]
    scratch = [pltpu.VMEM(acc_shape, F32) for _ in range(n_acc)] if nk > 1 else []
    body = functools.partial(_mm_body, tuple(pairs), len(ins), len(out_shapes), n_acc, dims, nk,
                             1 if j_outer else 0, epilogue)
    return pl.pallas_call(
        body, name=name, grid=grid, in_specs=in_specs, out_specs=out_specs, out_shape=out_shapes,
        scratch_shapes=scratch,
        compiler_params=_params(dimension_semantics=("arbitrary", "arbitrary", "arbitrary")),
    )(*ins)


def _wspec(r, c, l, by):
    if by == 1:
        return pl.BlockSpec((None, None, r, c), lambda i, j, k: (j, l, 0, 0))
    return pl.BlockSpec((None, None, r, c), lambda i, j, k: (k, l, 0, 0))


def _rms_bwd_epilogue(x_idx, g_idx, dxo_idx):
    def ep(vals, ins, outs, i):
        dh = vals[0]
        x = ins[x_idx][...]
        g = ins[g_idx][...]
        rstd = lax.rsqrt(jnp.mean(x * x, axis=-1, keepdims=True) + RMS_EPS)
        xhat = x * rstd
        dxhat = dh * g
        dx = rstd * (dxhat - xhat * jnp.mean(dxhat * xhat, axis=-1, keepdims=True))
        outs[0][...] = ins[dxo_idx][...] + dx
        dg = jnp.broadcast_to(jnp.sum(dh * xhat, axis=0, keepdims=True), outs[1].shape)

        @pl.when(i == 0)
        def _():
            outs[1][...] = dg

        @pl.when(i > 0)
        def _():
            outs[1][...] += dg
    return ep


def _rms_fwd(x, gain):
    T = x.shape[0]

    def body(x_ref, g_ref, h_ref):
        xv = x_ref[...]
        h = xv * lax.rsqrt(jnp.mean(xv * xv, axis=-1, keepdims=True) + RMS_EPS)
        h_ref[...] = (h * g_ref[...]).astype(BF16)

    return pl.pallas_call(
        body, name="rms_fwd", grid=(T // TM,),
        in_specs=[pl.BlockSpec((TM, D_MODEL), lambda i: (i, 0)), pl.BlockSpec((1, D_MODEL), lambda i: (0, 0))],
        out_specs=pl.BlockSpec((TM, D_MODEL), lambda i: (i, 0)),
        out_shape=jax.ShapeDtypeStruct((T, D_MODEL), BF16), compiler_params=_params(),
    )(x, gain)


def _rope_tables(T):
    pos = jnp.arange(T, dtype=F32)
    inv_freq = ROPE_THETA ** (-jnp.arange(0, ROPE_DIM, 2, dtype=F32) / ROPE_DIM)
    ang = pos[:, None] * inv_freq[None, :]
    cos, sin = jnp.cos(ang), jnp.sin(ang)
    half = ROPE_DIM // 2
    one = jnp.ones((T, HEAD_DIM - ROPE_DIM), F32)
    zero = jnp.zeros((T, HEAD_DIM - ROPE_DIM), F32)
    zh = jnp.zeros((T, half), F32)
    c = jnp.concatenate([cos, cos, one], axis=1)
    s1 = jnp.concatenate([-sin, zh, zero], axis=1)
    s2 = jnp.concatenate([zh, sin, zero], axis=1)
    return tuple(jnp.concatenate([t, t], axis=1) for t in (c, s1, s2))


def _rope_fwd(xv, c, s1, s2):
    w = xv.shape[1]
    half = ROPE_DIM // 2
    return xv * c + pltpu.roll(xv, w - half, 1) * s1 + pltpu.roll(xv, half, 1) * s2


def _rope_bwd(dy, c, s1, s2):
    w = dy.shape[1]
    half = ROPE_DIM // 2
    return dy * c + pltpu.roll(dy * s1, half, 1) + pltpu.roll(dy * s2, w - half, 1)


def _assemble_dproj(dqk, rest, gates, tabs):
    T = gates[0].shape[0]
    n_qk, n_rest = len(dqk), len(rest)
    width = (n_qk + n_rest) * D_ATT + 2 * D_MODEL

    def body(*refs):
        ins, (c_ref, s1_ref, s2_ref), o_ref = refs[:n_qk + n_rest + 2], refs[-4:-1], refs[-1]
        c = jnp.concatenate([c_ref[...]] * 2, axis=1)
        s1 = jnp.concatenate([s1_ref[...]] * 2, axis=1)
        s2 = jnp.concatenate([s2_ref[...]] * 2, axis=1)
        for b in range(n_qk + n_rest):
            v = ins[b][...]
            if b < n_qk:
                v = _rope_bwd(v, c, s1, s2)
            o_ref[:, b * D_ATT:(b + 1) * D_ATT] = v.astype(BF16)
        off = (n_qk + n_rest) * D_ATT
        o_ref[:, off:off + D_MODEL] = ins[-2][...]
        o_ref[:, off + D_MODEL:] = ins[-1][...]

    att = pl.BlockSpec((TM, D_ATT), lambda i: (i, 0))
    wide = pl.BlockSpec((TM, D_MODEL), lambda i: (i, 0))
    tab = pl.BlockSpec((TM, 128), lambda i: (i, 0))
    return pl.pallas_call(
        body, name="assemble_dproj", grid=(T // TM,),
        in_specs=[att] * (n_qk + n_rest) + [wide, wide, tab, tab, tab],
        out_specs=pl.BlockSpec((TM, width), lambda i: (i, 0)),
        out_shape=jax.ShapeDtypeStruct((T, width), BF16), compiler_params=_params(),
    )(*dqk, *rest, *gates, *tabs)


def _dil_merge(os_, lses):
    T = os_[0].shape[0]

    def body(o0, o1, o2, l0, l1, l2, o_ref, lse_ref):
        a, b, c = l0[...], l1[...], l2[...]
        m = jnp.maximum(jnp.maximum(a, b), c)
        ea, eb, ec = jnp.exp(a - m), jnp.exp(b - m), jnp.exp(c - m)
        den = ea + eb + ec
        o_ref[...] = (ea * o0[...] + eb * o1[...] + ec * o2[...]) / den
        lse_ref[...] = m + jnp.log(den)

    blk = pl.BlockSpec((TM, D_ATT), lambda i: (i, 0))
    sh = jax.ShapeDtypeStruct((T, D_ATT), F32)
    return pl.pallas_call(
        body, name="dil_merge", grid=(T // TM,), in_specs=[blk] * 6, out_specs=[blk, blk],
        out_shape=[sh, sh], compiler_params=_params(),
    )(*os_, *lses)


def _final_loss(x, gain, target):
    T = x.shape[0]

    def body(x_ref, g_ref, t_ref, dx_ref, dg_ref, loss_ref):
        xv = x_ref[...]
        g = g_ref[...]
        rstd = lax.rsqrt(jnp.mean(xv * xv, axis=-1, keepdims=True) + RMS_EPS)
        xhat = xv * rstd
        err = xhat * g - t_ref[...]
        loss = 0.5 * jnp.sum(jnp.mean(err * err, axis=-1, keepdims=True), axis=0, keepdims=True)
        dy = err * (1.0 / D_MODEL)
        dxhat = dy * g
        dx_ref[...] = rstd * (dxhat - xhat * jnp.mean(dxhat * xhat, axis=-1, keepdims=True))
        dg = jnp.broadcast_to(jnp.sum(dy * xhat, axis=0, keepdims=True), dg_ref.shape)
        ls = jnp.broadcast_to(loss, loss_ref.shape)

        @pl.when(pl.program_id(0) == 0)
        def _():
            dg_ref[...] = dg
            loss_ref[...] = ls

        @pl.when(pl.program_id(0) > 0)
        def _():
            dg_ref[...] += dg
            loss_ref[...] += ls

    blk = pl.BlockSpec((TM, D_MODEL), lambda i: (i, 0))
    row = pl.BlockSpec((1, D_MODEL), lambda i: (0, 0))
    acc = pl.BlockSpec((8, D_MODEL), lambda i: (0, 0))
    return pl.pallas_call(
        body, name="final_loss", grid=(T // TM,), in_specs=[blk, row, blk], out_specs=[blk, acc, acc],
        out_shape=[jax.ShapeDtypeStruct((T, D_MODEL), F32), jax.ShapeDtypeStruct((8, D_MODEL), F32),
                   jax.ShapeDtypeStruct((8, D_MODEL), F32)],
        compiler_params=_params(dimension_semantics=("arbitrary",)),
    )(x, gain, target)


def _pair_masks():
    lane = lax.broadcasted_iota(jnp.int32, (SPAN, 128), 1)
    return [lane < HEAD_DIM, lane >= HEAD_DIM]


def _stack_heads(x, masks):
    return jnp.concatenate([jnp.where(m, x, 0.0) for m in masks], axis=0)


def _unstack_heads(y, masks):
    rows = y.shape[0] // len(masks)
    out = jnp.where(masks[0], y[:rows], 0.0)
    for h in range(1, len(masks)):
        out = out + jnp.where(masks[h], y[rows * h:rows * (h + 1)], 0.0)
    return out


def _dil_rows(idx, d):
    u = idx // d
    r = idx - u * d
    own = pl.ds(u * (SPAN * d) + r, SPAN, stride=d) if d > 1 else pl.ds(pl.multiple_of(u * SPAN, SPAN), SPAN)
    up = jnp.maximum(u - 1, 0)
    prev = pl.ds(up * (SPAN * d) + r, SPAN, stride=d) if d > 1 else pl.ds(pl.multiple_of(up * SPAN, SPAN), SPAN)
    return u, own, prev


def _dil_valid(u):
    qi = lax.broadcasted_iota(jnp.int32, (2 * SPAN, 2 * SPAN), 0) & (SPAN - 1)
    kj = lax.broadcasted_iota(jnp.int32, (2 * SPAN, 2 * SPAN), 1)
    in_prev = (kj < SPAN) & (kj >= qi + jnp.where(u > 0, 0, SPAN))
    return in_prev | ((kj >= SPAN) & (kj - SPAN <= qi))


def _dil_keys(ref, own, prev):
    return jnp.concatenate([ref[prev, :], ref[own, :]], axis=0).astype(BF16)


def _dil_fwd(proj, g, d):
    T = proj.shape[0]
    n_iter = T // SPAN

    def body(q_ref, k_ref, v_ref, o_ref, lse_ref):
        masks = _pair_masks()

        def step(idx, carry):
            u, own, prev = _dil_rows(idx, d)
            qs = _stack_heads(q_ref[own, :] * (HEAD_DIM ** -0.5), masks).astype(BF16)
            kk = _dil_keys(k_ref, own, prev)
            vv = _dil_keys(v_ref, own, prev)
            s = jnp.where(_dil_valid(u), lax.dot_general(qs, kk, NT, preferred_element_type=F32), NEG)
            m = jnp.max(s, axis=1, keepdims=True)
            p = jnp.exp(s - m)
            den = jnp.sum(p, axis=1, keepdims=True)
            pv = lax.dot_general(p.astype(BF16), vv, NN, preferred_element_type=F32) / den
            o_ref[own, :] = _unstack_heads(pv, masks)
            lse_ref[own, :] = _unstack_heads(jnp.broadcast_to(m + jnp.log(den), pv.shape), masks)
            return carry

        lax.fori_loop(0, n_iter, step, 0, unroll=2)

    def col(b):
        return pl.BlockSpec((T, 128), lambda p: (0, b + p))

    sh = jax.ShapeDtypeStruct((T, D_ATT), F32)
    out = pl.BlockSpec((T, 128), lambda p: (0, p))
    return pl.pallas_call(
        body, name=f"dil_fwd_d{d}", grid=(2,),
        in_specs=[col(2 * g), col(6 + 2 * g), col(12 + 2 * g)], out_specs=[out, out], out_shape=[sh, sh],
        compiler_params=_params(dimension_semantics=("arbitrary",)),
    )(proj, proj, proj)


def _dil_bwd(proj, do, o_dil, lse, g, d):
    T = proj.shape[0]
    n_iter = T // SPAN

    def body(q_ref, k_ref, v_ref, do_ref, o_ref, lse_ref, dq_ref, dk_ref, dv_ref):
        masks = _pair_masks()
        head_lanes = jnp.concatenate(masks, axis=0)

        def step(idx, carry):
            u, own, prev = _dil_rows(idx, d)
            qs = _stack_heads(q_ref[own, :] * (HEAD_DIM ** -0.5), masks).astype(BF16)
            kk = _dil_keys(k_ref, own, prev)
            vv = _dil_keys(v_ref, own, prev)
            dom = _stack_heads(do_ref[own, :], masks)
            dos = dom.astype(BF16)
            delta = jnp.sum(dom * jnp.concatenate([o_ref[own, :]] * 2, axis=0), axis=1, keepdims=True)
            lrow = jnp.max(jnp.where(head_lanes, jnp.concatenate([lse_ref[own, :]] * 2, axis=0), NEG),
                           axis=1, keepdims=True)
            s = lax.dot_general(qs, kk, NT, preferred_element_type=F32)
            p = jnp.where(_dil_valid(u), jnp.exp(s - lrow), 0.0)
            dp = lax.dot_general(dos, vv, NT, preferred_element_type=F32)
            ds = (p * (dp - delta)).astype(BF16)
            dq = lax.dot_general(ds, kk, NN, preferred_element_type=F32)
            dkk = lax.dot_general(ds, qs, TN, preferred_element_type=F32)
            dvv = lax.dot_general(p.astype(BF16), dos, TN, preferred_element_type=F32)
            dq_ref[own, :] = _unstack_heads(dq, masks) * (HEAD_DIM ** -0.5)
            dk_ref[own, :] = dkk[SPAN:]
            dv_ref[own, :] = dvv[SPAN:]
            dk_ref[prev, :] = dk_ref[prev, :] + dkk[:SPAN]
            dv_ref[prev, :] = dv_ref[prev, :] + dvv[:SPAN]
            return carry

        lax.fori_loop(0, n_iter, step, 0, unroll=2)

    def col(b):
        return pl.BlockSpec((T, 128), lambda p: (0, b + p))

    sh = jax.ShapeDtypeStruct((T, D_ATT), F32)
    return pl.pallas_call(
        body, name=f"dil_bwd_d{d}", grid=(2,),
        in_specs=[col(2 * g), col(6 + 2 * g), col(12 + 2 * g), col(0), col(0), col(0)],
        out_specs=[col(0), col(0), col(0)], out_shape=[sh, sh, sh],
        compiler_params=_params(dimension_semantics=("arbitrary",)),
    )(proj, proj, proj, do, o_dil, lse)


SB_KT = 512


def _sb_tri(strict):
    a = lax.broadcasted_iota(jnp.int32, (Q_BLOCK, Q_BLOCK), 0)
    b = lax.broadcasted_iota(jnp.int32, (Q_BLOCK, Q_BLOCK), 1)
    return jnp.where((a > b) if strict else (a >= b), 1.0, 0.0).astype(BF16)


def _suffix(x, c, tri):
    r = x.shape[0]
    nb = x.shape[1] // Q_BLOCK
    blocks = [x[:, Q_BLOCK * b:Q_BLOCK * (b + 1)] for b in range(nb)]
    hi = [b.astype(BF16) for b in blocks]
    lo = [(b - h.astype(F32)).astype(BF16) for b, h in zip(blocks, hi)]
    y = lax.dot_general(jnp.concatenate(hi + lo, axis=0), tri, NN, preferred_element_type=F32)
    outs = [None] * nb
    run = c
    for b in reversed(range(nb)):
        outs[b] = run + y[r * b:r * (b + 1)] + y[r * (nb + b):r * (nb + b + 1)]
        run = run + jnp.sum(blocks[b], axis=1, keepdims=True)
    return jnp.concatenate(outs, axis=1), run


def _sb_tile(qs, kb, past, c, tri):
    z = lax.dot_general(qs, kb, NT, preferred_element_type=F32)
    lsz = jnp.minimum(z, 0.0) - jnp.log(1.0 + jnp.exp(-jnp.abs(z)))
    lk = lsz - z
    if past is not None:
        lk = jnp.where(past, lk, 0.0)
    after, c_new = _suffix(lk, c, tri)
    w = jnp.exp(lsz + after)
    if past is not None:
        w = jnp.where(past, w, 0.0)
    return z, lsz, w, c_new


SB_HEADS = D_ATT // HEAD_DIM
SB_ROWS = SB_HEADS * Q_BLOCK


def _sb_past(i, t):
    row = lax.broadcasted_iota(jnp.int32, (SB_ROWS, SB_KT), 0) & (Q_BLOCK - 1)
    col = lax.broadcasted_iota(jnp.int32, (SB_ROWS, SB_KT), 1)
    return col + t * SB_KT < row + i * Q_BLOCK


def _sb_head_masks():
    lane = lax.broadcasted_iota(jnp.int32, (Q_BLOCK, D_ATT), 1)
    return [(lane >= HEAD_DIM * h) & (lane < HEAD_DIM * (h + 1)) for h in range(SB_HEADS)]


def _sb_rows(t):
    return pl.ds(pl.multiple_of(t * SB_KT, SB_KT), SB_KT)


def _sb_fwd(proj):
    T = proj.shape[0]

    def body(q_ref, k_ref, v_ref, o_ref):
        i = pl.program_id(0)
        masks = _sb_head_masks()
        tri = _sb_tri(True)
        qs = _stack_heads(q_ref[...] * (HEAD_DIM ** -0.5), masks).astype(BF16)
        n_tiles = (i * Q_BLOCK) // SB_KT + 1

        def tile(t, carry, masked):
            kb = k_ref[_sb_rows(t), :].astype(BF16)
            vb = v_ref[_sb_rows(t), :].astype(BF16)
            acc, c = carry
            _, _, w, c = _sb_tile(qs, kb, _sb_past(i, t) if masked else None, c, tri)
            pv = lax.dot_general(w.astype(BF16), vb, NN, preferred_element_type=F32)
            return acc + _unstack_heads(pv, masks), c

        carry = tile(n_tiles - 1, (jnp.zeros((Q_BLOCK, D_ATT), F32), jnp.zeros((SB_ROWS, 1), F32)), True)
        carry = lax.fori_loop(0, n_tiles - 1, lambda tt, cr: tile(n_tiles - 2 - tt, cr, False), carry)
        o_ref[...] = carry[0]

    cb = COL_QS // D_ATT
    return pl.pallas_call(
        body, name="sb_fwd", grid=(T // Q_BLOCK,),
        in_specs=[pl.BlockSpec((Q_BLOCK, D_ATT), lambda i: (i, cb)),
                  pl.BlockSpec((T, D_ATT), lambda i: (0, cb + 1)),
                  pl.BlockSpec((T, D_ATT), lambda i: (0, cb + 2))],
        out_specs=pl.BlockSpec((Q_BLOCK, D_ATT), lambda i: (i, 0)),
        out_shape=jax.ShapeDtypeStruct((T, D_ATT), F32),
        compiler_params=_params(dimension_semantics=("arbitrary",)),
    )(proj, proj, proj)


def _sb_bwd(proj, do, o):
    T = proj.shape[0]

    def body(q_ref, k_ref, v_ref, do_ref, o_ref, dq_ref, dk_ref, dv_ref):
        i = pl.program_id(0)
        masks = _sb_head_masks()
        tri = _sb_tri(True)
        tri_incl = _sb_tri(False)

        @pl.when(i == 0)
        def _():
            dk_ref[...] = jnp.zeros_like(dk_ref)
            dv_ref[...] = jnp.zeros_like(dv_ref)

        qs = _stack_heads(q_ref[...] * (HEAD_DIM ** -0.5), masks).astype(BF16)
        dos = _stack_heads(do_ref[...], masks).astype(BF16)
        delta = jnp.sum(dos.astype(F32) * jnp.concatenate([o_ref[...]] * SB_HEADS, axis=0), axis=1, keepdims=True)
        n_tiles = (i * Q_BLOCK) // SB_KT + 1

        def tile(t, carry, masked):
            rows = _sb_rows(t)
            kb = k_ref[rows, :].astype(BF16)
            vb = v_ref[rows, :].astype(BF16)
            past = _sb_past(i, t) if masked else None
            dq, c, ce = carry
            z, lsz, w, c = _sb_tile(qs, kb, past, c, tri)
            gv = lax.dot_general(dos, vb, NT, preferred_element_type=F32)
            wb = w.astype(BF16)
            e = wb.astype(F32) * gv
            suf, ce = _suffix(e, ce, tri_incl)
            dz = e * jnp.exp(lsz - z) - (delta - suf) * jnp.exp(lsz)
            if masked:
                dz = jnp.where(past, dz, 0.0)
            dzb = dz.astype(BF16)
            dq = dq + _unstack_heads(lax.dot_general(dzb, kb, NN, preferred_element_type=F32), masks)
            dk_ref[rows, :] = dk_ref[rows, :] + lax.dot_general(dzb, qs, TN, preferred_element_type=F32)
            dv_ref[rows, :] = dv_ref[rows, :] + lax.dot_general(wb, dos, TN, preferred_element_type=F32)
            return dq, c, ce

        zcol = jnp.zeros((SB_ROWS, 1), F32)
        carry = tile(n_tiles - 1, (jnp.zeros((Q_BLOCK, D_ATT), F32), zcol, zcol), True)
        carry = lax.fori_loop(0, n_tiles - 1, lambda tt, cr: tile(n_tiles - 2 - tt, cr, False), carry)
        dq_ref[...] = carry[0] * (HEAD_DIM ** -0.5)

    cb = COL_QS // D_ATT
    blk = pl.BlockSpec((Q_BLOCK, D_ATT), lambda i: (i, 0))
    full = pl.BlockSpec((T, D_ATT), lambda i: (0, 0))
    sh = jax.ShapeDtypeStruct((T, D_ATT), F32)
    return pl.pallas_call(
        body, name="sb_bwd", grid=(T // Q_BLOCK,),
        in_specs=[pl.BlockSpec((Q_BLOCK, D_ATT), lambda i: (i, cb)),
                  pl.BlockSpec((T, D_ATT), lambda i: (0, cb + 1)),
                  pl.BlockSpec((T, D_ATT), lambda i: (0, cb + 2)), blk, blk],
        out_specs=[blk, full, full], out_shape=[sh, sh, sh],
        compiler_params=_params(dimension_semantics=("arbitrary",)),
    )(proj, proj, proj, do, o)


def _tok(c, by=None):
    if by is None:
        return pl.BlockSpec((TM, c), lambda i, j, k: (i, 0))
    if by == 1:
        return pl.BlockSpec((TM, c), lambda i, j, k: (i, j))
    return pl.BlockSpec((TM, c), lambda i, j, k: (i, k))


def _chunked(c, by):
    if by == 1:
        return pl.BlockSpec((None, TM, c), lambda i, j, k: (j, i, 0))
    return pl.BlockSpec((None, TM, c), lambda i, j, k: (k, i, 0))


def _gain_spec():
    return pl.BlockSpec((1, D_MODEL), lambda i, j, k: (0, 0))


def _all_chunks(rows, c):
    return pl.BlockSpec((N_CHIPS, rows, c), lambda i, j, k: (0, i, 0))


def _wfull(r, c, l):
    return pl.BlockSpec((N_CHIPS, None, r, c), lambda i, j, k: (0, l, 0, 0))


def _pick(idx, c):
    return lambda ins: ins[idx][c]


def _cols(idx, c, w):
    return lambda ins: ins[idx][:, c * w:(c + 1) * w]


def _rows(rows, width):
    return pl.BlockSpec((rows, width), lambda i, j, k: (i, 0))


def _whole(shape):
    return pl.BlockSpec(shape, lambda i, j, k: (0, 0))


def _ffn_fwd(x, gain, wg, wu, wd):
    T = x.shape[0]
    wg, wu, wd = (w.reshape(-1, D_MODEL) for w in (wg, wu, wd))
    ff = wd.shape[0]
    tm = TM // 2
    h = _rms_fwd(x, gain)

    def swiglu(vals, ins, outs, i):
        gt, up = vals
        s = _sigmoid(gt)
        sil = gt * s
        outs[0][...] = sil.astype(BF16)
        outs[1][...] = (up * (s * (1.0 + gt * (1.0 - s)))).astype(BF16)
        outs[2][...] = (sil * up).astype(BF16)

    ash = jax.ShapeDtypeStruct((T, ff), BF16)
    sil, up_dsil, act = _mm(
        "ffn_up", [h, wg, wu], [_rows(tm, D_MODEL), _whole(wg.shape), _whole(wu.shape)],
        [(0, 1, 0), (0, 2, 1)], 2, None, NT, (T // tm, 1, 1), swiglu, [ash] * 3, [_rows(tm, ff)] * 3)

    def resid(vals, ins, outs, i):
        outs[0][...] = ins[2][...] + 0.5 * vals[0]

    (y,) = _mm(
        "ffn_down", [act, wd, x], [_rows(TM, ff), _whole(wd.shape), _tok(D_MODEL)], [(0, 1, 0)], 1, None, NN,
        (T // TM, 1, 1), resid, [jax.ShapeDtypeStruct((T, D_MODEL), F32)], [_tok(D_MODEL)])
    return y, (x, h, sil, up_dsil, act)


def _ffn_bwd(dxo, gain, wg, wu, wd, saved):
    x, h, sil, up_dsil, act = saved
    T = x.shape[0]
    n_chips, _, ffs, _ = wd.shape
    wg, wu, wd = (w.reshape(-1, D_MODEL) for w in (wg, wu, wd))
    ff = wd.shape[0]
    tk = TM
    tm = TM // 2

    def dswiglu(vals, ins, outs, i):
        da = 0.5 * vals[0]
        outs[0][...] = (da * ins[3][...].astype(F32)).astype(BF16)
        outs[1][...] = (da * ins[2][...].astype(F32)).astype(BF16)

    ash = jax.ShapeDtypeStruct((T, ff), BF16)
    dgate, dup = _mm(
        "ffn_dact", [dxo, wd, sil, up_dsil], [_rows(tm, D_MODEL), _whole(wd.shape), _rows(tm, ff), _rows(tm, ff)],
        [(0, 1, 0)], 1, None, NT, (T // tm, 1, 1), dswiglu, [ash, ash], [_rows(tm, ff)] * 2)

    def half(vals, ins, outs, i):
        outs[0][...] = (0.5 * vals[0]).astype(BF16)

    def cast(vals, ins, outs, i):
        outs[0][...] = vals[0].astype(BF16)

    tok_k = pl.BlockSpec((tk, D_MODEL), lambda i, j, k: (k, 0))
    hid_k = pl.BlockSpec((tk, ff), lambda i, j, k: (k, 0))
    wsh = jax.ShapeDtypeStruct((ff, D_MODEL), BF16)
    (dwd,) = _mm("ffn_dwd", [act, dxo], [hid_k, tok_k], [(0, 1, 0)], 1, (ff, D_MODEL), TN, (1, 1, T // tk), half,
                 [wsh], [_whole((ff, D_MODEL))])

    dx, dgain = _mm(
        "ffn_dx", [dgate, dup, wg, wu, x, gain, dxo],
        [_rows(tm, ff), _rows(tm, ff), _whole(wg.shape), _whole(wu.shape), _rows(tm, D_MODEL), _gain_spec(),
         _rows(tm, D_MODEL)],
        [(0, 2, 0), (1, 3, 0)], 1, None, NN, (T // tm, 1, 1), _rms_bwd_epilogue(4, 5, 6),
        [jax.ShapeDtypeStruct((T, D_MODEL), F32), jax.ShapeDtypeStruct((8, D_MODEL), F32)],
        [_rows(tm, D_MODEL), pl.BlockSpec((8, D_MODEL), lambda i, j, k: (0, 0))])

    dws = []
    for dact in (dgate, dup):
        dws += _mm("ffn_dwgu", [dact, h], [hid_k, tok_k], [(0, 1, 0)], 1, (ff, D_MODEL), TN, (1, 1, T // tk), cast,
                   [wsh], [_whole((ff, D_MODEL))])
    dwg, dwu, dwd = (w.reshape(n_chips, ffs, D_MODEL) for w in (dws[0], dws[1], dwd))
    return dx, dgain, dwg, dwu, dwd


def _joined_mixer_weights(wpd, wps, wo):
    n, _, r, c = wpd.shape
    wpd_n, wps_n = (w[:, 0].transpose(1, 0, 2).reshape(r, n * c) for w in (wpd, wps))
    return wpd_n, wps_n, wo.reshape(-1, wo.shape[3])


def _mixer_fwd(x, gain, W, l, tabs):
    T = x.shape[0]
    win, wpd, wps, wo = W["w_in"], W["w_proj_dil"], W["w_proj_sb"], W["w_out"]
    cin = win.shape[3]
    cp = wpd.shape[3]
    h = _rms_fwd(x, gain)

    n_rope = 6 * D_ATT

    tm = TM // 2

    def roped(vals, ins, outs, i):
        for j, v in enumerate(vals):
            lo = j * cin
            k = min(max(n_rope - lo, 0), cin)
            if k:
                tab = [jnp.concatenate([ins[t][...]] * (k // 128), axis=1) for t in (2, 3, 4)]
                outs[0][:, lo:lo + k] = _rope_fwd(v[:, :k], *tab)
            if k < cin:
                outs[0][:, lo + k:lo + cin] = v[:, k:]

    (proj,) = _mm(
        "mix_in", [h, win, *tabs], [_rows(tm, D_MODEL), _wfull(D_MODEL, cin, l)] + [_rows(tm, 128)] * 3,
        [(0, _pick(1, c), c) for c in range(N_CHIPS)], N_CHIPS, None, NN, (T // tm, 1, 1), roped,
        [jax.ShapeDtypeStruct((T, N_CHIPS * cin), F32)], [_rows(tm, N_CHIPS * cin)])

    os_, lses = [], []
    for g, (window, dil) in enumerate(DIL_GROUPS):
        o_g, lse_g = _dil_fwd(proj, g, dil)
        os_.append(o_g)
        lses.append(lse_g)
    o_dil, lse = _dil_merge(os_, lses)
    o_sb = _sb_fwd(proj)

    def gated(vals, ins, outs, i):
        pd, ps = vals
        outs[0][...] = (_sigmoid(ins[4][...]) * pd + _sigmoid(ins[5][...]) * ps).astype(BF16)
        outs[1][...] = pd.astype(BF16)
        outs[2][...] = ps.astype(BF16)

    wpd_n, wps_n, wo_n = _joined_mixer_weights(wpd, wps, wo)
    gd_spec = pl.BlockSpec((TM, D_MODEL), lambda i, j, k: (i, COL_GD // D_MODEL))
    gs_spec = pl.BlockSpec((TM, D_MODEL), lambda i, j, k: (i, COL_GS // D_MODEL))
    ush = jax.ShapeDtypeStruct((T, D_MODEL), BF16)
    u, pd, ps = _mm(
        "mix_gate", [o_dil, o_sb, wpd_n, wps_n, proj, proj],
        [_tok(D_ATT), _tok(D_ATT), _whole(wpd_n.shape), _whole(wps_n.shape), gd_spec, gs_spec],
        [(0, 2, 0), (1, 3, 1)], 2, None, NN, (T // TM, 1, 1), gated, [ush] * 3, [_tok(D_MODEL)] * 3)

    def resid(vals, ins, outs, i):
        outs[0][...] = ins[2][...] + vals[0]

    (y,) = _mm(
        "mix_out", [u, wo_n, x], [_tok(D_MODEL), _whole(wo_n.shape), _tok(D_MODEL)], [(0, 1, 0)], 1, None, NN,
        (T // TM, 1, 1), resid, [jax.ShapeDtypeStruct((T, D_MODEL), F32)], [_tok(D_MODEL)])
    return y, (x, h, proj, o_dil, lse, o_sb, u, pd, ps)


def _mixer_bwd(dxo, gain, W, l, tabs, saved):
    x, h, proj, o_dil, lse, o_sb, u, pd, ps = saved
    T = x.shape[0]
    win, wpd, wps, wo = W["w_in"], W["w_proj_dil"], W["w_proj_sb"], W["w_out"]
    cin = win.shape[3]
    cp = wpd.shape[3]
    tk = TM
    tm = TM // 2
    row = pl.BlockSpec((tm, D_MODEL), lambda i, j, k: (i, 0))

    def dgated(vals, ins, outs, i):
        du = vals[0]
        sd = _sigmoid(ins[4][...])
        ss = _sigmoid(ins[5][...])
        outs[0][...] = (du * sd).astype(BF16)
        outs[1][...] = (du * ss).astype(BF16)
        outs[2][...] = (du * ins[2][...].astype(F32) * sd * (1.0 - sd)).astype(BF16)
        outs[3][...] = (du * ins[3][...].astype(F32) * ss * (1.0 - ss)).astype(BF16)

    wpd_n, wps_n, wo_n = _joined_mixer_weights(wpd, wps, wo)
    gd_spec = pl.BlockSpec((TM, D_MODEL), lambda i, j, k: (i, COL_GD // D_MODEL))
    gs_spec = pl.BlockSpec((TM, D_MODEL), lambda i, j, k: (i, COL_GS // D_MODEL))
    ush = jax.ShapeDtypeStruct((T, D_MODEL), BF16)
    dpd, dps, dgd, dgs = _mm(
        "mix_du", [dxo, wo_n, pd, ps, proj, proj],
        [_tok(D_MODEL), _whole(wo_n.shape), _tok(D_MODEL), _tok(D_MODEL), gd_spec, gs_spec],
        [(0, 1, 0)], 1, None, NT, (T // TM, 1, 1), dgated, [ush] * 4, [_tok(D_MODEL)] * 4)

    def one(vals, ins, outs, i):
        outs[0][...] = vals[0].astype(BF16)

    def two(vals, ins, outs, i):
        outs[0][...] = vals[0].astype(BF16)
        outs[1][...] = vals[1].astype(BF16)

    tok_k = pl.BlockSpec((tk, D_MODEL), lambda i, j, k: (k, 0))
    att_k = pl.BlockSpec((tk, D_ATT), lambda i, j, k: (k, 0))
    (dwo_n,) = _mm("mix_dwo", [u, dxo], [tok_k, tok_k], [(0, 1, 0)], 1, (D_MODEL, D_MODEL), TN, (1, 1, T // tk), one,
                   [jax.ShapeDtypeStruct((D_MODEL, D_MODEL), BF16)], [_whole((D_MODEL, D_MODEL))])

    def plain2(vals, ins, outs, i):
        outs[0][...] = vals[0]
        outs[1][...] = vals[1]

    ash = jax.ShapeDtypeStruct((T, D_ATT), F32)
    do_dil, do_sb = _mm(
        "mix_do", [dpd, dps, wpd_n, wps_n], [_tok(D_MODEL), _tok(D_MODEL), _whole(wpd_n.shape), _whole(wps_n.shape)],
        [(0, 2, 0), (1, 3, 1)], 2, None, NT, (T // TM, 1, 1), plain2, [ash, ash], [_tok(D_ATT)] * 2)

    psh = jax.ShapeDtypeStruct((D_ATT, D_MODEL), BF16)
    dwpd_n, dwps_n = _mm(
        "mix_dwp", [o_dil, o_sb, dpd, dps], [att_k, att_k, tok_k, tok_k], [(0, 2, 0), (1, 3, 1)], 2,
        (D_ATT, D_MODEL), TN, (1, 1, T // tk), two, [psh, psh], [_whole((D_ATT, D_MODEL))] * 2)
    dwpd, dwps = (w.reshape(D_ATT, N_CHIPS, cp).transpose(1, 0, 2) for w in (dwpd_n, dwps_n))
    dwo = dwo_n.reshape(N_CHIPS, cp, D_MODEL)

    dqs, dks, dvs = [], [], []
    for g, (window, dil) in enumerate(DIL_GROUPS):
        dq, dk, dv = _dil_bwd(proj, do_dil, o_dil, lse, g, dil)
        dqs.append(dq)
        dks.append(dk)
        dvs.append(dv)
    dq_s, dk_s, dv_s = _sb_bwd(proj, do_sb, o_sb)
    dproj = _assemble_dproj(dqs + dks, dvs + [dq_s, dk_s, dv_s], [dgd, dgs], tabs)

    dx, dgain = _mm(
        "mix_dx", [dproj, win, x, gain, dxo],
        [pl.BlockSpec((tm, N_CHIPS * cin), lambda i, j, k: (i, 0)), _wfull(D_MODEL, cin, l), row, _gain_spec(), row],
        [(_cols(0, c, cin), _pick(1, c), 0) for c in range(N_CHIPS)], 1, None, NT, (T // tm, 1, 1),
        _rms_bwd_epilogue(2, 3, 4),
        [jax.ShapeDtypeStruct((T, D_MODEL), F32), jax.ShapeDtypeStruct((8, D_MODEL), F32)],
        [row, pl.BlockSpec((8, D_MODEL), lambda i, j, k: (0, 0))])

    (dwin,) = _mm(
        "mix_dwin", [h, dproj],
        [pl.BlockSpec((tk, D_MODEL), lambda i, j, k: (k, 0)), pl.BlockSpec((tk, cin), lambda i, j, k: (k, j))],
        [(0, 1, 0)], 1, (D_MODEL, cin), TN, (1, N_CHIPS, T // tk), one,
        [jax.ShapeDtypeStruct((N_CHIPS, D_MODEL, cin), BF16)],
        [pl.BlockSpec((None, D_MODEL, cin), lambda i, j, k: (j, 0, 0))])
    return dx, dgain, dwin, dwpd, dwps, dwo


def _local_step(x, target, norms, norm_final, weights_of, on_grads):
    T = x.shape[0]
    tabs = _rope_tables(T)
    saved, held = [], []
    for l in range(DEPTH):
        w1 = weights_of(l, 0, x)
        x, s1 = _ffn_fwd(x, norms["norm_ffn1"][l:l + 1], w1["ffn1_w_gate"], w1["ffn1_w_up"], w1["ffn1_w_down"])
        w2 = weights_of(l, 1, x)
        x, s2 = _mixer_fwd(x, norms["norm_mix"][l:l + 1], w2, 0, tabs)
        w3 = weights_of(l, 2, x)
        x, s3 = _ffn_fwd(x, norms["norm_ffn2"][l:l + 1], w3["ffn2_w_gate"], w3["ffn2_w_up"], w3["ffn2_w_down"])
        saved.append((s1, s2, s3))
        held.append((w1, w2, w3))
    dx, dg_final, loss = _final_loss(x, norm_final.reshape(1, D_MODEL), target)
    gains = [None] * DEPTH
    for l in reversed(range(DEPTH)):
        s1, s2, s3 = saved[l]
        w1, w2, w3 = held[l]
        dx, dg2, dwg2, dwu2, dwd2 = _ffn_bwd(dx, norms["norm_ffn2"][l:l + 1], w3["ffn2_w_gate"], w3["ffn2_w_up"],
                                             w3["ffn2_w_down"], s3)
        dx = on_grads(l, 2, dict(ffn2_w_gate=dwg2, ffn2_w_up=dwu2, ffn2_w_down=dwd2), dx)
        dx, dgm, dwin, dwpd, dwps, dwo = _mixer_bwd(dx, norms["norm_mix"][l:l + 1], w2, 0, tabs, s2)
        dx = on_grads(l, 1, dict(w_in=dwin, w_proj_dil=dwpd, w_proj_sb=dwps, w_out=dwo), dx)
        dx, dg1, dwg1, dwu1, dwd1 = _ffn_bwd(dx, norms["norm_ffn1"][l:l + 1], w1["ffn1_w_gate"], w1["ffn1_w_up"],
                                             w1["ffn1_w_down"], s1)
        dx = on_grads(l, 0, dict(ffn1_w_gate=dwg1, ffn1_w_up=dwu1, ffn1_w_down=dwd1), dx)
        gains[l] = dict(norm_ffn1=dg1, norm_mix=dgm, norm_ffn2=dg2)
    return loss, dx, gains, dg_final


def _place():
    x, y, c = lax.axis_index("x"), lax.axis_index("y"), lax.axis_index("c")
    chips = [(1 - x, y), (x, 1 - y), (1 - x, 1 - y)]
    return x, y, c, chips


def _half(c, r):
    return pl.ds(pl.multiple_of(c * (r // 2), 8), r // 2)


def _cast_into_slot(ws, ls, me_arr, after):
    n = len(ws)
    late = [] if after is None else [after]

    def body(me_ref, *refs):
        for a in range(n):
            refs[len(refs) - n + a][...] = refs[a][...].astype(BF16)

    def src(w, l):
        return pl.BlockSpec((None, w.shape[1] // 4, w.shape[2]), lambda i, me: (l, i, 0))

    def dst(w):
        return pl.BlockSpec((None, None, w.shape[1] // 4, w.shape[2]), lambda i, me: (me[0], 0, i, 0))

    return pl.pallas_call(
        body, name="cast_weights",
        grid_spec=pltpu.PrefetchScalarGridSpec(
            num_scalar_prefetch=1, grid=(4,),
            in_specs=[src(w, l) for w, l in zip(ws, ls)] + [pl.BlockSpec(memory_space=pl.ANY)] * len(late),
            out_specs=[dst(w) for w in ws]),
        out_shape=[jax.ShapeDtypeStruct((N_CHIPS, 1) + w.shape[1:], BF16) for w in ws], compiler_params=_params(),
    )(me_arr, *ws, *late)


HBM_SPEC = pl.BlockSpec(memory_space=pltpu.HBM)
SEM_SPEC = pl.BlockSpec(memory_space=pltpu.SEMAPHORE)
SPLIT_COPY = pltpu.CompilerParams(has_side_effects=pltpu.SideEffectType.DATAFLOW_SIDE_EFFECTING)


def _gather_piece(ref, chip_id, c):
    return ref.at[chip_id, 0, _half(c, ref.shape[2]), :]


def _gather_start(tag, bufs):
    n = len(bufs)

    def body(*refs):
        out_refs = refs[n:2 * n]
        send_sems, recv_sems, token = refs[2 * n:]
        x, y, c, chips = _place()
        me = 2 * x + y
        for a in range(n):
            piece = _gather_piece(out_refs[a], me, c)
            for j, chip in enumerate(chips):
                pltpu.make_async_remote_copy(
                    src_ref=piece, dst_ref=piece, send_sem=send_sems.at[3 * a + j], recv_sem=recv_sems.at[3 * a + j],
                    device_id=(*chip, c), device_id_type=MESH).start()
        token[...] = jnp.zeros_like(token)

    outs = pl.pallas_call(
        body, name=f"gather_start_{tag}", in_specs=[HBM_SPEC] * n,
        out_specs=[HBM_SPEC] * n + [SEM_SPEC, SEM_SPEC, pl.BlockSpec(memory_space=pltpu.VMEM)],
        out_shape=[pltpu.HBM(b.shape, b.dtype) for b in bufs] + [pltpu.SemaphoreType.DMA((3 * n,))] * 2
        + [jax.ShapeDtypeStruct((8, 128), F32)],
        input_output_aliases={a: a for a in range(n)}, compiler_params=SPLIT_COPY,
    )(*[pltpu.with_memory_space_constraint(b, pltpu.HBM) for b in bufs])
    return outs[:n], outs[n], outs[n + 1], outs[n + 2]


def _gather_wait(k, bufs, places, send_sems, recv_sems, after):
    m = len(bufs)

    def body(*refs):
        in_refs = refs[:m]
        ssem, rsem = refs[m], refs[m + 1]
        x, y, c, chips = _place()
        me = 2 * x + y
        for t, a in enumerate(places):
            for j, chip in enumerate(chips):
                cp = pltpu.make_async_remote_copy(
                    src_ref=_gather_piece(in_refs[t], me, c),
                    dst_ref=_gather_piece(in_refs[t], 2 * chip[0] + chip[1], c),
                    send_sem=ssem.at[3 * a + j], recv_sem=rsem.at[3 * a + j], device_id=(*chip, c),
                    device_id_type=MESH)
                cp.wait_send()
                cp.wait_recv()

    return pl.pallas_call(
        body, name=f"gather_wait_{k}",
        in_specs=[HBM_SPEC] * m + [SEM_SPEC, SEM_SPEC, pl.BlockSpec(memory_space=pl.ANY)], out_specs=[HBM_SPEC] * m,
        out_shape=[pltpu.HBM(b.shape, b.dtype) for b in bufs], input_output_aliases={t: t for t in range(m)},
        compiler_params=SPLIT_COPY,
    )(*bufs, send_sems, recv_sems, after)


def _gather_relay(bufs):
    n = len(bufs)

    def body(*refs):
        out_refs = refs[n:2 * n]
        send_sems, recv_sems = refs[2 * n:]
        x, y, c, chips = _place()
        cps = []
        for a in range(n):
            for j, chip in enumerate(chips):
                piece = _gather_piece(out_refs[a], 2 * chip[0] + chip[1], c)
                cps.append(pltpu.make_async_remote_copy(
                    src_ref=piece, dst_ref=piece, send_sem=send_sems.at[a, j], recv_sem=recv_sems.at[a, j],
                    device_id=(x, y, 1 - c), device_id_type=MESH))
        for cp in cps:
            cp.start()
        for a in range(n):
            for j, chip in enumerate(chips):
                theirs = _gather_piece(out_refs[a], 2 * chip[0] + chip[1], 1 - c)
                pltpu.make_async_remote_copy(
                    src_ref=theirs, dst_ref=theirs, send_sem=send_sems.at[a, j], recv_sem=recv_sems.at[a, j],
                    device_id=(x, y, 1 - c), device_id_type=MESH).wait_recv()
        for cp in cps:
            cp.wait_send()

    any_spec = pl.BlockSpec(memory_space=pl.ANY)
    return pl.pallas_call(
        body, name="gather_relay", in_specs=[any_spec] * n, out_specs=[any_spec] * n,
        out_shape=[jax.ShapeDtypeStruct(b.shape, b.dtype) for b in bufs],
        input_output_aliases={a: a for a in range(n)},
        scratch_shapes=[pltpu.SemaphoreType.DMA((n, 3))] * 2,
    )(*bufs)


def _exchange_halves(gs):
    n = len(gs)

    def body(*refs):
        g_refs, out_refs = refs[:n], refs[n:2 * n]
        send_sems, recv_sems = refs[2 * n:]
        x, y, c, _ = _place()
        cps = []
        for a in range(n):
            r = g_refs[a].shape[1]
            cps.append(pltpu.make_async_remote_copy(
                src_ref=g_refs[a].at[:, _half(1 - c, r), :], dst_ref=out_refs[a],
                send_sem=send_sems.at[a], recv_sem=recv_sems.at[a], device_id=(x, y, 1 - c), device_id_type=MESH))
        for cp in cps:
            cp.start()
        for cp in cps:
            cp.wait()

    any_spec = pl.BlockSpec(memory_space=pl.ANY)
    return pl.pallas_call(
        body, name="grad_to_sibling", in_specs=[any_spec] * n, out_specs=[any_spec] * n,
        out_shape=[jax.ShapeDtypeStruct((g.shape[0], g.shape[1] // 2, g.shape[2]), g.dtype) for g in gs],
        scratch_shapes=[pltpu.SemaphoreType.DMA((n,))] * 2,
    )(*gs)


def _add_half(gs, gots, c_arr):
    n = len(gs)

    def body(c_ref, *refs):
        for a in range(n):
            refs[2 * n + a][...] = (refs[a][...].astype(F32) + refs[n + a][...].astype(F32)).astype(BF16)

    def own(g):
        return pl.BlockSpec((None, g.shape[1] // 2, g.shape[2]), lambda k, cr: (k, cr[0], 0))

    def half(g):
        return pl.BlockSpec((None, g.shape[1] // 2, g.shape[2]), lambda k, cr: (k, 0, 0))

    return pl.pallas_call(
        body, name="grad_add_half",
        grid_spec=pltpu.PrefetchScalarGridSpec(
            num_scalar_prefetch=1, grid=(N_CHIPS,),
            in_specs=[own(g) for g in gs] + [half(g) for g in gs], out_specs=[half(g) for g in gs]),
        out_shape=[jax.ShapeDtypeStruct(got.shape, BF16) for got in gots], compiler_params=_params(),
    )(c_arr, *gs, *gots)


def _scatter_start(k, ss, thru):
    n = len(ss)

    def body(*refs):
        s_refs, land_refs = refs[2 * n + 1:3 * n + 1], refs[3 * n + 1:4 * n + 1]
        send_sems, recv_sems = refs[4 * n + 2:]
        x, y, c, chips = _place()
        me = 2 * x + y
        for a in range(n):
            for j, chip in enumerate(chips):
                pltpu.make_async_remote_copy(
                    src_ref=s_refs[a].at[2 * chip[0] + chip[1]], dst_ref=land_refs[a].at[me],
                    send_sem=send_sems.at[3 * a + j], recv_sem=recv_sems.at[3 * a + j], device_id=(*chip, c),
                    device_id_type=MESH).start()

    lands = [lax.empty(s.shape, s.dtype) for s in ss]
    hbm = [pltpu.HBM(s.shape, s.dtype) for s in ss]
    outs = pl.pallas_call(
        body, name=f"grad_scatter_start_{k}", in_specs=[HBM_SPEC] * (2 * n + 1),
        out_specs=[HBM_SPEC] * (2 * n + 1) + [SEM_SPEC, SEM_SPEC],
        out_shape=hbm + hbm + [pltpu.HBM(thru.shape, thru.dtype)] + [pltpu.SemaphoreType.DMA((3 * n,))] * 2,
        input_output_aliases={a: a for a in range(2 * n + 1)}, compiler_params=SPLIT_COPY,
    )(*[pltpu.with_memory_space_constraint(v, pltpu.HBM) for v in list(ss) + lands + [thru]])
    return (outs[:n], outs[n:2 * n], outs[2 * n + 1], outs[2 * n + 2]), outs[2 * n]


def _scatter_wait(k, ss, lands, send_sems, recv_sems, after):
    n = len(ss)

    def body(*refs):
        s_refs, land_refs = refs[:n], refs[n:2 * n]
        ssem, rsem = refs[2 * n], refs[2 * n + 1]
        x, y, c, chips = _place()
        me = 2 * x + y
        for a in range(n):
            for j, chip in enumerate(chips):
                cid = 2 * chip[0] + chip[1]
                cp = pltpu.make_async_remote_copy(
                    src_ref=s_refs[a].at[cid], dst_ref=land_refs[a].at[cid], send_sem=ssem.at[3 * a + j],
                    recv_sem=rsem.at[3 * a + j], device_id=(*chip, c), device_id_type=MESH)
                cp.wait_send()
                cp.wait_recv()

    hbm = [pltpu.HBM(s.shape, s.dtype) for s in ss]
    outs = pl.pallas_call(
        body, name=f"grad_scatter_wait_{k}",
        in_specs=[HBM_SPEC] * (2 * n) + [SEM_SPEC, SEM_SPEC, pl.BlockSpec(memory_space=pl.ANY)],
        out_specs=[HBM_SPEC] * (2 * n), out_shape=hbm + hbm,
        input_output_aliases={a: a for a in range(2 * n)}, compiler_params=SPLIT_COPY,
    )(*ss, *lands, send_sems, recv_sems, after)
    return outs[:n], outs[n:]


def _sum_chips(lands, ss, me_arr):
    n = len(lands)

    def body(me_ref, *refs):
        for own in range(N_CHIPS):
            @pl.when(me_ref[0] == own)
            def _(own=own):
                for a in range(n):
                    acc = None
                    for k in range(N_CHIPS):
                        term = (refs[n + a][...] if k == own else refs[a][k]).astype(F32)
                        acc = term if acc is None else acc + term
                    refs[2 * n + a][...] = acc

    return pl.pallas_call(
        body, name="grad_sum_chips",
        grid_spec=pltpu.PrefetchScalarGridSpec(
            num_scalar_prefetch=1, grid=(1,),
            in_specs=[pl.BlockSpec(la.shape, lambda i, me: (0, 0, 0)) for la in lands]
            + [pl.BlockSpec((None,) + la.shape[1:], lambda i, me: (me[0], 0, 0)) for la in lands],
            out_specs=[pl.BlockSpec(la.shape[1:], lambda i, me: (0, 0)) for la in lands]),
        out_shape=[jax.ShapeDtypeStruct(la.shape[1:], F32) for la in lands], compiler_params=_params(),
    )(me_arr, *lands, *ss)


def _swap_halves(fs):
    n = len(fs)

    def body(*refs):
        f_refs, out_refs = refs[:n], refs[n:2 * n]
        send_sems, recv_sems = refs[2 * n:]
        x, y, c, _ = _place()
        cps = [pltpu.make_async_remote_copy(
            src_ref=f_refs[a], dst_ref=out_refs[a], send_sem=send_sems.at[a], recv_sem=recv_sems.at[a],
            device_id=(x, y, 1 - c), device_id_type=MESH) for a in range(n)]
        for cp in cps:
            cp.start()
        for cp in cps:
            cp.wait()

    any_spec = pl.BlockSpec(memory_space=pl.ANY)
    return pl.pallas_call(
        body, name="grad_swap_halves", in_specs=[any_spec] * n, out_specs=[any_spec] * n,
        out_shape=[jax.ShapeDtypeStruct(f.shape, f.dtype) for f in fs],
        scratch_shapes=[pltpu.SemaphoreType.DMA((n,))] * 2,
    )(*fs)


def _allreduce_rows(stats):
    def body(s_ref, o_ref, buf, send_sems, recv_sems):
        x, y, c, _ = _place()
        me = 4 * x + 2 * y + c
        buf[me] = s_ref[...]
        cps = []
        for k in range(1, 8):
            px = jnp.where(k & 4, 1 - x, x)
            py = jnp.where(k & 2, 1 - y, y)
            pc = jnp.where(k & 1, 1 - c, c)
            cps.append(pltpu.make_async_remote_copy(
                src_ref=s_ref, dst_ref=buf.at[me], send_sem=send_sems.at[k - 1], recv_sem=recv_sems.at[k - 1],
                device_id=(px, py, pc), device_id_type=MESH))
        for cp in cps:
            cp.start()
        for cp in cps:
            cp.wait()
        acc = buf[0]
        for d in range(1, 8):
            acc = acc + buf[d]
        o_ref[...] = acc

    vm = pl.BlockSpec(memory_space=pltpu.VMEM)
    return pl.pallas_call(
        body, name="allreduce_rows", in_specs=[vm], out_specs=vm,
        out_shape=jax.ShapeDtypeStruct(stats.shape, F32),
        scratch_shapes=[pltpu.VMEM((8,) + stats.shape, F32), pltpu.SemaphoreType.DMA((7,)),
                        pltpu.SemaphoreType.DMA((7,))],
    )(stats)


def _adamw_math(w, g, m, v):
    m = ADAM_B1 * m + (1.0 - ADAM_B1) * g
    v = ADAM_B2 * v + (1.0 - ADAM_B2) * (g * g)
    m_hat = m / (1.0 - ADAM_B1 ** ADAM_STEP)
    v_hat = v / (1.0 - ADAM_B2 ** ADAM_STEP)
    delta = -ADAM_LR * (m_hat / (jnp.sqrt(v_hat) + ADAM_EPS) + ADAM_WD * w)
    return delta, m, v


def _adamw(ws, ms, vs, mines, theirs, l, c_arr, earlier):
    n = len(ws)
    held = [t for e in earlier if e is not None for t in e]
    assert len(held) in (0, 4 * n)

    def body(c_ref, *refs):
        outs = refs[len(refs) - 4 * n:]
        for a in range(n):
            w_ref, m_ref, v_ref, a_ref, b_ref = refs[5 * a:5 * a + 5]
            g = jnp.where(pl.program_id(0) == c_ref[0], a_ref[...], b_ref[...])
            delta, mn, vn = _adamw_math(w_ref[...], g, m_ref[...], v_ref[...])
            outs[4 * a][...] = g
            outs[4 * a + 1][...] = delta
            outs[4 * a + 2][...] = mn
            outs[4 * a + 3][...] = vn

    def blk(w):
        tr = w.shape[1] // 4
        return pl.BlockSpec((None, tr, w.shape[2]), lambda hh, i, cr: (l, 2 * hh + i, 0))

    def half(w):
        return pl.BlockSpec((w.shape[1] // 4, w.shape[2]), lambda hh, i, cr: (i, 0))

    outs = pl.pallas_call(
        body, name="adamw",
        grid_spec=pltpu.PrefetchScalarGridSpec(
            num_scalar_prefetch=1, grid=(2, 2),
            in_specs=[sp for w in ws for sp in (blk(w), blk(w), blk(w), half(w), half(w))]
            + [pl.BlockSpec(memory_space=pl.ANY)] * len(held),
            out_specs=[blk(w) for w in ws for _ in range(4)]),
        out_shape=[jax.ShapeDtypeStruct(w.shape, F32) for w in ws for _ in range(4)],
        input_output_aliases={1 + 5 * n + t: t for t in range(len(held))}, compiler_params=_params(),
    )(c_arr, *[t for grp in zip(ws, ms, vs, mines, theirs) for t in grp], *held)
    return [outs[4 * a:4 * a + 4] for a in range(n)]


def _adamw_rows(w, m, v, g):
    def body(w_ref, m_ref, v_ref, g_ref, d_ref, mo_ref, vo_ref):
        delta, mn, vn = _adamw_math(w_ref[...], g_ref[...], m_ref[...], v_ref[...])
        d_ref[...] = delta
        mo_ref[...] = mn
        vo_ref[...] = vn

    vm = pl.BlockSpec(memory_space=pltpu.VMEM)
    sh = jax.ShapeDtypeStruct(w.shape, F32)
    return pl.pallas_call(body, name="adamw_rows", in_specs=[vm] * 4, out_specs=[vm] * 3, out_shape=[sh] * 3)(w, m, v, g)


SUBLAYERS = (("ffn1_w_gate", "ffn1_w_up", "ffn1_w_down"), ("w_in", "w_proj_dil", "w_proj_sb", "w_out"),
             ("ffn2_w_gate", "ffn2_w_up", "ffn2_w_down"))
TRANSPOSED = ("ffn1_w_gate", "ffn1_w_up", "ffn2_w_gate", "ffn2_w_up")
LAG = 2


def _pick_row(blocks):
    row = lax.broadcasted_iota(jnp.int32, (8, D_MODEL), 0)
    out = jnp.zeros((8, D_MODEL), F32)
    for i, b in enumerate(blocks):
        out = out + jnp.where(row == i, b, 0.0)
    return out


def kernel(x, norm_ffn1, ffn1_w_gate, ffn1_w_up, ffn1_w_down, norm_mix, w_in, w_proj_dil, w_proj_sb, w_out, norm_ffn2, ffn2_w_gate, ffn2_w_up, ffn2_w_down, norm_final, loss_target, m_norm_ffn1, m_ffn1_w_gate, m_ffn1_w_up, m_ffn1_w_down, m_norm_mix, m_w_in, m_w_proj_dil, m_w_proj_sb, m_w_out, m_norm_ffn2, m_ffn2_w_gate, m_ffn2_w_up, m_ffn2_w_down, m_norm_final, v_norm_ffn1, v_ffn1_w_gate, v_ffn1_w_up, v_ffn1_w_down, v_norm_mix, v_w_in, v_w_proj_dil, v_w_proj_sb, v_w_out, v_norm_ffn2, v_ffn2_w_gate, v_ffn2_w_up, v_ffn2_w_down, v_norm_final):
    given = dict(locals())
    for n in TRANSPOSED:
        for k in ("", "m_", "v_"):
            given[k + n] = jnp.swapaxes(given[k + n], 1, 2)
    weights = {n: given[n] for n in WEIGHT_NAMES}
    norms = {n: given[n] for n in NORM_NAMES}

    c_arr = lax.axis_index("c").astype(jnp.int32).reshape(1)
    me_arr = (2 * lax.axis_index("x") + lax.axis_index("y")).astype(jnp.int32).reshape(1)
    order = [(l, s, n) for l in range(DEPTH) for s in range(len(SUBLAYERS)) for n in SUBLAYERS[s]]
    n_first = len(SUBLAYERS[0])
    sent, token = {}, None
    for tag, idxs in (("a", range(n_first)), ("b", range(n_first, len(order)))):
        cast = _cast_into_slot([weights[order[i][2]] for i in idxs], [order[i][0] for i in idxs], me_arr, token)
        bufs, send_sems, recv_sems, token = _gather_start(tag, cast)
        for p, i in enumerate(idxs):
            sent[i] = (bufs[p], p, send_sems, recv_sems)

    def weights_of(l, s, after):
        idxs = [i for i, (ll, ss, _) in enumerate(order) if (ll, ss) == (l, s)]
        got = _gather_wait(len(SUBLAYERS) * l + s, [sent[i][0] for i in idxs], [sent[i][1] for i in idxs],
                           sent[idxs[0]][2], sent[idxs[0]][3], after)
        return {order[i][2]: g for i, g in zip(idxs, _gather_relay(got))}

    out = {}
    in_flight = []

    def finish(l, s, names, sums, lands, ssem, rsem, after):
        sums, lands = _scatter_wait(len(SUBLAYERS) * l + s, sums, lands, ssem, rsem, after)
        mine = _sum_chips(lands, sums, me_arr)
        theirs = _swap_halves(mine)
        res = _adamw([weights[n] for n in names], [given["m_" + n] for n in names], [given["v_" + n] for n in names],
                     mine, theirs, l, c_arr, [out.get(n) for n in names])
        out.update(zip(names, res))

    def on_grads(l, s, grads, after):
        names = list(grads)
        gs = [grads[n] for n in names]
        sums = _add_half(gs, _exchange_halves(gs), c_arr)
        sent, after = _scatter_start(len(SUBLAYERS) * l + s, sums, after)
        in_flight.append((l, s, names) + sent)
        if len(in_flight) > LAG:
            finish(*in_flight.pop(0), after)
        return after

    loss_blk, grad_x, gains, dg_final = _local_step(x[0], loss_target[0], norms, norm_final, weights_of, on_grads)
    while in_flight:
        finish(*in_flight.pop(0), grad_x)
    out = {k + n: (jnp.swapaxes(v, 1, 2) if n in TRANSPOSED else v)
           for n, res in out.items() for k, v in zip(("grad_", "delta_", "new_m_", "new_v_"), res)}
    out["grad_x"] = grad_x[None]

    rows = [gains[l][n] for n in NORM_NAMES for l in range(DEPTH)] + [dg_final, loss_blk]
    total = _allreduce_rows(_pick_row(rows))
    out["loss"] = total[7, 0]
    wn = jnp.concatenate([given[n] for n in NORM_NAMES] + [norm_final[None], jnp.zeros((1, D_MODEL), F32)])
    mn_ = jnp.concatenate([given["m_" + n] for n in NORM_NAMES] + [m_norm_final[None], jnp.zeros((1, D_MODEL), F32)])
    vn_ = jnp.concatenate([given["v_" + n] for n in NORM_NAMES] + [v_norm_final[None], jnp.ones((1, D_MODEL), F32)])
    d_n, m_n, v_n = _adamw_rows(wn, mn_, vn_, total)
    for i, n in enumerate(NORM_NAMES):
        sl = slice(i * DEPTH, (i + 1) * DEPTH)
        out["grad_" + n], out["delta_" + n], out["new_m_" + n], out["new_v_" + n] = total[sl], d_n[sl], m_n[sl], v_n[sl]
    out["grad_norm_final"], out["delta_norm_final"] = total[6], d_n[6]
    out["new_m_norm_final"], out["new_v_norm_final"] = m_n[6], v_n[6]

    names = ["norm_ffn1", "ffn1_w_gate", "ffn1_w_up", "ffn1_w_down", "norm_mix", "w_in", "w_proj_dil", "w_proj_sb",
             "w_out", "norm_ffn2", "ffn2_w_gate", "ffn2_w_up", "ffn2_w_down", "norm_final"]
    return (out["loss"], out["grad_x"], *[out["grad_" + n] for n in names], *[out["delta_" + n] for n in names],
            *[out["new_m_" + n] for n in names], *[out["new_v_" + n] for n in names])
```

```python
import functools

import jax
import jax.numpy as jnp
from jax import lax
from jax.experimental import pallas as pl
from jax.experimental.pallas import tpu as pltpu

F32 = jnp.float32
BF16 = jnp.bfloat16

D_MODEL = 1024
DEPTH = 2
N_CHIPS = 4
HEAD_DIM = 64
ROPE_DIM = 16
ROPE_THETA = 500000.0
DIL_GROUPS = ((128, 1), (512, 4), (2048, 16))
SPAN = 128
Q_BLOCK = 128
RMS_EPS = 1e-6
D_ATT = 256
COL_QS = 2304
COL_GD = 3072
COL_GS = 4096
ADAM_LR, ADAM_B1, ADAM_B2, ADAM_EPS, ADAM_WD, ADAM_STEP = 0.001, 0.9, 0.999, 1e-08, 0.01, 10

VMEM_LIMIT = 52 * 1024 * 1024
TM = 512
NEG = -1e30

NN = (((1,), (0,)), ((), ()))
NT = (((1,), (1,)), ((), ()))
TN = (((0,), (0,)), ((), ()))
MESH = pl.DeviceIdType.MESH

WEIGHT_NAMES = ("ffn1_w_gate", "ffn1_w_up", "ffn1_w_down", "w_in", "w_proj_dil",
                "w_proj_sb", "w_out", "ffn2_w_gate", "ffn2_w_up", "ffn2_w_down")
NORM_NAMES = ("norm_ffn1", "norm_mix", "norm_ffn2")


def _params(**kw):
    return pltpu.CompilerParams(vmem_limit_bytes=VMEM_LIMIT, **kw)


def _sigmoid(x):
    return 0.5 * jnp.tanh(0.5 * x) + 0.5


def _mm_body(pairs, n_in, n_out, n_acc, dims, nk, i_axis, epilogue, *refs):
    ins = refs[:n_in]
    outs = refs[n_in:n_in + n_out]
    accs = refs[n_in + n_out:]
    i = pl.program_id(i_axis)
    k = pl.program_id(2)

    def operand(a):
        return (a(ins) if callable(a) else ins[a][...]).astype(BF16)

    def dot(ia, ib):
        return lax.dot_general(operand(ia), operand(ib), dims, preferred_element_type=F32)

    if nk == 1:
        parts = [None] * n_acc
        for ia, ib, ic in pairs:
            parts[ic] = dot(ia, ib) if parts[ic] is None else parts[ic] + dot(ia, ib)
        epilogue(parts, ins, outs, i)
        return

    @pl.when(k == 0)
    def _():
        for c in range(n_acc):
            accs[c][...] = jnp.zeros_like(accs[c])

    for ia, ib, ic in pairs:
        accs[ic][...] += dot(ia, ib)

    @pl.when(k == nk - 1)
    def _():
        epilogue([a[...] for a in accs], ins, outs, i)


def _j_outer(spec):
    f = spec.index_map
    return pl.BlockSpec(spec.block_shape, lambda j, i, k: f(i, j, k))


def _mm(name, ins, in_specs, pairs, n_acc, acc_shape, dims, grid, epilogue, out_shapes, out_specs, j_outer=False):
    nk = grid[2]
    if j_outer:
        grid = (grid[1], grid[0], grid[2])
        in_specs = [_j_outer(s) for s in in_specs]
        out_specs = [_j_outer(s) for s in out_specs]
    scratch = [pltpu.VMEM(acc_shape, F32) for _ in range(n_acc)] if nk > 1 else []
    body = functools.partial(_mm_body, tuple(pairs), len(ins), len(out_shapes), n_acc, dims, nk,
                             1 if j_outer else 0, epilogue)
    return pl.pallas_call(
        body, name=name, grid=grid, in_specs=in_specs, out_specs=out_specs, out_shape=out_shapes,
        scratch_shapes=scratch,
        compiler_params=_params(dimension_semantics=("arbitrary", "arbitrary", "arbitrary")),
    )(*ins)


def _wspec(r, c, l, by):
    if by == 1:
        return pl.BlockSpec((None, None, r, c), lambda i, j, k: (j, l, 0, 0))
    return pl.BlockSpec((None, None, r, c), lambda i, j, k: (k, l, 0, 0))


def _rms_bwd_epilogue(x_idx, g_idx, dxo_idx):
    def ep(vals, ins, outs, i):
        dh = vals[0]
        x = ins[x_idx][...]
        g = ins[g_idx][...]
        rstd = lax.rsqrt(jnp.mean(x * x, axis=-1, keepdims=True) + RMS_EPS)
        xhat = x * rstd
        dxhat = dh * g
        dx = rstd * (dxhat - xhat * jnp.mean(dxhat * xhat, axis=-1, keepdims=True))
        outs[0][...] = ins[dxo_idx][...] + dx
        dg = jnp.broadcast_to(jnp.sum(dh * xhat, axis=0, keepdims=True), outs[1].shape)

        @pl.when(i == 0)
        def _():
            outs[1][...] = dg

        @pl.when(i > 0)
        def _():
            outs[1][...] += dg
    return ep


def _rms_fwd(x, gain):
    T = x.shape[0]

    def body(x_ref, g_ref, h_ref):
        xv = x_ref[...]
        h = xv * lax.rsqrt(jnp.mean(xv * xv, axis=-1, keepdims=True) + RMS_EPS)
        h_ref[...] = (h * g_ref[...]).astype(BF16)

    return pl.pallas_call(
        body, name="rms_fwd", grid=(T // TM,),
        in_specs=[pl.BlockSpec((TM, D_MODEL), lambda i: (i, 0)), pl.BlockSpec((1, D_MODEL), lambda i: (0, 0))],
        out_specs=pl.BlockSpec((TM, D_MODEL), lambda i: (i, 0)),
        out_shape=jax.ShapeDtypeStruct((T, D_MODEL), BF16), compiler_params=_params(),
    )(x, gain)


def _rope_tables(T):
    pos = jnp.arange(T, dtype=F32)
    inv_freq = ROPE_THETA ** (-jnp.arange(0, ROPE_DIM, 2, dtype=F32) / ROPE_DIM)
    ang = pos[:, None] * inv_freq[None, :]
    cos, sin = jnp.cos(ang), jnp.sin(ang)
    half = ROPE_DIM // 2
    one = jnp.ones((T, HEAD_DIM - ROPE_DIM), F32)
    zero = jnp.zeros((T, HEAD_DIM - ROPE_DIM), F32)
    zh = jnp.zeros((T, half), F32)
    c = jnp.concatenate([cos, cos, one], axis=1)
    s1 = jnp.concatenate([-sin, zh, zero], axis=1)
    s2 = jnp.concatenate([zh, sin, zero], axis=1)
    return tuple(jnp.concatenate([t, t], axis=1) for t in (c, s1, s2))


def _rope_fwd(xv, c, s1, s2):
    w = xv.shape[1]
    half = ROPE_DIM // 2
    return xv * c + pltpu.roll(xv, w - half, 1) * s1 + pltpu.roll(xv, half, 1) * s2


def _rope_bwd(dy, c, s1, s2):
    w = dy.shape[1]
    half = ROPE_DIM // 2
    return dy * c + pltpu.roll(dy * s1, half, 1) + pltpu.roll(dy * s2, w - half, 1)


def _assemble_dproj(dqk, rest, gates, tabs):
    T = gates[0].shape[0]
    n_qk, n_rest = len(dqk), len(rest)
    width = (n_qk + n_rest) * D_ATT + 2 * D_MODEL

    def body(*refs):
        ins, (c_ref, s1_ref, s2_ref), o_ref = refs[:n_qk + n_rest + 2], refs[-4:-1], refs[-1]
        c = jnp.concatenate([c_ref[...]] * 2, axis=1)
        s1 = jnp.concatenate([s1_ref[...]] * 2, axis=1)
        s2 = jnp.concatenate([s2_ref[...]] * 2, axis=1)
        for b in range(n_qk + n_rest):
            v = ins[b][...]
            if b < n_qk:
                v = _rope_bwd(v, c, s1, s2)
            o_ref[:, b * D_ATT:(b + 1) * D_ATT] = v.astype(BF16)
        off = (n_qk + n_rest) * D_ATT
        o_ref[:, off:off + D_MODEL] = ins[-2][...]
        o_ref[:, off + D_MODEL:] = ins[-1][...]

    att = pl.BlockSpec((TM, D_ATT), lambda i: (i, 0))
    wide = pl.BlockSpec((TM, D_MODEL), lambda i: (i, 0))
    tab = pl.BlockSpec((TM, 128), lambda i: (i, 0))
    return pl.pallas_call(
        body, name="assemble_dproj", grid=(T // TM,),
        in_specs=[att] * (n_qk + n_rest) + [wide, wide, tab, tab, tab],
        out_specs=pl.BlockSpec((TM, width), lambda i: (i, 0)),
        out_shape=jax.ShapeDtypeStruct((T, width), BF16), compiler_params=_params(),
    )(*dqk, *rest, *gates, *tabs)


def _dil_merge(os_, lses):
    T = os_[0].shape[0]

    def body(o0, o1, o2, l0, l1, l2, o_ref, lse_ref):
        a, b, c = l0[...], l1[...], l2[...]
        m = jnp.maximum(jnp.maximum(a, b), c)
        ea, eb, ec = jnp.exp(a - m), jnp.exp(b - m), jnp.exp(c - m)
        den = ea + eb + ec
        o_ref[...] = (ea * o0[...] + eb * o1[...] + ec * o2[...]) / den
        lse_ref[...] = m + jnp.log(den)

    blk = pl.BlockSpec((TM, D_ATT), lambda i: (i, 0))
    sh = jax.ShapeDtypeStruct((T, D_ATT), F32)
    return pl.pallas_call(
        body, name="dil_merge", grid=(T // TM,), in_specs=[blk] * 6, out_specs=[blk, blk],
        out_shape=[sh, sh], compiler_params=_params(),
    )(*os_, *lses)


def _final_loss(x, gain, target):
    T = x.shape[0]

    def body(x_ref, g_ref, t_ref, dx_ref, dg_ref, loss_ref):
        xv = x_ref[...]
        g = g_ref[...]
        rstd = lax.rsqrt(jnp.mean(xv * xv, axis=-1, keepdims=True) + RMS_EPS)
        xhat = xv * rstd
        err = xhat * g - t_ref[...]
        loss = 0.5 * jnp.sum(jnp.mean(err * err, axis=-1, keepdims=True), axis=0, keepdims=True)
        dy = err * (1.0 / D_MODEL)
        dxhat = dy * g
        dx_ref[...] = rstd * (dxhat - xhat * jnp.mean(dxhat * xhat, axis=-1, keepdims=True))
        dg = jnp.broadcast_to(jnp.sum(dy * xhat, axis=0, keepdims=True), dg_ref.shape)
        ls = jnp.broadcast_to(loss, loss_ref.shape)

        @pl.when(pl.program_id(0) == 0)
        def _():
            dg_ref[...] = dg
            loss_ref[...] = ls

        @pl.when(pl.program_id(0) > 0)
        def _():
            dg_ref[...] += dg
            loss_ref[...] += ls

    blk = pl.BlockSpec((TM, D_MODEL), lambda i: (i, 0))
    row = pl.BlockSpec((1, D_MODEL), lambda i: (0, 0))
    acc = pl.BlockSpec((8, D_MODEL), lambda i: (0, 0))
    return pl.pallas_call(
        body, name="final_loss", grid=(T // TM,), in_specs=[blk, row, blk], out_specs=[blk, acc, acc],
        out_shape=[jax.ShapeDtypeStruct((T, D_MODEL), F32), jax.ShapeDtypeStruct((8, D_MODEL), F32),
                   jax.ShapeDtypeStruct((8, D_MODEL), F32)],
        compiler_params=_params(dimension_semantics=("arbitrary",)),
    )(x, gain, target)


def _pair_masks():
    lane = lax.broadcasted_iota(jnp.int32, (SPAN, 128), 1)
    return [lane < HEAD_DIM, lane >= HEAD_DIM]


def _stack_heads(x, masks):
    return jnp.concatenate([jnp.where(m, x, 0.0) for m in masks], axis=0)


def _unstack_heads(y, masks):
    rows = y.shape[0] // len(masks)
    out = jnp.where(masks[0], y[:rows], 0.0)
    for h in range(1, len(masks)):
        out = out + jnp.where(masks[h], y[rows * h:rows * (h + 1)], 0.0)
    return out


DIL_PAIR = 2


def _dil_rows(idx, d):
    u = idx // d
    r = idx - u * d
    own = pl.ds(u * (SPAN * d) + r, SPAN, stride=d) if d > 1 else pl.ds(pl.multiple_of(u * SPAN, SPAN), SPAN)
    up = jnp.maximum(u - 1, 0)
    prev = pl.ds(up * (SPAN * d) + r, SPAN, stride=d) if d > 1 else pl.ds(pl.multiple_of(up * SPAN, SPAN), SPAN)
    return u, own, prev


def _dil_valid(u):
    qi = lax.broadcasted_iota(jnp.int32, (2 * SPAN, 2 * SPAN), 0) & (SPAN - 1)
    kj = lax.broadcasted_iota(jnp.int32, (2 * SPAN, 2 * SPAN), 1)
    in_prev = (kj < SPAN) & (kj >= qi + jnp.where(u > 0, 0, SPAN))
    return in_prev | ((kj >= SPAN) & (kj - SPAN <= qi))


def _dil_keys(ref, own, prev):
    return jnp.concatenate([ref[prev, :], ref[own, :]], axis=0).astype(BF16)


def _dil_fwd(proj, g, d):
    T = proj.shape[0]
    n_iter = T // SPAN

    def body(q_ref, k_ref, v_ref, o_ref, lse_ref):
        masks = _pair_masks()

        def step(pair, carry):
            its = [_dil_rows(DIL_PAIR * pair + e, d) for e in range(DIL_PAIR)]
            qs = [_stack_heads(q_ref[own, :] * (HEAD_DIM ** -0.5), masks).astype(BF16) for _, own, _ in its]
            kks = [_dil_keys(k_ref, own, prev) for _, own, prev in its]
            vvs = [_dil_keys(v_ref, own, prev) for _, own, prev in its]
            ss = [jnp.where(_dil_valid(u), lax.dot_general(q, kk, NT, preferred_element_type=F32), NEG)
                  for (u, _, _), q, kk in zip(its, qs, kks)]
            ms = [jnp.max(s, axis=1, keepdims=True) for s in ss]
            ps = [jnp.exp(s - m) for s, m in zip(ss, ms)]
            dens = [jnp.sum(p, axis=1, keepdims=True) for p in ps]
            pvs = [lax.dot_general(p.astype(BF16), vv, NN, preferred_element_type=F32) / den
                   for p, vv, den in zip(ps, vvs, dens)]
            for (_, own, _), pv, m, den in zip(its, pvs, ms, dens):
                o_ref[own, :] = _unstack_heads(pv, masks)
                lse_ref[own, :] = _unstack_heads(jnp.broadcast_to(m + jnp.log(den), pv.shape), masks)
            return carry

        lax.fori_loop(0, n_iter // DIL_PAIR, step, 0)

    def col(b):
        return pl.BlockSpec((T, 128), lambda p: (0, b + p))

    sh = jax.ShapeDtypeStruct((T, D_ATT), F32)
    out = pl.BlockSpec((T, 128), lambda p: (0, p))
    return pl.pallas_call(
        body, name=f"dil_fwd_d{d}", grid=(2,),
        in_specs=[col(2 * g), col(6 + 2 * g), col(12 + 2 * g)], out_specs=[out, out], out_shape=[sh, sh],
        compiler_params=_params(dimension_semantics=("arbitrary",)),
    )(proj, proj, proj)


def _dil_bwd(proj, do, o_dil, lse, g, d):
    T = proj.shape[0]
    n_iter = T // SPAN

    def body(q_ref, k_ref, v_ref, do_ref, o_ref, lse_ref, dq_ref, dk_ref, dv_ref):
        masks = _pair_masks()
        head_lanes = jnp.concatenate(masks, axis=0)

        def step(pair, carry):
            its = [_dil_rows(DIL_PAIR * pair + e, d) for e in range(DIL_PAIR)]
            qs = [_stack_heads(q_ref[own, :] * (HEAD_DIM ** -0.5), masks).astype(BF16) for _, own, _ in its]
            kks = [_dil_keys(k_ref, own, prev) for _, own, prev in its]
            vvs = [_dil_keys(v_ref, own, prev) for _, own, prev in its]
            doms = [_stack_heads(do_ref[own, :], masks) for _, own, _ in its]
            dos = [dom.astype(BF16) for dom in doms]
            deltas = [jnp.sum(dom * jnp.concatenate([o_ref[own, :]] * 2, axis=0), axis=1, keepdims=True)
                      for dom, (_, own, _) in zip(doms, its)]
            lrows = [jnp.max(jnp.where(head_lanes, jnp.concatenate([lse_ref[own, :]] * 2, axis=0), NEG),
                             axis=1, keepdims=True) for _, own, _ in its]
            ss = [lax.dot_general(q, kk, NT, preferred_element_type=F32) for q, kk in zip(qs, kks)]
            dps = [lax.dot_general(do_b, vv, NT, preferred_element_type=F32) for do_b, vv in zip(dos, vvs)]
            ps = [jnp.where(_dil_valid(u), jnp.exp(s - lrow), 0.0) for (u, _, _), s, lrow in zip(its, ss, lrows)]
            dss = [(p * (dp - delta)).astype(BF16) for p, dp, delta in zip(ps, dps, deltas)]
            dqs = [lax.dot_general(ds, kk, NN, preferred_element_type=F32) for ds, kk in zip(dss, kks)]
            dkks = [lax.dot_general(ds, q, TN, preferred_element_type=F32) for ds, q in zip(dss, qs)]
            dvvs = [lax.dot_general(p.astype(BF16), do_b, TN, preferred_element_type=F32) for p, do_b in zip(ps, dos)]
            for (_, own, prev), dq, dkk, dvv in zip(its, dqs, dkks, dvvs):
                dq_ref[own, :] = _unstack_heads(dq, masks) * (HEAD_DIM ** -0.5)
                dk_ref[own, :] = dkk[SPAN:]
                dv_ref[own, :] = dvv[SPAN:]
                dk_ref[prev, :] = dk_ref[prev, :] + dkk[:SPAN]
                dv_ref[prev, :] = dv_ref[prev, :] + dvv[:SPAN]
            return carry

        lax.fori_loop(0, n_iter // DIL_PAIR, step, 0)

    def col(b):
        return pl.BlockSpec((T, 128), lambda p: (0, b + p))

    sh = jax.ShapeDtypeStruct((T, D_ATT), F32)
    return pl.pallas_call(
        body, name=f"dil_bwd_d{d}", grid=(2,),
        in_specs=[col(2 * g), col(6 + 2 * g), col(12 + 2 * g), col(0), col(0), col(0)],
        out_specs=[col(0), col(0), col(0)], out_shape=[sh, sh, sh],
        compiler_params=_params(dimension_semantics=("arbitrary",)),
    )(proj, proj, proj, do, o_dil, lse)


SB_KT = 512


def _sb_tri(strict):
    a = lax.broadcasted_iota(jnp.int32, (Q_BLOCK, Q_BLOCK), 0)
    b = lax.broadcasted_iota(jnp.int32, (Q_BLOCK, Q_BLOCK), 1)
    return jnp.where((a > b) if strict else (a >= b), 1.0, 0.0).astype(BF16)


def _split_stack(x):
    nb = x.shape[1] // Q_BLOCK
    blocks = [x[:, Q_BLOCK * b:Q_BLOCK * (b + 1)] for b in range(nb)]
    hi = [b.astype(BF16) for b in blocks]
    lo = [(b - h.astype(F32)).astype(BF16) for b, h in zip(blocks, hi)]
    return blocks, jnp.concatenate(hi + lo, axis=0)


def _suffix_from(y, blocks, c):
    r = blocks[0].shape[0]
    nb = len(blocks)
    outs = [None] * nb
    run = c
    for b in reversed(range(nb)):
        outs[b] = run + y[r * b:r * (b + 1)] + y[r * (nb + b):r * (nb + b + 1)]
        run = run + jnp.sum(blocks[b], axis=1, keepdims=True)
    return jnp.concatenate(outs, axis=1), run


SB_HEADS = D_ATT // HEAD_DIM
SB_CHAINS = 2
SB_ROWS = SB_HEADS // SB_CHAINS * Q_BLOCK


def _sb_past(i, t):
    row = lax.broadcasted_iota(jnp.int32, (SB_ROWS, SB_KT), 0) & (Q_BLOCK - 1)
    col = lax.broadcasted_iota(jnp.int32, (SB_ROWS, SB_KT), 1)
    return col + t * SB_KT < row + i * Q_BLOCK


def _sb_head_masks():
    lane = lax.broadcasted_iota(jnp.int32, (Q_BLOCK, D_ATT), 1)
    masks = [(lane >= HEAD_DIM * h) & (lane < HEAD_DIM * (h + 1)) for h in range(SB_HEADS)]
    per = SB_HEADS // SB_CHAINS
    return [masks[per * g:per * (g + 1)] for g in range(SB_CHAINS)]


def _sb_rows(t):
    return pl.ds(pl.multiple_of(t * SB_KT, SB_KT), SB_KT)


def _sb_log_terms(z, past):
    lsz = jnp.minimum(z, 0.0) - jnp.log(1.0 + jnp.exp(-jnp.abs(z)))
    lk = lsz - z
    return lsz, (lk if past is None else jnp.where(past, lk, 0.0))


def _sb_weights(lsz, after, past):
    w = jnp.exp(lsz + after)
    return w if past is None else jnp.where(past, w, 0.0)


def _sb_fwd(proj):
    T = proj.shape[0]

    def body(q_ref, k_ref, v_ref, o_ref):
        i = pl.program_id(0)
        masks = _sb_head_masks()
        tri = _sb_tri(True)
        q = q_ref[...] * (HEAD_DIM ** -0.5)
        qs = [_stack_heads(q, m).astype(BF16) for m in masks]
        n_tiles = (i * Q_BLOCK) // SB_KT + 1

        def tile(t, carry, masked):
            kb = k_ref[_sb_rows(t), :].astype(BF16)
            vb = v_ref[_sb_rows(t), :].astype(BF16)
            past = _sb_past(i, t) if masked else None
            acc, cs = carry[0], carry[1:]
            zs = [lax.dot_general(g, kb, NT, preferred_element_type=F32) for g in qs]
            logs = [_sb_log_terms(z, past) for z in zs]
            splits = [_split_stack(lk) for _, lk in logs]
            ys = [lax.dot_general(x, tri, NN, preferred_element_type=F32) for _, x in splits]
            sums = [_suffix_from(y, blocks, c) for y, (blocks, _), c in zip(ys, splits, cs)]
            ws = [_sb_weights(lsz, after, past).astype(BF16) for (lsz, _), (after, _) in zip(logs, sums)]
            for m, w in zip(masks, ws):
                acc = acc + _unstack_heads(lax.dot_general(w, vb, NN, preferred_element_type=F32), m)
            return (acc, *[c for _, c in sums])

        zcol = jnp.zeros((SB_ROWS, 1), F32)
        carry = tile(n_tiles - 1, (jnp.zeros((Q_BLOCK, D_ATT), F32),) + (zcol,) * SB_CHAINS, True)
        carry = lax.fori_loop(0, n_tiles - 1, lambda tt, cr: tile(n_tiles - 2 - tt, cr, False), carry)
        o_ref[...] = carry[0]

    cb = COL_QS // D_ATT
    return pl.pallas_call(
        body, name="sb_fwd", grid=(T // Q_BLOCK,),
        in_specs=[pl.BlockSpec((Q_BLOCK, D_ATT), lambda i: (i, cb)),
                  pl.BlockSpec((T, D_ATT), lambda i: (0, cb + 1)),
                  pl.BlockSpec((T, D_ATT), lambda i: (0, cb + 2))],
        out_specs=pl.BlockSpec((Q_BLOCK, D_ATT), lambda i: (i, 0)),
        out_shape=jax.ShapeDtypeStruct((T, D_ATT), F32),
        compiler_params=_params(dimension_semantics=("arbitrary",)),
    )(proj, proj, proj)


def _sb_bwd(proj, do, o):
    T = proj.shape[0]

    def body(q_ref, k_ref, v_ref, do_ref, o_ref, dq_ref, dk_ref, dv_ref):
        i = pl.program_id(0)
        masks = _sb_head_masks()
        tri = _sb_tri(True)
        tri_incl = _sb_tri(False)

        @pl.when(i == 0)
        def _():
            dk_ref[...] = jnp.zeros_like(dk_ref)
            dv_ref[...] = jnp.zeros_like(dv_ref)

        q = q_ref[...] * (HEAD_DIM ** -0.5)
        qs = [_stack_heads(q, m).astype(BF16) for m in masks]
        dos = [_stack_heads(do_ref[...], m).astype(BF16) for m in masks]
        o_rep = jnp.concatenate([o_ref[...]] * (SB_HEADS // SB_CHAINS), axis=0)
        deltas = [jnp.sum(d.astype(F32) * o_rep, axis=1, keepdims=True) for d in dos]
        n_tiles = (i * Q_BLOCK) // SB_KT + 1

        def tile(t, carry, masked):
            rows = _sb_rows(t)
            kb = k_ref[rows, :].astype(BF16)
            vb = v_ref[rows, :].astype(BF16)
            past = _sb_past(i, t) if masked else None
            dq, cs, ces = carry[0], carry[1:1 + SB_CHAINS], carry[1 + SB_CHAINS:]
            zs = [lax.dot_general(g, kb, NT, preferred_element_type=F32) for g in qs]
            gvs = [lax.dot_general(d, vb, NT, preferred_element_type=F32) for d in dos]
            logs = [_sb_log_terms(z, past) for z in zs]
            splits = [_split_stack(lk) for _, lk in logs]
            ys = [lax.dot_general(x, tri, NN, preferred_element_type=F32) for _, x in splits]
            sums = [_suffix_from(y, blocks, c) for y, (blocks, _), c in zip(ys, splits, cs)]
            wbs = [_sb_weights(lsz, after, past).astype(BF16) for (lsz, _), (after, _) in zip(logs, sums)]
            es = [wb.astype(F32) * gv for wb, gv in zip(wbs, gvs)]
            esplits = [_split_stack(e) for e in es]
            eys = [lax.dot_general(x, tri_incl, NN, preferred_element_type=F32) for _, x in esplits]
            esums = [_suffix_from(y, blocks, ce) for y, (blocks, _), ce in zip(eys, esplits, ces)]
            dzbs = []
            for e, (lsz, lk), (suf, _), delta in zip(es, logs, esums, deltas):
                dz = e * jnp.exp(lk) - (delta - suf) * jnp.exp(lsz)
                dzbs.append((dz if past is None else jnp.where(past, dz, 0.0)).astype(BF16))
            dk_t = dv_t = None
            for m, dzb, wb, g, d in zip(masks, dzbs, wbs, qs, dos):
                dq = dq + _unstack_heads(lax.dot_general(dzb, kb, NN, preferred_element_type=F32), m)
                a = lax.dot_general(dzb, g, TN, preferred_element_type=F32)
                b = lax.dot_general(wb, d, TN, preferred_element_type=F32)
                dk_t = a if dk_t is None else dk_t + a
                dv_t = b if dv_t is None else dv_t + b
            dk_ref[rows, :] = dk_ref[rows, :] + dk_t
            dv_ref[rows, :] = dv_ref[rows, :] + dv_t
            return (dq, *[c for _, c in sums], *[c for _, c in esums])

        zcol = jnp.zeros((SB_ROWS, 1), F32)
        carry = tile(n_tiles - 1, (jnp.zeros((Q_BLOCK, D_ATT), F32),) + (zcol,) * (2 * SB_CHAINS), True)
        carry = lax.fori_loop(0, n_tiles - 1, lambda tt, cr: tile(n_tiles - 2 - tt, cr, False), carry)
        dq_ref[...] = carry[0] * (HEAD_DIM ** -0.5)

    cb = COL_QS // D_ATT
    blk = pl.BlockSpec((Q_BLOCK, D_ATT), lambda i: (i, 0))
    full = pl.BlockSpec((T, D_ATT), lambda i: (0, 0))
    sh = jax.ShapeDtypeStruct((T, D_ATT), F32)
    return pl.pallas_call(
        body, name="sb_bwd", grid=(T // Q_BLOCK,),
        in_specs=[pl.BlockSpec((Q_BLOCK, D_ATT), lambda i: (i, cb)),
                  pl.BlockSpec((T, D_ATT), lambda i: (0, cb + 1)),
                  pl.BlockSpec((T, D_ATT), lambda i: (0, cb + 2)), blk, blk],
        out_specs=[blk, full, full], out_shape=[sh, sh, sh],
        compiler_params=_params(dimension_semantics=("arbitrary",)),
    )(proj, proj, proj, do, o)


def _tok(c, by=None):
    if by is None:
        return pl.BlockSpec((TM, c), lambda i, j, k: (i, 0))
    if by == 1:
        return pl.BlockSpec((TM, c), lambda i, j, k: (i, j))
    return pl.BlockSpec((TM, c), lambda i, j, k: (i, k))


def _chunked(c, by):
    if by == 1:
        return pl.BlockSpec((None, TM, c), lambda i, j, k: (j, i, 0))
    return pl.BlockSpec((None, TM, c), lambda i, j, k: (k, i, 0))


def _gain_spec():
    return pl.BlockSpec((1, D_MODEL), lambda i, j, k: (0, 0))


def _all_chunks(rows, c):
    return pl.BlockSpec((N_CHIPS, rows, c), lambda i, j, k: (0, i, 0))


def _wfull(r, c, l):
    return pl.BlockSpec((N_CHIPS, None, r, c), lambda i, j, k: (0, l, 0, 0))


def _pick(idx, c):
    return lambda ins: ins[idx][c]


def _cols(idx, c, w):
    return lambda ins: ins[idx][:, c * w:(c + 1) * w]


def _rows(rows, width):
    return pl.BlockSpec((rows, width), lambda i, j, k: (i, 0))


def _whole(shape):
    return pl.BlockSpec(shape, lambda i, j, k: (0, 0))


def _ffn_fwd(x, gain, wg, wu, wd):
    T = x.shape[0]
    wg, wu, wd = (w.reshape(-1, D_MODEL) for w in (wg, wu, wd))
    ff = wd.shape[0]
    tm = TM // 2
    h = _rms_fwd(x, gain)

    def swiglu(vals, ins, outs, i):
        gt, up = vals
        s = _sigmoid(gt)
        sil = gt * s
        outs[0][...] = sil.astype(BF16)
        outs[1][...] = (up * (s * (1.0 + gt * (1.0 - s)))).astype(BF16)
        outs[2][...] = (sil * up).astype(BF16)

    ash = jax.ShapeDtypeStruct((T, ff), BF16)
    sil, up_dsil, act = _mm(
        "ffn_up", [h, wg, wu], [_rows(tm, D_MODEL), _whole(wg.shape), _whole(wu.shape)],
        [(0, 1, 0), (0, 2, 1)], 2, None, NT, (T // tm, 1, 1), swiglu, [ash] * 3, [_rows(tm, ff)] * 3)

    def resid(vals, ins, outs, i):
        outs[0][...] = ins[2][...] + 0.5 * vals[0]

    (y,) = _mm(
        "ffn_down", [act, wd, x], [_rows(TM, ff), _whole(wd.shape), _tok(D_MODEL)], [(0, 1, 0)], 1, None, NN,
        (T // TM, 1, 1), resid, [jax.ShapeDtypeStruct((T, D_MODEL), F32)], [_tok(D_MODEL)])
    return y, (x, h, sil, up_dsil, act)


def _ffn_bwd(dxo, gain, wg, wu, wd, saved):
    x, h, sil, up_dsil, act = saved
    T = x.shape[0]
    n_chips, _, ffs, _ = wd.shape
    wg, wu, wd = (w.reshape(-1, D_MODEL) for w in (wg, wu, wd))
    ff = wd.shape[0]
    tk = TM
    tm = TM // 2

    def dswiglu(vals, ins, outs, i):
        da = 0.5 * vals[0]
        outs[0][...] = (da * ins[3][...].astype(F32)).astype(BF16)
        outs[1][...] = (da * ins[2][...].astype(F32)).astype(BF16)

    ash = jax.ShapeDtypeStruct((T, ff), BF16)
    dgate, dup = _mm(
        "ffn_dact", [dxo, wd, sil, up_dsil], [_rows(tm, D_MODEL), _whole(wd.shape), _rows(tm, ff), _rows(tm, ff)],
        [(0, 1, 0)], 1, None, NT, (T // tm, 1, 1), dswiglu, [ash, ash], [_rows(tm, ff)] * 2)

    def half(vals, ins, outs, i):
        outs[0][...] = (0.5 * vals[0]).astype(BF16)

    def cast(vals, ins, outs, i):
        outs[0][...] = vals[0].astype(BF16)

    tok_k = pl.BlockSpec((tk, D_MODEL), lambda i, j, k: (k, 0))
    hid_k = pl.BlockSpec((tk, ff), lambda i, j, k: (k, 0))
    wsh = jax.ShapeDtypeStruct((ff, D_MODEL), BF16)
    (dwd,) = _mm("ffn_dwd", [act, dxo], [hid_k, tok_k], [(0, 1, 0)], 1, (ff, D_MODEL), TN, (1, 1, T // tk), half,
                 [wsh], [_whole((ff, D_MODEL))])

    dx, dgain = _mm(
        "ffn_dx", [dgate, dup, wg, wu, x, gain, dxo],
        [_rows(tm, ff), _rows(tm, ff), _whole(wg.shape), _whole(wu.shape), _rows(tm, D_MODEL), _gain_spec(),
         _rows(tm, D_MODEL)],
        [(0, 2, 0), (1, 3, 0)], 1, None, NN, (T // tm, 1, 1), _rms_bwd_epilogue(4, 5, 6),
        [jax.ShapeDtypeStruct((T, D_MODEL), F32), jax.ShapeDtypeStruct((8, D_MODEL), F32)],
        [_rows(tm, D_MODEL), pl.BlockSpec((8, D_MODEL), lambda i, j, k: (0, 0))])

    dws = []
    for dact in (dgate, dup):
        dws += _mm("ffn_dwgu", [dact, h], [hid_k, tok_k], [(0, 1, 0)], 1, (ff, D_MODEL), TN, (1, 1, T // tk), cast,
                   [wsh], [_whole((ff, D_MODEL))])
    dwg, dwu, dwd = (w.reshape(n_chips, ffs, D_MODEL) for w in (dws[0], dws[1], dwd))
    return dx, dgain, dwg, dwu, dwd


def _joined_mixer_weights(wpd, wps, wo):
    n, _, r, c = wpd.shape
    wpd_n, wps_n = (w[:, 0].transpose(1, 0, 2).reshape(r, n * c) for w in (wpd, wps))
    return wpd_n, wps_n, wo.reshape(-1, wo.shape[3])


def _mixer_fwd(x, gain, W, l, tabs):
    T = x.shape[0]
    win, wpd, wps, wo = W["w_in"], W["w_proj_dil"], W["w_proj_sb"], W["w_out"]
    cin = win.shape[3]
    cp = wpd.shape[3]
    h = _rms_fwd(x, gain)

    n_rope = 6 * D_ATT

    tm = TM // 2

    def roped(vals, ins, outs, i):
        for j, v in enumerate(vals):
            lo = j * cin
            k = min(max(n_rope - lo, 0), cin)
            if k:
                tab = [jnp.concatenate([ins[t][...]] * (k // 128), axis=1) for t in (2, 3, 4)]
                outs[0][:, lo:lo + k] = _rope_fwd(v[:, :k], *tab)
            if k < cin:
                outs[0][:, lo + k:lo + cin] = v[:, k:]

    (proj,) = _mm(
        "mix_in", [h, win, *tabs], [_rows(tm, D_MODEL), _wfull(D_MODEL, cin, l)] + [_rows(tm, 128)] * 3,
        [(0, _pick(1, c), c) for c in range(N_CHIPS)], N_CHIPS, None, NN, (T // tm, 1, 1), roped,
        [jax.ShapeDtypeStruct((T, N_CHIPS * cin), F32)], [_rows(tm, N_CHIPS * cin)])

    os_, lses = [], []
    for g, (window, dil) in enumerate(DIL_GROUPS):
        o_g, lse_g = _dil_fwd(proj, g, dil)
        os_.append(o_g)
        lses.append(lse_g)
    o_dil, lse = _dil_merge(os_, lses)
    o_sb = _sb_fwd(proj)

    def gated(vals, ins, outs, i):
        pd, ps = vals
        outs[0][...] = (_sigmoid(ins[4][...]) * pd + _sigmoid(ins[5][...]) * ps).astype(BF16)
        outs[1][...] = pd.astype(BF16)
        outs[2][...] = ps.astype(BF16)

    wpd_n, wps_n, wo_n = _joined_mixer_weights(wpd, wps, wo)
    gd_spec = pl.BlockSpec((TM, D_MODEL), lambda i, j, k: (i, COL_GD // D_MODEL))
    gs_spec = pl.BlockSpec((TM, D_MODEL), lambda i, j, k: (i, COL_GS // D_MODEL))
    ush = jax.ShapeDtypeStruct((T, D_MODEL), BF16)
    u, pd, ps = _mm(
        "mix_gate", [o_dil, o_sb, wpd_n, wps_n, proj, proj],
        [_tok(D_ATT), _tok(D_ATT), _whole(wpd_n.shape), _whole(wps_n.shape), gd_spec, gs_spec],
        [(0, 2, 0), (1, 3, 1)], 2, None, NN, (T // TM, 1, 1), gated, [ush] * 3, [_tok(D_MODEL)] * 3)

    def resid(vals, ins, outs, i):
        outs[0][...] = ins[2][...] + vals[0]

    (y,) = _mm(
        "mix_out", [u, wo_n, x], [_tok(D_MODEL), _whole(wo_n.shape), _tok(D_MODEL)], [(0, 1, 0)], 1, None, NN,
        (T // TM, 1, 1), resid, [jax.ShapeDtypeStruct((T, D_MODEL), F32)], [_tok(D_MODEL)])
    return y, (x, h, proj, o_dil, lse, o_sb, u, pd, ps)


def _mixer_bwd(dxo, gain, W, l, tabs, saved):
    x, h, proj, o_dil, lse, o_sb, u, pd, ps = saved
    T = x.shape[0]
    win, wpd, wps, wo = W["w_in"], W["w_proj_dil"], W["w_proj_sb"], W["w_out"]
    cin = win.shape[3]
    cp = wpd.shape[3]
    tk = TM
    tm = TM // 2
    row = pl.BlockSpec((tm, D_MODEL), lambda i, j, k: (i, 0))

    def dgated(vals, ins, outs, i):
        du = vals[0]
        sd = _sigmoid(ins[4][...])
        ss = _sigmoid(ins[5][...])
        outs[0][...] = (du * sd).astype(BF16)
        outs[1][...] = (du * ss).astype(BF16)
        outs[2][...] = (du * ins[2][...].astype(F32) * sd * (1.0 - sd)).astype(BF16)
        outs[3][...] = (du * ins[3][...].astype(F32) * ss * (1.0 - ss)).astype(BF16)

    wpd_n, wps_n, wo_n = _joined_mixer_weights(wpd, wps, wo)
    gd_spec = pl.BlockSpec((TM, D_MODEL), lambda i, j, k: (i, COL_GD // D_MODEL))
    gs_spec = pl.BlockSpec((TM, D_MODEL), lambda i, j, k: (i, COL_GS // D_MODEL))
    ush = jax.ShapeDtypeStruct((T, D_MODEL), BF16)
    dpd, dps, dgd, dgs = _mm(
        "mix_du", [dxo, wo_n, pd, ps, proj, proj],
        [_tok(D_MODEL), _whole(wo_n.shape), _tok(D_MODEL), _tok(D_MODEL), gd_spec, gs_spec],
        [(0, 1, 0)], 1, None, NT, (T // TM, 1, 1), dgated, [ush] * 4, [_tok(D_MODEL)] * 4)

    def one(vals, ins, outs, i):
        outs[0][...] = vals[0].astype(BF16)

    def two(vals, ins, outs, i):
        outs[0][...] = vals[0].astype(BF16)
        outs[1][...] = vals[1].astype(BF16)

    tok_k = pl.BlockSpec((tk, D_MODEL), lambda i, j, k: (k, 0))
    att_k = pl.BlockSpec((tk, D_ATT), lambda i, j, k: (k, 0))
    (dwo_n,) = _mm("mix_dwo", [u, dxo], [tok_k, tok_k], [(0, 1, 0)], 1, (D_MODEL, D_MODEL), TN, (1, 1, T // tk), one,
                   [jax.ShapeDtypeStruct((D_MODEL, D_MODEL), BF16)], [_whole((D_MODEL, D_MODEL))])

    def plain2(vals, ins, outs, i):
        outs[0][...] = vals[0]
        outs[1][...] = vals[1]

    ash = jax.ShapeDtypeStruct((T, D_ATT), F32)
    do_dil, do_sb = _mm(
        "mix_do", [dpd, dps, wpd_n, wps_n], [_tok(D_MODEL), _tok(D_MODEL), _whole(wpd_n.shape), _whole(wps_n.shape)],
        [(0, 2, 0), (1, 3, 1)], 2, None, NT, (T // TM, 1, 1), plain2, [ash, ash], [_tok(D_ATT)] * 2)

    psh = jax.ShapeDtypeStruct((D_ATT, D_MODEL), BF16)
    dwpd_n, dwps_n = _mm(
        "mix_dwp", [o_dil, o_sb, dpd, dps], [att_k, att_k, tok_k, tok_k], [(0, 2, 0), (1, 3, 1)], 2,
        (D_ATT, D_MODEL), TN, (1, 1, T // tk), two, [psh, psh], [_whole((D_ATT, D_MODEL))] * 2)
    dwpd, dwps = (w.reshape(D_ATT, N_CHIPS, cp).transpose(1, 0, 2) for w in (dwpd_n, dwps_n))
    dwo = dwo_n.reshape(N_CHIPS, cp, D_MODEL)

    dqs, dks, dvs = [], [], []
    for g, (window, dil) in enumerate(DIL_GROUPS):
        dq, dk, dv = _dil_bwd(proj, do_dil, o_dil, lse, g, dil)
        dqs.append(dq)
        dks.append(dk)
        dvs.append(dv)
    dq_s, dk_s, dv_s = _sb_bwd(proj, do_sb, o_sb)
    dproj = _assemble_dproj(dqs + dks, dvs + [dq_s, dk_s, dv_s], [dgd, dgs], tabs)

    dx, dgain = _mm(
        "mix_dx", [dproj, win, x, gain, dxo],
        [pl.BlockSpec((tm, N_CHIPS * cin), lambda i, j, k: (i, 0)), _wfull(D_MODEL, cin, l), row, _gain_spec(), row],
        [(_cols(0, c, cin), _pick(1, c), 0) for c in range(N_CHIPS)], 1, None, NT, (T // tm, 1, 1),
        _rms_bwd_epilogue(2, 3, 4),
        [jax.ShapeDtypeStruct((T, D_MODEL), F32), jax.ShapeDtypeStruct((8, D_MODEL), F32)],
        [row, pl.BlockSpec((8, D_MODEL), lambda i, j, k: (0, 0))])

    (dwin,) = _mm(
        "mix_dwin", [h, dproj],
        [pl.BlockSpec((tk, D_MODEL), lambda i, j, k: (k, 0)), pl.BlockSpec((tk, cin), lambda i, j, k: (k, j))],
        [(0, 1, 0)], 1, (D_MODEL, cin), TN, (1, N_CHIPS, T // tk), one,
        [jax.ShapeDtypeStruct((N_CHIPS, D_MODEL, cin), BF16)],
        [pl.BlockSpec((None, D_MODEL, cin), lambda i, j, k: (j, 0, 0))])
    return dx, dgain, dwin, dwpd, dwps, dwo


def _local_step(x, target, norms, norm_final, weights_of, on_grads):
    T = x.shape[0]
    tabs = _rope_tables(T)
    saved, held = [], []
    for l in range(DEPTH):
        w1 = weights_of(l, 0, x)
        x, s1 = _ffn_fwd(x, norms["norm_ffn1"][l:l + 1], w1["ffn1_w_gate"], w1["ffn1_w_up"], w1["ffn1_w_down"])
        w2 = weights_of(l, 1, x)
        x, s2 = _mixer_fwd(x, norms["norm_mix"][l:l + 1], w2, 0, tabs)
        w3 = weights_of(l, 2, x)
        x, s3 = _ffn_fwd(x, norms["norm_ffn2"][l:l + 1], w3["ffn2_w_gate"], w3["ffn2_w_up"], w3["ffn2_w_down"])
        saved.append((s1, s2, s3))
        held.append((w1, w2, w3))
    dx, dg_final, loss = _final_loss(x, norm_final.reshape(1, D_MODEL), target)
    gains = [None] * DEPTH
    for l in reversed(range(DEPTH)):
        s1, s2, s3 = saved[l]
        w1, w2, w3 = held[l]
        dx, dg2, dwg2, dwu2, dwd2 = _ffn_bwd(dx, norms["norm_ffn2"][l:l + 1], w3["ffn2_w_gate"], w3["ffn2_w_up"],
                                             w3["ffn2_w_down"], s3)
        dx = on_grads(l, 2, dict(ffn2_w_gate=dwg2, ffn2_w_up=dwu2, ffn2_w_down=dwd2), dx)
        dx, dgm, dwin, dwpd, dwps, dwo = _mixer_bwd(dx, norms["norm_mix"][l:l + 1], w2, 0, tabs, s2)
        dx = on_grads(l, 1, dict(w_in=dwin, w_proj_dil=dwpd, w_proj_sb=dwps, w_out=dwo), dx)
        dx, dg1, dwg1, dwu1, dwd1 = _ffn_bwd(dx, norms["norm_ffn1"][l:l + 1], w1["ffn1_w_gate"], w1["ffn1_w_up"],
                                             w1["ffn1_w_down"], s1)
        dx = on_grads(l, 0, dict(ffn1_w_gate=dwg1, ffn1_w_up=dwu1, ffn1_w_down=dwd1), dx)
        gains[l] = dict(norm_ffn1=dg1, norm_mix=dgm, norm_ffn2=dg2)
    return loss, dx, gains, dg_final


def _place():
    x, y, c = lax.axis_index("x"), lax.axis_index("y"), lax.axis_index("c")
    chips = [(1 - x, y), (x, 1 - y), (1 - x, 1 - y)]
    return x, y, c, chips


def _half(c, r):
    return pl.ds(pl.multiple_of(c * (r // 2), 8), r // 2)


def _cast_into_slot(ws, ls, me_arr, after):
    n = len(ws)
    late = [] if after is None else [after]

    def body(me_ref, *refs):
        for a in range(n):
            refs[len(refs) - n + a][...] = refs[a][...].astype(BF16)

    def src(w, l):
        return pl.BlockSpec((None, w.shape[1] // 4, w.shape[2]), lambda i, me: (l, i, 0))

    def dst(w):
        return pl.BlockSpec((None, None, w.shape[1] // 4, w.shape[2]), lambda i, me: (me[0], 0, i, 0))

    return pl.pallas_call(
        body, name="cast_weights",
        grid_spec=pltpu.PrefetchScalarGridSpec(
            num_scalar_prefetch=1, grid=(4,),
            in_specs=[src(w, l) for w, l in zip(ws, ls)] + [pl.BlockSpec(memory_space=pl.ANY)] * len(late),
            out_specs=[dst(w) for w in ws]),
        out_shape=[jax.ShapeDtypeStruct((N_CHIPS, 1) + w.shape[1:], BF16) for w in ws], compiler_params=_params(),
    )(me_arr, *ws, *late)


HBM_SPEC = pl.BlockSpec(memory_space=pltpu.HBM)
SEM_SPEC = pl.BlockSpec(memory_space=pltpu.SEMAPHORE)
SPLIT_COPY = pltpu.CompilerParams(has_side_effects=pltpu.SideEffectType.DATAFLOW_SIDE_EFFECTING)


def _gather_piece(ref, chip_id, c):
    return ref.at[chip_id, 0, _half(c, ref.shape[2]), :]


def _gather_start(tag, bufs):
    n = len(bufs)

    def body(*refs):
        out_refs = refs[n:2 * n]
        send_sems, recv_sems, token = refs[2 * n:]
        x, y, c, chips = _place()
        me = 2 * x + y
        for a in range(n):
            piece = _gather_piece(out_refs[a], me, c)
            for j, chip in enumerate(chips):
                pltpu.make_async_remote_copy(
                    src_ref=piece, dst_ref=piece, send_sem=send_sems.at[3 * a + j], recv_sem=recv_sems.at[3 * a + j],
                    device_id=(*chip, c), device_id_type=MESH).start()
        token[...] = jnp.zeros_like(token)

    outs = pl.pallas_call(
        body, name=f"gather_start_{tag}", in_specs=[HBM_SPEC] * n,
        out_specs=[HBM_SPEC] * n + [SEM_SPEC, SEM_SPEC, pl.BlockSpec(memory_space=pltpu.VMEM)],
        out_shape=[pltpu.HBM(b.shape, b.dtype) for b in bufs] + [pltpu.SemaphoreType.DMA((3 * n,))] * 2
        + [jax.ShapeDtypeStruct((8, 128), F32)],
        input_output_aliases={a: a for a in range(n)}, compiler_params=SPLIT_COPY,
    )(*[pltpu.with_memory_space_constraint(b, pltpu.HBM) for b in bufs])
    return outs[:n], outs[n], outs[n + 1], outs[n + 2]


def _gather_wait(k, bufs, places, send_sems, recv_sems, after):
    m = len(bufs)

    def body(*refs):
        in_refs = refs[:m]
        ssem, rsem = refs[m], refs[m + 1]
        x, y, c, chips = _place()
        me = 2 * x + y
        for t, a in enumerate(places):
            for j, chip in enumerate(chips):
                cp = pltpu.make_async_remote_copy(
                    src_ref=_gather_piece(in_refs[t], me, c),
                    dst_ref=_gather_piece(in_refs[t], 2 * chip[0] + chip[1], c),
                    send_sem=ssem.at[3 * a + j], recv_sem=rsem.at[3 * a + j], device_id=(*chip, c),
                    device_id_type=MESH)
                cp.wait_send()
                cp.wait_recv()

    return pl.pallas_call(
        body, name=f"gather_wait_{k}",
        in_specs=[HBM_SPEC] * m + [SEM_SPEC, SEM_SPEC, pl.BlockSpec(memory_space=pl.ANY)], out_specs=[HBM_SPEC] * m,
        out_shape=[pltpu.HBM(b.shape, b.dtype) for b in bufs], input_output_aliases={t: t for t in range(m)},
        compiler_params=SPLIT_COPY,
    )(*bufs, send_sems, recv_sems, after)


def _gather_relay(bufs):
    n = len(bufs)

    def body(*refs):
        out_refs = refs[n:2 * n]
        send_sems, recv_sems = refs[2 * n:]
        x, y, c, chips = _place()
        cps = []
        for a in range(n):
            for j, chip in enumerate(chips):
                piece = _gather_piece(out_refs[a], 2 * chip[0] + chip[1], c)
                cps.append(pltpu.make_async_remote_copy(
                    src_ref=piece, dst_ref=piece, send_sem=send_sems.at[a, j], recv_sem=recv_sems.at[a, j],
                    device_id=(x, y, 1 - c), device_id_type=MESH))
        for cp in cps:
            cp.start()
        for a in range(n):
            for j, chip in enumerate(chips):
                theirs = _gather_piece(out_refs[a], 2 * chip[0] + chip[1], 1 - c)
                pltpu.make_async_remote_copy(
                    src_ref=theirs, dst_ref=theirs, send_sem=send_sems.at[a, j], recv_sem=recv_sems.at[a, j],
                    device_id=(x, y, 1 - c), device_id_type=MESH).wait_recv()
        for cp in cps:
            cp.wait_send()

    any_spec = pl.BlockSpec(memory_space=pl.ANY)
    return pl.pallas_call(
        body, name="gather_relay", in_specs=[any_spec] * n, out_specs=[any_spec] * n,
        out_shape=[jax.ShapeDtypeStruct(b.shape, b.dtype) for b in bufs],
        input_output_aliases={a: a for a in range(n)},
        scratch_shapes=[pltpu.SemaphoreType.DMA((n, 3))] * 2,
    )(*bufs)


def _exchange_halves(gs):
    n = len(gs)

    def body(*refs):
        g_refs, out_refs = refs[:n], refs[n:2 * n]
        send_sems, recv_sems = refs[2 * n:]
        x, y, c, _ = _place()
        cps = []
        for a in range(n):
            r = g_refs[a].shape[1]
            cps.append(pltpu.make_async_remote_copy(
                src_ref=g_refs[a].at[:, _half(1 - c, r), :], dst_ref=out_refs[a],
                send_sem=send_sems.at[a], recv_sem=recv_sems.at[a], device_id=(x, y, 1 - c), device_id_type=MESH))
        for cp in cps:
            cp.start()
        for cp in cps:
            cp.wait()

    any_spec = pl.BlockSpec(memory_space=pl.ANY)
    return pl.pallas_call(
        body, name="grad_to_sibling", in_specs=[any_spec] * n, out_specs=[any_spec] * n,
        out_shape=[jax.ShapeDtypeStruct((g.shape[0], g.shape[1] // 2, g.shape[2]), g.dtype) for g in gs],
        scratch_shapes=[pltpu.SemaphoreType.DMA((n,))] * 2,
    )(*gs)


def _add_half(gs, gots, c_arr):
    n = len(gs)

    def body(c_ref, *refs):
        for a in range(n):
            refs[2 * n + a][...] = (refs[a][...].astype(F32) + refs[n + a][...].astype(F32)).astype(BF16)

    def own(g):
        return pl.BlockSpec((None, g.shape[1] // 2, g.shape[2]), lambda k, cr: (k, cr[0], 0))

    def half(g):
        return pl.BlockSpec((None, g.shape[1] // 2, g.shape[2]), lambda k, cr: (k, 0, 0))

    return pl.pallas_call(
        body, name="grad_add_half",
        grid_spec=pltpu.PrefetchScalarGridSpec(
            num_scalar_prefetch=1, grid=(N_CHIPS,),
            in_specs=[own(g) for g in gs] + [half(g) for g in gs], out_specs=[half(g) for g in gs]),
        out_shape=[jax.ShapeDtypeStruct(got.shape, BF16) for got in gots], compiler_params=_params(),
    )(c_arr, *gs, *gots)


def _scatter_start(k, ss, thru):
    n = len(ss)

    def body(*refs):
        s_refs, land_refs = refs[2 * n + 1:3 * n + 1], refs[3 * n + 1:4 * n + 1]
        send_sems, recv_sems = refs[4 * n + 2:]
        x, y, c, chips = _place()
        me = 2 * x + y
        for a in range(n):
            for j, chip in enumerate(chips):
                pltpu.make_async_remote_copy(
                    src_ref=s_refs[a].at[2 * chip[0] + chip[1]], dst_ref=land_refs[a].at[me],
                    send_sem=send_sems.at[3 * a + j], recv_sem=recv_sems.at[3 * a + j], device_id=(*chip, c),
                    device_id_type=MESH).start()

    lands = [lax.empty(s.shape, s.dtype) for s in ss]
    hbm = [pltpu.HBM(s.shape, s.dtype) for s in ss]
    outs = pl.pallas_call(
        body, name=f"grad_scatter_start_{k}", in_specs=[HBM_SPEC] * (2 * n + 1),
        out_specs=[HBM_SPEC] * (2 * n + 1) + [SEM_SPEC, SEM_SPEC],
        out_shape=hbm + hbm + [pltpu.HBM(thru.shape, thru.dtype)] + [pltpu.SemaphoreType.DMA((3 * n,))] * 2,
        input_output_aliases={a: a for a in range(2 * n + 1)}, compiler_params=SPLIT_COPY,
    )(*[pltpu.with_memory_space_constraint(v, pltpu.HBM) for v in list(ss) + lands + [thru]])
    return (outs[:n], outs[n:2 * n], outs[2 * n + 1], outs[2 * n + 2]), outs[2 * n]


def _scatter_wait(k, ss, lands, send_sems, recv_sems, after):
    n = len(ss)

    def body(*refs):
        s_refs, land_refs = refs[:n], refs[n:2 * n]
        ssem, rsem = refs[2 * n], refs[2 * n + 1]
        x, y, c, chips = _place()
        me = 2 * x + y
        for a in range(n):
            for j, chip in enumerate(chips):
                cid = 2 * chip[0] + chip[1]
                cp = pltpu.make_async_remote_copy(
                    src_ref=s_refs[a].at[cid], dst_ref=land_refs[a].at[cid], send_sem=ssem.at[3 * a + j],
                    recv_sem=rsem.at[3 * a + j], device_id=(*chip, c), device_id_type=MESH)
                cp.wait_send()
                cp.wait_recv()

    hbm = [pltpu.HBM(s.shape, s.dtype) for s in ss]
    outs = pl.pallas_call(
        body, name=f"grad_scatter_wait_{k}",
        in_specs=[HBM_SPEC] * (2 * n) + [SEM_SPEC, SEM_SPEC, pl.BlockSpec(memory_space=pl.ANY)],
        out_specs=[HBM_SPEC] * (2 * n), out_shape=hbm + hbm,
        input_output_aliases={a: a for a in range(2 * n)}, compiler_params=SPLIT_COPY,
    )(*ss, *lands, send_sems, recv_sems, after)
    return outs[:n], outs[n:]


def _sum_chips(lands, ss, me_arr):
    n = len(lands)

    def body(me_ref, *refs):
        for own in range(N_CHIPS):
            @pl.when(me_ref[0] == own)
            def _(own=own):
                for a in range(n):
                    acc = None
                    for k in range(N_CHIPS):
                        term = (refs[n + a][...] if k == own else refs[a][k]).astype(F32)
                        acc = term if acc is None else acc + term
                    refs[2 * n + a][...] = acc

    return pl.pallas_call(
        body, name="grad_sum_chips",
        grid_spec=pltpu.PrefetchScalarGridSpec(
            num_scalar_prefetch=1, grid=(1,),
            in_specs=[pl.BlockSpec(la.shape, lambda i, me: (0, 0, 0)) for la in lands]
            + [pl.BlockSpec((None,) + la.shape[1:], lambda i, me: (me[0], 0, 0)) for la in lands],
            out_specs=[pl.BlockSpec(la.shape[1:], lambda i, me: (0, 0)) for la in lands]),
        out_shape=[jax.ShapeDtypeStruct(la.shape[1:], F32) for la in lands], compiler_params=_params(),
    )(me_arr, *lands, *ss)


def _swap_halves(fs):
    n = len(fs)

    def body(*refs):
        f_refs, out_refs = refs[:n], refs[n:2 * n]
        send_sems, recv_sems = refs[2 * n:]
        x, y, c, _ = _place()
        cps = [pltpu.make_async_remote_copy(
            src_ref=f_refs[a], dst_ref=out_refs[a], send_sem=send_sems.at[a], recv_sem=recv_sems.at[a],
            device_id=(x, y, 1 - c), device_id_type=MESH) for a in range(n)]
        for cp in cps:
            cp.start()
        for cp in cps:
            cp.wait()

    any_spec = pl.BlockSpec(memory_space=pl.ANY)
    return pl.pallas_call(
        body, name="grad_swap_halves", in_specs=[any_spec] * n, out_specs=[any_spec] * n,
        out_shape=[jax.ShapeDtypeStruct(f.shape, f.dtype) for f in fs],
        scratch_shapes=[pltpu.SemaphoreType.DMA((n,))] * 2,
    )(*fs)


def _allreduce_rows(stats):
    def body(s_ref, o_ref, buf, send_sems, recv_sems):
        x, y, c, _ = _place()
        me = 4 * x + 2 * y + c
        buf[me] = s_ref[...]
        cps = []
        for k in range(1, 8):
            px = jnp.where(k & 4, 1 - x, x)
            py = jnp.where(k & 2, 1 - y, y)
            pc = jnp.where(k & 1, 1 - c, c)
            cps.append(pltpu.make_async_remote_copy(
                src_ref=s_ref, dst_ref=buf.at[me], send_sem=send_sems.at[k - 1], recv_sem=recv_sems.at[k - 1],
                device_id=(px, py, pc), device_id_type=MESH))
        for cp in cps:
            cp.start()
        for cp in cps:
            cp.wait()
        acc = buf[0]
        for d in range(1, 8):
            acc = acc + buf[d]
        o_ref[...] = acc

    vm = pl.BlockSpec(memory_space=pltpu.VMEM)
    return pl.pallas_call(
        body, name="allreduce_rows", in_specs=[vm], out_specs=vm,
        out_shape=jax.ShapeDtypeStruct(stats.shape, F32),
        scratch_shapes=[pltpu.VMEM((8,) + stats.shape, F32), pltpu.SemaphoreType.DMA((7,)),
                        pltpu.SemaphoreType.DMA((7,))],
    )(stats)


def _adamw_math(w, g, m, v):
    m = ADAM_B1 * m + (1.0 - ADAM_B1) * g
    v = ADAM_B2 * v + (1.0 - ADAM_B2) * (g * g)
    m_hat = m / (1.0 - ADAM_B1 ** ADAM_STEP)
    v_hat = v / (1.0 - ADAM_B2 ** ADAM_STEP)
    delta = -ADAM_LR * (m_hat / (jnp.sqrt(v_hat) + ADAM_EPS) + ADAM_WD * w)
    return delta, m, v


def _adamw(ws, ms, vs, mines, theirs, l, c_arr, earlier):
    n = len(ws)
    held = [t for e in earlier if e is not None for t in e]
    assert len(held) in (0, 4 * n)

    def body(c_ref, *refs):
        outs = refs[len(refs) - 4 * n:]
        for a in range(n):
            w_ref, m_ref, v_ref, a_ref, b_ref = refs[5 * a:5 * a + 5]
            g = jnp.where(pl.program_id(0) == c_ref[0], a_ref[...], b_ref[...])
            delta, mn, vn = _adamw_math(w_ref[...], g, m_ref[...], v_ref[...])
            outs[4 * a][...] = g
            outs[4 * a + 1][...] = delta
            outs[4 * a + 2][...] = mn
            outs[4 * a + 3][...] = vn

    def blk(w):
        tr = w.shape[1] // 4
        return pl.BlockSpec((None, tr, w.shape[2]), lambda hh, i, cr: (l, 2 * hh + i, 0))

    def half(w):
        return pl.BlockSpec((w.shape[1] // 4, w.shape[2]), lambda hh, i, cr: (i, 0))

    outs = pl.pallas_call(
        body, name="adamw",
        grid_spec=pltpu.PrefetchScalarGridSpec(
            num_scalar_prefetch=1, grid=(2, 2),
            in_specs=[sp for w in ws for sp in (blk(w), blk(w), blk(w), half(w), half(w))]
            + [pl.BlockSpec(memory_space=pl.ANY)] * len(held),
            out_specs=[blk(w) for w in ws for _ in range(4)]),
        out_shape=[jax.ShapeDtypeStruct(w.shape, F32) for w in ws for _ in range(4)],
        input_output_aliases={1 + 5 * n + t: t for t in range(len(held))}, compiler_params=_params(),
    )(c_arr, *[t for grp in zip(ws, ms, vs, mines, theirs) for t in grp], *held)
    return [outs[4 * a:4 * a + 4] for a in range(n)]


def _adamw_rows(w, m, v, g):
    def body(w_ref, m_ref, v_ref, g_ref, d_ref, mo_ref, vo_ref):
        delta, mn, vn = _adamw_math(w_ref[...], g_ref[...], m_ref[...], v_ref[...])
        d_ref[...] = delta
        mo_ref[...] = mn
        vo_ref[...] = vn

    vm = pl.BlockSpec(memory_space=pltpu.VMEM)
    sh = jax.ShapeDtypeStruct(w.shape, F32)
    return pl.pallas_call(body, name="adamw_rows", in_specs=[vm] * 4, out_specs=[vm] * 3, out_shape=[sh] * 3)(w, m, v, g)


SUBLAYERS = (("ffn1_w_gate", "ffn1_w_up", "ffn1_w_down"), ("w_in", "w_proj_dil", "w_proj_sb", "w_out"),
             ("ffn2_w_gate", "ffn2_w_up", "ffn2_w_down"))
TRANSPOSED = ("ffn1_w_gate", "ffn1_w_up", "ffn2_w_gate", "ffn2_w_up")
LAG = 2


def _pick_row(blocks):
    row = lax.broadcasted_iota(jnp.int32, (8, D_MODEL), 0)
    out = jnp.zeros((8, D_MODEL), F32)
    for i, b in enumerate(blocks):
        out = out + jnp.where(row == i, b, 0.0)
    return out


def kernel(x, norm_ffn1, ffn1_w_gate, ffn1_w_up, ffn1_w_down, norm_mix, w_in, w_proj_dil, w_proj_sb, w_out, norm_ffn2, ffn2_w_gate, ffn2_w_up, ffn2_w_down, norm_final, loss_target, m_norm_ffn1, m_ffn1_w_gate, m_ffn1_w_up, m_ffn1_w_down, m_norm_mix, m_w_in, m_w_proj_dil, m_w_proj_sb, m_w_out, m_norm_ffn2, m_ffn2_w_gate, m_ffn2_w_up, m_ffn2_w_down, m_norm_final, v_norm_ffn1, v_ffn1_w_gate, v_ffn1_w_up, v_ffn1_w_down, v_norm_mix, v_w_in, v_w_proj_dil, v_w_proj_sb, v_w_out, v_norm_ffn2, v_ffn2_w_gate, v_ffn2_w_up, v_ffn2_w_down, v_norm_final):
    given = dict(locals())
    for n in TRANSPOSED:
        for k in ("", "m_", "v_"):
            given[k + n] = jnp.swapaxes(given[k + n], 1, 2)
    weights = {n: given[n] for n in WEIGHT_NAMES}
    norms = {n: given[n] for n in NORM_NAMES}

    c_arr = lax.axis_index("c").astype(jnp.int32).reshape(1)
    me_arr = (2 * lax.axis_index("x") + lax.axis_index("y")).astype(jnp.int32).reshape(1)
    order = [(l, s, n) for l in range(DEPTH) for s in range(len(SUBLAYERS)) for n in SUBLAYERS[s]]
    n_first = len(SUBLAYERS[0])
    sent, token = {}, None
    for tag, idxs in (("a", range(n_first)), ("b", range(n_first, len(order)))):
        cast = _cast_into_slot([weights[order[i][2]] for i in idxs], [order[i][0] for i in idxs], me_arr, token)
        bufs, send_sems, recv_sems, token = _gather_start(tag, cast)
        for p, i in enumerate(idxs):
            sent[i] = (bufs[p], p, send_sems, recv_sems)

    def weights_of(l, s, after):
        idxs = [i for i, (ll, ss, _) in enumerate(order) if (ll, ss) == (l, s)]
        got = _gather_wait(len(SUBLAYERS) * l + s, [sent[i][0] for i in idxs], [sent[i][1] for i in idxs],
                           sent[idxs[0]][2], sent[idxs[0]][3], after)
        return {order[i][2]: g for i, g in zip(idxs, _gather_relay(got))}

    out = {}
    in_flight = []

    def finish(l, s, names, sums, lands, ssem, rsem, after):
        sums, lands = _scatter_wait(len(SUBLAYERS) * l + s, sums, lands, ssem, rsem, after)
        mine = _sum_chips(lands, sums, me_arr)
        theirs = _swap_halves(mine)
        res = _adamw([weights[n] for n in names], [given["m_" + n] for n in names], [given["v_" + n] for n in names],
                     mine, theirs, l, c_arr, [out.get(n) for n in names])
        out.update(zip(names, res))

    def on_grads(l, s, grads, after):
        names = list(grads)
        gs = [grads[n] for n in names]
        sums = _add_half(gs, _exchange_halves(gs), c_arr)
        sent, after = _scatter_start(len(SUBLAYERS) * l + s, sums, after)
        in_flight.append((l, s, names) + sent)
        if len(in_flight) > LAG:
            finish(*in_flight.pop(0), after)
        return after

    loss_blk, grad_x, gains, dg_final = _local_step(x[0], loss_target[0], norms, norm_final, weights_of, on_grads)
    while in_flight:
        finish(*in_flight.pop(0), grad_x)
    out = {k + n: (jnp.swapaxes(v, 1, 2) if n in TRANSPOSED else v)
           for n, res in out.items() for k, v in zip(("grad_", "delta_", "new_m_", "new_v_"), res)}
    out["grad_x"] = grad_x[None]

    rows = [gains[l][n] for n in NORM_NAMES for l in range(DEPTH)] + [dg_final, loss_blk]
    total = _allreduce_rows(_pick_row(rows))
    out["loss"] = total[7, 0]
    wn = jnp.concatenate([given[n] for n in NORM_NAMES] + [norm_final[None], jnp.zeros((1, D_MODEL), F32)])
    mn_ = jnp.concatenate([given["m_" + n] for n in NORM_NAMES] + [m_norm_final[None], jnp.zeros((1, D_MODEL), F32)])
    vn_ = jnp.concatenate([given["v_" + n] for n in NORM_NAMES] + [v_norm_final[None], jnp.ones((1, D_MODEL), F32)])
    d_n, m_n, v_n = _adamw_rows(wn, mn_, vn_, total)
    for i, n in enumerate(NORM_NAMES):
        sl = slice(i * DEPTH, (i + 1) * DEPTH)
        out["grad_" + n], out["delta_" + n], out["new_m_" + n], out["new_v_" + n] = total[sl], d_n[sl], m_n[sl], v_n[sl]
    out["grad_norm_final"], out["delta_norm_final"] = total[6], d_n[6]
    out["new_m_norm_final"], out["new_v_norm_final"] = m_n[6], v_n[6]

    names = ["norm_ffn1", "ffn1_w_gate", "ffn1_w_up", "ffn1_w_down", "norm_mix", "w_in", "w_proj_dil", "w_proj_sb",
             "w_out", "norm_ffn2", "ffn2_w_gate", "ffn2_w_up", "ffn2_w_down", "norm_final"]
    return (out["loss"], out["grad_x"], *[out["grad_" + n] for n in names], *[out["delta_" + n] for n in names],
            *[out["new_m_" + n] for n in names], *[out["new_v_" + n] for n in names])
```

```python
import functools

import jax
import jax.numpy as jnp
from jax import lax
from jax.experimental import pallas as pl
from jax.experimental.pallas import tpu as pltpu

F32 = jnp.float32
BF16 = jnp.bfloat16

D_MODEL = 1024
DEPTH = 2
N_CHIPS = 4
HEAD_DIM = 64
ROPE_DIM = 16
ROPE_THETA = 500000.0
DIL_GROUPS = ((128, 1), (512, 4), (2048, 16))
SPAN = 128
Q_BLOCK = 128
RMS_EPS = 1e-6
D_ATT = 256
COL_QS = 2304
COL_GD = 3072
COL_GS = 4096
ADAM_LR, ADAM_B1, ADAM_B2, ADAM_EPS, ADAM_WD, ADAM_STEP = 0.001, 0.9, 0.999, 1e-08, 0.01, 10

VMEM_LIMIT = 52 * 1024 * 1024
TM = 512
NEG = -1e30

NN = (((1,), (0,)), ((), ()))
NT = (((1,), (1,)), ((), ()))
TN = (((0,), (0,)), ((), ()))
MESH = pl.DeviceIdType.MESH

WEIGHT_NAMES = ("ffn1_w_gate", "ffn1_w_up", "ffn1_w_down", "w_in", "w_proj_dil",
                "w_proj_sb", "w_out", "ffn2_w_gate", "ffn2_w_up", "ffn2_w_down")
NORM_NAMES = ("norm_ffn1", "norm_mix", "norm_ffn2")


def _params(**kw):
    return pltpu.CompilerParams(vmem_limit_bytes=VMEM_LIMIT, **kw)


def _sigmoid(x):
    return 0.5 * jnp.tanh(0.5 * x) + 0.5


def _mm_body(pairs, n_in, n_out, n_acc, dims, nk, i_axis, epilogue, *refs):
    ins = refs[:n_in]
    outs = refs[n_in:n_in + n_out]
    accs = refs[n_in + n_out:]
    i = pl.program_id(i_axis)
    k = pl.program_id(2)

    def operand(a):
        return (a(ins) if callable(a) else ins[a][...]).astype(BF16)

    def dot(ia, ib):
        return lax.dot_general(operand(ia), operand(ib), dims, preferred_element_type=F32)

    if nk == 1:
        parts = [None] * n_acc
        for ia, ib, ic in pairs:
            parts[ic] = dot(ia, ib) if parts[ic] is None else parts[ic] + dot(ia, ib)
        epilogue(parts, ins, outs, i)
        return

    @pl.when(k == 0)
    def _():
        for c in range(n_acc):
            accs[c][...] = jnp.zeros_like(accs[c])

    for ia, ib, ic in pairs:
        accs[ic][...] += dot(ia, ib)

    @pl.when(k == nk - 1)
    def _():
        epilogue([a[...] for a in accs], ins, outs, i)


def _j_outer(spec):
    f = spec.index_map
    return pl.BlockSpec(spec.block_shape, lambda j, i, k: f(i, j, k))


def _mm(name, ins, in_specs, pairs, n_acc, acc_shape, dims, grid, epilogue, out_shapes, out_specs, j_outer=False):
    nk = grid[2]
    if j_outer:
        grid = (grid[1], grid[0], grid[2])
        in_specs = [_j_outer(s) for s in in_specs]
        out_specs = [_j_outer(s) for s in out_specs]
    scratch = [pltpu.VMEM(acc_shape, F32) for _ in range(n_acc)] if nk > 1 else []
    body = functools.partial(_mm_body, tuple(pairs), len(ins), len(out_shapes), n_acc, dims, nk,
                             1 if j_outer else 0, epilogue)
    return pl.pallas_call(
        body, name=name, grid=grid, in_specs=in_specs, out_specs=out_specs, out_shape=out_shapes,
        scratch_shapes=scratch,
        compiler_params=_params(dimension_semantics=("arbitrary", "arbitrary", "arbitrary")),
    )(*ins)


def _wspec(r, c, l, by):
    if by == 1:
        return pl.BlockSpec((None, None, r, c), lambda i, j, k: (j, l, 0, 0))
    return pl.BlockSpec((None, None, r, c), lambda i, j, k: (k, l, 0, 0))


def _rms_bwd_epilogue(x_idx, g_idx, dxo_idx):
    def ep(vals, ins, outs, i):
        dh = vals[0]
        x = ins[x_idx][...]
        g = ins[g_idx][...]
        rstd = lax.rsqrt(jnp.mean(x * x, axis=-1, keepdims=True) + RMS_EPS)
        xhat = x * rstd
        dxhat = dh * g
        dx = rstd * (dxhat - xhat * jnp.mean(dxhat * xhat, axis=-1, keepdims=True))
        outs[0][...] = ins[dxo_idx][...] + dx
        dg = jnp.broadcast_to(jnp.sum(dh * xhat, axis=0, keepdims=True), outs[1].shape)

        @pl.when(i == 0)
        def _():
            outs[1][...] = dg

        @pl.when(i > 0)
        def _():
            outs[1][...] += dg
    return ep


def _normed(x_idx, g_idx):
    seen = {}

    def f(ins):
        if id(ins) not in seen:
            xv = ins[x_idx][...]
            h = xv * lax.rsqrt(jnp.mean(xv * xv, axis=-1, keepdims=True) + RMS_EPS)
            seen[id(ins)] = (ins, (h * ins[g_idx][...]).astype(BF16))
        return seen[id(ins)][1]
    return f


def _rope_tables(T):
    pos = jnp.arange(T, dtype=F32)
    inv_freq = ROPE_THETA ** (-jnp.arange(0, ROPE_DIM, 2, dtype=F32) / ROPE_DIM)
    ang = pos[:, None] * inv_freq[None, :]
    cos, sin = jnp.cos(ang), jnp.sin(ang)
    half = ROPE_DIM // 2
    one = jnp.ones((T, HEAD_DIM - ROPE_DIM), F32)
    zero = jnp.zeros((T, HEAD_DIM - ROPE_DIM), F32)
    zh = jnp.zeros((T, half), F32)
    c = jnp.concatenate([cos, cos, one], axis=1)
    s1 = jnp.concatenate([-sin, zh, zero], axis=1)
    s2 = jnp.concatenate([zh, sin, zero], axis=1)
    return tuple(jnp.concatenate([t, t], axis=1) for t in (c, s1, s2))


def _rope_fwd(xv, c, s1, s2):
    w = xv.shape[1]
    half = ROPE_DIM // 2
    return xv * c + pltpu.roll(xv, w - half, 1) * s1 + pltpu.roll(xv, half, 1) * s2


def _rope_bwd(dy, c, s1, s2):
    w = dy.shape[1]
    half = ROPE_DIM // 2
    return dy * c + pltpu.roll(dy * s1, half, 1) + pltpu.roll(dy * s2, w - half, 1)


def _assemble_dproj(dqk, rest, gates, tabs):
    T = gates[0].shape[0]
    n_qk, n_rest = len(dqk), len(rest)
    width = (n_qk + n_rest) * D_ATT + 2 * D_MODEL

    def body(*refs):
        ins, (c_ref, s1_ref, s2_ref), o_ref = refs[:n_qk + n_rest + 2], refs[-4:-1], refs[-1]
        c = jnp.concatenate([c_ref[...]] * 2, axis=1)
        s1 = jnp.concatenate([s1_ref[...]] * 2, axis=1)
        s2 = jnp.concatenate([s2_ref[...]] * 2, axis=1)
        for b in range(n_qk + n_rest):
            v = ins[b][...]
            if b < n_qk:
                v = _rope_bwd(v, c, s1, s2)
            o_ref[:, b * D_ATT:(b + 1) * D_ATT] = v.astype(BF16)
        off = (n_qk + n_rest) * D_ATT
        o_ref[:, off:off + D_MODEL] = ins[-2][...]
        o_ref[:, off + D_MODEL:] = ins[-1][...]

    att = pl.BlockSpec((TM, D_ATT), lambda i: (i, 0))
    wide = pl.BlockSpec((TM, D_MODEL), lambda i: (i, 0))
    tab = pl.BlockSpec((TM, 128), lambda i: (i, 0))
    return pl.pallas_call(
        body, name="assemble_dproj", grid=(T // TM,),
        in_specs=[att] * (n_qk + n_rest) + [wide, wide, tab, tab, tab],
        out_specs=pl.BlockSpec((TM, width), lambda i: (i, 0)),
        out_shape=jax.ShapeDtypeStruct((T, width), BF16), compiler_params=_params(),
    )(*dqk, *rest, *gates, *tabs)


def _dil_merge(os_, lses):
    T = os_[0].shape[0]

    def body(o0, o1, o2, l0, l1, l2, o_ref, lse_ref):
        a, b, c = l0[...], l1[...], l2[...]
        m = jnp.maximum(jnp.maximum(a, b), c)
        ea, eb, ec = jnp.exp(a - m), jnp.exp(b - m), jnp.exp(c - m)
        den = ea + eb + ec
        o_ref[...] = (ea * o0[...] + eb * o1[...] + ec * o2[...]) / den
        lse_ref[...] = m + jnp.log(den)

    blk = pl.BlockSpec((TM, D_ATT), lambda i: (i, 0))
    sh = jax.ShapeDtypeStruct((T, D_ATT), F32)
    return pl.pallas_call(
        body, name="dil_merge", grid=(T // TM,), in_specs=[blk] * 6, out_specs=[blk, blk],
        out_shape=[sh, sh], compiler_params=_params(),
    )(*os_, *lses)


def _final_loss(x, gain, target):
    T = x.shape[0]

    def body(x_ref, g_ref, t_ref, dx_ref, dg_ref, loss_ref):
        xv = x_ref[...]
        g = g_ref[...]
        rstd = lax.rsqrt(jnp.mean(xv * xv, axis=-1, keepdims=True) + RMS_EPS)
        xhat = xv * rstd
        err = xhat * g - t_ref[...]
        loss = 0.5 * jnp.sum(jnp.mean(err * err, axis=-1, keepdims=True), axis=0, keepdims=True)
        dy = err * (1.0 / D_MODEL)
        dxhat = dy * g
        dx_ref[...] = rstd * (dxhat - xhat * jnp.mean(dxhat * xhat, axis=-1, keepdims=True))
        dg = jnp.broadcast_to(jnp.sum(dy * xhat, axis=0, keepdims=True), dg_ref.shape)
        ls = jnp.broadcast_to(loss, loss_ref.shape)

        @pl.when(pl.program_id(0) == 0)
        def _():
            dg_ref[...] = dg
            loss_ref[...] = ls

        @pl.when(pl.program_id(0) > 0)
        def _():
            dg_ref[...] += dg
            loss_ref[...] += ls

    blk = pl.BlockSpec((TM, D_MODEL), lambda i: (i, 0))
    row = pl.BlockSpec((1, D_MODEL), lambda i: (0, 0))
    acc = pl.BlockSpec((8, D_MODEL), lambda i: (0, 0))
    return pl.pallas_call(
        body, name="final_loss", grid=(T // TM,), in_specs=[blk, row, blk], out_specs=[blk, acc, acc],
        out_shape=[jax.ShapeDtypeStruct((T, D_MODEL), F32), jax.ShapeDtypeStruct((8, D_MODEL), F32),
                   jax.ShapeDtypeStruct((8, D_MODEL), F32)],
        compiler_params=_params(dimension_semantics=("arbitrary",)),
    )(x, gain, target)


def _pair_masks():
    lane = lax.broadcasted_iota(jnp.int32, (SPAN, 128), 1)
    return [lane < HEAD_DIM, lane >= HEAD_DIM]


def _stack_heads(x, masks):
    return jnp.concatenate([jnp.where(m, x, 0.0) for m in masks], axis=0)


def _unstack_heads(y, masks):
    rows = y.shape[0] // len(masks)
    out = jnp.where(masks[0], y[:rows], 0.0)
    for h in range(1, len(masks)):
        out = out + jnp.where(masks[h], y[rows * h:rows * (h + 1)], 0.0)
    return out


DIL_PAIR = 2


def _dil_rows(idx, d):
    u = idx // d
    r = idx - u * d
    own = pl.ds(u * (SPAN * d) + r, SPAN, stride=d) if d > 1 else pl.ds(pl.multiple_of(u * SPAN, SPAN), SPAN)
    up = jnp.maximum(u - 1, 0)
    prev = pl.ds(up * (SPAN * d) + r, SPAN, stride=d) if d > 1 else pl.ds(pl.multiple_of(up * SPAN, SPAN), SPAN)
    return u, own, prev


def _dil_valid(u):
    qi = lax.broadcasted_iota(jnp.int32, (2 * SPAN, 2 * SPAN), 0) & (SPAN - 1)
    kj = lax.broadcasted_iota(jnp.int32, (2 * SPAN, 2 * SPAN), 1)
    in_prev = (kj < SPAN) & (kj >= qi + jnp.where(u > 0, 0, SPAN))
    return in_prev | ((kj >= SPAN) & (kj - SPAN <= qi))


def _dil_keys(ref, own, prev):
    return jnp.concatenate([ref[prev, :], ref[own, :]], axis=0).astype(BF16)


def _dil_fwd(proj, g, d):
    T = proj.shape[0]
    n_iter = T // SPAN

    def body(q_ref, k_ref, v_ref, o_ref, lse_ref):
        masks = _pair_masks()

        def step(pair, carry):
            its = [_dil_rows(DIL_PAIR * pair + e, d) for e in range(DIL_PAIR)]
            qs = [_stack_heads(q_ref[own, :] * (HEAD_DIM ** -0.5), masks).astype(BF16) for _, own, _ in its]
            kks = [_dil_keys(k_ref, own, prev) for _, own, prev in its]
            vvs = [_dil_keys(v_ref, own, prev) for _, own, prev in its]
            ss = [jnp.where(_dil_valid(u), lax.dot_general(q, kk, NT, preferred_element_type=F32), NEG)
                  for (u, _, _), q, kk in zip(its, qs, kks)]
            ms = [jnp.max(s, axis=1, keepdims=True) for s in ss]
            ps = [jnp.exp(s - m) for s, m in zip(ss, ms)]
            dens = [jnp.sum(p, axis=1, keepdims=True) for p in ps]
            pvs = [lax.dot_general(p.astype(BF16), vv, NN, preferred_element_type=F32) / den
                   for p, vv, den in zip(ps, vvs, dens)]
            for (_, own, _), pv, m, den in zip(its, pvs, ms, dens):
                o_ref[own, :] = _unstack_heads(pv, masks)
                lse_ref[own, :] = _unstack_heads(jnp.broadcast_to(m + jnp.log(den), pv.shape), masks)
            return carry

        lax.fori_loop(0, n_iter // DIL_PAIR, step, 0)

    def col(b):
        return pl.BlockSpec((T, 128), lambda p: (0, b + p))

    sh = jax.ShapeDtypeStruct((T, D_ATT), F32)
    out = pl.BlockSpec((T, 128), lambda p: (0, p))
    return pl.pallas_call(
        body, name=f"dil_fwd_d{d}", grid=(2,),
        in_specs=[col(2 * g), col(6 + 2 * g), col(12 + 2 * g)], out_specs=[out, out], out_shape=[sh, sh],
        compiler_params=_params(dimension_semantics=("arbitrary",)),
    )(proj, proj, proj)


def _dil_bwd(proj, do, o_dil, lse, g, d):
    T = proj.shape[0]
    n_iter = T // SPAN

    def body(q_ref, k_ref, v_ref, do_ref, o_ref, lse_ref, dq_ref, dk_ref, dv_ref):
        masks = _pair_masks()
        head_lanes = jnp.concatenate(masks, axis=0)

        def step(pair, carry):
            its = [_dil_rows(DIL_PAIR * pair + e, d) for e in range(DIL_PAIR)]
            qs = [_stack_heads(q_ref[own, :] * (HEAD_DIM ** -0.5), masks).astype(BF16) for _, own, _ in its]
            kks = [_dil_keys(k_ref, own, prev) for _, own, prev in its]
            vvs = [_dil_keys(v_ref, own, prev) for _, own, prev in its]
            doms = [_stack_heads(do_ref[own, :], masks) for _, own, _ in its]
            dos = [dom.astype(BF16) for dom in doms]
            deltas = [jnp.sum(dom * jnp.concatenate([o_ref[own, :]] * 2, axis=0), axis=1, keepdims=True)
                      for dom, (_, own, _) in zip(doms, its)]
            lrows = [jnp.max(jnp.where(head_lanes, jnp.concatenate([lse_ref[own, :]] * 2, axis=0), NEG),
                             axis=1, keepdims=True) for _, own, _ in its]
            ss = [lax.dot_general(q, kk, NT, preferred_element_type=F32) for q, kk in zip(qs, kks)]
            dps = [lax.dot_general(do_b, vv, NT, preferred_element_type=F32) for do_b, vv in zip(dos, vvs)]
            ps = [jnp.where(_dil_valid(u), jnp.exp(s - lrow), 0.0) for (u, _, _), s, lrow in zip(its, ss, lrows)]
            dss = [(p * (dp - delta)).astype(BF16) for p, dp, delta in zip(ps, dps, deltas)]
            dqs = [lax.dot_general(ds, kk, NN, preferred_element_type=F32) for ds, kk in zip(dss, kks)]
            dkks = [lax.dot_general(ds, q, TN, preferred_element_type=F32) for ds, q in zip(dss, qs)]
            dvvs = [lax.dot_general(p.astype(BF16), do_b, TN, preferred_element_type=F32) for p, do_b in zip(ps, dos)]
            for (_, own, prev), dq, dkk, dvv in zip(its, dqs, dkks, dvvs):
                dq_ref[own, :] = _unstack_heads(dq, masks) * (HEAD_DIM ** -0.5)
                dk_ref[own, :] = dkk[SPAN:]
                dv_ref[own, :] = dvv[SPAN:]
                dk_ref[prev, :] = dk_ref[prev, :] + dkk[:SPAN]
                dv_ref[prev, :] = dv_ref[prev, :] + dvv[:SPAN]
            return carry

        lax.fori_loop(0, n_iter // DIL_PAIR, step, 0)

    def col(b):
        return pl.BlockSpec((T, 128), lambda p: (0, b + p))

    sh = jax.ShapeDtypeStruct((T, D_ATT), F32)
    return pl.pallas_call(
        body, name=f"dil_bwd_d{d}", grid=(2,),
        in_specs=[col(2 * g), col(6 + 2 * g), col(12 + 2 * g), col(0), col(0), col(0)],
        out_specs=[col(0), col(0), col(0)], out_shape=[sh, sh, sh],
        compiler_params=_params(dimension_semantics=("arbitrary",)),
    )(proj, proj, proj, do, o_dil, lse)


SB_KT = 512


def _sb_tri(strict):
    a = lax.broadcasted_iota(jnp.int32, (Q_BLOCK, Q_BLOCK), 0)
    b = lax.broadcasted_iota(jnp.int32, (Q_BLOCK, Q_BLOCK), 1)
    return jnp.where((a > b) if strict else (a >= b), 1.0, 0.0).astype(BF16)


def _split_stack(x):
    nb = x.shape[1] // Q_BLOCK
    blocks = [x[:, Q_BLOCK * b:Q_BLOCK * (b + 1)] for b in range(nb)]
    hi = [b.astype(BF16) for b in blocks]
    lo = [(b - h.astype(F32)).astype(BF16) for b, h in zip(blocks, hi)]
    return blocks, jnp.concatenate(hi + lo, axis=0)


def _suffix_from(y, blocks, c):
    r = blocks[0].shape[0]
    nb = len(blocks)
    outs = [None] * nb
    run = c
    for b in reversed(range(nb)):
        outs[b] = run + y[r * b:r * (b + 1)] + y[r * (nb + b):r * (nb + b + 1)]
        run = run + jnp.sum(blocks[b], axis=1, keepdims=True)
    return jnp.concatenate(outs, axis=1), run


SB_HEADS = D_ATT // HEAD_DIM
SB_FWD_CHAINS = 2
SB_BWD_CHAINS = 1


def _sb_past(i, t, rows):
    row = lax.broadcasted_iota(jnp.int32, (rows, SB_KT), 0) & (Q_BLOCK - 1)
    col = lax.broadcasted_iota(jnp.int32, (rows, SB_KT), 1)
    return col + t * SB_KT < row + i * Q_BLOCK


def _sb_head_masks(chains):
    lane = lax.broadcasted_iota(jnp.int32, (Q_BLOCK, D_ATT), 1)
    masks = [(lane >= HEAD_DIM * h) & (lane < HEAD_DIM * (h + 1)) for h in range(SB_HEADS)]
    per = SB_HEADS // chains
    return [masks[per * g:per * (g + 1)] for g in range(chains)]


def _sb_rows(t):
    return pl.ds(pl.multiple_of(t * SB_KT, SB_KT), SB_KT)


def _sb_log_terms(z, past):
    lsz = jnp.minimum(z, 0.0) - jnp.log(1.0 + jnp.exp(-jnp.abs(z)))
    lk = lsz - z
    return lsz, (lk if past is None else jnp.where(past, lk, 0.0))


def _sb_weights(lsz, after, past):
    w = jnp.exp(lsz + after)
    return w if past is None else jnp.where(past, w, 0.0)


def _sb_fwd(proj):
    T = proj.shape[0]

    def body(q_ref, k_ref, v_ref, o_ref):
        i = pl.program_id(0)
        masks = _sb_head_masks(SB_FWD_CHAINS)
        rows = SB_HEADS // SB_FWD_CHAINS * Q_BLOCK
        tri = _sb_tri(True)
        q = q_ref[...] * (HEAD_DIM ** -0.5)
        qs = [_stack_heads(q, m).astype(BF16) for m in masks]
        n_tiles = (i * Q_BLOCK) // SB_KT + 1

        def tile(t, carry, masked):
            kb = k_ref[_sb_rows(t), :].astype(BF16)
            vb = v_ref[_sb_rows(t), :].astype(BF16)
            past = _sb_past(i, t, rows) if masked else None
            acc, cs = carry[0], carry[1:]
            zs = [lax.dot_general(g, kb, NT, preferred_element_type=F32) for g in qs]
            logs = [_sb_log_terms(z, past) for z in zs]
            splits = [_split_stack(lk) for _, lk in logs]
            ys = [lax.dot_general(x, tri, NN, preferred_element_type=F32) for _, x in splits]
            sums = [_suffix_from(y, blocks, c) for y, (blocks, _), c in zip(ys, splits, cs)]
            ws = [_sb_weights(lsz, after, past).astype(BF16) for (lsz, _), (after, _) in zip(logs, sums)]
            for m, w in zip(masks, ws):
                acc = acc + _unstack_heads(lax.dot_general(w, vb, NN, preferred_element_type=F32), m)
            return (acc, *[c for _, c in sums])

        zcol = jnp.zeros((rows, 1), F32)
        carry = tile(n_tiles - 1, (jnp.zeros((Q_BLOCK, D_ATT), F32),) + (zcol,) * SB_FWD_CHAINS, True)
        carry = lax.fori_loop(0, n_tiles - 1, lambda tt, cr: tile(n_tiles - 2 - tt, cr, False), carry)
        o_ref[...] = carry[0]

    cb = COL_QS // D_ATT
    return pl.pallas_call(
        body, name="sb_fwd", grid=(T // Q_BLOCK,),
        in_specs=[pl.BlockSpec((Q_BLOCK, D_ATT), lambda i: (i, cb)),
                  pl.BlockSpec((T, D_ATT), lambda i: (0, cb + 1)),
                  pl.BlockSpec((T, D_ATT), lambda i: (0, cb + 2))],
        out_specs=pl.BlockSpec((Q_BLOCK, D_ATT), lambda i: (i, 0)),
        out_shape=jax.ShapeDtypeStruct((T, D_ATT), F32),
        compiler_params=_params(dimension_semantics=("arbitrary",)),
    )(proj, proj, proj)


def _sb_bwd(proj, do, o):
    T = proj.shape[0]

    def body(q_ref, k_ref, v_ref, do_ref, o_ref, dq_ref, dk_ref, dv_ref):
        i = pl.program_id(0)
        masks = _sb_head_masks(SB_BWD_CHAINS)
        n_rows = SB_HEADS // SB_BWD_CHAINS * Q_BLOCK
        tri = _sb_tri(True)
        tri_incl = _sb_tri(False)

        @pl.when(i == 0)
        def _():
            dk_ref[...] = jnp.zeros_like(dk_ref)
            dv_ref[...] = jnp.zeros_like(dv_ref)

        q = q_ref[...] * (HEAD_DIM ** -0.5)
        qs = [_stack_heads(q, m).astype(BF16) for m in masks]
        dos = [_stack_heads(do_ref[...], m).astype(BF16) for m in masks]
        o_rep = jnp.concatenate([o_ref[...]] * (SB_HEADS // SB_BWD_CHAINS), axis=0)
        deltas = [jnp.sum(d.astype(F32) * o_rep, axis=1, keepdims=True) for d in dos]
        n_tiles = (i * Q_BLOCK) // SB_KT + 1

        def tile(t, carry, masked):
            rows = _sb_rows(t)
            kb = k_ref[rows, :].astype(BF16)
            vb = v_ref[rows, :].astype(BF16)
            past = _sb_past(i, t, n_rows) if masked else None
            dq, cs, ces = carry[0], carry[1:1 + SB_BWD_CHAINS], carry[1 + SB_BWD_CHAINS:]
            zs = [lax.dot_general(g, kb, NT, preferred_element_type=F32) for g in qs]
            gvs = [lax.dot_general(d, vb, NT, preferred_element_type=F32) for d in dos]
            logs = [_sb_log_terms(z, past) for z in zs]
            splits = [_split_stack(lk) for _, lk in logs]
            ys = [lax.dot_general(x, tri, NN, preferred_element_type=F32) for _, x in splits]
            sums = [_suffix_from(y, blocks, c) for y, (blocks, _), c in zip(ys, splits, cs)]
            wbs = [_sb_weights(lsz, after, past).astype(BF16) for (lsz, _), (after, _) in zip(logs, sums)]
            es = [wb.astype(F32) * gv for wb, gv in zip(wbs, gvs)]
            esplits = [_split_stack(e) for e in es]
            eys = [lax.dot_general(x, tri_incl, NN, preferred_element_type=F32) for _, x in esplits]
            esums = [_suffix_from(y, blocks, ce) for y, (blocks, _), ce in zip(eys, esplits, ces)]
            dzbs = []
            for e, (lsz, lk), (suf, _), delta in zip(es, logs, esums, deltas):
                dz = e * jnp.exp(lk) - (delta - suf) * jnp.exp(lsz)
                dzbs.append((dz if past is None else jnp.where(past, dz, 0.0)).astype(BF16))
            dk_t = dv_t = None
            for m, dzb, wb, g, d in zip(masks, dzbs, wbs, qs, dos):
                dq = dq + _unstack_heads(lax.dot_general(dzb, kb, NN, preferred_element_type=F32), m)
                a = lax.dot_general(dzb, g, TN, preferred_element_type=F32)
                b = lax.dot_general(wb, d, TN, preferred_element_type=F32)
                dk_t = a if dk_t is None else dk_t + a
                dv_t = b if dv_t is None else dv_t + b
            dk_ref[rows, :] = dk_ref[rows, :] + dk_t
            dv_ref[rows, :] = dv_ref[rows, :] + dv_t
            return (dq, *[c for _, c in sums], *[c for _, c in esums])

        zcol = jnp.zeros((n_rows, 1), F32)
        carry = tile(n_tiles - 1, (jnp.zeros((Q_BLOCK, D_ATT), F32),) + (zcol,) * (2 * SB_BWD_CHAINS), True)
        carry = lax.fori_loop(0, n_tiles - 1, lambda tt, cr: tile(n_tiles - 2 - tt, cr, False), carry)
        dq_ref[...] = carry[0] * (HEAD_DIM ** -0.5)

    cb = COL_QS // D_ATT
    blk = pl.BlockSpec((Q_BLOCK, D_ATT), lambda i: (i, 0))
    full = pl.BlockSpec((T, D_ATT), lambda i: (0, 0))
    sh = jax.ShapeDtypeStruct((T, D_ATT), F32)
    return pl.pallas_call(
        body, name="sb_bwd", grid=(T // Q_BLOCK,),
        in_specs=[pl.BlockSpec((Q_BLOCK, D_ATT), lambda i: (i, cb)),
                  pl.BlockSpec((T, D_ATT), lambda i: (0, cb + 1)),
                  pl.BlockSpec((T, D_ATT), lambda i: (0, cb + 2)), blk, blk],
        out_specs=[blk, full, full], out_shape=[sh, sh, sh],
        compiler_params=_params(dimension_semantics=("arbitrary",)),
    )(proj, proj, proj, do, o)


def _tok(c, by=None):
    if by is None:
        return pl.BlockSpec((TM, c), lambda i, j, k: (i, 0))
    if by == 1:
        return pl.BlockSpec((TM, c), lambda i, j, k: (i, j))
    return pl.BlockSpec((TM, c), lambda i, j, k: (i, k))


def _chunked(c, by):
    if by == 1:
        return pl.BlockSpec((None, TM, c), lambda i, j, k: (j, i, 0))
    return pl.BlockSpec((None, TM, c), lambda i, j, k: (k, i, 0))


def _gain_spec():
    return pl.BlockSpec((1, D_MODEL), lambda i, j, k: (0, 0))


def _all_chunks(rows, c):
    return pl.BlockSpec((N_CHIPS, rows, c), lambda i, j, k: (0, i, 0))


def _wfull(r, c, l):
    return pl.BlockSpec((N_CHIPS, None, r, c), lambda i, j, k: (0, l, 0, 0))


def _pick(idx, c):
    return lambda ins: ins[idx][c]


def _cols(idx, c, w):
    return lambda ins: ins[idx][:, c * w:(c + 1) * w]


def _rows(rows, width):
    return pl.BlockSpec((rows, width), lambda i, j, k: (i, 0))


def _whole(shape):
    return pl.BlockSpec(shape, lambda i, j, k: (0, 0))


def _ffn_fwd(x, gain, wg, wu, wd):
    T = x.shape[0]
    wg, wu, wd = (w.reshape(-1, D_MODEL) for w in (wg, wu, wd))
    ff = wd.shape[0]
    tm = TM // 2
    normed = _normed(0, 3)

    def swiglu(vals, ins, outs, i):
        gt, up = vals
        s = _sigmoid(gt)
        sil = gt * s
        outs[0][...] = sil.astype(BF16)
        outs[1][...] = (up * (s * (1.0 + gt * (1.0 - s)))).astype(BF16)
        outs[2][...] = (sil * up).astype(BF16)
        outs[3][...] = normed(ins)

    ash = jax.ShapeDtypeStruct((T, ff), BF16)
    sil, up_dsil, act, h = _mm(
        "ffn_up", [x, wg, wu, gain], [_rows(tm, D_MODEL), _whole(wg.shape), _whole(wu.shape), _gain_spec()],
        [(normed, 1, 0), (normed, 2, 1)], 2, None, NT, (T // tm, 1, 1), swiglu,
        [ash] * 3 + [jax.ShapeDtypeStruct((T, D_MODEL), BF16)], [_rows(tm, ff)] * 3 + [_rows(tm, D_MODEL)])

    def resid(vals, ins, outs, i):
        outs[0][...] = ins[2][...] + 0.5 * vals[0]

    (y,) = _mm(
        "ffn_down", [act, wd, x], [_rows(TM, ff), _whole(wd.shape), _tok(D_MODEL)], [(0, 1, 0)], 1, None, NN,
        (T // TM, 1, 1), resid, [jax.ShapeDtypeStruct((T, D_MODEL), F32)], [_tok(D_MODEL)])
    return y, (x, h, sil, up_dsil, act)


def _ffn_bwd(dxo, gain, wg, wu, wd, saved):
    x, h, sil, up_dsil, act = saved
    T = x.shape[0]
    n_chips, _, ffs, _ = wd.shape
    wg, wu, wd = (w.reshape(-1, D_MODEL) for w in (wg, wu, wd))
    ff = wd.shape[0]
    tk = TM
    tm = TM // 2

    def dswiglu(vals, ins, outs, i):
        da = 0.5 * vals[0]
        outs[0][...] = (da * ins[3][...].astype(F32)).astype(BF16)
        outs[1][...] = (da * ins[2][...].astype(F32)).astype(BF16)

    ash = jax.ShapeDtypeStruct((T, ff), BF16)
    dgate, dup = _mm(
        "ffn_dact", [dxo, wd, sil, up_dsil], [_rows(tm, D_MODEL), _whole(wd.shape), _rows(tm, ff), _rows(tm, ff)],
        [(0, 1, 0)], 1, None, NT, (T // tm, 1, 1), dswiglu, [ash, ash], [_rows(tm, ff)] * 2)

    def half(vals, ins, outs, i):
        outs[0][...] = (0.5 * vals[0]).astype(BF16)

    def cast(vals, ins, outs, i):
        outs[0][...] = vals[0].astype(BF16)

    tok_k = pl.BlockSpec((tk, D_MODEL), lambda i, j, k: (k, 0))
    hid_k = pl.BlockSpec((tk, ff), lambda i, j, k: (k, 0))
    wsh = jax.ShapeDtypeStruct((ff, D_MODEL), BF16)
    (dwd,) = _mm("ffn_dwd", [act, dxo], [hid_k, tok_k], [(0, 1, 0)], 1, (ff, D_MODEL), TN, (1, 1, T // tk), half,
                 [wsh], [_whole((ff, D_MODEL))])

    dx, dgain = _mm(
        "ffn_dx", [dgate, dup, wg, wu, x, gain, dxo],
        [_rows(tm, ff), _rows(tm, ff), _whole(wg.shape), _whole(wu.shape), _rows(tm, D_MODEL), _gain_spec(),
         _rows(tm, D_MODEL)],
        [(0, 2, 0), (1, 3, 0)], 1, None, NN, (T // tm, 1, 1), _rms_bwd_epilogue(4, 5, 6),
        [jax.ShapeDtypeStruct((T, D_MODEL), F32), jax.ShapeDtypeStruct((8, D_MODEL), F32)],
        [_rows(tm, D_MODEL), pl.BlockSpec((8, D_MODEL), lambda i, j, k: (0, 0))])

    dws = []
    for dact in (dgate, dup):
        dws += _mm("ffn_dwgu", [dact, h], [hid_k, tok_k], [(0, 1, 0)], 1, (ff, D_MODEL), TN, (1, 1, T // tk), cast,
                   [wsh], [_whole((ff, D_MODEL))])
    dwg, dwu, dwd = (w.reshape(n_chips, ffs, D_MODEL) for w in (dws[0], dws[1], dwd))
    return dx, dgain, dwg, dwu, dwd


def _joined_mixer_weights(wpd, wps, wo):
    n, _, r, c = wpd.shape
    wpd_n, wps_n = (w[:, 0].transpose(1, 0, 2).reshape(r, n * c) for w in (wpd, wps))
    return wpd_n, wps_n, wo.reshape(-1, wo.shape[3])


def _mixer_fwd(x, gain, W, l, tabs):
    T = x.shape[0]
    win, wpd, wps, wo = W["w_in"], W["w_proj_dil"], W["w_proj_sb"], W["w_out"]
    cin = win.shape[3]
    cp = wpd.shape[3]
    normed = _normed(0, 5)
    n_rope = 6 * D_ATT

    tm = TM // 2

    def roped(vals, ins, outs, i):
        for j, v in enumerate(vals):
            lo = j * cin
            k = min(max(n_rope - lo, 0), cin)
            if k:
                tab = [jnp.concatenate([ins[t][...]] * (k // 128), axis=1) for t in (2, 3, 4)]
                outs[0][:, lo:lo + k] = _rope_fwd(v[:, :k], *tab)
            if k < cin:
                outs[0][:, lo + k:lo + cin] = v[:, k:]
        outs[1][...] = normed(ins)

    proj, h = _mm(
        "mix_in", [x, win, *tabs, gain],
        [_rows(tm, D_MODEL), _wfull(D_MODEL, cin, l)] + [_rows(tm, 128)] * 3 + [_gain_spec()],
        [(normed, _pick(1, c), c) for c in range(N_CHIPS)], N_CHIPS, None, NN, (T // tm, 1, 1), roped,
        [jax.ShapeDtypeStruct((T, N_CHIPS * cin), F32), jax.ShapeDtypeStruct((T, D_MODEL), BF16)],
        [_rows(tm, N_CHIPS * cin), _rows(tm, D_MODEL)])

    os_, lses = [], []
    for g, (window, dil) in enumerate(DIL_GROUPS):
        o_g, lse_g = _dil_fwd(proj, g, dil)
        os_.append(o_g)
        lses.append(lse_g)
    o_dil, lse = _dil_merge(os_, lses)
    o_sb = _sb_fwd(proj)

    def gated(vals, ins, outs, i):
        pd, ps = vals
        outs[0][...] = (_sigmoid(ins[4][...]) * pd + _sigmoid(ins[5][...]) * ps).astype(BF16)
        outs[1][...] = pd.astype(BF16)
        outs[2][...] = ps.astype(BF16)

    wpd_n, wps_n, wo_n = _joined_mixer_weights(wpd, wps, wo)
    gd_spec = pl.BlockSpec((TM, D_MODEL), lambda i, j, k: (i, COL_GD // D_MODEL))
    gs_spec = pl.BlockSpec((TM, D_MODEL), lambda i, j, k: (i, COL_GS // D_MODEL))
    ush = jax.ShapeDtypeStruct((T, D_MODEL), BF16)
    u, pd, ps = _mm(
        "mix_gate", [o_dil, o_sb, wpd_n, wps_n, proj, proj],
        [_tok(D_ATT), _tok(D_ATT), _whole(wpd_n.shape), _whole(wps_n.shape), gd_spec, gs_spec],
        [(0, 2, 0), (1, 3, 1)], 2, None, NN, (T // TM, 1, 1), gated, [ush] * 3, [_tok(D_MODEL)] * 3)

    def resid(vals, ins, outs, i):
        outs[0][...] = ins[2][...] + vals[0]

    (y,) = _mm(
        "mix_out", [u, wo_n, x], [_tok(D_MODEL), _whole(wo_n.shape), _tok(D_MODEL)], [(0, 1, 0)], 1, None, NN,
        (T // TM, 1, 1), resid, [jax.ShapeDtypeStruct((T, D_MODEL), F32)], [_tok(D_MODEL)])
    return y, (x, h, proj, o_dil, lse, o_sb, u, pd, ps)


def _mixer_bwd(dxo, gain, W, l, tabs, saved):
    x, h, proj, o_dil, lse, o_sb, u, pd, ps = saved
    T = x.shape[0]
    win, wpd, wps, wo = W["w_in"], W["w_proj_dil"], W["w_proj_sb"], W["w_out"]
    cin = win.shape[3]
    cp = wpd.shape[3]
    tk = TM
    tm = TM // 2
    row = pl.BlockSpec((tm, D_MODEL), lambda i, j, k: (i, 0))

    def dgated(vals, ins, outs, i):
        du = vals[0]
        sd = _sigmoid(ins[4][...])
        ss = _sigmoid(ins[5][...])
        outs[0][...] = (du * sd).astype(BF16)
        outs[1][...] = (du * ss).astype(BF16)
        outs[2][...] = (du * ins[2][...].astype(F32) * sd * (1.0 - sd)).astype(BF16)
        outs[3][...] = (du * ins[3][...].astype(F32) * ss * (1.0 - ss)).astype(BF16)

    wpd_n, wps_n, wo_n = _joined_mixer_weights(wpd, wps, wo)
    gd_spec = pl.BlockSpec((TM, D_MODEL), lambda i, j, k: (i, COL_GD // D_MODEL))
    gs_spec = pl.BlockSpec((TM, D_MODEL), lambda i, j, k: (i, COL_GS // D_MODEL))
    ush = jax.ShapeDtypeStruct((T, D_MODEL), BF16)
    dpd, dps, dgd, dgs = _mm(
        "mix_du", [dxo, wo_n, pd, ps, proj, proj],
        [_tok(D_MODEL), _whole(wo_n.shape), _tok(D_MODEL), _tok(D_MODEL), gd_spec, gs_spec],
        [(0, 1, 0)], 1, None, NT, (T // TM, 1, 1), dgated, [ush] * 4, [_tok(D_MODEL)] * 4)

    def one(vals, ins, outs, i):
        outs[0][...] = vals[0].astype(BF16)

    def two(vals, ins, outs, i):
        outs[0][...] = vals[0].astype(BF16)
        outs[1][...] = vals[1].astype(BF16)

    tok_k = pl.BlockSpec((tk, D_MODEL), lambda i, j, k: (k, 0))
    att_k = pl.BlockSpec((tk, D_ATT), lambda i, j, k: (k, 0))
    (dwo_n,) = _mm("mix_dwo", [u, dxo], [tok_k, tok_k], [(0, 1, 0)], 1, (D_MODEL, D_MODEL), TN, (1, 1, T // tk), one,
                   [jax.ShapeDtypeStruct((D_MODEL, D_MODEL), BF16)], [_whole((D_MODEL, D_MODEL))])

    def plain2(vals, ins, outs, i):
        outs[0][...] = vals[0]
        outs[1][...] = vals[1]

    ash = jax.ShapeDtypeStruct((T, D_ATT), F32)
    do_dil, do_sb = _mm(
        "mix_do", [dpd, dps, wpd_n, wps_n], [_tok(D_MODEL), _tok(D_MODEL), _whole(wpd_n.shape), _whole(wps_n.shape)],
        [(0, 2, 0), (1, 3, 1)], 2, None, NT, (T // TM, 1, 1), plain2, [ash, ash], [_tok(D_ATT)] * 2)

    psh = jax.ShapeDtypeStruct((D_ATT, D_MODEL), BF16)
    dwpd_n, dwps_n = _mm(
        "mix_dwp", [o_dil, o_sb, dpd, dps], [att_k, att_k, tok_k, tok_k], [(0, 2, 0), (1, 3, 1)], 2,
        (D_ATT, D_MODEL), TN, (1, 1, T // tk), two, [psh, psh], [_whole((D_ATT, D_MODEL))] * 2)
    dwpd, dwps = (w.reshape(D_ATT, N_CHIPS, cp).transpose(1, 0, 2) for w in (dwpd_n, dwps_n))
    dwo = dwo_n.reshape(N_CHIPS, cp, D_MODEL)

    dqs, dks, dvs = [], [], []
    for g, (window, dil) in enumerate(DIL_GROUPS):
        dq, dk, dv = _dil_bwd(proj, do_dil, o_dil, lse, g, dil)
        dqs.append(dq)
        dks.append(dk)
        dvs.append(dv)
    dq_s, dk_s, dv_s = _sb_bwd(proj, do_sb, o_sb)
    dproj = _assemble_dproj(dqs + dks, dvs + [dq_s, dk_s, dv_s], [dgd, dgs], tabs)

    dx, dgain = _mm(
        "mix_dx", [dproj, win, x, gain, dxo],
        [pl.BlockSpec((tm, N_CHIPS * cin), lambda i, j, k: (i, 0)), _wfull(D_MODEL, cin, l), row, _gain_spec(), row],
        [(_cols(0, c, cin), _pick(1, c), 0) for c in range(N_CHIPS)], 1, None, NT, (T // tm, 1, 1),
        _rms_bwd_epilogue(2, 3, 4),
        [jax.ShapeDtypeStruct((T, D_MODEL), F32), jax.ShapeDtypeStruct((8, D_MODEL), F32)],
        [row, pl.BlockSpec((8, D_MODEL), lambda i, j, k: (0, 0))])

    (dwin,) = _mm(
        "mix_dwin", [h, dproj],
        [pl.BlockSpec((tk, D_MODEL), lambda i, j, k: (k, 0)), pl.BlockSpec((tk, cin), lambda i, j, k: (k, j))],
        [(0, 1, 0)], 1, (D_MODEL, cin), TN, (1, N_CHIPS, T // tk), one,
        [jax.ShapeDtypeStruct((N_CHIPS, D_MODEL, cin), BF16)],
        [pl.BlockSpec((None, D_MODEL, cin), lambda i, j, k: (j, 0, 0))])
    return dx, dgain, dwin, dwpd, dwps, dwo


def _local_step(x, target, norms, norm_final, weights_of, on_grads):
    T = x.shape[0]
    tabs = _rope_tables(T)
    saved, held = [], []
    for l in range(DEPTH):
        w1 = weights_of(l, 0, x)
        x, s1 = _ffn_fwd(x, norms["norm_ffn1"][l:l + 1], w1["ffn1_w_gate"], w1["ffn1_w_up"], w1["ffn1_w_down"])
        w2 = weights_of(l, 1, x)
        x, s2 = _mixer_fwd(x, norms["norm_mix"][l:l + 1], w2, 0, tabs)
        w3 = weights_of(l, 2, x)
        x, s3 = _ffn_fwd(x, norms["norm_ffn2"][l:l + 1], w3["ffn2_w_gate"], w3["ffn2_w_up"], w3["ffn2_w_down"])
        saved.append((s1, s2, s3))
        held.append((w1, w2, w3))
    dx, dg_final, loss = _final_loss(x, norm_final.reshape(1, D_MODEL), target)
    gains = [None] * DEPTH
    for l in reversed(range(DEPTH)):
        s1, s2, s3 = saved[l]
        w1, w2, w3 = held[l]
        dx, dg2, dwg2, dwu2, dwd2 = _ffn_bwd(dx, norms["norm_ffn2"][l:l + 1], w3["ffn2_w_gate"], w3["ffn2_w_up"],
                                             w3["ffn2_w_down"], s3)
        dx = on_grads(l, 2, dict(ffn2_w_gate=dwg2, ffn2_w_up=dwu2, ffn2_w_down=dwd2), dx)
        dx, dgm, dwin, dwpd, dwps, dwo = _mixer_bwd(dx, norms["norm_mix"][l:l + 1], w2, 0, tabs, s2)
        dx = on_grads(l, 1, dict(w_in=dwin, w_proj_dil=dwpd, w_proj_sb=dwps, w_out=dwo), dx)
        dx, dg1, dwg1, dwu1, dwd1 = _ffn_bwd(dx, norms["norm_ffn1"][l:l + 1], w1["ffn1_w_gate"], w1["ffn1_w_up"],
                                             w1["ffn1_w_down"], s1)
        dx = on_grads(l, 0, dict(ffn1_w_gate=dwg1, ffn1_w_up=dwu1, ffn1_w_down=dwd1), dx)
        gains[l] = dict(norm_ffn1=dg1, norm_mix=dgm, norm_ffn2=dg2)
    return loss, dx, gains, dg_final


def _place():
    x, y, c = lax.axis_index("x"), lax.axis_index("y"), lax.axis_index("c")
    chips = [(1 - x, y), (x, 1 - y), (1 - x, 1 - y)]
    return x, y, c, chips


def _half(c, r):
    return pl.ds(pl.multiple_of(c * (r // 2), 8), r // 2)


def _cast_into_slot(ws, ls, me_arr, after):
    n = len(ws)
    late = [] if after is None else [after]

    def body(me_ref, *refs):
        for a in range(n):
            refs[len(refs) - n + a][...] = refs[a][...].astype(BF16)

    def src(w, l):
        return pl.BlockSpec((None, w.shape[1] // 4, w.shape[2]), lambda i, me: (l, i, 0))

    def dst(w):
        return pl.BlockSpec((None, None, w.shape[1] // 4, w.shape[2]), lambda i, me: (me[0], 0, i, 0))

    return pl.pallas_call(
        body, name="cast_weights",
        grid_spec=pltpu.PrefetchScalarGridSpec(
            num_scalar_prefetch=1, grid=(4,),
            in_specs=[src(w, l) for w, l in zip(ws, ls)] + [pl.BlockSpec(memory_space=pl.ANY)] * len(late),
            out_specs=[dst(w) for w in ws]),
        out_shape=[jax.ShapeDtypeStruct((N_CHIPS, 1) + w.shape[1:], BF16) for w in ws], compiler_params=_params(),
    )(me_arr, *ws, *late)


HBM_SPEC = pl.BlockSpec(memory_space=pltpu.HBM)
SEM_SPEC = pl.BlockSpec(memory_space=pltpu.SEMAPHORE)
SPLIT_COPY = pltpu.CompilerParams(has_side_effects=pltpu.SideEffectType.DATAFLOW_SIDE_EFFECTING)


def _gather_piece(ref, chip_id, c):
    return ref.at[chip_id, 0, _half(c, ref.shape[2]), :]


def _gather_start(tag, bufs):
    n = len(bufs)

    def body(*refs):
        out_refs = refs[n:2 * n]
        send_sems, recv_sems, token = refs[2 * n:]
        x, y, c, chips = _place()
        me = 2 * x + y
        for a in range(n):
            piece = _gather_piece(out_refs[a], me, c)
            for j, chip in enumerate(chips):
                pltpu.make_async_remote_copy(
                    src_ref=piece, dst_ref=piece, send_sem=send_sems.at[3 * a + j], recv_sem=recv_sems.at[3 * a + j],
                    device_id=(*chip, c), device_id_type=MESH).start()
        token[...] = jnp.zeros_like(token)

    outs = pl.pallas_call(
        body, name=f"gather_start_{tag}", in_specs=[HBM_SPEC] * n,
        out_specs=[HBM_SPEC] * n + [SEM_SPEC, SEM_SPEC, pl.BlockSpec(memory_space=pltpu.VMEM)],
        out_shape=[pltpu.HBM(b.shape, b.dtype) for b in bufs] + [pltpu.SemaphoreType.DMA((3 * n,))] * 2
        + [jax.ShapeDtypeStruct((8, 128), F32)],
        input_output_aliases={a: a for a in range(n)}, compiler_params=SPLIT_COPY,
    )(*[pltpu.with_memory_space_constraint(b, pltpu.HBM) for b in bufs])
    return outs[:n], outs[n], outs[n + 1], outs[n + 2]


def _gather_wait(k, bufs, places, send_sems, recv_sems, after):
    m = len(bufs)

    def body(*refs):
        in_refs = refs[:m]
        ssem, rsem = refs[m], refs[m + 1]
        x, y, c, chips = _place()
        me = 2 * x + y
        for t, a in enumerate(places):
            for j, chip in enumerate(chips):
                cp = pltpu.make_async_remote_copy(
                    src_ref=_gather_piece(in_refs[t], me, c),
                    dst_ref=_gather_piece(in_refs[t], 2 * chip[0] + chip[1], c),
                    send_sem=ssem.at[3 * a + j], recv_sem=rsem.at[3 * a + j], device_id=(*chip, c),
                    device_id_type=MESH)
                cp.wait_send()
                cp.wait_recv()

    return pl.pallas_call(
        body, name=f"gather_wait_{k}",
        in_specs=[HBM_SPEC] * m + [SEM_SPEC, SEM_SPEC, pl.BlockSpec(memory_space=pl.ANY)], out_specs=[HBM_SPEC] * m,
        out_shape=[pltpu.HBM(b.shape, b.dtype) for b in bufs], input_output_aliases={t: t for t in range(m)},
        compiler_params=SPLIT_COPY,
    )(*bufs, send_sems, recv_sems, after)


def _gather_relay(bufs):
    n = len(bufs)

    def body(*refs):
        out_refs = refs[n:2 * n]
        send_sems, recv_sems = refs[2 * n:]
        x, y, c, chips = _place()
        cps = []
        for a in range(n):
            for j, chip in enumerate(chips):
                piece = _gather_piece(out_refs[a], 2 * chip[0] + chip[1], c)
                cps.append(pltpu.make_async_remote_copy(
                    src_ref=piece, dst_ref=piece, send_sem=send_sems.at[a, j], recv_sem=recv_sems.at[a, j],
                    device_id=(x, y, 1 - c), device_id_type=MESH))
        for cp in cps:
            cp.start()
        for a in range(n):
            for j, chip in enumerate(chips):
                theirs = _gather_piece(out_refs[a], 2 * chip[0] + chip[1], 1 - c)
                pltpu.make_async_remote_copy(
                    src_ref=theirs, dst_ref=theirs, send_sem=send_sems.at[a, j], recv_sem=recv_sems.at[a, j],
                    device_id=(x, y, 1 - c), device_id_type=MESH).wait_recv()
        for cp in cps:
            cp.wait_send()

    any_spec = pl.BlockSpec(memory_space=pl.ANY)
    return pl.pallas_call(
        body, name="gather_relay", in_specs=[any_spec] * n, out_specs=[any_spec] * n,
        out_shape=[jax.ShapeDtypeStruct(b.shape, b.dtype) for b in bufs],
        input_output_aliases={a: a for a in range(n)},
        scratch_shapes=[pltpu.SemaphoreType.DMA((n, 3))] * 2,
    )(*bufs)


def _exchange_halves(gs):
    n = len(gs)

    def body(*refs):
        g_refs, out_refs = refs[:n], refs[n:2 * n]
        send_sems, recv_sems = refs[2 * n:]
        x, y, c, _ = _place()
        cps = []
        for a in range(n):
            r = g_refs[a].shape[1]
            cps.append(pltpu.make_async_remote_copy(
                src_ref=g_refs[a].at[:, _half(1 - c, r), :], dst_ref=out_refs[a],
                send_sem=send_sems.at[a], recv_sem=recv_sems.at[a], device_id=(x, y, 1 - c), device_id_type=MESH))
        for cp in cps:
            cp.start()
        for cp in cps:
            cp.wait()

    any_spec = pl.BlockSpec(memory_space=pl.ANY)
    return pl.pallas_call(
        body, name="grad_to_sibling", in_specs=[any_spec] * n, out_specs=[any_spec] * n,
        out_shape=[jax.ShapeDtypeStruct((g.shape[0], g.shape[1] // 2, g.shape[2]), g.dtype) for g in gs],
        scratch_shapes=[pltpu.SemaphoreType.DMA((n,))] * 2,
    )(*gs)


def _add_half(gs, gots, c_arr):
    n = len(gs)

    def body(c_ref, *refs):
        for a in range(n):
            refs[2 * n + a][...] = (refs[a][...].astype(F32) + refs[n + a][...].astype(F32)).astype(BF16)

    def own(g):
        return pl.BlockSpec((None, g.shape[1] // 2, g.shape[2]), lambda k, cr: (k, cr[0], 0))

    def half(g):
        return pl.BlockSpec((None, g.shape[1] // 2, g.shape[2]), lambda k, cr: (k, 0, 0))

    return pl.pallas_call(
        body, name="grad_add_half",
        grid_spec=pltpu.PrefetchScalarGridSpec(
            num_scalar_prefetch=1, grid=(N_CHIPS,),
            in_specs=[own(g) for g in gs] + [half(g) for g in gs], out_specs=[half(g) for g in gs]),
        out_shape=[jax.ShapeDtypeStruct(got.shape, BF16) for got in gots], compiler_params=_params(),
    )(c_arr, *gs, *gots)


def _scatter_start(k, ss, thru):
    n = len(ss)

    def body(*refs):
        s_refs, land_refs = refs[2 * n + 1:3 * n + 1], refs[3 * n + 1:4 * n + 1]
        send_sems, recv_sems = refs[4 * n + 2:]
        x, y, c, chips = _place()
        me = 2 * x + y
        for a in range(n):
            for j, chip in enumerate(chips):
                pltpu.make_async_remote_copy(
                    src_ref=s_refs[a].at[2 * chip[0] + chip[1]], dst_ref=land_refs[a].at[me],
                    send_sem=send_sems.at[3 * a + j], recv_sem=recv_sems.at[3 * a + j], device_id=(*chip, c),
                    device_id_type=MESH).start()

    lands = [lax.empty(s.shape, s.dtype) for s in ss]
    hbm = [pltpu.HBM(s.shape, s.dtype) for s in ss]
    outs = pl.pallas_call(
        body, name=f"grad_scatter_start_{k}", in_specs=[HBM_SPEC] * (2 * n + 1),
        out_specs=[HBM_SPEC] * (2 * n + 1) + [SEM_SPEC, SEM_SPEC],
        out_shape=hbm + hbm + [pltpu.HBM(thru.shape, thru.dtype)] + [pltpu.SemaphoreType.DMA((3 * n,))] * 2,
        input_output_aliases={a: a for a in range(2 * n + 1)}, compiler_params=SPLIT_COPY,
    )(*[pltpu.with_memory_space_constraint(v, pltpu.HBM) for v in list(ss) + lands + [thru]])
    return (outs[:n], outs[n:2 * n], outs[2 * n + 1], outs[2 * n + 2]), outs[2 * n]


def _scatter_wait(k, ss, lands, send_sems, recv_sems, after):
    n = len(ss)

    def body(*refs):
        s_refs, land_refs = refs[:n], refs[n:2 * n]
        ssem, rsem = refs[2 * n], refs[2 * n + 1]
        x, y, c, chips = _place()
        me = 2 * x + y
        for a in range(n):
            for j, chip in enumerate(chips):
                cid = 2 * chip[0] + chip[1]
                cp = pltpu.make_async_remote_copy(
                    src_ref=s_refs[a].at[cid], dst_ref=land_refs[a].at[cid], send_sem=ssem.at[3 * a + j],
                    recv_sem=rsem.at[3 * a + j], device_id=(*chip, c), device_id_type=MESH)
                cp.wait_send()
                cp.wait_recv()

    hbm = [pltpu.HBM(s.shape, s.dtype) for s in ss]
    outs = pl.pallas_call(
        body, name=f"grad_scatter_wait_{k}",
        in_specs=[HBM_SPEC] * (2 * n) + [SEM_SPEC, SEM_SPEC, pl.BlockSpec(memory_space=pl.ANY)],
        out_specs=[HBM_SPEC] * (2 * n), out_shape=hbm + hbm,
        input_output_aliases={a: a for a in range(2 * n)}, compiler_params=SPLIT_COPY,
    )(*ss, *lands, send_sems, recv_sems, after)
    return outs[:n], outs[n:]


def _sum_chips(lands, ss, me_arr):
    n = len(lands)

    def body(me_ref, *refs):
        for own in range(N_CHIPS):
            @pl.when(me_ref[0] == own)
            def _(own=own):
                for a in range(n):
                    acc = None
                    for k in range(N_CHIPS):
                        term = (refs[n + a][...] if k == own else refs[a][k]).astype(F32)
                        acc = term if acc is None else acc + term
                    refs[2 * n + a][...] = acc

    return pl.pallas_call(
        body, name="grad_sum_chips",
        grid_spec=pltpu.PrefetchScalarGridSpec(
            num_scalar_prefetch=1, grid=(1,),
            in_specs=[pl.BlockSpec(la.shape, lambda i, me: (0, 0, 0)) for la in lands]
            + [pl.BlockSpec((None,) + la.shape[1:], lambda i, me: (me[0], 0, 0)) for la in lands],
            out_specs=[pl.BlockSpec(la.shape[1:], lambda i, me: (0, 0)) for la in lands]),
        out_shape=[jax.ShapeDtypeStruct(la.shape[1:], F32) for la in lands], compiler_params=_params(),
    )(me_arr, *lands, *ss)


def _swap_halves(fs):
    n = len(fs)

    def body(*refs):
        f_refs, out_refs = refs[:n], refs[n:2 * n]
        send_sems, recv_sems = refs[2 * n:]
        x, y, c, _ = _place()
        cps = [pltpu.make_async_remote_copy(
            src_ref=f_refs[a], dst_ref=out_refs[a], send_sem=send_sems.at[a], recv_sem=recv_sems.at[a],
            device_id=(x, y, 1 - c), device_id_type=MESH) for a in range(n)]
        for cp in cps:
            cp.start()
        for cp in cps:
            cp.wait()

    any_spec = pl.BlockSpec(memory_space=pl.ANY)
    return pl.pallas_call(
        body, name="grad_swap_halves", in_specs=[any_spec] * n, out_specs=[any_spec] * n,
        out_shape=[jax.ShapeDtypeStruct(f.shape, f.dtype) for f in fs],
        scratch_shapes=[pltpu.SemaphoreType.DMA((n,))] * 2,
    )(*fs)


def _allreduce_rows(stats):
    def body(s_ref, o_ref, buf, send_sems, recv_sems):
        x, y, c, _ = _place()
        me = 4 * x + 2 * y + c
        buf[me] = s_ref[...]
        cps = []
        for k in range(1, 8):
            px = jnp.where(k & 4, 1 - x, x)
            py = jnp.where(k & 2, 1 - y, y)
            pc = jnp.where(k & 1, 1 - c, c)
            cps.append(pltpu.make_async_remote_copy(
                src_ref=s_ref, dst_ref=buf.at[me], send_sem=send_sems.at[k - 1], recv_sem=recv_sems.at[k - 1],
                device_id=(px, py, pc), device_id_type=MESH))
        for cp in cps:
            cp.start()
        for cp in cps:
            cp.wait()
        acc = buf[0]
        for d in range(1, 8):
            acc = acc + buf[d]
        o_ref[...] = acc

    vm = pl.BlockSpec(memory_space=pltpu.VMEM)
    return pl.pallas_call(
        body, name="allreduce_rows", in_specs=[vm], out_specs=vm,
        out_shape=jax.ShapeDtypeStruct(stats.shape, F32),
        scratch_shapes=[pltpu.VMEM((8,) + stats.shape, F32), pltpu.SemaphoreType.DMA((7,)),
                        pltpu.SemaphoreType.DMA((7,))],
    )(stats)


def _adamw_math(w, g, m, v):
    m = ADAM_B1 * m + (1.0 - ADAM_B1) * g
    v = ADAM_B2 * v + (1.0 - ADAM_B2) * (g * g)
    m_hat = m / (1.0 - ADAM_B1 ** ADAM_STEP)
    v_hat = v / (1.0 - ADAM_B2 ** ADAM_STEP)
    delta = -ADAM_LR * (m_hat / (jnp.sqrt(v_hat) + ADAM_EPS) + ADAM_WD * w)
    return delta, m, v


def _adamw(ws, ms, vs, mines, theirs, l, c_arr, earlier):
    n = len(ws)
    held = [t for e in earlier if e is not None for t in e]
    assert len(held) in (0, 4 * n)

    def body(c_ref, *refs):
        outs = refs[len(refs) - 4 * n:]
        for a in range(n):
            w_ref, m_ref, v_ref, a_ref, b_ref = refs[5 * a:5 * a + 5]
            g = jnp.where(pl.program_id(0) == c_ref[0], a_ref[...], b_ref[...])
            delta, mn, vn = _adamw_math(w_ref[...], g, m_ref[...], v_ref[...])
            outs[4 * a][...] = g
            outs[4 * a + 1][...] = delta
            outs[4 * a + 2][...] = mn
            outs[4 * a + 3][...] = vn

    def blk(w):
        tr = w.shape[1] // 4
        return pl.BlockSpec((None, tr, w.shape[2]), lambda hh, i, cr: (l, 2 * hh + i, 0))

    def half(w):
        return pl.BlockSpec((w.shape[1] // 4, w.shape[2]), lambda hh, i, cr: (i, 0))

    outs = pl.pallas_call(
        body, name="adamw",
        grid_spec=pltpu.PrefetchScalarGridSpec(
            num_scalar_prefetch=1, grid=(2, 2),
            in_specs=[sp for w in ws for sp in (blk(w), blk(w), blk(w), half(w), half(w))]
            + [pl.BlockSpec(memory_space=pl.ANY)] * len(held),
            out_specs=[blk(w) for w in ws for _ in range(4)]),
        out_shape=[jax.ShapeDtypeStruct(w.shape, F32) for w in ws for _ in range(4)],
        input_output_aliases={1 + 5 * n + t: t for t in range(len(held))}, compiler_params=_params(),
    )(c_arr, *[t for grp in zip(ws, ms, vs, mines, theirs) for t in grp], *held)
    return [outs[4 * a:4 * a + 4] for a in range(n)]


def _adamw_rows(w, m, v, g):
    def body(w_ref, m_ref, v_ref, g_ref, d_ref, mo_ref, vo_ref):
        delta, mn, vn = _adamw_math(w_ref[...], g_ref[...], m_ref[...], v_ref[...])
        d_ref[...] = delta
        mo_ref[...] = mn
        vo_ref[...] = vn

    vm = pl.BlockSpec(memory_space=pltpu.VMEM)
    sh = jax.ShapeDtypeStruct(w.shape, F32)
    return pl.pallas_call(body, name="adamw_rows", in_specs=[vm] * 4, out_specs=[vm] * 3, out_shape=[sh] * 3)(w, m, v, g)


SUBLAYERS = (("ffn1_w_gate", "ffn1_w_up", "ffn1_w_down"), ("w_in", "w_proj_dil", "w_proj_sb", "w_out"),
             ("ffn2_w_gate", "ffn2_w_up", "ffn2_w_down"))
TRANSPOSED = ("ffn1_w_gate", "ffn1_w_up", "ffn2_w_gate", "ffn2_w_up")
LAG = 2


def _pick_row(blocks):
    row = lax.broadcasted_iota(jnp.int32, (8, D_MODEL), 0)
    out = jnp.zeros((8, D_MODEL), F32)
    for i, b in enumerate(blocks):
        out = out + jnp.where(row == i, b, 0.0)
    return out


def kernel(x, norm_ffn1, ffn1_w_gate, ffn1_w_up, ffn1_w_down, norm_mix, w_in, w_proj_dil, w_proj_sb, w_out, norm_ffn2, ffn2_w_gate, ffn2_w_up, ffn2_w_down, norm_final, loss_target, m_norm_ffn1, m_ffn1_w_gate, m_ffn1_w_up, m_ffn1_w_down, m_norm_mix, m_w_in, m_w_proj_dil, m_w_proj_sb, m_w_out, m_norm_ffn2, m_ffn2_w_gate, m_ffn2_w_up, m_ffn2_w_down, m_norm_final, v_norm_ffn1, v_ffn1_w_gate, v_ffn1_w_up, v_ffn1_w_down, v_norm_mix, v_w_in, v_w_proj_dil, v_w_proj_sb, v_w_out, v_norm_ffn2, v_ffn2_w_gate, v_ffn2_w_up, v_ffn2_w_down, v_norm_final):
    given = dict(locals())
    for n in TRANSPOSED:
        for k in ("", "m_", "v_"):
            given[k + n] = jnp.swapaxes(given[k + n], 1, 2)
    weights = {n: given[n] for n in WEIGHT_NAMES}
    norms = {n: given[n] for n in NORM_NAMES}

    c_arr = lax.axis_index("c").astype(jnp.int32).reshape(1)
    me_arr = (2 * lax.axis_index("x") + lax.axis_index("y")).astype(jnp.int32).reshape(1)
    order = [(l, s, n) for l in range(DEPTH) for s in range(len(SUBLAYERS)) for n in SUBLAYERS[s]]
    n_first = len(SUBLAYERS[0])
    sent, token = {}, None
    for tag, idxs in (("a", range(n_first)), ("b", range(n_first, len(order)))):
        cast = _cast_into_slot([weights[order[i][2]] for i in idxs], [order[i][0] for i in idxs], me_arr, token)
        bufs, send_sems, recv_sems, token = _gather_start(tag, cast)
        for p, i in enumerate(idxs):
            sent[i] = (bufs[p], p, send_sems, recv_sems)

    def weights_of(l, s, after):
        idxs = [i for i, (ll, ss, _) in enumerate(order) if (ll, ss) == (l, s)]
        got = _gather_wait(len(SUBLAYERS) * l + s, [sent[i][0] for i in idxs], [sent[i][1] for i in idxs],
                           sent[idxs[0]][2], sent[idxs[0]][3], after)
        return {order[i][2]: g for i, g in zip(idxs, _gather_relay(got))}

    out = {}
    in_flight = []

    def finish(l, s, names, sums, lands, ssem, rsem, after):
        sums, lands = _scatter_wait(len(SUBLAYERS) * l + s, sums, lands, ssem, rsem, after)
        mine = _sum_chips(lands, sums, me_arr)
        theirs = _swap_halves(mine)
        res = _adamw([weights[n] for n in names], [given["m_" + n] for n in names], [given["v_" + n] for n in names],
                     mine, theirs, l, c_arr, [out.get(n) for n in names])
        out.update(zip(names, res))

    def on_grads(l, s, grads, after):
        names = list(grads)
        gs = [grads[n] for n in names]
        sums = _add_half(gs, _exchange_halves(gs), c_arr)
        sent, after = _scatter_start(len(SUBLAYERS) * l + s, sums, after)
        in_flight.append((l, s, names) + sent)
        if len(in_flight) > LAG:
            finish(*in_flight.pop(0), after)
        return after

    loss_blk, grad_x, gains, dg_final = _local_step(x[0], loss_target[0], norms, norm_final, weights_of, on_grads)
    while in_flight:
        finish(*in_flight.pop(0), grad_x)
    out = {k + n: (jnp.swapaxes(v, 1, 2) if n in TRANSPOSED else v)
           for n, res in out.items() for k, v in zip(("grad_", "delta_", "new_m_", "new_v_"), res)}
    out["grad_x"] = grad_x[None]

    rows = [gains[l][n] for n in NORM_NAMES for l in range(DEPTH)] + [dg_final, loss_blk]
    total = _allreduce_rows(_pick_row(rows))
    out["loss"] = total[7, 0]
    wn = jnp.concatenate([given[n] for n in NORM_NAMES] + [norm_final[None], jnp.zeros((1, D_MODEL), F32)])
    mn_ = jnp.concatenate([given["m_" + n] for n in NORM_NAMES] + [m_norm_final[None], jnp.zeros((1, D_MODEL), F32)])
    vn_ = jnp.concatenate([given["v_" + n] for n in NORM_NAMES] + [v_norm_final[None], jnp.ones((1, D_MODEL), F32)])
    d_n, m_n, v_n = _adamw_rows(wn, mn_, vn_, total)
    for i, n in enumerate(NORM_NAMES):
        sl = slice(i * DEPTH, (i + 1) * DEPTH)
        out["grad_" + n], out["delta_" + n], out["new_m_" + n], out["new_v_" + n] = total[sl], d_n[sl], m_n[sl], v_n[sl]
    out["grad_norm_final"], out["delta_norm_final"] = total[6], d_n[6]
    out["new_m_norm_final"], out["new_v_norm_final"] = m_n[6], v_n[6]

    names = ["norm_ffn1", "ffn1_w_gate", "ffn1_w_up", "ffn1_w_down", "norm_mix", "w_in", "w_proj_dil", "w_proj_sb",
             "w_out", "norm_ffn2", "ffn2_w_gate", "ffn2_w_up", "ffn2_w_down", "norm_final"]
    return (out["loss"], out["grad_x"], *[out["grad_" + n] for n in names], *[out["delta_" + n] for n in names],
            *[out["new_m_" + n] for n in names], *[out["new_v_" + n] for n in names])
```

```python
import functools

import jax
import jax.numpy as jnp
from jax import lax
from jax.experimental import pallas as pl
from jax.experimental.pallas import tpu as pltpu

F32 = jnp.float32
BF16 = jnp.bfloat16

D_MODEL = 1024
DEPTH = 2
N_CHIPS = 4
HEAD_DIM = 64
ROPE_DIM = 16
ROPE_THETA = 500000.0
DIL_GROUPS = ((128, 1), (512, 4), (2048, 16))
SPAN = 128
Q_BLOCK = 128
RMS_EPS = 1e-6
D_ATT = 256
COL_QS = 2304
COL_GD = 3072
COL_GS = 4096
ADAM_LR, ADAM_B1, ADAM_B2, ADAM_EPS, ADAM_WD, ADAM_STEP = 0.001, 0.9, 0.999, 1e-08, 0.01, 10

VMEM_LIMIT = 52 * 1024 * 1024
TM = 512
NEG = -1e30

NN = (((1,), (0,)), ((), ()))
NT = (((1,), (1,)), ((), ()))
TN = (((0,), (0,)), ((), ()))
MESH = pl.DeviceIdType.MESH

WEIGHT_NAMES = ("ffn1_w_gate", "ffn1_w_up", "ffn1_w_down", "w_in", "w_proj_dil",
                "w_proj_sb", "w_out", "ffn2_w_gate", "ffn2_w_up", "ffn2_w_down")
NORM_NAMES = ("norm_ffn1", "norm_mix", "norm_ffn2")


def _params(**kw):
    return pltpu.CompilerParams(vmem_limit_bytes=VMEM_LIMIT, **kw)


def _sigmoid(x):
    return 0.5 * jnp.tanh(0.5 * x) + 0.5


def _mm_body(pairs, n_in, n_out, n_acc, dims, nk, i_axis, epilogue, *refs):
    ins = refs[:n_in]
    outs = refs[n_in:n_in + n_out]
    accs = refs[n_in + n_out:]
    i = pl.program_id(i_axis)
    k = pl.program_id(2)

    def operand(a):
        return (a(ins) if callable(a) else ins[a][...]).astype(BF16)

    def dot(ia, ib):
        return lax.dot_general(operand(ia), operand(ib), dims, preferred_element_type=F32)

    if nk == 1:
        parts = [None] * n_acc
        for ia, ib, ic in pairs:
            parts[ic] = dot(ia, ib) if parts[ic] is None else parts[ic] + dot(ia, ib)
        epilogue(parts, ins, outs, i)
        return

    @pl.when(k == 0)
    def _():
        for c in range(n_acc):
            accs[c][...] = jnp.zeros_like(accs[c])

    for ia, ib, ic in pairs:
        accs[ic][...] += dot(ia, ib)

    @pl.when(k == nk - 1)
    def _():
        epilogue([a[...] for a in accs], ins, outs, i)


def _j_outer(spec):
    f = spec.index_map
    return pl.BlockSpec(spec.block_shape, lambda j, i, k: f(i, j, k))


def _mm(name, ins, in_specs, pairs, n_acc, acc_shape, dims, grid, epilogue, out_shapes, out_specs, j_outer=False):
    nk = grid[2]
    if j_outer:
        grid = (grid[1], grid[0], grid[2])
        in_specs = [_j_outer(s) for s in in_specs]
        out_specs = [_j_outer(s) for s in out_specs]
    scratch = [pltpu.VMEM(acc_shape, F32) for _ in range(n_acc)] if nk > 1 else []
    body = functools.partial(_mm_body, tuple(pairs), len(ins), len(out_shapes), n_acc, dims, nk,
                             1 if j_outer else 0, epilogue)
    return pl.pallas_call(
        body, name=name, grid=grid, in_specs=in_specs, out_specs=out_specs, out_shape=out_shapes,
        scratch_shapes=scratch,
        compiler_params=_params(dimension_semantics=("arbitrary", "arbitrary", "arbitrary")),
    )(*ins)


def _wspec(r, c, l, by):
    if by == 1:
        return pl.BlockSpec((None, None, r, c), lambda i, j, k: (j, l, 0, 0))
    return pl.BlockSpec((None, None, r, c), lambda i, j, k: (k, l, 0, 0))


def _rms_bwd_epilogue(x_idx, g_idx, dxo_idx):
    def ep(vals, ins, outs, i):
        dh = vals[0]
        x = ins[x_idx][...]
        g = ins[g_idx][...]
        rstd = lax.rsqrt(jnp.mean(x * x, axis=-1, keepdims=True) + RMS_EPS)
        xhat = x * rstd
        dxhat = dh * g
        dx = rstd * (dxhat - xhat * jnp.mean(dxhat * xhat, axis=-1, keepdims=True))
        outs[0][...] = ins[dxo_idx][...] + dx
        dg = jnp.broadcast_to(jnp.sum(dh * xhat, axis=0, keepdims=True), outs[1].shape)

        @pl.when(i == 0)
        def _():
            outs[1][...] = dg

        @pl.when(i > 0)
        def _():
            outs[1][...] += dg
    return ep


def _normed(x_idx, g_idx):
    seen = {}

    def f(ins):
        if id(ins) not in seen:
            xv = ins[x_idx][...]
            h = xv * lax.rsqrt(jnp.mean(xv * xv, axis=-1, keepdims=True) + RMS_EPS)
            seen[id(ins)] = (ins, (h * ins[g_idx][...]).astype(BF16))
        return seen[id(ins)][1]
    return f


def _rope_tables(T):
    pos = jnp.arange(T, dtype=F32)
    inv_freq = ROPE_THETA ** (-jnp.arange(0, ROPE_DIM, 2, dtype=F32) / ROPE_DIM)
    ang = pos[:, None] * inv_freq[None, :]
    cos, sin = jnp.cos(ang), jnp.sin(ang)
    half = ROPE_DIM // 2
    one = jnp.ones((T, HEAD_DIM - ROPE_DIM), F32)
    zero = jnp.zeros((T, HEAD_DIM - ROPE_DIM), F32)
    zh = jnp.zeros((T, half), F32)
    c = jnp.concatenate([cos, cos, one], axis=1)
    s1 = jnp.concatenate([-sin, zh, zero], axis=1)
    s2 = jnp.concatenate([zh, sin, zero], axis=1)
    return tuple(jnp.concatenate([t, t], axis=1) for t in (c, s1, s2))


def _rope_fwd(xv, c, s1, s2):
    w = xv.shape[1]
    half = ROPE_DIM // 2
    return xv * c + pltpu.roll(xv, w - half, 1) * s1 + pltpu.roll(xv, half, 1) * s2


def _rope_bwd(dy, c, s1, s2):
    w = dy.shape[1]
    half = ROPE_DIM // 2
    return dy * c + pltpu.roll(dy * s1, half, 1) + pltpu.roll(dy * s2, w - half, 1)


def _assemble_dproj(dqk, rest, gates, tabs):
    T = gates[0].shape[0]
    n_qk, n_rest = len(dqk), len(rest)
    width = (n_qk + n_rest) * D_ATT + 2 * D_MODEL

    def body(*refs):
        ins, (c_ref, s1_ref, s2_ref), o_ref = refs[:n_qk + n_rest + 2], refs[-4:-1], refs[-1]
        c = jnp.concatenate([c_ref[...]] * 2, axis=1)
        s1 = jnp.concatenate([s1_ref[...]] * 2, axis=1)
        s2 = jnp.concatenate([s2_ref[...]] * 2, axis=1)
        for b in range(n_qk + n_rest):
            v = ins[b][...]
            if b < n_qk:
                v = _rope_bwd(v, c, s1, s2)
            o_ref[:, b * D_ATT:(b + 1) * D_ATT] = v.astype(BF16)
        off = (n_qk + n_rest) * D_ATT
        o_ref[:, off:off + D_MODEL] = ins[-2][...]
        o_ref[:, off + D_MODEL:] = ins[-1][...]

    att = pl.BlockSpec((TM, D_ATT), lambda i: (i, 0))
    wide = pl.BlockSpec((TM, D_MODEL), lambda i: (i, 0))
    tab = pl.BlockSpec((TM, 128), lambda i: (i, 0))
    return pl.pallas_call(
        body, name="assemble_dproj", grid=(T // TM,),
        in_specs=[att] * (n_qk + n_rest) + [wide, wide, tab, tab, tab],
        out_specs=pl.BlockSpec((TM, width), lambda i: (i, 0)),
        out_shape=jax.ShapeDtypeStruct((T, width), BF16), compiler_params=_params(),
    )(*dqk, *rest, *gates, *tabs)


def _dil_merge(os_, lses):
    T = os_[0].shape[0]

    def body(o0, o1, o2, l0, l1, l2, o_ref, lse_ref):
        a, b, c = l0[...], l1[...], l2[...]
        m = jnp.maximum(jnp.maximum(a, b), c)
        ea, eb, ec = jnp.exp(a - m), jnp.exp(b - m), jnp.exp(c - m)
        den = ea + eb + ec
        o_ref[...] = (ea * o0[...] + eb * o1[...] + ec * o2[...]) / den
        lse_ref[...] = m + jnp.log(den)

    blk = pl.BlockSpec((TM, D_ATT), lambda i: (i, 0))
    sh = jax.ShapeDtypeStruct((T, D_ATT), F32)
    return pl.pallas_call(
        body, name="dil_merge", grid=(T // TM,), in_specs=[blk] * 6, out_specs=[blk, blk],
        out_shape=[sh, sh], compiler_params=_params(),
    )(*os_, *lses)


def _final_loss(x, gain, target):
    T = x.shape[0]

    def body(x_ref, g_ref, t_ref, dx_ref, dg_ref, loss_ref):
        xv = x_ref[...]
        g = g_ref[...]
        rstd = lax.rsqrt(jnp.mean(xv * xv, axis=-1, keepdims=True) + RMS_EPS)
        xhat = xv * rstd
        err = xhat * g - t_ref[...]
        loss = 0.5 * jnp.sum(jnp.mean(err * err, axis=-1, keepdims=True), axis=0, keepdims=True)
        dy = err * (1.0 / D_MODEL)
        dxhat = dy * g
        dx_ref[...] = rstd * (dxhat - xhat * jnp.mean(dxhat * xhat, axis=-1, keepdims=True))
        dg = jnp.broadcast_to(jnp.sum(dy * xhat, axis=0, keepdims=True), dg_ref.shape)
        ls = jnp.broadcast_to(loss, loss_ref.shape)

        @pl.when(pl.program_id(0) == 0)
        def _():
            dg_ref[...] = dg
            loss_ref[...] = ls

        @pl.when(pl.program_id(0) > 0)
        def _():
            dg_ref[...] += dg
            loss_ref[...] += ls

    blk = pl.BlockSpec((TM, D_MODEL), lambda i: (i, 0))
    row = pl.BlockSpec((1, D_MODEL), lambda i: (0, 0))
    acc = pl.BlockSpec((8, D_MODEL), lambda i: (0, 0))
    return pl.pallas_call(
        body, name="final_loss", grid=(T // TM,), in_specs=[blk, row, blk], out_specs=[blk, acc, acc],
        out_shape=[jax.ShapeDtypeStruct((T, D_MODEL), F32), jax.ShapeDtypeStruct((8, D_MODEL), F32),
                   jax.ShapeDtypeStruct((8, D_MODEL), F32)],
        compiler_params=_params(dimension_semantics=("arbitrary",)),
    )(x, gain, target)


def _pair_masks():
    lane = lax.broadcasted_iota(jnp.int32, (SPAN, 128), 1)
    return [lane < HEAD_DIM, lane >= HEAD_DIM]


def _stack_heads(x, masks):
    return jnp.concatenate([jnp.where(m, x, 0.0) for m in masks], axis=0)


def _unstack_heads(y, masks):
    rows = y.shape[0] // len(masks)
    out = jnp.where(masks[0], y[:rows], 0.0)
    for h in range(1, len(masks)):
        out = out + jnp.where(masks[h], y[rows * h:rows * (h + 1)], 0.0)
    return out


DIL_PAIR = 2


def _dil_rows(idx, d):
    u = idx // d
    r = idx - u * d
    own = pl.ds(u * (SPAN * d) + r, SPAN, stride=d) if d > 1 else pl.ds(pl.multiple_of(u * SPAN, SPAN), SPAN)
    up = jnp.maximum(u - 1, 0)
    prev = pl.ds(up * (SPAN * d) + r, SPAN, stride=d) if d > 1 else pl.ds(pl.multiple_of(up * SPAN, SPAN), SPAN)
    return u, own, prev


def _dil_valid(u):
    qi = lax.broadcasted_iota(jnp.int32, (2 * SPAN, 2 * SPAN), 0) & (SPAN - 1)
    kj = lax.broadcasted_iota(jnp.int32, (2 * SPAN, 2 * SPAN), 1)
    in_prev = (kj < SPAN) & (kj >= qi + jnp.where(u > 0, 0, SPAN))
    return in_prev | ((kj >= SPAN) & (kj - SPAN <= qi))


def _dil_keys(ref, own, prev):
    return jnp.concatenate([ref[prev, :], ref[own, :]], axis=0).astype(BF16)


def _dil_fwd(proj, g, d):
    T = proj.shape[0]
    n_iter = T // SPAN

    def body(q_ref, k_ref, v_ref, o_ref, lse_ref):
        masks = _pair_masks()

        def step(pair, carry):
            its = [_dil_rows(DIL_PAIR * pair + e, d) for e in range(DIL_PAIR)]
            qs = [_stack_heads(q_ref[own, :] * (HEAD_DIM ** -0.5), masks).astype(BF16) for _, own, _ in its]
            kks = [_dil_keys(k_ref, own, prev) for _, own, prev in its]
            vvs = [_dil_keys(v_ref, own, prev) for _, own, prev in its]
            ss = [jnp.where(_dil_valid(u), lax.dot_general(q, kk, NT, preferred_element_type=F32), NEG)
                  for (u, _, _), q, kk in zip(its, qs, kks)]
            ms = [jnp.max(s, axis=1, keepdims=True) for s in ss]
            ps = [jnp.exp(s - m) for s, m in zip(ss, ms)]
            dens = [jnp.sum(p, axis=1, keepdims=True) for p in ps]
            pvs = [lax.dot_general(p.astype(BF16), vv, NN, preferred_element_type=F32) / den
                   for p, vv, den in zip(ps, vvs, dens)]
            for (_, own, _), pv, m, den in zip(its, pvs, ms, dens):
                o_ref[own, :] = _unstack_heads(pv, masks)
                lse_ref[own, :] = _unstack_heads(jnp.broadcast_to(m + jnp.log(den), pv.shape), masks)
            return carry

        lax.fori_loop(0, n_iter // DIL_PAIR, step, 0)

    def col(b):
        return pl.BlockSpec((T, 128), lambda p: (0, b + p))

    sh = jax.ShapeDtypeStruct((T, D_ATT), F32)
    out = pl.BlockSpec((T, 128), lambda p: (0, p))
    return pl.pallas_call(
        body, name=f"dil_fwd_d{d}", grid=(2,),
        in_specs=[col(2 * g), col(6 + 2 * g), col(12 + 2 * g)], out_specs=[out, out], out_shape=[sh, sh],
        compiler_params=_params(dimension_semantics=("arbitrary",)),
    )(proj, proj, proj)


def _dil_bwd(proj, do, o_dil, lse, g, d):
    T = proj.shape[0]
    n_iter = T // SPAN

    def body(q_ref, k_ref, v_ref, do_ref, o_ref, lse_ref, dq_ref, dk_ref, dv_ref):
        masks = _pair_masks()
        head_lanes = jnp.concatenate(masks, axis=0)

        def step(pair, carry):
            its = [_dil_rows(DIL_PAIR * pair + e, d) for e in range(DIL_PAIR)]
            qs = [_stack_heads(q_ref[own, :] * (HEAD_DIM ** -0.5), masks).astype(BF16) for _, own, _ in its]
            kks = [_dil_keys(k_ref, own, prev) for _, own, prev in its]
            vvs = [_dil_keys(v_ref, own, prev) for _, own, prev in its]
            doms = [_stack_heads(do_ref[own, :], masks) for _, own, _ in its]
            dos = [dom.astype(BF16) for dom in doms]
            deltas = [jnp.sum(dom * jnp.concatenate([o_ref[own, :]] * 2, axis=0), axis=1, keepdims=True)
                      for dom, (_, own, _) in zip(doms, its)]
            lrows = [jnp.max(jnp.where(head_lanes, jnp.concatenate([lse_ref[own, :]] * 2, axis=0), NEG),
                             axis=1, keepdims=True) for _, own, _ in its]
            ss = [lax.dot_general(q, kk, NT, preferred_element_type=F32) for q, kk in zip(qs, kks)]
            dps = [lax.dot_general(do_b, vv, NT, preferred_element_type=F32) for do_b, vv in zip(dos, vvs)]
            ps = [jnp.where(_dil_valid(u), jnp.exp(s - lrow), 0.0) for (u, _, _), s, lrow in zip(its, ss, lrows)]
            dss = [(p * (dp - delta)).astype(BF16) for p, dp, delta in zip(ps, dps, deltas)]
            dqs = [lax.dot_general(ds, kk, NN, preferred_element_type=F32) for ds, kk in zip(dss, kks)]
            dkks = [lax.dot_general(ds, q, TN, preferred_element_type=F32) for ds, q in zip(dss, qs)]
            dvvs = [lax.dot_general(p.astype(BF16), do_b, TN, preferred_element_type=F32) for p, do_b in zip(ps, dos)]
            for (_, own, prev), dq, dkk, dvv in zip(its, dqs, dkks, dvvs):
                dq_ref[own, :] = _unstack_heads(dq, masks) * (HEAD_DIM ** -0.5)
                dk_ref[own, :] = dkk[SPAN:]
                dv_ref[own, :] = dvv[SPAN:]
                dk_ref[prev, :] = dk_ref[prev, :] + dkk[:SPAN]
                dv_ref[prev, :] = dv_ref[prev, :] + dvv[:SPAN]
            return carry

        lax.fori_loop(0, n_iter // DIL_PAIR, step, 0)

    def col(b):
        return pl.BlockSpec((T, 128), lambda p: (0, b + p))

    sh = jax.ShapeDtypeStruct((T, D_ATT), F32)
    return pl.pallas_call(
        body, name=f"dil_bwd_d{d}", grid=(2,),
        in_specs=[col(2 * g), col(6 + 2 * g), col(12 + 2 * g), col(0), col(0), col(0)],
        out_specs=[col(0), col(0), col(0)], out_shape=[sh, sh, sh],
        compiler_params=_params(dimension_semantics=("arbitrary",)),
    )(proj, proj, proj, do, o_dil, lse)


SB_KT = 512


def _sb_tri(strict):
    a = lax.broadcasted_iota(jnp.int32, (Q_BLOCK, Q_BLOCK), 0)
    b = lax.broadcasted_iota(jnp.int32, (Q_BLOCK, Q_BLOCK), 1)
    return jnp.where((a > b) if strict else (a >= b), 1.0, 0.0).astype(BF16)


def _split_stack(x):
    nb = x.shape[1] // Q_BLOCK
    blocks = [x[:, Q_BLOCK * b:Q_BLOCK * (b + 1)] for b in range(nb)]
    hi = [b.astype(BF16) for b in blocks]
    lo = [(b - h.astype(F32)).astype(BF16) for b, h in zip(blocks, hi)]
    return blocks, jnp.concatenate(hi + lo, axis=0)


def _suffix_from(y, blocks, c):
    r = blocks[0].shape[0]
    nb = len(blocks)
    outs = [None] * nb
    run = c
    for b in reversed(range(nb)):
        outs[b] = run + y[r * b:r * (b + 1)] + y[r * (nb + b):r * (nb + b + 1)]
        run = run + jnp.sum(blocks[b], axis=1, keepdims=True)
    return jnp.concatenate(outs, axis=1), run


SB_HEADS = D_ATT // HEAD_DIM
SB_FWD_CHAINS = 2
SB_BWD_CHAINS = 1


def _sb_past(i, t, rows):
    row = lax.broadcasted_iota(jnp.int32, (rows, SB_KT), 0) & (Q_BLOCK - 1)
    col = lax.broadcasted_iota(jnp.int32, (rows, SB_KT), 1)
    return col + t * SB_KT < row + i * Q_BLOCK


def _sb_head_masks(chains):
    lane = lax.broadcasted_iota(jnp.int32, (Q_BLOCK, D_ATT), 1)
    masks = [(lane >= HEAD_DIM * h) & (lane < HEAD_DIM * (h + 1)) for h in range(SB_HEADS)]
    per = SB_HEADS // chains
    return [masks[per * g:per * (g + 1)] for g in range(chains)]


def _sb_rows(t):
    return pl.ds(pl.multiple_of(t * SB_KT, SB_KT), SB_KT)


def _sb_log_terms(z, past):
    lsz = jnp.minimum(z, 0.0) - jnp.log(1.0 + jnp.exp(-jnp.abs(z)))
    lk = lsz - z
    return lsz, (lk if past is None else jnp.where(past, lk, 0.0))


def _sb_weights(lsz, after, past):
    w = jnp.exp(lsz + after)
    return w if past is None else jnp.where(past, w, 0.0)


def _sb_fwd(proj):
    T = proj.shape[0]

    def body(q_ref, k_ref, v_ref, o_ref):
        i = pl.program_id(0)
        masks = _sb_head_masks(SB_FWD_CHAINS)
        rows = SB_HEADS // SB_FWD_CHAINS * Q_BLOCK
        tri = _sb_tri(True)
        q = q_ref[...] * (HEAD_DIM ** -0.5)
        qs = [_stack_heads(q, m).astype(BF16) for m in masks]
        n_tiles = (i * Q_BLOCK) // SB_KT + 1

        def tile(t, carry, masked):
            kb = k_ref[_sb_rows(t), :].astype(BF16)
            vb = v_ref[_sb_rows(t), :].astype(BF16)
            past = _sb_past(i, t, rows) if masked else None
            acc, cs = carry[0], carry[1:]
            zs = [lax.dot_general(g, kb, NT, preferred_element_type=F32) for g in qs]
            logs = [_sb_log_terms(z, past) for z in zs]
            splits = [_split_stack(lk) for _, lk in logs]
            ys = [lax.dot_general(x, tri, NN, preferred_element_type=F32) for _, x in splits]
            sums = [_suffix_from(y, blocks, c) for y, (blocks, _), c in zip(ys, splits, cs)]
            ws = [_sb_weights(lsz, after, past).astype(BF16) for (lsz, _), (after, _) in zip(logs, sums)]
            for m, w in zip(masks, ws):
                acc = acc + _unstack_heads(lax.dot_general(w, vb, NN, preferred_element_type=F32), m)
            return (acc, *[c for _, c in sums])

        zcol = jnp.zeros((rows, 1), F32)
        carry = tile(n_tiles - 1, (jnp.zeros((Q_BLOCK, D_ATT), F32),) + (zcol,) * SB_FWD_CHAINS, True)
        carry = lax.fori_loop(0, n_tiles - 1, lambda tt, cr: tile(n_tiles - 2 - tt, cr, False), carry)
        o_ref[...] = carry[0]

    cb = COL_QS // D_ATT
    return pl.pallas_call(
        body, name="sb_fwd", grid=(T // Q_BLOCK,),
        in_specs=[pl.BlockSpec((Q_BLOCK, D_ATT), lambda i: (i, cb)),
                  pl.BlockSpec((T, D_ATT), lambda i: (0, cb + 1)),
                  pl.BlockSpec((T, D_ATT), lambda i: (0, cb + 2))],
        out_specs=pl.BlockSpec((Q_BLOCK, D_ATT), lambda i: (i, 0)),
        out_shape=jax.ShapeDtypeStruct((T, D_ATT), F32),
        compiler_params=_params(dimension_semantics=("arbitrary",)),
    )(proj, proj, proj)


def _sb_bwd(proj, do, o):
    T = proj.shape[0]

    def body(q_ref, k_ref, v_ref, do_ref, o_ref, dq_ref, dk_ref, dv_ref):
        i = pl.program_id(0)
        masks = _sb_head_masks(SB_BWD_CHAINS)
        n_rows = SB_HEADS // SB_BWD_CHAINS * Q_BLOCK
        tri = _sb_tri(True)
        tri_incl = _sb_tri(False)

        @pl.when(i == 0)
        def _():
            dk_ref[...] = jnp.zeros_like(dk_ref)
            dv_ref[...] = jnp.zeros_like(dv_ref)

        q = q_ref[...] * (HEAD_DIM ** -0.5)
        qs = [_stack_heads(q, m).astype(BF16) for m in masks]
        dos = [_stack_heads(do_ref[...], m).astype(BF16) for m in masks]
        o_rep = jnp.concatenate([o_ref[...]] * (SB_HEADS // SB_BWD_CHAINS), axis=0)
        deltas = [jnp.sum(d.astype(F32) * o_rep, axis=1, keepdims=True) for d in dos]
        n_tiles = (i * Q_BLOCK) // SB_KT + 1

        def tile(t, carry, masked):
            rows = _sb_rows(t)
            kb = k_ref[rows, :].astype(BF16)
            vb = v_ref[rows, :].astype(BF16)
            past = _sb_past(i, t, n_rows) if masked else None
            dq, cs, ces = carry[0], carry[1:1 + SB_BWD_CHAINS], carry[1 + SB_BWD_CHAINS:]
            zs = [lax.dot_general(g, kb, NT, preferred_element_type=F32) for g in qs]
            gvs = [lax.dot_general(d, vb, NT, preferred_element_type=F32) for d in dos]
            logs = [_sb_log_terms(z, past) for z in zs]
            splits = [_split_stack(lk) for _, lk in logs]
            ys = [lax.dot_general(x, tri, NN, preferred_element_type=F32) for _, x in splits]
            sums = [_suffix_from(y, blocks, c) for y, (blocks, _), c in zip(ys, splits, cs)]
            wbs = [_sb_weights(lsz, after, past).astype(BF16) for (lsz, _), (after, _) in zip(logs, sums)]
            es = [wb.astype(F32) * gv for wb, gv in zip(wbs, gvs)]
            esplits = [_split_stack(e) for e in es]
            eys = [lax.dot_general(x, tri_incl, NN, preferred_element_type=F32) for _, x in esplits]
            esums = [_suffix_from(y, blocks, ce) for y, (blocks, _), ce in zip(eys, esplits, ces)]
            dzbs = []
            for e, (lsz, lk), (suf, _), delta in zip(es, logs, esums, deltas):
                dz = e * jnp.exp(lk) - (delta - suf) * jnp.exp(lsz)
                dzbs.append((dz if past is None else jnp.where(past, dz, 0.0)).astype(BF16))
            dk_t = dv_t = None
            for m, dzb, wb, g, d in zip(masks, dzbs, wbs, qs, dos):
                dq = dq + _unstack_heads(lax.dot_general(dzb, kb, NN, preferred_element_type=F32), m)
                a = lax.dot_general(dzb, g, TN, preferred_element_type=F32)
                b = lax.dot_general(wb, d, TN, preferred_element_type=F32)
                dk_t = a if dk_t is None else dk_t + a
                dv_t = b if dv_t is None else dv_t + b
            dk_ref[rows, :] = dk_ref[rows, :] + dk_t
            dv_ref[rows, :] = dv_ref[rows, :] + dv_t
            return (dq, *[c for _, c in sums], *[c for _, c in esums])

        zcol = jnp.zeros((n_rows, 1), F32)
        carry = tile(n_tiles - 1, (jnp.zeros((Q_BLOCK, D_ATT), F32),) + (zcol,) * (2 * SB_BWD_CHAINS), True)
        carry = lax.fori_loop(0, n_tiles - 1, lambda tt, cr: tile(n_tiles - 2 - tt, cr, False), carry)
        dq_ref[...] = carry[0] * (HEAD_DIM ** -0.5)

    cb = COL_QS // D_ATT
    blk = pl.BlockSpec((Q_BLOCK, D_ATT), lambda i: (i, 0))
    full = pl.BlockSpec((T, D_ATT), lambda i: (0, 0))
    sh = jax.ShapeDtypeStruct((T, D_ATT), F32)
    return pl.pallas_call(
        body, name="sb_bwd", grid=(T // Q_BLOCK,),
        in_specs=[pl.BlockSpec((Q_BLOCK, D_ATT), lambda i: (i, cb)),
                  pl.BlockSpec((T, D_ATT), lambda i: (0, cb + 1)),
                  pl.BlockSpec((T, D_ATT), lambda i: (0, cb + 2)), blk, blk],
        out_specs=[blk, full, full], out_shape=[sh, sh, sh],
        compiler_params=_params(dimension_semantics=("arbitrary",)),
    )(proj, proj, proj, do, o)


def _tok(c, by=None):
    if by is None:
        return pl.BlockSpec((TM, c), lambda i, j, k: (i, 0))
    if by == 1:
        return pl.BlockSpec((TM, c), lambda i, j, k: (i, j))
    return pl.BlockSpec((TM, c), lambda i, j, k: (i, k))


def _chunked(c, by):
    if by == 1:
        return pl.BlockSpec((None, TM, c), lambda i, j, k: (j, i, 0))
    return pl.BlockSpec((None, TM, c), lambda i, j, k: (k, i, 0))


def _gain_spec():
    return pl.BlockSpec((1, D_MODEL), lambda i, j, k: (0, 0))


def _all_chunks(rows, c):
    return pl.BlockSpec((N_CHIPS, rows, c), lambda i, j, k: (0, i, 0))


def _wfull(r, c, l):
    return pl.BlockSpec((N_CHIPS, None, r, c), lambda i, j, k: (0, l, 0, 0))


def _pick(idx, c):
    return lambda ins: ins[idx][c]


def _cols(idx, c, w):
    return lambda ins: ins[idx][:, c * w:(c + 1) * w]


def _rows(rows, width):
    return pl.BlockSpec((rows, width), lambda i, j, k: (i, 0))


def _whole(shape):
    return pl.BlockSpec(shape, lambda i, j, k: (0, 0))


def _ffn_fwd(x, gain, wg, wu, wd):
    T = x.shape[0]
    wg, wu, wd = (w.reshape(-1, D_MODEL) for w in (wg, wu, wd))
    ff = wd.shape[0]
    tm = TM // 2
    normed = _normed(0, 3)

    def swiglu(vals, ins, outs, i):
        gt, up = vals
        s = _sigmoid(gt)
        sil = gt * s
        outs[0][...] = sil.astype(BF16)
        outs[1][...] = (up * (s * (1.0 + gt * (1.0 - s)))).astype(BF16)
        outs[2][...] = (sil * up).astype(BF16)
        outs[3][...] = normed(ins)

    ash = jax.ShapeDtypeStruct((T, ff), BF16)
    sil, up_dsil, act, h = _mm(
        "ffn_up", [x, wg, wu, gain], [_rows(tm, D_MODEL), _whole(wg.shape), _whole(wu.shape), _gain_spec()],
        [(normed, 1, 0), (normed, 2, 1)], 2, None, NT, (T // tm, 1, 1), swiglu,
        [ash] * 3 + [jax.ShapeDtypeStruct((T, D_MODEL), BF16)], [_rows(tm, ff)] * 3 + [_rows(tm, D_MODEL)])

    def resid(vals, ins, outs, i):
        outs[0][...] = ins[2][...] + 0.5 * vals[0]

    (y,) = _mm(
        "ffn_down", [act, wd, x], [_rows(TM, ff), _whole(wd.shape), _tok(D_MODEL)], [(0, 1, 0)], 1, None, NN,
        (T // TM, 1, 1), resid, [jax.ShapeDtypeStruct((T, D_MODEL), F32)], [_tok(D_MODEL)])
    return y, (x, h, sil, up_dsil, act)


def _ffn_bwd(dxo, gain, wg, wu, wd, saved):
    x, h, sil, up_dsil, act = saved
    T = x.shape[0]
    n_chips, _, ffs, _ = wd.shape
    wg, wu, wd = (w.reshape(-1, D_MODEL) for w in (wg, wu, wd))
    ff = wd.shape[0]
    tk = TM
    tm = TM // 2

    def dswiglu(vals, ins, outs, i):
        da = 0.5 * vals[0]
        outs[0][...] = (da * ins[3][...].astype(F32)).astype(BF16)
        outs[1][...] = (da * ins[2][...].astype(F32)).astype(BF16)

    ash = jax.ShapeDtypeStruct((T, ff), BF16)
    dgate, dup = _mm(
        "ffn_dact", [dxo, wd, sil, up_dsil], [_rows(tm, D_MODEL), _whole(wd.shape), _rows(tm, ff), _rows(tm, ff)],
        [(0, 1, 0)], 1, None, NT, (T // tm, 1, 1), dswiglu, [ash, ash], [_rows(tm, ff)] * 2)

    def half(vals, ins, outs, i):
        outs[0][...] = (0.5 * vals[0]).astype(BF16)

    def cast(vals, ins, outs, i):
        outs[0][...] = vals[0].astype(BF16)

    tok_k = pl.BlockSpec((tk, D_MODEL), lambda i, j, k: (k, 0))
    hid_k = pl.BlockSpec((tk, ff), lambda i, j, k: (k, 0))
    wsh = jax.ShapeDtypeStruct((ff, D_MODEL), BF16)
    (dwd,) = _mm("ffn_dwd", [act, dxo], [hid_k, tok_k], [(0, 1, 0)], 1, (ff, D_MODEL), TN, (1, 1, T // tk), half,
                 [wsh], [_whole((ff, D_MODEL))])

    dx, dgain = _mm(
        "ffn_dx", [dgate, dup, wg, wu, x, gain, dxo],
        [_rows(tm, ff), _rows(tm, ff), _whole(wg.shape), _whole(wu.shape), _rows(tm, D_MODEL), _gain_spec(),
         _rows(tm, D_MODEL)],
        [(0, 2, 0), (1, 3, 0)], 1, None, NN, (T // tm, 1, 1), _rms_bwd_epilogue(4, 5, 6),
        [jax.ShapeDtypeStruct((T, D_MODEL), F32), jax.ShapeDtypeStruct((8, D_MODEL), F32)],
        [_rows(tm, D_MODEL), pl.BlockSpec((8, D_MODEL), lambda i, j, k: (0, 0))])

    dws = []
    for dact in (dgate, dup):
        dws += _mm("ffn_dwgu", [dact, h], [hid_k, tok_k], [(0, 1, 0)], 1, (ff, D_MODEL), TN, (1, 1, T // tk), cast,
                   [wsh], [_whole((ff, D_MODEL))])
    dwg, dwu, dwd = (w.reshape(n_chips, ffs, D_MODEL) for w in (dws[0], dws[1], dwd))
    return dx, dgain, dwg, dwu, dwd


def _joined_mixer_weights(wpd, wps, wo):
    n, _, r, c = wpd.shape
    wpd_n, wps_n = (w[:, 0].transpose(1, 0, 2).reshape(r, n * c) for w in (wpd, wps))
    return wpd_n, wps_n, wo.reshape(-1, wo.shape[3])


def _mixer_fwd(x, gain, W, l, tabs):
    T = x.shape[0]
    win, wpd, wps, wo = W["w_in"], W["w_proj_dil"], W["w_proj_sb"], W["w_out"]
    cin = win.shape[3]
    cp = wpd.shape[3]
    normed = _normed(0, 5)
    n_rope = 6 * D_ATT

    tm = TM // 2

    def roped(vals, ins, outs, i):
        for j, v in enumerate(vals):
            lo = j * cin
            k = min(max(n_rope - lo, 0), cin)
            if k:
                tab = [jnp.concatenate([ins[t][...]] * (k // 128), axis=1) for t in (2, 3, 4)]
                outs[0][:, lo:lo + k] = _rope_fwd(v[:, :k], *tab)
            if k < cin:
                outs[0][:, lo + k:lo + cin] = v[:, k:]
        outs[1][...] = normed(ins)

    proj, h = _mm(
        "mix_in", [x, win, *tabs, gain],
        [_rows(tm, D_MODEL), _wfull(D_MODEL, cin, l)] + [_rows(tm, 128)] * 3 + [_gain_spec()],
        [(normed, _pick(1, c), c) for c in range(N_CHIPS)], N_CHIPS, None, NN, (T // tm, 1, 1), roped,
        [jax.ShapeDtypeStruct((T, N_CHIPS * cin), F32), jax.ShapeDtypeStruct((T, D_MODEL), BF16)],
        [_rows(tm, N_CHIPS * cin), _rows(tm, D_MODEL)])

    os_, lses = [], []
    for g, (window, dil) in enumerate(DIL_GROUPS):
        o_g, lse_g = _dil_fwd(proj, g, dil)
        os_.append(o_g)
        lses.append(lse_g)
    o_dil, lse = _dil_merge(os_, lses)
    o_sb = _sb_fwd(proj)

    def gated(vals, ins, outs, i):
        pd, ps = vals
        outs[0][...] = (_sigmoid(ins[4][...]) * pd + _sigmoid(ins[5][...]) * ps).astype(BF16)
        outs[1][...] = pd.astype(BF16)
        outs[2][...] = ps.astype(BF16)

    wpd_n, wps_n, wo_n = _joined_mixer_weights(wpd, wps, wo)
    gd_spec = pl.BlockSpec((TM, D_MODEL), lambda i, j, k: (i, COL_GD // D_MODEL))
    gs_spec = pl.BlockSpec((TM, D_MODEL), lambda i, j, k: (i, COL_GS // D_MODEL))
    ush = jax.ShapeDtypeStruct((T, D_MODEL), BF16)
    u, pd, ps = _mm(
        "mix_gate", [o_dil, o_sb, wpd_n, wps_n, proj, proj],
        [_tok(D_ATT), _tok(D_ATT), _whole(wpd_n.shape), _whole(wps_n.shape), gd_spec, gs_spec],
        [(0, 2, 0), (1, 3, 1)], 2, None, NN, (T // TM, 1, 1), gated, [ush] * 3, [_tok(D_MODEL)] * 3)

    def resid(vals, ins, outs, i):
        outs[0][...] = ins[2][...] + vals[0]

    (y,) = _mm(
        "mix_out", [u, wo_n, x], [_tok(D_MODEL), _whole(wo_n.shape), _tok(D_MODEL)], [(0, 1, 0)], 1, None, NN,
        (T // TM, 1, 1), resid, [jax.ShapeDtypeStruct((T, D_MODEL), F32)], [_tok(D_MODEL)])
    return y, (x, h, proj, o_dil, lse, o_sb, u, pd, ps)


def _mixer_bwd(dxo, gain, W, l, tabs, saved):
    x, h, proj, o_dil, lse, o_sb, u, pd, ps = saved
    T = x.shape[0]
    win, wpd, wps, wo = W["w_in"], W["w_proj_dil"], W["w_proj_sb"], W["w_out"]
    cin = win.shape[3]
    cp = wpd.shape[3]
    tk = TM
    tm = TM // 2
    row = pl.BlockSpec((tm, D_MODEL), lambda i, j, k: (i, 0))

    def dgated(vals, ins, outs, i):
        du = vals[0]
        sd = _sigmoid(ins[4][...])
        ss = _sigmoid(ins[5][...])
        outs[0][...] = (du * sd).astype(BF16)
        outs[1][...] = (du * ss).astype(BF16)
        outs[2][...] = (du * ins[2][...].astype(F32) * sd * (1.0 - sd)).astype(BF16)
        outs[3][...] = (du * ins[3][...].astype(F32) * ss * (1.0 - ss)).astype(BF16)

    wpd_n, wps_n, wo_n = _joined_mixer_weights(wpd, wps, wo)
    gd_spec = pl.BlockSpec((TM, D_MODEL), lambda i, j, k: (i, COL_GD // D_MODEL))
    gs_spec = pl.BlockSpec((TM, D_MODEL), lambda i, j, k: (i, COL_GS // D_MODEL))
    ush = jax.ShapeDtypeStruct((T, D_MODEL), BF16)
    dpd, dps, dgd, dgs = _mm(
        "mix_du", [dxo, wo_n, pd, ps, proj, proj],
        [_tok(D_MODEL), _whole(wo_n.shape), _tok(D_MODEL), _tok(D_MODEL), gd_spec, gs_spec],
        [(0, 1, 0)], 1, None, NT, (T // TM, 1, 1), dgated, [ush] * 4, [_tok(D_MODEL)] * 4)

    def one(vals, ins, outs, i):
        outs[0][...] = vals[0].astype(BF16)

    def two(vals, ins, outs, i):
        outs[0][...] = vals[0].astype(BF16)
        outs[1][...] = vals[1].astype(BF16)

    tok_k = pl.BlockSpec((tk, D_MODEL), lambda i, j, k: (k, 0))
    att_k = pl.BlockSpec((tk, D_ATT), lambda i, j, k: (k, 0))
    (dwo_n,) = _mm("mix_dwo", [u, dxo], [tok_k, tok_k], [(0, 1, 0)], 1, (D_MODEL, D_MODEL), TN, (1, 1, T // tk), one,
                   [jax.ShapeDtypeStruct((D_MODEL, D_MODEL), BF16)], [_whole((D_MODEL, D_MODEL))])

    def plain2(vals, ins, outs, i):
        outs[0][...] = vals[0]
        outs[1][...] = vals[1]

    ash = jax.ShapeDtypeStruct((T, D_ATT), F32)
    do_dil, do_sb = _mm(
        "mix_do", [dpd, dps, wpd_n, wps_n], [_tok(D_MODEL), _tok(D_MODEL), _whole(wpd_n.shape), _whole(wps_n.shape)],
        [(0, 2, 0), (1, 3, 1)], 2, None, NT, (T // TM, 1, 1), plain2, [ash, ash], [_tok(D_ATT)] * 2)

    psh = jax.ShapeDtypeStruct((D_ATT, D_MODEL), BF16)
    dwpd_n, dwps_n = _mm(
        "mix_dwp", [o_dil, o_sb, dpd, dps], [att_k, att_k, tok_k, tok_k], [(0, 2, 0), (1, 3, 1)], 2,
        (D_ATT, D_MODEL), TN, (1, 1, T // tk), two, [psh, psh], [_whole((D_ATT, D_MODEL))] * 2)
    dwpd, dwps = (w.reshape(D_ATT, N_CHIPS, cp).transpose(1, 0, 2) for w in (dwpd_n, dwps_n))
    dwo = dwo_n.reshape(N_CHIPS, cp, D_MODEL)

    dqs, dks, dvs = [], [], []
    for g, (window, dil) in enumerate(DIL_GROUPS):
        dq, dk, dv = _dil_bwd(proj, do_dil, o_dil, lse, g, dil)
        dqs.append(dq)
        dks.append(dk)
        dvs.append(dv)
    dq_s, dk_s, dv_s = _sb_bwd(proj, do_sb, o_sb)
    dproj = _assemble_dproj(dqs + dks, dvs + [dq_s, dk_s, dv_s], [dgd, dgs], tabs)

    dx, dgain = _mm(
        "mix_dx", [dproj, win, x, gain, dxo],
        [pl.BlockSpec((tm, N_CHIPS * cin), lambda i, j, k: (i, 0)), _wfull(D_MODEL, cin, l), row, _gain_spec(), row],
        [(_cols(0, c, cin), _pick(1, c), 0) for c in range(N_CHIPS)], 1, None, NT, (T // tm, 1, 1),
        _rms_bwd_epilogue(2, 3, 4),
        [jax.ShapeDtypeStruct((T, D_MODEL), F32), jax.ShapeDtypeStruct((8, D_MODEL), F32)],
        [row, pl.BlockSpec((8, D_MODEL), lambda i, j, k: (0, 0))])

    (dwin,) = _mm(
        "mix_dwin", [h, dproj],
        [pl.BlockSpec((tk, D_MODEL), lambda i, j, k: (k, 0)), pl.BlockSpec((tk, cin), lambda i, j, k: (k, j))],
        [(0, 1, 0)], 1, (D_MODEL, cin), TN, (1, N_CHIPS, T // tk), one,
        [jax.ShapeDtypeStruct((N_CHIPS, D_MODEL, cin), BF16)],
        [pl.BlockSpec((None, D_MODEL, cin), lambda i, j, k: (j, 0, 0))])
    return dx, dgain, dwin, dwpd, dwps, dwo


def _local_step(x, target, norms, norm_final, weights_of, on_grads):
    T = x.shape[0]
    tabs = _rope_tables(T)
    saved, held = [], []
    for l in range(DEPTH):
        w1 = weights_of(l, 0, x)
        x, s1 = _ffn_fwd(x, norms["norm_ffn1"][l:l + 1], w1["ffn1_w_gate"], w1["ffn1_w_up"], w1["ffn1_w_down"])
        w2 = weights_of(l, 1, x)
        x, s2 = _mixer_fwd(x, norms["norm_mix"][l:l + 1], w2, 0, tabs)
        w3 = weights_of(l, 2, x)
        x, s3 = _ffn_fwd(x, norms["norm_ffn2"][l:l + 1], w3["ffn2_w_gate"], w3["ffn2_w_up"], w3["ffn2_w_down"])
        saved.append((s1, s2, s3))
        held.append((w1, w2, w3))
    dx, dg_final, loss = _final_loss(x, norm_final.reshape(1, D_MODEL), target)
    gains = [None] * DEPTH
    for l in reversed(range(DEPTH)):
        s1, s2, s3 = saved[l]
        w1, w2, w3 = held[l]
        dx, dg2, dwg2, dwu2, dwd2 = _ffn_bwd(dx, norms["norm_ffn2"][l:l + 1], w3["ffn2_w_gate"], w3["ffn2_w_up"],
                                             w3["ffn2_w_down"], s3)
        dx = on_grads(l, 2, dict(ffn2_w_gate=dwg2, ffn2_w_up=dwu2, ffn2_w_down=dwd2), dx)
        dx, dgm, dwin, dwpd, dwps, dwo = _mixer_bwd(dx, norms["norm_mix"][l:l + 1], w2, 0, tabs, s2)
        dx = on_grads(l, 1, dict(w_in=dwin, w_proj_dil=dwpd, w_proj_sb=dwps, w_out=dwo), dx)
        dx, dg1, dwg1, dwu1, dwd1 = _ffn_bwd(dx, norms["norm_ffn1"][l:l + 1], w1["ffn1_w_gate"], w1["ffn1_w_up"],
                                             w1["ffn1_w_down"], s1)
        dx = on_grads(l, 0, dict(ffn1_w_gate=dwg1, ffn1_w_up=dwu1, ffn1_w_down=dwd1), dx)
        gains[l] = dict(norm_ffn1=dg1, norm_mix=dgm, norm_ffn2=dg2)
    return loss, dx, gains, dg_final


def _place():
    x, y, c = lax.axis_index("x"), lax.axis_index("y"), lax.axis_index("c")
    chips = [(1 - x, y), (x, 1 - y), (1 - x, 1 - y)]
    return x, y, c, chips


def _half(c, r):
    return pl.ds(pl.multiple_of(c * (r // 2), 8), r // 2)


def _cast_into_slot(ws, ls, me_arr, after):
    n = len(ws)
    late = [] if after is None else [after]

    def body(me_ref, *refs):
        for a in range(n):
            refs[len(refs) - n + a][...] = refs[a][...].astype(BF16)

    def src(w, l):
        return pl.BlockSpec((None, w.shape[1] // 4, w.shape[2]), lambda i, me: (l, i, 0))

    def dst(w):
        return pl.BlockSpec((None, None, w.shape[1] // 4, w.shape[2]), lambda i, me: (me[0], 0, i, 0))

    return pl.pallas_call(
        body, name="cast_weights",
        grid_spec=pltpu.PrefetchScalarGridSpec(
            num_scalar_prefetch=1, grid=(4,),
            in_specs=[src(w, l) for w, l in zip(ws, ls)] + [pl.BlockSpec(memory_space=pl.ANY)] * len(late),
            out_specs=[dst(w) for w in ws]),
        out_shape=[jax.ShapeDtypeStruct((N_CHIPS, 1) + w.shape[1:], BF16) for w in ws], compiler_params=_params(),
    )(me_arr, *ws, *late)


HBM_SPEC = pl.BlockSpec(memory_space=pltpu.HBM)
SEM_SPEC = pl.BlockSpec(memory_space=pltpu.SEMAPHORE)
SPLIT_COPY = pltpu.CompilerParams(has_side_effects=pltpu.SideEffectType.DATAFLOW_SIDE_EFFECTING)


def _gather_piece(ref, chip_id, c):
    return ref.at[chip_id, 0, _half(c, ref.shape[2]), :]


def _gather_start(tag, bufs):
    n = len(bufs)

    def body(*refs):
        out_refs = refs[n:2 * n]
        send_sems, recv_sems, token = refs[2 * n:]
        x, y, c, chips = _place()
        me = 2 * x + y
        for a in range(n):
            piece = _gather_piece(out_refs[a], me, c)
            for j, chip in enumerate(chips):
                pltpu.make_async_remote_copy(
                    src_ref=piece, dst_ref=piece, send_sem=send_sems.at[3 * a + j], recv_sem=recv_sems.at[3 * a + j],
                    device_id=(*chip, c), device_id_type=MESH).start()
        token[...] = jnp.zeros_like(token)

    outs = pl.pallas_call(
        body, name=f"gather_start_{tag}", in_specs=[HBM_SPEC] * n,
        out_specs=[HBM_SPEC] * n + [SEM_SPEC, SEM_SPEC, pl.BlockSpec(memory_space=pltpu.VMEM)],
        out_shape=[pltpu.HBM(b.shape, b.dtype) for b in bufs] + [pltpu.SemaphoreType.DMA((3 * n,))] * 2
        + [jax.ShapeDtypeStruct((8, 128), F32)],
        input_output_aliases={a: a for a in range(n)}, compiler_params=SPLIT_COPY,
    )(*[pltpu.with_memory_space_constraint(b, pltpu.HBM) for b in bufs])
    return outs[:n], outs[n], outs[n + 1], outs[n + 2]


def _gather_wait(k, bufs, places, send_sems, recv_sems, after):
    m = len(bufs)

    def body(*refs):
        in_refs = refs[:m]
        ssem, rsem = refs[m], refs[m + 1]
        x, y, c, chips = _place()
        me = 2 * x + y
        for t, a in enumerate(places):
            for j, chip in enumerate(chips):
                cp = pltpu.make_async_remote_copy(
                    src_ref=_gather_piece(in_refs[t], me, c),
                    dst_ref=_gather_piece(in_refs[t], 2 * chip[0] + chip[1], c),
                    send_sem=ssem.at[3 * a + j], recv_sem=rsem.at[3 * a + j], device_id=(*chip, c),
                    device_id_type=MESH)
                cp.wait_send()
                cp.wait_recv()

    return pl.pallas_call(
        body, name=f"gather_wait_{k}",
        in_specs=[HBM_SPEC] * m + [SEM_SPEC, SEM_SPEC, pl.BlockSpec(memory_space=pl.ANY)], out_specs=[HBM_SPEC] * m,
        out_shape=[pltpu.HBM(b.shape, b.dtype) for b in bufs], input_output_aliases={t: t for t in range(m)},
        compiler_params=SPLIT_COPY,
    )(*bufs, send_sems, recv_sems, after)


def _gather_relay(bufs):
    n = len(bufs)

    def body(*refs):
        out_refs = refs[n:2 * n]
        send_sems, recv_sems = refs[2 * n:]
        x, y, c, chips = _place()
        cps = []
        for a in range(n):
            for j, chip in enumerate(chips):
                piece = _gather_piece(out_refs[a], 2 * chip[0] + chip[1], c)
                cps.append(pltpu.make_async_remote_copy(
                    src_ref=piece, dst_ref=piece, send_sem=send_sems.at[a, j], recv_sem=recv_sems.at[a, j],
                    device_id=(x, y, 1 - c), device_id_type=MESH))
        for cp in cps:
            cp.start()
        for a in range(n):
            for j, chip in enumerate(chips):
                theirs = _gather_piece(out_refs[a], 2 * chip[0] + chip[1], 1 - c)
                pltpu.make_async_remote_copy(
                    src_ref=theirs, dst_ref=theirs, send_sem=send_sems.at[a, j], recv_sem=recv_sems.at[a, j],
                    device_id=(x, y, 1 - c), device_id_type=MESH).wait_recv()
        for cp in cps:
            cp.wait_send()

    any_spec = pl.BlockSpec(memory_space=pl.ANY)
    return pl.pallas_call(
        body, name="gather_relay", in_specs=[any_spec] * n, out_specs=[any_spec] * n,
        out_shape=[jax.ShapeDtypeStruct(b.shape, b.dtype) for b in bufs],
        input_output_aliases={a: a for a in range(n)},
        scratch_shapes=[pltpu.SemaphoreType.DMA((n, 3))] * 2,
    )(*bufs)


def _other_half(ref, c):
    return ref.at[:, _half(1 - c, ref.shape[1]), :]


def _all_of(ref, c):
    return ref


def _sibling_start(name, srcs, pick, land_shapes, thru):
    n = len(srcs)
    n_thru = 0 if thru is None else 1
    lands = [lax.empty(sh, s.dtype) for sh, s in zip(land_shapes, srcs)]

    def body(*refs):
        k = 2 * n + n_thru
        s_refs, land_refs = refs[k:k + n], refs[k + n:k + 2 * n]
        send_sems, recv_sems, token = refs[2 * k:]
        x, y, c, _ = _place()
        for a in range(n):
            pltpu.make_async_remote_copy(
                src_ref=pick(s_refs[a], c), dst_ref=land_refs[a], send_sem=send_sems.at[a],
                recv_sem=recv_sems.at[a], device_id=(x, y, 1 - c), device_id_type=MESH).start()
        token[...] = jnp.zeros_like(token)

    vals = list(srcs) + lands + ([] if thru is None else [thru])
    outs = pl.pallas_call(
        body, name=name, in_specs=[HBM_SPEC] * len(vals),
        out_specs=[HBM_SPEC] * len(vals) + [SEM_SPEC, SEM_SPEC, pl.BlockSpec(memory_space=pltpu.VMEM)],
        out_shape=[pltpu.HBM(v.shape, v.dtype) for v in vals] + [pltpu.SemaphoreType.DMA((n,))] * 2
        + [jax.ShapeDtypeStruct((8, 128), F32)],
        input_output_aliases={a: a for a in range(len(vals))}, compiler_params=SPLIT_COPY,
    )(*[pltpu.with_memory_space_constraint(v, pltpu.HBM) for v in vals])
    k = len(vals)
    return (outs[:n], outs[n:2 * n], outs[k], outs[k + 1]), (outs[2 * n] if n_thru else None), outs[k + 2]


def _sibling_wait(name, srcs, pick, lands, send_sems, recv_sems, after):
    n = len(srcs)

    def body(*refs):
        s_refs, land_refs = refs[:n], refs[n:2 * n]
        ssem, rsem = refs[2 * n], refs[2 * n + 1]
        x, y, c, _ = _place()
        for a in range(n):
            cp = pltpu.make_async_remote_copy(
                src_ref=pick(s_refs[a], c), dst_ref=land_refs[a], send_sem=ssem.at[a], recv_sem=rsem.at[a],
                device_id=(x, y, 1 - c), device_id_type=MESH)
            cp.wait_send()
            cp.wait_recv()

    vals = list(srcs) + list(lands)
    outs = pl.pallas_call(
        body, name=name, in_specs=[HBM_SPEC] * (2 * n) + [SEM_SPEC, SEM_SPEC, pl.BlockSpec(memory_space=pl.ANY)],
        out_specs=[HBM_SPEC] * (2 * n), out_shape=[pltpu.HBM(v.shape, v.dtype) for v in vals],
        input_output_aliases={a: a for a in range(2 * n)}, compiler_params=SPLIT_COPY,
    )(*vals, send_sems, recv_sems, after)
    return outs[n:]


def _add_half(gs, gots, c_arr):
    n = len(gs)

    def body(c_ref, *refs):
        for a in range(n):
            refs[2 * n + a][...] = (refs[a][...].astype(F32) + refs[n + a][...].astype(F32)).astype(BF16)

    def own(g):
        return pl.BlockSpec((None, g.shape[1] // 2, g.shape[2]), lambda k, cr: (k, cr[0], 0))

    def half(g):
        return pl.BlockSpec((None, g.shape[1] // 2, g.shape[2]), lambda k, cr: (k, 0, 0))

    return pl.pallas_call(
        body, name="grad_add_half",
        grid_spec=pltpu.PrefetchScalarGridSpec(
            num_scalar_prefetch=1, grid=(N_CHIPS,),
            in_specs=[own(g) for g in gs] + [half(g) for g in gs], out_specs=[half(g) for g in gs]),
        out_shape=[jax.ShapeDtypeStruct(got.shape, BF16) for got in gots], compiler_params=_params(),
    )(c_arr, *gs, *gots)


def _scatter_start(k, ss, thru):
    n = len(ss)

    def body(*refs):
        s_refs, land_refs = refs[2 * n + 1:3 * n + 1], refs[3 * n + 1:4 * n + 1]
        send_sems, recv_sems = refs[4 * n + 2:]
        x, y, c, chips = _place()
        me = 2 * x + y
        for a in range(n):
            for j, chip in enumerate(chips):
                pltpu.make_async_remote_copy(
                    src_ref=s_refs[a].at[2 * chip[0] + chip[1]], dst_ref=land_refs[a].at[me],
                    send_sem=send_sems.at[3 * a + j], recv_sem=recv_sems.at[3 * a + j], device_id=(*chip, c),
                    device_id_type=MESH).start()

    lands = [lax.empty(s.shape, s.dtype) for s in ss]
    hbm = [pltpu.HBM(s.shape, s.dtype) for s in ss]
    outs = pl.pallas_call(
        body, name=f"grad_scatter_start_{k}", in_specs=[HBM_SPEC] * (2 * n + 1),
        out_specs=[HBM_SPEC] * (2 * n + 1) + [SEM_SPEC, SEM_SPEC],
        out_shape=hbm + hbm + [pltpu.HBM(thru.shape, thru.dtype)] + [pltpu.SemaphoreType.DMA((3 * n,))] * 2,
        input_output_aliases={a: a for a in range(2 * n + 1)}, compiler_params=SPLIT_COPY,
    )(*[pltpu.with_memory_space_constraint(v, pltpu.HBM) for v in list(ss) + lands + [thru]])
    return (outs[:n], outs[n:2 * n], outs[2 * n + 1], outs[2 * n + 2]), outs[2 * n]


def _scatter_wait(k, ss, lands, send_sems, recv_sems, after):
    n = len(ss)

    def body(*refs):
        s_refs, land_refs = refs[:n], refs[n:2 * n]
        ssem, rsem = refs[2 * n], refs[2 * n + 1]
        x, y, c, chips = _place()
        me = 2 * x + y
        for a in range(n):
            for j, chip in enumerate(chips):
                cid = 2 * chip[0] + chip[1]
                cp = pltpu.make_async_remote_copy(
                    src_ref=s_refs[a].at[cid], dst_ref=land_refs[a].at[cid], send_sem=ssem.at[3 * a + j],
                    recv_sem=rsem.at[3 * a + j], device_id=(*chip, c), device_id_type=MESH)
                cp.wait_send()
                cp.wait_recv()

    hbm = [pltpu.HBM(s.shape, s.dtype) for s in ss]
    outs = pl.pallas_call(
        body, name=f"grad_scatter_wait_{k}",
        in_specs=[HBM_SPEC] * (2 * n) + [SEM_SPEC, SEM_SPEC, pl.BlockSpec(memory_space=pl.ANY)],
        out_specs=[HBM_SPEC] * (2 * n), out_shape=hbm + hbm,
        input_output_aliases={a: a for a in range(2 * n)}, compiler_params=SPLIT_COPY,
    )(*ss, *lands, send_sems, recv_sems, after)
    return outs[:n], outs[n:]


def _sum_chips(lands, ss, me_arr):
    n = len(lands)

    def body(me_ref, *refs):
        for own in range(N_CHIPS):
            @pl.when(me_ref[0] == own)
            def _(own=own):
                for a in range(n):
                    acc = None
                    for k in range(N_CHIPS):
                        term = (refs[n + a][...] if k == own else refs[a][k]).astype(F32)
                        acc = term if acc is None else acc + term
                    refs[2 * n + a][...] = acc

    return pl.pallas_call(
        body, name="grad_sum_chips",
        grid_spec=pltpu.PrefetchScalarGridSpec(
            num_scalar_prefetch=1, grid=(1,),
            in_specs=[pl.BlockSpec(la.shape, lambda i, me: (0, 0, 0)) for la in lands]
            + [pl.BlockSpec((None,) + la.shape[1:], lambda i, me: (me[0], 0, 0)) for la in lands],
            out_specs=[pl.BlockSpec(la.shape[1:], lambda i, me: (0, 0)) for la in lands]),
        out_shape=[jax.ShapeDtypeStruct(la.shape[1:], F32) for la in lands], compiler_params=_params(),
    )(me_arr, *lands, *ss)


def _allreduce_rows(stats):
    def body(s_ref, o_ref, buf, send_sems, recv_sems):
        x, y, c, _ = _place()
        me = 4 * x + 2 * y + c
        buf[me] = s_ref[...]
        cps = []
        for k in range(1, 8):
            px = jnp.where(k & 4, 1 - x, x)
            py = jnp.where(k & 2, 1 - y, y)
            pc = jnp.where(k & 1, 1 - c, c)
            cps.append(pltpu.make_async_remote_copy(
                src_ref=s_ref, dst_ref=buf.at[me], send_sem=send_sems.at[k - 1], recv_sem=recv_sems.at[k - 1],
                device_id=(px, py, pc), device_id_type=MESH))
        for cp in cps:
            cp.start()
        for cp in cps:
            cp.wait()
        acc = buf[0]
        for d in range(1, 8):
            acc = acc + buf[d]
        o_ref[...] = acc

    vm = pl.BlockSpec(memory_space=pltpu.VMEM)
    return pl.pallas_call(
        body, name="allreduce_rows", in_specs=[vm], out_specs=vm,
        out_shape=jax.ShapeDtypeStruct(stats.shape, F32),
        scratch_shapes=[pltpu.VMEM((8,) + stats.shape, F32), pltpu.SemaphoreType.DMA((7,)),
                        pltpu.SemaphoreType.DMA((7,))],
    )(stats)


def _adamw_math(w, g, m, v):
    m = ADAM_B1 * m + (1.0 - ADAM_B1) * g
    v = ADAM_B2 * v + (1.0 - ADAM_B2) * (g * g)
    m_hat = m / (1.0 - ADAM_B1 ** ADAM_STEP)
    v_hat = v / (1.0 - ADAM_B2 ** ADAM_STEP)
    delta = -ADAM_LR * (m_hat / (jnp.sqrt(v_hat) + ADAM_EPS) + ADAM_WD * w)
    return delta, m, v


def _adamw(ws, ms, vs, mines, theirs, l, c_arr, earlier, after):
    n = len(ws)
    held = [t for e in earlier if e is not None for t in e]
    assert len(held) in (0, 4 * n)
    late = [] if after is None else [after]

    def body(c_ref, *refs):
        outs = refs[len(refs) - 4 * n:]
        for a in range(n):
            w_ref, m_ref, v_ref, a_ref, b_ref = refs[5 * a:5 * a + 5]
            g = jnp.where(pl.program_id(0) == c_ref[0], a_ref[...], b_ref[...])
            delta, mn, vn = _adamw_math(w_ref[...], g, m_ref[...], v_ref[...])
            outs[4 * a][...] = g
            outs[4 * a + 1][...] = delta
            outs[4 * a + 2][...] = mn
            outs[4 * a + 3][...] = vn

    def blk(w):
        tr = w.shape[1] // 4
        return pl.BlockSpec((None, tr, w.shape[2]), lambda hh, i, cr: (l, 2 * hh + i, 0))

    def half(w):
        return pl.BlockSpec((w.shape[1] // 4, w.shape[2]), lambda hh, i, cr: (i, 0))

    outs = pl.pallas_call(
        body, name="adamw",
        grid_spec=pltpu.PrefetchScalarGridSpec(
            num_scalar_prefetch=1, grid=(2, 2),
            in_specs=[sp for w in ws for sp in (blk(w), blk(w), blk(w), half(w), half(w))]
            + [pl.BlockSpec(memory_space=pl.ANY)] * (len(held) + len(late)),
            out_specs=[blk(w) for w in ws for _ in range(4)]),
        out_shape=[jax.ShapeDtypeStruct(w.shape, F32) for w in ws for _ in range(4)],
        input_output_aliases={1 + 5 * n + t: t for t in range(len(held))}, compiler_params=_params(),
    )(c_arr, *[t for grp in zip(ws, ms, vs, mines, theirs) for t in grp], *held, *late)
    return [outs[4 * a:4 * a + 4] for a in range(n)]


def _adamw_rows(w, m, v, g):
    def body(w_ref, m_ref, v_ref, g_ref, d_ref, mo_ref, vo_ref):
        delta, mn, vn = _adamw_math(w_ref[...], g_ref[...], m_ref[...], v_ref[...])
        d_ref[...] = delta
        mo_ref[...] = mn
        vo_ref[...] = vn

    vm = pl.BlockSpec(memory_space=pltpu.VMEM)
    sh = jax.ShapeDtypeStruct(w.shape, F32)
    return pl.pallas_call(body, name="adamw_rows", in_specs=[vm] * 4, out_specs=[vm] * 3, out_shape=[sh] * 3)(w, m, v, g)


SUBLAYERS = (("ffn1_w_gate", "ffn1_w_up", "ffn1_w_down"), ("w_in", "w_proj_dil", "w_proj_sb", "w_out"),
             ("ffn2_w_gate", "ffn2_w_up", "ffn2_w_down"))
TRANSPOSED = ("ffn1_w_gate", "ffn1_w_up", "ffn2_w_gate", "ffn2_w_up")


def _pick_row(blocks):
    row = lax.broadcasted_iota(jnp.int32, (8, D_MODEL), 0)
    out = jnp.zeros((8, D_MODEL), F32)
    for i, b in enumerate(blocks):
        out = out + jnp.where(row == i, b, 0.0)
    return out


def kernel(x, norm_ffn1, ffn1_w_gate, ffn1_w_up, ffn1_w_down, norm_mix, w_in, w_proj_dil, w_proj_sb, w_out, norm_ffn2, ffn2_w_gate, ffn2_w_up, ffn2_w_down, norm_final, loss_target, m_norm_ffn1, m_ffn1_w_gate, m_ffn1_w_up, m_ffn1_w_down, m_norm_mix, m_w_in, m_w_proj_dil, m_w_proj_sb, m_w_out, m_norm_ffn2, m_ffn2_w_gate, m_ffn2_w_up, m_ffn2_w_down, m_norm_final, v_norm_ffn1, v_ffn1_w_gate, v_ffn1_w_up, v_ffn1_w_down, v_norm_mix, v_w_in, v_w_proj_dil, v_w_proj_sb, v_w_out, v_norm_ffn2, v_ffn2_w_gate, v_ffn2_w_up, v_ffn2_w_down, v_norm_final):
    given = dict(locals())
    for n in TRANSPOSED:
        for k in ("", "m_", "v_"):
            given[k + n] = jnp.swapaxes(given[k + n], 1, 2)
    weights = {n: given[n] for n in WEIGHT_NAMES}
    norms = {n: given[n] for n in NORM_NAMES}

    c_arr = lax.axis_index("c").astype(jnp.int32).reshape(1)
    me_arr = (2 * lax.axis_index("x") + lax.axis_index("y")).astype(jnp.int32).reshape(1)
    order = [(l, s, n) for l in range(DEPTH) for s in range(len(SUBLAYERS)) for n in SUBLAYERS[s]]
    n_first = len(SUBLAYERS[0])
    sent, token = {}, None
    for tag, idxs in (("a", range(n_first)), ("b", range(n_first, len(order)))):
        cast = _cast_into_slot([weights[order[i][2]] for i in idxs], [order[i][0] for i in idxs], me_arr, token)
        bufs, send_sems, recv_sems, token = _gather_start(tag, cast)
        for p, i in enumerate(idxs):
            sent[i] = (bufs[p], p, send_sems, recv_sems)

    def weights_of(l, s, after):
        idxs = [i for i, (ll, ss, _) in enumerate(order) if (ll, ss) == (l, s)]
        got = _gather_wait(len(SUBLAYERS) * l + s, [sent[i][0] for i in idxs], [sent[i][1] for i in idxs],
                           sent[idxs[0]][2], sent[idxs[0]][3], after)
        return {order[i][2]: g for i, g in zip(idxs, _gather_relay(got))}

    out = {}
    to_add, in_flight = [], []

    def add_and_scatter(after):
        l, s, names, gs, lands, ssem, rsem = to_add.pop(0)
        k = len(SUBLAYERS) * l + s
        got = _sibling_wait(f"grad_exchange_wait_{k}", gs, _other_half, lands, ssem, rsem, after)
        sent, after = _scatter_start(k, _add_half(gs, got, c_arr), after)
        in_flight.append((l, s, names) + sent)
        return after

    def on_grads(l, s, grads, after):
        names = list(grads)
        k = len(SUBLAYERS) * l + s
        gs = [grads[n] for n in names]
        sent, after, _ = _sibling_start(f"grad_exchange_start_{k}", gs, _other_half,
                                        [(g.shape[0], g.shape[1] // 2, g.shape[2]) for g in gs], after)
        if to_add:
            after = add_and_scatter(after)
        to_add.append((l, s, names) + sent)
        return after

    loss_blk, grad_x, gains, dg_final = _local_step(x[0], loss_target[0], norms, norm_final, weights_of, on_grads)
    grad_x = add_and_scatter(grad_x)

    def update(l, names, mine, swap, after):
        theirs = _sibling_wait(f"grad_swap_wait_{l}_{names[0]}", mine, _all_of, *swap, grad_x if after is None else after)
        res = _adamw([weights[n] for n in names], [given["m_" + n] for n in names], [given["v_" + n] for n in names],
                     mine, theirs, l, c_arr, [out.get(n) for n in names], after)
        out.update(zip(names, res))

    waiting = None
    for l, s, names, sums, lands, ssem, rsem in in_flight:
        sums, lands = _scatter_wait(len(SUBLAYERS) * l + s, sums, lands, ssem, rsem, grad_x)
        mine = _sum_chips(lands, sums, me_arr)
        (mine, *swap), _, token = _sibling_start(f"grad_swap_start_{l}_{names[0]}", mine, _all_of,
                                                 [m.shape for m in mine], None)
        if waiting is not None:
            update(*waiting, token)
        waiting = (l, names, mine, swap)
    update(*waiting, None)
    out = {k + n: (jnp.swapaxes(v, 1, 2) if n in TRANSPOSED else v)
           for n, res in out.items() for k, v in zip(("grad_", "delta_", "new_m_", "new_v_"), res)}
    out["grad_x"] = grad_x[None]

    rows = [gains[l][n] for n in NORM_NAMES for l in range(DEPTH)] + [dg_final, loss_blk]
    total = _allreduce_rows(_pick_row(rows))
    out["loss"] = total[7, 0]
    wn = jnp.concatenate([given[n] for n in NORM_NAMES] + [norm_final[None], jnp.zeros((1, D_MODEL), F32)])
    mn_ = jnp.concatenate([given["m_" + n] for n in NORM_NAMES] + [m_norm_final[None], jnp.zeros((1, D_MODEL), F32)])
    vn_ = jnp.concatenate([given["v_" + n] for n in NORM_NAMES] + [v_norm_final[None], jnp.ones((1, D_MODEL), F32)])
    d_n, m_n, v_n = _adamw_rows(wn, mn_, vn_, total)
    for i, n in enumerate(NORM_NAMES):
        sl = slice(i * DEPTH, (i + 1) * DEPTH)
        out["grad_" + n], out["delta_" + n], out["new_m_" + n], out["new_v_" + n] = total[sl], d_n[sl], m_n[sl], v_n[sl]
    out["grad_norm_final"], out["delta_norm_final"] = total[6], d_n[6]
    out["new_m_norm_final"], out["new_v_norm_final"] = m_n[6], v_n[6]

    names = ["norm_ffn1", "ffn1_w_gate", "ffn1_w_up", "ffn1_w_down", "norm_mix", "w_in", "w_proj_dil", "w_proj_sb",
             "w_out", "norm_ffn2", "ffn2_w_gate", "ffn2_w_up", "ffn2_w_down", "norm_final"]
    return (out["loss"], out["grad_x"], *[out["grad_" + n] for n in names], *[out["delta_" + n] for n in names],
            *[out["new_m_" + n] for n in names], *[out["new_v_" + n] for n in names])
```

```python
import functools

import jax
import jax.numpy as jnp
from jax import lax
from jax.experimental import pallas as pl
from jax.experimental.pallas import tpu as pltpu

F32 = jnp.float32
BF16 = jnp.bfloat16

D_MODEL = 1024
DEPTH = 2
N_CHIPS = 4
HEAD_DIM = 64
ROPE_DIM = 16
ROPE_THETA = 500000.0
DIL_GROUPS = ((128, 1), (512, 4), (2048, 16))
SPAN = 128
Q_BLOCK = 128
RMS_EPS = 1e-6
D_ATT = 256
COL_QS = 2304
COL_GD = 3072
COL_GS = 4096
ADAM_LR, ADAM_B1, ADAM_B2, ADAM_EPS, ADAM_WD, ADAM_STEP = 0.001, 0.9, 0.999, 1e-08, 0.01, 10

VMEM_LIMIT = 52 * 1024 * 1024
TM = 512
NEG = -1e30

NN = (((1,), (0,)), ((), ()))
NT = (((1,), (1,)), ((), ()))
TN = (((0,), (0,)), ((), ()))
MESH = pl.DeviceIdType.MESH

WEIGHT_NAMES = ("ffn1_w_gate", "ffn1_w_up", "ffn1_w_down", "w_in", "w_proj_dil",
                "w_proj_sb", "w_out", "ffn2_w_gate", "ffn2_w_up", "ffn2_w_down")
NORM_NAMES = ("norm_ffn1", "norm_mix", "norm_ffn2")


def _params(**kw):
    return pltpu.CompilerParams(vmem_limit_bytes=VMEM_LIMIT, **kw)


def _sigmoid(x):
    return 0.5 * jnp.tanh(0.5 * x) + 0.5


def _mm_body(pairs, n_in, n_out, n_acc, dims, nk, i_axis, epilogue, *refs):
    ins = refs[:n_in]
    outs = refs[n_in:n_in + n_out]
    accs = refs[n_in + n_out:]
    i = pl.program_id(i_axis)
    k = pl.program_id(2)

    def operand(a):
        return (a(ins) if callable(a) else ins[a][...]).astype(BF16)

    def dot(ia, ib):
        return lax.dot_general(operand(ia), operand(ib), dims, preferred_element_type=F32)

    if nk == 1:
        parts = [None] * n_acc
        for ia, ib, ic in pairs:
            parts[ic] = dot(ia, ib) if parts[ic] is None else parts[ic] + dot(ia, ib)
        epilogue(parts, ins, outs, i)
        return

    @pl.when(k == 0)
    def _():
        for c in range(n_acc):
            accs[c][...] = jnp.zeros_like(accs[c])

    for ia, ib, ic in pairs:
        accs[ic][...] += dot(ia, ib)

    @pl.when(k == nk - 1)
    def _():
        epilogue([a[...] for a in accs], ins, outs, i)


def _j_outer(spec):
    f = spec.index_map
    return pl.BlockSpec(spec.block_shape, lambda j, i, k: f(i, j, k))


def _mm(name, ins, in_specs, pairs, n_acc, acc_shape, dims, grid, epilogue, out_shapes, out_specs, j_outer=False):
    nk = grid[2]
    if j_outer:
        grid = (grid[1], grid[0], grid[2])
        in_specs = [_j_outer(s) for s in in_specs]
        out_specs = [_j_outer(s) for s in out_specs]
    scratch = [pltpu.VMEM(acc_shape, F32) for _ in range(n_acc)] if nk > 1 else []
    body = functools.partial(_mm_body, tuple(pairs), len(ins), len(out_shapes), n_acc, dims, nk,
                             1 if j_outer else 0, epilogue)
    return pl.pallas_call(
        body, name=name, grid=grid, in_specs=in_specs, out_specs=out_specs, out_shape=out_shapes,
        scratch_shapes=scratch,
        compiler_params=_params(dimension_semantics=("arbitrary", "arbitrary", "arbitrary")),
    )(*ins)


def _wspec(r, c, l, by):
    if by == 1:
        return pl.BlockSpec((None, None, r, c), lambda i, j, k: (j, l, 0, 0))
    return pl.BlockSpec((None, None, r, c), lambda i, j, k: (k, l, 0, 0))


def _rms_bwd_epilogue(x_idx, g_idx, dxo_idx):
    def ep(vals, ins, outs, i):
        dh = vals[0]
        x = ins[x_idx][...]
        g = ins[g_idx][...]
        rstd = lax.rsqrt(jnp.mean(x * x, axis=-1, keepdims=True) + RMS_EPS)
        xhat = x * rstd
        dxhat = dh * g
        dx = rstd * (dxhat - xhat * jnp.mean(dxhat * xhat, axis=-1, keepdims=True))
        outs[0][...] = ins[dxo_idx][...] + dx
        dg = jnp.broadcast_to(jnp.sum(dh * xhat, axis=0, keepdims=True), outs[1].shape)

        @pl.when(i == 0)
        def _():
            outs[1][...] = dg

        @pl.when(i > 0)
        def _():
            outs[1][...] += dg
    return ep


def _normed(x_idx, g_idx):
    seen = {}

    def f(ins):
        if id(ins) not in seen:
            xv = ins[x_idx][...]
            h = xv * lax.rsqrt(jnp.mean(xv * xv, axis=-1, keepdims=True) + RMS_EPS)
            seen[id(ins)] = (ins, (h * ins[g_idx][...]).astype(BF16))
        return seen[id(ins)][1]
    return f


def _rope_tables(T):
    pos = jnp.arange(T, dtype=F32)
    inv_freq = ROPE_THETA ** (-jnp.arange(0, ROPE_DIM, 2, dtype=F32) / ROPE_DIM)
    ang = pos[:, None] * inv_freq[None, :]
    cos, sin = jnp.cos(ang), jnp.sin(ang)
    half = ROPE_DIM // 2
    one = jnp.ones((T, HEAD_DIM - ROPE_DIM), F32)
    zero = jnp.zeros((T, HEAD_DIM - ROPE_DIM), F32)
    zh = jnp.zeros((T, half), F32)
    c = jnp.concatenate([cos, cos, one], axis=1)
    s1 = jnp.concatenate([-sin, zh, zero], axis=1)
    s2 = jnp.concatenate([zh, sin, zero], axis=1)
    return tuple(jnp.concatenate([t, t], axis=1) for t in (c, s1, s2))


def _rope_fwd(xv, c, s1, s2):
    w = xv.shape[1]
    half = ROPE_DIM // 2
    return xv * c + pltpu.roll(xv, w - half, 1) * s1 + pltpu.roll(xv, half, 1) * s2


def _rope_bwd(dy, c, s1, s2):
    w = dy.shape[1]
    half = ROPE_DIM // 2
    return dy * c + pltpu.roll(dy * s1, half, 1) + pltpu.roll(dy * s2, w - half, 1)


def _assemble_dproj(dqk, rest, gates, tabs):
    T = gates[0].shape[0]
    n_qk, n_rest = len(dqk), len(rest)
    width = (n_qk + n_rest) * D_ATT + 2 * D_MODEL

    def body(*refs):
        ins, (c_ref, s1_ref, s2_ref), o_ref = refs[:n_qk + n_rest + 2], refs[-4:-1], refs[-1]
        c = jnp.concatenate([c_ref[...]] * 2, axis=1)
        s1 = jnp.concatenate([s1_ref[...]] * 2, axis=1)
        s2 = jnp.concatenate([s2_ref[...]] * 2, axis=1)
        for b in range(n_qk + n_rest):
            v = ins[b][...]
            if b < n_qk:
                v = _rope_bwd(v, c, s1, s2)
            o_ref[:, b * D_ATT:(b + 1) * D_ATT] = v.astype(BF16)
        off = (n_qk + n_rest) * D_ATT
        o_ref[:, off:off + D_MODEL] = ins[-2][...]
        o_ref[:, off + D_MODEL:] = ins[-1][...]

    att = pl.BlockSpec((TM, D_ATT), lambda i: (i, 0))
    wide = pl.BlockSpec((TM, D_MODEL), lambda i: (i, 0))
    tab = pl.BlockSpec((TM, 128), lambda i: (i, 0))
    return pl.pallas_call(
        body, name="assemble_dproj", grid=(T // TM,),
        in_specs=[att] * (n_qk + n_rest) + [wide, wide, tab, tab, tab],
        out_specs=pl.BlockSpec((TM, width), lambda i: (i, 0)),
        out_shape=jax.ShapeDtypeStruct((T, width), BF16), compiler_params=_params(),
    )(*dqk, *rest, *gates, *tabs)


def _dil_merge(os_, lses):
    T = os_[0].shape[0]

    def body(o0, o1, o2, l0, l1, l2, o_ref, lse_ref):
        a, b, c = l0[...], l1[...], l2[...]
        m = jnp.maximum(jnp.maximum(a, b), c)
        ea, eb, ec = jnp.exp(a - m), jnp.exp(b - m), jnp.exp(c - m)
        den = ea + eb + ec
        o_ref[...] = (ea * o0[...] + eb * o1[...] + ec * o2[...]) / den
        lse_ref[...] = m + jnp.log(den)

    blk = pl.BlockSpec((TM, D_ATT), lambda i: (i, 0))
    sh = jax.ShapeDtypeStruct((T, D_ATT), F32)
    return pl.pallas_call(
        body, name="dil_merge", grid=(T // TM,), in_specs=[blk] * 6, out_specs=[blk, blk],
        out_shape=[sh, sh], compiler_params=_params(),
    )(*os_, *lses)


def _final_loss(x, gain, target):
    T = x.shape[0]

    def body(x_ref, g_ref, t_ref, dx_ref, dg_ref, loss_ref):
        xv = x_ref[...]
        g = g_ref[...]
        rstd = lax.rsqrt(jnp.mean(xv * xv, axis=-1, keepdims=True) + RMS_EPS)
        xhat = xv * rstd
        err = xhat * g - t_ref[...]
        loss = 0.5 * jnp.sum(jnp.mean(err * err, axis=-1, keepdims=True), axis=0, keepdims=True)
        dy = err * (1.0 / D_MODEL)
        dxhat = dy * g
        dx_ref[...] = rstd * (dxhat - xhat * jnp.mean(dxhat * xhat, axis=-1, keepdims=True))
        dg = jnp.broadcast_to(jnp.sum(dy * xhat, axis=0, keepdims=True), dg_ref.shape)
        ls = jnp.broadcast_to(loss, loss_ref.shape)

        @pl.when(pl.program_id(0) == 0)
        def _():
            dg_ref[...] = dg
            loss_ref[...] = ls

        @pl.when(pl.program_id(0) > 0)
        def _():
            dg_ref[...] += dg
            loss_ref[...] += ls

    blk = pl.BlockSpec((TM, D_MODEL), lambda i: (i, 0))
    row = pl.BlockSpec((1, D_MODEL), lambda i: (0, 0))
    acc = pl.BlockSpec((8, D_MODEL), lambda i: (0, 0))
    return pl.pallas_call(
        body, name="final_loss", grid=(T // TM,), in_specs=[blk, row, blk], out_specs=[blk, acc, acc],
        out_shape=[jax.ShapeDtypeStruct((T, D_MODEL), F32), jax.ShapeDtypeStruct((8, D_MODEL), F32),
                   jax.ShapeDtypeStruct((8, D_MODEL), F32)],
        compiler_params=_params(dimension_semantics=("arbitrary",)),
    )(x, gain, target)


def _pair_masks():
    lane = lax.broadcasted_iota(jnp.int32, (SPAN, 128), 1)
    return [lane < HEAD_DIM, lane >= HEAD_DIM]


def _stack_heads(x, masks):
    return jnp.concatenate([jnp.where(m, x, 0.0) for m in masks], axis=0)


def _unstack_heads(y, masks):
    rows = y.shape[0] // len(masks)
    out = jnp.where(masks[0], y[:rows], 0.0)
    for h in range(1, len(masks)):
        out = out + jnp.where(masks[h], y[rows * h:rows * (h + 1)], 0.0)
    return out


DIL_PAIR = 2


def _dil_rows(idx, d):
    u = idx // d
    r = idx - u * d
    own = pl.ds(u * (SPAN * d) + r, SPAN, stride=d) if d > 1 else pl.ds(pl.multiple_of(u * SPAN, SPAN), SPAN)
    up = jnp.maximum(u - 1, 0)
    prev = pl.ds(up * (SPAN * d) + r, SPAN, stride=d) if d > 1 else pl.ds(pl.multiple_of(up * SPAN, SPAN), SPAN)
    return u, own, prev


def _dil_valid(u):
    qi = lax.broadcasted_iota(jnp.int32, (2 * SPAN, 2 * SPAN), 0) & (SPAN - 1)
    kj = lax.broadcasted_iota(jnp.int32, (2 * SPAN, 2 * SPAN), 1)
    in_prev = (kj < SPAN) & (kj >= qi + jnp.where(u > 0, 0, SPAN))
    return in_prev | ((kj >= SPAN) & (kj - SPAN <= qi))


def _dil_keys(ref, own, prev):
    return jnp.concatenate([ref[prev, :], ref[own, :]], axis=0).astype(BF16)


def _dil_fwd(proj, g, d):
    T = proj.shape[0]
    n_iter = T // SPAN

    def body(q_ref, k_ref, v_ref, o_ref, lse_ref):
        masks = _pair_masks()

        def step(pair, carry):
            its = [_dil_rows(DIL_PAIR * pair + e, d) for e in range(DIL_PAIR)]
            qs = [_stack_heads(q_ref[own, :] * (HEAD_DIM ** -0.5), masks).astype(BF16) for _, own, _ in its]
            kks = [_dil_keys(k_ref, own, prev) for _, own, prev in its]
            vvs = [_dil_keys(v_ref, own, prev) for _, own, prev in its]
            ss = [jnp.where(_dil_valid(u), lax.dot_general(q, kk, NT, preferred_element_type=F32), NEG)
                  for (u, _, _), q, kk in zip(its, qs, kks)]
            ms = [jnp.max(s, axis=1, keepdims=True) for s in ss]
            ps = [jnp.exp(s - m) for s, m in zip(ss, ms)]
            dens = [jnp.sum(p, axis=1, keepdims=True) for p in ps]
            pvs = [lax.dot_general(p.astype(BF16), vv, NN, preferred_element_type=F32) / den
                   for p, vv, den in zip(ps, vvs, dens)]
            for (_, own, _), pv, m, den in zip(its, pvs, ms, dens):
                o_ref[own, :] = _unstack_heads(pv, masks)
                lse_ref[own, :] = _unstack_heads(jnp.broadcast_to(m + jnp.log(den), pv.shape), masks)
            return carry

        lax.fori_loop(0, n_iter // DIL_PAIR, step, 0)

    def col(b):
        return pl.BlockSpec((T, 128), lambda p: (0, b + p))

    sh = jax.ShapeDtypeStruct((T, D_ATT), F32)
    out = pl.BlockSpec((T, 128), lambda p: (0, p))
    return pl.pallas_call(
        body, name=f"dil_fwd_d{d}", grid=(2,),
        in_specs=[col(2 * g), col(6 + 2 * g), col(12 + 2 * g)], out_specs=[out, out], out_shape=[sh, sh],
        compiler_params=_params(dimension_semantics=("arbitrary",)),
    )(proj, proj, proj)


def _dil_bwd(proj, do, o_dil, lse, g, d):
    T = proj.shape[0]
    n_iter = T // SPAN

    def body(q_ref, k_ref, v_ref, do_ref, o_ref, lse_ref, dq_ref, dk_ref, dv_ref):
        masks = _pair_masks()
        head_lanes = jnp.concatenate(masks, axis=0)

        def step(pair, carry):
            its = [_dil_rows(DIL_PAIR * pair + e, d) for e in range(DIL_PAIR)]
            qs = [_stack_heads(q_ref[own, :] * (HEAD_DIM ** -0.5), masks).astype(BF16) for _, own, _ in its]
            kks = [_dil_keys(k_ref, own, prev) for _, own, prev in its]
            vvs = [_dil_keys(v_ref, own, prev) for _, own, prev in its]
            doms = [_stack_heads(do_ref[own, :], masks) for _, own, _ in its]
            dos = [dom.astype(BF16) for dom in doms]
            deltas = [jnp.sum(dom * jnp.concatenate([o_ref[own, :]] * 2, axis=0), axis=1, keepdims=True)
                      for dom, (_, own, _) in zip(doms, its)]
            lrows = [jnp.max(jnp.where(head_lanes, jnp.concatenate([lse_ref[own, :]] * 2, axis=0), NEG),
                             axis=1, keepdims=True) for _, own, _ in its]
            ss = [lax.dot_general(q, kk, NT, preferred_element_type=F32) for q, kk in zip(qs, kks)]
            dps = [lax.dot_general(do_b, vv, NT, preferred_element_type=F32) for do_b, vv in zip(dos, vvs)]
            ps = [jnp.where(_dil_valid(u), jnp.exp(s - lrow), 0.0) for (u, _, _), s, lrow in zip(its, ss, lrows)]
            dss = [(p * (dp - delta)).astype(BF16) for p, dp, delta in zip(ps, dps, deltas)]
            dqs = [lax.dot_general(ds, kk, NN, preferred_element_type=F32) for ds, kk in zip(dss, kks)]
            dkks = [lax.dot_general(ds, q, TN, preferred_element_type=F32) for ds, q in zip(dss, qs)]
            dvvs = [lax.dot_general(p.astype(BF16), do_b, TN, preferred_element_type=F32) for p, do_b in zip(ps, dos)]
            for (_, own, prev), dq, dkk, dvv in zip(its, dqs, dkks, dvvs):
                dq_ref[own, :] = _unstack_heads(dq, masks) * (HEAD_DIM ** -0.5)
                dk_ref[own, :] = dkk[SPAN:]
                dv_ref[own, :] = dvv[SPAN:]
                dk_ref[prev, :] = dk_ref[prev, :] + dkk[:SPAN]
                dv_ref[prev, :] = dv_ref[prev, :] + dvv[:SPAN]
            return carry

        lax.fori_loop(0, n_iter // DIL_PAIR, step, 0)

    def col(b):
        return pl.BlockSpec((T, 128), lambda p: (0, b + p))

    sh = jax.ShapeDtypeStruct((T, D_ATT), F32)
    return pl.pallas_call(
        body, name=f"dil_bwd_d{d}", grid=(2,),
        in_specs=[col(2 * g), col(6 + 2 * g), col(12 + 2 * g), col(0), col(0), col(0)],
        out_specs=[col(0), col(0), col(0)], out_shape=[sh, sh, sh],
        compiler_params=_params(dimension_semantics=("arbitrary",)),
    )(proj, proj, proj, do, o_dil, lse)


SB_KT = 512


def _sb_tri(strict):
    a = lax.broadcasted_iota(jnp.int32, (Q_BLOCK, Q_BLOCK), 0)
    b = lax.broadcasted_iota(jnp.int32, (Q_BLOCK, Q_BLOCK), 1)
    return jnp.where((a > b) if strict else (a >= b), 1.0, 0.0).astype(BF16)


def _split_stack(x):
    nb = x.shape[1] // Q_BLOCK
    blocks = [x[:, Q_BLOCK * b:Q_BLOCK * (b + 1)] for b in range(nb)]
    hi = [b.astype(BF16) for b in blocks]
    lo = [(b - h.astype(F32)).astype(BF16) for b, h in zip(blocks, hi)]
    return blocks, jnp.concatenate(hi + lo, axis=0)


def _suffix_from(y, blocks, c):
    r = blocks[0].shape[0]
    nb = len(blocks)
    outs = [None] * nb
    run = c
    for b in reversed(range(nb)):
        outs[b] = run + y[r * b:r * (b + 1)] + y[r * (nb + b):r * (nb + b + 1)]
        run = run + jnp.sum(blocks[b], axis=1, keepdims=True)
    return jnp.concatenate(outs, axis=1), run


SB_HEADS = D_ATT // HEAD_DIM
SB_FWD_CHAINS = 2
SB_BWD_CHAINS = 1


def _sb_past(i, t, rows):
    row = lax.broadcasted_iota(jnp.int32, (rows, SB_KT), 0) & (Q_BLOCK - 1)
    col = lax.broadcasted_iota(jnp.int32, (rows, SB_KT), 1)
    return col + t * SB_KT < row + i * Q_BLOCK


def _sb_head_masks(chains):
    lane = lax.broadcasted_iota(jnp.int32, (Q_BLOCK, D_ATT), 1)
    masks = [(lane >= HEAD_DIM * h) & (lane < HEAD_DIM * (h + 1)) for h in range(SB_HEADS)]
    per = SB_HEADS // chains
    return [masks[per * g:per * (g + 1)] for g in range(chains)]


def _sb_rows(t):
    return pl.ds(pl.multiple_of(t * SB_KT, SB_KT), SB_KT)


def _sb_log_terms(z, past):
    lsz = jnp.minimum(z, 0.0) - jnp.log(1.0 + jnp.exp(-jnp.abs(z)))
    lk = lsz - z
    return lsz, (lk if past is None else jnp.where(past, lk, 0.0))


def _sb_weights(lsz, after, past):
    w = jnp.exp(lsz + after)
    return w if past is None else jnp.where(past, w, 0.0)


def _sb_fwd(proj):
    T = proj.shape[0]

    def body(q_ref, k_ref, v_ref, o_ref):
        i = pl.program_id(0)
        masks = _sb_head_masks(SB_FWD_CHAINS)
        rows = SB_HEADS // SB_FWD_CHAINS * Q_BLOCK
        tri = _sb_tri(True)
        q = q_ref[...] * (HEAD_DIM ** -0.5)
        qs = [_stack_heads(q, m).astype(BF16) for m in masks]
        n_tiles = (i * Q_BLOCK) // SB_KT + 1

        def tile(t, carry, masked):
            kb = k_ref[_sb_rows(t), :].astype(BF16)
            vb = v_ref[_sb_rows(t), :].astype(BF16)
            past = _sb_past(i, t, rows) if masked else None
            acc, cs = carry[0], carry[1:]
            zs = [lax.dot_general(g, kb, NT, preferred_element_type=F32) for g in qs]
            logs = [_sb_log_terms(z, past) for z in zs]
            splits = [_split_stack(lk) for _, lk in logs]
            ys = [lax.dot_general(x, tri, NN, preferred_element_type=F32) for _, x in splits]
            sums = [_suffix_from(y, blocks, c) for y, (blocks, _), c in zip(ys, splits, cs)]
            ws = [_sb_weights(lsz, after, past).astype(BF16) for (lsz, _), (after, _) in zip(logs, sums)]
            for m, w in zip(masks, ws):
                acc = acc + _unstack_heads(lax.dot_general(w, vb, NN, preferred_element_type=F32), m)
            return (acc, *[c for _, c in sums])

        zcol = jnp.zeros((rows, 1), F32)
        carry = tile(n_tiles - 1, (jnp.zeros((Q_BLOCK, D_ATT), F32),) + (zcol,) * SB_FWD_CHAINS, True)
        carry = lax.fori_loop(0, n_tiles - 1, lambda tt, cr: tile(n_tiles - 2 - tt, cr, False), carry)
        o_ref[...] = carry[0]

    cb = COL_QS // D_ATT
    return pl.pallas_call(
        body, name="sb_fwd", grid=(T // Q_BLOCK,),
        in_specs=[pl.BlockSpec((Q_BLOCK, D_ATT), lambda i: (i, cb)),
                  pl.BlockSpec((T, D_ATT), lambda i: (0, cb + 1)),
                  pl.BlockSpec((T, D_ATT), lambda i: (0, cb + 2))],
        out_specs=pl.BlockSpec((Q_BLOCK, D_ATT), lambda i: (i, 0)),
        out_shape=jax.ShapeDtypeStruct((T, D_ATT), F32),
        compiler_params=_params(dimension_semantics=("arbitrary",)),
    )(proj, proj, proj)


def _sb_bwd(proj, do, o):
    T = proj.shape[0]

    def body(q_ref, k_ref, v_ref, do_ref, o_ref, dq_ref, dk_ref, dv_ref):
        i = pl.program_id(0)
        masks = _sb_head_masks(SB_BWD_CHAINS)
        n_rows = SB_HEADS // SB_BWD_CHAINS * Q_BLOCK
        tri = _sb_tri(True)
        tri_incl = _sb_tri(False)

        @pl.when(i == 0)
        def _():
            dk_ref[...] = jnp.zeros_like(dk_ref)
            dv_ref[...] = jnp.zeros_like(dv_ref)

        q = q_ref[...] * (HEAD_DIM ** -0.5)
        qs = [_stack_heads(q, m).astype(BF16) for m in masks]
        dos = [_stack_heads(do_ref[...], m).astype(BF16) for m in masks]
        o_rep = jnp.concatenate([o_ref[...]] * (SB_HEADS // SB_BWD_CHAINS), axis=0)
        deltas = [jnp.sum(d.astype(F32) * o_rep, axis=1, keepdims=True) for d in dos]
        n_tiles = (i * Q_BLOCK) // SB_KT + 1

        def tile(t, carry, masked):
            rows = _sb_rows(t)
            kb = k_ref[rows, :].astype(BF16)
            vb = v_ref[rows, :].astype(BF16)
            past = _sb_past(i, t, n_rows) if masked else None
            dq, cs, ces = carry[0], carry[1:1 + SB_BWD_CHAINS], carry[1 + SB_BWD_CHAINS:]
            zs = [lax.dot_general(g, kb, NT, preferred_element_type=F32) for g in qs]
            gvs = [lax.dot_general(d, vb, NT, preferred_element_type=F32) for d in dos]
            logs = [_sb_log_terms(z, past) for z in zs]
            splits = [_split_stack(lk) for _, lk in logs]
            ys = [lax.dot_general(x, tri, NN, preferred_element_type=F32) for _, x in splits]
            sums = [_suffix_from(y, blocks, c) for y, (blocks, _), c in zip(ys, splits, cs)]
            wbs = [_sb_weights(lsz, after, past).astype(BF16) for (lsz, _), (after, _) in zip(logs, sums)]
            es = [wb.astype(F32) * gv for wb, gv in zip(wbs, gvs)]
            esplits = [_split_stack(e) for e in es]
            eys = [lax.dot_general(x, tri_incl, NN, preferred_element_type=F32) for _, x in esplits]
            esums = [_suffix_from(y, blocks, ce) for y, (blocks, _), ce in zip(eys, esplits, ces)]
            dzbs = []
            for e, (lsz, lk), (suf, _), delta in zip(es, logs, esums, deltas):
                dz = e * jnp.exp(lk) - (delta - suf) * jnp.exp(lsz)
                dzbs.append((dz if past is None else jnp.where(past, dz, 0.0)).astype(BF16))
            dk_t = dv_t = None
            for m, dzb, wb, g, d in zip(masks, dzbs, wbs, qs, dos):
                dq = dq + _unstack_heads(lax.dot_general(dzb, kb, NN, preferred_element_type=F32), m)
                a = lax.dot_general(dzb, g, TN, preferred_element_type=F32)
                b = lax.dot_general(wb, d, TN, preferred_element_type=F32)
                dk_t = a if dk_t is None else dk_t + a
                dv_t = b if dv_t is None else dv_t + b
            dk_ref[rows, :] = dk_ref[rows, :] + dk_t
            dv_ref[rows, :] = dv_ref[rows, :] + dv_t
            return (dq, *[c for _, c in sums], *[c for _, c in esums])

        zcol = jnp.zeros((n_rows, 1), F32)
        carry = tile(n_tiles - 1, (jnp.zeros((Q_BLOCK, D_ATT), F32),) + (zcol,) * (2 * SB_BWD_CHAINS), True)
        carry = lax.fori_loop(0, n_tiles - 1, lambda tt, cr: tile(n_tiles - 2 - tt, cr, False), carry)
        dq_ref[...] = carry[0] * (HEAD_DIM ** -0.5)

    cb = COL_QS // D_ATT
    blk = pl.BlockSpec((Q_BLOCK, D_ATT), lambda i: (i, 0))
    full = pl.BlockSpec((T, D_ATT), lambda i: (0, 0))
    sh = jax.ShapeDtypeStruct((T, D_ATT), F32)
    return pl.pallas_call(
        body, name="sb_bwd", grid=(T // Q_BLOCK,),
        in_specs=[pl.BlockSpec((Q_BLOCK, D_ATT), lambda i: (i, cb)),
                  pl.BlockSpec((T, D_ATT), lambda i: (0, cb + 1)),
                  pl.BlockSpec((T, D_ATT), lambda i: (0, cb + 2)), blk, blk],
        out_specs=[blk, full, full], out_shape=[sh, sh, sh],
        compiler_params=_params(dimension_semantics=("arbitrary",)),
    )(proj, proj, proj, do, o)


def _tok(c, by=None):
    if by is None:
        return pl.BlockSpec((TM, c), lambda i, j, k: (i, 0))
    if by == 1:
        return pl.BlockSpec((TM, c), lambda i, j, k: (i, j))
    return pl.BlockSpec((TM, c), lambda i, j, k: (i, k))


def _chunked(c, by):
    if by == 1:
        return pl.BlockSpec((None, TM, c), lambda i, j, k: (j, i, 0))
    return pl.BlockSpec((None, TM, c), lambda i, j, k: (k, i, 0))


def _gain_spec():
    return pl.BlockSpec((1, D_MODEL), lambda i, j, k: (0, 0))


def _all_chunks(rows, c):
    return pl.BlockSpec((N_CHIPS, rows, c), lambda i, j, k: (0, i, 0))


def _wfull(r, c, l):
    return pl.BlockSpec((N_CHIPS, None, r, c), lambda i, j, k: (0, l, 0, 0))


def _pick(idx, c):
    return lambda ins: ins[idx][c]


def _cols(idx, c, w):
    return lambda ins: ins[idx][:, c * w:(c + 1) * w]


def _rows(rows, width):
    return pl.BlockSpec((rows, width), lambda i, j, k: (i, 0))


def _whole(shape):
    return pl.BlockSpec(shape, lambda i, j, k: (0, 0))


def _ffn_fwd(x, gain, wg, wu, wd):
    T = x.shape[0]
    wg, wu, wd = (w.reshape(-1, D_MODEL) for w in (wg, wu, wd))
    ff = wd.shape[0]
    tm = TM // 2
    normed = _normed(0, 3)

    def swiglu(vals, ins, outs, i):
        gt, up = vals
        s = _sigmoid(gt)
        sil = gt * s
        outs[0][...] = sil.astype(BF16)
        outs[1][...] = (up * (s * (1.0 + gt * (1.0 - s)))).astype(BF16)
        outs[2][...] = (sil * up).astype(BF16)
        outs[3][...] = normed(ins)

    ash = jax.ShapeDtypeStruct((T, ff), BF16)
    sil, up_dsil, act, h = _mm(
        "ffn_up", [x, wg, wu, gain], [_rows(tm, D_MODEL), _whole(wg.shape), _whole(wu.shape), _gain_spec()],
        [(normed, 1, 0), (normed, 2, 1)], 2, None, NT, (T // tm, 1, 1), swiglu,
        [ash] * 3 + [jax.ShapeDtypeStruct((T, D_MODEL), BF16)], [_rows(tm, ff)] * 3 + [_rows(tm, D_MODEL)])

    def resid(vals, ins, outs, i):
        outs[0][...] = ins[2][...] + 0.5 * vals[0]

    (y,) = _mm(
        "ffn_down", [act, wd, x], [_rows(TM, ff), _whole(wd.shape), _tok(D_MODEL)], [(0, 1, 0)], 1, None, NN,
        (T // TM, 1, 1), resid, [jax.ShapeDtypeStruct((T, D_MODEL), F32)], [_tok(D_MODEL)])
    return y, (x, h, sil, up_dsil, act)


def _ffn_bwd(dxo, gain, wg, wu, wd, saved):
    x, h, sil, up_dsil, act = saved
    T = x.shape[0]
    n_chips, _, ffs, _ = wd.shape
    wg, wu, wd = (w.reshape(-1, D_MODEL) for w in (wg, wu, wd))
    ff = wd.shape[0]
    tk = TM
    tm = TM // 2

    def dswiglu(vals, ins, outs, i):
        da = 0.5 * vals[0]
        outs[0][...] = (da * ins[3][...].astype(F32)).astype(BF16)
        outs[1][...] = (da * ins[2][...].astype(F32)).astype(BF16)

    ash = jax.ShapeDtypeStruct((T, ff), BF16)
    dgate, dup = _mm(
        "ffn_dact", [dxo, wd, sil, up_dsil], [_rows(tm, D_MODEL), _whole(wd.shape), _rows(tm, ff), _rows(tm, ff)],
        [(0, 1, 0)], 1, None, NT, (T // tm, 1, 1), dswiglu, [ash, ash], [_rows(tm, ff)] * 2)

    def half(vals, ins, outs, i):
        outs[0][...] = (0.5 * vals[0]).astype(BF16)

    def cast(vals, ins, outs, i):
        outs[0][...] = vals[0].astype(BF16)

    tok_k = pl.BlockSpec((tk, D_MODEL), lambda i, j, k: (k, 0))
    hid_k = pl.BlockSpec((tk, ff), lambda i, j, k: (k, 0))
    wsh = jax.ShapeDtypeStruct((ff, D_MODEL), BF16)
    (dwd,) = _mm("ffn_dwd", [act, dxo], [hid_k, tok_k], [(0, 1, 0)], 1, (ff, D_MODEL), TN, (1, 1, T // tk), half,
                 [wsh], [_whole((ff, D_MODEL))])

    dx, dgain = _mm(
        "ffn_dx", [dgate, dup, wg, wu, x, gain, dxo],
        [_rows(tm, ff), _rows(tm, ff), _whole(wg.shape), _whole(wu.shape), _rows(tm, D_MODEL), _gain_spec(),
         _rows(tm, D_MODEL)],
        [(0, 2, 0), (1, 3, 0)], 1, None, NN, (T // tm, 1, 1), _rms_bwd_epilogue(4, 5, 6),
        [jax.ShapeDtypeStruct((T, D_MODEL), F32), jax.ShapeDtypeStruct((8, D_MODEL), F32)],
        [_rows(tm, D_MODEL), pl.BlockSpec((8, D_MODEL), lambda i, j, k: (0, 0))])

    dws = []
    for dact in (dgate, dup):
        dws += _mm("ffn_dwgu", [dact, h], [hid_k, tok_k], [(0, 1, 0)], 1, (ff, D_MODEL), TN, (1, 1, T // tk), cast,
                   [wsh], [_whole((ff, D_MODEL))])
    dwg, dwu, dwd = (w.reshape(n_chips, ffs, D_MODEL) for w in (dws[0], dws[1], dwd))
    return dx, dgain, dwg, dwu, dwd


def _joined_mixer_weights(wpd, wps, wo):
    n, _, r, c = wpd.shape
    wpd_n, wps_n = (w[:, 0].transpose(1, 0, 2).reshape(r, n * c) for w in (wpd, wps))
    return wpd_n, wps_n, wo.reshape(-1, wo.shape[3])


def _mixer_fwd(x, gain, W, l, tabs):
    T = x.shape[0]
    win, wpd, wps, wo = W["w_in"], W["w_proj_dil"], W["w_proj_sb"], W["w_out"]
    cin = win.shape[3]
    cp = wpd.shape[3]
    normed = _normed(0, 5)
    n_rope = 6 * D_ATT

    tm = TM // 2

    def roped(vals, ins, outs, i):
        for j, v in enumerate(vals):
            lo = j * cin
            k = min(max(n_rope - lo, 0), cin)
            if k:
                tab = [jnp.concatenate([ins[t][...]] * (k // 128), axis=1) for t in (2, 3, 4)]
                outs[0][:, lo:lo + k] = _rope_fwd(v[:, :k], *tab)
            if k < cin:
                outs[0][:, lo + k:lo + cin] = v[:, k:]
        outs[1][...] = normed(ins)

    proj, h = _mm(
        "mix_in", [x, win, *tabs, gain],
        [_rows(tm, D_MODEL), _wfull(D_MODEL, cin, l)] + [_rows(tm, 128)] * 3 + [_gain_spec()],
        [(normed, _pick(1, c), c) for c in range(N_CHIPS)], N_CHIPS, None, NN, (T // tm, 1, 1), roped,
        [jax.ShapeDtypeStruct((T, N_CHIPS * cin), F32), jax.ShapeDtypeStruct((T, D_MODEL), BF16)],
        [_rows(tm, N_CHIPS * cin), _rows(tm, D_MODEL)])

    os_, lses = [], []
    for g, (window, dil) in enumerate(DIL_GROUPS):
        o_g, lse_g = _dil_fwd(proj, g, dil)
        os_.append(o_g)
        lses.append(lse_g)
    o_dil, lse = _dil_merge(os_, lses)
    o_sb = _sb_fwd(proj)

    def gated(vals, ins, outs, i):
        pd, ps = vals
        outs[0][...] = (_sigmoid(ins[4][...]) * pd + _sigmoid(ins[5][...]) * ps).astype(BF16)
        outs[1][...] = pd.astype(BF16)
        outs[2][...] = ps.astype(BF16)

    wpd_n, wps_n, wo_n = _joined_mixer_weights(wpd, wps, wo)
    gd_spec = pl.BlockSpec((TM, D_MODEL), lambda i, j, k: (i, COL_GD // D_MODEL))
    gs_spec = pl.BlockSpec((TM, D_MODEL), lambda i, j, k: (i, COL_GS // D_MODEL))
    ush = jax.ShapeDtypeStruct((T, D_MODEL), BF16)
    u, pd, ps = _mm(
        "mix_gate", [o_dil, o_sb, wpd_n, wps_n, proj, proj],
        [_tok(D_ATT), _tok(D_ATT), _whole(wpd_n.shape), _whole(wps_n.shape), gd_spec, gs_spec],
        [(0, 2, 0), (1, 3, 1)], 2, None, NN, (T // TM, 1, 1), gated, [ush] * 3, [_tok(D_MODEL)] * 3)

    def resid(vals, ins, outs, i):
        outs[0][...] = ins[2][...] + vals[0]

    (y,) = _mm(
        "mix_out", [u, wo_n, x], [_tok(D_MODEL), _whole(wo_n.shape), _tok(D_MODEL)], [(0, 1, 0)], 1, None, NN,
        (T // TM, 1, 1), resid, [jax.ShapeDtypeStruct((T, D_MODEL), F32)], [_tok(D_MODEL)])
    return y, (x, h, proj, o_dil, lse, o_sb, u, pd, ps)


def _mixer_bwd(dxo, gain, W, l, tabs, saved):
    x, h, proj, o_dil, lse, o_sb, u, pd, ps = saved
    T = x.shape[0]
    win, wpd, wps, wo = W["w_in"], W["w_proj_dil"], W["w_proj_sb"], W["w_out"]
    cin = win.shape[3]
    cp = wpd.shape[3]
    tk = TM
    tm = TM // 2
    row = pl.BlockSpec((tm, D_MODEL), lambda i, j, k: (i, 0))

    def dgated(vals, ins, outs, i):
        du = vals[0]
        sd = _sigmoid(ins[4][...])
        ss = _sigmoid(ins[5][...])
        outs[0][...] = (du * sd).astype(BF16)
        outs[1][...] = (du * ss).astype(BF16)
        outs[2][...] = (du * ins[2][...].astype(F32) * sd * (1.0 - sd)).astype(BF16)
        outs[3][...] = (du * ins[3][...].astype(F32) * ss * (1.0 - ss)).astype(BF16)

    wpd_n, wps_n, wo_n = _joined_mixer_weights(wpd, wps, wo)
    gd_spec = pl.BlockSpec((TM, D_MODEL), lambda i, j, k: (i, COL_GD // D_MODEL))
    gs_spec = pl.BlockSpec((TM, D_MODEL), lambda i, j, k: (i, COL_GS // D_MODEL))
    ush = jax.ShapeDtypeStruct((T, D_MODEL), BF16)
    dpd, dps, dgd, dgs = _mm(
        "mix_du", [dxo, wo_n, pd, ps, proj, proj],
        [_tok(D_MODEL), _whole(wo_n.shape), _tok(D_MODEL), _tok(D_MODEL), gd_spec, gs_spec],
        [(0, 1, 0)], 1, None, NT, (T // TM, 1, 1), dgated, [ush] * 4, [_tok(D_MODEL)] * 4)

    def one(vals, ins, outs, i):
        outs[0][...] = vals[0].astype(BF16)

    def two(vals, ins, outs, i):
        outs[0][...] = vals[0].astype(BF16)
        outs[1][...] = vals[1].astype(BF16)

    tok_k = pl.BlockSpec((tk, D_MODEL), lambda i, j, k: (k, 0))
    att_k = pl.BlockSpec((tk, D_ATT), lambda i, j, k: (k, 0))
    (dwo_n,) = _mm("mix_dwo", [u, dxo], [tok_k, tok_k], [(0, 1, 0)], 1, (D_MODEL, D_MODEL), TN, (1, 1, T // tk), one,
                   [jax.ShapeDtypeStruct((D_MODEL, D_MODEL), BF16)], [_whole((D_MODEL, D_MODEL))])

    def plain2(vals, ins, outs, i):
        outs[0][...] = vals[0]
        outs[1][...] = vals[1]

    ash = jax.ShapeDtypeStruct((T, D_ATT), F32)
    do_dil, do_sb = _mm(
        "mix_do", [dpd, dps, wpd_n, wps_n], [_tok(D_MODEL), _tok(D_MODEL), _whole(wpd_n.shape), _whole(wps_n.shape)],
        [(0, 2, 0), (1, 3, 1)], 2, None, NT, (T // TM, 1, 1), plain2, [ash, ash], [_tok(D_ATT)] * 2)

    psh = jax.ShapeDtypeStruct((D_ATT, D_MODEL), BF16)
    dwpd_n, dwps_n = _mm(
        "mix_dwp", [o_dil, o_sb, dpd, dps], [att_k, att_k, tok_k, tok_k], [(0, 2, 0), (1, 3, 1)], 2,
        (D_ATT, D_MODEL), TN, (1, 1, T // tk), two, [psh, psh], [_whole((D_ATT, D_MODEL))] * 2)
    dwpd, dwps = (w.reshape(D_ATT, N_CHIPS, cp).transpose(1, 0, 2) for w in (dwpd_n, dwps_n))
    dwo = dwo_n.reshape(N_CHIPS, cp, D_MODEL)

    dqs, dks, dvs = [], [], []
    for g, (window, dil) in enumerate(DIL_GROUPS):
        dq, dk, dv = _dil_bwd(proj, do_dil, o_dil, lse, g, dil)
        dqs.append(dq)
        dks.append(dk)
        dvs.append(dv)
    dq_s, dk_s, dv_s = _sb_bwd(proj, do_sb, o_sb)
    dproj = _assemble_dproj(dqs + dks, dvs + [dq_s, dk_s, dv_s], [dgd, dgs], tabs)

    dx, dgain = _mm(
        "mix_dx", [dproj, win, x, gain, dxo],
        [pl.BlockSpec((tm, N_CHIPS * cin), lambda i, j, k: (i, 0)), _wfull(D_MODEL, cin, l), row, _gain_spec(), row],
        [(_cols(0, c, cin), _pick(1, c), 0) for c in range(N_CHIPS)], 1, None, NT, (T // tm, 1, 1),
        _rms_bwd_epilogue(2, 3, 4),
        [jax.ShapeDtypeStruct((T, D_MODEL), F32), jax.ShapeDtypeStruct((8, D_MODEL), F32)],
        [row, pl.BlockSpec((8, D_MODEL), lambda i, j, k: (0, 0))])

    (dwin,) = _mm(
        "mix_dwin", [h, dproj],
        [pl.BlockSpec((tk, D_MODEL), lambda i, j, k: (k, 0)), pl.BlockSpec((tk, cin), lambda i, j, k: (k, j))],
        [(0, 1, 0)], 1, (D_MODEL, cin), TN, (1, N_CHIPS, T // tk), one,
        [jax.ShapeDtypeStruct((N_CHIPS, D_MODEL, cin), BF16)],
        [pl.BlockSpec((None, D_MODEL, cin), lambda i, j, k: (j, 0, 0))])
    return dx, dgain, dwin, dwpd, dwps, dwo


def _local_step(x, target, norms, norm_final, weights_of, on_grads):
    T = x.shape[0]
    tabs = _rope_tables(T)
    saved, held = [], []
    for l in range(DEPTH):
        w1 = weights_of(l, 0, x)
        x, s1 = _ffn_fwd(x, norms["norm_ffn1"][l:l + 1], w1["ffn1_w_gate"], w1["ffn1_w_up"], w1["ffn1_w_down"])
        w2 = weights_of(l, 1, x)
        x, s2 = _mixer_fwd(x, norms["norm_mix"][l:l + 1], w2, 0, tabs)
        w3 = weights_of(l, 2, x)
        x, s3 = _ffn_fwd(x, norms["norm_ffn2"][l:l + 1], w3["ffn2_w_gate"], w3["ffn2_w_up"], w3["ffn2_w_down"])
        saved.append((s1, s2, s3))
        held.append((w1, w2, w3))
    dx, dg_final, loss = _final_loss(x, norm_final.reshape(1, D_MODEL), target)
    gains = [None] * DEPTH
    for l in reversed(range(DEPTH)):
        s1, s2, s3 = saved[l]
        w1, w2, w3 = held[l]
        dx, dg2, dwg2, dwu2, dwd2 = _ffn_bwd(dx, norms["norm_ffn2"][l:l + 1], w3["ffn2_w_gate"], w3["ffn2_w_up"],
                                             w3["ffn2_w_down"], s3)
        dx = on_grads(l, 2, dict(ffn2_w_gate=dwg2, ffn2_w_up=dwu2, ffn2_w_down=dwd2), dx)
        dx, dgm, dwin, dwpd, dwps, dwo = _mixer_bwd(dx, norms["norm_mix"][l:l + 1], w2, 0, tabs, s2)
        dx = on_grads(l, 1, dict(w_in=dwin, w_proj_dil=dwpd, w_proj_sb=dwps, w_out=dwo), dx)
        dx, dg1, dwg1, dwu1, dwd1 = _ffn_bwd(dx, norms["norm_ffn1"][l:l + 1], w1["ffn1_w_gate"], w1["ffn1_w_up"],
                                             w1["ffn1_w_down"], s1)
        dx = on_grads(l, 0, dict(ffn1_w_gate=dwg1, ffn1_w_up=dwu1, ffn1_w_down=dwd1), dx)
        gains[l] = dict(norm_ffn1=dg1, norm_mix=dgm, norm_ffn2=dg2)
    return loss, dx, gains, dg_final


def _place():
    x, y, c = lax.axis_index("x"), lax.axis_index("y"), lax.axis_index("c")
    chips = [(1 - x, y), (x, 1 - y), (1 - x, 1 - y)]
    return x, y, c, chips


def _half(c, r):
    return pl.ds(pl.multiple_of(c * (r // 2), 8), r // 2)


def _cast_into_slot(ws, ls, me_arr, after):
    n = len(ws)
    late = [] if after is None else [after]

    def body(me_ref, *refs):
        for a in range(n):
            refs[len(refs) - n + a][...] = refs[a][...].astype(BF16)

    def src(w, l):
        return pl.BlockSpec((None, w.shape[1] // 4, w.shape[2]), lambda i, me: (l, i, 0))

    def dst(w):
        return pl.BlockSpec((None, None, w.shape[1] // 4, w.shape[2]), lambda i, me: (me[0], 0, i, 0))

    return pl.pallas_call(
        body, name="cast_weights",
        grid_spec=pltpu.PrefetchScalarGridSpec(
            num_scalar_prefetch=1, grid=(4,),
            in_specs=[src(w, l) for w, l in zip(ws, ls)] + [pl.BlockSpec(memory_space=pl.ANY)] * len(late),
            out_specs=[dst(w) for w in ws]),
        out_shape=[jax.ShapeDtypeStruct((N_CHIPS, 1) + w.shape[1:], BF16) for w in ws], compiler_params=_params(),
    )(me_arr, *ws, *late)


HBM_SPEC = pl.BlockSpec(memory_space=pltpu.HBM)
SEM_SPEC = pl.BlockSpec(memory_space=pltpu.SEMAPHORE)
SPLIT_COPY = pltpu.CompilerParams(has_side_effects=pltpu.SideEffectType.DATAFLOW_SIDE_EFFECTING)


def _gather_piece(ref, chip_id, c):
    return ref.at[chip_id, 0, _half(c, ref.shape[2]), :]


def _gather_start(tag, bufs):
    n = len(bufs)

    def body(*refs):
        out_refs = refs[n:2 * n]
        send_sems, recv_sems, token = refs[2 * n:]
        x, y, c, chips = _place()
        me = 2 * x + y
        for a in range(n):
            piece = _gather_piece(out_refs[a], me, c)
            for j, chip in enumerate(chips):
                pltpu.make_async_remote_copy(
                    src_ref=piece, dst_ref=piece, send_sem=send_sems.at[3 * a + j], recv_sem=recv_sems.at[3 * a + j],
                    device_id=(*chip, c), device_id_type=MESH).start()
        token[...] = jnp.zeros_like(token)

    outs = pl.pallas_call(
        body, name=f"gather_start_{tag}", in_specs=[HBM_SPEC] * n,
        out_specs=[HBM_SPEC] * n + [SEM_SPEC, SEM_SPEC, pl.BlockSpec(memory_space=pltpu.VMEM)],
        out_shape=[pltpu.HBM(b.shape, b.dtype) for b in bufs] + [pltpu.SemaphoreType.DMA((3 * n,))] * 2
        + [jax.ShapeDtypeStruct((8, 128), F32)],
        input_output_aliases={a: a for a in range(n)}, compiler_params=SPLIT_COPY,
    )(*[pltpu.with_memory_space_constraint(b, pltpu.HBM) for b in bufs])
    return outs[:n], outs[n], outs[n + 1], outs[n + 2]


def _gather_wait(k, bufs, places, send_sems, recv_sems, after):
    m = len(bufs)

    def body(*refs):
        in_refs = refs[:m]
        ssem, rsem = refs[m], refs[m + 1]
        x, y, c, chips = _place()
        me = 2 * x + y
        for t, a in enumerate(places):
            for j, chip in enumerate(chips):
                cp = pltpu.make_async_remote_copy(
                    src_ref=_gather_piece(in_refs[t], me, c),
                    dst_ref=_gather_piece(in_refs[t], 2 * chip[0] + chip[1], c),
                    send_sem=ssem.at[3 * a + j], recv_sem=rsem.at[3 * a + j], device_id=(*chip, c),
                    device_id_type=MESH)
                cp.wait_send()
                cp.wait_recv()

    return pl.pallas_call(
        body, name=f"gather_wait_{k}",
        in_specs=[HBM_SPEC] * m + [SEM_SPEC, SEM_SPEC, pl.BlockSpec(memory_space=pl.ANY)], out_specs=[HBM_SPEC] * m,
        out_shape=[pltpu.HBM(b.shape, b.dtype) for b in bufs], input_output_aliases={t: t for t in range(m)},
        compiler_params=SPLIT_COPY,
    )(*bufs, send_sems, recv_sems, after)


def _gather_relay(bufs):
    n = len(bufs)

    def body(*refs):
        out_refs = refs[n:2 * n]
        send_sems, recv_sems = refs[2 * n:]
        x, y, c, chips = _place()
        cps = []
        for a in range(n):
            for j, chip in enumerate(chips):
                piece = _gather_piece(out_refs[a], 2 * chip[0] + chip[1], c)
                cps.append(pltpu.make_async_remote_copy(
                    src_ref=piece, dst_ref=piece, send_sem=send_sems.at[a, j], recv_sem=recv_sems.at[a, j],
                    device_id=(x, y, 1 - c), device_id_type=MESH))
        for cp in cps:
            cp.start()
        for a in range(n):
            for j, chip in enumerate(chips):
                theirs = _gather_piece(out_refs[a], 2 * chip[0] + chip[1], 1 - c)
                pltpu.make_async_remote_copy(
                    src_ref=theirs, dst_ref=theirs, send_sem=send_sems.at[a, j], recv_sem=recv_sems.at[a, j],
                    device_id=(x, y, 1 - c), device_id_type=MESH).wait_recv()
        for cp in cps:
            cp.wait_send()

    any_spec = pl.BlockSpec(memory_space=pl.ANY)
    return pl.pallas_call(
        body, name="gather_relay", in_specs=[any_spec] * n, out_specs=[any_spec] * n,
        out_shape=[jax.ShapeDtypeStruct(b.shape, b.dtype) for b in bufs],
        input_output_aliases={a: a for a in range(n)},
        scratch_shapes=[pltpu.SemaphoreType.DMA((n, 3))] * 2,
    )(*bufs)


def _other_half(ref, c):
    return ref.at[:, _half(1 - c, ref.shape[1]), :]


def _all_of(ref, c):
    return ref


def _sibling_start(name, srcs, pick, land_shapes, thru):
    n = len(srcs)
    lands = [lax.empty(sh, s.dtype) for sh, s in zip(land_shapes, srcs)]
    kept = lands + ([] if thru is None else [thru])
    m = len(kept)

    def body(*refs):
        s_refs, land_refs = refs[:n], refs[n + m:n + m + n]
        send_sems, recv_sems, token = refs[n + 2 * m:]
        x, y, c, _ = _place()
        for a in range(n):
            pltpu.make_async_remote_copy(
                src_ref=pick(s_refs[a], c), dst_ref=land_refs[a], send_sem=send_sems.at[a],
                recv_sem=recv_sems.at[a], device_id=(x, y, 1 - c), device_id_type=MESH).start()
        token[...] = jnp.zeros_like(token)

    outs = pl.pallas_call(
        body, name=name, in_specs=[HBM_SPEC] * (n + m),
        out_specs=[HBM_SPEC] * m + [SEM_SPEC, SEM_SPEC, pl.BlockSpec(memory_space=pltpu.VMEM)],
        out_shape=[pltpu.HBM(v.shape, v.dtype) for v in kept] + [pltpu.SemaphoreType.DMA((n,))] * 2
        + [jax.ShapeDtypeStruct((8, 128), F32)],
        input_output_aliases={n + a: a for a in range(m)}, compiler_params=SPLIT_COPY,
    )(*[pltpu.with_memory_space_constraint(v, pltpu.HBM) for v in list(srcs) + kept])
    return (outs[:n], outs[m], outs[m + 1]), (outs[n] if thru is not None else None), outs[m + 2]


def _sibling_wait(name, srcs, pick, lands, send_sems, recv_sems, after):
    n = len(srcs)

    def body(*refs):
        s_refs, land_refs = refs[:n], refs[n:2 * n]
        ssem, rsem = refs[2 * n], refs[2 * n + 1]
        x, y, c, _ = _place()
        for a in range(n):
            cp = pltpu.make_async_remote_copy(
                src_ref=pick(s_refs[a], c), dst_ref=land_refs[a], send_sem=ssem.at[a], recv_sem=rsem.at[a],
                device_id=(x, y, 1 - c), device_id_type=MESH)
            cp.wait_send()
            cp.wait_recv()

    return pl.pallas_call(
        body, name=name, in_specs=[HBM_SPEC] * (2 * n) + [SEM_SPEC, SEM_SPEC, pl.BlockSpec(memory_space=pl.ANY)],
        out_specs=[HBM_SPEC] * n, out_shape=[pltpu.HBM(v.shape, v.dtype) for v in lands],
        input_output_aliases={n + a: a for a in range(n)}, compiler_params=SPLIT_COPY,
    )(*srcs, *lands, send_sems, recv_sems, after)


def _add_half(gs, gots, c_arr):
    n = len(gs)

    def body(c_ref, *refs):
        for a in range(n):
            refs[2 * n + a][...] = (refs[a][...].astype(F32) + refs[n + a][...].astype(F32)).astype(BF16)

    def own(g):
        return pl.BlockSpec((None, g.shape[1] // 2, g.shape[2]), lambda k, cr: (k, cr[0], 0))

    def half(g):
        return pl.BlockSpec((None, g.shape[1] // 2, g.shape[2]), lambda k, cr: (k, 0, 0))

    return pl.pallas_call(
        body, name="grad_add_half",
        grid_spec=pltpu.PrefetchScalarGridSpec(
            num_scalar_prefetch=1, grid=(N_CHIPS,),
            in_specs=[own(g) for g in gs] + [half(g) for g in gs], out_specs=[half(g) for g in gs]),
        out_shape=[jax.ShapeDtypeStruct(got.shape, BF16) for got in gots], compiler_params=_params(),
    )(c_arr, *gs, *gots)


def _scatter_start(k, ss, thru):
    n = len(ss)

    def body(*refs):
        s_refs, land_refs = refs[2 * n + 1:3 * n + 1], refs[3 * n + 1:4 * n + 1]
        send_sems, recv_sems = refs[4 * n + 2:]
        x, y, c, chips = _place()
        me = 2 * x + y
        for a in range(n):
            for j, chip in enumerate(chips):
                pltpu.make_async_remote_copy(
                    src_ref=s_refs[a].at[2 * chip[0] + chip[1]], dst_ref=land_refs[a].at[me],
                    send_sem=send_sems.at[3 * a + j], recv_sem=recv_sems.at[3 * a + j], device_id=(*chip, c),
                    device_id_type=MESH).start()

    lands = [lax.empty(s.shape, s.dtype) for s in ss]
    hbm = [pltpu.HBM(s.shape, s.dtype) for s in ss]
    outs = pl.pallas_call(
        body, name=f"grad_scatter_start_{k}", in_specs=[HBM_SPEC] * (2 * n + 1),
        out_specs=[HBM_SPEC] * (2 * n + 1) + [SEM_SPEC, SEM_SPEC],
        out_shape=hbm + hbm + [pltpu.HBM(thru.shape, thru.dtype)] + [pltpu.SemaphoreType.DMA((3 * n,))] * 2,
        input_output_aliases={a: a for a in range(2 * n + 1)}, compiler_params=SPLIT_COPY,
    )(*[pltpu.with_memory_space_constraint(v, pltpu.HBM) for v in list(ss) + lands + [thru]])
    return (outs[:n], outs[n:2 * n], outs[2 * n + 1], outs[2 * n + 2]), outs[2 * n]


def _scatter_wait(k, ss, lands, send_sems, recv_sems, after):
    n = len(ss)

    def body(*refs):
        s_refs, land_refs = refs[:n], refs[n:2 * n]
        ssem, rsem = refs[2 * n], refs[2 * n + 1]
        x, y, c, chips = _place()
        me = 2 * x + y
        for a in range(n):
            for j, chip in enumerate(chips):
                cid = 2 * chip[0] + chip[1]
                cp = pltpu.make_async_remote_copy(
                    src_ref=s_refs[a].at[cid], dst_ref=land_refs[a].at[cid], send_sem=ssem.at[3 * a + j],
                    recv_sem=rsem.at[3 * a + j], device_id=(*chip, c), device_id_type=MESH)
                cp.wait_send()
                cp.wait_recv()

    hbm = [pltpu.HBM(s.shape, s.dtype) for s in ss]
    outs = pl.pallas_call(
        body, name=f"grad_scatter_wait_{k}",
        in_specs=[HBM_SPEC] * (2 * n) + [SEM_SPEC, SEM_SPEC, pl.BlockSpec(memory_space=pl.ANY)],
        out_specs=[HBM_SPEC] * (2 * n), out_shape=hbm + hbm,
        input_output_aliases={a: a for a in range(2 * n)}, compiler_params=SPLIT_COPY,
    )(*ss, *lands, send_sems, recv_sems, after)
    return outs[:n], outs[n:]


def _sum_chips(lands, ss, me_arr):
    n = len(lands)

    def body(me_ref, *refs):
        for own in range(N_CHIPS):
            @pl.when(me_ref[0] == own)
            def _(own=own):
                for a in range(n):
                    acc = None
                    for k in range(N_CHIPS):
                        term = (refs[n + a][...] if k == own else refs[a][k]).astype(F32)
                        acc = term if acc is None else acc + term
                    refs[2 * n + a][...] = acc

    return pl.pallas_call(
        body, name="grad_sum_chips",
        grid_spec=pltpu.PrefetchScalarGridSpec(
            num_scalar_prefetch=1, grid=(1,),
            in_specs=[pl.BlockSpec(la.shape, lambda i, me: (0, 0, 0)) for la in lands]
            + [pl.BlockSpec((None,) + la.shape[1:], lambda i, me: (me[0], 0, 0)) for la in lands],
            out_specs=[pl.BlockSpec(la.shape[1:], lambda i, me: (0, 0)) for la in lands]),
        out_shape=[jax.ShapeDtypeStruct(la.shape[1:], F32) for la in lands], compiler_params=_params(),
    )(me_arr, *lands, *ss)


def _allreduce_rows(stats):
    def body(s_ref, o_ref, buf, send_sems, recv_sems):
        x, y, c, _ = _place()
        me = 4 * x + 2 * y + c
        buf[me] = s_ref[...]
        cps = []
        for k in range(1, 8):
            px = jnp.where(k & 4, 1 - x, x)
            py = jnp.where(k & 2, 1 - y, y)
            pc = jnp.where(k & 1, 1 - c, c)
            cps.append(pltpu.make_async_remote_copy(
                src_ref=s_ref, dst_ref=buf.at[me], send_sem=send_sems.at[k - 1], recv_sem=recv_sems.at[k - 1],
                device_id=(px, py, pc), device_id_type=MESH))
        for cp in cps:
            cp.start()
        for cp in cps:
            cp.wait()
        acc = buf[0]
        for d in range(1, 8):
            acc = acc + buf[d]
        o_ref[...] = acc

    vm = pl.BlockSpec(memory_space=pltpu.VMEM)
    return pl.pallas_call(
        body, name="allreduce_rows", in_specs=[vm], out_specs=vm,
        out_shape=jax.ShapeDtypeStruct(stats.shape, F32),
        scratch_shapes=[pltpu.VMEM((8,) + stats.shape, F32), pltpu.SemaphoreType.DMA((7,)),
                        pltpu.SemaphoreType.DMA((7,))],
    )(stats)


def _adamw_math(w, g, m, v):
    m = ADAM_B1 * m + (1.0 - ADAM_B1) * g
    v = ADAM_B2 * v + (1.0 - ADAM_B2) * (g * g)
    m_hat = m / (1.0 - ADAM_B1 ** ADAM_STEP)
    v_hat = v / (1.0 - ADAM_B2 ** ADAM_STEP)
    delta = -ADAM_LR * (m_hat / (jnp.sqrt(v_hat) + ADAM_EPS) + ADAM_WD * w)
    return delta, m, v


def _adamw(ws, ms, vs, mines, theirs, l, c_arr, earlier, after):
    n = len(ws)
    held = [t for e in earlier if e is not None for t in e]
    assert len(held) in (0, 4 * n)
    late = [] if after is None else [after]

    def body(c_ref, *refs):
        outs = refs[len(refs) - 4 * n:]
        for a in range(n):
            w_ref, m_ref, v_ref, a_ref, b_ref = refs[5 * a:5 * a + 5]
            g = jnp.where(pl.program_id(0) == c_ref[0], a_ref[...], b_ref[...])
            delta, mn, vn = _adamw_math(w_ref[...], g, m_ref[...], v_ref[...])
            outs[4 * a][...] = g
            outs[4 * a + 1][...] = delta
            outs[4 * a + 2][...] = mn
            outs[4 * a + 3][...] = vn

    def blk(w):
        tr = w.shape[1] // 4
        return pl.BlockSpec((None, tr, w.shape[2]), lambda hh, i, cr: (l, 2 * hh + i, 0))

    def half(w):
        return pl.BlockSpec((w.shape[1] // 4, w.shape[2]), lambda hh, i, cr: (i, 0))

    outs = pl.pallas_call(
        body, name="adamw",
        grid_spec=pltpu.PrefetchScalarGridSpec(
            num_scalar_prefetch=1, grid=(2, 2),
            in_specs=[sp for w in ws for sp in (blk(w), blk(w), blk(w), half(w), half(w))]
            + [pl.BlockSpec(memory_space=pl.ANY)] * (len(held) + len(late)),
            out_specs=[blk(w) for w in ws for _ in range(4)]),
        out_shape=[jax.ShapeDtypeStruct(w.shape, F32) for w in ws for _ in range(4)],
        input_output_aliases={1 + 5 * n + t: t for t in range(len(held))}, compiler_params=_params(),
    )(c_arr, *[t for grp in zip(ws, ms, vs, mines, theirs) for t in grp], *held, *late)
    return [outs[4 * a:4 * a + 4] for a in range(n)]


def _adamw_rows(w, m, v, g):
    def body(w_ref, m_ref, v_ref, g_ref, d_ref, mo_ref, vo_ref):
        delta, mn, vn = _adamw_math(w_ref[...], g_ref[...], m_ref[...], v_ref[...])
        d_ref[...] = delta
        mo_ref[...] = mn
        vo_ref[...] = vn

    vm = pl.BlockSpec(memory_space=pltpu.VMEM)
    sh = jax.ShapeDtypeStruct(w.shape, F32)
    return pl.pallas_call(body, name="adamw_rows", in_specs=[vm] * 4, out_specs=[vm] * 3, out_shape=[sh] * 3)(w, m, v, g)


SUBLAYERS = (("ffn1_w_gate", "ffn1_w_up", "ffn1_w_down"), ("w_in", "w_proj_dil", "w_proj_sb", "w_out"),
             ("ffn2_w_gate", "ffn2_w_up", "ffn2_w_down"))
TRANSPOSED = ("ffn1_w_gate", "ffn1_w_up", "ffn2_w_gate", "ffn2_w_up")


def _pick_row(blocks):
    row = lax.broadcasted_iota(jnp.int32, (8, D_MODEL), 0)
    out = jnp.zeros((8, D_MODEL), F32)
    for i, b in enumerate(blocks):
        out = out + jnp.where(row == i, b, 0.0)
    return out


def kernel(x, norm_ffn1, ffn1_w_gate, ffn1_w_up, ffn1_w_down, norm_mix, w_in, w_proj_dil, w_proj_sb, w_out, norm_ffn2, ffn2_w_gate, ffn2_w_up, ffn2_w_down, norm_final, loss_target, m_norm_ffn1, m_ffn1_w_gate, m_ffn1_w_up, m_ffn1_w_down, m_norm_mix, m_w_in, m_w_proj_dil, m_w_proj_sb, m_w_out, m_norm_ffn2, m_ffn2_w_gate, m_ffn2_w_up, m_ffn2_w_down, m_norm_final, v_norm_ffn1, v_ffn1_w_gate, v_ffn1_w_up, v_ffn1_w_down, v_norm_mix, v_w_in, v_w_proj_dil, v_w_proj_sb, v_w_out, v_norm_ffn2, v_ffn2_w_gate, v_ffn2_w_up, v_ffn2_w_down, v_norm_final):
    given = dict(locals())
    for n in TRANSPOSED:
        for k in ("", "m_", "v_"):
            given[k + n] = jnp.swapaxes(given[k + n], 1, 2)
    weights = {n: given[n] for n in WEIGHT_NAMES}
    norms = {n: given[n] for n in NORM_NAMES}

    c_arr = lax.axis_index("c").astype(jnp.int32).reshape(1)
    me_arr = (2 * lax.axis_index("x") + lax.axis_index("y")).astype(jnp.int32).reshape(1)
    order = [(l, s, n) for l in range(DEPTH) for s in range(len(SUBLAYERS)) for n in SUBLAYERS[s]]
    n_first = len(SUBLAYERS[0])
    sent, token = {}, None
    for tag, idxs in (("a", range(n_first)), ("b", range(n_first, len(order)))):
        cast = _cast_into_slot([weights[order[i][2]] for i in idxs], [order[i][0] for i in idxs], me_arr, token)
        bufs, send_sems, recv_sems, token = _gather_start(tag, cast)
        for p, i in enumerate(idxs):
            sent[i] = (bufs[p], p, send_sems, recv_sems)

    def weights_of(l, s, after):
        idxs = [i for i, (ll, ss, _) in enumerate(order) if (ll, ss) == (l, s)]
        got = _gather_wait(len(SUBLAYERS) * l + s, [sent[i][0] for i in idxs], [sent[i][1] for i in idxs],
                           sent[idxs[0]][2], sent[idxs[0]][3], after)
        return {order[i][2]: g for i, g in zip(idxs, _gather_relay(got))}

    out = {}
    to_add, in_flight = [], []

    def add_and_scatter(after):
        l, s, names, gs, lands, ssem, rsem = to_add.pop(0)
        k = len(SUBLAYERS) * l + s
        got = _sibling_wait(f"grad_exchange_wait_{k}", gs, _other_half, lands, ssem, rsem, after)
        sent, after = _scatter_start(k, _add_half(gs, got, c_arr), after)
        in_flight.append((l, s, names) + sent)
        return after

    def on_grads(l, s, grads, after):
        names = list(grads)
        k = len(SUBLAYERS) * l + s
        gs = [grads[n] for n in names]
        sent, after, _ = _sibling_start(f"grad_exchange_start_{k}", gs, _other_half,
                                        [(g.shape[0], g.shape[1] // 2, g.shape[2]) for g in gs], after)
        if to_add:
            after = add_and_scatter(after)
        to_add.append((l, s, names, gs) + sent)
        return after

    loss_blk, grad_x, gains, dg_final = _local_step(x[0], loss_target[0], norms, norm_final, weights_of, on_grads)
    grad_x = add_and_scatter(grad_x)

    def update(l, names, mine, swap, after):
        theirs = _sibling_wait(f"grad_swap_wait_{l}_{names[0]}", mine, _all_of, *swap, grad_x if after is None else after)
        res = _adamw([weights[n] for n in names], [given["m_" + n] for n in names], [given["v_" + n] for n in names],
                     mine, theirs, l, c_arr, [out.get(n) for n in names], after)
        out.update(zip(names, res))

    waiting = None
    for l, s, names, sums, lands, ssem, rsem in in_flight:
        sums, lands = _scatter_wait(len(SUBLAYERS) * l + s, sums, lands, ssem, rsem, grad_x)
        mine = _sum_chips(lands, sums, me_arr)
        swap, _, token = _sibling_start(f"grad_swap_start_{l}_{names[0]}", mine, _all_of,
                                        [m.shape for m in mine], None)
        if waiting is not None:
            update(*waiting, token)
        waiting = (l, names, mine, swap)
    update(*waiting, None)
    out = {k + n: (jnp.swapaxes(v, 1, 2) if n in TRANSPOSED else v)
           for n, res in out.items() for k, v in zip(("grad_", "delta_", "new_m_", "new_v_"), res)}
    out["grad_x"] = grad_x[None]

    rows = [gains[l][n] for n in NORM_NAMES for l in range(DEPTH)] + [dg_final, loss_blk]
    total = _allreduce_rows(_pick_row(rows))
    out["loss"] = total[7, 0]
    wn = jnp.concatenate([given[n] for n in NORM_NAMES] + [norm_final[None], jnp.zeros((1, D_MODEL), F32)])
    mn_ = jnp.concatenate([given["m_" + n] for n in NORM_NAMES] + [m_norm_final[None], jnp.zeros((1, D_MODEL), F32)])
    vn_ = jnp.concatenate([given["v_" + n] for n in NORM_NAMES] + [v_norm_final[None], jnp.ones((1, D_MODEL), F32)])
    d_n, m_n, v_n = _adamw_rows(wn, mn_, vn_, total)
    for i, n in enumerate(NORM_NAMES):
        sl = slice(i * DEPTH, (i + 1) * DEPTH)
        out["grad_" + n], out["delta_" + n], out["new_m_" + n], out["new_v_" + n] = total[sl], d_n[sl], m_n[sl], v_n[sl]
    out["grad_norm_final"], out["delta_norm_final"] = total[6], d_n[6]
    out["new_m_norm_final"], out["new_v_norm_final"] = m_n[6], v_n[6]

    names = ["norm_ffn1", "ffn1_w_gate", "ffn1_w_up", "ffn1_w_down", "norm_mix", "w_in", "w_proj_dil", "w_proj_sb",
             "w_out", "norm_ffn2", "ffn2_w_gate", "ffn2_w_up", "ffn2_w_down", "norm_final"]
    return (out["loss"], out["grad_x"], *[out["grad_" + n] for n in names], *[out["delta_" + n] for n in names],
            *[out["new_m_" + n] for n in names], *[out["new_v_" + n] for n in names])
```

```python
import functools

import jax
import jax.numpy as jnp
from jax import lax
from jax.experimental import pallas as pl
from jax.experimental.pallas import tpu as pltpu

F32 = jnp.float32
BF16 = jnp.bfloat16

D_MODEL = 1024
DEPTH = 2
N_CHIPS = 4
HEAD_DIM = 64
ROPE_DIM = 16
ROPE_THETA = 500000.0
DIL_GROUPS = ((128, 1), (512, 4), (2048, 16))
SPAN = 128
Q_BLOCK = 128
RMS_EPS = 1e-6
D_ATT = 256
COL_QS = 2304
COL_GD = 3072
COL_GS = 4096
ADAM_LR, ADAM_B1, ADAM_B2, ADAM_EPS, ADAM_WD, ADAM_STEP = 0.001, 0.9, 0.999, 1e-08, 0.01, 10

VMEM_LIMIT = 52 * 1024 * 1024
TM = 512
NEG = -1e30

NN = (((1,), (0,)), ((), ()))
NT = (((1,), (1,)), ((), ()))
TN = (((0,), (0,)), ((), ()))
MESH = pl.DeviceIdType.MESH

WEIGHT_NAMES = ("ffn1_w_gate", "ffn1_w_up", "ffn1_w_down", "w_in", "w_proj_dil",
                "w_proj_sb", "w_out", "ffn2_w_gate", "ffn2_w_up", "ffn2_w_down")
NORM_NAMES = ("norm_ffn1", "norm_mix", "norm_ffn2")


def _params(**kw):
    return pltpu.CompilerParams(vmem_limit_bytes=VMEM_LIMIT, **kw)


def _sigmoid(x):
    return 0.5 * jnp.tanh(0.5 * x) + 0.5


def _mm_body(pairs, n_in, n_out, n_acc, dims, nk, epilogue, *refs):
    ins = refs[:n_in]
    outs = refs[n_in:n_in + n_out]
    accs = refs[n_in + n_out:]
    i = pl.program_id(0)
    k = pl.program_id(2)

    def operand(a):
        return (a(ins) if callable(a) else ins[a][...]).astype(BF16)

    def dot(ia, ib):
        return lax.dot_general(operand(ia), operand(ib), dims, preferred_element_type=F32)

    if nk == 1:
        parts = [None] * n_acc
        for ia, ib, ic in pairs:
            parts[ic] = dot(ia, ib) if parts[ic] is None else parts[ic] + dot(ia, ib)
        epilogue(parts, ins, outs, i)
        return

    @pl.when(k == 0)
    def _():
        for c in range(n_acc):
            accs[c][...] = jnp.zeros_like(accs[c])

    for ia, ib, ic in pairs:
        accs[ic][...] += dot(ia, ib)

    @pl.when(k == nk - 1)
    def _():
        epilogue([a[...] for a in accs], ins, outs, i)


def _mm(name, ins, in_specs, pairs, n_acc, acc_shape, dims, grid, epilogue, out_shapes, out_specs):
    nk = grid[2]
    scratch = [pltpu.VMEM(acc_shape, F32) for _ in range(n_acc)] if nk > 1 else []
    body = functools.partial(_mm_body, tuple(pairs), len(ins), len(out_shapes), n_acc, dims, nk, epilogue)
    return pl.pallas_call(
        body, name=name, grid=grid, in_specs=in_specs, out_specs=out_specs, out_shape=out_shapes,
        scratch_shapes=scratch,
        compiler_params=_params(dimension_semantics=("arbitrary", "arbitrary", "arbitrary")),
    )(*ins)


def _rms_bwd_epilogue(x_idx, g_idx, dxo_idx):
    def ep(vals, ins, outs, i):
        dh = vals[0]
        x = ins[x_idx][...]
        g = ins[g_idx][...]
        rstd = lax.rsqrt(jnp.mean(x * x, axis=-1, keepdims=True) + RMS_EPS)
        xhat = x * rstd
        dxhat = dh * g
        dx = rstd * (dxhat - xhat * jnp.mean(dxhat * xhat, axis=-1, keepdims=True))
        outs[0][...] = ins[dxo_idx][...] + dx
        dg = jnp.broadcast_to(jnp.sum(dh * xhat, axis=0, keepdims=True), outs[1].shape)

        @pl.when(i == 0)
        def _():
            outs[1][...] = dg

        @pl.when(i > 0)
        def _():
            outs[1][...] += dg
    return ep


def _normed(x_idx, g_idx):
    seen = {}

    def f(ins):
        if id(ins) not in seen:
            xv = ins[x_idx][...]
            h = xv * lax.rsqrt(jnp.mean(xv * xv, axis=-1, keepdims=True) + RMS_EPS)
            seen[id(ins)] = (ins, (h * ins[g_idx][...]).astype(BF16))
        return seen[id(ins)][1]
    return f


def _rope_tables(T):
    pos = jnp.arange(T, dtype=F32)
    inv_freq = ROPE_THETA ** (-jnp.arange(0, ROPE_DIM, 2, dtype=F32) / ROPE_DIM)
    ang = pos[:, None] * inv_freq[None, :]
    cos, sin = jnp.cos(ang), jnp.sin(ang)
    half = ROPE_DIM // 2
    one = jnp.ones((T, HEAD_DIM - ROPE_DIM), F32)
    zero = jnp.zeros((T, HEAD_DIM - ROPE_DIM), F32)
    zh = jnp.zeros((T, half), F32)
    c = jnp.concatenate([cos, cos, one], axis=1)
    s1 = jnp.concatenate([-sin, zh, zero], axis=1)
    s2 = jnp.concatenate([zh, sin, zero], axis=1)
    return tuple(jnp.concatenate([t, t], axis=1) for t in (c, s1, s2))


def _rope_fwd(xv, c, s1, s2):
    w = xv.shape[1]
    half = ROPE_DIM // 2
    return xv * c + pltpu.roll(xv, w - half, 1) * s1 + pltpu.roll(xv, half, 1) * s2


def _rope_bwd(dy, c, s1, s2):
    w = dy.shape[1]
    half = ROPE_DIM // 2
    return dy * c + pltpu.roll(dy * s1, half, 1) + pltpu.roll(dy * s2, w - half, 1)


def _assemble_dproj(dqk, rest, gates, tabs):
    T = gates[0].shape[0]
    n_qk, n_rest = len(dqk), len(rest)
    width = (n_qk + n_rest) * D_ATT + 2 * D_MODEL

    def body(*refs):
        ins, (c_ref, s1_ref, s2_ref), o_ref = refs[:n_qk + n_rest + 2], refs[-4:-1], refs[-1]
        c = jnp.concatenate([c_ref[...]] * 2, axis=1)
        s1 = jnp.concatenate([s1_ref[...]] * 2, axis=1)
        s2 = jnp.concatenate([s2_ref[...]] * 2, axis=1)
        for b in range(n_qk + n_rest):
            v = ins[b][...]
            if b < n_qk:
                v = _rope_bwd(v, c, s1, s2)
            o_ref[:, b * D_ATT:(b + 1) * D_ATT] = v.astype(BF16)
        off = (n_qk + n_rest) * D_ATT
        o_ref[:, off:off + D_MODEL] = ins[-2][...]
        o_ref[:, off + D_MODEL:] = ins[-1][...]

    att = pl.BlockSpec((TM, D_ATT), lambda i: (i, 0))
    wide = pl.BlockSpec((TM, D_MODEL), lambda i: (i, 0))
    tab = pl.BlockSpec((TM, 128), lambda i: (i, 0))
    return pl.pallas_call(
        body, name="assemble_dproj", grid=(T // TM,),
        in_specs=[att] * (n_qk + n_rest) + [wide, wide, tab, tab, tab],
        out_specs=pl.BlockSpec((TM, width), lambda i: (i, 0)),
        out_shape=jax.ShapeDtypeStruct((T, width), BF16), compiler_params=_params(),
    )(*dqk, *rest, *gates, *tabs)


def _dil_merge(os_, lses):
    T = os_[0].shape[0]

    def body(o0, o1, o2, l0, l1, l2, o_ref, lse_ref):
        a, b, c = l0[...], l1[...], l2[...]
        m = jnp.maximum(jnp.maximum(a, b), c)
        ea, eb, ec = jnp.exp(a - m), jnp.exp(b - m), jnp.exp(c - m)
        den = ea + eb + ec
        o_ref[...] = (ea * o0[...] + eb * o1[...] + ec * o2[...]) / den
        lse_ref[...] = m + jnp.log(den)

    blk = pl.BlockSpec((TM, D_ATT), lambda i: (i, 0))
    sh = jax.ShapeDtypeStruct((T, D_ATT), F32)
    return pl.pallas_call(
        body, name="dil_merge", grid=(T // TM,), in_specs=[blk] * 6, out_specs=[blk, blk],
        out_shape=[sh, sh], compiler_params=_params(),
    )(*os_, *lses)


def _final_loss(x, gain, target):
    T = x.shape[0]

    def body(x_ref, g_ref, t_ref, dx_ref, dg_ref, loss_ref):
        xv = x_ref[...]
        g = g_ref[...]
        rstd = lax.rsqrt(jnp.mean(xv * xv, axis=-1, keepdims=True) + RMS_EPS)
        xhat = xv * rstd
        err = xhat * g - t_ref[...]
        loss = 0.5 * jnp.sum(jnp.mean(err * err, axis=-1, keepdims=True), axis=0, keepdims=True)
        dy = err * (1.0 / D_MODEL)
        dxhat = dy * g
        dx_ref[...] = rstd * (dxhat - xhat * jnp.mean(dxhat * xhat, axis=-1, keepdims=True))
        dg = jnp.broadcast_to(jnp.sum(dy * xhat, axis=0, keepdims=True), dg_ref.shape)
        ls = jnp.broadcast_to(loss, loss_ref.shape)

        @pl.when(pl.program_id(0) == 0)
        def _():
            dg_ref[...] = dg
            loss_ref[...] = ls

        @pl.when(pl.program_id(0) > 0)
        def _():
            dg_ref[...] += dg
            loss_ref[...] += ls

    blk = pl.BlockSpec((TM, D_MODEL), lambda i: (i, 0))
    row = pl.BlockSpec((1, D_MODEL), lambda i: (0, 0))
    acc = pl.BlockSpec((8, D_MODEL), lambda i: (0, 0))
    return pl.pallas_call(
        body, name="final_loss", grid=(T // TM,), in_specs=[blk, row, blk], out_specs=[blk, acc, acc],
        out_shape=[jax.ShapeDtypeStruct((T, D_MODEL), F32), jax.ShapeDtypeStruct((8, D_MODEL), F32),
                   jax.ShapeDtypeStruct((8, D_MODEL), F32)],
        compiler_params=_params(dimension_semantics=("arbitrary",)),
    )(x, gain, target)


def _pair_masks():
    lane = lax.broadcasted_iota(jnp.int32, (SPAN, 128), 1)
    return [lane < HEAD_DIM, lane >= HEAD_DIM]


def _stack_heads(x, masks):
    return jnp.concatenate([jnp.where(m, x, 0.0) for m in masks], axis=0)


def _unstack_heads(y, masks):
    rows = y.shape[0] // len(masks)
    out = jnp.where(masks[0], y[:rows], 0.0)
    for h in range(1, len(masks)):
        out = out + jnp.where(masks[h], y[rows * h:rows * (h + 1)], 0.0)
    return out


DIL_PAIR = 2


def _dil_rows(idx, d):
    u = idx // d
    r = idx - u * d
    own = pl.ds(u * (SPAN * d) + r, SPAN, stride=d) if d > 1 else pl.ds(pl.multiple_of(u * SPAN, SPAN), SPAN)
    up = jnp.maximum(u - 1, 0)
    prev = pl.ds(up * (SPAN * d) + r, SPAN, stride=d) if d > 1 else pl.ds(pl.multiple_of(up * SPAN, SPAN), SPAN)
    return u, own, prev


def _dil_valid(u):
    qi = lax.broadcasted_iota(jnp.int32, (2 * SPAN, 2 * SPAN), 0) & (SPAN - 1)
    kj = lax.broadcasted_iota(jnp.int32, (2 * SPAN, 2 * SPAN), 1)
    in_prev = (kj < SPAN) & (kj >= qi + jnp.where(u > 0, 0, SPAN))
    return in_prev | ((kj >= SPAN) & (kj - SPAN <= qi))


def _dil_keys(ref, own, prev):
    return jnp.concatenate([ref[prev, :], ref[own, :]], axis=0).astype(BF16)


def _dil_fwd(proj, g, d):
    T = proj.shape[0]
    n_iter = T // SPAN

    def body(q_ref, k_ref, v_ref, o_ref, lse_ref):
        masks = _pair_masks()

        def step(pair, carry):
            its = [_dil_rows(DIL_PAIR * pair + e, d) for e in range(DIL_PAIR)]
            qs = [_stack_heads(q_ref[own, :] * (HEAD_DIM ** -0.5), masks).astype(BF16) for _, own, _ in its]
            kks = [_dil_keys(k_ref, own, prev) for _, own, prev in its]
            vvs = [_dil_keys(v_ref, own, prev) for _, own, prev in its]
            ss = [jnp.where(_dil_valid(u), lax.dot_general(q, kk, NT, preferred_element_type=F32), NEG)
                  for (u, _, _), q, kk in zip(its, qs, kks)]
            ms = [jnp.max(s, axis=1, keepdims=True) for s in ss]
            ps = [jnp.exp(s - m) for s, m in zip(ss, ms)]
            dens = [jnp.sum(p, axis=1, keepdims=True) for p in ps]
            pvs = [lax.dot_general(p.astype(BF16), vv, NN, preferred_element_type=F32) / den
                   for p, vv, den in zip(ps, vvs, dens)]
            for (_, own, _), pv, m, den in zip(its, pvs, ms, dens):
                o_ref[own, :] = _unstack_heads(pv, masks)
                lse_ref[own, :] = _unstack_heads(jnp.broadcast_to(m + jnp.log(den), pv.shape), masks)
            return carry

        lax.fori_loop(0, n_iter // DIL_PAIR, step, 0)

    def col(b):
        return pl.BlockSpec((T, 128), lambda p: (0, b + p))

    sh = jax.ShapeDtypeStruct((T, D_ATT), F32)
    out = pl.BlockSpec((T, 128), lambda p: (0, p))
    return pl.pallas_call(
        body, name=f"dil_fwd_d{d}", grid=(2,),
        in_specs=[col(2 * g), col(6 + 2 * g), col(12 + 2 * g)], out_specs=[out, out], out_shape=[sh, sh],
        compiler_params=_params(dimension_semantics=("arbitrary",)),
    )(proj, proj, proj)


def _dil_bwd(proj, do, o_dil, lse, g, d):
    T = proj.shape[0]
    n_iter = T // SPAN

    def body(q_ref, k_ref, v_ref, do_ref, o_ref, lse_ref, dq_ref, dk_ref, dv_ref):
        masks = _pair_masks()
        head_lanes = jnp.concatenate(masks, axis=0)

        def step(pair, carry):
            its = [_dil_rows(DIL_PAIR * pair + e, d) for e in range(DIL_PAIR)]
            qs = [_stack_heads(q_ref[own, :] * (HEAD_DIM ** -0.5), masks).astype(BF16) for _, own, _ in its]
            kks = [_dil_keys(k_ref, own, prev) for _, own, prev in its]
            vvs = [_dil_keys(v_ref, own, prev) for _, own, prev in its]
            doms = [_stack_heads(do_ref[own, :], masks) for _, own, _ in its]
            dos = [dom.astype(BF16) for dom in doms]
            deltas = [jnp.sum(dom * jnp.concatenate([o_ref[own, :]] * 2, axis=0), axis=1, keepdims=True)
                      for dom, (_, own, _) in zip(doms, its)]
            lrows = [jnp.max(jnp.where(head_lanes, jnp.concatenate([lse_ref[own, :]] * 2, axis=0), NEG),
                             axis=1, keepdims=True) for _, own, _ in its]
            ss = [lax.dot_general(q, kk, NT, preferred_element_type=F32) for q, kk in zip(qs, kks)]
            dps = [lax.dot_general(do_b, vv, NT, preferred_element_type=F32) for do_b, vv in zip(dos, vvs)]
            ps = [jnp.where(_dil_valid(u), jnp.exp(s - lrow), 0.0) for (u, _, _), s, lrow in zip(its, ss, lrows)]
            dss = [(p * (dp - delta)).astype(BF16) for p, dp, delta in zip(ps, dps, deltas)]
            dqs = [lax.dot_general(ds, kk, NN, preferred_element_type=F32) for ds, kk in zip(dss, kks)]
            dkks = [lax.dot_general(ds, q, TN, preferred_element_type=F32) for ds, q in zip(dss, qs)]
            dvvs = [lax.dot_general(p.astype(BF16), do_b, TN, preferred_element_type=F32) for p, do_b in zip(ps, dos)]
            for (_, own, prev), dq, dkk, dvv in zip(its, dqs, dkks, dvvs):
                dq_ref[own, :] = _unstack_heads(dq, masks) * (HEAD_DIM ** -0.5)
                dk_ref[own, :] = dkk[SPAN:]
                dv_ref[own, :] = dvv[SPAN:]
                dk_ref[prev, :] = dk_ref[prev, :] + dkk[:SPAN]
                dv_ref[prev, :] = dv_ref[prev, :] + dvv[:SPAN]
            return carry

        lax.fori_loop(0, n_iter // DIL_PAIR, step, 0)

    def col(b):
        return pl.BlockSpec((T, 128), lambda p: (0, b + p))

    sh = jax.ShapeDtypeStruct((T, D_ATT), F32)
    return pl.pallas_call(
        body, name=f"dil_bwd_d{d}", grid=(2,),
        in_specs=[col(2 * g), col(6 + 2 * g), col(12 + 2 * g), col(0), col(0), col(0)],
        out_specs=[col(0), col(0), col(0)], out_shape=[sh, sh, sh],
        compiler_params=_params(dimension_semantics=("arbitrary",)),
    )(proj, proj, proj, do, o_dil, lse)


SB_KT = 512


def _sb_tri(strict):
    a = lax.broadcasted_iota(jnp.int32, (Q_BLOCK, Q_BLOCK), 0)
    b = lax.broadcasted_iota(jnp.int32, (Q_BLOCK, Q_BLOCK), 1)
    return jnp.where((a > b) if strict else (a >= b), 1.0, 0.0).astype(BF16)


def _split_stack(x):
    nb = x.shape[1] // Q_BLOCK
    blocks = [x[:, Q_BLOCK * b:Q_BLOCK * (b + 1)] for b in range(nb)]
    hi = [b.astype(BF16) for b in blocks]
    lo = [(b - h.astype(F32)).astype(BF16) for b, h in zip(blocks, hi)]
    return blocks, jnp.concatenate(hi + lo, axis=0)


def _suffix_from(y, blocks, c):
    r = blocks[0].shape[0]
    nb = len(blocks)
    outs = [None] * nb
    run = c
    for b in reversed(range(nb)):
        outs[b] = run + y[r * b:r * (b + 1)] + y[r * (nb + b):r * (nb + b + 1)]
        run = run + jnp.sum(blocks[b], axis=1, keepdims=True)
    return jnp.concatenate(outs, axis=1), run


SB_HEADS = D_ATT // HEAD_DIM
SB_FWD_CHAINS = 2
SB_BWD_CHAINS = 1


def _sb_past(i, t, rows):
    row = lax.broadcasted_iota(jnp.int32, (rows, SB_KT), 0) & (Q_BLOCK - 1)
    col = lax.broadcasted_iota(jnp.int32, (rows, SB_KT), 1)
    return col + t * SB_KT < row + i * Q_BLOCK


def _sb_head_masks(chains):
    lane = lax.broadcasted_iota(jnp.int32, (Q_BLOCK, D_ATT), 1)
    masks = [(lane >= HEAD_DIM * h) & (lane < HEAD_DIM * (h + 1)) for h in range(SB_HEADS)]
    per = SB_HEADS // chains
    return [masks[per * g:per * (g + 1)] for g in range(chains)]


def _sb_rows(t):
    return pl.ds(pl.multiple_of(t * SB_KT, SB_KT), SB_KT)


def _sb_log_terms(z, past):
    lsz = jnp.minimum(z, 0.0) - jnp.log(1.0 + jnp.exp(-jnp.abs(z)))
    lk = lsz - z
    return lsz, (lk if past is None else jnp.where(past, lk, 0.0))


def _sb_weights(lsz, after, past):
    w = jnp.exp(lsz + after)
    return w if past is None else jnp.where(past, w, 0.0)


def _sb_fwd(proj):
    T = proj.shape[0]

    def body(q_ref, k_ref, v_ref, o_ref):
        i = pl.program_id(0)
        masks = _sb_head_masks(SB_FWD_CHAINS)
        rows = SB_HEADS // SB_FWD_CHAINS * Q_BLOCK
        tri = _sb_tri(True)
        q = q_ref[...] * (HEAD_DIM ** -0.5)
        qs = [_stack_heads(q, m).astype(BF16) for m in masks]
        n_tiles = (i * Q_BLOCK) // SB_KT + 1

        def tile(t, carry, masked):
            kb = k_ref[_sb_rows(t), :].astype(BF16)
            vb = v_ref[_sb_rows(t), :].astype(BF16)
            past = _sb_past(i, t, rows) if masked else None
            acc, cs = carry[0], carry[1:]
            zs = [lax.dot_general(g, kb, NT, preferred_element_type=F32) for g in qs]
            logs = [_sb_log_terms(z, past) for z in zs]
            splits = [_split_stack(lk) for _, lk in logs]
            ys = [lax.dot_general(x, tri, NN, preferred_element_type=F32) for _, x in splits]
            sums = [_suffix_from(y, blocks, c) for y, (blocks, _), c in zip(ys, splits, cs)]
            ws = [_sb_weights(lsz, after, past).astype(BF16) for (lsz, _), (after, _) in zip(logs, sums)]
            for m, w in zip(masks, ws):
                acc = acc + _unstack_heads(lax.dot_general(w, vb, NN, preferred_element_type=F32), m)
            return (acc, *[c for _, c in sums])

        zcol = jnp.zeros((rows, 1), F32)
        carry = tile(n_tiles - 1, (jnp.zeros((Q_BLOCK, D_ATT), F32),) + (zcol,) * SB_FWD_CHAINS, True)
        carry = lax.fori_loop(0, n_tiles - 1, lambda tt, cr: tile(n_tiles - 2 - tt, cr, False), carry)
        o_ref[...] = carry[0]

    cb = COL_QS // D_ATT
    return pl.pallas_call(
        body, name="sb_fwd", grid=(T // Q_BLOCK,),
        in_specs=[pl.BlockSpec((Q_BLOCK, D_ATT), lambda i: (i, cb)),
                  pl.BlockSpec((T, D_ATT), lambda i: (0, cb + 1)),
                  pl.BlockSpec((T, D_ATT), lambda i: (0, cb + 2))],
        out_specs=pl.BlockSpec((Q_BLOCK, D_ATT), lambda i: (i, 0)),
        out_shape=jax.ShapeDtypeStruct((T, D_ATT), F32),
        compiler_params=_params(dimension_semantics=("arbitrary",)),
    )(proj, proj, proj)


def _sb_bwd(proj, do, o):
    T = proj.shape[0]

    def body(q_ref, k_ref, v_ref, do_ref, o_ref, dq_ref, dk_ref, dv_ref):
        i = pl.program_id(0)
        masks = _sb_head_masks(SB_BWD_CHAINS)
        n_rows = SB_HEADS // SB_BWD_CHAINS * Q_BLOCK
        tri = _sb_tri(True)
        tri_incl = _sb_tri(False)

        @pl.when(i == 0)
        def _():
            dk_ref[...] = jnp.zeros_like(dk_ref)
            dv_ref[...] = jnp.zeros_like(dv_ref)

        q = q_ref[...] * (HEAD_DIM ** -0.5)
        qs = [_stack_heads(q, m).astype(BF16) for m in masks]
        dos = [_stack_heads(do_ref[...], m).astype(BF16) for m in masks]
        o_rep = jnp.concatenate([o_ref[...]] * (SB_HEADS // SB_BWD_CHAINS), axis=0)
        deltas = [jnp.sum(d.astype(F32) * o_rep, axis=1, keepdims=True) for d in dos]
        n_tiles = (i * Q_BLOCK) // SB_KT + 1

        def tile(t, carry, masked):
            rows = _sb_rows(t)
            kb = k_ref[rows, :].astype(BF16)
            vb = v_ref[rows, :].astype(BF16)
            past = _sb_past(i, t, n_rows) if masked else None
            dq, cs, ces = carry[0], carry[1:1 + SB_BWD_CHAINS], carry[1 + SB_BWD_CHAINS:]
            zs = [lax.dot_general(g, kb, NT, preferred_element_type=F32) for g in qs]
            gvs = [lax.dot_general(d, vb, NT, preferred_element_type=F32) for d in dos]
            logs = [_sb_log_terms(z, past) for z in zs]
            splits = [_split_stack(lk) for _, lk in logs]
            ys = [lax.dot_general(x, tri, NN, preferred_element_type=F32) for _, x in splits]
            sums = [_suffix_from(y, blocks, c) for y, (blocks, _), c in zip(ys, splits, cs)]
            wbs = [_sb_weights(lsz, after, past).astype(BF16) for (lsz, _), (after, _) in zip(logs, sums)]
            es = [wb.astype(F32) * gv for wb, gv in zip(wbs, gvs)]
            esplits = [_split_stack(e) for e in es]
            eys = [lax.dot_general(x, tri_incl, NN, preferred_element_type=F32) for _, x in esplits]
            esums = [_suffix_from(y, blocks, ce) for y, (blocks, _), ce in zip(eys, esplits, ces)]
            dzbs = []
            for e, (lsz, lk), (suf, _), delta in zip(es, logs, esums, deltas):
                dz = e * jnp.exp(lk) - (delta - suf) * jnp.exp(lsz)
                dzbs.append((dz if past is None else jnp.where(past, dz, 0.0)).astype(BF16))
            dk_t = dv_t = None
            for m, dzb, wb, g, d in zip(masks, dzbs, wbs, qs, dos):
                dq = dq + _unstack_heads(lax.dot_general(dzb, kb, NN, preferred_element_type=F32), m)
                a = lax.dot_general(dzb, g, TN, preferred_element_type=F32)
                b = lax.dot_general(wb, d, TN, preferred_element_type=F32)
                dk_t = a if dk_t is None else dk_t + a
                dv_t = b if dv_t is None else dv_t + b
            dk_ref[rows, :] = dk_ref[rows, :] + dk_t
            dv_ref[rows, :] = dv_ref[rows, :] + dv_t
            return (dq, *[c for _, c in sums], *[c for _, c in esums])

        zcol = jnp.zeros((n_rows, 1), F32)
        carry = tile(n_tiles - 1, (jnp.zeros((Q_BLOCK, D_ATT), F32),) + (zcol,) * (2 * SB_BWD_CHAINS), True)
        carry = lax.fori_loop(0, n_tiles - 1, lambda tt, cr: tile(n_tiles - 2 - tt, cr, False), carry)
        dq_ref[...] = carry[0] * (HEAD_DIM ** -0.5)

    cb = COL_QS // D_ATT
    blk = pl.BlockSpec((Q_BLOCK, D_ATT), lambda i: (i, 0))
    full = pl.BlockSpec((T, D_ATT), lambda i: (0, 0))
    sh = jax.ShapeDtypeStruct((T, D_ATT), F32)
    return pl.pallas_call(
        body, name="sb_bwd", grid=(T // Q_BLOCK,),
        in_specs=[pl.BlockSpec((Q_BLOCK, D_ATT), lambda i: (i, cb)),
                  pl.BlockSpec((T, D_ATT), lambda i: (0, cb + 1)),
                  pl.BlockSpec((T, D_ATT), lambda i: (0, cb + 2)), blk, blk],
        out_specs=[blk, full, full], out_shape=[sh, sh, sh],
        compiler_params=_params(dimension_semantics=("arbitrary",)),
    )(proj, proj, proj, do, o)


def _tok(c, by=None):
    if by is None:
        return pl.BlockSpec((TM, c), lambda i, j, k: (i, 0))
    if by == 1:
        return pl.BlockSpec((TM, c), lambda i, j, k: (i, j))
    return pl.BlockSpec((TM, c), lambda i, j, k: (i, k))


def _gain_spec():
    return pl.BlockSpec((1, D_MODEL), lambda i, j, k: (0, 0))


def _wfull(r, c, l):
    return pl.BlockSpec((N_CHIPS, None, r, c), lambda i, j, k: (0, l, 0, 0))


def _pick(idx, c):
    return lambda ins: ins[idx][c]


def _cols(idx, c, w):
    return lambda ins: ins[idx][:, c * w:(c + 1) * w]


def _rows(rows, width):
    return pl.BlockSpec((rows, width), lambda i, j, k: (i, 0))


def _whole(shape):
    return pl.BlockSpec(shape, lambda i, j, k: (0, 0))


def _ffn_fwd(x, gain, wg, wu, wd):
    T = x.shape[0]
    wg, wu, wd = (w.reshape(-1, D_MODEL) for w in (wg, wu, wd))
    ff = wd.shape[0]
    tm = TM // 2
    normed = _normed(0, 3)

    def swiglu(vals, ins, outs, i):
        gt, up = vals
        s = _sigmoid(gt)
        sil = gt * s
        outs[0][...] = sil.astype(BF16)
        outs[1][...] = (up * (s * (1.0 + gt * (1.0 - s)))).astype(BF16)
        outs[2][...] = (sil * up).astype(BF16)
        outs[3][...] = normed(ins)

    ash = jax.ShapeDtypeStruct((T, ff), BF16)
    sil, up_dsil, act, h = _mm(
        "ffn_up", [x, wg, wu, gain], [_rows(tm, D_MODEL), _whole(wg.shape), _whole(wu.shape), _gain_spec()],
        [(normed, 1, 0), (normed, 2, 1)], 2, None, NT, (T // tm, 1, 1), swiglu,
        [ash] * 3 + [jax.ShapeDtypeStruct((T, D_MODEL), BF16)], [_rows(tm, ff)] * 3 + [_rows(tm, D_MODEL)])

    def resid(vals, ins, outs, i):
        outs[0][...] = ins[2][...] + 0.5 * vals[0]

    (y,) = _mm(
        "ffn_down", [act, wd, x], [_rows(TM, ff), _whole(wd.shape), _tok(D_MODEL)], [(0, 1, 0)], 1, None, NN,
        (T // TM, 1, 1), resid, [jax.ShapeDtypeStruct((T, D_MODEL), F32)], [_tok(D_MODEL)])
    return y, (x, h, sil, up_dsil, act)


def _ffn_bwd(dxo, gain, wg, wu, wd, saved):
    x, h, sil, up_dsil, act = saved
    T = x.shape[0]
    n_chips, _, ffs, _ = wd.shape
    wg, wu, wd = (w.reshape(-1, D_MODEL) for w in (wg, wu, wd))
    ff = wd.shape[0]
    tk = TM
    tm = TM // 2

    def dswiglu(vals, ins, outs, i):
        da = 0.5 * vals[0]
        outs[0][...] = (da * ins[3][...].astype(F32)).astype(BF16)
        outs[1][...] = (da * ins[2][...].astype(F32)).astype(BF16)

    ash = jax.ShapeDtypeStruct((T, ff), BF16)
    dgate, dup = _mm(
        "ffn_dact", [dxo, wd, sil, up_dsil], [_rows(tm, D_MODEL), _whole(wd.shape), _rows(tm, ff), _rows(tm, ff)],
        [(0, 1, 0)], 1, None, NT, (T // tm, 1, 1), dswiglu, [ash, ash], [_rows(tm, ff)] * 2)

    def half(vals, ins, outs, i):
        outs[0][...] = (0.5 * vals[0]).astype(BF16)

    def cast(vals, ins, outs, i):
        outs[0][...] = vals[0].astype(BF16)

    tok_k = pl.BlockSpec((tk, D_MODEL), lambda i, j, k: (k, 0))
    hid_k = pl.BlockSpec((tk, ff), lambda i, j, k: (k, 0))
    wsh = jax.ShapeDtypeStruct((ff, D_MODEL), BF16)
    (dwd,) = _mm("ffn_dwd", [act, dxo], [hid_k, tok_k], [(0, 1, 0)], 1, (ff, D_MODEL), TN, (1, 1, T // tk), half,
                 [wsh], [_whole((ff, D_MODEL))])

    dx, dgain = _mm(
        "ffn_dx", [dgate, dup, wg, wu, x, gain, dxo],
        [_rows(tm, ff), _rows(tm, ff), _whole(wg.shape), _whole(wu.shape), _rows(tm, D_MODEL), _gain_spec(),
         _rows(tm, D_MODEL)],
        [(0, 2, 0), (1, 3, 0)], 1, None, NN, (T // tm, 1, 1), _rms_bwd_epilogue(4, 5, 6),
        [jax.ShapeDtypeStruct((T, D_MODEL), F32), jax.ShapeDtypeStruct((8, D_MODEL), F32)],
        [_rows(tm, D_MODEL), pl.BlockSpec((8, D_MODEL), lambda i, j, k: (0, 0))])

    dws = []
    for dact in (dgate, dup):
        dws += _mm("ffn_dwgu", [dact, h], [hid_k, tok_k], [(0, 1, 0)], 1, (ff, D_MODEL), TN, (1, 1, T // tk), cast,
                   [wsh], [_whole((ff, D_MODEL))])
    dwg, dwu, dwd = (w.reshape(n_chips, ffs, D_MODEL) for w in (dws[0], dws[1], dwd))
    return dx, dgain, dwg, dwu, dwd


def _joined_mixer_weights(wpd, wps, wo):
    n, _, r, c = wpd.shape
    wpd_n, wps_n = (w[:, 0].transpose(1, 0, 2).reshape(r, n * c) for w in (wpd, wps))
    return wpd_n, wps_n, wo.reshape(-1, wo.shape[3])


def _mixer_fwd(x, gain, W, l, tabs):
    T = x.shape[0]
    win, wpd, wps, wo = W["w_in"], W["w_proj_dil"], W["w_proj_sb"], W["w_out"]
    cin = win.shape[3]
    cp = wpd.shape[3]
    normed = _normed(0, 5)
    n_rope = 6 * D_ATT

    tm = TM // 2

    def roped(vals, ins, outs, i):
        for j, v in enumerate(vals):
            lo = j * cin
            k = min(max(n_rope - lo, 0), cin)
            if k:
                tab = [jnp.concatenate([ins[t][...]] * (k // 128), axis=1) for t in (2, 3, 4)]
                outs[0][:, lo:lo + k] = _rope_fwd(v[:, :k], *tab)
            if k < cin:
                outs[0][:, lo + k:lo + cin] = v[:, k:]
        outs[1][...] = normed(ins)

    proj, h = _mm(
        "mix_in", [x, win, *tabs, gain],
        [_rows(tm, D_MODEL), _wfull(D_MODEL, cin, l)] + [_rows(tm, 128)] * 3 + [_gain_spec()],
        [(normed, _pick(1, c), c) for c in range(N_CHIPS)], N_CHIPS, None, NN, (T // tm, 1, 1), roped,
        [jax.ShapeDtypeStruct((T, N_CHIPS * cin), F32), jax.ShapeDtypeStruct((T, D_MODEL), BF16)],
        [_rows(tm, N_CHIPS * cin), _rows(tm, D_MODEL)])

    os_, lses = [], []
    for g, (window, dil) in enumerate(DIL_GROUPS):
        o_g, lse_g = _dil_fwd(proj, g, dil)
        os_.append(o_g)
        lses.append(lse_g)
    o_dil, lse = _dil_merge(os_, lses)
    o_sb = _sb_fwd(proj)

    def gated(vals, ins, outs, i):
        pd, ps = vals
        outs[0][...] = (_sigmoid(ins[4][...]) * pd + _sigmoid(ins[5][...]) * ps).astype(BF16)
        outs[1][...] = pd.astype(BF16)
        outs[2][...] = ps.astype(BF16)

    wpd_n, wps_n, wo_n = _joined_mixer_weights(wpd, wps, wo)
    gd_spec = pl.BlockSpec((TM, D_MODEL), lambda i, j, k: (i, COL_GD // D_MODEL))
    gs_spec = pl.BlockSpec((TM, D_MODEL), lambda i, j, k: (i, COL_GS // D_MODEL))
    ush = jax.ShapeDtypeStruct((T, D_MODEL), BF16)
    u, pd, ps = _mm(
        "mix_gate", [o_dil, o_sb, wpd_n, wps_n, proj, proj],
        [_tok(D_ATT), _tok(D_ATT), _whole(wpd_n.shape), _whole(wps_n.shape), gd_spec, gs_spec],
        [(0, 2, 0), (1, 3, 1)], 2, None, NN, (T // TM, 1, 1), gated, [ush] * 3, [_tok(D_MODEL)] * 3)

    def resid(vals, ins, outs, i):
        outs[0][...] = ins[2][...] + vals[0]

    (y,) = _mm(
        "mix_out", [u, wo_n, x], [_tok(D_MODEL), _whole(wo_n.shape), _tok(D_MODEL)], [(0, 1, 0)], 1, None, NN,
        (T // TM, 1, 1), resid, [jax.ShapeDtypeStruct((T, D_MODEL), F32)], [_tok(D_MODEL)])
    return y, (x, h, proj, o_dil, lse, o_sb, u, pd, ps)


def _mixer_bwd(dxo, gain, W, l, tabs, saved):
    x, h, proj, o_dil, lse, o_sb, u, pd, ps = saved
    T = x.shape[0]
    win, wpd, wps, wo = W["w_in"], W["w_proj_dil"], W["w_proj_sb"], W["w_out"]
    cin = win.shape[3]
    cp = wpd.shape[3]
    tk = TM
    tm = TM // 2
    row = pl.BlockSpec((tm, D_MODEL), lambda i, j, k: (i, 0))

    def dgated(vals, ins, outs, i):
        du = vals[0]
        sd = _sigmoid(ins[4][...])
        ss = _sigmoid(ins[5][...])
        outs[0][...] = (du * sd).astype(BF16)
        outs[1][...] = (du * ss).astype(BF16)
        outs[2][...] = (du * ins[2][...].astype(F32) * sd * (1.0 - sd)).astype(BF16)
        outs[3][...] = (du * ins[3][...].astype(F32) * ss * (1.0 - ss)).astype(BF16)

    wpd_n, wps_n, wo_n = _joined_mixer_weights(wpd, wps, wo)
    gd_spec = pl.BlockSpec((TM, D_MODEL), lambda i, j, k: (i, COL_GD // D_MODEL))
    gs_spec = pl.BlockSpec((TM, D_MODEL), lambda i, j, k: (i, COL_GS // D_MODEL))
    ush = jax.ShapeDtypeStruct((T, D_MODEL), BF16)
    dpd, dps, dgd, dgs = _mm(
        "mix_du", [dxo, wo_n, pd, ps, proj, proj],
        [_tok(D_MODEL), _whole(wo_n.shape), _tok(D_MODEL), _tok(D_MODEL), gd_spec, gs_spec],
        [(0, 1, 0)], 1, None, NT, (T // TM, 1, 1), dgated, [ush] * 4, [_tok(D_MODEL)] * 4)

    def one(vals, ins, outs, i):
        outs[0][...] = vals[0].astype(BF16)

    def two(vals, ins, outs, i):
        outs[0][...] = vals[0].astype(BF16)
        outs[1][...] = vals[1].astype(BF16)

    tok_k = pl.BlockSpec((tk, D_MODEL), lambda i, j, k: (k, 0))
    att_k = pl.BlockSpec((tk, D_ATT), lambda i, j, k: (k, 0))
    (dwo_n,) = _mm("mix_dwo", [u, dxo], [tok_k, tok_k], [(0, 1, 0)], 1, (D_MODEL, D_MODEL), TN, (1, 1, T // tk), one,
                   [jax.ShapeDtypeStruct((D_MODEL, D_MODEL), BF16)], [_whole((D_MODEL, D_MODEL))])

    def plain2(vals, ins, outs, i):
        outs[0][...] = vals[0]
        outs[1][...] = vals[1]

    ash = jax.ShapeDtypeStruct((T, D_ATT), F32)
    do_dil, do_sb = _mm(
        "mix_do", [dpd, dps, wpd_n, wps_n], [_tok(D_MODEL), _tok(D_MODEL), _whole(wpd_n.shape), _whole(wps_n.shape)],
        [(0, 2, 0), (1, 3, 1)], 2, None, NT, (T // TM, 1, 1), plain2, [ash, ash], [_tok(D_ATT)] * 2)

    psh = jax.ShapeDtypeStruct((D_ATT, D_MODEL), BF16)
    dwpd_n, dwps_n = _mm(
        "mix_dwp", [o_dil, o_sb, dpd, dps], [att_k, att_k, tok_k, tok_k], [(0, 2, 0), (1, 3, 1)], 2,
        (D_ATT, D_MODEL), TN, (1, 1, T // tk), two, [psh, psh], [_whole((D_ATT, D_MODEL))] * 2)
    dwpd, dwps = (w.reshape(D_ATT, N_CHIPS, cp).transpose(1, 0, 2) for w in (dwpd_n, dwps_n))
    dwo = dwo_n.reshape(N_CHIPS, cp, D_MODEL)

    dqs, dks, dvs = [], [], []
    for g, (window, dil) in enumerate(DIL_GROUPS):
        dq, dk, dv = _dil_bwd(proj, do_dil, o_dil, lse, g, dil)
        dqs.append(dq)
        dks.append(dk)
        dvs.append(dv)
    dq_s, dk_s, dv_s = _sb_bwd(proj, do_sb, o_sb)
    dproj = _assemble_dproj(dqs + dks, dvs + [dq_s, dk_s, dv_s], [dgd, dgs], tabs)

    dx, dgain = _mm(
        "mix_dx", [dproj, win, x, gain, dxo],
        [pl.BlockSpec((tm, N_CHIPS * cin), lambda i, j, k: (i, 0)), _wfull(D_MODEL, cin, l), row, _gain_spec(), row],
        [(_cols(0, c, cin), _pick(1, c), 0) for c in range(N_CHIPS)], 1, None, NT, (T // tm, 1, 1),
        _rms_bwd_epilogue(2, 3, 4),
        [jax.ShapeDtypeStruct((T, D_MODEL), F32), jax.ShapeDtypeStruct((8, D_MODEL), F32)],
        [row, pl.BlockSpec((8, D_MODEL), lambda i, j, k: (0, 0))])

    (dwin,) = _mm(
        "mix_dwin", [h, dproj],
        [pl.BlockSpec((tk, D_MODEL), lambda i, j, k: (k, 0)), pl.BlockSpec((tk, cin), lambda i, j, k: (k, j))],
        [(0, 1, 0)], 1, (D_MODEL, cin), TN, (1, N_CHIPS, T // tk), one,
        [jax.ShapeDtypeStruct((N_CHIPS, D_MODEL, cin), BF16)],
        [pl.BlockSpec((None, D_MODEL, cin), lambda i, j, k: (j, 0, 0))])
    return dx, dgain, dwin, dwpd, dwps, dwo


def _local_step(x, target, norms, norm_final, weights_of, on_grads):
    T = x.shape[0]
    tabs = _rope_tables(T)
    saved, held = [], []
    for l in range(DEPTH):
        w1 = weights_of(l, 0, x)
        x, s1 = _ffn_fwd(x, norms["norm_ffn1"][l:l + 1], w1["ffn1_w_gate"], w1["ffn1_w_up"], w1["ffn1_w_down"])
        w2 = weights_of(l, 1, x)
        x, s2 = _mixer_fwd(x, norms["norm_mix"][l:l + 1], w2, 0, tabs)
        w3 = weights_of(l, 2, x)
        x, s3 = _ffn_fwd(x, norms["norm_ffn2"][l:l + 1], w3["ffn2_w_gate"], w3["ffn2_w_up"], w3["ffn2_w_down"])
        saved.append((s1, s2, s3))
        held.append((w1, w2, w3))
    dx, dg_final, loss = _final_loss(x, norm_final.reshape(1, D_MODEL), target)
    gains = [None] * DEPTH
    for l in reversed(range(DEPTH)):
        s1, s2, s3 = saved[l]
        w1, w2, w3 = held[l]
        dx, dg2, dwg2, dwu2, dwd2 = _ffn_bwd(dx, norms["norm_ffn2"][l:l + 1], w3["ffn2_w_gate"], w3["ffn2_w_up"],
                                             w3["ffn2_w_down"], s3)
        dx = on_grads(l, 2, dict(ffn2_w_gate=dwg2, ffn2_w_up=dwu2, ffn2_w_down=dwd2), dx)
        dx, dgm, dwin, dwpd, dwps, dwo = _mixer_bwd(dx, norms["norm_mix"][l:l + 1], w2, 0, tabs, s2)
        dx = on_grads(l, 1, dict(w_in=dwin, w_proj_dil=dwpd, w_proj_sb=dwps, w_out=dwo), dx)
        dx, dg1, dwg1, dwu1, dwd1 = _ffn_bwd(dx, norms["norm_ffn1"][l:l + 1], w1["ffn1_w_gate"], w1["ffn1_w_up"],
                                             w1["ffn1_w_down"], s1)
        dx = on_grads(l, 0, dict(ffn1_w_gate=dwg1, ffn1_w_up=dwu1, ffn1_w_down=dwd1), dx)
        gains[l] = dict(norm_ffn1=dg1, norm_mix=dgm, norm_ffn2=dg2)
    return loss, dx, gains, dg_final


def _place():
    x, y, c = lax.axis_index("x"), lax.axis_index("y"), lax.axis_index("c")
    chips = [(1 - x, y), (x, 1 - y), (1 - x, 1 - y)]
    return x, y, c, chips


def _half(c, r):
    return pl.ds(pl.multiple_of(c * (r // 2), 8), r // 2)


def _cast_into_slot(ws, ls, me_arr, after):
    n = len(ws)
    late = [] if after is None else [after]

    def body(me_ref, *refs):
        for a in range(n):
            refs[len(refs) - n + a][...] = refs[a][...].astype(BF16)

    def src(w, l):
        return pl.BlockSpec((None, w.shape[1] // 4, w.shape[2]), lambda i, me: (l, i, 0))

    def dst(w):
        return pl.BlockSpec((None, None, w.shape[1] // 4, w.shape[2]), lambda i, me: (me[0], 0, i, 0))

    return pl.pallas_call(
        body, name="cast_weights",
        grid_spec=pltpu.PrefetchScalarGridSpec(
            num_scalar_prefetch=1, grid=(4,),
            in_specs=[src(w, l) for w, l in zip(ws, ls)] + [pl.BlockSpec(memory_space=pl.ANY)] * len(late),
            out_specs=[dst(w) for w in ws]),
        out_shape=[jax.ShapeDtypeStruct((N_CHIPS, 1) + w.shape[1:], BF16) for w in ws], compiler_params=_params(),
    )(me_arr, *ws, *late)


HBM_SPEC = pl.BlockSpec(memory_space=pltpu.HBM)
SEM_SPEC = pl.BlockSpec(memory_space=pltpu.SEMAPHORE)
SPLIT_COPY = pltpu.CompilerParams(has_side_effects=pltpu.SideEffectType.DATAFLOW_SIDE_EFFECTING)


def _gather_piece(ref, chip_id, c):
    return ref.at[chip_id, 0, _half(c, ref.shape[2]), :]


def _gather_start(tag, bufs, direct):
    n = len(bufs)

    def body(*refs):
        out_refs = refs[n:2 * n]
        send_sems, recv_sems, token = refs[2 * n:]
        x, y, c, chips = _place()
        me = 2 * x + y
        for a in range(n):
            piece = _gather_piece(out_refs[a], me, c)
            for j, chip in enumerate(chips):
                for to in ((0, 1) if direct[a] else (c,)):
                    pltpu.make_async_remote_copy(
                        src_ref=piece, dst_ref=piece, send_sem=send_sems.at[6 * a + 2 * j + to],
                        recv_sem=recv_sems.at[6 * a + 2 * j + c], device_id=(*chip, to), device_id_type=MESH).start()
        token[...] = jnp.zeros_like(token)

    outs = pl.pallas_call(
        body, name=f"gather_start_{tag}", in_specs=[HBM_SPEC] * n,
        out_specs=[HBM_SPEC] * n + [SEM_SPEC, SEM_SPEC, pl.BlockSpec(memory_space=pltpu.VMEM)],
        out_shape=[pltpu.HBM(b.shape, b.dtype) for b in bufs] + [pltpu.SemaphoreType.DMA((6 * n,))] * 2
        + [jax.ShapeDtypeStruct((8, 128), F32)],
        input_output_aliases={a: a for a in range(n)}, compiler_params=SPLIT_COPY,
    )(*[pltpu.with_memory_space_constraint(b, pltpu.HBM) for b in bufs])
    return outs[:n], outs[n], outs[n + 1], outs[n + 2]


def _gather_wait(k, bufs, places, direct, send_sems, recv_sems, after):
    m = len(bufs)

    def body(*refs):
        in_refs = refs[:m]
        ssem, rsem = refs[m], refs[m + 1]
        x, y, c, chips = _place()
        me = 2 * x + y
        for t, a in enumerate(places):
            for j, chip in enumerate(chips):
                for core in ((0, 1) if direct else (c,)):
                    cp = pltpu.make_async_remote_copy(
                        src_ref=_gather_piece(in_refs[t], me, c),
                        dst_ref=_gather_piece(in_refs[t], 2 * chip[0] + chip[1], core),
                        send_sem=ssem.at[6 * a + 2 * j + core], recv_sem=rsem.at[6 * a + 2 * j + core],
                        device_id=(*chip, core), device_id_type=MESH)
                    cp.wait_send()
                    cp.wait_recv()

    return pl.pallas_call(
        body, name=f"gather_wait_{k}",
        in_specs=[HBM_SPEC] * m + [SEM_SPEC, SEM_SPEC, pl.BlockSpec(memory_space=pl.ANY)], out_specs=[HBM_SPEC] * m,
        out_shape=[pltpu.HBM(b.shape, b.dtype) for b in bufs], input_output_aliases={t: t for t in range(m)},
        compiler_params=SPLIT_COPY,
    )(*bufs, send_sems, recv_sems, after)


def _gather_relay(bufs):
    n = len(bufs)

    def body(*refs):
        out_refs = refs[n:2 * n]
        send_sems, recv_sems = refs[2 * n:]
        x, y, c, chips = _place()
        cps = []
        for a in range(n):
            for j, chip in enumerate(chips):
                piece = _gather_piece(out_refs[a], 2 * chip[0] + chip[1], c)
                cps.append(pltpu.make_async_remote_copy(
                    src_ref=piece, dst_ref=piece, send_sem=send_sems.at[a, j], recv_sem=recv_sems.at[a, j],
                    device_id=(x, y, 1 - c), device_id_type=MESH))
        for cp in cps:
            cp.start()
        for a in range(n):
            for j, chip in enumerate(chips):
                theirs = _gather_piece(out_refs[a], 2 * chip[0] + chip[1], 1 - c)
                pltpu.make_async_remote_copy(
                    src_ref=theirs, dst_ref=theirs, send_sem=send_sems.at[a, j], recv_sem=recv_sems.at[a, j],
                    device_id=(x, y, 1 - c), device_id_type=MESH).wait_recv()
        for cp in cps:
            cp.wait_send()

    any_spec = pl.BlockSpec(memory_space=pl.ANY)
    return pl.pallas_call(
        body, name="gather_relay", in_specs=[any_spec] * n, out_specs=[any_spec] * n,
        out_shape=[jax.ShapeDtypeStruct(b.shape, b.dtype) for b in bufs],
        input_output_aliases={a: a for a in range(n)},
        scratch_shapes=[pltpu.SemaphoreType.DMA((n, 3))] * 2,
    )(*bufs)


def _other_half(ref, c):
    return ref.at[:, _half(1 - c, ref.shape[1]), :]


def _all_of(ref, c):
    return ref


def _sibling_start(name, srcs, pick, land_shapes, thru):
    n = len(srcs)
    lands = [lax.empty(sh, s.dtype) for sh, s in zip(land_shapes, srcs)]
    kept = lands + ([] if thru is None else [thru])
    m = len(kept)

    def body(*refs):
        s_refs, land_refs = refs[:n], refs[n + m:n + m + n]
        send_sems, recv_sems, token = refs[n + 2 * m:]
        x, y, c, _ = _place()
        for a in range(n):
            pltpu.make_async_remote_copy(
                src_ref=pick(s_refs[a], c), dst_ref=land_refs[a], send_sem=send_sems.at[a],
                recv_sem=recv_sems.at[a], device_id=(x, y, 1 - c), device_id_type=MESH).start()
        token[...] = jnp.zeros_like(token)

    outs = pl.pallas_call(
        body, name=name, in_specs=[HBM_SPEC] * (n + m),
        out_specs=[HBM_SPEC] * m + [SEM_SPEC, SEM_SPEC, pl.BlockSpec(memory_space=pltpu.VMEM)],
        out_shape=[pltpu.HBM(v.shape, v.dtype) for v in kept] + [pltpu.SemaphoreType.DMA((n,))] * 2
        + [jax.ShapeDtypeStruct((8, 128), F32)],
        input_output_aliases={n + a: a for a in range(m)}, compiler_params=SPLIT_COPY,
    )(*[pltpu.with_memory_space_constraint(v, pltpu.HBM) for v in list(srcs) + kept])
    return (outs[:n], outs[m], outs[m + 1]), (outs[n] if thru is not None else None), outs[m + 2]


def _sibling_wait(name, srcs, pick, lands, send_sems, recv_sems, after):
    n = len(srcs)

    def body(*refs):
        s_refs, land_refs = refs[:n], refs[n:2 * n]
        ssem, rsem = refs[2 * n], refs[2 * n + 1]
        x, y, c, _ = _place()
        for a in range(n):
            cp = pltpu.make_async_remote_copy(
                src_ref=pick(s_refs[a], c), dst_ref=land_refs[a], send_sem=ssem.at[a], recv_sem=rsem.at[a],
                device_id=(x, y, 1 - c), device_id_type=MESH)
            cp.wait_send()
            cp.wait_recv()

    return pl.pallas_call(
        body, name=name, in_specs=[HBM_SPEC] * (2 * n) + [SEM_SPEC, SEM_SPEC, pl.BlockSpec(memory_space=pl.ANY)],
        out_specs=[HBM_SPEC] * n, out_shape=[pltpu.HBM(v.shape, v.dtype) for v in lands],
        input_output_aliases={n + a: a for a in range(n)}, compiler_params=SPLIT_COPY,
    )(*srcs, *lands, send_sems, recv_sems, after)


def _add_half(gs, gots, c_arr):
    n = len(gs)

    def body(c_ref, *refs):
        for a in range(n):
            refs[2 * n + a][...] = (refs[a][...].astype(F32) + refs[n + a][...].astype(F32)).astype(BF16)

    def own(g):
        return pl.BlockSpec((None, g.shape[1] // 2, g.shape[2]), lambda k, cr: (k, cr[0], 0))

    def half(g):
        return pl.BlockSpec((None, g.shape[1] // 2, g.shape[2]), lambda k, cr: (k, 0, 0))

    return pl.pallas_call(
        body, name="grad_add_half",
        grid_spec=pltpu.PrefetchScalarGridSpec(
            num_scalar_prefetch=1, grid=(N_CHIPS,),
            in_specs=[own(g) for g in gs] + [half(g) for g in gs], out_specs=[half(g) for g in gs]),
        out_shape=[jax.ShapeDtypeStruct(got.shape, BF16) for got in gots], compiler_params=_params(),
    )(c_arr, *gs, *gots)


def _scatter_start(k, ss, thru):
    n = len(ss)

    def body(*refs):
        s_refs, land_refs = refs[2 * n + 1:3 * n + 1], refs[3 * n + 1:4 * n + 1]
        send_sems, recv_sems = refs[4 * n + 2:]
        x, y, c, chips = _place()
        me = 2 * x + y
        for a in range(n):
            for j, chip in enumerate(chips):
                pltpu.make_async_remote_copy(
                    src_ref=s_refs[a].at[2 * chip[0] + chip[1]], dst_ref=land_refs[a].at[me],
                    send_sem=send_sems.at[3 * a + j], recv_sem=recv_sems.at[3 * a + j], device_id=(*chip, c),
                    device_id_type=MESH).start()

    lands = [lax.empty(s.shape, s.dtype) for s in ss]
    hbm = [pltpu.HBM(s.shape, s.dtype) for s in ss]
    outs = pl.pallas_call(
        body, name=f"grad_scatter_start_{k}", in_specs=[HBM_SPEC] * (2 * n + 1),
        out_specs=[HBM_SPEC] * (2 * n + 1) + [SEM_SPEC, SEM_SPEC],
        out_shape=hbm + hbm + [pltpu.HBM(thru.shape, thru.dtype)] + [pltpu.SemaphoreType.DMA((3 * n,))] * 2,
        input_output_aliases={a: a for a in range(2 * n + 1)}, compiler_params=SPLIT_COPY,
    )(*[pltpu.with_memory_space_constraint(v, pltpu.HBM) for v in list(ss) + lands + [thru]])
    return (outs[:n], outs[n:2 * n], outs[2 * n + 1], outs[2 * n + 2]), outs[2 * n]


def _scatter_wait(k, ss, lands, send_sems, recv_sems, after):
    n = len(ss)

    def body(*refs):
        s_refs, land_refs = refs[:n], refs[n:2 * n]
        ssem, rsem = refs[2 * n], refs[2 * n + 1]
        x, y, c, chips = _place()
        me = 2 * x + y
        for a in range(n):
            for j, chip in enumerate(chips):
                cid = 2 * chip[0] + chip[1]
                cp = pltpu.make_async_remote_copy(
                    src_ref=s_refs[a].at[cid], dst_ref=land_refs[a].at[cid], send_sem=ssem.at[3 * a + j],
                    recv_sem=rsem.at[3 * a + j], device_id=(*chip, c), device_id_type=MESH)
                cp.wait_send()
                cp.wait_recv()

    hbm = [pltpu.HBM(s.shape, s.dtype) for s in ss]
    outs = pl.pallas_call(
        body, name=f"grad_scatter_wait_{k}",
        in_specs=[HBM_SPEC] * (2 * n) + [SEM_SPEC, SEM_SPEC, pl.BlockSpec(memory_space=pl.ANY)],
        out_specs=[HBM_SPEC] * (2 * n), out_shape=hbm + hbm,
        input_output_aliases={a: a for a in range(2 * n)}, compiler_params=SPLIT_COPY,
    )(*ss, *lands, send_sems, recv_sems, after)
    return outs[:n], outs[n:]


def _sum_chips(lands, ss, me_arr):
    n = len(lands)

    def body(me_ref, *refs):
        for own in range(N_CHIPS):
            @pl.when(me_ref[0] == own)
            def _(own=own):
                for a in range(n):
                    acc = None
                    for k in range(N_CHIPS):
                        term = (refs[n + a][...] if k == own else refs[a][k]).astype(F32)
                        acc = term if acc is None else acc + term
                    refs[2 * n + a][...] = acc

    return pl.pallas_call(
        body, name="grad_sum_chips",
        grid_spec=pltpu.PrefetchScalarGridSpec(
            num_scalar_prefetch=1, grid=(1,),
            in_specs=[pl.BlockSpec(la.shape, lambda i, me: (0, 0, 0)) for la in lands]
            + [pl.BlockSpec((None,) + la.shape[1:], lambda i, me: (me[0], 0, 0)) for la in lands],
            out_specs=[pl.BlockSpec(la.shape[1:], lambda i, me: (0, 0)) for la in lands]),
        out_shape=[jax.ShapeDtypeStruct(la.shape[1:], F32) for la in lands], compiler_params=_params(),
    )(me_arr, *lands, *ss)


def _allreduce_rows(stats):
    def body(s_ref, o_ref, buf, send_sems, recv_sems):
        x, y, c, _ = _place()
        me = 4 * x + 2 * y + c
        buf[me] = s_ref[...]
        cps = []
        for k in range(1, 8):
            px = jnp.where(k & 4, 1 - x, x)
            py = jnp.where(k & 2, 1 - y, y)
            pc = jnp.where(k & 1, 1 - c, c)
            cps.append(pltpu.make_async_remote_copy(
                src_ref=s_ref, dst_ref=buf.at[me], send_sem=send_sems.at[k - 1], recv_sem=recv_sems.at[k - 1],
                device_id=(px, py, pc), device_id_type=MESH))
        for cp in cps:
            cp.start()
        for cp in cps:
            cp.wait()
        acc = buf[0]
        for d in range(1, 8):
            acc = acc + buf[d]
        o_ref[...] = acc

    vm = pl.BlockSpec(memory_space=pltpu.VMEM)
    return pl.pallas_call(
        body, name="allreduce_rows", in_specs=[vm], out_specs=vm,
        out_shape=jax.ShapeDtypeStruct(stats.shape, F32),
        scratch_shapes=[pltpu.VMEM((8,) + stats.shape, F32), pltpu.SemaphoreType.DMA((7,)),
                        pltpu.SemaphoreType.DMA((7,))],
    )(stats)


def _adamw_math(w, g, m, v):
    m = ADAM_B1 * m + (1.0 - ADAM_B1) * g
    v = ADAM_B2 * v + (1.0 - ADAM_B2) * (g * g)
    m_hat = m / (1.0 - ADAM_B1 ** ADAM_STEP)
    v_hat = v / (1.0 - ADAM_B2 ** ADAM_STEP)
    delta = -ADAM_LR * (m_hat / (jnp.sqrt(v_hat) + ADAM_EPS) + ADAM_WD * w)
    return delta, m, v


def _adamw(ws, ms, vs, mines, theirs, l, c_arr, earlier, after):
    n = len(ws)
    held = [t for e in earlier if e is not None for t in e]
    assert len(held) in (0, 4 * n)
    late = [] if after is None else [after]

    def body(c_ref, *refs):
        outs = refs[len(refs) - 4 * n:]
        for a in range(n):
            w_ref, m_ref, v_ref, a_ref, b_ref = refs[5 * a:5 * a + 5]
            g = jnp.where(pl.program_id(0) == c_ref[0], a_ref[...], b_ref[...])
            delta, mn, vn = _adamw_math(w_ref[...], g, m_ref[...], v_ref[...])
            outs[4 * a][...] = g
            outs[4 * a + 1][...] = delta
            outs[4 * a + 2][...] = mn
            outs[4 * a + 3][...] = vn

    def blk(w):
        tr = w.shape[1] // 4
        return pl.BlockSpec((None, tr, w.shape[2]), lambda hh, i, cr: (l, 2 * hh + i, 0))

    def half(w):
        return pl.BlockSpec((w.shape[1] // 4, w.shape[2]), lambda hh, i, cr: (i, 0))

    outs = pl.pallas_call(
        body, name="adamw",
        grid_spec=pltpu.PrefetchScalarGridSpec(
            num_scalar_prefetch=1, grid=(2, 2),
            in_specs=[sp for w in ws for sp in (blk(w), blk(w), blk(w), half(w), half(w))]
            + [pl.BlockSpec(memory_space=pl.ANY)] * (len(held) + len(late)),
            out_specs=[blk(w) for w in ws for _ in range(4)]),
        out_shape=[jax.ShapeDtypeStruct(w.shape, F32) for w in ws for _ in range(4)],
        input_output_aliases={1 + 5 * n + t: t for t in range(len(held))}, compiler_params=_params(),
    )(c_arr, *[t for grp in zip(ws, ms, vs, mines, theirs) for t in grp], *held, *late)
    return [outs[4 * a:4 * a + 4] for a in range(n)]


def _adamw_rows(w, m, v, g):
    def body(w_ref, m_ref, v_ref, g_ref, d_ref, mo_ref, vo_ref):
        delta, mn, vn = _adamw_math(w_ref[...], g_ref[...], m_ref[...], v_ref[...])
        d_ref[...] = delta
        mo_ref[...] = mn
        vo_ref[...] = vn

    vm = pl.BlockSpec(memory_space=pltpu.VMEM)
    sh = jax.ShapeDtypeStruct(w.shape, F32)
    return pl.pallas_call(body, name="adamw_rows", in_specs=[vm] * 4, out_specs=[vm] * 3, out_shape=[sh] * 3)(w, m, v, g)


SUBLAYERS = (("ffn1_w_gate", "ffn1_w_up", "ffn1_w_down"), ("w_in", "w_proj_dil", "w_proj_sb", "w_out"),
             ("ffn2_w_gate", "ffn2_w_up", "ffn2_w_down"))
TRANSPOSED = ("ffn1_w_gate", "ffn1_w_up", "ffn2_w_gate", "ffn2_w_up")


def _pick_row(blocks):
    row = lax.broadcasted_iota(jnp.int32, (8, D_MODEL), 0)
    out = jnp.zeros((8, D_MODEL), F32)
    for i, b in enumerate(blocks):
        out = out + jnp.where(row == i, b, 0.0)
    return out


def kernel(x, norm_ffn1, ffn1_w_gate, ffn1_w_up, ffn1_w_down, norm_mix, w_in, w_proj_dil, w_proj_sb, w_out, norm_ffn2, ffn2_w_gate, ffn2_w_up, ffn2_w_down, norm_final, loss_target, m_norm_ffn1, m_ffn1_w_gate, m_ffn1_w_up, m_ffn1_w_down, m_norm_mix, m_w_in, m_w_proj_dil, m_w_proj_sb, m_w_out, m_norm_ffn2, m_ffn2_w_gate, m_ffn2_w_up, m_ffn2_w_down, m_norm_final, v_norm_ffn1, v_ffn1_w_gate, v_ffn1_w_up, v_ffn1_w_down, v_norm_mix, v_w_in, v_w_proj_dil, v_w_proj_sb, v_w_out, v_norm_ffn2, v_ffn2_w_gate, v_ffn2_w_up, v_ffn2_w_down, v_norm_final):
    given = dict(locals())
    for n in TRANSPOSED:
        for k in ("", "m_", "v_"):
            given[k + n] = jnp.swapaxes(given[k + n], 1, 2)
    weights = {n: given[n] for n in WEIGHT_NAMES}
    norms = {n: given[n] for n in NORM_NAMES}

    c_arr = lax.axis_index("c").astype(jnp.int32).reshape(1)
    me_arr = (2 * lax.axis_index("x") + lax.axis_index("y")).astype(jnp.int32).reshape(1)
    order = [(l, s, n) for l in range(DEPTH) for s in range(len(SUBLAYERS)) for n in SUBLAYERS[s]]
    n_first = len(SUBLAYERS[0])
    sent, token = {}, None
    for tag, idxs in (("a", range(n_first)), ("b", range(n_first, len(order)))):
        cast = _cast_into_slot([weights[order[i][2]] for i in idxs], [order[i][0] for i in idxs], me_arr, token)
        bufs, send_sems, recv_sems, token = _gather_start(tag, cast, [order[i][0] > 0 for i in idxs])
        for p, i in enumerate(idxs):
            sent[i] = (bufs[p], p, send_sems, recv_sems)

    def weights_of(l, s, after):
        idxs = [i for i, (ll, ss, _) in enumerate(order) if (ll, ss) == (l, s)]
        got = _gather_wait(len(SUBLAYERS) * l + s, [sent[i][0] for i in idxs], [sent[i][1] for i in idxs], l > 0,
                           sent[idxs[0]][2], sent[idxs[0]][3], after)
        return {order[i][2]: g for i, g in zip(idxs, got if l > 0 else _gather_relay(got))}

    out = {}
    to_add, in_flight = [], []

    def add_and_scatter(after):
        l, s, names, gs, lands, ssem, rsem = to_add.pop(0)
        k = len(SUBLAYERS) * l + s
        got = _sibling_wait(f"grad_exchange_wait_{k}", gs, _other_half, lands, ssem, rsem, after)
        sent, after = _scatter_start(k, _add_half(gs, got, c_arr), after)
        in_flight.append((l, s, names) + sent)
        return after

    def on_grads(l, s, grads, after):
        names = list(grads)
        k = len(SUBLAYERS) * l + s
        gs = [grads[n] for n in names]
        sent, after, _ = _sibling_start(f"grad_exchange_start_{k}", gs, _other_half,
                                        [(g.shape[0], g.shape[1] // 2, g.shape[2]) for g in gs], after)
        if to_add:
            after = add_and_scatter(after)
        to_add.append((l, s, names, gs) + sent)
        return after

    loss_blk, grad_x, gains, dg_final = _local_step(x[0], loss_target[0], norms, norm_final, weights_of, on_grads)
    grad_x = add_and_scatter(grad_x)

    def update(l, names, mine, swap, after):
        theirs = _sibling_wait(f"grad_swap_wait_{l}_{names[0]}", mine, _all_of, *swap, grad_x if after is None else after)
        res = _adamw([weights[n] for n in names], [given["m_" + n] for n in names], [given["v_" + n] for n in names],
                     mine, theirs, l, c_arr, [out.get(n) for n in names], after)
        out.update(zip(names, res))

    waiting = None
    for l, s, names, sums, lands, ssem, rsem in in_flight:
        sums, lands = _scatter_wait(len(SUBLAYERS) * l + s, sums, lands, ssem, rsem, grad_x)
        mine = _sum_chips(lands, sums, me_arr)
        swap, _, token = _sibling_start(f"grad_swap_start_{l}_{names[0]}", mine, _all_of,
                                        [m.shape for m in mine], None)
        if waiting is not None:
            update(*waiting, token)
        waiting = (l, names, mine, swap)
    update(*waiting, None)
    out = {k + n: (jnp.swapaxes(v, 1, 2) if n in TRANSPOSED else v)
           for n, res in out.items() for k, v in zip(("grad_", "delta_", "new_m_", "new_v_"), res)}
    out["grad_x"] = grad_x[None]

    rows = [gains[l][n] for n in NORM_NAMES for l in range(DEPTH)] + [dg_final, loss_blk]
    total = _allreduce_rows(_pick_row(rows))
    out["loss"] = total[7, 0]
    wn = jnp.concatenate([given[n] for n in NORM_NAMES] + [norm_final[None], jnp.zeros((1, D_MODEL), F32)])
    mn_ = jnp.concatenate([given["m_" + n] for n in NORM_NAMES] + [m_norm_final[None], jnp.zeros((1, D_MODEL), F32)])
    vn_ = jnp.concatenate([given["v_" + n] for n in NORM_NAMES] + [v_norm_final[None], jnp.ones((1, D_MODEL), F32)])
    d_n, m_n, v_n = _adamw_rows(wn, mn_, vn_, total)
    for i, n in enumerate(NORM_NAMES):
        sl = slice(i * DEPTH, (i + 1) * DEPTH)
        out["grad_" + n], out["delta_" + n], out["new_m_" + n], out["new_v_" + n] = total[sl], d_n[sl], m_n[sl], v_n[sl]
    out["grad_norm_final"], out["delta_norm_final"] = total[6], d_n[6]
    out["new_m_norm_final"], out["new_v_norm_final"] = m_n[6], v_n[6]

    names = ["norm_ffn1", "ffn1_w_gate", "ffn1_w_up", "ffn1_w_down", "norm_mix", "w_in", "w_proj_dil", "w_proj_sb",
             "w_out", "norm_ffn2", "ffn2_w_gate", "ffn2_w_up", "ffn2_w_down", "norm_final"]
    return (out["loss"], out["grad_x"], *[out["grad_" + n] for n in names], *[out["delta_" + n] for n in names],
            *[out["new_m_" + n] for n in names], *[out["new_v_" + n] for n in names])
```

```python
import functools

import jax
import jax.numpy as jnp
from jax import lax
from jax.experimental import pallas as pl
from jax.experimental.pallas import tpu as pltpu

F32 = jnp.float32
BF16 = jnp.bfloat16

D_MODEL = 1024
DEPTH = 2
N_CHIPS = 4
HEAD_DIM = 64
ROPE_DIM = 16
ROPE_THETA = 500000.0
DIL_GROUPS = ((128, 1), (512, 4), (2048, 16))
SPAN = 128
Q_BLOCK = 128
RMS_EPS = 1e-6
D_ATT = 256
COL_QS = 2304
COL_GD = 3072
COL_GS = 4096
ADAM_LR, ADAM_B1, ADAM_B2, ADAM_EPS, ADAM_WD, ADAM_STEP = 0.001, 0.9, 0.999, 1e-08, 0.01, 10

VMEM_LIMIT = 52 * 1024 * 1024
TM = 512
NEG = -1e30

NN = (((1,), (0,)), ((), ()))
NT = (((1,), (1,)), ((), ()))
TN = (((0,), (0,)), ((), ()))
MESH = pl.DeviceIdType.MESH

WEIGHT_NAMES = ("ffn1_w_gate", "ffn1_w_up", "ffn1_w_down", "w_in", "w_proj_dil",
                "w_proj_sb", "w_out", "ffn2_w_gate", "ffn2_w_up", "ffn2_w_down")
NORM_NAMES = ("norm_ffn1", "norm_mix", "norm_ffn2")


def _params(**kw):
    return pltpu.CompilerParams(vmem_limit_bytes=VMEM_LIMIT, **kw)


def _sigmoid(x):
    return 0.5 * jnp.tanh(0.5 * x) + 0.5


def _mm_body(pairs, n_in, n_out, n_acc, dims, nk, epilogue, *refs):
    ins = refs[:n_in]
    outs = refs[n_in:n_in + n_out]
    accs = refs[n_in + n_out:]
    i = pl.program_id(0)
    k = pl.program_id(2)

    def operand(a):
        return (a(ins) if callable(a) else ins[a][...]).astype(BF16)

    def dot(ia, ib):
        return lax.dot_general(operand(ia), operand(ib), dims, preferred_element_type=F32)

    if nk == 1:
        parts = [None] * n_acc
        for ia, ib, ic in pairs:
            parts[ic] = dot(ia, ib) if parts[ic] is None else parts[ic] + dot(ia, ib)
        epilogue(parts, ins, outs, i)
        return

    @pl.when(k == 0)
    def _():
        for c in range(n_acc):
            accs[c][...] = jnp.zeros_like(accs[c])

    for ia, ib, ic in pairs:
        accs[ic][...] += dot(ia, ib)

    @pl.when(k == nk - 1)
    def _():
        epilogue([a[...] for a in accs], ins, outs, i)


def _mm(name, ins, in_specs, pairs, n_acc, acc_shape, dims, grid, epilogue, out_shapes, out_specs):
    nk = grid[2]
    scratch = [pltpu.VMEM(acc_shape, F32) for _ in range(n_acc)] if nk > 1 else []
    body = functools.partial(_mm_body, tuple(pairs), len(ins), len(out_shapes), n_acc, dims, nk, epilogue)
    return pl.pallas_call(
        body, name=name, grid=grid, in_specs=in_specs, out_specs=out_specs, out_shape=out_shapes,
        scratch_shapes=scratch,
        compiler_params=_params(dimension_semantics=("arbitrary", "arbitrary", "arbitrary")),
    )(*ins)


def _rms_bwd_epilogue(x_idx, g_idx, dxo_idx):
    def ep(vals, ins, outs, i):
        dh = vals[0]
        x = ins[x_idx][...]
        g = ins[g_idx][...]
        rstd = lax.rsqrt(jnp.mean(x * x, axis=-1, keepdims=True) + RMS_EPS)
        xhat = x * rstd
        dxhat = dh * g
        dx = rstd * (dxhat - xhat * jnp.mean(dxhat * xhat, axis=-1, keepdims=True))
        outs[0][...] = ins[dxo_idx][...] + dx
        dg = jnp.broadcast_to(jnp.sum(dh * xhat, axis=0, keepdims=True), outs[1].shape)

        @pl.when(i == 0)
        def _():
            outs[1][...] = dg

        @pl.when(i > 0)
        def _():
            outs[1][...] += dg
    return ep


def _normed(x_idx, g_idx):
    seen = {}

    def f(ins):
        if id(ins) not in seen:
            xv = ins[x_idx][...]
            h = xv * lax.rsqrt(jnp.mean(xv * xv, axis=-1, keepdims=True) + RMS_EPS)
            seen[id(ins)] = (ins, (h * ins[g_idx][...]).astype(BF16))
        return seen[id(ins)][1]
    return f


def _rope_tables(T):
    half = ROPE_DIM // 2
    lane = jnp.arange(128) % HEAD_DIM
    inv_freq = ROPE_THETA ** (-(2 * (lane % half)).astype(F32) / ROPE_DIM)
    ang = jnp.arange(T, dtype=F32)[:, None] * inv_freq[None, :]
    cos, sin = jnp.cos(ang), jnp.sin(ang)
    c = jnp.where(lane < ROPE_DIM, cos, 1.0)
    s1 = jnp.where(lane < half, -sin, 0.0)
    s2 = jnp.where((lane >= half) & (lane < ROPE_DIM), sin, 0.0)
    return c, s1, s2


def _rope_fwd(xv, c, s1, s2):
    w = xv.shape[1]
    half = ROPE_DIM // 2
    return xv * c + pltpu.roll(xv, w - half, 1) * s1 + pltpu.roll(xv, half, 1) * s2


def _rope_bwd(dy, c, s1, s2):
    w = dy.shape[1]
    half = ROPE_DIM // 2
    return dy * c + pltpu.roll(dy * s1, half, 1) + pltpu.roll(dy * s2, w - half, 1)


def _assemble_dproj(dqk, rest, gates, tabs):
    T = gates[0].shape[0]
    n_qk, n_rest = len(dqk), len(rest)
    width = (n_qk + n_rest) * D_ATT + 2 * D_MODEL

    def body(*refs):
        ins, (c_ref, s1_ref, s2_ref), o_ref = refs[:n_qk + n_rest + 2], refs[-4:-1], refs[-1]
        c = jnp.concatenate([c_ref[...]] * 2, axis=1)
        s1 = jnp.concatenate([s1_ref[...]] * 2, axis=1)
        s2 = jnp.concatenate([s2_ref[...]] * 2, axis=1)
        for b in range(n_qk + n_rest):
            v = ins[b][...]
            if b < n_qk:
                v = _rope_bwd(v, c, s1, s2)
            o_ref[:, b * D_ATT:(b + 1) * D_ATT] = v.astype(BF16)
        off = (n_qk + n_rest) * D_ATT
        o_ref[:, off:off + D_MODEL] = ins[-2][...]
        o_ref[:, off + D_MODEL:] = ins[-1][...]

    att = pl.BlockSpec((TM, D_ATT), lambda i: (i, 0))
    wide = pl.BlockSpec((TM, D_MODEL), lambda i: (i, 0))
    tab = pl.BlockSpec((TM, 128), lambda i: (i, 0))
    return pl.pallas_call(
        body, name="assemble_dproj", grid=(T // TM,),
        in_specs=[att] * (n_qk + n_rest) + [wide, wide, tab, tab, tab],
        out_specs=pl.BlockSpec((TM, width), lambda i: (i, 0)),
        out_shape=jax.ShapeDtypeStruct((T, width), BF16), compiler_params=_params(),
    )(*dqk, *rest, *gates, *tabs)


def _dil_merge(os_, lses):
    T = os_[0].shape[0]

    def body(o0, o1, o2, l0, l1, l2, o_ref, lse_ref):
        a, b, c = l0[...], l1[...], l2[...]
        m = jnp.maximum(jnp.maximum(a, b), c)
        ea, eb, ec = jnp.exp(a - m), jnp.exp(b - m), jnp.exp(c - m)
        den = ea + eb + ec
        o_ref[...] = (ea * o0[...] + eb * o1[...] + ec * o2[...]) / den
        lse_ref[...] = m + jnp.log(den)

    blk = pl.BlockSpec((TM, D_ATT), lambda i: (i, 0))
    sh = jax.ShapeDtypeStruct((T, D_ATT), F32)
    return pl.pallas_call(
        body, name="dil_merge", grid=(T // TM,), in_specs=[blk] * 6, out_specs=[blk, blk],
        out_shape=[sh, sh], compiler_params=_params(),
    )(*os_, *lses)


def _final_loss(x, gain, target):
    T = x.shape[0]

    def body(x_ref, g_ref, t_ref, dx_ref, dg_ref, loss_ref):
        xv = x_ref[...]
        g = g_ref[...]
        rstd = lax.rsqrt(jnp.mean(xv * xv, axis=-1, keepdims=True) + RMS_EPS)
        xhat = xv * rstd
        err = xhat * g - t_ref[...]
        loss = 0.5 * jnp.sum(jnp.mean(err * err, axis=-1, keepdims=True), axis=0, keepdims=True)
        dy = err * (1.0 / D_MODEL)
        dxhat = dy * g
        dx_ref[...] = rstd * (dxhat - xhat * jnp.mean(dxhat * xhat, axis=-1, keepdims=True))
        dg = jnp.broadcast_to(jnp.sum(dy * xhat, axis=0, keepdims=True), dg_ref.shape)
        ls = jnp.broadcast_to(loss, loss_ref.shape)

        @pl.when(pl.program_id(0) == 0)
        def _():
            dg_ref[...] = dg
            loss_ref[...] = ls

        @pl.when(pl.program_id(0) > 0)
        def _():
            dg_ref[...] += dg
            loss_ref[...] += ls

    blk = pl.BlockSpec((TM, D_MODEL), lambda i: (i, 0))
    row = pl.BlockSpec((1, D_MODEL), lambda i: (0, 0))
    acc = pl.BlockSpec((8, D_MODEL), lambda i: (0, 0))
    return pl.pallas_call(
        body, name="final_loss", grid=(T // TM,), in_specs=[blk, row, blk], out_specs=[blk, acc, acc],
        out_shape=[jax.ShapeDtypeStruct((T, D_MODEL), F32), jax.ShapeDtypeStruct((8, D_MODEL), F32),
                   jax.ShapeDtypeStruct((8, D_MODEL), F32)],
        compiler_params=_params(dimension_semantics=("arbitrary",)),
    )(x, gain, target)


def _pair_masks():
    lane = lax.broadcasted_iota(jnp.int32, (SPAN, 128), 1)
    return [lane < HEAD_DIM, lane >= HEAD_DIM]


def _stack_heads(x, masks):
    return jnp.concatenate([jnp.where(m, x, 0.0) for m in masks], axis=0)


def _unstack_heads(y, masks):
    rows = y.shape[0] // len(masks)
    out = jnp.where(masks[0], y[:rows], 0.0)
    for h in range(1, len(masks)):
        out = out + jnp.where(masks[h], y[rows * h:rows * (h + 1)], 0.0)
    return out


DIL_PAIR = 2


def _dil_rows(idx, d):
    u = idx // d
    r = idx - u * d
    own = pl.ds(u * (SPAN * d) + r, SPAN, stride=d) if d > 1 else pl.ds(pl.multiple_of(u * SPAN, SPAN), SPAN)
    up = jnp.maximum(u - 1, 0)
    prev = pl.ds(up * (SPAN * d) + r, SPAN, stride=d) if d > 1 else pl.ds(pl.multiple_of(up * SPAN, SPAN), SPAN)
    return u, own, prev


def _dil_valid(u):
    qi = lax.broadcasted_iota(jnp.int32, (2 * SPAN, 2 * SPAN), 0) & (SPAN - 1)
    kj = lax.broadcasted_iota(jnp.int32, (2 * SPAN, 2 * SPAN), 1)
    in_prev = (kj < SPAN) & (kj >= qi + jnp.where(u > 0, 0, SPAN))
    return in_prev | ((kj >= SPAN) & (kj - SPAN <= qi))


def _dil_keys(ref, own, prev):
    return jnp.concatenate([ref[prev, :], ref[own, :]], axis=0).astype(BF16)


def _dil_fwd(proj, g, d):
    T = proj.shape[0]
    n_iter = T // SPAN

    def body(q_ref, k_ref, v_ref, o_ref, lse_ref):
        masks = _pair_masks()

        def step(pair, carry):
            its = [_dil_rows(DIL_PAIR * pair + e, d) for e in range(DIL_PAIR)]
            qs = [_stack_heads(q_ref[own, :] * (HEAD_DIM ** -0.5), masks).astype(BF16) for _, own, _ in its]
            kks = [_dil_keys(k_ref, own, prev) for _, own, prev in its]
            vvs = [_dil_keys(v_ref, own, prev) for _, own, prev in its]
            ss = [jnp.where(_dil_valid(u), lax.dot_general(q, kk, NT, preferred_element_type=F32), NEG)
                  for (u, _, _), q, kk in zip(its, qs, kks)]
            ms = [jnp.max(s, axis=1, keepdims=True) for s in ss]
            ps = [jnp.exp(s - m) for s, m in zip(ss, ms)]
            dens = [jnp.sum(p, axis=1, keepdims=True) for p in ps]
            pvs = [lax.dot_general(p.astype(BF16), vv, NN, preferred_element_type=F32) / den
                   for p, vv, den in zip(ps, vvs, dens)]
            for (_, own, _), pv, m, den in zip(its, pvs, ms, dens):
                o_ref[own, :] = _unstack_heads(pv, masks)
                lse_ref[own, :] = _unstack_heads(jnp.broadcast_to(m + jnp.log(den), pv.shape), masks)
            return carry

        lax.fori_loop(0, n_iter // DIL_PAIR, step, 0)

    def col(b):
        return pl.BlockSpec((T, 128), lambda p: (0, b + p))

    sh = jax.ShapeDtypeStruct((T, D_ATT), F32)
    out = pl.BlockSpec((T, 128), lambda p: (0, p))
    return pl.pallas_call(
        body, name=f"dil_fwd_d{d}", grid=(2,),
        in_specs=[col(2 * g), col(6 + 2 * g), col(12 + 2 * g)], out_specs=[out, out], out_shape=[sh, sh],
        compiler_params=_params(dimension_semantics=("arbitrary",)),
    )(proj, proj, proj)


def _dil_bwd(proj, do, o_dil, lse, g, d):
    T = proj.shape[0]
    n_iter = T // SPAN

    def body(q_ref, k_ref, v_ref, do_ref, o_ref, lse_ref, dq_ref, dk_ref, dv_ref):
        masks = _pair_masks()
        head_lanes = jnp.concatenate(masks, axis=0)

        def step(pair, carry):
            its = [_dil_rows(DIL_PAIR * pair + e, d) for e in range(DIL_PAIR)]
            qs = [_stack_heads(q_ref[own, :] * (HEAD_DIM ** -0.5), masks).astype(BF16) for _, own, _ in its]
            kks = [_dil_keys(k_ref, own, prev) for _, own, prev in its]
            vvs = [_dil_keys(v_ref, own, prev) for _, own, prev in its]
            doms = [_stack_heads(do_ref[own, :], masks) for _, own, _ in its]
            dos = [dom.astype(BF16) for dom in doms]
            deltas = [jnp.sum(dom * jnp.concatenate([o_ref[own, :]] * 2, axis=0), axis=1, keepdims=True)
                      for dom, (_, own, _) in zip(doms, its)]
            lrows = [jnp.max(jnp.where(head_lanes, jnp.concatenate([lse_ref[own, :]] * 2, axis=0), NEG),
                             axis=1, keepdims=True) for _, own, _ in its]
            ss = [lax.dot_general(q, kk, NT, preferred_element_type=F32) for q, kk in zip(qs, kks)]
            dps = [lax.dot_general(do_b, vv, NT, preferred_element_type=F32) for do_b, vv in zip(dos, vvs)]
            ps = [jnp.where(_dil_valid(u), jnp.exp(s - lrow), 0.0) for (u, _, _), s, lrow in zip(its, ss, lrows)]
            dss = [(p * (dp - delta)).astype(BF16) for p, dp, delta in zip(ps, dps, deltas)]
            dqs = [lax.dot_general(ds, kk, NN, preferred_element_type=F32) for ds, kk in zip(dss, kks)]
            dkks = [lax.dot_general(ds, q, TN, preferred_element_type=F32) for ds, q in zip(dss, qs)]
            dvvs = [lax.dot_general(p.astype(BF16), do_b, TN, preferred_element_type=F32) for p, do_b in zip(ps, dos)]
            for (_, own, prev), dq, dkk, dvv in zip(its, dqs, dkks, dvvs):
                dq_ref[own, :] = _unstack_heads(dq, masks) * (HEAD_DIM ** -0.5)
                dk_ref[own, :] = dkk[SPAN:]
                dv_ref[own, :] = dvv[SPAN:]
                dk_ref[prev, :] = dk_ref[prev, :] + dkk[:SPAN]
                dv_ref[prev, :] = dv_ref[prev, :] + dvv[:SPAN]
            return carry

        lax.fori_loop(0, n_iter // DIL_PAIR, step, 0)

    def col(b):
        return pl.BlockSpec((T, 128), lambda p: (0, b + p))

    sh = jax.ShapeDtypeStruct((T, D_ATT), F32)
    return pl.pallas_call(
        body, name=f"dil_bwd_d{d}", grid=(2,),
        in_specs=[col(2 * g), col(6 + 2 * g), col(12 + 2 * g), col(0), col(0), col(0)],
        out_specs=[col(0), col(0), col(0)], out_shape=[sh, sh, sh],
        compiler_params=_params(dimension_semantics=("arbitrary",)),
    )(proj, proj, proj, do, o_dil, lse)


SB_KT = 512


def _sb_tri(strict):
    a = lax.broadcasted_iota(jnp.int32, (Q_BLOCK, Q_BLOCK), 0)
    b = lax.broadcasted_iota(jnp.int32, (Q_BLOCK, Q_BLOCK), 1)
    return jnp.where((a > b) if strict else (a >= b), 1.0, 0.0).astype(BF16)


def _split_stack(x):
    nb = x.shape[1] // Q_BLOCK
    blocks = [x[:, Q_BLOCK * b:Q_BLOCK * (b + 1)] for b in range(nb)]
    hi = [b.astype(BF16) for b in blocks]
    lo = [(b - h.astype(F32)).astype(BF16) for b, h in zip(blocks, hi)]
    return blocks, jnp.concatenate(hi + lo, axis=0)


def _suffix_from(y, blocks, c):
    r = blocks[0].shape[0]
    nb = len(blocks)
    outs = [None] * nb
    run = c
    for b in reversed(range(nb)):
        outs[b] = run + y[r * b:r * (b + 1)] + y[r * (nb + b):r * (nb + b + 1)]
        run = run + jnp.sum(blocks[b], axis=1, keepdims=True)
    return jnp.concatenate(outs, axis=1), run


SB_HEADS = D_ATT // HEAD_DIM
SB_FWD_CHAINS = 2
SB_BWD_CHAINS = 1


def _sb_past(i, t, rows):
    row = lax.broadcasted_iota(jnp.int32, (rows, SB_KT), 0) & (Q_BLOCK - 1)
    col = lax.broadcasted_iota(jnp.int32, (rows, SB_KT), 1)
    return col + t * SB_KT < row + i * Q_BLOCK


def _sb_head_masks(chains):
    lane = lax.broadcasted_iota(jnp.int32, (Q_BLOCK, D_ATT), 1)
    masks = [(lane >= HEAD_DIM * h) & (lane < HEAD_DIM * (h + 1)) for h in range(SB_HEADS)]
    per = SB_HEADS // chains
    return [masks[per * g:per * (g + 1)] for g in range(chains)]


def _sb_rows(t):
    return pl.ds(pl.multiple_of(t * SB_KT, SB_KT), SB_KT)


def _sb_log_terms(z, past):
    lsz = jnp.minimum(z, 0.0) - jnp.log(1.0 + jnp.exp(-jnp.abs(z)))
    lk = lsz - z
    return lsz, (lk if past is None else jnp.where(past, lk, 0.0))


def _sb_weights(lsz, after, past):
    w = jnp.exp(lsz + after)
    return w if past is None else jnp.where(past, w, 0.0)


def _sb_fwd(proj):
    T = proj.shape[0]

    def body(q_ref, k_ref, v_ref, o_ref):
        i = pl.program_id(0)
        masks = _sb_head_masks(SB_FWD_CHAINS)
        rows = SB_HEADS // SB_FWD_CHAINS * Q_BLOCK
        tri = _sb_tri(True)
        q = q_ref[...] * (HEAD_DIM ** -0.5)
        qs = [_stack_heads(q, m).astype(BF16) for m in masks]
        n_tiles = (i * Q_BLOCK) // SB_KT + 1

        def tile(t, carry, masked):
            kb = k_ref[_sb_rows(t), :].astype(BF16)
            vb = v_ref[_sb_rows(t), :].astype(BF16)
            past = _sb_past(i, t, rows) if masked else None
            acc, cs = carry[0], carry[1:]
            zs = [lax.dot_general(g, kb, NT, preferred_element_type=F32) for g in qs]
            logs = [_sb_log_terms(z, past) for z in zs]
            splits = [_split_stack(lk) for _, lk in logs]
            ys = [lax.dot_general(x, tri, NN, preferred_element_type=F32) for _, x in splits]
            sums = [_suffix_from(y, blocks, c) for y, (blocks, _), c in zip(ys, splits, cs)]
            ws = [_sb_weights(lsz, after, past).astype(BF16) for (lsz, _), (after, _) in zip(logs, sums)]
            for m, w in zip(masks, ws):
                acc = acc + _unstack_heads(lax.dot_general(w, vb, NN, preferred_element_type=F32), m)
            return (acc, *[c for _, c in sums])

        zcol = jnp.zeros((rows, 1), F32)
        carry = tile(n_tiles - 1, (jnp.zeros((Q_BLOCK, D_ATT), F32),) + (zcol,) * SB_FWD_CHAINS, True)
        carry = lax.fori_loop(0, n_tiles - 1, lambda tt, cr: tile(n_tiles - 2 - tt, cr, False), carry)
        o_ref[...] = carry[0]

    cb = COL_QS // D_ATT
    return pl.pallas_call(
        body, name="sb_fwd", grid=(T // Q_BLOCK,),
        in_specs=[pl.BlockSpec((Q_BLOCK, D_ATT), lambda i: (i, cb)),
                  pl.BlockSpec((T, D_ATT), lambda i: (0, cb + 1)),
                  pl.BlockSpec((T, D_ATT), lambda i: (0, cb + 2))],
        out_specs=pl.BlockSpec((Q_BLOCK, D_ATT), lambda i: (i, 0)),
        out_shape=jax.ShapeDtypeStruct((T, D_ATT), F32),
        compiler_params=_params(dimension_semantics=("arbitrary",)),
    )(proj, proj, proj)


def _sb_bwd(proj, do, o):
    T = proj.shape[0]

    def body(q_ref, k_ref, v_ref, do_ref, o_ref, dq_ref, dk_ref, dv_ref):
        i = pl.program_id(0)
        masks = _sb_head_masks(SB_BWD_CHAINS)
        n_rows = SB_HEADS // SB_BWD_CHAINS * Q_BLOCK
        tri = _sb_tri(True)
        tri_incl = _sb_tri(False)

        @pl.when(i == 0)
        def _():
            dk_ref[...] = jnp.zeros_like(dk_ref)
            dv_ref[...] = jnp.zeros_like(dv_ref)

        q = q_ref[...] * (HEAD_DIM ** -0.5)
        qs = [_stack_heads(q, m).astype(BF16) for m in masks]
        dos = [_stack_heads(do_ref[...], m).astype(BF16) for m in masks]
        o_rep = jnp.concatenate([o_ref[...]] * (SB_HEADS // SB_BWD_CHAINS), axis=0)
        deltas = [jnp.sum(d.astype(F32) * o_rep, axis=1, keepdims=True) for d in dos]
        n_tiles = (i * Q_BLOCK) // SB_KT + 1

        def tile(t, carry, masked):
            rows = _sb_rows(t)
            kb = k_ref[rows, :].astype(BF16)
            vb = v_ref[rows, :].astype(BF16)
            past = _sb_past(i, t, n_rows) if masked else None
            dq, cs, ces = carry[0], carry[1:1 + SB_BWD_CHAINS], carry[1 + SB_BWD_CHAINS:]
            zs = [lax.dot_general(g, kb, NT, preferred_element_type=F32) for g in qs]
            gvs = [lax.dot_general(d, vb, NT, preferred_element_type=F32) for d in dos]
            logs = [_sb_log_terms(z, past) for z in zs]
            splits = [_split_stack(lk) for _, lk in logs]
            ys = [lax.dot_general(x, tri, NN, preferred_element_type=F32) for _, x in splits]
            sums = [_suffix_from(y, blocks, c) for y, (blocks, _), c in zip(ys, splits, cs)]
            wbs = [_sb_weights(lsz, after, past).astype(BF16) for (lsz, _), (after, _) in zip(logs, sums)]
            es = [wb.astype(F32) * gv for wb, gv in zip(wbs, gvs)]
            esplits = [_split_stack(e) for e in es]
            eys = [lax.dot_general(x, tri_incl, NN, preferred_element_type=F32) for _, x in esplits]
            esums = [_suffix_from(y, blocks, ce) for y, (blocks, _), ce in zip(eys, esplits, ces)]
            dzbs = []
            for e, (lsz, lk), (suf, _), delta in zip(es, logs, esums, deltas):
                dz = e * jnp.exp(lk) - (delta - suf) * jnp.exp(lsz)
                dzbs.append((dz if past is None else jnp.where(past, dz, 0.0)).astype(BF16))
            dk_t = dv_t = None
            for m, dzb, wb, g, d in zip(masks, dzbs, wbs, qs, dos):
                dq = dq + _unstack_heads(lax.dot_general(dzb, kb, NN, preferred_element_type=F32), m)
                a = lax.dot_general(dzb, g, TN, preferred_element_type=F32)
                b = lax.dot_general(wb, d, TN, preferred_element_type=F32)
                dk_t = a if dk_t is None else dk_t + a
                dv_t = b if dv_t is None else dv_t + b
            dk_ref[rows, :] = dk_ref[rows, :] + dk_t
            dv_ref[rows, :] = dv_ref[rows, :] + dv_t
            return (dq, *[c for _, c in sums], *[c for _, c in esums])

        zcol = jnp.zeros((n_rows, 1), F32)
        carry = tile(n_tiles - 1, (jnp.zeros((Q_BLOCK, D_ATT), F32),) + (zcol,) * (2 * SB_BWD_CHAINS), True)
        carry = lax.fori_loop(0, n_tiles - 1, lambda tt, cr: tile(n_tiles - 2 - tt, cr, False), carry)
        dq_ref[...] = carry[0] * (HEAD_DIM ** -0.5)

    cb = COL_QS // D_ATT
    blk = pl.BlockSpec((Q_BLOCK, D_ATT), lambda i: (i, 0))
    full = pl.BlockSpec((T, D_ATT), lambda i: (0, 0))
    sh = jax.ShapeDtypeStruct((T, D_ATT), F32)
    return pl.pallas_call(
        body, name="sb_bwd", grid=(T // Q_BLOCK,),
        in_specs=[pl.BlockSpec((Q_BLOCK, D_ATT), lambda i: (i, cb)),
                  pl.BlockSpec((T, D_ATT), lambda i: (0, cb + 1)),
                  pl.BlockSpec((T, D_ATT), lambda i: (0, cb + 2)), blk, blk],
        out_specs=[blk, full, full], out_shape=[sh, sh, sh],
        compiler_params=_params(dimension_semantics=("arbitrary",)),
    )(proj, proj, proj, do, o)


def _tok(c, by=None):
    if by is None:
        return pl.BlockSpec((TM, c), lambda i, j, k: (i, 0))
    if by == 1:
        return pl.BlockSpec((TM, c), lambda i, j, k: (i, j))
    return pl.BlockSpec((TM, c), lambda i, j, k: (i, k))


def _gain_spec():
    return pl.BlockSpec((1, D_MODEL), lambda i, j, k: (0, 0))


def _wfull(r, c, l):
    return pl.BlockSpec((N_CHIPS, None, r, c), lambda i, j, k: (0, l, 0, 0), pipeline_mode=pl.Buffered(1))


def _pick(idx, c):
    return lambda ins: ins[idx][c]


def _cols(idx, c, w):
    return lambda ins: ins[idx][:, c * w:(c + 1) * w]


def _rows(rows, width):
    return pl.BlockSpec((rows, width), lambda i, j, k: (i, 0))


def _whole(shape):
    return pl.BlockSpec(shape, lambda i, j, k: (0, 0), pipeline_mode=pl.Buffered(1))


def _ffn_fwd(x, gain, wg, wu, wd):
    T = x.shape[0]
    wg, wu, wd = (w.reshape(-1, D_MODEL) for w in (wg, wu, wd))
    ff = wd.shape[0]
    tm = TM
    normed = _normed(0, 3)

    def swiglu(vals, ins, outs, i):
        gt, up = vals
        s = _sigmoid(gt)
        sil = gt * s
        outs[0][...] = sil.astype(BF16)
        outs[1][...] = (up * (s * (1.0 + gt * (1.0 - s)))).astype(BF16)
        outs[2][...] = (sil * up).astype(BF16)
        outs[3][...] = normed(ins)

    ash = jax.ShapeDtypeStruct((T, ff), BF16)
    sil, up_dsil, act, h = _mm(
        "ffn_up", [x, wg, wu, gain], [_rows(tm, D_MODEL), _whole(wg.shape), _whole(wu.shape), _gain_spec()],
        [(normed, 1, 0), (normed, 2, 1)], 2, None, NT, (T // tm, 1, 1), swiglu,
        [ash] * 3 + [jax.ShapeDtypeStruct((T, D_MODEL), BF16)], [_rows(tm, ff)] * 3 + [_rows(tm, D_MODEL)])

    def resid(vals, ins, outs, i):
        outs[0][...] = ins[2][...] + 0.5 * vals[0]

    (y,) = _mm(
        "ffn_down", [act, wd, x], [_rows(TM, ff), _whole(wd.shape), _tok(D_MODEL)], [(0, 1, 0)], 1, None, NN,
        (T // TM, 1, 1), resid, [jax.ShapeDtypeStruct((T, D_MODEL), F32)], [_tok(D_MODEL)])
    return y, (x, h, sil, up_dsil, act)


def _ffn_bwd(dxo, gain, wg, wu, wd, saved):
    x, h, sil, up_dsil, act = saved
    T = x.shape[0]
    n_chips, _, ffs, _ = wd.shape
    wg, wu, wd = (w.reshape(-1, D_MODEL) for w in (wg, wu, wd))
    ff = wd.shape[0]
    tk = TM
    tm = TM

    def dswiglu(vals, ins, outs, i):
        da = 0.5 * vals[0]
        outs[0][...] = (da * ins[3][...].astype(F32)).astype(BF16)
        outs[1][...] = (da * ins[2][...].astype(F32)).astype(BF16)

    ash = jax.ShapeDtypeStruct((T, ff), BF16)
    dgate, dup = _mm(
        "ffn_dact", [dxo, wd, sil, up_dsil], [_rows(tm, D_MODEL), _whole(wd.shape), _rows(tm, ff), _rows(tm, ff)],
        [(0, 1, 0)], 1, None, NT, (T // tm, 1, 1), dswiglu, [ash, ash], [_rows(tm, ff)] * 2)

    def half(vals, ins, outs, i):
        outs[0][...] = (0.5 * vals[0]).astype(BF16)

    def cast(vals, ins, outs, i):
        outs[0][...] = vals[0].astype(BF16)

    tok_k = pl.BlockSpec((tk, D_MODEL), lambda i, j, k: (k, 0))
    hid_k = pl.BlockSpec((tk, ff), lambda i, j, k: (k, 0))
    wsh = jax.ShapeDtypeStruct((ff, D_MODEL), BF16)
    (dwd,) = _mm("ffn_dwd", [act, dxo], [hid_k, tok_k], [(0, 1, 0)], 1, (ff, D_MODEL), TN, (1, 1, T // tk), half,
                 [wsh], [_whole((ff, D_MODEL))])

    dx, dgain = _mm(
        "ffn_dx", [dgate, dup, wg, wu, x, gain, dxo],
        [_rows(tm, ff), _rows(tm, ff), _whole(wg.shape), _whole(wu.shape), _rows(tm, D_MODEL), _gain_spec(),
         _rows(tm, D_MODEL)],
        [(0, 2, 0), (1, 3, 0)], 1, None, NN, (T // tm, 1, 1), _rms_bwd_epilogue(4, 5, 6),
        [jax.ShapeDtypeStruct((T, D_MODEL), F32), jax.ShapeDtypeStruct((8, D_MODEL), F32)],
        [_rows(tm, D_MODEL), pl.BlockSpec((8, D_MODEL), lambda i, j, k: (0, 0))])

    dws = []
    for dact in (dgate, dup):
        dws += _mm("ffn_dwgu", [dact, h], [hid_k, tok_k], [(0, 1, 0)], 1, (ff, D_MODEL), TN, (1, 1, T // tk), cast,
                   [wsh], [_whole((ff, D_MODEL))])
    dwg, dwu, dwd = (w.reshape(n_chips, ffs, D_MODEL) for w in (dws[0], dws[1], dwd))
    return dx, dgain, dwg, dwu, dwd


def _joined_mixer_weights(wpd, wps, wo):
    n, _, r, c = wpd.shape
    wpd_n, wps_n = (w[:, 0].transpose(1, 0, 2).reshape(r, n * c) for w in (wpd, wps))
    return wpd_n, wps_n, wo.reshape(-1, wo.shape[3])


def _mixer_fwd(x, gain, W, l, tabs):
    T = x.shape[0]
    win, wpd, wps, wo = W["w_in"], W["w_proj_dil"], W["w_proj_sb"], W["w_out"]
    cin = win.shape[3]
    cp = wpd.shape[3]
    normed = _normed(0, 5)
    n_rope = 6 * D_ATT

    tm = TM // 2

    def roped(vals, ins, outs, i):
        for j, v in enumerate(vals):
            lo = j * cin
            k = min(max(n_rope - lo, 0), cin)
            if k:
                tab = [jnp.concatenate([ins[t][...]] * (k // 128), axis=1) for t in (2, 3, 4)]
                outs[0][:, lo:lo + k] = _rope_fwd(v[:, :k], *tab)
            if k < cin:
                outs[0][:, lo + k:lo + cin] = v[:, k:]
        outs[1][...] = normed(ins)

    proj, h = _mm(
        "mix_in", [x, win, *tabs, gain],
        [_rows(tm, D_MODEL), _wfull(D_MODEL, cin, l)] + [_rows(tm, 128)] * 3 + [_gain_spec()],
        [(normed, _pick(1, c), c) for c in range(N_CHIPS)], N_CHIPS, None, NN, (T // tm, 1, 1), roped,
        [jax.ShapeDtypeStruct((T, N_CHIPS * cin), F32), jax.ShapeDtypeStruct((T, D_MODEL), BF16)],
        [_rows(tm, N_CHIPS * cin), _rows(tm, D_MODEL)])

    os_, lses = [], []
    for g, (window, dil) in enumerate(DIL_GROUPS):
        o_g, lse_g = _dil_fwd(proj, g, dil)
        os_.append(o_g)
        lses.append(lse_g)
    o_dil, lse = _dil_merge(os_, lses)
    o_sb = _sb_fwd(proj)

    def gated(vals, ins, outs, i):
        pd, ps = vals
        outs[0][...] = (_sigmoid(ins[4][...]) * pd + _sigmoid(ins[5][...]) * ps).astype(BF16)
        outs[1][...] = pd.astype(BF16)
        outs[2][...] = ps.astype(BF16)

    wpd_n, wps_n, wo_n = _joined_mixer_weights(wpd, wps, wo)
    gd_spec = pl.BlockSpec((TM, D_MODEL), lambda i, j, k: (i, COL_GD // D_MODEL))
    gs_spec = pl.BlockSpec((TM, D_MODEL), lambda i, j, k: (i, COL_GS // D_MODEL))
    ush = jax.ShapeDtypeStruct((T, D_MODEL), BF16)
    u, pd, ps = _mm(
        "mix_gate", [o_dil, o_sb, wpd_n, wps_n, proj, proj],
        [_tok(D_ATT), _tok(D_ATT), _whole(wpd_n.shape), _whole(wps_n.shape), gd_spec, gs_spec],
        [(0, 2, 0), (1, 3, 1)], 2, None, NN, (T // TM, 1, 1), gated, [ush] * 3, [_tok(D_MODEL)] * 3)

    def resid(vals, ins, outs, i):
        outs[0][...] = ins[2][...] + vals[0]

    (y,) = _mm(
        "mix_out", [u, wo_n, x], [_tok(D_MODEL), _whole(wo_n.shape), _tok(D_MODEL)], [(0, 1, 0)], 1, None, NN,
        (T // TM, 1, 1), resid, [jax.ShapeDtypeStruct((T, D_MODEL), F32)], [_tok(D_MODEL)])
    return y, (x, h, proj, o_dil, lse, o_sb, u, pd, ps)


def _mixer_bwd(dxo, gain, W, l, tabs, saved):
    x, h, proj, o_dil, lse, o_sb, u, pd, ps = saved
    T = x.shape[0]
    win, wpd, wps, wo = W["w_in"], W["w_proj_dil"], W["w_proj_sb"], W["w_out"]
    cin = win.shape[3]
    cp = wpd.shape[3]
    tk = TM
    tm = TM
    row = pl.BlockSpec((tm, D_MODEL), lambda i, j, k: (i, 0))

    def dgated(vals, ins, outs, i):
        du = vals[0]
        sd = _sigmoid(ins[4][...])
        ss = _sigmoid(ins[5][...])
        outs[0][...] = (du * sd).astype(BF16)
        outs[1][...] = (du * ss).astype(BF16)
        outs[2][...] = (du * ins[2][...].astype(F32) * sd * (1.0 - sd)).astype(BF16)
        outs[3][...] = (du * ins[3][...].astype(F32) * ss * (1.0 - ss)).astype(BF16)

    wpd_n, wps_n, wo_n = _joined_mixer_weights(wpd, wps, wo)
    gd_spec = pl.BlockSpec((TM, D_MODEL), lambda i, j, k: (i, COL_GD // D_MODEL))
    gs_spec = pl.BlockSpec((TM, D_MODEL), lambda i, j, k: (i, COL_GS // D_MODEL))
    ush = jax.ShapeDtypeStruct((T, D_MODEL), BF16)
    dpd, dps, dgd, dgs = _mm(
        "mix_du", [dxo, wo_n, pd, ps, proj, proj],
        [_tok(D_MODEL), _whole(wo_n.shape), _tok(D_MODEL), _tok(D_MODEL), gd_spec, gs_spec],
        [(0, 1, 0)], 1, None, NT, (T // TM, 1, 1), dgated, [ush] * 4, [_tok(D_MODEL)] * 4)

    def one(vals, ins, outs, i):
        outs[0][...] = vals[0].astype(BF16)

    def two(vals, ins, outs, i):
        outs[0][...] = vals[0].astype(BF16)
        outs[1][...] = vals[1].astype(BF16)

    tok_k = pl.BlockSpec((tk, D_MODEL), lambda i, j, k: (k, 0))
    att_k = pl.BlockSpec((tk, D_ATT), lambda i, j, k: (k, 0))
    (dwo_n,) = _mm("mix_dwo", [u, dxo], [tok_k, tok_k], [(0, 1, 0)], 1, (D_MODEL, D_MODEL), TN, (1, 1, T // tk), one,
                   [jax.ShapeDtypeStruct((D_MODEL, D_MODEL), BF16)], [_whole((D_MODEL, D_MODEL))])

    def plain2(vals, ins, outs, i):
        outs[0][...] = vals[0]
        outs[1][...] = vals[1]

    ash = jax.ShapeDtypeStruct((T, D_ATT), F32)
    do_dil, do_sb = _mm(
        "mix_do", [dpd, dps, wpd_n, wps_n], [_tok(D_MODEL), _tok(D_MODEL), _whole(wpd_n.shape), _whole(wps_n.shape)],
        [(0, 2, 0), (1, 3, 1)], 2, None, NT, (T // TM, 1, 1), plain2, [ash, ash], [_tok(D_ATT)] * 2)

    psh = jax.ShapeDtypeStruct((D_ATT, D_MODEL), BF16)
    dwpd_n, dwps_n = _mm(
        "mix_dwp", [o_dil, o_sb, dpd, dps], [att_k, att_k, tok_k, tok_k], [(0, 2, 0), (1, 3, 1)], 2,
        (D_ATT, D_MODEL), TN, (1, 1, T // tk), two, [psh, psh], [_whole((D_ATT, D_MODEL))] * 2)
    dwpd, dwps = (w.reshape(D_ATT, N_CHIPS, cp).transpose(1, 0, 2) for w in (dwpd_n, dwps_n))
    dwo = dwo_n.reshape(N_CHIPS, cp, D_MODEL)

    dqs, dks, dvs = [], [], []
    for g, (window, dil) in enumerate(DIL_GROUPS):
        dq, dk, dv = _dil_bwd(proj, do_dil, o_dil, lse, g, dil)
        dqs.append(dq)
        dks.append(dk)
        dvs.append(dv)
    dq_s, dk_s, dv_s = _sb_bwd(proj, do_sb, o_sb)
    dproj = _assemble_dproj(dqs + dks, dvs + [dq_s, dk_s, dv_s], [dgd, dgs], tabs)

    dx, dgain = _mm(
        "mix_dx", [dproj, win, x, gain, dxo],
        [pl.BlockSpec((tm, N_CHIPS * cin), lambda i, j, k: (i, 0)), _wfull(D_MODEL, cin, l), row, _gain_spec(), row],
        [(_cols(0, c, cin), _pick(1, c), 0) for c in range(N_CHIPS)], 1, None, NT, (T // tm, 1, 1),
        _rms_bwd_epilogue(2, 3, 4),
        [jax.ShapeDtypeStruct((T, D_MODEL), F32), jax.ShapeDtypeStruct((8, D_MODEL), F32)],
        [row, pl.BlockSpec((8, D_MODEL), lambda i, j, k: (0, 0))])

    (dwin,) = _mm(
        "mix_dwin", [h, dproj],
        [pl.BlockSpec((tk, D_MODEL), lambda i, j, k: (k, 0)), pl.BlockSpec((tk, cin), lambda i, j, k: (k, j))],
        [(0, 1, 0)], 1, (D_MODEL, cin), TN, (1, N_CHIPS, T // tk), one,
        [jax.ShapeDtypeStruct((N_CHIPS, D_MODEL, cin), BF16)],
        [pl.BlockSpec((None, D_MODEL, cin), lambda i, j, k: (j, 0, 0))])
    return dx, dgain, dwin, dwpd, dwps, dwo


def _local_step(x, target, norms, norm_final, weights_of, on_grads):
    T = x.shape[0]
    tabs = _rope_tables(T)
    saved, held = [], []
    for l in range(DEPTH):
        w1 = weights_of(l, 0, x)
        x, s1 = _ffn_fwd(x, norms["norm_ffn1"][l:l + 1], w1["ffn1_w_gate"], w1["ffn1_w_up"], w1["ffn1_w_down"])
        w2 = weights_of(l, 1, x)
        x, s2 = _mixer_fwd(x, norms["norm_mix"][l:l + 1], w2, 0, tabs)
        w3 = weights_of(l, 2, x)
        x, s3 = _ffn_fwd(x, norms["norm_ffn2"][l:l + 1], w3["ffn2_w_gate"], w3["ffn2_w_up"], w3["ffn2_w_down"])
        saved.append((s1, s2, s3))
        held.append((w1, w2, w3))
    dx, dg_final, loss = _final_loss(x, norm_final.reshape(1, D_MODEL), target)
    gains = [None] * DEPTH
    for l in reversed(range(DEPTH)):
        s1, s2, s3 = saved[l]
        w1, w2, w3 = held[l]
        dx, dg2, dwg2, dwu2, dwd2 = _ffn_bwd(dx, norms["norm_ffn2"][l:l + 1], w3["ffn2_w_gate"], w3["ffn2_w_up"],
                                             w3["ffn2_w_down"], s3)
        dx = on_grads(l, 2, dict(ffn2_w_gate=dwg2, ffn2_w_up=dwu2, ffn2_w_down=dwd2), dx)
        dx, dgm, dwin, dwpd, dwps, dwo = _mixer_bwd(dx, norms["norm_mix"][l:l + 1], w2, 0, tabs, s2)
        dx = on_grads(l, 1, dict(w_in=dwin, w_proj_dil=dwpd, w_proj_sb=dwps, w_out=dwo), dx)
        dx, dg1, dwg1, dwu1, dwd1 = _ffn_bwd(dx, norms["norm_ffn1"][l:l + 1], w1["ffn1_w_gate"], w1["ffn1_w_up"],
                                             w1["ffn1_w_down"], s1)
        dx = on_grads(l, 0, dict(ffn1_w_gate=dwg1, ffn1_w_up=dwu1, ffn1_w_down=dwd1), dx)
        gains[l] = dict(norm_ffn1=dg1, norm_mix=dgm, norm_ffn2=dg2)
    return loss, dx, gains, dg_final


def _place():
    x, y, c = lax.axis_index("x"), lax.axis_index("y"), lax.axis_index("c")
    chips = [(1 - x, y), (x, 1 - y), (1 - x, 1 - y)]
    return x, y, c, chips


def _half(c, r):
    return pl.ds(pl.multiple_of(c * (r // 2), 8), r // 2)


def _cast_into_slot(ws, ls, me_arr, after):
    n = len(ws)
    late = [] if after is None else [after]

    def body(me_ref, *refs):
        for a in range(n):
            refs[len(refs) - n + a][...] = refs[a][...].astype(BF16)

    def src(w, l):
        return pl.BlockSpec((None, w.shape[1] // 4, w.shape[2]), lambda i, me: (l, i, 0))

    def dst(w):
        return pl.BlockSpec((None, None, w.shape[1] // 4, w.shape[2]), lambda i, me: (me[0], 0, i, 0))

    return pl.pallas_call(
        body, name="cast_weights",
        grid_spec=pltpu.PrefetchScalarGridSpec(
            num_scalar_prefetch=1, grid=(4,),
            in_specs=[src(w, l) for w, l in zip(ws, ls)] + [pl.BlockSpec(memory_space=pl.ANY)] * len(late),
            out_specs=[dst(w) for w in ws]),
        out_shape=[jax.ShapeDtypeStruct((N_CHIPS, 1) + w.shape[1:], BF16) for w in ws], compiler_params=_params(),
    )(me_arr, *ws, *late)


HBM_SPEC = pl.BlockSpec(memory_space=pltpu.HBM)
SEM_SPEC = pl.BlockSpec(memory_space=pltpu.SEMAPHORE)
SPLIT_COPY = pltpu.CompilerParams(has_side_effects=pltpu.SideEffectType.DATAFLOW_SIDE_EFFECTING)


def _gather_piece(ref, chip_id, c):
    return ref.at[chip_id, 0, _half(c, ref.shape[2]), :]


def _gather_start(tag, bufs, direct):
    n = len(bufs)

    def body(*refs):
        out_refs = refs[n:2 * n]
        send_sems, recv_sems, token = refs[2 * n:]
        x, y, c, chips = _place()
        me = 2 * x + y
        for a in range(n):
            piece = _gather_piece(out_refs[a], me, c)
            for j, chip in enumerate(chips):
                for to in ((0, 1) if direct[a] else (c,)):
                    pltpu.make_async_remote_copy(
                        src_ref=piece, dst_ref=piece, send_sem=send_sems.at[6 * a + 2 * j + to],
                        recv_sem=recv_sems.at[6 * a + 2 * j + c], device_id=(*chip, to), device_id_type=MESH).start()
        token[...] = jnp.zeros_like(token)

    outs = pl.pallas_call(
        body, name=f"gather_start_{tag}", in_specs=[HBM_SPEC] * n,
        out_specs=[HBM_SPEC] * n + [SEM_SPEC, SEM_SPEC, pl.BlockSpec(memory_space=pltpu.VMEM)],
        out_shape=[pltpu.HBM(b.shape, b.dtype) for b in bufs] + [pltpu.SemaphoreType.DMA((6 * n,))] * 2
        + [jax.ShapeDtypeStruct((8, 128), F32)],
        input_output_aliases={a: a for a in range(n)}, compiler_params=SPLIT_COPY,
    )(*[pltpu.with_memory_space_constraint(b, pltpu.HBM) for b in bufs])
    return outs[:n], outs[n], outs[n + 1], outs[n + 2]


def _gather_wait(k, bufs, places, direct, send_sems, recv_sems, after):
    m = len(bufs)

    def body(*refs):
        in_refs = refs[:m]
        ssem, rsem = refs[m], refs[m + 1]
        x, y, c, chips = _place()
        me = 2 * x + y
        for t, a in enumerate(places):
            for j, chip in enumerate(chips):
                for core in ((0, 1) if direct else (c,)):
                    cp = pltpu.make_async_remote_copy(
                        src_ref=_gather_piece(in_refs[t], me, c),
                        dst_ref=_gather_piece(in_refs[t], 2 * chip[0] + chip[1], core),
                        send_sem=ssem.at[6 * a + 2 * j + core], recv_sem=rsem.at[6 * a + 2 * j + core],
                        device_id=(*chip, core), device_id_type=MESH)
                    cp.wait_send()
                    cp.wait_recv()

    return pl.pallas_call(
        body, name=f"gather_wait_{k}",
        in_specs=[HBM_SPEC] * m + [SEM_SPEC, SEM_SPEC, pl.BlockSpec(memory_space=pl.ANY)], out_specs=[HBM_SPEC] * m,
        out_shape=[pltpu.HBM(b.shape, b.dtype) for b in bufs], input_output_aliases={t: t for t in range(m)},
        compiler_params=SPLIT_COPY,
    )(*bufs, send_sems, recv_sems, after)


def _gather_relay(bufs):
    n = len(bufs)

    def body(*refs):
        out_refs = refs[n:2 * n]
        send_sems, recv_sems = refs[2 * n:]
        x, y, c, chips = _place()
        cps = []
        for a in range(n):
            for j, chip in enumerate(chips):
                piece = _gather_piece(out_refs[a], 2 * chip[0] + chip[1], c)
                cps.append(pltpu.make_async_remote_copy(
                    src_ref=piece, dst_ref=piece, send_sem=send_sems.at[a, j], recv_sem=recv_sems.at[a, j],
                    device_id=(x, y, 1 - c), device_id_type=MESH))
        for cp in cps:
            cp.start()
        for a in range(n):
            for j, chip in enumerate(chips):
                theirs = _gather_piece(out_refs[a], 2 * chip[0] + chip[1], 1 - c)
                pltpu.make_async_remote_copy(
                    src_ref=theirs, dst_ref=theirs, send_sem=send_sems.at[a, j], recv_sem=recv_sems.at[a, j],
                    device_id=(x, y, 1 - c), device_id_type=MESH).wait_recv()
        for cp in cps:
            cp.wait_send()

    any_spec = pl.BlockSpec(memory_space=pl.ANY)
    return pl.pallas_call(
        body, name="gather_relay", in_specs=[any_spec] * n, out_specs=[any_spec] * n,
        out_shape=[jax.ShapeDtypeStruct(b.shape, b.dtype) for b in bufs],
        input_output_aliases={a: a for a in range(n)},
        scratch_shapes=[pltpu.SemaphoreType.DMA((n, 3))] * 2,
    )(*bufs)


def _other_half(ref, c):
    return ref.at[:, _half(1 - c, ref.shape[1]), :]


def _all_of(ref, c):
    return ref


def _sibling_start(name, srcs, pick, land_shapes, thru):
    n = len(srcs)
    lands = [lax.empty(sh, s.dtype) for sh, s in zip(land_shapes, srcs)]
    kept = lands + ([] if thru is None else [thru])
    m = len(kept)

    def body(*refs):
        s_refs, land_refs = refs[:n], refs[n + m:n + m + n]
        send_sems, recv_sems, token = refs[n + 2 * m:]
        x, y, c, _ = _place()
        for a in range(n):
            pltpu.make_async_remote_copy(
                src_ref=pick(s_refs[a], c), dst_ref=land_refs[a], send_sem=send_sems.at[a],
                recv_sem=recv_sems.at[a], device_id=(x, y, 1 - c), device_id_type=MESH).start()
        token[...] = jnp.zeros_like(token)

    outs = pl.pallas_call(
        body, name=name, in_specs=[HBM_SPEC] * (n + m),
        out_specs=[HBM_SPEC] * m + [SEM_SPEC, SEM_SPEC, pl.BlockSpec(memory_space=pltpu.VMEM)],
        out_shape=[pltpu.HBM(v.shape, v.dtype) for v in kept] + [pltpu.SemaphoreType.DMA((n,))] * 2
        + [jax.ShapeDtypeStruct((8, 128), F32)],
        input_output_aliases={n + a: a for a in range(m)}, compiler_params=SPLIT_COPY,
    )(*[pltpu.with_memory_space_constraint(v, pltpu.HBM) for v in list(srcs) + kept])
    return (outs[:n], outs[m], outs[m + 1]), (outs[n] if thru is not None else None), outs[m + 2]


def _sibling_wait(name, srcs, pick, lands, send_sems, recv_sems, after):
    n = len(srcs)

    def body(*refs):
        s_refs, land_refs = refs[:n], refs[n:2 * n]
        ssem, rsem = refs[2 * n], refs[2 * n + 1]
        x, y, c, _ = _place()
        for a in range(n):
            cp = pltpu.make_async_remote_copy(
                src_ref=pick(s_refs[a], c), dst_ref=land_refs[a], send_sem=ssem.at[a], recv_sem=rsem.at[a],
                device_id=(x, y, 1 - c), device_id_type=MESH)
            cp.wait_send()
            cp.wait_recv()

    return pl.pallas_call(
        body, name=name, in_specs=[HBM_SPEC] * (2 * n) + [SEM_SPEC, SEM_SPEC, pl.BlockSpec(memory_space=pl.ANY)],
        out_specs=[HBM_SPEC] * n, out_shape=[pltpu.HBM(v.shape, v.dtype) for v in lands],
        input_output_aliases={n + a: a for a in range(n)}, compiler_params=SPLIT_COPY,
    )(*srcs, *lands, send_sems, recv_sems, after)


def _add_half(gs, gots, c_arr):
    n = len(gs)

    def body(c_ref, *refs):
        for a in range(n):
            refs[2 * n + a][...] = (refs[a][...].astype(F32) + refs[n + a][...].astype(F32)).astype(BF16)

    def own(g):
        return pl.BlockSpec((None, g.shape[1] // 2, g.shape[2]), lambda k, cr: (k, cr[0], 0))

    def half(g):
        return pl.BlockSpec((None, g.shape[1] // 2, g.shape[2]), lambda k, cr: (k, 0, 0))

    return pl.pallas_call(
        body, name="grad_add_half",
        grid_spec=pltpu.PrefetchScalarGridSpec(
            num_scalar_prefetch=1, grid=(N_CHIPS,),
            in_specs=[own(g) for g in gs] + [half(g) for g in gs], out_specs=[half(g) for g in gs]),
        out_shape=[jax.ShapeDtypeStruct(got.shape, BF16) for got in gots], compiler_params=_params(),
    )(c_arr, *gs, *gots)


def _scatter_start(k, ss, thru):
    n = len(ss)

    def body(*refs):
        s_refs, land_refs = refs[2 * n + 1:3 * n + 1], refs[3 * n + 1:4 * n + 1]
        send_sems, recv_sems = refs[4 * n + 2:]
        x, y, c, chips = _place()
        me = 2 * x + y
        for a in range(n):
            for j, chip in enumerate(chips):
                pltpu.make_async_remote_copy(
                    src_ref=s_refs[a].at[2 * chip[0] + chip[1]], dst_ref=land_refs[a].at[me],
                    send_sem=send_sems.at[3 * a + j], recv_sem=recv_sems.at[3 * a + j], device_id=(*chip, c),
                    device_id_type=MESH).start()

    lands = [lax.empty(s.shape, s.dtype) for s in ss]
    hbm = [pltpu.HBM(s.shape, s.dtype) for s in ss]
    outs = pl.pallas_call(
        body, name=f"grad_scatter_start_{k}", in_specs=[HBM_SPEC] * (2 * n + 1),
        out_specs=[HBM_SPEC] * (2 * n + 1) + [SEM_SPEC, SEM_SPEC],
        out_shape=hbm + hbm + [pltpu.HBM(thru.shape, thru.dtype)] + [pltpu.SemaphoreType.DMA((3 * n,))] * 2,
        input_output_aliases={a: a for a in range(2 * n + 1)}, compiler_params=SPLIT_COPY,
    )(*[pltpu.with_memory_space_constraint(v, pltpu.HBM) for v in list(ss) + lands + [thru]])
    return (outs[:n], outs[n:2 * n], outs[2 * n + 1], outs[2 * n + 2]), outs[2 * n]


def _scatter_wait(k, ss, lands, send_sems, recv_sems, after):
    n = len(ss)

    def body(*refs):
        s_refs, land_refs = refs[:n], refs[n:2 * n]
        ssem, rsem = refs[2 * n], refs[2 * n + 1]
        x, y, c, chips = _place()
        me = 2 * x + y
        for a in range(n):
            for j, chip in enumerate(chips):
                cid = 2 * chip[0] + chip[1]
                cp = pltpu.make_async_remote_copy(
                    src_ref=s_refs[a].at[cid], dst_ref=land_refs[a].at[cid], send_sem=ssem.at[3 * a + j],
                    recv_sem=rsem.at[3 * a + j], device_id=(*chip, c), device_id_type=MESH)
                cp.wait_send()
                cp.wait_recv()

    hbm = [pltpu.HBM(s.shape, s.dtype) for s in ss]
    outs = pl.pallas_call(
        body, name=f"grad_scatter_wait_{k}",
        in_specs=[HBM_SPEC] * (2 * n) + [SEM_SPEC, SEM_SPEC, pl.BlockSpec(memory_space=pl.ANY)],
        out_specs=[HBM_SPEC] * (2 * n), out_shape=hbm + hbm,
        input_output_aliases={a: a for a in range(2 * n)}, compiler_params=SPLIT_COPY,
    )(*ss, *lands, send_sems, recv_sems, after)
    return outs[:n], outs[n:]


def _sum_chips(lands, ss, me_arr):
    n = len(lands)

    def body(me_ref, *refs):
        for own in range(N_CHIPS):
            @pl.when(me_ref[0] == own)
            def _(own=own):
                for a in range(n):
                    acc = None
                    for k in range(N_CHIPS):
                        term = (refs[n + a][...] if k == own else refs[a][k]).astype(F32)
                        acc = term if acc is None else acc + term
                    refs[2 * n + a][...] = acc

    return pl.pallas_call(
        body, name="grad_sum_chips",
        grid_spec=pltpu.PrefetchScalarGridSpec(
            num_scalar_prefetch=1, grid=(1,),
            in_specs=[pl.BlockSpec(la.shape, lambda i, me: (0, 0, 0)) for la in lands]
            + [pl.BlockSpec((None,) + la.shape[1:], lambda i, me: (me[0], 0, 0)) for la in lands],
            out_specs=[pl.BlockSpec(la.shape[1:], lambda i, me: (0, 0)) for la in lands]),
        out_shape=[jax.ShapeDtypeStruct(la.shape[1:], F32) for la in lands], compiler_params=_params(),
    )(me_arr, *lands, *ss)


def _allreduce_rows(stats):
    def body(s_ref, o_ref, buf, send_sems, recv_sems):
        x, y, c, _ = _place()
        me = 4 * x + 2 * y + c
        buf[me] = s_ref[...]
        cps = []
        for k in range(1, 8):
            px = jnp.where(k & 4, 1 - x, x)
            py = jnp.where(k & 2, 1 - y, y)
            pc = jnp.where(k & 1, 1 - c, c)
            cps.append(pltpu.make_async_remote_copy(
                src_ref=s_ref, dst_ref=buf.at[me], send_sem=send_sems.at[k - 1], recv_sem=recv_sems.at[k - 1],
                device_id=(px, py, pc), device_id_type=MESH))
        for cp in cps:
            cp.start()
        for cp in cps:
            cp.wait()
        acc = buf[0]
        for d in range(1, 8):
            acc = acc + buf[d]
        o_ref[...] = acc

    vm = pl.BlockSpec(memory_space=pltpu.VMEM)
    return pl.pallas_call(
        body, name="allreduce_rows", in_specs=[vm], out_specs=vm,
        out_shape=jax.ShapeDtypeStruct(stats.shape, F32),
        scratch_shapes=[pltpu.VMEM((8,) + stats.shape, F32), pltpu.SemaphoreType.DMA((7,)),
                        pltpu.SemaphoreType.DMA((7,))],
    )(stats)


def _adamw_math(w, g, m, v):
    m = ADAM_B1 * m + (1.0 - ADAM_B1) * g
    v = ADAM_B2 * v + (1.0 - ADAM_B2) * (g * g)
    m_hat = m / (1.0 - ADAM_B1 ** ADAM_STEP)
    v_hat = v / (1.0 - ADAM_B2 ** ADAM_STEP)
    delta = -ADAM_LR * (m_hat / (jnp.sqrt(v_hat) + ADAM_EPS) + ADAM_WD * w)
    return delta, m, v


def _adamw(ws, ms, vs, mines, theirs, l, c_arr, earlier, after):
    n = len(ws)
    held = [t for e in earlier if e is not None for t in e]
    assert len(held) in (0, 4 * n)
    late = [] if after is None else [after]

    def body(c_ref, *refs):
        outs = refs[len(refs) - 4 * n:]
        for a in range(n):
            w_ref, m_ref, v_ref, a_ref, b_ref = refs[5 * a:5 * a + 5]
            g = jnp.where(pl.program_id(0) == c_ref[0], a_ref[...], b_ref[...])
            delta, mn, vn = _adamw_math(w_ref[...], g, m_ref[...], v_ref[...])
            outs[4 * a][...] = g
            outs[4 * a + 1][...] = delta
            outs[4 * a + 2][...] = mn
            outs[4 * a + 3][...] = vn

    def blk(w):
        tr = w.shape[1] // 4
        return pl.BlockSpec((None, tr, w.shape[2]), lambda hh, i, cr: (l, 2 * hh + i, 0))

    def half(w):
        return pl.BlockSpec((w.shape[1] // 4, w.shape[2]), lambda hh, i, cr: (i, 0))

    outs = pl.pallas_call(
        body, name="adamw",
        grid_spec=pltpu.PrefetchScalarGridSpec(
            num_scalar_prefetch=1, grid=(2, 2),
            in_specs=[sp for w in ws for sp in (blk(w), blk(w), blk(w), half(w), half(w))]
            + [pl.BlockSpec(memory_space=pl.ANY)] * (len(held) + len(late)),
            out_specs=[blk(w) for w in ws for _ in range(4)]),
        out_shape=[jax.ShapeDtypeStruct(w.shape, F32) for w in ws for _ in range(4)],
        input_output_aliases={1 + 5 * n + t: t for t in range(len(held))}, compiler_params=_params(),
    )(c_arr, *[t for grp in zip(ws, ms, vs, mines, theirs) for t in grp], *held, *late)
    return [outs[4 * a:4 * a + 4] for a in range(n)]


def _adamw_rows(w, m, v, g):
    def body(w_ref, m_ref, v_ref, g_ref, d_ref, mo_ref, vo_ref):
        delta, mn, vn = _adamw_math(w_ref[...], g_ref[...], m_ref[...], v_ref[...])
        d_ref[...] = delta
        mo_ref[...] = mn
        vo_ref[...] = vn

    vm = pl.BlockSpec(memory_space=pltpu.VMEM)
    sh = jax.ShapeDtypeStruct(w.shape, F32)
    return pl.pallas_call(body, name="adamw_rows", in_specs=[vm] * 4, out_specs=[vm] * 3, out_shape=[sh] * 3)(w, m, v, g)


SUBLAYERS = (("ffn1_w_gate", "ffn1_w_up", "ffn1_w_down"), ("w_in", "w_proj_dil", "w_proj_sb", "w_out"),
             ("ffn2_w_gate", "ffn2_w_up", "ffn2_w_down"))
TRANSPOSED = ("ffn1_w_gate", "ffn1_w_up", "ffn2_w_gate", "ffn2_w_up")


def _pick_row(blocks):
    row = lax.broadcasted_iota(jnp.int32, (8, D_MODEL), 0)
    out = jnp.zeros((8, D_MODEL), F32)
    for i, b in enumerate(blocks):
        out = out + jnp.where(row == i, b, 0.0)
    return out


def kernel(x, norm_ffn1, ffn1_w_gate, ffn1_w_up, ffn1_w_down, norm_mix, w_in, w_proj_dil, w_proj_sb, w_out, norm_ffn2, ffn2_w_gate, ffn2_w_up, ffn2_w_down, norm_final, loss_target, m_norm_ffn1, m_ffn1_w_gate, m_ffn1_w_up, m_ffn1_w_down, m_norm_mix, m_w_in, m_w_proj_dil, m_w_proj_sb, m_w_out, m_norm_ffn2, m_ffn2_w_gate, m_ffn2_w_up, m_ffn2_w_down, m_norm_final, v_norm_ffn1, v_ffn1_w_gate, v_ffn1_w_up, v_ffn1_w_down, v_norm_mix, v_w_in, v_w_proj_dil, v_w_proj_sb, v_w_out, v_norm_ffn2, v_ffn2_w_gate, v_ffn2_w_up, v_ffn2_w_down, v_norm_final):
    given = dict(locals())
    for n in TRANSPOSED:
        for k in ("", "m_", "v_"):
            given[k + n] = jnp.swapaxes(given[k + n], 1, 2)
    weights = {n: given[n] for n in WEIGHT_NAMES}
    norms = {n: given[n] for n in NORM_NAMES}

    c_arr = lax.axis_index("c").astype(jnp.int32).reshape(1)
    me_arr = (2 * lax.axis_index("x") + lax.axis_index("y")).astype(jnp.int32).reshape(1)
    order = [(l, s, n) for l in range(DEPTH) for s in range(len(SUBLAYERS)) for n in SUBLAYERS[s]]
    n_first = len(SUBLAYERS[0])
    sent, token = {}, None
    for tag, idxs in (("a", range(n_first)), ("b", range(n_first, len(order)))):
        cast = _cast_into_slot([weights[order[i][2]] for i in idxs], [order[i][0] for i in idxs], me_arr, token)
        bufs, send_sems, recv_sems, token = _gather_start(tag, cast, [order[i][0] > 0 for i in idxs])
        for p, i in enumerate(idxs):
            sent[i] = (bufs[p], p, send_sems, recv_sems)

    def weights_of(l, s, after):
        idxs = [i for i, (ll, ss, _) in enumerate(order) if (ll, ss) == (l, s)]
        got = _gather_wait(len(SUBLAYERS) * l + s, [sent[i][0] for i in idxs], [sent[i][1] for i in idxs], l > 0,
                           sent[idxs[0]][2], sent[idxs[0]][3], after)
        return {order[i][2]: g for i, g in zip(idxs, got if l > 0 else _gather_relay(got))}

    out = {}
    to_add, in_flight = [], []

    def add_and_scatter(after):
        l, s, names, gs, lands, ssem, rsem = to_add.pop(0)
        k = len(SUBLAYERS) * l + s
        got = _sibling_wait(f"grad_exchange_wait_{k}", gs, _other_half, lands, ssem, rsem, after)
        sent, after = _scatter_start(k, _add_half(gs, got, c_arr), after)
        in_flight.append((l, s, names) + sent)
        return after

    def on_grads(l, s, grads, after):
        names = list(grads)
        k = len(SUBLAYERS) * l + s
        gs = [grads[n] for n in names]
        sent, after, _ = _sibling_start(f"grad_exchange_start_{k}", gs, _other_half,
                                        [(g.shape[0], g.shape[1] // 2, g.shape[2]) for g in gs], after)
        if to_add:
            after = add_and_scatter(after)
        to_add.append((l, s, names, gs) + sent)
        return after

    loss_blk, grad_x, gains, dg_final = _local_step(x[0], loss_target[0], norms, norm_final, weights_of, on_grads)
    grad_x = add_and_scatter(grad_x)

    def update(l, names, mine, swap, after):
        theirs = _sibling_wait(f"grad_swap_wait_{l}_{names[0]}", mine, _all_of, *swap, grad_x if after is None else after)
        res = _adamw([weights[n] for n in names], [given["m_" + n] for n in names], [given["v_" + n] for n in names],
                     mine, theirs, l, c_arr, [out.get(n) for n in names], after)
        out.update(zip(names, res))

    waiting = None
    for l, s, names, sums, lands, ssem, rsem in in_flight:
        sums, lands = _scatter_wait(len(SUBLAYERS) * l + s, sums, lands, ssem, rsem, grad_x)
        mine = _sum_chips(lands, sums, me_arr)
        swap, _, token = _sibling_start(f"grad_swap_start_{l}_{names[0]}", mine, _all_of,
                                        [m.shape for m in mine], None)
        if waiting is not None:
            update(*waiting, token)
        waiting = (l, names, mine, swap)
    update(*waiting, None)
    out = {k + n: (jnp.swapaxes(v, 1, 2) if n in TRANSPOSED else v)
           for n, res in out.items() for k, v in zip(("grad_", "delta_", "new_m_", "new_v_"), res)}
    out["grad_x"] = grad_x[None]

    rows = [gains[l][n] for n in NORM_NAMES for l in range(DEPTH)] + [dg_final, loss_blk]
    total = _allreduce_rows(_pick_row(rows))
    out["loss"] = total[7, 0]
    wn = jnp.concatenate([given[n] for n in NORM_NAMES] + [norm_final[None], jnp.zeros((1, D_MODEL), F32)])
    mn_ = jnp.concatenate([given["m_" + n] for n in NORM_NAMES] + [m_norm_final[None], jnp.zeros((1, D_MODEL), F32)])
    vn_ = jnp.concatenate([given["v_" + n] for n in NORM_NAMES] + [v_norm_final[None], jnp.ones((1, D_MODEL), F32)])
    d_n, m_n, v_n = _adamw_rows(wn, mn_, vn_, total)
    for i, n in enumerate(NORM_NAMES):
        sl = slice(i * DEPTH, (i + 1) * DEPTH)
        out["grad_" + n], out["delta_" + n], out["new_m_" + n], out["new_v_" + n] = total[sl], d_n[sl], m_n[sl], v_n[sl]
    out["grad_norm_final"], out["delta_norm_final"] = total[6], d_n[6]
    out["new_m_norm_final"], out["new_v_norm_final"] = m_n[6], v_n[6]

    names = ["norm_ffn1", "ffn1_w_gate", "ffn1_w_up", "ffn1_w_down", "norm_mix", "w_in", "w_proj_dil", "w_proj_sb",
             "w_out", "norm_ffn2", "ffn2_w_gate", "ffn2_w_up", "ffn2_w_down", "norm_final"]
    return (out["loss"], out["grad_x"], *[out["grad_" + n] for n in names], *[out["delta_" + n] for n in names],
            *[out["new_m_" + n] for n in names], *[out["new_v_" + n] for n in names])
```

```python
import functools

import jax
import jax.numpy as jnp
from jax import lax
from jax.experimental import pallas as pl
from jax.experimental.pallas import tpu as pltpu

F32 = jnp.float32
BF16 = jnp.bfloat16

D_MODEL = 1024
DEPTH = 2
N_CHIPS = 4
HEAD_DIM = 64
ROPE_DIM = 16
ROPE_THETA = 500000.0
DIL_GROUPS = ((128, 1), (512, 4), (2048, 16))
SPAN = 128
Q_BLOCK = 128
RMS_EPS = 1e-6
D_ATT = 256
COL_QS = 2304
COL_GD = 3072
COL_GS = 4096
ADAM_LR, ADAM_B1, ADAM_B2, ADAM_EPS, ADAM_WD, ADAM_STEP = 0.001, 0.9, 0.999, 1e-08, 0.01, 10

VMEM_LIMIT = 52 * 1024 * 1024
TM = 512
NEG = -1e30

NN = (((1,), (0,)), ((), ()))
NT = (((1,), (1,)), ((), ()))
TN = (((0,), (0,)), ((), ()))
MESH = pl.DeviceIdType.MESH

WEIGHT_NAMES = ("ffn1_w_gate", "ffn1_w_up", "ffn1_w_down", "w_in", "w_proj_dil",
                "w_proj_sb", "w_out", "ffn2_w_gate", "ffn2_w_up", "ffn2_w_down")
NORM_NAMES = ("norm_ffn1", "norm_mix", "norm_ffn2")


def _params(**kw):
    return pltpu.CompilerParams(vmem_limit_bytes=VMEM_LIMIT, **kw)


def _sigmoid(x):
    return 0.5 * jnp.tanh(0.5 * x) + 0.5


def _mm_body(pairs, n_in, n_out, n_acc, dims, nk, epilogue, *refs):
    ins = refs[:n_in]
    outs = refs[n_in:n_in + n_out]
    accs = refs[n_in + n_out:]
    i = pl.program_id(0)
    k = pl.program_id(2)

    def operand(a):
        return (a(ins) if callable(a) else ins[a][...]).astype(BF16)

    def dot(ia, ib):
        return lax.dot_general(operand(ia), operand(ib), dims, preferred_element_type=F32)

    if nk == 1:
        parts = [None] * n_acc
        for ia, ib, ic in pairs:
            parts[ic] = dot(ia, ib) if parts[ic] is None else parts[ic] + dot(ia, ib)
        epilogue(parts, ins, outs, i)
        return

    @pl.when(k == 0)
    def _():
        for c in range(n_acc):
            accs[c][...] = jnp.zeros_like(accs[c])

    for ia, ib, ic in pairs:
        accs[ic][...] += dot(ia, ib)

    @pl.when(k == nk - 1)
    def _():
        epilogue([a[...] for a in accs], ins, outs, i)


def _mm(name, ins, in_specs, pairs, n_acc, acc_shape, dims, grid, epilogue, out_shapes, out_specs):
    nk = grid[2]
    scratch = [pltpu.VMEM(acc_shape, F32) for _ in range(n_acc)] if nk > 1 else []
    body = functools.partial(_mm_body, tuple(pairs), len(ins), len(out_shapes), n_acc, dims, nk, epilogue)
    return pl.pallas_call(
        body, name=name, grid=grid, in_specs=in_specs, out_specs=out_specs, out_shape=out_shapes,
        scratch_shapes=scratch,
        compiler_params=_params(dimension_semantics=("arbitrary", "arbitrary", "arbitrary")),
    )(*ins)


def _rms_bwd_epilogue(x_idx, g_idx, dxo_idx):
    def ep(vals, ins, outs, i):
        dh = vals[0]
        x = ins[x_idx][...]
        g = ins[g_idx][...]
        rstd = lax.rsqrt(jnp.mean(x * x, axis=-1, keepdims=True) + RMS_EPS)
        xhat = x * rstd
        dxhat = dh * g
        dx = rstd * (dxhat - xhat * jnp.mean(dxhat * xhat, axis=-1, keepdims=True))
        outs[0][...] = ins[dxo_idx][...] + dx
        dg = jnp.broadcast_to(jnp.sum(dh * xhat, axis=0, keepdims=True), outs[1].shape)

        @pl.when(i == 0)
        def _():
            outs[1][...] = dg

        @pl.when(i > 0)
        def _():
            outs[1][...] += dg
    return ep


def _normed(x_idx, g_idx):
    seen = {}

    def f(ins):
        if id(ins) not in seen:
            xv = ins[x_idx][...]
            h = xv * lax.rsqrt(jnp.mean(xv * xv, axis=-1, keepdims=True) + RMS_EPS)
            seen[id(ins)] = (ins, (h * ins[g_idx][...]).astype(BF16))
        return seen[id(ins)][1]
    return f


def _rope_tables(T):
    half = ROPE_DIM // 2
    lane = jnp.arange(128) % HEAD_DIM
    inv_freq = ROPE_THETA ** (-(2 * (lane % half)).astype(F32) / ROPE_DIM)
    ang = jnp.arange(T, dtype=F32)[:, None] * inv_freq[None, :]
    cos, sin = jnp.cos(ang), jnp.sin(ang)
    c = jnp.where(lane < ROPE_DIM, cos, 1.0)
    s1 = jnp.where(lane < half, -sin, 0.0)
    s2 = jnp.where((lane >= half) & (lane < ROPE_DIM), sin, 0.0)
    return c, s1, s2


def _rope_fwd(xv, c, s1, s2):
    w = xv.shape[1]
    half = ROPE_DIM // 2
    return xv * c + pltpu.roll(xv, w - half, 1) * s1 + pltpu.roll(xv, half, 1) * s2


def _rope_bwd(dy, c, s1, s2):
    w = dy.shape[1]
    half = ROPE_DIM // 2
    return dy * c + pltpu.roll(dy * s1, half, 1) + pltpu.roll(dy * s2, w - half, 1)


def _assemble_dproj(dqk, rest, gates, tabs):
    T = gates[0].shape[0]
    n_qk, n_rest = len(dqk), len(rest)
    width = (n_qk + n_rest) * D_ATT + 2 * D_MODEL

    def body(*refs):
        ins, (c_ref, s1_ref, s2_ref), o_ref = refs[:n_qk + n_rest + 2], refs[-4:-1], refs[-1]
        c = jnp.concatenate([c_ref[...]] * 2, axis=1)
        s1 = jnp.concatenate([s1_ref[...]] * 2, axis=1)
        s2 = jnp.concatenate([s2_ref[...]] * 2, axis=1)
        for b in range(n_qk + n_rest):
            v = ins[b][...]
            if b < n_qk:
                v = _rope_bwd(v, c, s1, s2)
            o_ref[:, b * D_ATT:(b + 1) * D_ATT] = v.astype(BF16)
        off = (n_qk + n_rest) * D_ATT
        o_ref[:, off:off + D_MODEL] = ins[-2][...]
        o_ref[:, off + D_MODEL:] = ins[-1][...]

    att = pl.BlockSpec((TM, D_ATT), lambda i: (i, 0))
    wide = pl.BlockSpec((TM, D_MODEL), lambda i: (i, 0))
    tab = pl.BlockSpec((TM, 128), lambda i: (i, 0))
    return pl.pallas_call(
        body, name="assemble_dproj", grid=(T // TM,),
        in_specs=[att] * (n_qk + n_rest) + [wide, wide, tab, tab, tab],
        out_specs=pl.BlockSpec((TM, width), lambda i: (i, 0)),
        out_shape=jax.ShapeDtypeStruct((T, width), BF16), compiler_params=_params(),
    )(*dqk, *rest, *gates, *tabs)


def _dil_merge(os_, lses):
    T = os_[0].shape[0]

    def body(o0, o1, o2, l0, l1, l2, o_ref, lse_ref):
        a, b, c = l0[...], l1[...], l2[...]
        m = jnp.maximum(jnp.maximum(a, b), c)
        ea, eb, ec = jnp.exp(a - m), jnp.exp(b - m), jnp.exp(c - m)
        den = ea + eb + ec
        o_ref[...] = (ea * o0[...] + eb * o1[...] + ec * o2[...]) / den
        lse_ref[...] = m + jnp.log(den)

    blk = pl.BlockSpec((TM, D_ATT), lambda i: (i, 0))
    sh = jax.ShapeDtypeStruct((T, D_ATT), F32)
    return pl.pallas_call(
        body, name="dil_merge", grid=(T // TM,), in_specs=[blk] * 6, out_specs=[blk, blk],
        out_shape=[sh, sh], compiler_params=_params(),
    )(*os_, *lses)


def _final_loss(x, gain, target):
    T = x.shape[0]

    def body(x_ref, g_ref, t_ref, dx_ref, dg_ref, loss_ref):
        xv = x_ref[...]
        g = g_ref[...]
        rstd = lax.rsqrt(jnp.mean(xv * xv, axis=-1, keepdims=True) + RMS_EPS)
        xhat = xv * rstd
        err = xhat * g - t_ref[...]
        loss = 0.5 * jnp.sum(jnp.mean(err * err, axis=-1, keepdims=True), axis=0, keepdims=True)
        dy = err * (1.0 / D_MODEL)
        dxhat = dy * g
        dx_ref[...] = rstd * (dxhat - xhat * jnp.mean(dxhat * xhat, axis=-1, keepdims=True))
        dg = jnp.broadcast_to(jnp.sum(dy * xhat, axis=0, keepdims=True), dg_ref.shape)
        ls = jnp.broadcast_to(loss, loss_ref.shape)

        @pl.when(pl.program_id(0) == 0)
        def _():
            dg_ref[...] = dg
            loss_ref[...] = ls

        @pl.when(pl.program_id(0) > 0)
        def _():
            dg_ref[...] += dg
            loss_ref[...] += ls

    blk = pl.BlockSpec((TM, D_MODEL), lambda i: (i, 0))
    row = pl.BlockSpec((1, D_MODEL), lambda i: (0, 0))
    acc = pl.BlockSpec((8, D_MODEL), lambda i: (0, 0))
    return pl.pallas_call(
        body, name="final_loss", grid=(T // TM,), in_specs=[blk, row, blk], out_specs=[blk, acc, acc],
        out_shape=[jax.ShapeDtypeStruct((T, D_MODEL), F32), jax.ShapeDtypeStruct((8, D_MODEL), F32),
                   jax.ShapeDtypeStruct((8, D_MODEL), F32)],
        compiler_params=_params(dimension_semantics=("arbitrary",)),
    )(x, gain, target)


def _pair_masks():
    lane = lax.broadcasted_iota(jnp.int32, (SPAN, 128), 1)
    return [lane < HEAD_DIM, lane >= HEAD_DIM]


def _stack_heads(x, masks):
    return jnp.concatenate([jnp.where(m, x, 0.0) for m in masks], axis=0)


def _unstack_heads(y, masks):
    rows = y.shape[0] // len(masks)
    out = jnp.where(masks[0], y[:rows], 0.0)
    for h in range(1, len(masks)):
        out = out + jnp.where(masks[h], y[rows * h:rows * (h + 1)], 0.0)
    return out


DIL_PAIR = 2


def _dil_rows(idx, d):
    u = idx // d
    r = idx - u * d
    own = pl.ds(u * (SPAN * d) + r, SPAN, stride=d) if d > 1 else pl.ds(pl.multiple_of(u * SPAN, SPAN), SPAN)
    up = jnp.maximum(u - 1, 0)
    prev = pl.ds(up * (SPAN * d) + r, SPAN, stride=d) if d > 1 else pl.ds(pl.multiple_of(up * SPAN, SPAN), SPAN)
    return u, own, prev


def _dil_valid(u):
    qi = lax.broadcasted_iota(jnp.int32, (2 * SPAN, 2 * SPAN), 0) & (SPAN - 1)
    kj = lax.broadcasted_iota(jnp.int32, (2 * SPAN, 2 * SPAN), 1)
    in_prev = (kj < SPAN) & (kj >= qi + jnp.where(u > 0, 0, SPAN))
    return in_prev | ((kj >= SPAN) & (kj - SPAN <= qi))


def _dil_keys(ref, own, prev):
    return jnp.concatenate([ref[prev, :], ref[own, :]], axis=0).astype(BF16)


def _dil_fwd(proj, g, d):
    T = proj.shape[0]
    n_iter = T // SPAN

    def body(q_ref, k_ref, v_ref, o_ref, lse_ref):
        masks = _pair_masks()

        def step(pair, carry):
            its = [_dil_rows(DIL_PAIR * pair + e, d) for e in range(DIL_PAIR)]
            qs = [_stack_heads(q_ref[own, :] * (HEAD_DIM ** -0.5), masks).astype(BF16) for _, own, _ in its]
            kks = [_dil_keys(k_ref, own, prev) for _, own, prev in its]
            vvs = [_dil_keys(v_ref, own, prev) for _, own, prev in its]
            ss = [jnp.where(_dil_valid(u), lax.dot_general(q, kk, NT, preferred_element_type=F32), NEG)
                  for (u, _, _), q, kk in zip(its, qs, kks)]
            ms = [jnp.max(s, axis=1, keepdims=True) for s in ss]
            ps = [jnp.exp(s - m) for s, m in zip(ss, ms)]
            dens = [jnp.sum(p, axis=1, keepdims=True) for p in ps]
            pvs = [lax.dot_general(p.astype(BF16), vv, NN, preferred_element_type=F32) / den
                   for p, vv, den in zip(ps, vvs, dens)]
            for (_, own, _), pv, m, den in zip(its, pvs, ms, dens):
                o_ref[own, :] = _unstack_heads(pv, masks)
                lse_ref[own, :] = _unstack_heads(jnp.broadcast_to(m + jnp.log(den), pv.shape), masks)
            return carry

        lax.fori_loop(0, n_iter // DIL_PAIR, step, 0)

    def col(b):
        return pl.BlockSpec((T, 128), lambda p: (0, b + p))

    sh = jax.ShapeDtypeStruct((T, D_ATT), F32)
    out = pl.BlockSpec((T, 128), lambda p: (0, p))
    return pl.pallas_call(
        body, name=f"dil_fwd_d{d}", grid=(2,),
        in_specs=[col(2 * g), col(6 + 2 * g), col(12 + 2 * g)], out_specs=[out, out], out_shape=[sh, sh],
        compiler_params=_params(dimension_semantics=("arbitrary",)),
    )(proj, proj, proj)


def _dil_bwd(proj, do, o_dil, lse, g, d):
    T = proj.shape[0]
    n_iter = T // SPAN

    def body(q_ref, k_ref, v_ref, do_ref, o_ref, lse_ref, dq_ref, dk_ref, dv_ref):
        masks = _pair_masks()
        head_lanes = jnp.concatenate(masks, axis=0)

        def step(pair, carry):
            its = [_dil_rows(DIL_PAIR * pair + e, d) for e in range(DIL_PAIR)]
            qs = [_stack_heads(q_ref[own, :] * (HEAD_DIM ** -0.5), masks).astype(BF16) for _, own, _ in its]
            kks = [_dil_keys(k_ref, own, prev) for _, own, prev in its]
            vvs = [_dil_keys(v_ref, own, prev) for _, own, prev in its]
            doms = [_stack_heads(do_ref[own, :], masks) for _, own, _ in its]
            dos = [dom.astype(BF16) for dom in doms]
            deltas = [jnp.sum(dom * jnp.concatenate([o_ref[own, :]] * 2, axis=0), axis=1, keepdims=True)
                      for dom, (_, own, _) in zip(doms, its)]
            lrows = [jnp.max(jnp.where(head_lanes, jnp.concatenate([lse_ref[own, :]] * 2, axis=0), NEG),
                             axis=1, keepdims=True) for _, own, _ in its]
            ss = [lax.dot_general(q, kk, NT, preferred_element_type=F32) for q, kk in zip(qs, kks)]
            dps = [lax.dot_general(do_b, vv, NT, preferred_element_type=F32) for do_b, vv in zip(dos, vvs)]
            ps = [jnp.where(_dil_valid(u), jnp.exp(s - lrow), 0.0) for (u, _, _), s, lrow in zip(its, ss, lrows)]
            dss = [(p * (dp - delta)).astype(BF16) for p, dp, delta in zip(ps, dps, deltas)]
            dqs = [lax.dot_general(ds, kk, NN, preferred_element_type=F32) for ds, kk in zip(dss, kks)]
            dkks = [lax.dot_general(ds, q, TN, preferred_element_type=F32) for ds, q in zip(dss, qs)]
            dvvs = [lax.dot_general(p.astype(BF16), do_b, TN, preferred_element_type=F32) for p, do_b in zip(ps, dos)]
            for (_, own, prev), dq, dkk, dvv in zip(its, dqs, dkks, dvvs):
                dq_ref[own, :] = _unstack_heads(dq, masks) * (HEAD_DIM ** -0.5)
                dk_ref[own, :] = dkk[SPAN:]
                dv_ref[own, :] = dvv[SPAN:]
                dk_ref[prev, :] = dk_ref[prev, :] + dkk[:SPAN]
                dv_ref[prev, :] = dv_ref[prev, :] + dvv[:SPAN]
            return carry

        lax.fori_loop(0, n_iter // DIL_PAIR, step, 0)

    def col(b):
        return pl.BlockSpec((T, 128), lambda p: (0, b + p))

    sh = jax.ShapeDtypeStruct((T, D_ATT), F32)
    return pl.pallas_call(
        body, name=f"dil_bwd_d{d}", grid=(2,),
        in_specs=[col(2 * g), col(6 + 2 * g), col(12 + 2 * g), col(0), col(0), col(0)],
        out_specs=[col(0), col(0), col(0)], out_shape=[sh, sh, sh],
        compiler_params=_params(dimension_semantics=("arbitrary",)),
    )(proj, proj, proj, do, o_dil, lse)


SB_KT = 512


def _sb_tri(strict):
    a = lax.broadcasted_iota(jnp.int32, (Q_BLOCK, Q_BLOCK), 0)
    b = lax.broadcasted_iota(jnp.int32, (Q_BLOCK, Q_BLOCK), 1)
    return jnp.where((a > b) if strict else (a >= b), 1.0, 0.0).astype(BF16)


def _split_stack(x):
    nb = x.shape[1] // Q_BLOCK
    blocks = [x[:, Q_BLOCK * b:Q_BLOCK * (b + 1)] for b in range(nb)]
    hi = [b.astype(BF16) for b in blocks]
    lo = [(b - h.astype(F32)).astype(BF16) for b, h in zip(blocks, hi)]
    return blocks, jnp.concatenate(hi + lo, axis=0)


def _suffix_from(y, blocks, c):
    r = blocks[0].shape[0]
    nb = len(blocks)
    outs = [None] * nb
    run = c
    for b in reversed(range(nb)):
        outs[b] = run + y[r * b:r * (b + 1)] + y[r * (nb + b):r * (nb + b + 1)]
        run = run + jnp.sum(blocks[b], axis=1, keepdims=True)
    return jnp.concatenate(outs, axis=1), run


SB_HEADS = D_ATT // HEAD_DIM
SB_FWD_CHAINS = 2
SB_BWD_CHAINS = 1


def _sb_past(i, t, rows):
    row = lax.broadcasted_iota(jnp.int32, (rows, SB_KT), 0) & (Q_BLOCK - 1)
    col = lax.broadcasted_iota(jnp.int32, (rows, SB_KT), 1)
    return col + t * SB_KT < row + i * Q_BLOCK


def _sb_head_masks(chains):
    lane = lax.broadcasted_iota(jnp.int32, (Q_BLOCK, D_ATT), 1)
    masks = [(lane >= HEAD_DIM * h) & (lane < HEAD_DIM * (h + 1)) for h in range(SB_HEADS)]
    per = SB_HEADS // chains
    return [masks[per * g:per * (g + 1)] for g in range(chains)]


def _sb_rows(t):
    return pl.ds(pl.multiple_of(t * SB_KT, SB_KT), SB_KT)


def _sb_log_terms(z, past):
    lsz = jnp.minimum(z, 0.0) - jnp.log(1.0 + jnp.exp(-jnp.abs(z)))
    lk = lsz - z
    return lsz, (lk if past is None else jnp.where(past, lk, 0.0))


def _sb_weights(lsz, after, past):
    w = jnp.exp(lsz + after)
    return w if past is None else jnp.where(past, w, 0.0)


def _sb_fwd(proj):
    T = proj.shape[0]

    def body(q_ref, k_ref, v_ref, o_ref):
        i = pl.program_id(0)
        masks = _sb_head_masks(SB_FWD_CHAINS)
        rows = SB_HEADS // SB_FWD_CHAINS * Q_BLOCK
        tri = _sb_tri(True)
        q = q_ref[...] * (HEAD_DIM ** -0.5)
        qs = [_stack_heads(q, m).astype(BF16) for m in masks]
        n_tiles = (i * Q_BLOCK) // SB_KT + 1

        def tile(t, carry, masked):
            kb = k_ref[_sb_rows(t), :].astype(BF16)
            vb = v_ref[_sb_rows(t), :].astype(BF16)
            past = _sb_past(i, t, rows) if masked else None
            acc, cs = carry[0], carry[1:]
            zs = [lax.dot_general(g, kb, NT, preferred_element_type=F32) for g in qs]
            logs = [_sb_log_terms(z, past) for z in zs]
            splits = [_split_stack(lk) for _, lk in logs]
            ys = [lax.dot_general(x, tri, NN, preferred_element_type=F32) for _, x in splits]
            sums = [_suffix_from(y, blocks, c) for y, (blocks, _), c in zip(ys, splits, cs)]
            ws = [_sb_weights(lsz, after, past).astype(BF16) for (lsz, _), (after, _) in zip(logs, sums)]
            for m, w in zip(masks, ws):
                acc = acc + _unstack_heads(lax.dot_general(w, vb, NN, preferred_element_type=F32), m)
            return (acc, *[c for _, c in sums])

        zcol = jnp.zeros((rows, 1), F32)
        carry = tile(n_tiles - 1, (jnp.zeros((Q_BLOCK, D_ATT), F32),) + (zcol,) * SB_FWD_CHAINS, True)
        carry = lax.fori_loop(0, n_tiles - 1, lambda tt, cr: tile(n_tiles - 2 - tt, cr, False), carry)
        o_ref[...] = carry[0]

    cb = COL_QS // D_ATT
    return pl.pallas_call(
        body, name="sb_fwd", grid=(T // Q_BLOCK,),
        in_specs=[pl.BlockSpec((Q_BLOCK, D_ATT), lambda i: (i, cb)),
                  pl.BlockSpec((T, D_ATT), lambda i: (0, cb + 1)),
                  pl.BlockSpec((T, D_ATT), lambda i: (0, cb + 2))],
        out_specs=pl.BlockSpec((Q_BLOCK, D_ATT), lambda i: (i, 0)),
        out_shape=jax.ShapeDtypeStruct((T, D_ATT), F32),
        compiler_params=_params(dimension_semantics=("arbitrary",)),
    )(proj, proj, proj)


def _sb_bwd(proj, do, o):
    T = proj.shape[0]

    def body(q_ref, k_ref, v_ref, do_ref, o_ref, dq_ref, dk_ref, dv_ref):
        i = pl.program_id(0)
        masks = _sb_head_masks(SB_BWD_CHAINS)
        n_rows = SB_HEADS // SB_BWD_CHAINS * Q_BLOCK
        tri = _sb_tri(True)
        tri_incl = _sb_tri(False)

        @pl.when(i == 0)
        def _():
            dk_ref[...] = jnp.zeros_like(dk_ref)
            dv_ref[...] = jnp.zeros_like(dv_ref)

        q = q_ref[...] * (HEAD_DIM ** -0.5)
        qs = [_stack_heads(q, m).astype(BF16) for m in masks]
        dos = [_stack_heads(do_ref[...], m).astype(BF16) for m in masks]
        o_rep = jnp.concatenate([o_ref[...]] * (SB_HEADS // SB_BWD_CHAINS), axis=0)
        deltas = [jnp.sum(d.astype(F32) * o_rep, axis=1, keepdims=True) for d in dos]
        n_tiles = (i * Q_BLOCK) // SB_KT + 1

        def tile(t, carry, masked):
            rows = _sb_rows(t)
            kb = k_ref[rows, :].astype(BF16)
            vb = v_ref[rows, :].astype(BF16)
            past = _sb_past(i, t, n_rows) if masked else None
            dq, cs, ces = carry[0], carry[1:1 + SB_BWD_CHAINS], carry[1 + SB_BWD_CHAINS:]
            zs = [lax.dot_general(g, kb, NT, preferred_element_type=F32) for g in qs]
            gvs = [lax.dot_general(d, vb, NT, preferred_element_type=F32) for d in dos]
            logs = [_sb_log_terms(z, past) for z in zs]
            splits = [_split_stack(lk) for _, lk in logs]
            ys = [lax.dot_general(x, tri, NN, preferred_element_type=F32) for _, x in splits]
            sums = [_suffix_from(y, blocks, c) for y, (blocks, _), c in zip(ys, splits, cs)]
            wbs = [_sb_weights(lsz, after, past).astype(BF16) for (lsz, _), (after, _) in zip(logs, sums)]
            es = [wb.astype(F32) * gv for wb, gv in zip(wbs, gvs)]
            esplits = [_split_stack(e) for e in es]
            eys = [lax.dot_general(x, tri_incl, NN, preferred_element_type=F32) for _, x in esplits]
            esums = [_suffix_from(y, blocks, ce) for y, (blocks, _), ce in zip(eys, esplits, ces)]
            dzbs = []
            for e, (lsz, lk), (suf, _), delta in zip(es, logs, esums, deltas):
                dz = e * jnp.exp(lk) - (delta - suf) * jnp.exp(lsz)
                dzbs.append((dz if past is None else jnp.where(past, dz, 0.0)).astype(BF16))
            dk_t = dv_t = None
            for m, dzb, wb, g, d in zip(masks, dzbs, wbs, qs, dos):
                dq = dq + _unstack_heads(lax.dot_general(dzb, kb, NN, preferred_element_type=F32), m)
                a = lax.dot_general(dzb, g, TN, preferred_element_type=F32)
                b = lax.dot_general(wb, d, TN, preferred_element_type=F32)
                dk_t = a if dk_t is None else dk_t + a
                dv_t = b if dv_t is None else dv_t + b
            dk_ref[rows, :] = dk_ref[rows, :] + dk_t
            dv_ref[rows, :] = dv_ref[rows, :] + dv_t
            return (dq, *[c for _, c in sums], *[c for _, c in esums])

        zcol = jnp.zeros((n_rows, 1), F32)
        carry = tile(n_tiles - 1, (jnp.zeros((Q_BLOCK, D_ATT), F32),) + (zcol,) * (2 * SB_BWD_CHAINS), True)
        carry = lax.fori_loop(0, n_tiles - 1, lambda tt, cr: tile(n_tiles - 2 - tt, cr, False), carry)
        dq_ref[...] = carry[0] * (HEAD_DIM ** -0.5)

    cb = COL_QS // D_ATT
    blk = pl.BlockSpec((Q_BLOCK, D_ATT), lambda i: (i, 0))
    full = pl.BlockSpec((T, D_ATT), lambda i: (0, 0))
    sh = jax.ShapeDtypeStruct((T, D_ATT), F32)
    return pl.pallas_call(
        body, name="sb_bwd", grid=(T // Q_BLOCK,),
        in_specs=[pl.BlockSpec((Q_BLOCK, D_ATT), lambda i: (i, cb)),
                  pl.BlockSpec((T, D_ATT), lambda i: (0, cb + 1)),
                  pl.BlockSpec((T, D_ATT), lambda i: (0, cb + 2)), blk, blk],
        out_specs=[blk, full, full], out_shape=[sh, sh, sh],
        compiler_params=_params(dimension_semantics=("arbitrary",)),
    )(proj, proj, proj, do, o)


def _tok(c, by=None):
    if by is None:
        return pl.BlockSpec((TM, c), lambda i, j, k: (i, 0))
    if by == 1:
        return pl.BlockSpec((TM, c), lambda i, j, k: (i, j))
    return pl.BlockSpec((TM, c), lambda i, j, k: (i, k))


def _gain_spec():
    return pl.BlockSpec((1, D_MODEL), lambda i, j, k: (0, 0))


def _wfull(r, c, l):
    return pl.BlockSpec((N_CHIPS, None, r, c), lambda i, j, k: (0, l, 0, 0), pipeline_mode=pl.Buffered(1))


def _pick(idx, c):
    return lambda ins: ins[idx][c]


def _cols(idx, c, w):
    return lambda ins: ins[idx][:, c * w:(c + 1) * w]


def _rows(rows, width):
    return pl.BlockSpec((rows, width), lambda i, j, k: (i, 0))


def _whole(shape):
    return pl.BlockSpec(shape, lambda i, j, k: (0, 0), pipeline_mode=pl.Buffered(1))


def _ffn_fwd(x, gain, wg, wu, wd):
    T = x.shape[0]
    wg, wu, wd = (w.reshape(-1, D_MODEL) for w in (wg, wu, wd))
    ff = wd.shape[0]
    tm = TM // 2
    normed = _normed(0, 3)

    def swiglu(vals, ins, outs, i):
        gt, up = vals
        s = _sigmoid(gt)
        sil = gt * s
        outs[0][...] = sil.astype(BF16)
        outs[1][...] = (up * (s * (1.0 + gt * (1.0 - s)))).astype(BF16)
        outs[2][...] = (sil * up).astype(BF16)
        outs[3][...] = normed(ins)

    ash = jax.ShapeDtypeStruct((T, ff), BF16)
    sil, up_dsil, act, h = _mm(
        "ffn_up", [x, wg, wu, gain], [_rows(tm, D_MODEL), _whole(wg.shape), _whole(wu.shape), _gain_spec()],
        [(normed, 1, 0), (normed, 2, 1)], 2, None, NT, (T // tm, 1, 1), swiglu,
        [ash] * 3 + [jax.ShapeDtypeStruct((T, D_MODEL), BF16)], [_rows(tm, ff)] * 3 + [_rows(tm, D_MODEL)])

    def resid(vals, ins, outs, i):
        outs[0][...] = ins[2][...] + 0.5 * vals[0]

    (y,) = _mm(
        "ffn_down", [act, wd, x], [_rows(TM, ff), _whole(wd.shape), _tok(D_MODEL)], [(0, 1, 0)], 1, None, NN,
        (T // TM, 1, 1), resid, [jax.ShapeDtypeStruct((T, D_MODEL), F32)], [_tok(D_MODEL)])
    return y, (x, h, sil, up_dsil, act)


def _ffn_bwd(dxo, gain, wg, wu, wd, saved):
    x, h, sil, up_dsil, act = saved
    T = x.shape[0]
    n_chips, _, ffs, _ = wd.shape
    wg, wu, wd = (w.reshape(-1, D_MODEL) for w in (wg, wu, wd))
    ff = wd.shape[0]
    tk = TM
    tm = TM

    def dswiglu(vals, ins, outs, i):
        da = 0.5 * vals[0]
        outs[0][...] = (da * ins[3][...].astype(F32)).astype(BF16)
        outs[1][...] = (da * ins[2][...].astype(F32)).astype(BF16)

    ash = jax.ShapeDtypeStruct((T, ff), BF16)
    dgate, dup = _mm(
        "ffn_dact", [dxo, wd, sil, up_dsil], [_rows(tm, D_MODEL), _whole(wd.shape), _rows(tm, ff), _rows(tm, ff)],
        [(0, 1, 0)], 1, None, NT, (T // tm, 1, 1), dswiglu, [ash, ash], [_rows(tm, ff)] * 2)

    def half(vals, ins, outs, i):
        outs[0][...] = (0.5 * vals[0]).astype(BF16)

    def cast(vals, ins, outs, i):
        outs[0][...] = vals[0].astype(BF16)

    tok_k = pl.BlockSpec((tk, D_MODEL), lambda i, j, k: (k, 0))
    hid_k = pl.BlockSpec((tk, ff), lambda i, j, k: (k, 0))
    wsh = jax.ShapeDtypeStruct((ff, D_MODEL), BF16)
    (dwd,) = _mm("ffn_dwd", [act, dxo], [hid_k, tok_k], [(0, 1, 0)], 1, (ff, D_MODEL), TN, (1, 1, T // tk), half,
                 [wsh], [_whole((ff, D_MODEL))])

    tx = TM // 2
    dx, dgain = _mm(
        "ffn_dx", [dgate, dup, wg, wu, x, gain, dxo],
        [_rows(tx, ff), _rows(tx, ff), _whole(wg.shape), _whole(wu.shape), _rows(tx, D_MODEL), _gain_spec(),
         _rows(tx, D_MODEL)],
        [(0, 2, 0), (1, 3, 0)], 1, None, NN, (T // tx, 1, 1), _rms_bwd_epilogue(4, 5, 6),
        [jax.ShapeDtypeStruct((T, D_MODEL), F32), jax.ShapeDtypeStruct((8, D_MODEL), F32)],
        [_rows(tx, D_MODEL), pl.BlockSpec((8, D_MODEL), lambda i, j, k: (0, 0))])

    dws = []
    for dact in (dgate, dup):
        dws += _mm("ffn_dwgu", [dact, h], [hid_k, tok_k], [(0, 1, 0)], 1, (ff, D_MODEL), TN, (1, 1, T // tk), cast,
                   [wsh], [_whole((ff, D_MODEL))])
    dwg, dwu, dwd = (w.reshape(n_chips, ffs, D_MODEL) for w in (dws[0], dws[1], dwd))
    return dx, dgain, dwg, dwu, dwd


def _joined_mixer_weights(wpd, wps, wo):
    n, _, r, c = wpd.shape
    wpd_n, wps_n = (w[:, 0].transpose(1, 0, 2).reshape(r, n * c) for w in (wpd, wps))
    return wpd_n, wps_n, wo.reshape(-1, wo.shape[3])


def _mixer_fwd(x, gain, W, l, tabs):
    T = x.shape[0]
    win, wpd, wps, wo = W["w_in"], W["w_proj_dil"], W["w_proj_sb"], W["w_out"]
    cin = win.shape[3]
    cp = wpd.shape[3]
    normed = _normed(0, 5)
    n_rope = 6 * D_ATT

    tm = TM // 2

    def roped(vals, ins, outs, i):
        for j, v in enumerate(vals):
            lo = j * cin
            k = min(max(n_rope - lo, 0), cin)
            if k:
                tab = [jnp.concatenate([ins[t][...]] * (k // 128), axis=1) for t in (2, 3, 4)]
                outs[0][:, lo:lo + k] = _rope_fwd(v[:, :k], *tab)
            if k < cin:
                outs[0][:, lo + k:lo + cin] = v[:, k:]
        outs[1][...] = normed(ins)

    proj, h = _mm(
        "mix_in", [x, win, *tabs, gain],
        [_rows(tm, D_MODEL), _wfull(D_MODEL, cin, l)] + [_rows(tm, 128)] * 3 + [_gain_spec()],
        [(normed, _pick(1, c), c) for c in range(N_CHIPS)], N_CHIPS, None, NN, (T // tm, 1, 1), roped,
        [jax.ShapeDtypeStruct((T, N_CHIPS * cin), F32), jax.ShapeDtypeStruct((T, D_MODEL), BF16)],
        [_rows(tm, N_CHIPS * cin), _rows(tm, D_MODEL)])

    os_, lses = [], []
    for g, (window, dil) in enumerate(DIL_GROUPS):
        o_g, lse_g = _dil_fwd(proj, g, dil)
        os_.append(o_g)
        lses.append(lse_g)
    o_dil, lse = _dil_merge(os_, lses)
    o_sb = _sb_fwd(proj)

    def gated(vals, ins, outs, i):
        pd, ps = vals
        outs[0][...] = (_sigmoid(ins[4][...]) * pd + _sigmoid(ins[5][...]) * ps).astype(BF16)
        outs[1][...] = pd.astype(BF16)
        outs[2][...] = ps.astype(BF16)

    wpd_n, wps_n, wo_n = _joined_mixer_weights(wpd, wps, wo)
    gd_spec = pl.BlockSpec((TM, D_MODEL), lambda i, j, k: (i, COL_GD // D_MODEL))
    gs_spec = pl.BlockSpec((TM, D_MODEL), lambda i, j, k: (i, COL_GS // D_MODEL))
    ush = jax.ShapeDtypeStruct((T, D_MODEL), BF16)
    u, pd, ps = _mm(
        "mix_gate", [o_dil, o_sb, wpd_n, wps_n, proj, proj],
        [_tok(D_ATT), _tok(D_ATT), _whole(wpd_n.shape), _whole(wps_n.shape), gd_spec, gs_spec],
        [(0, 2, 0), (1, 3, 1)], 2, None, NN, (T // TM, 1, 1), gated, [ush] * 3, [_tok(D_MODEL)] * 3)

    def resid(vals, ins, outs, i):
        outs[0][...] = ins[2][...] + vals[0]

    (y,) = _mm(
        "mix_out", [u, wo_n, x], [_tok(D_MODEL), _whole(wo_n.shape), _tok(D_MODEL)], [(0, 1, 0)], 1, None, NN,
        (T // TM, 1, 1), resid, [jax.ShapeDtypeStruct((T, D_MODEL), F32)], [_tok(D_MODEL)])
    return y, (x, h, proj, o_dil, lse, o_sb, u, pd, ps)


def _mixer_bwd(dxo, gain, W, l, tabs, saved):
    x, h, proj, o_dil, lse, o_sb, u, pd, ps = saved
    T = x.shape[0]
    win, wpd, wps, wo = W["w_in"], W["w_proj_dil"], W["w_proj_sb"], W["w_out"]
    cin = win.shape[3]
    cp = wpd.shape[3]
    tk = TM
    tm = TM
    row = pl.BlockSpec((tm, D_MODEL), lambda i, j, k: (i, 0))

    def dgated(vals, ins, outs, i):
        du = vals[0]
        sd = _sigmoid(ins[4][...])
        ss = _sigmoid(ins[5][...])
        outs[0][...] = (du * sd).astype(BF16)
        outs[1][...] = (du * ss).astype(BF16)
        outs[2][...] = (du * ins[2][...].astype(F32) * sd * (1.0 - sd)).astype(BF16)
        outs[3][...] = (du * ins[3][...].astype(F32) * ss * (1.0 - ss)).astype(BF16)

    wpd_n, wps_n, wo_n = _joined_mixer_weights(wpd, wps, wo)
    gd_spec = pl.BlockSpec((TM, D_MODEL), lambda i, j, k: (i, COL_GD // D_MODEL))
    gs_spec = pl.BlockSpec((TM, D_MODEL), lambda i, j, k: (i, COL_GS // D_MODEL))
    ush = jax.ShapeDtypeStruct((T, D_MODEL), BF16)
    dpd, dps, dgd, dgs = _mm(
        "mix_du", [dxo, wo_n, pd, ps, proj, proj],
        [_tok(D_MODEL), _whole(wo_n.shape), _tok(D_MODEL), _tok(D_MODEL), gd_spec, gs_spec],
        [(0, 1, 0)], 1, None, NT, (T // TM, 1, 1), dgated, [ush] * 4, [_tok(D_MODEL)] * 4)

    def one(vals, ins, outs, i):
        outs[0][...] = vals[0].astype(BF16)

    def two(vals, ins, outs, i):
        outs[0][...] = vals[0].astype(BF16)
        outs[1][...] = vals[1].astype(BF16)

    tok_k = pl.BlockSpec((tk, D_MODEL), lambda i, j, k: (k, 0))
    att_k = pl.BlockSpec((tk, D_ATT), lambda i, j, k: (k, 0))
    (dwo_n,) = _mm("mix_dwo", [u, dxo], [tok_k, tok_k], [(0, 1, 0)], 1, (D_MODEL, D_MODEL), TN, (1, 1, T // tk), one,
                   [jax.ShapeDtypeStruct((D_MODEL, D_MODEL), BF16)], [_whole((D_MODEL, D_MODEL))])

    def plain2(vals, ins, outs, i):
        outs[0][...] = vals[0]
        outs[1][...] = vals[1]

    ash = jax.ShapeDtypeStruct((T, D_ATT), F32)
    do_dil, do_sb = _mm(
        "mix_do", [dpd, dps, wpd_n, wps_n], [_tok(D_MODEL), _tok(D_MODEL), _whole(wpd_n.shape), _whole(wps_n.shape)],
        [(0, 2, 0), (1, 3, 1)], 2, None, NT, (T // TM, 1, 1), plain2, [ash, ash], [_tok(D_ATT)] * 2)

    psh = jax.ShapeDtypeStruct((D_ATT, D_MODEL), BF16)
    dwpd_n, dwps_n = _mm(
        "mix_dwp", [o_dil, o_sb, dpd, dps], [att_k, att_k, tok_k, tok_k], [(0, 2, 0), (1, 3, 1)], 2,
        (D_ATT, D_MODEL), TN, (1, 1, T // tk), two, [psh, psh], [_whole((D_ATT, D_MODEL))] * 2)
    dwpd, dwps = (w.reshape(D_ATT, N_CHIPS, cp).transpose(1, 0, 2) for w in (dwpd_n, dwps_n))
    dwo = dwo_n.reshape(N_CHIPS, cp, D_MODEL)

    dqs, dks, dvs = [], [], []
    for g, (window, dil) in enumerate(DIL_GROUPS):
        dq, dk, dv = _dil_bwd(proj, do_dil, o_dil, lse, g, dil)
        dqs.append(dq)
        dks.append(dk)
        dvs.append(dv)
    dq_s, dk_s, dv_s = _sb_bwd(proj, do_sb, o_sb)
    dproj = _assemble_dproj(dqs + dks, dvs + [dq_s, dk_s, dv_s], [dgd, dgs], tabs)

    dx, dgain = _mm(
        "mix_dx", [dproj, win, x, gain, dxo],
        [pl.BlockSpec((tm, N_CHIPS * cin), lambda i, j, k: (i, 0)), _wfull(D_MODEL, cin, l), row, _gain_spec(), row],
        [(_cols(0, c, cin), _pick(1, c), 0) for c in range(N_CHIPS)], 1, None, NT, (T // tm, 1, 1),
        _rms_bwd_epilogue(2, 3, 4),
        [jax.ShapeDtypeStruct((T, D_MODEL), F32), jax.ShapeDtypeStruct((8, D_MODEL), F32)],
        [row, pl.BlockSpec((8, D_MODEL), lambda i, j, k: (0, 0))])

    (dwin,) = _mm(
        "mix_dwin", [h, dproj],
        [pl.BlockSpec((tk, D_MODEL), lambda i, j, k: (k, 0)), pl.BlockSpec((tk, cin), lambda i, j, k: (k, j))],
        [(0, 1, 0)], 1, (D_MODEL, cin), TN, (1, N_CHIPS, T // tk), one,
        [jax.ShapeDtypeStruct((N_CHIPS, D_MODEL, cin), BF16)],
        [pl.BlockSpec((None, D_MODEL, cin), lambda i, j, k: (j, 0, 0))])
    return dx, dgain, dwin, dwpd, dwps, dwo


def _local_step(x, target, norms, norm_final, weights_of, on_grads):
    T = x.shape[0]
    tabs = _rope_tables(T)
    saved, held = [], []
    for l in range(DEPTH):
        w1 = weights_of(l, 0, x)
        x, s1 = _ffn_fwd(x, norms["norm_ffn1"][l:l + 1], w1["ffn1_w_gate"], w1["ffn1_w_up"], w1["ffn1_w_down"])
        w2 = weights_of(l, 1, x)
        x, s2 = _mixer_fwd(x, norms["norm_mix"][l:l + 1], w2, 0, tabs)
        w3 = weights_of(l, 2, x)
        x, s3 = _ffn_fwd(x, norms["norm_ffn2"][l:l + 1], w3["ffn2_w_gate"], w3["ffn2_w_up"], w3["ffn2_w_down"])
        saved.append((s1, s2, s3))
        held.append((w1, w2, w3))
    dx, dg_final, loss = _final_loss(x, norm_final.reshape(1, D_MODEL), target)
    gains = [None] * DEPTH
    for l in reversed(range(DEPTH)):
        s1, s2, s3 = saved[l]
        w1, w2, w3 = held[l]
        dx, dg2, dwg2, dwu2, dwd2 = _ffn_bwd(dx, norms["norm_ffn2"][l:l + 1], w3["ffn2_w_gate"], w3["ffn2_w_up"],
                                             w3["ffn2_w_down"], s3)
        dx = on_grads(l, 2, dict(ffn2_w_gate=dwg2, ffn2_w_up=dwu2, ffn2_w_down=dwd2), dx)
        dx, dgm, dwin, dwpd, dwps, dwo = _mixer_bwd(dx, norms["norm_mix"][l:l + 1], w2, 0, tabs, s2)
        dx = on_grads(l, 1, dict(w_in=dwin, w_proj_dil=dwpd, w_proj_sb=dwps, w_out=dwo), dx)
        dx, dg1, dwg1, dwu1, dwd1 = _ffn_bwd(dx, norms["norm_ffn1"][l:l + 1], w1["ffn1_w_gate"], w1["ffn1_w_up"],
                                             w1["ffn1_w_down"], s1)
        dx = on_grads(l, 0, dict(ffn1_w_gate=dwg1, ffn1_w_up=dwu1, ffn1_w_down=dwd1), dx)
        gains[l] = dict(norm_ffn1=dg1, norm_mix=dgm, norm_ffn2=dg2)
    return loss, dx, gains, dg_final


def _place():
    x, y, c = lax.axis_index("x"), lax.axis_index("y"), lax.axis_index("c")
    chips = [(1 - x, y), (x, 1 - y), (1 - x, 1 - y)]
    return x, y, c, chips


def _half(c, r):
    return pl.ds(pl.multiple_of(c * (r // 2), 8), r // 2)


def _cast_into_slot(ws, ls, me_arr, after):
    n = len(ws)
    late = [] if after is None else [after]

    def body(me_ref, *refs):
        for a in range(n):
            refs[len(refs) - n + a][...] = refs[a][...].astype(BF16)

    def src(w, l):
        return pl.BlockSpec((None, w.shape[1] // 4, w.shape[2]), lambda i, me: (l, i, 0))

    def dst(w):
        return pl.BlockSpec((None, None, w.shape[1] // 4, w.shape[2]), lambda i, me: (me[0], 0, i, 0))

    return pl.pallas_call(
        body, name="cast_weights",
        grid_spec=pltpu.PrefetchScalarGridSpec(
            num_scalar_prefetch=1, grid=(4,),
            in_specs=[src(w, l) for w, l in zip(ws, ls)] + [pl.BlockSpec(memory_space=pl.ANY)] * len(late),
            out_specs=[dst(w) for w in ws]),
        out_shape=[jax.ShapeDtypeStruct((N_CHIPS, 1) + w.shape[1:], BF16) for w in ws], compiler_params=_params(),
    )(me_arr, *ws, *late)


HBM_SPEC = pl.BlockSpec(memory_space=pltpu.HBM)
SEM_SPEC = pl.BlockSpec(memory_space=pltpu.SEMAPHORE)
SPLIT_COPY = pltpu.CompilerParams(has_side_effects=pltpu.SideEffectType.DATAFLOW_SIDE_EFFECTING)


def _gather_piece(ref, chip_id, c):
    return ref.at[chip_id, 0, _half(c, ref.shape[2]), :]


def _gather_start(tag, bufs, direct):
    n = len(bufs)

    def body(*refs):
        out_refs = refs[n:2 * n]
        send_sems, recv_sems, token = refs[2 * n:]
        x, y, c, chips = _place()
        me = 2 * x + y
        for a in range(n):
            piece = _gather_piece(out_refs[a], me, c)
            for j, chip in enumerate(chips):
                for to in ((0, 1) if direct[a] else (c,)):
                    pltpu.make_async_remote_copy(
                        src_ref=piece, dst_ref=piece, send_sem=send_sems.at[6 * a + 2 * j + to],
                        recv_sem=recv_sems.at[6 * a + 2 * j + c], device_id=(*chip, to), device_id_type=MESH).start()
        token[...] = jnp.zeros_like(token)

    outs = pl.pallas_call(
        body, name=f"gather_start_{tag}", in_specs=[HBM_SPEC] * n,
        out_specs=[HBM_SPEC] * n + [SEM_SPEC, SEM_SPEC, pl.BlockSpec(memory_space=pltpu.VMEM)],
        out_shape=[pltpu.HBM(b.shape, b.dtype) for b in bufs] + [pltpu.SemaphoreType.DMA((6 * n,))] * 2
        + [jax.ShapeDtypeStruct((8, 128), F32)],
        input_output_aliases={a: a for a in range(n)}, compiler_params=SPLIT_COPY,
    )(*[pltpu.with_memory_space_constraint(b, pltpu.HBM) for b in bufs])
    return outs[:n], outs[n], outs[n + 1], outs[n + 2]


def _gather_wait(k, bufs, places, direct, send_sems, recv_sems, after):
    m = len(bufs)

    def body(*refs):
        in_refs = refs[:m]
        ssem, rsem = refs[m], refs[m + 1]
        x, y, c, chips = _place()
        me = 2 * x + y
        for t, a in enumerate(places):
            for j, chip in enumerate(chips):
                for core in ((0, 1) if direct else (c,)):
                    cp = pltpu.make_async_remote_copy(
                        src_ref=_gather_piece(in_refs[t], me, c),
                        dst_ref=_gather_piece(in_refs[t], 2 * chip[0] + chip[1], core),
                        send_sem=ssem.at[6 * a + 2 * j + core], recv_sem=rsem.at[6 * a + 2 * j + core],
                        device_id=(*chip, core), device_id_type=MESH)
                    cp.wait_send()
                    cp.wait_recv()

    return pl.pallas_call(
        body, name=f"gather_wait_{k}",
        in_specs=[HBM_SPEC] * m + [SEM_SPEC, SEM_SPEC, pl.BlockSpec(memory_space=pl.ANY)], out_specs=[HBM_SPEC] * m,
        out_shape=[pltpu.HBM(b.shape, b.dtype) for b in bufs], input_output_aliases={t: t for t in range(m)},
        compiler_params=SPLIT_COPY,
    )(*bufs, send_sems, recv_sems, after)


def _gather_relay(bufs):
    n = len(bufs)

    def body(*refs):
        out_refs = refs[n:2 * n]
        send_sems, recv_sems = refs[2 * n:]
        x, y, c, chips = _place()
        cps = []
        for a in range(n):
            for j, chip in enumerate(chips):
                piece = _gather_piece(out_refs[a], 2 * chip[0] + chip[1], c)
                cps.append(pltpu.make_async_remote_copy(
                    src_ref=piece, dst_ref=piece, send_sem=send_sems.at[a, j], recv_sem=recv_sems.at[a, j],
                    device_id=(x, y, 1 - c), device_id_type=MESH))
        for cp in cps:
            cp.start()
        for a in range(n):
            for j, chip in enumerate(chips):
                theirs = _gather_piece(out_refs[a], 2 * chip[0] + chip[1], 1 - c)
                pltpu.make_async_remote_copy(
                    src_ref=theirs, dst_ref=theirs, send_sem=send_sems.at[a, j], recv_sem=recv_sems.at[a, j],
                    device_id=(x, y, 1 - c), device_id_type=MESH).wait_recv()
        for cp in cps:
            cp.wait_send()

    any_spec = pl.BlockSpec(memory_space=pl.ANY)
    return pl.pallas_call(
        body, name="gather_relay", in_specs=[any_spec] * n, out_specs=[any_spec] * n,
        out_shape=[jax.ShapeDtypeStruct(b.shape, b.dtype) for b in bufs],
        input_output_aliases={a: a for a in range(n)},
        scratch_shapes=[pltpu.SemaphoreType.DMA((n, 3))] * 2,
    )(*bufs)


def _other_half(ref, c):
    return ref.at[:, _half(1 - c, ref.shape[1]), :]


def _all_of(ref, c):
    return ref


def _sibling_start(name, srcs, pick, land_shapes, thru):
    n = len(srcs)
    lands = [lax.empty(sh, s.dtype) for sh, s in zip(land_shapes, srcs)]
    kept = lands + ([] if thru is None else [thru])
    m = len(kept)

    def body(*refs):
        s_refs, land_refs = refs[:n], refs[n + m:n + m + n]
        send_sems, recv_sems, token = refs[n + 2 * m:]
        x, y, c, _ = _place()
        for a in range(n):
            pltpu.make_async_remote_copy(
                src_ref=pick(s_refs[a], c), dst_ref=land_refs[a], send_sem=send_sems.at[a],
                recv_sem=recv_sems.at[a], device_id=(x, y, 1 - c), device_id_type=MESH).start()
        token[...] = jnp.zeros_like(token)

    outs = pl.pallas_call(
        body, name=name, in_specs=[HBM_SPEC] * (n + m),
        out_specs=[HBM_SPEC] * m + [SEM_SPEC, SEM_SPEC, pl.BlockSpec(memory_space=pltpu.VMEM)],
        out_shape=[pltpu.HBM(v.shape, v.dtype) for v in kept] + [pltpu.SemaphoreType.DMA((n,))] * 2
        + [jax.ShapeDtypeStruct((8, 128), F32)],
        input_output_aliases={n + a: a for a in range(m)}, compiler_params=SPLIT_COPY,
    )(*[pltpu.with_memory_space_constraint(v, pltpu.HBM) for v in list(srcs) + kept])
    return (outs[:n], outs[m], outs[m + 1]), (outs[n] if thru is not None else None), outs[m + 2]


def _sibling_wait(name, srcs, pick, lands, send_sems, recv_sems, after):
    n = len(srcs)

    def body(*refs):
        s_refs, land_refs = refs[:n], refs[n:2 * n]
        ssem, rsem = refs[2 * n], refs[2 * n + 1]
        x, y, c, _ = _place()
        for a in range(n):
            cp = pltpu.make_async_remote_copy(
                src_ref=pick(s_refs[a], c), dst_ref=land_refs[a], send_sem=ssem.at[a], recv_sem=rsem.at[a],
                device_id=(x, y, 1 - c), device_id_type=MESH)
            cp.wait_send()
            cp.wait_recv()

    return pl.pallas_call(
        body, name=name, in_specs=[HBM_SPEC] * (2 * n) + [SEM_SPEC, SEM_SPEC, pl.BlockSpec(memory_space=pl.ANY)],
        out_specs=[HBM_SPEC] * n, out_shape=[pltpu.HBM(v.shape, v.dtype) for v in lands],
        input_output_aliases={n + a: a for a in range(n)}, compiler_params=SPLIT_COPY,
    )(*srcs, *lands, send_sems, recv_sems, after)


def _add_half(gs, gots, c_arr):
    n = len(gs)

    def body(c_ref, *refs):
        for a in range(n):
            refs[2 * n + a][...] = (refs[a][...].astype(F32) + refs[n + a][...].astype(F32)).astype(BF16)

    def own(g):
        return pl.BlockSpec((None, g.shape[1] // 2, g.shape[2]), lambda k, cr: (k, cr[0], 0))

    def half(g):
        return pl.BlockSpec((None, g.shape[1] // 2, g.shape[2]), lambda k, cr: (k, 0, 0))

    return pl.pallas_call(
        body, name="grad_add_half",
        grid_spec=pltpu.PrefetchScalarGridSpec(
            num_scalar_prefetch=1, grid=(N_CHIPS,),
            in_specs=[own(g) for g in gs] + [half(g) for g in gs], out_specs=[half(g) for g in gs]),
        out_shape=[jax.ShapeDtypeStruct(got.shape, BF16) for got in gots], compiler_params=_params(),
    )(c_arr, *gs, *gots)


def _scatter_start(k, ss, thru):
    n = len(ss)

    def body(*refs):
        s_refs, land_refs = refs[2 * n + 1:3 * n + 1], refs[3 * n + 1:4 * n + 1]
        send_sems, recv_sems = refs[4 * n + 2:]
        x, y, c, chips = _place()
        me = 2 * x + y
        for a in range(n):
            for j, chip in enumerate(chips):
                pltpu.make_async_remote_copy(
                    src_ref=s_refs[a].at[2 * chip[0] + chip[1]], dst_ref=land_refs[a].at[me],
                    send_sem=send_sems.at[3 * a + j], recv_sem=recv_sems.at[3 * a + j], device_id=(*chip, c),
                    device_id_type=MESH).start()

    lands = [lax.empty(s.shape, s.dtype) for s in ss]
    hbm = [pltpu.HBM(s.shape, s.dtype) for s in ss]
    outs = pl.pallas_call(
        body, name=f"grad_scatter_start_{k}", in_specs=[HBM_SPEC] * (2 * n + 1),
        out_specs=[HBM_SPEC] * (2 * n + 1) + [SEM_SPEC, SEM_SPEC],
        out_shape=hbm + hbm + [pltpu.HBM(thru.shape, thru.dtype)] + [pltpu.SemaphoreType.DMA((3 * n,))] * 2,
        input_output_aliases={a: a for a in range(2 * n + 1)}, compiler_params=SPLIT_COPY,
    )(*[pltpu.with_memory_space_constraint(v, pltpu.HBM) for v in list(ss) + lands + [thru]])
    return (outs[:n], outs[n:2 * n], outs[2 * n + 1], outs[2 * n + 2]), outs[2 * n]


def _scatter_wait(k, ss, lands, send_sems, recv_sems, after):
    n = len(ss)

    def body(*refs):
        s_refs, land_refs = refs[:n], refs[n:2 * n]
        ssem, rsem = refs[2 * n], refs[2 * n + 1]
        x, y, c, chips = _place()
        me = 2 * x + y
        for a in range(n):
            for j, chip in enumerate(chips):
                cid = 2 * chip[0] + chip[1]
                cp = pltpu.make_async_remote_copy(
                    src_ref=s_refs[a].at[cid], dst_ref=land_refs[a].at[cid], send_sem=ssem.at[3 * a + j],
                    recv_sem=rsem.at[3 * a + j], device_id=(*chip, c), device_id_type=MESH)
                cp.wait_send()
                cp.wait_recv()

    hbm = [pltpu.HBM(s.shape, s.dtype) for s in ss]
    outs = pl.pallas_call(
        body, name=f"grad_scatter_wait_{k}",
        in_specs=[HBM_SPEC] * (2 * n) + [SEM_SPEC, SEM_SPEC, pl.BlockSpec(memory_space=pl.ANY)],
        out_specs=[HBM_SPEC] * (2 * n), out_shape=hbm + hbm,
        input_output_aliases={a: a for a in range(2 * n)}, compiler_params=SPLIT_COPY,
    )(*ss, *lands, send_sems, recv_sems, after)
    return outs[:n], outs[n:]


def _sum_chips(lands, ss, me_arr):
    n = len(lands)

    def body(me_ref, *refs):
        for own in range(N_CHIPS):
            @pl.when(me_ref[0] == own)
            def _(own=own):
                for a in range(n):
                    acc = None
                    for k in range(N_CHIPS):
                        term = (refs[n + a][...] if k == own else refs[a][k]).astype(F32)
                        acc = term if acc is None else acc + term
                    refs[2 * n + a][...] = acc

    return pl.pallas_call(
        body, name="grad_sum_chips",
        grid_spec=pltpu.PrefetchScalarGridSpec(
            num_scalar_prefetch=1, grid=(1,),
            in_specs=[pl.BlockSpec(la.shape, lambda i, me: (0, 0, 0)) for la in lands]
            + [pl.BlockSpec((None,) + la.shape[1:], lambda i, me: (me[0], 0, 0)) for la in lands],
            out_specs=[pl.BlockSpec(la.shape[1:], lambda i, me: (0, 0)) for la in lands]),
        out_shape=[jax.ShapeDtypeStruct(la.shape[1:], F32) for la in lands], compiler_params=_params(),
    )(me_arr, *lands, *ss)


def _allreduce_rows(stats):
    def body(s_ref, o_ref, buf, send_sems, recv_sems):
        x, y, c, _ = _place()
        me = 4 * x + 2 * y + c
        buf[me] = s_ref[...]
        cps = []
        for k in range(1, 8):
            px = jnp.where(k & 4, 1 - x, x)
            py = jnp.where(k & 2, 1 - y, y)
            pc = jnp.where(k & 1, 1 - c, c)
            cps.append(pltpu.make_async_remote_copy(
                src_ref=s_ref, dst_ref=buf.at[me], send_sem=send_sems.at[k - 1], recv_sem=recv_sems.at[k - 1],
                device_id=(px, py, pc), device_id_type=MESH))
        for cp in cps:
            cp.start()
        for cp in cps:
            cp.wait()
        acc = buf[0]
        for d in range(1, 8):
            acc = acc + buf[d]
        o_ref[...] = acc

    vm = pl.BlockSpec(memory_space=pltpu.VMEM)
    return pl.pallas_call(
        body, name="allreduce_rows", in_specs=[vm], out_specs=vm,
        out_shape=jax.ShapeDtypeStruct(stats.shape, F32),
        scratch_shapes=[pltpu.VMEM((8,) + stats.shape, F32), pltpu.SemaphoreType.DMA((7,)),
                        pltpu.SemaphoreType.DMA((7,))],
    )(stats)


def _adamw_math(w, g, m, v):
    m = ADAM_B1 * m + (1.0 - ADAM_B1) * g
    v = ADAM_B2 * v + (1.0 - ADAM_B2) * (g * g)
    m_hat = m / (1.0 - ADAM_B1 ** ADAM_STEP)
    v_hat = v / (1.0 - ADAM_B2 ** ADAM_STEP)
    delta = -ADAM_LR * (m_hat / (jnp.sqrt(v_hat) + ADAM_EPS) + ADAM_WD * w)
    return delta, m, v


def _adamw(ws, ms, vs, mines, theirs, l, c_arr, earlier, after):
    n = len(ws)
    held = [t for e in earlier if e is not None for t in e]
    assert len(held) in (0, 4 * n)
    late = [] if after is None else [after]

    def body(c_ref, *refs):
        outs = refs[len(refs) - 4 * n:]
        for a in range(n):
            w_ref, m_ref, v_ref, a_ref, b_ref = refs[5 * a:5 * a + 5]
            g = jnp.where(pl.program_id(0) == c_ref[0], a_ref[...], b_ref[...])
            delta, mn, vn = _adamw_math(w_ref[...], g, m_ref[...], v_ref[...])
            outs[4 * a][...] = g
            outs[4 * a + 1][...] = delta
            outs[4 * a + 2][...] = mn
            outs[4 * a + 3][...] = vn

    def blk(w):
        tr = w.shape[1] // 4
        return pl.BlockSpec((None, tr, w.shape[2]), lambda hh, i, cr: (l, 2 * hh + i, 0))

    def half(w):
        return pl.BlockSpec((w.shape[1] // 4, w.shape[2]), lambda hh, i, cr: (i, 0))

    outs = pl.pallas_call(
        body, name="adamw",
        grid_spec=pltpu.PrefetchScalarGridSpec(
            num_scalar_prefetch=1, grid=(2, 2),
            in_specs=[sp for w in ws for sp in (blk(w), blk(w), blk(w), half(w), half(w))]
            + [pl.BlockSpec(memory_space=pl.ANY)] * (len(held) + len(late)),
            out_specs=[blk(w) for w in ws for _ in range(4)]),
        out_shape=[jax.ShapeDtypeStruct(w.shape, F32) for w in ws for _ in range(4)],
        input_output_aliases={1 + 5 * n + t: t for t in range(len(held))}, compiler_params=_params(),
    )(c_arr, *[t for grp in zip(ws, ms, vs, mines, theirs) for t in grp], *held, *late)
    return [outs[4 * a:4 * a + 4] for a in range(n)]


def _adamw_rows(w, m, v, g):
    def body(w_ref, m_ref, v_ref, g_ref, d_ref, mo_ref, vo_ref):
        delta, mn, vn = _adamw_math(w_ref[...], g_ref[...], m_ref[...], v_ref[...])
        d_ref[...] = delta
        mo_ref[...] = mn
        vo_ref[...] = vn

    vm = pl.BlockSpec(memory_space=pltpu.VMEM)
    sh = jax.ShapeDtypeStruct(w.shape, F32)
    return pl.pallas_call(body, name="adamw_rows", in_specs=[vm] * 4, out_specs=[vm] * 3, out_shape=[sh] * 3)(w, m, v, g)


SUBLAYERS = (("ffn1_w_gate", "ffn1_w_up", "ffn1_w_down"), ("w_in", "w_proj_dil", "w_proj_sb", "w_out"),
             ("ffn2_w_gate", "ffn2_w_up", "ffn2_w_down"))
TRANSPOSED = ("ffn1_w_gate", "ffn1_w_up", "ffn2_w_gate", "ffn2_w_up")


def _pick_row(blocks):
    row = lax.broadcasted_iota(jnp.int32, (8, D_MODEL), 0)
    out = jnp.zeros((8, D_MODEL), F32)
    for i, b in enumerate(blocks):
        out = out + jnp.where(row == i, b, 0.0)
    return out


def kernel(x, norm_ffn1, ffn1_w_gate, ffn1_w_up, ffn1_w_down, norm_mix, w_in, w_proj_dil, w_proj_sb, w_out, norm_ffn2, ffn2_w_gate, ffn2_w_up, ffn2_w_down, norm_final, loss_target, m_norm_ffn1, m_ffn1_w_gate, m_ffn1_w_up, m_ffn1_w_down, m_norm_mix, m_w_in, m_w_proj_dil, m_w_proj_sb, m_w_out, m_norm_ffn2, m_ffn2_w_gate, m_ffn2_w_up, m_ffn2_w_down, m_norm_final, v_norm_ffn1, v_ffn1_w_gate, v_ffn1_w_up, v_ffn1_w_down, v_norm_mix, v_w_in, v_w_proj_dil, v_w_proj_sb, v_w_out, v_norm_ffn2, v_ffn2_w_gate, v_ffn2_w_up, v_ffn2_w_down, v_norm_final):
    given = dict(locals())
    for n in TRANSPOSED:
        for k in ("", "m_", "v_"):
            given[k + n] = jnp.swapaxes(given[k + n], 1, 2)
    weights = {n: given[n] for n in WEIGHT_NAMES}
    norms = {n: given[n] for n in NORM_NAMES}

    c_arr = lax.axis_index("c").astype(jnp.int32).reshape(1)
    me_arr = (2 * lax.axis_index("x") + lax.axis_index("y")).astype(jnp.int32).reshape(1)
    order = [(l, s, n) for l in range(DEPTH) for s in range(len(SUBLAYERS)) for n in SUBLAYERS[s]]
    n_first = len(SUBLAYERS[0])
    sent, token = {}, None
    for tag, idxs in (("a", range(n_first)), ("b", range(n_first, len(order)))):
        cast = _cast_into_slot([weights[order[i][2]] for i in idxs], [order[i][0] for i in idxs], me_arr, token)
        bufs, send_sems, recv_sems, token = _gather_start(tag, cast, [order[i][0] > 0 for i in idxs])
        for p, i in enumerate(idxs):
            sent[i] = (bufs[p], p, send_sems, recv_sems)

    def weights_of(l, s, after):
        idxs = [i for i, (ll, ss, _) in enumerate(order) if (ll, ss) == (l, s)]
        got = _gather_wait(len(SUBLAYERS) * l + s, [sent[i][0] for i in idxs], [sent[i][1] for i in idxs], l > 0,
                           sent[idxs[0]][2], sent[idxs[0]][3], after)
        return {order[i][2]: g for i, g in zip(idxs, got if l > 0 else _gather_relay(got))}

    out = {}
    to_add, in_flight = [], []

    def add_and_scatter(after):
        l, s, names, gs, lands, ssem, rsem = to_add.pop(0)
        k = len(SUBLAYERS) * l + s
        got = _sibling_wait(f"grad_exchange_wait_{k}", gs, _other_half, lands, ssem, rsem, after)
        sent, after = _scatter_start(k, _add_half(gs, got, c_arr), after)
        in_flight.append((l, s, names) + sent)
        return after

    def on_grads(l, s, grads, after):
        names = list(grads)
        k = len(SUBLAYERS) * l + s
        gs = [grads[n] for n in names]
        sent, after, _ = _sibling_start(f"grad_exchange_start_{k}", gs, _other_half,
                                        [(g.shape[0], g.shape[1] // 2, g.shape[2]) for g in gs], after)
        if to_add:
            after = add_and_scatter(after)
        to_add.append((l, s, names, gs) + sent)
        return after

    loss_blk, grad_x, gains, dg_final = _local_step(x[0], loss_target[0], norms, norm_final, weights_of, on_grads)
    grad_x = add_and_scatter(grad_x)

    def update(l, names, mine, swap, after):
        theirs = _sibling_wait(f"grad_swap_wait_{l}_{names[0]}", mine, _all_of, *swap, grad_x if after is None else after)
        res = _adamw([weights[n] for n in names], [given["m_" + n] for n in names], [given["v_" + n] for n in names],
                     mine, theirs, l, c_arr, [out.get(n) for n in names], after)
        out.update(zip(names, res))

    waiting = None
    for l, s, names, sums, lands, ssem, rsem in in_flight:
        sums, lands = _scatter_wait(len(SUBLAYERS) * l + s, sums, lands, ssem, rsem, grad_x)
        mine = _sum_chips(lands, sums, me_arr)
        swap, _, token = _sibling_start(f"grad_swap_start_{l}_{names[0]}", mine, _all_of,
                                        [m.shape for m in mine], None)
        if waiting is not None:
            update(*waiting, token)
        waiting = (l, names, mine, swap)
    update(*waiting, None)
    out = {k + n: (jnp.swapaxes(v, 1, 2) if n in TRANSPOSED else v)
           for n, res in out.items() for k, v in zip(("grad_", "delta_", "new_m_", "new_v_"), res)}
    out["grad_x"] = grad_x[None]

    rows = [gains[l][n] for n in NORM_NAMES for l in range(DEPTH)] + [dg_final, loss_blk]
    total = _allreduce_rows(_pick_row(rows))
    out["loss"] = total[7, 0]
    wn = jnp.concatenate([given[n] for n in NORM_NAMES] + [norm_final[None], jnp.zeros((1, D_MODEL), F32)])
    mn_ = jnp.concatenate([given["m_" + n] for n in NORM_NAMES] + [m_norm_final[None], jnp.zeros((1, D_MODEL), F32)])
    vn_ = jnp.concatenate([given["v_" + n] for n in NORM_NAMES] + [v_norm_final[None], jnp.ones((1, D_MODEL), F32)])
    d_n, m_n, v_n = _adamw_rows(wn, mn_, vn_, total)
    for i, n in enumerate(NORM_NAMES):
        sl = slice(i * DEPTH, (i + 1) * DEPTH)
        out["grad_" + n], out["delta_" + n], out["new_m_" + n], out["new_v_" + n] = total[sl], d_n[sl], m_n[sl], v_n[sl]
    out["grad_norm_final"], out["delta_norm_final"] = total[6], d_n[6]
    out["new_m_norm_final"], out["new_v_norm_final"] = m_n[6], v_n[6]

    names = ["norm_ffn1", "ffn1_w_gate", "ffn1_w_up", "ffn1_w_down", "norm_mix", "w_in", "w_proj_dil", "w_proj_sb",
             "w_out", "norm_ffn2", "ffn2_w_gate", "ffn2_w_up", "ffn2_w_down", "norm_final"]
    return (out["loss"], out["grad_x"], *[out["grad_" + n] for n in names], *[out["delta_" + n] for n in names],
            *[out["new_m_" + n] for n in names], *[out["new_v_" + n] for n in names])
```

```python
import functools

import jax
import jax.numpy as jnp
from jax import lax
from jax.experimental import pallas as pl
from jax.experimental.pallas import tpu as pltpu

F32 = jnp.float32
BF16 = jnp.bfloat16

D_MODEL = 1024
DEPTH = 2
N_CHIPS = 4
HEAD_DIM = 64
ROPE_DIM = 16
ROPE_THETA = 500000.0
DIL_GROUPS = ((128, 1), (512, 4), (2048, 16))
SPAN = 128
Q_BLOCK = 128
RMS_EPS = 1e-6
D_ATT = 256
COL_QS = 2304
COL_GD = 3072
COL_GS = 4096
ADAM_LR, ADAM_B1, ADAM_B2, ADAM_EPS, ADAM_WD, ADAM_STEP = 0.001, 0.9, 0.999, 1e-08, 0.01, 10

VMEM_LIMIT = 52 * 1024 * 1024
TM = 512
NEG = -1e30

NN = (((1,), (0,)), ((), ()))
NT = (((1,), (1,)), ((), ()))
TN = (((0,), (0,)), ((), ()))
MESH = pl.DeviceIdType.MESH

WEIGHT_NAMES = ("ffn1_w_gate", "ffn1_w_up", "ffn1_w_down", "w_in", "w_proj_dil",
                "w_proj_sb", "w_out", "ffn2_w_gate", "ffn2_w_up", "ffn2_w_down")
NORM_NAMES = ("norm_ffn1", "norm_mix", "norm_ffn2")


def _params(**kw):
    return pltpu.CompilerParams(vmem_limit_bytes=VMEM_LIMIT, **kw)


def _sigmoid(x):
    return 0.5 * jnp.tanh(0.5 * x) + 0.5


def _mm_body(pairs, n_in, n_out, n_acc, dims, nk, epilogue, *refs):
    ins = refs[:n_in]
    outs = refs[n_in:n_in + n_out]
    accs = refs[n_in + n_out:]
    i = pl.program_id(0)
    k = pl.program_id(2)

    def operand(a):
        return (a(ins) if callable(a) else ins[a][...]).astype(BF16)

    def dot(ia, ib):
        return lax.dot_general(operand(ia), operand(ib), dims, preferred_element_type=F32)

    if nk == 1:
        parts = [None] * n_acc
        for ia, ib, ic in pairs:
            parts[ic] = dot(ia, ib) if parts[ic] is None else parts[ic] + dot(ia, ib)
        epilogue(parts, ins, outs, i)
        return

    @pl.when(k == 0)
    def _():
        for c in range(n_acc):
            accs[c][...] = jnp.zeros_like(accs[c])

    for ia, ib, ic in pairs:
        accs[ic][...] += dot(ia, ib)

    @pl.when(k == nk - 1)
    def _():
        epilogue([a[...] for a in accs], ins, outs, i)


def _mm(name, ins, in_specs, pairs, n_acc, acc_shape, dims, grid, epilogue, out_shapes, out_specs):
    nk = grid[2]
    scratch = [pltpu.VMEM(acc_shape, F32) for _ in range(n_acc)] if nk > 1 else []
    body = functools.partial(_mm_body, tuple(pairs), len(ins), len(out_shapes), n_acc, dims, nk, epilogue)
    return pl.pallas_call(
        body, name=name, grid=grid, in_specs=in_specs, out_specs=out_specs, out_shape=out_shapes,
        scratch_shapes=scratch,
        compiler_params=_params(dimension_semantics=("arbitrary", "arbitrary", "arbitrary")),
    )(*ins)


def _rms_bwd_epilogue(x_idx, g_idx, dxo_idx):
    def ep(vals, ins, outs, i):
        dh = vals[0]
        x = ins[x_idx][...]
        g = ins[g_idx][...]
        rstd = lax.rsqrt(jnp.mean(x * x, axis=-1, keepdims=True) + RMS_EPS)
        xhat = x * rstd
        dxhat = dh * g
        dx = rstd * (dxhat - xhat * jnp.mean(dxhat * xhat, axis=-1, keepdims=True))
        outs[0][...] = ins[dxo_idx][...] + dx
        dg = jnp.broadcast_to(jnp.sum(dh * xhat, axis=0, keepdims=True), outs[1].shape)

        @pl.when(i == 0)
        def _():
            outs[1][...] = dg

        @pl.when(i > 0)
        def _():
            outs[1][...] += dg
    return ep


def _normed(x_idx, g_idx):
    seen = {}

    def f(ins):
        if id(ins) not in seen:
            xv = ins[x_idx][...]
            h = xv * lax.rsqrt(jnp.mean(xv * xv, axis=-1, keepdims=True) + RMS_EPS)
            seen[id(ins)] = (ins, (h * ins[g_idx][...]).astype(BF16))
        return seen[id(ins)][1]
    return f


def _rope_tables(T):
    half = ROPE_DIM // 2
    lane = jnp.arange(128) % HEAD_DIM
    inv_freq = ROPE_THETA ** (-(2 * (lane % half)).astype(F32) / ROPE_DIM)
    ang = jnp.arange(T, dtype=F32)[:, None] * inv_freq[None, :]
    cos, sin = jnp.cos(ang), jnp.sin(ang)
    c = jnp.where(lane < ROPE_DIM, cos, 1.0)
    s1 = jnp.where(lane < half, -sin, 0.0)
    s2 = jnp.where((lane >= half) & (lane < ROPE_DIM), sin, 0.0)
    return c, s1, s2


def _rope_fwd(xv, c, s1, s2):
    w = xv.shape[1]
    half = ROPE_DIM // 2
    return xv * c + pltpu.roll(xv, w - half, 1) * s1 + pltpu.roll(xv, half, 1) * s2


def _rope_bwd(dy, c, s1, s2):
    w = dy.shape[1]
    half = ROPE_DIM // 2
    return dy * c + pltpu.roll(dy * s1, half, 1) + pltpu.roll(dy * s2, w - half, 1)


def _assemble_dproj(dqk, rest, turned, gates, tabs):
    T = gates[0].shape[0]
    n_qk, n_rest = len(dqk), len(rest) + len(turned)
    width = (n_qk + n_rest) * D_ATT + 2 * D_MODEL

    def body(*refs):
        ins, (c_ref, s1_ref, s2_ref), o_ref = refs[:n_qk + n_rest + 2], refs[-4:-1], refs[-1]
        c = jnp.concatenate([c_ref[...]] * 2, axis=1)
        s1 = jnp.concatenate([s1_ref[...]] * 2, axis=1)
        s2 = jnp.concatenate([s2_ref[...]] * 2, axis=1)
        for b in range(n_qk + n_rest):
            v = ins[b][...]
            if b < n_qk:
                v = _rope_bwd(v, c, s1, s2)
            if b >= n_qk + len(rest):
                v = v.T
            o_ref[:, b * D_ATT:(b + 1) * D_ATT] = v.astype(BF16)
        off = (n_qk + n_rest) * D_ATT
        o_ref[:, off:off + D_MODEL] = ins[-2][...]
        o_ref[:, off + D_MODEL:] = ins[-1][...]

    att = pl.BlockSpec((TM, D_ATT), lambda i: (i, 0))
    att_turned = pl.BlockSpec((D_ATT, TM), lambda i: (0, i))
    wide = pl.BlockSpec((TM, D_MODEL), lambda i: (i, 0))
    tab = pl.BlockSpec((TM, 128), lambda i: (i, 0))
    return pl.pallas_call(
        body, name="assemble_dproj", grid=(T // TM,),
        in_specs=[att] * (n_qk + len(rest)) + [att_turned] * len(turned) + [wide, wide, tab, tab, tab],
        out_specs=pl.BlockSpec((TM, width), lambda i: (i, 0)),
        out_shape=jax.ShapeDtypeStruct((T, width), BF16), compiler_params=_params(),
    )(*dqk, *rest, *turned, *gates, *tabs)


def _dil_merge(os_, lses):
    T = os_[0].shape[0]

    def body(o0, o1, o2, l0, l1, l2, o_ref, lse_ref):
        a, b, c = l0[...], l1[...], l2[...]
        m = jnp.maximum(jnp.maximum(a, b), c)
        ea, eb, ec = jnp.exp(a - m), jnp.exp(b - m), jnp.exp(c - m)
        den = ea + eb + ec
        o_ref[...] = (ea * o0[...] + eb * o1[...] + ec * o2[...]) / den
        lse_ref[...] = m + jnp.log(den)

    blk = pl.BlockSpec((TM, D_ATT), lambda i: (i, 0))
    sh = jax.ShapeDtypeStruct((T, D_ATT), F32)
    return pl.pallas_call(
        body, name="dil_merge", grid=(T // TM,), in_specs=[blk] * 6, out_specs=[blk, blk],
        out_shape=[sh, sh], compiler_params=_params(),
    )(*os_, *lses)


def _final_loss(x, gain, target):
    T = x.shape[0]

    def body(x_ref, g_ref, t_ref, dx_ref, dg_ref, loss_ref):
        xv = x_ref[...]
        g = g_ref[...]
        rstd = lax.rsqrt(jnp.mean(xv * xv, axis=-1, keepdims=True) + RMS_EPS)
        xhat = xv * rstd
        err = xhat * g - t_ref[...]
        loss = 0.5 * jnp.sum(jnp.mean(err * err, axis=-1, keepdims=True), axis=0, keepdims=True)
        dy = err * (1.0 / D_MODEL)
        dxhat = dy * g
        dx_ref[...] = rstd * (dxhat - xhat * jnp.mean(dxhat * xhat, axis=-1, keepdims=True))
        dg = jnp.broadcast_to(jnp.sum(dy * xhat, axis=0, keepdims=True), dg_ref.shape)
        ls = jnp.broadcast_to(loss, loss_ref.shape)

        @pl.when(pl.program_id(0) == 0)
        def _():
            dg_ref[...] = dg
            loss_ref[...] = ls

        @pl.when(pl.program_id(0) > 0)
        def _():
            dg_ref[...] += dg
            loss_ref[...] += ls

    blk = pl.BlockSpec((TM, D_MODEL), lambda i: (i, 0))
    row = pl.BlockSpec((1, D_MODEL), lambda i: (0, 0))
    acc = pl.BlockSpec((8, D_MODEL), lambda i: (0, 0))
    return pl.pallas_call(
        body, name="final_loss", grid=(T // TM,), in_specs=[blk, row, blk], out_specs=[blk, acc, acc],
        out_shape=[jax.ShapeDtypeStruct((T, D_MODEL), F32), jax.ShapeDtypeStruct((8, D_MODEL), F32),
                   jax.ShapeDtypeStruct((8, D_MODEL), F32)],
        compiler_params=_params(dimension_semantics=("arbitrary",)),
    )(x, gain, target)


def _pair_masks():
    lane = lax.broadcasted_iota(jnp.int32, (SPAN, 128), 1)
    return [lane < HEAD_DIM, lane >= HEAD_DIM]


def _stack_heads(x, masks):
    return jnp.concatenate([jnp.where(m, x, 0.0) for m in masks], axis=0)


def _unstack_heads(y, masks):
    rows = y.shape[0] // len(masks)
    out = jnp.where(masks[0], y[:rows], 0.0)
    for h in range(1, len(masks)):
        out = out + jnp.where(masks[h], y[rows * h:rows * (h + 1)], 0.0)
    return out


DIL_PAIR = 2


def _dil_rows(idx, d):
    u = idx // d
    r = idx - u * d
    own = pl.ds(u * (SPAN * d) + r, SPAN, stride=d) if d > 1 else pl.ds(pl.multiple_of(u * SPAN, SPAN), SPAN)
    up = jnp.maximum(u - 1, 0)
    prev = pl.ds(up * (SPAN * d) + r, SPAN, stride=d) if d > 1 else pl.ds(pl.multiple_of(up * SPAN, SPAN), SPAN)
    return u, own, prev


def _dil_valid(u):
    qi = lax.broadcasted_iota(jnp.int32, (2 * SPAN, 2 * SPAN), 0) & (SPAN - 1)
    kj = lax.broadcasted_iota(jnp.int32, (2 * SPAN, 2 * SPAN), 1)
    in_prev = (kj < SPAN) & (kj >= qi + jnp.where(u > 0, 0, SPAN))
    return in_prev | ((kj >= SPAN) & (kj - SPAN <= qi))


def _dil_keys(ref, own, prev):
    return jnp.concatenate([ref[prev, :], ref[own, :]], axis=0).astype(BF16)


def _dil_fwd(proj, g, d):
    T = proj.shape[0]
    n_iter = T // SPAN

    def body(q_ref, k_ref, v_ref, o_ref, lse_ref):
        masks = _pair_masks()

        def step(pair, carry):
            its = [_dil_rows(DIL_PAIR * pair + e, d) for e in range(DIL_PAIR)]
            qs = [_stack_heads(q_ref[own, :] * (HEAD_DIM ** -0.5), masks).astype(BF16) for _, own, _ in its]
            kks = [_dil_keys(k_ref, own, prev) for _, own, prev in its]
            vvs = [_dil_keys(v_ref, own, prev) for _, own, prev in its]
            ss = [jnp.where(_dil_valid(u), lax.dot_general(q, kk, NT, preferred_element_type=F32), NEG)
                  for (u, _, _), q, kk in zip(its, qs, kks)]
            ms = [jnp.max(s, axis=1, keepdims=True) for s in ss]
            ps = [jnp.exp(s - m) for s, m in zip(ss, ms)]
            dens = [jnp.sum(p, axis=1, keepdims=True) for p in ps]
            pvs = [lax.dot_general(p.astype(BF16), vv, NN, preferred_element_type=F32) / den
                   for p, vv, den in zip(ps, vvs, dens)]
            for (_, own, _), pv, m, den in zip(its, pvs, ms, dens):
                o_ref[own, :] = _unstack_heads(pv, masks)
                lse_ref[own, :] = _unstack_heads(jnp.broadcast_to(m + jnp.log(den), pv.shape), masks)
            return carry

        lax.fori_loop(0, n_iter // DIL_PAIR, step, 0)

    def col(b):
        return pl.BlockSpec((T, 128), lambda p: (0, b + p))

    sh = jax.ShapeDtypeStruct((T, D_ATT), F32)
    out = pl.BlockSpec((T, 128), lambda p: (0, p))
    return pl.pallas_call(
        body, name=f"dil_fwd_d{d}", grid=(2,),
        in_specs=[col(2 * g), col(6 + 2 * g), col(12 + 2 * g)], out_specs=[out, out], out_shape=[sh, sh],
        compiler_params=_params(dimension_semantics=("arbitrary",)),
    )(proj, proj, proj)


def _dil_bwd(proj, do, o_dil, lse, g, d):
    T = proj.shape[0]
    n_iter = T // SPAN

    def body(q_ref, k_ref, v_ref, do_ref, o_ref, lse_ref, dq_ref, dk_ref, dv_ref):
        masks = _pair_masks()
        head_lanes = jnp.concatenate(masks, axis=0)

        def step(pair, carry):
            its = [_dil_rows(DIL_PAIR * pair + e, d) for e in range(DIL_PAIR)]
            qs = [_stack_heads(q_ref[own, :] * (HEAD_DIM ** -0.5), masks).astype(BF16) for _, own, _ in its]
            kks = [_dil_keys(k_ref, own, prev) for _, own, prev in its]
            vvs = [_dil_keys(v_ref, own, prev) for _, own, prev in its]
            doms = [_stack_heads(do_ref[own, :], masks) for _, own, _ in its]
            dos = [dom.astype(BF16) for dom in doms]
            deltas = [jnp.sum(dom * jnp.concatenate([o_ref[own, :]] * 2, axis=0), axis=1, keepdims=True)
                      for dom, (_, own, _) in zip(doms, its)]
            lrows = [jnp.max(jnp.where(head_lanes, jnp.concatenate([lse_ref[own, :]] * 2, axis=0), NEG),
                             axis=1, keepdims=True) for _, own, _ in its]
            ss = [lax.dot_general(q, kk, NT, preferred_element_type=F32) for q, kk in zip(qs, kks)]
            dps = [lax.dot_general(do_b, vv, NT, preferred_element_type=F32) for do_b, vv in zip(dos, vvs)]
            ps = [jnp.where(_dil_valid(u), jnp.exp(s - lrow), 0.0) for (u, _, _), s, lrow in zip(its, ss, lrows)]
            dss = [(p * (dp - delta)).astype(BF16) for p, dp, delta in zip(ps, dps, deltas)]
            dqs = [lax.dot_general(ds, kk, NN, preferred_element_type=F32) for ds, kk in zip(dss, kks)]
            dkks = [lax.dot_general(ds, q, TN, preferred_element_type=F32) for ds, q in zip(dss, qs)]
            dvvs = [lax.dot_general(p.astype(BF16), do_b, TN, preferred_element_type=F32) for p, do_b in zip(ps, dos)]
            for (_, own, prev), dq, dkk, dvv in zip(its, dqs, dkks, dvvs):
                dq_ref[own, :] = _unstack_heads(dq, masks) * (HEAD_DIM ** -0.5)
                dk_ref[own, :] = dkk[SPAN:]
                dv_ref[own, :] = dvv[SPAN:]
                dk_ref[prev, :] = dk_ref[prev, :] + dkk[:SPAN]
                dv_ref[prev, :] = dv_ref[prev, :] + dvv[:SPAN]
            return carry

        lax.fori_loop(0, n_iter // DIL_PAIR, step, 0)

    def col(b):
        return pl.BlockSpec((T, 128), lambda p: (0, b + p))

    sh = jax.ShapeDtypeStruct((T, D_ATT), F32)
    return pl.pallas_call(
        body, name=f"dil_bwd_d{d}", grid=(2,),
        in_specs=[col(2 * g), col(6 + 2 * g), col(12 + 2 * g), col(0), col(0), col(0)],
        out_specs=[col(0), col(0), col(0)], out_shape=[sh, sh, sh],
        compiler_params=_params(dimension_semantics=("arbitrary",)),
    )(proj, proj, proj, do, o_dil, lse)


SB_KT = 512


def _sb_tri(strict):
    a = lax.broadcasted_iota(jnp.int32, (Q_BLOCK, Q_BLOCK), 0)
    b = lax.broadcasted_iota(jnp.int32, (Q_BLOCK, Q_BLOCK), 1)
    return jnp.where((a > b) if strict else (a >= b), 1.0, 0.0).astype(BF16)


def _split_stack(x):
    nb = x.shape[1] // Q_BLOCK
    blocks = [x[:, Q_BLOCK * b:Q_BLOCK * (b + 1)] for b in range(nb)]
    hi = [b.astype(BF16) for b in blocks]
    lo = [(b - h.astype(F32)).astype(BF16) for b, h in zip(blocks, hi)]
    return blocks, jnp.concatenate(hi + lo, axis=0)


def _suffix_from(y, blocks, c):
    r = blocks[0].shape[0]
    nb = len(blocks)
    outs = [None] * nb
    run = c
    for b in reversed(range(nb)):
        outs[b] = run + y[r * b:r * (b + 1)] + y[r * (nb + b):r * (nb + b + 1)]
        run = run + jnp.sum(blocks[b], axis=1, keepdims=True)
    return jnp.concatenate(outs, axis=1), run


SB_HEADS = D_ATT // HEAD_DIM
SB_FWD_CHAINS = 2
SB_BWD_CHAINS = 1


def _sb_past(i, t, rows):
    row = lax.broadcasted_iota(jnp.int32, (rows, SB_KT), 0) & (Q_BLOCK - 1)
    col = lax.broadcasted_iota(jnp.int32, (rows, SB_KT), 1)
    return col + t * SB_KT < row + i * Q_BLOCK


def _sb_head_masks(chains):
    lane = lax.broadcasted_iota(jnp.int32, (Q_BLOCK, D_ATT), 1)
    masks = [(lane >= HEAD_DIM * h) & (lane < HEAD_DIM * (h + 1)) for h in range(SB_HEADS)]
    per = SB_HEADS // chains
    return [masks[per * g:per * (g + 1)] for g in range(chains)]


def _sb_rows(t):
    return pl.ds(pl.multiple_of(t * SB_KT, SB_KT), SB_KT)


def _sb_log_terms(z, past):
    lsz = jnp.minimum(z, 0.0) - jnp.log(1.0 + jnp.exp(-jnp.abs(z)))
    lk = lsz - z
    return lsz, (lk if past is None else jnp.where(past, lk, 0.0))


def _sb_weights(lsz, after, past):
    w = jnp.exp(lsz + after)
    return w if past is None else jnp.where(past, w, 0.0)


def _sb_fwd(proj):
    T = proj.shape[0]
    rows = SB_HEADS // SB_FWD_CHAINS * Q_BLOCK

    def body(q_ref, k_ref, v_ref, o_ref, z_buf, w_buf):
        i = pl.program_id(0)
        masks = _sb_head_masks(SB_FWD_CHAINS)
        tri = _sb_tri(True)
        q = q_ref[...] * (HEAD_DIM ** -0.5)
        qs = [_stack_heads(q, m).astype(BF16) for m in masks]
        n_tiles = (i * Q_BLOCK) // SB_KT + 1

        def scores(t):
            kb = k_ref[_sb_rows(t), :].astype(BF16)
            return [lax.dot_general(g, kb, NT, preferred_element_type=F32) for g in qs]

        def weights(zs, cs, past, between=lambda: None):
            logs = [_sb_log_terms(z, past) for z in zs]
            splits = [_split_stack(lk) for _, lk in logs]
            ys = [lax.dot_general(x, tri, NN, preferred_element_type=F32) for _, x in splits]
            between()
            sums = [_suffix_from(y, blocks, c) for y, (blocks, _), c in zip(ys, splits, cs)]
            ws = [_sb_weights(lsz, after, past).astype(BF16) for (lsz, _), (after, _) in zip(logs, sums)]
            return ws, [c for _, c in sums]

        def values(acc, slot, t):
            vb = v_ref[_sb_rows(t), :].astype(BF16)
            for g, m in enumerate(masks):
                acc = acc + _unstack_heads(lax.dot_general(w_buf[slot, g], vb, NN, preferred_element_type=F32), m)
            return acc

        def keep(buf, slot, xs):
            for g, x in enumerate(xs):
                buf[slot, g] = x

        zs = scores(n_tiles - 1)
        keep(z_buf, 0, scores(jnp.maximum(n_tiles - 2, 0)))
        ws, cs = weights(zs, [jnp.zeros((rows, 1), F32)] * SB_FWD_CHAINS, _sb_past(i, n_tiles - 1, rows))
        keep(w_buf, 0, ws)

        def step(tt, carry):
            t = n_tiles - 2 - tt
            cur = tt & 1
            acc = values(carry[0], cur, t + 1)
            ws, cs = weights([z_buf[cur, g] for g in range(SB_FWD_CHAINS)], carry[1:], None,
                             lambda: keep(z_buf, 1 - cur, scores(jnp.maximum(t - 1, 0))))
            keep(w_buf, 1 - cur, ws)
            return (acc, *cs)

        carry = lax.fori_loop(0, n_tiles - 1, step, (jnp.zeros((Q_BLOCK, D_ATT), F32), *cs))
        o_ref[...] = values(carry[0], (n_tiles - 1) & 1, 0)

    cb = COL_QS // D_ATT
    return pl.pallas_call(
        body, name="sb_fwd", grid=(T // Q_BLOCK,),
        in_specs=[pl.BlockSpec((Q_BLOCK, D_ATT), lambda i: (i, cb)),
                  pl.BlockSpec((T, D_ATT), lambda i: (0, cb + 1)),
                  pl.BlockSpec((T, D_ATT), lambda i: (0, cb + 2))],
        out_specs=pl.BlockSpec((Q_BLOCK, D_ATT), lambda i: (i, 0)),
        out_shape=jax.ShapeDtypeStruct((T, D_ATT), F32),
        scratch_shapes=[pltpu.VMEM((2, SB_FWD_CHAINS, rows, SB_KT), F32),
                        pltpu.VMEM((2, SB_FWD_CHAINS, rows, SB_KT), BF16)],
        compiler_params=_params(dimension_semantics=("arbitrary",)),
    )(proj, proj, proj)


def _sb_bwd(proj, do, o):
    T = proj.shape[0]
    n_rows = SB_HEADS // SB_BWD_CHAINS * Q_BLOCK

    def body(q_ref, k_ref, v_ref, do_ref, o_ref, dq_ref, dk_ref, dv_ref, z_buf, gv_buf, dz_buf, w_buf):
        i = pl.program_id(0)
        masks = _sb_head_masks(SB_BWD_CHAINS)
        tri = _sb_tri(True)
        tri_incl = _sb_tri(False)

        @pl.when(i == 0)
        def _():
            dk_ref[...] = jnp.zeros_like(dk_ref)
            dv_ref[...] = jnp.zeros_like(dv_ref)

        q = q_ref[...] * (HEAD_DIM ** -0.5)
        qs = [_stack_heads(q, m).astype(BF16) for m in masks]
        dos = [_stack_heads(do_ref[...], m).astype(BF16) for m in masks]
        qts = [_stack_heads(q, m).T.astype(BF16) for m in masks]
        dots = [_stack_heads(do_ref[...], m).T.astype(BF16) for m in masks]
        o_rep = jnp.concatenate([o_ref[...]] * (SB_HEADS // SB_BWD_CHAINS), axis=0)
        deltas = [jnp.sum(d.astype(F32) * o_rep, axis=1, keepdims=True) for d in dos]
        n_tiles = (i * Q_BLOCK) // SB_KT + 1

        def scores(t):
            kb = k_ref[_sb_rows(t), :].astype(BF16)
            return [lax.dot_general(g, kb, NT, preferred_element_type=F32) for g in qs]

        def value_grads(t):
            vb = v_ref[_sb_rows(t), :].astype(BF16)
            return [lax.dot_general(d, vb, NT, preferred_element_type=F32) for d in dos]

        def keep(buf, slot, xs):
            for g, x in enumerate(xs):
                buf[slot, g] = x

        def kept(buf, slot):
            return [buf[slot, g] for g in range(SB_BWD_CHAINS)]

        def score_grads(zs, gvs, cs, ces, past, after_first=lambda: None, after_second=lambda: None):
            logs = [_sb_log_terms(z, past) for z in zs]
            splits = [_split_stack(lk) for _, lk in logs]
            ys = [lax.dot_general(x, tri, NN, preferred_element_type=F32) for _, x in splits]
            after_first()
            sums = [_suffix_from(y, blocks, c) for y, (blocks, _), c in zip(ys, splits, cs)]
            wbs = [_sb_weights(lsz, after, past).astype(BF16) for (lsz, _), (after, _) in zip(logs, sums)]
            es = [wb.astype(F32) * gv for wb, gv in zip(wbs, gvs())]
            esplits = [_split_stack(e) for e in es]
            eys = [lax.dot_general(x, tri_incl, NN, preferred_element_type=F32) for _, x in esplits]
            after_second()
            esums = [_suffix_from(y, blocks, ce) for y, (blocks, _), ce in zip(eys, esplits, ces)]
            dzbs = []
            for e, (lsz, lk), (suf, _), delta in zip(es, logs, esums, deltas):
                dz = e * jnp.exp(lk) - (delta - suf) * jnp.exp(lsz)
                dzbs.append((dz if past is None else jnp.where(past, dz, 0.0)).astype(BF16))
            return dzbs, wbs, [c for _, c in sums], [c for _, c in esums]

        def outputs(dq, slot, t):
            rows = _sb_rows(t)
            kb = k_ref[rows, :].astype(BF16)
            dk_t = dv_t = None
            for m, dzb, wb, g, d in zip(masks, kept(dz_buf, slot), kept(w_buf, slot), qts, dots):
                dq = dq + _unstack_heads(lax.dot_general(dzb, kb, NN, preferred_element_type=F32), m)
                a = lax.dot_general(g, dzb, NN, preferred_element_type=F32)
                b = lax.dot_general(d, wb, NN, preferred_element_type=F32)
                dk_t = a if dk_t is None else dk_t + a
                dv_t = b if dv_t is None else dv_t + b
            dk_ref[:, rows] = dk_ref[:, rows] + dk_t
            dv_ref[:, rows] = dv_ref[:, rows] + dv_t
            return dq

        zcol = [jnp.zeros((n_rows, 1), F32)] * SB_BWD_CHAINS
        ahead = jnp.maximum(n_tiles - 2, 0)
        zs, gvs = scores(n_tiles - 1), value_grads(n_tiles - 1)
        keep(z_buf, 0, scores(ahead))
        keep(gv_buf, 0, value_grads(ahead))
        dzbs, wbs, cs, ces = score_grads(zs, lambda: gvs, zcol, zcol, _sb_past(i, n_tiles - 1, n_rows))
        keep(dz_buf, 0, dzbs)
        keep(w_buf, 0, wbs)

        def step(tt, carry):
            t = n_tiles - 2 - tt
            cur = tt & 1
            ahead = jnp.maximum(t - 1, 0)
            dq = outputs(carry[0], cur, t + 1)
            dzbs, wbs, cs, ces = score_grads(
                kept(z_buf, cur), lambda: kept(gv_buf, cur),
                carry[1:1 + SB_BWD_CHAINS], carry[1 + SB_BWD_CHAINS:], None,
                lambda: keep(z_buf, 1 - cur, scores(ahead)),
                lambda: keep(gv_buf, 1 - cur, value_grads(ahead)))
            keep(dz_buf, 1 - cur, dzbs)
            keep(w_buf, 1 - cur, wbs)
            return (dq, *cs, *ces)

        carry = lax.fori_loop(0, n_tiles - 1, step, (jnp.zeros((Q_BLOCK, D_ATT), F32), *cs, *ces))
        dq_ref[...] = outputs(carry[0], (n_tiles - 1) & 1, 0) * (HEAD_DIM ** -0.5)

    cb = COL_QS // D_ATT
    blk = pl.BlockSpec((Q_BLOCK, D_ATT), lambda i: (i, 0))
    turned = pl.BlockSpec((D_ATT, T), lambda i: (0, 0))
    sh = jax.ShapeDtypeStruct((T, D_ATT), F32)
    sh_turned = jax.ShapeDtypeStruct((D_ATT, T), F32)
    kept_f32 = pltpu.VMEM((2, SB_BWD_CHAINS, n_rows, SB_KT), F32)
    kept_bf16 = pltpu.VMEM((2, SB_BWD_CHAINS, n_rows, SB_KT), BF16)
    return pl.pallas_call(
        body, name="sb_bwd", grid=(T // Q_BLOCK,),
        in_specs=[pl.BlockSpec((Q_BLOCK, D_ATT), lambda i: (i, cb)),
                  pl.BlockSpec((T, D_ATT), lambda i: (0, cb + 1)),
                  pl.BlockSpec((T, D_ATT), lambda i: (0, cb + 2)), blk, blk],
        out_specs=[blk, turned, turned], out_shape=[sh, sh_turned, sh_turned],
        scratch_shapes=[kept_f32, kept_f32, kept_bf16, kept_bf16],
        compiler_params=_params(dimension_semantics=("arbitrary",)),
    )(proj, proj, proj, do, o)


def _tok(c, by=None):
    if by is None:
        return pl.BlockSpec((TM, c), lambda i, j, k: (i, 0))
    if by == 1:
        return pl.BlockSpec((TM, c), lambda i, j, k: (i, j))
    return pl.BlockSpec((TM, c), lambda i, j, k: (i, k))


def _gain_spec():
    return pl.BlockSpec((1, D_MODEL), lambda i, j, k: (0, 0))


def _wfull(r, c, l):
    return pl.BlockSpec((N_CHIPS, None, r, c), lambda i, j, k: (0, l, 0, 0), pipeline_mode=pl.Buffered(1))


def _pick(idx, c):
    return lambda ins: ins[idx][c]


def _cols(idx, c, w):
    return lambda ins: ins[idx][:, c * w:(c + 1) * w]


def _rows(rows, width):
    return pl.BlockSpec((rows, width), lambda i, j, k: (i, 0))


def _whole(shape):
    return pl.BlockSpec(shape, lambda i, j, k: (0, 0), pipeline_mode=pl.Buffered(1))


def _ffn_fwd(x, gain, wg, wu, wd):
    T = x.shape[0]
    wg, wu, wd = (w.reshape(-1, D_MODEL) for w in (wg, wu, wd))
    ff = wd.shape[0]
    tm = TM // 2
    normed = _normed(0, 3)

    def swiglu(vals, ins, outs, i):
        gt, up = vals
        s = _sigmoid(gt)
        sil = gt * s
        outs[0][...] = sil.astype(BF16)
        outs[1][...] = (up * (s * (1.0 + gt * (1.0 - s)))).astype(BF16)
        outs[2][...] = (sil * up).astype(BF16)
        outs[3][...] = normed(ins)

    ash = jax.ShapeDtypeStruct((T, ff), BF16)
    sil, up_dsil, act, h = _mm(
        "ffn_up", [x, wg, wu, gain], [_rows(tm, D_MODEL), _whole(wg.shape), _whole(wu.shape), _gain_spec()],
        [(normed, 1, 0), (normed, 2, 1)], 2, None, NT, (T // tm, 1, 1), swiglu,
        [ash] * 3 + [jax.ShapeDtypeStruct((T, D_MODEL), BF16)], [_rows(tm, ff)] * 3 + [_rows(tm, D_MODEL)])

    def resid(vals, ins, outs, i):
        outs[0][...] = ins[2][...] + 0.5 * vals[0]

    (y,) = _mm(
        "ffn_down", [act, wd, x], [_rows(TM, ff), _whole(wd.shape), _tok(D_MODEL)], [(0, 1, 0)], 1, None, NN,
        (T // TM, 1, 1), resid, [jax.ShapeDtypeStruct((T, D_MODEL), F32)], [_tok(D_MODEL)])
    return y, (x, h, sil, up_dsil, act)


def _ffn_bwd(dxo, gain, wg, wu, wd, saved):
    x, h, sil, up_dsil, act = saved
    T = x.shape[0]
    n_chips, _, ffs, _ = wd.shape
    wg, wu, wd = (w.reshape(-1, D_MODEL) for w in (wg, wu, wd))
    ff = wd.shape[0]
    tk = TM
    tm = TM

    def dswiglu(vals, ins, outs, i):
        da = 0.5 * vals[0]
        outs[0][...] = (da * ins[3][...].astype(F32)).astype(BF16)
        outs[1][...] = (da * ins[2][...].astype(F32)).astype(BF16)

    ash = jax.ShapeDtypeStruct((T, ff), BF16)
    dgate, dup = _mm(
        "ffn_dact", [dxo, wd, sil, up_dsil], [_rows(tm, D_MODEL), _whole(wd.shape), _rows(tm, ff), _rows(tm, ff)],
        [(0, 1, 0)], 1, None, NT, (T // tm, 1, 1), dswiglu, [ash, ash], [_rows(tm, ff)] * 2)

    def half(vals, ins, outs, i):
        outs[0][...] = (0.5 * vals[0]).astype(BF16)

    def cast(vals, ins, outs, i):
        outs[0][...] = vals[0].astype(BF16)

    tok_k = pl.BlockSpec((tk, D_MODEL), lambda i, j, k: (k, 0))
    hid_k = pl.BlockSpec((tk, ff), lambda i, j, k: (k, 0))
    wsh = jax.ShapeDtypeStruct((ff, D_MODEL), BF16)
    (dwd,) = _mm("ffn_dwd", [act, dxo], [hid_k, tok_k], [(0, 1, 0)], 1, (ff, D_MODEL), TN, (1, 1, T // tk), half,
                 [wsh], [_whole((ff, D_MODEL))])

    tx = TM // 2
    dx, dgain = _mm(
        "ffn_dx", [dgate, dup, wg, wu, x, gain, dxo],
        [_rows(tx, ff), _rows(tx, ff), _whole(wg.shape), _whole(wu.shape), _rows(tx, D_MODEL), _gain_spec(),
         _rows(tx, D_MODEL)],
        [(0, 2, 0), (1, 3, 0)], 1, None, NN, (T // tx, 1, 1), _rms_bwd_epilogue(4, 5, 6),
        [jax.ShapeDtypeStruct((T, D_MODEL), F32), jax.ShapeDtypeStruct((8, D_MODEL), F32)],
        [_rows(tx, D_MODEL), pl.BlockSpec((8, D_MODEL), lambda i, j, k: (0, 0))])

    dws = []
    for dact in (dgate, dup):
        dws += _mm("ffn_dwgu", [dact, h], [hid_k, tok_k], [(0, 1, 0)], 1, (ff, D_MODEL), TN, (1, 1, T // tk), cast,
                   [wsh], [_whole((ff, D_MODEL))])
    dwg, dwu, dwd = (w.reshape(n_chips, ffs, D_MODEL) for w in (dws[0], dws[1], dwd))
    return dx, dgain, dwg, dwu, dwd


def _joined_mixer_weights(wpd, wps, wo):
    n, _, r, c = wpd.shape
    wpd_n, wps_n = (w[:, 0].transpose(1, 0, 2).reshape(r, n * c) for w in (wpd, wps))
    return wpd_n, wps_n, wo.reshape(-1, wo.shape[3])


def _mixer_fwd(x, gain, W, l, tabs):
    T = x.shape[0]
    win, wpd, wps, wo = W["w_in"], W["w_proj_dil"], W["w_proj_sb"], W["w_out"]
    cin = win.shape[3]
    cp = wpd.shape[3]
    normed = _normed(0, 5)
    n_rope = 6 * D_ATT

    tm = TM // 2

    def roped(vals, ins, outs, i):
        for j, v in enumerate(vals):
            lo = j * cin
            k = min(max(n_rope - lo, 0), cin)
            if k:
                tab = [jnp.concatenate([ins[t][...]] * (k // 128), axis=1) for t in (2, 3, 4)]
                outs[0][:, lo:lo + k] = _rope_fwd(v[:, :k], *tab)
            if k < cin:
                outs[0][:, lo + k:lo + cin] = v[:, k:]
        outs[1][...] = normed(ins)

    proj, h = _mm(
        "mix_in", [x, win, *tabs, gain],
        [_rows(tm, D_MODEL), _wfull(D_MODEL, cin, l)] + [_rows(tm, 128)] * 3 + [_gain_spec()],
        [(normed, _pick(1, c), c) for c in range(N_CHIPS)], N_CHIPS, None, NN, (T // tm, 1, 1), roped,
        [jax.ShapeDtypeStruct((T, N_CHIPS * cin), F32), jax.ShapeDtypeStruct((T, D_MODEL), BF16)],
        [_rows(tm, N_CHIPS * cin), _rows(tm, D_MODEL)])

    os_, lses = [], []
    for g, (window, dil) in enumerate(DIL_GROUPS):
        o_g, lse_g = _dil_fwd(proj, g, dil)
        os_.append(o_g)
        lses.append(lse_g)
    o_dil, lse = _dil_merge(os_, lses)
    o_sb = _sb_fwd(proj)

    def gated(vals, ins, outs, i):
        pd, ps = vals
        outs[0][...] = (_sigmoid(ins[4][...]) * pd + _sigmoid(ins[5][...]) * ps).astype(BF16)
        outs[1][...] = pd.astype(BF16)
        outs[2][...] = ps.astype(BF16)

    wpd_n, wps_n, wo_n = _joined_mixer_weights(wpd, wps, wo)
    gd_spec = pl.BlockSpec((TM, D_MODEL), lambda i, j, k: (i, COL_GD // D_MODEL))
    gs_spec = pl.BlockSpec((TM, D_MODEL), lambda i, j, k: (i, COL_GS // D_MODEL))
    ush = jax.ShapeDtypeStruct((T, D_MODEL), BF16)
    u, pd, ps = _mm(
        "mix_gate", [o_dil, o_sb, wpd_n, wps_n, proj, proj],
        [_tok(D_ATT), _tok(D_ATT), _whole(wpd_n.shape), _whole(wps_n.shape), gd_spec, gs_spec],
        [(0, 2, 0), (1, 3, 1)], 2, None, NN, (T // TM, 1, 1), gated, [ush] * 3, [_tok(D_MODEL)] * 3)

    def resid(vals, ins, outs, i):
        outs[0][...] = ins[2][...] + vals[0]

    (y,) = _mm(
        "mix_out", [u, wo_n, x], [_tok(D_MODEL), _whole(wo_n.shape), _tok(D_MODEL)], [(0, 1, 0)], 1, None, NN,
        (T // TM, 1, 1), resid, [jax.ShapeDtypeStruct((T, D_MODEL), F32)], [_tok(D_MODEL)])
    return y, (x, h, proj, o_dil, lse, o_sb, u, pd, ps)


def _mixer_bwd(dxo, gain, W, l, tabs, saved):
    x, h, proj, o_dil, lse, o_sb, u, pd, ps = saved
    T = x.shape[0]
    win, wpd, wps, wo = W["w_in"], W["w_proj_dil"], W["w_proj_sb"], W["w_out"]
    cin = win.shape[3]
    cp = wpd.shape[3]
    tk = TM
    tm = TM
    row = pl.BlockSpec((tm, D_MODEL), lambda i, j, k: (i, 0))

    def dgated(vals, ins, outs, i):
        du = vals[0]
        sd = _sigmoid(ins[4][...])
        ss = _sigmoid(ins[5][...])
        outs[0][...] = (du * sd).astype(BF16)
        outs[1][...] = (du * ss).astype(BF16)
        outs[2][...] = (du * ins[2][...].astype(F32) * sd * (1.0 - sd)).astype(BF16)
        outs[3][...] = (du * ins[3][...].astype(F32) * ss * (1.0 - ss)).astype(BF16)

    wpd_n, wps_n, wo_n = _joined_mixer_weights(wpd, wps, wo)
    gd_spec = pl.BlockSpec((TM, D_MODEL), lambda i, j, k: (i, COL_GD // D_MODEL))
    gs_spec = pl.BlockSpec((TM, D_MODEL), lambda i, j, k: (i, COL_GS // D_MODEL))
    ush = jax.ShapeDtypeStruct((T, D_MODEL), BF16)
    dpd, dps, dgd, dgs = _mm(
        "mix_du", [dxo, wo_n, pd, ps, proj, proj],
        [_tok(D_MODEL), _whole(wo_n.shape), _tok(D_MODEL), _tok(D_MODEL), gd_spec, gs_spec],
        [(0, 1, 0)], 1, None, NT, (T // TM, 1, 1), dgated, [ush] * 4, [_tok(D_MODEL)] * 4)

    def one(vals, ins, outs, i):
        outs[0][...] = vals[0].astype(BF16)

    def two(vals, ins, outs, i):
        outs[0][...] = vals[0].astype(BF16)
        outs[1][...] = vals[1].astype(BF16)

    tok_k = pl.BlockSpec((tk, D_MODEL), lambda i, j, k: (k, 0))
    att_k = pl.BlockSpec((tk, D_ATT), lambda i, j, k: (k, 0))
    (dwo_n,) = _mm("mix_dwo", [u, dxo], [tok_k, tok_k], [(0, 1, 0)], 1, (D_MODEL, D_MODEL), TN, (1, 1, T // tk), one,
                   [jax.ShapeDtypeStruct((D_MODEL, D_MODEL), BF16)], [_whole((D_MODEL, D_MODEL))])

    def plain2(vals, ins, outs, i):
        outs[0][...] = vals[0]
        outs[1][...] = vals[1]

    ash = jax.ShapeDtypeStruct((T, D_ATT), F32)
    do_dil, do_sb = _mm(
        "mix_do", [dpd, dps, wpd_n, wps_n], [_tok(D_MODEL), _tok(D_MODEL), _whole(wpd_n.shape), _whole(wps_n.shape)],
        [(0, 2, 0), (1, 3, 1)], 2, None, NT, (T // TM, 1, 1), plain2, [ash, ash], [_tok(D_ATT)] * 2)

    psh = jax.ShapeDtypeStruct((D_ATT, D_MODEL), BF16)
    dwpd_n, dwps_n = _mm(
        "mix_dwp", [o_dil, o_sb, dpd, dps], [att_k, att_k, tok_k, tok_k], [(0, 2, 0), (1, 3, 1)], 2,
        (D_ATT, D_MODEL), TN, (1, 1, T // tk), two, [psh, psh], [_whole((D_ATT, D_MODEL))] * 2)
    dwpd, dwps = (w.reshape(D_ATT, N_CHIPS, cp).transpose(1, 0, 2) for w in (dwpd_n, dwps_n))
    dwo = dwo_n.reshape(N_CHIPS, cp, D_MODEL)

    dqs, dks, dvs = [], [], []
    for g, (window, dil) in enumerate(DIL_GROUPS):
        dq, dk, dv = _dil_bwd(proj, do_dil, o_dil, lse, g, dil)
        dqs.append(dq)
        dks.append(dk)
        dvs.append(dv)
    dq_s, dk_s, dv_s = _sb_bwd(proj, do_sb, o_sb)
    dproj = _assemble_dproj(dqs + dks, dvs + [dq_s], [dk_s, dv_s], [dgd, dgs], tabs)

    dx, dgain = _mm(
        "mix_dx", [dproj, win, x, gain, dxo],
        [pl.BlockSpec((tm, N_CHIPS * cin), lambda i, j, k: (i, 0)), _wfull(D_MODEL, cin, l), row, _gain_spec(), row],
        [(_cols(0, c, cin), _pick(1, c), 0) for c in range(N_CHIPS)], 1, None, NT, (T // tm, 1, 1),
        _rms_bwd_epilogue(2, 3, 4),
        [jax.ShapeDtypeStruct((T, D_MODEL), F32), jax.ShapeDtypeStruct((8, D_MODEL), F32)],
        [row, pl.BlockSpec((8, D_MODEL), lambda i, j, k: (0, 0))])

    (dwin,) = _mm(
        "mix_dwin", [h, dproj],
        [pl.BlockSpec((tk, D_MODEL), lambda i, j, k: (k, 0)), pl.BlockSpec((tk, cin), lambda i, j, k: (k, j))],
        [(0, 1, 0)], 1, (D_MODEL, cin), TN, (1, N_CHIPS, T // tk), one,
        [jax.ShapeDtypeStruct((N_CHIPS, D_MODEL, cin), BF16)],
        [pl.BlockSpec((None, D_MODEL, cin), lambda i, j, k: (j, 0, 0))])
    return dx, dgain, dwin, dwpd, dwps, dwo


def _local_step(x, target, norms, norm_final, weights_of, on_grads):
    T = x.shape[0]
    tabs = _rope_tables(T)
    saved, held = [], []
    for l in range(DEPTH):
        w1 = weights_of(l, 0, x)
        x, s1 = _ffn_fwd(x, norms["norm_ffn1"][l:l + 1], w1["ffn1_w_gate"], w1["ffn1_w_up"], w1["ffn1_w_down"])
        w2 = weights_of(l, 1, x)
        x, s2 = _mixer_fwd(x, norms["norm_mix"][l:l + 1], w2, 0, tabs)
        w3 = weights_of(l, 2, x)
        x, s3 = _ffn_fwd(x, norms["norm_ffn2"][l:l + 1], w3["ffn2_w_gate"], w3["ffn2_w_up"], w3["ffn2_w_down"])
        saved.append((s1, s2, s3))
        held.append((w1, w2, w3))
    dx, dg_final, loss = _final_loss(x, norm_final.reshape(1, D_MODEL), target)
    gains = [None] * DEPTH
    for l in reversed(range(DEPTH)):
        s1, s2, s3 = saved[l]
        w1, w2, w3 = held[l]
        dx, dg2, dwg2, dwu2, dwd2 = _ffn_bwd(dx, norms["norm_ffn2"][l:l + 1], w3["ffn2_w_gate"], w3["ffn2_w_up"],
                                             w3["ffn2_w_down"], s3)
        dx = on_grads(l, 2, dict(ffn2_w_gate=dwg2, ffn2_w_up=dwu2, ffn2_w_down=dwd2), dx)
        dx, dgm, dwin, dwpd, dwps, dwo = _mixer_bwd(dx, norms["norm_mix"][l:l + 1], w2, 0, tabs, s2)
        dx = on_grads(l, 1, dict(w_in=dwin, w_proj_dil=dwpd, w_proj_sb=dwps, w_out=dwo), dx)
        dx, dg1, dwg1, dwu1, dwd1 = _ffn_bwd(dx, norms["norm_ffn1"][l:l + 1], w1["ffn1_w_gate"], w1["ffn1_w_up"],
                                             w1["ffn1_w_down"], s1)
        dx = on_grads(l, 0, dict(ffn1_w_gate=dwg1, ffn1_w_up=dwu1, ffn1_w_down=dwd1), dx)
        gains[l] = dict(norm_ffn1=dg1, norm_mix=dgm, norm_ffn2=dg2)
    return loss, dx, gains, dg_final


def _place():
    x, y, c = lax.axis_index("x"), lax.axis_index("y"), lax.axis_index("c")
    chips = [(1 - x, y), (x, 1 - y), (1 - x, 1 - y)]
    return x, y, c, chips


def _half(c, r):
    return pl.ds(pl.multiple_of(c * (r // 2), 8), r // 2)


def _cast_into_slot(ws, ls, me_arr, after):
    n = len(ws)
    late = [] if after is None else [after]

    def body(me_ref, *refs):
        for a in range(n):
            refs[len(refs) - n + a][...] = refs[a][...].astype(BF16)

    def src(w, l):
        return pl.BlockSpec((None, w.shape[1] // 4, w.shape[2]), lambda i, me: (l, i, 0))

    def dst(w):
        return pl.BlockSpec((None, None, w.shape[1] // 4, w.shape[2]), lambda i, me: (me[0], 0, i, 0))

    return pl.pallas_call(
        body, name="cast_weights",
        grid_spec=pltpu.PrefetchScalarGridSpec(
            num_scalar_prefetch=1, grid=(4,),
            in_specs=[src(w, l) for w, l in zip(ws, ls)] + [pl.BlockSpec(memory_space=pl.ANY)] * len(late),
            out_specs=[dst(w) for w in ws]),
        out_shape=[jax.ShapeDtypeStruct((N_CHIPS, 1) + w.shape[1:], BF16) for w in ws], compiler_params=_params(),
    )(me_arr, *ws, *late)


HBM_SPEC = pl.BlockSpec(memory_space=pltpu.HBM)
SEM_SPEC = pl.BlockSpec(memory_space=pltpu.SEMAPHORE)
SPLIT_COPY = pltpu.CompilerParams(has_side_effects=pltpu.SideEffectType.DATAFLOW_SIDE_EFFECTING)


def _gather_piece(ref, chip_id, c):
    return ref.at[chip_id, 0, _half(c, ref.shape[2]), :]


def _gather_start(tag, bufs, direct):
    n = len(bufs)

    def body(*refs):
        out_refs = refs[n:2 * n]
        send_sems, recv_sems, token = refs[2 * n:]
        x, y, c, chips = _place()
        me = 2 * x + y
        for a in range(n):
            piece = _gather_piece(out_refs[a], me, c)
            for j, chip in enumerate(chips):
                for to in ((0, 1) if direct[a] else (c,)):
                    pltpu.make_async_remote_copy(
                        src_ref=piece, dst_ref=piece, send_sem=send_sems.at[6 * a + 2 * j + to],
                        recv_sem=recv_sems.at[6 * a + 2 * j + c], device_id=(*chip, to), device_id_type=MESH).start()
        token[...] = jnp.zeros_like(token)

    outs = pl.pallas_call(
        body, name=f"gather_start_{tag}", in_specs=[HBM_SPEC] * n,
        out_specs=[HBM_SPEC] * n + [SEM_SPEC, SEM_SPEC, pl.BlockSpec(memory_space=pltpu.VMEM)],
        out_shape=[pltpu.HBM(b.shape, b.dtype) for b in bufs] + [pltpu.SemaphoreType.DMA((6 * n,))] * 2
        + [jax.ShapeDtypeStruct((8, 128), F32)],
        input_output_aliases={a: a for a in range(n)}, compiler_params=SPLIT_COPY,
    )(*[pltpu.with_memory_space_constraint(b, pltpu.HBM) for b in bufs])
    return outs[:n], outs[n], outs[n + 1], outs[n + 2]


def _gather_wait(k, bufs, places, direct, send_sems, recv_sems, after):
    m = len(bufs)

    def body(*refs):
        in_refs = refs[:m]
        ssem, rsem = refs[m], refs[m + 1]
        x, y, c, chips = _place()
        me = 2 * x + y
        for t, a in enumerate(places):
            for j, chip in enumerate(chips):
                for core in ((0, 1) if direct else (c,)):
                    cp = pltpu.make_async_remote_copy(
                        src_ref=_gather_piece(in_refs[t], me, c),
                        dst_ref=_gather_piece(in_refs[t], 2 * chip[0] + chip[1], core),
                        send_sem=ssem.at[6 * a + 2 * j + core], recv_sem=rsem.at[6 * a + 2 * j + core],
                        device_id=(*chip, core), device_id_type=MESH)
                    cp.wait_send()
                    cp.wait_recv()

    return pl.pallas_call(
        body, name=f"gather_wait_{k}",
        in_specs=[HBM_SPEC] * m + [SEM_SPEC, SEM_SPEC, pl.BlockSpec(memory_space=pl.ANY)], out_specs=[HBM_SPEC] * m,
        out_shape=[pltpu.HBM(b.shape, b.dtype) for b in bufs], input_output_aliases={t: t for t in range(m)},
        compiler_params=SPLIT_COPY,
    )(*bufs, send_sems, recv_sems, after)


def _gather_relay(bufs):
    n = len(bufs)

    def body(*refs):
        out_refs = refs[n:2 * n]
        send_sems, recv_sems = refs[2 * n:]
        x, y, c, chips = _place()
        cps = []
        for a in range(n):
            for j, chip in enumerate(chips):
                piece = _gather_piece(out_refs[a], 2 * chip[0] + chip[1], c)
                cps.append(pltpu.make_async_remote_copy(
                    src_ref=piece, dst_ref=piece, send_sem=send_sems.at[a, j], recv_sem=recv_sems.at[a, j],
                    device_id=(x, y, 1 - c), device_id_type=MESH))
        for cp in cps:
            cp.start()
        for a in range(n):
            for j, chip in enumerate(chips):
                theirs = _gather_piece(out_refs[a], 2 * chip[0] + chip[1], 1 - c)
                pltpu.make_async_remote_copy(
                    src_ref=theirs, dst_ref=theirs, send_sem=send_sems.at[a, j], recv_sem=recv_sems.at[a, j],
                    device_id=(x, y, 1 - c), device_id_type=MESH).wait_recv()
        for cp in cps:
            cp.wait_send()

    any_spec = pl.BlockSpec(memory_space=pl.ANY)
    return pl.pallas_call(
        body, name="gather_relay", in_specs=[any_spec] * n, out_specs=[any_spec] * n,
        out_shape=[jax.ShapeDtypeStruct(b.shape, b.dtype) for b in bufs],
        input_output_aliases={a: a for a in range(n)},
        scratch_shapes=[pltpu.SemaphoreType.DMA((n, 3))] * 2,
    )(*bufs)


def _other_half(ref, c):
    return ref.at[:, _half(1 - c, ref.shape[1]), :]


def _all_of(ref, c):
    return ref


def _sibling_start(name, srcs, pick, land_shapes, thru):
    n = len(srcs)
    lands = [lax.empty(sh, s.dtype) for sh, s in zip(land_shapes, srcs)]
    kept = lands + ([] if thru is None else [thru])
    m = len(kept)

    def body(*refs):
        s_refs, land_refs = refs[:n], refs[n + m:n + m + n]
        send_sems, recv_sems, token = refs[n + 2 * m:]
        x, y, c, _ = _place()
        for a in range(n):
            pltpu.make_async_remote_copy(
                src_ref=pick(s_refs[a], c), dst_ref=land_refs[a], send_sem=send_sems.at[a],
                recv_sem=recv_sems.at[a], device_id=(x, y, 1 - c), device_id_type=MESH).start()
        token[...] = jnp.zeros_like(token)

    outs = pl.pallas_call(
        body, name=name, in_specs=[HBM_SPEC] * (n + m),
        out_specs=[HBM_SPEC] * m + [SEM_SPEC, SEM_SPEC, pl.BlockSpec(memory_space=pltpu.VMEM)],
        out_shape=[pltpu.HBM(v.shape, v.dtype) for v in kept] + [pltpu.SemaphoreType.DMA((n,))] * 2
        + [jax.ShapeDtypeStruct((8, 128), F32)],
        input_output_aliases={n + a: a for a in range(m)}, compiler_params=SPLIT_COPY,
    )(*[pltpu.with_memory_space_constraint(v, pltpu.HBM) for v in list(srcs) + kept])
    return (outs[:n], outs[m], outs[m + 1]), (outs[n] if thru is not None else None), outs[m + 2]


def _sibling_wait(name, srcs, pick, lands, send_sems, recv_sems, after):
    n = len(srcs)

    def body(*refs):
        s_refs, land_refs = refs[:n], refs[n:2 * n]
        ssem, rsem = refs[2 * n], refs[2 * n + 1]
        x, y, c, _ = _place()
        for a in range(n):
            cp = pltpu.make_async_remote_copy(
                src_ref=pick(s_refs[a], c), dst_ref=land_refs[a], send_sem=ssem.at[a], recv_sem=rsem.at[a],
                device_id=(x, y, 1 - c), device_id_type=MESH)
            cp.wait_send()
            cp.wait_recv()

    return pl.pallas_call(
        body, name=name, in_specs=[HBM_SPEC] * (2 * n) + [SEM_SPEC, SEM_SPEC, pl.BlockSpec(memory_space=pl.ANY)],
        out_specs=[HBM_SPEC] * n, out_shape=[pltpu.HBM(v.shape, v.dtype) for v in lands],
        input_output_aliases={n + a: a for a in range(n)}, compiler_params=SPLIT_COPY,
    )(*srcs, *lands, send_sems, recv_sems, after)


def _add_half(gs, gots, c_arr):
    n = len(gs)

    def body(c_ref, *refs):
        for a in range(n):
            refs[2 * n + a][...] = (refs[a][...].astype(F32) + refs[n + a][...].astype(F32)).astype(BF16)

    def own(g):
        return pl.BlockSpec((None, g.shape[1] // 2, g.shape[2]), lambda k, cr: (k, cr[0], 0))

    def half(g):
        return pl.BlockSpec((None, g.shape[1] // 2, g.shape[2]), lambda k, cr: (k, 0, 0))

    return pl.pallas_call(
        body, name="grad_add_half",
        grid_spec=pltpu.PrefetchScalarGridSpec(
            num_scalar_prefetch=1, grid=(N_CHIPS,),
            in_specs=[own(g) for g in gs] + [half(g) for g in gs], out_specs=[half(g) for g in gs]),
        out_shape=[jax.ShapeDtypeStruct(got.shape, BF16) for got in gots], compiler_params=_params(),
    )(c_arr, *gs, *gots)


def _scatter_start(k, ss, thru):
    n = len(ss)

    def body(*refs):
        s_refs, land_refs = refs[2 * n + 1:3 * n + 1], refs[3 * n + 1:4 * n + 1]
        send_sems, recv_sems = refs[4 * n + 2:]
        x, y, c, chips = _place()
        me = 2 * x + y
        for a in range(n):
            for j, chip in enumerate(chips):
                pltpu.make_async_remote_copy(
                    src_ref=s_refs[a].at[2 * chip[0] + chip[1]], dst_ref=land_refs[a].at[me],
                    send_sem=send_sems.at[3 * a + j], recv_sem=recv_sems.at[3 * a + j], device_id=(*chip, c),
                    device_id_type=MESH).start()

    lands = [lax.empty(s.shape, s.dtype) for s in ss]
    hbm = [pltpu.HBM(s.shape, s.dtype) for s in ss]
    outs = pl.pallas_call(
        body, name=f"grad_scatter_start_{k}", in_specs=[HBM_SPEC] * (2 * n + 1),
        out_specs=[HBM_SPEC] * (2 * n + 1) + [SEM_SPEC, SEM_SPEC],
        out_shape=hbm + hbm + [pltpu.HBM(thru.shape, thru.dtype)] + [pltpu.SemaphoreType.DMA((3 * n,))] * 2,
        input_output_aliases={a: a for a in range(2 * n + 1)}, compiler_params=SPLIT_COPY,
    )(*[pltpu.with_memory_space_constraint(v, pltpu.HBM) for v in list(ss) + lands + [thru]])
    return (outs[:n], outs[n:2 * n], outs[2 * n + 1], outs[2 * n + 2]), outs[2 * n]


def _scatter_wait(k, ss, lands, send_sems, recv_sems, after):
    n = len(ss)

    def body(*refs):
        s_refs, land_refs = refs[:n], refs[n:2 * n]
        ssem, rsem = refs[2 * n], refs[2 * n + 1]
        x, y, c, chips = _place()
        me = 2 * x + y
        for a in range(n):
            for j, chip in enumerate(chips):
                cid = 2 * chip[0] + chip[1]
                cp = pltpu.make_async_remote_copy(
                    src_ref=s_refs[a].at[cid], dst_ref=land_refs[a].at[cid], send_sem=ssem.at[3 * a + j],
                    recv_sem=rsem.at[3 * a + j], device_id=(*chip, c), device_id_type=MESH)
                cp.wait_send()
                cp.wait_recv()

    hbm = [pltpu.HBM(s.shape, s.dtype) for s in ss]
    outs = pl.pallas_call(
        body, name=f"grad_scatter_wait_{k}",
        in_specs=[HBM_SPEC] * (2 * n) + [SEM_SPEC, SEM_SPEC, pl.BlockSpec(memory_space=pl.ANY)],
        out_specs=[HBM_SPEC] * (2 * n), out_shape=hbm + hbm,
        input_output_aliases={a: a for a in range(2 * n)}, compiler_params=SPLIT_COPY,
    )(*ss, *lands, send_sems, recv_sems, after)
    return outs[:n], outs[n:]


def _sum_chips(lands, ss, me_arr):
    n = len(lands)

    def body(me_ref, *refs):
        for own in range(N_CHIPS):
            @pl.when(me_ref[0] == own)
            def _(own=own):
                for a in range(n):
                    acc = None
                    for k in range(N_CHIPS):
                        term = (refs[n + a][...] if k == own else refs[a][k]).astype(F32)
                        acc = term if acc is None else acc + term
                    refs[2 * n + a][...] = acc

    return pl.pallas_call(
        body, name="grad_sum_chips",
        grid_spec=pltpu.PrefetchScalarGridSpec(
            num_scalar_prefetch=1, grid=(1,),
            in_specs=[pl.BlockSpec(la.shape, lambda i, me: (0, 0, 0)) for la in lands]
            + [pl.BlockSpec((None,) + la.shape[1:], lambda i, me: (me[0], 0, 0)) for la in lands],
            out_specs=[pl.BlockSpec(la.shape[1:], lambda i, me: (0, 0)) for la in lands]),
        out_shape=[jax.ShapeDtypeStruct(la.shape[1:], F32) for la in lands], compiler_params=_params(),
    )(me_arr, *lands, *ss)


def _allreduce_rows(stats):
    def body(s_ref, o_ref, buf, send_sems, recv_sems):
        x, y, c, _ = _place()
        me = 4 * x + 2 * y + c
        buf[me] = s_ref[...]
        cps = []
        for k in range(1, 8):
            px = jnp.where(k & 4, 1 - x, x)
            py = jnp.where(k & 2, 1 - y, y)
            pc = jnp.where(k & 1, 1 - c, c)
            cps.append(pltpu.make_async_remote_copy(
                src_ref=s_ref, dst_ref=buf.at[me], send_sem=send_sems.at[k - 1], recv_sem=recv_sems.at[k - 1],
                device_id=(px, py, pc), device_id_type=MESH))
        for cp in cps:
            cp.start()
        for cp in cps:
            cp.wait()
        acc = buf[0]
        for d in range(1, 8):
            acc = acc + buf[d]
        o_ref[...] = acc

    vm = pl.BlockSpec(memory_space=pltpu.VMEM)
    return pl.pallas_call(
        body, name="allreduce_rows", in_specs=[vm], out_specs=vm,
        out_shape=jax.ShapeDtypeStruct(stats.shape, F32),
        scratch_shapes=[pltpu.VMEM((8,) + stats.shape, F32), pltpu.SemaphoreType.DMA((7,)),
                        pltpu.SemaphoreType.DMA((7,))],
    )(stats)


def _adamw_math(w, g, m, v):
    m = ADAM_B1 * m + (1.0 - ADAM_B1) * g
    v = ADAM_B2 * v + (1.0 - ADAM_B2) * (g * g)
    m_hat = m / (1.0 - ADAM_B1 ** ADAM_STEP)
    v_hat = v / (1.0 - ADAM_B2 ** ADAM_STEP)
    delta = -ADAM_LR * (m_hat / (jnp.sqrt(v_hat) + ADAM_EPS) + ADAM_WD * w)
    return delta, m, v


def _adamw(ws, ms, vs, mines, theirs, l, c_arr, earlier, after):
    n = len(ws)
    held = [t for e in earlier if e is not None for t in e]
    assert len(held) in (0, 4 * n)
    late = [] if after is None else [after]

    def body(c_ref, *refs):
        outs = refs[len(refs) - 4 * n:]
        for a in range(n):
            w_ref, m_ref, v_ref, a_ref, b_ref = refs[5 * a:5 * a + 5]
            g = jnp.where(pl.program_id(0) == c_ref[0], a_ref[...], b_ref[...])
            delta, mn, vn = _adamw_math(w_ref[...], g, m_ref[...], v_ref[...])
            outs[4 * a][...] = g
            outs[4 * a + 1][...] = delta
            outs[4 * a + 2][...] = mn
            outs[4 * a + 3][...] = vn

    def blk(w):
        tr = w.shape[1] // 4
        return pl.BlockSpec((None, tr, w.shape[2]), lambda hh, i, cr: (l, 2 * hh + i, 0))

    def half(w):
        return pl.BlockSpec((w.shape[1] // 4, w.shape[2]), lambda hh, i, cr: (i, 0))

    outs = pl.pallas_call(
        body, name="adamw",
        grid_spec=pltpu.PrefetchScalarGridSpec(
            num_scalar_prefetch=1, grid=(2, 2),
            in_specs=[sp for w in ws for sp in (blk(w), blk(w), blk(w), half(w), half(w))]
            + [pl.BlockSpec(memory_space=pl.ANY)] * (len(held) + len(late)),
            out_specs=[blk(w) for w in ws for _ in range(4)]),
        out_shape=[jax.ShapeDtypeStruct(w.shape, F32) for w in ws for _ in range(4)],
        input_output_aliases={1 + 5 * n + t: t for t in range(len(held))}, compiler_params=_params(),
    )(c_arr, *[t for grp in zip(ws, ms, vs, mines, theirs) for t in grp], *held, *late)
    return [outs[4 * a:4 * a + 4] for a in range(n)]


def _adamw_rows(w, m, v, g):
    def body(w_ref, m_ref, v_ref, g_ref, d_ref, mo_ref, vo_ref):
        delta, mn, vn = _adamw_math(w_ref[...], g_ref[...], m_ref[...], v_ref[...])
        d_ref[...] = delta
        mo_ref[...] = mn
        vo_ref[...] = vn

    vm = pl.BlockSpec(memory_space=pltpu.VMEM)
    sh = jax.ShapeDtypeStruct(w.shape, F32)
    return pl.pallas_call(body, name="adamw_rows", in_specs=[vm] * 4, out_specs=[vm] * 3, out_shape=[sh] * 3)(w, m, v, g)


SUBLAYERS = (("ffn1_w_gate", "ffn1_w_up", "ffn1_w_down"), ("w_in", "w_proj_dil", "w_proj_sb", "w_out"),
             ("ffn2_w_gate", "ffn2_w_up", "ffn2_w_down"))
TRANSPOSED = ("ffn1_w_gate", "ffn1_w_up", "ffn2_w_gate", "ffn2_w_up")


def _pick_row(blocks):
    row = lax.broadcasted_iota(jnp.int32, (8, D_MODEL), 0)
    out = jnp.zeros((8, D_MODEL), F32)
    for i, b in enumerate(blocks):
        out = out + jnp.where(row == i, b, 0.0)
    return out


def kernel(x, norm_ffn1, ffn1_w_gate, ffn1_w_up, ffn1_w_down, norm_mix, w_in, w_proj_dil, w_proj_sb, w_out, norm_ffn2, ffn2_w_gate, ffn2_w_up, ffn2_w_down, norm_final, loss_target, m_norm_ffn1, m_ffn1_w_gate, m_ffn1_w_up, m_ffn1_w_down, m_norm_mix, m_w_in, m_w_proj_dil, m_w_proj_sb, m_w_out, m_norm_ffn2, m_ffn2_w_gate, m_ffn2_w_up, m_ffn2_w_down, m_norm_final, v_norm_ffn1, v_ffn1_w_gate, v_ffn1_w_up, v_ffn1_w_down, v_norm_mix, v_w_in, v_w_proj_dil, v_w_proj_sb, v_w_out, v_norm_ffn2, v_ffn2_w_gate, v_ffn2_w_up, v_ffn2_w_down, v_norm_final):
    given = dict(locals())
    for n in TRANSPOSED:
        for k in ("", "m_", "v_"):
            given[k + n] = jnp.swapaxes(given[k + n], 1, 2)
    weights = {n: given[n] for n in WEIGHT_NAMES}
    norms = {n: given[n] for n in NORM_NAMES}

    c_arr = lax.axis_index("c").astype(jnp.int32).reshape(1)
    me_arr = (2 * lax.axis_index("x") + lax.axis_index("y")).astype(jnp.int32).reshape(1)
    order = [(l, s, n) for l in range(DEPTH) for s in range(len(SUBLAYERS)) for n in SUBLAYERS[s]]
    n_first = len(SUBLAYERS[0])
    sent, token = {}, None
    for tag, idxs in (("a", range(n_first)), ("b", range(n_first, len(order)))):
        cast = _cast_into_slot([weights[order[i][2]] for i in idxs], [order[i][0] for i in idxs], me_arr, token)
        bufs, send_sems, recv_sems, token = _gather_start(tag, cast, [order[i][0] > 0 for i in idxs])
        for p, i in enumerate(idxs):
            sent[i] = (bufs[p], p, send_sems, recv_sems)

    def weights_of(l, s, after):
        idxs = [i for i, (ll, ss, _) in enumerate(order) if (ll, ss) == (l, s)]
        got = _gather_wait(len(SUBLAYERS) * l + s, [sent[i][0] for i in idxs], [sent[i][1] for i in idxs], l > 0,
                           sent[idxs[0]][2], sent[idxs[0]][3], after)
        return {order[i][2]: g for i, g in zip(idxs, got if l > 0 else _gather_relay(got))}

    out = {}
    to_add, in_flight = [], []

    def add_and_scatter(after):
        l, s, names, gs, lands, ssem, rsem = to_add.pop(0)
        k = len(SUBLAYERS) * l + s
        got = _sibling_wait(f"grad_exchange_wait_{k}", gs, _other_half, lands, ssem, rsem, after)
        sent, after = _scatter_start(k, _add_half(gs, got, c_arr), after)
        in_flight.append((l, s, names) + sent)
        return after

    def on_grads(l, s, grads, after):
        names = list(grads)
        k = len(SUBLAYERS) * l + s
        gs = [grads[n] for n in names]
        sent, after, _ = _sibling_start(f"grad_exchange_start_{k}", gs, _other_half,
                                        [(g.shape[0], g.shape[1] // 2, g.shape[2]) for g in gs], after)
        if to_add:
            after = add_and_scatter(after)
        to_add.append((l, s, names, gs) + sent)
        return after

    loss_blk, grad_x, gains, dg_final = _local_step(x[0], loss_target[0], norms, norm_final, weights_of, on_grads)
    grad_x = add_and_scatter(grad_x)

    def update(l, names, mine, swap, after):
        theirs = _sibling_wait(f"grad_swap_wait_{l}_{names[0]}", mine, _all_of, *swap, grad_x if after is None else after)
        res = _adamw([weights[n] for n in names], [given["m_" + n] for n in names], [given["v_" + n] for n in names],
                     mine, theirs, l, c_arr, [out.get(n) for n in names], after)
        out.update(zip(names, res))

    waiting = None
    for l, s, names, sums, lands, ssem, rsem in in_flight:
        sums, lands = _scatter_wait(len(SUBLAYERS) * l + s, sums, lands, ssem, rsem, grad_x)
        mine = _sum_chips(lands, sums, me_arr)
        swap, _, token = _sibling_start(f"grad_swap_start_{l}_{names[0]}", mine, _all_of,
                                        [m.shape for m in mine], None)
        if waiting is not None:
            update(*waiting, token)
        waiting = (l, names, mine, swap)
    update(*waiting, None)
    out = {k + n: (jnp.swapaxes(v, 1, 2) if n in TRANSPOSED else v)
           for n, res in out.items() for k, v in zip(("grad_", "delta_", "new_m_", "new_v_"), res)}
    out["grad_x"] = grad_x[None]

    rows = [gains[l][n] for n in NORM_NAMES for l in range(DEPTH)] + [dg_final, loss_blk]
    total = _allreduce_rows(_pick_row(rows))
    out["loss"] = total[7, 0]
    wn = jnp.concatenate([given[n] for n in NORM_NAMES] + [norm_final[None], jnp.zeros((1, D_MODEL), F32)])
    mn_ = jnp.concatenate([given["m_" + n] for n in NORM_NAMES] + [m_norm_final[None], jnp.zeros((1, D_MODEL), F32)])
    vn_ = jnp.concatenate([given["v_" + n] for n in NORM_NAMES] + [v_norm_final[None], jnp.ones((1, D_MODEL), F32)])
    d_n, m_n, v_n = _adamw_rows(wn, mn_, vn_, total)
    for i, n in enumerate(NORM_NAMES):
        sl = slice(i * DEPTH, (i + 1) * DEPTH)
        out["grad_" + n], out["delta_" + n], out["new_m_" + n], out["new_v_" + n] = total[sl], d_n[sl], m_n[sl], v_n[sl]
    out["grad_norm_final"], out["delta_norm_final"] = total[6], d_n[6]
    out["new_m_norm_final"], out["new_v_norm_final"] = m_n[6], v_n[6]

    names = ["norm_ffn1", "ffn1_w_gate", "ffn1_w_up", "ffn1_w_down", "norm_mix", "w_in", "w_proj_dil", "w_proj_sb",
             "w_out", "norm_ffn2", "ffn2_w_gate", "ffn2_w_up", "ffn2_w_down", "norm_final"]
    return (out["loss"], out["grad_x"], *[out["grad_" + n] for n in names], *[out["delta_" + n] for n in names],
            *[out["new_m_" + n] for n in names], *[out["new_v_" + n] for n in names])
```

```python
import functools

import jax
import jax.numpy as jnp
from jax import lax
from jax.experimental import pallas as pl
from jax.experimental.pallas import tpu as pltpu

F32 = jnp.float32
BF16 = jnp.bfloat16

D_MODEL = 1024
DEPTH = 2
N_CHIPS = 4
HEAD_DIM = 64
ROPE_DIM = 16
ROPE_THETA = 500000.0
DIL_GROUPS = ((128, 1), (512, 4), (2048, 16))
SPAN = 128
Q_BLOCK = 128
RMS_EPS = 1e-6
D_ATT = 256
COL_QS = 2304
COL_GD = 3072
COL_GS = 4096
ADAM_LR, ADAM_B1, ADAM_B2, ADAM_EPS, ADAM_WD, ADAM_STEP = 0.001, 0.9, 0.999, 1e-08, 0.01, 10

VMEM_LIMIT = 52 * 1024 * 1024
TM = 512
NEG = -1e30

NN = (((1,), (0,)), ((), ()))
NT = (((1,), (1,)), ((), ()))
TN = (((0,), (0,)), ((), ()))
MESH = pl.DeviceIdType.MESH

WEIGHT_NAMES = ("ffn1_w_gate", "ffn1_w_up", "ffn1_w_down", "w_in", "w_proj_dil",
                "w_proj_sb", "w_out", "ffn2_w_gate", "ffn2_w_up", "ffn2_w_down")
NORM_NAMES = ("norm_ffn1", "norm_mix", "norm_ffn2")


def _params(**kw):
    return pltpu.CompilerParams(vmem_limit_bytes=VMEM_LIMIT, **kw)


def _sigmoid(x):
    return 0.5 * jnp.tanh(0.5 * x) + 0.5


def _mm_body(pairs, n_in, n_out, n_acc, dims, nk, epilogue, *refs):
    ins = refs[:n_in]
    outs = refs[n_in:n_in + n_out]
    accs = refs[n_in + n_out:]
    i = pl.program_id(0)
    k = pl.program_id(2)

    def operand(a):
        return (a(ins) if callable(a) else ins[a][...]).astype(BF16)

    def dot(ia, ib):
        return lax.dot_general(operand(ia), operand(ib), dims, preferred_element_type=F32)

    if nk == 1:
        parts = [None] * n_acc
        for ia, ib, ic in pairs:
            parts[ic] = dot(ia, ib) if parts[ic] is None else parts[ic] + dot(ia, ib)
        epilogue(parts, ins, outs, i)
        return

    @pl.when(k == 0)
    def _():
        for c in range(n_acc):
            accs[c][...] = jnp.zeros_like(accs[c])

    for ia, ib, ic in pairs:
        accs[ic][...] += dot(ia, ib)

    @pl.when(k == nk - 1)
    def _():
        epilogue([a[...] for a in accs], ins, outs, i)


def _mm(name, ins, in_specs, pairs, n_acc, acc_shape, dims, grid, epilogue, out_shapes, out_specs):
    nk = grid[2]
    scratch = [pltpu.VMEM(acc_shape, F32) for _ in range(n_acc)] if nk > 1 else []
    body = functools.partial(_mm_body, tuple(pairs), len(ins), len(out_shapes), n_acc, dims, nk, epilogue)
    return pl.pallas_call(
        body, name=name, grid=grid, in_specs=in_specs, out_specs=out_specs, out_shape=out_shapes,
        scratch_shapes=scratch,
        compiler_params=_params(dimension_semantics=("arbitrary", "arbitrary", "arbitrary")),
    )(*ins)


def _rms_bwd_epilogue(x_idx, g_idx, dxo_idx):
    def ep(vals, ins, outs, i):
        dh = vals[0]
        x = ins[x_idx][...]
        g = ins[g_idx][...]
        rstd = lax.rsqrt(jnp.mean(x * x, axis=-1, keepdims=True) + RMS_EPS)
        xhat = x * rstd
        dxhat = dh * g
        dx = rstd * (dxhat - xhat * jnp.mean(dxhat * xhat, axis=-1, keepdims=True))
        outs[0][...] = ins[dxo_idx][...] + dx
        dg = jnp.broadcast_to(jnp.sum(dh * xhat, axis=0, keepdims=True), outs[1].shape)

        @pl.when(i == 0)
        def _():
            outs[1][...] = dg

        @pl.when(i > 0)
        def _():
            outs[1][...] += dg
    return ep


def _normed(x_idx, g_idx):
    seen = {}

    def f(ins):
        if id(ins) not in seen:
            xv = ins[x_idx][...]
            h = xv * lax.rsqrt(jnp.mean(xv * xv, axis=-1, keepdims=True) + RMS_EPS)
            seen[id(ins)] = (ins, (h * ins[g_idx][...]).astype(BF16))
        return seen[id(ins)][1]
    return f


def _rope_tables(T):
    half = ROPE_DIM // 2
    lane = jnp.arange(128) % HEAD_DIM
    inv_freq = ROPE_THETA ** (-(2 * (lane % half)).astype(F32) / ROPE_DIM)
    ang = jnp.arange(T, dtype=F32)[:, None] * inv_freq[None, :]
    cos, sin = jnp.cos(ang), jnp.sin(ang)
    c = jnp.where(lane < ROPE_DIM, cos, 1.0)
    s1 = jnp.where(lane < half, -sin, 0.0)
    s2 = jnp.where((lane >= half) & (lane < ROPE_DIM), sin, 0.0)
    return c, s1, s2


def _rope_fwd(xv, c, s1, s2):
    w = xv.shape[1]
    half = ROPE_DIM // 2
    return xv * c + pltpu.roll(xv, w - half, 1) * s1 + pltpu.roll(xv, half, 1) * s2


def _rope_bwd(dy, c, s1, s2):
    w = dy.shape[1]
    half = ROPE_DIM // 2
    return dy * c + pltpu.roll(dy * s1, half, 1) + pltpu.roll(dy * s2, w - half, 1)


def _assemble_dproj(dqk, rest, turned, gates, tabs):
    T = gates[0].shape[0]
    n_qk, n_rest = len(dqk), len(rest) + len(turned)
    width = (n_qk + n_rest) * D_ATT + 2 * D_MODEL

    def body(*refs):
        ins, (c_ref, s1_ref, s2_ref), o_ref = refs[:n_qk + n_rest + 2], refs[-4:-1], refs[-1]
        c = jnp.concatenate([c_ref[...]] * 2, axis=1)
        s1 = jnp.concatenate([s1_ref[...]] * 2, axis=1)
        s2 = jnp.concatenate([s2_ref[...]] * 2, axis=1)
        for b in range(n_qk + n_rest):
            v = ins[b][...]
            if b < n_qk:
                v = _rope_bwd(v, c, s1, s2)
            if b >= n_qk + len(rest):
                v = v.T
            o_ref[:, b * D_ATT:(b + 1) * D_ATT] = v.astype(BF16)
        off = (n_qk + n_rest) * D_ATT
        o_ref[:, off:off + D_MODEL] = ins[-2][...]
        o_ref[:, off + D_MODEL:] = ins[-1][...]

    att = pl.BlockSpec((TM, D_ATT), lambda i: (i, 0))
    att_turned = pl.BlockSpec((D_ATT, TM), lambda i: (0, i))
    wide = pl.BlockSpec((TM, D_MODEL), lambda i: (i, 0))
    tab = pl.BlockSpec((TM, 128), lambda i: (i, 0))
    return pl.pallas_call(
        body, name="assemble_dproj", grid=(T // TM,),
        in_specs=[att] * (n_qk + len(rest)) + [att_turned] * len(turned) + [wide, wide, tab, tab, tab],
        out_specs=pl.BlockSpec((TM, width), lambda i: (i, 0)),
        out_shape=jax.ShapeDtypeStruct((T, width), BF16), compiler_params=_params(),
    )(*dqk, *rest, *turned, *gates, *tabs)


def _dil_merge(os_, lses):
    T = os_[0].shape[0]

    def body(o0, o1, o2, l0, l1, l2, o_ref, lse_ref):
        a, b, c = l0[...], l1[...], l2[...]
        m = jnp.maximum(jnp.maximum(a, b), c)
        ea, eb, ec = jnp.exp(a - m), jnp.exp(b - m), jnp.exp(c - m)
        den = ea + eb + ec
        o_ref[...] = (ea * o0[...] + eb * o1[...] + ec * o2[...]) / den
        lse_ref[...] = m + jnp.log(den)

    blk = pl.BlockSpec((TM, D_ATT), lambda i: (i, 0))
    sh = jax.ShapeDtypeStruct((T, D_ATT), F32)
    return pl.pallas_call(
        body, name="dil_merge", grid=(T // TM,), in_specs=[blk] * 6, out_specs=[blk, blk],
        out_shape=[sh, sh], compiler_params=_params(),
    )(*os_, *lses)


def _final_loss(x, gain, target):
    T = x.shape[0]

    def body(x_ref, g_ref, t_ref, dx_ref, dg_ref, loss_ref):
        xv = x_ref[...]
        g = g_ref[...]
        rstd = lax.rsqrt(jnp.mean(xv * xv, axis=-1, keepdims=True) + RMS_EPS)
        xhat = xv * rstd
        err = xhat * g - t_ref[...]
        loss = 0.5 * jnp.sum(jnp.mean(err * err, axis=-1, keepdims=True), axis=0, keepdims=True)
        dy = err * (1.0 / D_MODEL)
        dxhat = dy * g
        dx_ref[...] = rstd * (dxhat - xhat * jnp.mean(dxhat * xhat, axis=-1, keepdims=True))
        dg = jnp.broadcast_to(jnp.sum(dy * xhat, axis=0, keepdims=True), dg_ref.shape)
        ls = jnp.broadcast_to(loss, loss_ref.shape)

        @pl.when(pl.program_id(0) == 0)
        def _():
            dg_ref[...] = dg
            loss_ref[...] = ls

        @pl.when(pl.program_id(0) > 0)
        def _():
            dg_ref[...] += dg
            loss_ref[...] += ls

    blk = pl.BlockSpec((TM, D_MODEL), lambda i: (i, 0))
    row = pl.BlockSpec((1, D_MODEL), lambda i: (0, 0))
    acc = pl.BlockSpec((8, D_MODEL), lambda i: (0, 0))
    return pl.pallas_call(
        body, name="final_loss", grid=(T // TM,), in_specs=[blk, row, blk], out_specs=[blk, acc, acc],
        out_shape=[jax.ShapeDtypeStruct((T, D_MODEL), F32), jax.ShapeDtypeStruct((8, D_MODEL), F32),
                   jax.ShapeDtypeStruct((8, D_MODEL), F32)],
        compiler_params=_params(dimension_semantics=("arbitrary",)),
    )(x, gain, target)


def _pair_masks():
    lane = lax.broadcasted_iota(jnp.int32, (SPAN, 128), 1)
    return [lane < HEAD_DIM, lane >= HEAD_DIM]


def _stack_heads(x, masks):
    return jnp.concatenate([jnp.where(m, x, 0.0) for m in masks], axis=0)


def _unstack_heads(y, masks):
    rows = y.shape[0] // len(masks)
    out = jnp.where(masks[0], y[:rows], 0.0)
    for h in range(1, len(masks)):
        out = out + jnp.where(masks[h], y[rows * h:rows * (h + 1)], 0.0)
    return out


DIL_PAIR = 2


def _dil_rows(idx, d):
    u = idx // d
    r = idx - u * d
    own = pl.ds(u * (SPAN * d) + r, SPAN, stride=d) if d > 1 else pl.ds(pl.multiple_of(u * SPAN, SPAN), SPAN)
    up = jnp.maximum(u - 1, 0)
    prev = pl.ds(up * (SPAN * d) + r, SPAN, stride=d) if d > 1 else pl.ds(pl.multiple_of(up * SPAN, SPAN), SPAN)
    return u, own, prev


def _dil_valid(u):
    qi = lax.broadcasted_iota(jnp.int32, (2 * SPAN, 2 * SPAN), 0) & (SPAN - 1)
    kj = lax.broadcasted_iota(jnp.int32, (2 * SPAN, 2 * SPAN), 1)
    in_prev = (kj < SPAN) & (kj >= qi + jnp.where(u > 0, 0, SPAN))
    return in_prev | ((kj >= SPAN) & (kj - SPAN <= qi))


def _dil_keys(ref, own, prev):
    return jnp.concatenate([ref[prev, :], ref[own, :]], axis=0).astype(BF16)


def _dil_fwd(proj, g, d):
    T = proj.shape[0]
    n_iter = T // SPAN

    def body(q_ref, k_ref, v_ref, o_ref, lse_ref):
        masks = _pair_masks()

        def step(pair, carry):
            its = [_dil_rows(DIL_PAIR * pair + e, d) for e in range(DIL_PAIR)]
            qs = [_stack_heads(q_ref[own, :] * (HEAD_DIM ** -0.5), masks).astype(BF16) for _, own, _ in its]
            kks = [_dil_keys(k_ref, own, prev) for _, own, prev in its]
            vvs = [_dil_keys(v_ref, own, prev) for _, own, prev in its]
            ss = [jnp.where(_dil_valid(u), lax.dot_general(q, kk, NT, preferred_element_type=F32), NEG)
                  for (u, _, _), q, kk in zip(its, qs, kks)]
            ms = [jnp.max(s, axis=1, keepdims=True) for s in ss]
            ps = [jnp.exp(s - m) for s, m in zip(ss, ms)]
            dens = [jnp.sum(p, axis=1, keepdims=True) for p in ps]
            pvs = [lax.dot_general(p.astype(BF16), vv, NN, preferred_element_type=F32) / den
                   for p, vv, den in zip(ps, vvs, dens)]
            for (_, own, _), pv, m, den in zip(its, pvs, ms, dens):
                o_ref[own, :] = _unstack_heads(pv, masks)
                lse_ref[own, :] = _unstack_heads(jnp.broadcast_to(m + jnp.log(den), pv.shape), masks)
            return carry

        lax.fori_loop(0, n_iter // DIL_PAIR, step, 0)

    def col(b):
        return pl.BlockSpec((T, 128), lambda p: (0, b + p))

    sh = jax.ShapeDtypeStruct((T, D_ATT), F32)
    out = pl.BlockSpec((T, 128), lambda p: (0, p))
    return pl.pallas_call(
        body, name=f"dil_fwd_d{d}", grid=(2,),
        in_specs=[col(2 * g), col(6 + 2 * g), col(12 + 2 * g)], out_specs=[out, out], out_shape=[sh, sh],
        compiler_params=_params(dimension_semantics=("arbitrary",)),
    )(proj, proj, proj)


def _dil_bwd(proj, do, o_dil, lse, g, d):
    T = proj.shape[0]
    n_iter = T // SPAN

    def body(q_ref, k_ref, v_ref, do_ref, o_ref, lse_ref, dq_ref, dk_ref, dv_ref):
        masks = _pair_masks()
        head_lanes = jnp.concatenate(masks, axis=0)

        def step(pair, carry):
            its = [_dil_rows(DIL_PAIR * pair + e, d) for e in range(DIL_PAIR)]
            qs = [_stack_heads(q_ref[own, :] * (HEAD_DIM ** -0.5), masks).astype(BF16) for _, own, _ in its]
            kks = [_dil_keys(k_ref, own, prev) for _, own, prev in its]
            vvs = [_dil_keys(v_ref, own, prev) for _, own, prev in its]
            doms = [_stack_heads(do_ref[own, :], masks) for _, own, _ in its]
            dos = [dom.astype(BF16) for dom in doms]
            deltas = [jnp.sum(dom * jnp.concatenate([o_ref[own, :]] * 2, axis=0), axis=1, keepdims=True)
                      for dom, (_, own, _) in zip(doms, its)]
            lrows = [jnp.max(jnp.where(head_lanes, jnp.concatenate([lse_ref[own, :]] * 2, axis=0), NEG),
                             axis=1, keepdims=True) for _, own, _ in its]
            ss = [lax.dot_general(q, kk, NT, preferred_element_type=F32) for q, kk in zip(qs, kks)]
            dps = [lax.dot_general(do_b, vv, NT, preferred_element_type=F32) for do_b, vv in zip(dos, vvs)]
            ps = [jnp.where(_dil_valid(u), jnp.exp(s - lrow), 0.0) for (u, _, _), s, lrow in zip(its, ss, lrows)]
            dss = [(p * (dp - delta)).astype(BF16) for p, dp, delta in zip(ps, dps, deltas)]
            dqs = [lax.dot_general(ds, kk, NN, preferred_element_type=F32) for ds, kk in zip(dss, kks)]
            dkks = [lax.dot_general(ds, q, TN, preferred_element_type=F32) for ds, q in zip(dss, qs)]
            dvvs = [lax.dot_general(p.astype(BF16), do_b, TN, preferred_element_type=F32) for p, do_b in zip(ps, dos)]
            for (_, own, prev), dq, dkk, dvv in zip(its, dqs, dkks, dvvs):
                dq_ref[own, :] = _unstack_heads(dq, masks) * (HEAD_DIM ** -0.5)
                dk_ref[own, :] = dkk[SPAN:]
                dv_ref[own, :] = dvv[SPAN:]
                dk_ref[prev, :] = dk_ref[prev, :] + dkk[:SPAN]
                dv_ref[prev, :] = dv_ref[prev, :] + dvv[:SPAN]
            return carry

        lax.fori_loop(0, n_iter // DIL_PAIR, step, 0)

    def col(b):
        return pl.BlockSpec((T, 128), lambda p: (0, b + p))

    sh = jax.ShapeDtypeStruct((T, D_ATT), F32)
    return pl.pallas_call(
        body, name=f"dil_bwd_d{d}", grid=(2,),
        in_specs=[col(2 * g), col(6 + 2 * g), col(12 + 2 * g), col(0), col(0), col(0)],
        out_specs=[col(0), col(0), col(0)], out_shape=[sh, sh, sh],
        compiler_params=_params(dimension_semantics=("arbitrary",)),
    )(proj, proj, proj, do, o_dil, lse)


SB_KT = 512


def _sb_tri(strict):
    a = lax.broadcasted_iota(jnp.int32, (Q_BLOCK, Q_BLOCK), 0)
    b = lax.broadcasted_iota(jnp.int32, (Q_BLOCK, Q_BLOCK), 1)
    return jnp.where((a > b) if strict else (a >= b), 1.0, 0.0).astype(BF16)


def _split_stack(x):
    nb = x.shape[1] // Q_BLOCK
    blocks = [x[:, Q_BLOCK * b:Q_BLOCK * (b + 1)] for b in range(nb)]
    hi = [b.astype(BF16) for b in blocks]
    lo = [(b - h.astype(F32)).astype(BF16) for b, h in zip(blocks, hi)]
    return blocks, jnp.concatenate(hi + lo, axis=0)


def _suffix_from(y, blocks, c):
    r = blocks[0].shape[0]
    nb = len(blocks)
    outs = [None] * nb
    run = c
    for b in reversed(range(nb)):
        outs[b] = run + y[r * b:r * (b + 1)] + y[r * (nb + b):r * (nb + b + 1)]
        run = run + jnp.sum(blocks[b], axis=1, keepdims=True)
    return jnp.concatenate(outs, axis=1), run


SB_HEADS = D_ATT // HEAD_DIM
SB_FWD_CHAINS = 2
SB_BWD_CHAINS = 1


SB_PLACES = SB_KT // Q_BLOCK


def _sb_past(place, rows):
    row = lax.broadcasted_iota(jnp.int32, (rows, Q_BLOCK * (place + 1)), 0) & (Q_BLOCK - 1)
    col = lax.broadcasted_iota(jnp.int32, (rows, Q_BLOCK * (place + 1)), 1)
    return col < row + place * Q_BLOCK


def _sb_head_masks(chains):
    lane = lax.broadcasted_iota(jnp.int32, (Q_BLOCK, D_ATT), 1)
    masks = [(lane >= HEAD_DIM * h) & (lane < HEAD_DIM * (h + 1)) for h in range(SB_HEADS)]
    per = SB_HEADS // chains
    return [masks[per * g:per * (g + 1)] for g in range(chains)]


def _sb_rows(t, width=SB_KT):
    return pl.ds(pl.multiple_of(t * SB_KT, SB_KT), width)


def _sb_widen(x):
    if x.shape[1] == SB_KT:
        return x
    return jnp.concatenate([x, jnp.zeros((x.shape[0], SB_KT - x.shape[1]), x.dtype)], axis=1)


def _sb_log_terms(z, past):
    lsz = jnp.minimum(z, 0.0) - jnp.log(1.0 + jnp.exp(-jnp.abs(z)))
    lk = lsz - z
    return lsz, (lk if past is None else jnp.where(past, lk, 0.0))


def _sb_weights(lsz, after, past):
    w = jnp.exp(lsz + after)
    return w if past is None else jnp.where(past, w, 0.0)


def _sb_fwd(proj):
    T = proj.shape[0]
    rows = SB_HEADS // SB_FWD_CHAINS * Q_BLOCK

    def at_place(place, n_tiles, q_ref, k_ref, v_ref, o_ref, z_buf, w_buf):
        masks = _sb_head_masks(SB_FWD_CHAINS)
        tri = _sb_tri(True)
        q = q_ref[...] * (HEAD_DIM ** -0.5)
        qs = [_stack_heads(q, m).astype(BF16) for m in masks]

        def scores(t, width=SB_KT):
            kb = k_ref[_sb_rows(t, width), :].astype(BF16)
            return [lax.dot_general(g, kb, NT, preferred_element_type=F32) for g in qs]

        def weights(zs, cs, past, between=lambda: None):
            logs = [_sb_log_terms(z, past) for z in zs]
            splits = [_split_stack(lk) for _, lk in logs]
            ys = [lax.dot_general(x, tri, NN, preferred_element_type=F32) for _, x in splits]
            between()
            sums = [_suffix_from(y, blocks, c) for y, (blocks, _), c in zip(ys, splits, cs)]
            ws = [_sb_weights(lsz, after, past).astype(BF16) for (lsz, _), (after, _) in zip(logs, sums)]
            return ws, [c for _, c in sums]

        def values(acc, slot, t):
            vb = v_ref[_sb_rows(t), :].astype(BF16)
            for g, m in enumerate(masks):
                acc = acc + _unstack_heads(lax.dot_general(w_buf[slot, g], vb, NN, preferred_element_type=F32), m)
            return acc

        def keep(buf, slot, xs):
            for g, x in enumerate(xs):
                buf[slot, g] = x

        zs = scores(n_tiles - 1, Q_BLOCK * (place + 1))
        keep(z_buf, 0, scores(jnp.maximum(n_tiles - 2, 0)))
        ws, cs = weights(zs, [jnp.zeros((rows, 1), F32)] * SB_FWD_CHAINS, _sb_past(place, rows))
        keep(w_buf, 0, [_sb_widen(w) for w in ws])

        def step(tt, carry):
            t = n_tiles - 2 - tt
            cur = tt & 1
            acc = values(carry[0], cur, t + 1)
            ws, cs = weights([z_buf[cur, g] for g in range(SB_FWD_CHAINS)], carry[1:], None,
                             lambda: keep(z_buf, 1 - cur, scores(jnp.maximum(t - 1, 0))))
            keep(w_buf, 1 - cur, ws)
            return (acc, *cs)

        carry = lax.fori_loop(0, n_tiles - 1, step, (jnp.zeros((Q_BLOCK, D_ATT), F32), *cs))
        o_ref[...] = values(carry[0], (n_tiles - 1) & 1, 0)

    def body(*refs):
        n_tiles = pl.program_id(1) + 1
        for place in range(SB_PLACES):
            pl.when(pl.program_id(0) == place)(functools.partial(at_place, place, n_tiles, *refs))

    cb = COL_QS // D_ATT
    return pl.pallas_call(
        body, name="sb_fwd", grid=(SB_PLACES, T // SB_KT),
        in_specs=[pl.BlockSpec((Q_BLOCK, D_ATT), lambda p, j: (SB_PLACES * j + p, cb)),
                  pl.BlockSpec((T, D_ATT), lambda p, j: (0, cb + 1)),
                  pl.BlockSpec((T, D_ATT), lambda p, j: (0, cb + 2))],
        out_specs=pl.BlockSpec((Q_BLOCK, D_ATT), lambda p, j: (SB_PLACES * j + p, 0)),
        out_shape=jax.ShapeDtypeStruct((T, D_ATT), F32),
        scratch_shapes=[pltpu.VMEM((2, SB_FWD_CHAINS, rows, SB_KT), F32),
                        pltpu.VMEM((2, SB_FWD_CHAINS, rows, SB_KT), BF16)],
        compiler_params=_params(dimension_semantics=("arbitrary", "arbitrary")),
    )(proj, proj, proj)


def _sb_bwd(proj, do, o):
    T = proj.shape[0]
    n_rows = SB_HEADS // SB_BWD_CHAINS * Q_BLOCK

    def at_place(place, n_tiles, q_ref, k_ref, v_ref, do_ref, o_ref, dq_ref, dk_ref, dv_ref,
                 z_buf, gv_buf, dz_buf, w_buf):
        masks = _sb_head_masks(SB_BWD_CHAINS)
        tri = _sb_tri(True)
        tri_incl = _sb_tri(False)

        q = q_ref[...] * (HEAD_DIM ** -0.5)
        qs = [_stack_heads(q, m).astype(BF16) for m in masks]
        dos = [_stack_heads(do_ref[...], m).astype(BF16) for m in masks]
        qts = [_stack_heads(q, m).T.astype(BF16) for m in masks]
        dots = [_stack_heads(do_ref[...], m).T.astype(BF16) for m in masks]
        o_rep = jnp.concatenate([o_ref[...]] * (SB_HEADS // SB_BWD_CHAINS), axis=0)
        deltas = [jnp.sum(d.astype(F32) * o_rep, axis=1, keepdims=True) for d in dos]

        def scores(t, width=SB_KT):
            kb = k_ref[_sb_rows(t, width), :].astype(BF16)
            return [lax.dot_general(g, kb, NT, preferred_element_type=F32) for g in qs]

        def value_grads(t, width=SB_KT):
            vb = v_ref[_sb_rows(t, width), :].astype(BF16)
            return [lax.dot_general(d, vb, NT, preferred_element_type=F32) for d in dos]

        def keep(buf, slot, xs):
            for g, x in enumerate(xs):
                buf[slot, g] = x

        def kept(buf, slot):
            return [buf[slot, g] for g in range(SB_BWD_CHAINS)]

        def score_grads(zs, gvs, cs, ces, past, after_first=lambda: None, after_second=lambda: None):
            logs = [_sb_log_terms(z, past) for z in zs]
            splits = [_split_stack(lk) for _, lk in logs]
            ys = [lax.dot_general(x, tri, NN, preferred_element_type=F32) for _, x in splits]
            after_first()
            sums = [_suffix_from(y, blocks, c) for y, (blocks, _), c in zip(ys, splits, cs)]
            wbs = [_sb_weights(lsz, after, past).astype(BF16) for (lsz, _), (after, _) in zip(logs, sums)]
            es = [wb.astype(F32) * gv for wb, gv in zip(wbs, gvs())]
            esplits = [_split_stack(e) for e in es]
            eys = [lax.dot_general(x, tri_incl, NN, preferred_element_type=F32) for _, x in esplits]
            after_second()
            esums = [_suffix_from(y, blocks, ce) for y, (blocks, _), ce in zip(eys, esplits, ces)]
            dzbs = []
            for e, (lsz, lk), (suf, _), delta in zip(es, logs, esums, deltas):
                dz = e * jnp.exp(lk) - (delta - suf) * jnp.exp(lsz)
                dzbs.append((dz if past is None else jnp.where(past, dz, 0.0)).astype(BF16))
            return dzbs, wbs, [c for _, c in sums], [c for _, c in esums]

        def outputs(dq, slot, t):
            kb = k_ref[_sb_rows(t), :].astype(BF16)
            dk_t = dv_t = None
            for m, dzb, wb, g, d in zip(masks, kept(dz_buf, slot), kept(w_buf, slot), qts, dots):
                dq = dq + _unstack_heads(lax.dot_general(dzb, kb, NN, preferred_element_type=F32), m)
                a = lax.dot_general(g, dzb, NN, preferred_element_type=F32)
                b = lax.dot_general(d, wb, NN, preferred_element_type=F32)
                dk_t = a if dk_t is None else dk_t + a
                dv_t = b if dv_t is None else dv_t + b
            return dq, dk_t, dv_t

        def add_columns(t, dk_t, dv_t):
            rows = _sb_rows(t)
            dk_ref[:, rows] = dk_ref[:, rows] + dk_t
            dv_ref[:, rows] = dv_ref[:, rows] + dv_t

        zcol = [jnp.zeros((n_rows, 1), F32)] * SB_BWD_CHAINS
        ahead = jnp.maximum(n_tiles - 2, 0)
        width = Q_BLOCK * (place + 1)
        zs, gvs = scores(n_tiles - 1, width), value_grads(n_tiles - 1, width)
        keep(z_buf, 0, scores(ahead))
        keep(gv_buf, 0, value_grads(ahead))
        dzbs, wbs, cs, ces = score_grads(zs, lambda: gvs, zcol, zcol, _sb_past(place, n_rows))
        keep(dz_buf, 0, [_sb_widen(x) for x in dzbs])
        keep(w_buf, 0, [_sb_widen(x) for x in wbs])

        def step(tt, carry):
            t = n_tiles - 2 - tt
            cur = tt & 1
            ahead = jnp.maximum(t - 1, 0)
            dq, dk_t, dv_t = outputs(carry[0], cur, t + 1)
            dzbs, wbs, cs, ces = score_grads(
                kept(z_buf, cur), lambda: kept(gv_buf, cur),
                carry[1:1 + SB_BWD_CHAINS], carry[1 + SB_BWD_CHAINS:], None,
                lambda: keep(z_buf, 1 - cur, scores(ahead)),
                lambda: keep(gv_buf, 1 - cur, value_grads(ahead)))
            keep(dz_buf, 1 - cur, dzbs)
            keep(w_buf, 1 - cur, wbs)
            add_columns(t + 1, dk_t, dv_t)
            return (dq, *cs, *ces)

        carry = lax.fori_loop(0, n_tiles - 1, step, (jnp.zeros((Q_BLOCK, D_ATT), F32), *cs, *ces))
        dq, dk_t, dv_t = outputs(carry[0], (n_tiles - 1) & 1, 0)
        add_columns(0, dk_t, dv_t)
        dq_ref[...] = dq * (HEAD_DIM ** -0.5)

    def body(*refs):
        n_tiles = pl.program_id(1) + 1
        dk_ref, dv_ref = refs[6:8]

        @pl.when((pl.program_id(0) == 0) & (n_tiles == 1))
        def _():
            dk_ref[...] = jnp.zeros_like(dk_ref)
            dv_ref[...] = jnp.zeros_like(dv_ref)

        for place in range(SB_PLACES):
            pl.when(pl.program_id(0) == place)(functools.partial(at_place, place, n_tiles, *refs))

    cb = COL_QS // D_ATT
    blk = pl.BlockSpec((Q_BLOCK, D_ATT), lambda p, j: (SB_PLACES * j + p, 0))
    turned = pl.BlockSpec((D_ATT, T), lambda p, j: (0, 0))
    sh = jax.ShapeDtypeStruct((T, D_ATT), F32)
    sh_turned = jax.ShapeDtypeStruct((D_ATT, T), F32)
    kept_f32 = pltpu.VMEM((2, SB_BWD_CHAINS, n_rows, SB_KT), F32)
    kept_bf16 = pltpu.VMEM((2, SB_BWD_CHAINS, n_rows, SB_KT), BF16)
    return pl.pallas_call(
        body, name="sb_bwd", grid=(SB_PLACES, T // SB_KT),
        in_specs=[pl.BlockSpec((Q_BLOCK, D_ATT), lambda p, j: (SB_PLACES * j + p, cb)),
                  pl.BlockSpec((T, D_ATT), lambda p, j: (0, cb + 1)),
                  pl.BlockSpec((T, D_ATT), lambda p, j: (0, cb + 2)), blk, blk],
        out_specs=[blk, turned, turned], out_shape=[sh, sh_turned, sh_turned],
        scratch_shapes=[kept_f32, kept_f32, kept_bf16, kept_bf16],
        compiler_params=_params(dimension_semantics=("arbitrary", "arbitrary")),
    )(proj, proj, proj, do, o)


def _tok(c, by=None):
    if by is None:
        return pl.BlockSpec((TM, c), lambda i, j, k: (i, 0))
    if by == 1:
        return pl.BlockSpec((TM, c), lambda i, j, k: (i, j))
    return pl.BlockSpec((TM, c), lambda i, j, k: (i, k))


def _gain_spec():
    return pl.BlockSpec((1, D_MODEL), lambda i, j, k: (0, 0))


def _wfull(r, c, l):
    return pl.BlockSpec((N_CHIPS, None, r, c), lambda i, j, k: (0, l, 0, 0), pipeline_mode=pl.Buffered(1))


def _pick(idx, c):
    return lambda ins: ins[idx][c]


def _cols(idx, c, w):
    return lambda ins: ins[idx][:, c * w:(c + 1) * w]


def _rows(rows, width):
    return pl.BlockSpec((rows, width), lambda i, j, k: (i, 0))


def _whole(shape):
    return pl.BlockSpec(shape, lambda i, j, k: (0, 0), pipeline_mode=pl.Buffered(1))


def _ffn_fwd(x, gain, wg, wu, wd):
    T = x.shape[0]
    wg, wu, wd = (w.reshape(-1, D_MODEL) for w in (wg, wu, wd))
    ff = wd.shape[0]
    tm = TM // 2
    normed = _normed(0, 3)

    def swiglu(vals, ins, outs, i):
        gt, up = vals
        s = _sigmoid(gt)
        sil = gt * s
        outs[0][...] = sil.astype(BF16)
        outs[1][...] = (up * (s * (1.0 + gt * (1.0 - s)))).astype(BF16)
        outs[2][...] = (sil * up).astype(BF16)
        outs[3][...] = normed(ins)

    ash = jax.ShapeDtypeStruct((T, ff), BF16)
    sil, up_dsil, act, h = _mm(
        "ffn_up", [x, wg, wu, gain], [_rows(tm, D_MODEL), _whole(wg.shape), _whole(wu.shape), _gain_spec()],
        [(normed, 1, 0), (normed, 2, 1)], 2, None, NT, (T // tm, 1, 1), swiglu,
        [ash] * 3 + [jax.ShapeDtypeStruct((T, D_MODEL), BF16)], [_rows(tm, ff)] * 3 + [_rows(tm, D_MODEL)])

    def resid(vals, ins, outs, i):
        outs[0][...] = ins[2][...] + 0.5 * vals[0]

    (y,) = _mm(
        "ffn_down", [act, wd, x], [_rows(TM, ff), _whole(wd.shape), _tok(D_MODEL)], [(0, 1, 0)], 1, None, NN,
        (T // TM, 1, 1), resid, [jax.ShapeDtypeStruct((T, D_MODEL), F32)], [_tok(D_MODEL)])
    return y, (x, h, sil, up_dsil, act)


def _ffn_bwd(dxo, gain, wg, wu, wd, saved):
    x, h, sil, up_dsil, act = saved
    T = x.shape[0]
    n_chips, _, ffs, _ = wd.shape
    wg, wu, wd = (w.reshape(-1, D_MODEL) for w in (wg, wu, wd))
    ff = wd.shape[0]
    tk = TM
    tm = TM

    def dswiglu(vals, ins, outs, i):
        da = 0.5 * vals[0]
        outs[0][...] = (da * ins[3][...].astype(F32)).astype(BF16)
        outs[1][...] = (da * ins[2][...].astype(F32)).astype(BF16)

    ash = jax.ShapeDtypeStruct((T, ff), BF16)
    dgate, dup = _mm(
        "ffn_dact", [dxo, wd, sil, up_dsil], [_rows(tm, D_MODEL), _whole(wd.shape), _rows(tm, ff), _rows(tm, ff)],
        [(0, 1, 0)], 1, None, NT, (T // tm, 1, 1), dswiglu, [ash, ash], [_rows(tm, ff)] * 2)

    def half(vals, ins, outs, i):
        outs[0][...] = (0.5 * vals[0]).astype(BF16)

    def cast(vals, ins, outs, i):
        outs[0][...] = vals[0].astype(BF16)

    tok_k = pl.BlockSpec((tk, D_MODEL), lambda i, j, k: (k, 0))
    hid_k = pl.BlockSpec((tk, ff), lambda i, j, k: (k, 0))
    wsh = jax.ShapeDtypeStruct((ff, D_MODEL), BF16)
    (dwd,) = _mm("ffn_dwd", [act, dxo], [hid_k, tok_k], [(0, 1, 0)], 1, (ff, D_MODEL), TN, (1, 1, T // tk), half,
                 [wsh], [_whole((ff, D_MODEL))])

    tx = TM // 2
    dx, dgain = _mm(
        "ffn_dx", [dgate, dup, wg, wu, x, gain, dxo],
        [_rows(tx, ff), _rows(tx, ff), _whole(wg.shape), _whole(wu.shape), _rows(tx, D_MODEL), _gain_spec(),
         _rows(tx, D_MODEL)],
        [(0, 2, 0), (1, 3, 0)], 1, None, NN, (T // tx, 1, 1), _rms_bwd_epilogue(4, 5, 6),
        [jax.ShapeDtypeStruct((T, D_MODEL), F32), jax.ShapeDtypeStruct((8, D_MODEL), F32)],
        [_rows(tx, D_MODEL), pl.BlockSpec((8, D_MODEL), lambda i, j, k: (0, 0))])

    dws = []
    for dact in (dgate, dup):
        dws += _mm("ffn_dwgu", [dact, h], [hid_k, tok_k], [(0, 1, 0)], 1, (ff, D_MODEL), TN, (1, 1, T // tk), cast,
                   [wsh], [_whole((ff, D_MODEL))])
    dwg, dwu, dwd = (w.reshape(n_chips, ffs, D_MODEL) for w in (dws[0], dws[1], dwd))
    return dx, dgain, dwg, dwu, dwd


def _joined_mixer_weights(wpd, wps, wo):
    n, _, r, c = wpd.shape
    wpd_n, wps_n = (w[:, 0].transpose(1, 0, 2).reshape(r, n * c) for w in (wpd, wps))
    return wpd_n, wps_n, wo.reshape(-1, wo.shape[3])


def _mixer_fwd(x, gain, W, l, tabs):
    T = x.shape[0]
    win, wpd, wps, wo = W["w_in"], W["w_proj_dil"], W["w_proj_sb"], W["w_out"]
    cin = win.shape[3]
    cp = wpd.shape[3]
    normed = _normed(0, 5)
    n_rope = 6 * D_ATT

    tm = TM // 2

    def roped(vals, ins, outs, i):
        for j, v in enumerate(vals):
            lo = j * cin
            k = min(max(n_rope - lo, 0), cin)
            if k:
                tab = [jnp.concatenate([ins[t][...]] * (k // 128), axis=1) for t in (2, 3, 4)]
                outs[0][:, lo:lo + k] = _rope_fwd(v[:, :k], *tab)
            if k < cin:
                outs[0][:, lo + k:lo + cin] = v[:, k:]
        outs[1][...] = normed(ins)

    proj, h = _mm(
        "mix_in", [x, win, *tabs, gain],
        [_rows(tm, D_MODEL), _wfull(D_MODEL, cin, l)] + [_rows(tm, 128)] * 3 + [_gain_spec()],
        [(normed, _pick(1, c), c) for c in range(N_CHIPS)], N_CHIPS, None, NN, (T // tm, 1, 1), roped,
        [jax.ShapeDtypeStruct((T, N_CHIPS * cin), F32), jax.ShapeDtypeStruct((T, D_MODEL), BF16)],
        [_rows(tm, N_CHIPS * cin), _rows(tm, D_MODEL)])

    os_, lses = [], []
    for g, (window, dil) in enumerate(DIL_GROUPS):
        o_g, lse_g = _dil_fwd(proj, g, dil)
        os_.append(o_g)
        lses.append(lse_g)
    o_dil, lse = _dil_merge(os_, lses)
    o_sb = _sb_fwd(proj)

    def gated(vals, ins, outs, i):
        pd, ps = vals
        outs[0][...] = (_sigmoid(ins[4][...]) * pd + _sigmoid(ins[5][...]) * ps).astype(BF16)
        outs[1][...] = pd.astype(BF16)
        outs[2][...] = ps.astype(BF16)

    wpd_n, wps_n, wo_n = _joined_mixer_weights(wpd, wps, wo)
    gd_spec = pl.BlockSpec((TM, D_MODEL), lambda i, j, k: (i, COL_GD // D_MODEL))
    gs_spec = pl.BlockSpec((TM, D_MODEL), lambda i, j, k: (i, COL_GS // D_MODEL))
    ush = jax.ShapeDtypeStruct((T, D_MODEL), BF16)
    u, pd, ps = _mm(
        "mix_gate", [o_dil, o_sb, wpd_n, wps_n, proj, proj],
        [_tok(D_ATT), _tok(D_ATT), _whole(wpd_n.shape), _whole(wps_n.shape), gd_spec, gs_spec],
        [(0, 2, 0), (1, 3, 1)], 2, None, NN, (T // TM, 1, 1), gated, [ush] * 3, [_tok(D_MODEL)] * 3)

    def resid(vals, ins, outs, i):
        outs[0][...] = ins[2][...] + vals[0]

    (y,) = _mm(
        "mix_out", [u, wo_n, x], [_tok(D_MODEL), _whole(wo_n.shape), _tok(D_MODEL)], [(0, 1, 0)], 1, None, NN,
        (T // TM, 1, 1), resid, [jax.ShapeDtypeStruct((T, D_MODEL), F32)], [_tok(D_MODEL)])
    return y, (x, h, proj, o_dil, lse, o_sb, u, pd, ps)


def _mixer_bwd(dxo, gain, W, l, tabs, saved):
    x, h, proj, o_dil, lse, o_sb, u, pd, ps = saved
    T = x.shape[0]
    win, wpd, wps, wo = W["w_in"], W["w_proj_dil"], W["w_proj_sb"], W["w_out"]
    cin = win.shape[3]
    cp = wpd.shape[3]
    tk = TM
    tm = TM
    row = pl.BlockSpec((tm, D_MODEL), lambda i, j, k: (i, 0))

    def dgated(vals, ins, outs, i):
        du = vals[0]
        sd = _sigmoid(ins[4][...])
        ss = _sigmoid(ins[5][...])
        outs[0][...] = (du * sd).astype(BF16)
        outs[1][...] = (du * ss).astype(BF16)
        outs[2][...] = (du * ins[2][...].astype(F32) * sd * (1.0 - sd)).astype(BF16)
        outs[3][...] = (du * ins[3][...].astype(F32) * ss * (1.0 - ss)).astype(BF16)

    wpd_n, wps_n, wo_n = _joined_mixer_weights(wpd, wps, wo)
    gd_spec = pl.BlockSpec((TM, D_MODEL), lambda i, j, k: (i, COL_GD // D_MODEL))
    gs_spec = pl.BlockSpec((TM, D_MODEL), lambda i, j, k: (i, COL_GS // D_MODEL))
    ush = jax.ShapeDtypeStruct((T, D_MODEL), BF16)
    dpd, dps, dgd, dgs = _mm(
        "mix_du", [dxo, wo_n, pd, ps, proj, proj],
        [_tok(D_MODEL), _whole(wo_n.shape), _tok(D_MODEL), _tok(D_MODEL), gd_spec, gs_spec],
        [(0, 1, 0)], 1, None, NT, (T // TM, 1, 1), dgated, [ush] * 4, [_tok(D_MODEL)] * 4)

    def one(vals, ins, outs, i):
        outs[0][...] = vals[0].astype(BF16)

    def two(vals, ins, outs, i):
        outs[0][...] = vals[0].astype(BF16)
        outs[1][...] = vals[1].astype(BF16)

    tok_k = pl.BlockSpec((tk, D_MODEL), lambda i, j, k: (k, 0))
    att_k = pl.BlockSpec((tk, D_ATT), lambda i, j, k: (k, 0))
    (dwo_n,) = _mm("mix_dwo", [u, dxo], [tok_k, tok_k], [(0, 1, 0)], 1, (D_MODEL, D_MODEL), TN, (1, 1, T // tk), one,
                   [jax.ShapeDtypeStruct((D_MODEL, D_MODEL), BF16)], [_whole((D_MODEL, D_MODEL))])

    def plain2(vals, ins, outs, i):
        outs[0][...] = vals[0]
        outs[1][...] = vals[1]

    ash = jax.ShapeDtypeStruct((T, D_ATT), F32)
    do_dil, do_sb = _mm(
        "mix_do", [dpd, dps, wpd_n, wps_n], [_tok(D_MODEL), _tok(D_MODEL), _whole(wpd_n.shape), _whole(wps_n.shape)],
        [(0, 2, 0), (1, 3, 1)], 2, None, NT, (T // TM, 1, 1), plain2, [ash, ash], [_tok(D_ATT)] * 2)

    psh = jax.ShapeDtypeStruct((D_ATT, D_MODEL), BF16)
    dwpd_n, dwps_n = _mm(
        "mix_dwp", [o_dil, o_sb, dpd, dps], [att_k, att_k, tok_k, tok_k], [(0, 2, 0), (1, 3, 1)], 2,
        (D_ATT, D_MODEL), TN, (1, 1, T // tk), two, [psh, psh], [_whole((D_ATT, D_MODEL))] * 2)
    dwpd, dwps = (w.reshape(D_ATT, N_CHIPS, cp).transpose(1, 0, 2) for w in (dwpd_n, dwps_n))
    dwo = dwo_n.reshape(N_CHIPS, cp, D_MODEL)

    dqs, dks, dvs = [], [], []
    for g, (window, dil) in enumerate(DIL_GROUPS):
        dq, dk, dv = _dil_bwd(proj, do_dil, o_dil, lse, g, dil)
        dqs.append(dq)
        dks.append(dk)
        dvs.append(dv)
    dq_s, dk_s, dv_s = _sb_bwd(proj, do_sb, o_sb)
    dproj = _assemble_dproj(dqs + dks, dvs + [dq_s], [dk_s, dv_s], [dgd, dgs], tabs)

    dx, dgain = _mm(
        "mix_dx", [dproj, win, x, gain, dxo],
        [pl.BlockSpec((tm, N_CHIPS * cin), lambda i, j, k: (i, 0)), _wfull(D_MODEL, cin, l), row, _gain_spec(), row],
        [(_cols(0, c, cin), _pick(1, c), 0) for c in range(N_CHIPS)], 1, None, NT, (T // tm, 1, 1),
        _rms_bwd_epilogue(2, 3, 4),
        [jax.ShapeDtypeStruct((T, D_MODEL), F32), jax.ShapeDtypeStruct((8, D_MODEL), F32)],
        [row, pl.BlockSpec((8, D_MODEL), lambda i, j, k: (0, 0))])

    (dwin,) = _mm(
        "mix_dwin", [h, dproj],
        [pl.BlockSpec((tk, D_MODEL), lambda i, j, k: (k, 0)), pl.BlockSpec((tk, cin), lambda i, j, k: (k, j))],
        [(0, 1, 0)], 1, (D_MODEL, cin), TN, (1, N_CHIPS, T // tk), one,
        [jax.ShapeDtypeStruct((N_CHIPS, D_MODEL, cin), BF16)],
        [pl.BlockSpec((None, D_MODEL, cin), lambda i, j, k: (j, 0, 0))])
    return dx, dgain, dwin, dwpd, dwps, dwo


def _local_step(x, target, norms, norm_final, weights_of, on_grads):
    T = x.shape[0]
    tabs = _rope_tables(T)
    saved, held = [], []
    for l in range(DEPTH):
        w1 = weights_of(l, 0, x)
        x, s1 = _ffn_fwd(x, norms["norm_ffn1"][l:l + 1], w1["ffn1_w_gate"], w1["ffn1_w_up"], w1["ffn1_w_down"])
        w2 = weights_of(l, 1, x)
        x, s2 = _mixer_fwd(x, norms["norm_mix"][l:l + 1], w2, 0, tabs)
        w3 = weights_of(l, 2, x)
        x, s3 = _ffn_fwd(x, norms["norm_ffn2"][l:l + 1], w3["ffn2_w_gate"], w3["ffn2_w_up"], w3["ffn2_w_down"])
        saved.append((s1, s2, s3))
        held.append((w1, w2, w3))
    dx, dg_final, loss = _final_loss(x, norm_final.reshape(1, D_MODEL), target)
    gains = [None] * DEPTH
    for l in reversed(range(DEPTH)):
        s1, s2, s3 = saved[l]
        w1, w2, w3 = held[l]
        dx, dg2, dwg2, dwu2, dwd2 = _ffn_bwd(dx, norms["norm_ffn2"][l:l + 1], w3["ffn2_w_gate"], w3["ffn2_w_up"],
                                             w3["ffn2_w_down"], s3)
        dx = on_grads(l, 2, dict(ffn2_w_gate=dwg2, ffn2_w_up=dwu2, ffn2_w_down=dwd2), dx)
        dx, dgm, dwin, dwpd, dwps, dwo = _mixer_bwd(dx, norms["norm_mix"][l:l + 1], w2, 0, tabs, s2)
        dx = on_grads(l, 1, dict(w_in=dwin, w_proj_dil=dwpd, w_proj_sb=dwps, w_out=dwo), dx)
        dx, dg1, dwg1, dwu1, dwd1 = _ffn_bwd(dx, norms["norm_ffn1"][l:l + 1], w1["ffn1_w_gate"], w1["ffn1_w_up"],
                                             w1["ffn1_w_down"], s1)
        dx = on_grads(l, 0, dict(ffn1_w_gate=dwg1, ffn1_w_up=dwu1, ffn1_w_down=dwd1), dx)
        gains[l] = dict(norm_ffn1=dg1, norm_mix=dgm, norm_ffn2=dg2)
    return loss, dx, gains, dg_final


def _place():
    x, y, c = lax.axis_index("x"), lax.axis_index("y"), lax.axis_index("c")
    chips = [(1 - x, y), (x, 1 - y), (1 - x, 1 - y)]
    return x, y, c, chips


def _half(c, r):
    return pl.ds(pl.multiple_of(c * (r // 2), 8), r // 2)


def _cast_into_slot(ws, ls, me_arr, after):
    n = len(ws)
    late = [] if after is None else [after]

    def body(me_ref, *refs):
        for a in range(n):
            refs[len(refs) - n + a][...] = refs[a][...].astype(BF16)

    def src(w, l):
        return pl.BlockSpec((None, w.shape[1] // 4, w.shape[2]), lambda i, me: (l, i, 0))

    def dst(w):
        return pl.BlockSpec((None, None, w.shape[1] // 4, w.shape[2]), lambda i, me: (me[0], 0, i, 0))

    return pl.pallas_call(
        body, name="cast_weights",
        grid_spec=pltpu.PrefetchScalarGridSpec(
            num_scalar_prefetch=1, grid=(4,),
            in_specs=[src(w, l) for w, l in zip(ws, ls)] + [pl.BlockSpec(memory_space=pl.ANY)] * len(late),
            out_specs=[dst(w) for w in ws]),
        out_shape=[jax.ShapeDtypeStruct((N_CHIPS, 1) + w.shape[1:], BF16) for w in ws], compiler_params=_params(),
    )(me_arr, *ws, *late)


HBM_SPEC = pl.BlockSpec(memory_space=pltpu.HBM)
SEM_SPEC = pl.BlockSpec(memory_space=pltpu.SEMAPHORE)
SPLIT_COPY = pltpu.CompilerParams(has_side_effects=pltpu.SideEffectType.DATAFLOW_SIDE_EFFECTING)


def _gather_piece(ref, chip_id, c):
    return ref.at[chip_id, 0, _half(c, ref.shape[2]), :]


def _gather_start(tag, bufs, direct):
    n = len(bufs)

    def body(*refs):
        out_refs = refs[n:2 * n]
        send_sems, recv_sems, token = refs[2 * n:]
        x, y, c, chips = _place()
        me = 2 * x + y
        for a in range(n):
            piece = _gather_piece(out_refs[a], me, c)
            for j, chip in enumerate(chips):
                for to in ((0, 1) if direct[a] else (c,)):
                    pltpu.make_async_remote_copy(
                        src_ref=piece, dst_ref=piece, send_sem=send_sems.at[6 * a + 2 * j + to],
                        recv_sem=recv_sems.at[6 * a + 2 * j + c], device_id=(*chip, to), device_id_type=MESH).start()
        token[...] = jnp.zeros_like(token)

    outs = pl.pallas_call(
        body, name=f"gather_start_{tag}", in_specs=[HBM_SPEC] * n,
        out_specs=[HBM_SPEC] * n + [SEM_SPEC, SEM_SPEC, pl.BlockSpec(memory_space=pltpu.VMEM)],
        out_shape=[pltpu.HBM(b.shape, b.dtype) for b in bufs] + [pltpu.SemaphoreType.DMA((6 * n,))] * 2
        + [jax.ShapeDtypeStruct((8, 128), F32)],
        input_output_aliases={a: a for a in range(n)}, compiler_params=SPLIT_COPY,
    )(*[pltpu.with_memory_space_constraint(b, pltpu.HBM) for b in bufs])
    return outs[:n], outs[n], outs[n + 1], outs[n + 2]


def _gather_wait(k, bufs, places, direct, send_sems, recv_sems, after):
    m = len(bufs)

    def body(*refs):
        in_refs = refs[:m]
        ssem, rsem = refs[m], refs[m + 1]
        x, y, c, chips = _place()
        me = 2 * x + y
        for t, a in enumerate(places):
            for j, chip in enumerate(chips):
                for core in ((0, 1) if direct else (c,)):
                    cp = pltpu.make_async_remote_copy(
                        src_ref=_gather_piece(in_refs[t], me, c),
                        dst_ref=_gather_piece(in_refs[t], 2 * chip[0] + chip[1], core),
                        send_sem=ssem.at[6 * a + 2 * j + core], recv_sem=rsem.at[6 * a + 2 * j + core],
                        device_id=(*chip, core), device_id_type=MESH)
                    cp.wait_send()
                    cp.wait_recv()

    return pl.pallas_call(
        body, name=f"gather_wait_{k}",
        in_specs=[HBM_SPEC] * m + [SEM_SPEC, SEM_SPEC, pl.BlockSpec(memory_space=pl.ANY)], out_specs=[HBM_SPEC] * m,
        out_shape=[pltpu.HBM(b.shape, b.dtype) for b in bufs], input_output_aliases={t: t for t in range(m)},
        compiler_params=SPLIT_COPY,
    )(*bufs, send_sems, recv_sems, after)


def _gather_relay(bufs):
    n = len(bufs)

    def body(*refs):
        out_refs = refs[n:2 * n]
        send_sems, recv_sems = refs[2 * n:]
        x, y, c, chips = _place()
        cps = []
        for a in range(n):
            for j, chip in enumerate(chips):
                piece = _gather_piece(out_refs[a], 2 * chip[0] + chip[1], c)
                cps.append(pltpu.make_async_remote_copy(
                    src_ref=piece, dst_ref=piece, send_sem=send_sems.at[a, j], recv_sem=recv_sems.at[a, j],
                    device_id=(x, y, 1 - c), device_id_type=MESH))
        for cp in cps:
            cp.start()
        for a in range(n):
            for j, chip in enumerate(chips):
                theirs = _gather_piece(out_refs[a], 2 * chip[0] + chip[1], 1 - c)
                pltpu.make_async_remote_copy(
                    src_ref=theirs, dst_ref=theirs, send_sem=send_sems.at[a, j], recv_sem=recv_sems.at[a, j],
                    device_id=(x, y, 1 - c), device_id_type=MESH).wait_recv()
        for cp in cps:
            cp.wait_send()

    any_spec = pl.BlockSpec(memory_space=pl.ANY)
    return pl.pallas_call(
        body, name="gather_relay", in_specs=[any_spec] * n, out_specs=[any_spec] * n,
        out_shape=[jax.ShapeDtypeStruct(b.shape, b.dtype) for b in bufs],
        input_output_aliases={a: a for a in range(n)},
        scratch_shapes=[pltpu.SemaphoreType.DMA((n, 3))] * 2,
    )(*bufs)


def _other_half(ref, c):
    return ref.at[:, _half(1 - c, ref.shape[1]), :]


def _all_of(ref, c):
    return ref


def _sibling_start(name, srcs, pick, land_shapes, thru):
    n = len(srcs)
    lands = [lax.empty(sh, s.dtype) for sh, s in zip(land_shapes, srcs)]
    kept = lands + ([] if thru is None else [thru])
    m = len(kept)

    def body(*refs):
        s_refs, land_refs = refs[:n], refs[n + m:n + m + n]
        send_sems, recv_sems, token = refs[n + 2 * m:]
        x, y, c, _ = _place()
        for a in range(n):
            pltpu.make_async_remote_copy(
                src_ref=pick(s_refs[a], c), dst_ref=land_refs[a], send_sem=send_sems.at[a],
                recv_sem=recv_sems.at[a], device_id=(x, y, 1 - c), device_id_type=MESH).start()
        token[...] = jnp.zeros_like(token)

    outs = pl.pallas_call(
        body, name=name, in_specs=[HBM_SPEC] * (n + m),
        out_specs=[HBM_SPEC] * m + [SEM_SPEC, SEM_SPEC, pl.BlockSpec(memory_space=pltpu.VMEM)],
        out_shape=[pltpu.HBM(v.shape, v.dtype) for v in kept] + [pltpu.SemaphoreType.DMA((n,))] * 2
        + [jax.ShapeDtypeStruct((8, 128), F32)],
        input_output_aliases={n + a: a for a in range(m)}, compiler_params=SPLIT_COPY,
    )(*[pltpu.with_memory_space_constraint(v, pltpu.HBM) for v in list(srcs) + kept])
    return (outs[:n], outs[m], outs[m + 1]), (outs[n] if thru is not None else None), outs[m + 2]


def _sibling_wait(name, srcs, pick, lands, send_sems, recv_sems, after):
    n = len(srcs)

    def body(*refs):
        s_refs, land_refs = refs[:n], refs[n:2 * n]
        ssem, rsem = refs[2 * n], refs[2 * n + 1]
        x, y, c, _ = _place()
        for a in range(n):
            cp = pltpu.make_async_remote_copy(
                src_ref=pick(s_refs[a], c), dst_ref=land_refs[a], send_sem=ssem.at[a], recv_sem=rsem.at[a],
                device_id=(x, y, 1 - c), device_id_type=MESH)
            cp.wait_send()
            cp.wait_recv()

    return pl.pallas_call(
        body, name=name, in_specs=[HBM_SPEC] * (2 * n) + [SEM_SPEC, SEM_SPEC, pl.BlockSpec(memory_space=pl.ANY)],
        out_specs=[HBM_SPEC] * n, out_shape=[pltpu.HBM(v.shape, v.dtype) for v in lands],
        input_output_aliases={n + a: a for a in range(n)}, compiler_params=SPLIT_COPY,
    )(*srcs, *lands, send_sems, recv_sems, after)


def _add_half(gs, gots, c_arr):
    n = len(gs)

    def body(c_ref, *refs):
        for a in range(n):
            refs[2 * n + a][...] = (refs[a][...].astype(F32) + refs[n + a][...].astype(F32)).astype(BF16)

    def own(g):
        return pl.BlockSpec((None, g.shape[1] // 2, g.shape[2]), lambda k, cr: (k, cr[0], 0))

    def half(g):
        return pl.BlockSpec((None, g.shape[1] // 2, g.shape[2]), lambda k, cr: (k, 0, 0))

    return pl.pallas_call(
        body, name="grad_add_half",
        grid_spec=pltpu.PrefetchScalarGridSpec(
            num_scalar_prefetch=1, grid=(N_CHIPS,),
            in_specs=[own(g) for g in gs] + [half(g) for g in gs], out_specs=[half(g) for g in gs]),
        out_shape=[jax.ShapeDtypeStruct(got.shape, BF16) for got in gots], compiler_params=_params(),
    )(c_arr, *gs, *gots)


def _scatter_start(k, ss, thru):
    n = len(ss)

    def body(*refs):
        s_refs, land_refs = refs[2 * n + 1:3 * n + 1], refs[3 * n + 1:4 * n + 1]
        send_sems, recv_sems = refs[4 * n + 2:]
        x, y, c, chips = _place()
        me = 2 * x + y
        for a in range(n):
            for j, chip in enumerate(chips):
                pltpu.make_async_remote_copy(
                    src_ref=s_refs[a].at[2 * chip[0] + chip[1]], dst_ref=land_refs[a].at[me],
                    send_sem=send_sems.at[3 * a + j], recv_sem=recv_sems.at[3 * a + j], device_id=(*chip, c),
                    device_id_type=MESH).start()

    lands = [lax.empty(s.shape, s.dtype) for s in ss]
    hbm = [pltpu.HBM(s.shape, s.dtype) for s in ss]
    outs = pl.pallas_call(
        body, name=f"grad_scatter_start_{k}", in_specs=[HBM_SPEC] * (2 * n + 1),
        out_specs=[HBM_SPEC] * (2 * n + 1) + [SEM_SPEC, SEM_SPEC],
        out_shape=hbm + hbm + [pltpu.HBM(thru.shape, thru.dtype)] + [pltpu.SemaphoreType.DMA((3 * n,))] * 2,
        input_output_aliases={a: a for a in range(2 * n + 1)}, compiler_params=SPLIT_COPY,
    )(*[pltpu.with_memory_space_constraint(v, pltpu.HBM) for v in list(ss) + lands + [thru]])
    return (outs[:n], outs[n:2 * n], outs[2 * n + 1], outs[2 * n + 2]), outs[2 * n]


def _scatter_wait(k, ss, lands, send_sems, recv_sems, after):
    n = len(ss)

    def body(*refs):
        s_refs, land_refs = refs[:n], refs[n:2 * n]
        ssem, rsem = refs[2 * n], refs[2 * n + 1]
        x, y, c, chips = _place()
        me = 2 * x + y
        for a in range(n):
            for j, chip in enumerate(chips):
                cid = 2 * chip[0] + chip[1]
                cp = pltpu.make_async_remote_copy(
                    src_ref=s_refs[a].at[cid], dst_ref=land_refs[a].at[cid], send_sem=ssem.at[3 * a + j],
                    recv_sem=rsem.at[3 * a + j], device_id=(*chip, c), device_id_type=MESH)
                cp.wait_send()
                cp.wait_recv()

    hbm = [pltpu.HBM(s.shape, s.dtype) for s in ss]
    outs = pl.pallas_call(
        body, name=f"grad_scatter_wait_{k}",
        in_specs=[HBM_SPEC] * (2 * n) + [SEM_SPEC, SEM_SPEC, pl.BlockSpec(memory_space=pl.ANY)],
        out_specs=[HBM_SPEC] * (2 * n), out_shape=hbm + hbm,
        input_output_aliases={a: a for a in range(2 * n)}, compiler_params=SPLIT_COPY,
    )(*ss, *lands, send_sems, recv_sems, after)
    return outs[:n], outs[n:]


def _sum_chips(lands, ss, me_arr):
    n = len(lands)

    def body(me_ref, *refs):
        for own in range(N_CHIPS):
            @pl.when(me_ref[0] == own)
            def _(own=own):
                for a in range(n):
                    acc = None
                    for k in range(N_CHIPS):
                        term = (refs[n + a][...] if k == own else refs[a][k]).astype(F32)
                        acc = term if acc is None else acc + term
                    refs[2 * n + a][...] = acc

    return pl.pallas_call(
        body, name="grad_sum_chips",
        grid_spec=pltpu.PrefetchScalarGridSpec(
            num_scalar_prefetch=1, grid=(1,),
            in_specs=[pl.BlockSpec(la.shape, lambda i, me: (0, 0, 0)) for la in lands]
            + [pl.BlockSpec((None,) + la.shape[1:], lambda i, me: (me[0], 0, 0)) for la in lands],
            out_specs=[pl.BlockSpec(la.shape[1:], lambda i, me: (0, 0)) for la in lands]),
        out_shape=[jax.ShapeDtypeStruct(la.shape[1:], F32) for la in lands], compiler_params=_params(),
    )(me_arr, *lands, *ss)


def _allreduce_rows(stats):
    def body(s_ref, o_ref, buf, send_sems, recv_sems):
        x, y, c, _ = _place()
        me = 4 * x + 2 * y + c
        buf[me] = s_ref[...]
        cps = []
        for k in range(1, 8):
            px = jnp.where(k & 4, 1 - x, x)
            py = jnp.where(k & 2, 1 - y, y)
            pc = jnp.where(k & 1, 1 - c, c)
            cps.append(pltpu.make_async_remote_copy(
                src_ref=s_ref, dst_ref=buf.at[me], send_sem=send_sems.at[k - 1], recv_sem=recv_sems.at[k - 1],
                device_id=(px, py, pc), device_id_type=MESH))
        for cp in cps:
            cp.start()
        for cp in cps:
            cp.wait()
        acc = buf[0]
        for d in range(1, 8):
            acc = acc + buf[d]
        o_ref[...] = acc

    vm = pl.BlockSpec(memory_space=pltpu.VMEM)
    return pl.pallas_call(
        body, name="allreduce_rows", in_specs=[vm], out_specs=vm,
        out_shape=jax.ShapeDtypeStruct(stats.shape, F32),
        scratch_shapes=[pltpu.VMEM((8,) + stats.shape, F32), pltpu.SemaphoreType.DMA((7,)),
                        pltpu.SemaphoreType.DMA((7,))],
    )(stats)


def _adamw_math(w, g, m, v):
    m = ADAM_B1 * m + (1.0 - ADAM_B1) * g
    v = ADAM_B2 * v + (1.0 - ADAM_B2) * (g * g)
    m_hat = m / (1.0 - ADAM_B1 ** ADAM_STEP)
    v_hat = v / (1.0 - ADAM_B2 ** ADAM_STEP)
    delta = -ADAM_LR * (m_hat / (jnp.sqrt(v_hat) + ADAM_EPS) + ADAM_WD * w)
    return delta, m, v


def _adamw(ws, ms, vs, mines, theirs, l, c_arr, earlier, after):
    n = len(ws)
    held = [t for e in earlier if e is not None for t in e]
    assert len(held) in (0, 4 * n)
    late = [] if after is None else [after]

    def body(c_ref, *refs):
        outs = refs[len(refs) - 4 * n:]
        for a in range(n):
            w_ref, m_ref, v_ref, a_ref, b_ref = refs[5 * a:5 * a + 5]
            g = jnp.where(pl.program_id(0) == c_ref[0], a_ref[...], b_ref[...])
            delta, mn, vn = _adamw_math(w_ref[...], g, m_ref[...], v_ref[...])
            outs[4 * a][...] = g
            outs[4 * a + 1][...] = delta
            outs[4 * a + 2][...] = mn
            outs[4 * a + 3][...] = vn

    def blk(w):
        tr = w.shape[1] // 4
        return pl.BlockSpec((None, tr, w.shape[2]), lambda hh, i, cr: (l, 2 * hh + i, 0))

    def half(w):
        return pl.BlockSpec((w.shape[1] // 4, w.shape[2]), lambda hh, i, cr: (i, 0))

    outs = pl.pallas_call(
        body, name="adamw",
        grid_spec=pltpu.PrefetchScalarGridSpec(
            num_scalar_prefetch=1, grid=(2, 2),
            in_specs=[sp for w in ws for sp in (blk(w), blk(w), blk(w), half(w), half(w))]
            + [pl.BlockSpec(memory_space=pl.ANY)] * (len(held) + len(late)),
            out_specs=[blk(w) for w in ws for _ in range(4)]),
        out_shape=[jax.ShapeDtypeStruct(w.shape, F32) for w in ws for _ in range(4)],
        input_output_aliases={1 + 5 * n + t: t for t in range(len(held))}, compiler_params=_params(),
    )(c_arr, *[t for grp in zip(ws, ms, vs, mines, theirs) for t in grp], *held, *late)
    return [outs[4 * a:4 * a + 4] for a in range(n)]


def _adamw_rows(w, m, v, g):
    def body(w_ref, m_ref, v_ref, g_ref, d_ref, mo_ref, vo_ref):
        delta, mn, vn = _adamw_math(w_ref[...], g_ref[...], m_ref[...], v_ref[...])
        d_ref[...] = delta
        mo_ref[...] = mn
        vo_ref[...] = vn

    vm = pl.BlockSpec(memory_space=pltpu.VMEM)
    sh = jax.ShapeDtypeStruct(w.shape, F32)
    return pl.pallas_call(body, name="adamw_rows", in_specs=[vm] * 4, out_specs=[vm] * 3, out_shape=[sh] * 3)(w, m, v, g)


SUBLAYERS = (("ffn1_w_gate", "ffn1_w_up", "ffn1_w_down"), ("w_in", "w_proj_dil", "w_proj_sb", "w_out"),
             ("ffn2_w_gate", "ffn2_w_up", "ffn2_w_down"))
TRANSPOSED = ("ffn1_w_gate", "ffn1_w_up", "ffn2_w_gate", "ffn2_w_up")


def _pick_row(blocks):
    row = lax.broadcasted_iota(jnp.int32, (8, D_MODEL), 0)
    out = jnp.zeros((8, D_MODEL), F32)
    for i, b in enumerate(blocks):
        out = out + jnp.where(row == i, b, 0.0)
    return out


def kernel(x, norm_ffn1, ffn1_w_gate, ffn1_w_up, ffn1_w_down, norm_mix, w_in, w_proj_dil, w_proj_sb, w_out, norm_ffn2, ffn2_w_gate, ffn2_w_up, ffn2_w_down, norm_final, loss_target, m_norm_ffn1, m_ffn1_w_gate, m_ffn1_w_up, m_ffn1_w_down, m_norm_mix, m_w_in, m_w_proj_dil, m_w_proj_sb, m_w_out, m_norm_ffn2, m_ffn2_w_gate, m_ffn2_w_up, m_ffn2_w_down, m_norm_final, v_norm_ffn1, v_ffn1_w_gate, v_ffn1_w_up, v_ffn1_w_down, v_norm_mix, v_w_in, v_w_proj_dil, v_w_proj_sb, v_w_out, v_norm_ffn2, v_ffn2_w_gate, v_ffn2_w_up, v_ffn2_w_down, v_norm_final):
    given = dict(locals())
    for n in TRANSPOSED:
        for k in ("", "m_", "v_"):
            given[k + n] = jnp.swapaxes(given[k + n], 1, 2)
    weights = {n: given[n] for n in WEIGHT_NAMES}
    norms = {n: given[n] for n in NORM_NAMES}

    c_arr = lax.axis_index("c").astype(jnp.int32).reshape(1)
    me_arr = (2 * lax.axis_index("x") + lax.axis_index("y")).astype(jnp.int32).reshape(1)
    order = [(l, s, n) for l in range(DEPTH) for s in range(len(SUBLAYERS)) for n in SUBLAYERS[s]]
    n_first = len(SUBLAYERS[0])
    sent, token = {}, None
    for tag, idxs in (("a", range(n_first)), ("b", range(n_first, len(order)))):
        cast = _cast_into_slot([weights[order[i][2]] for i in idxs], [order[i][0] for i in idxs], me_arr, token)
        bufs, send_sems, recv_sems, token = _gather_start(tag, cast, [order[i][0] > 0 for i in idxs])
        for p, i in enumerate(idxs):
            sent[i] = (bufs[p], p, send_sems, recv_sems)

    def weights_of(l, s, after):
        idxs = [i for i, (ll, ss, _) in enumerate(order) if (ll, ss) == (l, s)]
        got = _gather_wait(len(SUBLAYERS) * l + s, [sent[i][0] for i in idxs], [sent[i][1] for i in idxs], l > 0,
                           sent[idxs[0]][2], sent[idxs[0]][3], after)
        return {order[i][2]: g for i, g in zip(idxs, got if l > 0 else _gather_relay(got))}

    out = {}
    to_add, in_flight = [], []

    def add_and_scatter(after):
        l, s, names, gs, lands, ssem, rsem = to_add.pop(0)
        k = len(SUBLAYERS) * l + s
        got = _sibling_wait(f"grad_exchange_wait_{k}", gs, _other_half, lands, ssem, rsem, after)
        sent, after = _scatter_start(k, _add_half(gs, got, c_arr), after)
        in_flight.append((l, s, names) + sent)
        return after

    def on_grads(l, s, grads, after):
        names = list(grads)
        k = len(SUBLAYERS) * l + s
        gs = [grads[n] for n in names]
        sent, after, _ = _sibling_start(f"grad_exchange_start_{k}", gs, _other_half,
                                        [(g.shape[0], g.shape[1] // 2, g.shape[2]) for g in gs], after)
        if to_add:
            after = add_and_scatter(after)
        to_add.append((l, s, names, gs) + sent)
        return after

    loss_blk, grad_x, gains, dg_final = _local_step(x[0], loss_target[0], norms, norm_final, weights_of, on_grads)
    grad_x = add_and_scatter(grad_x)

    def update(l, names, mine, swap, after):
        theirs = _sibling_wait(f"grad_swap_wait_{l}_{names[0]}", mine, _all_of, *swap, grad_x if after is None else after)
        res = _adamw([weights[n] for n in names], [given["m_" + n] for n in names], [given["v_" + n] for n in names],
                     mine, theirs, l, c_arr, [out.get(n) for n in names], after)
        out.update(zip(names, res))

    waiting = None
    for l, s, names, sums, lands, ssem, rsem in in_flight:
        sums, lands = _scatter_wait(len(SUBLAYERS) * l + s, sums, lands, ssem, rsem, grad_x)
        mine = _sum_chips(lands, sums, me_arr)
        swap, _, token = _sibling_start(f"grad_swap_start_{l}_{names[0]}", mine, _all_of,
                                        [m.shape for m in mine], None)
        if waiting is not None:
            update(*waiting, token)
        waiting = (l, names, mine, swap)
    update(*waiting, None)
    out = {k + n: (jnp.swapaxes(v, 1, 2) if n in TRANSPOSED else v)
           for n, res in out.items() for k, v in zip(("grad_", "delta_", "new_m_", "new_v_"), res)}
    out["grad_x"] = grad_x[None]

    rows = [gains[l][n] for n in NORM_NAMES for l in range(DEPTH)] + [dg_final, loss_blk]
    total = _allreduce_rows(_pick_row(rows))
    out["loss"] = total[7, 0]
    wn = jnp.concatenate([given[n] for n in NORM_NAMES] + [norm_final[None], jnp.zeros((1, D_MODEL), F32)])
    mn_ = jnp.concatenate([given["m_" + n] for n in NORM_NAMES] + [m_norm_final[None], jnp.zeros((1, D_MODEL), F32)])
    vn_ = jnp.concatenate([given["v_" + n] for n in NORM_NAMES] + [v_norm_final[None], jnp.ones((1, D_MODEL), F32)])
    d_n, m_n, v_n = _adamw_rows(wn, mn_, vn_, total)
    for i, n in enumerate(NORM_NAMES):
        sl = slice(i * DEPTH, (i + 1) * DEPTH)
        out["grad_" + n], out["delta_" + n], out["new_m_" + n], out["new_v_" + n] = total[sl], d_n[sl], m_n[sl], v_n[sl]
    out["grad_norm_final"], out["delta_norm_final"] = total[6], d_n[6]
    out["new_m_norm_final"], out["new_v_norm_final"] = m_n[6], v_n[6]

    names = ["norm_ffn1", "ffn1_w_gate", "ffn1_w_up", "ffn1_w_down", "norm_mix", "w_in", "w_proj_dil", "w_proj_sb",
             "w_out", "norm_ffn2", "ffn2_w_gate", "ffn2_w_up", "ffn2_w_down", "norm_final"]
    return (out["loss"], out["grad_x"], *[out["grad_" + n] for n in names], *[out["delta_" + n] for n in names],
            *[out["new_m_" + n] for n in names], *[out["new_v_" + n] for n in names])
```

```python
import functools

import jax
import jax.numpy as jnp
from jax import lax
from jax.experimental import pallas as pl
from jax.experimental.pallas import tpu as pltpu

F32 = jnp.float32
BF16 = jnp.bfloat16

D_MODEL = 1024
DEPTH = 2
N_CHIPS = 4
HEAD_DIM = 64
ROPE_DIM = 16
ROPE_THETA = 500000.0
DIL_GROUPS = ((128, 1), (512, 4), (2048, 16))
SPAN = 128
Q_BLOCK = 128
RMS_EPS = 1e-6
D_ATT = 256
COL_QS = 2304
COL_GD = 3072
COL_GS = 4096
ADAM_LR, ADAM_B1, ADAM_B2, ADAM_EPS, ADAM_WD, ADAM_STEP = 0.001, 0.9, 0.999, 1e-08, 0.01, 10

VMEM_LIMIT = 52 * 1024 * 1024
TM = 512
NEG = -1e30

NN = (((1,), (0,)), ((), ()))
NT = (((1,), (1,)), ((), ()))
TN = (((0,), (0,)), ((), ()))
MESH = pl.DeviceIdType.MESH

WEIGHT_NAMES = ("ffn1_w_gate", "ffn1_w_up", "ffn1_w_down", "w_in", "w_proj_dil",
                "w_proj_sb", "w_out", "ffn2_w_gate", "ffn2_w_up", "ffn2_w_down")
NORM_NAMES = ("norm_ffn1", "norm_mix", "norm_ffn2")


def _params(**kw):
    return pltpu.CompilerParams(vmem_limit_bytes=VMEM_LIMIT, **kw)


def _sigmoid(x):
    return 0.5 * jnp.tanh(0.5 * x) + 0.5


def _mm_body(pairs, n_in, n_out, n_acc, dims, nk, epilogue, *refs):
    ins = refs[:n_in]
    outs = refs[n_in:n_in + n_out]
    accs = refs[n_in + n_out:]
    i = pl.program_id(0)
    k = pl.program_id(2)

    def operand(a):
        return (a(ins) if callable(a) else ins[a][...]).astype(BF16)

    def dot(ia, ib):
        return lax.dot_general(operand(ia), operand(ib), dims, preferred_element_type=F32)

    if nk == 1:
        parts = [None] * n_acc
        for ia, ib, ic in pairs:
            parts[ic] = dot(ia, ib) if parts[ic] is None else parts[ic] + dot(ia, ib)
        epilogue(parts, ins, outs, i)
        return

    @pl.when(k == 0)
    def _():
        for c in range(n_acc):
            accs[c][...] = jnp.zeros_like(accs[c])

    for ia, ib, ic in pairs:
        accs[ic][...] += dot(ia, ib)

    @pl.when(k == nk - 1)
    def _():
        epilogue([a[...] for a in accs], ins, outs, i)


def _mm(name, ins, in_specs, pairs, n_acc, acc_shape, dims, grid, epilogue, out_shapes, out_specs):
    nk = grid[2]
    scratch = [pltpu.VMEM(acc_shape, F32) for _ in range(n_acc)] if nk > 1 else []
    body = functools.partial(_mm_body, tuple(pairs), len(ins), len(out_shapes), n_acc, dims, nk, epilogue)
    return pl.pallas_call(
        body, name=name, grid=grid, in_specs=in_specs, out_specs=out_specs, out_shape=out_shapes,
        scratch_shapes=scratch,
        compiler_params=_params(dimension_semantics=("arbitrary", "arbitrary", "arbitrary")),
    )(*ins)


def _rms_bwd_epilogue(x_idx, g_idx, dxo_idx):
    def ep(vals, ins, outs, i):
        dh = vals[0]
        x = ins[x_idx][...]
        g = ins[g_idx][...]
        rstd = lax.rsqrt(jnp.mean(x * x, axis=-1, keepdims=True) + RMS_EPS)
        xhat = x * rstd
        dxhat = dh * g
        dx = rstd * (dxhat - xhat * jnp.mean(dxhat * xhat, axis=-1, keepdims=True))
        outs[0][...] = ins[dxo_idx][...] + dx
        dg = jnp.broadcast_to(jnp.sum(dh * xhat, axis=0, keepdims=True), outs[1].shape)

        @pl.when(i == 0)
        def _():
            outs[1][...] = dg

        @pl.when(i > 0)
        def _():
            outs[1][...] += dg
    return ep


def _normed(x_idx, g_idx):
    seen = {}

    def f(ins):
        if id(ins) not in seen:
            xv = ins[x_idx][...]
            h = xv * lax.rsqrt(jnp.mean(xv * xv, axis=-1, keepdims=True) + RMS_EPS)
            seen[id(ins)] = (ins, (h * ins[g_idx][...]).astype(BF16))
        return seen[id(ins)][1]
    return f


def _rope_tables(T):
    half = ROPE_DIM // 2
    lane = jnp.arange(128) % HEAD_DIM
    inv_freq = ROPE_THETA ** (-(2 * (lane % half)).astype(F32) / ROPE_DIM)
    ang = jnp.arange(T, dtype=F32)[:, None] * inv_freq[None, :]
    cos, sin = jnp.cos(ang), jnp.sin(ang)
    c = jnp.where(lane < ROPE_DIM, cos, 1.0)
    s1 = jnp.where(lane < half, -sin, 0.0)
    s2 = jnp.where((lane >= half) & (lane < ROPE_DIM), sin, 0.0)
    return c, s1, s2


def _rope_fwd(xv, c, s1, s2):
    w = xv.shape[1]
    half = ROPE_DIM // 2
    return xv * c + pltpu.roll(xv, w - half, 1) * s1 + pltpu.roll(xv, half, 1) * s2


def _rope_bwd(dy, c, s1, s2):
    w = dy.shape[1]
    half = ROPE_DIM // 2
    return dy * c + pltpu.roll(dy * s1, half, 1) + pltpu.roll(dy * s2, w - half, 1)


def _assemble_dproj(dqk, rest, turned, gates, tabs):
    T = gates[0].shape[0]
    n_qk, n_rest = len(dqk), len(rest) + len(turned)
    width = (n_qk + n_rest) * D_ATT + 2 * D_MODEL

    def body(*refs):
        ins, (c_ref, s1_ref, s2_ref), o_ref = refs[:n_qk + n_rest + 2], refs[-4:-1], refs[-1]
        c = jnp.concatenate([c_ref[...]] * 2, axis=1)
        s1 = jnp.concatenate([s1_ref[...]] * 2, axis=1)
        s2 = jnp.concatenate([s2_ref[...]] * 2, axis=1)
        for b in range(n_qk + n_rest):
            v = ins[b][...]
            if b < n_qk:
                v = _rope_bwd(v, c, s1, s2)
            if b >= n_qk + len(rest):
                v = v.T
            o_ref[:, b * D_ATT:(b + 1) * D_ATT] = v.astype(BF16)
        off = (n_qk + n_rest) * D_ATT
        o_ref[:, off:off + D_MODEL] = ins[-2][...]
        o_ref[:, off + D_MODEL:] = ins[-1][...]

    att = pl.BlockSpec((TM, D_ATT), lambda i: (i, 0))
    att_turned = pl.BlockSpec((D_ATT, TM), lambda i: (0, i))
    wide = pl.BlockSpec((TM, D_MODEL), lambda i: (i, 0))
    tab = pl.BlockSpec((TM, 128), lambda i: (i, 0))
    return pl.pallas_call(
        body, name="assemble_dproj", grid=(T // TM,),
        in_specs=[att] * (n_qk + len(rest)) + [att_turned] * len(turned) + [wide, wide, tab, tab, tab],
        out_specs=pl.BlockSpec((TM, width), lambda i: (i, 0)),
        out_shape=jax.ShapeDtypeStruct((T, width), BF16), compiler_params=_params(),
    )(*dqk, *rest, *turned, *gates, *tabs)


def _dil_merge(os_, lses):
    T = os_[0].shape[0]

    def body(o0, o1, o2, l0, l1, l2, o_ref, lse_ref):
        a, b, c = l0[...], l1[...], l2[...]
        m = jnp.maximum(jnp.maximum(a, b), c)
        ea, eb, ec = jnp.exp(a - m), jnp.exp(b - m), jnp.exp(c - m)
        den = ea + eb + ec
        o_ref[...] = (ea * o0[...] + eb * o1[...] + ec * o2[...]) / den
        lse_ref[...] = m + jnp.log(den)

    blk = pl.BlockSpec((TM, D_ATT), lambda i: (i, 0))
    sh = jax.ShapeDtypeStruct((T, D_ATT), F32)
    return pl.pallas_call(
        body, name="dil_merge", grid=(T // TM,), in_specs=[blk] * 6, out_specs=[blk, blk],
        out_shape=[sh, sh], compiler_params=_params(),
    )(*os_, *lses)


def _final_loss(x, gain, target):
    T = x.shape[0]

    def body(x_ref, g_ref, t_ref, dx_ref, dg_ref, loss_ref):
        xv = x_ref[...]
        g = g_ref[...]
        rstd = lax.rsqrt(jnp.mean(xv * xv, axis=-1, keepdims=True) + RMS_EPS)
        xhat = xv * rstd
        err = xhat * g - t_ref[...]
        loss = 0.5 * jnp.sum(jnp.mean(err * err, axis=-1, keepdims=True), axis=0, keepdims=True)
        dy = err * (1.0 / D_MODEL)
        dxhat = dy * g
        dx_ref[...] = rstd * (dxhat - xhat * jnp.mean(dxhat * xhat, axis=-1, keepdims=True))
        dg = jnp.broadcast_to(jnp.sum(dy * xhat, axis=0, keepdims=True), dg_ref.shape)
        ls = jnp.broadcast_to(loss, loss_ref.shape)

        @pl.when(pl.program_id(0) == 0)
        def _():
            dg_ref[...] = dg
            loss_ref[...] = ls

        @pl.when(pl.program_id(0) > 0)
        def _():
            dg_ref[...] += dg
            loss_ref[...] += ls

    blk = pl.BlockSpec((TM, D_MODEL), lambda i: (i, 0))
    row = pl.BlockSpec((1, D_MODEL), lambda i: (0, 0))
    acc = pl.BlockSpec((8, D_MODEL), lambda i: (0, 0))
    return pl.pallas_call(
        body, name="final_loss", grid=(T // TM,), in_specs=[blk, row, blk], out_specs=[blk, acc, acc],
        out_shape=[jax.ShapeDtypeStruct((T, D_MODEL), F32), jax.ShapeDtypeStruct((8, D_MODEL), F32),
                   jax.ShapeDtypeStruct((8, D_MODEL), F32)],
        compiler_params=_params(dimension_semantics=("arbitrary",)),
    )(x, gain, target)


def _pair_masks():
    lane = lax.broadcasted_iota(jnp.int32, (SPAN, 128), 1)
    return [lane < HEAD_DIM, lane >= HEAD_DIM]


def _stack_heads(x, masks):
    return jnp.concatenate([jnp.where(m, x, 0.0) for m in masks], axis=0)


def _unstack_heads(y, masks):
    rows = y.shape[0] // len(masks)
    out = jnp.where(masks[0], y[:rows], 0.0)
    for h in range(1, len(masks)):
        out = out + jnp.where(masks[h], y[rows * h:rows * (h + 1)], 0.0)
    return out


DIL_PAIR = 2


def _dil_rows(idx, d):
    u = idx // d
    r = idx - u * d
    own = pl.ds(u * (SPAN * d) + r, SPAN, stride=d) if d > 1 else pl.ds(pl.multiple_of(u * SPAN, SPAN), SPAN)
    up = jnp.maximum(u - 1, 0)
    prev = pl.ds(up * (SPAN * d) + r, SPAN, stride=d) if d > 1 else pl.ds(pl.multiple_of(up * SPAN, SPAN), SPAN)
    return u, own, prev


def _dil_valid(u):
    qi = lax.broadcasted_iota(jnp.int32, (2 * SPAN, 2 * SPAN), 0) & (SPAN - 1)
    kj = lax.broadcasted_iota(jnp.int32, (2 * SPAN, 2 * SPAN), 1)
    in_prev = (kj < SPAN) & (kj >= qi + jnp.where(u > 0, 0, SPAN))
    return in_prev | ((kj >= SPAN) & (kj - SPAN <= qi))


def _dil_keys(ref, own, prev):
    return jnp.concatenate([ref[prev, :], ref[own, :]], axis=0).astype(BF16)


def _dil_fwd(proj, g, d):
    T = proj.shape[0]
    n_iter = T // SPAN

    def body(q_ref, k_ref, v_ref, o_ref, lse_ref):
        masks = _pair_masks()

        def step(pair, carry):
            its = [_dil_rows(DIL_PAIR * pair + e, d) for e in range(DIL_PAIR)]
            qs = [_stack_heads(q_ref[own, :] * (HEAD_DIM ** -0.5), masks).astype(BF16) for _, own, _ in its]
            kks = [_dil_keys(k_ref, own, prev) for _, own, prev in its]
            vvs = [_dil_keys(v_ref, own, prev) for _, own, prev in its]
            ss = [jnp.where(_dil_valid(u), lax.dot_general(q, kk, NT, preferred_element_type=F32), NEG)
                  for (u, _, _), q, kk in zip(its, qs, kks)]
            ms = [jnp.max(s, axis=1, keepdims=True) for s in ss]
            ps = [jnp.exp(s - m) for s, m in zip(ss, ms)]
            dens = [jnp.sum(p, axis=1, keepdims=True) for p in ps]
            pvs = [lax.dot_general(p.astype(BF16), vv, NN, preferred_element_type=F32) / den
                   for p, vv, den in zip(ps, vvs, dens)]
            for (_, own, _), pv, m, den in zip(its, pvs, ms, dens):
                o_ref[own, :] = _unstack_heads(pv, masks)
                lse_ref[own, :] = _unstack_heads(jnp.broadcast_to(m + jnp.log(den), pv.shape), masks)
            return carry

        lax.fori_loop(0, n_iter // DIL_PAIR, step, 0)

    def col(b):
        return pl.BlockSpec((T, 128), lambda p: (0, b + p))

    sh = jax.ShapeDtypeStruct((T, D_ATT), F32)
    out = pl.BlockSpec((T, 128), lambda p: (0, p))
    return pl.pallas_call(
        body, name=f"dil_fwd_d{d}", grid=(2,),
        in_specs=[col(2 * g), col(6 + 2 * g), col(12 + 2 * g)], out_specs=[out, out], out_shape=[sh, sh],
        compiler_params=_params(dimension_semantics=("arbitrary",)),
    )(proj, proj, proj)


def _dil_bwd(proj, do, o_dil, lse, g, d):
    T = proj.shape[0]
    n_iter = T // SPAN

    def body(q_ref, k_ref, v_ref, do_ref, o_ref, lse_ref, dq_ref, dk_ref, dv_ref):
        masks = _pair_masks()
        head_lanes = jnp.concatenate(masks, axis=0)

        def step(pair, carry):
            its = [_dil_rows(DIL_PAIR * pair + e, d) for e in range(DIL_PAIR)]
            qs = [_stack_heads(q_ref[own, :] * (HEAD_DIM ** -0.5), masks).astype(BF16) for _, own, _ in its]
            kks = [_dil_keys(k_ref, own, prev) for _, own, prev in its]
            vvs = [_dil_keys(v_ref, own, prev) for _, own, prev in its]
            doms = [_stack_heads(do_ref[own, :], masks) for _, own, _ in its]
            dos = [dom.astype(BF16) for dom in doms]
            deltas = [jnp.sum(dom * jnp.concatenate([o_ref[own, :]] * 2, axis=0), axis=1, keepdims=True)
                      for dom, (_, own, _) in zip(doms, its)]
            lrows = [jnp.max(jnp.where(head_lanes, jnp.concatenate([lse_ref[own, :]] * 2, axis=0), NEG),
                             axis=1, keepdims=True) for _, own, _ in its]
            ss = [lax.dot_general(q, kk, NT, preferred_element_type=F32) for q, kk in zip(qs, kks)]
            dps = [lax.dot_general(do_b, vv, NT, preferred_element_type=F32) for do_b, vv in zip(dos, vvs)]
            ps = [jnp.where(_dil_valid(u), jnp.exp(s - lrow), 0.0) for (u, _, _), s, lrow in zip(its, ss, lrows)]
            dss = [(p * (dp - delta)).astype(BF16) for p, dp, delta in zip(ps, dps, deltas)]
            dqs = [lax.dot_general(ds, kk, NN, preferred_element_type=F32) for ds, kk in zip(dss, kks)]
            dkks = [lax.dot_general(ds, q, TN, preferred_element_type=F32) for ds, q in zip(dss, qs)]
            dvvs = [lax.dot_general(p.astype(BF16), do_b, TN, preferred_element_type=F32) for p, do_b in zip(ps, dos)]
            for (_, own, prev), dq, dkk, dvv in zip(its, dqs, dkks, dvvs):
                dq_ref[own, :] = _unstack_heads(dq, masks) * (HEAD_DIM ** -0.5)
                dk_ref[own, :] = dkk[SPAN:]
                dv_ref[own, :] = dvv[SPAN:]
                dk_ref[prev, :] = dk_ref[prev, :] + dkk[:SPAN]
                dv_ref[prev, :] = dv_ref[prev, :] + dvv[:SPAN]
            return carry

        lax.fori_loop(0, n_iter // DIL_PAIR, step, 0)

    def col(b):
        return pl.BlockSpec((T, 128), lambda p: (0, b + p))

    sh = jax.ShapeDtypeStruct((T, D_ATT), F32)
    return pl.pallas_call(
        body, name=f"dil_bwd_d{d}", grid=(2,),
        in_specs=[col(2 * g), col(6 + 2 * g), col(12 + 2 * g), col(0), col(0), col(0)],
        out_specs=[col(0), col(0), col(0)], out_shape=[sh, sh, sh],
        compiler_params=_params(dimension_semantics=("arbitrary",)),
    )(proj, proj, proj, do, o_dil, lse)


SB_KT = 512
LOG2_E = 1.4426950408889634


def _sb_tri(strict):
    a = lax.broadcasted_iota(jnp.int32, (Q_BLOCK, Q_BLOCK), 0)
    b = lax.broadcasted_iota(jnp.int32, (Q_BLOCK, Q_BLOCK), 1)
    return jnp.where((a > b) if strict else (a >= b), 1.0, 0.0).astype(BF16)


def _split_stack(x):
    nb = x.shape[1] // Q_BLOCK
    blocks = [x[:, Q_BLOCK * b:Q_BLOCK * (b + 1)] for b in range(nb)]
    hi = [b.astype(BF16) for b in blocks]
    lo = [(b - h.astype(F32)).astype(BF16) for b, h in zip(blocks, hi)]
    return blocks, jnp.concatenate(hi + lo, axis=0)


def _suffix_from(y, blocks, c):
    r = blocks[0].shape[0]
    nb = len(blocks)
    outs = [None] * nb
    run = c
    for b in reversed(range(nb)):
        outs[b] = run + y[r * b:r * (b + 1)] + y[r * (nb + b):r * (nb + b + 1)]
        run = run + jnp.sum(blocks[b], axis=1, keepdims=True)
    return jnp.concatenate(outs, axis=1), run


SB_HEADS = D_ATT // HEAD_DIM
SB_FWD_CHAINS = 2
SB_BWD_CHAINS = 1


SB_PLACES = SB_KT // Q_BLOCK


def _sb_past(place, rows):
    row = lax.broadcasted_iota(jnp.int32, (rows, Q_BLOCK * (place + 1)), 0) & (Q_BLOCK - 1)
    col = lax.broadcasted_iota(jnp.int32, (rows, Q_BLOCK * (place + 1)), 1)
    return col < row + place * Q_BLOCK


def _sb_head_masks(chains):
    lane = lax.broadcasted_iota(jnp.int32, (Q_BLOCK, D_ATT), 1)
    masks = [(lane >= HEAD_DIM * h) & (lane < HEAD_DIM * (h + 1)) for h in range(SB_HEADS)]
    per = SB_HEADS // chains
    return [masks[per * g:per * (g + 1)] for g in range(chains)]


def _sb_rows(t, width=SB_KT):
    return pl.ds(pl.multiple_of(t * SB_KT, SB_KT), width)


def _sb_widen(x):
    if x.shape[1] == SB_KT:
        return x
    return jnp.concatenate([x, jnp.zeros((x.shape[0], SB_KT - x.shape[1]), x.dtype)], axis=1)


def _sb_log_terms(z, past):
    lsz = jnp.minimum(z, 0.0) - jnp.log(1.0 + jnp.exp2(jnp.abs(z) * -LOG2_E))
    lk = lsz - z
    return lsz, (lk if past is None else jnp.where(past, lk, 0.0))


def _sb_weights(lsz, after, past):
    w = jnp.exp(lsz + after)
    return w if past is None else jnp.where(past, w, 0.0)


def _sb_fwd(proj):
    T = proj.shape[0]
    rows = SB_HEADS // SB_FWD_CHAINS * Q_BLOCK

    def at_place(place, n_tiles, q_ref, k_ref, v_ref, o_ref, z_buf, w_buf):
        masks = _sb_head_masks(SB_FWD_CHAINS)
        tri = _sb_tri(True)
        q = q_ref[...] * (HEAD_DIM ** -0.5)
        qs = [_stack_heads(q, m).astype(BF16) for m in masks]

        def scores(t, width=SB_KT):
            kb = k_ref[_sb_rows(t, width), :].astype(BF16)
            return [lax.dot_general(g, kb, NT, preferred_element_type=F32) for g in qs]

        def weights(zs, cs, past, between=lambda: None):
            logs = [_sb_log_terms(z, past) for z in zs]
            splits = [_split_stack(lk) for _, lk in logs]
            ys = [lax.dot_general(x, tri, NN, preferred_element_type=F32) for _, x in splits]
            between()
            sums = [_suffix_from(y, blocks, c) for y, (blocks, _), c in zip(ys, splits, cs)]
            ws = [_sb_weights(lsz, after, past).astype(BF16) for (lsz, _), (after, _) in zip(logs, sums)]
            return ws, [c for _, c in sums]

        def values(acc, slot, t):
            vb = v_ref[_sb_rows(t), :].astype(BF16)
            for g, m in enumerate(masks):
                acc = acc + _unstack_heads(lax.dot_general(w_buf[slot, g], vb, NN, preferred_element_type=F32), m)
            return acc

        def keep(buf, slot, xs):
            for g, x in enumerate(xs):
                buf[slot, g] = x

        zs = scores(n_tiles - 1, Q_BLOCK * (place + 1))
        keep(z_buf, 0, scores(jnp.maximum(n_tiles - 2, 0)))
        ws, cs = weights(zs, [jnp.zeros((rows, 1), F32)] * SB_FWD_CHAINS, _sb_past(place, rows))
        keep(w_buf, 0, [_sb_widen(w) for w in ws])

        def step(tt, carry):
            t = n_tiles - 2 - tt
            cur = tt & 1
            acc = values(carry[0], cur, t + 1)
            ws, cs = weights([z_buf[cur, g] for g in range(SB_FWD_CHAINS)], carry[1:], None,
                             lambda: keep(z_buf, 1 - cur, scores(jnp.maximum(t - 1, 0))))
            keep(w_buf, 1 - cur, ws)
            return (acc, *cs)

        carry = lax.fori_loop(0, n_tiles - 1, step, (jnp.zeros((Q_BLOCK, D_ATT), F32), *cs))
        o_ref[...] = values(carry[0], (n_tiles - 1) & 1, 0)

    def body(*refs):
        n_tiles = pl.program_id(1) + 1
        for place in range(SB_PLACES):
            pl.when(pl.program_id(0) == place)(functools.partial(at_place, place, n_tiles, *refs))

    cb = COL_QS // D_ATT
    return pl.pallas_call(
        body, name="sb_fwd", grid=(SB_PLACES, T // SB_KT),
        in_specs=[pl.BlockSpec((Q_BLOCK, D_ATT), lambda p, j: (SB_PLACES * j + p, cb)),
                  pl.BlockSpec((T, D_ATT), lambda p, j: (0, cb + 1)),
                  pl.BlockSpec((T, D_ATT), lambda p, j: (0, cb + 2))],
        out_specs=pl.BlockSpec((Q_BLOCK, D_ATT), lambda p, j: (SB_PLACES * j + p, 0)),
        out_shape=jax.ShapeDtypeStruct((T, D_ATT), F32),
        scratch_shapes=[pltpu.VMEM((2, SB_FWD_CHAINS, rows, SB_KT), F32),
                        pltpu.VMEM((2, SB_FWD_CHAINS, rows, SB_KT), BF16)],
        compiler_params=_params(dimension_semantics=("arbitrary", "arbitrary")),
    )(proj, proj, proj)


def _sb_bwd(proj, do, o):
    T = proj.shape[0]
    n_rows = SB_HEADS // SB_BWD_CHAINS * Q_BLOCK

    def at_place(place, n_tiles, q_ref, k_ref, v_ref, do_ref, o_ref, dq_ref, dk_ref, dv_ref,
                 z_buf, gv_buf, dz_buf, w_buf):
        masks = _sb_head_masks(SB_BWD_CHAINS)
        tri = _sb_tri(True)
        tri_incl = _sb_tri(False)

        q = q_ref[...] * (HEAD_DIM ** -0.5)
        qs = [_stack_heads(q, m).astype(BF16) for m in masks]
        dos = [_stack_heads(do_ref[...], m).astype(BF16) for m in masks]
        qts = [_stack_heads(q, m).T.astype(BF16) for m in masks]
        dots = [_stack_heads(do_ref[...], m).T.astype(BF16) for m in masks]
        o_rep = jnp.concatenate([o_ref[...]] * (SB_HEADS // SB_BWD_CHAINS), axis=0)
        deltas = [jnp.sum(d.astype(F32) * o_rep, axis=1, keepdims=True) for d in dos]

        def scores(t, width=SB_KT):
            kb = k_ref[_sb_rows(t, width), :].astype(BF16)
            return [lax.dot_general(g, kb, NT, preferred_element_type=F32) for g in qs]

        def value_grads(t, width=SB_KT):
            vb = v_ref[_sb_rows(t, width), :].astype(BF16)
            return [lax.dot_general(d, vb, NT, preferred_element_type=F32) for d in dos]

        def keep(buf, slot, xs):
            for g, x in enumerate(xs):
                buf[slot, g] = x

        def kept(buf, slot):
            return [buf[slot, g] for g in range(SB_BWD_CHAINS)]

        def score_grads(zs, gvs, cs, ces, past, after_first=lambda: None, after_second=lambda: None):
            logs = [_sb_log_terms(z, past) for z in zs]
            splits = [_split_stack(lk) for _, lk in logs]
            ys = [lax.dot_general(x, tri, NN, preferred_element_type=F32) for _, x in splits]
            after_first()
            sums = [_suffix_from(y, blocks, c) for y, (blocks, _), c in zip(ys, splits, cs)]
            wbs = [_sb_weights(lsz, after, past).astype(BF16) for (lsz, _), (after, _) in zip(logs, sums)]
            es = [wb.astype(F32) * gv for wb, gv in zip(wbs, gvs())]
            esplits = [_split_stack(e) for e in es]
            eys = [lax.dot_general(x, tri_incl, NN, preferred_element_type=F32) for _, x in esplits]
            after_second()
            esums = [_suffix_from(y, blocks, ce) for y, (blocks, _), ce in zip(eys, esplits, ces)]
            dzbs = []
            for e, (lsz, _), (suf, _), delta in zip(es, logs, esums, deltas):
                dz = e - jnp.exp(lsz) * (e + (delta - suf))
                dzbs.append((dz if past is None else jnp.where(past, dz, 0.0)).astype(BF16))
            return dzbs, wbs, [c for _, c in sums], [c for _, c in esums]

        def outputs(dq, slot, t):
            rows = _sb_rows(t)
            kb = k_ref[rows, :].astype(BF16)
            dk_t = dv_t = None
            for m, dzb, wb, g, d in zip(masks, kept(dz_buf, slot), kept(w_buf, slot), qts, dots):
                dq = dq + _unstack_heads(lax.dot_general(dzb, kb, NN, preferred_element_type=F32), m)
                a = lax.dot_general(g, dzb, NN, preferred_element_type=F32)
                b = lax.dot_general(d, wb, NN, preferred_element_type=F32)
                dk_t = a if dk_t is None else dk_t + a
                dv_t = b if dv_t is None else dv_t + b
            dk_ref[:, rows] = dk_ref[:, rows] + dk_t
            dv_ref[:, rows] = dv_ref[:, rows] + dv_t
            return dq

        zcol = [jnp.zeros((n_rows, 1), F32)] * SB_BWD_CHAINS
        ahead = jnp.maximum(n_tiles - 2, 0)
        width = Q_BLOCK * (place + 1)
        zs, gvs = scores(n_tiles - 1, width), value_grads(n_tiles - 1, width)
        keep(z_buf, 0, scores(ahead))
        keep(gv_buf, 0, value_grads(ahead))
        dzbs, wbs, cs, ces = score_grads(zs, lambda: gvs, zcol, zcol, _sb_past(place, n_rows))
        keep(dz_buf, 0, [_sb_widen(x) for x in dzbs])
        keep(w_buf, 0, [_sb_widen(x) for x in wbs])

        def step(tt, carry):
            t = n_tiles - 2 - tt
            cur = tt & 1
            ahead = jnp.maximum(t - 1, 0)
            dq = outputs(carry[0], cur, t + 1)
            dzbs, wbs, cs, ces = score_grads(
                kept(z_buf, cur), lambda: kept(gv_buf, cur),
                carry[1:1 + SB_BWD_CHAINS], carry[1 + SB_BWD_CHAINS:], None,
                lambda: keep(z_buf, 1 - cur, scores(ahead)),
                lambda: keep(gv_buf, 1 - cur, value_grads(ahead)))
            keep(dz_buf, 1 - cur, dzbs)
            keep(w_buf, 1 - cur, wbs)
            return (dq, *cs, *ces)

        carry = lax.fori_loop(0, n_tiles - 1, step, (jnp.zeros((Q_BLOCK, D_ATT), F32), *cs, *ces))
        dq_ref[...] = outputs(carry[0], (n_tiles - 1) & 1, 0) * (HEAD_DIM ** -0.5)

    def body(*refs):
        n_tiles = pl.program_id(1) + 1
        dk_ref, dv_ref = refs[6:8]

        @pl.when((pl.program_id(0) == 0) & (n_tiles == 1))
        def _():
            dk_ref[...] = jnp.zeros_like(dk_ref)
            dv_ref[...] = jnp.zeros_like(dv_ref)

        for place in range(SB_PLACES):
            pl.when(pl.program_id(0) == place)(functools.partial(at_place, place, n_tiles, *refs))

    cb = COL_QS // D_ATT
    blk = pl.BlockSpec((Q_BLOCK, D_ATT), lambda p, j: (SB_PLACES * j + p, 0))
    turned = pl.BlockSpec((D_ATT, T), lambda p, j: (0, 0))
    sh = jax.ShapeDtypeStruct((T, D_ATT), F32)
    sh_turned = jax.ShapeDtypeStruct((D_ATT, T), F32)
    kept_f32 = pltpu.VMEM((2, SB_BWD_CHAINS, n_rows, SB_KT), F32)
    kept_bf16 = pltpu.VMEM((2, SB_BWD_CHAINS, n_rows, SB_KT), BF16)
    return pl.pallas_call(
        body, name="sb_bwd", grid=(SB_PLACES, T // SB_KT),
        in_specs=[pl.BlockSpec((Q_BLOCK, D_ATT), lambda p, j: (SB_PLACES * j + p, cb)),
                  pl.BlockSpec((T, D_ATT), lambda p, j: (0, cb + 1)),
                  pl.BlockSpec((T, D_ATT), lambda p, j: (0, cb + 2)), blk, blk],
        out_specs=[blk, turned, turned], out_shape=[sh, sh_turned, sh_turned],
        scratch_shapes=[kept_f32, kept_f32, kept_bf16, kept_bf16],
        compiler_params=_params(dimension_semantics=("arbitrary", "arbitrary")),
    )(proj, proj, proj, do, o)


def _tok(c, by=None):
    if by is None:
        return pl.BlockSpec((TM, c), lambda i, j, k: (i, 0))
    if by == 1:
        return pl.BlockSpec((TM, c), lambda i, j, k: (i, j))
    return pl.BlockSpec((TM, c), lambda i, j, k: (i, k))


def _gain_spec():
    return pl.BlockSpec((1, D_MODEL), lambda i, j, k: (0, 0))


def _wfull(r, c, l):
    return pl.BlockSpec((N_CHIPS, None, r, c), lambda i, j, k: (0, l, 0, 0), pipeline_mode=pl.Buffered(1))


def _pick(idx, c):
    return lambda ins: ins[idx][c]


def _cols(idx, c, w):
    return lambda ins: ins[idx][:, c * w:(c + 1) * w]


def _rows(rows, width):
    return pl.BlockSpec((rows, width), lambda i, j, k: (i, 0))


def _whole(shape):
    return pl.BlockSpec(shape, lambda i, j, k: (0, 0), pipeline_mode=pl.Buffered(1))


def _ffn_fwd(x, gain, wg, wu, wd):
    T = x.shape[0]
    wg, wu, wd = (w.reshape(-1, D_MODEL) for w in (wg, wu, wd))
    ff = wd.shape[0]
    tm = TM // 2
    normed = _normed(0, 3)

    def swiglu(vals, ins, outs, i):
        gt, up = vals
        s = _sigmoid(gt)
        sil = gt * s
        outs[0][...] = sil.astype(BF16)
        outs[1][...] = (up * (s * (1.0 + gt * (1.0 - s)))).astype(BF16)
        outs[2][...] = (sil * up).astype(BF16)
        outs[3][...] = normed(ins)

    ash = jax.ShapeDtypeStruct((T, ff), BF16)
    sil, up_dsil, act, h = _mm(
        "ffn_up", [x, wg, wu, gain], [_rows(tm, D_MODEL), _whole(wg.shape), _whole(wu.shape), _gain_spec()],
        [(normed, 1, 0), (normed, 2, 1)], 2, None, NT, (T // tm, 1, 1), swiglu,
        [ash] * 3 + [jax.ShapeDtypeStruct((T, D_MODEL), BF16)], [_rows(tm, ff)] * 3 + [_rows(tm, D_MODEL)])

    def resid(vals, ins, outs, i):
        outs[0][...] = ins[2][...] + 0.5 * vals[0]

    (y,) = _mm(
        "ffn_down", [act, wd, x], [_rows(TM, ff), _whole(wd.shape), _tok(D_MODEL)], [(0, 1, 0)], 1, None, NN,
        (T // TM, 1, 1), resid, [jax.ShapeDtypeStruct((T, D_MODEL), F32)], [_tok(D_MODEL)])
    return y, (x, h, sil, up_dsil, act)


def _ffn_bwd(dxo, gain, wg, wu, wd, saved):
    x, h, sil, up_dsil, act = saved
    T = x.shape[0]
    n_chips, _, ffs, _ = wd.shape
    wg, wu, wd = (w.reshape(-1, D_MODEL) for w in (wg, wu, wd))
    ff = wd.shape[0]
    tk = TM
    tm = TM

    def dswiglu(vals, ins, outs, i):
        da = 0.5 * vals[0]
        outs[0][...] = (da * ins[3][...].astype(F32)).astype(BF16)
        outs[1][...] = (da * ins[2][...].astype(F32)).astype(BF16)

    ash = jax.ShapeDtypeStruct((T, ff), BF16)
    dgate, dup = _mm(
        "ffn_dact", [dxo, wd, sil, up_dsil], [_rows(tm, D_MODEL), _whole(wd.shape), _rows(tm, ff), _rows(tm, ff)],
        [(0, 1, 0)], 1, None, NT, (T // tm, 1, 1), dswiglu, [ash, ash], [_rows(tm, ff)] * 2)

    def half(vals, ins, outs, i):
        outs[0][...] = (0.5 * vals[0]).astype(BF16)

    def cast(vals, ins, outs, i):
        outs[0][...] = vals[0].astype(BF16)

    tok_k = pl.BlockSpec((tk, D_MODEL), lambda i, j, k: (k, 0))
    hid_k = pl.BlockSpec((tk, ff), lambda i, j, k: (k, 0))
    wsh = jax.ShapeDtypeStruct((ff, D_MODEL), BF16)
    (dwd,) = _mm("ffn_dwd", [act, dxo], [hid_k, tok_k], [(0, 1, 0)], 1, (ff, D_MODEL), TN, (1, 1, T // tk), half,
                 [wsh], [_whole((ff, D_MODEL))])

    tx = TM // 2
    dx, dgain = _mm(
        "ffn_dx", [dgate, dup, wg, wu, x, gain, dxo],
        [_rows(tx, ff), _rows(tx, ff), _whole(wg.shape), _whole(wu.shape), _rows(tx, D_MODEL), _gain_spec(),
         _rows(tx, D_MODEL)],
        [(0, 2, 0), (1, 3, 0)], 1, None, NN, (T // tx, 1, 1), _rms_bwd_epilogue(4, 5, 6),
        [jax.ShapeDtypeStruct((T, D_MODEL), F32), jax.ShapeDtypeStruct((8, D_MODEL), F32)],
        [_rows(tx, D_MODEL), pl.BlockSpec((8, D_MODEL), lambda i, j, k: (0, 0))])

    dws = []
    for dact in (dgate, dup):
        dws += _mm("ffn_dwgu", [dact, h], [hid_k, tok_k], [(0, 1, 0)], 1, (ff, D_MODEL), TN, (1, 1, T // tk), cast,
                   [wsh], [_whole((ff, D_MODEL))])
    dwg, dwu, dwd = (w.reshape(n_chips, ffs, D_MODEL) for w in (dws[0], dws[1], dwd))
    return dx, dgain, dwg, dwu, dwd


def _joined_mixer_weights(wpd, wps, wo):
    n, _, r, c = wpd.shape
    wpd_n, wps_n = (w[:, 0].transpose(1, 0, 2).reshape(r, n * c) for w in (wpd, wps))
    return wpd_n, wps_n, wo.reshape(-1, wo.shape[3])


def _mixer_fwd(x, gain, W, l, tabs):
    T = x.shape[0]
    win, wpd, wps, wo = W["w_in"], W["w_proj_dil"], W["w_proj_sb"], W["w_out"]
    cin = win.shape[3]
    cp = wpd.shape[3]
    normed = _normed(0, 5)
    n_rope = 6 * D_ATT

    tm = TM // 2

    def roped(vals, ins, outs, i):
        for j, v in enumerate(vals):
            lo = j * cin
            k = min(max(n_rope - lo, 0), cin)
            if k:
                tab = [jnp.concatenate([ins[t][...]] * (k // 128), axis=1) for t in (2, 3, 4)]
                outs[0][:, lo:lo + k] = _rope_fwd(v[:, :k], *tab)
            if k < cin:
                outs[0][:, lo + k:lo + cin] = v[:, k:]
        outs[1][...] = normed(ins)

    proj, h = _mm(
        "mix_in", [x, win, *tabs, gain],
        [_rows(tm, D_MODEL), _wfull(D_MODEL, cin, l)] + [_rows(tm, 128)] * 3 + [_gain_spec()],
        [(normed, _pick(1, c), c) for c in range(N_CHIPS)], N_CHIPS, None, NN, (T // tm, 1, 1), roped,
        [jax.ShapeDtypeStruct((T, N_CHIPS * cin), F32), jax.ShapeDtypeStruct((T, D_MODEL), BF16)],
        [_rows(tm, N_CHIPS * cin), _rows(tm, D_MODEL)])

    os_, lses = [], []
    for g, (window, dil) in enumerate(DIL_GROUPS):
        o_g, lse_g = _dil_fwd(proj, g, dil)
        os_.append(o_g)
        lses.append(lse_g)
    o_dil, lse = _dil_merge(os_, lses)
    o_sb = _sb_fwd(proj)

    def gated(vals, ins, outs, i):
        pd, ps = vals
        outs[0][...] = (_sigmoid(ins[4][...]) * pd + _sigmoid(ins[5][...]) * ps).astype(BF16)
        outs[1][...] = pd.astype(BF16)
        outs[2][...] = ps.astype(BF16)

    wpd_n, wps_n, wo_n = _joined_mixer_weights(wpd, wps, wo)
    gd_spec = pl.BlockSpec((TM, D_MODEL), lambda i, j, k: (i, COL_GD // D_MODEL))
    gs_spec = pl.BlockSpec((TM, D_MODEL), lambda i, j, k: (i, COL_GS // D_MODEL))
    ush = jax.ShapeDtypeStruct((T, D_MODEL), BF16)
    u, pd, ps = _mm(
        "mix_gate", [o_dil, o_sb, wpd_n, wps_n, proj, proj],
        [_tok(D_ATT), _tok(D_ATT), _whole(wpd_n.shape), _whole(wps_n.shape), gd_spec, gs_spec],
        [(0, 2, 0), (1, 3, 1)], 2, None, NN, (T // TM, 1, 1), gated, [ush] * 3, [_tok(D_MODEL)] * 3)

    def resid(vals, ins, outs, i):
        outs[0][...] = ins[2][...] + vals[0]

    (y,) = _mm(
        "mix_out", [u, wo_n, x], [_tok(D_MODEL), _whole(wo_n.shape), _tok(D_MODEL)], [(0, 1, 0)], 1, None, NN,
        (T // TM, 1, 1), resid, [jax.ShapeDtypeStruct((T, D_MODEL), F32)], [_tok(D_MODEL)])
    return y, (x, h, proj, o_dil, lse, o_sb, u, pd, ps)


def _mixer_bwd(dxo, gain, W, l, tabs, saved):
    x, h, proj, o_dil, lse, o_sb, u, pd, ps = saved
    T = x.shape[0]
    win, wpd, wps, wo = W["w_in"], W["w_proj_dil"], W["w_proj_sb"], W["w_out"]
    cin = win.shape[3]
    cp = wpd.shape[3]
    tk = TM
    tm = TM
    row = pl.BlockSpec((tm, D_MODEL), lambda i, j, k: (i, 0))

    def dgated(vals, ins, outs, i):
        du = vals[0]
        sd = _sigmoid(ins[4][...])
        ss = _sigmoid(ins[5][...])
        outs[0][...] = (du * sd).astype(BF16)
        outs[1][...] = (du * ss).astype(BF16)
        outs[2][...] = (du * ins[2][...].astype(F32) * sd * (1.0 - sd)).astype(BF16)
        outs[3][...] = (du * ins[3][...].astype(F32) * ss * (1.0 - ss)).astype(BF16)

    wpd_n, wps_n, wo_n = _joined_mixer_weights(wpd, wps, wo)
    gd_spec = pl.BlockSpec((TM, D_MODEL), lambda i, j, k: (i, COL_GD // D_MODEL))
    gs_spec = pl.BlockSpec((TM, D_MODEL), lambda i, j, k: (i, COL_GS // D_MODEL))
    ush = jax.ShapeDtypeStruct((T, D_MODEL), BF16)
    dpd, dps, dgd, dgs = _mm(
        "mix_du", [dxo, wo_n, pd, ps, proj, proj],
        [_tok(D_MODEL), _whole(wo_n.shape), _tok(D_MODEL), _tok(D_MODEL), gd_spec, gs_spec],
        [(0, 1, 0)], 1, None, NT, (T // TM, 1, 1), dgated, [ush] * 4, [_tok(D_MODEL)] * 4)

    def one(vals, ins, outs, i):
        outs[0][...] = vals[0].astype(BF16)

    def two(vals, ins, outs, i):
        outs[0][...] = vals[0].astype(BF16)
        outs[1][...] = vals[1].astype(BF16)

    tok_k = pl.BlockSpec((tk, D_MODEL), lambda i, j, k: (k, 0))
    att_k = pl.BlockSpec((tk, D_ATT), lambda i, j, k: (k, 0))
    (dwo_n,) = _mm("mix_dwo", [u, dxo], [tok_k, tok_k], [(0, 1, 0)], 1, (D_MODEL, D_MODEL), TN, (1, 1, T // tk), one,
                   [jax.ShapeDtypeStruct((D_MODEL, D_MODEL), BF16)], [_whole((D_MODEL, D_MODEL))])

    def plain2(vals, ins, outs, i):
        outs[0][...] = vals[0]
        outs[1][...] = vals[1]

    ash = jax.ShapeDtypeStruct((T, D_ATT), F32)
    do_dil, do_sb = _mm(
        "mix_do", [dpd, dps, wpd_n, wps_n], [_tok(D_MODEL), _tok(D_MODEL), _whole(wpd_n.shape), _whole(wps_n.shape)],
        [(0, 2, 0), (1, 3, 1)], 2, None, NT, (T // TM, 1, 1), plain2, [ash, ash], [_tok(D_ATT)] * 2)

    psh = jax.ShapeDtypeStruct((D_ATT, D_MODEL), BF16)
    dwpd_n, dwps_n = _mm(
        "mix_dwp", [o_dil, o_sb, dpd, dps], [att_k, att_k, tok_k, tok_k], [(0, 2, 0), (1, 3, 1)], 2,
        (D_ATT, D_MODEL), TN, (1, 1, T // tk), two, [psh, psh], [_whole((D_ATT, D_MODEL))] * 2)
    dwpd, dwps = (w.reshape(D_ATT, N_CHIPS, cp).transpose(1, 0, 2) for w in (dwpd_n, dwps_n))
    dwo = dwo_n.reshape(N_CHIPS, cp, D_MODEL)

    dqs, dks, dvs = [], [], []
    for g, (window, dil) in enumerate(DIL_GROUPS):
        dq, dk, dv = _dil_bwd(proj, do_dil, o_dil, lse, g, dil)
        dqs.append(dq)
        dks.append(dk)
        dvs.append(dv)
    dq_s, dk_s, dv_s = _sb_bwd(proj, do_sb, o_sb)
    dproj = _assemble_dproj(dqs + dks, dvs + [dq_s], [dk_s, dv_s], [dgd, dgs], tabs)

    dx, dgain = _mm(
        "mix_dx", [dproj, win, x, gain, dxo],
        [pl.BlockSpec((tm, N_CHIPS * cin), lambda i, j, k: (i, 0)), _wfull(D_MODEL, cin, l), row, _gain_spec(), row],
        [(_cols(0, c, cin), _pick(1, c), 0) for c in range(N_CHIPS)], 1, None, NT, (T // tm, 1, 1),
        _rms_bwd_epilogue(2, 3, 4),
        [jax.ShapeDtypeStruct((T, D_MODEL), F32), jax.ShapeDtypeStruct((8, D_MODEL), F32)],
        [row, pl.BlockSpec((8, D_MODEL), lambda i, j, k: (0, 0))])

    (dwin,) = _mm(
        "mix_dwin", [h, dproj],
        [pl.BlockSpec((tk, D_MODEL), lambda i, j, k: (k, 0)), pl.BlockSpec((tk, cin), lambda i, j, k: (k, j))],
        [(0, 1, 0)], 1, (D_MODEL, cin), TN, (1, N_CHIPS, T // tk), one,
        [jax.ShapeDtypeStruct((N_CHIPS, D_MODEL, cin), BF16)],
        [pl.BlockSpec((None, D_MODEL, cin), lambda i, j, k: (j, 0, 0))])
    return dx, dgain, dwin, dwpd, dwps, dwo


def _local_step(x, target, norms, norm_final, weights_of, on_grads):
    T = x.shape[0]
    tabs = _rope_tables(T)
    saved, held = [], []
    for l in range(DEPTH):
        w1 = weights_of(l, 0, x)
        x, s1 = _ffn_fwd(x, norms["norm_ffn1"][l:l + 1], w1["ffn1_w_gate"], w1["ffn1_w_up"], w1["ffn1_w_down"])
        w2 = weights_of(l, 1, x)
        x, s2 = _mixer_fwd(x, norms["norm_mix"][l:l + 1], w2, 0, tabs)
        w3 = weights_of(l, 2, x)
        x, s3 = _ffn_fwd(x, norms["norm_ffn2"][l:l + 1], w3["ffn2_w_gate"], w3["ffn2_w_up"], w3["ffn2_w_down"])
        saved.append((s1, s2, s3))
        held.append((w1, w2, w3))
    dx, dg_final, loss = _final_loss(x, norm_final.reshape(1, D_MODEL), target)
    gains = [None] * DEPTH
    for l in reversed(range(DEPTH)):
        s1, s2, s3 = saved[l]
        w1, w2, w3 = held[l]
        dx, dg2, dwg2, dwu2, dwd2 = _ffn_bwd(dx, norms["norm_ffn2"][l:l + 1], w3["ffn2_w_gate"], w3["ffn2_w_up"],
                                             w3["ffn2_w_down"], s3)
        dx = on_grads(l, 2, dict(ffn2_w_gate=dwg2, ffn2_w_up=dwu2, ffn2_w_down=dwd2), dx)
        dx, dgm, dwin, dwpd, dwps, dwo = _mixer_bwd(dx, norms["norm_mix"][l:l + 1], w2, 0, tabs, s2)
        dx = on_grads(l, 1, dict(w_in=dwin, w_proj_dil=dwpd, w_proj_sb=dwps, w_out=dwo), dx)
        dx, dg1, dwg1, dwu1, dwd1 = _ffn_bwd(dx, norms["norm_ffn1"][l:l + 1], w1["ffn1_w_gate"], w1["ffn1_w_up"],
                                             w1["ffn1_w_down"], s1)
        dx = on_grads(l, 0, dict(ffn1_w_gate=dwg1, ffn1_w_up=dwu1, ffn1_w_down=dwd1), dx)
        gains[l] = dict(norm_ffn1=dg1, norm_mix=dgm, norm_ffn2=dg2)
    return loss, dx, gains, dg_final


def _place():
    x, y, c = lax.axis_index("x"), lax.axis_index("y"), lax.axis_index("c")
    chips = [(1 - x, y), (x, 1 - y), (1 - x, 1 - y)]
    return x, y, c, chips


def _half(c, r):
    return pl.ds(pl.multiple_of(c * (r // 2), 8), r // 2)


def _cast_into_slot(ws, ls, me_arr, after):
    n = len(ws)
    late = [] if after is None else [after]

    def body(me_ref, *refs):
        for a in range(n):
            refs[len(refs) - n + a][...] = refs[a][...].astype(BF16)

    def src(w, l):
        return pl.BlockSpec((None, w.shape[1] // 4, w.shape[2]), lambda i, me: (l, i, 0))

    def dst(w):
        return pl.BlockSpec((None, None, w.shape[1] // 4, w.shape[2]), lambda i, me: (me[0], 0, i, 0))

    return pl.pallas_call(
        body, name="cast_weights",
        grid_spec=pltpu.PrefetchScalarGridSpec(
            num_scalar_prefetch=1, grid=(4,),
            in_specs=[src(w, l) for w, l in zip(ws, ls)] + [pl.BlockSpec(memory_space=pl.ANY)] * len(late),
            out_specs=[dst(w) for w in ws]),
        out_shape=[jax.ShapeDtypeStruct((N_CHIPS, 1) + w.shape[1:], BF16) for w in ws], compiler_params=_params(),
    )(me_arr, *ws, *late)


HBM_SPEC = pl.BlockSpec(memory_space=pltpu.HBM)
SEM_SPEC = pl.BlockSpec(memory_space=pltpu.SEMAPHORE)
SPLIT_COPY = pltpu.CompilerParams(has_side_effects=pltpu.SideEffectType.DATAFLOW_SIDE_EFFECTING)


def _gather_piece(ref, chip_id, c):
    return ref.at[chip_id, 0, _half(c, ref.shape[2]), :]


def _gather_start(tag, bufs, direct):
    n = len(bufs)

    def body(*refs):
        out_refs = refs[n:2 * n]
        send_sems, recv_sems, token = refs[2 * n:]
        x, y, c, chips = _place()
        me = 2 * x + y
        for a in range(n):
            piece = _gather_piece(out_refs[a], me, c)
            for j, chip in enumerate(chips):
                for to in ((0, 1) if direct[a] else (c,)):
                    pltpu.make_async_remote_copy(
                        src_ref=piece, dst_ref=piece, send_sem=send_sems.at[6 * a + 2 * j + to],
                        recv_sem=recv_sems.at[6 * a + 2 * j + c], device_id=(*chip, to), device_id_type=MESH).start()
        token[...] = jnp.zeros_like(token)

    outs = pl.pallas_call(
        body, name=f"gather_start_{tag}", in_specs=[HBM_SPEC] * n,
        out_specs=[HBM_SPEC] * n + [SEM_SPEC, SEM_SPEC, pl.BlockSpec(memory_space=pltpu.VMEM)],
        out_shape=[pltpu.HBM(b.shape, b.dtype) for b in bufs] + [pltpu.SemaphoreType.DMA((6 * n,))] * 2
        + [jax.ShapeDtypeStruct((8, 128), F32)],
        input_output_aliases={a: a for a in range(n)}, compiler_params=SPLIT_COPY,
    )(*[pltpu.with_memory_space_constraint(b, pltpu.HBM) for b in bufs])
    return outs[:n], outs[n], outs[n + 1], outs[n + 2]


def _gather_wait(k, bufs, places, direct, send_sems, recv_sems, after):
    m = len(bufs)

    def body(*refs):
        in_refs = refs[:m]
        ssem, rsem = refs[m], refs[m + 1]
        x, y, c, chips = _place()
        me = 2 * x + y
        for t, a in enumerate(places):
            for j, chip in enumerate(chips):
                for core in ((0, 1) if direct else (c,)):
                    cp = pltpu.make_async_remote_copy(
                        src_ref=_gather_piece(in_refs[t], me, c),
                        dst_ref=_gather_piece(in_refs[t], 2 * chip[0] + chip[1], core),
                        send_sem=ssem.at[6 * a + 2 * j + core], recv_sem=rsem.at[6 * a + 2 * j + core],
                        device_id=(*chip, core), device_id_type=MESH)
                    cp.wait_send()
                    cp.wait_recv()

    return pl.pallas_call(
        body, name=f"gather_wait_{k}",
        in_specs=[HBM_SPEC] * m + [SEM_SPEC, SEM_SPEC, pl.BlockSpec(memory_space=pl.ANY)], out_specs=[HBM_SPEC] * m,
        out_shape=[pltpu.HBM(b.shape, b.dtype) for b in bufs], input_output_aliases={t: t for t in range(m)},
        compiler_params=SPLIT_COPY,
    )(*bufs, send_sems, recv_sems, after)


def _gather_relay(bufs):
    n = len(bufs)

    def body(*refs):
        out_refs = refs[n:2 * n]
        send_sems, recv_sems = refs[2 * n:]
        x, y, c, chips = _place()
        cps = []
        for a in range(n):
            for j, chip in enumerate(chips):
                piece = _gather_piece(out_refs[a], 2 * chip[0] + chip[1], c)
                cps.append(pltpu.make_async_remote_copy(
                    src_ref=piece, dst_ref=piece, send_sem=send_sems.at[a, j], recv_sem=recv_sems.at[a, j],
                    device_id=(x, y, 1 - c), device_id_type=MESH))
        for cp in cps:
            cp.start()
        for a in range(n):
            for j, chip in enumerate(chips):
                theirs = _gather_piece(out_refs[a], 2 * chip[0] + chip[1], 1 - c)
                pltpu.make_async_remote_copy(
                    src_ref=theirs, dst_ref=theirs, send_sem=send_sems.at[a, j], recv_sem=recv_sems.at[a, j],
                    device_id=(x, y, 1 - c), device_id_type=MESH).wait_recv()
        for cp in cps:
            cp.wait_send()

    any_spec = pl.BlockSpec(memory_space=pl.ANY)
    return pl.pallas_call(
        body, name="gather_relay", in_specs=[any_spec] * n, out_specs=[any_spec] * n,
        out_shape=[jax.ShapeDtypeStruct(b.shape, b.dtype) for b in bufs],
        input_output_aliases={a: a for a in range(n)},
        scratch_shapes=[pltpu.SemaphoreType.DMA((n, 3))] * 2,
    )(*bufs)


def _other_half(ref, c):
    return ref.at[:, _half(1 - c, ref.shape[1]), :]


def _all_of(ref, c):
    return ref


def _sibling_start(name, srcs, pick, land_shapes, thru):
    n = len(srcs)
    lands = [lax.empty(sh, s.dtype) for sh, s in zip(land_shapes, srcs)]
    kept = lands + ([] if thru is None else [thru])
    m = len(kept)

    def body(*refs):
        s_refs, land_refs = refs[:n], refs[n + m:n + m + n]
        send_sems, recv_sems, token = refs[n + 2 * m:]
        x, y, c, _ = _place()
        for a in range(n):
            pltpu.make_async_remote_copy(
                src_ref=pick(s_refs[a], c), dst_ref=land_refs[a], send_sem=send_sems.at[a],
                recv_sem=recv_sems.at[a], device_id=(x, y, 1 - c), device_id_type=MESH).start()
        token[...] = jnp.zeros_like(token)

    outs = pl.pallas_call(
        body, name=name, in_specs=[HBM_SPEC] * (n + m),
        out_specs=[HBM_SPEC] * m + [SEM_SPEC, SEM_SPEC, pl.BlockSpec(memory_space=pltpu.VMEM)],
        out_shape=[pltpu.HBM(v.shape, v.dtype) for v in kept] + [pltpu.SemaphoreType.DMA((n,))] * 2
        + [jax.ShapeDtypeStruct((8, 128), F32)],
        input_output_aliases={n + a: a for a in range(m)}, compiler_params=SPLIT_COPY,
    )(*[pltpu.with_memory_space_constraint(v, pltpu.HBM) for v in list(srcs) + kept])
    return (outs[:n], outs[m], outs[m + 1]), (outs[n] if thru is not None else None), outs[m + 2]


def _sibling_wait(name, srcs, pick, lands, send_sems, recv_sems, after):
    n = len(srcs)

    def body(*refs):
        s_refs, land_refs = refs[:n], refs[n:2 * n]
        ssem, rsem = refs[2 * n], refs[2 * n + 1]
        x, y, c, _ = _place()
        for a in range(n):
            cp = pltpu.make_async_remote_copy(
                src_ref=pick(s_refs[a], c), dst_ref=land_refs[a], send_sem=ssem.at[a], recv_sem=rsem.at[a],
                device_id=(x, y, 1 - c), device_id_type=MESH)
            cp.wait_send()
            cp.wait_recv()

    return pl.pallas_call(
        body, name=name, in_specs=[HBM_SPEC] * (2 * n) + [SEM_SPEC, SEM_SPEC, pl.BlockSpec(memory_space=pl.ANY)],
        out_specs=[HBM_SPEC] * n, out_shape=[pltpu.HBM(v.shape, v.dtype) for v in lands],
        input_output_aliases={n + a: a for a in range(n)}, compiler_params=SPLIT_COPY,
    )(*srcs, *lands, send_sems, recv_sems, after)


def _add_half(gs, gots, c_arr):
    n = len(gs)

    def body(c_ref, *refs):
        for a in range(n):
            refs[2 * n + a][...] = (refs[a][...].astype(F32) + refs[n + a][...].astype(F32)).astype(BF16)

    def own(g):
        return pl.BlockSpec((None, g.shape[1] // 2, g.shape[2]), lambda k, cr: (k, cr[0], 0))

    def half(g):
        return pl.BlockSpec((None, g.shape[1] // 2, g.shape[2]), lambda k, cr: (k, 0, 0))

    return pl.pallas_call(
        body, name="grad_add_half",
        grid_spec=pltpu.PrefetchScalarGridSpec(
            num_scalar_prefetch=1, grid=(N_CHIPS,),
            in_specs=[own(g) for g in gs] + [half(g) for g in gs], out_specs=[half(g) for g in gs]),
        out_shape=[jax.ShapeDtypeStruct(got.shape, BF16) for got in gots], compiler_params=_params(),
    )(c_arr, *gs, *gots)


def _scatter_start(k, ss, thru):
    n = len(ss)

    def body(*refs):
        s_refs, land_refs = refs[2 * n + 1:3 * n + 1], refs[3 * n + 1:4 * n + 1]
        send_sems, recv_sems = refs[4 * n + 2:]
        x, y, c, chips = _place()
        me = 2 * x + y
        for a in range(n):
            for j, chip in enumerate(chips):
                pltpu.make_async_remote_copy(
                    src_ref=s_refs[a].at[2 * chip[0] + chip[1]], dst_ref=land_refs[a].at[me],
                    send_sem=send_sems.at[3 * a + j], recv_sem=recv_sems.at[3 * a + j], device_id=(*chip, c),
                    device_id_type=MESH).start()

    lands = [lax.empty(s.shape, s.dtype) for s in ss]
    hbm = [pltpu.HBM(s.shape, s.dtype) for s in ss]
    outs = pl.pallas_call(
        body, name=f"grad_scatter_start_{k}", in_specs=[HBM_SPEC] * (2 * n + 1),
        out_specs=[HBM_SPEC] * (2 * n + 1) + [SEM_SPEC, SEM_SPEC],
        out_shape=hbm + hbm + [pltpu.HBM(thru.shape, thru.dtype)] + [pltpu.SemaphoreType.DMA((3 * n,))] * 2,
        input_output_aliases={a: a for a in range(2 * n + 1)}, compiler_params=SPLIT_COPY,
    )(*[pltpu.with_memory_space_constraint(v, pltpu.HBM) for v in list(ss) + lands + [thru]])
    return (outs[:n], outs[n:2 * n], outs[2 * n + 1], outs[2 * n + 2]), outs[2 * n]


def _scatter_wait(k, ss, lands, send_sems, recv_sems, after):
    n = len(ss)

    def body(*refs):
        s_refs, land_refs = refs[:n], refs[n:2 * n]
        ssem, rsem = refs[2 * n], refs[2 * n + 1]
        x, y, c, chips = _place()
        me = 2 * x + y
        for a in range(n):
            for j, chip in enumerate(chips):
                cid = 2 * chip[0] + chip[1]
                cp = pltpu.make_async_remote_copy(
                    src_ref=s_refs[a].at[cid], dst_ref=land_refs[a].at[cid], send_sem=ssem.at[3 * a + j],
                    recv_sem=rsem.at[3 * a + j], device_id=(*chip, c), device_id_type=MESH)
                cp.wait_send()
                cp.wait_recv()

    hbm = [pltpu.HBM(s.shape, s.dtype) for s in ss]
    outs = pl.pallas_call(
        body, name=f"grad_scatter_wait_{k}",
        in_specs=[HBM_SPEC] * (2 * n) + [SEM_SPEC, SEM_SPEC, pl.BlockSpec(memory_space=pl.ANY)],
        out_specs=[HBM_SPEC] * (2 * n), out_shape=hbm + hbm,
        input_output_aliases={a: a for a in range(2 * n)}, compiler_params=SPLIT_COPY,
    )(*ss, *lands, send_sems, recv_sems, after)
    return outs[:n], outs[n:]


def _sum_chips(lands, ss, me_arr):
    n = len(lands)

    def body(me_ref, *refs):
        for own in range(N_CHIPS):
            @pl.when(me_ref[0] == own)
            def _(own=own):
                for a in range(n):
                    acc = None
                    for k in range(N_CHIPS):
                        term = (refs[n + a][...] if k == own else refs[a][k]).astype(F32)
                        acc = term if acc is None else acc + term
                    refs[2 * n + a][...] = acc

    return pl.pallas_call(
        body, name="grad_sum_chips",
        grid_spec=pltpu.PrefetchScalarGridSpec(
            num_scalar_prefetch=1, grid=(1,),
            in_specs=[pl.BlockSpec(la.shape, lambda i, me: (0, 0, 0)) for la in lands]
            + [pl.BlockSpec((None,) + la.shape[1:], lambda i, me: (me[0], 0, 0)) for la in lands],
            out_specs=[pl.BlockSpec(la.shape[1:], lambda i, me: (0, 0)) for la in lands]),
        out_shape=[jax.ShapeDtypeStruct(la.shape[1:], F32) for la in lands], compiler_params=_params(),
    )(me_arr, *lands, *ss)


def _allreduce_rows(stats):
    def body(s_ref, o_ref, buf, send_sems, recv_sems):
        x, y, c, _ = _place()
        me = 4 * x + 2 * y + c
        buf[me] = s_ref[...]
        cps = []
        for k in range(1, 8):
            px = jnp.where(k & 4, 1 - x, x)
            py = jnp.where(k & 2, 1 - y, y)
            pc = jnp.where(k & 1, 1 - c, c)
            cps.append(pltpu.make_async_remote_copy(
                src_ref=s_ref, dst_ref=buf.at[me], send_sem=send_sems.at[k - 1], recv_sem=recv_sems.at[k - 1],
                device_id=(px, py, pc), device_id_type=MESH))
        for cp in cps:
            cp.start()
        for cp in cps:
            cp.wait()
        acc = buf[0]
        for d in range(1, 8):
            acc = acc + buf[d]
        o_ref[...] = acc

    vm = pl.BlockSpec(memory_space=pltpu.VMEM)
    return pl.pallas_call(
        body, name="allreduce_rows", in_specs=[vm], out_specs=vm,
        out_shape=jax.ShapeDtypeStruct(stats.shape, F32),
        scratch_shapes=[pltpu.VMEM((8,) + stats.shape, F32), pltpu.SemaphoreType.DMA((7,)),
                        pltpu.SemaphoreType.DMA((7,))],
    )(stats)


def _adamw_math(w, g, m, v):
    m = ADAM_B1 * m + (1.0 - ADAM_B1) * g
    v = ADAM_B2 * v + (1.0 - ADAM_B2) * (g * g)
    m_hat = m / (1.0 - ADAM_B1 ** ADAM_STEP)
    v_hat = v / (1.0 - ADAM_B2 ** ADAM_STEP)
    delta = -ADAM_LR * (m_hat / (jnp.sqrt(v_hat) + ADAM_EPS) + ADAM_WD * w)
    return delta, m, v


def _adamw(ws, ms, vs, mines, theirs, l, c_arr, earlier, after):
    n = len(ws)
    held = [t for e in earlier if e is not None for t in e]
    assert len(held) in (0, 4 * n)
    late = [] if after is None else [after]

    def body(c_ref, *refs):
        outs = refs[len(refs) - 4 * n:]
        for a in range(n):
            w_ref, m_ref, v_ref, a_ref, b_ref = refs[5 * a:5 * a + 5]
            g = jnp.where(pl.program_id(0) == c_ref[0], a_ref[...], b_ref[...])
            delta, mn, vn = _adamw_math(w_ref[...], g, m_ref[...], v_ref[...])
            outs[4 * a][...] = g
            outs[4 * a + 1][...] = delta
            outs[4 * a + 2][...] = mn
            outs[4 * a + 3][...] = vn

    def blk(w):
        tr = w.shape[1] // 4
        return pl.BlockSpec((None, tr, w.shape[2]), lambda hh, i, cr: (l, 2 * hh + i, 0))

    def half(w):
        return pl.BlockSpec((w.shape[1] // 4, w.shape[2]), lambda hh, i, cr: (i, 0))

    outs = pl.pallas_call(
        body, name="adamw",
        grid_spec=pltpu.PrefetchScalarGridSpec(
            num_scalar_prefetch=1, grid=(2, 2),
            in_specs=[sp for w in ws for sp in (blk(w), blk(w), blk(w), half(w), half(w))]
            + [pl.BlockSpec(memory_space=pl.ANY)] * (len(held) + len(late)),
            out_specs=[blk(w) for w in ws for _ in range(4)]),
        out_shape=[jax.ShapeDtypeStruct(w.shape, F32) for w in ws for _ in range(4)],
        input_output_aliases={1 + 5 * n + t: t for t in range(len(held))}, compiler_params=_params(),
    )(c_arr, *[t for grp in zip(ws, ms, vs, mines, theirs) for t in grp], *held, *late)
    return [outs[4 * a:4 * a + 4] for a in range(n)]


def _adamw_rows(w, m, v, g):
    def body(w_ref, m_ref, v_ref, g_ref, d_ref, mo_ref, vo_ref):
        delta, mn, vn = _adamw_math(w_ref[...], g_ref[...], m_ref[...], v_ref[...])
        d_ref[...] = delta
        mo_ref[...] = mn
        vo_ref[...] = vn

    vm = pl.BlockSpec(memory_space=pltpu.VMEM)
    sh = jax.ShapeDtypeStruct(w.shape, F32)
    return pl.pallas_call(body, name="adamw_rows", in_specs=[vm] * 4, out_specs=[vm] * 3, out_shape=[sh] * 3)(w, m, v, g)


SUBLAYERS = (("ffn1_w_gate", "ffn1_w_up", "ffn1_w_down"), ("w_in", "w_proj_dil", "w_proj_sb", "w_out"),
             ("ffn2_w_gate", "ffn2_w_up", "ffn2_w_down"))
TRANSPOSED = ("ffn1_w_gate", "ffn1_w_up", "ffn2_w_gate", "ffn2_w_up")


def _pick_row(blocks):
    row = lax.broadcasted_iota(jnp.int32, (8, D_MODEL), 0)
    out = jnp.zeros((8, D_MODEL), F32)
    for i, b in enumerate(blocks):
        out = out + jnp.where(row == i, b, 0.0)
    return out


def kernel(x, norm_ffn1, ffn1_w_gate, ffn1_w_up, ffn1_w_down, norm_mix, w_in, w_proj_dil, w_proj_sb, w_out, norm_ffn2, ffn2_w_gate, ffn2_w_up, ffn2_w_down, norm_final, loss_target, m_norm_ffn1, m_ffn1_w_gate, m_ffn1_w_up, m_ffn1_w_down, m_norm_mix, m_w_in, m_w_proj_dil, m_w_proj_sb, m_w_out, m_norm_ffn2, m_ffn2_w_gate, m_ffn2_w_up, m_ffn2_w_down, m_norm_final, v_norm_ffn1, v_ffn1_w_gate, v_ffn1_w_up, v_ffn1_w_down, v_norm_mix, v_w_in, v_w_proj_dil, v_w_proj_sb, v_w_out, v_norm_ffn2, v_ffn2_w_gate, v_ffn2_w_up, v_ffn2_w_down, v_norm_final):
    given = dict(locals())
    for n in TRANSPOSED:
        for k in ("", "m_", "v_"):
            given[k + n] = jnp.swapaxes(given[k + n], 1, 2)
    weights = {n: given[n] for n in WEIGHT_NAMES}
    norms = {n: given[n] for n in NORM_NAMES}

    c_arr = lax.axis_index("c").astype(jnp.int32).reshape(1)
    me_arr = (2 * lax.axis_index("x") + lax.axis_index("y")).astype(jnp.int32).reshape(1)
    order = [(l, s, n) for l in range(DEPTH) for s in range(len(SUBLAYERS)) for n in SUBLAYERS[s]]
    n_first = len(SUBLAYERS[0])
    sent, token = {}, None
    for tag, idxs in (("a", range(n_first)), ("b", range(n_first, len(order)))):
        cast = _cast_into_slot([weights[order[i][2]] for i in idxs], [order[i][0] for i in idxs], me_arr, token)
        bufs, send_sems, recv_sems, token = _gather_start(tag, cast, [order[i][0] > 0 for i in idxs])
        for p, i in enumerate(idxs):
            sent[i] = (bufs[p], p, send_sems, recv_sems)

    def weights_of(l, s, after):
        idxs = [i for i, (ll, ss, _) in enumerate(order) if (ll, ss) == (l, s)]
        got = _gather_wait(len(SUBLAYERS) * l + s, [sent[i][0] for i in idxs], [sent[i][1] for i in idxs], l > 0,
                           sent[idxs[0]][2], sent[idxs[0]][3], after)
        return {order[i][2]: g for i, g in zip(idxs, got if l > 0 else _gather_relay(got))}

    out = {}
    to_add, in_flight = [], []

    def add_and_scatter(after):
        l, s, names, gs, lands, ssem, rsem = to_add.pop(0)
        k = len(SUBLAYERS) * l + s
        got = _sibling_wait(f"grad_exchange_wait_{k}", gs, _other_half, lands, ssem, rsem, after)
        sent, after = _scatter_start(k, _add_half(gs, got, c_arr), after)
        in_flight.append((l, s, names) + sent)
        return after

    def on_grads(l, s, grads, after):
        names = list(grads)
        k = len(SUBLAYERS) * l + s
        gs = [grads[n] for n in names]
        sent, after, _ = _sibling_start(f"grad_exchange_start_{k}", gs, _other_half,
                                        [(g.shape[0], g.shape[1] // 2, g.shape[2]) for g in gs], after)
        if to_add:
            after = add_and_scatter(after)
        to_add.append((l, s, names, gs) + sent)
        return after

    loss_blk, grad_x, gains, dg_final = _local_step(x[0], loss_target[0], norms, norm_final, weights_of, on_grads)
    grad_x = add_and_scatter(grad_x)

    def update(l, names, mine, swap, after):
        theirs = _sibling_wait(f"grad_swap_wait_{l}_{names[0]}", mine, _all_of, *swap, grad_x if after is None else after)
        res = _adamw([weights[n] for n in names], [given["m_" + n] for n in names], [given["v_" + n] for n in names],
                     mine, theirs, l, c_arr, [out.get(n) for n in names], after)
        out.update(zip(names, res))

    waiting = None
    for l, s, names, sums, lands, ssem, rsem in in_flight:
        sums, lands = _scatter_wait(len(SUBLAYERS) * l + s, sums, lands, ssem, rsem, grad_x)
        mine = _sum_chips(lands, sums, me_arr)
        swap, _, token = _sibling_start(f"grad_swap_start_{l}_{names[0]}", mine, _all_of,
                                        [m.shape for m in mine], None)
        if waiting is not None:
            update(*waiting, token)
        waiting = (l, names, mine, swap)
    update(*waiting, None)
    out = {k + n: (jnp.swapaxes(v, 1, 2) if n in TRANSPOSED else v)
           for n, res in out.items() for k, v in zip(("grad_", "delta_", "new_m_", "new_v_"), res)}
    out["grad_x"] = grad_x[None]

    rows = [gains[l][n] for n in NORM_NAMES for l in range(DEPTH)] + [dg_final, loss_blk]
    total = _allreduce_rows(_pick_row(rows))
    out["loss"] = total[7, 0]
    wn = jnp.concatenate([given[n] for n in NORM_NAMES] + [norm_final[None], jnp.zeros((1, D_MODEL), F32)])
    mn_ = jnp.concatenate([given["m_" + n] for n in NORM_NAMES] + [m_norm_final[None], jnp.zeros((1, D_MODEL), F32)])
    vn_ = jnp.concatenate([given["v_" + n] for n in NORM_NAMES] + [v_norm_final[None], jnp.ones((1, D_MODEL), F32)])
    d_n, m_n, v_n = _adamw_rows(wn, mn_, vn_, total)
    for i, n in enumerate(NORM_NAMES):
        sl = slice(i * DEPTH, (i + 1) * DEPTH)
        out["grad_" + n], out["delta_" + n], out["new_m_" + n], out["new_v_" + n] = total[sl], d_n[sl], m_n[sl], v_n[sl]
    out["grad_norm_final"], out["delta_norm_final"] = total[6], d_n[6]
    out["new_m_norm_final"], out["new_v_norm_final"] = m_n[6], v_n[6]

    names = ["norm_ffn1", "ffn1_w_gate", "ffn1_w_up", "ffn1_w_down", "norm_mix", "w_in", "w_proj_dil", "w_proj_sb",
             "w_out", "norm_ffn2", "ffn2_w_gate", "ffn2_w_up", "ffn2_w_down", "norm_final"]
    return (out["loss"], out["grad_x"], *[out["grad_" + n] for n in names], *[out["delta_" + n] for n in names],
            *[out["new_m_" + n] for n in names], *[out["new_v_" + n] for n in names])
```

```python
import functools

import jax
import jax.numpy as jnp
from jax import lax
from jax.experimental import pallas as pl
from jax.experimental.pallas import tpu as pltpu

F32 = jnp.float32
BF16 = jnp.bfloat16

D_MODEL = 1024
DEPTH = 2
N_CHIPS = 4
HEAD_DIM = 64
ROPE_DIM = 16
ROPE_THETA = 500000.0
DIL_GROUPS = ((128, 1), (512, 4), (2048, 16))
SPAN = 128
Q_BLOCK = 128
RMS_EPS = 1e-6
D_ATT = 256
COL_QS = 2304
COL_GD = 3072
COL_GS = 4096
ADAM_LR, ADAM_B1, ADAM_B2, ADAM_EPS, ADAM_WD, ADAM_STEP = 0.001, 0.9, 0.999, 1e-08, 0.01, 10

VMEM_LIMIT = 52 * 1024 * 1024
TM = 512
NEG = -1e30

NN = (((1,), (0,)), ((), ()))
NT = (((1,), (1,)), ((), ()))
TN = (((0,), (0,)), ((), ()))
MESH = pl.DeviceIdType.MESH

WEIGHT_NAMES = ("ffn1_w_gate", "ffn1_w_up", "ffn1_w_down", "w_in", "w_proj_dil",
                "w_proj_sb", "w_out", "ffn2_w_gate", "ffn2_w_up", "ffn2_w_down")
NORM_NAMES = ("norm_ffn1", "norm_mix", "norm_ffn2")


def _params(**kw):
    return pltpu.CompilerParams(vmem_limit_bytes=VMEM_LIMIT, **kw)


def _sigmoid(x):
    return 0.5 * jnp.tanh(0.5 * x) + 0.5


def _mm_body(pairs, n_in, n_out, n_acc, dims, nk, epilogue, *refs):
    ins = refs[:n_in]
    outs = refs[n_in:n_in + n_out]
    accs = refs[n_in + n_out:]
    i = pl.program_id(0)
    k = pl.program_id(2)

    def operand(a):
        return (a(ins) if callable(a) else ins[a][...]).astype(BF16)

    def dot(ia, ib):
        return lax.dot_general(operand(ia), operand(ib), dims, preferred_element_type=F32)

    if nk == 1:
        parts = [None] * n_acc
        for ia, ib, ic in pairs:
            parts[ic] = dot(ia, ib) if parts[ic] is None else parts[ic] + dot(ia, ib)
        epilogue(parts, ins, outs, i)
        return

    @pl.when(k == 0)
    def _():
        for c in range(n_acc):
            accs[c][...] = jnp.zeros_like(accs[c])

    for ia, ib, ic in pairs:
        accs[ic][...] += dot(ia, ib)

    @pl.when(k == nk - 1)
    def _():
        epilogue([a[...] for a in accs], ins, outs, i)


def _mm(name, ins, in_specs, pairs, n_acc, acc_shape, dims, grid, epilogue, out_shapes, out_specs):
    nk = grid[2]
    scratch = [pltpu.VMEM(acc_shape, F32) for _ in range(n_acc)] if nk > 1 else []
    body = functools.partial(_mm_body, tuple(pairs), len(ins), len(out_shapes), n_acc, dims, nk, epilogue)
    return pl.pallas_call(
        body, name=name, grid=grid, in_specs=in_specs, out_specs=out_specs, out_shape=out_shapes,
        scratch_shapes=scratch,
        compiler_params=_params(dimension_semantics=("arbitrary", "arbitrary", "arbitrary")),
    )(*ins)


def _rms_bwd_epilogue(x_idx, g_idx, dxo_idx):
    def ep(vals, ins, outs, i):
        dh = vals[0]
        x = ins[x_idx][...]
        g = ins[g_idx][...]
        rstd = lax.rsqrt(jnp.mean(x * x, axis=-1, keepdims=True) + RMS_EPS)
        xhat = x * rstd
        dxhat = dh * g
        dx = rstd * (dxhat - xhat * jnp.mean(dxhat * xhat, axis=-1, keepdims=True))
        outs[0][...] = ins[dxo_idx][...] + dx
        dg = jnp.broadcast_to(jnp.sum(dh * xhat, axis=0, keepdims=True), outs[1].shape)

        @pl.when(i == 0)
        def _():
            outs[1][...] = dg

        @pl.when(i > 0)
        def _():
            outs[1][...] += dg
    return ep


def _normed(x_idx, g_idx):
    seen = {}

    def f(ins):
        if id(ins) not in seen:
            xv = ins[x_idx][...]
            h = xv * lax.rsqrt(jnp.mean(xv * xv, axis=-1, keepdims=True) + RMS_EPS)
            seen[id(ins)] = (ins, (h * ins[g_idx][...]).astype(BF16))
        return seen[id(ins)][1]
    return f


def _rope_tables(T):
    half = ROPE_DIM // 2
    lane = jnp.arange(128) % HEAD_DIM
    inv_freq = ROPE_THETA ** (-(2 * (lane % half)).astype(F32) / ROPE_DIM)
    ang = jnp.arange(T, dtype=F32)[:, None] * inv_freq[None, :]
    cos, sin = jnp.cos(ang), jnp.sin(ang)
    c = jnp.where(lane < ROPE_DIM, cos, 1.0)
    s1 = jnp.where(lane < half, -sin, 0.0)
    s2 = jnp.where((lane >= half) & (lane < ROPE_DIM), sin, 0.0)
    return c, s1, s2


def _rope_fwd(xv, c, s1, s2):
    w = xv.shape[1]
    half = ROPE_DIM // 2
    return xv * c + pltpu.roll(xv, w - half, 1) * s1 + pltpu.roll(xv, half, 1) * s2


def _rope_bwd(dy, c, s1, s2):
    w = dy.shape[1]
    half = ROPE_DIM // 2
    return dy * c + pltpu.roll(dy * s1, half, 1) + pltpu.roll(dy * s2, w - half, 1)


def _assemble_dproj(dqk, rest, turned, gates, tabs):
    T = gates[0].shape[0]
    n_qk, n_rest = len(dqk), len(rest) + len(turned)
    width = (n_qk + n_rest) * D_ATT + 2 * D_MODEL

    def body(*refs):
        ins, (c_ref, s1_ref, s2_ref), o_ref = refs[:n_qk + n_rest + 2], refs[-4:-1], refs[-1]
        c = jnp.concatenate([c_ref[...]] * 2, axis=1)
        s1 = jnp.concatenate([s1_ref[...]] * 2, axis=1)
        s2 = jnp.concatenate([s2_ref[...]] * 2, axis=1)
        for b in range(n_qk + n_rest):
            v = ins[b][...]
            if b < n_qk:
                v = _rope_bwd(v, c, s1, s2)
            if b >= n_qk + len(rest):
                v = v.T
            o_ref[:, b * D_ATT:(b + 1) * D_ATT] = v.astype(BF16)
        off = (n_qk + n_rest) * D_ATT
        o_ref[:, off:off + D_MODEL] = ins[-2][...]
        o_ref[:, off + D_MODEL:] = ins[-1][...]

    att = pl.BlockSpec((TM, D_ATT), lambda i: (i, 0))
    att_turned = pl.BlockSpec((D_ATT, TM), lambda i: (0, i))
    wide = pl.BlockSpec((TM, D_MODEL), lambda i: (i, 0))
    tab = pl.BlockSpec((TM, 128), lambda i: (i, 0))
    return pl.pallas_call(
        body, name="assemble_dproj", grid=(T // TM,),
        in_specs=[att] * (n_qk + len(rest)) + [att_turned] * len(turned) + [wide, wide, tab, tab, tab],
        out_specs=pl.BlockSpec((TM, width), lambda i: (i, 0)),
        out_shape=jax.ShapeDtypeStruct((T, width), BF16), compiler_params=_params(),
    )(*dqk, *rest, *turned, *gates, *tabs)


def _dil_merge(os_, lses):
    T = os_[0].shape[0]

    def body(o0, o1, o2, l0, l1, l2, o_ref, lse_ref):
        a, b, c = l0[...], l1[...], l2[...]
        m = jnp.maximum(jnp.maximum(a, b), c)
        ea, eb, ec = jnp.exp(a - m), jnp.exp(b - m), jnp.exp(c - m)
        den = ea + eb + ec
        o_ref[...] = (ea * o0[...] + eb * o1[...] + ec * o2[...]) / den
        lse_ref[...] = m + jnp.log(den)

    blk = pl.BlockSpec((TM, D_ATT), lambda i: (i, 0))
    sh = jax.ShapeDtypeStruct((T, D_ATT), F32)
    return pl.pallas_call(
        body, name="dil_merge", grid=(T // TM,), in_specs=[blk] * 6, out_specs=[blk, blk],
        out_shape=[sh, sh], compiler_params=_params(),
    )(*os_, *lses)


def _final_loss(x, gain, target):
    T = x.shape[0]

    def body(x_ref, g_ref, t_ref, dx_ref, dg_ref, loss_ref):
        xv = x_ref[...]
        g = g_ref[...]
        rstd = lax.rsqrt(jnp.mean(xv * xv, axis=-1, keepdims=True) + RMS_EPS)
        xhat = xv * rstd
        err = xhat * g - t_ref[...]
        loss = 0.5 * jnp.sum(jnp.mean(err * err, axis=-1, keepdims=True), axis=0, keepdims=True)
        dy = err * (1.0 / D_MODEL)
        dxhat = dy * g
        dx_ref[...] = rstd * (dxhat - xhat * jnp.mean(dxhat * xhat, axis=-1, keepdims=True))
        dg = jnp.broadcast_to(jnp.sum(dy * xhat, axis=0, keepdims=True), dg_ref.shape)
        ls = jnp.broadcast_to(loss, loss_ref.shape)

        @pl.when(pl.program_id(0) == 0)
        def _():
            dg_ref[...] = dg
            loss_ref[...] = ls

        @pl.when(pl.program_id(0) > 0)
        def _():
            dg_ref[...] += dg
            loss_ref[...] += ls

    blk = pl.BlockSpec((TM, D_MODEL), lambda i: (i, 0))
    row = pl.BlockSpec((1, D_MODEL), lambda i: (0, 0))
    acc = pl.BlockSpec((8, D_MODEL), lambda i: (0, 0))
    return pl.pallas_call(
        body, name="final_loss", grid=(T // TM,), in_specs=[blk, row, blk], out_specs=[blk, acc, acc],
        out_shape=[jax.ShapeDtypeStruct((T, D_MODEL), F32), jax.ShapeDtypeStruct((8, D_MODEL), F32),
                   jax.ShapeDtypeStruct((8, D_MODEL), F32)],
        compiler_params=_params(dimension_semantics=("arbitrary",)),
    )(x, gain, target)


def _pair_masks():
    lane = lax.broadcasted_iota(jnp.int32, (SPAN, 128), 1)
    return [lane < HEAD_DIM, lane >= HEAD_DIM]


def _stack_heads(x, masks):
    return jnp.concatenate([jnp.where(m, x, 0.0) for m in masks], axis=0)


def _unstack_heads(y, masks):
    rows = y.shape[0] // len(masks)
    out = jnp.where(masks[0], y[:rows], 0.0)
    for h in range(1, len(masks)):
        out = out + jnp.where(masks[h], y[rows * h:rows * (h + 1)], 0.0)
    return out


DIL_PAIR = 2


def _dil_rows(idx, d):
    u = idx // d
    r = idx - u * d
    own = pl.ds(u * (SPAN * d) + r, SPAN, stride=d) if d > 1 else pl.ds(pl.multiple_of(u * SPAN, SPAN), SPAN)
    up = jnp.maximum(u - 1, 0)
    prev = pl.ds(up * (SPAN * d) + r, SPAN, stride=d) if d > 1 else pl.ds(pl.multiple_of(up * SPAN, SPAN), SPAN)
    return u, own, prev


def _dil_valid(u):
    qi = lax.broadcasted_iota(jnp.int32, (2 * SPAN, 2 * SPAN), 0) & (SPAN - 1)
    kj = lax.broadcasted_iota(jnp.int32, (2 * SPAN, 2 * SPAN), 1)
    in_prev = (kj < SPAN) & (kj >= qi + jnp.where(u > 0, 0, SPAN))
    return in_prev | ((kj >= SPAN) & (kj - SPAN <= qi))


def _dil_keys(ref, own, prev):
    return jnp.concatenate([ref[prev, :], ref[own, :]], axis=0).astype(BF16)


def _dil_fwd(proj, g, d):
    T = proj.shape[0]
    n_iter = T // SPAN

    def body(q_ref, k_ref, v_ref, o_ref, lse_ref):
        masks = _pair_masks()

        def step(pair, carry):
            its = [_dil_rows(DIL_PAIR * pair + e, d) for e in range(DIL_PAIR)]
            qs = [_stack_heads(q_ref[own, :] * (HEAD_DIM ** -0.5), masks).astype(BF16) for _, own, _ in its]
            kks = [_dil_keys(k_ref, own, prev) for _, own, prev in its]
            vvs = [_dil_keys(v_ref, own, prev) for _, own, prev in its]
            ss = [jnp.where(_dil_valid(u), lax.dot_general(q, kk, NT, preferred_element_type=F32), NEG)
                  for (u, _, _), q, kk in zip(its, qs, kks)]
            ms = [jnp.max(s, axis=1, keepdims=True) for s in ss]
            ps = [jnp.exp(s - m) for s, m in zip(ss, ms)]
            dens = [jnp.sum(p, axis=1, keepdims=True) for p in ps]
            pvs = [lax.dot_general(p.astype(BF16), vv, NN, preferred_element_type=F32) / den
                   for p, vv, den in zip(ps, vvs, dens)]
            for (_, own, _), pv, m, den in zip(its, pvs, ms, dens):
                o_ref[own, :] = _unstack_heads(pv, masks)
                lse_ref[own, :] = _unstack_heads(jnp.broadcast_to(m + jnp.log(den), pv.shape), masks)
            return carry

        lax.fori_loop(0, n_iter // DIL_PAIR, step, 0)

    def col(b):
        return pl.BlockSpec((T, 128), lambda p: (0, b + p))

    sh = jax.ShapeDtypeStruct((T, D_ATT), F32)
    out = pl.BlockSpec((T, 128), lambda p: (0, p))
    return pl.pallas_call(
        body, name=f"dil_fwd_d{d}", grid=(2,),
        in_specs=[col(2 * g), col(6 + 2 * g), col(12 + 2 * g)], out_specs=[out, out], out_shape=[sh, sh],
        compiler_params=_params(dimension_semantics=("arbitrary",)),
    )(proj, proj, proj)


def _dil_bwd(proj, do, o_dil, lse, g, d):
    T = proj.shape[0]
    n_iter = T // SPAN

    def body(q_ref, k_ref, v_ref, do_ref, o_ref, lse_ref, dq_ref, dk_ref, dv_ref):
        masks = _pair_masks()
        head_lanes = jnp.concatenate(masks, axis=0)

        def step(pair, carry):
            its = [_dil_rows(DIL_PAIR * pair + e, d) for e in range(DIL_PAIR)]
            qs = [_stack_heads(q_ref[own, :] * (HEAD_DIM ** -0.5), masks).astype(BF16) for _, own, _ in its]
            kks = [_dil_keys(k_ref, own, prev) for _, own, prev in its]
            vvs = [_dil_keys(v_ref, own, prev) for _, own, prev in its]
            doms = [_stack_heads(do_ref[own, :], masks) for _, own, _ in its]
            dos = [dom.astype(BF16) for dom in doms]
            deltas = [jnp.sum(dom * jnp.concatenate([o_ref[own, :]] * 2, axis=0), axis=1, keepdims=True)
                      for dom, (_, own, _) in zip(doms, its)]
            lrows = [jnp.max(jnp.where(head_lanes, jnp.concatenate([lse_ref[own, :]] * 2, axis=0), NEG),
                             axis=1, keepdims=True) for _, own, _ in its]
            ss = [lax.dot_general(q, kk, NT, preferred_element_type=F32) for q, kk in zip(qs, kks)]
            dps = [lax.dot_general(do_b, vv, NT, preferred_element_type=F32) for do_b, vv in zip(dos, vvs)]
            ps = [jnp.where(_dil_valid(u), jnp.exp(s - lrow), 0.0) for (u, _, _), s, lrow in zip(its, ss, lrows)]
            dss = [(p * (dp - delta)).astype(BF16) for p, dp, delta in zip(ps, dps, deltas)]
            dqs = [lax.dot_general(ds, kk, NN, preferred_element_type=F32) for ds, kk in zip(dss, kks)]
            dkks = [lax.dot_general(ds, q, TN, preferred_element_type=F32) for ds, q in zip(dss, qs)]
            dvvs = [lax.dot_general(p.astype(BF16), do_b, TN, preferred_element_type=F32) for p, do_b in zip(ps, dos)]
            for (_, own, prev), dq, dkk, dvv in zip(its, dqs, dkks, dvvs):
                dq_ref[own, :] = _unstack_heads(dq, masks) * (HEAD_DIM ** -0.5)
                dk_ref[own, :] = dkk[SPAN:]
                dv_ref[own, :] = dvv[SPAN:]
                dk_ref[prev, :] = dk_ref[prev, :] + dkk[:SPAN]
                dv_ref[prev, :] = dv_ref[prev, :] + dvv[:SPAN]
            return carry

        lax.fori_loop(0, n_iter // DIL_PAIR, step, 0)

    def col(b):
        return pl.BlockSpec((T, 128), lambda p: (0, b + p))

    sh = jax.ShapeDtypeStruct((T, D_ATT), F32)
    return pl.pallas_call(
        body, name=f"dil_bwd_d{d}", grid=(2,),
        in_specs=[col(2 * g), col(6 + 2 * g), col(12 + 2 * g), col(0), col(0), col(0)],
        out_specs=[col(0), col(0), col(0)], out_shape=[sh, sh, sh],
        compiler_params=_params(dimension_semantics=("arbitrary",)),
    )(proj, proj, proj, do, o_dil, lse)


SB_KT = 512
LOG2_E = 1.4426950408889634


def _sb_tri(strict):
    a = lax.broadcasted_iota(jnp.int32, (Q_BLOCK, Q_BLOCK), 0)
    b = lax.broadcasted_iota(jnp.int32, (Q_BLOCK, Q_BLOCK), 1)
    return jnp.where((a > b) if strict else (a >= b), 1.0, 0.0).astype(BF16)


def _split_stack(x):
    nb = x.shape[1] // Q_BLOCK
    blocks = [x[:, Q_BLOCK * b:Q_BLOCK * (b + 1)] for b in range(nb)]
    hi = [b.astype(BF16) for b in blocks]
    lo = [(b - h.astype(F32)).astype(BF16) for b, h in zip(blocks, hi)]
    return blocks, jnp.concatenate(hi + lo, axis=0)


def _suffix_from(y, blocks, c):
    r = blocks[0].shape[0]
    nb = len(blocks)
    outs = [None] * nb
    run = c
    for b in reversed(range(nb)):
        outs[b] = run + y[r * b:r * (b + 1)] + y[r * (nb + b):r * (nb + b + 1)]
        run = run + jnp.sum(blocks[b], axis=1, keepdims=True)
    return jnp.concatenate(outs, axis=1), run


SB_HEADS = D_ATT // HEAD_DIM
SB_FWD_CHAINS = 2
SB_BWD_CHAINS = 1


SB_PLACES = SB_KT // Q_BLOCK


def _sb_past(place, rows):
    row = lax.broadcasted_iota(jnp.int32, (rows, Q_BLOCK * (place + 1)), 0) & (Q_BLOCK - 1)
    col = lax.broadcasted_iota(jnp.int32, (rows, Q_BLOCK * (place + 1)), 1)
    return col < row + place * Q_BLOCK


def _sb_head_masks(chains):
    lane = lax.broadcasted_iota(jnp.int32, (Q_BLOCK, D_ATT), 1)
    masks = [(lane >= HEAD_DIM * h) & (lane < HEAD_DIM * (h + 1)) for h in range(SB_HEADS)]
    per = SB_HEADS // chains
    return [masks[per * g:per * (g + 1)] for g in range(chains)]


def _sb_rows(t, width=SB_KT):
    return pl.ds(pl.multiple_of(t * SB_KT, SB_KT), width)


def _sb_widen(x):
    if x.shape[1] == SB_KT:
        return x
    return jnp.concatenate([x, jnp.zeros((x.shape[0], SB_KT - x.shape[1]), x.dtype)], axis=1)


def _sb_log_terms(z, past):
    lsz = jnp.minimum(z, 0.0) - jnp.log(1.0 + jnp.exp2(jnp.abs(z) * -LOG2_E))
    lk = lsz - z
    return lsz, (lk if past is None else jnp.where(past, lk, 0.0))


def _sb_weights(lsz, after, past):
    w = jnp.exp(lsz + after)
    return w if past is None else jnp.where(past, w, 0.0)


def _sb_fwd(proj):
    T = proj.shape[0]
    rows = SB_HEADS // SB_FWD_CHAINS * Q_BLOCK

    def at_place(place, n_tiles, q_ref, k_ref, v_ref, o_ref, z_buf, w_buf):
        masks = _sb_head_masks(SB_FWD_CHAINS)
        tri = _sb_tri(True)
        q = q_ref[...] * (HEAD_DIM ** -0.5)
        qs = [_stack_heads(q, m).astype(BF16) for m in masks]

        def scores(t, width=SB_KT):
            kb = k_ref[_sb_rows(t, width), :].astype(BF16)
            return [lax.dot_general(g, kb, NT, preferred_element_type=F32) for g in qs]

        def weights(zs, cs, past, between=lambda: None):
            logs = [_sb_log_terms(z, past) for z in zs]
            splits = [_split_stack(lk) for _, lk in logs]
            ys = [lax.dot_general(x, tri, NN, preferred_element_type=F32) for _, x in splits]
            between()
            sums = [_suffix_from(y, blocks, c) for y, (blocks, _), c in zip(ys, splits, cs)]
            ws = [_sb_weights(lsz, after, past).astype(BF16) for (lsz, _), (after, _) in zip(logs, sums)]
            return ws, [c for _, c in sums]

        def values(acc, slot, t):
            vb = v_ref[_sb_rows(t), :].astype(BF16)
            for g, m in enumerate(masks):
                acc = acc + _unstack_heads(lax.dot_general(w_buf[slot, g], vb, NN, preferred_element_type=F32), m)
            return acc

        def keep(buf, slot, xs):
            for g, x in enumerate(xs):
                buf[slot, g] = x

        zs = scores(n_tiles - 1, Q_BLOCK * (place + 1))
        keep(z_buf, 0, scores(jnp.maximum(n_tiles - 2, 0)))
        ws, cs = weights(zs, [jnp.zeros((rows, 1), F32)] * SB_FWD_CHAINS, _sb_past(place, rows))
        keep(w_buf, 0, [_sb_widen(w) for w in ws])

        def step(tt, carry):
            t = n_tiles - 2 - tt
            cur = tt & 1
            acc = values(carry[0], cur, t + 1)
            ws, cs = weights([z_buf[cur, g] for g in range(SB_FWD_CHAINS)], carry[1:], None,
                             lambda: keep(z_buf, 1 - cur, scores(jnp.maximum(t - 1, 0))))
            keep(w_buf, 1 - cur, ws)
            return (acc, *cs)

        carry = lax.fori_loop(0, n_tiles - 1, step, (jnp.zeros((Q_BLOCK, D_ATT), F32), *cs))
        o_ref[...] = values(carry[0], (n_tiles - 1) & 1, 0)

    def body(*refs):
        n_tiles = pl.program_id(1) + 1
        for place in range(SB_PLACES):
            pl.when(pl.program_id(0) == place)(functools.partial(at_place, place, n_tiles, *refs))

    cb = COL_QS // D_ATT
    return pl.pallas_call(
        body, name="sb_fwd", grid=(SB_PLACES, T // SB_KT),
        in_specs=[pl.BlockSpec((Q_BLOCK, D_ATT), lambda p, j: (SB_PLACES * j + p, cb)),
                  pl.BlockSpec((T, D_ATT), lambda p, j: (0, cb + 1)),
                  pl.BlockSpec((T, D_ATT), lambda p, j: (0, cb + 2))],
        out_specs=pl.BlockSpec((Q_BLOCK, D_ATT), lambda p, j: (SB_PLACES * j + p, 0)),
        out_shape=jax.ShapeDtypeStruct((T, D_ATT), F32),
        scratch_shapes=[pltpu.VMEM((2, SB_FWD_CHAINS, rows, SB_KT), F32),
                        pltpu.VMEM((2, SB_FWD_CHAINS, rows, SB_KT), BF16)],
        compiler_params=_params(dimension_semantics=("arbitrary", "arbitrary")),
    )(proj, proj, proj)


def _sb_bwd(proj, do, o):
    T = proj.shape[0]
    n_rows = SB_HEADS // SB_BWD_CHAINS * Q_BLOCK

    def at_place(place, n_tiles, q_ref, k_ref, v_ref, do_ref, o_ref, dq_ref, dk_ref, dv_ref,
                 z_buf, gv_buf, dz_buf, w_buf):
        masks = _sb_head_masks(SB_BWD_CHAINS)
        tri = _sb_tri(True)
        tri_incl = _sb_tri(False)

        q = q_ref[...] * (HEAD_DIM ** -0.5)
        qs = [_stack_heads(q, m).astype(BF16) for m in masks]
        dos = [_stack_heads(do_ref[...], m).astype(BF16) for m in masks]
        qts = [_stack_heads(q, m).T.astype(BF16) for m in masks]
        dots = [_stack_heads(do_ref[...], m).T.astype(BF16) for m in masks]
        o_rep = jnp.concatenate([o_ref[...]] * (SB_HEADS // SB_BWD_CHAINS), axis=0)
        deltas = [jnp.sum(d.astype(F32) * o_rep, axis=1, keepdims=True) for d in dos]

        def scores(t, width=SB_KT):
            kb = k_ref[_sb_rows(t, width), :].astype(BF16)
            return [lax.dot_general(g, kb, NT, preferred_element_type=F32) for g in qs]

        def value_grads(t, width=SB_KT):
            vb = v_ref[_sb_rows(t, width), :].astype(BF16)
            return [lax.dot_general(d, vb, NT, preferred_element_type=F32) for d in dos]

        def keep(buf, slot, xs):
            for g, x in enumerate(xs):
                buf[slot, g] = x

        def kept(buf, slot):
            return [buf[slot, g] for g in range(SB_BWD_CHAINS)]

        def score_grads(zs, gvs, cs, ces, past, after_first=lambda: None, after_second=lambda: None):
            logs = [_sb_log_terms(z, past) for z in zs]
            splits = [_split_stack(lk) for _, lk in logs]
            ys = [lax.dot_general(x, tri, NN, preferred_element_type=F32) for _, x in splits]
            after_first()
            sums = [_suffix_from(y, blocks, c) for y, (blocks, _), c in zip(ys, splits, cs)]
            wbs = [_sb_weights(lsz, after, past).astype(BF16) for (lsz, _), (after, _) in zip(logs, sums)]
            es = [wb.astype(F32) * gv for wb, gv in zip(wbs, gvs())]
            esplits = [_split_stack(e) for e in es]
            eys = [lax.dot_general(x, tri_incl, NN, preferred_element_type=F32) for _, x in esplits]
            after_second()
            esums = [_suffix_from(y, blocks, ce) for y, (blocks, _), ce in zip(eys, esplits, ces)]
            dzbs = []
            for e, (lsz, _), (suf, _), delta in zip(es, logs, esums, deltas):
                dz = e - jnp.exp(lsz) * (e + (delta - suf))
                dzbs.append((dz if past is None else jnp.where(past, dz, 0.0)).astype(BF16))
            return dzbs, wbs, [c for _, c in sums], [c for _, c in esums]

        def outputs(dq, slot, t):
            rows = _sb_rows(t)
            kb = k_ref[rows, :].astype(BF16)
            dk_t = dv_t = None
            for m, dzb, wb, g, d in zip(masks, kept(dz_buf, slot), kept(w_buf, slot), qts, dots):
                dq = dq + _unstack_heads(lax.dot_general(dzb, kb, NN, preferred_element_type=F32), m)
                a = lax.dot_general(g, dzb, NN, preferred_element_type=F32)
                b = lax.dot_general(d, wb, NN, preferred_element_type=F32)
                dk_t = a if dk_t is None else dk_t + a
                dv_t = b if dv_t is None else dv_t + b
            dk_ref[:, rows] = dk_ref[:, rows] + dk_t
            dv_ref[:, rows] = dv_ref[:, rows] + dv_t
            return dq

        zcol = [jnp.zeros((n_rows, 1), F32)] * SB_BWD_CHAINS
        ahead = jnp.maximum(n_tiles - 2, 0)
        width = Q_BLOCK * (place + 1)
        zs, gvs = scores(n_tiles - 1, width), value_grads(n_tiles - 1, width)
        keep(z_buf, 0, scores(ahead))
        keep(gv_buf, 0, value_grads(ahead))
        dzbs, wbs, cs, ces = score_grads(zs, lambda: gvs, zcol, zcol, _sb_past(place, n_rows))
        keep(dz_buf, 0, [_sb_widen(x) for x in dzbs])
        keep(w_buf, 0, [_sb_widen(x) for x in wbs])

        def step(tt, carry):
            t = n_tiles - 2 - tt
            cur = tt & 1
            ahead = jnp.maximum(t - 1, 0)
            dq = outputs(carry[0], cur, t + 1)
            dzbs, wbs, cs, ces = score_grads(
                kept(z_buf, cur), lambda: kept(gv_buf, cur),
                carry[1:1 + SB_BWD_CHAINS], carry[1 + SB_BWD_CHAINS:], None,
                lambda: keep(z_buf, 1 - cur, scores(ahead)),
                lambda: keep(gv_buf, 1 - cur, value_grads(ahead)))
            keep(dz_buf, 1 - cur, dzbs)
            keep(w_buf, 1 - cur, wbs)
            return (dq, *cs, *ces)

        carry = lax.fori_loop(0, n_tiles - 1, step, (jnp.zeros((Q_BLOCK, D_ATT), F32), *cs, *ces))
        dq_ref[...] = outputs(carry[0], (n_tiles - 1) & 1, 0) * (HEAD_DIM ** -0.5)

    def body(*refs):
        n_tiles = pl.program_id(1) + 1
        dk_ref, dv_ref = refs[6:8]

        @pl.when((pl.program_id(0) == 0) & (n_tiles == 1))
        def _():
            dk_ref[...] = jnp.zeros_like(dk_ref)
            dv_ref[...] = jnp.zeros_like(dv_ref)

        for place in range(SB_PLACES):
            pl.when(pl.program_id(0) == place)(functools.partial(at_place, place, n_tiles, *refs))

    cb = COL_QS // D_ATT
    blk = pl.BlockSpec((Q_BLOCK, D_ATT), lambda p, j: (SB_PLACES * j + p, 0))
    turned = pl.BlockSpec((D_ATT, T), lambda p, j: (0, 0))
    sh = jax.ShapeDtypeStruct((T, D_ATT), F32)
    sh_turned = jax.ShapeDtypeStruct((D_ATT, T), F32)
    kept_f32 = pltpu.VMEM((2, SB_BWD_CHAINS, n_rows, SB_KT), F32)
    kept_bf16 = pltpu.VMEM((2, SB_BWD_CHAINS, n_rows, SB_KT), BF16)
    return pl.pallas_call(
        body, name="sb_bwd", grid=(SB_PLACES, T // SB_KT),
        in_specs=[pl.BlockSpec((Q_BLOCK, D_ATT), lambda p, j: (SB_PLACES * j + p, cb)),
                  pl.BlockSpec((T, D_ATT), lambda p, j: (0, cb + 1)),
                  pl.BlockSpec((T, D_ATT), lambda p, j: (0, cb + 2)), blk, blk],
        out_specs=[blk, turned, turned], out_shape=[sh, sh_turned, sh_turned],
        scratch_shapes=[kept_f32, kept_f32, kept_bf16, kept_bf16],
        compiler_params=_params(dimension_semantics=("arbitrary", "arbitrary")),
    )(proj, proj, proj, do, o)


def _tok(c, by=None):
    if by is None:
        return pl.BlockSpec((TM, c), lambda i, j, k: (i, 0))
    if by == 1:
        return pl.BlockSpec((TM, c), lambda i, j, k: (i, j))
    return pl.BlockSpec((TM, c), lambda i, j, k: (i, k))


def _gain_spec():
    return pl.BlockSpec((1, D_MODEL), lambda i, j, k: (0, 0))


def _wfull(r, c, l):
    return pl.BlockSpec((N_CHIPS, None, r, c), lambda i, j, k: (0, l, 0, 0), pipeline_mode=pl.Buffered(1))


def _pick(idx, c):
    return lambda ins: ins[idx][c]


def _cols(idx, c, w):
    return lambda ins: ins[idx][:, c * w:(c + 1) * w]


def _rows(rows, width):
    return pl.BlockSpec((rows, width), lambda i, j, k: (i, 0))


def _whole(shape):
    return pl.BlockSpec(shape, lambda i, j, k: (0, 0), pipeline_mode=pl.Buffered(1))


def _ffn_fwd(x, gain, wg, wu, wd):
    T = x.shape[0]
    wg, wu = (w.reshape(-1, D_MODEL) for w in (wg, wu))
    ff = wg.shape[0]
    tm = TM // 2
    normed = _normed(0, 3)

    def swiglu(vals, ins, outs, i):
        gt, up = vals
        s = _sigmoid(gt)
        sil = gt * s
        outs[0][...] = sil.astype(BF16)
        outs[1][...] = (up * (s * (1.0 + gt * (1.0 - s)))).astype(BF16)
        outs[2][...] = (sil * up).astype(BF16)
        outs[3][...] = normed(ins)

    ash = jax.ShapeDtypeStruct((T, ff), BF16)
    sil, up_dsil, act, h = _mm(
        "ffn_up", [x, wg, wu, gain], [_rows(tm, D_MODEL), _whole(wg.shape), _whole(wu.shape), _gain_spec()],
        [(normed, 1, 0), (normed, 2, 1)], 2, None, NT, (T // tm, 1, 1), swiglu,
        [ash] * 3 + [jax.ShapeDtypeStruct((T, D_MODEL), BF16)], [_rows(tm, ff)] * 3 + [_rows(tm, D_MODEL)])

    def resid(vals, ins, outs, i):
        outs[0][...] = ins[2][...] + 0.5 * vals[0]

    wd_chips = wd(act) if callable(wd) else wd
    wd = wd_chips.reshape(-1, D_MODEL)
    (y,) = _mm(
        "ffn_down", [act, wd, x], [_rows(TM, ff), _whole(wd.shape), _tok(D_MODEL)], [(0, 1, 0)], 1, None, NN,
        (T // TM, 1, 1), resid, [jax.ShapeDtypeStruct((T, D_MODEL), F32)], [_tok(D_MODEL)])
    return y, (x, h, sil, up_dsil, act), wd_chips


def _ffn_bwd(dxo, gain, wg, wu, wd, saved):
    x, h, sil, up_dsil, act = saved
    T = x.shape[0]
    n_chips, _, ffs, _ = wd.shape
    wg, wu, wd = (w.reshape(-1, D_MODEL) for w in (wg, wu, wd))
    ff = wd.shape[0]
    tk = TM
    tm = TM

    def dswiglu(vals, ins, outs, i):
        da = 0.5 * vals[0]
        outs[0][...] = (da * ins[3][...].astype(F32)).astype(BF16)
        outs[1][...] = (da * ins[2][...].astype(F32)).astype(BF16)

    ash = jax.ShapeDtypeStruct((T, ff), BF16)
    dgate, dup = _mm(
        "ffn_dact", [dxo, wd, sil, up_dsil], [_rows(tm, D_MODEL), _whole(wd.shape), _rows(tm, ff), _rows(tm, ff)],
        [(0, 1, 0)], 1, None, NT, (T // tm, 1, 1), dswiglu, [ash, ash], [_rows(tm, ff)] * 2)

    def half(vals, ins, outs, i):
        outs[0][...] = (0.5 * vals[0]).astype(BF16)

    def cast(vals, ins, outs, i):
        outs[0][...] = vals[0].astype(BF16)

    tok_k = pl.BlockSpec((tk, D_MODEL), lambda i, j, k: (k, 0))
    hid_k = pl.BlockSpec((tk, ff), lambda i, j, k: (k, 0))
    wsh = jax.ShapeDtypeStruct((ff, D_MODEL), BF16)
    (dwd,) = _mm("ffn_dwd", [act, dxo], [hid_k, tok_k], [(0, 1, 0)], 1, (ff, D_MODEL), TN, (1, 1, T // tk), half,
                 [wsh], [_whole((ff, D_MODEL))])

    tx = TM // 2
    dx, dgain = _mm(
        "ffn_dx", [dgate, dup, wg, wu, x, gain, dxo],
        [_rows(tx, ff), _rows(tx, ff), _whole(wg.shape), _whole(wu.shape), _rows(tx, D_MODEL), _gain_spec(),
         _rows(tx, D_MODEL)],
        [(0, 2, 0), (1, 3, 0)], 1, None, NN, (T // tx, 1, 1), _rms_bwd_epilogue(4, 5, 6),
        [jax.ShapeDtypeStruct((T, D_MODEL), F32), jax.ShapeDtypeStruct((8, D_MODEL), F32)],
        [_rows(tx, D_MODEL), pl.BlockSpec((8, D_MODEL), lambda i, j, k: (0, 0))])

    dws = []
    for dact in (dgate, dup):
        dws += _mm("ffn_dwgu", [dact, h], [hid_k, tok_k], [(0, 1, 0)], 1, (ff, D_MODEL), TN, (1, 1, T // tk), cast,
                   [wsh], [_whole((ff, D_MODEL))])
    dwg, dwu, dwd = (w.reshape(n_chips, ffs, D_MODEL) for w in (dws[0], dws[1], dwd))
    return dx, dgain, dwg, dwu, dwd


def _joined_mixer_weights(wpd, wps, wo):
    n, _, r, c = wpd.shape
    wpd_n, wps_n = (w[:, 0].transpose(1, 0, 2).reshape(r, n * c) for w in (wpd, wps))
    return wpd_n, wps_n, wo.reshape(-1, wo.shape[3])


def _mixer_fwd(x, gain, W, l, tabs):
    T = x.shape[0]
    win, wpd, wps, wo = W["w_in"], W["w_proj_dil"], W["w_proj_sb"], W["w_out"]
    cin = win.shape[3]
    cp = wpd.shape[3]
    normed = _normed(0, 5)
    n_rope = 6 * D_ATT

    tm = TM // 2

    def roped(vals, ins, outs, i):
        for j, v in enumerate(vals):
            lo = j * cin
            k = min(max(n_rope - lo, 0), cin)
            if k:
                tab = [jnp.concatenate([ins[t][...]] * (k // 128), axis=1) for t in (2, 3, 4)]
                outs[0][:, lo:lo + k] = _rope_fwd(v[:, :k], *tab)
            if k < cin:
                outs[0][:, lo + k:lo + cin] = v[:, k:]
        outs[1][...] = normed(ins)

    proj, h = _mm(
        "mix_in", [x, win, *tabs, gain],
        [_rows(tm, D_MODEL), _wfull(D_MODEL, cin, l)] + [_rows(tm, 128)] * 3 + [_gain_spec()],
        [(normed, _pick(1, c), c) for c in range(N_CHIPS)], N_CHIPS, None, NN, (T // tm, 1, 1), roped,
        [jax.ShapeDtypeStruct((T, N_CHIPS * cin), F32), jax.ShapeDtypeStruct((T, D_MODEL), BF16)],
        [_rows(tm, N_CHIPS * cin), _rows(tm, D_MODEL)])

    os_, lses = [], []
    for g, (window, dil) in enumerate(DIL_GROUPS):
        o_g, lse_g = _dil_fwd(proj, g, dil)
        os_.append(o_g)
        lses.append(lse_g)
    o_dil, lse = _dil_merge(os_, lses)
    o_sb = _sb_fwd(proj)

    def gated(vals, ins, outs, i):
        pd, ps = vals
        outs[0][...] = (_sigmoid(ins[4][...]) * pd + _sigmoid(ins[5][...]) * ps).astype(BF16)
        outs[1][...] = pd.astype(BF16)
        outs[2][...] = ps.astype(BF16)

    wpd_n, wps_n, wo_n = _joined_mixer_weights(wpd, wps, wo)
    gd_spec = pl.BlockSpec((TM, D_MODEL), lambda i, j, k: (i, COL_GD // D_MODEL))
    gs_spec = pl.BlockSpec((TM, D_MODEL), lambda i, j, k: (i, COL_GS // D_MODEL))
    ush = jax.ShapeDtypeStruct((T, D_MODEL), BF16)
    u, pd, ps = _mm(
        "mix_gate", [o_dil, o_sb, wpd_n, wps_n, proj, proj],
        [_tok(D_ATT), _tok(D_ATT), _whole(wpd_n.shape), _whole(wps_n.shape), gd_spec, gs_spec],
        [(0, 2, 0), (1, 3, 1)], 2, None, NN, (T // TM, 1, 1), gated, [ush] * 3, [_tok(D_MODEL)] * 3)

    def resid(vals, ins, outs, i):
        outs[0][...] = ins[2][...] + vals[0]

    (y,) = _mm(
        "mix_out", [u, wo_n, x], [_tok(D_MODEL), _whole(wo_n.shape), _tok(D_MODEL)], [(0, 1, 0)], 1, None, NN,
        (T // TM, 1, 1), resid, [jax.ShapeDtypeStruct((T, D_MODEL), F32)], [_tok(D_MODEL)])
    return y, (x, h, proj, o_dil, lse, o_sb, u, pd, ps)


def _mixer_bwd(dxo, gain, W, l, tabs, saved):
    x, h, proj, o_dil, lse, o_sb, u, pd, ps = saved
    T = x.shape[0]
    win, wpd, wps, wo = W["w_in"], W["w_proj_dil"], W["w_proj_sb"], W["w_out"]
    cin = win.shape[3]
    cp = wpd.shape[3]
    tk = TM
    tm = TM
    row = pl.BlockSpec((tm, D_MODEL), lambda i, j, k: (i, 0))

    def dgated(vals, ins, outs, i):
        du = vals[0]
        sd = _sigmoid(ins[4][...])
        ss = _sigmoid(ins[5][...])
        outs[0][...] = (du * sd).astype(BF16)
        outs[1][...] = (du * ss).astype(BF16)
        outs[2][...] = (du * ins[2][...].astype(F32) * sd * (1.0 - sd)).astype(BF16)
        outs[3][...] = (du * ins[3][...].astype(F32) * ss * (1.0 - ss)).astype(BF16)

    wpd_n, wps_n, wo_n = _joined_mixer_weights(wpd, wps, wo)
    gd_spec = pl.BlockSpec((TM, D_MODEL), lambda i, j, k: (i, COL_GD // D_MODEL))
    gs_spec = pl.BlockSpec((TM, D_MODEL), lambda i, j, k: (i, COL_GS // D_MODEL))
    ush = jax.ShapeDtypeStruct((T, D_MODEL), BF16)
    dpd, dps, dgd, dgs = _mm(
        "mix_du", [dxo, wo_n, pd, ps, proj, proj],
        [_tok(D_MODEL), _whole(wo_n.shape), _tok(D_MODEL), _tok(D_MODEL), gd_spec, gs_spec],
        [(0, 1, 0)], 1, None, NT, (T // TM, 1, 1), dgated, [ush] * 4, [_tok(D_MODEL)] * 4)

    def one(vals, ins, outs, i):
        outs[0][...] = vals[0].astype(BF16)

    def two(vals, ins, outs, i):
        outs[0][...] = vals[0].astype(BF16)
        outs[1][...] = vals[1].astype(BF16)

    tok_k = pl.BlockSpec((tk, D_MODEL), lambda i, j, k: (k, 0))
    att_k = pl.BlockSpec((tk, D_ATT), lambda i, j, k: (k, 0))
    (dwo_n,) = _mm("mix_dwo", [u, dxo], [tok_k, tok_k], [(0, 1, 0)], 1, (D_MODEL, D_MODEL), TN, (1, 1, T // tk), one,
                   [jax.ShapeDtypeStruct((D_MODEL, D_MODEL), BF16)], [_whole((D_MODEL, D_MODEL))])

    def plain2(vals, ins, outs, i):
        outs[0][...] = vals[0]
        outs[1][...] = vals[1]

    ash = jax.ShapeDtypeStruct((T, D_ATT), F32)
    do_dil, do_sb = _mm(
        "mix_do", [dpd, dps, wpd_n, wps_n], [_tok(D_MODEL), _tok(D_MODEL), _whole(wpd_n.shape), _whole(wps_n.shape)],
        [(0, 2, 0), (1, 3, 1)], 2, None, NT, (T // TM, 1, 1), plain2, [ash, ash], [_tok(D_ATT)] * 2)

    psh = jax.ShapeDtypeStruct((D_ATT, D_MODEL), BF16)
    dwpd_n, dwps_n = _mm(
        "mix_dwp", [o_dil, o_sb, dpd, dps], [att_k, att_k, tok_k, tok_k], [(0, 2, 0), (1, 3, 1)], 2,
        (D_ATT, D_MODEL), TN, (1, 1, T // tk), two, [psh, psh], [_whole((D_ATT, D_MODEL))] * 2)
    dwpd, dwps = (w.reshape(D_ATT, N_CHIPS, cp).transpose(1, 0, 2) for w in (dwpd_n, dwps_n))
    dwo = dwo_n.reshape(N_CHIPS, cp, D_MODEL)

    dqs, dks, dvs = [], [], []
    for g, (window, dil) in enumerate(DIL_GROUPS):
        dq, dk, dv = _dil_bwd(proj, do_dil, o_dil, lse, g, dil)
        dqs.append(dq)
        dks.append(dk)
        dvs.append(dv)
    dq_s, dk_s, dv_s = _sb_bwd(proj, do_sb, o_sb)
    dproj = _assemble_dproj(dqs + dks, dvs + [dq_s], [dk_s, dv_s], [dgd, dgs], tabs)

    dx, dgain = _mm(
        "mix_dx", [dproj, win, x, gain, dxo],
        [pl.BlockSpec((tm, N_CHIPS * cin), lambda i, j, k: (i, 0)), _wfull(D_MODEL, cin, l), row, _gain_spec(), row],
        [(_cols(0, c, cin), _pick(1, c), 0) for c in range(N_CHIPS)], 1, None, NT, (T // tm, 1, 1),
        _rms_bwd_epilogue(2, 3, 4),
        [jax.ShapeDtypeStruct((T, D_MODEL), F32), jax.ShapeDtypeStruct((8, D_MODEL), F32)],
        [row, pl.BlockSpec((8, D_MODEL), lambda i, j, k: (0, 0))])

    (dwin,) = _mm(
        "mix_dwin", [h, dproj],
        [pl.BlockSpec((tk, D_MODEL), lambda i, j, k: (k, 0)), pl.BlockSpec((tk, cin), lambda i, j, k: (k, j))],
        [(0, 1, 0)], 1, (D_MODEL, cin), TN, (1, N_CHIPS, T // tk), one,
        [jax.ShapeDtypeStruct((N_CHIPS, D_MODEL, cin), BF16)],
        [pl.BlockSpec((None, D_MODEL, cin), lambda i, j, k: (j, 0, 0))])
    return dx, dgain, dwin, dwpd, dwps, dwo


def _local_step(x, target, norms, norm_final, weights_of, on_grads):
    T = x.shape[0]
    tabs = _rope_tables(T)
    saved, held = [], []
    for l in range(DEPTH):
        w1 = weights_of(l, 0, x)
        x, s1, w1["ffn1_w_down"] = _ffn_fwd(x, norms["norm_ffn1"][l:l + 1], w1["ffn1_w_gate"], w1["ffn1_w_up"],
                                             w1["ffn1_w_down"])
        w2 = weights_of(l, 1, x)
        x, s2 = _mixer_fwd(x, norms["norm_mix"][l:l + 1], w2, 0, tabs)
        w3 = weights_of(l, 2, x)
        x, s3, _ = _ffn_fwd(x, norms["norm_ffn2"][l:l + 1], w3["ffn2_w_gate"], w3["ffn2_w_up"], w3["ffn2_w_down"])
        saved.append((s1, s2, s3))
        held.append((w1, w2, w3))
    dx, dg_final, loss = _final_loss(x, norm_final.reshape(1, D_MODEL), target)
    gains = [None] * DEPTH
    for l in reversed(range(DEPTH)):
        s1, s2, s3 = saved[l]
        w1, w2, w3 = held[l]
        dx, dg2, dwg2, dwu2, dwd2 = _ffn_bwd(dx, norms["norm_ffn2"][l:l + 1], w3["ffn2_w_gate"], w3["ffn2_w_up"],
                                             w3["ffn2_w_down"], s3)
        dx = on_grads(l, 2, dict(ffn2_w_gate=dwg2, ffn2_w_up=dwu2, ffn2_w_down=dwd2), dx)
        dx, dgm, dwin, dwpd, dwps, dwo = _mixer_bwd(dx, norms["norm_mix"][l:l + 1], w2, 0, tabs, s2)
        dx = on_grads(l, 1, dict(w_in=dwin, w_proj_dil=dwpd, w_proj_sb=dwps, w_out=dwo), dx)
        dx, dg1, dwg1, dwu1, dwd1 = _ffn_bwd(dx, norms["norm_ffn1"][l:l + 1], w1["ffn1_w_gate"], w1["ffn1_w_up"],
                                             w1["ffn1_w_down"], s1)
        dx = on_grads(l, 0, dict(ffn1_w_gate=dwg1, ffn1_w_up=dwu1, ffn1_w_down=dwd1), dx)
        gains[l] = dict(norm_ffn1=dg1, norm_mix=dgm, norm_ffn2=dg2)
    return loss, dx, gains, dg_final


def _place():
    x, y, c = lax.axis_index("x"), lax.axis_index("y"), lax.axis_index("c")
    chips = [(1 - x, y), (x, 1 - y), (1 - x, 1 - y)]
    return x, y, c, chips


def _half(c, r):
    return pl.ds(pl.multiple_of(c * (r // 2), 8), r // 2)


def _cast_into_slot(ws, ls, me_arr, after):
    n = len(ws)
    late = [] if after is None else [after]

    def body(me_ref, *refs):
        for a in range(n):
            refs[len(refs) - n + a][...] = refs[a][...].astype(BF16)

    def src(w, l):
        return pl.BlockSpec((None, w.shape[1] // 4, w.shape[2]), lambda i, me: (l, i, 0))

    def dst(w):
        return pl.BlockSpec((None, None, w.shape[1] // 4, w.shape[2]), lambda i, me: (me[0], 0, i, 0))

    return pl.pallas_call(
        body, name="cast_weights",
        grid_spec=pltpu.PrefetchScalarGridSpec(
            num_scalar_prefetch=1, grid=(4,),
            in_specs=[src(w, l) for w, l in zip(ws, ls)] + [pl.BlockSpec(memory_space=pl.ANY)] * len(late),
            out_specs=[dst(w) for w in ws]),
        out_shape=[jax.ShapeDtypeStruct((N_CHIPS, 1) + w.shape[1:], BF16) for w in ws], compiler_params=_params(),
    )(me_arr, *ws, *late)


HBM_SPEC = pl.BlockSpec(memory_space=pltpu.HBM)
SEM_SPEC = pl.BlockSpec(memory_space=pltpu.SEMAPHORE)
SPLIT_COPY = pltpu.CompilerParams(has_side_effects=pltpu.SideEffectType.DATAFLOW_SIDE_EFFECTING)


def _gather_piece(ref, chip_id, c):
    return ref.at[chip_id, 0, _half(c, ref.shape[2]), :]


def _gather_start(tag, bufs, direct):
    n = len(bufs)

    def body(*refs):
        out_refs = refs[n:2 * n]
        send_sems, recv_sems, token = refs[2 * n:]
        x, y, c, chips = _place()
        me = 2 * x + y
        for a in range(n):
            piece = _gather_piece(out_refs[a], me, c)
            for j, chip in enumerate(chips):
                for to in ((0, 1) if direct[a] else (c,)):
                    pltpu.make_async_remote_copy(
                        src_ref=piece, dst_ref=piece, send_sem=send_sems.at[6 * a + 2 * j + to],
                        recv_sem=recv_sems.at[6 * a + 2 * j + c], device_id=(*chip, to), device_id_type=MESH).start()
        token[...] = jnp.zeros_like(token)

    outs = pl.pallas_call(
        body, name=f"gather_start_{tag}", in_specs=[HBM_SPEC] * n,
        out_specs=[HBM_SPEC] * n + [SEM_SPEC, SEM_SPEC, pl.BlockSpec(memory_space=pltpu.VMEM)],
        out_shape=[pltpu.HBM(b.shape, b.dtype) for b in bufs] + [pltpu.SemaphoreType.DMA((6 * n,))] * 2
        + [jax.ShapeDtypeStruct((8, 128), F32)],
        input_output_aliases={a: a for a in range(n)}, compiler_params=SPLIT_COPY,
    )(*[pltpu.with_memory_space_constraint(b, pltpu.HBM) for b in bufs])
    return outs[:n], outs[n], outs[n + 1], outs[n + 2]


def _gather_wait(k, bufs, places, direct, send_sems, recv_sems, after):
    m = len(bufs)

    def body(*refs):
        in_refs = refs[:m]
        ssem, rsem = refs[m], refs[m + 1]
        x, y, c, chips = _place()
        me = 2 * x + y
        for t, a in enumerate(places):
            for j, chip in enumerate(chips):
                for core in ((0, 1) if direct else (c,)):
                    cp = pltpu.make_async_remote_copy(
                        src_ref=_gather_piece(in_refs[t], me, c),
                        dst_ref=_gather_piece(in_refs[t], 2 * chip[0] + chip[1], core),
                        send_sem=ssem.at[6 * a + 2 * j + core], recv_sem=rsem.at[6 * a + 2 * j + core],
                        device_id=(*chip, core), device_id_type=MESH)
                    cp.wait_send()
                    cp.wait_recv()

    return pl.pallas_call(
        body, name=f"gather_wait_{k}",
        in_specs=[HBM_SPEC] * m + [SEM_SPEC, SEM_SPEC, pl.BlockSpec(memory_space=pl.ANY)], out_specs=[HBM_SPEC] * m,
        out_shape=[pltpu.HBM(b.shape, b.dtype) for b in bufs], input_output_aliases={t: t for t in range(m)},
        compiler_params=SPLIT_COPY,
    )(*bufs, send_sems, recv_sems, after)


def _gather_relay(bufs):
    n = len(bufs)

    def body(*refs):
        out_refs = refs[n:2 * n]
        send_sems, recv_sems = refs[2 * n:]
        x, y, c, chips = _place()
        cps = []
        for a in range(n):
            for j, chip in enumerate(chips):
                piece = _gather_piece(out_refs[a], 2 * chip[0] + chip[1], c)
                cps.append(pltpu.make_async_remote_copy(
                    src_ref=piece, dst_ref=piece, send_sem=send_sems.at[a, j], recv_sem=recv_sems.at[a, j],
                    device_id=(x, y, 1 - c), device_id_type=MESH))
        for cp in cps:
            cp.start()
        for a in range(n):
            for j, chip in enumerate(chips):
                theirs = _gather_piece(out_refs[a], 2 * chip[0] + chip[1], 1 - c)
                pltpu.make_async_remote_copy(
                    src_ref=theirs, dst_ref=theirs, send_sem=send_sems.at[a, j], recv_sem=recv_sems.at[a, j],
                    device_id=(x, y, 1 - c), device_id_type=MESH).wait_recv()
        for cp in cps:
            cp.wait_send()

    any_spec = pl.BlockSpec(memory_space=pl.ANY)
    return pl.pallas_call(
        body, name="gather_relay", in_specs=[any_spec] * n, out_specs=[any_spec] * n,
        out_shape=[jax.ShapeDtypeStruct(b.shape, b.dtype) for b in bufs],
        input_output_aliases={a: a for a in range(n)},
        scratch_shapes=[pltpu.SemaphoreType.DMA((n, 3))] * 2,
    )(*bufs)


def _other_half(ref, c):
    return ref.at[:, _half(1 - c, ref.shape[1]), :]


def _all_of(ref, c):
    return ref


def _sibling_start(name, srcs, pick, land_shapes, thru):
    n = len(srcs)
    lands = [lax.empty(sh, s.dtype) for sh, s in zip(land_shapes, srcs)]
    kept = lands + ([] if thru is None else [thru])
    m = len(kept)

    def body(*refs):
        s_refs, land_refs = refs[:n], refs[n + m:n + m + n]
        send_sems, recv_sems, token = refs[n + 2 * m:]
        x, y, c, _ = _place()
        for a in range(n):
            pltpu.make_async_remote_copy(
                src_ref=pick(s_refs[a], c), dst_ref=land_refs[a], send_sem=send_sems.at[a],
                recv_sem=recv_sems.at[a], device_id=(x, y, 1 - c), device_id_type=MESH).start()
        token[...] = jnp.zeros_like(token)

    outs = pl.pallas_call(
        body, name=name, in_specs=[HBM_SPEC] * (n + m),
        out_specs=[HBM_SPEC] * m + [SEM_SPEC, SEM_SPEC, pl.BlockSpec(memory_space=pltpu.VMEM)],
        out_shape=[pltpu.HBM(v.shape, v.dtype) for v in kept] + [pltpu.SemaphoreType.DMA((n,))] * 2
        + [jax.ShapeDtypeStruct((8, 128), F32)],
        input_output_aliases={n + a: a for a in range(m)}, compiler_params=SPLIT_COPY,
    )(*[pltpu.with_memory_space_constraint(v, pltpu.HBM) for v in list(srcs) + kept])
    return (outs[:n], outs[m], outs[m + 1]), (outs[n] if thru is not None else None), outs[m + 2]


def _sibling_wait(name, srcs, pick, lands, send_sems, recv_sems, after):
    n = len(srcs)

    def body(*refs):
        s_refs, land_refs = refs[:n], refs[n:2 * n]
        ssem, rsem = refs[2 * n], refs[2 * n + 1]
        x, y, c, _ = _place()
        for a in range(n):
            cp = pltpu.make_async_remote_copy(
                src_ref=pick(s_refs[a], c), dst_ref=land_refs[a], send_sem=ssem.at[a], recv_sem=rsem.at[a],
                device_id=(x, y, 1 - c), device_id_type=MESH)
            cp.wait_send()
            cp.wait_recv()

    return pl.pallas_call(
        body, name=name, in_specs=[HBM_SPEC] * (2 * n) + [SEM_SPEC, SEM_SPEC, pl.BlockSpec(memory_space=pl.ANY)],
        out_specs=[HBM_SPEC] * n, out_shape=[pltpu.HBM(v.shape, v.dtype) for v in lands],
        input_output_aliases={n + a: a for a in range(n)}, compiler_params=SPLIT_COPY,
    )(*srcs, *lands, send_sems, recv_sems, after)


def _add_half(gs, gots, c_arr):
    n = len(gs)

    def body(c_ref, *refs):
        for a in range(n):
            refs[2 * n + a][...] = (refs[a][...].astype(F32) + refs[n + a][...].astype(F32)).astype(BF16)

    def own(g):
        return pl.BlockSpec((None, g.shape[1] // 2, g.shape[2]), lambda k, cr: (k, cr[0], 0))

    def half(g):
        return pl.BlockSpec((None, g.shape[1] // 2, g.shape[2]), lambda k, cr: (k, 0, 0))

    return pl.pallas_call(
        body, name="grad_add_half",
        grid_spec=pltpu.PrefetchScalarGridSpec(
            num_scalar_prefetch=1, grid=(N_CHIPS,),
            in_specs=[own(g) for g in gs] + [half(g) for g in gs], out_specs=[half(g) for g in gs]),
        out_shape=[jax.ShapeDtypeStruct(got.shape, BF16) for got in gots], compiler_params=_params(),
    )(c_arr, *gs, *gots)


def _scatter_start(k, ss, thru):
    n = len(ss)

    def body(*refs):
        s_refs, land_refs = refs[2 * n + 1:3 * n + 1], refs[3 * n + 1:4 * n + 1]
        send_sems, recv_sems = refs[4 * n + 2:]
        x, y, c, chips = _place()
        me = 2 * x + y
        for a in range(n):
            for j, chip in enumerate(chips):
                pltpu.make_async_remote_copy(
                    src_ref=s_refs[a].at[2 * chip[0] + chip[1]], dst_ref=land_refs[a].at[me],
                    send_sem=send_sems.at[3 * a + j], recv_sem=recv_sems.at[3 * a + j], device_id=(*chip, c),
                    device_id_type=MESH).start()

    lands = [lax.empty(s.shape, s.dtype) for s in ss]
    hbm = [pltpu.HBM(s.shape, s.dtype) for s in ss]
    outs = pl.pallas_call(
        body, name=f"grad_scatter_start_{k}", in_specs=[HBM_SPEC] * (2 * n + 1),
        out_specs=[HBM_SPEC] * (2 * n + 1) + [SEM_SPEC, SEM_SPEC],
        out_shape=hbm + hbm + [pltpu.HBM(thru.shape, thru.dtype)] + [pltpu.SemaphoreType.DMA((3 * n,))] * 2,
        input_output_aliases={a: a for a in range(2 * n + 1)}, compiler_params=SPLIT_COPY,
    )(*[pltpu.with_memory_space_constraint(v, pltpu.HBM) for v in list(ss) + lands + [thru]])
    return (outs[:n], outs[n:2 * n], outs[2 * n + 1], outs[2 * n + 2]), outs[2 * n]


def _scatter_wait(k, ss, lands, send_sems, recv_sems, after):
    n = len(ss)

    def body(*refs):
        s_refs, land_refs = refs[:n], refs[n:2 * n]
        ssem, rsem = refs[2 * n], refs[2 * n + 1]
        x, y, c, chips = _place()
        me = 2 * x + y
        for a in range(n):
            for j, chip in enumerate(chips):
                cid = 2 * chip[0] + chip[1]
                cp = pltpu.make_async_remote_copy(
                    src_ref=s_refs[a].at[cid], dst_ref=land_refs[a].at[cid], send_sem=ssem.at[3 * a + j],
                    recv_sem=rsem.at[3 * a + j], device_id=(*chip, c), device_id_type=MESH)
                cp.wait_send()
                cp.wait_recv()

    hbm = [pltpu.HBM(s.shape, s.dtype) for s in ss]
    outs = pl.pallas_call(
        body, name=f"grad_scatter_wait_{k}",
        in_specs=[HBM_SPEC] * (2 * n) + [SEM_SPEC, SEM_SPEC, pl.BlockSpec(memory_space=pl.ANY)],
        out_specs=[HBM_SPEC] * (2 * n), out_shape=hbm + hbm,
        input_output_aliases={a: a for a in range(2 * n)}, compiler_params=SPLIT_COPY,
    )(*ss, *lands, send_sems, recv_sems, after)
    return outs[:n], outs[n:]


def _sum_chips(lands, ss, me_arr):
    n = len(lands)

    def body(me_ref, *refs):
        for own in range(N_CHIPS):
            @pl.when(me_ref[0] == own)
            def _(own=own):
                for a in range(n):
                    acc = None
                    for k in range(N_CHIPS):
                        term = (refs[n + a][...] if k == own else refs[a][k]).astype(F32)
                        acc = term if acc is None else acc + term
                    refs[2 * n + a][...] = acc

    return pl.pallas_call(
        body, name="grad_sum_chips",
        grid_spec=pltpu.PrefetchScalarGridSpec(
            num_scalar_prefetch=1, grid=(1,),
            in_specs=[pl.BlockSpec(la.shape, lambda i, me: (0, 0, 0)) for la in lands]
            + [pl.BlockSpec((None,) + la.shape[1:], lambda i, me: (me[0], 0, 0)) for la in lands],
            out_specs=[pl.BlockSpec(la.shape[1:], lambda i, me: (0, 0)) for la in lands]),
        out_shape=[jax.ShapeDtypeStruct(la.shape[1:], F32) for la in lands], compiler_params=_params(),
    )(me_arr, *lands, *ss)


def _allreduce_rows(stats):
    def body(s_ref, o_ref, buf, send_sems, recv_sems):
        x, y, c, _ = _place()
        me = 4 * x + 2 * y + c
        buf[me] = s_ref[...]
        cps = []
        for k in range(1, 8):
            px = jnp.where(k & 4, 1 - x, x)
            py = jnp.where(k & 2, 1 - y, y)
            pc = jnp.where(k & 1, 1 - c, c)
            cps.append(pltpu.make_async_remote_copy(
                src_ref=s_ref, dst_ref=buf.at[me], send_sem=send_sems.at[k - 1], recv_sem=recv_sems.at[k - 1],
                device_id=(px, py, pc), device_id_type=MESH))
        for cp in cps:
            cp.start()
        for cp in cps:
            cp.wait()
        acc = buf[0]
        for d in range(1, 8):
            acc = acc + buf[d]
        o_ref[...] = acc

    vm = pl.BlockSpec(memory_space=pltpu.VMEM)
    return pl.pallas_call(
        body, name="allreduce_rows", in_specs=[vm], out_specs=vm,
        out_shape=jax.ShapeDtypeStruct(stats.shape, F32),
        scratch_shapes=[pltpu.VMEM((8,) + stats.shape, F32), pltpu.SemaphoreType.DMA((7,)),
                        pltpu.SemaphoreType.DMA((7,))],
    )(stats)


def _adamw_math(w, g, m, v):
    m = ADAM_B1 * m + (1.0 - ADAM_B1) * g
    v = ADAM_B2 * v + (1.0 - ADAM_B2) * (g * g)
    m_hat = m / (1.0 - ADAM_B1 ** ADAM_STEP)
    v_hat = v / (1.0 - ADAM_B2 ** ADAM_STEP)
    delta = -ADAM_LR * (m_hat / (jnp.sqrt(v_hat) + ADAM_EPS) + ADAM_WD * w)
    return delta, m, v


def _adamw(ws, ms, vs, mines, theirs, l, c_arr, earlier, after):
    n = len(ws)
    held = [t for e in earlier if e is not None for t in e]
    assert len(held) in (0, 4 * n)
    late = [] if after is None else [after]

    def body(c_ref, *refs):
        outs = refs[len(refs) - 4 * n:]
        for a in range(n):
            w_ref, m_ref, v_ref, a_ref, b_ref = refs[5 * a:5 * a + 5]
            g = jnp.where(pl.program_id(0) == c_ref[0], a_ref[...], b_ref[...])
            delta, mn, vn = _adamw_math(w_ref[...], g, m_ref[...], v_ref[...])
            outs[4 * a][...] = g
            outs[4 * a + 1][...] = delta
            outs[4 * a + 2][...] = mn
            outs[4 * a + 3][...] = vn

    def blk(w):
        tr = w.shape[1] // 4
        return pl.BlockSpec((None, tr, w.shape[2]), lambda hh, i, cr: (l, 2 * hh + i, 0))

    def half(w):
        return pl.BlockSpec((w.shape[1] // 4, w.shape[2]), lambda hh, i, cr: (i, 0))

    outs = pl.pallas_call(
        body, name="adamw",
        grid_spec=pltpu.PrefetchScalarGridSpec(
            num_scalar_prefetch=1, grid=(2, 2),
            in_specs=[sp for w in ws for sp in (blk(w), blk(w), blk(w), half(w), half(w))]
            + [pl.BlockSpec(memory_space=pl.ANY)] * (len(held) + len(late)),
            out_specs=[blk(w) for w in ws for _ in range(4)]),
        out_shape=[jax.ShapeDtypeStruct(w.shape, F32) for w in ws for _ in range(4)],
        input_output_aliases={1 + 5 * n + t: t for t in range(len(held))}, compiler_params=_params(),
    )(c_arr, *[t for grp in zip(ws, ms, vs, mines, theirs) for t in grp], *held, *late)
    return [outs[4 * a:4 * a + 4] for a in range(n)]


def _adamw_rows(w, m, v, g):
    def body(w_ref, m_ref, v_ref, g_ref, d_ref, mo_ref, vo_ref):
        delta, mn, vn = _adamw_math(w_ref[...], g_ref[...], m_ref[...], v_ref[...])
        d_ref[...] = delta
        mo_ref[...] = mn
        vo_ref[...] = vn

    vm = pl.BlockSpec(memory_space=pltpu.VMEM)
    sh = jax.ShapeDtypeStruct(w.shape, F32)
    return pl.pallas_call(body, name="adamw_rows", in_specs=[vm] * 4, out_specs=[vm] * 3, out_shape=[sh] * 3)(w, m, v, g)


SUBLAYERS = (("ffn1_w_gate", "ffn1_w_up", "ffn1_w_down"), ("w_in", "w_proj_dil", "w_proj_sb", "w_out"),
             ("ffn2_w_gate", "ffn2_w_up", "ffn2_w_down"))
TRANSPOSED = ("ffn1_w_gate", "ffn1_w_up", "ffn2_w_gate", "ffn2_w_up")


def _pick_row(blocks):
    row = lax.broadcasted_iota(jnp.int32, (8, D_MODEL), 0)
    out = jnp.zeros((8, D_MODEL), F32)
    for i, b in enumerate(blocks):
        out = out + jnp.where(row == i, b, 0.0)
    return out


def kernel(x, norm_ffn1, ffn1_w_gate, ffn1_w_up, ffn1_w_down, norm_mix, w_in, w_proj_dil, w_proj_sb, w_out, norm_ffn2, ffn2_w_gate, ffn2_w_up, ffn2_w_down, norm_final, loss_target, m_norm_ffn1, m_ffn1_w_gate, m_ffn1_w_up, m_ffn1_w_down, m_norm_mix, m_w_in, m_w_proj_dil, m_w_proj_sb, m_w_out, m_norm_ffn2, m_ffn2_w_gate, m_ffn2_w_up, m_ffn2_w_down, m_norm_final, v_norm_ffn1, v_ffn1_w_gate, v_ffn1_w_up, v_ffn1_w_down, v_norm_mix, v_w_in, v_w_proj_dil, v_w_proj_sb, v_w_out, v_norm_ffn2, v_ffn2_w_gate, v_ffn2_w_up, v_ffn2_w_down, v_norm_final):
    given = dict(locals())
    for n in TRANSPOSED:
        for k in ("", "m_", "v_"):
            given[k + n] = jnp.swapaxes(given[k + n], 1, 2)
    weights = {n: given[n] for n in WEIGHT_NAMES}
    norms = {n: given[n] for n in NORM_NAMES}

    c_arr = lax.axis_index("c").astype(jnp.int32).reshape(1)
    me_arr = (2 * lax.axis_index("x") + lax.axis_index("y")).astype(jnp.int32).reshape(1)
    order = [(l, s, n) for l in range(DEPTH) for s in range(len(SUBLAYERS)) for n in SUBLAYERS[s]]
    n_first = len(SUBLAYERS[0])
    sent, token = {}, None
    for tag, idxs in (("a", range(n_first)), ("b", range(n_first, len(order)))):
        cast = _cast_into_slot([weights[order[i][2]] for i in idxs], [order[i][0] for i in idxs], me_arr, token)
        bufs, send_sems, recv_sems, token = _gather_start(tag, cast, [order[i][0] > 0 for i in idxs])
        for p, i in enumerate(idxs):
            sent[i] = (bufs[p], p, send_sems, recv_sems)

    def arrived(tag, idxs, after):
        direct = order[idxs[0]][0] > 0
        got = _gather_wait(tag, [sent[i][0] for i in idxs], [sent[i][1] for i in idxs], direct,
                           sent[idxs[0]][2], sent[idxs[0]][3], after)
        return {order[i][2]: g for i, g in zip(idxs, got if direct else _gather_relay(got))}

    def weights_of(l, s, after):
        idxs = [i for i, (ll, ss, _) in enumerate(order) if (ll, ss) == (l, s)]
        k = len(SUBLAYERS) * l + s
        if k > 0:
            return arrived(k, idxs, after)
        last = order[idxs[-1]][2]
        got = arrived(f"{k}_first", idxs[:-1], after)
        got[last] = lambda after: arrived(f"{k}_last", idxs[-1:], after)[last]
        return got

    out = {}
    to_add, in_flight = [], []

    def add_and_scatter(after):
        l, s, names, gs, lands, ssem, rsem = to_add.pop(0)
        k = len(SUBLAYERS) * l + s
        got = _sibling_wait(f"grad_exchange_wait_{k}", gs, _other_half, lands, ssem, rsem, after)
        sent, after = _scatter_start(k, _add_half(gs, got, c_arr), after)
        in_flight.append((l, s, names) + sent)
        return after

    def on_grads(l, s, grads, after):
        names = list(grads)
        k = len(SUBLAYERS) * l + s
        gs = [grads[n] for n in names]
        sent, after, _ = _sibling_start(f"grad_exchange_start_{k}", gs, _other_half,
                                        [(g.shape[0], g.shape[1] // 2, g.shape[2]) for g in gs], after)
        if to_add:
            after = add_and_scatter(after)
        to_add.append((l, s, names, gs) + sent)
        return after

    loss_blk, grad_x, gains, dg_final = _local_step(x[0], loss_target[0], norms, norm_final, weights_of, on_grads)
    grad_x = add_and_scatter(grad_x)

    def update(l, names, mine, swap, after):
        theirs = _sibling_wait(f"grad_swap_wait_{l}_{names[0]}", mine, _all_of, *swap, grad_x if after is None else after)
        res = _adamw([weights[n] for n in names], [given["m_" + n] for n in names], [given["v_" + n] for n in names],
                     mine, theirs, l, c_arr, [out.get(n) for n in names], after)
        out.update(zip(names, res))

    waiting = None
    for l, s, names, sums, lands, ssem, rsem in in_flight:
        sums, lands = _scatter_wait(len(SUBLAYERS) * l + s, sums, lands, ssem, rsem, grad_x)
        mine = _sum_chips(lands, sums, me_arr)
        swap, _, token = _sibling_start(f"grad_swap_start_{l}_{names[0]}", mine, _all_of,
                                        [m.shape for m in mine], None)
        if waiting is not None:
            update(*waiting, token)
        waiting = (l, names, mine, swap)
    update(*waiting, None)
    out = {k + n: (jnp.swapaxes(v, 1, 2) if n in TRANSPOSED else v)
           for n, res in out.items() for k, v in zip(("grad_", "delta_", "new_m_", "new_v_"), res)}
    out["grad_x"] = grad_x[None]

    rows = [gains[l][n] for n in NORM_NAMES for l in range(DEPTH)] + [dg_final, loss_blk]
    total = _allreduce_rows(_pick_row(rows))
    out["loss"] = total[7, 0]
    wn = jnp.concatenate([given[n] for n in NORM_NAMES] + [norm_final[None], jnp.zeros((1, D_MODEL), F32)])
    mn_ = jnp.concatenate([given["m_" + n] for n in NORM_NAMES] + [m_norm_final[None], jnp.zeros((1, D_MODEL), F32)])
    vn_ = jnp.concatenate([given["v_" + n] for n in NORM_NAMES] + [v_norm_final[None], jnp.ones((1, D_MODEL), F32)])
    d_n, m_n, v_n = _adamw_rows(wn, mn_, vn_, total)
    for i, n in enumerate(NORM_NAMES):
        sl = slice(i * DEPTH, (i + 1) * DEPTH)
        out["grad_" + n], out["delta_" + n], out["new_m_" + n], out["new_v_" + n] = total[sl], d_n[sl], m_n[sl], v_n[sl]
    out["grad_norm_final"], out["delta_norm_final"] = total[6], d_n[6]
    out["new_m_norm_final"], out["new_v_norm_final"] = m_n[6], v_n[6]

    names = ["norm_ffn1", "ffn1_w_gate", "ffn1_w_up", "ffn1_w_down", "norm_mix", "w_in", "w_proj_dil", "w_proj_sb",
             "w_out", "norm_ffn2", "ffn2_w_gate", "ffn2_w_up", "ffn2_w_down", "norm_final"]
    return (out["loss"], out["grad_x"], *[out["grad_" + n] for n in names], *[out["delta_" + n] for n in names],
            *[out["new_m_" + n] for n in names], *[out["new_v_" + n] for n in names])
```

```python
import functools

import jax
import jax.numpy as jnp
from jax import lax
from jax.experimental import pallas as pl
from jax.experimental.pallas import tpu as pltpu

F32 = jnp.float32
BF16 = jnp.bfloat16

D_MODEL = 1024
DEPTH = 2
N_CHIPS = 4
HEAD_DIM = 64
ROPE_DIM = 16
ROPE_THETA = 500000.0
DIL_GROUPS = ((128, 1), (512, 4), (2048, 16))
SPAN = 128
Q_BLOCK = 128
RMS_EPS = 1e-6
D_ATT = 256
COL_QS = 2304
COL_GD = 3072
COL_GS = 4096
ADAM_LR, ADAM_B1, ADAM_B2, ADAM_EPS, ADAM_WD, ADAM_STEP = 0.001, 0.9, 0.999, 1e-08, 0.01, 10

VMEM_LIMIT = 52 * 1024 * 1024
TM = 512
NEG = -1e30

NN = (((1,), (0,)), ((), ()))
NT = (((1,), (1,)), ((), ()))
TN = (((0,), (0,)), ((), ()))
MESH = pl.DeviceIdType.MESH

WEIGHT_NAMES = ("ffn1_w_gate", "ffn1_w_up", "ffn1_w_down", "w_in", "w_proj_dil",
                "w_proj_sb", "w_out", "ffn2_w_gate", "ffn2_w_up", "ffn2_w_down")
NORM_NAMES = ("norm_ffn1", "norm_mix", "norm_ffn2")


def _params(**kw):
    return pltpu.CompilerParams(vmem_limit_bytes=VMEM_LIMIT, **kw)


def _sigmoid(x):
    return 0.5 * jnp.tanh(0.5 * x) + 0.5


def _mm_body(pairs, n_in, n_out, n_acc, dims, nk, epilogue, *refs):
    ins = refs[:n_in]
    outs = refs[n_in:n_in + n_out]
    accs = refs[n_in + n_out:]
    i = pl.program_id(0)
    k = pl.program_id(2)

    def operand(a):
        return (a(ins) if callable(a) else ins[a][...]).astype(BF16)

    def dot(ia, ib):
        return lax.dot_general(operand(ia), operand(ib), dims, preferred_element_type=F32)

    if nk == 1:
        parts = [None] * n_acc
        for ia, ib, ic in pairs:
            parts[ic] = dot(ia, ib) if parts[ic] is None else parts[ic] + dot(ia, ib)
        epilogue(parts, ins, outs, i)
        return

    @pl.when(k == 0)
    def _():
        for c in range(n_acc):
            accs[c][...] = jnp.zeros_like(accs[c])

    for ia, ib, ic in pairs:
        accs[ic][...] += dot(ia, ib)

    @pl.when(k == nk - 1)
    def _():
        epilogue([a[...] for a in accs], ins, outs, i)


def _mm(name, ins, in_specs, pairs, n_acc, acc_shape, dims, grid, epilogue, out_shapes, out_specs):
    nk = grid[2]
    scratch = [pltpu.VMEM(acc_shape, F32) for _ in range(n_acc)] if nk > 1 else []
    body = functools.partial(_mm_body, tuple(pairs), len(ins), len(out_shapes), n_acc, dims, nk, epilogue)
    return pl.pallas_call(
        body, name=name, grid=grid, in_specs=in_specs, out_specs=out_specs, out_shape=out_shapes,
        scratch_shapes=scratch,
        compiler_params=_params(dimension_semantics=("arbitrary", "arbitrary", "arbitrary")),
    )(*ins)


def _rms_bwd_epilogue(x_idx, g_idx, dxo_idx):
    def ep(vals, ins, outs, i):
        dh = vals[0]
        x = ins[x_idx][...]
        g = ins[g_idx][...]
        rstd = lax.rsqrt(jnp.mean(x * x, axis=-1, keepdims=True) + RMS_EPS)
        xhat = x * rstd
        dxhat = dh * g
        dx = rstd * (dxhat - xhat * jnp.mean(dxhat * xhat, axis=-1, keepdims=True))
        outs[0][...] = ins[dxo_idx][...] + dx
        dg = jnp.broadcast_to(jnp.sum(dh * xhat, axis=0, keepdims=True), outs[1].shape)

        @pl.when(i == 0)
        def _():
            outs[1][...] = dg

        @pl.when(i > 0)
        def _():
            outs[1][...] += dg
    return ep


def _normed(x_idx, g_idx):
    seen = {}

    def f(ins):
        if id(ins) not in seen:
            xv = ins[x_idx][...]
            h = xv * lax.rsqrt(jnp.mean(xv * xv, axis=-1, keepdims=True) + RMS_EPS)
            seen[id(ins)] = (ins, (h * ins[g_idx][...]).astype(BF16))
        return seen[id(ins)][1]
    return f


def _rope_tables(T):
    half = ROPE_DIM // 2
    lane = jnp.arange(128) % HEAD_DIM
    inv_freq = ROPE_THETA ** (-(2 * (lane % half)).astype(F32) / ROPE_DIM)
    ang = jnp.arange(T, dtype=F32)[:, None] * inv_freq[None, :]
    cos, sin = jnp.cos(ang), jnp.sin(ang)
    c = jnp.where(lane < ROPE_DIM, cos, 1.0)
    s1 = jnp.where(lane < half, -sin, 0.0)
    s2 = jnp.where((lane >= half) & (lane < ROPE_DIM), sin, 0.0)
    return c, s1, s2


def _rope_fwd(xv, c, s1, s2):
    w = xv.shape[1]
    half = ROPE_DIM // 2
    return xv * c + pltpu.roll(xv, w - half, 1) * s1 + pltpu.roll(xv, half, 1) * s2


def _rope_bwd(dy, c, s1, s2):
    w = dy.shape[1]
    half = ROPE_DIM // 2
    return dy * c + pltpu.roll(dy * s1, half, 1) + pltpu.roll(dy * s2, w - half, 1)


def _assemble_dproj(dqk, rest, turned, gates, tabs):
    T = gates[0].shape[0]
    n_qk, n_rest = len(dqk), len(rest) + len(turned)
    width = (n_qk + n_rest) * D_ATT + 2 * D_MODEL

    def body(*refs):
        ins, (c_ref, s1_ref, s2_ref), o_ref = refs[:n_qk + n_rest + 2], refs[-4:-1], refs[-1]
        c = jnp.concatenate([c_ref[...]] * 2, axis=1)
        s1 = jnp.concatenate([s1_ref[...]] * 2, axis=1)
        s2 = jnp.concatenate([s2_ref[...]] * 2, axis=1)
        for b in range(n_qk + n_rest):
            v = ins[b][...]
            if b < n_qk:
                v = _rope_bwd(v, c, s1, s2)
            if b >= n_qk + len(rest):
                v = v.T
            o_ref[:, b * D_ATT:(b + 1) * D_ATT] = v.astype(BF16)
        off = (n_qk + n_rest) * D_ATT
        o_ref[:, off:off + D_MODEL] = ins[-2][...]
        o_ref[:, off + D_MODEL:] = ins[-1][...]

    att = pl.BlockSpec((TM, D_ATT), lambda i: (i, 0))
    att_turned = pl.BlockSpec((D_ATT, TM), lambda i: (0, i))
    wide = pl.BlockSpec((TM, D_MODEL), lambda i: (i, 0))
    tab = pl.BlockSpec((TM, 128), lambda i: (i, 0))
    return pl.pallas_call(
        body, name="assemble_dproj", grid=(T // TM,),
        in_specs=[att] * (n_qk + len(rest)) + [att_turned] * len(turned) + [wide, wide, tab, tab, tab],
        out_specs=pl.BlockSpec((TM, width), lambda i: (i, 0)),
        out_shape=jax.ShapeDtypeStruct((T, width), BF16), compiler_params=_params(),
    )(*dqk, *rest, *turned, *gates, *tabs)


def _dil_merge(os_, lses):
    T = os_[0].shape[0]

    def body(o0, o1, o2, l0, l1, l2, o_ref, lse_ref):
        a, b, c = l0[...], l1[...], l2[...]
        m = jnp.maximum(jnp.maximum(a, b), c)
        ea, eb, ec = jnp.exp(a - m), jnp.exp(b - m), jnp.exp(c - m)
        den = ea + eb + ec
        o_ref[...] = (ea * o0[...] + eb * o1[...] + ec * o2[...]) / den
        lse_ref[...] = m + jnp.log(den)

    blk = pl.BlockSpec((TM, D_ATT), lambda i: (i, 0))
    sh = jax.ShapeDtypeStruct((T, D_ATT), F32)
    return pl.pallas_call(
        body, name="dil_merge", grid=(T // TM,), in_specs=[blk] * 6, out_specs=[blk, blk],
        out_shape=[sh, sh], compiler_params=_params(),
    )(*os_, *lses)


def _final_loss(x, gain, target):
    T = x.shape[0]

    def body(x_ref, g_ref, t_ref, dx_ref, dg_ref, loss_ref):
        xv = x_ref[...]
        g = g_ref[...]
        rstd = lax.rsqrt(jnp.mean(xv * xv, axis=-1, keepdims=True) + RMS_EPS)
        xhat = xv * rstd
        err = xhat * g - t_ref[...]
        loss = 0.5 * jnp.sum(jnp.mean(err * err, axis=-1, keepdims=True), axis=0, keepdims=True)
        dy = err * (1.0 / D_MODEL)
        dxhat = dy * g
        dx_ref[...] = rstd * (dxhat - xhat * jnp.mean(dxhat * xhat, axis=-1, keepdims=True))
        dg = jnp.broadcast_to(jnp.sum(dy * xhat, axis=0, keepdims=True), dg_ref.shape)
        ls = jnp.broadcast_to(loss, loss_ref.shape)

        @pl.when(pl.program_id(0) == 0)
        def _():
            dg_ref[...] = dg
            loss_ref[...] = ls

        @pl.when(pl.program_id(0) > 0)
        def _():
            dg_ref[...] += dg
            loss_ref[...] += ls

    blk = pl.BlockSpec((TM, D_MODEL), lambda i: (i, 0))
    row = pl.BlockSpec((1, D_MODEL), lambda i: (0, 0))
    acc = pl.BlockSpec((8, D_MODEL), lambda i: (0, 0))
    return pl.pallas_call(
        body, name="final_loss", grid=(T // TM,), in_specs=[blk, row, blk], out_specs=[blk, acc, acc],
        out_shape=[jax.ShapeDtypeStruct((T, D_MODEL), F32), jax.ShapeDtypeStruct((8, D_MODEL), F32),
                   jax.ShapeDtypeStruct((8, D_MODEL), F32)],
        compiler_params=_params(dimension_semantics=("arbitrary",)),
    )(x, gain, target)


def _pair_masks():
    lane = lax.broadcasted_iota(jnp.int32, (SPAN, 128), 1)
    return [lane < HEAD_DIM, lane >= HEAD_DIM]


def _stack_heads(x, masks):
    return jnp.concatenate([jnp.where(m, x, 0.0) for m in masks], axis=0)


def _unstack_heads(y, masks):
    rows = y.shape[0] // len(masks)
    out = jnp.where(masks[0], y[:rows], 0.0)
    for h in range(1, len(masks)):
        out = out + jnp.where(masks[h], y[rows * h:rows * (h + 1)], 0.0)
    return out


DIL_PAIR = 2


def _dil_rows(idx, d):
    u = idx // d
    r = idx - u * d
    own = pl.ds(u * (SPAN * d) + r, SPAN, stride=d) if d > 1 else pl.ds(pl.multiple_of(u * SPAN, SPAN), SPAN)
    up = jnp.maximum(u - 1, 0)
    prev = pl.ds(up * (SPAN * d) + r, SPAN, stride=d) if d > 1 else pl.ds(pl.multiple_of(up * SPAN, SPAN), SPAN)
    return u, own, prev


def _dil_valid(u):
    qi = lax.broadcasted_iota(jnp.int32, (2 * SPAN, 2 * SPAN), 0) & (SPAN - 1)
    kj = lax.broadcasted_iota(jnp.int32, (2 * SPAN, 2 * SPAN), 1)
    in_prev = (kj < SPAN) & (kj >= qi + jnp.where(u > 0, 0, SPAN))
    return in_prev | ((kj >= SPAN) & (kj - SPAN <= qi))


def _dil_keys(ref, own, prev):
    return jnp.concatenate([ref[prev, :], ref[own, :]], axis=0).astype(BF16)


def _dil_fwd(proj, g, d):
    T = proj.shape[0]
    n_iter = T // SPAN

    def body(q_ref, k_ref, v_ref, o_ref, lse_ref):
        masks = _pair_masks()

        def step(pair, carry):
            its = [_dil_rows(DIL_PAIR * pair + e, d) for e in range(DIL_PAIR)]
            qs = [_stack_heads(q_ref[own, :] * (HEAD_DIM ** -0.5), masks).astype(BF16) for _, own, _ in its]
            kks = [_dil_keys(k_ref, own, prev) for _, own, prev in its]
            vvs = [_dil_keys(v_ref, own, prev) for _, own, prev in its]
            ss = [jnp.where(_dil_valid(u), lax.dot_general(q, kk, NT, preferred_element_type=F32), NEG)
                  for (u, _, _), q, kk in zip(its, qs, kks)]
            ms = [jnp.max(s, axis=1, keepdims=True) for s in ss]
            ps = [jnp.exp(s - m) for s, m in zip(ss, ms)]
            dens = [jnp.sum(p, axis=1, keepdims=True) for p in ps]
            pvs = [lax.dot_general(p.astype(BF16), vv, NN, preferred_element_type=F32) / den
                   for p, vv, den in zip(ps, vvs, dens)]
            for (_, own, _), pv, m, den in zip(its, pvs, ms, dens):
                o_ref[own, :] = _unstack_heads(pv, masks)
                lse_ref[own, :] = _unstack_heads(jnp.broadcast_to(m + jnp.log(den), pv.shape), masks)
            return carry

        lax.fori_loop(0, n_iter // DIL_PAIR, step, 0)

    def col(b):
        return pl.BlockSpec((T, 128), lambda p: (0, b + p))

    sh = jax.ShapeDtypeStruct((T, D_ATT), F32)
    out = pl.BlockSpec((T, 128), lambda p: (0, p))
    return pl.pallas_call(
        body, name=f"dil_fwd_d{d}", grid=(2,),
        in_specs=[col(2 * g), col(6 + 2 * g), col(12 + 2 * g)], out_specs=[out, out], out_shape=[sh, sh],
        compiler_params=_params(dimension_semantics=("arbitrary",)),
    )(proj, proj, proj)


def _dil_bwd(proj, do, o_dil, lse, g, d):
    T = proj.shape[0]
    n_iter = T // SPAN

    def body(q_ref, k_ref, v_ref, do_ref, o_ref, lse_ref, dq_ref, dk_ref, dv_ref):
        masks = _pair_masks()
        head_lanes = jnp.concatenate(masks, axis=0)

        def step(pair, carry):
            its = [_dil_rows(DIL_PAIR * pair + e, d) for e in range(DIL_PAIR)]
            qs = [_stack_heads(q_ref[own, :] * (HEAD_DIM ** -0.5), masks).astype(BF16) for _, own, _ in its]
            kks = [_dil_keys(k_ref, own, prev) for _, own, prev in its]
            vvs = [_dil_keys(v_ref, own, prev) for _, own, prev in its]
            doms = [_stack_heads(do_ref[own, :], masks) for _, own, _ in its]
            dos = [dom.astype(BF16) for dom in doms]
            deltas = [jnp.sum(dom * jnp.concatenate([o_ref[own, :]] * 2, axis=0), axis=1, keepdims=True)
                      for dom, (_, own, _) in zip(doms, its)]
            lrows = [jnp.max(jnp.where(head_lanes, jnp.concatenate([lse_ref[own, :]] * 2, axis=0), NEG),
                             axis=1, keepdims=True) for _, own, _ in its]
            ss = [lax.dot_general(q, kk, NT, preferred_element_type=F32) for q, kk in zip(qs, kks)]
            dps = [lax.dot_general(do_b, vv, NT, preferred_element_type=F32) for do_b, vv in zip(dos, vvs)]
            ps = [jnp.where(_dil_valid(u), jnp.exp(s - lrow), 0.0) for (u, _, _), s, lrow in zip(its, ss, lrows)]
            dss = [(p * (dp - delta)).astype(BF16) for p, dp, delta in zip(ps, dps, deltas)]
            dqs = [lax.dot_general(ds, kk, NN, preferred_element_type=F32) for ds, kk in zip(dss, kks)]
            dkks = [lax.dot_general(ds, q, TN, preferred_element_type=F32) for ds, q in zip(dss, qs)]
            dvvs = [lax.dot_general(p.astype(BF16), do_b, TN, preferred_element_type=F32) for p, do_b in zip(ps, dos)]
            for (_, own, prev), dq, dkk, dvv in zip(its, dqs, dkks, dvvs):
                dq_ref[own, :] = _unstack_heads(dq, masks) * (HEAD_DIM ** -0.5)
                dk_ref[own, :] = dkk[SPAN:]
                dv_ref[own, :] = dvv[SPAN:]
                dk_ref[prev, :] = dk_ref[prev, :] + dkk[:SPAN]
                dv_ref[prev, :] = dv_ref[prev, :] + dvv[:SPAN]
            return carry

        lax.fori_loop(0, n_iter // DIL_PAIR, step, 0)

    def col(b):
        return pl.BlockSpec((T, 128), lambda p: (0, b + p))

    sh = jax.ShapeDtypeStruct((T, D_ATT), F32)
    return pl.pallas_call(
        body, name=f"dil_bwd_d{d}", grid=(2,),
        in_specs=[col(2 * g), col(6 + 2 * g), col(12 + 2 * g), col(0), col(0), col(0)],
        out_specs=[col(0), col(0), col(0)], out_shape=[sh, sh, sh],
        compiler_params=_params(dimension_semantics=("arbitrary",)),
    )(proj, proj, proj, do, o_dil, lse)


SB_KT = 512
LOG2_E = 1.4426950408889634


def _sb_tri(strict):
    a = lax.broadcasted_iota(jnp.int32, (Q_BLOCK, Q_BLOCK), 0)
    b = lax.broadcasted_iota(jnp.int32, (Q_BLOCK, Q_BLOCK), 1)
    return jnp.where((a > b) if strict else (a >= b), 1.0, 0.0).astype(BF16)


def _split_stack(x):
    nb = x.shape[1] // Q_BLOCK
    blocks = [x[:, Q_BLOCK * b:Q_BLOCK * (b + 1)] for b in range(nb)]
    hi = [b.astype(BF16) for b in blocks]
    lo = [(b - h.astype(F32)).astype(BF16) for b, h in zip(blocks, hi)]
    return blocks, jnp.concatenate(hi + lo, axis=0)


def _suffix_from(y, blocks, c):
    r = blocks[0].shape[0]
    nb = len(blocks)
    outs = [None] * nb
    run = c
    for b in reversed(range(nb)):
        outs[b] = run + y[r * b:r * (b + 1)] + y[r * (nb + b):r * (nb + b + 1)]
        run = run + jnp.sum(blocks[b], axis=1, keepdims=True)
    return jnp.concatenate(outs, axis=1), run


SB_HEADS = D_ATT // HEAD_DIM
SB_FWD_CHAINS = 2
SB_BWD_CHAINS = 1


SB_PLACES = SB_KT // Q_BLOCK


def _sb_past(place, rows):
    row = lax.broadcasted_iota(jnp.int32, (rows, Q_BLOCK * (place + 1)), 0) & (Q_BLOCK - 1)
    col = lax.broadcasted_iota(jnp.int32, (rows, Q_BLOCK * (place + 1)), 1)
    return col < row + place * Q_BLOCK


def _sb_head_masks(chains):
    lane = lax.broadcasted_iota(jnp.int32, (Q_BLOCK, D_ATT), 1)
    masks = [(lane >= HEAD_DIM * h) & (lane < HEAD_DIM * (h + 1)) for h in range(SB_HEADS)]
    per = SB_HEADS // chains
    return [masks[per * g:per * (g + 1)] for g in range(chains)]


def _sb_rows(t, width=SB_KT):
    return pl.ds(pl.multiple_of(t * SB_KT, SB_KT), width)


def _sb_widen(x):
    if x.shape[1] == SB_KT:
        return x
    return jnp.concatenate([x, jnp.zeros((x.shape[0], SB_KT - x.shape[1]), x.dtype)], axis=1)


def _sb_log_terms(z, past):
    lsz = jnp.minimum(z, 0.0) - jnp.log(1.0 + jnp.exp2(jnp.abs(z) * -LOG2_E))
    lk = lsz - z
    return lsz, (lk if past is None else jnp.where(past, lk, 0.0))


def _sb_weights(lsz, after, past):
    w = jnp.exp(lsz + after)
    return w if past is None else jnp.where(past, w, 0.0)


def _sb_fwd(proj):
    T = proj.shape[0]
    rows = SB_HEADS // SB_FWD_CHAINS * Q_BLOCK

    def at_place(place, n_tiles, q_ref, k_ref, v_ref, o_ref, z_buf, w_buf):
        masks = _sb_head_masks(SB_FWD_CHAINS)
        tri = _sb_tri(True)
        q = q_ref[...] * (HEAD_DIM ** -0.5)
        qs = [_stack_heads(q, m).astype(BF16) for m in masks]

        def scores(t, width=SB_KT):
            kb = k_ref[_sb_rows(t, width), :].astype(BF16)
            return [lax.dot_general(g, kb, NT, preferred_element_type=F32) for g in qs]

        def weights(zs, cs, past, between=lambda: None):
            logs = [_sb_log_terms(z, past) for z in zs]
            splits = [_split_stack(lk) for _, lk in logs]
            ys = [lax.dot_general(x, tri, NN, preferred_element_type=F32) for _, x in splits]
            between()
            sums = [_suffix_from(y, blocks, c) for y, (blocks, _), c in zip(ys, splits, cs)]
            ws = [_sb_weights(lsz, after, past).astype(BF16) for (lsz, _), (after, _) in zip(logs, sums)]
            return ws, [c for _, c in sums]

        def values(acc, slot, t):
            vb = v_ref[_sb_rows(t), :].astype(BF16)
            for g, m in enumerate(masks):
                acc = acc + _unstack_heads(lax.dot_general(w_buf[slot, g], vb, NN, preferred_element_type=F32), m)
            return acc

        def keep(buf, slot, xs):
            for g, x in enumerate(xs):
                buf[slot, g] = x

        zs = scores(n_tiles - 1, Q_BLOCK * (place + 1))
        keep(z_buf, 0, scores(jnp.maximum(n_tiles - 2, 0)))
        ws, cs = weights(zs, [jnp.zeros((rows, 1), F32)] * SB_FWD_CHAINS, _sb_past(place, rows))
        keep(w_buf, 0, [_sb_widen(w) for w in ws])

        def step(tt, carry):
            t = n_tiles - 2 - tt
            cur = tt & 1
            acc = values(carry[0], cur, t + 1)
            ws, cs = weights([z_buf[cur, g] for g in range(SB_FWD_CHAINS)], carry[1:], None,
                             lambda: keep(z_buf, 1 - cur, scores(jnp.maximum(t - 1, 0))))
            keep(w_buf, 1 - cur, ws)
            return (acc, *cs)

        carry = lax.fori_loop(0, n_tiles - 1, step, (jnp.zeros((Q_BLOCK, D_ATT), F32), *cs))
        o_ref[...] = values(carry[0], (n_tiles - 1) & 1, 0)

    def body(*refs):
        n_tiles = pl.program_id(1) + 1
        for place in range(SB_PLACES):
            pl.when(pl.program_id(0) == place)(functools.partial(at_place, place, n_tiles, *refs))

    cb = COL_QS // D_ATT
    return pl.pallas_call(
        body, name="sb_fwd", grid=(SB_PLACES, T // SB_KT),
        in_specs=[pl.BlockSpec((Q_BLOCK, D_ATT), lambda p, j: (SB_PLACES * j + p, cb)),
                  pl.BlockSpec((T, D_ATT), lambda p, j: (0, cb + 1)),
                  pl.BlockSpec((T, D_ATT), lambda p, j: (0, cb + 2))],
        out_specs=pl.BlockSpec((Q_BLOCK, D_ATT), lambda p, j: (SB_PLACES * j + p, 0)),
        out_shape=jax.ShapeDtypeStruct((T, D_ATT), F32),
        scratch_shapes=[pltpu.VMEM((2, SB_FWD_CHAINS, rows, SB_KT), F32),
                        pltpu.VMEM((2, SB_FWD_CHAINS, rows, SB_KT), BF16)],
        compiler_params=_params(dimension_semantics=("arbitrary", "arbitrary")),
    )(proj, proj, proj)


def _sb_bwd(proj, do, o):
    T = proj.shape[0]
    n_rows = SB_HEADS // SB_BWD_CHAINS * Q_BLOCK

    def at_place(place, n_tiles, q_ref, k_ref, v_ref, do_ref, o_ref, dq_ref, dk_ref, dv_ref,
                 z_buf, gv_buf, dz_buf, w_buf):
        masks = _sb_head_masks(SB_BWD_CHAINS)
        tri = _sb_tri(True)
        tri_incl = _sb_tri(False)

        q = q_ref[...] * (HEAD_DIM ** -0.5)
        qs = [_stack_heads(q, m).astype(BF16) for m in masks]
        dos = [_stack_heads(do_ref[...], m).astype(BF16) for m in masks]
        qts = [_stack_heads(q, m).T.astype(BF16) for m in masks]
        dots = [_stack_heads(do_ref[...], m).T.astype(BF16) for m in masks]
        o_rep = jnp.concatenate([o_ref[...]] * (SB_HEADS // SB_BWD_CHAINS), axis=0)
        deltas = [jnp.sum(d.astype(F32) * o_rep, axis=1, keepdims=True) for d in dos]

        def scores(t, width=SB_KT):
            kb = k_ref[_sb_rows(t, width), :].astype(BF16)
            return [lax.dot_general(g, kb, NT, preferred_element_type=F32) for g in qs]

        def value_grads(t, width=SB_KT):
            vb = v_ref[_sb_rows(t, width), :].astype(BF16)
            return [lax.dot_general(d, vb, NT, preferred_element_type=F32) for d in dos]

        def keep(buf, slot, xs):
            for g, x in enumerate(xs):
                buf[slot, g] = x

        def kept(buf, slot):
            return [buf[slot, g] for g in range(SB_BWD_CHAINS)]

        def score_grads(zs, gvs, cs, ces, past, after_first=lambda: None, after_second=lambda: None):
            logs = [_sb_log_terms(z, past) for z in zs]
            splits = [_split_stack(lk) for _, lk in logs]
            ys = [lax.dot_general(x, tri, NN, preferred_element_type=F32) for _, x in splits]
            after_first()
            sums = [_suffix_from(y, blocks, c) for y, (blocks, _), c in zip(ys, splits, cs)]
            wbs = [_sb_weights(lsz, after, past).astype(BF16) for (lsz, _), (after, _) in zip(logs, sums)]
            es = [wb.astype(F32) * gv for wb, gv in zip(wbs, gvs())]
            esplits = [_split_stack(e) for e in es]
            eys = [lax.dot_general(x, tri_incl, NN, preferred_element_type=F32) for _, x in esplits]
            after_second()
            esums = [_suffix_from(y, blocks, ce) for y, (blocks, _), ce in zip(eys, esplits, ces)]
            dzbs = []
            for e, (lsz, _), (suf, _), delta in zip(es, logs, esums, deltas):
                dz = e - jnp.exp(lsz) * (e + (delta - suf))
                dzbs.append((dz if past is None else jnp.where(past, dz, 0.0)).astype(BF16))
            return dzbs, wbs, [c for _, c in sums], [c for _, c in esums]

        def outputs(dq, slot, t):
            rows = _sb_rows(t)
            kb = k_ref[rows, :].astype(BF16)
            dk_t = dv_t = None
            for m, dzb, wb, g, d in zip(masks, kept(dz_buf, slot), kept(w_buf, slot), qts, dots):
                dq = dq + _unstack_heads(lax.dot_general(dzb, kb, NN, preferred_element_type=F32), m)
                a = lax.dot_general(g, dzb, NN, preferred_element_type=F32)
                b = lax.dot_general(d, wb, NN, preferred_element_type=F32)
                dk_t = a if dk_t is None else dk_t + a
                dv_t = b if dv_t is None else dv_t + b
            dk_ref[:, rows] = dk_ref[:, rows] + dk_t
            dv_ref[:, rows] = dv_ref[:, rows] + dv_t
            return dq

        zcol = [jnp.zeros((n_rows, 1), F32)] * SB_BWD_CHAINS
        ahead = jnp.maximum(n_tiles - 2, 0)
        width = Q_BLOCK * (place + 1)
        zs, gvs = scores(n_tiles - 1, width), value_grads(n_tiles - 1, width)
        keep(z_buf, 0, scores(ahead))
        keep(gv_buf, 0, value_grads(ahead))
        dzbs, wbs, cs, ces = score_grads(zs, lambda: gvs, zcol, zcol, _sb_past(place, n_rows))
        keep(dz_buf, 0, [_sb_widen(x) for x in dzbs])
        keep(w_buf, 0, [_sb_widen(x) for x in wbs])

        def step(tt, carry):
            t = n_tiles - 2 - tt
            cur = tt & 1
            ahead = jnp.maximum(t - 1, 0)
            dq = outputs(carry[0], cur, t + 1)
            dzbs, wbs, cs, ces = score_grads(
                kept(z_buf, cur), lambda: kept(gv_buf, cur),
                carry[1:1 + SB_BWD_CHAINS], carry[1 + SB_BWD_CHAINS:], None,
                lambda: keep(z_buf, 1 - cur, scores(ahead)),
                lambda: keep(gv_buf, 1 - cur, value_grads(ahead)))
            keep(dz_buf, 1 - cur, dzbs)
            keep(w_buf, 1 - cur, wbs)
            return (dq, *cs, *ces)

        carry = lax.fori_loop(0, n_tiles - 1, step, (jnp.zeros((Q_BLOCK, D_ATT), F32), *cs, *ces))
        dq_ref[...] = outputs(carry[0], (n_tiles - 1) & 1, 0) * (HEAD_DIM ** -0.5)

    def body(*refs):
        n_tiles = pl.program_id(1) + 1
        dk_ref, dv_ref = refs[6:8]

        @pl.when((pl.program_id(0) == 0) & (n_tiles == 1))
        def _():
            dk_ref[...] = jnp.zeros_like(dk_ref)
            dv_ref[...] = jnp.zeros_like(dv_ref)

        for place in range(SB_PLACES):
            pl.when(pl.program_id(0) == place)(functools.partial(at_place, place, n_tiles, *refs))

    cb = COL_QS // D_ATT
    blk = pl.BlockSpec((Q_BLOCK, D_ATT), lambda p, j: (SB_PLACES * j + p, 0))
    turned = pl.BlockSpec((D_ATT, T), lambda p, j: (0, 0))
    sh = jax.ShapeDtypeStruct((T, D_ATT), F32)
    sh_turned = jax.ShapeDtypeStruct((D_ATT, T), F32)
    kept_f32 = pltpu.VMEM((2, SB_BWD_CHAINS, n_rows, SB_KT), F32)
    kept_bf16 = pltpu.VMEM((2, SB_BWD_CHAINS, n_rows, SB_KT), BF16)
    return pl.pallas_call(
        body, name="sb_bwd", grid=(SB_PLACES, T // SB_KT),
        in_specs=[pl.BlockSpec((Q_BLOCK, D_ATT), lambda p, j: (SB_PLACES * j + p, cb)),
                  pl.BlockSpec((T, D_ATT), lambda p, j: (0, cb + 1)),
                  pl.BlockSpec((T, D_ATT), lambda p, j: (0, cb + 2)), blk, blk],
        out_specs=[blk, turned, turned], out_shape=[sh, sh_turned, sh_turned],
        scratch_shapes=[kept_f32, kept_f32, kept_bf16, kept_bf16],
        compiler_params=_params(dimension_semantics=("arbitrary", "arbitrary")),
    )(proj, proj, proj, do, o)


def _tok(c, by=None):
    if by is None:
        return pl.BlockSpec((TM, c), lambda i, j, k: (i, 0))
    if by == 1:
        return pl.BlockSpec((TM, c), lambda i, j, k: (i, j))
    return pl.BlockSpec((TM, c), lambda i, j, k: (i, k))


def _gain_spec():
    return pl.BlockSpec((1, D_MODEL), lambda i, j, k: (0, 0))


def _wfull(r, c, l):
    return pl.BlockSpec((N_CHIPS, None, r, c), lambda i, j, k: (0, l, 0, 0), pipeline_mode=pl.Buffered(1))


def _pick(idx, c):
    return lambda ins: ins[idx][c]


def _cols(idx, c, w):
    return lambda ins: ins[idx][:, c * w:(c + 1) * w]


def _rows(rows, width):
    return pl.BlockSpec((rows, width), lambda i, j, k: (i, 0))


def _whole(shape):
    return pl.BlockSpec(shape, lambda i, j, k: (0, 0), pipeline_mode=pl.Buffered(1))


def _ffn_fwd(x, gain, wg, wu, wd):
    T = x.shape[0]
    wg, wu = (w.reshape(-1, D_MODEL) for w in (wg, wu))
    ff = wg.shape[0]
    tm = TM // 2
    normed = _normed(0, 3)

    def swiglu(vals, ins, outs, i):
        gt, up = vals
        s = _sigmoid(gt)
        sil = gt * s
        outs[0][...] = sil.astype(BF16)
        outs[1][...] = (up * (s * (1.0 + gt * (1.0 - s)))).astype(BF16)
        outs[2][...] = (sil * up).astype(BF16)
        outs[3][...] = normed(ins)

    ash = jax.ShapeDtypeStruct((T, ff), BF16)
    sil, up_dsil, act, h = _mm(
        "ffn_up", [x, wg, wu, gain], [_rows(tm, D_MODEL), _whole(wg.shape), _whole(wu.shape), _gain_spec()],
        [(normed, 1, 0), (normed, 2, 1)], 2, None, NT, (T // tm, 1, 1), swiglu,
        [ash] * 3 + [jax.ShapeDtypeStruct((T, D_MODEL), BF16)], [_rows(tm, ff)] * 3 + [_rows(tm, D_MODEL)])

    def resid(vals, ins, outs, i):
        outs[0][...] = ins[2][...] + 0.5 * vals[0]

    wd_chips = wd(act) if callable(wd) else wd
    wd = wd_chips.reshape(-1, D_MODEL)
    (y,) = _mm(
        "ffn_down", [act, wd, x], [_rows(TM, ff), _whole(wd.shape), _tok(D_MODEL)], [(0, 1, 0)], 1, None, NN,
        (T // TM, 1, 1), resid, [jax.ShapeDtypeStruct((T, D_MODEL), F32)], [_tok(D_MODEL)])
    return y, (x, h, sil, up_dsil, act), wd_chips


def _ffn_bwd(dxo, gain, wg, wu, wd, saved):
    x, h, sil, up_dsil, act = saved
    T = x.shape[0]
    n_chips, _, ffs, _ = wd.shape
    wg, wu, wd = (w.reshape(-1, D_MODEL) for w in (wg, wu, wd))
    ff = wd.shape[0]
    tk = TM
    tm = TM

    def dswiglu(vals, ins, outs, i):
        da = 0.5 * vals[0]
        outs[0][...] = (da * ins[3][...].astype(F32)).astype(BF16)
        outs[1][...] = (da * ins[2][...].astype(F32)).astype(BF16)

    ash = jax.ShapeDtypeStruct((T, ff), BF16)
    dgate, dup = _mm(
        "ffn_dact", [dxo, wd, sil, up_dsil], [_rows(tm, D_MODEL), _whole(wd.shape), _rows(tm, ff), _rows(tm, ff)],
        [(0, 1, 0)], 1, None, NT, (T // tm, 1, 1), dswiglu, [ash, ash], [_rows(tm, ff)] * 2)

    def half(vals, ins, outs, i):
        outs[0][...] = (0.5 * vals[0]).astype(BF16)

    def cast(vals, ins, outs, i):
        outs[0][...] = vals[0].astype(BF16)

    tok_k = pl.BlockSpec((tk, D_MODEL), lambda i, j, k: (k, 0))
    hid_k = pl.BlockSpec((tk, ff), lambda i, j, k: (k, 0))
    wsh = jax.ShapeDtypeStruct((ff, D_MODEL), BF16)
    (dwd,) = _mm("ffn_dwd", [act, dxo], [hid_k, tok_k], [(0, 1, 0)], 1, (ff, D_MODEL), TN, (1, 1, T // tk), half,
                 [wsh], [_whole((ff, D_MODEL))])

    tx = TM // 2
    dx, dgain = _mm(
        "ffn_dx", [dgate, dup, wg, wu, x, gain, dxo],
        [_rows(tx, ff), _rows(tx, ff), _whole(wg.shape), _whole(wu.shape), _rows(tx, D_MODEL), _gain_spec(),
         _rows(tx, D_MODEL)],
        [(0, 2, 0), (1, 3, 0)], 1, None, NN, (T // tx, 1, 1), _rms_bwd_epilogue(4, 5, 6),
        [jax.ShapeDtypeStruct((T, D_MODEL), F32), jax.ShapeDtypeStruct((8, D_MODEL), F32)],
        [_rows(tx, D_MODEL), pl.BlockSpec((8, D_MODEL), lambda i, j, k: (0, 0))])

    dws = []
    for dact in (dgate, dup):
        dws += _mm("ffn_dwgu", [dact, h], [hid_k, tok_k], [(0, 1, 0)], 1, (ff, D_MODEL), TN, (1, 1, T // tk), cast,
                   [wsh], [_whole((ff, D_MODEL))])
    dwg, dwu, dwd = (w.reshape(n_chips, ffs, D_MODEL) for w in (dws[0], dws[1], dwd))
    return dx, dgain, dwg, dwu, dwd


def _joined_mixer_weights(wpd, wps, wo):
    n, _, r, c = wpd.shape
    wpd_n, wps_n = (w[:, 0].transpose(1, 0, 2).reshape(r, n * c) for w in (wpd, wps))
    return wpd_n, wps_n, wo.reshape(-1, wo.shape[3])


def _mixer_fwd(x, gain, W, l, tabs):
    T = x.shape[0]
    win, wpd, wps, wo = W["w_in"], W["w_proj_dil"], W["w_proj_sb"], W["w_out"]
    cin = win.shape[3]
    cp = wpd.shape[3]
    normed = _normed(0, 5)
    n_rope = 6 * D_ATT

    tm = TM // 2

    def roped(vals, ins, outs, i):
        for j, v in enumerate(vals):
            lo = j * cin
            k = min(max(n_rope - lo, 0), cin)
            if k:
                tab = [jnp.concatenate([ins[t][...]] * (k // 128), axis=1) for t in (2, 3, 4)]
                outs[0][:, lo:lo + k] = _rope_fwd(v[:, :k], *tab)
            if k < cin:
                outs[0][:, lo + k:lo + cin] = v[:, k:]
        outs[1][...] = normed(ins)

    proj, h = _mm(
        "mix_in", [x, win, *tabs, gain],
        [_rows(tm, D_MODEL), _wfull(D_MODEL, cin, l)] + [_rows(tm, 128)] * 3 + [_gain_spec()],
        [(normed, _pick(1, c), c) for c in range(N_CHIPS)], N_CHIPS, None, NN, (T // tm, 1, 1), roped,
        [jax.ShapeDtypeStruct((T, N_CHIPS * cin), F32), jax.ShapeDtypeStruct((T, D_MODEL), BF16)],
        [_rows(tm, N_CHIPS * cin), _rows(tm, D_MODEL)])

    os_, lses = [], []
    for g, (window, dil) in enumerate(DIL_GROUPS):
        o_g, lse_g = _dil_fwd(proj, g, dil)
        os_.append(o_g)
        lses.append(lse_g)
    o_dil, lse = _dil_merge(os_, lses)
    o_sb = _sb_fwd(proj)

    def gated(vals, ins, outs, i):
        pd, ps = vals
        outs[0][...] = (_sigmoid(ins[4][...]) * pd + _sigmoid(ins[5][...]) * ps).astype(BF16)
        outs[1][...] = pd.astype(BF16)
        outs[2][...] = ps.astype(BF16)

    wpd_n, wps_n, wo_n = _joined_mixer_weights(wpd, wps, wo)
    gd_spec = pl.BlockSpec((TM, D_MODEL), lambda i, j, k: (i, COL_GD // D_MODEL))
    gs_spec = pl.BlockSpec((TM, D_MODEL), lambda i, j, k: (i, COL_GS // D_MODEL))
    ush = jax.ShapeDtypeStruct((T, D_MODEL), BF16)
    u, pd, ps = _mm(
        "mix_gate", [o_dil, o_sb, wpd_n, wps_n, proj, proj],
        [_tok(D_ATT), _tok(D_ATT), _whole(wpd_n.shape), _whole(wps_n.shape), gd_spec, gs_spec],
        [(0, 2, 0), (1, 3, 1)], 2, None, NN, (T // TM, 1, 1), gated, [ush] * 3, [_tok(D_MODEL)] * 3)

    def resid(vals, ins, outs, i):
        outs[0][...] = ins[2][...] + vals[0]

    (y,) = _mm(
        "mix_out", [u, wo_n, x], [_tok(D_MODEL), _whole(wo_n.shape), _tok(D_MODEL)], [(0, 1, 0)], 1, None, NN,
        (T // TM, 1, 1), resid, [jax.ShapeDtypeStruct((T, D_MODEL), F32)], [_tok(D_MODEL)])
    return y, (x, h, proj, o_dil, lse, o_sb, u, pd, ps)


def _mixer_bwd(dxo, gain, W, l, tabs, saved):
    x, h, proj, o_dil, lse, o_sb, u, pd, ps = saved
    T = x.shape[0]
    win, wpd, wps, wo = W["w_in"], W["w_proj_dil"], W["w_proj_sb"], W["w_out"]
    cin = win.shape[3]
    cp = wpd.shape[3]
    tk = TM
    tm = TM
    row = pl.BlockSpec((tm, D_MODEL), lambda i, j, k: (i, 0))

    def dgated(vals, ins, outs, i):
        du = vals[0]
        sd = _sigmoid(ins[4][...])
        ss = _sigmoid(ins[5][...])
        outs[0][...] = (du * sd).astype(BF16)
        outs[1][...] = (du * ss).astype(BF16)
        outs[2][...] = (du * ins[2][...].astype(F32) * sd * (1.0 - sd)).astype(BF16)
        outs[3][...] = (du * ins[3][...].astype(F32) * ss * (1.0 - ss)).astype(BF16)

    wpd_n, wps_n, wo_n = _joined_mixer_weights(wpd, wps, wo)
    gd_spec = pl.BlockSpec((TM, D_MODEL), lambda i, j, k: (i, COL_GD // D_MODEL))
    gs_spec = pl.BlockSpec((TM, D_MODEL), lambda i, j, k: (i, COL_GS // D_MODEL))
    ush = jax.ShapeDtypeStruct((T, D_MODEL), BF16)
    dpd, dps, dgd, dgs = _mm(
        "mix_du", [dxo, wo_n, pd, ps, proj, proj],
        [_tok(D_MODEL), _whole(wo_n.shape), _tok(D_MODEL), _tok(D_MODEL), gd_spec, gs_spec],
        [(0, 1, 0)], 1, None, NT, (T // TM, 1, 1), dgated, [ush] * 4, [_tok(D_MODEL)] * 4)

    def one(vals, ins, outs, i):
        outs[0][...] = vals[0].astype(BF16)

    def two(vals, ins, outs, i):
        outs[0][...] = vals[0].astype(BF16)
        outs[1][...] = vals[1].astype(BF16)

    tok_k = pl.BlockSpec((tk, D_MODEL), lambda i, j, k: (k, 0))
    att_k = pl.BlockSpec((tk, D_ATT), lambda i, j, k: (k, 0))
    (dwo_n,) = _mm("mix_dwo", [u, dxo], [tok_k, tok_k], [(0, 1, 0)], 1, (D_MODEL, D_MODEL), TN, (1, 1, T // tk), one,
                   [jax.ShapeDtypeStruct((D_MODEL, D_MODEL), BF16)], [_whole((D_MODEL, D_MODEL))])

    def plain2(vals, ins, outs, i):
        outs[0][...] = vals[0]
        outs[1][...] = vals[1]

    ash = jax.ShapeDtypeStruct((T, D_ATT), F32)
    do_dil, do_sb = _mm(
        "mix_do", [dpd, dps, wpd_n, wps_n], [_tok(D_MODEL), _tok(D_MODEL), _whole(wpd_n.shape), _whole(wps_n.shape)],
        [(0, 2, 0), (1, 3, 1)], 2, None, NT, (T // TM, 1, 1), plain2, [ash, ash], [_tok(D_ATT)] * 2)

    psh = jax.ShapeDtypeStruct((D_ATT, D_MODEL), BF16)
    dwpd_n, dwps_n = _mm(
        "mix_dwp", [o_dil, o_sb, dpd, dps], [att_k, att_k, tok_k, tok_k], [(0, 2, 0), (1, 3, 1)], 2,
        (D_ATT, D_MODEL), TN, (1, 1, T // tk), two, [psh, psh], [_whole((D_ATT, D_MODEL))] * 2)
    dwpd, dwps = (w.reshape(D_ATT, N_CHIPS, cp).transpose(1, 0, 2) for w in (dwpd_n, dwps_n))
    dwo = dwo_n.reshape(N_CHIPS, cp, D_MODEL)

    dqs, dks, dvs = [], [], []
    for g, (window, dil) in enumerate(DIL_GROUPS):
        dq, dk, dv = _dil_bwd(proj, do_dil, o_dil, lse, g, dil)
        dqs.append(dq)
        dks.append(dk)
        dvs.append(dv)
    dq_s, dk_s, dv_s = _sb_bwd(proj, do_sb, o_sb)
    dproj = _assemble_dproj(dqs + dks, dvs + [dq_s], [dk_s, dv_s], [dgd, dgs], tabs)

    dx, dgain = _mm(
        "mix_dx", [dproj, win, x, gain, dxo],
        [pl.BlockSpec((tm, N_CHIPS * cin), lambda i, j, k: (i, 0)), _wfull(D_MODEL, cin, l), row, _gain_spec(), row],
        [(_cols(0, c, cin), _pick(1, c), 0) for c in range(N_CHIPS)], 1, None, NT, (T // tm, 1, 1),
        _rms_bwd_epilogue(2, 3, 4),
        [jax.ShapeDtypeStruct((T, D_MODEL), F32), jax.ShapeDtypeStruct((8, D_MODEL), F32)],
        [row, pl.BlockSpec((8, D_MODEL), lambda i, j, k: (0, 0))])

    (dwin,) = _mm(
        "mix_dwin", [h, dproj],
        [pl.BlockSpec((tk, D_MODEL), lambda i, j, k: (k, 0)), pl.BlockSpec((tk, cin), lambda i, j, k: (k, j))],
        [(0, 1, 0)], 1, (D_MODEL, cin), TN, (1, N_CHIPS, T // tk), one,
        [jax.ShapeDtypeStruct((N_CHIPS, D_MODEL, cin), BF16)],
        [pl.BlockSpec((None, D_MODEL, cin), lambda i, j, k: (j, 0, 0))])
    return dx, dgain, dwin, dwpd, dwps, dwo


def _local_step(x, target, norms, norm_final, weights_of, on_grads):
    T = x.shape[0]
    tabs = _rope_tables(T)
    saved, held = [], []
    for l in range(DEPTH):
        w1 = weights_of(l, 0, x)
        x, s1, w1["ffn1_w_down"] = _ffn_fwd(x, norms["norm_ffn1"][l:l + 1], w1["ffn1_w_gate"], w1["ffn1_w_up"],
                                             w1["ffn1_w_down"])
        w2 = weights_of(l, 1, x)
        x, s2 = _mixer_fwd(x, norms["norm_mix"][l:l + 1], w2, 0, tabs)
        w3 = weights_of(l, 2, x)
        x, s3, _ = _ffn_fwd(x, norms["norm_ffn2"][l:l + 1], w3["ffn2_w_gate"], w3["ffn2_w_up"], w3["ffn2_w_down"])
        saved.append((s1, s2, s3))
        held.append((w1, w2, w3))
    dx, dg_final, loss = _final_loss(x, norm_final.reshape(1, D_MODEL), target)
    gains = [None] * DEPTH
    for l in reversed(range(DEPTH)):
        s1, s2, s3 = saved[l]
        w1, w2, w3 = held[l]
        dx, dg2, dwg2, dwu2, dwd2 = _ffn_bwd(dx, norms["norm_ffn2"][l:l + 1], w3["ffn2_w_gate"], w3["ffn2_w_up"],
                                             w3["ffn2_w_down"], s3)
        dx = on_grads(l, 2, dict(ffn2_w_gate=dwg2, ffn2_w_up=dwu2, ffn2_w_down=dwd2), dx)
        dx, dgm, dwin, dwpd, dwps, dwo = _mixer_bwd(dx, norms["norm_mix"][l:l + 1], w2, 0, tabs, s2)
        dx = on_grads(l, 1, dict(w_in=dwin, w_proj_dil=dwpd, w_proj_sb=dwps, w_out=dwo), dx)
        dx, dg1, dwg1, dwu1, dwd1 = _ffn_bwd(dx, norms["norm_ffn1"][l:l + 1], w1["ffn1_w_gate"], w1["ffn1_w_up"],
                                             w1["ffn1_w_down"], s1)
        dx = on_grads(l, 0, dict(ffn1_w_gate=dwg1, ffn1_w_up=dwu1, ffn1_w_down=dwd1), dx)
        gains[l] = dict(norm_ffn1=dg1, norm_mix=dgm, norm_ffn2=dg2)
    return loss, dx, gains, dg_final


def _place():
    x, y, c = lax.axis_index("x"), lax.axis_index("y"), lax.axis_index("c")
    chips = [(1 - x, y), (x, 1 - y), (1 - x, 1 - y)]
    return x, y, c, chips


def _half(c, r):
    return pl.ds(pl.multiple_of(c * (r // 2), 8), r // 2)


def _cast_into_slot(ws, ls, me_arr, after):
    n = len(ws)
    late = [] if after is None else [after]

    def body(me_ref, *refs):
        for a in range(n):
            refs[len(refs) - n + a][...] = refs[a][...].astype(BF16)

    def src(w, l):
        return pl.BlockSpec((None, w.shape[1] // 4, w.shape[2]), lambda i, me: (l, i, 0))

    def dst(w):
        return pl.BlockSpec((None, None, w.shape[1] // 4, w.shape[2]), lambda i, me: (me[0], 0, i, 0))

    return pl.pallas_call(
        body, name="cast_weights",
        grid_spec=pltpu.PrefetchScalarGridSpec(
            num_scalar_prefetch=1, grid=(4,),
            in_specs=[src(w, l) for w, l in zip(ws, ls)] + [pl.BlockSpec(memory_space=pl.ANY)] * len(late),
            out_specs=[dst(w) for w in ws]),
        out_shape=[jax.ShapeDtypeStruct((N_CHIPS, 1) + w.shape[1:], BF16) for w in ws], compiler_params=_params(),
    )(me_arr, *ws, *late)


HBM_SPEC = pl.BlockSpec(memory_space=pltpu.HBM)
SEM_SPEC = pl.BlockSpec(memory_space=pltpu.SEMAPHORE)
SPLIT_COPY = pltpu.CompilerParams(has_side_effects=pltpu.SideEffectType.DATAFLOW_SIDE_EFFECTING)


def _gather_piece(ref, chip_id, c):
    return ref.at[chip_id, 0, _half(c, ref.shape[2]), :]


def _gather_start(tag, bufs, direct):
    n = len(bufs)

    def body(*refs):
        out_refs = refs[n:2 * n]
        send_sems, recv_sems, token = refs[2 * n:]
        x, y, c, chips = _place()
        me = 2 * x + y
        for a in range(n):
            piece = _gather_piece(out_refs[a], me, c)
            for j, chip in enumerate(chips):
                for to in ((0, 1) if direct[a] else (c,)):
                    pltpu.make_async_remote_copy(
                        src_ref=piece, dst_ref=piece, send_sem=send_sems.at[6 * a + 2 * j + to],
                        recv_sem=recv_sems.at[6 * a + 2 * j + c], device_id=(*chip, to), device_id_type=MESH).start()
        token[...] = jnp.zeros_like(token)

    outs = pl.pallas_call(
        body, name=f"gather_start_{tag}", in_specs=[HBM_SPEC] * n,
        out_specs=[HBM_SPEC] * n + [SEM_SPEC, SEM_SPEC, pl.BlockSpec(memory_space=pltpu.VMEM)],
        out_shape=[pltpu.HBM(b.shape, b.dtype) for b in bufs] + [pltpu.SemaphoreType.DMA((6 * n,))] * 2
        + [jax.ShapeDtypeStruct((8, 128), F32)],
        input_output_aliases={a: a for a in range(n)}, compiler_params=SPLIT_COPY,
    )(*[pltpu.with_memory_space_constraint(b, pltpu.HBM) for b in bufs])
    return outs[:n], outs[n], outs[n + 1], outs[n + 2]


def _gather_wait(k, bufs, places, direct, send_sems, recv_sems, after):
    m = len(bufs)

    def body(*refs):
        in_refs = refs[:m]
        ssem, rsem = refs[m], refs[m + 1]
        x, y, c, chips = _place()
        me = 2 * x + y
        for t, a in enumerate(places):
            for j, chip in enumerate(chips):
                for core in ((0, 1) if direct else (c,)):
                    cp = pltpu.make_async_remote_copy(
                        src_ref=_gather_piece(in_refs[t], me, c),
                        dst_ref=_gather_piece(in_refs[t], 2 * chip[0] + chip[1], core),
                        send_sem=ssem.at[6 * a + 2 * j + core], recv_sem=rsem.at[6 * a + 2 * j + core],
                        device_id=(*chip, core), device_id_type=MESH)
                    cp.wait_send()
                    cp.wait_recv()

    return pl.pallas_call(
        body, name=f"gather_wait_{k}",
        in_specs=[HBM_SPEC] * m + [SEM_SPEC, SEM_SPEC, pl.BlockSpec(memory_space=pl.ANY)], out_specs=[HBM_SPEC] * m,
        out_shape=[pltpu.HBM(b.shape, b.dtype) for b in bufs], input_output_aliases={t: t for t in range(m)},
        compiler_params=SPLIT_COPY,
    )(*bufs, send_sems, recv_sems, after)


def _gather_relay(bufs):
    n = len(bufs)

    def body(*refs):
        out_refs = refs[n:2 * n]
        send_sems, recv_sems = refs[2 * n:]
        x, y, c, chips = _place()
        cps = []
        for a in range(n):
            for j, chip in enumerate(chips):
                piece = _gather_piece(out_refs[a], 2 * chip[0] + chip[1], c)
                cps.append(pltpu.make_async_remote_copy(
                    src_ref=piece, dst_ref=piece, send_sem=send_sems.at[a, j], recv_sem=recv_sems.at[a, j],
                    device_id=(x, y, 1 - c), device_id_type=MESH))
        for cp in cps:
            cp.start()
        for a in range(n):
            for j, chip in enumerate(chips):
                theirs = _gather_piece(out_refs[a], 2 * chip[0] + chip[1], 1 - c)
                pltpu.make_async_remote_copy(
                    src_ref=theirs, dst_ref=theirs, send_sem=send_sems.at[a, j], recv_sem=recv_sems.at[a, j],
                    device_id=(x, y, 1 - c), device_id_type=MESH).wait_recv()
        for cp in cps:
            cp.wait_send()

    any_spec = pl.BlockSpec(memory_space=pl.ANY)
    return pl.pallas_call(
        body, name="gather_relay", in_specs=[any_spec] * n, out_specs=[any_spec] * n,
        out_shape=[jax.ShapeDtypeStruct(b.shape, b.dtype) for b in bufs],
        input_output_aliases={a: a for a in range(n)},
        scratch_shapes=[pltpu.SemaphoreType.DMA((n, 3))] * 2,
    )(*bufs)


def _other_half(ref, c):
    return ref.at[:, _half(1 - c, ref.shape[1]), :]


def _all_of(ref, c):
    return ref


def _sibling_start(name, srcs, pick, land_shapes, thru):
    n = len(srcs)
    lands = [lax.empty(sh, s.dtype) for sh, s in zip(land_shapes, srcs)]
    kept = lands + ([] if thru is None else [thru])
    m = len(kept)

    def body(*refs):
        s_refs, land_refs = refs[:n], refs[n + m:n + m + n]
        send_sems, recv_sems, token = refs[n + 2 * m:]
        x, y, c, _ = _place()
        for a in range(n):
            pltpu.make_async_remote_copy(
                src_ref=pick(s_refs[a], c), dst_ref=land_refs[a], send_sem=send_sems.at[a],
                recv_sem=recv_sems.at[a], device_id=(x, y, 1 - c), device_id_type=MESH).start()
        token[...] = jnp.zeros_like(token)

    outs = pl.pallas_call(
        body, name=name, in_specs=[HBM_SPEC] * (n + m),
        out_specs=[HBM_SPEC] * m + [SEM_SPEC, SEM_SPEC, pl.BlockSpec(memory_space=pltpu.VMEM)],
        out_shape=[pltpu.HBM(v.shape, v.dtype) for v in kept] + [pltpu.SemaphoreType.DMA((n,))] * 2
        + [jax.ShapeDtypeStruct((8, 128), F32)],
        input_output_aliases={n + a: a for a in range(m)}, compiler_params=SPLIT_COPY,
    )(*[pltpu.with_memory_space_constraint(v, pltpu.HBM) for v in list(srcs) + kept])
    return (outs[:n], outs[m], outs[m + 1]), (outs[n] if thru is not None else None), outs[m + 2]


def _sibling_wait(name, srcs, pick, lands, send_sems, recv_sems, after):
    n = len(srcs)

    def body(*refs):
        s_refs, land_refs = refs[:n], refs[n:2 * n]
        ssem, rsem = refs[2 * n], refs[2 * n + 1]
        x, y, c, _ = _place()
        for a in range(n):
            cp = pltpu.make_async_remote_copy(
                src_ref=pick(s_refs[a], c), dst_ref=land_refs[a], send_sem=ssem.at[a], recv_sem=rsem.at[a],
                device_id=(x, y, 1 - c), device_id_type=MESH)
            cp.wait_send()
            cp.wait_recv()

    return pl.pallas_call(
        body, name=name, in_specs=[HBM_SPEC] * (2 * n) + [SEM_SPEC, SEM_SPEC, pl.BlockSpec(memory_space=pl.ANY)],
        out_specs=[HBM_SPEC] * n, out_shape=[pltpu.HBM(v.shape, v.dtype) for v in lands],
        input_output_aliases={n + a: a for a in range(n)}, compiler_params=SPLIT_COPY,
    )(*srcs, *lands, send_sems, recv_sems, after)


def _add_half(gs, gots, c_arr):
    n = len(gs)

    def body(c_ref, *refs):
        for a in range(n):
            refs[2 * n + a][...] = (refs[a][...].astype(F32) + refs[n + a][...].astype(F32)).astype(BF16)

    def own(g):
        return pl.BlockSpec((None, g.shape[1] // 2, g.shape[2]), lambda k, cr: (k, cr[0], 0))

    def half(g):
        return pl.BlockSpec((None, g.shape[1] // 2, g.shape[2]), lambda k, cr: (k, 0, 0))

    return pl.pallas_call(
        body, name="grad_add_half",
        grid_spec=pltpu.PrefetchScalarGridSpec(
            num_scalar_prefetch=1, grid=(N_CHIPS,),
            in_specs=[own(g) for g in gs] + [half(g) for g in gs], out_specs=[half(g) for g in gs]),
        out_shape=[jax.ShapeDtypeStruct(got.shape, BF16) for got in gots], compiler_params=_params(),
    )(c_arr, *gs, *gots)


def _scatter_start(k, ss, thru):
    n = len(ss)

    def body(*refs):
        s_refs, land_refs = refs[2 * n + 1:3 * n + 1], refs[3 * n + 1:4 * n + 1]
        send_sems, recv_sems = refs[4 * n + 2:]
        x, y, c, chips = _place()
        me = 2 * x + y
        for a in range(n):
            for j, chip in enumerate(chips):
                pltpu.make_async_remote_copy(
                    src_ref=s_refs[a].at[2 * chip[0] + chip[1]], dst_ref=land_refs[a].at[me],
                    send_sem=send_sems.at[3 * a + j], recv_sem=recv_sems.at[3 * a + j], device_id=(*chip, c),
                    device_id_type=MESH).start()

    lands = [lax.empty(s.shape, s.dtype) for s in ss]
    hbm = [pltpu.HBM(s.shape, s.dtype) for s in ss]
    outs = pl.pallas_call(
        body, name=f"grad_scatter_start_{k}", in_specs=[HBM_SPEC] * (2 * n + 1),
        out_specs=[HBM_SPEC] * (2 * n + 1) + [SEM_SPEC, SEM_SPEC],
        out_shape=hbm + hbm + [pltpu.HBM(thru.shape, thru.dtype)] + [pltpu.SemaphoreType.DMA((3 * n,))] * 2,
        input_output_aliases={a: a for a in range(2 * n + 1)}, compiler_params=SPLIT_COPY,
    )(*[pltpu.with_memory_space_constraint(v, pltpu.HBM) for v in list(ss) + lands + [thru]])
    return (outs[:n], outs[n:2 * n], outs[2 * n + 1], outs[2 * n + 2]), outs[2 * n]


def _scatter_wait(k, ss, lands, send_sems, recv_sems, after):
    n = len(ss)

    def body(*refs):
        s_refs, land_refs = refs[:n], refs[n:2 * n]
        ssem, rsem = refs[2 * n], refs[2 * n + 1]
        x, y, c, chips = _place()
        me = 2 * x + y
        for a in range(n):
            for j, chip in enumerate(chips):
                cid = 2 * chip[0] + chip[1]
                cp = pltpu.make_async_remote_copy(
                    src_ref=s_refs[a].at[cid], dst_ref=land_refs[a].at[cid], send_sem=ssem.at[3 * a + j],
                    recv_sem=rsem.at[3 * a + j], device_id=(*chip, c), device_id_type=MESH)
                cp.wait_send()
                cp.wait_recv()

    hbm = [pltpu.HBM(s.shape, s.dtype) for s in ss]
    outs = pl.pallas_call(
        body, name=f"grad_scatter_wait_{k}",
        in_specs=[HBM_SPEC] * (2 * n) + [SEM_SPEC, SEM_SPEC, pl.BlockSpec(memory_space=pl.ANY)],
        out_specs=[HBM_SPEC] * (2 * n), out_shape=hbm + hbm,
        input_output_aliases={a: a for a in range(2 * n)}, compiler_params=SPLIT_COPY,
    )(*ss, *lands, send_sems, recv_sems, after)
    return outs[:n], outs[n:]


def _sum_chips(lands, ss, me_arr):
    n = len(lands)

    def body(me_ref, *refs):
        for own in range(N_CHIPS):
            @pl.when(me_ref[0] == own)
            def _(own=own):
                for a in range(n):
                    acc = None
                    for k in range(N_CHIPS):
                        term = (refs[n + a][...] if k == own else refs[a][k]).astype(F32)
                        acc = term if acc is None else acc + term
                    refs[2 * n + a][...] = acc

    return pl.pallas_call(
        body, name="grad_sum_chips",
        grid_spec=pltpu.PrefetchScalarGridSpec(
            num_scalar_prefetch=1, grid=(1,),
            in_specs=[pl.BlockSpec(la.shape, lambda i, me: (0, 0, 0)) for la in lands]
            + [pl.BlockSpec((None,) + la.shape[1:], lambda i, me: (me[0], 0, 0)) for la in lands],
            out_specs=[pl.BlockSpec(la.shape[1:], lambda i, me: (0, 0)) for la in lands]),
        out_shape=[jax.ShapeDtypeStruct(la.shape[1:], F32) for la in lands], compiler_params=_params(),
    )(me_arr, *lands, *ss)


def _allreduce_rows(stats):
    def body(s_ref, o_ref, buf, send_sems, recv_sems):
        x, y, c, _ = _place()
        me = 4 * x + 2 * y + c
        buf[me] = s_ref[...]
        cps = []
        for k in range(1, 8):
            px = jnp.where(k & 4, 1 - x, x)
            py = jnp.where(k & 2, 1 - y, y)
            pc = jnp.where(k & 1, 1 - c, c)
            cps.append(pltpu.make_async_remote_copy(
                src_ref=s_ref, dst_ref=buf.at[me], send_sem=send_sems.at[k - 1], recv_sem=recv_sems.at[k - 1],
                device_id=(px, py, pc), device_id_type=MESH))
        for cp in cps:
            cp.start()
        for cp in cps:
            cp.wait()
        acc = buf[0]
        for d in range(1, 8):
            acc = acc + buf[d]
        o_ref[...] = acc

    vm = pl.BlockSpec(memory_space=pltpu.VMEM)
    return pl.pallas_call(
        body, name="allreduce_rows", in_specs=[vm], out_specs=vm,
        out_shape=jax.ShapeDtypeStruct(stats.shape, F32),
        scratch_shapes=[pltpu.VMEM((8,) + stats.shape, F32), pltpu.SemaphoreType.DMA((7,)),
                        pltpu.SemaphoreType.DMA((7,))],
    )(stats)


def _adamw_math(w, g, m, v):
    m = ADAM_B1 * m + (1.0 - ADAM_B1) * g
    v = ADAM_B2 * v + (1.0 - ADAM_B2) * (g * g)
    m_hat = m / (1.0 - ADAM_B1 ** ADAM_STEP)
    v_hat = v / (1.0 - ADAM_B2 ** ADAM_STEP)
    delta = -ADAM_LR * (m_hat / (jnp.sqrt(v_hat) + ADAM_EPS) + ADAM_WD * w)
    return delta, m, v


def _adamw(ws, ms, vs, mines, theirs, l, c_arr, earlier, after):
    n = len(ws)
    held = [t for e in earlier if e is not None for t in e]
    assert len(held) in (0, 4 * n)
    late = [] if after is None else [after]

    def body(c_ref, *refs):
        outs = refs[len(refs) - 4 * n:]
        for a in range(n):
            w_ref, m_ref, v_ref, a_ref, b_ref = refs[5 * a:5 * a + 5]
            g = jnp.where(pl.program_id(0) == c_ref[0], a_ref[...], b_ref[...])
            delta, mn, vn = _adamw_math(w_ref[...], g, m_ref[...], v_ref[...])
            outs[4 * a][...] = g
            outs[4 * a + 1][...] = delta
            outs[4 * a + 2][...] = mn
            outs[4 * a + 3][...] = vn

    def blk(w):
        tr = w.shape[1] // 4
        return pl.BlockSpec((None, tr, w.shape[2]), lambda hh, i, cr: (l, 2 * hh + i, 0))

    def half(w):
        return pl.BlockSpec((w.shape[1] // 4, w.shape[2]), lambda hh, i, cr: (i, 0))

    outs = pl.pallas_call(
        body, name="adamw",
        grid_spec=pltpu.PrefetchScalarGridSpec(
            num_scalar_prefetch=1, grid=(2, 2),
            in_specs=[sp for w in ws for sp in (blk(w), blk(w), blk(w), half(w), half(w))]
            + [pl.BlockSpec(memory_space=pl.ANY)] * (len(held) + len(late)),
            out_specs=[blk(w) for w in ws for _ in range(4)]),
        out_shape=[jax.ShapeDtypeStruct(w.shape, F32) for w in ws for _ in range(4)],
        input_output_aliases={1 + 5 * n + t: t for t in range(len(held))}, compiler_params=_params(),
    )(c_arr, *[t for grp in zip(ws, ms, vs, mines, theirs) for t in grp], *held, *late)
    return [outs[4 * a:4 * a + 4] for a in range(n)]


def _adamw_rows(w, m, v, g):
    def body(w_ref, m_ref, v_ref, g_ref, d_ref, mo_ref, vo_ref):
        delta, mn, vn = _adamw_math(w_ref[...], g_ref[...], m_ref[...], v_ref[...])
        d_ref[...] = delta
        mo_ref[...] = mn
        vo_ref[...] = vn

    vm = pl.BlockSpec(memory_space=pltpu.VMEM)
    sh = jax.ShapeDtypeStruct(w.shape, F32)
    return pl.pallas_call(body, name="adamw_rows", in_specs=[vm] * 4, out_specs=[vm] * 3, out_shape=[sh] * 3)(w, m, v, g)


SUBLAYERS = (("ffn1_w_gate", "ffn1_w_up", "ffn1_w_down"), ("w_in", "w_proj_dil", "w_proj_sb", "w_out"),
             ("ffn2_w_gate", "ffn2_w_up", "ffn2_w_down"))
TRANSPOSED = ("ffn1_w_gate", "ffn1_w_up", "ffn2_w_gate", "ffn2_w_up")


def _pick_row(blocks):
    row = lax.broadcasted_iota(jnp.int32, (8, D_MODEL), 0)
    out = jnp.zeros((8, D_MODEL), F32)
    for i, b in enumerate(blocks):
        out = out + jnp.where(row == i, b, 0.0)
    return out


def kernel(x, norm_ffn1, ffn1_w_gate, ffn1_w_up, ffn1_w_down, norm_mix, w_in, w_proj_dil, w_proj_sb, w_out, norm_ffn2, ffn2_w_gate, ffn2_w_up, ffn2_w_down, norm_final, loss_target, m_norm_ffn1, m_ffn1_w_gate, m_ffn1_w_up, m_ffn1_w_down, m_norm_mix, m_w_in, m_w_proj_dil, m_w_proj_sb, m_w_out, m_norm_ffn2, m_ffn2_w_gate, m_ffn2_w_up, m_ffn2_w_down, m_norm_final, v_norm_ffn1, v_ffn1_w_gate, v_ffn1_w_up, v_ffn1_w_down, v_norm_mix, v_w_in, v_w_proj_dil, v_w_proj_sb, v_w_out, v_norm_ffn2, v_ffn2_w_gate, v_ffn2_w_up, v_ffn2_w_down, v_norm_final):
    given = dict(locals())
    for n in TRANSPOSED:
        for k in ("", "m_", "v_"):
            given[k + n] = jnp.swapaxes(given[k + n], 1, 2)
    weights = {n: given[n] for n in WEIGHT_NAMES}
    norms = {n: given[n] for n in NORM_NAMES}

    c_arr = lax.axis_index("c").astype(jnp.int32).reshape(1)
    me_arr = (2 * lax.axis_index("x") + lax.axis_index("y")).astype(jnp.int32).reshape(1)
    order = [(l, s, n) for l in range(DEPTH) for s in range(len(SUBLAYERS)) for n in SUBLAYERS[s]]
    n_first = len(SUBLAYERS[0])
    sent, token = {}, None
    for tag, idxs in (("a", range(n_first)), ("b", range(n_first, len(order)))):
        cast = _cast_into_slot([weights[order[i][2]] for i in idxs], [order[i][0] for i in idxs], me_arr, token)
        bufs, send_sems, recv_sems, token = _gather_start(tag, cast, [order[i][:2] != (0, 0) for i in idxs])
        for p, i in enumerate(idxs):
            sent[i] = (bufs[p], p, send_sems, recv_sems)

    def arrived(tag, idxs, after):
        direct = order[idxs[0]][:2] != (0, 0)
        got = _gather_wait(tag, [sent[i][0] for i in idxs], [sent[i][1] for i in idxs], direct,
                           sent[idxs[0]][2], sent[idxs[0]][3], after)
        return {order[i][2]: g for i, g in zip(idxs, got if direct else _gather_relay(got))}

    def weights_of(l, s, after):
        idxs = [i for i, (ll, ss, _) in enumerate(order) if (ll, ss) == (l, s)]
        k = len(SUBLAYERS) * l + s
        if k > 0:
            return arrived(k, idxs, after)
        last = order[idxs[-1]][2]
        got = arrived(f"{k}_first", idxs[:-1], after)
        got[last] = lambda after: arrived(f"{k}_last", idxs[-1:], after)[last]
        return got

    out = {}
    to_add, in_flight = [], []

    def add_and_scatter(after):
        l, s, names, gs, lands, ssem, rsem = to_add.pop(0)
        k = len(SUBLAYERS) * l + s
        got = _sibling_wait(f"grad_exchange_wait_{k}", gs, _other_half, lands, ssem, rsem, after)
        sent, after = _scatter_start(k, _add_half(gs, got, c_arr), after)
        in_flight.append((l, s, names) + sent)
        return after

    def on_grads(l, s, grads, after):
        names = list(grads)
        k = len(SUBLAYERS) * l + s
        gs = [grads[n] for n in names]
        sent, after, _ = _sibling_start(f"grad_exchange_start_{k}", gs, _other_half,
                                        [(g.shape[0], g.shape[1] // 2, g.shape[2]) for g in gs], after)
        if to_add:
            after = add_and_scatter(after)
        to_add.append((l, s, names, gs) + sent)
        return after

    loss_blk, grad_x, gains, dg_final = _local_step(x[0], loss_target[0], norms, norm_final, weights_of, on_grads)
    grad_x = add_and_scatter(grad_x)

    def update(l, names, mine, swap, after):
        theirs = _sibling_wait(f"grad_swap_wait_{l}_{names[0]}", mine, _all_of, *swap, grad_x if after is None else after)
        res = _adamw([weights[n] for n in names], [given["m_" + n] for n in names], [given["v_" + n] for n in names],
                     mine, theirs, l, c_arr, [out.get(n) for n in names], after)
        out.update(zip(names, res))

    waiting = None
    for l, s, names, sums, lands, ssem, rsem in in_flight:
        sums, lands = _scatter_wait(len(SUBLAYERS) * l + s, sums, lands, ssem, rsem, grad_x)
        mine = _sum_chips(lands, sums, me_arr)
        swap, _, token = _sibling_start(f"grad_swap_start_{l}_{names[0]}", mine, _all_of,
                                        [m.shape for m in mine], None)
        if waiting is not None:
            update(*waiting, token)
        waiting = (l, names, mine, swap)
    update(*waiting, None)
    out = {k + n: (jnp.swapaxes(v, 1, 2) if n in TRANSPOSED else v)
           for n, res in out.items() for k, v in zip(("grad_", "delta_", "new_m_", "new_v_"), res)}
    out["grad_x"] = grad_x[None]

    rows = [gains[l][n] for n in NORM_NAMES for l in range(DEPTH)] + [dg_final, loss_blk]
    total = _allreduce_rows(_pick_row(rows))
    out["loss"] = total[7, 0]
    wn = jnp.concatenate([given[n] for n in NORM_NAMES] + [norm_final[None], jnp.zeros((1, D_MODEL), F32)])
    mn_ = jnp.concatenate([given["m_" + n] for n in NORM_NAMES] + [m_norm_final[None], jnp.zeros((1, D_MODEL), F32)])
    vn_ = jnp.concatenate([given["v_" + n] for n in NORM_NAMES] + [v_norm_final[None], jnp.ones((1, D_MODEL), F32)])
    d_n, m_n, v_n = _adamw_rows(wn, mn_, vn_, total)
    for i, n in enumerate(NORM_NAMES):
        sl = slice(i * DEPTH, (i + 1) * DEPTH)
        out["grad_" + n], out["delta_" + n], out["new_m_" + n], out["new_v_" + n] = total[sl], d_n[sl], m_n[sl], v_n[sl]
    out["grad_norm_final"], out["delta_norm_final"] = total[6], d_n[6]
    out["new_m_norm_final"], out["new_v_norm_final"] = m_n[6], v_n[6]

    names = ["norm_ffn1", "ffn1_w_gate", "ffn1_w_up", "ffn1_w_down", "norm_mix", "w_in", "w_proj_dil", "w_proj_sb",
             "w_out", "norm_ffn2", "ffn2_w_gate", "ffn2_w_up", "ffn2_w_down", "norm_final"]
    return (out["loss"], out["grad_x"], *[out["grad_" + n] for n in names], *[out["delta_" + n] for n in names],
            *[out["new_m_" + n] for n in names], *[out["new_v_" + n] for n in names])
```

```python
import functools

import jax
import jax.numpy as jnp
from jax import lax
from jax.experimental import pallas as pl
from jax.experimental.pallas import tpu as pltpu

F32 = jnp.float32
BF16 = jnp.bfloat16

D_MODEL = 1024
DEPTH = 2
N_CHIPS = 4
HEAD_DIM = 64
ROPE_DIM = 16
ROPE_THETA = 500000.0
DIL_GROUPS = ((128, 1), (512, 4), (2048, 16))
SPAN = 128
Q_BLOCK = 128
RMS_EPS = 1e-6
D_ATT = 256
COL_QS = 2304
COL_GD = 3072
COL_GS = 4096
ADAM_LR, ADAM_B1, ADAM_B2, ADAM_EPS, ADAM_WD, ADAM_STEP = 0.001, 0.9, 0.999, 1e-08, 0.01, 10

VMEM_LIMIT = 52 * 1024 * 1024
TM = 512
NEG = -1e30

NN = (((1,), (0,)), ((), ()))
NT = (((1,), (1,)), ((), ()))
TN = (((0,), (0,)), ((), ()))
MESH = pl.DeviceIdType.MESH

WEIGHT_NAMES = ("ffn1_w_gate", "ffn1_w_up", "ffn1_w_down", "w_in", "w_proj_dil",
                "w_proj_sb", "w_out", "ffn2_w_gate", "ffn2_w_up", "ffn2_w_down")
NORM_NAMES = ("norm_ffn1", "norm_mix", "norm_ffn2")


def _params(**kw):
    return pltpu.CompilerParams(vmem_limit_bytes=VMEM_LIMIT, **kw)


def _sigmoid(x):
    return 0.5 * jnp.tanh(0.5 * x) + 0.5


def _mm_body(pairs, n_in, n_out, n_acc, dims, nk, epilogue, *refs):
    ins = refs[:n_in]
    outs = refs[n_in:n_in + n_out]
    accs = refs[n_in + n_out:]
    i = pl.program_id(0)
    k = pl.program_id(2)

    def operand(a):
        return (a(ins) if callable(a) else ins[a][...]).astype(BF16)

    def dot(ia, ib):
        return lax.dot_general(operand(ia), operand(ib), dims, preferred_element_type=F32)

    if nk == 1:
        parts = [None] * n_acc
        for ia, ib, ic in pairs:
            parts[ic] = dot(ia, ib) if parts[ic] is None else parts[ic] + dot(ia, ib)
        epilogue(parts, ins, outs, i)
        return

    @pl.when(k == 0)
    def _():
        for c in range(n_acc):
            accs[c][...] = jnp.zeros_like(accs[c])

    for ia, ib, ic in pairs:
        accs[ic][...] += dot(ia, ib)

    @pl.when(k == nk - 1)
    def _():
        epilogue([a[...] for a in accs], ins, outs, i)


def _mm(name, ins, in_specs, pairs, n_acc, acc_shape, dims, grid, epilogue, out_shapes, out_specs):
    nk = grid[2]
    scratch = [pltpu.VMEM(acc_shape, F32) for _ in range(n_acc)] if nk > 1 else []
    body = functools.partial(_mm_body, tuple(pairs), len(ins), len(out_shapes), n_acc, dims, nk, epilogue)
    return pl.pallas_call(
        body, name=name, grid=grid, in_specs=in_specs, out_specs=out_specs, out_shape=out_shapes,
        scratch_shapes=scratch,
        compiler_params=_params(dimension_semantics=("arbitrary", "arbitrary", "arbitrary")),
    )(*ins)


def _rms_bwd_epilogue(x_idx, g_idx, dxo_idx):
    def ep(vals, ins, outs, i):
        dh = vals[0]
        x = ins[x_idx][...]
        g = ins[g_idx][...]
        rstd = lax.rsqrt(jnp.mean(x * x, axis=-1, keepdims=True) + RMS_EPS)
        xhat = x * rstd
        dxhat = dh * g
        dx = rstd * (dxhat - xhat * jnp.mean(dxhat * xhat, axis=-1, keepdims=True))
        outs[0][...] = ins[dxo_idx][...] + dx
        dg = jnp.broadcast_to(jnp.sum(dh * xhat, axis=0, keepdims=True), outs[1].shape)

        @pl.when(i == 0)
        def _():
            outs[1][...] = dg

        @pl.when(i > 0)
        def _():
            outs[1][...] += dg
    return ep


def _normed(x_idx, g_idx):
    seen = {}

    def f(ins):
        if id(ins) not in seen:
            xv = ins[x_idx][...]
            h = xv * lax.rsqrt(jnp.mean(xv * xv, axis=-1, keepdims=True) + RMS_EPS)
            seen[id(ins)] = (ins, (h * ins[g_idx][...]).astype(BF16))
        return seen[id(ins)][1]
    return f


def _rope_tables(T):
    half = ROPE_DIM // 2
    lane = jnp.arange(128) % HEAD_DIM
    inv_freq = ROPE_THETA ** (-(2 * (lane % half)).astype(F32) / ROPE_DIM)
    ang = jnp.arange(T, dtype=F32)[:, None] * inv_freq[None, :]
    cos, sin = jnp.cos(ang), jnp.sin(ang)
    c = jnp.where(lane < ROPE_DIM, cos, 1.0)
    s1 = jnp.where(lane < half, -sin, 0.0)
    s2 = jnp.where((lane >= half) & (lane < ROPE_DIM), sin, 0.0)
    return c, s1, s2


def _rope_fwd(xv, c, s1, s2):
    w = xv.shape[1]
    half = ROPE_DIM // 2
    return xv * c + pltpu.roll(xv, w - half, 1) * s1 + pltpu.roll(xv, half, 1) * s2


def _rope_bwd(dy, c, s1, s2):
    w = dy.shape[1]
    half = ROPE_DIM // 2
    return dy * c + pltpu.roll(dy * s1, half, 1) + pltpu.roll(dy * s2, w - half, 1)


def _assemble_dproj(dqk, rest, turned, gates, tabs):
    T = gates[0].shape[0]
    n_qk, n_rest = len(dqk), len(rest) + len(turned)
    width = (n_qk + n_rest) * D_ATT + 2 * D_MODEL

    def body(*refs):
        ins, (c_ref, s1_ref, s2_ref), o_ref = refs[:n_qk + n_rest + 2], refs[-4:-1], refs[-1]
        c = jnp.concatenate([c_ref[...]] * 2, axis=1)
        s1 = jnp.concatenate([s1_ref[...]] * 2, axis=1)
        s2 = jnp.concatenate([s2_ref[...]] * 2, axis=1)
        for b in range(n_qk + n_rest):
            v = ins[b][...]
            if b < n_qk:
                v = _rope_bwd(v, c, s1, s2)
            if b >= n_qk + len(rest):
                v = v.T
            o_ref[:, b * D_ATT:(b + 1) * D_ATT] = v.astype(BF16)
        off = (n_qk + n_rest) * D_ATT
        o_ref[:, off:off + D_MODEL] = ins[-2][...]
        o_ref[:, off + D_MODEL:] = ins[-1][...]

    att = pl.BlockSpec((TM, D_ATT), lambda i: (i, 0))
    att_turned = pl.BlockSpec((D_ATT, TM), lambda i: (0, i))
    wide = pl.BlockSpec((TM, D_MODEL), lambda i: (i, 0))
    tab = pl.BlockSpec((TM, 128), lambda i: (i, 0))
    return pl.pallas_call(
        body, name="assemble_dproj", grid=(T // TM,),
        in_specs=[att] * (n_qk + len(rest)) + [att_turned] * len(turned) + [wide, wide, tab, tab, tab],
        out_specs=pl.BlockSpec((TM, width), lambda i: (i, 0)),
        out_shape=jax.ShapeDtypeStruct((T, width), BF16), compiler_params=_params(),
    )(*dqk, *rest, *turned, *gates, *tabs)


def _dil_merge(os_, lses):
    T = os_[0].shape[0]

    def body(o0, o1, o2, l0, l1, l2, o_ref, lse_ref):
        a, b, c = l0[...], l1[...], l2[...]
        m = jnp.maximum(jnp.maximum(a, b), c)
        ea, eb, ec = jnp.exp(a - m), jnp.exp(b - m), jnp.exp(c - m)
        den = ea + eb + ec
        o_ref[...] = (ea * o0[...] + eb * o1[...] + ec * o2[...]) / den
        lse_ref[...] = m + jnp.log(den)

    blk = pl.BlockSpec((TM, D_ATT), lambda i: (i, 0))
    sh = jax.ShapeDtypeStruct((T, D_ATT), F32)
    return pl.pallas_call(
        body, name="dil_merge", grid=(T // TM,), in_specs=[blk] * 6, out_specs=[blk, blk],
        out_shape=[sh, sh], compiler_params=_params(),
    )(*os_, *lses)


def _final_loss(x, gain, target):
    T = x.shape[0]

    def body(x_ref, g_ref, t_ref, dx_ref, dg_ref, loss_ref):
        xv = x_ref[...]
        g = g_ref[...]
        rstd = lax.rsqrt(jnp.mean(xv * xv, axis=-1, keepdims=True) + RMS_EPS)
        xhat = xv * rstd
        err = xhat * g - t_ref[...]
        loss = 0.5 * jnp.sum(jnp.mean(err * err, axis=-1, keepdims=True), axis=0, keepdims=True)
        dy = err * (1.0 / D_MODEL)
        dxhat = dy * g
        dx_ref[...] = rstd * (dxhat - xhat * jnp.mean(dxhat * xhat, axis=-1, keepdims=True))
        dg = jnp.broadcast_to(jnp.sum(dy * xhat, axis=0, keepdims=True), dg_ref.shape)
        ls = jnp.broadcast_to(loss, loss_ref.shape)

        @pl.when(pl.program_id(0) == 0)
        def _():
            dg_ref[...] = dg
            loss_ref[...] = ls

        @pl.when(pl.program_id(0) > 0)
        def _():
            dg_ref[...] += dg
            loss_ref[...] += ls

    blk = pl.BlockSpec((TM, D_MODEL), lambda i: (i, 0))
    row = pl.BlockSpec((1, D_MODEL), lambda i: (0, 0))
    acc = pl.BlockSpec((8, D_MODEL), lambda i: (0, 0))
    return pl.pallas_call(
        body, name="final_loss", grid=(T // TM,), in_specs=[blk, row, blk], out_specs=[blk, acc, acc],
        out_shape=[jax.ShapeDtypeStruct((T, D_MODEL), F32), jax.ShapeDtypeStruct((8, D_MODEL), F32),
                   jax.ShapeDtypeStruct((8, D_MODEL), F32)],
        compiler_params=_params(dimension_semantics=("arbitrary",)),
    )(x, gain, target)


def _pair_masks():
    lane = lax.broadcasted_iota(jnp.int32, (SPAN, 128), 1)
    return [lane < HEAD_DIM, lane >= HEAD_DIM]


def _stack_heads(x, masks):
    return jnp.concatenate([jnp.where(m, x, 0.0) for m in masks], axis=0)


def _unstack_heads(y, masks):
    rows = y.shape[0] // len(masks)
    out = jnp.where(masks[0], y[:rows], 0.0)
    for h in range(1, len(masks)):
        out = out + jnp.where(masks[h], y[rows * h:rows * (h + 1)], 0.0)
    return out


DIL_PAIR = 2


def _dil_rows(idx, d):
    u = idx // d
    r = idx - u * d
    own = pl.ds(u * (SPAN * d) + r, SPAN, stride=d) if d > 1 else pl.ds(pl.multiple_of(u * SPAN, SPAN), SPAN)
    up = jnp.maximum(u - 1, 0)
    prev = pl.ds(up * (SPAN * d) + r, SPAN, stride=d) if d > 1 else pl.ds(pl.multiple_of(up * SPAN, SPAN), SPAN)
    return u, own, prev


def _dil_valid(u):
    qi = lax.broadcasted_iota(jnp.int32, (2 * SPAN, 2 * SPAN), 0) & (SPAN - 1)
    kj = lax.broadcasted_iota(jnp.int32, (2 * SPAN, 2 * SPAN), 1)
    in_prev = (kj < SPAN) & (kj >= qi + jnp.where(u > 0, 0, SPAN))
    return in_prev | ((kj >= SPAN) & (kj - SPAN <= qi))


def _dil_keys(ref, own, prev):
    return jnp.concatenate([ref[prev, :], ref[own, :]], axis=0).astype(BF16)


def _dil_fwd(proj, g, d):
    T = proj.shape[0]
    n_iter = T // SPAN

    def body(q_ref, k_ref, v_ref, o_ref, lse_ref):
        masks = _pair_masks()

        def step(pair, carry):
            its = [_dil_rows(DIL_PAIR * pair + e, d) for e in range(DIL_PAIR)]
            qs = [_stack_heads(q_ref[own, :] * (HEAD_DIM ** -0.5), masks).astype(BF16) for _, own, _ in its]
            kks = [_dil_keys(k_ref, own, prev) for _, own, prev in its]
            vvs = [_dil_keys(v_ref, own, prev) for _, own, prev in its]
            ss = [jnp.where(_dil_valid(u), lax.dot_general(q, kk, NT, preferred_element_type=F32), NEG)
                  for (u, _, _), q, kk in zip(its, qs, kks)]
            ms = [jnp.max(s, axis=1, keepdims=True) for s in ss]
            ps = [jnp.exp(s - m) for s, m in zip(ss, ms)]
            dens = [jnp.sum(p, axis=1, keepdims=True) for p in ps]
            pvs = [lax.dot_general(p.astype(BF16), vv, NN, preferred_element_type=F32) / den
                   for p, vv, den in zip(ps, vvs, dens)]
            for (_, own, _), pv, m, den in zip(its, pvs, ms, dens):
                o_ref[own, :] = _unstack_heads(pv, masks)
                lse_ref[own, :] = _unstack_heads(jnp.broadcast_to(m + jnp.log(den), pv.shape), masks)
            return carry

        lax.fori_loop(0, n_iter // DIL_PAIR, step, 0)

    def col(b):
        return pl.BlockSpec((T, 128), lambda p: (0, b + p))

    sh = jax.ShapeDtypeStruct((T, D_ATT), F32)
    out = pl.BlockSpec((T, 128), lambda p: (0, p))
    return pl.pallas_call(
        body, name=f"dil_fwd_d{d}", grid=(2,),
        in_specs=[col(2 * g), col(6 + 2 * g), col(12 + 2 * g)], out_specs=[out, out], out_shape=[sh, sh],
        compiler_params=_params(dimension_semantics=("arbitrary",)),
    )(proj, proj, proj)


def _dil_bwd(proj, do, o_dil, lse, g, d):
    T = proj.shape[0]
    n_iter = T // SPAN

    def body(q_ref, k_ref, v_ref, do_ref, o_ref, lse_ref, dq_ref, dk_ref, dv_ref):
        masks = _pair_masks()
        head_lanes = jnp.concatenate(masks, axis=0)

        def step(pair, carry):
            its = [_dil_rows(DIL_PAIR * pair + e, d) for e in range(DIL_PAIR)]
            qs = [_stack_heads(q_ref[own, :] * (HEAD_DIM ** -0.5), masks).astype(BF16) for _, own, _ in its]
            kks = [_dil_keys(k_ref, own, prev) for _, own, prev in its]
            vvs = [_dil_keys(v_ref, own, prev) for _, own, prev in its]
            doms = [_stack_heads(do_ref[own, :], masks) for _, own, _ in its]
            dos = [dom.astype(BF16) for dom in doms]
            deltas = [jnp.sum(dom * jnp.concatenate([o_ref[own, :]] * 2, axis=0), axis=1, keepdims=True)
                      for dom, (_, own, _) in zip(doms, its)]
            lrows = [jnp.max(jnp.where(head_lanes, jnp.concatenate([lse_ref[own, :]] * 2, axis=0), NEG),
                             axis=1, keepdims=True) for _, own, _ in its]
            ss = [lax.dot_general(q, kk, NT, preferred_element_type=F32) for q, kk in zip(qs, kks)]
            dps = [lax.dot_general(do_b, vv, NT, preferred_element_type=F32) for do_b, vv in zip(dos, vvs)]
            ps = [jnp.where(_dil_valid(u), jnp.exp(s - lrow), 0.0) for (u, _, _), s, lrow in zip(its, ss, lrows)]
            dss = [(p * (dp - delta)).astype(BF16) for p, dp, delta in zip(ps, dps, deltas)]
            dqs = [lax.dot_general(ds, kk, NN, preferred_element_type=F32) for ds, kk in zip(dss, kks)]
            dkks = [lax.dot_general(ds, q, TN, preferred_element_type=F32) for ds, q in zip(dss, qs)]
            dvvs = [lax.dot_general(p.astype(BF16), do_b, TN, preferred_element_type=F32) for p, do_b in zip(ps, dos)]
            for (_, own, prev), dq, dkk, dvv in zip(its, dqs, dkks, dvvs):
                dq_ref[own, :] = _unstack_heads(dq, masks) * (HEAD_DIM ** -0.5)
                dk_ref[own, :] = dkk[SPAN:]
                dv_ref[own, :] = dvv[SPAN:]
                dk_ref[prev, :] = dk_ref[prev, :] + dkk[:SPAN]
                dv_ref[prev, :] = dv_ref[prev, :] + dvv[:SPAN]
            return carry

        lax.fori_loop(0, n_iter // DIL_PAIR, step, 0)

    def col(b):
        return pl.BlockSpec((T, 128), lambda p: (0, b + p))

    sh = jax.ShapeDtypeStruct((T, D_ATT), F32)
    return pl.pallas_call(
        body, name=f"dil_bwd_d{d}", grid=(2,),
        in_specs=[col(2 * g), col(6 + 2 * g), col(12 + 2 * g), col(0), col(0), col(0)],
        out_specs=[col(0), col(0), col(0)], out_shape=[sh, sh, sh],
        compiler_params=_params(dimension_semantics=("arbitrary",)),
    )(proj, proj, proj, do, o_dil, lse)


SB_KT = 512
LOG2_E = 1.4426950408889634


def _sb_tri(strict):
    a = lax.broadcasted_iota(jnp.int32, (Q_BLOCK, Q_BLOCK), 0)
    b = lax.broadcasted_iota(jnp.int32, (Q_BLOCK, Q_BLOCK), 1)
    return jnp.where((a > b) if strict else (a >= b), 1.0, 0.0).astype(BF16)


def _split_stack(x):
    nb = x.shape[1] // Q_BLOCK
    blocks = [x[:, Q_BLOCK * b:Q_BLOCK * (b + 1)] for b in range(nb)]
    hi = [b.astype(BF16) for b in blocks]
    lo = [(b - h.astype(F32)).astype(BF16) for b, h in zip(blocks, hi)]
    return blocks, jnp.concatenate(hi + lo, axis=0)


def _suffix_from(y, blocks, c):
    r = blocks[0].shape[0]
    nb = len(blocks)
    outs = [None] * nb
    run = c
    for b in reversed(range(nb)):
        outs[b] = run + y[r * b:r * (b + 1)] + y[r * (nb + b):r * (nb + b + 1)]
        run = run + jnp.sum(blocks[b], axis=1, keepdims=True)
    return jnp.concatenate(outs, axis=1), run


SB_HEADS = D_ATT // HEAD_DIM
SB_FWD_CHAINS = 2
SB_BWD_CHAINS = 1


SB_PLACES = SB_KT // Q_BLOCK


def _sb_past(place, rows):
    row = lax.broadcasted_iota(jnp.int32, (rows, Q_BLOCK * (place + 1)), 0) & (Q_BLOCK - 1)
    col = lax.broadcasted_iota(jnp.int32, (rows, Q_BLOCK * (place + 1)), 1)
    return col < row + place * Q_BLOCK


def _sb_head_masks(chains):
    lane = lax.broadcasted_iota(jnp.int32, (Q_BLOCK, D_ATT), 1)
    masks = [(lane >= HEAD_DIM * h) & (lane < HEAD_DIM * (h + 1)) for h in range(SB_HEADS)]
    per = SB_HEADS // chains
    return [masks[per * g:per * (g + 1)] for g in range(chains)]


def _sb_rows(t, width=SB_KT):
    return pl.ds(pl.multiple_of(t * SB_KT, SB_KT), width)


def _sb_widen(x):
    if x.shape[1] == SB_KT:
        return x
    return jnp.concatenate([x, jnp.zeros((x.shape[0], SB_KT - x.shape[1]), x.dtype)], axis=1)


def _sb_log_terms(z, past):
    lsz = jnp.minimum(z, 0.0) - jnp.log(1.0 + jnp.exp2(jnp.abs(z) * -LOG2_E))
    lk = lsz - z
    return lsz, (lk if past is None else jnp.where(past, lk, 0.0))


def _sb_weights(lsz, after, past):
    w = jnp.exp(lsz + after)
    return w if past is None else jnp.where(past, w, 0.0)


def _sb_fwd(proj):
    T = proj.shape[0]
    rows = SB_HEADS // SB_FWD_CHAINS * Q_BLOCK

    def at_place(place, n_tiles, q_ref, k_ref, v_ref, o_ref, z_buf, w_buf):
        masks = _sb_head_masks(SB_FWD_CHAINS)
        tri = _sb_tri(True)
        q = q_ref[...] * (HEAD_DIM ** -0.5)
        qs = [_stack_heads(q, m).astype(BF16) for m in masks]

        def scores(t, width=SB_KT):
            kb = k_ref[_sb_rows(t, width), :].astype(BF16)
            return [lax.dot_general(g, kb, NT, preferred_element_type=F32) for g in qs]

        def weights(zs, cs, past, between=lambda: None):
            logs = [_sb_log_terms(z, past) for z in zs]
            splits = [_split_stack(lk) for _, lk in logs]
            ys = [lax.dot_general(x, tri, NN, preferred_element_type=F32) for _, x in splits]
            between()
            sums = [_suffix_from(y, blocks, c) for y, (blocks, _), c in zip(ys, splits, cs)]
            ws = [_sb_weights(lsz, after, past).astype(BF16) for (lsz, _), (after, _) in zip(logs, sums)]
            return ws, [c for _, c in sums]

        def values(acc, slot, t):
            vb = v_ref[_sb_rows(t), :].astype(BF16)
            for g, m in enumerate(masks):
                acc = acc + _unstack_heads(lax.dot_general(w_buf[slot, g], vb, NN, preferred_element_type=F32), m)
            return acc

        def keep(buf, slot, xs):
            for g, x in enumerate(xs):
                buf[slot, g] = x

        zs = scores(n_tiles - 1, Q_BLOCK * (place + 1))
        keep(z_buf, 0, scores(jnp.maximum(n_tiles - 2, 0)))
        ws, cs = weights(zs, [jnp.zeros((rows, 1), F32)] * SB_FWD_CHAINS, _sb_past(place, rows))
        keep(w_buf, 0, [_sb_widen(w) for w in ws])

        def step(tt, carry):
            t = n_tiles - 2 - tt
            cur = tt & 1
            acc = values(carry[0], cur, t + 1)
            ws, cs = weights([z_buf[cur, g] for g in range(SB_FWD_CHAINS)], carry[1:], None,
                             lambda: keep(z_buf, 1 - cur, scores(jnp.maximum(t - 1, 0))))
            keep(w_buf, 1 - cur, ws)
            return (acc, *cs)

        carry = lax.fori_loop(0, n_tiles - 1, step, (jnp.zeros((Q_BLOCK, D_ATT), F32), *cs))
        o_ref[...] = values(carry[0], (n_tiles - 1) & 1, 0)

    def body(*refs):
        n_tiles = pl.program_id(1) + 1
        for place in range(SB_PLACES):
            pl.when(pl.program_id(0) == place)(functools.partial(at_place, place, n_tiles, *refs))

    cb = COL_QS // D_ATT
    return pl.pallas_call(
        body, name="sb_fwd", grid=(SB_PLACES, T // SB_KT),
        in_specs=[pl.BlockSpec((Q_BLOCK, D_ATT), lambda p, j: (SB_PLACES * j + p, cb)),
                  pl.BlockSpec((T, D_ATT), lambda p, j: (0, cb + 1)),
                  pl.BlockSpec((T, D_ATT), lambda p, j: (0, cb + 2))],
        out_specs=pl.BlockSpec((Q_BLOCK, D_ATT), lambda p, j: (SB_PLACES * j + p, 0)),
        out_shape=jax.ShapeDtypeStruct((T, D_ATT), F32),
        scratch_shapes=[pltpu.VMEM((2, SB_FWD_CHAINS, rows, SB_KT), F32),
                        pltpu.VMEM((2, SB_FWD_CHAINS, rows, SB_KT), BF16)],
        compiler_params=_params(dimension_semantics=("arbitrary", "arbitrary")),
    )(proj, proj, proj)


def _sb_bwd(proj, do, o):
    T = proj.shape[0]
    n_rows = SB_HEADS // SB_BWD_CHAINS * Q_BLOCK

    def at_place(place, n_tiles, q_ref, k_ref, v_ref, do_ref, o_ref, dq_ref, dk_ref, dv_ref,
                 z_buf, gv_buf, dz_buf, w_buf):
        masks = _sb_head_masks(SB_BWD_CHAINS)
        tri = _sb_tri(True)
        tri_incl = _sb_tri(False)

        q = q_ref[...] * (HEAD_DIM ** -0.5)
        qs = [_stack_heads(q, m).astype(BF16) for m in masks]
        dos = [_stack_heads(do_ref[...], m).astype(BF16) for m in masks]
        qts = [_stack_heads(q, m).T.astype(BF16) for m in masks]
        dots = [_stack_heads(do_ref[...], m).T.astype(BF16) for m in masks]
        o_rep = jnp.concatenate([o_ref[...]] * (SB_HEADS // SB_BWD_CHAINS), axis=0)
        deltas = [jnp.sum(d.astype(F32) * o_rep, axis=1, keepdims=True) for d in dos]

        def scores(t, width=SB_KT):
            kb = k_ref[_sb_rows(t, width), :].astype(BF16)
            return [lax.dot_general(g, kb, NT, preferred_element_type=F32) for g in qs]

        def value_grads(t, width=SB_KT):
            vb = v_ref[_sb_rows(t, width), :].astype(BF16)
            return [lax.dot_general(d, vb, NT, preferred_element_type=F32) for d in dos]

        def keep(buf, slot, xs):
            for g, x in enumerate(xs):
                buf[slot, g] = x

        def kept(buf, slot):
            return [buf[slot, g] for g in range(SB_BWD_CHAINS)]

        def score_grads(zs, gvs, cs, ces, past, after_first=lambda: None, after_second=lambda: None):
            logs = [_sb_log_terms(z, past) for z in zs]
            splits = [_split_stack(lk) for _, lk in logs]
            ys = [lax.dot_general(x, tri, NN, preferred_element_type=F32) for _, x in splits]
            after_first()
            sums = [_suffix_from(y, blocks, c) for y, (blocks, _), c in zip(ys, splits, cs)]
            wbs = [_sb_weights(lsz, after, past).astype(BF16) for (lsz, _), (after, _) in zip(logs, sums)]
            es = [wb.astype(F32) * gv for wb, gv in zip(wbs, gvs())]
            esplits = [_split_stack(e) for e in es]
            eys = [lax.dot_general(x, tri_incl, NN, preferred_element_type=F32) for _, x in esplits]
            after_second()
            esums = [_suffix_from(y, blocks, ce) for y, (blocks, _), ce in zip(eys, esplits, ces)]
            dzbs = []
            for e, (lsz, _), (suf, _), delta in zip(es, logs, esums, deltas):
                dz = e - jnp.exp(lsz) * (e + (delta - suf))
                dzbs.append((dz if past is None else jnp.where(past, dz, 0.0)).astype(BF16))
            return dzbs, wbs, [c for _, c in sums], [c for _, c in esums]

        def outputs(dq, slot, t):
            rows = _sb_rows(t)
            kb = k_ref[rows, :].astype(BF16)
            dk_t = dv_t = None
            for m, dzb, wb, g, d in zip(masks, kept(dz_buf, slot), kept(w_buf, slot), qts, dots):
                dq = dq + _unstack_heads(lax.dot_general(dzb, kb, NN, preferred_element_type=F32), m)
                a = lax.dot_general(g, dzb, NN, preferred_element_type=F32)
                b = lax.dot_general(d, wb, NN, preferred_element_type=F32)
                dk_t = a if dk_t is None else dk_t + a
                dv_t = b if dv_t is None else dv_t + b
            dk_ref[:, rows] = dk_ref[:, rows] + dk_t
            dv_ref[:, rows] = dv_ref[:, rows] + dv_t
            return dq

        zcol = [jnp.zeros((n_rows, 1), F32)] * SB_BWD_CHAINS
        ahead = jnp.maximum(n_tiles - 2, 0)
        width = Q_BLOCK * (place + 1)
        zs, gvs = scores(n_tiles - 1, width), value_grads(n_tiles - 1, width)
        keep(z_buf, 0, scores(ahead))
        keep(gv_buf, 0, value_grads(ahead))
        dzbs, wbs, cs, ces = score_grads(zs, lambda: gvs, zcol, zcol, _sb_past(place, n_rows))
        keep(dz_buf, 0, [_sb_widen(x) for x in dzbs])
        keep(w_buf, 0, [_sb_widen(x) for x in wbs])

        def step(tt, carry):
            t = n_tiles - 2 - tt
            cur = tt & 1
            ahead = jnp.maximum(t - 1, 0)
            dq = outputs(carry[0], cur, t + 1)
            dzbs, wbs, cs, ces = score_grads(
                kept(z_buf, cur), lambda: kept(gv_buf, cur),
                carry[1:1 + SB_BWD_CHAINS], carry[1 + SB_BWD_CHAINS:], None,
                lambda: keep(z_buf, 1 - cur, scores(ahead)),
                lambda: keep(gv_buf, 1 - cur, value_grads(ahead)))
            keep(dz_buf, 1 - cur, dzbs)
            keep(w_buf, 1 - cur, wbs)
            return (dq, *cs, *ces)

        carry = lax.fori_loop(0, n_tiles - 1, step, (jnp.zeros((Q_BLOCK, D_ATT), F32), *cs, *ces))
        dq_ref[...] = outputs(carry[0], (n_tiles - 1) & 1, 0) * (HEAD_DIM ** -0.5)

    def body(*refs):
        n_tiles = pl.program_id(1) + 1
        dk_ref, dv_ref = refs[6:8]

        @pl.when((pl.program_id(0) == 0) & (n_tiles == 1))
        def _():
            dk_ref[...] = jnp.zeros_like(dk_ref)
            dv_ref[...] = jnp.zeros_like(dv_ref)

        for place in range(SB_PLACES):
            pl.when(pl.program_id(0) == place)(functools.partial(at_place, place, n_tiles, *refs))

    cb = COL_QS // D_ATT
    blk = pl.BlockSpec((Q_BLOCK, D_ATT), lambda p, j: (SB_PLACES * j + p, 0))
    turned = pl.BlockSpec((D_ATT, T), lambda p, j: (0, 0))
    sh = jax.ShapeDtypeStruct((T, D_ATT), F32)
    sh_turned = jax.ShapeDtypeStruct((D_ATT, T), F32)
    kept_f32 = pltpu.VMEM((2, SB_BWD_CHAINS, n_rows, SB_KT), F32)
    kept_bf16 = pltpu.VMEM((2, SB_BWD_CHAINS, n_rows, SB_KT), BF16)
    return pl.pallas_call(
        body, name="sb_bwd", grid=(SB_PLACES, T // SB_KT),
        in_specs=[pl.BlockSpec((Q_BLOCK, D_ATT), lambda p, j: (SB_PLACES * j + p, cb)),
                  pl.BlockSpec((T, D_ATT), lambda p, j: (0, cb + 1)),
                  pl.BlockSpec((T, D_ATT), lambda p, j: (0, cb + 2)), blk, blk],
        out_specs=[blk, turned, turned], out_shape=[sh, sh_turned, sh_turned],
        scratch_shapes=[kept_f32, kept_f32, kept_bf16, kept_bf16],
        compiler_params=_params(dimension_semantics=("arbitrary", "arbitrary")),
    )(proj, proj, proj, do, o)


def _tok(c, by=None):
    if by is None:
        return pl.BlockSpec((TM, c), lambda i, j, k: (i, 0))
    if by == 1:
        return pl.BlockSpec((TM, c), lambda i, j, k: (i, j))
    return pl.BlockSpec((TM, c), lambda i, j, k: (i, k))


def _gain_spec():
    return pl.BlockSpec((1, D_MODEL), lambda i, j, k: (0, 0))


def _wfull(r, c, l):
    return pl.BlockSpec((N_CHIPS, None, r, c), lambda i, j, k: (0, l, 0, 0), pipeline_mode=pl.Buffered(1))


def _pick(idx, c):
    return lambda ins: ins[idx][c]


def _cols(idx, c, w):
    return lambda ins: ins[idx][:, c * w:(c + 1) * w]


def _rows(rows, width):
    return pl.BlockSpec((rows, width), lambda i, j, k: (i, 0))


def _whole(shape):
    return pl.BlockSpec(shape, lambda i, j, k: (0, 0), pipeline_mode=pl.Buffered(1))


def _ffn_fwd(x, gain, wg, wu, wd):
    T = x.shape[0]
    wg, wu = (w.reshape(-1, D_MODEL) for w in (wg, wu))
    ff = wg.shape[0]
    tm = TM // 2
    normed = _normed(0, 3)

    def swiglu(vals, ins, outs, i):
        gt, up = vals
        s = _sigmoid(gt)
        sil = gt * s
        outs[0][...] = sil.astype(BF16)
        outs[1][...] = (up * (s * (1.0 + gt * (1.0 - s)))).astype(BF16)
        outs[2][...] = (sil * up).astype(BF16)
        outs[3][...] = normed(ins)

    ash = jax.ShapeDtypeStruct((T, ff), BF16)
    sil, up_dsil, act, h = _mm(
        "ffn_up", [x, wg, wu, gain], [_rows(tm, D_MODEL), _whole(wg.shape), _whole(wu.shape), _gain_spec()],
        [(normed, 1, 0), (normed, 2, 1)], 2, None, NT, (T // tm, 1, 1), swiglu,
        [ash] * 3 + [jax.ShapeDtypeStruct((T, D_MODEL), BF16)], [_rows(tm, ff)] * 3 + [_rows(tm, D_MODEL)])

    def resid(vals, ins, outs, i):
        outs[0][...] = ins[2][...] + 0.5 * vals[0]

    wd_chips = wd(act) if callable(wd) else wd
    wd = wd_chips.reshape(-1, D_MODEL)
    (y,) = _mm(
        "ffn_down", [act, wd, x], [_rows(TM, ff), _whole(wd.shape), _tok(D_MODEL)], [(0, 1, 0)], 1, None, NN,
        (T // TM, 1, 1), resid, [jax.ShapeDtypeStruct((T, D_MODEL), F32)], [_tok(D_MODEL)])
    return y, (x, h, sil, up_dsil, act), wd_chips


def _ffn_bwd(dxo, gain, wg, wu, wd, saved):
    x, h, sil, up_dsil, act = saved
    T = x.shape[0]
    n_chips, _, ffs, _ = wd.shape
    wg, wu, wd = (w.reshape(-1, D_MODEL) for w in (wg, wu, wd))
    ff = wd.shape[0]
    tk = TM
    tm = TM

    def dswiglu(vals, ins, outs, i):
        da = 0.5 * vals[0]
        outs[0][...] = (da * ins[3][...].astype(F32)).astype(BF16)
        outs[1][...] = (da * ins[2][...].astype(F32)).astype(BF16)

    ash = jax.ShapeDtypeStruct((T, ff), BF16)
    dgate, dup = _mm(
        "ffn_dact", [dxo, wd, sil, up_dsil], [_rows(tm, D_MODEL), _whole(wd.shape), _rows(tm, ff), _rows(tm, ff)],
        [(0, 1, 0)], 1, None, NT, (T // tm, 1, 1), dswiglu, [ash, ash], [_rows(tm, ff)] * 2)

    def half(vals, ins, outs, i):
        outs[0][...] = (0.5 * vals[0]).astype(BF16)

    def cast(vals, ins, outs, i):
        outs[0][...] = vals[0].astype(BF16)

    tok_k = pl.BlockSpec((tk, D_MODEL), lambda i, j, k: (k, 0))
    hid_k = pl.BlockSpec((tk, ff), lambda i, j, k: (k, 0))
    wsh = jax.ShapeDtypeStruct((ff, D_MODEL), BF16)
    (dwd,) = _mm("ffn_dwd", [act, dxo], [hid_k, tok_k], [(0, 1, 0)], 1, (ff, D_MODEL), TN, (1, 1, T // tk), half,
                 [wsh], [_whole((ff, D_MODEL))])

    tx = TM // 2
    dx, dgain = _mm(
        "ffn_dx", [dgate, dup, wg, wu, x, gain, dxo],
        [_rows(tx, ff), _rows(tx, ff), _whole(wg.shape), _whole(wu.shape), _rows(tx, D_MODEL), _gain_spec(),
         _rows(tx, D_MODEL)],
        [(0, 2, 0), (1, 3, 0)], 1, None, NN, (T // tx, 1, 1), _rms_bwd_epilogue(4, 5, 6),
        [jax.ShapeDtypeStruct((T, D_MODEL), F32), jax.ShapeDtypeStruct((8, D_MODEL), F32)],
        [_rows(tx, D_MODEL), pl.BlockSpec((8, D_MODEL), lambda i, j, k: (0, 0))])

    dws = []
    for dact in (dgate, dup):
        dws += _mm("ffn_dwgu", [dact, h], [hid_k, tok_k], [(0, 1, 0)], 1, (ff, D_MODEL), TN, (1, 1, T // tk), cast,
                   [wsh], [_whole((ff, D_MODEL))])
    dwg, dwu, dwd = (w.reshape(n_chips, ffs, D_MODEL) for w in (dws[0], dws[1], dwd))
    return dx, dgain, dwg, dwu, dwd


def _joined_mixer_weights(wpd, wps, wo):
    n, _, r, c = wpd.shape
    wpd_n, wps_n = (w[:, 0].transpose(1, 0, 2).reshape(r, n * c) for w in (wpd, wps))
    return wpd_n, wps_n, wo.reshape(-1, wo.shape[3])


def _mixer_fwd(x, gain, W, l, tabs):
    T = x.shape[0]
    win, wpd, wps, wo = W["w_in"], W["w_proj_dil"], W["w_proj_sb"], W["w_out"]
    cin = win.shape[3]
    cp = wpd.shape[3]
    normed = _normed(0, 5)
    n_rope = 6 * D_ATT

    tm = TM // 2

    def roped(vals, ins, outs, i):
        for j, v in enumerate(vals):
            lo = j * cin
            k = min(max(n_rope - lo, 0), cin)
            if k:
                tab = [jnp.concatenate([ins[t][...]] * (k // 128), axis=1) for t in (2, 3, 4)]
                outs[0][:, lo:lo + k] = _rope_fwd(v[:, :k], *tab)
            if k < cin:
                outs[0][:, lo + k:lo + cin] = v[:, k:]
        outs[1][...] = normed(ins)

    proj, h = _mm(
        "mix_in", [x, win, *tabs, gain],
        [_rows(tm, D_MODEL), _wfull(D_MODEL, cin, l)] + [_rows(tm, 128)] * 3 + [_gain_spec()],
        [(normed, _pick(1, c), c) for c in range(N_CHIPS)], N_CHIPS, None, NN, (T // tm, 1, 1), roped,
        [jax.ShapeDtypeStruct((T, N_CHIPS * cin), F32), jax.ShapeDtypeStruct((T, D_MODEL), BF16)],
        [_rows(tm, N_CHIPS * cin), _rows(tm, D_MODEL)])

    os_, lses = [], []
    for g, (window, dil) in enumerate(DIL_GROUPS):
        o_g, lse_g = _dil_fwd(proj, g, dil)
        os_.append(o_g)
        lses.append(lse_g)
    o_dil, lse = _dil_merge(os_, lses)
    o_sb = _sb_fwd(proj)

    def gated(vals, ins, outs, i):
        pd, ps = vals
        outs[0][...] = (_sigmoid(ins[4][...]) * pd + _sigmoid(ins[5][...]) * ps).astype(BF16)
        outs[1][...] = pd.astype(BF16)
        outs[2][...] = ps.astype(BF16)

    wpd_n, wps_n, wo_n = _joined_mixer_weights(wpd, wps, wo)
    gd_spec = pl.BlockSpec((TM, D_MODEL), lambda i, j, k: (i, COL_GD // D_MODEL))
    gs_spec = pl.BlockSpec((TM, D_MODEL), lambda i, j, k: (i, COL_GS // D_MODEL))
    ush = jax.ShapeDtypeStruct((T, D_MODEL), BF16)
    u, pd, ps = _mm(
        "mix_gate", [o_dil, o_sb, wpd_n, wps_n, proj, proj],
        [_tok(D_ATT), _tok(D_ATT), _whole(wpd_n.shape), _whole(wps_n.shape), gd_spec, gs_spec],
        [(0, 2, 0), (1, 3, 1)], 2, None, NN, (T // TM, 1, 1), gated, [ush] * 3, [_tok(D_MODEL)] * 3)

    def resid(vals, ins, outs, i):
        outs[0][...] = ins[2][...] + vals[0]

    (y,) = _mm(
        "mix_out", [u, wo_n, x], [_tok(D_MODEL), _whole(wo_n.shape), _tok(D_MODEL)], [(0, 1, 0)], 1, None, NN,
        (T // TM, 1, 1), resid, [jax.ShapeDtypeStruct((T, D_MODEL), F32)], [_tok(D_MODEL)])
    return y, (x, h, proj, o_dil, lse, o_sb, u, pd, ps)


def _mixer_bwd(dxo, gain, W, l, tabs, saved):
    x, h, proj, o_dil, lse, o_sb, u, pd, ps = saved
    T = x.shape[0]
    win, wpd, wps, wo = W["w_in"], W["w_proj_dil"], W["w_proj_sb"], W["w_out"]
    cin = win.shape[3]
    cp = wpd.shape[3]
    tk = TM
    tm = TM
    row = pl.BlockSpec((tm, D_MODEL), lambda i, j, k: (i, 0))

    def dgated(vals, ins, outs, i):
        du = vals[0]
        sd = _sigmoid(ins[4][...])
        ss = _sigmoid(ins[5][...])
        outs[0][...] = (du * sd).astype(BF16)
        outs[1][...] = (du * ss).astype(BF16)
        outs[2][...] = (du * ins[2][...].astype(F32) * sd * (1.0 - sd)).astype(BF16)
        outs[3][...] = (du * ins[3][...].astype(F32) * ss * (1.0 - ss)).astype(BF16)

    wpd_n, wps_n, wo_n = _joined_mixer_weights(wpd, wps, wo)
    gd_spec = pl.BlockSpec((TM, D_MODEL), lambda i, j, k: (i, COL_GD // D_MODEL))
    gs_spec = pl.BlockSpec((TM, D_MODEL), lambda i, j, k: (i, COL_GS // D_MODEL))
    ush = jax.ShapeDtypeStruct((T, D_MODEL), BF16)
    dpd, dps, dgd, dgs = _mm(
        "mix_du", [dxo, wo_n, pd, ps, proj, proj],
        [_tok(D_MODEL), _whole(wo_n.shape), _tok(D_MODEL), _tok(D_MODEL), gd_spec, gs_spec],
        [(0, 1, 0)], 1, None, NT, (T // TM, 1, 1), dgated, [ush] * 4, [_tok(D_MODEL)] * 4)

    def one(vals, ins, outs, i):
        outs[0][...] = vals[0].astype(BF16)

    def two(vals, ins, outs, i):
        outs[0][...] = vals[0].astype(BF16)
        outs[1][...] = vals[1].astype(BF16)

    tok_k = pl.BlockSpec((tk, D_MODEL), lambda i, j, k: (k, 0))
    att_k = pl.BlockSpec((tk, D_ATT), lambda i, j, k: (k, 0))
    (dwo_n,) = _mm("mix_dwo", [u, dxo], [tok_k, tok_k], [(0, 1, 0)], 1, (D_MODEL, D_MODEL), TN, (1, 1, T // tk), one,
                   [jax.ShapeDtypeStruct((D_MODEL, D_MODEL), BF16)], [_whole((D_MODEL, D_MODEL))])

    def plain2(vals, ins, outs, i):
        outs[0][...] = vals[0]
        outs[1][...] = vals[1]

    ash = jax.ShapeDtypeStruct((T, D_ATT), F32)
    do_dil, do_sb = _mm(
        "mix_do", [dpd, dps, wpd_n, wps_n], [_tok(D_MODEL), _tok(D_MODEL), _whole(wpd_n.shape), _whole(wps_n.shape)],
        [(0, 2, 0), (1, 3, 1)], 2, None, NT, (T // TM, 1, 1), plain2, [ash, ash], [_tok(D_ATT)] * 2)

    psh = jax.ShapeDtypeStruct((D_ATT, D_MODEL), BF16)
    dwpd_n, dwps_n = _mm(
        "mix_dwp", [o_dil, o_sb, dpd, dps], [att_k, att_k, tok_k, tok_k], [(0, 2, 0), (1, 3, 1)], 2,
        (D_ATT, D_MODEL), TN, (1, 1, T // tk), two, [psh, psh], [_whole((D_ATT, D_MODEL))] * 2)
    dwpd, dwps = (w.reshape(D_ATT, N_CHIPS, cp).transpose(1, 0, 2) for w in (dwpd_n, dwps_n))
    dwo = dwo_n.reshape(N_CHIPS, cp, D_MODEL)

    dqs, dks, dvs = [], [], []
    for g, (window, dil) in enumerate(DIL_GROUPS):
        dq, dk, dv = _dil_bwd(proj, do_dil, o_dil, lse, g, dil)
        dqs.append(dq)
        dks.append(dk)
        dvs.append(dv)
    dq_s, dk_s, dv_s = _sb_bwd(proj, do_sb, o_sb)
    dproj = _assemble_dproj(dqs + dks, dvs + [dq_s], [dk_s, dv_s], [dgd, dgs], tabs)

    dx, dgain = _mm(
        "mix_dx", [dproj, win, x, gain, dxo],
        [pl.BlockSpec((tm, N_CHIPS * cin), lambda i, j, k: (i, 0)), _wfull(D_MODEL, cin, l), row, _gain_spec(), row],
        [(_cols(0, c, cin), _pick(1, c), 0) for c in range(N_CHIPS)], 1, None, NT, (T // tm, 1, 1),
        _rms_bwd_epilogue(2, 3, 4),
        [jax.ShapeDtypeStruct((T, D_MODEL), F32), jax.ShapeDtypeStruct((8, D_MODEL), F32)],
        [row, pl.BlockSpec((8, D_MODEL), lambda i, j, k: (0, 0))])

    (dwin,) = _mm(
        "mix_dwin", [h, dproj],
        [pl.BlockSpec((tk, D_MODEL), lambda i, j, k: (k, 0)), pl.BlockSpec((tk, cin), lambda i, j, k: (k, j))],
        [(0, 1, 0)], 1, (D_MODEL, cin), TN, (1, N_CHIPS, T // tk), one,
        [jax.ShapeDtypeStruct((N_CHIPS, D_MODEL, cin), BF16)],
        [pl.BlockSpec((None, D_MODEL, cin), lambda i, j, k: (j, 0, 0))])
    return dx, dgain, dwin, dwpd, dwps, dwo


def _local_step(x, target, norms, norm_final, weights_of, on_grads):
    T = x.shape[0]
    tabs = _rope_tables(T)
    saved, held = [], []
    for l in range(DEPTH):
        w1 = weights_of(l, 0, x)
        x, s1, w1["ffn1_w_down"] = _ffn_fwd(x, norms["norm_ffn1"][l:l + 1], w1["ffn1_w_gate"], w1["ffn1_w_up"],
                                             w1["ffn1_w_down"])
        w2 = weights_of(l, 1, x)
        x, s2 = _mixer_fwd(x, norms["norm_mix"][l:l + 1], w2, 0, tabs)
        w3 = weights_of(l, 2, x)
        x, s3, _ = _ffn_fwd(x, norms["norm_ffn2"][l:l + 1], w3["ffn2_w_gate"], w3["ffn2_w_up"], w3["ffn2_w_down"])
        saved.append((s1, s2, s3))
        held.append((w1, w2, w3))
    dx, dg_final, loss = _final_loss(x, norm_final.reshape(1, D_MODEL), target)
    gains = [None] * DEPTH
    for l in reversed(range(DEPTH)):
        s1, s2, s3 = saved[l]
        w1, w2, w3 = held[l]
        dx, dg2, dwg2, dwu2, dwd2 = _ffn_bwd(dx, norms["norm_ffn2"][l:l + 1], w3["ffn2_w_gate"], w3["ffn2_w_up"],
                                             w3["ffn2_w_down"], s3)
        dx = on_grads(l, 2, dict(ffn2_w_gate=dwg2, ffn2_w_up=dwu2, ffn2_w_down=dwd2), dx)
        dx, dgm, dwin, dwpd, dwps, dwo = _mixer_bwd(dx, norms["norm_mix"][l:l + 1], w2, 0, tabs, s2)
        dx = on_grads(l, 1, dict(w_in=dwin, w_proj_dil=dwpd, w_proj_sb=dwps, w_out=dwo), dx)
        dx, dg1, dwg1, dwu1, dwd1 = _ffn_bwd(dx, norms["norm_ffn1"][l:l + 1], w1["ffn1_w_gate"], w1["ffn1_w_up"],
                                             w1["ffn1_w_down"], s1)
        dx = on_grads(l, 0, dict(ffn1_w_gate=dwg1, ffn1_w_up=dwu1, ffn1_w_down=dwd1), dx)
        gains[l] = dict(norm_ffn1=dg1, norm_mix=dgm, norm_ffn2=dg2)
    return loss, dx, gains, dg_final


def _place():
    x, y, c = lax.axis_index("x"), lax.axis_index("y"), lax.axis_index("c")
    chips = [(1 - x, y), (x, 1 - y), (1 - x, 1 - y)]
    return x, y, c, chips


def _half(c, r):
    return pl.ds(pl.multiple_of(c * (r // 2), 8), r // 2)


def _cast_into_slot(ws, ls, me_arr, after):
    n = len(ws)
    late = [] if after is None else [after]

    def body(me_ref, *refs):
        for a in range(n):
            refs[len(refs) - n + a][...] = refs[a][...].astype(BF16)

    def src(w, l):
        return pl.BlockSpec((None, w.shape[1] // 4, w.shape[2]), lambda i, me: (l, i, 0))

    def dst(w):
        return pl.BlockSpec((None, None, w.shape[1] // 4, w.shape[2]), lambda i, me: (me[0], 0, i, 0))

    return pl.pallas_call(
        body, name="cast_weights",
        grid_spec=pltpu.PrefetchScalarGridSpec(
            num_scalar_prefetch=1, grid=(4,),
            in_specs=[src(w, l) for w, l in zip(ws, ls)] + [pl.BlockSpec(memory_space=pl.ANY)] * len(late),
            out_specs=[dst(w) for w in ws]),
        out_shape=[jax.ShapeDtypeStruct((N_CHIPS, 1) + w.shape[1:], BF16) for w in ws], compiler_params=_params(),
    )(me_arr, *ws, *late)


HBM_SPEC = pl.BlockSpec(memory_space=pltpu.HBM)
SEM_SPEC = pl.BlockSpec(memory_space=pltpu.SEMAPHORE)
SPLIT_COPY = pltpu.CompilerParams(has_side_effects=pltpu.SideEffectType.DATAFLOW_SIDE_EFFECTING)


def _gather_piece(ref, chip_id, c):
    return ref.at[chip_id, 0, _half(c, ref.shape[2]), :]


def _gather_start(tag, bufs, direct):
    n = len(bufs)

    def body(*refs):
        out_refs = refs[n:2 * n]
        send_sems, recv_sems, token = refs[2 * n:]
        x, y, c, chips = _place()
        me = 2 * x + y
        for a in range(n):
            piece = _gather_piece(out_refs[a], me, c)
            for j, chip in enumerate(chips):
                for to in ((0, 1) if direct[a] else (c,)):
                    pltpu.make_async_remote_copy(
                        src_ref=piece, dst_ref=piece, send_sem=send_sems.at[6 * a + 2 * j + to],
                        recv_sem=recv_sems.at[6 * a + 2 * j + c], device_id=(*chip, to), device_id_type=MESH).start()
        token[...] = jnp.zeros_like(token)

    outs = pl.pallas_call(
        body, name=f"gather_start_{tag}", in_specs=[HBM_SPEC] * n,
        out_specs=[HBM_SPEC] * n + [SEM_SPEC, SEM_SPEC, pl.BlockSpec(memory_space=pltpu.VMEM)],
        out_shape=[pltpu.HBM(b.shape, b.dtype) for b in bufs] + [pltpu.SemaphoreType.DMA((6 * n,))] * 2
        + [jax.ShapeDtypeStruct((8, 128), F32)],
        input_output_aliases={a: a for a in range(n)}, compiler_params=SPLIT_COPY,
    )(*[pltpu.with_memory_space_constraint(b, pltpu.HBM) for b in bufs])
    return outs[:n], outs[n], outs[n + 1], outs[n + 2]


def _gather_wait(k, bufs, places, direct, send_sems, recv_sems, after):
    m = len(bufs)

    def body(*refs):
        in_refs = refs[:m]
        ssem, rsem = refs[m], refs[m + 1]
        x, y, c, chips = _place()
        me = 2 * x + y
        for t, a in enumerate(places):
            for j, chip in enumerate(chips):
                for core in ((0, 1) if direct else (c,)):
                    cp = pltpu.make_async_remote_copy(
                        src_ref=_gather_piece(in_refs[t], me, c),
                        dst_ref=_gather_piece(in_refs[t], 2 * chip[0] + chip[1], core),
                        send_sem=ssem.at[6 * a + 2 * j + core], recv_sem=rsem.at[6 * a + 2 * j + core],
                        device_id=(*chip, core), device_id_type=MESH)
                    cp.wait_send()
                    cp.wait_recv()

    return pl.pallas_call(
        body, name=f"gather_wait_{k}",
        in_specs=[HBM_SPEC] * m + [SEM_SPEC, SEM_SPEC, pl.BlockSpec(memory_space=pl.ANY)], out_specs=[HBM_SPEC] * m,
        out_shape=[pltpu.HBM(b.shape, b.dtype) for b in bufs], input_output_aliases={t: t for t in range(m)},
        compiler_params=SPLIT_COPY,
    )(*bufs, send_sems, recv_sems, after)


def _gather_relay(bufs):
    n = len(bufs)

    def body(*refs):
        out_refs = refs[n:2 * n]
        send_sems, recv_sems = refs[2 * n:]
        x, y, c, chips = _place()
        cps = []
        for a in range(n):
            for j, chip in enumerate(chips):
                piece = _gather_piece(out_refs[a], 2 * chip[0] + chip[1], c)
                cps.append(pltpu.make_async_remote_copy(
                    src_ref=piece, dst_ref=piece, send_sem=send_sems.at[a, j], recv_sem=recv_sems.at[a, j],
                    device_id=(x, y, 1 - c), device_id_type=MESH))
        for cp in cps:
            cp.start()
        for a in range(n):
            for j, chip in enumerate(chips):
                theirs = _gather_piece(out_refs[a], 2 * chip[0] + chip[1], 1 - c)
                pltpu.make_async_remote_copy(
                    src_ref=theirs, dst_ref=theirs, send_sem=send_sems.at[a, j], recv_sem=recv_sems.at[a, j],
                    device_id=(x, y, 1 - c), device_id_type=MESH).wait_recv()
        for cp in cps:
            cp.wait_send()

    any_spec = pl.BlockSpec(memory_space=pl.ANY)
    return pl.pallas_call(
        body, name="gather_relay", in_specs=[any_spec] * n, out_specs=[any_spec] * n,
        out_shape=[jax.ShapeDtypeStruct(b.shape, b.dtype) for b in bufs],
        input_output_aliases={a: a for a in range(n)},
        scratch_shapes=[pltpu.SemaphoreType.DMA((n, 3))] * 2,
    )(*bufs)


def _other_half(ref, c):
    return ref.at[:, _half(1 - c, ref.shape[1]), :]


def _all_of(ref, c):
    return ref


def _sibling_start(name, srcs, pick, land_shapes, thru):
    n = len(srcs)
    lands = [lax.empty(sh, s.dtype) for sh, s in zip(land_shapes, srcs)]
    kept = lands + ([] if thru is None else [thru])
    m = len(kept)

    def body(*refs):
        s_refs, land_refs = refs[:n], refs[n + m:n + m + n]
        send_sems, recv_sems, token = refs[n + 2 * m:]
        x, y, c, _ = _place()
        for a in range(n):
            pltpu.make_async_remote_copy(
                src_ref=pick(s_refs[a], c), dst_ref=land_refs[a], send_sem=send_sems.at[a],
                recv_sem=recv_sems.at[a], device_id=(x, y, 1 - c), device_id_type=MESH).start()
        token[...] = jnp.zeros_like(token)

    outs = pl.pallas_call(
        body, name=name, in_specs=[HBM_SPEC] * (n + m),
        out_specs=[HBM_SPEC] * m + [SEM_SPEC, SEM_SPEC, pl.BlockSpec(memory_space=pltpu.VMEM)],
        out_shape=[pltpu.HBM(v.shape, v.dtype) for v in kept] + [pltpu.SemaphoreType.DMA((n,))] * 2
        + [jax.ShapeDtypeStruct((8, 128), F32)],
        input_output_aliases={n + a: a for a in range(m)}, compiler_params=SPLIT_COPY,
    )(*[pltpu.with_memory_space_constraint(v, pltpu.HBM) for v in list(srcs) + kept])
    return (outs[:n], outs[m], outs[m + 1]), (outs[n] if thru is not None else None), outs[m + 2]


def _sibling_wait(name, srcs, pick, lands, send_sems, recv_sems, after):
    n = len(srcs)

    def body(*refs):
        s_refs, land_refs = refs[:n], refs[n:2 * n]
        ssem, rsem = refs[2 * n], refs[2 * n + 1]
        x, y, c, _ = _place()
        for a in range(n):
            cp = pltpu.make_async_remote_copy(
                src_ref=pick(s_refs[a], c), dst_ref=land_refs[a], send_sem=ssem.at[a], recv_sem=rsem.at[a],
                device_id=(x, y, 1 - c), device_id_type=MESH)
            cp.wait_send()
            cp.wait_recv()

    return pl.pallas_call(
        body, name=name, in_specs=[HBM_SPEC] * (2 * n) + [SEM_SPEC, SEM_SPEC, pl.BlockSpec(memory_space=pl.ANY)],
        out_specs=[HBM_SPEC] * n, out_shape=[pltpu.HBM(v.shape, v.dtype) for v in lands],
        input_output_aliases={n + a: a for a in range(n)}, compiler_params=SPLIT_COPY,
    )(*srcs, *lands, send_sems, recv_sems, after)


def _add_half(gs, gots, c_arr):
    n = len(gs)

    def body(c_ref, *refs):
        for a in range(n):
            refs[2 * n + a][...] = (refs[a][...].astype(F32) + refs[n + a][...].astype(F32)).astype(BF16)

    def own(g):
        return pl.BlockSpec((None, g.shape[1] // 2, g.shape[2]), lambda k, cr: (k, cr[0], 0))

    def half(g):
        return pl.BlockSpec((None, g.shape[1] // 2, g.shape[2]), lambda k, cr: (k, 0, 0))

    return pl.pallas_call(
        body, name="grad_add_half",
        grid_spec=pltpu.PrefetchScalarGridSpec(
            num_scalar_prefetch=1, grid=(N_CHIPS,),
            in_specs=[own(g) for g in gs] + [half(g) for g in gs], out_specs=[half(g) for g in gs]),
        out_shape=[jax.ShapeDtypeStruct(got.shape, BF16) for got in gots], compiler_params=_params(),
    )(c_arr, *gs, *gots)


def _scatter_start(k, ss, thru):
    n = len(ss)

    def body(*refs):
        s_refs, land_refs = refs[2 * n + 1:3 * n + 1], refs[3 * n + 1:4 * n + 1]
        send_sems, recv_sems = refs[4 * n + 2:]
        x, y, c, chips = _place()
        me = 2 * x + y
        for a in range(n):
            for j, chip in enumerate(chips):
                pltpu.make_async_remote_copy(
                    src_ref=s_refs[a].at[2 * chip[0] + chip[1]], dst_ref=land_refs[a].at[me],
                    send_sem=send_sems.at[3 * a + j], recv_sem=recv_sems.at[3 * a + j], device_id=(*chip, c),
                    device_id_type=MESH).start()

    lands = [lax.empty(s.shape, s.dtype) for s in ss]
    hbm = [pltpu.HBM(s.shape, s.dtype) for s in ss]
    outs = pl.pallas_call(
        body, name=f"grad_scatter_start_{k}", in_specs=[HBM_SPEC] * (2 * n + 1),
        out_specs=[HBM_SPEC] * (2 * n + 1) + [SEM_SPEC, SEM_SPEC],
        out_shape=hbm + hbm + [pltpu.HBM(thru.shape, thru.dtype)] + [pltpu.SemaphoreType.DMA((3 * n,))] * 2,
        input_output_aliases={a: a for a in range(2 * n + 1)}, compiler_params=SPLIT_COPY,
    )(*[pltpu.with_memory_space_constraint(v, pltpu.HBM) for v in list(ss) + lands + [thru]])
    return (outs[:n], outs[n:2 * n], outs[2 * n + 1], outs[2 * n + 2]), outs[2 * n]


def _scatter_wait(k, ss, lands, send_sems, recv_sems, after):
    n = len(ss)

    def body(*refs):
        s_refs, land_refs = refs[:n], refs[n:2 * n]
        ssem, rsem = refs[2 * n], refs[2 * n + 1]
        x, y, c, chips = _place()
        me = 2 * x + y
        for a in range(n):
            for j, chip in enumerate(chips):
                cid = 2 * chip[0] + chip[1]
                cp = pltpu.make_async_remote_copy(
                    src_ref=s_refs[a].at[cid], dst_ref=land_refs[a].at[cid], send_sem=ssem.at[3 * a + j],
                    recv_sem=rsem.at[3 * a + j], device_id=(*chip, c), device_id_type=MESH)
                cp.wait_send()
                cp.wait_recv()

    hbm = [pltpu.HBM(s.shape, s.dtype) for s in ss]
    outs = pl.pallas_call(
        body, name=f"grad_scatter_wait_{k}",
        in_specs=[HBM_SPEC] * (2 * n) + [SEM_SPEC, SEM_SPEC, pl.BlockSpec(memory_space=pl.ANY)],
        out_specs=[HBM_SPEC] * (2 * n), out_shape=hbm + hbm,
        input_output_aliases={a: a for a in range(2 * n)}, compiler_params=SPLIT_COPY,
    )(*ss, *lands, send_sems, recv_sems, after)
    return outs[:n], outs[n:]


def _sum_chips(lands, ss, me_arr):
    n = len(lands)

    def body(me_ref, *refs):
        for own in range(N_CHIPS):
            @pl.when(me_ref[0] == own)
            def _(own=own):
                for a in range(n):
                    acc = None
                    for k in range(N_CHIPS):
                        term = (refs[n + a][...] if k == own else refs[a][k]).astype(F32)
                        acc = term if acc is None else acc + term
                    refs[2 * n + a][...] = acc

    return pl.pallas_call(
        body, name="grad_sum_chips",
        grid_spec=pltpu.PrefetchScalarGridSpec(
            num_scalar_prefetch=1, grid=(1,),
            in_specs=[pl.BlockSpec(la.shape, lambda i, me: (0, 0, 0)) for la in lands]
            + [pl.BlockSpec((None,) + la.shape[1:], lambda i, me: (me[0], 0, 0)) for la in lands],
            out_specs=[pl.BlockSpec(la.shape[1:], lambda i, me: (0, 0)) for la in lands]),
        out_shape=[jax.ShapeDtypeStruct(la.shape[1:], F32) for la in lands], compiler_params=_params(),
    )(me_arr, *lands, *ss)


def _allreduce_rows(stats):
    def body(s_ref, o_ref, buf, send_sems, recv_sems):
        x, y, c, _ = _place()
        me = 4 * x + 2 * y + c
        buf[me] = s_ref[...]
        cps = []
        for k in range(1, 8):
            px = jnp.where(k & 4, 1 - x, x)
            py = jnp.where(k & 2, 1 - y, y)
            pc = jnp.where(k & 1, 1 - c, c)
            cps.append(pltpu.make_async_remote_copy(
                src_ref=s_ref, dst_ref=buf.at[me], send_sem=send_sems.at[k - 1], recv_sem=recv_sems.at[k - 1],
                device_id=(px, py, pc), device_id_type=MESH))
        for cp in cps:
            cp.start()
        for cp in cps:
            cp.wait()
        acc = buf[0]
        for d in range(1, 8):
            acc = acc + buf[d]
        o_ref[...] = acc

    vm = pl.BlockSpec(memory_space=pltpu.VMEM)
    return pl.pallas_call(
        body, name="allreduce_rows", in_specs=[vm], out_specs=vm,
        out_shape=jax.ShapeDtypeStruct(stats.shape, F32),
        scratch_shapes=[pltpu.VMEM((8,) + stats.shape, F32), pltpu.SemaphoreType.DMA((7,)),
                        pltpu.SemaphoreType.DMA((7,))],
    )(stats)


def _adamw_math(w, g, m, v):
    m = ADAM_B1 * m + (1.0 - ADAM_B1) * g
    v = ADAM_B2 * v + (1.0 - ADAM_B2) * (g * g)
    m_hat = m / (1.0 - ADAM_B1 ** ADAM_STEP)
    v_hat = v / (1.0 - ADAM_B2 ** ADAM_STEP)
    delta = -ADAM_LR * (m_hat / (jnp.sqrt(v_hat) + ADAM_EPS) + ADAM_WD * w)
    return delta, m, v


def _adamw(ws, ms, vs, mines, theirs, l, c_arr, earlier, after):
    n = len(ws)
    held = [t for e in earlier if e is not None for t in e]
    assert len(held) in (0, 4 * n)
    late = [] if after is None else [after]

    def body(c_ref, *refs):
        outs = refs[len(refs) - 4 * n:]
        for a in range(n):
            w_ref, m_ref, v_ref, a_ref, b_ref = refs[5 * a:5 * a + 5]
            g = jnp.where(pl.program_id(0) == c_ref[0], a_ref[...], b_ref[...])
            delta, mn, vn = _adamw_math(w_ref[...], g, m_ref[...], v_ref[...])
            outs[4 * a][...] = g
            outs[4 * a + 1][...] = delta
            outs[4 * a + 2][...] = mn
            outs[4 * a + 3][...] = vn

    def blk(w):
        tr = w.shape[1] // 4
        return pl.BlockSpec((None, tr, w.shape[2]), lambda hh, i, cr: (l, 2 * hh + i, 0))

    def half(w):
        return pl.BlockSpec((w.shape[1] // 4, w.shape[2]), lambda hh, i, cr: (i, 0))

    outs = pl.pallas_call(
        body, name="adamw",
        grid_spec=pltpu.PrefetchScalarGridSpec(
            num_scalar_prefetch=1, grid=(2, 2),
            in_specs=[sp for w in ws for sp in (blk(w), blk(w), blk(w), half(w), half(w))]
            + [pl.BlockSpec(memory_space=pl.ANY)] * (len(held) + len(late)),
            out_specs=[blk(w) for w in ws for _ in range(4)]),
        out_shape=[jax.ShapeDtypeStruct(w.shape, F32) for w in ws for _ in range(4)],
        input_output_aliases={1 + 5 * n + t: t for t in range(len(held))}, compiler_params=_params(),
    )(c_arr, *[t for grp in zip(ws, ms, vs, mines, theirs) for t in grp], *held, *late)
    return [outs[4 * a:4 * a + 4] for a in range(n)]


def _adamw_rows(w, m, v, g):
    def body(w_ref, m_ref, v_ref, g_ref, d_ref, mo_ref, vo_ref):
        delta, mn, vn = _adamw_math(w_ref[...], g_ref[...], m_ref[...], v_ref[...])
        d_ref[...] = delta
        mo_ref[...] = mn
        vo_ref[...] = vn

    vm = pl.BlockSpec(memory_space=pltpu.VMEM)
    sh = jax.ShapeDtypeStruct(w.shape, F32)
    return pl.pallas_call(body, name="adamw_rows", in_specs=[vm] * 4, out_specs=[vm] * 3, out_shape=[sh] * 3)(w, m, v, g)


SUBLAYERS = (("ffn1_w_gate", "ffn1_w_up", "ffn1_w_down"), ("w_in", "w_proj_dil", "w_proj_sb", "w_out"),
             ("ffn2_w_gate", "ffn2_w_up", "ffn2_w_down"))
TRANSPOSED = ("ffn1_w_gate", "ffn1_w_up", "ffn2_w_gate", "ffn2_w_up")


def _pick_row(blocks):
    row = lax.broadcasted_iota(jnp.int32, (8, D_MODEL), 0)
    out = jnp.zeros((8, D_MODEL), F32)
    for i, b in enumerate(blocks):
        out = out + jnp.where(row == i, b, 0.0)
    return out


def kernel(x, norm_ffn1, ffn1_w_gate, ffn1_w_up, ffn1_w_down, norm_mix, w_in, w_proj_dil, w_proj_sb, w_out, norm_ffn2, ffn2_w_gate, ffn2_w_up, ffn2_w_down, norm_final, loss_target, m_norm_ffn1, m_ffn1_w_gate, m_ffn1_w_up, m_ffn1_w_down, m_norm_mix, m_w_in, m_w_proj_dil, m_w_proj_sb, m_w_out, m_norm_ffn2, m_ffn2_w_gate, m_ffn2_w_up, m_ffn2_w_down, m_norm_final, v_norm_ffn1, v_ffn1_w_gate, v_ffn1_w_up, v_ffn1_w_down, v_norm_mix, v_w_in, v_w_proj_dil, v_w_proj_sb, v_w_out, v_norm_ffn2, v_ffn2_w_gate, v_ffn2_w_up, v_ffn2_w_down, v_norm_final):
    given = dict(locals())
    for n in TRANSPOSED:
        for k in ("", "m_", "v_"):
            given[k + n] = jnp.swapaxes(given[k + n], 1, 2)
    weights = {n: given[n] for n in WEIGHT_NAMES}
    norms = {n: given[n] for n in NORM_NAMES}

    c_arr = lax.axis_index("c").astype(jnp.int32).reshape(1)
    me_arr = (2 * lax.axis_index("x") + lax.axis_index("y")).astype(jnp.int32).reshape(1)
    order = [(l, s, n) for l in range(DEPTH) for s in range(len(SUBLAYERS)) for n in SUBLAYERS[s]]
    n_first, n_second = len(SUBLAYERS[0]) - 1, len(SUBLAYERS[0]) + len(SUBLAYERS[1])
    sent, token = {}, None
    for tag, idxs in (("a", range(n_first)), ("b", range(n_first, n_second)), ("c", range(n_second, len(order)))):
        cast = _cast_into_slot([weights[order[i][2]] for i in idxs], [order[i][0] for i in idxs], me_arr, token)
        bufs, send_sems, recv_sems, token = _gather_start(tag, cast, [order[i][0] > 0 for i in idxs])
        for p, i in enumerate(idxs):
            sent[i] = (bufs[p], p, send_sems, recv_sems)

    def arrived(tag, idxs, after):
        direct = order[idxs[0]][0] > 0
        got = _gather_wait(tag, [sent[i][0] for i in idxs], [sent[i][1] for i in idxs], direct,
                           sent[idxs[0]][2], sent[idxs[0]][3], after)
        return {order[i][2]: g for i, g in zip(idxs, got if direct else _gather_relay(got))}

    def weights_of(l, s, after):
        idxs = [i for i, (ll, ss, _) in enumerate(order) if (ll, ss) == (l, s)]
        k = len(SUBLAYERS) * l + s
        if k > 0:
            return arrived(k, idxs, after)
        last = order[idxs[-1]][2]
        got = arrived(f"{k}_first", idxs[:-1], after)
        got[last] = lambda after: arrived(f"{k}_last", idxs[-1:], after)[last]
        return got

    out = {}
    to_add, in_flight = [], []

    def add_and_scatter(after):
        l, s, names, gs, lands, ssem, rsem = to_add.pop(0)
        k = len(SUBLAYERS) * l + s
        got = _sibling_wait(f"grad_exchange_wait_{k}", gs, _other_half, lands, ssem, rsem, after)
        sent, after = _scatter_start(k, _add_half(gs, got, c_arr), after)
        in_flight.append((l, s, names) + sent)
        return after

    def on_grads(l, s, grads, after):
        names = list(grads)
        k = len(SUBLAYERS) * l + s
        gs = [grads[n] for n in names]
        sent, after, _ = _sibling_start(f"grad_exchange_start_{k}", gs, _other_half,
                                        [(g.shape[0], g.shape[1] // 2, g.shape[2]) for g in gs], after)
        if to_add:
            after = add_and_scatter(after)
        to_add.append((l, s, names, gs) + sent)
        return after

    loss_blk, grad_x, gains, dg_final = _local_step(x[0], loss_target[0], norms, norm_final, weights_of, on_grads)
    grad_x = add_and_scatter(grad_x)

    def update(l, names, mine, swap, after):
        theirs = _sibling_wait(f"grad_swap_wait_{l}_{names[0]}", mine, _all_of, *swap, grad_x if after is None else after)
        res = _adamw([weights[n] for n in names], [given["m_" + n] for n in names], [given["v_" + n] for n in names],
                     mine, theirs, l, c_arr, [out.get(n) for n in names], after)
        out.update(zip(names, res))

    waiting = None
    for l, s, names, sums, lands, ssem, rsem in in_flight:
        sums, lands = _scatter_wait(len(SUBLAYERS) * l + s, sums, lands, ssem, rsem, grad_x)
        mine = _sum_chips(lands, sums, me_arr)
        swap, _, token = _sibling_start(f"grad_swap_start_{l}_{names[0]}", mine, _all_of,
                                        [m.shape for m in mine], None)
        if waiting is not None:
            update(*waiting, token)
        waiting = (l, names, mine, swap)
    update(*waiting, None)
    out = {k + n: (jnp.swapaxes(v, 1, 2) if n in TRANSPOSED else v)
           for n, res in out.items() for k, v in zip(("grad_", "delta_", "new_m_", "new_v_"), res)}
    out["grad_x"] = grad_x[None]

    rows = [gains[l][n] for n in NORM_NAMES for l in range(DEPTH)] + [dg_final, loss_blk]
    total = _allreduce_rows(_pick_row(rows))
    out["loss"] = total[7, 0]
    wn = jnp.concatenate([given[n] for n in NORM_NAMES] + [norm_final[None], jnp.zeros((1, D_MODEL), F32)])
    mn_ = jnp.concatenate([given["m_" + n] for n in NORM_NAMES] + [m_norm_final[None], jnp.zeros((1, D_MODEL), F32)])
    vn_ = jnp.concatenate([given["v_" + n] for n in NORM_NAMES] + [v_norm_final[None], jnp.ones((1, D_MODEL), F32)])
    d_n, m_n, v_n = _adamw_rows(wn, mn_, vn_, total)
    for i, n in enumerate(NORM_NAMES):
        sl = slice(i * DEPTH, (i + 1) * DEPTH)
        out["grad_" + n], out["delta_" + n], out["new_m_" + n], out["new_v_" + n] = total[sl], d_n[sl], m_n[sl], v_n[sl]
    out["grad_norm_final"], out["delta_norm_final"] = total[6], d_n[6]
    out["new_m_norm_final"], out["new_v_norm_final"] = m_n[6], v_n[6]

    names = ["norm_ffn1", "ffn1_w_gate", "ffn1_w_up", "ffn1_w_down", "norm_mix", "w_in", "w_proj_dil", "w_proj_sb",
             "w_out", "norm_ffn2", "ffn2_w_gate", "ffn2_w_up", "ffn2_w_down", "norm_final"]
    return (out["loss"], out["grad_x"], *[out["grad_" + n] for n in names], *[out["delta_" + n] for n in names],
            *[out["new_m_" + n] for n in names], *[out["new_v_" + n] for n in names])
```

```python
import functools

import jax
import jax.numpy as jnp
from jax import lax
from jax.experimental import pallas as pl
from jax.experimental.pallas import tpu as pltpu

F32 = jnp.float32
BF16 = jnp.bfloat16

D_MODEL = 1024
DEPTH = 2
N_CHIPS = 4
HEAD_DIM = 64
ROPE_DIM = 16
ROPE_THETA = 500000.0
DIL_GROUPS = ((128, 1), (512, 4), (2048, 16))
SPAN = 128
Q_BLOCK = 128
RMS_EPS = 1e-6
D_ATT = 256
COL_QS = 2304
COL_GD = 3072
COL_GS = 4096
ADAM_LR, ADAM_B1, ADAM_B2, ADAM_EPS, ADAM_WD, ADAM_STEP = 0.001, 0.9, 0.999, 1e-08, 0.01, 10

VMEM_LIMIT = 52 * 1024 * 1024
TM = 512
NEG = -1e30

NN = (((1,), (0,)), ((), ()))
NT = (((1,), (1,)), ((), ()))
TN = (((0,), (0,)), ((), ()))
MESH = pl.DeviceIdType.MESH

WEIGHT_NAMES = ("ffn1_w_gate", "ffn1_w_up", "ffn1_w_down", "w_in", "w_proj_dil",
                "w_proj_sb", "w_out", "ffn2_w_gate", "ffn2_w_up", "ffn2_w_down")
NORM_NAMES = ("norm_ffn1", "norm_mix", "norm_ffn2")


def _params(**kw):
    return pltpu.CompilerParams(vmem_limit_bytes=VMEM_LIMIT, **kw)


def _sigmoid(x):
    return 0.5 * jnp.tanh(0.5 * x) + 0.5


def _mm_body(pairs, n_in, n_out, n_acc, dims, nk, epilogue, *refs):
    ins = refs[:n_in]
    outs = refs[n_in:n_in + n_out]
    accs = refs[n_in + n_out:]
    i = pl.program_id(0)
    k = pl.program_id(2)

    def operand(a):
        return (a(ins) if callable(a) else ins[a][...]).astype(BF16)

    def dot(ia, ib):
        return lax.dot_general(operand(ia), operand(ib), dims, preferred_element_type=F32)

    if nk == 1:
        parts = [None] * n_acc
        for ia, ib, ic in pairs:
            parts[ic] = dot(ia, ib) if parts[ic] is None else parts[ic] + dot(ia, ib)
        epilogue(parts, ins, outs, i)
        return

    @pl.when(k == 0)
    def _():
        for c in range(n_acc):
            accs[c][...] = jnp.zeros_like(accs[c])

    for ia, ib, ic in pairs:
        accs[ic][...] += dot(ia, ib)

    @pl.when(k == nk - 1)
    def _():
        epilogue([a[...] for a in accs], ins, outs, i)


def _mm(name, ins, in_specs, pairs, n_acc, acc_shape, dims, grid, epilogue, out_shapes, out_specs):
    nk = grid[2]
    scratch = [pltpu.VMEM(acc_shape, F32) for _ in range(n_acc)] if nk > 1 else []
    body = functools.partial(_mm_body, tuple(pairs), len(ins), len(out_shapes), n_acc, dims, nk, epilogue)
    return pl.pallas_call(
        body, name=name, grid=grid, in_specs=in_specs, out_specs=out_specs, out_shape=out_shapes,
        scratch_shapes=scratch,
        compiler_params=_params(dimension_semantics=("arbitrary", "arbitrary", "arbitrary")),
    )(*ins)


def _rms_bwd_epilogue(x_idx, g_idx, dxo_idx):
    def ep(vals, ins, outs, i):
        dh = vals[0]
        x = ins[x_idx][...]
        g = ins[g_idx][...]
        rstd = lax.rsqrt(jnp.mean(x * x, axis=-1, keepdims=True) + RMS_EPS)
        xhat = x * rstd
        dxhat = dh * g
        dx = rstd * (dxhat - xhat * jnp.mean(dxhat * xhat, axis=-1, keepdims=True))
        outs[0][...] = ins[dxo_idx][...] + dx
        dg = jnp.broadcast_to(jnp.sum(dh * xhat, axis=0, keepdims=True), outs[1].shape)

        @pl.when(i == 0)
        def _():
            outs[1][...] = dg

        @pl.when(i > 0)
        def _():
            outs[1][...] += dg
    return ep


def _normed(x_idx, g_idx):
    seen = {}

    def f(ins):
        if id(ins) not in seen:
            xv = ins[x_idx][...]
            h = xv * lax.rsqrt(jnp.mean(xv * xv, axis=-1, keepdims=True) + RMS_EPS)
            seen[id(ins)] = (ins, (h * ins[g_idx][...]).astype(BF16))
        return seen[id(ins)][1]
    return f


def _rope_tables(T):
    half = ROPE_DIM // 2
    lane = jnp.arange(128) % HEAD_DIM
    inv_freq = ROPE_THETA ** (-(2 * (lane % half)).astype(F32) / ROPE_DIM)
    ang = jnp.arange(T, dtype=F32)[:, None] * inv_freq[None, :]
    cos, sin = jnp.cos(ang), jnp.sin(ang)
    c = jnp.where(lane < ROPE_DIM, cos, 1.0)
    s1 = jnp.where(lane < half, -sin, 0.0)
    s2 = jnp.where((lane >= half) & (lane < ROPE_DIM), sin, 0.0)
    return c, s1, s2


def _rope_fwd(xv, c, s1, s2):
    w = xv.shape[1]
    half = ROPE_DIM // 2
    return xv * c + pltpu.roll(xv, w - half, 1) * s1 + pltpu.roll(xv, half, 1) * s2


def _rope_bwd(dy, c, s1, s2):
    w = dy.shape[1]
    half = ROPE_DIM // 2
    return dy * c + pltpu.roll(dy * s1, half, 1) + pltpu.roll(dy * s2, w - half, 1)


def _assemble_dproj(dqk, rest, turned, gates, tabs):
    T = gates[0].shape[0]
    n_qk, n_rest = len(dqk), len(rest) + len(turned)
    width = (n_qk + n_rest) * D_ATT + 2 * D_MODEL

    def body(*refs):
        ins, (c_ref, s1_ref, s2_ref), o_ref = refs[:n_qk + n_rest + 2], refs[-4:-1], refs[-1]
        c = jnp.concatenate([c_ref[...]] * 2, axis=1)
        s1 = jnp.concatenate([s1_ref[...]] * 2, axis=1)
        s2 = jnp.concatenate([s2_ref[...]] * 2, axis=1)
        for b in range(n_qk + n_rest):
            v = ins[b][...]
            if b < n_qk:
                v = _rope_bwd(v, c, s1, s2)
            if b >= n_qk + len(rest):
                v = v.T
            o_ref[:, b * D_ATT:(b + 1) * D_ATT] = v.astype(BF16)
        off = (n_qk + n_rest) * D_ATT
        o_ref[:, off:off + D_MODEL] = ins[-2][...]
        o_ref[:, off + D_MODEL:] = ins[-1][...]

    att = pl.BlockSpec((TM, D_ATT), lambda i: (i, 0))
    att_turned = pl.BlockSpec((D_ATT, TM), lambda i: (0, i))
    wide = pl.BlockSpec((TM, D_MODEL), lambda i: (i, 0))
    tab = pl.BlockSpec((TM, 128), lambda i: (i, 0))
    return pl.pallas_call(
        body, name="assemble_dproj", grid=(T // TM,),
        in_specs=[att] * (n_qk + len(rest)) + [att_turned] * len(turned) + [wide, wide, tab, tab, tab],
        out_specs=pl.BlockSpec((TM, width), lambda i: (i, 0)),
        out_shape=jax.ShapeDtypeStruct((T, width), BF16), compiler_params=_params(),
    )(*dqk, *rest, *turned, *gates, *tabs)


def _dil_merge(os_, lses):
    T = os_[0].shape[0]

    def body(o0, o1, o2, l0, l1, l2, o_ref, lse_ref):
        a, b, c = l0[...], l1[...], l2[...]
        m = jnp.maximum(jnp.maximum(a, b), c)
        ea, eb, ec = jnp.exp(a - m), jnp.exp(b - m), jnp.exp(c - m)
        den = ea + eb + ec
        o_ref[...] = (ea * o0[...] + eb * o1[...] + ec * o2[...]) / den
        lse_ref[...] = m + jnp.log(den)

    blk = pl.BlockSpec((TM, D_ATT), lambda i: (i, 0))
    sh = jax.ShapeDtypeStruct((T, D_ATT), F32)
    return pl.pallas_call(
        body, name="dil_merge", grid=(T // TM,), in_specs=[blk] * 6, out_specs=[blk, blk],
        out_shape=[sh, sh], compiler_params=_params(),
    )(*os_, *lses)


def _final_loss(x, gain, target):
    T = x.shape[0]

    def body(x_ref, g_ref, t_ref, dx_ref, dg_ref, loss_ref):
        xv = x_ref[...]
        g = g_ref[...]
        rstd = lax.rsqrt(jnp.mean(xv * xv, axis=-1, keepdims=True) + RMS_EPS)
        xhat = xv * rstd
        err = xhat * g - t_ref[...]
        loss = 0.5 * jnp.sum(jnp.mean(err * err, axis=-1, keepdims=True), axis=0, keepdims=True)
        dy = err * (1.0 / D_MODEL)
        dxhat = dy * g
        dx_ref[...] = rstd * (dxhat - xhat * jnp.mean(dxhat * xhat, axis=-1, keepdims=True))
        dg = jnp.broadcast_to(jnp.sum(dy * xhat, axis=0, keepdims=True), dg_ref.shape)
        ls = jnp.broadcast_to(loss, loss_ref.shape)

        @pl.when(pl.program_id(0) == 0)
        def _():
            dg_ref[...] = dg
            loss_ref[...] = ls

        @pl.when(pl.program_id(0) > 0)
        def _():
            dg_ref[...] += dg
            loss_ref[...] += ls

    blk = pl.BlockSpec((TM, D_MODEL), lambda i: (i, 0))
    row = pl.BlockSpec((1, D_MODEL), lambda i: (0, 0))
    acc = pl.BlockSpec((8, D_MODEL), lambda i: (0, 0))
    return pl.pallas_call(
        body, name="final_loss", grid=(T // TM,), in_specs=[blk, row, blk], out_specs=[blk, acc, acc],
        out_shape=[jax.ShapeDtypeStruct((T, D_MODEL), F32), jax.ShapeDtypeStruct((8, D_MODEL), F32),
                   jax.ShapeDtypeStruct((8, D_MODEL), F32)],
        compiler_params=_params(dimension_semantics=("arbitrary",)),
    )(x, gain, target)


def _pair_masks():
    lane = lax.broadcasted_iota(jnp.int32, (SPAN, 128), 1)
    return [lane < HEAD_DIM, lane >= HEAD_DIM]


def _stack_heads(x, masks):
    return jnp.concatenate([jnp.where(m, x, 0.0) for m in masks], axis=0)


def _unstack_heads(y, masks):
    rows = y.shape[0] // len(masks)
    out = jnp.where(masks[0], y[:rows], 0.0)
    for h in range(1, len(masks)):
        out = out + jnp.where(masks[h], y[rows * h:rows * (h + 1)], 0.0)
    return out


DIL_PAIR = 2


def _dil_rows(idx, d):
    u = idx // d
    r = idx - u * d
    own = pl.ds(u * (SPAN * d) + r, SPAN, stride=d) if d > 1 else pl.ds(pl.multiple_of(u * SPAN, SPAN), SPAN)
    up = jnp.maximum(u - 1, 0)
    prev = pl.ds(up * (SPAN * d) + r, SPAN, stride=d) if d > 1 else pl.ds(pl.multiple_of(up * SPAN, SPAN), SPAN)
    return u, own, prev


def _dil_valid(u):
    qi = lax.broadcasted_iota(jnp.int32, (2 * SPAN, 2 * SPAN), 0) & (SPAN - 1)
    kj = lax.broadcasted_iota(jnp.int32, (2 * SPAN, 2 * SPAN), 1)
    in_prev = (kj < SPAN) & (kj >= qi + jnp.where(u > 0, 0, SPAN))
    return in_prev | ((kj >= SPAN) & (kj - SPAN <= qi))


def _dil_keys(ref, own, prev):
    return jnp.concatenate([ref[prev, :], ref[own, :]], axis=0).astype(BF16)


def _dil_fwd(proj, g, d):
    T = proj.shape[0]
    n_iter = T // SPAN

    def body(q_ref, k_ref, v_ref, o_ref, lse_ref):
        masks = _pair_masks()

        def step(pair, carry):
            its = [_dil_rows(DIL_PAIR * pair + e, d) for e in range(DIL_PAIR)]
            qs = [_stack_heads(q_ref[own, :] * (HEAD_DIM ** -0.5), masks).astype(BF16) for _, own, _ in its]
            kks = [_dil_keys(k_ref, own, prev) for _, own, prev in its]
            vvs = [_dil_keys(v_ref, own, prev) for _, own, prev in its]
            ss = [jnp.where(_dil_valid(u), lax.dot_general(q, kk, NT, preferred_element_type=F32), NEG)
                  for (u, _, _), q, kk in zip(its, qs, kks)]
            ms = [jnp.max(s, axis=1, keepdims=True) for s in ss]
            ps = [jnp.exp(s - m) for s, m in zip(ss, ms)]
            dens = [jnp.sum(p, axis=1, keepdims=True) for p in ps]
            pvs = [lax.dot_general(p.astype(BF16), vv, NN, preferred_element_type=F32) / den
                   for p, vv, den in zip(ps, vvs, dens)]
            for (_, own, _), pv, m, den in zip(its, pvs, ms, dens):
                o_ref[own, :] = _unstack_heads(pv, masks)
                lse_ref[own, :] = _unstack_heads(jnp.broadcast_to(m + jnp.log(den), pv.shape), masks)
            return carry

        lax.fori_loop(0, n_iter // DIL_PAIR, step, 0)

    def col(b):
        return pl.BlockSpec((T, 128), lambda p: (0, b + p))

    sh = jax.ShapeDtypeStruct((T, D_ATT), F32)
    out = pl.BlockSpec((T, 128), lambda p: (0, p))
    return pl.pallas_call(
        body, name=f"dil_fwd_d{d}", grid=(2,),
        in_specs=[col(2 * g), col(6 + 2 * g), col(12 + 2 * g)], out_specs=[out, out], out_shape=[sh, sh],
        compiler_params=_params(dimension_semantics=("arbitrary",)),
    )(proj, proj, proj)


def _dil_bwd(proj, do, o_dil, lse, g, d):
    T = proj.shape[0]
    n_iter = T // SPAN

    def body(q_ref, k_ref, v_ref, do_ref, o_ref, lse_ref, dq_ref, dk_ref, dv_ref):
        masks = _pair_masks()
        head_lanes = jnp.concatenate(masks, axis=0)

        def step(pair, carry):
            its = [_dil_rows(DIL_PAIR * pair + e, d) for e in range(DIL_PAIR)]
            qs = [_stack_heads(q_ref[own, :] * (HEAD_DIM ** -0.5), masks).astype(BF16) for _, own, _ in its]
            kks = [_dil_keys(k_ref, own, prev) for _, own, prev in its]
            vvs = [_dil_keys(v_ref, own, prev) for _, own, prev in its]
            doms = [_stack_heads(do_ref[own, :], masks) for _, own, _ in its]
            dos = [dom.astype(BF16) for dom in doms]
            deltas = [jnp.sum(dom * jnp.concatenate([o_ref[own, :]] * 2, axis=0), axis=1, keepdims=True)
                      for dom, (_, own, _) in zip(doms, its)]
            lrows = [jnp.max(jnp.where(head_lanes, jnp.concatenate([lse_ref[own, :]] * 2, axis=0), NEG),
                             axis=1, keepdims=True) for _, own, _ in its]
            ss = [lax.dot_general(q, kk, NT, preferred_element_type=F32) for q, kk in zip(qs, kks)]
            dps = [lax.dot_general(do_b, vv, NT, preferred_element_type=F32) for do_b, vv in zip(dos, vvs)]
            ps = [jnp.where(_dil_valid(u), jnp.exp(s - lrow), 0.0) for (u, _, _), s, lrow in zip(its, ss, lrows)]
            dss = [(p * (dp - delta)).astype(BF16) for p, dp, delta in zip(ps, dps, deltas)]
            dqs = [lax.dot_general(ds, kk, NN, preferred_element_type=F32) for ds, kk in zip(dss, kks)]
            dkks = [lax.dot_general(ds, q, TN, preferred_element_type=F32) for ds, q in zip(dss, qs)]
            dvvs = [lax.dot_general(p.astype(BF16), do_b, TN, preferred_element_type=F32) for p, do_b in zip(ps, dos)]
            for (_, own, prev), dq, dkk, dvv in zip(its, dqs, dkks, dvvs):
                dq_ref[own, :] = _unstack_heads(dq, masks) * (HEAD_DIM ** -0.5)
                dk_ref[own, :] = dkk[SPAN:]
                dv_ref[own, :] = dvv[SPAN:]
                dk_ref[prev, :] = dk_ref[prev, :] + dkk[:SPAN]
                dv_ref[prev, :] = dv_ref[prev, :] + dvv[:SPAN]
            return carry

        lax.fori_loop(0, n_iter // DIL_PAIR, step, 0)

    def col(b):
        return pl.BlockSpec((T, 128), lambda p: (0, b + p))

    sh = jax.ShapeDtypeStruct((T, D_ATT), F32)
    return pl.pallas_call(
        body, name=f"dil_bwd_d{d}", grid=(2,),
        in_specs=[col(2 * g), col(6 + 2 * g), col(12 + 2 * g), col(0), col(0), col(0)],
        out_specs=[col(0), col(0), col(0)], out_shape=[sh, sh, sh],
        compiler_params=_params(dimension_semantics=("arbitrary",)),
    )(proj, proj, proj, do, o_dil, lse)


SB_KT = 512
LOG2_E = 1.4426950408889634


def _sb_tri(strict):
    a = lax.broadcasted_iota(jnp.int32, (Q_BLOCK, Q_BLOCK), 0)
    b = lax.broadcasted_iota(jnp.int32, (Q_BLOCK, Q_BLOCK), 1)
    return jnp.where((a > b) if strict else (a >= b), 1.0, 0.0).astype(BF16)


def _split_stack(x):
    nb = x.shape[1] // Q_BLOCK
    blocks = [x[:, Q_BLOCK * b:Q_BLOCK * (b + 1)] for b in range(nb)]
    hi = [b.astype(BF16) for b in blocks]
    lo = [(b - h.astype(F32)).astype(BF16) for b, h in zip(blocks, hi)]
    return blocks, jnp.concatenate(hi + lo, axis=0)


def _suffix_from(y, blocks, c):
    r = blocks[0].shape[0]
    nb = len(blocks)
    outs = [None] * nb
    run = c
    for b in reversed(range(nb)):
        outs[b] = run + y[r * b:r * (b + 1)] + y[r * (nb + b):r * (nb + b + 1)]
        run = run + jnp.sum(blocks[b], axis=1, keepdims=True)
    return jnp.concatenate(outs, axis=1), run


SB_HEADS = D_ATT // HEAD_DIM
SB_FWD_CHAINS = 2
SB_BWD_CHAINS = 1


SB_PLACES = SB_KT // Q_BLOCK


def _sb_past(place, rows):
    row = lax.broadcasted_iota(jnp.int32, (rows, Q_BLOCK * (place + 1)), 0) & (Q_BLOCK - 1)
    col = lax.broadcasted_iota(jnp.int32, (rows, Q_BLOCK * (place + 1)), 1)
    return col < row + place * Q_BLOCK


def _sb_head_masks(chains):
    lane = lax.broadcasted_iota(jnp.int32, (Q_BLOCK, D_ATT), 1)
    masks = [(lane >= HEAD_DIM * h) & (lane < HEAD_DIM * (h + 1)) for h in range(SB_HEADS)]
    per = SB_HEADS // chains
    return [masks[per * g:per * (g + 1)] for g in range(chains)]


def _sb_rows(t, width=SB_KT):
    return pl.ds(pl.multiple_of(t * SB_KT, SB_KT), width)


def _sb_widen(x):
    if x.shape[1] == SB_KT:
        return x
    return jnp.concatenate([x, jnp.zeros((x.shape[0], SB_KT - x.shape[1]), x.dtype)], axis=1)


def _sb_log_terms(z, past):
    lsz = jnp.minimum(z, 0.0) - jnp.log(1.0 + jnp.exp2(jnp.abs(z) * -LOG2_E))
    lk = lsz - z
    return lsz, (lk if past is None else jnp.where(past, lk, 0.0))


def _sb_weights(lsz, after, past):
    w = jnp.exp(lsz + after)
    return w if past is None else jnp.where(past, w, 0.0)


def _sb_fwd(proj):
    T = proj.shape[0]
    rows = SB_HEADS // SB_FWD_CHAINS * Q_BLOCK

    def at_place(place, n_tiles, q_ref, k_ref, v_ref, o_ref, z_buf, w_buf):
        masks = _sb_head_masks(SB_FWD_CHAINS)
        tri = _sb_tri(True)
        q = q_ref[...] * (HEAD_DIM ** -0.5)
        qs = [_stack_heads(q, m).astype(BF16) for m in masks]

        def scores(t, width=SB_KT):
            kb = k_ref[_sb_rows(t, width), :].astype(BF16)
            return [lax.dot_general(g, kb, NT, preferred_element_type=F32) for g in qs]

        def weights(zs, cs, past, between=lambda: None):
            logs = [_sb_log_terms(z, past) for z in zs]
            splits = [_split_stack(lk) for _, lk in logs]
            ys = [lax.dot_general(x, tri, NN, preferred_element_type=F32) for _, x in splits]
            between()
            sums = [_suffix_from(y, blocks, c) for y, (blocks, _), c in zip(ys, splits, cs)]
            ws = [_sb_weights(lsz, after, past).astype(BF16) for (lsz, _), (after, _) in zip(logs, sums)]
            return ws, [c for _, c in sums]

        def values(acc, slot, t):
            vb = v_ref[_sb_rows(t), :].astype(BF16)
            for g, m in enumerate(masks):
                acc = acc + _unstack_heads(lax.dot_general(w_buf[slot, g], vb, NN, preferred_element_type=F32), m)
            return acc

        def keep(buf, slot, xs):
            for g, x in enumerate(xs):
                buf[slot, g] = x

        zs = scores(n_tiles - 1, Q_BLOCK * (place + 1))
        keep(z_buf, 0, scores(jnp.maximum(n_tiles - 2, 0)))
        ws, cs = weights(zs, [jnp.zeros((rows, 1), F32)] * SB_FWD_CHAINS, _sb_past(place, rows))
        keep(w_buf, 0, [_sb_widen(w) for w in ws])

        def step(tt, carry):
            t = n_tiles - 2 - tt
            cur = tt & 1
            acc = values(carry[0], cur, t + 1)
            ws, cs = weights([z_buf[cur, g] for g in range(SB_FWD_CHAINS)], carry[1:], None,
                             lambda: keep(z_buf, 1 - cur, scores(jnp.maximum(t - 1, 0))))
            keep(w_buf, 1 - cur, ws)
            return (acc, *cs)

        carry = lax.fori_loop(0, n_tiles - 1, step, (jnp.zeros((Q_BLOCK, D_ATT), F32), *cs))
        o_ref[...] = values(carry[0], (n_tiles - 1) & 1, 0)

    def body(*refs):
        n_tiles = pl.program_id(1) + 1
        for place in range(SB_PLACES):
            pl.when(pl.program_id(0) == place)(functools.partial(at_place, place, n_tiles, *refs))

    cb = COL_QS // D_ATT
    return pl.pallas_call(
        body, name="sb_fwd", grid=(SB_PLACES, T // SB_KT),
        in_specs=[pl.BlockSpec((Q_BLOCK, D_ATT), lambda p, j: (SB_PLACES * j + p, cb)),
                  pl.BlockSpec((T, D_ATT), lambda p, j: (0, cb + 1)),
                  pl.BlockSpec((T, D_ATT), lambda p, j: (0, cb + 2))],
        out_specs=pl.BlockSpec((Q_BLOCK, D_ATT), lambda p, j: (SB_PLACES * j + p, 0)),
        out_shape=jax.ShapeDtypeStruct((T, D_ATT), F32),
        scratch_shapes=[pltpu.VMEM((2, SB_FWD_CHAINS, rows, SB_KT), F32),
                        pltpu.VMEM((2, SB_FWD_CHAINS, rows, SB_KT), BF16)],
        compiler_params=_params(dimension_semantics=("arbitrary", "arbitrary")),
    )(proj, proj, proj)


def _sb_bwd(proj, do, o):
    T = proj.shape[0]
    n_rows = SB_HEADS // SB_BWD_CHAINS * Q_BLOCK

    def at_place(place, n_tiles, q_ref, k_ref, v_ref, do_ref, o_ref, dq_ref, dk_ref, dv_ref,
                 z_buf, gv_buf, dz_buf, w_buf):
        masks = _sb_head_masks(SB_BWD_CHAINS)
        tri = _sb_tri(True)
        tri_incl = _sb_tri(False)

        q = q_ref[...] * (HEAD_DIM ** -0.5)
        qs = [_stack_heads(q, m).astype(BF16) for m in masks]
        dos = [_stack_heads(do_ref[...], m).astype(BF16) for m in masks]
        qts = [_stack_heads(q, m).T.astype(BF16) for m in masks]
        dots = [_stack_heads(do_ref[...], m).T.astype(BF16) for m in masks]
        o_rep = jnp.concatenate([o_ref[...]] * (SB_HEADS // SB_BWD_CHAINS), axis=0)
        deltas = [jnp.sum(d.astype(F32) * o_rep, axis=1, keepdims=True) for d in dos]

        def scores(t, width=SB_KT):
            kb = k_ref[_sb_rows(t, width), :].astype(BF16)
            return [lax.dot_general(g, kb, NT, preferred_element_type=F32) for g in qs]

        def value_grads(t, width=SB_KT):
            vb = v_ref[_sb_rows(t, width), :].astype(BF16)
            return [lax.dot_general(d, vb, NT, preferred_element_type=F32) for d in dos]

        def keep(buf, slot, xs):
            for g, x in enumerate(xs):
                buf[slot, g] = x

        def kept(buf, slot):
            return [buf[slot, g] for g in range(SB_BWD_CHAINS)]

        def score_grads(zs, gvs, cs, ces, past, after_first=lambda: None, after_second=lambda: None):
            logs = [_sb_log_terms(z, past) for z in zs]
            splits = [_split_stack(lk) for _, lk in logs]
            ys = [lax.dot_general(x, tri, NN, preferred_element_type=F32) for _, x in splits]
            after_first()
            sums = [_suffix_from(y, blocks, c) for y, (blocks, _), c in zip(ys, splits, cs)]
            wbs = [_sb_weights(lsz, after, past).astype(BF16) for (lsz, _), (after, _) in zip(logs, sums)]
            es = [wb.astype(F32) * gv for wb, gv in zip(wbs, gvs())]
            esplits = [_split_stack(e) for e in es]
            eys = [lax.dot_general(x, tri_incl, NN, preferred_element_type=F32) for _, x in esplits]
            after_second()
            esums = [_suffix_from(y, blocks, ce) for y, (blocks, _), ce in zip(eys, esplits, ces)]
            dzbs = []
            for e, (lsz, _), (suf, _), delta in zip(es, logs, esums, deltas):
                dz = e - jnp.exp(lsz) * (e + (delta - suf))
                dzbs.append((dz if past is None else jnp.where(past, dz, 0.0)).astype(BF16))
            return dzbs, wbs, [c for _, c in sums], [c for _, c in esums]

        def outputs(dq, slot, t):
            rows = _sb_rows(t)
            kb = k_ref[rows, :].astype(BF16)
            dk_t = dv_t = None
            for m, dzb, wb, g, d in zip(masks, kept(dz_buf, slot), kept(w_buf, slot), qts, dots):
                dq = dq + _unstack_heads(lax.dot_general(dzb, kb, NN, preferred_element_type=F32), m)
                a = lax.dot_general(g, dzb, NN, preferred_element_type=F32)
                b = lax.dot_general(d, wb, NN, preferred_element_type=F32)
                dk_t = a if dk_t is None else dk_t + a
                dv_t = b if dv_t is None else dv_t + b
            dk_ref[:, rows] = dk_ref[:, rows] + dk_t
            dv_ref[:, rows] = dv_ref[:, rows] + dv_t
            return dq

        zcol = [jnp.zeros((n_rows, 1), F32)] * SB_BWD_CHAINS
        ahead = jnp.maximum(n_tiles - 2, 0)
        width = Q_BLOCK * (place + 1)
        zs, gvs = scores(n_tiles - 1, width), value_grads(n_tiles - 1, width)
        keep(z_buf, 0, scores(ahead))
        keep(gv_buf, 0, value_grads(ahead))
        dzbs, wbs, cs, ces = score_grads(zs, lambda: gvs, zcol, zcol, _sb_past(place, n_rows))
        keep(dz_buf, 0, [_sb_widen(x) for x in dzbs])
        keep(w_buf, 0, [_sb_widen(x) for x in wbs])

        def step(tt, carry):
            t = n_tiles - 2 - tt
            cur = tt & 1
            ahead = jnp.maximum(t - 1, 0)
            dq = outputs(carry[0], cur, t + 1)
            dzbs, wbs, cs, ces = score_grads(
                kept(z_buf, cur), lambda: kept(gv_buf, cur),
                carry[1:1 + SB_BWD_CHAINS], carry[1 + SB_BWD_CHAINS:], None,
                lambda: keep(z_buf, 1 - cur, scores(ahead)),
                lambda: keep(gv_buf, 1 - cur, value_grads(ahead)))
            keep(dz_buf, 1 - cur, dzbs)
            keep(w_buf, 1 - cur, wbs)
            return (dq, *cs, *ces)

        carry = lax.fori_loop(0, n_tiles - 1, step, (jnp.zeros((Q_BLOCK, D_ATT), F32), *cs, *ces))
        dq_ref[...] = outputs(carry[0], (n_tiles - 1) & 1, 0) * (HEAD_DIM ** -0.5)

    def body(*refs):
        n_tiles = pl.program_id(1) + 1
        dk_ref, dv_ref = refs[6:8]

        @pl.when((pl.program_id(0) == 0) & (n_tiles == 1))
        def _():
            dk_ref[...] = jnp.zeros_like(dk_ref)
            dv_ref[...] = jnp.zeros_like(dv_ref)

        for place in range(SB_PLACES):
            pl.when(pl.program_id(0) == place)(functools.partial(at_place, place, n_tiles, *refs))

    cb = COL_QS // D_ATT
    blk = pl.BlockSpec((Q_BLOCK, D_ATT), lambda p, j: (SB_PLACES * j + p, 0))
    turned = pl.BlockSpec((D_ATT, T), lambda p, j: (0, 0))
    sh = jax.ShapeDtypeStruct((T, D_ATT), F32)
    sh_turned = jax.ShapeDtypeStruct((D_ATT, T), F32)
    kept_f32 = pltpu.VMEM((2, SB_BWD_CHAINS, n_rows, SB_KT), F32)
    kept_bf16 = pltpu.VMEM((2, SB_BWD_CHAINS, n_rows, SB_KT), BF16)
    return pl.pallas_call(
        body, name="sb_bwd", grid=(SB_PLACES, T // SB_KT),
        in_specs=[pl.BlockSpec((Q_BLOCK, D_ATT), lambda p, j: (SB_PLACES * j + p, cb)),
                  pl.BlockSpec((T, D_ATT), lambda p, j: (0, cb + 1)),
                  pl.BlockSpec((T, D_ATT), lambda p, j: (0, cb + 2)), blk, blk],
        out_specs=[blk, turned, turned], out_shape=[sh, sh_turned, sh_turned],
        scratch_shapes=[kept_f32, kept_f32, kept_bf16, kept_bf16],
        compiler_params=_params(dimension_semantics=("arbitrary", "arbitrary")),
    )(proj, proj, proj, do, o)


def _tok(c, by=None):
    if by is None:
        return pl.BlockSpec((TM, c), lambda i, j, k: (i, 0))
    if by == 1:
        return pl.BlockSpec((TM, c), lambda i, j, k: (i, j))
    return pl.BlockSpec((TM, c), lambda i, j, k: (i, k))


def _gain_spec():
    return pl.BlockSpec((1, D_MODEL), lambda i, j, k: (0, 0))


def _wfull(r, c, l):
    return pl.BlockSpec((N_CHIPS, None, r, c), lambda i, j, k: (0, l, 0, 0), pipeline_mode=pl.Buffered(1))


def _pick(idx, c):
    return lambda ins: ins[idx][c]


def _cols(idx, c, w):
    return lambda ins: ins[idx][:, c * w:(c + 1) * w]


def _rows(rows, width):
    return pl.BlockSpec((rows, width), lambda i, j, k: (i, 0))


def _whole(shape):
    return pl.BlockSpec(shape, lambda i, j, k: (0, 0), pipeline_mode=pl.Buffered(1))


def _ffn_fwd(x, gain, wg, wu, wd):
    T = x.shape[0]
    wg, wu = (w.reshape(-1, D_MODEL) for w in (wg, wu))
    ff = wg.shape[0]
    tm = TM // 2
    normed = _normed(0, 3)

    def swiglu(vals, ins, outs, i):
        gt, up = vals
        s = _sigmoid(gt)
        sil = gt * s
        outs[0][...] = sil.astype(BF16)
        outs[1][...] = (up * (s * (1.0 + gt * (1.0 - s)))).astype(BF16)
        outs[2][...] = (sil * up).astype(BF16)
        outs[3][...] = normed(ins)

    ash = jax.ShapeDtypeStruct((T, ff), BF16)
    sil, up_dsil, act, h = _mm(
        "ffn_up", [x, wg, wu, gain], [_rows(tm, D_MODEL), _whole(wg.shape), _whole(wu.shape), _gain_spec()],
        [(normed, 1, 0), (normed, 2, 1)], 2, None, NT, (T // tm, 1, 1), swiglu,
        [ash] * 3 + [jax.ShapeDtypeStruct((T, D_MODEL), BF16)], [_rows(tm, ff)] * 3 + [_rows(tm, D_MODEL)])

    def resid(vals, ins, outs, i):
        outs[0][...] = ins[2][...] + 0.5 * vals[0]

    wd_chips = wd(act) if callable(wd) else wd
    wd = wd_chips.reshape(-1, D_MODEL)
    (y,) = _mm(
        "ffn_down", [act, wd, x], [_rows(TM, ff), _whole(wd.shape), _tok(D_MODEL)], [(0, 1, 0)], 1, None, NN,
        (T // TM, 1, 1), resid, [jax.ShapeDtypeStruct((T, D_MODEL), F32)], [_tok(D_MODEL)])
    return y, (x, h, sil, up_dsil, act), wd_chips


def _ffn_bwd(dxo, gain, wg, wu, wd, saved):
    x, h, sil, up_dsil, act = saved
    T = x.shape[0]
    n_chips, _, ffs, _ = wd.shape
    wg, wu, wd = (w.reshape(-1, D_MODEL) for w in (wg, wu, wd))
    ff = wd.shape[0]
    tk = TM
    tm = TM

    def dswiglu(vals, ins, outs, i):
        da = 0.5 * vals[0]
        outs[0][...] = (da * ins[3][...].astype(F32)).astype(BF16)
        outs[1][...] = (da * ins[2][...].astype(F32)).astype(BF16)

    ash = jax.ShapeDtypeStruct((T, ff), BF16)
    dgate, dup = _mm(
        "ffn_dact", [dxo, wd, sil, up_dsil], [_rows(tm, D_MODEL), _whole(wd.shape), _rows(tm, ff), _rows(tm, ff)],
        [(0, 1, 0)], 1, None, NT, (T // tm, 1, 1), dswiglu, [ash, ash], [_rows(tm, ff)] * 2)

    def half(vals, ins, outs, i):
        outs[0][...] = (0.5 * vals[0]).astype(BF16)

    def cast(vals, ins, outs, i):
        outs[0][...] = vals[0].astype(BF16)

    tok_k = pl.BlockSpec((tk, D_MODEL), lambda i, j, k: (k, 0))
    hid_k = pl.BlockSpec((tk, ff), lambda i, j, k: (k, 0))
    wsh = jax.ShapeDtypeStruct((ff, D_MODEL), BF16)
    (dwd,) = _mm("ffn_dwd", [act, dxo], [hid_k, tok_k], [(0, 1, 0)], 1, (ff, D_MODEL), TN, (1, 1, T // tk), half,
                 [wsh], [_whole((ff, D_MODEL))])

    tx = TM // 2
    dx, dgain = _mm(
        "ffn_dx", [dgate, dup, wg, wu, x, gain, dxo],
        [_rows(tx, ff), _rows(tx, ff), _whole(wg.shape), _whole(wu.shape), _rows(tx, D_MODEL), _gain_spec(),
         _rows(tx, D_MODEL)],
        [(0, 2, 0), (1, 3, 0)], 1, None, NN, (T // tx, 1, 1), _rms_bwd_epilogue(4, 5, 6),
        [jax.ShapeDtypeStruct((T, D_MODEL), F32), jax.ShapeDtypeStruct((8, D_MODEL), F32)],
        [_rows(tx, D_MODEL), pl.BlockSpec((8, D_MODEL), lambda i, j, k: (0, 0))])

    dws = []
    for dact in (dgate, dup):
        dws += _mm("ffn_dwgu", [dact, h], [hid_k, tok_k], [(0, 1, 0)], 1, (ff, D_MODEL), TN, (1, 1, T // tk), cast,
                   [wsh], [_whole((ff, D_MODEL))])
    dwg, dwu, dwd = (w.reshape(n_chips, ffs, D_MODEL) for w in (dws[0], dws[1], dwd))
    return dx, dgain, dwg, dwu, dwd


def _joined_mixer_weights(wpd, wps, wo):
    n, _, r, c = wpd.shape
    wpd_n, wps_n = (w[:, 0].transpose(1, 0, 2).reshape(r, n * c) for w in (wpd, wps))
    return wpd_n, wps_n, wo.reshape(-1, wo.shape[3])


def _mixer_fwd(x, gain, W, l, tabs):
    T = x.shape[0]
    win, wpd, wps, wo = W["w_in"], W["w_proj_dil"], W["w_proj_sb"], W["w_out"]
    cin = win.shape[3]
    cp = wpd.shape[3]
    normed = _normed(0, 5)
    n_rope = 6 * D_ATT

    tm = TM // 2

    def roped(vals, ins, outs, i):
        for j, v in enumerate(vals):
            lo = j * cin
            k = min(max(n_rope - lo, 0), cin)
            if k:
                tab = [jnp.concatenate([ins[t][...]] * (k // 128), axis=1) for t in (2, 3, 4)]
                outs[0][:, lo:lo + k] = _rope_fwd(v[:, :k], *tab)
            if k < cin:
                outs[0][:, lo + k:lo + cin] = v[:, k:]
        outs[1][...] = normed(ins)

    proj, h = _mm(
        "mix_in", [x, win, *tabs, gain],
        [_rows(tm, D_MODEL), _wfull(D_MODEL, cin, l)] + [_rows(tm, 128)] * 3 + [_gain_spec()],
        [(normed, _pick(1, c), c) for c in range(N_CHIPS)], N_CHIPS, None, NN, (T // tm, 1, 1), roped,
        [jax.ShapeDtypeStruct((T, N_CHIPS * cin), F32), jax.ShapeDtypeStruct((T, D_MODEL), BF16)],
        [_rows(tm, N_CHIPS * cin), _rows(tm, D_MODEL)])

    os_, lses = [], []
    for g, (window, dil) in enumerate(DIL_GROUPS):
        o_g, lse_g = _dil_fwd(proj, g, dil)
        os_.append(o_g)
        lses.append(lse_g)
    o_dil, lse = _dil_merge(os_, lses)
    o_sb = _sb_fwd(proj)

    def gated(vals, ins, outs, i):
        pd, ps = vals
        outs[0][...] = (_sigmoid(ins[4][...]) * pd + _sigmoid(ins[5][...]) * ps).astype(BF16)
        outs[1][...] = pd.astype(BF16)
        outs[2][...] = ps.astype(BF16)

    wpd_n, wps_n, wo_n = _joined_mixer_weights(wpd, wps, wo)
    gd_spec = pl.BlockSpec((TM, D_MODEL), lambda i, j, k: (i, COL_GD // D_MODEL))
    gs_spec = pl.BlockSpec((TM, D_MODEL), lambda i, j, k: (i, COL_GS // D_MODEL))
    ush = jax.ShapeDtypeStruct((T, D_MODEL), BF16)
    u, pd, ps = _mm(
        "mix_gate", [o_dil, o_sb, wpd_n, wps_n, proj, proj],
        [_tok(D_ATT), _tok(D_ATT), _whole(wpd_n.shape), _whole(wps_n.shape), gd_spec, gs_spec],
        [(0, 2, 0), (1, 3, 1)], 2, None, NN, (T // TM, 1, 1), gated, [ush] * 3, [_tok(D_MODEL)] * 3)

    def resid(vals, ins, outs, i):
        outs[0][...] = ins[2][...] + vals[0]

    (y,) = _mm(
        "mix_out", [u, wo_n, x], [_tok(D_MODEL), _whole(wo_n.shape), _tok(D_MODEL)], [(0, 1, 0)], 1, None, NN,
        (T // TM, 1, 1), resid, [jax.ShapeDtypeStruct((T, D_MODEL), F32)], [_tok(D_MODEL)])
    return y, (x, h, proj, o_dil, lse, o_sb, u, pd, ps)


def _mixer_bwd(dxo, gain, W, l, tabs, saved):
    x, h, proj, o_dil, lse, o_sb, u, pd, ps = saved
    T = x.shape[0]
    win, wpd, wps, wo = W["w_in"], W["w_proj_dil"], W["w_proj_sb"], W["w_out"]
    cin = win.shape[3]
    cp = wpd.shape[3]
    tk = TM
    tm = TM
    row = pl.BlockSpec((tm, D_MODEL), lambda i, j, k: (i, 0))

    def dgated(vals, ins, outs, i):
        du = vals[0]
        sd = _sigmoid(ins[4][...])
        ss = _sigmoid(ins[5][...])
        outs[0][...] = (du * sd).astype(BF16)
        outs[1][...] = (du * ss).astype(BF16)
        outs[2][...] = (du * ins[2][...].astype(F32) * sd * (1.0 - sd)).astype(BF16)
        outs[3][...] = (du * ins[3][...].astype(F32) * ss * (1.0 - ss)).astype(BF16)

    wpd_n, wps_n, wo_n = _joined_mixer_weights(wpd, wps, wo)
    gd_spec = pl.BlockSpec((TM, D_MODEL), lambda i, j, k: (i, COL_GD // D_MODEL))
    gs_spec = pl.BlockSpec((TM, D_MODEL), lambda i, j, k: (i, COL_GS // D_MODEL))
    ush = jax.ShapeDtypeStruct((T, D_MODEL), BF16)
    dpd, dps, dgd, dgs = _mm(
        "mix_du", [dxo, wo_n, pd, ps, proj, proj],
        [_tok(D_MODEL), _whole(wo_n.shape), _tok(D_MODEL), _tok(D_MODEL), gd_spec, gs_spec],
        [(0, 1, 0)], 1, None, NT, (T // TM, 1, 1), dgated, [ush] * 4, [_tok(D_MODEL)] * 4)

    def one(vals, ins, outs, i):
        outs[0][...] = vals[0].astype(BF16)

    def two(vals, ins, outs, i):
        outs[0][...] = vals[0].astype(BF16)
        outs[1][...] = vals[1].astype(BF16)

    tok_k = pl.BlockSpec((tk, D_MODEL), lambda i, j, k: (k, 0))
    att_k = pl.BlockSpec((tk, D_ATT), lambda i, j, k: (k, 0))
    (dwo_n,) = _mm("mix_dwo", [u, dxo], [tok_k, tok_k], [(0, 1, 0)], 1, (D_MODEL, D_MODEL), TN, (1, 1, T // tk), one,
                   [jax.ShapeDtypeStruct((D_MODEL, D_MODEL), BF16)], [_whole((D_MODEL, D_MODEL))])

    def plain2(vals, ins, outs, i):
        outs[0][...] = vals[0]
        outs[1][...] = vals[1]

    ash = jax.ShapeDtypeStruct((T, D_ATT), F32)
    do_dil, do_sb = _mm(
        "mix_do", [dpd, dps, wpd_n, wps_n], [_tok(D_MODEL), _tok(D_MODEL), _whole(wpd_n.shape), _whole(wps_n.shape)],
        [(0, 2, 0), (1, 3, 1)], 2, None, NT, (T // TM, 1, 1), plain2, [ash, ash], [_tok(D_ATT)] * 2)

    psh = jax.ShapeDtypeStruct((D_ATT, D_MODEL), BF16)
    dwpd_n, dwps_n = _mm(
        "mix_dwp", [o_dil, o_sb, dpd, dps], [att_k, att_k, tok_k, tok_k], [(0, 2, 0), (1, 3, 1)], 2,
        (D_ATT, D_MODEL), TN, (1, 1, T // tk), two, [psh, psh], [_whole((D_ATT, D_MODEL))] * 2)
    dwpd, dwps = (w.reshape(D_ATT, N_CHIPS, cp).transpose(1, 0, 2) for w in (dwpd_n, dwps_n))
    dwo = dwo_n.reshape(N_CHIPS, cp, D_MODEL)

    dqs, dks, dvs = [], [], []
    for g, (window, dil) in enumerate(DIL_GROUPS):
        dq, dk, dv = _dil_bwd(proj, do_dil, o_dil, lse, g, dil)
        dqs.append(dq)
        dks.append(dk)
        dvs.append(dv)
    dq_s, dk_s, dv_s = _sb_bwd(proj, do_sb, o_sb)
    dproj = _assemble_dproj(dqs + dks, dvs + [dq_s], [dk_s, dv_s], [dgd, dgs], tabs)

    dx, dgain = _mm(
        "mix_dx", [dproj, win, x, gain, dxo],
        [pl.BlockSpec((tm, N_CHIPS * cin), lambda i, j, k: (i, 0)), _wfull(D_MODEL, cin, l), row, _gain_spec(), row],
        [(_cols(0, c, cin), _pick(1, c), 0) for c in range(N_CHIPS)], 1, None, NT, (T // tm, 1, 1),
        _rms_bwd_epilogue(2, 3, 4),
        [jax.ShapeDtypeStruct((T, D_MODEL), F32), jax.ShapeDtypeStruct((8, D_MODEL), F32)],
        [row, pl.BlockSpec((8, D_MODEL), lambda i, j, k: (0, 0))])

    (dwin,) = _mm(
        "mix_dwin", [h, dproj],
        [pl.BlockSpec((tk, D_MODEL), lambda i, j, k: (k, 0)), pl.BlockSpec((tk, cin), lambda i, j, k: (k, j))],
        [(0, 1, 0)], 1, (D_MODEL, cin), TN, (1, N_CHIPS, T // tk), one,
        [jax.ShapeDtypeStruct((N_CHIPS, D_MODEL, cin), BF16)],
        [pl.BlockSpec((None, D_MODEL, cin), lambda i, j, k: (j, 0, 0))])
    return dx, dgain, dwin, dwpd, dwps, dwo


def _local_step(x, target, norms, norm_final, weights_of, on_grads):
    T = x.shape[0]
    tabs = _rope_tables(T)
    saved, held = [], []
    for l in range(DEPTH):
        w1 = weights_of(l, 0, x)
        x, s1, w1["ffn1_w_down"] = _ffn_fwd(x, norms["norm_ffn1"][l:l + 1], w1["ffn1_w_gate"], w1["ffn1_w_up"],
                                             w1["ffn1_w_down"])
        w2 = weights_of(l, 1, x)
        x, s2 = _mixer_fwd(x, norms["norm_mix"][l:l + 1], w2, 0, tabs)
        w3 = weights_of(l, 2, x)
        x, s3, _ = _ffn_fwd(x, norms["norm_ffn2"][l:l + 1], w3["ffn2_w_gate"], w3["ffn2_w_up"], w3["ffn2_w_down"])
        saved.append((s1, s2, s3))
        held.append((w1, w2, w3))
    dx, dg_final, loss = _final_loss(x, norm_final.reshape(1, D_MODEL), target)
    gains = [None] * DEPTH
    for l in reversed(range(DEPTH)):
        s1, s2, s3 = saved[l]
        w1, w2, w3 = held[l]
        dx, dg2, dwg2, dwu2, dwd2 = _ffn_bwd(dx, norms["norm_ffn2"][l:l + 1], w3["ffn2_w_gate"], w3["ffn2_w_up"],
                                             w3["ffn2_w_down"], s3)
        dx = on_grads(l, 2, dict(ffn2_w_gate=dwg2, ffn2_w_up=dwu2, ffn2_w_down=dwd2), dx)
        dx, dgm, dwin, dwpd, dwps, dwo = _mixer_bwd(dx, norms["norm_mix"][l:l + 1], w2, 0, tabs, s2)
        dx = on_grads(l, 1, dict(w_in=dwin, w_proj_dil=dwpd, w_proj_sb=dwps, w_out=dwo), dx)
        dx, dg1, dwg1, dwu1, dwd1 = _ffn_bwd(dx, norms["norm_ffn1"][l:l + 1], w1["ffn1_w_gate"], w1["ffn1_w_up"],
                                             w1["ffn1_w_down"], s1)
        dx = on_grads(l, 0, dict(ffn1_w_gate=dwg1, ffn1_w_up=dwu1, ffn1_w_down=dwd1), dx)
        gains[l] = dict(norm_ffn1=dg1, norm_mix=dgm, norm_ffn2=dg2)
    return loss, dx, gains, dg_final


def _place():
    x, y, c = lax.axis_index("x"), lax.axis_index("y"), lax.axis_index("c")
    chips = [(1 - x, y), (x, 1 - y), (1 - x, 1 - y)]
    return x, y, c, chips


def _half(c, r):
    return pl.ds(pl.multiple_of(c * (r // 2), 8), r // 2)


def _cast_into_slot(ws, ls, me_arr, after):
    n = len(ws)
    late = [] if after is None else [after]

    def body(me_ref, *refs):
        for a in range(n):
            refs[len(refs) - n + a][...] = refs[a][...].astype(BF16)

    def src(w, l):
        return pl.BlockSpec((None, w.shape[1] // 4, w.shape[2]), lambda i, me: (l, i, 0))

    def dst(w):
        return pl.BlockSpec((None, None, w.shape[1] // 4, w.shape[2]), lambda i, me: (me[0], 0, i, 0))

    return pl.pallas_call(
        body, name="cast_weights",
        grid_spec=pltpu.PrefetchScalarGridSpec(
            num_scalar_prefetch=1, grid=(4,),
            in_specs=[src(w, l) for w, l in zip(ws, ls)] + [pl.BlockSpec(memory_space=pl.ANY)] * len(late),
            out_specs=[dst(w) for w in ws]),
        out_shape=[jax.ShapeDtypeStruct((N_CHIPS, 1) + w.shape[1:], BF16) for w in ws], compiler_params=_params(),
    )(me_arr, *ws, *late)


HBM_SPEC = pl.BlockSpec(memory_space=pltpu.HBM)
SEM_SPEC = pl.BlockSpec(memory_space=pltpu.SEMAPHORE)
SPLIT_COPY = pltpu.CompilerParams(has_side_effects=pltpu.SideEffectType.DATAFLOW_SIDE_EFFECTING)


def _gather_piece(ref, chip_id, c):
    return ref.at[chip_id, 0, _half(c, ref.shape[2]), :]


def _gather_start(tag, bufs, direct, after=None):
    n = len(bufs)
    late = [] if after is None else [after]
    n_in = n + len(late)

    def body(*refs):
        out_refs = refs[n_in:n_in + n]
        send_sems, recv_sems, token = refs[n_in + n:]
        x, y, c, chips = _place()
        me = 2 * x + y
        for a in range(n):
            piece = _gather_piece(out_refs[a], me, c)
            for j, chip in enumerate(chips):
                for to in ((0, 1) if direct[a] else (c,)):
                    pltpu.make_async_remote_copy(
                        src_ref=piece, dst_ref=piece, send_sem=send_sems.at[6 * a + 2 * j + to],
                        recv_sem=recv_sems.at[6 * a + 2 * j + c], device_id=(*chip, to), device_id_type=MESH).start()
        token[...] = jnp.zeros_like(token)

    outs = pl.pallas_call(
        body, name=f"gather_start_{tag}", in_specs=[HBM_SPEC] * n + [pl.BlockSpec(memory_space=pl.ANY)] * len(late),
        out_specs=[HBM_SPEC] * n + [SEM_SPEC, SEM_SPEC, pl.BlockSpec(memory_space=pltpu.VMEM)],
        out_shape=[pltpu.HBM(b.shape, b.dtype) for b in bufs] + [pltpu.SemaphoreType.DMA((6 * n,))] * 2
        + [jax.ShapeDtypeStruct((8, 128), F32)],
        input_output_aliases={a: a for a in range(n)}, compiler_params=SPLIT_COPY,
    )(*[pltpu.with_memory_space_constraint(b, pltpu.HBM) for b in bufs], *late)
    return outs[:n], outs[n], outs[n + 1], outs[n + 2]


def _gather_wait(k, bufs, places, direct, send_sems, recv_sems, after):
    m = len(bufs)

    def body(*refs):
        in_refs = refs[:m]
        ssem, rsem = refs[m], refs[m + 1]
        x, y, c, chips = _place()
        me = 2 * x + y
        for t, a in enumerate(places):
            for j, chip in enumerate(chips):
                for core in ((0, 1) if direct else (c,)):
                    cp = pltpu.make_async_remote_copy(
                        src_ref=_gather_piece(in_refs[t], me, c),
                        dst_ref=_gather_piece(in_refs[t], 2 * chip[0] + chip[1], core),
                        send_sem=ssem.at[6 * a + 2 * j + core], recv_sem=rsem.at[6 * a + 2 * j + core],
                        device_id=(*chip, core), device_id_type=MESH)
                    cp.wait_send()
                    cp.wait_recv()

    return pl.pallas_call(
        body, name=f"gather_wait_{k}",
        in_specs=[HBM_SPEC] * m + [SEM_SPEC, SEM_SPEC, pl.BlockSpec(memory_space=pl.ANY)], out_specs=[HBM_SPEC] * m,
        out_shape=[pltpu.HBM(b.shape, b.dtype) for b in bufs], input_output_aliases={t: t for t in range(m)},
        compiler_params=SPLIT_COPY,
    )(*bufs, send_sems, recv_sems, after)


def _gather_relay(bufs, after=None):
    n = len(bufs)
    late = [] if after is None else [after]
    n_in = n + len(late)

    def body(*refs):
        out_refs = refs[n_in:n_in + n]
        send_sems, recv_sems = refs[n_in + n:]
        x, y, c, chips = _place()
        cps = []
        for a in range(n):
            for j, chip in enumerate(chips):
                piece = _gather_piece(out_refs[a], 2 * chip[0] + chip[1], c)
                cps.append(pltpu.make_async_remote_copy(
                    src_ref=piece, dst_ref=piece, send_sem=send_sems.at[a, j], recv_sem=recv_sems.at[a, j],
                    device_id=(x, y, 1 - c), device_id_type=MESH))
        for cp in cps:
            cp.start()
        for a in range(n):
            for j, chip in enumerate(chips):
                theirs = _gather_piece(out_refs[a], 2 * chip[0] + chip[1], 1 - c)
                pltpu.make_async_remote_copy(
                    src_ref=theirs, dst_ref=theirs, send_sem=send_sems.at[a, j], recv_sem=recv_sems.at[a, j],
                    device_id=(x, y, 1 - c), device_id_type=MESH).wait_recv()
        for cp in cps:
            cp.wait_send()

    any_spec = pl.BlockSpec(memory_space=pl.ANY)
    return pl.pallas_call(
        body, name="gather_relay", in_specs=[any_spec] * n_in, out_specs=[any_spec] * n,
        out_shape=[jax.ShapeDtypeStruct(b.shape, b.dtype) for b in bufs],
        input_output_aliases={a: a for a in range(n)},
        scratch_shapes=[pltpu.SemaphoreType.DMA((n, 3))] * 2,
    )(*bufs, *late)


def _other_half(ref, c):
    return ref.at[:, _half(1 - c, ref.shape[1]), :]


def _all_of(ref, c):
    return ref


def _sibling_start(name, srcs, pick, land_shapes, thru):
    n = len(srcs)
    lands = [lax.empty(sh, s.dtype) for sh, s in zip(land_shapes, srcs)]
    kept = lands + ([] if thru is None else [thru])
    m = len(kept)

    def body(*refs):
        s_refs, land_refs = refs[:n], refs[n + m:n + m + n]
        send_sems, recv_sems, token = refs[n + 2 * m:]
        x, y, c, _ = _place()
        for a in range(n):
            pltpu.make_async_remote_copy(
                src_ref=pick(s_refs[a], c), dst_ref=land_refs[a], send_sem=send_sems.at[a],
                recv_sem=recv_sems.at[a], device_id=(x, y, 1 - c), device_id_type=MESH).start()
        token[...] = jnp.zeros_like(token)

    outs = pl.pallas_call(
        body, name=name, in_specs=[HBM_SPEC] * (n + m),
        out_specs=[HBM_SPEC] * m + [SEM_SPEC, SEM_SPEC, pl.BlockSpec(memory_space=pltpu.VMEM)],
        out_shape=[pltpu.HBM(v.shape, v.dtype) for v in kept] + [pltpu.SemaphoreType.DMA((n,))] * 2
        + [jax.ShapeDtypeStruct((8, 128), F32)],
        input_output_aliases={n + a: a for a in range(m)}, compiler_params=SPLIT_COPY,
    )(*[pltpu.with_memory_space_constraint(v, pltpu.HBM) for v in list(srcs) + kept])
    return (outs[:n], outs[m], outs[m + 1]), (outs[n] if thru is not None else None), outs[m + 2]


def _sibling_wait(name, srcs, pick, lands, send_sems, recv_sems, after):
    n = len(srcs)

    def body(*refs):
        s_refs, land_refs = refs[:n], refs[n:2 * n]
        ssem, rsem = refs[2 * n], refs[2 * n + 1]
        x, y, c, _ = _place()
        for a in range(n):
            cp = pltpu.make_async_remote_copy(
                src_ref=pick(s_refs[a], c), dst_ref=land_refs[a], send_sem=ssem.at[a], recv_sem=rsem.at[a],
                device_id=(x, y, 1 - c), device_id_type=MESH)
            cp.wait_send()
            cp.wait_recv()

    return pl.pallas_call(
        body, name=name, in_specs=[HBM_SPEC] * (2 * n) + [SEM_SPEC, SEM_SPEC, pl.BlockSpec(memory_space=pl.ANY)],
        out_specs=[HBM_SPEC] * n, out_shape=[pltpu.HBM(v.shape, v.dtype) for v in lands],
        input_output_aliases={n + a: a for a in range(n)}, compiler_params=SPLIT_COPY,
    )(*srcs, *lands, send_sems, recv_sems, after)


def _add_half(gs, gots, c_arr):
    n = len(gs)

    def body(c_ref, *refs):
        for a in range(n):
            refs[2 * n + a][...] = (refs[a][...].astype(F32) + refs[n + a][...].astype(F32)).astype(BF16)

    def own(g):
        return pl.BlockSpec((None, g.shape[1] // 2, g.shape[2]), lambda k, cr: (k, cr[0], 0))

    def half(g):
        return pl.BlockSpec((None, g.shape[1] // 2, g.shape[2]), lambda k, cr: (k, 0, 0))

    return pl.pallas_call(
        body, name="grad_add_half",
        grid_spec=pltpu.PrefetchScalarGridSpec(
            num_scalar_prefetch=1, grid=(N_CHIPS,),
            in_specs=[own(g) for g in gs] + [half(g) for g in gs], out_specs=[half(g) for g in gs]),
        out_shape=[jax.ShapeDtypeStruct(got.shape, BF16) for got in gots], compiler_params=_params(),
    )(c_arr, *gs, *gots)


def _scatter_start(k, ss, thru):
    n = len(ss)

    def body(*refs):
        s_refs, land_refs = refs[2 * n + 1:3 * n + 1], refs[3 * n + 1:4 * n + 1]
        send_sems, recv_sems = refs[4 * n + 2:]
        x, y, c, chips = _place()
        me = 2 * x + y
        for a in range(n):
            for j, chip in enumerate(chips):
                pltpu.make_async_remote_copy(
                    src_ref=s_refs[a].at[2 * chip[0] + chip[1]], dst_ref=land_refs[a].at[me],
                    send_sem=send_sems.at[3 * a + j], recv_sem=recv_sems.at[3 * a + j], device_id=(*chip, c),
                    device_id_type=MESH).start()

    lands = [lax.empty(s.shape, s.dtype) for s in ss]
    hbm = [pltpu.HBM(s.shape, s.dtype) for s in ss]
    outs = pl.pallas_call(
        body, name=f"grad_scatter_start_{k}", in_specs=[HBM_SPEC] * (2 * n + 1),
        out_specs=[HBM_SPEC] * (2 * n + 1) + [SEM_SPEC, SEM_SPEC],
        out_shape=hbm + hbm + [pltpu.HBM(thru.shape, thru.dtype)] + [pltpu.SemaphoreType.DMA((3 * n,))] * 2,
        input_output_aliases={a: a for a in range(2 * n + 1)}, compiler_params=SPLIT_COPY,
    )(*[pltpu.with_memory_space_constraint(v, pltpu.HBM) for v in list(ss) + lands + [thru]])
    return (outs[:n], outs[n:2 * n], outs[2 * n + 1], outs[2 * n + 2]), outs[2 * n]


def _scatter_wait(k, ss, lands, send_sems, recv_sems, after):
    n = len(ss)

    def body(*refs):
        s_refs, land_refs = refs[:n], refs[n:2 * n]
        ssem, rsem = refs[2 * n], refs[2 * n + 1]
        x, y, c, chips = _place()
        me = 2 * x + y
        for a in range(n):
            for j, chip in enumerate(chips):
                cid = 2 * chip[0] + chip[1]
                cp = pltpu.make_async_remote_copy(
                    src_ref=s_refs[a].at[cid], dst_ref=land_refs[a].at[cid], send_sem=ssem.at[3 * a + j],
                    recv_sem=rsem.at[3 * a + j], device_id=(*chip, c), device_id_type=MESH)
                cp.wait_send()
                cp.wait_recv()

    hbm = [pltpu.HBM(s.shape, s.dtype) for s in ss]
    outs = pl.pallas_call(
        body, name=f"grad_scatter_wait_{k}",
        in_specs=[HBM_SPEC] * (2 * n) + [SEM_SPEC, SEM_SPEC, pl.BlockSpec(memory_space=pl.ANY)],
        out_specs=[HBM_SPEC] * (2 * n), out_shape=hbm + hbm,
        input_output_aliases={a: a for a in range(2 * n)}, compiler_params=SPLIT_COPY,
    )(*ss, *lands, send_sems, recv_sems, after)
    return outs[:n], outs[n:]


def _sum_chips(lands, ss, me_arr):
    n = len(lands)

    def body(me_ref, *refs):
        for own in range(N_CHIPS):
            @pl.when(me_ref[0] == own)
            def _(own=own):
                for a in range(n):
                    acc = None
                    for k in range(N_CHIPS):
                        term = (refs[n + a][...] if k == own else refs[a][k]).astype(F32)
                        acc = term if acc is None else acc + term
                    refs[2 * n + a][...] = acc

    return pl.pallas_call(
        body, name="grad_sum_chips",
        grid_spec=pltpu.PrefetchScalarGridSpec(
            num_scalar_prefetch=1, grid=(1,),
            in_specs=[pl.BlockSpec(la.shape, lambda i, me: (0, 0, 0)) for la in lands]
            + [pl.BlockSpec((None,) + la.shape[1:], lambda i, me: (me[0], 0, 0)) for la in lands],
            out_specs=[pl.BlockSpec(la.shape[1:], lambda i, me: (0, 0)) for la in lands]),
        out_shape=[jax.ShapeDtypeStruct(la.shape[1:], F32) for la in lands], compiler_params=_params(),
    )(me_arr, *lands, *ss)


def _allreduce_rows(stats):
    def body(s_ref, o_ref, buf, send_sems, recv_sems):
        x, y, c, _ = _place()
        me = 4 * x + 2 * y + c
        buf[me] = s_ref[...]
        cps = []
        for k in range(1, 8):
            px = jnp.where(k & 4, 1 - x, x)
            py = jnp.where(k & 2, 1 - y, y)
            pc = jnp.where(k & 1, 1 - c, c)
            cps.append(pltpu.make_async_remote_copy(
                src_ref=s_ref, dst_ref=buf.at[me], send_sem=send_sems.at[k - 1], recv_sem=recv_sems.at[k - 1],
                device_id=(px, py, pc), device_id_type=MESH))
        for cp in cps:
            cp.start()
        for cp in cps:
            cp.wait()
        acc = buf[0]
        for d in range(1, 8):
            acc = acc + buf[d]
        o_ref[...] = acc

    vm = pl.BlockSpec(memory_space=pltpu.VMEM)
    return pl.pallas_call(
        body, name="allreduce_rows", in_specs=[vm], out_specs=vm,
        out_shape=jax.ShapeDtypeStruct(stats.shape, F32),
        scratch_shapes=[pltpu.VMEM((8,) + stats.shape, F32), pltpu.SemaphoreType.DMA((7,)),
                        pltpu.SemaphoreType.DMA((7,))],
    )(stats)


def _adamw_math(w, g, m, v):
    m = ADAM_B1 * m + (1.0 - ADAM_B1) * g
    v = ADAM_B2 * v + (1.0 - ADAM_B2) * (g * g)
    m_hat = m / (1.0 - ADAM_B1 ** ADAM_STEP)
    v_hat = v / (1.0 - ADAM_B2 ** ADAM_STEP)
    delta = -ADAM_LR * (m_hat / (jnp.sqrt(v_hat) + ADAM_EPS) + ADAM_WD * w)
    return delta, m, v


def _adamw(ws, ms, vs, mines, theirs, l, c_arr, earlier, after):
    n = len(ws)
    held = [t for e in earlier if e is not None for t in e]
    assert len(held) in (0, 4 * n)
    late = [] if after is None else [after]

    def body(c_ref, *refs):
        outs = refs[len(refs) - 4 * n:]
        for a in range(n):
            w_ref, m_ref, v_ref, a_ref, b_ref = refs[5 * a:5 * a + 5]
            g = jnp.where(pl.program_id(0) == c_ref[0], a_ref[...], b_ref[...])
            delta, mn, vn = _adamw_math(w_ref[...], g, m_ref[...], v_ref[...])
            outs[4 * a][...] = g
            outs[4 * a + 1][...] = delta
            outs[4 * a + 2][...] = mn
            outs[4 * a + 3][...] = vn

    def blk(w):
        tr = w.shape[1] // 4
        return pl.BlockSpec((None, tr, w.shape[2]), lambda hh, i, cr: (l, 2 * hh + i, 0))

    def half(w):
        return pl.BlockSpec((w.shape[1] // 4, w.shape[2]), lambda hh, i, cr: (i, 0))

    outs = pl.pallas_call(
        body, name="adamw",
        grid_spec=pltpu.PrefetchScalarGridSpec(
            num_scalar_prefetch=1, grid=(2, 2),
            in_specs=[sp for w in ws for sp in (blk(w), blk(w), blk(w), half(w), half(w))]
            + [pl.BlockSpec(memory_space=pl.ANY)] * (len(held) + len(late)),
            out_specs=[blk(w) for w in ws for _ in range(4)]),
        out_shape=[jax.ShapeDtypeStruct(w.shape, F32) for w in ws for _ in range(4)],
        input_output_aliases={1 + 5 * n + t: t for t in range(len(held))}, compiler_params=_params(),
    )(c_arr, *[t for grp in zip(ws, ms, vs, mines, theirs) for t in grp], *held, *late)
    return [outs[4 * a:4 * a + 4] for a in range(n)]


def _adamw_rows(w, m, v, g):
    def body(w_ref, m_ref, v_ref, g_ref, d_ref, mo_ref, vo_ref):
        delta, mn, vn = _adamw_math(w_ref[...], g_ref[...], m_ref[...], v_ref[...])
        d_ref[...] = delta
        mo_ref[...] = mn
        vo_ref[...] = vn

    vm = pl.BlockSpec(memory_space=pltpu.VMEM)
    sh = jax.ShapeDtypeStruct(w.shape, F32)
    return pl.pallas_call(body, name="adamw_rows", in_specs=[vm] * 4, out_specs=[vm] * 3, out_shape=[sh] * 3)(w, m, v, g)


SUBLAYERS = (("ffn1_w_gate", "ffn1_w_up", "ffn1_w_down"), ("w_in", "w_proj_dil", "w_proj_sb", "w_out"),
             ("ffn2_w_gate", "ffn2_w_up", "ffn2_w_down"))
TRANSPOSED = ("ffn1_w_gate", "ffn1_w_up", "ffn2_w_gate", "ffn2_w_up")


def _pick_row(blocks):
    row = lax.broadcasted_iota(jnp.int32, (8, D_MODEL), 0)
    out = jnp.zeros((8, D_MODEL), F32)
    for i, b in enumerate(blocks):
        out = out + jnp.where(row == i, b, 0.0)
    return out


def kernel(x, norm_ffn1, ffn1_w_gate, ffn1_w_up, ffn1_w_down, norm_mix, w_in, w_proj_dil, w_proj_sb, w_out, norm_ffn2, ffn2_w_gate, ffn2_w_up, ffn2_w_down, norm_final, loss_target, m_norm_ffn1, m_ffn1_w_gate, m_ffn1_w_up, m_ffn1_w_down, m_norm_mix, m_w_in, m_w_proj_dil, m_w_proj_sb, m_w_out, m_norm_ffn2, m_ffn2_w_gate, m_ffn2_w_up, m_ffn2_w_down, m_norm_final, v_norm_ffn1, v_ffn1_w_gate, v_ffn1_w_up, v_ffn1_w_down, v_norm_mix, v_w_in, v_w_proj_dil, v_w_proj_sb, v_w_out, v_norm_ffn2, v_ffn2_w_gate, v_ffn2_w_up, v_ffn2_w_down, v_norm_final):
    given = dict(locals())
    for n in TRANSPOSED:
        for k in ("", "m_", "v_"):
            given[k + n] = jnp.swapaxes(given[k + n], 1, 2)
    weights = {n: given[n] for n in WEIGHT_NAMES}
    norms = {n: given[n] for n in NORM_NAMES}

    c_arr = lax.axis_index("c").astype(jnp.int32).reshape(1)
    me_arr = (2 * lax.axis_index("x") + lax.axis_index("y")).astype(jnp.int32).reshape(1)
    order = [(l, s, n) for l in range(DEPTH) for s in range(len(SUBLAYERS)) for n in SUBLAYERS[s]]
    n_first, n_second = len(SUBLAYERS[0]) - 1, len(SUBLAYERS[0]) + len(SUBLAYERS[1])
    waves = (("a", range(n_first)), ("b", range(n_first, n_second)), ("c", range(n_second, len(order))))
    sent, casts, token = {}, {}, None

    def start(tag, idxs, after=None):
        bufs, send_sems, recv_sems, token = _gather_start(tag, casts[tag], [order[i][0] > 0 for i in idxs], after)
        for p, i in enumerate(idxs):
            sent[i] = (bufs[p], p, send_sems, recv_sems)
        return token

    for tag, idxs in waves:
        casts[tag] = _cast_into_slot([weights[order[i][2]] for i in idxs], [order[i][0] for i in idxs], me_arr, token)
        token = start(tag, idxs) if tag == "a" else casts[tag][0]

    def arrived(tag, idxs, after):
        direct = order[idxs[0]][0] > 0
        got = _gather_wait(tag, [sent[i][0] for i in idxs], [sent[i][1] for i in idxs], direct,
                           sent[idxs[0]][2], sent[idxs[0]][3], after)
        return {order[i][2]: g for i, g in zip(idxs, got if direct else _gather_relay(got))}

    def weights_of(l, s, after):
        idxs = [i for i, (ll, ss, _) in enumerate(order) if (ll, ss) == (l, s)]
        k = len(SUBLAYERS) * l + s
        if k > 0:
            return arrived(k, idxs, after)
        last, first = order[idxs[-1]][2], idxs[:-1]
        waited = _gather_wait(f"{k}_first", [sent[i][0] for i in first], [sent[i][1] for i in first], False,
                              sent[first[0]][2], sent[first[0]][3], after)
        token = waited[0]
        for tag, wave in waves[1:]:
            token = start(tag, wave, token)
        got = {order[i][2]: g for i, g in zip(first, _gather_relay(waited, token))}
        got[last] = lambda after: arrived(f"{k}_last", idxs[-1:], after)[last]
        return got

    out = {}
    to_add, in_flight = [], []

    def add_and_scatter(after):
        l, s, names, gs, lands, ssem, rsem = to_add.pop(0)
        k = len(SUBLAYERS) * l + s
        got = _sibling_wait(f"grad_exchange_wait_{k}", gs, _other_half, lands, ssem, rsem, after)
        sent, after = _scatter_start(k, _add_half(gs, got, c_arr), after)
        in_flight.append((l, s, names) + sent)
        return after

    def on_grads(l, s, grads, after):
        names = list(grads)
        k = len(SUBLAYERS) * l + s
        gs = [grads[n] for n in names]
        sent, after, _ = _sibling_start(f"grad_exchange_start_{k}", gs, _other_half,
                                        [(g.shape[0], g.shape[1] // 2, g.shape[2]) for g in gs], after)
        if to_add:
            after = add_and_scatter(after)
        to_add.append((l, s, names, gs) + sent)
        return after

    loss_blk, grad_x, gains, dg_final = _local_step(x[0], loss_target[0], norms, norm_final, weights_of, on_grads)
    grad_x = add_and_scatter(grad_x)

    def update(l, names, mine, swap, after):
        theirs = _sibling_wait(f"grad_swap_wait_{l}_{names[0]}", mine, _all_of, *swap, grad_x if after is None else after)
        res = _adamw([weights[n] for n in names], [given["m_" + n] for n in names], [given["v_" + n] for n in names],
                     mine, theirs, l, c_arr, [out.get(n) for n in names], after)
        out.update(zip(names, res))

    waiting = None
    for l, s, names, sums, lands, ssem, rsem in in_flight:
        sums, lands = _scatter_wait(len(SUBLAYERS) * l + s, sums, lands, ssem, rsem, grad_x)
        mine = _sum_chips(lands, sums, me_arr)
        swap, _, token = _sibling_start(f"grad_swap_start_{l}_{names[0]}", mine, _all_of,
                                        [m.shape for m in mine], None)
        if waiting is not None:
            update(*waiting, token)
        waiting = (l, names, mine, swap)
    update(*waiting, None)
    out = {k + n: (jnp.swapaxes(v, 1, 2) if n in TRANSPOSED else v)
           for n, res in out.items() for k, v in zip(("grad_", "delta_", "new_m_", "new_v_"), res)}
    out["grad_x"] = grad_x[None]

    rows = [gains[l][n] for n in NORM_NAMES for l in range(DEPTH)] + [dg_final, loss_blk]
    total = _allreduce_rows(_pick_row(rows))
    out["loss"] = total[7, 0]
    wn = jnp.concatenate([given[n] for n in NORM_NAMES] + [norm_final[None], jnp.zeros((1, D_MODEL), F32)])
    mn_ = jnp.concatenate([given["m_" + n] for n in NORM_NAMES] + [m_norm_final[None], jnp.zeros((1, D_MODEL), F32)])
    vn_ = jnp.concatenate([given["v_" + n] for n in NORM_NAMES] + [v_norm_final[None], jnp.ones((1, D_MODEL), F32)])
    d_n, m_n, v_n = _adamw_rows(wn, mn_, vn_, total)
    for i, n in enumerate(NORM_NAMES):
        sl = slice(i * DEPTH, (i + 1) * DEPTH)
        out["grad_" + n], out["delta_" + n], out["new_m_" + n], out["new_v_" + n] = total[sl], d_n[sl], m_n[sl], v_n[sl]
    out["grad_norm_final"], out["delta_norm_final"] = total[6], d_n[6]
    out["new_m_norm_final"], out["new_v_norm_final"] = m_n[6], v_n[6]

    names = ["norm_ffn1", "ffn1_w_gate", "ffn1_w_up", "ffn1_w_down", "norm_mix", "w_in", "w_proj_dil", "w_proj_sb",
             "w_out", "norm_ffn2", "ffn2_w_gate", "ffn2_w_up", "ffn2_w_down", "norm_final"]
    return (out["loss"], out["grad_x"], *[out["grad_" + n] for n in names], *[out["delta_" + n] for n in names],
            *[out["new_m_" + n] for n in names], *[out["new_v_" + n] for n in names])
```
